```python
import math
import jax, jax.numpy as jnp
from jax import lax
import numpy as np

D_MODEL = 1024
BATCH = 8
SEQ = 4096
DEPTH = 1

HEAD_DIM = 64
MLA_HEADS = 8
MLA_Q_RANK = 256
MLA_KV_RANK = 128
MLA_NOPE_DIM = 64
MLA_ROPE_DIM = 32
MLA_V_DIM = HEAD_DIM
ROPE_THETA = 10000.0
DIL_HEADS = 8
DIL_PAIRS = ((128, 1), (512, 4), (2048, 16))
DIL_BLOCK = 128
DIL_WIDTH = DIL_HEADS * HEAD_DIM
MLA_WIDTH = MLA_HEADS * MLA_V_DIM
MIX_WIDTH = MLA_WIDTH + DIL_WIDTH
IN_SPLITS = (MLA_Q_RANK, MLA_KV_RANK, MLA_ROPE_DIM, DIL_WIDTH, DIL_WIDTH, DIL_WIDTH)
IN_WIDTH = sum(IN_SPLITS)
D_FF = 2816
CONV_WIDTH = 3
Q_BLOCK = 128
DN_ALPHA = (2.0 * DEPTH) ** 0.25
DN_BETA = (8.0 * DEPTH) ** -0.25
LN_EPS = 1e-5
RMS_EPS = 1e-6

kernel_name = "hybrid_mla_dilated_swa_convffn_deepnorm"


def layer_norm(x, g, b):
    xf = x.astype(jnp.float32)
    mu = jnp.mean(xf, axis=-1, keepdims=True)
    var = jnp.mean(jnp.square(xf - mu), axis=-1, keepdims=True)
    y = (xf - mu) * lax.rsqrt(var + LN_EPS) * g.astype(jnp.float32) + b.astype(jnp.float32)
    return y.astype(x.dtype)


def rms_norm(x, g):
    xf = x.astype(jnp.float32)
    y = xf * lax.rsqrt(jnp.mean(jnp.square(xf), axis=-1, keepdims=True) + RMS_EPS)
    return (y * g.astype(jnp.float32)).astype(x.dtype)


def apply_rope(x, pos):
    half = x.shape[-1] // 2
    freqs = ROPE_THETA ** (-jnp.arange(half, dtype=jnp.float32) / half)
    ang = pos.astype(jnp.float32)[:, None] * freqs[None, :]
    cos = jnp.cos(ang)[None, :, None, :]
    sin = jnp.sin(ang)[None, :, None, :]
    xf = x.astype(jnp.float32)
    x1, x2 = xf[..., :half], xf[..., half:]
    out = jnp.concatenate([x1 * cos - x2 * sin, x1 * sin + x2 * cos], axis=-1)
    return out.astype(x.dtype)


def alibi_slopes(n):
    return 2.0 ** (-8.0 * jnp.arange(1, n + 1, dtype=jnp.float32) / n)


def mla_attention(c_q, c_kv, k_rope, g_cq, g_ckv, w_uq, w_uk, w_uv):
    B, S, _ = c_q.shape
    pos = jnp.arange(S)
    c_q = rms_norm(c_q, g_cq)
    c_kv = rms_norm(c_kv, g_ckv)
    q = jnp.einsum('bsr,rhe->bshe', c_q, w_uq)
    q_nope, q_rope = q[..., :MLA_NOPE_DIM], q[..., MLA_NOPE_DIM:]
    k_nope = jnp.einsum('bsr,rhe->bshe', c_kv, w_uk)
    v = jnp.einsum('bsr,rhe->bshe', c_kv, w_uv)
    q_rope = apply_rope(q_rope, pos)
    k_rope = apply_rope(k_rope[:, :, None, :], pos)
    qf = jnp.concatenate([q_nope, q_rope], axis=-1)
    kf = jnp.concatenate([k_nope, jnp.broadcast_to(k_rope, k_nope.shape[:3] + (MLA_ROPE_DIM,))], axis=-1)
    scale = 1.0 / math.sqrt(MLA_NOPE_DIM + MLA_ROPE_DIM)
    nb = S // Q_BLOCK
    q_blocks = qf.reshape(B, nb, Q_BLOCK, MLA_HEADS, -1).transpose(1, 0, 2, 3, 4)
    k_pos = jnp.arange(S)

    def one_block(args):
        qb, bi = args
        s = jnp.einsum('bqhe,bkhe->bhqk', qb, kf).astype(jnp.float32) * scale
        q_pos = bi * Q_BLOCK + jnp.arange(Q_BLOCK)
        causal = q_pos[:, None] >= k_pos[None, :]
        s = jnp.where(causal[None, None], s, -jnp.inf)
        p = jax.nn.softmax(s, axis=-1).astype(v.dtype)
        return jnp.einsum('bhqk,bkhe->bqhe', p, v)

    o = lax.map(one_block, (q_blocks, jnp.arange(nb)))
    return o.transpose(1, 0, 2, 3, 4).reshape(B, S, MLA_HEADS * MLA_V_DIM)


def dilated_branch(q, k, v, slopes, window, dil):
    B, S, H, E = q.shape
    n_back = window // dil
    blk = DIL_BLOCK
    L = -(-S // (dil * blk)) * dil * blk
    M = L // dil
    nb = M // blk

    def to_sub(a):
        a = jnp.pad(a, ((0, 0), (0, L - S), (0, 0), (0, 0)))
        a = a.reshape(B, M, dil, H, E).transpose(0, 2, 1, 3, 4)
        return a.reshape(B, dil, nb, blk, H, E)

    def with_prev(ab):
        prev = jnp.pad(ab, ((0, 0), (0, 0), (1, 0), (0, 0), (0, 0), (0, 0)))[:, :, :-1]
        return jnp.concatenate([prev, ab], axis=3)

    qb = to_sub(q)
    kb = with_prev(to_sub(k))
    vb = with_prev(to_sub(v))
    s = jnp.einsum('bdnqhe,bdnkhe->bdnhqk', qb, kb).astype(jnp.float32) / math.sqrt(E)
    jq = jnp.arange(nb)[:, None] * blk + jnp.arange(blk)[None, :]
    jk = jnp.arange(nb)[:, None] * blk - blk + jnp.arange(2 * blk)[None, :]
    off = jq[:, :, None] - jk[:, None, :]
    valid = (off >= 0) & (off <= n_back) & (jk[:, None, :] >= 0)
    dist = (off * dil).astype(jnp.float32)
    bias = -slopes[None, :, None, None] * dist[:, None]
    s = jnp.where(valid[:, None], s + bias, -jnp.inf)
    lse = jax.nn.logsumexp(s, axis=-1)
    p = jnp.exp(s - lse[..., None]).astype(v.dtype)
    o = jnp.einsum('bdnhqk,bdnkhe->bdnqhe', p, vb)
    o = o.reshape(B, dil, M, H, E).transpose(0, 2, 1, 3, 4).reshape(B, L, H, E)[:, :S]
    lse = lse.transpose(0, 1, 2, 4, 3).reshape(B, dil, M, H).transpose(0, 2, 1, 3).reshape(B, L, H)[:, :S]
    return o, lse


def dilated_attention(q, k, v):
    B, S, H, E = q.shape
    slopes = alibi_slopes(H)
    outs, lses = [], []
    for window, dil in DIL_PAIRS:
        o, lse = dilated_branch(q, k, v, slopes, window, dil)
        outs.append(o.astype(jnp.float32))
        lses.append(lse)
    w = jax.nn.softmax(jnp.stack(lses, axis=0), axis=0)
    o = jnp.sum(w[..., None] * jnp.stack(outs, axis=0), axis=0)
    return o.astype(q.dtype).reshape(B, S, H * E)


def conv_gated_ffn(x, w_up, conv_w, conv_b, w_down):
    S = x.shape[1]
    u = x @ w_up
    y = conv_b
    for j in range(CONV_WIDTH):
        shift = CONV_WIDTH - 1 - j
        us = jnp.pad(u, ((0, 0), (shift, 0), (0, 0)))[:, :S] if shift else u
        y = y + conv_w[j] * us
    a, g = y[..., :D_FF], y[..., D_FF:]
    return (jax.nn.gelu(g) * a) @ w_down


def _fwd_setup_inputs(seed: int = 0) -> dict:
    key = jax.random.key(seed)
    ks = jax.random.split(key, 17)
    f32 = jnp.float32
    x = jax.random.normal(ks[0], (BATCH, SEQ, D_MODEL), f32)
    col_scale = jnp.concatenate([jnp.ones((IN_WIDTH - DIL_WIDTH,), f32),
                                 jnp.full((DIL_WIDTH,), DN_BETA, f32)])
    w_in = jax.random.normal(ks[1], (D_MODEL, IN_WIDTH), f32) * D_MODEL ** -0.5 * col_scale
    g_cq = 1.0 + 0.02 * jax.random.normal(ks[2], (MLA_Q_RANK,), f32)
    g_ckv = 1.0 + 0.02 * jax.random.normal(ks[3], (MLA_KV_RANK,), f32)
    w_uq = jax.random.normal(ks[4], (MLA_Q_RANK, MLA_HEADS, MLA_NOPE_DIM + MLA_ROPE_DIM), f32) * MLA_Q_RANK ** -0.5
    w_uk = jax.random.normal(ks[5], (MLA_KV_RANK, MLA_HEADS, MLA_NOPE_DIM), f32) * MLA_KV_RANK ** -0.5
    w_uv = jax.random.normal(ks[6], (MLA_KV_RANK, MLA_HEADS, MLA_V_DIM), f32) * MLA_KV_RANK ** -0.5 * DN_BETA
    w_o = jax.random.normal(ks[7], (MIX_WIDTH, D_MODEL), f32) * MIX_WIDTH ** -0.5 * DN_BETA
    ln1_g = 1.0 + 0.02 * jax.random.normal(ks[8], (D_MODEL,), f32)
    ln1_b = 0.02 * jax.random.normal(ks[9], (D_MODEL,), f32)
    w_up = jax.random.normal(ks[10], (D_MODEL, 2 * D_FF), f32) * D_MODEL ** -0.5 * DN_BETA
    conv_w = jax.random.normal(ks[11], (CONV_WIDTH, 2 * D_FF), f32) * CONV_WIDTH ** -0.5
    conv_b = 0.01 * jax.random.normal(ks[12], (2 * D_FF,), f32)
    w_down = jax.random.normal(ks[13], (D_FF, D_MODEL), f32) * D_FF ** -0.5 * DN_BETA
    ln2_g = 1.0 + 0.02 * jax.random.normal(ks[14], (D_MODEL,), f32)
    ln2_b = 0.02 * jax.random.normal(ks[15], (D_MODEL,), f32)
    return {"x": x, "w_in": w_in, "g_cq": g_cq, "g_ckv": g_ckv, "w_uq": w_uq,
            "w_uk": w_uk, "w_uv": w_uv, "w_o": w_o, "ln1_g": ln1_g, "ln1_b": ln1_b,
            "w_up": w_up, "conv_w": conv_w, "conv_b": conv_b, "w_down": w_down,
            "ln2_g": ln2_g, "ln2_b": ln2_b}


def _fwd_reference(x, w_in, g_cq, g_ckv, w_uq, w_uk, w_uv, w_o, ln1_g, ln1_b,
              w_up, conv_w, conv_b, w_down, ln2_g, ln2_b):
    B, S, _ = x.shape
    for _layer in range(DEPTH):
        h = x @ w_in
        idx = np.cumsum(IN_SPLITS)[:-1].tolist()
        c_q, c_kv, k_rope, q_d, k_d, v_d = jnp.split(h, idx, axis=-1)
        o_mla = mla_attention(c_q, c_kv, k_rope, g_cq, g_ckv, w_uq, w_uk, w_uv)
        o_dil = dilated_attention(q_d.reshape(B, S, DIL_HEADS, HEAD_DIM),
                                  k_d.reshape(B, S, DIL_HEADS, HEAD_DIM),
                                  v_d.reshape(B, S, DIL_HEADS, HEAD_DIM))
        mix = jnp.concatenate([o_mla, o_dil], axis=-1) @ w_o
        x = layer_norm(DN_ALPHA * x + mix, ln1_g, ln1_b)
        ffn = conv_gated_ffn(x, w_up, conv_w, conv_b, w_down)
        x = layer_norm(DN_ALPHA * x + ffn, ln2_g, ln2_b)
    return x


import jax as _jax
import jax.numpy as _jnp

TWIN_FORMAT = 'train_step'
FWD_PARAMS = ['x', 'w_in', 'g_cq', 'g_ckv', 'w_uq', 'w_uk', 'w_uv', 'w_o', 'ln1_g', 'ln1_b', 'w_up', 'conv_w', 'conv_b', 'w_down', 'ln2_g', 'ln2_b']
TWIN_WEIGHTS = ['w_in', 'g_cq', 'g_ckv', 'w_uq', 'w_uk', 'w_uv', 'w_o', 'ln1_g', 'ln1_b', 'w_up', 'conv_w', 'conv_b', 'w_down', 'ln2_g', 'ln2_b']
TWIN_DIFF_INPUT = 'x'
TWIN_INPUTS = ['x', 'w_in', 'g_cq', 'g_ckv', 'w_uq', 'w_uk', 'w_uv', 'w_o', 'ln1_g', 'ln1_b', 'w_up', 'conv_w', 'conv_b', 'w_down', 'ln2_g', 'ln2_b', 'loss_target', 'm_w_in', 'm_g_cq', 'm_g_ckv', 'm_w_uq', 'm_w_uk', 'm_w_uv', 'm_w_o', 'm_ln1_g', 'm_ln1_b', 'm_w_up', 'm_conv_w', 'm_conv_b', 'm_w_down', 'm_ln2_g', 'm_ln2_b', 'v_w_in', 'v_g_cq', 'v_g_ckv', 'v_w_uq', 'v_w_uk', 'v_w_uv', 'v_w_o', 'v_ln1_g', 'v_ln1_b', 'v_w_up', 'v_conv_w', 'v_conv_b', 'v_w_down', 'v_ln2_g', 'v_ln2_b']
TWIN_OUTPUTS = ['loss', 'grad_x', 'grad_w_in', 'grad_g_cq', 'grad_g_ckv', 'grad_w_uq', 'grad_w_uk', 'grad_w_uv', 'grad_w_o', 'grad_ln1_g', 'grad_ln1_b', 'grad_w_up', 'grad_conv_w', 'grad_conv_b', 'grad_w_down', 'grad_ln2_g', 'grad_ln2_b', 'delta_w_in', 'delta_g_cq', 'delta_g_ckv', 'delta_w_uq', 'delta_w_uk', 'delta_w_uv', 'delta_w_o', 'delta_ln1_g', 'delta_ln1_b', 'delta_w_up', 'delta_conv_w', 'delta_conv_b', 'delta_w_down', 'delta_ln2_g', 'delta_ln2_b', 'new_m_w_in', 'new_m_g_cq', 'new_m_g_ckv', 'new_m_w_uq', 'new_m_w_uk', 'new_m_w_uv', 'new_m_w_o', 'new_m_ln1_g', 'new_m_ln1_b', 'new_m_w_up', 'new_m_conv_w', 'new_m_conv_b', 'new_m_w_down', 'new_m_ln2_g', 'new_m_ln2_b', 'new_v_w_in', 'new_v_g_cq', 'new_v_g_ckv', 'new_v_w_uq', 'new_v_w_uk', 'new_v_w_uv', 'new_v_w_o', 'new_v_ln1_g', 'new_v_ln1_b', 'new_v_w_up', 'new_v_conv_w', 'new_v_conv_b', 'new_v_w_down', 'new_v_ln2_g', 'new_v_ln2_b']
TWIN_LEAF_KINDS = {'loss': 'loss', 'grad_x': 'grad_x', 'grad_w_in': 'grad_w', 'grad_g_cq': 'grad_w', 'grad_g_ckv': 'grad_w', 'grad_w_uq': 'grad_w', 'grad_w_uk': 'grad_w', 'grad_w_uv': 'grad_w', 'grad_w_o': 'grad_w', 'grad_ln1_g': 'grad_w', 'grad_ln1_b': 'grad_w', 'grad_w_up': 'grad_w', 'grad_conv_w': 'grad_w', 'grad_conv_b': 'grad_w', 'grad_w_down': 'grad_w', 'grad_ln2_g': 'grad_w', 'grad_ln2_b': 'grad_w', 'delta_w_in': 'delta_w', 'delta_g_cq': 'delta_w', 'delta_g_ckv': 'delta_w', 'delta_w_uq': 'delta_w', 'delta_w_uk': 'delta_w', 'delta_w_uv': 'delta_w', 'delta_w_o': 'delta_w', 'delta_ln1_g': 'delta_w', 'delta_ln1_b': 'delta_w', 'delta_w_up': 'delta_w', 'delta_conv_w': 'delta_w', 'delta_conv_b': 'delta_w', 'delta_w_down': 'delta_w', 'delta_ln2_g': 'delta_w', 'delta_ln2_b': 'delta_w', 'new_m_w_in': 'new_m', 'new_m_g_cq': 'new_m', 'new_m_g_ckv': 'new_m', 'new_m_w_uq': 'new_m', 'new_m_w_uk': 'new_m', 'new_m_w_uv': 'new_m', 'new_m_w_o': 'new_m', 'new_m_ln1_g': 'new_m', 'new_m_ln1_b': 'new_m', 'new_m_w_up': 'new_m', 'new_m_conv_w': 'new_m', 'new_m_conv_b': 'new_m', 'new_m_w_down': 'new_m', 'new_m_ln2_g': 'new_m', 'new_m_ln2_b': 'new_m', 'new_v_w_in': 'new_v', 'new_v_g_cq': 'new_v', 'new_v_g_ckv': 'new_v', 'new_v_w_uq': 'new_v', 'new_v_w_uk': 'new_v', 'new_v_w_uv': 'new_v', 'new_v_w_o': 'new_v', 'new_v_ln1_g': 'new_v', 'new_v_ln1_b': 'new_v', 'new_v_w_up': 'new_v', 'new_v_conv_w': 'new_v', 'new_v_conv_b': 'new_v', 'new_v_w_down': 'new_v', 'new_v_ln2_g': 'new_v', 'new_v_ln2_b': 'new_v'}


def _forward(args):
    return _fwd_reference(*[args[k] for k in FWD_PARAMS])


def _output_shape():
    out = _jax.eval_shape(lambda: _forward(_fwd_setup_inputs(0)))
    return out.shape, out.dtype

N_MICROBATCH = 1
ADAM_LR = 0.001
ADAM_B1 = 0.9
ADAM_B2 = 0.999
ADAM_EPS = 1e-08
ADAM_WD = 0.01
ADAM_STEP = 10
PER_EXAMPLE_BATCH_AXIS = {'x': 0, 'loss_target': 0}
SHARED_INPUTS = []
_WEIGHT_DTYPES = {'w_in': _jnp.float32, 'g_cq': _jnp.float32, 'g_ckv': _jnp.float32, 'w_uq': _jnp.float32, 'w_uk': _jnp.float32, 'w_uv': _jnp.float32, 'w_o': _jnp.float32, 'ln1_g': _jnp.float32, 'ln1_b': _jnp.float32, 'w_up': _jnp.float32, 'conv_w': _jnp.float32, 'conv_b': _jnp.float32, 'w_down': _jnp.float32, 'ln2_g': _jnp.float32, 'ln2_b': _jnp.float32}
MOMENT_SCALE = {'w_in': 2.227154e-02, 'g_cq': 1.440684e-02, 'g_ckv': 2.767620e-02, 'w_uq': 8.147621e-03, 'w_uk': 8.425949e-03, 'w_uv': 2.005159e-02, 'w_o': 2.820723e-02, 'ln1_g': 1.028991e+00, 'ln1_b': 4.493170e-01, 'w_up': 2.017649e-02, 'conv_w': 1.174023e-02, 'conv_b': 2.080691e-02, 'w_down': 3.286945e-02, 'ln2_g': 3.200208e+01, 'ln2_b': 6.367088e-01}


def _to_microbatches(a, axis):
    t = _jnp.moveaxis(a, axis, 0)
    t = t.reshape((N_MICROBATCH, t.shape[0] // N_MICROBATCH) + t.shape[1:])
    return _jnp.moveaxis(t, 1, axis + 1)


def setup_inputs(seed: int = 0) -> dict:
    inp = _fwd_setup_inputs(seed)
    key = _jax.random.fold_in(_jax.random.key(seed), 7919)
    shape, _ = _output_shape()
    out = dict(inp)
    out["loss_target"] = _jax.random.normal(_jax.random.fold_in(key, 0), shape, _jnp.float32)
    for i, name in enumerate(TWIN_WEIGHTS):
        w = inp[name].astype(_jnp.float32)
        if MOMENT_SCALE is None:
            s = _jnp.sqrt(_jnp.mean(_jnp.square(w)) + 1e-30)
        else:
            s = MOMENT_SCALE[name]
        km, kv = _jax.random.split(_jax.random.fold_in(key, i + 1))
        out[name] = w
        out["m_" + name] = s * _jax.random.normal(km, w.shape, _jnp.float32)
        out["v_" + name] = (s * s) * _jax.random.uniform(kv, w.shape, _jnp.float32, 0.5, 1.5)
    if N_MICROBATCH > 1:
        for name, axis in PER_EXAMPLE_BATCH_AXIS.items():
            out[name] = _to_microbatches(out[name], axis)
    return {'x': out['x'], 'w_in': out['w_in'], 'g_cq': out['g_cq'], 'g_ckv': out['g_ckv'], 'w_uq': out['w_uq'], 'w_uk': out['w_uk'], 'w_uv': out['w_uv'], 'w_o': out['w_o'], 'ln1_g': out['ln1_g'], 'ln1_b': out['ln1_b'], 'w_up': out['w_up'], 'conv_w': out['conv_w'], 'conv_b': out['conv_b'], 'w_down': out['w_down'], 'ln2_g': out['ln2_g'], 'ln2_b': out['ln2_b'], 'loss_target': out['loss_target'], 'm_w_in': out['m_w_in'], 'm_g_cq': out['m_g_cq'], 'm_g_ckv': out['m_g_ckv'], 'm_w_uq': out['m_w_uq'], 'm_w_uk': out['m_w_uk'], 'm_w_uv': out['m_w_uv'], 'm_w_o': out['m_w_o'], 'm_ln1_g': out['m_ln1_g'], 'm_ln1_b': out['m_ln1_b'], 'm_w_up': out['m_w_up'], 'm_conv_w': out['m_conv_w'], 'm_conv_b': out['m_conv_b'], 'm_w_down': out['m_w_down'], 'm_ln2_g': out['m_ln2_g'], 'm_ln2_b': out['m_ln2_b'], 'v_w_in': out['v_w_in'], 'v_g_cq': out['v_g_cq'], 'v_g_ckv': out['v_g_ckv'], 'v_w_uq': out['v_w_uq'], 'v_w_uk': out['v_w_uk'], 'v_w_uv': out['v_w_uv'], 'v_w_o': out['v_w_o'], 'v_ln1_g': out['v_ln1_g'], 'v_ln1_b': out['v_ln1_b'], 'v_w_up': out['v_w_up'], 'v_conv_w': out['v_conv_w'], 'v_conv_b': out['v_conv_b'], 'v_w_down': out['v_w_down'], 'v_ln2_g': out['v_ln2_g'], 'v_ln2_b': out['v_ln2_b']}


def _loss(weights, diff, rest, loss_target):
    with _jax.named_scope("forward"):
        args = {**rest, TWIN_DIFF_INPUT: diff, **{k: w.astype(_WEIGHT_DTYPES[k]) for k, w in weights.items()}}
        y = _forward(args)
    with _jax.named_scope("loss_head"):
        err = _jnp.square(y.astype(_jnp.float32) - loss_target)
        return 0.5 * _jnp.sum(_jnp.mean(err, axis=-1)) if err.ndim else 0.5 * err


def _adamw(w, g, m, v):
    m = ADAM_B1 * m + (1.0 - ADAM_B1) * g
    v = ADAM_B2 * v + (1.0 - ADAM_B2) * _jnp.square(g)
    m_hat = m / (1.0 - ADAM_B1 ** ADAM_STEP)
    v_hat = v / (1.0 - ADAM_B2 ** ADAM_STEP)
    delta = -ADAM_LR * (m_hat / (_jnp.sqrt(v_hat) + ADAM_EPS) + ADAM_WD * w)
    return delta, m, v


def reference(x, w_in, g_cq, g_ckv, w_uq, w_uk, w_uv, w_o, ln1_g, ln1_b, w_up, conv_w, conv_b, w_down, ln2_g, ln2_b, loss_target, m_w_in, m_g_cq, m_g_ckv, m_w_uq, m_w_uk, m_w_uv, m_w_o, m_ln1_g, m_ln1_b, m_w_up, m_conv_w, m_conv_b, m_w_down, m_ln2_g, m_ln2_b, v_w_in, v_g_cq, v_g_ckv, v_w_uq, v_w_uk, v_w_uv, v_w_o, v_ln1_g, v_ln1_b, v_w_up, v_conv_w, v_conv_b, v_w_down, v_ln2_g, v_ln2_b):
    given = dict(x=x, w_in=w_in, g_cq=g_cq, g_ckv=g_ckv, w_uq=w_uq, w_uk=w_uk, w_uv=w_uv, w_o=w_o, ln1_g=ln1_g, ln1_b=ln1_b, w_up=w_up, conv_w=conv_w, conv_b=conv_b, w_down=w_down, ln2_g=ln2_g, ln2_b=ln2_b, loss_target=loss_target, m_w_in=m_w_in, m_g_cq=m_g_cq, m_g_ckv=m_g_ckv, m_w_uq=m_w_uq, m_w_uk=m_w_uk, m_w_uv=m_w_uv, m_w_o=m_w_o, m_ln1_g=m_ln1_g, m_ln1_b=m_ln1_b, m_w_up=m_w_up, m_conv_w=m_conv_w, m_conv_b=m_conv_b, m_w_down=m_w_down, m_ln2_g=m_ln2_g, m_ln2_b=m_ln2_b, v_w_in=v_w_in, v_g_cq=v_g_cq, v_g_ckv=v_g_ckv, v_w_uq=v_w_uq, v_w_uk=v_w_uk, v_w_uv=v_w_uv, v_w_o=v_w_o, v_ln1_g=v_ln1_g, v_ln1_b=v_ln1_b, v_w_up=v_w_up, v_conv_w=v_conv_w, v_conv_b=v_conv_b, v_w_down=v_w_down, v_ln2_g=v_ln2_g, v_ln2_b=v_ln2_b)
    weights = {n: given[n] for n in TWIN_WEIGHTS}
    shared = {n: given[n] for n in SHARED_INPUTS}
    per_example = {n: given[n] for n in ['x']}
    grad_fn = _jax.value_and_grad(_loss, argnums=(0, 1))

    def one_microbatch(ex, loss_target):
        ex = dict(ex)
        diff = ex.pop(TWIN_DIFF_INPUT)
        return grad_fn(weights, diff, {**shared, **ex}, loss_target)

    if N_MICROBATCH == 1:
        loss, (grad_w, grad_x) = one_microbatch(per_example, given["loss_target"])
    else:
        def body(carry, xs):
            loss_sum, grad_sum = carry
            l_k, (gw_k, gx_k) = one_microbatch(xs[0], xs[1])
            with _jax.named_scope("update"):
                return (loss_sum + l_k, _jax.tree.map(_jnp.add, grad_sum, gw_k)), gx_k

        init = (_jnp.zeros((), _jnp.float32), _jax.tree.map(_jnp.zeros_like, weights))
        (loss, grad_w), grad_x = _jax.lax.scan(body, init, (per_example, given["loss_target"]))
    with _jax.named_scope("update"):
        delta_w, new_m, new_v = {}, {}, {}
        for n in TWIN_WEIGHTS:
            delta_w[n], new_m[n], new_v[n] = _adamw(weights[n], grad_w[n], given["m_" + n], given["v_" + n])
    return (loss, grad_x, *[grad_w[n] for n in TWIN_WEIGHTS], *[delta_w[n] for n in TWIN_WEIGHTS],
            *[new_m[n] for n in TWIN_WEIGHTS], *[new_v[n] for n in TWIN_WEIGHTS])
```

```python
import functools
import math

import jax
import jax.numpy as jnp
from jax import lax
from jax.experimental import pallas as pl
from jax.experimental.pallas import tpu as pltpu

F32 = jnp.float32
MXU_DTYPE = jnp.bfloat16
WIRE_DTYPE = jnp.bfloat16

N_DEV = 8
D_MODEL = 1024
HEADS = 8
HEAD_DIM = 64
LANES = 128
Q_RANK, KV_RANK, ROPE_DIM, NOPE_DIM = 256, 128, 32, 64
DIL_WIDTH = HEADS * HEAD_DIM
IN_WIDTH = 1952
IN_PAD = 2048
D_FF = 2816
ROPE_THETA = 10000.0
DIL_PAIRS = ((128, 1), (512, 4), (2048, 16))
DIL_BLOCK = 128
DN_ALPHA = 2.0 ** 0.25
LN_EPS = 1e-5
RMS_EPS = 1e-6
MLA_SCALE = 1.0 / math.sqrt(NOPE_DIM + ROPE_DIM)
DIL_SCALE = 1.0 / math.sqrt(HEAD_DIM)
ALIBI_SLOPES = tuple(2.0 ** (-8.0 * (h + 1) / HEADS) for h in range(HEADS))
NEG_BIG = -1e30
ADAM_LR, ADAM_B1, ADAM_B2, ADAM_EPS, ADAM_WD, ADAM_STEP = 0.001, 0.9, 0.999, 1e-08, 0.01, 10
VMEM_LIMIT = 48 * 1024 * 1024
PACK_COLS = 1024

MESH_AXES = ("x", "y", "c")


def _params(*sem):
    return pltpu.CompilerParams(dimension_semantics=sem or None, vmem_limit_bytes=VMEM_LIMIT)


def _tile(dim, cap):
    if dim <= cap:
        return dim
    best = None
    for t in range(LANES, cap + 1, LANES):
        if dim % t == 0:
            best = t
    assert best is not None, (dim, cap)
    return best


def _dot(a, b, ca, cb):
    return lax.dot_general(a, b, (((ca,), (cb,)), ((), ())), preferred_element_type=F32)


def _mm(a, b, *, name, ta=False, tb=False, out_dtype=F32, res=None, res_scale=1.0,
        a_heads=False, out_heads=False, tm=512, tn=1024, tk=512):
    if a_heads:
        hh, rows, _ = a.shape
        a_rows, a_cols = rows, hh * LANES
    else:
        a_rows, a_cols = a.shape
    m, k = (a_cols, a_rows) if ta else (a_rows, a_cols)
    n = b.shape[0] if tb else b.shape[1]
    assert (b.shape[1] if tb else b.shape[0]) == k
    tm, tn, tk = _tile(m, tm), _tile(n, tn), _tile(k, tk)
    if a_heads:
        if ta:
            tm = LANES
        else:
            tk = LANES
    if out_heads:
        tn = LANES
    nk = k // tk

    if a_heads:
        a_spec = (pl.BlockSpec((None, tk, LANES), lambda i, j, kk: (i, kk, 0)) if ta
                  else pl.BlockSpec((None, tm, LANES), lambda i, j, kk: (kk, i, 0)))
    else:
        a_spec = (pl.BlockSpec((tk, tm), lambda i, j, kk: (kk, i)) if ta
                  else pl.BlockSpec((tm, tk), lambda i, j, kk: (i, kk)))
    b_spec = (pl.BlockSpec((tn, tk), lambda i, j, kk: (j, kk)) if tb
              else pl.BlockSpec((tk, tn), lambda i, j, kk: (kk, j)))
    if out_heads:
        o_spec = pl.BlockSpec((None, tm, LANES), lambda i, j, kk: (j, i, 0))
        o_shape = jax.ShapeDtypeStruct((n // LANES, m, LANES), out_dtype)
    else:
        o_spec = pl.BlockSpec((tm, tn), lambda i, j, kk: (i, j))
        o_shape = jax.ShapeDtypeStruct((m, n), out_dtype)
    in_specs = [a_spec, b_spec]
    args = [a, b]
    if res is not None:
        in_specs.append(pl.BlockSpec((tm, tn), lambda i, j, kk: (i, j)))
        args.append(res)
    ca, cb = (0 if ta else 1), (1 if tb else 0)

    def body(*refs):
        if res is not None:
            a_ref, b_ref, r_ref, o_ref, acc_ref = refs
        else:
            a_ref, b_ref, o_ref, acc_ref = refs
        kk = pl.program_id(2)

        @pl.when(kk == 0)
        def _():
            acc_ref[...] = jnp.zeros_like(acc_ref)

        acc_ref[...] += _dot(a_ref[...].astype(MXU_DTYPE), b_ref[...].astype(MXU_DTYPE), ca, cb)

        @pl.when(kk == nk - 1)
        def _():
            out = acc_ref[...]
            if res is not None:
                out = out + res_scale * r_ref[...]
            o_ref[...] = out.astype(o_ref.dtype)

    return pl.pallas_call(
        body, name=name, grid=(m // tm, n // tn, nk), in_specs=in_specs, out_specs=o_spec, out_shape=o_shape,
        scratch_shapes=[pltpu.VMEM((tm, tn), F32)],
        compiler_params=_params("parallel", "parallel", "arbitrary"),
    )(*args)


def _rope_tables(seq):
    half = ROPE_DIM // 2
    freqs = ROPE_THETA ** (-jnp.arange(half, dtype=F32) / half)
    ang = jnp.arange(seq).astype(F32)[:, None] * freqs[None, :]
    cos, sin = jnp.cos(ang), jnp.sin(ang)
    one = jnp.ones((seq, NOPE_DIM), F32)
    tail = jnp.ones((seq, LANES - NOPE_DIM - ROPE_DIM), F32)
    ctab = jnp.concatenate([one, cos, cos, tail], axis=1)
    stab = jnp.concatenate([0 * one, -sin, sin, 0 * tail], axis=1)
    return ctab, stab


def _rope_swap(t):
    lane = lax.broadcasted_iota(jnp.int32, t.shape, 1)
    half = ROPE_DIM // 2
    return jnp.where(lane < NOPE_DIM + half, pltpu.roll(t, LANES - half, 1), pltpu.roll(t, half, 1))


def _rope(t, ctab, stab):
    return t * ctab + _rope_swap(t) * stab


def _rope_inv(t, ctab, stab):
    return t * ctab - _rope_swap(t) * stab


def _rms(x, g):
    r = lax.rsqrt(jnp.mean(x * x, axis=-1, keepdims=True) + RMS_EPS)
    xh = x * r
    return xh, r, xh * g


def _mla_prep(h, g_cq, g_ckv, wq, wk, wv, ctab, stab, tm=512):
    seq = h.shape[0]
    tm = min(tm, seq)

    def body(h_ref, gq_ref, gk_ref, wq_ref, wk_ref, wv_ref, c_ref, s_ref, q_out, k_out, v_out):
        hb = h_ref[...]
        ctab_, stab_ = c_ref[...], s_ref[...]
        _, _, cqn = _rms(hb[:, :Q_RANK], gq_ref[...])
        _, _, ckn = _rms(hb[:, Q_RANK:Q_RANK + KV_RANK], gk_ref[...])
        cqn = cqn.astype(MXU_DTYPE)
        ckn = ckn.astype(MXU_DTYPE)
        krr = _rope(hb[:, Q_RANK + KV_RANK:], ctab_, stab_)
        for hd in range(HEADS):
            q = _dot(cqn, wq_ref[hd], 1, 0)
            q_out[hd] = _rope(q, ctab_, stab_).astype(q_out.dtype)
            k_out[hd] = (_dot(ckn, wk_ref[hd], 1, 0) + krr).astype(k_out.dtype)
            v_out[hd] = _dot(ckn, wv_ref[hd], 1, 0).astype(v_out.dtype)

    full = lambda *shape: pl.BlockSpec(shape, lambda i: (0,) * len(shape))
    slab = pl.BlockSpec((HEADS, tm, LANES), lambda i: (0, i, 0))
    shp = jax.ShapeDtypeStruct((HEADS, seq, LANES), MXU_DTYPE)
    return pl.pallas_call(
        body, name="mla_prep", grid=(seq // tm,),
        in_specs=[pl.BlockSpec((tm, 512), lambda i: (i, 0)), full(1, Q_RANK), full(1, KV_RANK),
                  full(HEADS, Q_RANK, LANES), full(HEADS, KV_RANK, LANES), full(HEADS, KV_RANK, LANES),
                  pl.BlockSpec((tm, LANES), lambda i: (i, 0)), pl.BlockSpec((tm, LANES), lambda i: (i, 0))],
        out_specs=[slab, slab, slab], out_shape=[shp, shp, shp],
        compiler_params=_params("parallel"),
    )(h, g_cq, g_ckv, wq, wk, wv, ctab, stab)


def _mla_prep_bwd(h, g_cq, g_ckv, wq, wk, wv, ctab, stab, dq, dk, dv, tm=512):
    seq = h.shape[0]
    tm = min(tm, seq)

    def body(h_ref, gq_ref, gk_ref, wq_ref, wk_ref, wv_ref, c_ref, s_ref, dq_ref, dk_ref, dv_ref,
             dh_ref, dwq_ref, dwk_ref, dwv_ref, dgq_ref, dgk_ref):
        @pl.when(pl.program_id(0) == 0)
        def _():
            for r in (dwq_ref, dwk_ref, dwv_ref, dgq_ref, dgk_ref):
                r[...] = jnp.zeros_like(r)

        hb = h_ref[...]
        ctab_, stab_ = c_ref[...], s_ref[...]
        gq, gk = gq_ref[...], gk_ref[...]
        xq, rq, cqn = _rms(hb[:, :Q_RANK], gq)
        xk, rk, ckn = _rms(hb[:, Q_RANK:Q_RANK + KV_RANK], gk)
        cqn = cqn.astype(MXU_DTYPE)
        ckn = ckn.astype(MXU_DTYPE)
        d_cqn = jnp.zeros((tm, Q_RANK), F32)
        d_ckn = jnp.zeros((tm, KV_RANK), F32)
        d_krr = jnp.zeros((tm, LANES), F32)
        for hd in range(HEADS):
            dqh = _rope_inv(dq_ref[hd], ctab_, stab_).astype(MXU_DTYPE)
            d_cqn += _dot(dqh, wq_ref[hd], 1, 1)
            dwq_ref[hd] += _dot(cqn, dqh, 0, 0)
            dkh = dk_ref[hd]
            d_krr += dkh
            dkh = dkh.astype(MXU_DTYPE)
            d_ckn += _dot(dkh, wk_ref[hd], 1, 1)
            dwk_ref[hd] += _dot(ckn, dkh, 0, 0)
            dvh = dv_ref[hd].astype(MXU_DTYPE)
            d_ckn += _dot(dvh, wv_ref[hd], 1, 1)
            dwv_ref[hd] += _dot(ckn, dvh, 0, 0)
        lane = lax.broadcasted_iota(jnp.int32, (tm, LANES), 1)
        rot = (lane >= NOPE_DIM) & (lane < NOPE_DIM + ROPE_DIM)
        d_kr = jnp.where(rot, _rope_inv(jnp.where(rot, d_krr, 0.0), ctab_, stab_), 0.0)

        def rms_bwd(dy, xh, r, g, dg_ref):
            dg_ref[...] += jnp.sum(dy * xh, axis=0, keepdims=True)
            dxh = dy * g
            return r * (dxh - xh * jnp.mean(dxh * xh, axis=-1, keepdims=True))

        d_cq = rms_bwd(d_cqn, xq, rq, gq, dgq_ref)
        d_ck = rms_bwd(d_ckn, xk, rk, gk, dgk_ref)
        dh_ref[...] = jnp.concatenate([d_cq, d_ck, d_kr], axis=1).astype(dh_ref.dtype)

    full = lambda *shape: pl.BlockSpec(shape, lambda i: (0,) * len(shape))
    slab = pl.BlockSpec((HEADS, tm, LANES), lambda i: (0, i, 0))
    return pl.pallas_call(
        body, name="mla_prep_bwd", grid=(seq // tm,),
        in_specs=[pl.BlockSpec((tm, 512), lambda i: (i, 0)), full(1, Q_RANK), full(1, KV_RANK),
                  full(HEADS, Q_RANK, LANES), full(HEADS, KV_RANK, LANES), full(HEADS, KV_RANK, LANES),
                  pl.BlockSpec((tm, LANES), lambda i: (i, 0)), pl.BlockSpec((tm, LANES), lambda i: (i, 0)),
                  slab, slab, slab],
        out_specs=[pl.BlockSpec((tm, 512), lambda i: (i, 0)), full(HEADS, Q_RANK, LANES), full(HEADS, KV_RANK, LANES),
                   full(HEADS, KV_RANK, LANES), full(1, Q_RANK), full(1, KV_RANK)],
        out_shape=[jax.ShapeDtypeStruct((seq, 512), MXU_DTYPE), jax.ShapeDtypeStruct((HEADS, Q_RANK, LANES), F32),
                   jax.ShapeDtypeStruct((HEADS, KV_RANK, LANES), F32), jax.ShapeDtypeStruct((HEADS, KV_RANK, LANES), F32),
                   jax.ShapeDtypeStruct((1, Q_RANK), F32), jax.ShapeDtypeStruct((1, KV_RANK), F32)],
        compiler_params=_params("arbitrary"),
    )(h, g_cq, g_ckv, wq, wk, wv, ctab, stab, dq, dk, dv)


def _causal_mask(t):
    row = lax.broadcasted_iota(jnp.int32, (t, t), 0)
    col = lax.broadcasted_iota(jnp.int32, (t, t), 1)
    return row >= col


def _mla_attn_fwd(q, k, v, t=512):
    _, seq, _ = q.shape
    t = min(t, seq)

    def body(q_ref, k_ref, v_ref, o_ref, lse_ref, m_ref, l_ref, acc_ref):
        i = pl.program_id(1)
        qb = q_ref[...]
        m_ref[...] = jnp.full_like(m_ref, NEG_BIG)
        l_ref[...] = jnp.zeros_like(l_ref)
        acc_ref[...] = jnp.zeros_like(acc_ref)

        def step(j, masked):
            kb = k_ref[pl.ds(pl.multiple_of(j * t, t), t), :]
            vb = v_ref[pl.ds(pl.multiple_of(j * t, t), t), :]
            s = _dot(qb, kb, 1, 1) * MLA_SCALE
            if masked:
                s = jnp.where(_causal_mask(t), s, NEG_BIG)
            m_old = m_ref[...]
            m_new = jnp.maximum(m_old, jnp.max(s, axis=1, keepdims=True))
            p = jnp.exp(s - m_new)
            a = jnp.exp(m_old - m_new)
            l_ref[...] = a * l_ref[...] + jnp.sum(p, axis=1, keepdims=True)
            acc_ref[...] = a * acc_ref[...] + _dot(p.astype(MXU_DTYPE), vb, 1, 0)
            m_ref[...] = m_new

        def loop_body(j, c):
            step(j, False)
            return c

        lax.fori_loop(0, i, loop_body, 0)
        step(i, True)
        l = l_ref[...]
        o_ref[...] = acc_ref[...] / l
        lse_ref[...] = jnp.broadcast_to(m_ref[...] + jnp.log(l), lse_ref.shape)

    blk = pl.BlockSpec((None, t, LANES), lambda h, i: (h, i, 0))
    whole = pl.BlockSpec((None, seq, LANES), lambda h, i: (h, 0, 0))
    shp = jax.ShapeDtypeStruct((HEADS, seq, LANES), F32)
    return pl.pallas_call(
        body, name="mla_attn_fwd", grid=(HEADS, seq // t),
        in_specs=[blk, whole, whole], out_specs=[blk, blk], out_shape=[shp, shp],
        scratch_shapes=[pltpu.VMEM((t, 1), F32), pltpu.VMEM((t, 1), F32), pltpu.VMEM((t, LANES), F32)],
        compiler_params=_params("parallel", "arbitrary"),
    )(q, k, v)


def _mla_attn_bwd(q, k, v, o, lse, do, t=512):
    _, seq, _ = q.shape
    t = min(t, seq)
    nb = seq // t

    def body(q_ref, k_ref, v_ref, o_ref, lse_ref, do_ref, dq_ref, dk_ref, dv_ref, dl_ref, dka_ref, dva_ref):
        dq_ref[...] = jnp.zeros_like(dq_ref)

        def delta_body(i, c):
            rows = pl.ds(pl.multiple_of(i * t, t), t)
            dl_ref[rows, :] = jnp.sum(do_ref[rows, :] * o_ref[rows, :], axis=1, keepdims=True)
            return c

        lax.fori_loop(0, nb, delta_body, 0)

        def kblock(j, c):
            krows = pl.ds(pl.multiple_of(j * t, t), t)
            kb = k_ref[krows, :]
            vb = v_ref[krows, :]
            dka_ref[...] = jnp.zeros_like(dka_ref)
            dva_ref[...] = jnp.zeros_like(dva_ref)

            def qstep(i, masked):
                rows = pl.ds(pl.multiple_of(i * t, t), t)
                qb = q_ref[rows, :]
                dob = do_ref[rows, :].astype(MXU_DTYPE)
                s = _dot(qb, kb, 1, 1) * MLA_SCALE
                if masked:
                    s = jnp.where(_causal_mask(t), s, NEG_BIG)
                p = jnp.exp(s - lse_ref[rows, 0:1])
                dva_ref[...] += _dot(p.astype(MXU_DTYPE), dob, 0, 0)
                dp = _dot(dob, vb, 1, 1)
                ds = (p * (dp - dl_ref[rows, :]) * MLA_SCALE).astype(MXU_DTYPE)
                dka_ref[...] += _dot(ds, qb, 0, 0)
                dq_ref[rows, :] += _dot(ds, kb, 1, 0)

            qstep(j, True)

            def qloop(i, c2):
                qstep(i, False)
                return c2

            lax.fori_loop(j + 1, nb, qloop, 0)
            dk_ref[krows, :] = dka_ref[...]
            dv_ref[krows, :] = dva_ref[...]
            return c

        lax.fori_loop(0, nb, kblock, 0)

    whole = pl.BlockSpec((None, seq, LANES), lambda h: (h, 0, 0))
    shp = jax.ShapeDtypeStruct((HEADS, seq, LANES), F32)
    return pl.pallas_call(
        body, name="mla_attn_bwd", grid=(HEADS,),
        in_specs=[whole] * 6, out_specs=[whole] * 3, out_shape=[shp] * 3,
        scratch_shapes=[pltpu.VMEM((seq, 1), F32), pltpu.VMEM((t, LANES), F32), pltpu.VMEM((t, LANES), F32)],
        compiler_params=_params("parallel"),
    )(q, k, v, o, lse, do)


def _dil_scores(qh, kh, lim, keys_side):
    s = _dot(qh, kh, 1, 1) * DIL_SCALE
    b = DIL_BLOCK
    if keys_side:
        iq = lax.broadcasted_iota(jnp.int32, (b, 2 * b), 0)
        ik = lax.broadcasted_iota(jnp.int32, (b, 2 * b), 1)
        off = iq + b - ik
        ok = (off >= 0) & (off <= b) & (ik >= lim)
    else:
        iq = lax.broadcasted_iota(jnp.int32, (2 * b, b), 0)
        ik = lax.broadcasted_iota(jnp.int32, (2 * b, b), 1)
        off = iq - ik
        ok = (off >= 0) & (off <= b) & (iq < lim)
    return s, off, ok


def _dil_views(h, dil):
    seq = h.shape[0]
    return h.reshape(seq // dil, dil * h.shape[1])


def _dil_fwd(h, dil):
    seq = h.shape[0]
    m = seq // dil
    nb = m // DIL_BLOCK
    b = DIL_BLOCK
    hv = _dil_views(h, dil)

    def body(q_ref, kp_ref, kc_ref, vp_ref, vc_ref, o_ref, lse_ref):
        n = pl.program_id(1)
        q = q_ref[...]
        kcat = jnp.concatenate([kp_ref[...], kc_ref[...]], axis=0)
        vcat = jnp.concatenate([vp_ref[...], vc_ref[...]], axis=0)
        lane = lax.broadcasted_iota(jnp.int32, (b, LANES), 1)
        lse_all = jnp.zeros((b, LANES), F32)
        for hd in range(HEADS):
            cols = slice(HEAD_DIM * hd, HEAD_DIM * (hd + 1))
            s, off, ok = _dil_scores(q[:, cols].astype(MXU_DTYPE), kcat[:, cols].astype(MXU_DTYPE), jnp.where(n > 0, 0, b), True)
            s = jnp.where(ok, s - (ALIBI_SLOPES[hd] * dil) * off.astype(F32), NEG_BIG)
            mx = jnp.max(s, axis=1, keepdims=True)
            p = jnp.exp(s - mx)
            l = jnp.sum(p, axis=1, keepdims=True)
            o_ref[:, cols] = _dot(p.astype(MXU_DTYPE), vcat[:, cols].astype(MXU_DTYPE), 1, 0) / l
            lse_all = jnp.where(lane == hd, mx + jnp.log(l), lse_all)
        lse_ref[...] = lse_all

    def col(c):
        return lambda r, n: (n, r * 4 + c)

    def col_prev(c):
        return lambda r, n: (jnp.maximum(n - 1, 0), r * 4 + c)

    blk = lambda im: pl.BlockSpec((b, DIL_WIDTH), im)
    o, lse = pl.pallas_call(
        body, name=f"dil_fwd_{dil}", grid=(dil, nb),
        in_specs=[blk(col(1)), blk(col_prev(2)), blk(col(2)), blk(col_prev(3)), blk(col(3))],
        out_specs=[pl.BlockSpec((b, DIL_WIDTH), lambda r, n: (n, r)), pl.BlockSpec((b, LANES), lambda r, n: (n, r))],
        out_shape=[jax.ShapeDtypeStruct((m, dil * DIL_WIDTH), F32), jax.ShapeDtypeStruct((m, dil * LANES), F32)],
        compiler_params=_params("parallel", "parallel"),
    )(hv, hv, hv, hv, hv)
    return o.reshape(seq, DIL_WIDTH), lse.reshape(seq, LANES)


def _dil_combine(os_, lses, tm=512):
    seq = os_[0].shape[0]
    tm = min(tm, seq)

    def body(o1, o2, o3, l1, l2, l3, o_ref, lse_ref):
        ls = [l1[...], l2[...], l3[...]]
        mx = jnp.maximum(jnp.maximum(ls[0], ls[1]), ls[2])
        es = [jnp.exp(l - mx) for l in ls]
        den = es[0] + es[1] + es[2]
        lse_ref[...] = mx + jnp.log(den)
        ws = [e / den for e in es]
        for hd in range(HEADS):
            cols = slice(HEAD_DIM * hd, HEAD_DIM * (hd + 1))
            acc = ws[0][:, hd:hd + 1] * o1[:, cols]
            acc += ws[1][:, hd:hd + 1] * o2[:, cols]
            acc += ws[2][:, hd:hd + 1] * o3[:, cols]
            o_ref[:, cols] = acc

    ob = pl.BlockSpec((tm, DIL_WIDTH), lambda i: (i, 0))
    lb = pl.BlockSpec((tm, LANES), lambda i: (i, 0))
    return pl.pallas_call(
        body, name="dil_combine", grid=(seq // tm,), in_specs=[ob] * 3 + [lb] * 3, out_specs=[ob, lb],
        out_shape=[jax.ShapeDtypeStruct((seq, DIL_WIDTH), F32), jax.ShapeDtypeStruct((seq, LANES), F32)],
        compiler_params=_params("parallel"),
    )(*os_, *lses)


def _dil_bwd_q(h, o, lse, do, dil):
    seq = h.shape[0]
    m = seq // dil
    nb = m // DIL_BLOCK
    b = DIL_BLOCK
    hv = _dil_views(h, dil)
    ov, dov, lv = _dil_views(o, dil), _dil_views(do, dil), _dil_views(lse, dil)

    def body(q_ref, kp_ref, kc_ref, vp_ref, vc_ref, o_ref, do_ref, lse_ref, dq_ref):
        n = pl.program_id(1)
        q = q_ref[...]
        kcat = jnp.concatenate([kp_ref[...], kc_ref[...]], axis=0)
        vcat = jnp.concatenate([vp_ref[...], vc_ref[...]], axis=0)
        ob, dob, lse_b = o_ref[...], do_ref[...], lse_ref[...]
        for hd in range(HEADS):
            cols = slice(HEAD_DIM * hd, HEAD_DIM * (hd + 1))
            kh = kcat[:, cols].astype(MXU_DTYPE)
            s, off, ok = _dil_scores(q[:, cols].astype(MXU_DTYPE), kh, jnp.where(n > 0, 0, b), True)
            s = jnp.where(ok, s - (ALIBI_SLOPES[hd] * dil) * off.astype(F32), NEG_BIG)
            p = jnp.exp(s - lse_b[:, hd:hd + 1])
            doh = dob[:, cols]
            delta = jnp.sum(doh * ob[:, cols], axis=1, keepdims=True)
            dp = _dot(doh.astype(MXU_DTYPE), vcat[:, cols].astype(MXU_DTYPE), 1, 1)
            ds = (p * (dp - delta) * DIL_SCALE).astype(MXU_DTYPE)
            dq_ref[:, cols] = _dot(ds, kh, 1, 0)

    def col(c):
        return lambda r, n: (n, r * 4 + c)

    def col_prev(c):
        return lambda r, n: (jnp.maximum(n - 1, 0), r * 4 + c)

    blk = lambda im: pl.BlockSpec((b, DIL_WIDTH), im)
    own = lambda r, n: (n, r)
    dq = pl.pallas_call(
        body, name=f"dil_bwd_q_{dil}", grid=(dil, nb),
        in_specs=[blk(col(1)), blk(col_prev(2)), blk(col(2)), blk(col_prev(3)), blk(col(3)),
                  blk(own), blk(own), pl.BlockSpec((b, LANES), own)],
        out_specs=blk(own), out_shape=jax.ShapeDtypeStruct((m, dil * DIL_WIDTH), F32),
        compiler_params=_params("parallel", "parallel"),
    )(hv, hv, hv, hv, hv, ov, dov, lv)
    return dq.reshape(seq, DIL_WIDTH)


def _dil_bwd_kv(h, o, lse, do, dil):
    seq = h.shape[0]
    m = seq // dil
    nb = m // DIL_BLOCK
    b = DIL_BLOCK
    hv = _dil_views(h, dil)
    ov, dov, lv = _dil_views(o, dil), _dil_views(do, dil), _dil_views(lse, dil)

    def body(k_ref, v_ref, qc_ref, qn_ref, oc_ref, on_ref, doc_ref, don_ref, lc_ref, ln_ref, dk_ref, dv_ref):
        n = pl.program_id(1)
        k, v = k_ref[...], v_ref[...]
        qcat = jnp.concatenate([qc_ref[...], qn_ref[...]], axis=0)
        ocat = jnp.concatenate([oc_ref[...], on_ref[...]], axis=0)
        docat = jnp.concatenate([doc_ref[...], don_ref[...]], axis=0)
        lcat = jnp.concatenate([lc_ref[...], ln_ref[...]], axis=0)
        for hd in range(HEADS):
            cols = slice(HEAD_DIM * hd, HEAD_DIM * (hd + 1))
            qh = qcat[:, cols].astype(MXU_DTYPE)
            s, off, ok = _dil_scores(qh, k[:, cols].astype(MXU_DTYPE), jnp.where(n < nb - 1, 2 * b, b), False)
            s = jnp.where(ok, s - (ALIBI_SLOPES[hd] * dil) * off.astype(F32), NEG_BIG)
            p = jnp.where(ok, jnp.exp(s - lcat[:, hd:hd + 1]), 0.0)
            doh = docat[:, cols]
            delta = jnp.sum(doh * ocat[:, cols], axis=1, keepdims=True)
            dohm = doh.astype(MXU_DTYPE)
            dv_ref[:, cols] = _dot(p.astype(MXU_DTYPE), dohm, 0, 0)
            dp = _dot(dohm, v[:, cols].astype(MXU_DTYPE), 1, 1)
            ds = (p * (dp - delta) * DIL_SCALE).astype(MXU_DTYPE)
            dk_ref[:, cols] = _dot(ds, qh, 0, 0)

    def col(c):
        return lambda r, n: (n, r * 4 + c)

    def col_next(c):
        return lambda r, n: (jnp.minimum(n + 1, nb - 1), r * 4 + c)

    own = lambda r, n: (n, r)
    nxt = lambda r, n: (jnp.minimum(n + 1, nb - 1), r)
    blk = lambda im: pl.BlockSpec((b, DIL_WIDTH), im)
    lblk = lambda im: pl.BlockSpec((b, LANES), im)
    shp = jax.ShapeDtypeStruct((m, dil * DIL_WIDTH), F32)
    dk, dv = pl.pallas_call(
        body, name=f"dil_bwd_kv_{dil}", grid=(dil, nb),
        in_specs=[blk(col(2)), blk(col(3)), blk(col(1)), blk(col_next(1)), blk(own), blk(nxt), blk(own), blk(nxt),
                  lblk(own), lblk(nxt)],
        out_specs=[blk(own), blk(own)], out_shape=[shp, shp],
        compiler_params=_params("parallel", "parallel"),
    )(hv, hv, hv, hv, ov, ov, dov, dov, lv, lv)
    return dk.reshape(seq, DIL_WIDTH), dv.reshape(seq, DIL_WIDTH)


def _assemble_dh(dh_mla, dqs, dks, dvs, tm=512):
    seq = dh_mla.shape[0]
    tm = min(tm, seq)

    def body(a_ref, q1, q2, q3, k1, k2, k3, v1, v2, v3, o_ref):
        o_ref[:, 0:512] = a_ref[...].astype(o_ref.dtype)
        o_ref[:, 512:1024] = (q1[...] + q2[...] + q3[...]).astype(o_ref.dtype)
        o_ref[:, 1024:1536] = (k1[...] + k2[...] + k3[...]).astype(o_ref.dtype)
        o_ref[:, 1536:2048] = (v1[...] + v2[...] + v3[...]).astype(o_ref.dtype)

    blk = pl.BlockSpec((tm, 512), lambda i: (i, 0))
    return pl.pallas_call(
        body, name="assemble_dh", grid=(seq // tm,), in_specs=[blk] * 10,
        out_specs=pl.BlockSpec((tm, IN_PAD), lambda i: (i, 0)), out_shape=jax.ShapeDtypeStruct((seq, IN_PAD), MXU_DTYPE),
        compiler_params=_params("parallel"),
    )(dh_mla, *dqs, *dks, *dvs)


def _ln_stats(z):
    mu = jnp.mean(z, axis=-1, keepdims=True)
    zc = z - mu
    r = lax.rsqrt(jnp.mean(zc * zc, axis=-1, keepdims=True) + LN_EPS)
    return zc * r, r


def _ln_bwd_math(dy, xh, r, g):
    dxh = dy * g
    return r * (dxh - jnp.mean(dxh, axis=-1, keepdims=True) - xh * jnp.mean(dxh * xh, axis=-1, keepdims=True))


def _ln1_fwd(x0, mix_a, mix_b, g, b, tm=512):
    seq, d = x0.shape
    tm = min(tm, seq)

    def body(x_ref, a_ref, b2_ref, g_ref, b_ref, z_ref, y_ref):
        z = DN_ALPHA * x_ref[...] + (a_ref[...] + b2_ref[...])
        xh, _ = _ln_stats(z)
        z_ref[...] = z
        y_ref[...] = xh * g_ref[...] + b_ref[...]

    blk = pl.BlockSpec((tm, d), lambda i: (i, 0))
    vec = pl.BlockSpec((1, d), lambda i: (0, 0))
    shp = jax.ShapeDtypeStruct((seq, d), F32)
    return pl.pallas_call(body, name="ln1_fwd", grid=(seq // tm,), in_specs=[blk, blk, blk, vec, vec],
                          out_specs=[blk, blk], out_shape=[shp, shp], compiler_params=_params("parallel"))(x0, mix_a, mix_b, g, b)


def _ln_bwd(dy, z, g, name, tm=512):
    seq, d = z.shape
    tm = min(tm, seq)

    def body(dy_ref, z_ref, g_ref, dz_ref, dg_ref, db_ref):
        @pl.when(pl.program_id(0) == 0)
        def _():
            dg_ref[...] = jnp.zeros_like(dg_ref)
            db_ref[...] = jnp.zeros_like(db_ref)

        dyb = dy_ref[...]
        xh, r = _ln_stats(z_ref[...])
        dg_ref[...] += jnp.sum(dyb * xh, axis=0, keepdims=True)
        db_ref[...] += jnp.sum(dyb, axis=0, keepdims=True)
        dz_ref[...] = _ln_bwd_math(dyb, xh, r, g_ref[...])

    blk = pl.BlockSpec((tm, d), lambda i: (i, 0))
    vec = pl.BlockSpec((1, d), lambda i: (0, 0))
    return pl.pallas_call(
        body, name=name, grid=(seq // tm,), in_specs=[blk, blk, vec], out_specs=[blk, vec, vec],
        out_shape=[jax.ShapeDtypeStruct((seq, d), F32), jax.ShapeDtypeStruct((1, d), F32), jax.ShapeDtypeStruct((1, d), F32)],
        compiler_params=_params("arbitrary"))(dy, z, g)


def _ln2_loss_bwd(x1, ffn, target, g, b, tm=512):
    seq, d = x1.shape
    tm = min(tm, seq)

    def body(x_ref, f_ref, t_ref, g_ref, b_ref, dz_ref, loss_ref, dg_ref, db_ref):
        @pl.when(pl.program_id(0) == 0)
        def _():
            loss_ref[...] = jnp.zeros_like(loss_ref)
            dg_ref[...] = jnp.zeros_like(dg_ref)
            db_ref[...] = jnp.zeros_like(db_ref)

        gv = g_ref[...]
        z = DN_ALPHA * x_ref[...] + f_ref[...]
        xh, r = _ln_stats(z)
        err = (xh * gv + b_ref[...]) - t_ref[...]
        loss_ref[...] += 0.5 * jnp.sum(jnp.mean(err * err, axis=-1, keepdims=True), axis=0, keepdims=True)
        dy = err * (1.0 / d)
        dg_ref[...] += jnp.sum(dy * xh, axis=0, keepdims=True)
        db_ref[...] += jnp.sum(dy, axis=0, keepdims=True)
        dz_ref[...] = _ln_bwd_math(dy, xh, r, gv)

    blk = pl.BlockSpec((tm, d), lambda i: (i, 0))
    vec = pl.BlockSpec((1, d), lambda i: (0, 0))
    return pl.pallas_call(
        body, name="ln2_loss_bwd", grid=(seq // tm,), in_specs=[blk, blk, blk, vec, vec],
        out_specs=[blk, pl.BlockSpec((1, LANES), lambda i: (0, 0)), vec, vec],
        out_shape=[jax.ShapeDtypeStruct((seq, d), F32), jax.ShapeDtypeStruct((1, LANES), F32),
                   jax.ShapeDtypeStruct((1, d), F32), jax.ShapeDtypeStruct((1, d), F32)],
        compiler_params=_params("arbitrary"))(x1, ffn, target, g, b)


HALO = 16


def _conv_rows(e, w_ref, b_ref):
    y = b_ref[...] + w_ref[0:1, :] * pltpu.roll(e, 2, 0)
    y = y + w_ref[1:2, :] * pltpu.roll(e, 1, 0)
    return y + w_ref[2:3, :] * e


_GELU_C = math.sqrt(2.0 / math.pi)
_GELU_A = 0.044715


def _gelu(x):
    return 0.5 * x * (1.0 + jnp.tanh(_GELU_C * (x + _GELU_A * (x * x * x))))


def _gelu_grad(x):
    t = jnp.tanh(_GELU_C * (x + _GELU_A * (x * x * x)))
    return 0.5 * (1.0 + t) + 0.5 * x * (1.0 - t * t) * (_GELU_C * (1.0 + 3.0 * _GELU_A * (x * x)))


def _conv_gate_fwd(u, conv_w, conv_b, tm=512, tn=256):
    seq = u.shape[0]
    tm = min(tm, seq)
    nj = D_FF // tn

    def body(ua_ref, uap_ref, ug_ref, ugp_ref, wa_ref, wg_ref, ba_ref, bg_ref, o_ref):
        first = pl.program_id(0) == 0

        def ext(tile_ref, halo_ref):
            return jnp.concatenate([jnp.where(first, 0.0, halo_ref[...]), tile_ref[...]], axis=0)

        ya = _conv_rows(ext(ua_ref, uap_ref), wa_ref, ba_ref)[HALO:]
        yg = _conv_rows(ext(ug_ref, ugp_ref), wg_ref, bg_ref)[HALO:]
        o_ref[...] = (_gelu(yg) * ya).astype(o_ref.dtype)

    hb = tm // HALO
    tile = lambda off: pl.BlockSpec((tm, tn), lambda i, j: (i, j + off))
    prev = lambda off: pl.BlockSpec((HALO, tn), lambda i, j: (jnp.maximum(i * hb - 1, 0), j + off))
    wspec = lambda off: pl.BlockSpec((3, tn), lambda i, j: (0, j + off))
    bspec = lambda off: pl.BlockSpec((1, tn), lambda i, j: (0, j + off))
    return pl.pallas_call(
        body, name="conv_gate_fwd", grid=(seq // tm, nj),
        in_specs=[tile(0), prev(0), tile(nj), prev(nj), wspec(0), wspec(nj), bspec(0), bspec(nj)],
        out_specs=pl.BlockSpec((tm, tn), lambda i, j: (i, j)), out_shape=jax.ShapeDtypeStruct((seq, D_FF), MXU_DTYPE),
        compiler_params=_params("parallel", "parallel"),
    )(u, u, u, u, conv_w, conv_w, conv_b, conv_b)


def _conv_gate_bwd(u, d_act, conv_w, conv_b, tm=512, tn=256):
    seq = u.shape[0]
    tm = min(tm, seq)
    nj = D_FF // tn
    ni = seq // tm
    rows_e = tm + 2 * HALO

    def body(ua_ref, uap_ref, uan_ref, ug_ref, ugp_ref, ugn_ref, da_ref, dan_ref, wa_ref, wg_ref, ba_ref, bg_ref,
             du_ref, dw_ref, db_ref):
        half = pl.program_id(0)
        i = pl.program_id(2)
        first, last = i == 0, i == ni - 1

        @pl.when(i == 0)
        def _():
            dw_ref[...] = jnp.zeros_like(dw_ref)
            db_ref[...] = jnp.zeros_like(db_ref)

        def ext(tile_ref, prev_ref, next_ref):
            return jnp.concatenate([jnp.where(first, 0.0, prev_ref[...]), tile_ref[...],
                                    jnp.where(last, 0.0, next_ref[...])], axis=0)

        ea = ext(ua_ref, uap_ref, uan_ref)
        eg = ext(ug_ref, ugp_ref, ugn_ref)
        ya = _conv_rows(ea, wa_ref, ba_ref)
        yg = _conv_rows(eg, wg_ref, bg_ref)
        dact = jnp.concatenate([jnp.zeros((HALO, tn), F32), da_ref[...].astype(F32),
                                jnp.where(last, 0.0, dan_ref[...].astype(F32))], axis=0)

        def finish(dy, e_own, w_ref):
            du = w_ref[2:3, :] * dy + w_ref[1:2, :] * pltpu.roll(dy, rows_e - 1, 0) + w_ref[0:1, :] * pltpu.roll(dy, rows_e - 2, 0)
            du_ref[...] = du[HALO:HALO + tm].astype(du_ref.dtype)
            dyt = dy[HALO:HALO + tm]
            dw_ref[0:1, :] += jnp.sum(dyt * pltpu.roll(e_own, 2, 0)[HALO:HALO + tm], axis=0, keepdims=True)
            dw_ref[1:2, :] += jnp.sum(dyt * pltpu.roll(e_own, 1, 0)[HALO:HALO + tm], axis=0, keepdims=True)
            dw_ref[2:3, :] += jnp.sum(dyt * e_own[HALO:HALO + tm], axis=0, keepdims=True)
            db_ref[...] += jnp.sum(dyt, axis=0, keepdims=True)

        @pl.when(half == 0)
        def _():
            finish(dact * _gelu(yg), ea, wa_ref)

        @pl.when(half == 1)
        def _():
            finish(dact * ya * _gelu_grad(yg), eg, wg_ref)

    hb = tm // HALO
    nh = seq // HALO
    tile = lambda off: pl.BlockSpec((tm, tn), lambda hf, j, i: (i, j + off))
    prev = lambda off: pl.BlockSpec((HALO, tn), lambda hf, j, i: (jnp.maximum(i * hb - 1, 0), j + off))
    nxt = lambda off: pl.BlockSpec((HALO, tn), lambda hf, j, i: (jnp.minimum((i + 1) * hb, nh - 1), j + off))
    wspec = lambda off: pl.BlockSpec((3, tn), lambda hf, j, i: (0, j + off))
    bspec = lambda off: pl.BlockSpec((1, tn), lambda hf, j, i: (0, j + off))
    return pl.pallas_call(
        body, name="conv_gate_bwd", grid=(2, nj, ni),
        in_specs=[tile(0), prev(0), nxt(0), tile(nj), prev(nj), nxt(nj), tile(0), nxt(0),
                  wspec(0), wspec(nj), bspec(0), bspec(nj)],
        out_specs=[pl.BlockSpec((tm, tn), lambda hf, j, i: (i, hf * nj + j)),
                   pl.BlockSpec((3, tn), lambda hf, j, i: (0, hf * nj + j)),
                   pl.BlockSpec((1, tn), lambda hf, j, i: (0, hf * nj + j))],
        out_shape=[jax.ShapeDtypeStruct((seq, 2 * D_FF), MXU_DTYPE), jax.ShapeDtypeStruct((3, 2 * D_FF), F32),
                   jax.ShapeDtypeStruct((1, 2 * D_FF), F32)],
        compiler_params=_params("parallel", "parallel", "arbitrary"),
    )(u, u, u, u, u, u, d_act, d_act, conv_w, conv_w, conv_b, conv_b)


def _pad_heads(w, width):
    w = jnp.transpose(w, (1, 0, 2))
    return jnp.pad(w, ((0, 0), (0, 0), (0, LANES - width))).astype(MXU_DTYPE)


def _pad_w_in(w):
    z = lambda n: jnp.zeros((w.shape[0], n), w.dtype)
    return jnp.concatenate([w[:, :384], z(64), w[:, 384:416], z(32), w[:, 416:]], axis=1)


def _unpad_w_in(w):
    return jnp.concatenate([w[:, :384], w[:, 448:480], w[:, 512:]], axis=1)


def _layer_grads(x0, target, w_in, g_cq, g_ckv, w_uq, w_uk, w_uv, w_o, ln1_g, ln1_b, w_up, conv_w, conv_b, w_down,
                 ln2_g, ln2_b):
    seq = x0.shape[0]
    row = lambda v: v.reshape(1, -1).astype(F32)
    w_in_p = _pad_w_in(w_in).astype(MXU_DTYPE)
    wq = _pad_heads(w_uq, NOPE_DIM + ROPE_DIM)
    wk = _pad_heads(w_uk, NOPE_DIM)
    wv = _pad_heads(w_uv, HEAD_DIM)
    w_o_mla = jnp.pad(w_o[:512].reshape(HEADS, HEAD_DIM, D_MODEL), ((0, 0), (0, LANES - HEAD_DIM), (0, 0)))
    w_o_mla = w_o_mla.reshape(HEADS * LANES, D_MODEL).astype(MXU_DTYPE)
    w_o_dil = w_o[512:].astype(MXU_DTYPE)
    w_up = w_up.astype(MXU_DTYPE)
    w_down = w_down.astype(MXU_DTYPE)
    conv_w = conv_w.astype(F32)
    ctab, stab = _rope_tables(seq)
    gq, gk = row(g_cq), row(g_ckv)
    cb = row(conv_b)

    h = _mm(x0, w_in_p, name="mm_h")
    qf, kf, vp = _mla_prep(h, gq, gk, wq, wk, wv, ctab, stab)
    o_mla, lse_mla = _mla_attn_fwd(qf, kf, vp)
    branch = [_dil_fwd(h, dil) for _, dil in DIL_PAIRS]
    o_dil, lse_dil = _dil_combine([b[0] for b in branch], [b[1] for b in branch])
    mix_a = _mm(o_mla, w_o_mla, name="mm_mix_mla", a_heads=True)
    mix_b = _mm(o_dil, w_o_dil, name="mm_mix_dil")
    z1, x1 = _ln1_fwd(x0, mix_a, mix_b, row(ln1_g), row(ln1_b))
    u = _mm(x1, w_up, name="mm_up", tn=512)
    act = _conv_gate_fwd(u, conv_w, cb)
    ffn = _mm(act, w_down, name="mm_down")
    dz2, loss, d_ln2_g, d_ln2_b = _ln2_loss_bwd(x1, ffn, target, row(ln2_g), row(ln2_b))

    d_act = _mm(dz2, w_down, name="mm_d_act", tb=True, out_dtype=MXU_DTYPE)
    d_w_down = _mm(act, dz2, name="mm_dw_down", ta=True, out_dtype=MXU_DTYPE)
    du, d_conv_w, d_conv_b = _conv_gate_bwd(u, d_act, conv_w, cb)
    dx1 = _mm(du, w_up, name="mm_dx1", tb=True, res=dz2, res_scale=DN_ALPHA)
    d_w_up = _mm(x1, du, name="mm_dw_up", ta=True, out_dtype=MXU_DTYPE, tn=512)
    dz1, d_ln1_g, d_ln1_b = _ln_bwd(dx1, z1, row(ln1_g), "ln1_bwd")
    do_mla = _mm(dz1, w_o_mla, name="mm_do_mla", tb=True, out_heads=True)
    do_dil = _mm(dz1, w_o_dil, name="mm_do_dil", tb=True)
    d_w_o_mla = _mm(o_mla, dz1, name="mm_dw_o_mla", ta=True, a_heads=True, out_dtype=MXU_DTYPE)
    d_w_o_dil = _mm(o_dil, dz1, name="mm_dw_o_dil", ta=True, out_dtype=MXU_DTYPE)
    dqs, dks, dvs = [], [], []
    for _, dil in DIL_PAIRS:
        dqs.append(_dil_bwd_q(h, o_dil, lse_dil, do_dil, dil))
        dk_b, dv_b = _dil_bwd_kv(h, o_dil, lse_dil, do_dil, dil)
        dks.append(dk_b)
        dvs.append(dv_b)
    dqf, dkf, dvf = _mla_attn_bwd(qf, kf, vp, o_mla, lse_mla, do_mla)
    dh_mla, d_wq, d_wk, d_wv, d_gq, d_gk = _mla_prep_bwd(h, gq, gk, wq, wk, wv, ctab, stab, dqf, dkf, dvf)
    dh = _assemble_dh(dh_mla, dqs, dks, dvs)
    grad_x = _mm(dh, w_in_p, name="mm_dx0", tb=True, res=dz1, res_scale=DN_ALPHA)
    d_w_in = _unpad_w_in(_mm(x0, dh, name="mm_dw_in", ta=True, out_dtype=MXU_DTYPE))

    d_w_o = jnp.concatenate([d_w_o_mla.reshape(HEADS, LANES, D_MODEL)[:, :HEAD_DIM].reshape(512, D_MODEL), d_w_o_dil], axis=0)
    unhead = lambda d, width: jnp.transpose(d[:, :, :width], (1, 0, 2))
    grads = dict(
        w_in=d_w_in, g_cq=d_gq.reshape(-1), g_ckv=d_gk.reshape(-1), w_uq=unhead(d_wq, NOPE_DIM + ROPE_DIM),
        w_uk=unhead(d_wk, NOPE_DIM), w_uv=unhead(d_wv, HEAD_DIM), w_o=d_w_o, ln1_g=d_ln1_g.reshape(-1), ln1_b=d_ln1_b.reshape(-1),
        w_up=d_w_up, conv_w=d_conv_w, conv_b=d_conv_b.reshape(-1), w_down=d_w_down, ln2_g=d_ln2_g.reshape(-1),
        ln2_b=d_ln2_b.reshape(-1))
    return loss[0, 0], grad_x, grads


def _all_gather(blocks, name):
    na = len(blocks)

    def body(*refs):
        ins, outs = refs[:na], refs[na:2 * na]
        send_sems, recv_sems, local_sems = refs[2 * na:]
        x, y, c = lax.axis_index("x"), lax.axis_index("y"), lax.axis_index("c")
        me, sibling = (x, y, c), (x, y, 1 - c)
        chips = [(1 - x, y), (x, 1 - y), (1 - x, 1 - y)]

        def slot(out, pos):
            return out.at[4 * pos[0] + 2 * pos[1] + pos[2]]

        def copy(a, k, block, to, src=None):
            return pltpu.make_async_remote_copy(
                src_ref=slot(outs[a], block) if src is None else src, dst_ref=slot(outs[a], block),
                send_sem=send_sems.at[7 * a + k], recv_sem=recv_sems.at[7 * a + k],
                device_id=to, device_id_type=pl.DeviceIdType.MESH)

        mine = [pltpu.make_async_copy(ins[a], slot(outs[a], me), local_sems.at[a]) for a in range(na)]
        for cp in mine:
            cp.start()
        first = []
        for a in range(na):
            first.append(copy(a, 0, me, sibling, src=ins[a]))
            first += [copy(a, 1 + j, me, (*chip, c), src=ins[a]) for j, chip in enumerate(chips)]
        for cp in first:
            cp.start()
        passed = []
        for j, chip in enumerate(chips):
            for a in range(na):
                copy(a, 1 + j, (*chip, c), me).wait_recv()
                cp = copy(a, 4 + j, (*chip, c), sibling)
                cp.start()
                passed.append(cp)
        for a in range(na):
            copy(a, 0, sibling, me).wait_recv()
            for j, chip in enumerate(chips):
                copy(a, 4 + j, (*chip, 1 - c), me).wait_recv()
        for cp in first + passed:
            cp.wait_send()
        for cp in mine:
            cp.wait()

    any_spec = pl.BlockSpec(memory_space=pl.ANY)
    return pl.pallas_call(
        body, name=name, in_specs=[any_spec] * na, out_specs=[any_spec] * na,
        out_shape=[jax.ShapeDtypeStruct((N_DEV,) + b.shape, b.dtype) for b in blocks],
        scratch_shapes=[pltpu.SemaphoreType.DMA((7 * na,)), pltpu.SemaphoreType.DMA((7 * na,)), pltpu.SemaphoreType.DMA((na,))],
    )(*blocks)


def _exchange(parts, name):
    def body(in_ref, out_ref, send_sems, recv_sems, local_sem):
        x, y, c = lax.axis_index("x"), lax.axis_index("y"), lax.axis_index("c")
        me = 4 * x + 2 * y + c
        mine = pltpu.make_async_copy(in_ref.at[me], out_ref.at[me], local_sem)
        mine.start()
        copies = []
        for d in range(1, N_DEV):
            px, py, pc = x ^ (d >> 2), y ^ ((d >> 1) & 1), c ^ (d & 1)
            peer = 4 * px + 2 * py + pc
            cp = pltpu.make_async_remote_copy(
                src_ref=in_ref.at[peer], dst_ref=out_ref.at[me], send_sem=send_sems.at[d - 1], recv_sem=recv_sems.at[d - 1],
                device_id=(px, py, pc), device_id_type=pl.DeviceIdType.MESH)
            cp.start()
            copies.append((cp, peer))
        for d in range(1, N_DEV):
            px, py, pc = x ^ (d >> 2), y ^ ((d >> 1) & 1), c ^ (d & 1)
            peer = 4 * px + 2 * py + pc
            pltpu.make_async_remote_copy(
                src_ref=in_ref.at[me], dst_ref=out_ref.at[peer], send_sem=send_sems.at[d - 1], recv_sem=recv_sems.at[d - 1],
                device_id=(px, py, pc), device_id_type=pl.DeviceIdType.MESH).wait_recv()
        for cp, _ in copies:
            cp.wait_send()
        mine.wait()

    any_spec = pl.BlockSpec(memory_space=pl.ANY)
    return pl.pallas_call(
        body, name=name, in_specs=[any_spec], out_specs=any_spec, out_shape=jax.ShapeDtypeStruct(parts.shape, parts.dtype),
        scratch_shapes=[pltpu.SemaphoreType.DMA((N_DEV - 1,)), pltpu.SemaphoreType.DMA((N_DEV - 1,)), pltpu.SemaphoreType.DMA],
    )(parts)


def _adamw(parts, w, m, v, name):
    npart, r, n = parts.shape
    tr = r if r <= 256 else max(t for t in range(16, 257, 16) if r % t == 0)
    c1 = 1.0 - ADAM_B1 ** ADAM_STEP
    c2 = 1.0 - ADAM_B2 ** ADAM_STEP

    def body(p_ref, w_ref, m_ref, v_ref, g_out, d_out, m_out, v_out):
        g = p_ref[0].astype(F32)
        for s in range(1, npart):
            g = g + p_ref[s].astype(F32)
        m_new = ADAM_B1 * m_ref[...] + (1.0 - ADAM_B1) * g
        v_new = ADAM_B2 * v_ref[...] + (1.0 - ADAM_B2) * (g * g)
        g_out[...] = g
        m_out[...] = m_new
        v_out[...] = v_new
        d_out[...] = -ADAM_LR * ((m_new / c1) / (jnp.sqrt(v_new / c2) + ADAM_EPS) + ADAM_WD * w_ref[...])

    blk = pl.BlockSpec((tr, n), lambda i: (i, 0))
    shp = jax.ShapeDtypeStruct((r, n), F32)
    return pl.pallas_call(
        body, name=name, grid=(r // tr,), in_specs=[pl.BlockSpec((npart, tr, n), lambda i: (0, i, 0)), blk, blk, blk],
        out_specs=[blk] * 4, out_shape=[shp] * 4, compiler_params=_params("parallel"),
    )(parts, w, m, v)


def _pack(arrs, dtype, row_multiple=16):
    flat = jnp.concatenate([a.reshape(-1).astype(dtype) for a in arrs])
    rows = -(-flat.shape[0] // PACK_COLS)
    rows = -(-rows // row_multiple) * row_multiple
    return jnp.pad(flat, (0, rows * PACK_COLS - flat.shape[0])).reshape(rows, PACK_COLS)


def _unpack(buf, shapes):
    lead = buf.shape[:-2]
    flat = buf.reshape(lead + (-1,))
    out, at = [], 0
    for s in shapes:
        size = math.prod(s)
        out.append(flat[..., at:at + size].reshape(lead + tuple(s)))
        at += size
    return out


SHARDED = ("w_in", "w_uq", "w_o", "w_up", "w_down")
REPLICATED = ("g_cq", "g_ckv", "w_uk", "w_uv", "ln1_g", "ln1_b", "conv_b", "ln2_g", "ln2_b")
ALL_WEIGHTS = ("w_in", "g_cq", "g_ckv", "w_uq", "w_uk", "w_uv", "w_o", "ln1_g", "ln1_b", "w_up", "conv_w", "conv_b",
               "w_down", "ln2_g", "ln2_b")


def kernel(x, w_in, g_cq, g_ckv, w_uq, w_uk, w_uv, w_o, ln1_g, ln1_b, w_up, conv_w, conv_b, w_down, ln2_g, ln2_b, loss_target, m_w_in, m_g_cq, m_g_ckv, m_w_uq, m_w_uk, m_w_uv, m_w_o, m_ln1_g, m_ln1_b, m_w_up, m_conv_w, m_conv_b, m_w_down, m_ln2_g, m_ln2_b, v_w_in, v_g_cq, v_g_ckv, v_w_uq, v_w_uk, v_w_uv, v_w_o, v_ln1_g, v_ln1_b, v_w_up, v_conv_w, v_conv_b, v_w_down, v_ln2_g, v_ln2_b):
    w = dict(w_in=w_in, g_cq=g_cq, g_ckv=g_ckv, w_uq=w_uq, w_uk=w_uk, w_uv=w_uv, w_o=w_o, ln1_g=ln1_g, ln1_b=ln1_b,
             w_up=w_up, conv_w=conv_w, conv_b=conv_b, w_down=w_down, ln2_g=ln2_g, ln2_b=ln2_b)
    m = dict(w_in=m_w_in, g_cq=m_g_cq, g_ckv=m_g_ckv, w_uq=m_w_uq, w_uk=m_w_uk, w_uv=m_w_uv, w_o=m_w_o, ln1_g=m_ln1_g,
             ln1_b=m_ln1_b, w_up=m_w_up, conv_w=m_conv_w, conv_b=m_conv_b, w_down=m_w_down, ln2_g=m_ln2_g, ln2_b=m_ln2_b)
    v = dict(w_in=v_w_in, g_cq=v_g_cq, g_ckv=v_g_ckv, w_uq=v_w_uq, w_uk=v_w_uk, w_uv=v_w_uv, w_o=v_w_o, ln1_g=v_ln1_g,
             ln1_b=v_ln1_b, w_up=v_w_up, conv_w=v_conv_w, conv_b=v_conv_b, w_down=v_w_down, ln2_g=v_ln2_g, ln2_b=v_ln2_b)
    me = 4 * lax.axis_index("x") + 2 * lax.axis_index("y") + lax.axis_index("c")
    shard_shapes = [w[n].shape for n in SHARDED]

    conv_w_pad = jnp.pad(conv_w, ((0, 8 - conv_w.shape[0]), (0, 0)))
    g_pack, g_conv = _all_gather([_pack([w[n] for n in SHARDED], WIRE_DTYPE), conv_w_pad], "gather_weights")
    gw_in, gw_uq, gw_o, gw_up, gw_down = _unpack(g_pack, shard_shapes)
    cols = lambda g: jnp.transpose(g, (1, 0, 2)).reshape(g.shape[1], -1)
    full = dict(w)
    full["w_in"] = cols(gw_in)
    full["w_uq"] = gw_uq.reshape((-1,) + gw_uq.shape[2:])
    full["w_o"] = gw_o.reshape(-1, gw_o.shape[-1])
    full["w_up"] = cols(gw_up)
    full["w_down"] = gw_down.reshape(-1, gw_down.shape[-1])
    full["conv_w"] = cols(g_conv[:, :3])

    loss, grad_x, grads = _layer_grads(x[0], loss_target[0], *[full[n] for n in ALL_WEIGHTS])
    loss = lax.psum(loss, MESH_AXES)

    def blocks(name):
        g = grads[name]
        if name in ("w_in", "w_up"):
            return jnp.transpose(g.reshape(g.shape[0], N_DEV, -1), (1, 0, 2))
        return g.reshape((N_DEV, g.shape[0] // N_DEV) + g.shape[1:])

    per_dest = [blocks(n) for n in SHARDED]
    send = jnp.stack([_pack([b[j] for b in per_dest], WIRE_DTYPE) for j in range(N_DEV)])
    recv = _exchange(send, "exchange_grads")
    d_conv_w_pad = jnp.pad(grads["conv_w"], ((0, 8 - conv_w.shape[0]), (0, 0)))
    rep_all, cw_all = _all_gather([_pack([grads[n] for n in REPLICATED], F32, 8), d_conv_w_pad], "gather_small_grads")

    out = {}
    res = _adamw(recv, *[_pack([d[n] for n in SHARDED], F32) for d in (w, m, v)], "adamw_sharded")
    for kind, buf in zip(("grad", "delta", "new_m", "new_v"), res):
        for n, a in zip(SHARDED, _unpack(buf, shard_shapes)):
            out[kind, n] = a
    res = _adamw(rep_all, *[_pack([d[n] for n in REPLICATED], F32, 8) for d in (w, m, v)], "adamw_replicated")
    for kind, buf in zip(("grad", "delta", "new_m", "new_v"), res):
        for n, a in zip(REPLICATED, _unpack(buf, [w[n].shape for n in REPLICATED])):
            out[kind, n] = a
    ncw = conv_w.shape[1]
    cw_parts = lax.dynamic_slice_in_dim(cw_all[:, :conv_w.shape[0]], me * ncw, ncw, axis=2)
    res = _adamw(cw_parts, conv_w, m["conv_w"], v["conv_w"], "adamw_conv_w")
    for kind, a in zip(("grad", "delta", "new_m", "new_v"), res):
        out[kind, "conv_w"] = a

    return (loss, grad_x[None], *[out[kind, n] for kind in ("grad", "delta", "new_m", "new_v") for n in ALL_WEIGHTS])
```

```python
import functools
import math

import jax
import jax.numpy as jnp
from jax import lax
from jax.experimental import pallas as pl
from jax.experimental.pallas import tpu as pltpu

F32 = jnp.float32
MXU_DTYPE = jnp.bfloat16
WIRE_DTYPE = jnp.bfloat16

N_DEV = 8
D_MODEL = 1024
HEADS = 8
HEAD_DIM = 64
LANES = 128
Q_RANK, KV_RANK, ROPE_DIM, NOPE_DIM = 256, 128, 32, 64
DIL_WIDTH = HEADS * HEAD_DIM
IN_WIDTH = 1952
IN_PAD = 2048
D_FF = 2816
ROPE_THETA = 10000.0
DIL_PAIRS = ((128, 1), (512, 4), (2048, 16))
DIL_BLOCK = 128
DN_ALPHA = 2.0 ** 0.25
LN_EPS = 1e-5
RMS_EPS = 1e-6
MLA_SCALE = 1.0 / math.sqrt(NOPE_DIM + ROPE_DIM)
DIL_SCALE = 1.0 / math.sqrt(HEAD_DIM)
ALIBI_SLOPES = tuple(2.0 ** (-8.0 * (h + 1) / HEADS) for h in range(HEADS))
NEG_BIG = -1e30
ADAM_LR, ADAM_B1, ADAM_B2, ADAM_EPS, ADAM_WD, ADAM_STEP = 0.001, 0.9, 0.999, 1e-08, 0.01, 10
VMEM_LIMIT = 48 * 1024 * 1024
PACK_COLS = 1024

MESH_AXES = ("x", "y", "c")


def _params(*sem):
    return pltpu.CompilerParams(dimension_semantics=sem or None, vmem_limit_bytes=VMEM_LIMIT)


def _dot(a, b, ca, cb):
    return lax.dot_general(a, b, (((ca,), (cb,)), ((), ())), preferred_element_type=F32)


def _mm(a, b, *, name, tm, tn, tk, ta=False, tb=False, out_dtype=F32, res=None, res_scale=1.0):
    m, k = (a.shape[1], a.shape[0]) if ta else a.shape
    n = b.shape[0] if tb else b.shape[1]
    assert (b.shape[1] if tb else b.shape[0]) == k
    tm, tn, tk = min(tm, m), min(tn, n), min(tk, k)
    assert m % tm == 0 and n % tn == 0 and k % tk == 0, (name, m, n, k, tm, tn, tk)
    nk = k // tk
    a_spec = (pl.BlockSpec((tk, tm), lambda i, j, kk: (kk, i)) if ta
              else pl.BlockSpec((tm, tk), lambda i, j, kk: (i, kk)))
    b_spec = (pl.BlockSpec((tn, tk), lambda i, j, kk: (j, kk)) if tb
              else pl.BlockSpec((tk, tn), lambda i, j, kk: (kk, j)))
    o_spec = pl.BlockSpec((tm, tn), lambda i, j, kk: (i, j))
    in_specs = [a_spec, b_spec]
    args = [a, b]
    if res is not None:
        in_specs.append(o_spec)
        args.append(res)
    ca, cb = (0 if ta else 1), (1 if tb else 0)

    def finish(acc, r_ref, o_ref):
        if r_ref is not None:
            acc = acc + res_scale * r_ref[...]
        o_ref[...] = acc.astype(o_ref.dtype)

    def body(*refs):
        a_ref, b_ref = refs[:2]
        r_ref = refs[2] if res is not None else None
        o_ref = refs[3] if res is not None else refs[2]
        part = _dot(a_ref[...].astype(MXU_DTYPE), b_ref[...].astype(MXU_DTYPE), ca, cb)
        if nk == 1:
            finish(part, r_ref, o_ref)
            return
        acc_ref = refs[-1]
        kk = pl.program_id(2)

        @pl.when(kk == 0)
        def _():
            acc_ref[...] = part

        @pl.when(kk > 0)
        def _():
            acc_ref[...] += part

        @pl.when(kk == nk - 1)
        def _():
            finish(acc_ref[...], r_ref, o_ref)

    return pl.pallas_call(
        body, name=name, grid=(m // tm, n // tn, nk), in_specs=in_specs, out_specs=o_spec,
        out_shape=jax.ShapeDtypeStruct((m, n), out_dtype),
        scratch_shapes=[pltpu.VMEM((tm, tn), F32)] if nk > 1 else [],
        compiler_params=_params("parallel", "parallel", "arbitrary"),
    )(*args)


def _mm_do_mla(dz, w_o_mla, tm=1024):
    seq, d = dz.shape
    tm = min(tm, seq)

    def body(a_ref, b_ref, o_ref):
        a = a_ref[...].astype(MXU_DTYPE)
        for hd in range(HEADS):
            o_ref[hd] = _dot(a, b_ref[LANES * hd:LANES * (hd + 1), :], 1, 1)

    return pl.pallas_call(
        body, name="mm_do_mla", grid=(seq // tm,),
        in_specs=[pl.BlockSpec((tm, d), lambda i: (i, 0)), pl.BlockSpec((HEADS * LANES, d), lambda i: (0, 0))],
        out_specs=pl.BlockSpec((HEADS, tm, LANES), lambda i: (0, i, 0)),
        out_shape=jax.ShapeDtypeStruct((HEADS, seq, LANES), F32), compiler_params=_params("parallel"),
    )(dz, w_o_mla)


def _mm_dw_o_mla(o_mla, dz, tk=1024):
    seq, d = dz.shape
    tk = min(tk, seq)
    nk = seq // tk

    def body(a_ref, b_ref, o_ref, acc_ref):
        kk = pl.program_id(0)

        @pl.when(kk == 0)
        def _():
            acc_ref[...] = jnp.zeros_like(acc_ref)

        b = b_ref[...].astype(MXU_DTYPE)
        for hd in range(HEADS):
            acc_ref[LANES * hd:LANES * (hd + 1), :] += _dot(a_ref[hd].astype(MXU_DTYPE), b, 0, 0)

        @pl.when(kk == nk - 1)
        def _():
            o_ref[...] = acc_ref[...].astype(o_ref.dtype)

    return pl.pallas_call(
        body, name="mm_dw_o_mla", grid=(nk,),
        in_specs=[pl.BlockSpec((HEADS, tk, LANES), lambda kk: (0, kk, 0)), pl.BlockSpec((tk, d), lambda kk: (kk, 0))],
        out_specs=pl.BlockSpec((HEADS * LANES, d), lambda kk: (0, 0)),
        out_shape=jax.ShapeDtypeStruct((HEADS * LANES, d), MXU_DTYPE),
        scratch_shapes=[pltpu.VMEM((HEADS * LANES, d), F32)], compiler_params=_params("arbitrary"),
    )(o_mla, dz)


def _rope_tables(seq):
    half = ROPE_DIM // 2
    freqs = ROPE_THETA ** (-jnp.arange(half, dtype=F32) / half)
    ang = jnp.arange(seq).astype(F32)[:, None] * freqs[None, :]
    cos, sin = jnp.cos(ang), jnp.sin(ang)
    one = jnp.ones((seq, NOPE_DIM), F32)
    tail = jnp.ones((seq, LANES - NOPE_DIM - ROPE_DIM), F32)
    ctab = jnp.concatenate([one, cos, cos, tail], axis=1)
    stab = jnp.concatenate([0 * one, -sin, sin, 0 * tail], axis=1)
    return ctab, stab


def _rope_swap(t):
    lane = lax.broadcasted_iota(jnp.int32, t.shape, 1)
    half = ROPE_DIM // 2
    return jnp.where(lane < NOPE_DIM + half, pltpu.roll(t, LANES - half, 1), pltpu.roll(t, half, 1))


def _rope(t, ctab, stab):
    return t * ctab + _rope_swap(t) * stab


def _rope_inv(t, ctab, stab):
    return t * ctab - _rope_swap(t) * stab


def _rms(x, g):
    r = lax.rsqrt(jnp.mean(x * x, axis=-1, keepdims=True) + RMS_EPS)
    xh = x * r
    return xh, r, xh * g


def _mla_prep(h, g_cq, g_ckv, wq, wk, wv, ctab, stab, tm=512):
    seq = h.shape[0]
    tm = min(tm, seq)

    def body(h_ref, gq_ref, gk_ref, wq_ref, wk_ref, wv_ref, c_ref, s_ref, q_out, k_out, v_out):
        hb = h_ref[...]
        ctab_, stab_ = c_ref[...], s_ref[...]
        _, _, cqn = _rms(hb[:, :Q_RANK], gq_ref[...])
        _, _, ckn = _rms(hb[:, Q_RANK:Q_RANK + KV_RANK], gk_ref[...])
        cqn = cqn.astype(MXU_DTYPE)
        ckn = ckn.astype(MXU_DTYPE)
        krr = _rope(hb[:, Q_RANK + KV_RANK:], ctab_, stab_)
        for hd in range(HEADS):
            q = _dot(cqn, wq_ref[hd], 1, 0)
            q_out[hd] = _rope(q, ctab_, stab_).astype(q_out.dtype)
            k_out[hd] = (_dot(ckn, wk_ref[hd], 1, 0) + krr).astype(k_out.dtype)
            v_out[hd] = _dot(ckn, wv_ref[hd], 1, 0).astype(v_out.dtype)

    full = lambda *shape: pl.BlockSpec(shape, lambda i: (0,) * len(shape))
    slab = pl.BlockSpec((HEADS, tm, LANES), lambda i: (0, i, 0))
    shp = jax.ShapeDtypeStruct((HEADS, seq, LANES), MXU_DTYPE)
    return pl.pallas_call(
        body, name="mla_prep", grid=(seq // tm,),
        in_specs=[pl.BlockSpec((tm, 512), lambda i: (i, 0)), full(1, Q_RANK), full(1, KV_RANK),
                  full(HEADS, Q_RANK, LANES), full(HEADS, KV_RANK, LANES), full(HEADS, KV_RANK, LANES),
                  pl.BlockSpec((tm, LANES), lambda i: (i, 0)), pl.BlockSpec((tm, LANES), lambda i: (i, 0))],
        out_specs=[slab, slab, slab], out_shape=[shp, shp, shp],
        compiler_params=_params("parallel"),
    )(h, g_cq, g_ckv, wq, wk, wv, ctab, stab)


def _mla_prep_bwd(h, g_cq, g_ckv, wq, wk, wv, ctab, stab, dq, dk, dv, tm=512):
    seq = h.shape[0]
    tm = min(tm, seq)

    def body(h_ref, gq_ref, gk_ref, wq_ref, wk_ref, wv_ref, c_ref, s_ref, dq_ref, dk_ref, dv_ref,
             dh_ref, dwq_ref, dwk_ref, dwv_ref, dgq_ref, dgk_ref):
        @pl.when(pl.program_id(0) == 0)
        def _():
            for r in (dwq_ref, dwk_ref, dwv_ref, dgq_ref, dgk_ref):
                r[...] = jnp.zeros_like(r)

        hb = h_ref[...]
        ctab_, stab_ = c_ref[...], s_ref[...]
        gq, gk = gq_ref[...], gk_ref[...]
        xq, rq, cqn = _rms(hb[:, :Q_RANK], gq)
        xk, rk, ckn = _rms(hb[:, Q_RANK:Q_RANK + KV_RANK], gk)
        cqn = cqn.astype(MXU_DTYPE)
        ckn = ckn.astype(MXU_DTYPE)
        d_cqn = jnp.zeros((tm, Q_RANK), F32)
        d_ckn = jnp.zeros((tm, KV_RANK), F32)
        d_krr = jnp.zeros((tm, LANES), F32)
        for hd in range(HEADS):
            dqh = _rope_inv(dq_ref[hd], ctab_, stab_).astype(MXU_DTYPE)
            d_cqn += _dot(dqh, wq_ref[hd], 1, 1)
            dwq_ref[hd] += _dot(cqn, dqh, 0, 0)
            dkh = dk_ref[hd]
            d_krr += dkh
            dkh = dkh.astype(MXU_DTYPE)
            d_ckn += _dot(dkh, wk_ref[hd], 1, 1)
            dwk_ref[hd] += _dot(ckn, dkh, 0, 0)
            dvh = dv_ref[hd].astype(MXU_DTYPE)
            d_ckn += _dot(dvh, wv_ref[hd], 1, 1)
            dwv_ref[hd] += _dot(ckn, dvh, 0, 0)
        lane = lax.broadcasted_iota(jnp.int32, (tm, LANES), 1)
        rot = (lane >= NOPE_DIM) & (lane < NOPE_DIM + ROPE_DIM)
        d_kr = jnp.where(rot, _rope_inv(jnp.where(rot, d_krr, 0.0), ctab_, stab_), 0.0)

        def rms_bwd(dy, xh, r, g, dg_ref):
            dg_ref[...] += jnp.sum(dy * xh, axis=0, keepdims=True)
            dxh = dy * g
            return r * (dxh - xh * jnp.mean(dxh * xh, axis=-1, keepdims=True))

        d_cq = rms_bwd(d_cqn, xq, rq, gq, dgq_ref)
        d_ck = rms_bwd(d_ckn, xk, rk, gk, dgk_ref)
        dh_ref[...] = jnp.concatenate([d_cq, d_ck, d_kr], axis=1).astype(dh_ref.dtype)

    full = lambda *shape: pl.BlockSpec(shape, lambda i: (0,) * len(shape))
    slab = pl.BlockSpec((HEADS, tm, LANES), lambda i: (0, i, 0))
    return pl.pallas_call(
        body, name="mla_prep_bwd", grid=(seq // tm,),
        in_specs=[pl.BlockSpec((tm, 512), lambda i: (i, 0)), full(1, Q_RANK), full(1, KV_RANK),
                  full(HEADS, Q_RANK, LANES), full(HEADS, KV_RANK, LANES), full(HEADS, KV_RANK, LANES),
                  pl.BlockSpec((tm, LANES), lambda i: (i, 0)), pl.BlockSpec((tm, LANES), lambda i: (i, 0)),
                  slab, slab, slab],
        out_specs=[pl.BlockSpec((tm, 512), lambda i: (i, 0)), full(HEADS, Q_RANK, LANES), full(HEADS, KV_RANK, LANES),
                   full(HEADS, KV_RANK, LANES), full(1, Q_RANK), full(1, KV_RANK)],
        out_shape=[jax.ShapeDtypeStruct((seq, 512), MXU_DTYPE), jax.ShapeDtypeStruct((HEADS, Q_RANK, LANES), F32),
                   jax.ShapeDtypeStruct((HEADS, KV_RANK, LANES), F32), jax.ShapeDtypeStruct((HEADS, KV_RANK, LANES), F32),
                   jax.ShapeDtypeStruct((1, Q_RANK), F32), jax.ShapeDtypeStruct((1, KV_RANK), F32)],
        compiler_params=_params("arbitrary"),
    )(h, g_cq, g_ckv, wq, wk, wv, ctab, stab, dq, dk, dv)


def _causal_mask(t):
    row = lax.broadcasted_iota(jnp.int32, (t, t), 0)
    col = lax.broadcasted_iota(jnp.int32, (t, t), 1)
    return row >= col


def _mla_attn_fwd(q, k, v, t=512):
    _, seq, _ = q.shape
    t = min(t, seq)

    def body(q_ref, k_ref, v_ref, o_ref, ob_ref, lse_ref, m_ref, l_ref, acc_ref):
        i = pl.program_id(1)
        qb = q_ref[...]
        m_ref[...] = jnp.full_like(m_ref, NEG_BIG)
        l_ref[...] = jnp.zeros_like(l_ref)
        acc_ref[...] = jnp.zeros_like(acc_ref)

        def step(j, masked):
            kb = k_ref[pl.ds(pl.multiple_of(j * t, t), t), :]
            vb = v_ref[pl.ds(pl.multiple_of(j * t, t), t), :]
            s = _dot(qb, kb, 1, 1) * MLA_SCALE
            if masked:
                s = jnp.where(_causal_mask(t), s, NEG_BIG)
            m_old = m_ref[...]
            m_new = jnp.maximum(m_old, jnp.max(s, axis=1, keepdims=True))
            p = jnp.exp(s - m_new)
            a = jnp.exp(m_old - m_new)
            l_ref[...] = a * l_ref[...] + jnp.sum(p, axis=1, keepdims=True)
            acc_ref[...] = a * acc_ref[...] + _dot(p.astype(MXU_DTYPE), vb, 1, 0)
            m_ref[...] = m_new

        def loop_body(j, c):
            step(j, False)
            return c

        lax.fori_loop(0, i, loop_body, 0)
        step(i, True)
        l = l_ref[...]
        o = acc_ref[...] / l
        o_ref[...] = o
        ob_ref[...] = o.astype(ob_ref.dtype)
        lse_ref[...] = jnp.broadcast_to(m_ref[...] + jnp.log(l), lse_ref.shape)

    blk = pl.BlockSpec((None, t, LANES), lambda h, i: (h, i, 0))
    whole = pl.BlockSpec((None, seq, LANES), lambda h, i: (h, 0, 0))
    shp = jax.ShapeDtypeStruct((HEADS, seq, LANES), F32)
    return pl.pallas_call(
        body, name="mla_attn_fwd", grid=(HEADS, seq // t),
        in_specs=[blk, whole, whole], out_specs=[blk, blk, blk],
        out_shape=[shp, jax.ShapeDtypeStruct((HEADS, seq, LANES), MXU_DTYPE), shp],
        scratch_shapes=[pltpu.VMEM((t, 1), F32), pltpu.VMEM((t, 1), F32), pltpu.VMEM((t, LANES), F32)],
        compiler_params=_params("parallel", "arbitrary"),
    )(q, k, v)


def _mla_attn_bwd(q, k, v, o, lse, do, t=512):
    _, seq, _ = q.shape
    t = min(t, seq)
    nb = seq // t

    def body(q_ref, k_ref, v_ref, o_ref, lse_ref, do_ref, dq_ref, dk_ref, dv_ref, dl_ref, dka_ref, dva_ref):
        dq_ref[...] = jnp.zeros_like(dq_ref)

        def delta_body(i, c):
            rows = pl.ds(pl.multiple_of(i * t, t), t)
            dl_ref[rows, :] = jnp.sum(do_ref[rows, :] * o_ref[rows, :], axis=1, keepdims=True)
            return c

        lax.fori_loop(0, nb, delta_body, 0)

        def kblock(j, c):
            krows = pl.ds(pl.multiple_of(j * t, t), t)
            kb = k_ref[krows, :]
            vb = v_ref[krows, :]
            dka_ref[...] = jnp.zeros_like(dka_ref)
            dva_ref[...] = jnp.zeros_like(dva_ref)

            def qstep(i, masked):
                rows = pl.ds(pl.multiple_of(i * t, t), t)
                qb = q_ref[rows, :]
                dob = do_ref[rows, :].astype(MXU_DTYPE)
                s = _dot(qb, kb, 1, 1) * MLA_SCALE
                if masked:
                    s = jnp.where(_causal_mask(t), s, NEG_BIG)
                p = jnp.exp(s - lse_ref[rows, 0:1])
                dva_ref[...] += _dot(p.astype(MXU_DTYPE), dob, 0, 0)
                dp = _dot(dob, vb, 1, 1)
                ds = (p * (dp - dl_ref[rows, :]) * MLA_SCALE).astype(MXU_DTYPE)
                dka_ref[...] += _dot(ds, qb, 0, 0)
                dq_ref[rows, :] += _dot(ds, kb, 1, 0)

            qstep(j, True)

            def qloop(i, c2):
                qstep(i, False)
                return c2

            lax.fori_loop(j + 1, nb, qloop, 0)
            dk_ref[krows, :] = dka_ref[...]
            dv_ref[krows, :] = dva_ref[...]
            return c

        lax.fori_loop(0, nb, kblock, 0)

    whole = pl.BlockSpec((None, seq, LANES), lambda h: (h, 0, 0))
    shp = jax.ShapeDtypeStruct((HEADS, seq, LANES), F32)
    return pl.pallas_call(
        body, name="mla_attn_bwd", grid=(HEADS,),
        in_specs=[whole] * 6, out_specs=[whole] * 3, out_shape=[shp] * 3,
        scratch_shapes=[pltpu.VMEM((seq, 1), F32), pltpu.VMEM((t, LANES), F32), pltpu.VMEM((t, LANES), F32)],
        compiler_params=_params("parallel"),
    )(q, k, v, o, lse, do)


def _dil_scores(qh, kh, lim, keys_side):
    s = _dot(qh, kh, 1, 1) * DIL_SCALE
    b = DIL_BLOCK
    if keys_side:
        iq = lax.broadcasted_iota(jnp.int32, (b, 2 * b), 0)
        ik = lax.broadcasted_iota(jnp.int32, (b, 2 * b), 1)
        off = iq + b - ik
        ok = (off >= 0) & (off <= b) & (ik >= lim)
    else:
        iq = lax.broadcasted_iota(jnp.int32, (2 * b, b), 0)
        ik = lax.broadcasted_iota(jnp.int32, (2 * b, b), 1)
        off = iq - ik
        ok = (off >= 0) & (off <= b) & (iq < lim)
    return s, off, ok


def _dil_views(h, dil):
    seq = h.shape[0]
    return h.reshape(seq // dil, dil * h.shape[1])


def _dil_fwd(h, dil):
    seq = h.shape[0]
    m = seq // dil
    nb = m // DIL_BLOCK
    b = DIL_BLOCK
    hv = _dil_views(h, dil)

    def body(q_ref, kp_ref, kc_ref, vp_ref, vc_ref, o_ref, lse_ref):
        n = pl.program_id(1)
        q = q_ref[...]
        kcat = jnp.concatenate([kp_ref[...], kc_ref[...]], axis=0)
        vcat = jnp.concatenate([vp_ref[...], vc_ref[...]], axis=0)
        lane = lax.broadcasted_iota(jnp.int32, (b, LANES), 1)
        lse_all = jnp.zeros((b, LANES), F32)
        for hd in range(HEADS):
            cols = slice(HEAD_DIM * hd, HEAD_DIM * (hd + 1))
            s, off, ok = _dil_scores(q[:, cols].astype(MXU_DTYPE), kcat[:, cols].astype(MXU_DTYPE), jnp.where(n > 0, 0, b), True)
            s = jnp.where(ok, s - (ALIBI_SLOPES[hd] * dil) * off.astype(F32), NEG_BIG)
            mx = jnp.max(s, axis=1, keepdims=True)
            p = jnp.exp(s - mx)
            l = jnp.sum(p, axis=1, keepdims=True)
            o_ref[:, cols] = _dot(p.astype(MXU_DTYPE), vcat[:, cols].astype(MXU_DTYPE), 1, 0) / l
            lse_all = jnp.where(lane == hd, mx + jnp.log(l), lse_all)
        lse_ref[...] = lse_all

    def col(c):
        return lambda r, n: (n, r * 4 + c)

    def col_prev(c):
        return lambda r, n: (jnp.maximum(n - 1, 0), r * 4 + c)

    blk = lambda im: pl.BlockSpec((b, DIL_WIDTH), im)
    o, lse = pl.pallas_call(
        body, name=f"dil_fwd_{dil}", grid=(dil, nb),
        in_specs=[blk(col(1)), blk(col_prev(2)), blk(col(2)), blk(col_prev(3)), blk(col(3))],
        out_specs=[pl.BlockSpec((b, DIL_WIDTH), lambda r, n: (n, r)), pl.BlockSpec((b, LANES), lambda r, n: (n, r))],
        out_shape=[jax.ShapeDtypeStruct((m, dil * DIL_WIDTH), F32), jax.ShapeDtypeStruct((m, dil * LANES), F32)],
        compiler_params=_params("parallel", "parallel"),
    )(hv, hv, hv, hv, hv)
    return o.reshape(seq, DIL_WIDTH), lse.reshape(seq, LANES)


def _dil_combine(os_, lses, tm=512):
    seq = os_[0].shape[0]
    tm = min(tm, seq)

    def body(o1, o2, o3, l1, l2, l3, o_ref, ob_ref, lse_ref):
        ls = [l1[...], l2[...], l3[...]]
        mx = jnp.maximum(jnp.maximum(ls[0], ls[1]), ls[2])
        es = [jnp.exp(l - mx) for l in ls]
        den = es[0] + es[1] + es[2]
        lse_ref[...] = mx + jnp.log(den)
        ws = [e / den for e in es]
        for hd in range(HEADS):
            cols = slice(HEAD_DIM * hd, HEAD_DIM * (hd + 1))
            acc = ws[0][:, hd:hd + 1] * o1[:, cols]
            acc += ws[1][:, hd:hd + 1] * o2[:, cols]
            acc += ws[2][:, hd:hd + 1] * o3[:, cols]
            o_ref[:, cols] = acc
            ob_ref[:, cols] = acc.astype(ob_ref.dtype)

    ob = pl.BlockSpec((tm, DIL_WIDTH), lambda i: (i, 0))
    lb = pl.BlockSpec((tm, LANES), lambda i: (i, 0))
    return pl.pallas_call(
        body, name="dil_combine", grid=(seq // tm,), in_specs=[ob] * 3 + [lb] * 3, out_specs=[ob, ob, lb],
        out_shape=[jax.ShapeDtypeStruct((seq, DIL_WIDTH), F32), jax.ShapeDtypeStruct((seq, DIL_WIDTH), MXU_DTYPE),
                   jax.ShapeDtypeStruct((seq, LANES), F32)],
        compiler_params=_params("parallel"),
    )(*os_, *lses)


def _dil_bwd_q(h, o, lse, do, dil):
    seq = h.shape[0]
    m = seq // dil
    nb = m // DIL_BLOCK
    b = DIL_BLOCK
    hv = _dil_views(h, dil)
    ov, dov, lv = _dil_views(o, dil), _dil_views(do, dil), _dil_views(lse, dil)

    def body(q_ref, kp_ref, kc_ref, vp_ref, vc_ref, o_ref, do_ref, lse_ref, dq_ref):
        n = pl.program_id(1)
        q = q_ref[...]
        kcat = jnp.concatenate([kp_ref[...], kc_ref[...]], axis=0)
        vcat = jnp.concatenate([vp_ref[...], vc_ref[...]], axis=0)
        ob, dob, lse_b = o_ref[...], do_ref[...], lse_ref[...]
        for hd in range(HEADS):
            cols = slice(HEAD_DIM * hd, HEAD_DIM * (hd + 1))
            kh = kcat[:, cols].astype(MXU_DTYPE)
            s, off, ok = _dil_scores(q[:, cols].astype(MXU_DTYPE), kh, jnp.where(n > 0, 0, b), True)
            s = jnp.where(ok, s - (ALIBI_SLOPES[hd] * dil) * off.astype(F32), NEG_BIG)
            p = jnp.exp(s - lse_b[:, hd:hd + 1])
            doh = dob[:, cols]
            delta = jnp.sum(doh * ob[:, cols], axis=1, keepdims=True)
            dp = _dot(doh.astype(MXU_DTYPE), vcat[:, cols].astype(MXU_DTYPE), 1, 1)
            ds = (p * (dp - delta) * DIL_SCALE).astype(MXU_DTYPE)
            dq_ref[:, cols] = _dot(ds, kh, 1, 0)

    def col(c):
        return lambda r, n: (n, r * 4 + c)

    def col_prev(c):
        return lambda r, n: (jnp.maximum(n - 1, 0), r * 4 + c)

    blk = lambda im: pl.BlockSpec((b, DIL_WIDTH), im)
    own = lambda r, n: (n, r)
    dq = pl.pallas_call(
        body, name=f"dil_bwd_q_{dil}", grid=(dil, nb),
        in_specs=[blk(col(1)), blk(col_prev(2)), blk(col(2)), blk(col_prev(3)), blk(col(3)),
                  blk(own), blk(own), pl.BlockSpec((b, LANES), own)],
        out_specs=blk(own), out_shape=jax.ShapeDtypeStruct((m, dil * DIL_WIDTH), F32),
        compiler_params=_params("parallel", "parallel"),
    )(hv, hv, hv, hv, hv, ov, dov, lv)
    return dq.reshape(seq, DIL_WIDTH)


def _dil_bwd_kv(h, o, lse, do, dil):
    seq = h.shape[0]
    m = seq // dil
    nb = m // DIL_BLOCK
    b = DIL_BLOCK
    hv = _dil_views(h, dil)
    ov, dov, lv = _dil_views(o, dil), _dil_views(do, dil), _dil_views(lse, dil)

    def body(k_ref, v_ref, qc_ref, qn_ref, oc_ref, on_ref, doc_ref, don_ref, lc_ref, ln_ref, dk_ref, dv_ref):
        n = pl.program_id(1)
        k, v = k_ref[...], v_ref[...]
        qcat = jnp.concatenate([qc_ref[...], qn_ref[...]], axis=0)
        ocat = jnp.concatenate([oc_ref[...], on_ref[...]], axis=0)
        docat = jnp.concatenate([doc_ref[...], don_ref[...]], axis=0)
        lcat = jnp.concatenate([lc_ref[...], ln_ref[...]], axis=0)
        for hd in range(HEADS):
            cols = slice(HEAD_DIM * hd, HEAD_DIM * (hd + 1))
            qh = qcat[:, cols].astype(MXU_DTYPE)
            s, off, ok = _dil_scores(qh, k[:, cols].astype(MXU_DTYPE), jnp.where(n < nb - 1, 2 * b, b), False)
            s = jnp.where(ok, s - (ALIBI_SLOPES[hd] * dil) * off.astype(F32), NEG_BIG)
            p = jnp.where(ok, jnp.exp(s - lcat[:, hd:hd + 1]), 0.0)
            doh = docat[:, cols]
            delta = jnp.sum(doh * ocat[:, cols], axis=1, keepdims=True)
            dohm = doh.astype(MXU_DTYPE)
            dv_ref[:, cols] = _dot(p.astype(MXU_DTYPE), dohm, 0, 0)
            dp = _dot(dohm, v[:, cols].astype(MXU_DTYPE), 1, 1)
            ds = (p * (dp - delta) * DIL_SCALE).astype(MXU_DTYPE)
            dk_ref[:, cols] = _dot(ds, qh, 0, 0)

    def col(c):
        return lambda r, n: (n, r * 4 + c)

    def col_next(c):
        return lambda r, n: (jnp.minimum(n + 1, nb - 1), r * 4 + c)

    own = lambda r, n: (n, r)
    nxt = lambda r, n: (jnp.minimum(n + 1, nb - 1), r)
    blk = lambda im: pl.BlockSpec((b, DIL_WIDTH), im)
    lblk = lambda im: pl.BlockSpec((b, LANES), im)
    shp = jax.ShapeDtypeStruct((m, dil * DIL_WIDTH), F32)
    dk, dv = pl.pallas_call(
        body, name=f"dil_bwd_kv_{dil}", grid=(dil, nb),
        in_specs=[blk(col(2)), blk(col(3)), blk(col(1)), blk(col_next(1)), blk(own), blk(nxt), blk(own), blk(nxt),
                  lblk(own), lblk(nxt)],
        out_specs=[blk(own), blk(own)], out_shape=[shp, shp],
        compiler_params=_params("parallel", "parallel"),
    )(hv, hv, hv, hv, ov, ov, dov, dov, lv, lv)
    return dk.reshape(seq, DIL_WIDTH), dv.reshape(seq, DIL_WIDTH)


def _assemble_dh(dh_mla, dqs, dks, dvs, tm=512):
    seq = dh_mla.shape[0]
    tm = min(tm, seq)

    def body(a_ref, q1, q2, q3, k1, k2, k3, v1, v2, v3, o_ref):
        o_ref[:, 0:512] = a_ref[...].astype(o_ref.dtype)
        o_ref[:, 512:1024] = (q1[...] + q2[...] + q3[...]).astype(o_ref.dtype)
        o_ref[:, 1024:1536] = (k1[...] + k2[...] + k3[...]).astype(o_ref.dtype)
        o_ref[:, 1536:2048] = (v1[...] + v2[...] + v3[...]).astype(o_ref.dtype)

    blk = pl.BlockSpec((tm, 512), lambda i: (i, 0))
    return pl.pallas_call(
        body, name="assemble_dh", grid=(seq // tm,), in_specs=[blk] * 10,
        out_specs=pl.BlockSpec((tm, IN_PAD), lambda i: (i, 0)), out_shape=jax.ShapeDtypeStruct((seq, IN_PAD), MXU_DTYPE),
        compiler_params=_params("parallel"),
    )(dh_mla, *dqs, *dks, *dvs)


def _ln_stats(z):
    mu = jnp.mean(z, axis=-1, keepdims=True)
    zc = z - mu
    r = lax.rsqrt(jnp.mean(zc * zc, axis=-1, keepdims=True) + LN_EPS)
    return zc * r, r


def _ln_bwd_math(dy, xh, r, g):
    dxh = dy * g
    return r * (dxh - jnp.mean(dxh, axis=-1, keepdims=True) - xh * jnp.mean(dxh * xh, axis=-1, keepdims=True))


def _mix_ln1(o_mla, o_dil, w_o_mla, w_o_dil, x0, g, b, tm=512):
    seq, d = x0.shape
    tm = min(tm, seq)

    def body(om_ref, od_ref, wm_ref, wd_ref, x_ref, g_ref, b_ref, z_ref, y_ref, yb_ref):
        mix = _dot(od_ref[...], wd_ref[...], 1, 0)
        for hd in range(HEADS):
            mix += _dot(om_ref[hd], wm_ref[LANES * hd:LANES * (hd + 1), :], 1, 0)
        z = DN_ALPHA * x_ref[...] + mix
        xh, _ = _ln_stats(z)
        y = xh * g_ref[...] + b_ref[...]
        z_ref[...] = z
        y_ref[...] = y
        yb_ref[...] = y.astype(yb_ref.dtype)

    blk = pl.BlockSpec((tm, d), lambda i: (i, 0))
    vec = pl.BlockSpec((1, d), lambda i: (0, 0))
    shp = jax.ShapeDtypeStruct((seq, d), F32)
    return pl.pallas_call(
        body, name="mix_ln1", grid=(seq // tm,),
        in_specs=[pl.BlockSpec((HEADS, tm, LANES), lambda i: (0, i, 0)), pl.BlockSpec((tm, DIL_WIDTH), lambda i: (i, 0)),
                  pl.BlockSpec((HEADS * LANES, d), lambda i: (0, 0)), pl.BlockSpec((DIL_WIDTH, d), lambda i: (0, 0)), blk, vec, vec],
        out_specs=[blk, blk, blk], out_shape=[shp, shp, jax.ShapeDtypeStruct((seq, d), MXU_DTYPE)],
        compiler_params=_params("parallel"))(o_mla, o_dil, w_o_mla, w_o_dil, x0, g, b)


def _ln_bwd(dy, z, g, name, tm=512):
    seq, d = z.shape
    tm = min(tm, seq)

    def body(dy_ref, z_ref, g_ref, dz_ref, dzb_ref, dg_ref, db_ref):
        @pl.when(pl.program_id(0) == 0)
        def _():
            dg_ref[...] = jnp.zeros_like(dg_ref)
            db_ref[...] = jnp.zeros_like(db_ref)

        dyb = dy_ref[...]
        xh, r = _ln_stats(z_ref[...])
        dg_ref[...] += jnp.sum(dyb * xh, axis=0, keepdims=True)
        db_ref[...] += jnp.sum(dyb, axis=0, keepdims=True)
        dz = _ln_bwd_math(dyb, xh, r, g_ref[...])
        dz_ref[...] = dz
        dzb_ref[...] = dz.astype(dzb_ref.dtype)

    blk = pl.BlockSpec((tm, d), lambda i: (i, 0))
    vec = pl.BlockSpec((1, d), lambda i: (0, 0))
    return pl.pallas_call(
        body, name=name, grid=(seq // tm,), in_specs=[blk, blk, vec], out_specs=[blk, blk, vec, vec],
        out_shape=[jax.ShapeDtypeStruct((seq, d), F32), jax.ShapeDtypeStruct((seq, d), MXU_DTYPE),
                   jax.ShapeDtypeStruct((1, d), F32), jax.ShapeDtypeStruct((1, d), F32)],
        compiler_params=_params("arbitrary"))(dy, z, g)


def _ln2_loss_bwd(x1, ffn, target, g, b, tm=512):
    seq, d = x1.shape
    tm = min(tm, seq)

    def body(x_ref, f_ref, t_ref, g_ref, b_ref, dz_ref, dzb_ref, loss_ref, dg_ref, db_ref):
        @pl.when(pl.program_id(0) == 0)
        def _():
            loss_ref[...] = jnp.zeros_like(loss_ref)
            dg_ref[...] = jnp.zeros_like(dg_ref)
            db_ref[...] = jnp.zeros_like(db_ref)

        gv = g_ref[...]
        z = DN_ALPHA * x_ref[...] + f_ref[...]
        xh, r = _ln_stats(z)
        err = (xh * gv + b_ref[...]) - t_ref[...]
        loss_ref[...] += 0.5 * jnp.sum(jnp.mean(err * err, axis=-1, keepdims=True), axis=0, keepdims=True)
        dy = err * (1.0 / d)
        dg_ref[...] += jnp.sum(dy * xh, axis=0, keepdims=True)
        db_ref[...] += jnp.sum(dy, axis=0, keepdims=True)
        dz = _ln_bwd_math(dy, xh, r, gv)
        dz_ref[...] = dz
        dzb_ref[...] = dz.astype(dzb_ref.dtype)

    blk = pl.BlockSpec((tm, d), lambda i: (i, 0))
    vec = pl.BlockSpec((1, d), lambda i: (0, 0))
    return pl.pallas_call(
        body, name="ln2_loss_bwd", grid=(seq // tm,), in_specs=[blk, blk, blk, vec, vec],
        out_specs=[blk, blk, pl.BlockSpec((1, LANES), lambda i: (0, 0)), vec, vec],
        out_shape=[jax.ShapeDtypeStruct((seq, d), F32), jax.ShapeDtypeStruct((seq, d), MXU_DTYPE),
                   jax.ShapeDtypeStruct((1, LANES), F32),
                   jax.ShapeDtypeStruct((1, d), F32), jax.ShapeDtypeStruct((1, d), F32)],
        compiler_params=_params("arbitrary"))(x1, ffn, target, g, b)


HALO = 16


def _conv_rows(e, w_ref, b_ref):
    y = b_ref[...] + w_ref[0:1, :] * pltpu.roll(e, 2, 0)
    y = y + w_ref[1:2, :] * pltpu.roll(e, 1, 0)
    return y + w_ref[2:3, :] * e


_GELU_C = math.sqrt(2.0 / math.pi)
_GELU_A = 0.044715


def _gelu(x):
    return 0.5 * x * (1.0 + jnp.tanh(_GELU_C * (x + _GELU_A * (x * x * x))))


def _gelu_grad(x):
    t = jnp.tanh(_GELU_C * (x + _GELU_A * (x * x * x)))
    return 0.5 * (1.0 + t) + 0.5 * x * (1.0 - t * t) * (_GELU_C * (1.0 + 3.0 * _GELU_A * (x * x)))


def _conv_gate_fwd(u, conv_w, conv_b, tm=512, tn=256):
    seq = u.shape[0]
    tm = min(tm, seq)
    nj = D_FF // tn

    def body(ua_ref, uap_ref, ug_ref, ugp_ref, wa_ref, wg_ref, ba_ref, bg_ref, o_ref):
        first = pl.program_id(0) == 0

        def ext(tile_ref, halo_ref):
            return jnp.concatenate([jnp.where(first, 0.0, halo_ref[...]), tile_ref[...]], axis=0)

        ya = _conv_rows(ext(ua_ref, uap_ref), wa_ref, ba_ref)[HALO:]
        yg = _conv_rows(ext(ug_ref, ugp_ref), wg_ref, bg_ref)[HALO:]
        o_ref[...] = (_gelu(yg) * ya).astype(o_ref.dtype)

    hb = tm // HALO
    tile = lambda off: pl.BlockSpec((tm, tn), lambda i, j: (i, j + off))
    prev = lambda off: pl.BlockSpec((HALO, tn), lambda i, j: (jnp.maximum(i * hb - 1, 0), j + off))
    wspec = lambda off: pl.BlockSpec((3, tn), lambda i, j: (0, j + off))
    bspec = lambda off: pl.BlockSpec((1, tn), lambda i, j: (0, j + off))
    return pl.pallas_call(
        body, name="conv_gate_fwd", grid=(seq // tm, nj),
        in_specs=[tile(0), prev(0), tile(nj), prev(nj), wspec(0), wspec(nj), bspec(0), bspec(nj)],
        out_specs=pl.BlockSpec((tm, tn), lambda i, j: (i, j)), out_shape=jax.ShapeDtypeStruct((seq, D_FF), MXU_DTYPE),
        compiler_params=_params("parallel", "parallel"),
    )(u, u, u, u, conv_w, conv_w, conv_b, conv_b)


def _conv_gate_bwd(u, d_act, conv_w, conv_b, tm=512, tn=256):
    seq = u.shape[0]
    tm = min(tm, seq)
    nj = D_FF // tn
    ni = seq // tm
    rows_e = tm + 2 * HALO

    def body(ua_ref, uap_ref, uan_ref, ug_ref, ugp_ref, ugn_ref, da_ref, dan_ref, wa_ref, wg_ref, ba_ref, bg_ref,
             du_ref, dw_ref, db_ref):
        half = pl.program_id(0)
        i = pl.program_id(2)
        first, last = i == 0, i == ni - 1

        @pl.when(i == 0)
        def _():
            dw_ref[...] = jnp.zeros_like(dw_ref)
            db_ref[...] = jnp.zeros_like(db_ref)

        def ext(tile_ref, prev_ref, next_ref):
            return jnp.concatenate([jnp.where(first, 0.0, prev_ref[...]), tile_ref[...],
                                    jnp.where(last, 0.0, next_ref[...])], axis=0)

        ea = ext(ua_ref, uap_ref, uan_ref)
        eg = ext(ug_ref, ugp_ref, ugn_ref)
        ya = _conv_rows(ea, wa_ref, ba_ref)
        yg = _conv_rows(eg, wg_ref, bg_ref)
        dact = jnp.concatenate([jnp.zeros((HALO, tn), F32), da_ref[...].astype(F32),
                                jnp.where(last, 0.0, dan_ref[...].astype(F32))], axis=0)

        def finish(dy, e_own, w_ref):
            du = w_ref[2:3, :] * dy + w_ref[1:2, :] * pltpu.roll(dy, rows_e - 1, 0) + w_ref[0:1, :] * pltpu.roll(dy, rows_e - 2, 0)
            du_ref[...] = du[HALO:HALO + tm].astype(du_ref.dtype)
            dyt = dy[HALO:HALO + tm]
            dw_ref[0:1, :] += jnp.sum(dyt * pltpu.roll(e_own, 2, 0)[HALO:HALO + tm], axis=0, keepdims=True)
            dw_ref[1:2, :] += jnp.sum(dyt * pltpu.roll(e_own, 1, 0)[HALO:HALO + tm], axis=0, keepdims=True)
            dw_ref[2:3, :] += jnp.sum(dyt * e_own[HALO:HALO + tm], axis=0, keepdims=True)
            db_ref[...] += jnp.sum(dyt, axis=0, keepdims=True)

        @pl.when(half == 0)
        def _():
            finish(dact * _gelu(yg), ea, wa_ref)

        @pl.when(half == 1)
        def _():
            finish(dact * ya * _gelu_grad(yg), eg, wg_ref)

    hb = tm // HALO
    nh = seq // HALO
    tile = lambda off: pl.BlockSpec((tm, tn), lambda hf, j, i: (i, j + off))
    prev = lambda off: pl.BlockSpec((HALO, tn), lambda hf, j, i: (jnp.maximum(i * hb - 1, 0), j + off))
    nxt = lambda off: pl.BlockSpec((HALO, tn), lambda hf, j, i: (jnp.minimum((i + 1) * hb, nh - 1), j + off))
    wspec = lambda off: pl.BlockSpec((3, tn), lambda hf, j, i: (0, j + off))
    bspec = lambda off: pl.BlockSpec((1, tn), lambda hf, j, i: (0, j + off))
    return pl.pallas_call(
        body, name="conv_gate_bwd", grid=(2, nj, ni),
        in_specs=[tile(0), prev(0), nxt(0), tile(nj), prev(nj), nxt(nj), tile(0), nxt(0),
                  wspec(0), wspec(nj), bspec(0), bspec(nj)],
        out_specs=[pl.BlockSpec((tm, tn), lambda hf, j, i: (i, hf * nj + j)),
                   pl.BlockSpec((3, tn), lambda hf, j, i: (0, hf * nj + j)),
                   pl.BlockSpec((1, tn), lambda hf, j, i: (0, hf * nj + j))],
        out_shape=[jax.ShapeDtypeStruct((seq, 2 * D_FF), MXU_DTYPE), jax.ShapeDtypeStruct((3, 2 * D_FF), F32),
                   jax.ShapeDtypeStruct((1, 2 * D_FF), F32)],
        compiler_params=_params("parallel", "parallel", "arbitrary"),
    )(u, u, u, u, u, u, d_act, d_act, conv_w, conv_w, conv_b, conv_b)


def _pad_heads(w, width):
    w = jnp.transpose(w, (1, 0, 2))
    return jnp.pad(w, ((0, 0), (0, 0), (0, LANES - width))).astype(MXU_DTYPE)


def _unpad_heads(d, width):
    return jnp.transpose(d[:, :, :width], (1, 0, 2))


def _split_pad_rows(w_t):
    z = lambda n: jnp.zeros((n, w_t.shape[1]), w_t.dtype)
    return jnp.concatenate([w_t[:384], z(64), w_t[384:416], z(32), w_t[416:]], axis=0)


def _split_unpad_rows(w_p):
    return jnp.concatenate([w_p[:384], w_p[448:480], w_p[512:]], axis=0)


def _pad_w_o(w_o):
    mla = jnp.pad(w_o[:512].reshape(HEADS, HEAD_DIM, D_MODEL), ((0, 0), (0, LANES - HEAD_DIM), (0, 0)))
    return mla.reshape(HEADS * LANES, D_MODEL).astype(MXU_DTYPE), w_o[512:].astype(MXU_DTYPE)


def _unpad_w_o(d_mla, d_dil):
    return jnp.concatenate([d_mla.reshape(HEADS, LANES, D_MODEL)[:, :HEAD_DIM].reshape(512, D_MODEL), d_dil], axis=0)


def _row(v):
    return v.reshape(1, -1).astype(F32)


def _compute_weights(w):
    w_o_mla, w_o_dil = _pad_w_o(w["w_o"])
    return dict(
        w_in_t=_split_pad_rows(w["w_in"].T).astype(MXU_DTYPE), wq=_pad_heads(w["w_uq"], NOPE_DIM + ROPE_DIM),
        wk=_pad_heads(w["w_uk"], NOPE_DIM), wv=_pad_heads(w["w_uv"], HEAD_DIM), w_o_mla=w_o_mla, w_o_dil=w_o_dil,
        w_up_t=w["w_up"].T.astype(MXU_DTYPE), w_down=w["w_down"].astype(MXU_DTYPE), conv_w=w["conv_w"].astype(F32),
        g_cq=_row(w["g_cq"]), g_ckv=_row(w["g_ckv"]), ln1_g=_row(w["ln1_g"]), ln1_b=_row(w["ln1_b"]),
        conv_b=_row(w["conv_b"]), ln2_g=_row(w["ln2_g"]), ln2_b=_row(w["ln2_b"]))


def _natural_grads(g):
    return dict(
        w_in=_split_unpad_rows(g["w_in_t"]).T, g_cq=g["g_cq"].reshape(-1), g_ckv=g["g_ckv"].reshape(-1),
        w_uq=_unpad_heads(g["wq"], NOPE_DIM + ROPE_DIM), w_uk=_unpad_heads(g["wk"], NOPE_DIM),
        w_uv=_unpad_heads(g["wv"], HEAD_DIM), w_o=_unpad_w_o(g["w_o_mla"], g["w_o_dil"]), ln1_g=g["ln1_g"].reshape(-1),
        ln1_b=g["ln1_b"].reshape(-1), w_up=g["w_up_t"].T, conv_w=g["conv_w"], conv_b=g["conv_b"].reshape(-1),
        w_down=g["w_down"], ln2_g=g["ln2_g"].reshape(-1), ln2_b=g["ln2_b"].reshape(-1))


def _layer_grads(x0, target, cw):
    seq = x0.shape[0]
    ctab, stab = _rope_tables(seq)
    gq, gk, cb = cw["g_cq"], cw["g_ckv"], cw["conv_b"]
    wq, wk, wv = cw["wq"], cw["wk"], cw["wv"]
    x0b = x0.astype(MXU_DTYPE)

    h = _mm(x0b, cw["w_in_t"], name="mm_h", tb=True, tm=1024, tn=1024, tk=1024)
    qf, kf, vp = _mla_prep(h, gq, gk, wq, wk, wv, ctab, stab)
    o_mla, o_mla_b, lse_mla = _mla_attn_fwd(qf, kf, vp)
    branch = [_dil_fwd(h, dil) for _, dil in DIL_PAIRS]
    o_dil, o_dil_b, lse_dil = _dil_combine([b[0] for b in branch], [b[1] for b in branch])
    z1, x1, x1b = _mix_ln1(o_mla_b, o_dil_b, cw["w_o_mla"], cw["w_o_dil"], x0, cw["ln1_g"], cw["ln1_b"])
    u = _mm(x1b, cw["w_up_t"], name="mm_up", tb=True, tm=1024, tn=512, tk=1024)
    act = _conv_gate_fwd(u, cw["conv_w"], cb)
    ffn = _mm(act, cw["w_down"], name="mm_down", tm=1024, tn=1024, tk=2816)
    dz2, dz2b, loss, d_ln2_g, d_ln2_b = _ln2_loss_bwd(x1, ffn, target, cw["ln2_g"], cw["ln2_b"])

    d_act = _mm(dz2b, cw["w_down"], name="mm_d_act", tb=True, out_dtype=MXU_DTYPE, tm=1024, tn=1408, tk=1024)
    d_w_down = _mm(act, dz2b, name="mm_dw_down", ta=True, out_dtype=MXU_DTYPE, tm=1408, tn=1024, tk=1024)
    du, d_conv_w, d_conv_b = _conv_gate_bwd(u, d_act, cw["conv_w"], cb)
    dx1 = _mm(du, cw["w_up_t"], name="mm_dx1", res=dz2, res_scale=DN_ALPHA, tm=1024, tn=1024, tk=1408)
    d_w_up_t = _mm(du, x1b, name="mm_dw_up", ta=True, out_dtype=MXU_DTYPE, tm=1408, tn=1024, tk=1024)
    dz1, dz1b, d_ln1_g, d_ln1_b = _ln_bwd(dx1, z1, cw["ln1_g"], "ln1_bwd")
    do_mla = _mm_do_mla(dz1b, cw["w_o_mla"])
    do_dil = _mm(dz1b, cw["w_o_dil"], name="mm_do_dil", tb=True, tm=1024, tn=512, tk=1024)
    d_w_o_mla = _mm_dw_o_mla(o_mla_b, dz1b)
    d_w_o_dil = _mm(o_dil_b, dz1b, name="mm_dw_o_dil", ta=True, out_dtype=MXU_DTYPE, tm=512, tn=1024, tk=1024)
    dqs, dks, dvs = [], [], []
    for _, dil in DIL_PAIRS:
        dqs.append(_dil_bwd_q(h, o_dil, lse_dil, do_dil, dil))
        dk_b, dv_b = _dil_bwd_kv(h, o_dil, lse_dil, do_dil, dil)
        dks.append(dk_b)
        dvs.append(dv_b)
    dqf, dkf, dvf = _mla_attn_bwd(qf, kf, vp, o_mla, lse_mla, do_mla)
    dh_mla, d_wq, d_wk, d_wv, d_gq, d_gk = _mla_prep_bwd(h, gq, gk, wq, wk, wv, ctab, stab, dqf, dkf, dvf)
    dh = _assemble_dh(dh_mla, dqs, dks, dvs)
    grad_x = _mm(dh, cw["w_in_t"], name="mm_dx0", res=dz1, res_scale=DN_ALPHA, tm=1024, tn=1024, tk=2048)
    d_w_in_t = _mm(dh, x0b, name="mm_dw_in", ta=True, out_dtype=MXU_DTYPE, tm=1024, tn=1024, tk=1024)

    grads = dict(
        w_in_t=d_w_in_t, wq=d_wq, wk=d_wk, wv=d_wv, w_o_mla=d_w_o_mla, w_o_dil=d_w_o_dil, w_up_t=d_w_up_t, w_down=d_w_down,
        conv_w=d_conv_w, g_cq=d_gq, g_ckv=d_gk, ln1_g=d_ln1_g, ln1_b=d_ln1_b, conv_b=d_conv_b, ln2_g=d_ln2_g, ln2_b=d_ln2_b)
    return loss[0, 0], grad_x, grads


def _all_gather(blocks, name):
    na = len(blocks)

    def body(*refs):
        ins, outs = refs[:na], refs[na:2 * na]
        send_sems, recv_sems, local_sems = refs[2 * na:]
        x, y, c = lax.axis_index("x"), lax.axis_index("y"), lax.axis_index("c")
        me, sibling = (x, y, c), (x, y, 1 - c)
        chips = [(1 - x, y), (x, 1 - y), (1 - x, 1 - y)]

        def slot(out, pos):
            return out.at[4 * pos[0] + 2 * pos[1] + pos[2]]

        def copy(a, k, block, to, src=None):
            return pltpu.make_async_remote_copy(
                src_ref=slot(outs[a], block) if src is None else src, dst_ref=slot(outs[a], block),
                send_sem=send_sems.at[7 * a + k], recv_sem=recv_sems.at[7 * a + k],
                device_id=to, device_id_type=pl.DeviceIdType.MESH)

        mine = [pltpu.make_async_copy(ins[a], slot(outs[a], me), local_sems.at[a]) for a in range(na)]
        for cp in mine:
            cp.start()
        first = []
        for a in range(na):
            first.append(copy(a, 0, me, sibling, src=ins[a]))
            first += [copy(a, 1 + j, me, (*chip, c), src=ins[a]) for j, chip in enumerate(chips)]
        for cp in first:
            cp.start()
        passed = []
        for j, chip in enumerate(chips):
            for a in range(na):
                copy(a, 1 + j, (*chip, c), me).wait_recv()
                cp = copy(a, 4 + j, (*chip, c), sibling)
                cp.start()
                passed.append(cp)
        for a in range(na):
            copy(a, 0, sibling, me).wait_recv()
            for j, chip in enumerate(chips):
                copy(a, 4 + j, (*chip, 1 - c), me).wait_recv()
        for cp in first + passed:
            cp.wait_send()
        for cp in mine:
            cp.wait()

    any_spec = pl.BlockSpec(memory_space=pl.ANY)
    return pl.pallas_call(
        body, name=name, in_specs=[any_spec] * na, out_specs=[any_spec] * na,
        out_shape=[jax.ShapeDtypeStruct((N_DEV,) + b.shape, b.dtype) for b in blocks],
        scratch_shapes=[pltpu.SemaphoreType.DMA((7 * na,)), pltpu.SemaphoreType.DMA((7 * na,)), pltpu.SemaphoreType.DMA((na,))],
    )(*blocks)


def _exchange(parts, name):
    na = len(parts)

    def body(*refs):
        ins, outs = refs[:na], refs[na:2 * na]
        send_sems, recv_sems, local_sems = refs[2 * na:]
        x, y, c = lax.axis_index("x"), lax.axis_index("y"), lax.axis_index("c")
        me = 4 * x + 2 * y + c

        def peer_of(d):
            return x ^ (d >> 2), y ^ ((d >> 1) & 1), c ^ (d & 1)

        def copy(a, d):
            px, py, pc = peer_of(d)
            return pltpu.make_async_remote_copy(
                src_ref=ins[a].at[4 * px + 2 * py + pc], dst_ref=outs[a].at[me],
                send_sem=send_sems.at[7 * a + d - 1], recv_sem=recv_sems.at[7 * a + d - 1],
                device_id=(px, py, pc), device_id_type=pl.DeviceIdType.MESH)

        def arrival(a, d):
            px, py, pc = peer_of(d)
            return pltpu.make_async_remote_copy(
                src_ref=ins[a].at[me], dst_ref=outs[a].at[4 * px + 2 * py + pc],
                send_sem=send_sems.at[7 * a + d - 1], recv_sem=recv_sems.at[7 * a + d - 1],
                device_id=(px, py, pc), device_id_type=pl.DeviceIdType.MESH)

        mine = [pltpu.make_async_copy(ins[a].at[me], outs[a].at[me], local_sems.at[a]) for a in range(na)]
        for cp in mine:
            cp.start()
        sent = [copy(a, d) for a in range(na) for d in range(1, N_DEV)]
        for cp in sent:
            cp.start()
        for a in range(na):
            for d in range(1, N_DEV):
                arrival(a, d).wait_recv()
        for cp in sent:
            cp.wait_send()
        for cp in mine:
            cp.wait()

    any_spec = pl.BlockSpec(memory_space=pl.ANY)
    return pl.pallas_call(
        body, name=name, in_specs=[any_spec] * na, out_specs=[any_spec] * na,
        out_shape=[jax.ShapeDtypeStruct(p.shape, p.dtype) for p in parts],
        scratch_shapes=[pltpu.SemaphoreType.DMA((7 * na,)), pltpu.SemaphoreType.DMA((7 * na,)), pltpu.SemaphoreType.DMA((na,))],
    )(*parts)


def _sum_parts(parts, name):
    npart, r, n = parts.shape
    tr = r if r <= 256 else max(t for t in range(16, 257, 16) if r % t == 0)

    def body(p_ref, o_ref):
        g = p_ref[0].astype(F32)
        for s in range(1, npart):
            g = g + p_ref[s].astype(F32)
        o_ref[...] = g

    return pl.pallas_call(
        body, name=name, grid=(r // tr,), in_specs=[pl.BlockSpec((npart, tr, n), lambda i: (0, i, 0))],
        out_specs=pl.BlockSpec((tr, n), lambda i: (i, 0)), out_shape=jax.ShapeDtypeStruct((r, n), F32),
        compiler_params=_params("parallel"),
    )(parts)


def _adamw(parts, w, m, v, name):
    npart, r, n = parts.shape
    tr = r if r <= 256 else max(t for t in range(16, 257, 16) if r % t == 0)
    c1 = 1.0 - ADAM_B1 ** ADAM_STEP
    c2 = 1.0 - ADAM_B2 ** ADAM_STEP

    def body(p_ref, w_ref, m_ref, v_ref, g_out, d_out, m_out, v_out):
        g = p_ref[0].astype(F32)
        for s in range(1, npart):
            g = g + p_ref[s].astype(F32)
        m_new = ADAM_B1 * m_ref[...] + (1.0 - ADAM_B1) * g
        v_new = ADAM_B2 * v_ref[...] + (1.0 - ADAM_B2) * (g * g)
        g_out[...] = g
        m_out[...] = m_new
        v_out[...] = v_new
        d_out[...] = -ADAM_LR * ((m_new / c1) / (jnp.sqrt(v_new / c2) + ADAM_EPS) + ADAM_WD * w_ref[...])

    blk = pl.BlockSpec((tr, n), lambda i: (i, 0))
    shp = jax.ShapeDtypeStruct((r, n), F32)
    return pl.pallas_call(
        body, name=name, grid=(r // tr,), in_specs=[pl.BlockSpec((npart, tr, n), lambda i: (0, i, 0)), blk, blk, blk],
        out_specs=[blk] * 4, out_shape=[shp] * 4, compiler_params=_params("parallel"),
    )(parts, w, m, v)


def _pack(arrs, dtype, row_multiple=16):
    flat = jnp.concatenate([a.reshape(-1).astype(dtype) for a in arrs])
    rows = -(-flat.shape[0] // PACK_COLS)
    rows = -(-rows // row_multiple) * row_multiple
    return jnp.pad(flat, (0, rows * PACK_COLS - flat.shape[0])).reshape(rows, PACK_COLS)


def _unpack(buf, shapes):
    lead = buf.shape[:-2]
    flat = buf.reshape(lead + (-1,))
    out, at = [], 0
    for s in shapes:
        size = math.prod(s)
        out.append(flat[..., at:at + size].reshape(lead + tuple(s)))
        at += size
    return out


REPLICATED = ("g_cq", "g_ckv", "w_uk", "w_uv", "ln1_g", "ln1_b", "conv_b", "ln2_g", "ln2_b")
ALL_WEIGHTS = ("w_in", "g_cq", "g_ckv", "w_uq", "w_uk", "w_uv", "w_o", "ln1_g", "ln1_b", "w_up", "conv_w", "conv_b",
               "w_down", "ln2_g", "ln2_b")


def kernel(x, w_in, g_cq, g_ckv, w_uq, w_uk, w_uv, w_o, ln1_g, ln1_b, w_up, conv_w, conv_b, w_down, ln2_g, ln2_b, loss_target, m_w_in, m_g_cq, m_g_ckv, m_w_uq, m_w_uk, m_w_uv, m_w_o, m_ln1_g, m_ln1_b, m_w_up, m_conv_w, m_conv_b, m_w_down, m_ln2_g, m_ln2_b, v_w_in, v_g_cq, v_g_ckv, v_w_uq, v_w_uk, v_w_uv, v_w_o, v_ln1_g, v_ln1_b, v_w_up, v_conv_w, v_conv_b, v_w_down, v_ln2_g, v_ln2_b):
    w = dict(w_in=w_in, g_cq=g_cq, g_ckv=g_ckv, w_uq=w_uq, w_uk=w_uk, w_uv=w_uv, w_o=w_o, ln1_g=ln1_g, ln1_b=ln1_b,
             w_up=w_up, conv_w=conv_w, conv_b=conv_b, w_down=w_down, ln2_g=ln2_g, ln2_b=ln2_b)
    m = dict(w_in=m_w_in, g_cq=m_g_cq, g_ckv=m_g_ckv, w_uq=m_w_uq, w_uk=m_w_uk, w_uv=m_w_uv, w_o=m_w_o, ln1_g=m_ln1_g,
             ln1_b=m_ln1_b, w_up=m_w_up, conv_w=m_conv_w, conv_b=m_conv_b, w_down=m_w_down, ln2_g=m_ln2_g, ln2_b=m_ln2_b)
    v = dict(w_in=v_w_in, g_cq=v_g_cq, g_ckv=v_g_ckv, w_uq=v_w_uq, w_uk=v_w_uk, w_uv=v_w_uv, w_o=v_w_o, ln1_g=v_ln1_g,
             ln1_b=v_ln1_b, w_up=v_w_up, conv_w=v_conv_w, conv_b=v_conv_b, w_down=v_w_down, ln2_g=v_ln2_g, ln2_b=v_ln2_b)
    me = 4 * lax.axis_index("x") + 2 * lax.axis_index("y") + lax.axis_index("c")
    wire = lambda a: a.astype(WIRE_DTYPE)
    n_in = w_in.shape[1]
    n_in_pad = -(-n_in // 16) * 16
    pad_taps = lambda a: jnp.pad(a, ((0, 8 - a.shape[0]), (0, 0)))

    g_in, g_uq, g_o, g_up, g_down, g_conv = _all_gather(
        [jnp.pad(wire(w_in).T, ((0, n_in_pad - n_in), (0, 0))), wire(w_uq).reshape(w_uq.shape[0], -1), wire(w_o),
         wire(w_up).T, wire(w_down), pad_taps(conv_w)], "gather_weights")
    w_o_mla, w_o_dil = _pad_w_o(g_o.reshape(-1, D_MODEL))
    cw = dict(
        w_in_t=_split_pad_rows(g_in[:, :n_in].reshape(-1, D_MODEL)).astype(MXU_DTYPE),
        wq=_pad_heads(g_uq.reshape((-1,) + w_uq.shape[1:]), NOPE_DIM + ROPE_DIM),
        wk=_pad_heads(w_uk, NOPE_DIM), wv=_pad_heads(w_uv, HEAD_DIM), w_o_mla=w_o_mla, w_o_dil=w_o_dil,
        w_up_t=g_up.reshape(-1, D_MODEL).astype(MXU_DTYPE), w_down=g_down.reshape(-1, D_MODEL).astype(MXU_DTYPE),
        conv_w=jnp.transpose(g_conv[:, :conv_w.shape[0]], (1, 0, 2)).reshape(conv_w.shape[0], -1),
        g_cq=_row(g_cq), g_ckv=_row(g_ckv), ln1_g=_row(ln1_g), ln1_b=_row(ln1_b), conv_b=_row(conv_b),
        ln2_g=_row(ln2_g), ln2_b=_row(ln2_b))

    loss, grad_x, g = _layer_grads(x[0], loss_target[0], cw)
    loss = lax.psum(loss, MESH_AXES)

    blocks = lambda a: wire(a).reshape((N_DEV, a.shape[0] // N_DEV) + a.shape[1:])
    d_in = jnp.pad(blocks(_split_unpad_rows(g["w_in_t"])), ((0, 0), (0, n_in_pad - n_in), (0, 0)))
    d_uq = blocks(_unpad_heads(g["wq"], NOPE_DIM + ROPE_DIM).reshape(Q_RANK, -1))
    r_in, r_uq, r_o, r_up, r_down = _exchange(
        [d_in, d_uq, blocks(_unpad_w_o(g["w_o_mla"], g["w_o_dil"])), blocks(g["w_up_t"]), blocks(g["w_down"])],
        "exchange_grads")
    nat = dict(g_cq=g["g_cq"], g_ckv=g["g_ckv"], w_uk=_unpad_heads(g["wk"], NOPE_DIM), w_uv=_unpad_heads(g["wv"], HEAD_DIM),
               ln1_g=g["ln1_g"], ln1_b=g["ln1_b"], conv_b=g["conv_b"], ln2_g=g["ln2_g"], ln2_b=g["ln2_b"])
    rep_all, cw_all = _all_gather([_pack([nat[n] for n in REPLICATED], F32, 8), pad_taps(g["conv_w"])], "gather_small_grads")

    out = {}

    def update(name, parts, shape2d=None):
        w2, m2, v2 = [d[name].reshape(shape2d or d[name].shape) for d in (w, m, v)]
        res = _adamw(parts, w2, m2, v2, "adamw_" + name)
        for kind, a in zip(("grad", "delta", "new_m", "new_v"), res):
            out[kind, name] = a.reshape(w[name].shape)

    update("w_in", _sum_parts(r_in, "sum_w_in")[:n_in].T[None])
    update("w_uq", r_uq, (w_uq.shape[0], -1))
    update("w_o", r_o)
    update("w_up", _sum_parts(r_up, "sum_w_up").T[None])
    update("w_down", r_down)
    res = _adamw(rep_all, *[_pack([d[n] for n in REPLICATED], F32, 8) for d in (w, m, v)], "adamw_replicated")
    for kind, buf in zip(("grad", "delta", "new_m", "new_v"), res):
        for n, a in zip(REPLICATED, _unpack(buf, [w[n].shape for n in REPLICATED])):
            out[kind, n] = a
    ncw = conv_w.shape[1]
    update("conv_w", lax.dynamic_slice_in_dim(cw_all[:, :conv_w.shape[0]], me * ncw, ncw, axis=2))

    return (loss, grad_x[None], *[out[kind, n] for kind in ("grad", "delta", "new_m", "new_v") for n in ALL_WEIGHTS])
```

```python
import functools
import math

import jax
import jax.numpy as jnp
from jax import lax
from jax.experimental import pallas as pl
from jax.experimental.pallas import tpu as pltpu

F32 = jnp.float32
MXU_DTYPE = jnp.bfloat16
WIRE_DTYPE = jnp.bfloat16

N_DEV = 8
D_MODEL = 1024
HEADS = 8
HEAD_DIM = 64
LANES = 128
Q_RANK, KV_RANK, ROPE_DIM, NOPE_DIM = 256, 128, 32, 64
DIL_WIDTH = HEADS * HEAD_DIM
IN_WIDTH = 1952
IN_PAD = 2048
D_FF = 2816
ROPE_THETA = 10000.0
DIL_PAIRS = ((128, 1), (512, 4), (2048, 16))
DIL_BLOCK = 128
DN_ALPHA = 2.0 ** 0.25
LN_EPS = 1e-5
RMS_EPS = 1e-6
MLA_SCALE = 1.0 / math.sqrt(NOPE_DIM + ROPE_DIM)
DIL_SCALE = 1.0 / math.sqrt(HEAD_DIM)
ALIBI_SLOPES = tuple(2.0 ** (-8.0 * (h + 1) / HEADS) for h in range(HEADS))
NEG_BIG = -1e30
ADAM_LR, ADAM_B1, ADAM_B2, ADAM_EPS, ADAM_WD, ADAM_STEP = 0.001, 0.9, 0.999, 1e-08, 0.01, 10
VMEM_LIMIT = 48 * 1024 * 1024
PACK_COLS = 1024

MESH_AXES = ("x", "y", "c")


def _params(*sem):
    return pltpu.CompilerParams(dimension_semantics=sem or None, vmem_limit_bytes=VMEM_LIMIT)


def _dot(a, b, ca, cb):
    return lax.dot_general(a, b, (((ca,), (cb,)), ((), ())), preferred_element_type=F32)


def _mm(a, b, *, name, tm, tn, tk, ta=False, tb=False, out_dtype=F32, res=None, res_scale=1.0):
    m, k = (a.shape[1], a.shape[0]) if ta else a.shape
    n = b.shape[0] if tb else b.shape[1]
    assert (b.shape[1] if tb else b.shape[0]) == k
    tm, tn, tk = min(tm, m), min(tn, n), min(tk, k)
    assert m % tm == 0 and n % tn == 0 and k % tk == 0, (name, m, n, k, tm, tn, tk)
    nk = k // tk
    a_spec = (pl.BlockSpec((tk, tm), lambda i, j, kk: (kk, i)) if ta
              else pl.BlockSpec((tm, tk), lambda i, j, kk: (i, kk)))
    b_spec = (pl.BlockSpec((tn, tk), lambda i, j, kk: (j, kk)) if tb
              else pl.BlockSpec((tk, tn), lambda i, j, kk: (kk, j)))
    o_spec = pl.BlockSpec((tm, tn), lambda i, j, kk: (i, j))
    in_specs = [a_spec, b_spec]
    args = [a, b]
    if res is not None:
        in_specs.append(o_spec)
        args.append(res)
    ca, cb = (0 if ta else 1), (1 if tb else 0)

    def finish(acc, r_ref, o_ref):
        if r_ref is not None:
            acc = acc + res_scale * r_ref[...]
        o_ref[...] = acc.astype(o_ref.dtype)

    def body(*refs):
        a_ref, b_ref = refs[:2]
        r_ref = refs[2] if res is not None else None
        o_ref = refs[3] if res is not None else refs[2]
        part = _dot(a_ref[...].astype(MXU_DTYPE), b_ref[...].astype(MXU_DTYPE), ca, cb)
        if nk == 1:
            finish(part, r_ref, o_ref)
            return
        acc_ref = refs[-1]
        kk = pl.program_id(2)

        @pl.when(kk == 0)
        def _():
            acc_ref[...] = part

        @pl.when(kk > 0)
        def _():
            acc_ref[...] += part

        @pl.when(kk == nk - 1)
        def _():
            finish(acc_ref[...], r_ref, o_ref)

    return pl.pallas_call(
        body, name=name, grid=(m // tm, n // tn, nk), in_specs=in_specs, out_specs=o_spec,
        out_shape=jax.ShapeDtypeStruct((m, n), out_dtype),
        scratch_shapes=[pltpu.VMEM((tm, tn), F32)] if nk > 1 else [],
        compiler_params=_params("parallel", "parallel", "arbitrary"),
    )(*args)


def _mm_do_mla(dz, w_o_mla, tm=1024):
    seq, d = dz.shape
    tm = min(tm, seq)

    def body(a_ref, b_ref, o_ref):
        a = a_ref[...].astype(MXU_DTYPE)
        for hd in range(HEADS):
            o_ref[hd] = _dot(a, b_ref[LANES * hd:LANES * (hd + 1), :], 1, 1)

    return pl.pallas_call(
        body, name="mm_do_mla", grid=(seq // tm,),
        in_specs=[pl.BlockSpec((tm, d), lambda i: (i, 0)), pl.BlockSpec((HEADS * LANES, d), lambda i: (0, 0))],
        out_specs=pl.BlockSpec((HEADS, tm, LANES), lambda i: (0, i, 0)),
        out_shape=jax.ShapeDtypeStruct((HEADS, seq, LANES), F32), compiler_params=_params("parallel"),
    )(dz, w_o_mla)


def _mm_dw_o_mla(o_mla, dz, tk=1024):
    seq, d = dz.shape
    tk = min(tk, seq)
    nk = seq // tk

    def body(a_ref, b_ref, o_ref, acc_ref):
        kk = pl.program_id(0)

        @pl.when(kk == 0)
        def _():
            acc_ref[...] = jnp.zeros_like(acc_ref)

        b = b_ref[...].astype(MXU_DTYPE)
        for hd in range(HEADS):
            acc_ref[LANES * hd:LANES * (hd + 1), :] += _dot(a_ref[hd].astype(MXU_DTYPE), b, 0, 0)

        @pl.when(kk == nk - 1)
        def _():
            o_ref[...] = acc_ref[...].astype(o_ref.dtype)

    return pl.pallas_call(
        body, name="mm_dw_o_mla", grid=(nk,),
        in_specs=[pl.BlockSpec((HEADS, tk, LANES), lambda kk: (0, kk, 0)), pl.BlockSpec((tk, d), lambda kk: (kk, 0))],
        out_specs=pl.BlockSpec((HEADS * LANES, d), lambda kk: (0, 0)),
        out_shape=jax.ShapeDtypeStruct((HEADS * LANES, d), MXU_DTYPE),
        scratch_shapes=[pltpu.VMEM((HEADS * LANES, d), F32)], compiler_params=_params("arbitrary"),
    )(o_mla, dz)


def _rope_tables(seq):
    half = ROPE_DIM // 2
    freqs = ROPE_THETA ** (-jnp.arange(half, dtype=F32) / half)
    ang = jnp.arange(seq).astype(F32)[:, None] * freqs[None, :]
    cos, sin = jnp.cos(ang), jnp.sin(ang)
    one = jnp.ones((seq, NOPE_DIM), F32)
    tail = jnp.ones((seq, LANES - NOPE_DIM - ROPE_DIM), F32)
    ctab = jnp.concatenate([one, cos, cos, tail], axis=1)
    stab = jnp.concatenate([0 * one, -sin, sin, 0 * tail], axis=1)
    return ctab, stab


def _rope_swap(t):
    lane = lax.broadcasted_iota(jnp.int32, t.shape, 1)
    half = ROPE_DIM // 2
    return jnp.where(lane < NOPE_DIM + half, pltpu.roll(t, LANES - half, 1), pltpu.roll(t, half, 1))


def _rope(t, ctab, stab):
    return t * ctab + _rope_swap(t) * stab


def _rope_inv(t, ctab, stab):
    return t * ctab - _rope_swap(t) * stab


def _rms(x, g):
    r = lax.rsqrt(jnp.mean(x * x, axis=-1, keepdims=True) + RMS_EPS)
    xh = x * r
    return xh, r, xh * g


def _mla_prep(h, g_cq, g_ckv, wq, wk, wv, ctab, stab, tm=512):
    seq = h.shape[0]
    tm = min(tm, seq)

    def body(h_ref, gq_ref, gk_ref, wq_ref, wk_ref, wv_ref, c_ref, s_ref, q_out, k_out, v_out):
        hb = h_ref[...]
        ctab_, stab_ = c_ref[...], s_ref[...]
        _, _, cqn = _rms(hb[:, :Q_RANK], gq_ref[...])
        _, _, ckn = _rms(hb[:, Q_RANK:Q_RANK + KV_RANK], gk_ref[...])
        cqn = cqn.astype(MXU_DTYPE)
        ckn = ckn.astype(MXU_DTYPE)
        krr = _rope(hb[:, Q_RANK + KV_RANK:], ctab_, stab_)
        for hd in range(HEADS):
            q = _dot(cqn, wq_ref[hd], 1, 0)
            q_out[hd] = _rope(q, ctab_, stab_).astype(q_out.dtype)
            k_out[hd] = (_dot(ckn, wk_ref[hd], 1, 0) + krr).astype(k_out.dtype)
            v_out[hd] = _dot(ckn, wv_ref[hd], 1, 0).astype(v_out.dtype)

    full = lambda *shape: pl.BlockSpec(shape, lambda i: (0,) * len(shape))
    slab = pl.BlockSpec((HEADS, tm, LANES), lambda i: (0, i, 0))
    shp = jax.ShapeDtypeStruct((HEADS, seq, LANES), MXU_DTYPE)
    return pl.pallas_call(
        body, name="mla_prep", grid=(seq // tm,),
        in_specs=[pl.BlockSpec((tm, 512), lambda i: (i, 0)), full(1, Q_RANK), full(1, KV_RANK),
                  full(HEADS, Q_RANK, LANES), full(HEADS, KV_RANK, LANES), full(HEADS, KV_RANK, LANES),
                  pl.BlockSpec((tm, LANES), lambda i: (i, 0)), pl.BlockSpec((tm, LANES), lambda i: (i, 0))],
        out_specs=[slab, slab, slab], out_shape=[shp, shp, shp],
        compiler_params=_params("parallel"),
    )(h, g_cq, g_ckv, wq, wk, wv, ctab, stab)


def _mla_prep_bwd(h, g_cq, g_ckv, wq, wk, wv, ctab, stab, dq, dk, dv, tm=512):
    seq = h.shape[0]
    tm = min(tm, seq)

    def body(h_ref, gq_ref, gk_ref, wq_ref, wk_ref, wv_ref, c_ref, s_ref, dq_ref, dk_ref, dv_ref,
             dh_ref, dwq_ref, dwk_ref, dwv_ref, dgq_ref, dgk_ref):
        @pl.when(pl.program_id(0) == 0)
        def _():
            for r in (dwq_ref, dwk_ref, dwv_ref, dgq_ref, dgk_ref):
                r[...] = jnp.zeros_like(r)

        hb = h_ref[...]
        ctab_, stab_ = c_ref[...], s_ref[...]
        gq, gk = gq_ref[...], gk_ref[...]
        xq, rq, cqn = _rms(hb[:, :Q_RANK], gq)
        xk, rk, ckn = _rms(hb[:, Q_RANK:Q_RANK + KV_RANK], gk)
        cqn = cqn.astype(MXU_DTYPE)
        ckn = ckn.astype(MXU_DTYPE)
        d_cqn = jnp.zeros((tm, Q_RANK), F32)
        d_ckn = jnp.zeros((tm, KV_RANK), F32)
        d_krr = jnp.zeros((tm, LANES), F32)
        for hd in range(HEADS):
            dqh = _rope_inv(dq_ref[hd], ctab_, stab_).astype(MXU_DTYPE)
            d_cqn += _dot(dqh, wq_ref[hd], 1, 1)
            dwq_ref[hd] += _dot(cqn, dqh, 0, 0)
            dkh = dk_ref[hd]
            d_krr += dkh
            dkh = dkh.astype(MXU_DTYPE)
            d_ckn += _dot(dkh, wk_ref[hd], 1, 1)
            dwk_ref[hd] += _dot(ckn, dkh, 0, 0)
            dvh = dv_ref[hd].astype(MXU_DTYPE)
            d_ckn += _dot(dvh, wv_ref[hd], 1, 1)
            dwv_ref[hd] += _dot(ckn, dvh, 0, 0)
        lane = lax.broadcasted_iota(jnp.int32, (tm, LANES), 1)
        rot = (lane >= NOPE_DIM) & (lane < NOPE_DIM + ROPE_DIM)
        d_kr = jnp.where(rot, _rope_inv(jnp.where(rot, d_krr, 0.0), ctab_, stab_), 0.0)

        def rms_bwd(dy, xh, r, g, dg_ref):
            dg_ref[...] += jnp.sum(dy * xh, axis=0, keepdims=True)
            dxh = dy * g
            return r * (dxh - xh * jnp.mean(dxh * xh, axis=-1, keepdims=True))

        d_cq = rms_bwd(d_cqn, xq, rq, gq, dgq_ref)
        d_ck = rms_bwd(d_ckn, xk, rk, gk, dgk_ref)
        dh_ref[...] = jnp.concatenate([d_cq, d_ck, d_kr], axis=1).astype(dh_ref.dtype)

    full = lambda *shape: pl.BlockSpec(shape, lambda i: (0,) * len(shape))
    slab = pl.BlockSpec((HEADS, tm, LANES), lambda i: (0, i, 0))
    return pl.pallas_call(
        body, name="mla_prep_bwd", grid=(seq // tm,),
        in_specs=[pl.BlockSpec((tm, 512), lambda i: (i, 0)), full(1, Q_RANK), full(1, KV_RANK),
                  full(HEADS, Q_RANK, LANES), full(HEADS, KV_RANK, LANES), full(HEADS, KV_RANK, LANES),
                  pl.BlockSpec((tm, LANES), lambda i: (i, 0)), pl.BlockSpec((tm, LANES), lambda i: (i, 0)),
                  slab, slab, slab],
        out_specs=[pl.BlockSpec((tm, 512), lambda i: (i, 0)), full(HEADS, Q_RANK, LANES), full(HEADS, KV_RANK, LANES),
                   full(HEADS, KV_RANK, LANES), full(1, Q_RANK), full(1, KV_RANK)],
        out_shape=[jax.ShapeDtypeStruct((seq, 512), MXU_DTYPE), jax.ShapeDtypeStruct((HEADS, Q_RANK, LANES), F32),
                   jax.ShapeDtypeStruct((HEADS, KV_RANK, LANES), F32), jax.ShapeDtypeStruct((HEADS, KV_RANK, LANES), F32),
                   jax.ShapeDtypeStruct((1, Q_RANK), F32), jax.ShapeDtypeStruct((1, KV_RANK), F32)],
        compiler_params=_params("arbitrary"),
    )(h, g_cq, g_ckv, wq, wk, wv, ctab, stab, dq, dk, dv)


def _causal_mask(t):
    row = lax.broadcasted_iota(jnp.int32, (t, t), 0)
    col = lax.broadcasted_iota(jnp.int32, (t, t), 1)
    return row >= col


def _mla_attn_fwd(q, k, v, t=512):
    _, seq, _ = q.shape
    t = min(t, seq)

    def body(q_ref, k_ref, v_ref, o_ref, ob_ref, lse_ref, m_ref, l_ref, acc_ref):
        i = pl.program_id(1)
        qb = q_ref[...]
        m_ref[...] = jnp.full_like(m_ref, NEG_BIG)
        l_ref[...] = jnp.zeros_like(l_ref)
        acc_ref[...] = jnp.zeros_like(acc_ref)

        def step(j, masked):
            kb = k_ref[pl.ds(pl.multiple_of(j * t, t), t), :]
            vb = v_ref[pl.ds(pl.multiple_of(j * t, t), t), :]
            s = _dot(qb, kb, 1, 1) * MLA_SCALE
            if masked:
                s = jnp.where(_causal_mask(t), s, NEG_BIG)
            m_old = m_ref[...]
            m_new = jnp.maximum(m_old, jnp.max(s, axis=1, keepdims=True))
            p = jnp.exp(s - m_new)
            a = jnp.exp(m_old - m_new)
            l_ref[...] = a * l_ref[...] + jnp.sum(p, axis=1, keepdims=True)
            acc_ref[...] = a * acc_ref[...] + _dot(p.astype(MXU_DTYPE), vb, 1, 0)
            m_ref[...] = m_new

        def loop_body(j, c):
            step(j, False)
            return c

        lax.fori_loop(0, i, loop_body, 0)
        step(i, True)
        l = l_ref[...]
        o = acc_ref[...] / l
        o_ref[...] = o
        ob_ref[...] = o.astype(ob_ref.dtype)
        lse_ref[...] = jnp.broadcast_to(m_ref[...] + jnp.log(l), lse_ref.shape)

    blk = pl.BlockSpec((None, t, LANES), lambda h, i: (h, i, 0))
    whole = pl.BlockSpec((None, seq, LANES), lambda h, i: (h, 0, 0))
    shp = jax.ShapeDtypeStruct((HEADS, seq, LANES), F32)
    return pl.pallas_call(
        body, name="mla_attn_fwd", grid=(HEADS, seq // t),
        in_specs=[blk, whole, whole], out_specs=[blk, blk, blk],
        out_shape=[shp, jax.ShapeDtypeStruct((HEADS, seq, LANES), MXU_DTYPE), shp],
        scratch_shapes=[pltpu.VMEM((t, 1), F32), pltpu.VMEM((t, 1), F32), pltpu.VMEM((t, LANES), F32)],
        compiler_params=_params("parallel", "arbitrary"),
    )(q, k, v)


def _mla_attn_bwd(q, k, v, o, lse, do, t=512):
    _, seq, _ = q.shape
    t = min(t, seq)
    nb = seq // t

    def body(q_ref, k_ref, v_ref, o_ref, lse_ref, do_ref, dq_ref, dk_ref, dv_ref, dl_ref, dka_ref, dva_ref):
        dq_ref[...] = jnp.zeros_like(dq_ref)

        def delta_body(i, c):
            rows = pl.ds(pl.multiple_of(i * t, t), t)
            dl_ref[rows, :] = jnp.sum(do_ref[rows, :] * o_ref[rows, :], axis=1, keepdims=True)
            return c

        lax.fori_loop(0, nb, delta_body, 0)

        def kblock(j, c):
            krows = pl.ds(pl.multiple_of(j * t, t), t)
            kb = k_ref[krows, :]
            vb = v_ref[krows, :]
            dka_ref[...] = jnp.zeros_like(dka_ref)
            dva_ref[...] = jnp.zeros_like(dva_ref)

            def qstep(i, masked):
                rows = pl.ds(pl.multiple_of(i * t, t), t)
                qb = q_ref[rows, :]
                dob = do_ref[rows, :].astype(MXU_DTYPE)
                s = _dot(qb, kb, 1, 1) * MLA_SCALE
                if masked:
                    s = jnp.where(_causal_mask(t), s, NEG_BIG)
                p = jnp.exp(s - lse_ref[rows, 0:1])
                dva_ref[...] += _dot(p.astype(MXU_DTYPE), dob, 0, 0)
                dp = _dot(dob, vb, 1, 1)
                ds = (p * (dp - dl_ref[rows, :]) * MLA_SCALE).astype(MXU_DTYPE)
                dka_ref[...] += _dot(ds, qb, 0, 0)
                dq_ref[rows, :] += _dot(ds, kb, 1, 0)

            qstep(j, True)

            def qloop(i, c2):
                qstep(i, False)
                return c2

            lax.fori_loop(j + 1, nb, qloop, 0)
            dk_ref[krows, :] = dka_ref[...]
            dv_ref[krows, :] = dva_ref[...]
            return c

        lax.fori_loop(0, nb, kblock, 0)

    whole = pl.BlockSpec((None, seq, LANES), lambda h: (h, 0, 0))
    shp = jax.ShapeDtypeStruct((HEADS, seq, LANES), F32)
    return pl.pallas_call(
        body, name="mla_attn_bwd", grid=(HEADS,),
        in_specs=[whole] * 6, out_specs=[whole] * 3, out_shape=[shp] * 3,
        scratch_shapes=[pltpu.VMEM((seq, 1), F32), pltpu.VMEM((t, LANES), F32), pltpu.VMEM((t, LANES), F32)],
        compiler_params=_params("parallel"),
    )(q, k, v, o, lse, do)


DIL_CHUNK = DIL_BLOCK * max(d for _, d in DIL_PAIRS)
DIL_PAIR_LANES = 2 * HEAD_DIM
assert DIL_PAIR_LANES == LANES
DIL_UNROLL_FWD = 4
DIL_UNROLL_BWD = 4


def _dil_bias_tables(hp, dil):
    b = DIL_BLOCK
    iq = lax.broadcasted_iota(jnp.int32, (b, 2 * b), 0)
    ik = lax.broadcasted_iota(jnp.int32, (b, 2 * b), 1)
    off = iq + b - ik
    band = (off >= 0) & (off <= b)
    dist = (off * dil).astype(F32)
    tables = []
    for hh in range(2):
        slope = jnp.where(hp == 0, ALIBI_SLOPES[hh], jnp.where(hp == 1, ALIBI_SLOPES[2 + hh],
                          jnp.where(hp == 2, ALIBI_SLOPES[4 + hh], ALIBI_SLOPES[6 + hh]))).astype(F32)
        bias = -slope * dist
        tables.append((jnp.where(band, bias, NEG_BIG), jnp.where(band & (ik >= b), bias, NEG_BIG)))
    return tables


def _dil_rows(start, dil):
    return pl.ds(start, DIL_BLOCK) if dil == 1 else pl.ds(start, DIL_BLOCK, stride=dil)


def _dil_block_pos(blk, c, dil):
    sc, r = blk // dil, blk % dil
    q0 = sc * (DIL_BLOCK * dil) + r
    kcur0 = c * DIL_CHUNK + q0
    first = kcur0 < DIL_BLOCK * dil
    kprev0 = jnp.where(first, kcur0, kcur0 - DIL_BLOCK * dil)
    return q0, kcur0, kprev0, first


def _pair_cols(hh):
    return slice(HEAD_DIM * hh, HEAD_DIM * (hh + 1))


def _first_head_lanes(shape):
    return lax.broadcasted_iota(jnp.int32, shape, 1) < HEAD_DIM


def _split_pair(t):
    first = _first_head_lanes(t.shape)
    return jnp.where(first, t, 0.0).astype(MXU_DTYPE), jnp.where(first, 0.0, t).astype(MXU_DTYPE)


def _join_pair(ts):
    return jnp.where(_first_head_lanes(ts[0].shape), ts[0], ts[1])


def _dil_fwd(h):
    seq = h.shape[0]
    assert seq % DIL_CHUNK == 0
    nblk = DIL_CHUNK // DIL_BLOCK
    rc = 256

    def body(q_ref, k_ref, v_ref, o_ref, ob_ref, lse_ref, *scr):
        o_scr, l_scr = scr[:3], scr[3:]
        hp, c = pl.program_id(0), pl.program_id(1)
        for bi, (_, dil) in enumerate(DIL_PAIRS):
            tables = _dil_bias_tables(hp, dil)

            def block(blk, carry, bi=bi, dil=dil, tables=tables):
                q0, kcur0, kprev0, first = _dil_block_pos(blk, c, dil)
                qs = _split_pair(q_ref[_dil_rows(q0, dil), :] * DIL_SCALE)
                kcat = jnp.concatenate([k_ref[_dil_rows(kprev0, dil), :], k_ref[_dil_rows(kcur0, dil), :]], axis=0).astype(MXU_DTYPE)
                vcat = jnp.concatenate([v_ref[_dil_rows(kprev0, dil), :], v_ref[_dil_rows(kcur0, dil), :]], axis=0).astype(MXU_DTYPE)
                outs, lses = [], []
                for hh in range(2):
                    s = _dot(qs[hh], kcat, 1, 1) + jnp.where(first, tables[hh][1], tables[hh][0])
                    mx = jnp.max(s, axis=1, keepdims=True)
                    p = jnp.exp(s - mx)
                    l = jnp.sum(p, axis=1, keepdims=True)
                    outs.append(_dot(p.astype(MXU_DTYPE), vcat, 1, 0) * (1.0 / l))
                    lses.append(jnp.broadcast_to(mx + jnp.log(l), (DIL_BLOCK, LANES)))
                o_scr[bi][_dil_rows(q0, dil), :] = _join_pair(outs)
                l_scr[bi][_dil_rows(q0, dil), :] = _join_pair(lses)
                return carry

            lax.fori_loop(0, nblk, block, 0, unroll=DIL_UNROLL_FWD)

        def combine(i, carry):
            rows = pl.ds(pl.multiple_of(i * rc, rc), rc)
            ls = [l_scr[bi][rows, :] for bi in range(3)]
            mx = jnp.maximum(jnp.maximum(ls[0], ls[1]), ls[2])
            es = [jnp.exp(l - mx) for l in ls]
            den = es[0] + es[1] + es[2]
            o = (es[0] * o_scr[0][rows, :] + es[1] * o_scr[1][rows, :] + es[2] * o_scr[2][rows, :]) / den
            o_ref[rows, :] = o
            ob_ref[rows, :] = o.astype(ob_ref.dtype)
            lse_ref[rows, :] = mx + jnp.log(den)
            return carry

        lax.fori_loop(0, DIL_CHUNK // rc, combine, 0)

    nq = DIL_WIDTH // LANES
    chunk = lambda off: pl.BlockSpec((DIL_CHUNK, LANES), lambda hp, c: (c, off + hp))
    whole = lambda off: pl.BlockSpec((seq, LANES), lambda hp, c: (0, off + hp))
    shp = jax.ShapeDtypeStruct((seq, DIL_WIDTH), F32)
    return pl.pallas_call(
        body, name="dil_fwd", grid=(nq, seq // DIL_CHUNK),
        in_specs=[chunk(nq), whole(2 * nq), whole(3 * nq)], out_specs=[chunk(0), chunk(0), chunk(0)],
        out_shape=[shp, jax.ShapeDtypeStruct((seq, DIL_WIDTH), MXU_DTYPE), shp],
        scratch_shapes=[pltpu.VMEM((DIL_CHUNK, LANES), F32)] * 6,
        compiler_params=_params("parallel", "arbitrary"),
    )(h, h, h)


def _dil_bwd(h, o, lse, do):
    seq = h.shape[0]
    nblk = DIL_CHUNK // DIL_BLOCK
    rc = 256

    def body(q_ref, k_ref, v_ref, o_ref, lse_ref, do_ref, dq_ref, dk_ref, dv_ref, dl_scr):
        hp, c = pl.program_id(0), pl.program_id(1)

        @pl.when(c == 0)
        def _():
            dk_ref[...] = jnp.zeros_like(dk_ref)
            dv_ref[...] = jnp.zeros_like(dv_ref)

        def delta(i, carry):
            rows = pl.ds(pl.multiple_of(i * rc, rc), rc)
            prod = do_ref[rows, :] * o_ref[rows, :]
            dl_scr[rows, :] = jnp.concatenate(
                [jnp.broadcast_to(jnp.sum(prod[:, _pair_cols(hh)], axis=1, keepdims=True), (rc, HEAD_DIM)) for hh in range(2)], axis=1)
            return carry

        lax.fori_loop(0, DIL_CHUNK // rc, delta, 0)

        for bi, (_, dil) in enumerate(DIL_PAIRS):
            tables = _dil_bias_tables(hp, dil)

            def block(blk, carry, bi=bi, dil=dil, tables=tables):
                q0, kcur0, kprev0, first = _dil_block_pos(blk, c, dil)
                qrows = _dil_rows(q0, dil)
                qs = _split_pair(q_ref[qrows, :] * DIL_SCALE)
                kcat = jnp.concatenate([k_ref[_dil_rows(kprev0, dil), :], k_ref[_dil_rows(kcur0, dil), :]], axis=0).astype(MXU_DTYPE)
                vcat = jnp.concatenate([v_ref[_dil_rows(kprev0, dil), :], v_ref[_dil_rows(kcur0, dil), :]], axis=0).astype(MXU_DTYPE)
                dos = _split_pair(do_ref[qrows, :])
                lse_b = lse_ref[qrows, :]
                dl_b = dl_scr[qrows, :]
                dqs = []
                dk_b = dv_b = None
                for hh in range(2):
                    lane0 = HEAD_DIM * hh
                    s = _dot(qs[hh], kcat, 1, 1) + jnp.where(first, tables[hh][1], tables[hh][0])
                    p = jnp.exp(s - lse_b[:, lane0:lane0 + 1])
                    dp = _dot(dos[hh], vcat, 1, 1)
                    ds = (p * (dp - dl_b[:, lane0:lane0 + 1])).astype(MXU_DTYPE)
                    dqs.append(_dot(ds, kcat, 1, 0))
                    dk_h = _dot(ds, qs[hh], 0, 0)
                    dv_h = _dot(p.astype(MXU_DTYPE), dos[hh], 0, 0)
                    dk_b = dk_h if dk_b is None else dk_b + dk_h
                    dv_b = dv_h if dv_b is None else dv_b + dv_h
                dq_b = _join_pair(dqs) * DIL_SCALE
                if bi == 0:
                    dq_ref[qrows, :] = dq_b
                else:
                    dq_ref[qrows, :] += dq_b
                dk_ref[_dil_rows(kprev0, dil), :] += dk_b[:DIL_BLOCK]
                dv_ref[_dil_rows(kprev0, dil), :] += dv_b[:DIL_BLOCK]
                dk_ref[_dil_rows(kcur0, dil), :] += dk_b[DIL_BLOCK:]
                dv_ref[_dil_rows(kcur0, dil), :] += dv_b[DIL_BLOCK:]
                return carry

            lax.fori_loop(0, nblk, block, 0, unroll=DIL_UNROLL_BWD)

    nq = DIL_WIDTH // LANES
    chunk = lambda off: pl.BlockSpec((DIL_CHUNK, LANES), lambda hp, c: (c, off + hp))
    whole = lambda off: pl.BlockSpec((seq, LANES), lambda hp, c: (0, off + hp))
    shp = jax.ShapeDtypeStruct((seq, DIL_WIDTH), F32)
    return pl.pallas_call(
        body, name="dil_bwd", grid=(nq, seq // DIL_CHUNK),
        in_specs=[chunk(nq), whole(2 * nq), whole(3 * nq), chunk(0), chunk(0), chunk(0)],
        out_specs=[chunk(0), whole(0), whole(0)], out_shape=[shp, shp, shp],
        scratch_shapes=[pltpu.VMEM((DIL_CHUNK, LANES), F32)],
        compiler_params=_params("parallel", "arbitrary"),
    )(h, h, h, o, lse, do)


def _assemble_dh(dh_mla, dq, dk, dv, tm=512):
    seq = dh_mla.shape[0]
    tm = min(tm, seq)

    def body(a_ref, q_ref, k_ref, v_ref, o_ref):
        for j, r in enumerate((a_ref, q_ref, k_ref, v_ref)):
            o_ref[:, 512 * j:512 * (j + 1)] = r[...].astype(o_ref.dtype)

    blk = pl.BlockSpec((tm, 512), lambda i: (i, 0))
    return pl.pallas_call(
        body, name="assemble_dh", grid=(seq // tm,), in_specs=[blk] * 4,
        out_specs=pl.BlockSpec((tm, IN_PAD), lambda i: (i, 0)), out_shape=jax.ShapeDtypeStruct((seq, IN_PAD), MXU_DTYPE),
        compiler_params=_params("parallel"),
    )(dh_mla, dq, dk, dv)


def _ln_stats(z):
    mu = jnp.mean(z, axis=-1, keepdims=True)
    zc = z - mu
    r = lax.rsqrt(jnp.mean(zc * zc, axis=-1, keepdims=True) + LN_EPS)
    return zc * r, r


def _ln_bwd_math(dy, xh, r, g):
    dxh = dy * g
    return r * (dxh - jnp.mean(dxh, axis=-1, keepdims=True) - xh * jnp.mean(dxh * xh, axis=-1, keepdims=True))


def _mix_ln1(o_mla, o_dil, w_o_mla, w_o_dil, x0, g, b, tm=512):
    seq, d = x0.shape
    tm = min(tm, seq)

    def body(om_ref, od_ref, wm_ref, wd_ref, x_ref, g_ref, b_ref, z_ref, y_ref, yb_ref):
        mix = _dot(od_ref[...], wd_ref[...], 1, 0)
        for hd in range(HEADS):
            mix += _dot(om_ref[hd], wm_ref[LANES * hd:LANES * (hd + 1), :], 1, 0)
        z = DN_ALPHA * x_ref[...] + mix
        xh, _ = _ln_stats(z)
        y = xh * g_ref[...] + b_ref[...]
        z_ref[...] = z
        y_ref[...] = y
        yb_ref[...] = y.astype(yb_ref.dtype)

    blk = pl.BlockSpec((tm, d), lambda i: (i, 0))
    vec = pl.BlockSpec((1, d), lambda i: (0, 0))
    shp = jax.ShapeDtypeStruct((seq, d), F32)
    return pl.pallas_call(
        body, name="mix_ln1", grid=(seq // tm,),
        in_specs=[pl.BlockSpec((HEADS, tm, LANES), lambda i: (0, i, 0)), pl.BlockSpec((tm, DIL_WIDTH), lambda i: (i, 0)),
                  pl.BlockSpec((HEADS * LANES, d), lambda i: (0, 0)), pl.BlockSpec((DIL_WIDTH, d), lambda i: (0, 0)), blk, vec, vec],
        out_specs=[blk, blk, blk], out_shape=[shp, shp, jax.ShapeDtypeStruct((seq, d), MXU_DTYPE)],
        compiler_params=_params("parallel"))(o_mla, o_dil, w_o_mla, w_o_dil, x0, g, b)


def _ln_bwd(dy, z, g, name, tm=512):
    seq, d = z.shape
    tm = min(tm, seq)

    def body(dy_ref, z_ref, g_ref, dz_ref, dzb_ref, dg_ref, db_ref):
        @pl.when(pl.program_id(0) == 0)
        def _():
            dg_ref[...] = jnp.zeros_like(dg_ref)
            db_ref[...] = jnp.zeros_like(db_ref)

        dyb = dy_ref[...]
        xh, r = _ln_stats(z_ref[...])
        dg_ref[...] += jnp.sum(dyb * xh, axis=0, keepdims=True)
        db_ref[...] += jnp.sum(dyb, axis=0, keepdims=True)
        dz = _ln_bwd_math(dyb, xh, r, g_ref[...])
        dz_ref[...] = dz
        dzb_ref[...] = dz.astype(dzb_ref.dtype)

    blk = pl.BlockSpec((tm, d), lambda i: (i, 0))
    vec = pl.BlockSpec((1, d), lambda i: (0, 0))
    return pl.pallas_call(
        body, name=name, grid=(seq // tm,), in_specs=[blk, blk, vec], out_specs=[blk, blk, vec, vec],
        out_shape=[jax.ShapeDtypeStruct((seq, d), F32), jax.ShapeDtypeStruct((seq, d), MXU_DTYPE),
                   jax.ShapeDtypeStruct((1, d), F32), jax.ShapeDtypeStruct((1, d), F32)],
        compiler_params=_params("arbitrary"))(dy, z, g)


def _ln2_loss_bwd(x1, ffn, target, g, b, tm=512):
    seq, d = x1.shape
    tm = min(tm, seq)

    def body(x_ref, f_ref, t_ref, g_ref, b_ref, dz_ref, dzb_ref, loss_ref, dg_ref, db_ref):
        @pl.when(pl.program_id(0) == 0)
        def _():
            loss_ref[...] = jnp.zeros_like(loss_ref)
            dg_ref[...] = jnp.zeros_like(dg_ref)
            db_ref[...] = jnp.zeros_like(db_ref)

        gv = g_ref[...]
        z = DN_ALPHA * x_ref[...] + f_ref[...]
        xh, r = _ln_stats(z)
        err = (xh * gv + b_ref[...]) - t_ref[...]
        loss_ref[...] += 0.5 * jnp.sum(jnp.mean(err * err, axis=-1, keepdims=True), axis=0, keepdims=True)
        dy = err * (1.0 / d)
        dg_ref[...] += jnp.sum(dy * xh, axis=0, keepdims=True)
        db_ref[...] += jnp.sum(dy, axis=0, keepdims=True)
        dz = _ln_bwd_math(dy, xh, r, gv)
        dz_ref[...] = dz
        dzb_ref[...] = dz.astype(dzb_ref.dtype)

    blk = pl.BlockSpec((tm, d), lambda i: (i, 0))
    vec = pl.BlockSpec((1, d), lambda i: (0, 0))
    return pl.pallas_call(
        body, name="ln2_loss_bwd", grid=(seq // tm,), in_specs=[blk, blk, blk, vec, vec],
        out_specs=[blk, blk, pl.BlockSpec((1, LANES), lambda i: (0, 0)), vec, vec],
        out_shape=[jax.ShapeDtypeStruct((seq, d), F32), jax.ShapeDtypeStruct((seq, d), MXU_DTYPE),
                   jax.ShapeDtypeStruct((1, LANES), F32),
                   jax.ShapeDtypeStruct((1, d), F32), jax.ShapeDtypeStruct((1, d), F32)],
        compiler_params=_params("arbitrary"))(x1, ffn, target, g, b)


HALO = 16


def _conv_rows(e, w_ref, b_ref):
    y = b_ref[...] + w_ref[0:1, :] * pltpu.roll(e, 2, 0)
    y = y + w_ref[1:2, :] * pltpu.roll(e, 1, 0)
    return y + w_ref[2:3, :] * e


_GELU_C = math.sqrt(2.0 / math.pi)
_GELU_A = 0.044715


def _gelu(x):
    return 0.5 * x * (1.0 + jnp.tanh(_GELU_C * (x + _GELU_A * (x * x * x))))


def _gelu_grad(x):
    t = jnp.tanh(_GELU_C * (x + _GELU_A * (x * x * x)))
    return 0.5 * (1.0 + t) + 0.5 * x * (1.0 - t * t) * (_GELU_C * (1.0 + 3.0 * _GELU_A * (x * x)))


def _conv_gate_fwd(u, conv_w, conv_b, tm=512, tn=256):
    seq = u.shape[0]
    tm = min(tm, seq)
    nj = D_FF // tn

    def body(ua_ref, uap_ref, ug_ref, ugp_ref, wa_ref, wg_ref, ba_ref, bg_ref, o_ref):
        first = pl.program_id(0) == 0

        def ext(tile_ref, halo_ref):
            return jnp.concatenate([jnp.where(first, 0.0, halo_ref[...]), tile_ref[...]], axis=0)

        ya = _conv_rows(ext(ua_ref, uap_ref), wa_ref, ba_ref)[HALO:]
        yg = _conv_rows(ext(ug_ref, ugp_ref), wg_ref, bg_ref)[HALO:]
        o_ref[...] = (_gelu(yg) * ya).astype(o_ref.dtype)

    hb = tm // HALO
    tile = lambda off: pl.BlockSpec((tm, tn), lambda i, j: (i, j + off))
    prev = lambda off: pl.BlockSpec((HALO, tn), lambda i, j: (jnp.maximum(i * hb - 1, 0), j + off))
    wspec = lambda off: pl.BlockSpec((3, tn), lambda i, j: (0, j + off))
    bspec = lambda off: pl.BlockSpec((1, tn), lambda i, j: (0, j + off))
    return pl.pallas_call(
        body, name="conv_gate_fwd", grid=(seq // tm, nj),
        in_specs=[tile(0), prev(0), tile(nj), prev(nj), wspec(0), wspec(nj), bspec(0), bspec(nj)],
        out_specs=pl.BlockSpec((tm, tn), lambda i, j: (i, j)), out_shape=jax.ShapeDtypeStruct((seq, D_FF), MXU_DTYPE),
        compiler_params=_params("parallel", "parallel"),
    )(u, u, u, u, conv_w, conv_w, conv_b, conv_b)


def _conv_gate_bwd(u, d_act, conv_w, conv_b, tm=512, tn=256):
    seq = u.shape[0]
    tm = min(tm, seq)
    nj = D_FF // tn
    ni = seq // tm
    rows_e = tm + 2 * HALO

    def body(ua_ref, uap_ref, uan_ref, ug_ref, ugp_ref, ugn_ref, da_ref, dan_ref, wa_ref, wg_ref, ba_ref, bg_ref,
             du_ref, dw_ref, db_ref):
        half = pl.program_id(0)
        i = pl.program_id(2)
        first, last = i == 0, i == ni - 1

        @pl.when(i == 0)
        def _():
            dw_ref[...] = jnp.zeros_like(dw_ref)
            db_ref[...] = jnp.zeros_like(db_ref)

        def ext(tile_ref, prev_ref, next_ref):
            return jnp.concatenate([jnp.where(first, 0.0, prev_ref[...]), tile_ref[...],
                                    jnp.where(last, 0.0, next_ref[...])], axis=0)

        ea = ext(ua_ref, uap_ref, uan_ref)
        eg = ext(ug_ref, ugp_ref, ugn_ref)
        ya = _conv_rows(ea, wa_ref, ba_ref)
        yg = _conv_rows(eg, wg_ref, bg_ref)
        dact = jnp.concatenate([jnp.zeros((HALO, tn), F32), da_ref[...].astype(F32),
                                jnp.where(last, 0.0, dan_ref[...].astype(F32))], axis=0)

        def finish(dy, e_own, w_ref):
            du = w_ref[2:3, :] * dy + w_ref[1:2, :] * pltpu.roll(dy, rows_e - 1, 0) + w_ref[0:1, :] * pltpu.roll(dy, rows_e - 2, 0)
            du_ref[...] = du[HALO:HALO + tm].astype(du_ref.dtype)
            dyt = dy[HALO:HALO + tm]
            dw_ref[0:1, :] += jnp.sum(dyt * pltpu.roll(e_own, 2, 0)[HALO:HALO + tm], axis=0, keepdims=True)
            dw_ref[1:2, :] += jnp.sum(dyt * pltpu.roll(e_own, 1, 0)[HALO:HALO + tm], axis=0, keepdims=True)
            dw_ref[2:3, :] += jnp.sum(dyt * e_own[HALO:HALO + tm], axis=0, keepdims=True)
            db_ref[...] += jnp.sum(dyt, axis=0, keepdims=True)

        @pl.when(half == 0)
        def _():
            finish(dact * _gelu(yg), ea, wa_ref)

        @pl.when(half == 1)
        def _():
            finish(dact * ya * _gelu_grad(yg), eg, wg_ref)

    hb = tm // HALO
    nh = seq // HALO
    tile = lambda off: pl.BlockSpec((tm, tn), lambda hf, j, i: (i, j + off))
    prev = lambda off: pl.BlockSpec((HALO, tn), lambda hf, j, i: (jnp.maximum(i * hb - 1, 0), j + off))
    nxt = lambda off: pl.BlockSpec((HALO, tn), lambda hf, j, i: (jnp.minimum((i + 1) * hb, nh - 1), j + off))
    wspec = lambda off: pl.BlockSpec((3, tn), lambda hf, j, i: (0, j + off))
    bspec = lambda off: pl.BlockSpec((1, tn), lambda hf, j, i: (0, j + off))
    return pl.pallas_call(
        body, name="conv_gate_bwd", grid=(2, nj, ni),
        in_specs=[tile(0), prev(0), nxt(0), tile(nj), prev(nj), nxt(nj), tile(0), nxt(0),
                  wspec(0), wspec(nj), bspec(0), bspec(nj)],
        out_specs=[pl.BlockSpec((tm, tn), lambda hf, j, i: (i, hf * nj + j)),
                   pl.BlockSpec((3, tn), lambda hf, j, i: (0, hf * nj + j)),
                   pl.BlockSpec((1, tn), lambda hf, j, i: (0, hf * nj + j))],
        out_shape=[jax.ShapeDtypeStruct((seq, 2 * D_FF), MXU_DTYPE), jax.ShapeDtypeStruct((3, 2 * D_FF), F32),
                   jax.ShapeDtypeStruct((1, 2 * D_FF), F32)],
        compiler_params=_params("parallel", "parallel", "arbitrary"),
    )(u, u, u, u, u, u, d_act, d_act, conv_w, conv_w, conv_b, conv_b)


def _pad_heads(w, width):
    w = jnp.transpose(w, (1, 0, 2))
    return jnp.pad(w, ((0, 0), (0, 0), (0, LANES - width))).astype(MXU_DTYPE)


def _unpad_heads(d, width):
    return jnp.transpose(d[:, :, :width], (1, 0, 2))


def _split_pad_rows(w_t):
    z = lambda n: jnp.zeros((n, w_t.shape[1]), w_t.dtype)
    return jnp.concatenate([w_t[:384], z(64), w_t[384:416], z(32), w_t[416:]], axis=0)


def _split_unpad_rows(w_p):
    return jnp.concatenate([w_p[:384], w_p[448:480], w_p[512:]], axis=0)


def _pad_w_o(w_o):
    mla = jnp.pad(w_o[:512].reshape(HEADS, HEAD_DIM, D_MODEL), ((0, 0), (0, LANES - HEAD_DIM), (0, 0)))
    return mla.reshape(HEADS * LANES, D_MODEL).astype(MXU_DTYPE), w_o[512:].astype(MXU_DTYPE)


def _unpad_w_o(d_mla, d_dil):
    return jnp.concatenate([d_mla.reshape(HEADS, LANES, D_MODEL)[:, :HEAD_DIM].reshape(512, D_MODEL), d_dil], axis=0)


def _row(v):
    return v.reshape(1, -1).astype(F32)


def _compute_weights(w):
    w_o_mla, w_o_dil = _pad_w_o(w["w_o"])
    return dict(
        w_in_t=_split_pad_rows(w["w_in"].T).astype(MXU_DTYPE), wq=_pad_heads(w["w_uq"], NOPE_DIM + ROPE_DIM),
        wk=_pad_heads(w["w_uk"], NOPE_DIM), wv=_pad_heads(w["w_uv"], HEAD_DIM), w_o_mla=w_o_mla, w_o_dil=w_o_dil,
        w_up_t=w["w_up"].T.astype(MXU_DTYPE), w_down=w["w_down"].astype(MXU_DTYPE), conv_w=w["conv_w"].astype(F32),
        g_cq=_row(w["g_cq"]), g_ckv=_row(w["g_ckv"]), ln1_g=_row(w["ln1_g"]), ln1_b=_row(w["ln1_b"]),
        conv_b=_row(w["conv_b"]), ln2_g=_row(w["ln2_g"]), ln2_b=_row(w["ln2_b"]))


def _natural_grads(g):
    return dict(
        w_in=_split_unpad_rows(g["w_in_t"]).T, g_cq=g["g_cq"].reshape(-1), g_ckv=g["g_ckv"].reshape(-1),
        w_uq=_unpad_heads(g["wq"], NOPE_DIM + ROPE_DIM), w_uk=_unpad_heads(g["wk"], NOPE_DIM),
        w_uv=_unpad_heads(g["wv"], HEAD_DIM), w_o=_unpad_w_o(g["w_o_mla"], g["w_o_dil"]), ln1_g=g["ln1_g"].reshape(-1),
        ln1_b=g["ln1_b"].reshape(-1), w_up=g["w_up_t"].T, conv_w=g["conv_w"], conv_b=g["conv_b"].reshape(-1),
        w_down=g["w_down"], ln2_g=g["ln2_g"].reshape(-1), ln2_b=g["ln2_b"].reshape(-1))


def _layer_grads(x0, target, cw):
    seq = x0.shape[0]
    ctab, stab = _rope_tables(seq)
    gq, gk, cb = cw["g_cq"], cw["g_ckv"], cw["conv_b"]
    wq, wk, wv = cw["wq"], cw["wk"], cw["wv"]
    x0b = x0.astype(MXU_DTYPE)

    h = _mm(x0b, cw["w_in_t"], name="mm_h", tb=True, tm=1024, tn=1024, tk=1024)
    qf, kf, vp = _mla_prep(h, gq, gk, wq, wk, wv, ctab, stab)
    o_mla, o_mla_b, lse_mla = _mla_attn_fwd(qf, kf, vp)
    o_dil, o_dil_b, lse_dil = _dil_fwd(h)
    z1, x1, x1b = _mix_ln1(o_mla_b, o_dil_b, cw["w_o_mla"], cw["w_o_dil"], x0, cw["ln1_g"], cw["ln1_b"])
    u = _mm(x1b, cw["w_up_t"], name="mm_up", tb=True, tm=1024, tn=512, tk=1024)
    act = _conv_gate_fwd(u, cw["conv_w"], cb)
    ffn = _mm(act, cw["w_down"], name="mm_down", tm=1024, tn=1024, tk=2816)
    dz2, dz2b, loss, d_ln2_g, d_ln2_b = _ln2_loss_bwd(x1, ffn, target, cw["ln2_g"], cw["ln2_b"])

    d_act = _mm(dz2b, cw["w_down"], name="mm_d_act", tb=True, out_dtype=MXU_DTYPE, tm=1024, tn=1408, tk=1024)
    d_w_down = _mm(act, dz2b, name="mm_dw_down", ta=True, out_dtype=MXU_DTYPE, tm=1408, tn=1024, tk=1024)
    du, d_conv_w, d_conv_b = _conv_gate_bwd(u, d_act, cw["conv_w"], cb)
    dx1 = _mm(du, cw["w_up_t"], name="mm_dx1", res=dz2, res_scale=DN_ALPHA, tm=1024, tn=1024, tk=1408)
    d_w_up_t = _mm(du, x1b, name="mm_dw_up", ta=True, out_dtype=MXU_DTYPE, tm=1408, tn=1024, tk=1024)
    dz1, dz1b, d_ln1_g, d_ln1_b = _ln_bwd(dx1, z1, cw["ln1_g"], "ln1_bwd")
    do_mla = _mm_do_mla(dz1b, cw["w_o_mla"])
    do_dil = _mm(dz1b, cw["w_o_dil"], name="mm_do_dil", tb=True, tm=1024, tn=512, tk=1024)
    d_w_o_mla = _mm_dw_o_mla(o_mla_b, dz1b)
    d_w_o_dil = _mm(o_dil_b, dz1b, name="mm_dw_o_dil", ta=True, out_dtype=MXU_DTYPE, tm=512, tn=1024, tk=1024)
    dq_dil, dk_dil, dv_dil = _dil_bwd(h, o_dil, lse_dil, do_dil)
    dqf, dkf, dvf = _mla_attn_bwd(qf, kf, vp, o_mla, lse_mla, do_mla)
    dh_mla, d_wq, d_wk, d_wv, d_gq, d_gk = _mla_prep_bwd(h, gq, gk, wq, wk, wv, ctab, stab, dqf, dkf, dvf)
    dh = _assemble_dh(dh_mla, dq_dil, dk_dil, dv_dil)
    grad_x = _mm(dh, cw["w_in_t"], name="mm_dx0", res=dz1, res_scale=DN_ALPHA, tm=1024, tn=1024, tk=2048)
    d_w_in_t = _mm(dh, x0b, name="mm_dw_in", ta=True, out_dtype=MXU_DTYPE, tm=1024, tn=1024, tk=1024)

    grads = dict(
        w_in_t=d_w_in_t, wq=d_wq, wk=d_wk, wv=d_wv, w_o_mla=d_w_o_mla, w_o_dil=d_w_o_dil, w_up_t=d_w_up_t, w_down=d_w_down,
        conv_w=d_conv_w, g_cq=d_gq, g_ckv=d_gk, ln1_g=d_ln1_g, ln1_b=d_ln1_b, conv_b=d_conv_b, ln2_g=d_ln2_g, ln2_b=d_ln2_b)
    return loss[0, 0], grad_x, grads


def _all_gather(blocks, name):
    na = len(blocks)

    def body(*refs):
        ins, outs = refs[:na], refs[na:2 * na]
        send_sems, recv_sems, local_sems = refs[2 * na:]
        x, y, c = lax.axis_index("x"), lax.axis_index("y"), lax.axis_index("c")
        me, sibling = (x, y, c), (x, y, 1 - c)
        chips = [(1 - x, y), (x, 1 - y), (1 - x, 1 - y)]

        def slot(out, pos):
            return out.at[4 * pos[0] + 2 * pos[1] + pos[2]]

        def copy(a, k, block, to, src=None):
            return pltpu.make_async_remote_copy(
                src_ref=slot(outs[a], block) if src is None else src, dst_ref=slot(outs[a], block),
                send_sem=send_sems.at[7 * a + k], recv_sem=recv_sems.at[7 * a + k],
                device_id=to, device_id_type=pl.DeviceIdType.MESH)

        mine = [pltpu.make_async_copy(ins[a], slot(outs[a], me), local_sems.at[a]) for a in range(na)]
        for cp in mine:
            cp.start()
        first = []
        for a in range(na):
            first.append(copy(a, 0, me, sibling, src=ins[a]))
            first += [copy(a, 1 + j, me, (*chip, c), src=ins[a]) for j, chip in enumerate(chips)]
        for cp in first:
            cp.start()
        passed = []
        for j, chip in enumerate(chips):
            for a in range(na):
                copy(a, 1 + j, (*chip, c), me).wait_recv()
                cp = copy(a, 4 + j, (*chip, c), sibling)
                cp.start()
                passed.append(cp)
        for a in range(na):
            copy(a, 0, sibling, me).wait_recv()
            for j, chip in enumerate(chips):
                copy(a, 4 + j, (*chip, 1 - c), me).wait_recv()
        for cp in first + passed:
            cp.wait_send()
        for cp in mine:
            cp.wait()

    any_spec = pl.BlockSpec(memory_space=pl.ANY)
    return pl.pallas_call(
        body, name=name, in_specs=[any_spec] * na, out_specs=[any_spec] * na,
        out_shape=[jax.ShapeDtypeStruct((N_DEV,) + b.shape, b.dtype) for b in blocks],
        scratch_shapes=[pltpu.SemaphoreType.DMA((7 * na,)), pltpu.SemaphoreType.DMA((7 * na,)), pltpu.SemaphoreType.DMA((na,))],
    )(*blocks)


def _exchange(parts, name):
    na = len(parts)

    def body(*refs):
        ins, outs = refs[:na], refs[na:2 * na]
        send_sems, recv_sems, local_sems = refs[2 * na:]
        x, y, c = lax.axis_index("x"), lax.axis_index("y"), lax.axis_index("c")
        me = 4 * x + 2 * y + c

        def peer_of(d):
            return x ^ (d >> 2), y ^ ((d >> 1) & 1), c ^ (d & 1)

        def copy(a, d):
            px, py, pc = peer_of(d)
            return pltpu.make_async_remote_copy(
                src_ref=ins[a].at[4 * px + 2 * py + pc], dst_ref=outs[a].at[me],
                send_sem=send_sems.at[7 * a + d - 1], recv_sem=recv_sems.at[7 * a + d - 1],
                device_id=(px, py, pc), device_id_type=pl.DeviceIdType.MESH)

        def arrival(a, d):
            px, py, pc = peer_of(d)
            return pltpu.make_async_remote_copy(
                src_ref=ins[a].at[me], dst_ref=outs[a].at[4 * px + 2 * py + pc],
                send_sem=send_sems.at[7 * a + d - 1], recv_sem=recv_sems.at[7 * a + d - 1],
                device_id=(px, py, pc), device_id_type=pl.DeviceIdType.MESH)

        mine = [pltpu.make_async_copy(ins[a].at[me], outs[a].at[me], local_sems.at[a]) for a in range(na)]
        for cp in mine:
            cp.start()
        sent = [copy(a, d) for a in range(na) for d in range(1, N_DEV)]
        for cp in sent:
            cp.start()
        for a in range(na):
            for d in range(1, N_DEV):
                arrival(a, d).wait_recv()
        for cp in sent:
            cp.wait_send()
        for cp in mine:
            cp.wait()

    any_spec = pl.BlockSpec(memory_space=pl.ANY)
    return pl.pallas_call(
        body, name=name, in_specs=[any_spec] * na, out_specs=[any_spec] * na,
        out_shape=[jax.ShapeDtypeStruct(p.shape, p.dtype) for p in parts],
        scratch_shapes=[pltpu.SemaphoreType.DMA((7 * na,)), pltpu.SemaphoreType.DMA((7 * na,)), pltpu.SemaphoreType.DMA((na,))],
    )(*parts)


def _sum_parts(parts, name):
    npart, r, n = parts.shape
    tr = r if r <= 256 else max(t for t in range(16, 257, 16) if r % t == 0)

    def body(p_ref, o_ref):
        g = p_ref[0].astype(F32)
        for s in range(1, npart):
            g = g + p_ref[s].astype(F32)
        o_ref[...] = g

    return pl.pallas_call(
        body, name=name, grid=(r // tr,), in_specs=[pl.BlockSpec((npart, tr, n), lambda i: (0, i, 0))],
        out_specs=pl.BlockSpec((tr, n), lambda i: (i, 0)), out_shape=jax.ShapeDtypeStruct((r, n), F32),
        compiler_params=_params("parallel"),
    )(parts)


def _adamw(parts, w, m, v, name):
    npart, r, n = parts.shape
    tr = r if r <= 256 else max(t for t in range(16, 257, 16) if r % t == 0)
    c1 = 1.0 - ADAM_B1 ** ADAM_STEP
    c2 = 1.0 - ADAM_B2 ** ADAM_STEP

    def body(p_ref, w_ref, m_ref, v_ref, g_out, d_out, m_out, v_out):
        g = p_ref[0].astype(F32)
        for s in range(1, npart):
            g = g + p_ref[s].astype(F32)
        m_new = ADAM_B1 * m_ref[...] + (1.0 - ADAM_B1) * g
        v_new = ADAM_B2 * v_ref[...] + (1.0 - ADAM_B2) * (g * g)
        g_out[...] = g
        m_out[...] = m_new
        v_out[...] = v_new
        d_out[...] = -ADAM_LR * ((m_new / c1) / (jnp.sqrt(v_new / c2) + ADAM_EPS) + ADAM_WD * w_ref[...])

    blk = pl.BlockSpec((tr, n), lambda i: (i, 0))
    shp = jax.ShapeDtypeStruct((r, n), F32)
    return pl.pallas_call(
        body, name=name, grid=(r // tr,), in_specs=[pl.BlockSpec((npart, tr, n), lambda i: (0, i, 0)), blk, blk, blk],
        out_specs=[blk] * 4, out_shape=[shp] * 4, compiler_params=_params("parallel"),
    )(parts, w, m, v)


def _pack(arrs, dtype, row_multiple=16):
    flat = jnp.concatenate([a.reshape(-1).astype(dtype) for a in arrs])
    rows = -(-flat.shape[0] // PACK_COLS)
    rows = -(-rows // row_multiple) * row_multiple
    return jnp.pad(flat, (0, rows * PACK_COLS - flat.shape[0])).reshape(rows, PACK_COLS)


def _unpack(buf, shapes):
    lead = buf.shape[:-2]
    flat = buf.reshape(lead + (-1,))
    out, at = [], 0
    for s in shapes:
        size = math.prod(s)
        out.append(flat[..., at:at + size].reshape(lead + tuple(s)))
        at += size
    return out


REPLICATED = ("g_cq", "g_ckv", "w_uk", "w_uv", "ln1_g", "ln1_b", "conv_b", "ln2_g", "ln2_b")
ALL_WEIGHTS = ("w_in", "g_cq", "g_ckv", "w_uq", "w_uk", "w_uv", "w_o", "ln1_g", "ln1_b", "w_up", "conv_w", "conv_b",
               "w_down", "ln2_g", "ln2_b")


def kernel(x, w_in, g_cq, g_ckv, w_uq, w_uk, w_uv, w_o, ln1_g, ln1_b, w_up, conv_w, conv_b, w_down, ln2_g, ln2_b, loss_target, m_w_in, m_g_cq, m_g_ckv, m_w_uq, m_w_uk, m_w_uv, m_w_o, m_ln1_g, m_ln1_b, m_w_up, m_conv_w, m_conv_b, m_w_down, m_ln2_g, m_ln2_b, v_w_in, v_g_cq, v_g_ckv, v_w_uq, v_w_uk, v_w_uv, v_w_o, v_ln1_g, v_ln1_b, v_w_up, v_conv_w, v_conv_b, v_w_down, v_ln2_g, v_ln2_b):
    w = dict(w_in=w_in, g_cq=g_cq, g_ckv=g_ckv, w_uq=w_uq, w_uk=w_uk, w_uv=w_uv, w_o=w_o, ln1_g=ln1_g, ln1_b=ln1_b,
             w_up=w_up, conv_w=conv_w, conv_b=conv_b, w_down=w_down, ln2_g=ln2_g, ln2_b=ln2_b)
    m = dict(w_in=m_w_in, g_cq=m_g_cq, g_ckv=m_g_ckv, w_uq=m_w_uq, w_uk=m_w_uk, w_uv=m_w_uv, w_o=m_w_o, ln1_g=m_ln1_g,
             ln1_b=m_ln1_b, w_up=m_w_up, conv_w=m_conv_w, conv_b=m_conv_b, w_down=m_w_down, ln2_g=m_ln2_g, ln2_b=m_ln2_b)
    v = dict(w_in=v_w_in, g_cq=v_g_cq, g_ckv=v_g_ckv, w_uq=v_w_uq, w_uk=v_w_uk, w_uv=v_w_uv, w_o=v_w_o, ln1_g=v_ln1_g,
             ln1_b=v_ln1_b, w_up=v_w_up, conv_w=v_conv_w, conv_b=v_conv_b, w_down=v_w_down, ln2_g=v_ln2_g, ln2_b=v_ln2_b)
    me = 4 * lax.axis_index("x") + 2 * lax.axis_index("y") + lax.axis_index("c")
    wire = lambda a: a.astype(WIRE_DTYPE)
    n_in = w_in.shape[1]
    n_in_pad = -(-n_in // 16) * 16
    pad_taps = lambda a: jnp.pad(a, ((0, 8 - a.shape[0]), (0, 0)))

    g_in, g_uq, g_o, g_up, g_down, g_conv = _all_gather(
        [jnp.pad(wire(w_in).T, ((0, n_in_pad - n_in), (0, 0))), wire(w_uq).reshape(w_uq.shape[0], -1), wire(w_o),
         wire(w_up).T, wire(w_down), pad_taps(conv_w)], "gather_weights")
    w_o_mla, w_o_dil = _pad_w_o(g_o.reshape(-1, D_MODEL))
    cw = dict(
        w_in_t=_split_pad_rows(g_in[:, :n_in].reshape(-1, D_MODEL)).astype(MXU_DTYPE),
        wq=_pad_heads(g_uq.reshape((-1,) + w_uq.shape[1:]), NOPE_DIM + ROPE_DIM),
        wk=_pad_heads(w_uk, NOPE_DIM), wv=_pad_heads(w_uv, HEAD_DIM), w_o_mla=w_o_mla, w_o_dil=w_o_dil,
        w_up_t=g_up.reshape(-1, D_MODEL).astype(MXU_DTYPE), w_down=g_down.reshape(-1, D_MODEL).astype(MXU_DTYPE),
        conv_w=jnp.transpose(g_conv[:, :conv_w.shape[0]], (1, 0, 2)).reshape(conv_w.shape[0], -1),
        g_cq=_row(g_cq), g_ckv=_row(g_ckv), ln1_g=_row(ln1_g), ln1_b=_row(ln1_b), conv_b=_row(conv_b),
        ln2_g=_row(ln2_g), ln2_b=_row(ln2_b))

    loss, grad_x, g = _layer_grads(x[0], loss_target[0], cw)
    loss = lax.psum(loss, MESH_AXES)

    blocks = lambda a: wire(a).reshape((N_DEV, a.shape[0] // N_DEV) + a.shape[1:])
    d_in = jnp.pad(blocks(_split_unpad_rows(g["w_in_t"])), ((0, 0), (0, n_in_pad - n_in), (0, 0)))
    d_uq = blocks(_unpad_heads(g["wq"], NOPE_DIM + ROPE_DIM).reshape(Q_RANK, -1))
    r_in, r_uq, r_o, r_up, r_down = _exchange(
        [d_in, d_uq, blocks(_unpad_w_o(g["w_o_mla"], g["w_o_dil"])), blocks(g["w_up_t"]), blocks(g["w_down"])],
        "exchange_grads")
    nat = dict(g_cq=g["g_cq"], g_ckv=g["g_ckv"], w_uk=_unpad_heads(g["wk"], NOPE_DIM), w_uv=_unpad_heads(g["wv"], HEAD_DIM),
               ln1_g=g["ln1_g"], ln1_b=g["ln1_b"], conv_b=g["conv_b"], ln2_g=g["ln2_g"], ln2_b=g["ln2_b"])
    rep_all, cw_all = _all_gather([_pack([nat[n] for n in REPLICATED], F32, 8), pad_taps(g["conv_w"])], "gather_small_grads")

    out = {}

    def update(name, parts, shape2d=None):
        w2, m2, v2 = [d[name].reshape(shape2d or d[name].shape) for d in (w, m, v)]
        res = _adamw(parts, w2, m2, v2, "adamw_" + name)
        for kind, a in zip(("grad", "delta", "new_m", "new_v"), res):
            out[kind, name] = a.reshape(w[name].shape)

    update("w_in", _sum_parts(r_in, "sum_w_in")[:n_in].T[None])
    update("w_uq", r_uq, (w_uq.shape[0], -1))
    update("w_o", r_o)
    update("w_up", _sum_parts(r_up, "sum_w_up").T[None])
    update("w_down", r_down)
    res = _adamw(rep_all, *[_pack([d[n] for n in REPLICATED], F32, 8) for d in (w, m, v)], "adamw_replicated")
    for kind, buf in zip(("grad", "delta", "new_m", "new_v"), res):
        for n, a in zip(REPLICATED, _unpack(buf, [w[n].shape for n in REPLICATED])):
            out[kind, n] = a
    ncw = conv_w.shape[1]
    update("conv_w", lax.dynamic_slice_in_dim(cw_all[:, :conv_w.shape[0]], me * ncw, ncw, axis=2))

    return (loss, grad_x[None], *[out[kind, n] for kind in ("grad", "delta", "new_m", "new_v") for n in ALL_WEIGHTS])
```

```python
import functools
import math

import jax
import jax.numpy as jnp
from jax import lax
from jax.experimental import pallas as pl
from jax.experimental.pallas import tpu as pltpu

F32 = jnp.float32
MXU_DTYPE = jnp.bfloat16
WIRE_DTYPE = jnp.bfloat16

N_DEV = 8
D_MODEL = 1024
HEADS = 8
HEAD_DIM = 64
LANES = 128
Q_RANK, KV_RANK, ROPE_DIM, NOPE_DIM = 256, 128, 32, 64
DIL_WIDTH = HEADS * HEAD_DIM
IN_WIDTH = 1952
IN_PAD = 2048
D_FF = 2816
ROPE_THETA = 10000.0
DIL_PAIRS = ((128, 1), (512, 4), (2048, 16))
DIL_BLOCK = 128
DN_ALPHA = 2.0 ** 0.25
LN_EPS = 1e-5
RMS_EPS = 1e-6
MLA_SCALE = 1.0 / math.sqrt(NOPE_DIM + ROPE_DIM)
MLA_SCALE_LOG2 = MLA_SCALE * math.log2(math.e)
DIL_SCALE = 1.0 / math.sqrt(HEAD_DIM)
ALIBI_SLOPES = tuple(2.0 ** (-8.0 * (h + 1) / HEADS) for h in range(HEADS))
NEG_BIG = -1e30
ADAM_LR, ADAM_B1, ADAM_B2, ADAM_EPS, ADAM_WD, ADAM_STEP = 0.001, 0.9, 0.999, 1e-08, 0.01, 10
VMEM_LIMIT = 48 * 1024 * 1024
PACK_COLS = 1024

MESH_AXES = ("x", "y", "c")


def _params(*sem):
    return pltpu.CompilerParams(dimension_semantics=sem or None, vmem_limit_bytes=VMEM_LIMIT)


def _dot(a, b, ca, cb):
    return lax.dot_general(a, b, (((ca,), (cb,)), ((), ())), preferred_element_type=F32)


def _mm(a, b, *, name, tm, tn, tk, ta=False, tb=False, out_dtype=F32, res=None, res_scale=1.0):
    m, k = (a.shape[1], a.shape[0]) if ta else a.shape
    n = b.shape[0] if tb else b.shape[1]
    assert (b.shape[1] if tb else b.shape[0]) == k
    tm, tn, tk = min(tm, m), min(tn, n), min(tk, k)
    assert m % tm == 0 and n % tn == 0 and k % tk == 0, (name, m, n, k, tm, tn, tk)
    nk = k // tk
    a_spec = (pl.BlockSpec((tk, tm), lambda i, j, kk: (kk, i)) if ta
              else pl.BlockSpec((tm, tk), lambda i, j, kk: (i, kk)))
    b_spec = (pl.BlockSpec((tn, tk), lambda i, j, kk: (j, kk)) if tb
              else pl.BlockSpec((tk, tn), lambda i, j, kk: (kk, j)))
    o_spec = pl.BlockSpec((tm, tn), lambda i, j, kk: (i, j))
    in_specs = [a_spec, b_spec]
    args = [a, b]
    if res is not None:
        in_specs.append(o_spec)
        args.append(res)
    ca, cb = (0 if ta else 1), (1 if tb else 0)

    def finish(acc, r_ref, o_ref):
        if r_ref is not None:
            acc = acc + res_scale * r_ref[...]
        o_ref[...] = acc.astype(o_ref.dtype)

    def body(*refs):
        a_ref, b_ref = refs[:2]
        r_ref = refs[2] if res is not None else None
        o_ref = refs[3] if res is not None else refs[2]
        part = _dot(a_ref[...].astype(MXU_DTYPE), b_ref[...].astype(MXU_DTYPE), ca, cb)
        if nk == 1:
            finish(part, r_ref, o_ref)
            return
        acc_ref = refs[-1]
        kk = pl.program_id(2)

        @pl.when(kk == 0)
        def _():
            acc_ref[...] = part

        @pl.when(kk > 0)
        def _():
            acc_ref[...] += part

        @pl.when(kk == nk - 1)
        def _():
            finish(acc_ref[...], r_ref, o_ref)

    return pl.pallas_call(
        body, name=name, grid=(m // tm, n // tn, nk), in_specs=in_specs, out_specs=o_spec,
        out_shape=jax.ShapeDtypeStruct((m, n), out_dtype),
        scratch_shapes=[pltpu.VMEM((tm, tn), F32)] if nk > 1 else [],
        compiler_params=_params("parallel", "parallel", "arbitrary"),
    )(*args)


def _mm_do_mla(dz, w_o_mla, tm=1024):
    seq, d = dz.shape
    tm = min(tm, seq)

    def body(a_ref, b_ref, o_ref):
        a = a_ref[...].astype(MXU_DTYPE)
        for hd in range(HEADS):
            o_ref[hd] = _dot(a, b_ref[LANES * hd:LANES * (hd + 1), :], 1, 1)

    return pl.pallas_call(
        body, name="mm_do_mla", grid=(seq // tm,),
        in_specs=[pl.BlockSpec((tm, d), lambda i: (i, 0)), pl.BlockSpec((HEADS * LANES, d), lambda i: (0, 0))],
        out_specs=pl.BlockSpec((HEADS, tm, LANES), lambda i: (0, i, 0)),
        out_shape=jax.ShapeDtypeStruct((HEADS, seq, LANES), F32), compiler_params=_params("parallel"),
    )(dz, w_o_mla)


def _mm_dw_o_mla(o_mla, dz, tk=1024):
    seq, d = dz.shape
    tk = min(tk, seq)
    nk = seq // tk

    def body(a_ref, b_ref, o_ref, acc_ref):
        kk = pl.program_id(0)

        @pl.when(kk == 0)
        def _():
            acc_ref[...] = jnp.zeros_like(acc_ref)

        b = b_ref[...].astype(MXU_DTYPE)
        for hd in range(HEADS):
            acc_ref[LANES * hd:LANES * (hd + 1), :] += _dot(a_ref[hd].astype(MXU_DTYPE), b, 0, 0)

        @pl.when(kk == nk - 1)
        def _():
            o_ref[...] = acc_ref[...].astype(o_ref.dtype)

    return pl.pallas_call(
        body, name="mm_dw_o_mla", grid=(nk,),
        in_specs=[pl.BlockSpec((HEADS, tk, LANES), lambda kk: (0, kk, 0)), pl.BlockSpec((tk, d), lambda kk: (kk, 0))],
        out_specs=pl.BlockSpec((HEADS * LANES, d), lambda kk: (0, 0)),
        out_shape=jax.ShapeDtypeStruct((HEADS * LANES, d), MXU_DTYPE),
        scratch_shapes=[pltpu.VMEM((HEADS * LANES, d), F32)], compiler_params=_params("arbitrary"),
    )(o_mla, dz)


def _rope_tables(seq):
    half = ROPE_DIM // 2
    freqs = ROPE_THETA ** (-jnp.arange(half, dtype=F32) / half)
    ang = jnp.arange(seq).astype(F32)[:, None] * freqs[None, :]
    cos, sin = jnp.cos(ang), jnp.sin(ang)
    one = jnp.ones((seq, NOPE_DIM), F32)
    tail = jnp.ones((seq, LANES - NOPE_DIM - ROPE_DIM), F32)
    ctab = jnp.concatenate([one, cos, cos, tail], axis=1)
    stab = jnp.concatenate([0 * one, -sin, sin, 0 * tail], axis=1)
    return ctab, stab


def _rope_swap(t):
    lane = lax.broadcasted_iota(jnp.int32, t.shape, 1)
    half = ROPE_DIM // 2
    return jnp.where(lane < NOPE_DIM + half, pltpu.roll(t, LANES - half, 1), pltpu.roll(t, half, 1))


def _rope(t, ctab, stab):
    return t * ctab + _rope_swap(t) * stab


def _rope_inv(t, ctab, stab):
    return t * ctab - _rope_swap(t) * stab


def _rms(x, g):
    r = lax.rsqrt(jnp.mean(x * x, axis=-1, keepdims=True) + RMS_EPS)
    xh = x * r
    return xh, r, xh * g


def _mla_prep(h, g_cq, g_ckv, wq, wk, wv, ctab, stab, tm=512):
    seq = h.shape[0]
    tm = min(tm, seq)

    def body(h_ref, gq_ref, gk_ref, wq_ref, wk_ref, wv_ref, c_ref, s_ref, q_out, k_out, v_out):
        hb = h_ref[...]
        ctab_, stab_ = c_ref[...], s_ref[...]
        _, _, cqn = _rms(hb[:, :Q_RANK], gq_ref[...])
        _, _, ckn = _rms(hb[:, Q_RANK:Q_RANK + KV_RANK], gk_ref[...])
        cqn = cqn.astype(MXU_DTYPE)
        ckn = ckn.astype(MXU_DTYPE)
        krr = _rope(hb[:, Q_RANK + KV_RANK:], ctab_, stab_)
        for hd in range(HEADS):
            q = _dot(cqn, wq_ref[hd], 1, 0)
            q_out[hd] = _rope(q, ctab_, stab_).astype(q_out.dtype)
            k_out[hd] = (_dot(ckn, wk_ref[hd], 1, 0) + krr).astype(k_out.dtype)
            v_out[hd] = _dot(ckn, wv_ref[hd], 1, 0).astype(v_out.dtype)

    full = lambda *shape: pl.BlockSpec(shape, lambda i: (0,) * len(shape))
    slab = pl.BlockSpec((HEADS, tm, LANES), lambda i: (0, i, 0))
    shp = jax.ShapeDtypeStruct((HEADS, seq, LANES), MXU_DTYPE)
    return pl.pallas_call(
        body, name="mla_prep", grid=(seq // tm,),
        in_specs=[pl.BlockSpec((tm, 512), lambda i: (i, 0)), full(1, Q_RANK), full(1, KV_RANK),
                  full(HEADS, Q_RANK, LANES), full(HEADS, KV_RANK, LANES), full(HEADS, KV_RANK, LANES),
                  pl.BlockSpec((tm, LANES), lambda i: (i, 0)), pl.BlockSpec((tm, LANES), lambda i: (i, 0))],
        out_specs=[slab, slab, slab], out_shape=[shp, shp, shp],
        compiler_params=_params("parallel"),
    )(h, g_cq, g_ckv, wq, wk, wv, ctab, stab)


def _mla_prep_bwd(h, g_cq, g_ckv, wq, wk, wv, ctab, stab, dq, dk, dv, tm=512):
    seq = h.shape[0]
    tm = min(tm, seq)

    def body(h_ref, gq_ref, gk_ref, wq_ref, wk_ref, wv_ref, c_ref, s_ref, dq_ref, dk_ref, dv_ref,
             dh_ref, dwq_ref, dwk_ref, dwv_ref, dgq_ref, dgk_ref):
        @pl.when(pl.program_id(0) == 0)
        def _():
            for r in (dwq_ref, dwk_ref, dwv_ref, dgq_ref, dgk_ref):
                r[...] = jnp.zeros_like(r)

        hb = h_ref[...]
        ctab_, stab_ = c_ref[...], s_ref[...]
        gq, gk = gq_ref[...], gk_ref[...]
        xq, rq, cqn = _rms(hb[:, :Q_RANK], gq)
        xk, rk, ckn = _rms(hb[:, Q_RANK:Q_RANK + KV_RANK], gk)
        cqn = cqn.astype(MXU_DTYPE)
        ckn = ckn.astype(MXU_DTYPE)
        d_cqn = jnp.zeros((tm, Q_RANK), F32)
        d_ckn = jnp.zeros((tm, KV_RANK), F32)
        d_krr = jnp.zeros((tm, LANES), F32)
        for hd in range(HEADS):
            dqh = _rope_inv(dq_ref[hd], ctab_, stab_).astype(MXU_DTYPE)
            d_cqn += _dot(dqh, wq_ref[hd], 1, 1)
            dwq_ref[hd] += _dot(cqn, dqh, 0, 0)
            dkh = dk_ref[hd]
            d_krr += dkh
            dkh = dkh.astype(MXU_DTYPE)
            d_ckn += _dot(dkh, wk_ref[hd], 1, 1)
            dwk_ref[hd] += _dot(ckn, dkh, 0, 0)
            dvh = dv_ref[hd].astype(MXU_DTYPE)
            d_ckn += _dot(dvh, wv_ref[hd], 1, 1)
            dwv_ref[hd] += _dot(ckn, dvh, 0, 0)
        lane = lax.broadcasted_iota(jnp.int32, (tm, LANES), 1)
        rot = (lane >= NOPE_DIM) & (lane < NOPE_DIM + ROPE_DIM)
        d_kr = jnp.where(rot, _rope_inv(jnp.where(rot, d_krr, 0.0), ctab_, stab_), 0.0)

        def rms_bwd(dy, xh, r, g, dg_ref):
            dg_ref[...] += jnp.sum(dy * xh, axis=0, keepdims=True)
            dxh = dy * g
            return r * (dxh - xh * jnp.mean(dxh * xh, axis=-1, keepdims=True))

        d_cq = rms_bwd(d_cqn, xq, rq, gq, dgq_ref)
        d_ck = rms_bwd(d_ckn, xk, rk, gk, dgk_ref)
        dh_ref[...] = jnp.concatenate([d_cq, d_ck, d_kr], axis=1).astype(dh_ref.dtype)

    full = lambda *shape: pl.BlockSpec(shape, lambda i: (0,) * len(shape))
    slab = pl.BlockSpec((HEADS, tm, LANES), lambda i: (0, i, 0))
    return pl.pallas_call(
        body, name="mla_prep_bwd", grid=(seq // tm,),
        in_specs=[pl.BlockSpec((tm, 512), lambda i: (i, 0)), full(1, Q_RANK), full(1, KV_RANK),
                  full(HEADS, Q_RANK, LANES), full(HEADS, KV_RANK, LANES), full(HEADS, KV_RANK, LANES),
                  pl.BlockSpec((tm, LANES), lambda i: (i, 0)), pl.BlockSpec((tm, LANES), lambda i: (i, 0)),
                  slab, slab, slab],
        out_specs=[pl.BlockSpec((tm, 512), lambda i: (i, 0)), full(HEADS, Q_RANK, LANES), full(HEADS, KV_RANK, LANES),
                   full(HEADS, KV_RANK, LANES), full(1, Q_RANK), full(1, KV_RANK)],
        out_shape=[jax.ShapeDtypeStruct((seq, 512), MXU_DTYPE), jax.ShapeDtypeStruct((HEADS, Q_RANK, LANES), F32),
                   jax.ShapeDtypeStruct((HEADS, KV_RANK, LANES), F32), jax.ShapeDtypeStruct((HEADS, KV_RANK, LANES), F32),
                   jax.ShapeDtypeStruct((1, Q_RANK), F32), jax.ShapeDtypeStruct((1, KV_RANK), F32)],
        compiler_params=_params("arbitrary"),
    )(h, g_cq, g_ckv, wq, wk, wv, ctab, stab, dq, dk, dv)


def _causal_mask(t):
    row = lax.broadcasted_iota(jnp.int32, (t, t), 0)
    col = lax.broadcasted_iota(jnp.int32, (t, t), 1)
    return row >= col


def _mla_attn_fwd(q, k, v, t=512):
    _, seq, _ = q.shape
    t = min(t, seq)

    def body(q_ref, k_ref, v_ref, o_ref, ob_ref, lse_ref, m_ref, l_ref, acc_ref, s_ref):
        i = pl.program_id(1)
        qb = q_ref[...]
        m_ref[...] = jnp.full_like(m_ref, NEG_BIG)
        l_ref[...] = jnp.zeros_like(l_ref)
        acc_ref[...] = jnp.zeros_like(acc_ref)

        def scores(j):
            return _dot(qb, k_ref[pl.ds(pl.multiple_of(j * t, t), t), :], 1, 1) * MLA_SCALE_LOG2

        def softmax_pv(j, s, masked):
            vb = v_ref[pl.ds(pl.multiple_of(j * t, t), t), :]
            if masked:
                s = jnp.where(_causal_mask(t), s, NEG_BIG)
            m_old = m_ref[...]
            m_new = jnp.maximum(m_old, jnp.max(s, axis=1, keepdims=True))
            p = jnp.exp2(s - m_new)
            a = jnp.exp2(m_old - m_new)
            l_ref[...] = a * l_ref[...] + jnp.sum(p, axis=1, keepdims=True)
            acc_ref[...] = a * acc_ref[...] + _dot(p.astype(MXU_DTYPE), vb, 1, 0)
            m_ref[...] = m_new

        s_ref[...] = scores(0)

        def loop_body(j, c):
            s_next = scores(j + 1)
            softmax_pv(j, s_ref[...], False)
            s_ref[...] = s_next
            return c

        lax.fori_loop(0, i, loop_body, 0)
        softmax_pv(i, s_ref[...], True)
        l = l_ref[...]
        o = acc_ref[...] * (1.0 / l)
        o_ref[...] = o
        ob_ref[...] = o.astype(ob_ref.dtype)
        lse_ref[...] = jnp.broadcast_to(m_ref[...] + jnp.log2(l), lse_ref.shape)

    blk = pl.BlockSpec((None, t, LANES), lambda h, i: (h, i, 0))
    whole = pl.BlockSpec((None, seq, LANES), lambda h, i: (h, 0, 0))
    shp = jax.ShapeDtypeStruct((HEADS, seq, LANES), F32)
    return pl.pallas_call(
        body, name="mla_attn_fwd", grid=(HEADS, seq // t),
        in_specs=[blk, whole, whole], out_specs=[blk, blk, blk],
        out_shape=[shp, jax.ShapeDtypeStruct((HEADS, seq, LANES), MXU_DTYPE), shp],
        scratch_shapes=[pltpu.VMEM((t, 1), F32), pltpu.VMEM((t, 1), F32), pltpu.VMEM((t, LANES), F32), pltpu.VMEM((t, t), F32)],
        compiler_params=_params("parallel", "arbitrary"),
    )(q, k, v)


def _mla_attn_bwd(q, k, v, o, lse, do, t=512):
    _, seq, _ = q.shape
    t = min(t, seq)
    nb = seq // t

    def body(q_ref, k_ref, v_ref, o_ref, lse_ref, do_ref, dq_ref, dk_ref, dv_ref, dl_ref, dka_ref, dva_ref):
        dq_ref[...] = jnp.zeros_like(dq_ref)

        def delta_body(i, c):
            rows = pl.ds(pl.multiple_of(i * t, t), t)
            dl_ref[rows, :] = jnp.sum(do_ref[rows, :] * o_ref[rows, :], axis=1, keepdims=True)
            return c

        lax.fori_loop(0, nb, delta_body, 0)

        def kblock(j, c):
            krows = pl.ds(pl.multiple_of(j * t, t), t)
            kb = k_ref[krows, :]
            vb = v_ref[krows, :]
            dka_ref[...] = jnp.zeros_like(dka_ref)
            dva_ref[...] = jnp.zeros_like(dva_ref)

            def qstep(i, masked):
                rows = pl.ds(pl.multiple_of(i * t, t), t)
                qb = q_ref[rows, :]
                dob = do_ref[rows, :].astype(MXU_DTYPE)
                s = _dot(qb, kb, 1, 1) * MLA_SCALE_LOG2
                if masked:
                    s = jnp.where(_causal_mask(t), s, NEG_BIG)
                p = jnp.exp2(s - lse_ref[rows, 0:1])
                dva_ref[...] += _dot(p.astype(MXU_DTYPE), dob, 0, 0)
                dp = _dot(dob, vb, 1, 1)
                ds = (p * (dp - dl_ref[rows, :]) * MLA_SCALE).astype(MXU_DTYPE)
                dka_ref[...] += _dot(ds, qb, 0, 0)
                dq_ref[rows, :] += _dot(ds, kb, 1, 0)

            qstep(j, True)

            def qloop(i, c2):
                qstep(i, False)
                return c2

            lax.fori_loop(j + 1, nb, qloop, 0)
            dk_ref[krows, :] = dka_ref[...]
            dv_ref[krows, :] = dva_ref[...]
            return c

        lax.fori_loop(0, nb, kblock, 0)

    whole = pl.BlockSpec((None, seq, LANES), lambda h: (h, 0, 0))
    shp = jax.ShapeDtypeStruct((HEADS, seq, LANES), F32)
    return pl.pallas_call(
        body, name="mla_attn_bwd", grid=(HEADS,),
        in_specs=[whole] * 6, out_specs=[whole] * 3, out_shape=[shp] * 3,
        scratch_shapes=[pltpu.VMEM((seq, 1), F32), pltpu.VMEM((t, LANES), F32), pltpu.VMEM((t, LANES), F32)],
        compiler_params=_params("parallel"),
    )(q, k, v, o, lse, do)


DIL_CHUNK = DIL_BLOCK * max(d for _, d in DIL_PAIRS)
DIL_PAIR_LANES = 2 * HEAD_DIM
assert DIL_PAIR_LANES == LANES
DIL_UNROLL_FWD = 4
DIL_UNROLL_BWD = 4


def _dil_bias_tables(hp, dil):
    b = DIL_BLOCK
    iq = lax.broadcasted_iota(jnp.int32, (b, 2 * b), 0)
    ik = lax.broadcasted_iota(jnp.int32, (b, 2 * b), 1)
    off = iq + b - ik
    band = (off >= 0) & (off <= b)
    dist = (off * dil).astype(F32)
    tables = []
    for hh in range(2):
        slope = jnp.where(hp == 0, ALIBI_SLOPES[hh], jnp.where(hp == 1, ALIBI_SLOPES[2 + hh],
                          jnp.where(hp == 2, ALIBI_SLOPES[4 + hh], ALIBI_SLOPES[6 + hh]))).astype(F32)
        bias = -slope * dist
        tables.append((jnp.where(band, bias, NEG_BIG), jnp.where(band & (ik >= b), bias, NEG_BIG)))
    return tables


def _dil_rows(start, dil):
    return pl.ds(start, DIL_BLOCK) if dil == 1 else pl.ds(start, DIL_BLOCK, stride=dil)


def _dil_block_pos(blk, c, dil):
    sc, r = blk // dil, blk % dil
    q0 = sc * (DIL_BLOCK * dil) + r
    kcur0 = c * DIL_CHUNK + q0
    first = kcur0 < DIL_BLOCK * dil
    kprev0 = jnp.where(first, kcur0, kcur0 - DIL_BLOCK * dil)
    return q0, kcur0, kprev0, first


def _pair_cols(hh):
    return slice(HEAD_DIM * hh, HEAD_DIM * (hh + 1))


def _first_head_lanes(shape):
    return lax.broadcasted_iota(jnp.int32, shape, 1) < HEAD_DIM


def _split_pair(t):
    first = _first_head_lanes(t.shape)
    return jnp.where(first, t, 0.0).astype(MXU_DTYPE), jnp.where(first, 0.0, t).astype(MXU_DTYPE)


def _join_pair(ts):
    return jnp.where(_first_head_lanes(ts[0].shape), ts[0], ts[1])


def _dil_fwd(h):
    seq = h.shape[0]
    assert seq % DIL_CHUNK == 0
    nblk = DIL_CHUNK // DIL_BLOCK
    rc = 256

    def body(q_ref, k_ref, v_ref, o_ref, ob_ref, lse_ref, *scr):
        o_scr, l_scr = scr[:3], scr[3:]
        hp, c = pl.program_id(0), pl.program_id(1)
        for bi, (_, dil) in enumerate(DIL_PAIRS):
            tables = _dil_bias_tables(hp, dil)

            def block(blk, carry, bi=bi, dil=dil, tables=tables):
                q0, kcur0, kprev0, first = _dil_block_pos(blk, c, dil)
                qs = _split_pair(q_ref[_dil_rows(q0, dil), :] * DIL_SCALE)
                kcat = jnp.concatenate([k_ref[_dil_rows(kprev0, dil), :], k_ref[_dil_rows(kcur0, dil), :]], axis=0).astype(MXU_DTYPE)
                vcat = jnp.concatenate([v_ref[_dil_rows(kprev0, dil), :], v_ref[_dil_rows(kcur0, dil), :]], axis=0).astype(MXU_DTYPE)
                outs, lses = [], []
                for hh in range(2):
                    s = _dot(qs[hh], kcat, 1, 1) + jnp.where(first, tables[hh][1], tables[hh][0])
                    mx = jnp.max(s, axis=1, keepdims=True)
                    p = jnp.exp(s - mx)
                    l = jnp.sum(p, axis=1, keepdims=True)
                    outs.append(_dot(p.astype(MXU_DTYPE), vcat, 1, 0) * (1.0 / l))
                    lses.append(jnp.broadcast_to(mx + jnp.log(l), (DIL_BLOCK, LANES)))
                o_scr[bi][_dil_rows(q0, dil), :] = _join_pair(outs)
                l_scr[bi][_dil_rows(q0, dil), :] = _join_pair(lses)
                return carry

            lax.fori_loop(0, nblk, block, 0, unroll=DIL_UNROLL_FWD)

        def combine(i, carry):
            rows = pl.ds(pl.multiple_of(i * rc, rc), rc)
            ls = [l_scr[bi][rows, :] for bi in range(3)]
            mx = jnp.maximum(jnp.maximum(ls[0], ls[1]), ls[2])
            es = [jnp.exp(l - mx) for l in ls]
            den = es[0] + es[1] + es[2]
            o = (es[0] * o_scr[0][rows, :] + es[1] * o_scr[1][rows, :] + es[2] * o_scr[2][rows, :]) / den
            o_ref[rows, :] = o
            ob_ref[rows, :] = o.astype(ob_ref.dtype)
            lse_ref[rows, :] = mx + jnp.log(den)
            return carry

        lax.fori_loop(0, DIL_CHUNK // rc, combine, 0)

    nq = DIL_WIDTH // LANES
    chunk = lambda off: pl.BlockSpec((DIL_CHUNK, LANES), lambda hp, c: (c, off + hp))
    whole = lambda off: pl.BlockSpec((seq, LANES), lambda hp, c: (0, off + hp))
    shp = jax.ShapeDtypeStruct((seq, DIL_WIDTH), F32)
    return pl.pallas_call(
        body, name="dil_fwd", grid=(nq, seq // DIL_CHUNK),
        in_specs=[chunk(nq), whole(2 * nq), whole(3 * nq)], out_specs=[chunk(0), chunk(0), chunk(0)],
        out_shape=[shp, jax.ShapeDtypeStruct((seq, DIL_WIDTH), MXU_DTYPE), shp],
        scratch_shapes=[pltpu.VMEM((DIL_CHUNK, LANES), F32)] * 6,
        compiler_params=_params("parallel", "arbitrary"),
    )(h, h, h)


def _dil_bwd(h, o, lse, do):
    seq = h.shape[0]
    nblk = DIL_CHUNK // DIL_BLOCK
    rc = 256

    def body(q_ref, k_ref, v_ref, o_ref, lse_ref, do_ref, dq_ref, dk_ref, dv_ref, dl_scr):
        hp, c = pl.program_id(0), pl.program_id(1)

        @pl.when(c == 0)
        def _():
            dk_ref[...] = jnp.zeros_like(dk_ref)
            dv_ref[...] = jnp.zeros_like(dv_ref)

        def delta(i, carry):
            rows = pl.ds(pl.multiple_of(i * rc, rc), rc)
            prod = do_ref[rows, :] * o_ref[rows, :]
            dl_scr[rows, :] = jnp.concatenate(
                [jnp.broadcast_to(jnp.sum(prod[:, _pair_cols(hh)], axis=1, keepdims=True), (rc, HEAD_DIM)) for hh in range(2)], axis=1)
            return carry

        lax.fori_loop(0, DIL_CHUNK // rc, delta, 0)

        for bi, (_, dil) in enumerate(DIL_PAIRS):
            tables = _dil_bias_tables(hp, dil)

            def block(blk, carry, bi=bi, dil=dil, tables=tables):
                q0, kcur0, kprev0, first = _dil_block_pos(blk, c, dil)
                qrows = _dil_rows(q0, dil)
                qs = _split_pair(q_ref[qrows, :] * DIL_SCALE)
                kcat = jnp.concatenate([k_ref[_dil_rows(kprev0, dil), :], k_ref[_dil_rows(kcur0, dil), :]], axis=0).astype(MXU_DTYPE)
                vcat = jnp.concatenate([v_ref[_dil_rows(kprev0, dil), :], v_ref[_dil_rows(kcur0, dil), :]], axis=0).astype(MXU_DTYPE)
                dos = _split_pair(do_ref[qrows, :])
                lse_b = lse_ref[qrows, :]
                dl_b = dl_scr[qrows, :]
                dqs = []
                dk_b = dv_b = None
                for hh in range(2):
                    lane0 = HEAD_DIM * hh
                    s = _dot(qs[hh], kcat, 1, 1) + jnp.where(first, tables[hh][1], tables[hh][0])
                    p = jnp.exp(s - lse_b[:, lane0:lane0 + 1])
                    dp = _dot(dos[hh], vcat, 1, 1)
                    ds = (p * (dp - dl_b[:, lane0:lane0 + 1])).astype(MXU_DTYPE)
                    dqs.append(_dot(ds, kcat, 1, 0))
                    dk_h = _dot(ds, qs[hh], 0, 0)
                    dv_h = _dot(p.astype(MXU_DTYPE), dos[hh], 0, 0)
                    dk_b = dk_h if dk_b is None else dk_b + dk_h
                    dv_b = dv_h if dv_b is None else dv_b + dv_h
                dq_b = _join_pair(dqs) * DIL_SCALE
                if bi == 0:
                    dq_ref[qrows, :] = dq_b
                else:
                    dq_ref[qrows, :] += dq_b
                dk_ref[_dil_rows(kprev0, dil), :] += dk_b[:DIL_BLOCK]
                dv_ref[_dil_rows(kprev0, dil), :] += dv_b[:DIL_BLOCK]
                dk_ref[_dil_rows(kcur0, dil), :] += dk_b[DIL_BLOCK:]
                dv_ref[_dil_rows(kcur0, dil), :] += dv_b[DIL_BLOCK:]
                return carry

            lax.fori_loop(0, nblk, block, 0, unroll=DIL_UNROLL_BWD)

    nq = DIL_WIDTH // LANES
    chunk = lambda off: pl.BlockSpec((DIL_CHUNK, LANES), lambda hp, c: (c, off + hp))
    whole = lambda off: pl.BlockSpec((seq, LANES), lambda hp, c: (0, off + hp))
    shp = jax.ShapeDtypeStruct((seq, DIL_WIDTH), F32)
    return pl.pallas_call(
        body, name="dil_bwd", grid=(nq, seq // DIL_CHUNK),
        in_specs=[chunk(nq), whole(2 * nq), whole(3 * nq), chunk(0), chunk(0), chunk(0)],
        out_specs=[chunk(0), whole(0), whole(0)], out_shape=[shp, shp, shp],
        scratch_shapes=[pltpu.VMEM((DIL_CHUNK, LANES), F32)],
        compiler_params=_params("parallel", "arbitrary"),
    )(h, h, h, o, lse, do)


def _assemble_dh(dh_mla, dq, dk, dv, tm=512):
    seq = dh_mla.shape[0]
    tm = min(tm, seq)

    def body(a_ref, q_ref, k_ref, v_ref, o_ref):
        for j, r in enumerate((a_ref, q_ref, k_ref, v_ref)):
            o_ref[:, 512 * j:512 * (j + 1)] = r[...].astype(o_ref.dtype)

    blk = pl.BlockSpec((tm, 512), lambda i: (i, 0))
    return pl.pallas_call(
        body, name="assemble_dh", grid=(seq // tm,), in_specs=[blk] * 4,
        out_specs=pl.BlockSpec((tm, IN_PAD), lambda i: (i, 0)), out_shape=jax.ShapeDtypeStruct((seq, IN_PAD), MXU_DTYPE),
        compiler_params=_params("parallel"),
    )(dh_mla, dq, dk, dv)


def _ln_stats(z):
    mu = jnp.mean(z, axis=-1, keepdims=True)
    zc = z - mu
    r = lax.rsqrt(jnp.mean(zc * zc, axis=-1, keepdims=True) + LN_EPS)
    return zc * r, r


def _ln_bwd_math(dy, xh, r, g):
    dxh = dy * g
    return r * (dxh - jnp.mean(dxh, axis=-1, keepdims=True) - xh * jnp.mean(dxh * xh, axis=-1, keepdims=True))


def _mix_ln1(o_mla, o_dil, w_o_mla, w_o_dil, x0, g, b, tm=512):
    seq, d = x0.shape
    tm = min(tm, seq)

    def body(om_ref, od_ref, wm_ref, wd_ref, x_ref, g_ref, b_ref, z_ref, y_ref, yb_ref):
        mix = _dot(od_ref[...], wd_ref[...], 1, 0)
        for hd in range(HEADS):
            mix += _dot(om_ref[hd], wm_ref[LANES * hd:LANES * (hd + 1), :], 1, 0)
        z = DN_ALPHA * x_ref[...] + mix
        xh, _ = _ln_stats(z)
        y = xh * g_ref[...] + b_ref[...]
        z_ref[...] = z
        y_ref[...] = y
        yb_ref[...] = y.astype(yb_ref.dtype)

    blk = pl.BlockSpec((tm, d), lambda i: (i, 0))
    vec = pl.BlockSpec((1, d), lambda i: (0, 0))
    shp = jax.ShapeDtypeStruct((seq, d), F32)
    return pl.pallas_call(
        body, name="mix_ln1", grid=(seq // tm,),
        in_specs=[pl.BlockSpec((HEADS, tm, LANES), lambda i: (0, i, 0)), pl.BlockSpec((tm, DIL_WIDTH), lambda i: (i, 0)),
                  pl.BlockSpec((HEADS * LANES, d), lambda i: (0, 0)), pl.BlockSpec((DIL_WIDTH, d), lambda i: (0, 0)), blk, vec, vec],
        out_specs=[blk, blk, blk], out_shape=[shp, shp, jax.ShapeDtypeStruct((seq, d), MXU_DTYPE)],
        compiler_params=_params("parallel"))(o_mla, o_dil, w_o_mla, w_o_dil, x0, g, b)


def _ln_bwd(dy, z, g, name, tm=512):
    seq, d = z.shape
    tm = min(tm, seq)

    def body(dy_ref, z_ref, g_ref, dz_ref, dzb_ref, dg_ref, db_ref):
        @pl.when(pl.program_id(0) == 0)
        def _():
            dg_ref[...] = jnp.zeros_like(dg_ref)
            db_ref[...] = jnp.zeros_like(db_ref)

        dyb = dy_ref[...]
        xh, r = _ln_stats(z_ref[...])
        dg_ref[...] += jnp.sum(dyb * xh, axis=0, keepdims=True)
        db_ref[...] += jnp.sum(dyb, axis=0, keepdims=True)
        dz = _ln_bwd_math(dyb, xh, r, g_ref[...])
        dz_ref[...] = dz
        dzb_ref[...] = dz.astype(dzb_ref.dtype)

    blk = pl.BlockSpec((tm, d), lambda i: (i, 0))
    vec = pl.BlockSpec((1, d), lambda i: (0, 0))
    return pl.pallas_call(
        body, name=name, grid=(seq // tm,), in_specs=[blk, blk, vec], out_specs=[blk, blk, vec, vec],
        out_shape=[jax.ShapeDtypeStruct((seq, d), F32), jax.ShapeDtypeStruct((seq, d), MXU_DTYPE),
                   jax.ShapeDtypeStruct((1, d), F32), jax.ShapeDtypeStruct((1, d), F32)],
        compiler_params=_params("arbitrary"))(dy, z, g)


def _ln2_loss_bwd(x1, ffn, target, g, b, tm=512):
    seq, d = x1.shape
    tm = min(tm, seq)

    def body(x_ref, f_ref, t_ref, g_ref, b_ref, dz_ref, dzb_ref, loss_ref, dg_ref, db_ref):
        @pl.when(pl.program_id(0) == 0)
        def _():
            loss_ref[...] = jnp.zeros_like(loss_ref)
            dg_ref[...] = jnp.zeros_like(dg_ref)
            db_ref[...] = jnp.zeros_like(db_ref)

        gv = g_ref[...]
        z = DN_ALPHA * x_ref[...] + f_ref[...]
        xh, r = _ln_stats(z)
        err = (xh * gv + b_ref[...]) - t_ref[...]
        loss_ref[...] += 0.5 * jnp.sum(jnp.mean(err * err, axis=-1, keepdims=True), axis=0, keepdims=True)
        dy = err * (1.0 / d)
        dg_ref[...] += jnp.sum(dy * xh, axis=0, keepdims=True)
        db_ref[...] += jnp.sum(dy, axis=0, keepdims=True)
        dz = _ln_bwd_math(dy, xh, r, gv)
        dz_ref[...] = dz
        dzb_ref[...] = dz.astype(dzb_ref.dtype)

    blk = pl.BlockSpec((tm, d), lambda i: (i, 0))
    vec = pl.BlockSpec((1, d), lambda i: (0, 0))
    return pl.pallas_call(
        body, name="ln2_loss_bwd", grid=(seq // tm,), in_specs=[blk, blk, blk, vec, vec],
        out_specs=[blk, blk, pl.BlockSpec((1, LANES), lambda i: (0, 0)), vec, vec],
        out_shape=[jax.ShapeDtypeStruct((seq, d), F32), jax.ShapeDtypeStruct((seq, d), MXU_DTYPE),
                   jax.ShapeDtypeStruct((1, LANES), F32),
                   jax.ShapeDtypeStruct((1, d), F32), jax.ShapeDtypeStruct((1, d), F32)],
        compiler_params=_params("arbitrary"))(x1, ffn, target, g, b)


HALO = 16


def _conv_rows(e, w_ref, b_ref):
    y = b_ref[...] + w_ref[0:1, :] * pltpu.roll(e, 2, 0)
    y = y + w_ref[1:2, :] * pltpu.roll(e, 1, 0)
    return y + w_ref[2:3, :] * e


_GELU_C = math.sqrt(2.0 / math.pi)
_GELU_A = 0.044715


def _gelu(x):
    return 0.5 * x * (1.0 + jnp.tanh(_GELU_C * (x + _GELU_A * (x * x * x))))


CONV_TN = 256


def _ffn_interleave(a, axis):
    shp = a.shape
    a = a.reshape(shp[:axis] + (2, D_FF // CONV_TN, CONV_TN) + shp[axis + 1:])
    return jnp.swapaxes(a, axis, axis + 1).reshape(shp)


def _ffn_deinterleave(a, axis):
    shp = a.shape
    a = a.reshape(shp[:axis] + (D_FF // CONV_TN, 2, CONV_TN) + shp[axis + 1:])
    return jnp.swapaxes(a, axis, axis + 1).reshape(shp)


def _conv_gate_fwd(u, conv_w, conv_b, tm=1024):
    seq = u.shape[0]
    tm = min(tm, seq)
    tn = CONV_TN

    def body(u_ref, up_ref, w_ref, b_ref, o_ref):
        first = pl.program_id(0) == 0
        e = jnp.concatenate([jnp.where(first, 0.0, up_ref[...]), u_ref[...]], axis=0)
        y = _conv_rows(e, w_ref, b_ref)[HALO:]
        o_ref[...] = (_gelu(y[:, tn:]) * y[:, :tn]).astype(o_ref.dtype)

    hb = tm // HALO
    return pl.pallas_call(
        body, name="conv_gate_fwd", grid=(seq // tm, D_FF // tn),
        in_specs=[pl.BlockSpec((tm, 2 * tn), lambda i, j: (i, j)),
                  pl.BlockSpec((HALO, 2 * tn), lambda i, j: (jnp.maximum(i * hb - 1, 0), j)),
                  pl.BlockSpec((3, 2 * tn), lambda i, j: (0, j)), pl.BlockSpec((1, 2 * tn), lambda i, j: (0, j))],
        out_specs=pl.BlockSpec((tm, tn), lambda i, j: (i, j)), out_shape=jax.ShapeDtypeStruct((seq, D_FF), MXU_DTYPE),
        compiler_params=_params("parallel", "parallel"),
    )(u, u, conv_w, conv_b)


def _conv_gate_bwd(u, d_act, conv_w, conv_b, tm=512):
    seq = u.shape[0]
    tm = min(tm, seq)
    tn = CONV_TN
    ni = seq // tm
    rows_e = tm + 2 * HALO

    def body(u_ref, up_ref, un_ref, da_ref, dan_ref, w_ref, b_ref, du_ref, dw_ref, db_ref):
        i = pl.program_id(1)
        first, last = i == 0, i == ni - 1

        @pl.when(i == 0)
        def _():
            dw_ref[...] = jnp.zeros_like(dw_ref)
            db_ref[...] = jnp.zeros_like(db_ref)

        e = jnp.concatenate([jnp.where(first, 0.0, up_ref[...]), u_ref[...], jnp.where(last, 0.0, un_ref[...])], axis=0)
        y = _conv_rows(e, w_ref, b_ref)
        ya, yg = y[:, :tn], y[:, tn:]
        dact = jnp.concatenate([jnp.zeros((HALO, tn), F32), da_ref[...].astype(F32),
                                jnp.where(last, 0.0, dan_ref[...].astype(F32))], axis=0)
        th = jnp.tanh(_GELU_C * (yg + _GELU_A * (yg * yg * yg)))
        gelu = 0.5 * yg * (1.0 + th)
        gelu_grad = 0.5 * (1.0 + th) + 0.5 * yg * (1.0 - th * th) * (_GELU_C * (1.0 + 3.0 * _GELU_A * (yg * yg)))
        dy = jnp.concatenate([dact * gelu, dact * ya * gelu_grad], axis=1)
        du = w_ref[2:3, :] * dy + w_ref[1:2, :] * pltpu.roll(dy, rows_e - 1, 0) + w_ref[0:1, :] * pltpu.roll(dy, rows_e - 2, 0)
        du_ref[...] = du[HALO:HALO + tm].astype(du_ref.dtype)
        dyt = dy[HALO:HALO + tm]
        dw_ref[0:1, :] += jnp.sum(dyt * pltpu.roll(e, 2, 0)[HALO:HALO + tm], axis=0, keepdims=True)
        dw_ref[1:2, :] += jnp.sum(dyt * pltpu.roll(e, 1, 0)[HALO:HALO + tm], axis=0, keepdims=True)
        dw_ref[2:3, :] += jnp.sum(dyt * e[HALO:HALO + tm], axis=0, keepdims=True)
        db_ref[...] += jnp.sum(dyt, axis=0, keepdims=True)

    hb = tm // HALO
    nh = seq // HALO
    prev = lambda j, i: (jnp.maximum(i * hb - 1, 0), j)
    nxt = lambda j, i: (jnp.minimum((i + 1) * hb, nh - 1), j)
    return pl.pallas_call(
        body, name="conv_gate_bwd", grid=(D_FF // tn, ni),
        in_specs=[pl.BlockSpec((tm, 2 * tn), lambda j, i: (i, j)), pl.BlockSpec((HALO, 2 * tn), prev),
                  pl.BlockSpec((HALO, 2 * tn), nxt), pl.BlockSpec((tm, tn), lambda j, i: (i, j)), pl.BlockSpec((HALO, tn), nxt),
                  pl.BlockSpec((3, 2 * tn), lambda j, i: (0, j)), pl.BlockSpec((1, 2 * tn), lambda j, i: (0, j))],
        out_specs=[pl.BlockSpec((tm, 2 * tn), lambda j, i: (i, j)), pl.BlockSpec((3, 2 * tn), lambda j, i: (0, j)),
                   pl.BlockSpec((1, 2 * tn), lambda j, i: (0, j))],
        out_shape=[jax.ShapeDtypeStruct((seq, 2 * D_FF), MXU_DTYPE), jax.ShapeDtypeStruct((3, 2 * D_FF), F32),
                   jax.ShapeDtypeStruct((1, 2 * D_FF), F32)],
        compiler_params=_params("parallel", "arbitrary"),
    )(u, u, u, d_act, d_act, conv_w, conv_b)


def _pad_heads(w, width):
    w = jnp.transpose(w, (1, 0, 2))
    return jnp.pad(w, ((0, 0), (0, 0), (0, LANES - width))).astype(MXU_DTYPE)


def _unpad_heads(d, width):
    return jnp.transpose(d[:, :, :width], (1, 0, 2))


def _split_pad_rows(w_t):
    z = lambda n: jnp.zeros((n, w_t.shape[1]), w_t.dtype)
    return jnp.concatenate([w_t[:384], z(64), w_t[384:416], z(32), w_t[416:]], axis=0)


def _split_unpad_rows(w_p):
    return jnp.concatenate([w_p[:384], w_p[448:480], w_p[512:]], axis=0)


def _pad_w_o(w_o):
    mla = jnp.pad(w_o[:512].reshape(HEADS, HEAD_DIM, D_MODEL), ((0, 0), (0, LANES - HEAD_DIM), (0, 0)))
    return mla.reshape(HEADS * LANES, D_MODEL).astype(MXU_DTYPE), w_o[512:].astype(MXU_DTYPE)


def _unpad_w_o(d_mla, d_dil):
    return jnp.concatenate([d_mla.reshape(HEADS, LANES, D_MODEL)[:, :HEAD_DIM].reshape(512, D_MODEL), d_dil], axis=0)


def _row(v):
    return v.reshape(1, -1).astype(F32)


def _compute_weights(w):
    w_o_mla, w_o_dil = _pad_w_o(w["w_o"])
    return dict(
        w_in_t=_split_pad_rows(w["w_in"].T).astype(MXU_DTYPE), wq=_pad_heads(w["w_uq"], NOPE_DIM + ROPE_DIM),
        wk=_pad_heads(w["w_uk"], NOPE_DIM), wv=_pad_heads(w["w_uv"], HEAD_DIM), w_o_mla=w_o_mla, w_o_dil=w_o_dil,
        w_up_t=_ffn_interleave(w["w_up"].T, 0).astype(MXU_DTYPE), w_down=w["w_down"].astype(MXU_DTYPE),
        conv_w=_ffn_interleave(w["conv_w"].astype(F32), 1),
        g_cq=_row(w["g_cq"]), g_ckv=_row(w["g_ckv"]), ln1_g=_row(w["ln1_g"]), ln1_b=_row(w["ln1_b"]),
        conv_b=_ffn_interleave(_row(w["conv_b"]), 1), ln2_g=_row(w["ln2_g"]), ln2_b=_row(w["ln2_b"]))


def _natural_grads(g):
    return dict(
        w_in=_split_unpad_rows(g["w_in_t"]).T, g_cq=g["g_cq"].reshape(-1), g_ckv=g["g_ckv"].reshape(-1),
        w_uq=_unpad_heads(g["wq"], NOPE_DIM + ROPE_DIM), w_uk=_unpad_heads(g["wk"], NOPE_DIM),
        w_uv=_unpad_heads(g["wv"], HEAD_DIM), w_o=_unpad_w_o(g["w_o_mla"], g["w_o_dil"]), ln1_g=g["ln1_g"].reshape(-1),
        ln1_b=g["ln1_b"].reshape(-1), w_up=_ffn_deinterleave(g["w_up_t"], 0).T, conv_w=_ffn_deinterleave(g["conv_w"], 1),
        conv_b=_ffn_deinterleave(g["conv_b"], 1).reshape(-1),
        w_down=g["w_down"], ln2_g=g["ln2_g"].reshape(-1), ln2_b=g["ln2_b"].reshape(-1))


def _layer_grads(x0, target, cw):
    seq = x0.shape[0]
    ctab, stab = _rope_tables(seq)
    gq, gk, cb = cw["g_cq"], cw["g_ckv"], cw["conv_b"]
    wq, wk, wv = cw["wq"], cw["wk"], cw["wv"]
    x0b = x0.astype(MXU_DTYPE)

    h = _mm(x0b, cw["w_in_t"], name="mm_h", tb=True, tm=1024, tn=1024, tk=1024)
    qf, kf, vp = _mla_prep(h, gq, gk, wq, wk, wv, ctab, stab)
    o_mla, o_mla_b, lse_mla = _mla_attn_fwd(qf, kf, vp)
    o_dil, o_dil_b, lse_dil = _dil_fwd(h)
    z1, x1, x1b = _mix_ln1(o_mla_b, o_dil_b, cw["w_o_mla"], cw["w_o_dil"], x0, cw["ln1_g"], cw["ln1_b"])
    u = _mm(x1b, cw["w_up_t"], name="mm_up", tb=True, tm=1024, tn=512, tk=1024)
    act = _conv_gate_fwd(u, cw["conv_w"], cb)
    ffn = _mm(act, cw["w_down"], name="mm_down", tm=1024, tn=1024, tk=2816)
    dz2, dz2b, loss, d_ln2_g, d_ln2_b = _ln2_loss_bwd(x1, ffn, target, cw["ln2_g"], cw["ln2_b"])

    d_act = _mm(dz2b, cw["w_down"], name="mm_d_act", tb=True, out_dtype=MXU_DTYPE, tm=1024, tn=1408, tk=1024)
    d_w_down = _mm(act, dz2b, name="mm_dw_down", ta=True, out_dtype=MXU_DTYPE, tm=1408, tn=1024, tk=1024)
    du, d_conv_w, d_conv_b = _conv_gate_bwd(u, d_act, cw["conv_w"], cb)
    dx1 = _mm(du, cw["w_up_t"], name="mm_dx1", res=dz2, res_scale=DN_ALPHA, tm=1024, tn=1024, tk=1408)
    d_w_up_t = _mm(du, x1b, name="mm_dw_up", ta=True, out_dtype=MXU_DTYPE, tm=1408, tn=1024, tk=1024)
    dz1, dz1b, d_ln1_g, d_ln1_b = _ln_bwd(dx1, z1, cw["ln1_g"], "ln1_bwd")
    do_mla = _mm_do_mla(dz1b, cw["w_o_mla"])
    do_dil = _mm(dz1b, cw["w_o_dil"], name="mm_do_dil", tb=True, tm=1024, tn=512, tk=1024)
    d_w_o_mla = _mm_dw_o_mla(o_mla_b, dz1b)
    d_w_o_dil = _mm(o_dil_b, dz1b, name="mm_dw_o_dil", ta=True, out_dtype=MXU_DTYPE, tm=512, tn=1024, tk=1024)
    dq_dil, dk_dil, dv_dil = _dil_bwd(h, o_dil, lse_dil, do_dil)
    dqf, dkf, dvf = _mla_attn_bwd(qf, kf, vp, o_mla, lse_mla, do_mla)
    dh_mla, d_wq, d_wk, d_wv, d_gq, d_gk = _mla_prep_bwd(h, gq, gk, wq, wk, wv, ctab, stab, dqf, dkf, dvf)
    dh = _assemble_dh(dh_mla, dq_dil, dk_dil, dv_dil)
    grad_x = _mm(dh, cw["w_in_t"], name="mm_dx0", res=dz1, res_scale=DN_ALPHA, tm=1024, tn=1024, tk=2048)
    d_w_in_t = _mm(dh, x0b, name="mm_dw_in", ta=True, out_dtype=MXU_DTYPE, tm=1024, tn=1024, tk=1024)

    grads = dict(
        w_in_t=d_w_in_t, wq=d_wq, wk=d_wk, wv=d_wv, w_o_mla=d_w_o_mla, w_o_dil=d_w_o_dil, w_up_t=d_w_up_t, w_down=d_w_down,
        conv_w=d_conv_w, g_cq=d_gq, g_ckv=d_gk, ln1_g=d_ln1_g, ln1_b=d_ln1_b, conv_b=d_conv_b, ln2_g=d_ln2_g, ln2_b=d_ln2_b)
    return loss[0, 0], grad_x, grads


def _all_gather(blocks, name):
    na = len(blocks)

    def body(*refs):
        ins, outs = refs[:na], refs[na:2 * na]
        send_sems, recv_sems, local_sems = refs[2 * na:]
        x, y, c = lax.axis_index("x"), lax.axis_index("y"), lax.axis_index("c")
        me, sibling = (x, y, c), (x, y, 1 - c)
        chips = [(1 - x, y), (x, 1 - y), (1 - x, 1 - y)]

        def slot(out, pos):
            return out.at[4 * pos[0] + 2 * pos[1] + pos[2]]

        def copy(a, k, block, to, src=None):
            return pltpu.make_async_remote_copy(
                src_ref=slot(outs[a], block) if src is None else src, dst_ref=slot(outs[a], block),
                send_sem=send_sems.at[7 * a + k], recv_sem=recv_sems.at[7 * a + k],
                device_id=to, device_id_type=pl.DeviceIdType.MESH)

        mine = [pltpu.make_async_copy(ins[a], slot(outs[a], me), local_sems.at[a]) for a in range(na)]
        for cp in mine:
            cp.start()
        first = []
        for a in range(na):
            first.append(copy(a, 0, me, sibling, src=ins[a]))
            first += [copy(a, 1 + j, me, (*chip, c), src=ins[a]) for j, chip in enumerate(chips)]
        for cp in first:
            cp.start()
        passed = []
        for j, chip in enumerate(chips):
            for a in range(na):
                copy(a, 1 + j, (*chip, c), me).wait_recv()
                cp = copy(a, 4 + j, (*chip, c), sibling)
                cp.start()
                passed.append(cp)
        for a in range(na):
            copy(a, 0, sibling, me).wait_recv()
            for j, chip in enumerate(chips):
                copy(a, 4 + j, (*chip, 1 - c), me).wait_recv()
        for cp in first + passed:
            cp.wait_send()
        for cp in mine:
            cp.wait()

    any_spec = pl.BlockSpec(memory_space=pl.ANY)
    return pl.pallas_call(
        body, name=name, in_specs=[any_spec] * na, out_specs=[any_spec] * na,
        out_shape=[jax.ShapeDtypeStruct((N_DEV,) + b.shape, b.dtype) for b in blocks],
        scratch_shapes=[pltpu.SemaphoreType.DMA((7 * na,)), pltpu.SemaphoreType.DMA((7 * na,)), pltpu.SemaphoreType.DMA((na,))],
    )(*blocks)


def _exchange(parts, name):
    na = len(parts)

    def body(*refs):
        ins, outs = refs[:na], refs[na:2 * na]
        send_sems, recv_sems, local_sems = refs[2 * na:]
        x, y, c = lax.axis_index("x"), lax.axis_index("y"), lax.axis_index("c")
        me = 4 * x + 2 * y + c

        def peer_of(d):
            return x ^ (d >> 2), y ^ ((d >> 1) & 1), c ^ (d & 1)

        def copy(a, d):
            px, py, pc = peer_of(d)
            return pltpu.make_async_remote_copy(
                src_ref=ins[a].at[4 * px + 2 * py + pc], dst_ref=outs[a].at[me],
                send_sem=send_sems.at[7 * a + d - 1], recv_sem=recv_sems.at[7 * a + d - 1],
                device_id=(px, py, pc), device_id_type=pl.DeviceIdType.MESH)

        def arrival(a, d):
            px, py, pc = peer_of(d)
            return pltpu.make_async_remote_copy(
                src_ref=ins[a].at[me], dst_ref=outs[a].at[4 * px + 2 * py + pc],
                send_sem=send_sems.at[7 * a + d - 1], recv_sem=recv_sems.at[7 * a + d - 1],
                device_id=(px, py, pc), device_id_type=pl.DeviceIdType.MESH)

        mine = [pltpu.make_async_copy(ins[a].at[me], outs[a].at[me], local_sems.at[a]) for a in range(na)]
        for cp in mine:
            cp.start()
        sent = [copy(a, d) for a in range(na) for d in range(1, N_DEV)]
        for cp in sent:
            cp.start()
        for a in range(na):
            for d in range(1, N_DEV):
                arrival(a, d).wait_recv()
        for cp in sent:
            cp.wait_send()
        for cp in mine:
            cp.wait()

    any_spec = pl.BlockSpec(memory_space=pl.ANY)
    return pl.pallas_call(
        body, name=name, in_specs=[any_spec] * na, out_specs=[any_spec] * na,
        out_shape=[jax.ShapeDtypeStruct(p.shape, p.dtype) for p in parts],
        scratch_shapes=[pltpu.SemaphoreType.DMA((7 * na,)), pltpu.SemaphoreType.DMA((7 * na,)), pltpu.SemaphoreType.DMA((na,))],
    )(*parts)


def _sum_parts(parts, name):
    npart, r, n = parts.shape
    tr = r if r <= 256 else max(t for t in range(16, 257, 16) if r % t == 0)

    def body(p_ref, o_ref):
        g = p_ref[0].astype(F32)
        for s in range(1, npart):
            g = g + p_ref[s].astype(F32)
        o_ref[...] = g

    return pl.pallas_call(
        body, name=name, grid=(r // tr,), in_specs=[pl.BlockSpec((npart, tr, n), lambda i: (0, i, 0))],
        out_specs=pl.BlockSpec((tr, n), lambda i: (i, 0)), out_shape=jax.ShapeDtypeStruct((r, n), F32),
        compiler_params=_params("parallel"),
    )(parts)


def _adamw(parts, w, m, v, name):
    npart, r, n = parts.shape
    tr = r if r <= 256 else max(t for t in range(16, 257, 16) if r % t == 0)
    c1 = 1.0 - ADAM_B1 ** ADAM_STEP
    c2 = 1.0 - ADAM_B2 ** ADAM_STEP

    def body(p_ref, w_ref, m_ref, v_ref, g_out, d_out, m_out, v_out):
        g = p_ref[0].astype(F32)
        for s in range(1, npart):
            g = g + p_ref[s].astype(F32)
        m_new = ADAM_B1 * m_ref[...] + (1.0 - ADAM_B1) * g
        v_new = ADAM_B2 * v_ref[...] + (1.0 - ADAM_B2) * (g * g)
        g_out[...] = g
        m_out[...] = m_new
        v_out[...] = v_new
        d_out[...] = -ADAM_LR * ((m_new / c1) / (jnp.sqrt(v_new / c2) + ADAM_EPS) + ADAM_WD * w_ref[...])

    blk = pl.BlockSpec((tr, n), lambda i: (i, 0))
    shp = jax.ShapeDtypeStruct((r, n), F32)
    return pl.pallas_call(
        body, name=name, grid=(r // tr,), in_specs=[pl.BlockSpec((npart, tr, n), lambda i: (0, i, 0)), blk, blk, blk],
        out_specs=[blk] * 4, out_shape=[shp] * 4, compiler_params=_params("parallel"),
    )(parts, w, m, v)


def _pack(arrs, dtype, row_multiple=16):
    flat = jnp.concatenate([a.reshape(-1).astype(dtype) for a in arrs])
    rows = -(-flat.shape[0] // PACK_COLS)
    rows = -(-rows // row_multiple) * row_multiple
    return jnp.pad(flat, (0, rows * PACK_COLS - flat.shape[0])).reshape(rows, PACK_COLS)


def _unpack(buf, shapes):
    lead = buf.shape[:-2]
    flat = buf.reshape(lead + (-1,))
    out, at = [], 0
    for s in shapes:
        size = math.prod(s)
        out.append(flat[..., at:at + size].reshape(lead + tuple(s)))
        at += size
    return out


REPLICATED = ("g_cq", "g_ckv", "w_uk", "w_uv", "ln1_g", "ln1_b", "conv_b", "ln2_g", "ln2_b")
ALL_WEIGHTS = ("w_in", "g_cq", "g_ckv", "w_uq", "w_uk", "w_uv", "w_o", "ln1_g", "ln1_b", "w_up", "conv_w", "conv_b",
               "w_down", "ln2_g", "ln2_b")


def kernel(x, w_in, g_cq, g_ckv, w_uq, w_uk, w_uv, w_o, ln1_g, ln1_b, w_up, conv_w, conv_b, w_down, ln2_g, ln2_b, loss_target, m_w_in, m_g_cq, m_g_ckv, m_w_uq, m_w_uk, m_w_uv, m_w_o, m_ln1_g, m_ln1_b, m_w_up, m_conv_w, m_conv_b, m_w_down, m_ln2_g, m_ln2_b, v_w_in, v_g_cq, v_g_ckv, v_w_uq, v_w_uk, v_w_uv, v_w_o, v_ln1_g, v_ln1_b, v_w_up, v_conv_w, v_conv_b, v_w_down, v_ln2_g, v_ln2_b):
    w = dict(w_in=w_in, g_cq=g_cq, g_ckv=g_ckv, w_uq=w_uq, w_uk=w_uk, w_uv=w_uv, w_o=w_o, ln1_g=ln1_g, ln1_b=ln1_b,
             w_up=w_up, conv_w=conv_w, conv_b=conv_b, w_down=w_down, ln2_g=ln2_g, ln2_b=ln2_b)
    m = dict(w_in=m_w_in, g_cq=m_g_cq, g_ckv=m_g_ckv, w_uq=m_w_uq, w_uk=m_w_uk, w_uv=m_w_uv, w_o=m_w_o, ln1_g=m_ln1_g,
             ln1_b=m_ln1_b, w_up=m_w_up, conv_w=m_conv_w, conv_b=m_conv_b, w_down=m_w_down, ln2_g=m_ln2_g, ln2_b=m_ln2_b)
    v = dict(w_in=v_w_in, g_cq=v_g_cq, g_ckv=v_g_ckv, w_uq=v_w_uq, w_uk=v_w_uk, w_uv=v_w_uv, w_o=v_w_o, ln1_g=v_ln1_g,
             ln1_b=v_ln1_b, w_up=v_w_up, conv_w=v_conv_w, conv_b=v_conv_b, w_down=v_w_down, ln2_g=v_ln2_g, ln2_b=v_ln2_b)
    me = 4 * lax.axis_index("x") + 2 * lax.axis_index("y") + lax.axis_index("c")
    wire = lambda a: a.astype(WIRE_DTYPE)
    n_in = w_in.shape[1]
    n_in_pad = -(-n_in // 16) * 16
    pad_taps = lambda a: jnp.pad(a, ((0, 8 - a.shape[0]), (0, 0)))

    g_in, g_uq, g_o, g_up, g_down, g_conv = _all_gather(
        [jnp.pad(wire(w_in).T, ((0, n_in_pad - n_in), (0, 0))), wire(w_uq).reshape(w_uq.shape[0], -1), wire(w_o),
         wire(w_up).T, wire(w_down), pad_taps(conv_w)], "gather_weights")
    w_o_mla, w_o_dil = _pad_w_o(g_o.reshape(-1, D_MODEL))
    cw = dict(
        w_in_t=_split_pad_rows(g_in[:, :n_in].reshape(-1, D_MODEL)).astype(MXU_DTYPE),
        wq=_pad_heads(g_uq.reshape((-1,) + w_uq.shape[1:]), NOPE_DIM + ROPE_DIM),
        wk=_pad_heads(w_uk, NOPE_DIM), wv=_pad_heads(w_uv, HEAD_DIM), w_o_mla=w_o_mla, w_o_dil=w_o_dil,
        w_up_t=_ffn_interleave(g_up.reshape(-1, D_MODEL), 0).astype(MXU_DTYPE), w_down=g_down.reshape(-1, D_MODEL).astype(MXU_DTYPE),
        conv_w=_ffn_interleave(jnp.transpose(g_conv[:, :conv_w.shape[0]], (1, 0, 2)).reshape(conv_w.shape[0], -1), 1),
        g_cq=_row(g_cq), g_ckv=_row(g_ckv), ln1_g=_row(ln1_g), ln1_b=_row(ln1_b), conv_b=_ffn_interleave(_row(conv_b), 1),
        ln2_g=_row(ln2_g), ln2_b=_row(ln2_b))

    loss, grad_x, g = _layer_grads(x[0], loss_target[0], cw)
    loss = lax.psum(loss, MESH_AXES)

    blocks = lambda a: wire(a).reshape((N_DEV, a.shape[0] // N_DEV) + a.shape[1:])
    d_in = jnp.pad(blocks(_split_unpad_rows(g["w_in_t"])), ((0, 0), (0, n_in_pad - n_in), (0, 0)))
    d_uq = blocks(_unpad_heads(g["wq"], NOPE_DIM + ROPE_DIM).reshape(Q_RANK, -1))
    r_in, r_uq, r_o, r_up, r_down = _exchange(
        [d_in, d_uq, blocks(_unpad_w_o(g["w_o_mla"], g["w_o_dil"])), blocks(_ffn_deinterleave(g["w_up_t"], 0)), blocks(g["w_down"])],
        "exchange_grads")
    nat = dict(g_cq=g["g_cq"], g_ckv=g["g_ckv"], w_uk=_unpad_heads(g["wk"], NOPE_DIM), w_uv=_unpad_heads(g["wv"], HEAD_DIM),
               ln1_g=g["ln1_g"], ln1_b=g["ln1_b"], conv_b=_ffn_deinterleave(g["conv_b"], 1), ln2_g=g["ln2_g"], ln2_b=g["ln2_b"])
    rep_all, cw_all = _all_gather([_pack([nat[n] for n in REPLICATED], F32, 8), pad_taps(_ffn_deinterleave(g["conv_w"], 1))],
                                  "gather_small_grads")

    out = {}

    def update(name, parts, shape2d=None):
        w2, m2, v2 = [d[name].reshape(shape2d or d[name].shape) for d in (w, m, v)]
        res = _adamw(parts, w2, m2, v2, "adamw_" + name)
        for kind, a in zip(("grad", "delta", "new_m", "new_v"), res):
            out[kind, name] = a.reshape(w[name].shape)

    update("w_in", _sum_parts(r_in, "sum_w_in")[:n_in].T[None])
    update("w_uq", r_uq, (w_uq.shape[0], -1))
    update("w_o", r_o)
    update("w_up", _sum_parts(r_up, "sum_w_up").T[None])
    update("w_down", r_down)
    res = _adamw(rep_all, *[_pack([d[n] for n in REPLICATED], F32, 8) for d in (w, m, v)], "adamw_replicated")
    for kind, buf in zip(("grad", "delta", "new_m", "new_v"), res):
        for n, a in zip(REPLICATED, _unpack(buf, [w[n].shape for n in REPLICATED])):
            out[kind, n] = a
    ncw = conv_w.shape[1]
    update("conv_w", lax.dynamic_slice_in_dim(cw_all[:, :conv_w.shape[0]], me * ncw, ncw, axis=2))

    return (loss, grad_x[None], *[out[kind, n] for kind in ("grad", "delta", "new_m", "new_v") for n in ALL_WEIGHTS])
```

```python
import functools
import math

import jax
import jax.numpy as jnp
from jax import lax
from jax.experimental import pallas as pl
from jax.experimental.pallas import tpu as pltpu

F32 = jnp.float32
MXU_DTYPE = jnp.bfloat16
WIRE_DTYPE = jnp.bfloat16

N_DEV = 8
D_MODEL = 1024
HEADS = 8
HEAD_DIM = 64
LANES = 128
Q_RANK, KV_RANK, ROPE_DIM, NOPE_DIM = 256, 128, 32, 64
DIL_WIDTH = HEADS * HEAD_DIM
IN_WIDTH = 1952
IN_PAD = 2048
D_FF = 2816
ROPE_THETA = 10000.0
DIL_PAIRS = ((128, 1), (512, 4), (2048, 16))
DIL_BLOCK = 128
DN_ALPHA = 2.0 ** 0.25
LN_EPS = 1e-5
RMS_EPS = 1e-6
MLA_SCALE = 1.0 / math.sqrt(NOPE_DIM + ROPE_DIM)
MLA_SCALE_LOG2 = MLA_SCALE * math.log2(math.e)
DIL_SCALE = 1.0 / math.sqrt(HEAD_DIM)
ALIBI_SLOPES = tuple(2.0 ** (-8.0 * (h + 1) / HEADS) for h in range(HEADS))
NEG_BIG = -1e30
ADAM_LR, ADAM_B1, ADAM_B2, ADAM_EPS, ADAM_WD, ADAM_STEP = 0.001, 0.9, 0.999, 1e-08, 0.01, 10
VMEM_LIMIT = 48 * 1024 * 1024
PACK_COLS = 1024

MESH_AXES = ("x", "y", "c")


def _params(*sem):
    return pltpu.CompilerParams(dimension_semantics=sem or None, vmem_limit_bytes=VMEM_LIMIT)


def _dot(a, b, ca, cb):
    return lax.dot_general(a, b, (((ca,), (cb,)), ((), ())), preferred_element_type=F32)


def _mm(a, b, *, name, tm, tn, tk, ta=False, tb=False, out_dtype=F32, res=None, res_scale=1.0):
    m, k = (a.shape[1], a.shape[0]) if ta else a.shape
    n = b.shape[0] if tb else b.shape[1]
    assert (b.shape[1] if tb else b.shape[0]) == k
    tm, tn, tk = min(tm, m), min(tn, n), min(tk, k)
    assert m % tm == 0 and n % tn == 0 and k % tk == 0, (name, m, n, k, tm, tn, tk)
    nk = k // tk
    a_spec = (pl.BlockSpec((tk, tm), lambda i, j, kk: (kk, i)) if ta
              else pl.BlockSpec((tm, tk), lambda i, j, kk: (i, kk)))
    b_spec = (pl.BlockSpec((tn, tk), lambda i, j, kk: (j, kk)) if tb
              else pl.BlockSpec((tk, tn), lambda i, j, kk: (kk, j)))
    o_spec = pl.BlockSpec((tm, tn), lambda i, j, kk: (i, j))
    in_specs = [a_spec, b_spec]
    args = [a, b]
    if res is not None:
        in_specs.append(o_spec)
        args.append(res)
    ca, cb = (0 if ta else 1), (1 if tb else 0)

    def finish(acc, r_ref, o_ref):
        if r_ref is not None:
            acc = acc + res_scale * r_ref[...]
        o_ref[...] = acc.astype(o_ref.dtype)

    def body(*refs):
        a_ref, b_ref = refs[:2]
        r_ref = refs[2] if res is not None else None
        o_ref = refs[3] if res is not None else refs[2]
        part = _dot(a_ref[...].astype(MXU_DTYPE), b_ref[...].astype(MXU_DTYPE), ca, cb)
        if nk == 1:
            finish(part, r_ref, o_ref)
            return
        acc_ref = refs[-1]
        kk = pl.program_id(2)

        @pl.when(kk == 0)
        def _():
            acc_ref[...] = part

        @pl.when(kk > 0)
        def _():
            acc_ref[...] += part

        @pl.when(kk == nk - 1)
        def _():
            finish(acc_ref[...], r_ref, o_ref)

    return pl.pallas_call(
        body, name=name, grid=(m // tm, n // tn, nk), in_specs=in_specs, out_specs=o_spec,
        out_shape=jax.ShapeDtypeStruct((m, n), out_dtype),
        scratch_shapes=[pltpu.VMEM((tm, tn), F32)] if nk > 1 else [],
        compiler_params=_params("parallel", "parallel", "arbitrary"),
    )(*args)


def _mm_do_mla(dz, w_o_mla, tm=1024):
    seq, d = dz.shape
    tm = min(tm, seq)

    def body(a_ref, b_ref, o_ref):
        a = a_ref[...].astype(MXU_DTYPE)
        for hd in range(HEADS):
            o_ref[hd] = _dot(a, b_ref[LANES * hd:LANES * (hd + 1), :], 1, 1)

    return pl.pallas_call(
        body, name="mm_do_mla", grid=(seq // tm,),
        in_specs=[pl.BlockSpec((tm, d), lambda i: (i, 0)), pl.BlockSpec((HEADS * LANES, d), lambda i: (0, 0))],
        out_specs=pl.BlockSpec((HEADS, tm, LANES), lambda i: (0, i, 0)),
        out_shape=jax.ShapeDtypeStruct((HEADS, seq, LANES), F32), compiler_params=_params("parallel"),
    )(dz, w_o_mla)


def _mm_dw_o_mla(o_mla, dz, tk=1024):
    seq, d = dz.shape
    tk = min(tk, seq)
    nk = seq // tk

    def body(a_ref, b_ref, o_ref, acc_ref):
        kk = pl.program_id(0)

        @pl.when(kk == 0)
        def _():
            acc_ref[...] = jnp.zeros_like(acc_ref)

        b = b_ref[...].astype(MXU_DTYPE)
        for hd in range(HEADS):
            acc_ref[LANES * hd:LANES * (hd + 1), :] += _dot(a_ref[hd].astype(MXU_DTYPE), b, 0, 0)

        @pl.when(kk == nk - 1)
        def _():
            o_ref[...] = acc_ref[...].astype(o_ref.dtype)

    return pl.pallas_call(
        body, name="mm_dw_o_mla", grid=(nk,),
        in_specs=[pl.BlockSpec((HEADS, tk, LANES), lambda kk: (0, kk, 0)), pl.BlockSpec((tk, d), lambda kk: (kk, 0))],
        out_specs=pl.BlockSpec((HEADS * LANES, d), lambda kk: (0, 0)),
        out_shape=jax.ShapeDtypeStruct((HEADS * LANES, d), MXU_DTYPE),
        scratch_shapes=[pltpu.VMEM((HEADS * LANES, d), F32)], compiler_params=_params("arbitrary"),
    )(o_mla, dz)


def _rope_tables(seq):
    half = ROPE_DIM // 2
    freqs = ROPE_THETA ** (-jnp.arange(half, dtype=F32) / half)
    ang = jnp.arange(seq).astype(F32)[:, None] * freqs[None, :]
    cos, sin = jnp.cos(ang), jnp.sin(ang)
    one = jnp.ones((seq, NOPE_DIM), F32)
    tail = jnp.ones((seq, LANES - NOPE_DIM - ROPE_DIM), F32)
    ctab = jnp.concatenate([one, cos, cos, tail], axis=1)
    stab = jnp.concatenate([0 * one, -sin, sin, 0 * tail], axis=1)
    return ctab, stab


def _rope_swap(t):
    lane = lax.broadcasted_iota(jnp.int32, t.shape, 1)
    half = ROPE_DIM // 2
    return jnp.where(lane < NOPE_DIM + half, pltpu.roll(t, LANES - half, 1), pltpu.roll(t, half, 1))


def _rope(t, ctab, stab):
    return t * ctab + _rope_swap(t) * stab


def _rope_inv(t, ctab, stab):
    return t * ctab - _rope_swap(t) * stab


def _rms(x, g):
    r = lax.rsqrt(jnp.mean(x * x, axis=-1, keepdims=True) + RMS_EPS)
    xh = x * r
    return xh, r, xh * g


def _mla_prep(h, g_cq, g_ckv, wq, wk, wv, ctab, stab, tm=512):
    seq = h.shape[0]
    tm = min(tm, seq)

    def body(h_ref, gq_ref, gk_ref, wq_ref, wk_ref, wv_ref, c_ref, s_ref, q_out, k_out, v_out):
        hb = h_ref[...]
        ctab_, stab_ = c_ref[...], s_ref[...]
        _, _, cqn = _rms(hb[:, :Q_RANK], gq_ref[...])
        _, _, ckn = _rms(hb[:, Q_RANK:Q_RANK + KV_RANK], gk_ref[...])
        cqn = cqn.astype(MXU_DTYPE)
        ckn = ckn.astype(MXU_DTYPE)
        krr = _rope(hb[:, Q_RANK + KV_RANK:], ctab_, stab_)
        for hd in range(HEADS):
            q = _dot(cqn, wq_ref[hd], 1, 0)
            q_out[hd] = _rope(q, ctab_, stab_).astype(q_out.dtype)
            k_out[hd] = (_dot(ckn, wk_ref[hd], 1, 0) + krr).astype(k_out.dtype)
            v_out[hd] = _dot(ckn, wv_ref[hd], 1, 0).astype(v_out.dtype)

    full = lambda *shape: pl.BlockSpec(shape, lambda i: (0,) * len(shape))
    slab = pl.BlockSpec((HEADS, tm, LANES), lambda i: (0, i, 0))
    shp = jax.ShapeDtypeStruct((HEADS, seq, LANES), MXU_DTYPE)
    return pl.pallas_call(
        body, name="mla_prep", grid=(seq // tm,),
        in_specs=[pl.BlockSpec((tm, 512), lambda i: (i, 0)), full(1, Q_RANK), full(1, KV_RANK),
                  full(HEADS, Q_RANK, LANES), full(HEADS, KV_RANK, LANES), full(HEADS, KV_RANK, LANES),
                  pl.BlockSpec((tm, LANES), lambda i: (i, 0)), pl.BlockSpec((tm, LANES), lambda i: (i, 0))],
        out_specs=[slab, slab, slab], out_shape=[shp, shp, shp],
        compiler_params=_params("parallel"),
    )(h, g_cq, g_ckv, wq, wk, wv, ctab, stab)


def _mla_prep_bwd(h, g_cq, g_ckv, wq, wk, wv, ctab, stab, dq, dk, dv, tm=512):
    seq = h.shape[0]
    tm = min(tm, seq)

    def body(h_ref, gq_ref, gk_ref, wq_ref, wk_ref, wv_ref, c_ref, s_ref, dq_ref, dk_ref, dv_ref,
             dh_ref, dwq_ref, dwk_ref, dwv_ref, dgq_ref, dgk_ref):
        @pl.when(pl.program_id(0) == 0)
        def _():
            for r in (dwq_ref, dwk_ref, dwv_ref, dgq_ref, dgk_ref):
                r[...] = jnp.zeros_like(r)

        hb = h_ref[...]
        ctab_, stab_ = c_ref[...], s_ref[...]
        gq, gk = gq_ref[...], gk_ref[...]
        xq, rq, cqn = _rms(hb[:, :Q_RANK], gq)
        xk, rk, ckn = _rms(hb[:, Q_RANK:Q_RANK + KV_RANK], gk)
        cqn = cqn.astype(MXU_DTYPE)
        ckn = ckn.astype(MXU_DTYPE)
        d_cqn = jnp.zeros((tm, Q_RANK), F32)
        d_ckn = jnp.zeros((tm, KV_RANK), F32)
        d_krr = jnp.zeros((tm, LANES), F32)
        for hd in range(HEADS):
            dqh = _rope_inv(dq_ref[hd], ctab_, stab_).astype(MXU_DTYPE)
            d_cqn += _dot(dqh, wq_ref[hd], 1, 1)
            dwq_ref[hd] += _dot(cqn, dqh, 0, 0)
            dkh = dk_ref[hd]
            d_krr += dkh
            dkh = dkh.astype(MXU_DTYPE)
            d_ckn += _dot(dkh, wk_ref[hd], 1, 1)
            dwk_ref[hd] += _dot(ckn, dkh, 0, 0)
            dvh = dv_ref[hd].astype(MXU_DTYPE)
            d_ckn += _dot(dvh, wv_ref[hd], 1, 1)
            dwv_ref[hd] += _dot(ckn, dvh, 0, 0)
        lane = lax.broadcasted_iota(jnp.int32, (tm, LANES), 1)
        rot = (lane >= NOPE_DIM) & (lane < NOPE_DIM + ROPE_DIM)
        d_kr = jnp.where(rot, _rope_inv(jnp.where(rot, d_krr, 0.0), ctab_, stab_), 0.0)

        def rms_bwd(dy, xh, r, g, dg_ref):
            dg_ref[...] += jnp.sum(dy * xh, axis=0, keepdims=True)
            dxh = dy * g
            return r * (dxh - xh * jnp.mean(dxh * xh, axis=-1, keepdims=True))

        d_cq = rms_bwd(d_cqn, xq, rq, gq, dgq_ref)
        d_ck = rms_bwd(d_ckn, xk, rk, gk, dgk_ref)
        dh_ref[...] = jnp.concatenate([d_cq, d_ck, d_kr], axis=1).astype(dh_ref.dtype)

    full = lambda *shape: pl.BlockSpec(shape, lambda i: (0,) * len(shape))
    slab = pl.BlockSpec((HEADS, tm, LANES), lambda i: (0, i, 0))
    return pl.pallas_call(
        body, name="mla_prep_bwd", grid=(seq // tm,),
        in_specs=[pl.BlockSpec((tm, 512), lambda i: (i, 0)), full(1, Q_RANK), full(1, KV_RANK),
                  full(HEADS, Q_RANK, LANES), full(HEADS, KV_RANK, LANES), full(HEADS, KV_RANK, LANES),
                  pl.BlockSpec((tm, LANES), lambda i: (i, 0)), pl.BlockSpec((tm, LANES), lambda i: (i, 0)),
                  slab, slab, slab],
        out_specs=[pl.BlockSpec((tm, 512), lambda i: (i, 0)), full(HEADS, Q_RANK, LANES), full(HEADS, KV_RANK, LANES),
                   full(HEADS, KV_RANK, LANES), full(1, Q_RANK), full(1, KV_RANK)],
        out_shape=[jax.ShapeDtypeStruct((seq, 512), MXU_DTYPE), jax.ShapeDtypeStruct((HEADS, Q_RANK, LANES), F32),
                   jax.ShapeDtypeStruct((HEADS, KV_RANK, LANES), F32), jax.ShapeDtypeStruct((HEADS, KV_RANK, LANES), F32),
                   jax.ShapeDtypeStruct((1, Q_RANK), F32), jax.ShapeDtypeStruct((1, KV_RANK), F32)],
        compiler_params=_params("arbitrary"),
    )(h, g_cq, g_ckv, wq, wk, wv, ctab, stab, dq, dk, dv)


def _causal_mask(t):
    row = lax.broadcasted_iota(jnp.int32, (t, t), 0)
    col = lax.broadcasted_iota(jnp.int32, (t, t), 1)
    return row >= col


def _mla_attn_fwd(q, k, v, t=512):
    _, seq, _ = q.shape
    t = min(t, seq)

    def body(q_ref, k_ref, v_ref, o_ref, ob_ref, lse_ref, m_ref, l_ref, acc_ref, s_ref):
        i = pl.program_id(1)
        qb = q_ref[...]
        m_ref[...] = jnp.full_like(m_ref, NEG_BIG)
        l_ref[...] = jnp.zeros_like(l_ref)
        acc_ref[...] = jnp.zeros_like(acc_ref)

        def scores(j):
            return _dot(qb, k_ref[pl.ds(pl.multiple_of(j * t, t), t), :], 1, 1) * MLA_SCALE_LOG2

        def softmax_pv(j, s, masked):
            vb = v_ref[pl.ds(pl.multiple_of(j * t, t), t), :]
            if masked:
                s = jnp.where(_causal_mask(t), s, NEG_BIG)
            m_old = m_ref[...]
            m_new = jnp.maximum(m_old, jnp.max(s, axis=1, keepdims=True))
            p = jnp.exp2(s - m_new)
            a = jnp.exp2(m_old - m_new)
            l_ref[...] = a * l_ref[...] + jnp.sum(p, axis=1, keepdims=True)
            acc_ref[...] = a * acc_ref[...] + _dot(p.astype(MXU_DTYPE), vb, 1, 0)
            m_ref[...] = m_new

        s_ref[...] = scores(0)

        def loop_body(j, c):
            s_next = scores(j + 1)
            softmax_pv(j, s_ref[...], False)
            s_ref[...] = s_next
            return c

        lax.fori_loop(0, i, loop_body, 0)
        softmax_pv(i, s_ref[...], True)
        l = l_ref[...]
        o = acc_ref[...] * (1.0 / l)
        o_ref[...] = o
        ob_ref[...] = o.astype(ob_ref.dtype)
        lse_ref[...] = jnp.broadcast_to(m_ref[...] + jnp.log2(l), lse_ref.shape)

    blk = pl.BlockSpec((None, t, LANES), lambda h, i: (h, i, 0))
    whole = pl.BlockSpec((None, seq, LANES), lambda h, i: (h, 0, 0))
    shp = jax.ShapeDtypeStruct((HEADS, seq, LANES), F32)
    return pl.pallas_call(
        body, name="mla_attn_fwd", grid=(HEADS, seq // t),
        in_specs=[blk, whole, whole], out_specs=[blk, blk, blk],
        out_shape=[shp, jax.ShapeDtypeStruct((HEADS, seq, LANES), MXU_DTYPE), shp],
        scratch_shapes=[pltpu.VMEM((t, 1), F32), pltpu.VMEM((t, 1), F32), pltpu.VMEM((t, LANES), F32), pltpu.VMEM((t, t), F32)],
        compiler_params=_params("parallel", "arbitrary"),
    )(q, k, v)


def _mla_attn_bwd(q, k, v, o, lse, do, t=512):
    _, seq, _ = q.shape
    t = min(t, seq)
    nb = seq // t

    def body(q_ref, k_ref, v_ref, o_ref, lse_ref, do_ref, dq_ref, dk_ref, dv_ref, dl_ref, dka_ref, dva_ref):
        dq_ref[...] = jnp.zeros_like(dq_ref)

        def delta_body(i, c):
            rows = pl.ds(pl.multiple_of(i * t, t), t)
            dl_ref[rows, :] = jnp.sum(do_ref[rows, :] * o_ref[rows, :], axis=1, keepdims=True)
            return c

        lax.fori_loop(0, nb, delta_body, 0)

        def kblock(j, c):
            krows = pl.ds(pl.multiple_of(j * t, t), t)
            kb = k_ref[krows, :]
            vb = v_ref[krows, :]
            dka_ref[...] = jnp.zeros_like(dka_ref)
            dva_ref[...] = jnp.zeros_like(dva_ref)

            def qstep(i, masked):
                rows = pl.ds(pl.multiple_of(i * t, t), t)
                qb = q_ref[rows, :]
                dob = do_ref[rows, :].astype(MXU_DTYPE)
                s = _dot(qb, kb, 1, 1) * MLA_SCALE_LOG2
                if masked:
                    s = jnp.where(_causal_mask(t), s, NEG_BIG)
                p = jnp.exp2(s - lse_ref[rows, 0:1])
                dva_ref[...] += _dot(p.astype(MXU_DTYPE), dob, 0, 0)
                dp = _dot(dob, vb, 1, 1)
                ds = (p * (dp - dl_ref[rows, :]) * MLA_SCALE).astype(MXU_DTYPE)
                dka_ref[...] += _dot(ds, qb, 0, 0)
                dq_ref[rows, :] += _dot(ds, kb, 1, 0)

            qstep(j, True)

            def qloop(i, c2):
                qstep(i, False)
                return c2

            lax.fori_loop(j + 1, nb, qloop, 0)
            dk_ref[krows, :] = dka_ref[...]
            dv_ref[krows, :] = dva_ref[...]
            return c

        lax.fori_loop(0, nb, kblock, 0)

    whole = pl.BlockSpec((None, seq, LANES), lambda h: (h, 0, 0))
    shp = jax.ShapeDtypeStruct((HEADS, seq, LANES), F32)
    return pl.pallas_call(
        body, name="mla_attn_bwd", grid=(HEADS,),
        in_specs=[whole] * 6, out_specs=[whole] * 3, out_shape=[shp] * 3,
        scratch_shapes=[pltpu.VMEM((seq, 1), F32), pltpu.VMEM((t, LANES), F32), pltpu.VMEM((t, LANES), F32)],
        compiler_params=_params("parallel"),
    )(q, k, v, o, lse, do)


DIL_CHUNK = DIL_BLOCK * max(d for _, d in DIL_PAIRS)
DIL_PAIR_LANES = 2 * HEAD_DIM
assert DIL_PAIR_LANES == LANES
DIL_UNROLL_FWD = 4
DIL_UNROLL_BWD = 4


def _dil_bias_tables(hp, dil):
    b = DIL_BLOCK
    iq = lax.broadcasted_iota(jnp.int32, (b, 2 * b), 0)
    ik = lax.broadcasted_iota(jnp.int32, (b, 2 * b), 1)
    off = iq + b - ik
    band = (off >= 0) & (off <= b)
    dist = (off * dil).astype(F32)
    tables = []
    for hh in range(2):
        slope = jnp.where(hp == 0, ALIBI_SLOPES[hh], jnp.where(hp == 1, ALIBI_SLOPES[2 + hh],
                          jnp.where(hp == 2, ALIBI_SLOPES[4 + hh], ALIBI_SLOPES[6 + hh]))).astype(F32)
        bias = -slope * dist
        tables.append((jnp.where(band, bias, NEG_BIG), jnp.where(band & (ik >= b), bias, NEG_BIG)))
    return tables


def _dil_rows(start, dil):
    return pl.ds(start, DIL_BLOCK) if dil == 1 else pl.ds(start, DIL_BLOCK, stride=dil)


def _dil_block_pos(blk, c, dil):
    sc, r = blk // dil, blk % dil
    q0 = sc * (DIL_BLOCK * dil) + r
    kcur0 = c * DIL_CHUNK + q0
    first = kcur0 < DIL_BLOCK * dil
    kprev0 = jnp.where(first, kcur0, kcur0 - DIL_BLOCK * dil)
    return q0, kcur0, kprev0, first


def _pair_cols(hh):
    return slice(HEAD_DIM * hh, HEAD_DIM * (hh + 1))


def _first_head_lanes(shape):
    return lax.broadcasted_iota(jnp.int32, shape, 1) < HEAD_DIM


def _split_pair(t):
    first = _first_head_lanes(t.shape)
    return jnp.where(first, t, 0.0).astype(MXU_DTYPE), jnp.where(first, 0.0, t).astype(MXU_DTYPE)


def _join_pair(ts):
    return jnp.where(_first_head_lanes(ts[0].shape), ts[0], ts[1])


def _dil_fwd(h):
    seq = h.shape[0]
    assert seq % DIL_CHUNK == 0
    nblk = DIL_CHUNK // DIL_BLOCK
    rc = 256

    def body(q_ref, k_ref, v_ref, o_ref, ob_ref, lse_ref, *scr):
        o_scr, l_scr = scr[:3], scr[3:]
        hp, c = pl.program_id(0), pl.program_id(1)
        for bi, (_, dil) in enumerate(DIL_PAIRS):
            tables = _dil_bias_tables(hp, dil)

            def block(blk, carry, bi=bi, dil=dil, tables=tables):
                q0, kcur0, kprev0, first = _dil_block_pos(blk, c, dil)
                qs = _split_pair(q_ref[_dil_rows(q0, dil), :] * DIL_SCALE)
                kcat = jnp.concatenate([k_ref[_dil_rows(kprev0, dil), :], k_ref[_dil_rows(kcur0, dil), :]], axis=0).astype(MXU_DTYPE)
                vcat = jnp.concatenate([v_ref[_dil_rows(kprev0, dil), :], v_ref[_dil_rows(kcur0, dil), :]], axis=0).astype(MXU_DTYPE)
                outs, lses = [], []
                for hh in range(2):
                    s = _dot(qs[hh], kcat, 1, 1) + jnp.where(first, tables[hh][1], tables[hh][0])
                    mx = jnp.max(s, axis=1, keepdims=True)
                    p = jnp.exp(s - mx)
                    l = jnp.sum(p, axis=1, keepdims=True)
                    outs.append(_dot(p.astype(MXU_DTYPE), vcat, 1, 0) * (1.0 / l))
                    lses.append(jnp.broadcast_to(mx + jnp.log(l), (DIL_BLOCK, LANES)))
                o_scr[bi][_dil_rows(q0, dil), :] = _join_pair(outs)
                l_scr[bi][_dil_rows(q0, dil), :] = _join_pair(lses)
                return carry

            lax.fori_loop(0, nblk, block, 0, unroll=DIL_UNROLL_FWD)

        def combine(i, carry):
            rows = pl.ds(pl.multiple_of(i * rc, rc), rc)
            ls = [l_scr[bi][rows, :] for bi in range(3)]
            mx = jnp.maximum(jnp.maximum(ls[0], ls[1]), ls[2])
            es = [jnp.exp(l - mx) for l in ls]
            den = es[0] + es[1] + es[2]
            o = (es[0] * o_scr[0][rows, :] + es[1] * o_scr[1][rows, :] + es[2] * o_scr[2][rows, :]) / den
            o_ref[rows, :] = o
            ob_ref[rows, :] = o.astype(ob_ref.dtype)
            lse_ref[rows, :] = mx + jnp.log(den)
            return carry

        lax.fori_loop(0, DIL_CHUNK // rc, combine, 0)

    nq = DIL_WIDTH // LANES
    chunk = lambda off: pl.BlockSpec((DIL_CHUNK, LANES), lambda hp, c: (c, off + hp))
    whole = lambda off: pl.BlockSpec((seq, LANES), lambda hp, c: (0, off + hp))
    shp = jax.ShapeDtypeStruct((seq, DIL_WIDTH), F32)
    return pl.pallas_call(
        body, name="dil_fwd", grid=(nq, seq // DIL_CHUNK),
        in_specs=[chunk(nq), whole(2 * nq), whole(3 * nq)], out_specs=[chunk(0), chunk(0), chunk(0)],
        out_shape=[shp, jax.ShapeDtypeStruct((seq, DIL_WIDTH), MXU_DTYPE), shp],
        scratch_shapes=[pltpu.VMEM((DIL_CHUNK, LANES), F32)] * 6,
        compiler_params=_params("parallel", "arbitrary"),
    )(h, h, h)


def _dil_bwd(h, o, lse, do):
    seq = h.shape[0]
    nblk = DIL_CHUNK // DIL_BLOCK
    rc = 256

    def body(q_ref, k_ref, v_ref, o_ref, lse_ref, do_ref, dq_ref, dk_ref, dv_ref, dl_scr):
        hp, c = pl.program_id(0), pl.program_id(1)

        @pl.when(c == 0)
        def _():
            dk_ref[...] = jnp.zeros_like(dk_ref)
            dv_ref[...] = jnp.zeros_like(dv_ref)

        def delta(i, carry):
            rows = pl.ds(pl.multiple_of(i * rc, rc), rc)
            prod = do_ref[rows, :] * o_ref[rows, :]
            dl_scr[rows, :] = jnp.concatenate(
                [jnp.broadcast_to(jnp.sum(prod[:, _pair_cols(hh)], axis=1, keepdims=True), (rc, HEAD_DIM)) for hh in range(2)], axis=1)
            return carry

        lax.fori_loop(0, DIL_CHUNK // rc, delta, 0)

        for bi, (_, dil) in enumerate(DIL_PAIRS):
            tables = _dil_bias_tables(hp, dil)

            def block(blk, carry, bi=bi, dil=dil, tables=tables):
                q0, kcur0, kprev0, first = _dil_block_pos(blk, c, dil)
                qrows = _dil_rows(q0, dil)
                qs = _split_pair(q_ref[qrows, :] * DIL_SCALE)
                kcat = jnp.concatenate([k_ref[_dil_rows(kprev0, dil), :], k_ref[_dil_rows(kcur0, dil), :]], axis=0).astype(MXU_DTYPE)
                vcat = jnp.concatenate([v_ref[_dil_rows(kprev0, dil), :], v_ref[_dil_rows(kcur0, dil), :]], axis=0).astype(MXU_DTYPE)
                dos = _split_pair(do_ref[qrows, :])
                lse_b = lse_ref[qrows, :]
                dl_b = dl_scr[qrows, :]
                dqs = []
                dk_b = dv_b = None
                for hh in range(2):
                    lane0 = HEAD_DIM * hh
                    s = _dot(qs[hh], kcat, 1, 1) + jnp.where(first, tables[hh][1], tables[hh][0])
                    p = jnp.exp(s - lse_b[:, lane0:lane0 + 1])
                    dp = _dot(dos[hh], vcat, 1, 1)
                    ds = (p * (dp - dl_b[:, lane0:lane0 + 1])).astype(MXU_DTYPE)
                    dqs.append(_dot(ds, kcat, 1, 0))
                    dk_h = _dot(ds, qs[hh], 0, 0)
                    dv_h = _dot(p.astype(MXU_DTYPE), dos[hh], 0, 0)
                    dk_b = dk_h if dk_b is None else dk_b + dk_h
                    dv_b = dv_h if dv_b is None else dv_b + dv_h
                dq_b = _join_pair(dqs) * DIL_SCALE
                if bi == 0:
                    dq_ref[qrows, :] = dq_b
                else:
                    dq_ref[qrows, :] += dq_b
                dk_ref[_dil_rows(kprev0, dil), :] += dk_b[:DIL_BLOCK]
                dv_ref[_dil_rows(kprev0, dil), :] += dv_b[:DIL_BLOCK]
                dk_ref[_dil_rows(kcur0, dil), :] += dk_b[DIL_BLOCK:]
                dv_ref[_dil_rows(kcur0, dil), :] += dv_b[DIL_BLOCK:]
                return carry

            lax.fori_loop(0, nblk, block, 0, unroll=DIL_UNROLL_BWD)

    nq = DIL_WIDTH // LANES
    chunk = lambda off: pl.BlockSpec((DIL_CHUNK, LANES), lambda hp, c: (c, off + hp))
    whole = lambda off: pl.BlockSpec((seq, LANES), lambda hp, c: (0, off + hp))
    shp = jax.ShapeDtypeStruct((seq, DIL_WIDTH), F32)
    return pl.pallas_call(
        body, name="dil_bwd", grid=(nq, seq // DIL_CHUNK),
        in_specs=[chunk(nq), whole(2 * nq), whole(3 * nq), chunk(0), chunk(0), chunk(0)],
        out_specs=[chunk(0), whole(0), whole(0)], out_shape=[shp, shp, shp],
        scratch_shapes=[pltpu.VMEM((DIL_CHUNK, LANES), F32)],
        compiler_params=_params("parallel", "arbitrary"),
    )(h, h, h, o, lse, do)


def _assemble_dh(dh_mla, dq, dk, dv, tm=512):
    seq = dh_mla.shape[0]
    tm = min(tm, seq)

    def body(a_ref, q_ref, k_ref, v_ref, o_ref):
        for j, r in enumerate((a_ref, q_ref, k_ref, v_ref)):
            o_ref[:, 512 * j:512 * (j + 1)] = r[...].astype(o_ref.dtype)

    blk = pl.BlockSpec((tm, 512), lambda i: (i, 0))
    return pl.pallas_call(
        body, name="assemble_dh", grid=(seq // tm,), in_specs=[blk] * 4,
        out_specs=pl.BlockSpec((tm, IN_PAD), lambda i: (i, 0)), out_shape=jax.ShapeDtypeStruct((seq, IN_PAD), MXU_DTYPE),
        compiler_params=_params("parallel"),
    )(dh_mla, dq, dk, dv)


def _ln_stats(z):
    mu = jnp.mean(z, axis=-1, keepdims=True)
    zc = z - mu
    r = lax.rsqrt(jnp.mean(zc * zc, axis=-1, keepdims=True) + LN_EPS)
    return zc * r, r


def _ln_bwd_math(dy, xh, r, g):
    dxh = dy * g
    return r * (dxh - jnp.mean(dxh, axis=-1, keepdims=True) - xh * jnp.mean(dxh * xh, axis=-1, keepdims=True))


def _mix_ln1(o_mla, o_dil, w_o_mla, w_o_dil, x0, g, b, tm=512):
    seq, d = x0.shape
    tm = min(tm, seq)

    def body(om_ref, od_ref, wm_ref, wd_ref, x_ref, g_ref, b_ref, z_ref, y_ref, yb_ref):
        mix = _dot(od_ref[...], wd_ref[...], 1, 0)
        for hd in range(HEADS):
            mix += _dot(om_ref[hd], wm_ref[LANES * hd:LANES * (hd + 1), :], 1, 0)
        z = DN_ALPHA * x_ref[...] + mix
        xh, _ = _ln_stats(z)
        y = xh * g_ref[...] + b_ref[...]
        z_ref[...] = z
        y_ref[...] = y
        yb_ref[...] = y.astype(yb_ref.dtype)

    blk = pl.BlockSpec((tm, d), lambda i: (i, 0))
    vec = pl.BlockSpec((1, d), lambda i: (0, 0))
    shp = jax.ShapeDtypeStruct((seq, d), F32)
    return pl.pallas_call(
        body, name="mix_ln1", grid=(seq // tm,),
        in_specs=[pl.BlockSpec((HEADS, tm, LANES), lambda i: (0, i, 0)), pl.BlockSpec((tm, DIL_WIDTH), lambda i: (i, 0)),
                  pl.BlockSpec((HEADS * LANES, d), lambda i: (0, 0)), pl.BlockSpec((DIL_WIDTH, d), lambda i: (0, 0)), blk, vec, vec],
        out_specs=[blk, blk, blk], out_shape=[shp, shp, jax.ShapeDtypeStruct((seq, d), MXU_DTYPE)],
        compiler_params=_params("parallel"))(o_mla, o_dil, w_o_mla, w_o_dil, x0, g, b)


def _ln_bwd(dy, z, g, name, tm=512):
    seq, d = z.shape
    tm = min(tm, seq)

    def body(dy_ref, z_ref, g_ref, dz_ref, dzb_ref, dg_ref, db_ref):
        @pl.when(pl.program_id(0) == 0)
        def _():
            dg_ref[...] = jnp.zeros_like(dg_ref)
            db_ref[...] = jnp.zeros_like(db_ref)

        dyb = dy_ref[...]
        xh, r = _ln_stats(z_ref[...])
        dg_ref[...] += jnp.sum(dyb * xh, axis=0, keepdims=True)
        db_ref[...] += jnp.sum(dyb, axis=0, keepdims=True)
        dz = _ln_bwd_math(dyb, xh, r, g_ref[...])
        dz_ref[...] = dz
        dzb_ref[...] = dz.astype(dzb_ref.dtype)

    blk = pl.BlockSpec((tm, d), lambda i: (i, 0))
    vec = pl.BlockSpec((1, d), lambda i: (0, 0))
    return pl.pallas_call(
        body, name=name, grid=(seq // tm,), in_specs=[blk, blk, vec], out_specs=[blk, blk, vec, vec],
        out_shape=[jax.ShapeDtypeStruct((seq, d), F32), jax.ShapeDtypeStruct((seq, d), MXU_DTYPE),
                   jax.ShapeDtypeStruct((1, d), F32), jax.ShapeDtypeStruct((1, d), F32)],
        compiler_params=_params("arbitrary"))(dy, z, g)


def _ln2_loss_bwd(x1, ffn, target, g, b, tm=512):
    seq, d = x1.shape
    tm = min(tm, seq)

    def body(x_ref, f_ref, t_ref, g_ref, b_ref, dz_ref, dzb_ref, loss_ref, dg_ref, db_ref):
        @pl.when(pl.program_id(0) == 0)
        def _():
            loss_ref[...] = jnp.zeros_like(loss_ref)
            dg_ref[...] = jnp.zeros_like(dg_ref)
            db_ref[...] = jnp.zeros_like(db_ref)

        gv = g_ref[...]
        z = DN_ALPHA * x_ref[...] + f_ref[...]
        xh, r = _ln_stats(z)
        err = (xh * gv + b_ref[...]) - t_ref[...]
        loss_ref[...] += 0.5 * jnp.sum(jnp.mean(err * err, axis=-1, keepdims=True), axis=0, keepdims=True)
        dy = err * (1.0 / d)
        dg_ref[...] += jnp.sum(dy * xh, axis=0, keepdims=True)
        db_ref[...] += jnp.sum(dy, axis=0, keepdims=True)
        dz = _ln_bwd_math(dy, xh, r, gv)
        dz_ref[...] = dz
        dzb_ref[...] = dz.astype(dzb_ref.dtype)

    blk = pl.BlockSpec((tm, d), lambda i: (i, 0))
    vec = pl.BlockSpec((1, d), lambda i: (0, 0))
    return pl.pallas_call(
        body, name="ln2_loss_bwd", grid=(seq // tm,), in_specs=[blk, blk, blk, vec, vec],
        out_specs=[blk, blk, pl.BlockSpec((1, LANES), lambda i: (0, 0)), vec, vec],
        out_shape=[jax.ShapeDtypeStruct((seq, d), F32), jax.ShapeDtypeStruct((seq, d), MXU_DTYPE),
                   jax.ShapeDtypeStruct((1, LANES), F32),
                   jax.ShapeDtypeStruct((1, d), F32), jax.ShapeDtypeStruct((1, d), F32)],
        compiler_params=_params("arbitrary"))(x1, ffn, target, g, b)


HALO = 16


def _conv_rows(e, w_ref, b_ref):
    y = b_ref[...] + w_ref[0:1, :] * pltpu.roll(e, 2, 0)
    y = y + w_ref[1:2, :] * pltpu.roll(e, 1, 0)
    return y + w_ref[2:3, :] * e


_GELU_C = math.sqrt(2.0 / math.pi)
_GELU_A = 0.044715


def _gelu(x):
    return 0.5 * x * (1.0 + jnp.tanh(_GELU_C * (x + _GELU_A * (x * x * x))))


CONV_TN = 256


def _ffn_interleave(a, axis):
    shp = a.shape
    a = a.reshape(shp[:axis] + (2, D_FF // CONV_TN, CONV_TN) + shp[axis + 1:])
    return jnp.swapaxes(a, axis, axis + 1).reshape(shp)


def _ffn_deinterleave(a, axis):
    shp = a.shape
    a = a.reshape(shp[:axis] + (D_FF // CONV_TN, 2, CONV_TN) + shp[axis + 1:])
    return jnp.swapaxes(a, axis, axis + 1).reshape(shp)


def _conv_gate_fwd(u, conv_w, conv_b, tm=1024):
    seq = u.shape[0]
    tm = min(tm, seq)
    tn = CONV_TN

    def body(u_ref, up_ref, w_ref, b_ref, o_ref):
        first = pl.program_id(0) == 0
        e = jnp.concatenate([jnp.where(first, 0.0, up_ref[...]), u_ref[...]], axis=0)
        y = _conv_rows(e, w_ref, b_ref)[HALO:]
        o_ref[...] = (_gelu(y[:, tn:]) * y[:, :tn]).astype(o_ref.dtype)

    hb = tm // HALO
    return pl.pallas_call(
        body, name="conv_gate_fwd", grid=(seq // tm, D_FF // tn),
        in_specs=[pl.BlockSpec((tm, 2 * tn), lambda i, j: (i, j)),
                  pl.BlockSpec((HALO, 2 * tn), lambda i, j: (jnp.maximum(i * hb - 1, 0), j)),
                  pl.BlockSpec((3, 2 * tn), lambda i, j: (0, j)), pl.BlockSpec((1, 2 * tn), lambda i, j: (0, j))],
        out_specs=pl.BlockSpec((tm, tn), lambda i, j: (i, j)), out_shape=jax.ShapeDtypeStruct((seq, D_FF), MXU_DTYPE),
        compiler_params=_params("parallel", "parallel"),
    )(u, u, conv_w, conv_b)


def _conv_gate_bwd(u, d_act, conv_w, conv_b, tm=512):
    seq = u.shape[0]
    tm = min(tm, seq)
    tn = CONV_TN
    ni = seq // tm
    rows_e = tm + 2 * HALO

    def body(u_ref, up_ref, un_ref, da_ref, dan_ref, w_ref, b_ref, du_ref, dw_ref, db_ref):
        i = pl.program_id(1)
        first, last = i == 0, i == ni - 1

        @pl.when(i == 0)
        def _():
            dw_ref[...] = jnp.zeros_like(dw_ref)
            db_ref[...] = jnp.zeros_like(db_ref)

        e = jnp.concatenate([jnp.where(first, 0.0, up_ref[...]), u_ref[...], jnp.where(last, 0.0, un_ref[...])], axis=0)
        y = _conv_rows(e, w_ref, b_ref)
        ya, yg = y[:, :tn], y[:, tn:]
        dact = jnp.concatenate([jnp.zeros((HALO, tn), F32), da_ref[...].astype(F32),
                                jnp.where(last, 0.0, dan_ref[...].astype(F32))], axis=0)
        th = jnp.tanh(_GELU_C * (yg + _GELU_A * (yg * yg * yg)))
        gelu = 0.5 * yg * (1.0 + th)
        gelu_grad = 0.5 * (1.0 + th) + 0.5 * yg * (1.0 - th * th) * (_GELU_C * (1.0 + 3.0 * _GELU_A * (yg * yg)))
        dy = jnp.concatenate([dact * gelu, dact * ya * gelu_grad], axis=1)
        du = w_ref[2:3, :] * dy + w_ref[1:2, :] * pltpu.roll(dy, rows_e - 1, 0) + w_ref[0:1, :] * pltpu.roll(dy, rows_e - 2, 0)
        du_ref[...] = du[HALO:HALO + tm].astype(du_ref.dtype)
        dyt = dy[HALO:HALO + tm]
        dw_ref[0:1, :] += jnp.sum(dyt * pltpu.roll(e, 2, 0)[HALO:HALO + tm], axis=0, keepdims=True)
        dw_ref[1:2, :] += jnp.sum(dyt * pltpu.roll(e, 1, 0)[HALO:HALO + tm], axis=0, keepdims=True)
        dw_ref[2:3, :] += jnp.sum(dyt * e[HALO:HALO + tm], axis=0, keepdims=True)
        db_ref[...] += jnp.sum(dyt, axis=0, keepdims=True)

    hb = tm // HALO
    nh = seq // HALO
    prev = lambda j, i: (jnp.maximum(i * hb - 1, 0), j)
    nxt = lambda j, i: (jnp.minimum((i + 1) * hb, nh - 1), j)
    return pl.pallas_call(
        body, name="conv_gate_bwd", grid=(D_FF // tn, ni),
        in_specs=[pl.BlockSpec((tm, 2 * tn), lambda j, i: (i, j)), pl.BlockSpec((HALO, 2 * tn), prev),
                  pl.BlockSpec((HALO, 2 * tn), nxt), pl.BlockSpec((tm, tn), lambda j, i: (i, j)), pl.BlockSpec((HALO, tn), nxt),
                  pl.BlockSpec((3, 2 * tn), lambda j, i: (0, j)), pl.BlockSpec((1, 2 * tn), lambda j, i: (0, j))],
        out_specs=[pl.BlockSpec((tm, 2 * tn), lambda j, i: (i, j)), pl.BlockSpec((3, 2 * tn), lambda j, i: (0, j)),
                   pl.BlockSpec((1, 2 * tn), lambda j, i: (0, j))],
        out_shape=[jax.ShapeDtypeStruct((seq, 2 * D_FF), MXU_DTYPE), jax.ShapeDtypeStruct((3, 2 * D_FF), F32),
                   jax.ShapeDtypeStruct((1, 2 * D_FF), F32)],
        compiler_params=_params("parallel", "arbitrary"),
    )(u, u, u, d_act, d_act, conv_w, conv_b)


def _pad_heads(w, width):
    w = jnp.transpose(w, (1, 0, 2))
    return jnp.pad(w, ((0, 0), (0, 0), (0, LANES - width))).astype(MXU_DTYPE)


def _unpad_heads(d, width):
    return jnp.transpose(d[:, :, :width], (1, 0, 2))


def _split_pad_rows(w_t):
    z = lambda n: jnp.zeros((n, w_t.shape[1]), w_t.dtype)
    return jnp.concatenate([w_t[:384], z(64), w_t[384:416], z(32), w_t[416:]], axis=0)


def _split_unpad_rows(w_p):
    return jnp.concatenate([w_p[:384], w_p[448:480], w_p[512:]], axis=0)


def _pad_w_o(w_o):
    mla = jnp.pad(w_o[:512].reshape(HEADS, HEAD_DIM, D_MODEL), ((0, 0), (0, LANES - HEAD_DIM), (0, 0)))
    return mla.reshape(HEADS * LANES, D_MODEL).astype(MXU_DTYPE), w_o[512:].astype(MXU_DTYPE)


def _unpad_w_o(d_mla, d_dil):
    return jnp.concatenate([d_mla.reshape(HEADS, LANES, D_MODEL)[:, :HEAD_DIM].reshape(512, D_MODEL), d_dil], axis=0)


def _row(v):
    return v.reshape(1, -1).astype(F32)


def _compute_weights(w):
    w_o_mla, w_o_dil = _pad_w_o(w["w_o"])
    return dict(
        w_in_t=_split_pad_rows(w["w_in"].T).astype(MXU_DTYPE), wq=_pad_heads(w["w_uq"], NOPE_DIM + ROPE_DIM),
        wk=_pad_heads(w["w_uk"], NOPE_DIM), wv=_pad_heads(w["w_uv"], HEAD_DIM), w_o_mla=w_o_mla, w_o_dil=w_o_dil,
        w_up_t=_ffn_interleave(w["w_up"].T, 0).astype(MXU_DTYPE), w_down=w["w_down"].astype(MXU_DTYPE),
        conv_w=_ffn_interleave(w["conv_w"].astype(F32), 1),
        g_cq=_row(w["g_cq"]), g_ckv=_row(w["g_ckv"]), ln1_g=_row(w["ln1_g"]), ln1_b=_row(w["ln1_b"]),
        conv_b=_ffn_interleave(_row(w["conv_b"]), 1), ln2_g=_row(w["ln2_g"]), ln2_b=_row(w["ln2_b"]))


def _natural_grads(g):
    return dict(
        w_in=_split_unpad_rows(g["w_in_t"]).T, g_cq=g["g_cq"].reshape(-1), g_ckv=g["g_ckv"].reshape(-1),
        w_uq=_unpad_heads(g["wq"], NOPE_DIM + ROPE_DIM), w_uk=_unpad_heads(g["wk"], NOPE_DIM),
        w_uv=_unpad_heads(g["wv"], HEAD_DIM), w_o=_unpad_w_o(g["w_o_mla"], g["w_o_dil"]), ln1_g=g["ln1_g"].reshape(-1),
        ln1_b=g["ln1_b"].reshape(-1), w_up=_ffn_deinterleave(g["w_up_t"], 0).T, conv_w=_ffn_deinterleave(g["conv_w"], 1),
        conv_b=_ffn_deinterleave(g["conv_b"], 1).reshape(-1),
        w_down=g["w_down"], ln2_g=g["ln2_g"].reshape(-1), ln2_b=g["ln2_b"].reshape(-1))


def _layer_grads(x0, target, cw, x0_mxu=None, late_weights=None, on_ffn_grads=None):
    seq = x0.shape[0]
    ctab, stab = _rope_tables(seq)
    gq, gk = cw["g_cq"], cw["g_ckv"]
    wq, wk, wv = cw["wq"], cw["wk"], cw["wv"]
    x0b = x0.astype(MXU_DTYPE) if x0_mxu is None else x0_mxu

    h = _mm(x0b, cw["w_in_t"], name="mm_h", tb=True, tm=1024, tn=1024, tk=1024)
    qf, kf, vp = _mla_prep(h, gq, gk, wq, wk, wv, ctab, stab)
    o_mla, o_mla_b, lse_mla = _mla_attn_fwd(qf, kf, vp)
    o_dil, o_dil_b, lse_dil = _dil_fwd(h)
    if late_weights is not None:
        cw = {**cw, **late_weights(o_mla_b)}
    cb = cw["conv_b"]
    z1, x1, x1b = _mix_ln1(o_mla_b, o_dil_b, cw["w_o_mla"], cw["w_o_dil"], x0, cw["ln1_g"], cw["ln1_b"])
    u = _mm(x1b, cw["w_up_t"], name="mm_up", tb=True, tm=1024, tn=512, tk=1024)
    act = _conv_gate_fwd(u, cw["conv_w"], cb)
    ffn = _mm(act, cw["w_down"], name="mm_down", tm=1024, tn=1024, tk=2816)
    dz2, dz2b, loss, d_ln2_g, d_ln2_b = _ln2_loss_bwd(x1, ffn, target, cw["ln2_g"], cw["ln2_b"])

    d_act = _mm(dz2b, cw["w_down"], name="mm_d_act", tb=True, out_dtype=MXU_DTYPE, tm=1024, tn=1408, tk=1024)
    d_w_down = _mm(act, dz2b, name="mm_dw_down", ta=True, out_dtype=MXU_DTYPE, tm=1408, tn=1024, tk=1024)
    du, d_conv_w, d_conv_b = _conv_gate_bwd(u, d_act, cw["conv_w"], cb)
    dx1 = _mm(du, cw["w_up_t"], name="mm_dx1", res=dz2, res_scale=DN_ALPHA, tm=1024, tn=1024, tk=1408)
    d_w_up_t = _mm(du, x1b, name="mm_dw_up", ta=True, out_dtype=MXU_DTYPE, tm=1408, tn=1024, tk=1024)
    ln1_g = cw["ln1_g"] if on_ffn_grads is None else cw["ln1_g"] + on_ffn_grads(d_w_up_t, d_w_down)
    dz1, dz1b, d_ln1_g, d_ln1_b = _ln_bwd(dx1, z1, ln1_g, "ln1_bwd")
    do_mla = _mm_do_mla(dz1b, cw["w_o_mla"])
    do_dil = _mm(dz1b, cw["w_o_dil"], name="mm_do_dil", tb=True, tm=1024, tn=512, tk=1024)
    d_w_o_mla = _mm_dw_o_mla(o_mla_b, dz1b)
    d_w_o_dil = _mm(o_dil_b, dz1b, name="mm_dw_o_dil", ta=True, out_dtype=MXU_DTYPE, tm=512, tn=1024, tk=1024)
    dq_dil, dk_dil, dv_dil = _dil_bwd(h, o_dil, lse_dil, do_dil)
    dqf, dkf, dvf = _mla_attn_bwd(qf, kf, vp, o_mla, lse_mla, do_mla)
    dh_mla, d_wq, d_wk, d_wv, d_gq, d_gk = _mla_prep_bwd(h, gq, gk, wq, wk, wv, ctab, stab, dqf, dkf, dvf)
    dh = _assemble_dh(dh_mla, dq_dil, dk_dil, dv_dil)
    grad_x = _mm(dh, cw["w_in_t"], name="mm_dx0", res=dz1, res_scale=DN_ALPHA, tm=1024, tn=1024, tk=2048)
    d_w_in_t = _mm(dh, x0b, name="mm_dw_in", ta=True, out_dtype=MXU_DTYPE, tm=1024, tn=1024, tk=1024)

    grads = dict(
        w_in_t=d_w_in_t, wq=d_wq, wk=d_wk, wv=d_wv, w_o_mla=d_w_o_mla, w_o_dil=d_w_o_dil, w_up_t=d_w_up_t, w_down=d_w_down,
        conv_w=d_conv_w, g_cq=d_gq, g_ckv=d_gk, ln1_g=d_ln1_g, ln1_b=d_ln1_b, conv_b=d_conv_b, ln2_g=d_ln2_g, ln2_b=d_ln2_b)
    return loss[0, 0], grad_x, grads


def _all_gather(blocks, name):
    na = len(blocks)

    def body(*refs):
        ins, outs = refs[:na], refs[na:2 * na]
        send_sems, recv_sems, local_sems = refs[2 * na:]
        x, y, c = lax.axis_index("x"), lax.axis_index("y"), lax.axis_index("c")
        me, sibling = (x, y, c), (x, y, 1 - c)
        chips = [(1 - x, y), (x, 1 - y), (1 - x, 1 - y)]

        def slot(out, pos):
            return out.at[4 * pos[0] + 2 * pos[1] + pos[2]]

        def copy(a, k, block, to, src=None):
            return pltpu.make_async_remote_copy(
                src_ref=slot(outs[a], block) if src is None else src, dst_ref=slot(outs[a], block),
                send_sem=send_sems.at[7 * a + k], recv_sem=recv_sems.at[7 * a + k],
                device_id=to, device_id_type=pl.DeviceIdType.MESH)

        mine = [pltpu.make_async_copy(ins[a], slot(outs[a], me), local_sems.at[a]) for a in range(na)]
        for cp in mine:
            cp.start()
        first = []
        for a in range(na):
            first.append(copy(a, 0, me, sibling, src=ins[a]))
            first += [copy(a, 1 + j, me, (*chip, c), src=ins[a]) for j, chip in enumerate(chips)]
        for cp in first:
            cp.start()
        passed = []
        for j, chip in enumerate(chips):
            for a in range(na):
                copy(a, 1 + j, (*chip, c), me).wait_recv()
                cp = copy(a, 4 + j, (*chip, c), sibling)
                cp.start()
                passed.append(cp)
        for a in range(na):
            copy(a, 0, sibling, me).wait_recv()
            for j, chip in enumerate(chips):
                copy(a, 4 + j, (*chip, 1 - c), me).wait_recv()
        for cp in first + passed:
            cp.wait_send()
        for cp in mine:
            cp.wait()

    any_spec = pl.BlockSpec(memory_space=pl.ANY)
    return pl.pallas_call(
        body, name=name, in_specs=[any_spec] * na, out_specs=[any_spec] * na,
        out_shape=[jax.ShapeDtypeStruct((N_DEV,) + b.shape, b.dtype) for b in blocks],
        scratch_shapes=[pltpu.SemaphoreType.DMA((7 * na,)), pltpu.SemaphoreType.DMA((7 * na,)), pltpu.SemaphoreType.DMA((na,))],
    )(*blocks)


def _exchange(parts, name):
    na = len(parts)

    def body(*refs):
        ins, outs = refs[:na], refs[na:2 * na]
        send_sems, recv_sems, local_sems = refs[2 * na:]
        x, y, c = lax.axis_index("x"), lax.axis_index("y"), lax.axis_index("c")
        me = 4 * x + 2 * y + c

        def peer_of(d):
            return x ^ (d >> 2), y ^ ((d >> 1) & 1), c ^ (d & 1)

        def copy(a, d):
            px, py, pc = peer_of(d)
            return pltpu.make_async_remote_copy(
                src_ref=ins[a].at[4 * px + 2 * py + pc], dst_ref=outs[a].at[me],
                send_sem=send_sems.at[7 * a + d - 1], recv_sem=recv_sems.at[7 * a + d - 1],
                device_id=(px, py, pc), device_id_type=pl.DeviceIdType.MESH)

        def arrival(a, d):
            px, py, pc = peer_of(d)
            return pltpu.make_async_remote_copy(
                src_ref=ins[a].at[me], dst_ref=outs[a].at[4 * px + 2 * py + pc],
                send_sem=send_sems.at[7 * a + d - 1], recv_sem=recv_sems.at[7 * a + d - 1],
                device_id=(px, py, pc), device_id_type=pl.DeviceIdType.MESH)

        mine = [pltpu.make_async_copy(ins[a].at[me], outs[a].at[me], local_sems.at[a]) for a in range(na)]
        for cp in mine:
            cp.start()
        sent = [copy(a, d) for a in range(na) for d in range(1, N_DEV)]
        for cp in sent:
            cp.start()
        for a in range(na):
            for d in range(1, N_DEV):
                arrival(a, d).wait_recv()
        for cp in sent:
            cp.wait_send()
        for cp in mine:
            cp.wait()

    any_spec = pl.BlockSpec(memory_space=pl.ANY)
    return pl.pallas_call(
        body, name=name, in_specs=[any_spec] * na, out_specs=[any_spec] * na,
        out_shape=[jax.ShapeDtypeStruct(p.shape, p.dtype) for p in parts],
        scratch_shapes=[pltpu.SemaphoreType.DMA((7 * na,)), pltpu.SemaphoreType.DMA((7 * na,)), pltpu.SemaphoreType.DMA((na,))],
    )(*parts)


_HBM_SPEC = pl.BlockSpec(memory_space=pltpu.HBM)
_SEM_SPEC = pl.BlockSpec(memory_space=pltpu.SEMAPHORE)
_DATAFLOW = pltpu.CompilerParams(has_side_effects=pltpu.SideEffectType.DATAFLOW_SIDE_EFFECTING)


def _split_copies(ins, lands, send_sems, recv_sems, gather):
    x, y, c = lax.axis_index("x"), lax.axis_index("y"), lax.axis_index("c")
    me = 4 * x + 2 * y + c
    copies = []
    for a in range(len(ins)):
        for d in range(1, N_DEV):
            px, py, pc = x ^ (d >> 2), y ^ ((d >> 1) & 1), c ^ (d & 1)
            copies.append(pltpu.make_async_remote_copy(
                src_ref=ins[a] if gather else ins[a].at[4 * px + 2 * py + pc], dst_ref=lands[a].at[me],
                send_sem=send_sems.at[7 * a + d - 1], recv_sem=recv_sems.at[7 * a + d - 1],
                device_id=(px, py, pc), device_id_type=pl.DeviceIdType.MESH))
    return copies


def _send_start(srcs, gather, name):
    na = len(srcs)
    land_types = [pltpu.HBM(((N_DEV,) + s.shape) if gather else s.shape, s.dtype) for s in srcs]

    def body(*refs):
        ins, lands = refs[:na], refs[na:2 * na]
        send_sems, recv_sems, token = refs[2 * na], refs[2 * na + 1], refs[-1]
        for cp in _split_copies(ins, lands, send_sems, recv_sems, gather):
            cp.start()
        token[...] = jnp.zeros_like(token)

    hbm = lambda a: pltpu.with_memory_space_constraint(a, pltpu.HBM)
    outs = pl.pallas_call(
        body, name=name,
        out_shape=(pltpu.SemaphoreType.DMA((7 * na,)), pltpu.SemaphoreType.DMA((7 * na,)),
                   *[pltpu.HBM(s.shape, s.dtype) for s in srcs], *land_types, jax.ShapeDtypeStruct((8, LANES), F32)),
        in_specs=[_HBM_SPEC] * (2 * na),
        out_specs=(_SEM_SPEC, _SEM_SPEC, *[_HBM_SPEC] * (2 * na), pl.BlockSpec(memory_space=pltpu.VMEM)),
        input_output_aliases={i: 2 + i for i in range(2 * na)}, compiler_params=_DATAFLOW,
    )(*[hbm(s) for s in srcs], *[hbm(lax.empty(t.shape, t.dtype)) for t in land_types])
    return dict(send=outs[0], recv=outs[1], srcs=list(outs[2:2 + na]), lands=list(outs[2 + na:2 + 2 * na]), token=outs[-1],
                gather=gather)


def _send_wait(handle, after, name):
    na = len(handle["srcs"])
    gather = handle["gather"]

    def body(*refs):
        ins, lands = refs[:na], refs[na:2 * na]
        send_sems, recv_sems = refs[2 * na], refs[2 * na + 1]
        for cp in _split_copies(ins, lands, send_sems, recv_sems, gather):
            cp.wait_send()
            cp.wait_recv()

    both = handle["srcs"] + handle["lands"]
    outs = pl.pallas_call(
        body, name=name, out_shape=[pltpu.HBM(a.shape, a.dtype) for a in both],
        in_specs=[_HBM_SPEC] * (2 * na) + [_SEM_SPEC, _SEM_SPEC, pl.BlockSpec(memory_space=pl.ANY)],
        out_specs=[_HBM_SPEC] * (2 * na), input_output_aliases={i: i for i in range(2 * na)}, compiler_params=_DATAFLOW,
    )(*both, handle["send"], handle["recv"], after)
    return list(outs[:na]), list(outs[na:])


def _sum_parts(parts, name):
    npart, r, n = parts.shape
    tr = r if r <= 256 else max(t for t in range(16, 257, 16) if r % t == 0)

    def body(p_ref, o_ref):
        g = p_ref[0].astype(F32)
        for s in range(1, npart):
            g = g + p_ref[s].astype(F32)
        o_ref[...] = g

    return pl.pallas_call(
        body, name=name, grid=(r // tr,), in_specs=[pl.BlockSpec((npart, tr, n), lambda i: (0, i, 0))],
        out_specs=pl.BlockSpec((tr, n), lambda i: (i, 0)), out_shape=jax.ShapeDtypeStruct((r, n), F32),
        compiler_params=_params("parallel"),
    )(parts)


def _adamw(parts, w, m, v, name):
    npart, r, n = parts.shape
    tr = r if r <= 256 else max(t for t in range(16, 257, 16) if r % t == 0)
    c1 = 1.0 - ADAM_B1 ** ADAM_STEP
    c2 = 1.0 - ADAM_B2 ** ADAM_STEP

    def body(p_ref, w_ref, m_ref, v_ref, g_out, d_out, m_out, v_out):
        g = p_ref[0].astype(F32)
        for s in range(1, npart):
            g = g + p_ref[s].astype(F32)
        m_new = ADAM_B1 * m_ref[...] + (1.0 - ADAM_B1) * g
        v_new = ADAM_B2 * v_ref[...] + (1.0 - ADAM_B2) * (g * g)
        g_out[...] = g
        m_out[...] = m_new
        v_out[...] = v_new
        d_out[...] = -ADAM_LR * ((m_new / c1) / (jnp.sqrt(v_new / c2) + ADAM_EPS) + ADAM_WD * w_ref[...])

    blk = pl.BlockSpec((tr, n), lambda i: (i, 0))
    shp = jax.ShapeDtypeStruct((r, n), F32)
    return pl.pallas_call(
        body, name=name, grid=(r // tr,), in_specs=[pl.BlockSpec((npart, tr, n), lambda i: (0, i, 0)), blk, blk, blk],
        out_specs=[blk] * 4, out_shape=[shp] * 4, compiler_params=_params("parallel"),
    )(parts, w, m, v)


def _pack(arrs, dtype, row_multiple=16):
    flat = jnp.concatenate([a.reshape(-1).astype(dtype) for a in arrs])
    rows = -(-flat.shape[0] // PACK_COLS)
    rows = -(-rows // row_multiple) * row_multiple
    return jnp.pad(flat, (0, rows * PACK_COLS - flat.shape[0])).reshape(rows, PACK_COLS)


def _unpack(buf, shapes):
    lead = buf.shape[:-2]
    flat = buf.reshape(lead + (-1,))
    out, at = [], 0
    for s in shapes:
        size = math.prod(s)
        out.append(flat[..., at:at + size].reshape(lead + tuple(s)))
        at += size
    return out


REPLICATED = ("g_cq", "g_ckv", "w_uk", "w_uv", "ln1_g", "ln1_b", "conv_b", "ln2_g", "ln2_b")
ALL_WEIGHTS = ("w_in", "g_cq", "g_ckv", "w_uq", "w_uk", "w_uv", "w_o", "ln1_g", "ln1_b", "w_up", "conv_w", "conv_b",
               "w_down", "ln2_g", "ln2_b")


def kernel(x, w_in, g_cq, g_ckv, w_uq, w_uk, w_uv, w_o, ln1_g, ln1_b, w_up, conv_w, conv_b, w_down, ln2_g, ln2_b, loss_target, m_w_in, m_g_cq, m_g_ckv, m_w_uq, m_w_uk, m_w_uv, m_w_o, m_ln1_g, m_ln1_b, m_w_up, m_conv_w, m_conv_b, m_w_down, m_ln2_g, m_ln2_b, v_w_in, v_g_cq, v_g_ckv, v_w_uq, v_w_uk, v_w_uv, v_w_o, v_ln1_g, v_ln1_b, v_w_up, v_conv_w, v_conv_b, v_w_down, v_ln2_g, v_ln2_b):
    w = dict(w_in=w_in, g_cq=g_cq, g_ckv=g_ckv, w_uq=w_uq, w_uk=w_uk, w_uv=w_uv, w_o=w_o, ln1_g=ln1_g, ln1_b=ln1_b,
             w_up=w_up, conv_w=conv_w, conv_b=conv_b, w_down=w_down, ln2_g=ln2_g, ln2_b=ln2_b)
    m = dict(w_in=m_w_in, g_cq=m_g_cq, g_ckv=m_g_ckv, w_uq=m_w_uq, w_uk=m_w_uk, w_uv=m_w_uv, w_o=m_w_o, ln1_g=m_ln1_g,
             ln1_b=m_ln1_b, w_up=m_w_up, conv_w=m_conv_w, conv_b=m_conv_b, w_down=m_w_down, ln2_g=m_ln2_g, ln2_b=m_ln2_b)
    v = dict(w_in=v_w_in, g_cq=v_g_cq, g_ckv=v_g_ckv, w_uq=v_w_uq, w_uk=v_w_uk, w_uv=v_w_uv, w_o=v_w_o, ln1_g=v_ln1_g,
             ln1_b=v_ln1_b, w_up=v_w_up, conv_w=v_conv_w, conv_b=v_conv_b, w_down=v_w_down, ln2_g=v_ln2_g, ln2_b=v_ln2_b)
    me = 4 * lax.axis_index("x") + 2 * lax.axis_index("y") + lax.axis_index("c")
    wire = lambda a: a.astype(WIRE_DTYPE)
    n_in = w_in.shape[1]
    n_in_pad = -(-n_in // 16) * 16
    pad_taps = lambda a: jnp.pad(a, ((0, 8 - a.shape[0]), (0, 0)))

    own_slot = lambda buf, block: lax.dynamic_update_index_in_dim(buf, block, me, 0)
    blocks = lambda a: wire(a).reshape((N_DEV, a.shape[0] // N_DEV) + a.shape[1:])

    g_in, g_uq, g_conv = _all_gather(
        [jnp.pad(wire(w_in).T, ((0, n_in_pad - n_in), (0, 0))), wire(w_uq).reshape(w_uq.shape[0], -1), pad_taps(conv_w)],
        "gather_weights")
    late = _send_start([wire(w_o), wire(w_up).T, wire(w_down)], True, "gather_late_start")
    cw = dict(
        w_in_t=_split_pad_rows(g_in[:, :n_in].reshape(-1, D_MODEL)).astype(MXU_DTYPE),
        wq=_pad_heads(g_uq.reshape((-1,) + w_uq.shape[1:]), NOPE_DIM + ROPE_DIM),
        wk=_pad_heads(w_uk, NOPE_DIM), wv=_pad_heads(w_uv, HEAD_DIM),
        conv_w=_ffn_interleave(jnp.transpose(g_conv[:, :conv_w.shape[0]], (1, 0, 2)).reshape(conv_w.shape[0], -1), 1),
        g_cq=_row(g_cq), g_ckv=_row(g_ckv), ln1_g=_row(ln1_g), ln1_b=_row(ln1_b), conv_b=_ffn_interleave(_row(conv_b), 1),
        ln2_g=_row(ln2_g), ln2_b=_row(ln2_b))

    def late_weights(after):
        own, landed = _send_wait(late, after, "gather_late_wait")
        g_o, g_up, g_down = [own_slot(buf, blk) for buf, blk in zip(landed, own)]
        w_o_mla, w_o_dil = _pad_w_o(g_o.reshape(-1, D_MODEL))
        return dict(w_o_mla=w_o_mla, w_o_dil=w_o_dil, w_up_t=_ffn_interleave(g_up.reshape(-1, D_MODEL), 0).astype(MXU_DTYPE),
                    w_down=g_down.reshape(-1, D_MODEL).astype(MXU_DTYPE))

    ffn_sent = []

    def on_ffn_grads(d_w_up_t, d_w_down):
        ffn_sent.append(_send_start([blocks(_ffn_deinterleave(d_w_up_t, 0)), blocks(d_w_down)], False, "exchange_ffn_start"))
        return jnp.tile(ffn_sent[0]["token"][0:1], (1, D_MODEL // LANES))

    x0b = (x[0] + late["token"][0, 0]).astype(MXU_DTYPE)
    loss, grad_x, g = _layer_grads(x[0], loss_target[0], cw, x0b, late_weights, on_ffn_grads)
    loss = lax.psum(loss, MESH_AXES)

    d_in = jnp.pad(blocks(_split_unpad_rows(g["w_in_t"])), ((0, 0), (0, n_in_pad - n_in), (0, 0)))
    d_uq = blocks(_unpad_heads(g["wq"], NOPE_DIM + ROPE_DIM).reshape(Q_RANK, -1))
    r_in, r_uq, r_o = _exchange([d_in, d_uq, blocks(_unpad_w_o(g["w_o_mla"], g["w_o_dil"]))], "exchange_grads")
    (s_up, s_down), (r_up, r_down) = _send_wait(ffn_sent[0], g["w_in_t"], "exchange_ffn_wait")
    r_up = own_slot(r_up, lax.dynamic_index_in_dim(s_up, me, 0, keepdims=False))
    r_down = own_slot(r_down, lax.dynamic_index_in_dim(s_down, me, 0, keepdims=False))
    nat = dict(g_cq=g["g_cq"], g_ckv=g["g_ckv"], w_uk=_unpad_heads(g["wk"], NOPE_DIM), w_uv=_unpad_heads(g["wv"], HEAD_DIM),
               ln1_g=g["ln1_g"], ln1_b=g["ln1_b"], conv_b=_ffn_deinterleave(g["conv_b"], 1), ln2_g=g["ln2_g"], ln2_b=g["ln2_b"])
    rep_all, cw_all = _all_gather([_pack([nat[n] for n in REPLICATED], F32, 8), pad_taps(_ffn_deinterleave(g["conv_w"], 1))],
                                  "gather_small_grads")

    out = {}

    def update(name, parts, shape2d=None):
        w2, m2, v2 = [d[name].reshape(shape2d or d[name].shape) for d in (w, m, v)]
        res = _adamw(parts, w2, m2, v2, "adamw_" + name)
        for kind, a in zip(("grad", "delta", "new_m", "new_v"), res):
            out[kind, name] = a.reshape(w[name].shape)

    update("w_in", _sum_parts(r_in, "sum_w_in")[:n_in].T[None])
    update("w_uq", r_uq, (w_uq.shape[0], -1))
    update("w_o", r_o)
    update("w_up", _sum_parts(r_up, "sum_w_up").T[None])
    update("w_down", r_down)
    res = _adamw(rep_all, *[_pack([d[n] for n in REPLICATED], F32, 8) for d in (w, m, v)], "adamw_replicated")
    for kind, buf in zip(("grad", "delta", "new_m", "new_v"), res):
        for n, a in zip(REPLICATED, _unpack(buf, [w[n].shape for n in REPLICATED])):
            out[kind, n] = a
    ncw = conv_w.shape[1]
    update("conv_w", lax.dynamic_slice_in_dim(cw_all[:, :conv_w.shape[0]], me * ncw, ncw, axis=2))

    return (loss, grad_x[None], *[out[kind, n] for kind in ("grad", "delta", "new_m", "new_v") for n in ALL_WEIGHTS])
```

```python
import functools
import math

import jax
import jax.numpy as jnp
from jax import lax
from jax.experimental import pallas as pl
from jax.experimental.pallas import tpu as pltpu

F32 = jnp.float32
MXU_DTYPE = jnp.bfloat16
WIRE_DTYPE = jnp.bfloat16

N_DEV = 8
D_MODEL = 1024
HEADS = 8
HEAD_DIM = 64
LANES = 128
Q_RANK, KV_RANK, ROPE_DIM, NOPE_DIM = 256, 128, 32, 64
DIL_WIDTH = HEADS * HEAD_DIM
IN_WIDTH = 1952
IN_PAD = 2048
D_FF = 2816
ROPE_THETA = 10000.0
DIL_PAIRS = ((128, 1), (512, 4), (2048, 16))
DIL_BLOCK = 128
DN_ALPHA = 2.0 ** 0.25
LN_EPS = 1e-5
RMS_EPS = 1e-6
MLA_SCALE = 1.0 / math.sqrt(NOPE_DIM + ROPE_DIM)
MLA_SCALE_LOG2 = MLA_SCALE * math.log2(math.e)
DIL_SCALE = 1.0 / math.sqrt(HEAD_DIM)
ALIBI_SLOPES = tuple(2.0 ** (-8.0 * (h + 1) / HEADS) for h in range(HEADS))
NEG_BIG = -1e30
ADAM_LR, ADAM_B1, ADAM_B2, ADAM_EPS, ADAM_WD, ADAM_STEP = 0.001, 0.9, 0.999, 1e-08, 0.01, 10
VMEM_LIMIT = 48 * 1024 * 1024
PACK_COLS = 1024


def _params(*sem):
    return pltpu.CompilerParams(dimension_semantics=sem or None, vmem_limit_bytes=VMEM_LIMIT)


def _dot(a, b, ca, cb):
    return lax.dot_general(a, b, (((ca,), (cb,)), ((), ())), preferred_element_type=F32)


_ANY_SPEC = pl.BlockSpec(memory_space=pl.ANY)


def _mm(a, b, *, name, tm, tn, tk, ta=False, tb=False, out_dtype=F32, res=None, res_scale=1.0, after=()):
    m, k = (a.shape[1], a.shape[0]) if ta else a.shape
    n = b.shape[0] if tb else b.shape[1]
    assert (b.shape[1] if tb else b.shape[0]) == k
    tm, tn, tk = min(tm, m), min(tn, n), min(tk, k)
    assert m % tm == 0 and n % tn == 0 and k % tk == 0, (name, m, n, k, tm, tn, tk)
    nk = k // tk
    a_spec = (pl.BlockSpec((tk, tm), lambda i, j, kk: (kk, i)) if ta
              else pl.BlockSpec((tm, tk), lambda i, j, kk: (i, kk)))
    b_spec = (pl.BlockSpec((tn, tk), lambda i, j, kk: (j, kk)) if tb
              else pl.BlockSpec((tk, tn), lambda i, j, kk: (kk, j)))
    o_spec = pl.BlockSpec((tm, tn), lambda i, j, kk: (i, j))
    in_specs = [a_spec, b_spec]
    args = [a, b]
    if res is not None:
        in_specs.append(o_spec)
        args.append(res)
    n_in = len(args) + len(after)
    in_specs += [_ANY_SPEC] * len(after)
    args += list(after)
    ca, cb = (0 if ta else 1), (1 if tb else 0)

    def finish(acc, r_ref, o_ref):
        if r_ref is not None:
            acc = acc + res_scale * r_ref[...]
        o_ref[...] = acc.astype(o_ref.dtype)

    def body(*refs):
        a_ref, b_ref = refs[:2]
        r_ref = refs[2] if res is not None else None
        o_ref = refs[n_in]
        part = _dot(a_ref[...].astype(MXU_DTYPE), b_ref[...].astype(MXU_DTYPE), ca, cb)
        if nk == 1:
            finish(part, r_ref, o_ref)
            return
        acc_ref = refs[-1]
        kk = pl.program_id(2)

        @pl.when(kk == 0)
        def _():
            acc_ref[...] = part

        @pl.when(kk > 0)
        def _():
            acc_ref[...] += part

        @pl.when(kk == nk - 1)
        def _():
            finish(acc_ref[...], r_ref, o_ref)

    return pl.pallas_call(
        body, name=name, grid=(m // tm, n // tn, nk), in_specs=in_specs, out_specs=o_spec,
        out_shape=jax.ShapeDtypeStruct((m, n), out_dtype),
        scratch_shapes=[pltpu.VMEM((tm, tn), F32)] if nk > 1 else [],
        compiler_params=_params("parallel", "parallel", "arbitrary"),
    )(*args)


def _mm_do_mla(dz, w_o_mla, tm=1024):
    seq, d = dz.shape
    tm = min(tm, seq)

    def body(a_ref, b_ref, o_ref):
        a = a_ref[...].astype(MXU_DTYPE)
        for hd in range(HEADS):
            o_ref[hd] = _dot(a, b_ref[LANES * hd:LANES * (hd + 1), :], 1, 1)

    return pl.pallas_call(
        body, name="mm_do_mla", grid=(seq // tm,),
        in_specs=[pl.BlockSpec((tm, d), lambda i: (i, 0)), pl.BlockSpec((HEADS * LANES, d), lambda i: (0, 0))],
        out_specs=pl.BlockSpec((HEADS, tm, LANES), lambda i: (0, i, 0)),
        out_shape=jax.ShapeDtypeStruct((HEADS, seq, LANES), F32), compiler_params=_params("parallel"),
    )(dz, w_o_mla)


def _mm_dw_o_mla(o_mla, dz, tk=1024):
    seq, d = dz.shape
    tk = min(tk, seq)
    nk = seq // tk

    def body(a_ref, b_ref, o_ref, acc_ref):
        kk = pl.program_id(0)

        @pl.when(kk == 0)
        def _():
            acc_ref[...] = jnp.zeros_like(acc_ref)

        b = b_ref[...].astype(MXU_DTYPE)
        for hd in range(HEADS):
            acc_ref[LANES * hd:LANES * (hd + 1), :] += _dot(a_ref[hd].astype(MXU_DTYPE), b, 0, 0)

        @pl.when(kk == nk - 1)
        def _():
            o_ref[...] = acc_ref[...].astype(o_ref.dtype)

    return pl.pallas_call(
        body, name="mm_dw_o_mla", grid=(nk,),
        in_specs=[pl.BlockSpec((HEADS, tk, LANES), lambda kk: (0, kk, 0)), pl.BlockSpec((tk, d), lambda kk: (kk, 0))],
        out_specs=pl.BlockSpec((HEADS * LANES, d), lambda kk: (0, 0)),
        out_shape=jax.ShapeDtypeStruct((HEADS * LANES, d), MXU_DTYPE),
        scratch_shapes=[pltpu.VMEM((HEADS * LANES, d), F32)], compiler_params=_params("arbitrary"),
    )(o_mla, dz)


def _rope_tables(seq):
    half = ROPE_DIM // 2
    freqs = ROPE_THETA ** (-jnp.arange(half, dtype=F32) / half)
    ang = jnp.arange(seq).astype(F32)[:, None] * freqs[None, :]
    cos, sin = jnp.cos(ang), jnp.sin(ang)
    one = jnp.ones((seq, NOPE_DIM), F32)
    tail = jnp.ones((seq, LANES - NOPE_DIM - ROPE_DIM), F32)
    ctab = jnp.concatenate([one, cos, cos, tail], axis=1)
    stab = jnp.concatenate([0 * one, -sin, sin, 0 * tail], axis=1)
    return ctab, stab


def _rope_swap(t):
    lane = lax.broadcasted_iota(jnp.int32, t.shape, 1)
    half = ROPE_DIM // 2
    return jnp.where(lane < NOPE_DIM + half, pltpu.roll(t, LANES - half, 1), pltpu.roll(t, half, 1))


def _rope(t, ctab, stab):
    return t * ctab + _rope_swap(t) * stab


def _rope_inv(t, ctab, stab):
    return t * ctab - _rope_swap(t) * stab


def _rms(x, g):
    r = lax.rsqrt(jnp.mean(x * x, axis=-1, keepdims=True) + RMS_EPS)
    xh = x * r
    return xh, r, xh * g


def _mla_prep(h, g_cq, g_ckv, wq, wk, wv, ctab, stab, tm=512):
    seq = h.shape[0]
    tm = min(tm, seq)

    def body(h_ref, gq_ref, gk_ref, wq_ref, wk_ref, wv_ref, c_ref, s_ref, q_out, k_out, v_out):
        hb = h_ref[...]
        ctab_, stab_ = c_ref[...], s_ref[...]
        _, _, cqn = _rms(hb[:, :Q_RANK], gq_ref[...])
        _, _, ckn = _rms(hb[:, Q_RANK:Q_RANK + KV_RANK], gk_ref[...])
        cqn = cqn.astype(MXU_DTYPE)
        ckn = ckn.astype(MXU_DTYPE)
        krr = _rope(hb[:, Q_RANK + KV_RANK:], ctab_, stab_)
        for hd in range(HEADS):
            q = _dot(cqn, wq_ref[hd], 1, 0)
            q_out[hd] = _rope(q, ctab_, stab_).astype(q_out.dtype)
            k_out[hd] = (_dot(ckn, wk_ref[hd], 1, 0) + krr).astype(k_out.dtype)
            v_out[hd] = _dot(ckn, wv_ref[hd], 1, 0).astype(v_out.dtype)

    full = lambda *shape: pl.BlockSpec(shape, lambda i: (0,) * len(shape))
    slab = pl.BlockSpec((HEADS, tm, LANES), lambda i: (0, i, 0))
    shp = jax.ShapeDtypeStruct((HEADS, seq, LANES), MXU_DTYPE)
    return pl.pallas_call(
        body, name="mla_prep", grid=(seq // tm,),
        in_specs=[pl.BlockSpec((tm, 512), lambda i: (i, 0)), full(1, Q_RANK), full(1, KV_RANK),
                  full(HEADS, Q_RANK, LANES), full(HEADS, KV_RANK, LANES), full(HEADS, KV_RANK, LANES),
                  pl.BlockSpec((tm, LANES), lambda i: (i, 0)), pl.BlockSpec((tm, LANES), lambda i: (i, 0))],
        out_specs=[slab, slab, slab], out_shape=[shp, shp, shp],
        compiler_params=_params("parallel"),
    )(h, g_cq, g_ckv, wq, wk, wv, ctab, stab)


def _mla_prep_bwd(h, g_cq, g_ckv, wq, wk, wv, ctab, stab, dq, dk, dv, tm=512, after=()):
    seq = h.shape[0]
    tm = min(tm, seq)
    n_after = len(after)

    def body(h_ref, gq_ref, gk_ref, wq_ref, wk_ref, wv_ref, c_ref, s_ref, dq_ref, dk_ref, dv_ref, *rest):
        dh_ref, dwq_ref, dwk_ref, dwv_ref, dgq_ref, dgk_ref = rest[n_after:]

        @pl.when(pl.program_id(0) == 0)
        def _():
            for r in (dwq_ref, dwk_ref, dwv_ref, dgq_ref, dgk_ref):
                r[...] = jnp.zeros_like(r)

        hb = h_ref[...]
        ctab_, stab_ = c_ref[...], s_ref[...]
        gq, gk = gq_ref[...], gk_ref[...]
        xq, rq, cqn = _rms(hb[:, :Q_RANK], gq)
        xk, rk, ckn = _rms(hb[:, Q_RANK:Q_RANK + KV_RANK], gk)
        cqn = cqn.astype(MXU_DTYPE)
        ckn = ckn.astype(MXU_DTYPE)
        d_cqn = jnp.zeros((tm, Q_RANK), F32)
        d_ckn = jnp.zeros((tm, KV_RANK), F32)
        d_krr = jnp.zeros((tm, LANES), F32)
        for hd in range(HEADS):
            dqh = _rope_inv(dq_ref[hd], ctab_, stab_).astype(MXU_DTYPE)
            d_cqn += _dot(dqh, wq_ref[hd], 1, 1)
            dwq_ref[hd] += _dot(cqn, dqh, 0, 0)
            dkh = dk_ref[hd]
            d_krr += dkh
            dkh = dkh.astype(MXU_DTYPE)
            d_ckn += _dot(dkh, wk_ref[hd], 1, 1)
            dwk_ref[hd] += _dot(ckn, dkh, 0, 0)
            dvh = dv_ref[hd].astype(MXU_DTYPE)
            d_ckn += _dot(dvh, wv_ref[hd], 1, 1)
            dwv_ref[hd] += _dot(ckn, dvh, 0, 0)
        lane = lax.broadcasted_iota(jnp.int32, (tm, LANES), 1)
        rot = (lane >= NOPE_DIM) & (lane < NOPE_DIM + ROPE_DIM)
        d_kr = jnp.where(rot, _rope_inv(jnp.where(rot, d_krr, 0.0), ctab_, stab_), 0.0)

        def rms_bwd(dy, xh, r, g, dg_ref):
            dg_ref[...] += jnp.sum(dy * xh, axis=0, keepdims=True)
            dxh = dy * g
            return r * (dxh - xh * jnp.mean(dxh * xh, axis=-1, keepdims=True))

        d_cq = rms_bwd(d_cqn, xq, rq, gq, dgq_ref)
        d_ck = rms_bwd(d_ckn, xk, rk, gk, dgk_ref)
        dh_ref[...] = jnp.concatenate([d_cq, d_ck, d_kr], axis=1).astype(dh_ref.dtype)

    full = lambda *shape: pl.BlockSpec(shape, lambda i: (0,) * len(shape))
    slab = pl.BlockSpec((HEADS, tm, LANES), lambda i: (0, i, 0))
    return pl.pallas_call(
        body, name="mla_prep_bwd", grid=(seq // tm,),
        in_specs=[pl.BlockSpec((tm, 512), lambda i: (i, 0)), full(1, Q_RANK), full(1, KV_RANK),
                  full(HEADS, Q_RANK, LANES), full(HEADS, KV_RANK, LANES), full(HEADS, KV_RANK, LANES),
                  pl.BlockSpec((tm, LANES), lambda i: (i, 0)), pl.BlockSpec((tm, LANES), lambda i: (i, 0)),
                  slab, slab, slab] + [_ANY_SPEC] * n_after,
        out_specs=[pl.BlockSpec((tm, 512), lambda i: (i, 0)), full(HEADS, Q_RANK, LANES), full(HEADS, KV_RANK, LANES),
                   full(HEADS, KV_RANK, LANES), full(1, Q_RANK), full(1, KV_RANK)],
        out_shape=[jax.ShapeDtypeStruct((seq, 512), MXU_DTYPE), jax.ShapeDtypeStruct((HEADS, Q_RANK, LANES), F32),
                   jax.ShapeDtypeStruct((HEADS, KV_RANK, LANES), F32), jax.ShapeDtypeStruct((HEADS, KV_RANK, LANES), F32),
                   jax.ShapeDtypeStruct((1, Q_RANK), F32), jax.ShapeDtypeStruct((1, KV_RANK), F32)],
        compiler_params=_params("arbitrary"),
    )(h, g_cq, g_ckv, wq, wk, wv, ctab, stab, dq, dk, dv, *after)


def _causal_mask(t):
    row = lax.broadcasted_iota(jnp.int32, (t, t), 0)
    col = lax.broadcasted_iota(jnp.int32, (t, t), 1)
    return row >= col


def _mla_attn_fwd(q, k, v, t=512):
    _, seq, _ = q.shape
    t = min(t, seq)

    def body(q_ref, k_ref, v_ref, o_ref, ob_ref, lse_ref, m_ref, l_ref, acc_ref, s_ref):
        i = pl.program_id(1)
        qb = q_ref[...]
        m_ref[...] = jnp.full_like(m_ref, NEG_BIG)
        l_ref[...] = jnp.zeros_like(l_ref)
        acc_ref[...] = jnp.zeros_like(acc_ref)

        def scores(j):
            return _dot(qb, k_ref[pl.ds(pl.multiple_of(j * t, t), t), :], 1, 1) * MLA_SCALE_LOG2

        def softmax_pv(j, s, masked):
            vb = v_ref[pl.ds(pl.multiple_of(j * t, t), t), :]
            if masked:
                s = jnp.where(_causal_mask(t), s, NEG_BIG)
            m_old = m_ref[...]
            m_new = jnp.maximum(m_old, jnp.max(s, axis=1, keepdims=True))
            p = jnp.exp2(s - m_new)
            a = jnp.exp2(m_old - m_new)
            l_ref[...] = a * l_ref[...] + jnp.sum(p, axis=1, keepdims=True)
            acc_ref[...] = a * acc_ref[...] + _dot(p.astype(MXU_DTYPE), vb, 1, 0)
            m_ref[...] = m_new

        s_ref[...] = scores(0)

        def loop_body(j, c):
            s_next = scores(j + 1)
            softmax_pv(j, s_ref[...], False)
            s_ref[...] = s_next
            return c

        lax.fori_loop(0, i, loop_body, 0)
        softmax_pv(i, s_ref[...], True)
        l = l_ref[...]
        o = acc_ref[...] * (1.0 / l)
        o_ref[...] = o
        ob_ref[...] = o.astype(ob_ref.dtype)
        lse_ref[...] = jnp.broadcast_to(m_ref[...] + jnp.log2(l), lse_ref.shape)

    blk = pl.BlockSpec((None, t, LANES), lambda h, i: (h, i, 0))
    whole = pl.BlockSpec((None, seq, LANES), lambda h, i: (h, 0, 0))
    shp = jax.ShapeDtypeStruct((HEADS, seq, LANES), F32)
    return pl.pallas_call(
        body, name="mla_attn_fwd", grid=(HEADS, seq // t),
        in_specs=[blk, whole, whole], out_specs=[blk, blk, blk],
        out_shape=[shp, jax.ShapeDtypeStruct((HEADS, seq, LANES), MXU_DTYPE), shp],
        scratch_shapes=[pltpu.VMEM((t, 1), F32), pltpu.VMEM((t, 1), F32), pltpu.VMEM((t, LANES), F32), pltpu.VMEM((t, t), F32)],
        compiler_params=_params("parallel", "arbitrary"),
    )(q, k, v)


def _mla_attn_bwd(q, k, v, o, lse, do, t=512):
    _, seq, _ = q.shape
    t = min(t, seq)
    nb = seq // t

    def body(q_ref, k_ref, v_ref, o_ref, lse_ref, do_ref, dq_ref, dk_ref, dv_ref, dl_ref, dka_ref, dva_ref):
        dq_ref[...] = jnp.zeros_like(dq_ref)

        def delta_body(i, c):
            rows = pl.ds(pl.multiple_of(i * t, t), t)
            dl_ref[rows, :] = jnp.sum(do_ref[rows, :] * o_ref[rows, :], axis=1, keepdims=True)
            return c

        lax.fori_loop(0, nb, delta_body, 0)

        def kblock(j, c):
            krows = pl.ds(pl.multiple_of(j * t, t), t)
            kb = k_ref[krows, :]
            vb = v_ref[krows, :]
            dka_ref[...] = jnp.zeros_like(dka_ref)
            dva_ref[...] = jnp.zeros_like(dva_ref)

            def qstep(i, masked):
                rows = pl.ds(pl.multiple_of(i * t, t), t)
                qb = q_ref[rows, :]
                dob = do_ref[rows, :].astype(MXU_DTYPE)
                s = _dot(qb, kb, 1, 1) * MLA_SCALE_LOG2
                if masked:
                    s = jnp.where(_causal_mask(t), s, NEG_BIG)
                p = jnp.exp2(s - lse_ref[rows, 0:1])
                dva_ref[...] += _dot(p.astype(MXU_DTYPE), dob, 0, 0)
                dp = _dot(dob, vb, 1, 1)
                ds = (p * (dp - dl_ref[rows, :]) * MLA_SCALE).astype(MXU_DTYPE)
                dka_ref[...] += _dot(ds, qb, 0, 0)
                dq_ref[rows, :] += _dot(ds, kb, 1, 0)

            qstep(j, True)

            def qloop(i, c2):
                qstep(i, False)
                return c2

            lax.fori_loop(j + 1, nb, qloop, 0)
            dk_ref[krows, :] = dka_ref[...]
            dv_ref[krows, :] = dva_ref[...]
            return c

        lax.fori_loop(0, nb, kblock, 0)

    whole = pl.BlockSpec((None, seq, LANES), lambda h: (h, 0, 0))
    shp = jax.ShapeDtypeStruct((HEADS, seq, LANES), F32)
    return pl.pallas_call(
        body, name="mla_attn_bwd", grid=(HEADS,),
        in_specs=[whole] * 6, out_specs=[whole] * 3, out_shape=[shp] * 3,
        scratch_shapes=[pltpu.VMEM((seq, 1), F32), pltpu.VMEM((t, LANES), F32), pltpu.VMEM((t, LANES), F32)],
        compiler_params=_params("parallel"),
    )(q, k, v, o, lse, do)


DIL_CHUNK = DIL_BLOCK * max(d for _, d in DIL_PAIRS)
DIL_PAIR_LANES = 2 * HEAD_DIM
assert DIL_PAIR_LANES == LANES
DIL_UNROLL_FWD = 4
DIL_UNROLL_BWD = 4


def _dil_bias_tables(hp, dil):
    b = DIL_BLOCK
    iq = lax.broadcasted_iota(jnp.int32, (b, 2 * b), 0)
    ik = lax.broadcasted_iota(jnp.int32, (b, 2 * b), 1)
    off = iq + b - ik
    band = (off >= 0) & (off <= b)
    dist = (off * dil).astype(F32)
    tables = []
    for hh in range(2):
        slope = jnp.where(hp == 0, ALIBI_SLOPES[hh], jnp.where(hp == 1, ALIBI_SLOPES[2 + hh],
                          jnp.where(hp == 2, ALIBI_SLOPES[4 + hh], ALIBI_SLOPES[6 + hh]))).astype(F32)
        bias = -slope * dist
        tables.append((jnp.where(band, bias, NEG_BIG), jnp.where(band & (ik >= b), bias, NEG_BIG)))
    return tables


def _dil_rows(start, dil):
    return pl.ds(start, DIL_BLOCK) if dil == 1 else pl.ds(start, DIL_BLOCK, stride=dil)


def _dil_block_pos(blk, c, dil):
    sc, r = blk // dil, blk % dil
    q0 = sc * (DIL_BLOCK * dil) + r
    kcur0 = c * DIL_CHUNK + q0
    first = kcur0 < DIL_BLOCK * dil
    kprev0 = jnp.where(first, kcur0, kcur0 - DIL_BLOCK * dil)
    return q0, kcur0, kprev0, first


def _pair_cols(hh):
    return slice(HEAD_DIM * hh, HEAD_DIM * (hh + 1))


def _first_head_lanes(shape):
    return lax.broadcasted_iota(jnp.int32, shape, 1) < HEAD_DIM


def _split_pair(t):
    first = _first_head_lanes(t.shape)
    return jnp.where(first, t, 0.0).astype(MXU_DTYPE), jnp.where(first, 0.0, t).astype(MXU_DTYPE)


def _join_pair(ts):
    return jnp.where(_first_head_lanes(ts[0].shape), ts[0], ts[1])


def _dil_fwd(h):
    seq = h.shape[0]
    assert seq % DIL_CHUNK == 0
    nblk = DIL_CHUNK // DIL_BLOCK
    rc = 256

    def body(q_ref, k_ref, v_ref, o_ref, ob_ref, lse_ref, *scr):
        o_scr, l_scr = scr[:3], scr[3:]
        hp, c = pl.program_id(0), pl.program_id(1)
        for bi, (_, dil) in enumerate(DIL_PAIRS):
            tables = _dil_bias_tables(hp, dil)

            def block(blk, carry, bi=bi, dil=dil, tables=tables):
                q0, kcur0, kprev0, first = _dil_block_pos(blk, c, dil)
                qs = _split_pair(q_ref[_dil_rows(q0, dil), :] * DIL_SCALE)
                kcat = jnp.concatenate([k_ref[_dil_rows(kprev0, dil), :], k_ref[_dil_rows(kcur0, dil), :]], axis=0).astype(MXU_DTYPE)
                vcat = jnp.concatenate([v_ref[_dil_rows(kprev0, dil), :], v_ref[_dil_rows(kcur0, dil), :]], axis=0).astype(MXU_DTYPE)
                outs, lses = [], []
                for hh in range(2):
                    s = _dot(qs[hh], kcat, 1, 1) + jnp.where(first, tables[hh][1], tables[hh][0])
                    mx = jnp.max(s, axis=1, keepdims=True)
                    p = jnp.exp(s - mx)
                    l = jnp.sum(p, axis=1, keepdims=True)
                    outs.append(_dot(p.astype(MXU_DTYPE), vcat, 1, 0) * (1.0 / l))
                    lses.append(jnp.broadcast_to(mx + jnp.log(l), (DIL_BLOCK, LANES)))
                o_scr[bi][_dil_rows(q0, dil), :] = _join_pair(outs)
                l_scr[bi][_dil_rows(q0, dil), :] = _join_pair(lses)
                return carry

            lax.fori_loop(0, nblk, block, 0, unroll=DIL_UNROLL_FWD)

        def combine(i, carry):
            rows = pl.ds(pl.multiple_of(i * rc, rc), rc)
            ls = [l_scr[bi][rows, :] for bi in range(3)]
            mx = jnp.maximum(jnp.maximum(ls[0], ls[1]), ls[2])
            es = [jnp.exp(l - mx) for l in ls]
            den = es[0] + es[1] + es[2]
            o = (es[0] * o_scr[0][rows, :] + es[1] * o_scr[1][rows, :] + es[2] * o_scr[2][rows, :]) / den
            o_ref[rows, :] = o
            ob_ref[rows, :] = o.astype(ob_ref.dtype)
            lse_ref[rows, :] = mx + jnp.log(den)
            return carry

        lax.fori_loop(0, DIL_CHUNK // rc, combine, 0)

    nq = DIL_WIDTH // LANES
    chunk = lambda off: pl.BlockSpec((DIL_CHUNK, LANES), lambda hp, c: (c, off + hp))
    whole = lambda off: pl.BlockSpec((seq, LANES), lambda hp, c: (0, off + hp))
    shp = jax.ShapeDtypeStruct((seq, DIL_WIDTH), F32)
    return pl.pallas_call(
        body, name="dil_fwd", grid=(nq, seq // DIL_CHUNK),
        in_specs=[chunk(nq), whole(2 * nq), whole(3 * nq)], out_specs=[chunk(0), chunk(0), chunk(0)],
        out_shape=[shp, jax.ShapeDtypeStruct((seq, DIL_WIDTH), MXU_DTYPE), shp],
        scratch_shapes=[pltpu.VMEM((DIL_CHUNK, LANES), F32)] * 6,
        compiler_params=_params("parallel", "arbitrary"),
    )(h, h, h)


def _dil_bwd(h, o, lse, do):
    seq = h.shape[0]
    nblk = DIL_CHUNK // DIL_BLOCK
    rc = 256

    def body(q_ref, k_ref, v_ref, o_ref, lse_ref, do_ref, dq_ref, dk_ref, dv_ref, dl_scr):
        hp, c = pl.program_id(0), pl.program_id(1)

        @pl.when(c == 0)
        def _():
            dk_ref[...] = jnp.zeros_like(dk_ref)
            dv_ref[...] = jnp.zeros_like(dv_ref)

        def delta(i, carry):
            rows = pl.ds(pl.multiple_of(i * rc, rc), rc)
            prod = do_ref[rows, :] * o_ref[rows, :]
            dl_scr[rows, :] = jnp.concatenate(
                [jnp.broadcast_to(jnp.sum(prod[:, _pair_cols(hh)], axis=1, keepdims=True), (rc, HEAD_DIM)) for hh in range(2)], axis=1)
            return carry

        lax.fori_loop(0, DIL_CHUNK // rc, delta, 0)

        for bi, (_, dil) in enumerate(DIL_PAIRS):
            tables = _dil_bias_tables(hp, dil)

            def block(blk, carry, bi=bi, dil=dil, tables=tables):
                q0, kcur0, kprev0, first = _dil_block_pos(blk, c, dil)
                qrows = _dil_rows(q0, dil)
                qs = _split_pair(q_ref[qrows, :] * DIL_SCALE)
                kcat = jnp.concatenate([k_ref[_dil_rows(kprev0, dil), :], k_ref[_dil_rows(kcur0, dil), :]], axis=0).astype(MXU_DTYPE)
                vcat = jnp.concatenate([v_ref[_dil_rows(kprev0, dil), :], v_ref[_dil_rows(kcur0, dil), :]], axis=0).astype(MXU_DTYPE)
                dos = _split_pair(do_ref[qrows, :])
                lse_b = lse_ref[qrows, :]
                dl_b = dl_scr[qrows, :]
                dqs = []
                dk_b = dv_b = None
                for hh in range(2):
                    lane0 = HEAD_DIM * hh
                    s = _dot(qs[hh], kcat, 1, 1) + jnp.where(first, tables[hh][1], tables[hh][0])
                    p = jnp.exp(s - lse_b[:, lane0:lane0 + 1])
                    dp = _dot(dos[hh], vcat, 1, 1)
                    ds = (p * (dp - dl_b[:, lane0:lane0 + 1])).astype(MXU_DTYPE)
                    dqs.append(_dot(ds, kcat, 1, 0))
                    dk_h = _dot(ds, qs[hh], 0, 0)
                    dv_h = _dot(p.astype(MXU_DTYPE), dos[hh], 0, 0)
                    dk_b = dk_h if dk_b is None else dk_b + dk_h
                    dv_b = dv_h if dv_b is None else dv_b + dv_h
                dq_b = _join_pair(dqs) * DIL_SCALE
                if bi == 0:
                    dq_ref[qrows, :] = dq_b
                else:
                    dq_ref[qrows, :] += dq_b
                dk_ref[_dil_rows(kprev0, dil), :] += dk_b[:DIL_BLOCK]
                dv_ref[_dil_rows(kprev0, dil), :] += dv_b[:DIL_BLOCK]
                dk_ref[_dil_rows(kcur0, dil), :] += dk_b[DIL_BLOCK:]
                dv_ref[_dil_rows(kcur0, dil), :] += dv_b[DIL_BLOCK:]
                return carry

            lax.fori_loop(0, nblk, block, 0, unroll=DIL_UNROLL_BWD)

    nq = DIL_WIDTH // LANES
    chunk = lambda off: pl.BlockSpec((DIL_CHUNK, LANES), lambda hp, c: (c, off + hp))
    whole = lambda off: pl.BlockSpec((seq, LANES), lambda hp, c: (0, off + hp))
    shp = jax.ShapeDtypeStruct((seq, DIL_WIDTH), F32)
    return pl.pallas_call(
        body, name="dil_bwd", grid=(nq, seq // DIL_CHUNK),
        in_specs=[chunk(nq), whole(2 * nq), whole(3 * nq), chunk(0), chunk(0), chunk(0)],
        out_specs=[chunk(0), whole(0), whole(0)], out_shape=[shp, shp, shp],
        scratch_shapes=[pltpu.VMEM((DIL_CHUNK, LANES), F32)],
        compiler_params=_params("parallel", "arbitrary"),
    )(h, h, h, o, lse, do)


def _assemble_dh(dh_mla, dq, dk, dv, tm=512):
    seq = dh_mla.shape[0]
    tm = min(tm, seq)

    def body(a_ref, q_ref, k_ref, v_ref, o_ref):
        for j, r in enumerate((a_ref, q_ref, k_ref, v_ref)):
            o_ref[:, 512 * j:512 * (j + 1)] = r[...].astype(o_ref.dtype)

    blk = pl.BlockSpec((tm, 512), lambda i: (i, 0))
    return pl.pallas_call(
        body, name="assemble_dh", grid=(seq // tm,), in_specs=[blk] * 4,
        out_specs=pl.BlockSpec((tm, IN_PAD), lambda i: (i, 0)), out_shape=jax.ShapeDtypeStruct((seq, IN_PAD), MXU_DTYPE),
        compiler_params=_params("parallel"),
    )(dh_mla, dq, dk, dv)


def _ln_stats(z):
    mu = jnp.mean(z, axis=-1, keepdims=True)
    zc = z - mu
    r = lax.rsqrt(jnp.mean(zc * zc, axis=-1, keepdims=True) + LN_EPS)
    return zc * r, r


def _ln_bwd_math(dy, xh, r, g):
    dxh = dy * g
    return r * (dxh - jnp.mean(dxh, axis=-1, keepdims=True) - xh * jnp.mean(dxh * xh, axis=-1, keepdims=True))


def _mix_ln1(o_mla, o_dil, w_o_mla, w_o_dil, x0, g, b, tm=512):
    seq, d = x0.shape
    tm = min(tm, seq)

    def body(om_ref, od_ref, wm_ref, wd_ref, x_ref, g_ref, b_ref, z_ref, y_ref, yb_ref):
        mix = _dot(od_ref[...], wd_ref[...], 1, 0)
        for hd in range(HEADS):
            mix += _dot(om_ref[hd], wm_ref[LANES * hd:LANES * (hd + 1), :], 1, 0)
        z = DN_ALPHA * x_ref[...] + mix
        xh, _ = _ln_stats(z)
        y = xh * g_ref[...] + b_ref[...]
        z_ref[...] = z
        y_ref[...] = y
        yb_ref[...] = y.astype(yb_ref.dtype)

    blk = pl.BlockSpec((tm, d), lambda i: (i, 0))
    vec = pl.BlockSpec((1, d), lambda i: (0, 0))
    shp = jax.ShapeDtypeStruct((seq, d), F32)
    return pl.pallas_call(
        body, name="mix_ln1", grid=(seq // tm,),
        in_specs=[pl.BlockSpec((HEADS, tm, LANES), lambda i: (0, i, 0)), pl.BlockSpec((tm, DIL_WIDTH), lambda i: (i, 0)),
                  pl.BlockSpec((HEADS * LANES, d), lambda i: (0, 0)), pl.BlockSpec((DIL_WIDTH, d), lambda i: (0, 0)), blk, vec, vec],
        out_specs=[blk, blk, blk], out_shape=[shp, shp, jax.ShapeDtypeStruct((seq, d), MXU_DTYPE)],
        compiler_params=_params("parallel"))(o_mla, o_dil, w_o_mla, w_o_dil, x0, g, b)


def _ln_bwd(dy, z, g, name, tm=512, after=()):
    seq, d = z.shape
    tm = min(tm, seq)
    n_after = len(after)

    def body(dy_ref, z_ref, g_ref, *rest):
        dz_ref, dzb_ref, dg_ref, db_ref = rest[n_after:]

        @pl.when(pl.program_id(0) == 0)
        def _():
            dg_ref[...] = jnp.zeros_like(dg_ref)
            db_ref[...] = jnp.zeros_like(db_ref)

        dyb = dy_ref[...]
        xh, r = _ln_stats(z_ref[...])
        dg_ref[...] += jnp.sum(dyb * xh, axis=0, keepdims=True)
        db_ref[...] += jnp.sum(dyb, axis=0, keepdims=True)
        dz = _ln_bwd_math(dyb, xh, r, g_ref[...])
        dz_ref[...] = dz
        dzb_ref[...] = dz.astype(dzb_ref.dtype)

    blk = pl.BlockSpec((tm, d), lambda i: (i, 0))
    vec = pl.BlockSpec((1, d), lambda i: (0, 0))
    return pl.pallas_call(
        body, name=name, grid=(seq // tm,), in_specs=[blk, blk, vec] + [_ANY_SPEC] * n_after, out_specs=[blk, blk, vec, vec],
        out_shape=[jax.ShapeDtypeStruct((seq, d), F32), jax.ShapeDtypeStruct((seq, d), MXU_DTYPE),
                   jax.ShapeDtypeStruct((1, d), F32), jax.ShapeDtypeStruct((1, d), F32)],
        compiler_params=_params("arbitrary"))(dy, z, g, *after)


def _ln2_loss_bwd(x1, ffn, target, g, b, tm=512):
    seq, d = x1.shape
    tm = min(tm, seq)

    def body(x_ref, f_ref, t_ref, g_ref, b_ref, dz_ref, dzb_ref, loss_ref, dg_ref, db_ref):
        @pl.when(pl.program_id(0) == 0)
        def _():
            loss_ref[...] = jnp.zeros_like(loss_ref)
            dg_ref[...] = jnp.zeros_like(dg_ref)
            db_ref[...] = jnp.zeros_like(db_ref)

        gv = g_ref[...]
        z = DN_ALPHA * x_ref[...] + f_ref[...]
        xh, r = _ln_stats(z)
        err = (xh * gv + b_ref[...]) - t_ref[...]
        loss_ref[...] += 0.5 * jnp.sum(jnp.mean(err * err, axis=-1, keepdims=True), axis=0, keepdims=True)
        dy = err * (1.0 / d)
        dg_ref[...] += jnp.sum(dy * xh, axis=0, keepdims=True)
        db_ref[...] += jnp.sum(dy, axis=0, keepdims=True)
        dz = _ln_bwd_math(dy, xh, r, gv)
        dz_ref[...] = dz
        dzb_ref[...] = dz.astype(dzb_ref.dtype)

    blk = pl.BlockSpec((tm, d), lambda i: (i, 0))
    vec = pl.BlockSpec((1, d), lambda i: (0, 0))
    return pl.pallas_call(
        body, name="ln2_loss_bwd", grid=(seq // tm,), in_specs=[blk, blk, blk, vec, vec],
        out_specs=[blk, blk, pl.BlockSpec((1, LANES), lambda i: (0, 0)), vec, vec],
        out_shape=[jax.ShapeDtypeStruct((seq, d), F32), jax.ShapeDtypeStruct((seq, d), MXU_DTYPE),
                   jax.ShapeDtypeStruct((1, LANES), F32),
                   jax.ShapeDtypeStruct((1, d), F32), jax.ShapeDtypeStruct((1, d), F32)],
        compiler_params=_params("arbitrary"))(x1, ffn, target, g, b)


HALO = 16


def _conv_rows(e, w_ref, b_ref):
    y = b_ref[...] + w_ref[0:1, :] * pltpu.roll(e, 2, 0)
    y = y + w_ref[1:2, :] * pltpu.roll(e, 1, 0)
    return y + w_ref[2:3, :] * e


_GELU_C = math.sqrt(2.0 / math.pi)
_GELU_A = 0.044715


def _gelu(x):
    return 0.5 * x * (1.0 + jnp.tanh(_GELU_C * (x + _GELU_A * (x * x * x))))


CONV_TN = 256


def _ffn_interleave(a, axis):
    shp = a.shape
    a = a.reshape(shp[:axis] + (2, D_FF // CONV_TN, CONV_TN) + shp[axis + 1:])
    return jnp.swapaxes(a, axis, axis + 1).reshape(shp)


def _ffn_deinterleave(a, axis):
    shp = a.shape
    a = a.reshape(shp[:axis] + (D_FF // CONV_TN, 2, CONV_TN) + shp[axis + 1:])
    return jnp.swapaxes(a, axis, axis + 1).reshape(shp)


def _conv_gate_fwd(u, conv_w, conv_b, tm=1024):
    seq = u.shape[0]
    tm = min(tm, seq)
    tn = CONV_TN

    def body(u_ref, up_ref, w_ref, b_ref, o_ref):
        first = pl.program_id(0) == 0
        e = jnp.concatenate([jnp.where(first, 0.0, up_ref[...]), u_ref[...]], axis=0)
        y = _conv_rows(e, w_ref, b_ref)[HALO:]
        o_ref[...] = (_gelu(y[:, tn:]) * y[:, :tn]).astype(o_ref.dtype)

    hb = tm // HALO
    return pl.pallas_call(
        body, name="conv_gate_fwd", grid=(seq // tm, D_FF // tn),
        in_specs=[pl.BlockSpec((tm, 2 * tn), lambda i, j: (i, j)),
                  pl.BlockSpec((HALO, 2 * tn), lambda i, j: (jnp.maximum(i * hb - 1, 0), j)),
                  pl.BlockSpec((3, 2 * tn), lambda i, j: (0, j)), pl.BlockSpec((1, 2 * tn), lambda i, j: (0, j))],
        out_specs=pl.BlockSpec((tm, tn), lambda i, j: (i, j)), out_shape=jax.ShapeDtypeStruct((seq, D_FF), MXU_DTYPE),
        compiler_params=_params("parallel", "parallel"),
    )(u, u, conv_w, conv_b)


def _conv_gate_bwd(u, d_act, conv_w, conv_b, tm=512):
    seq = u.shape[0]
    tm = min(tm, seq)
    tn = CONV_TN
    ni = seq // tm
    rows_e = tm + 2 * HALO

    def body(u_ref, up_ref, un_ref, da_ref, dan_ref, w_ref, b_ref, du_ref, dw_ref, db_ref):
        i = pl.program_id(1)
        first, last = i == 0, i == ni - 1

        @pl.when(i == 0)
        def _():
            dw_ref[...] = jnp.zeros_like(dw_ref)
            db_ref[...] = jnp.zeros_like(db_ref)

        e = jnp.concatenate([jnp.where(first, 0.0, up_ref[...]), u_ref[...], jnp.where(last, 0.0, un_ref[...])], axis=0)
        y = _conv_rows(e, w_ref, b_ref)
        ya, yg = y[:, :tn], y[:, tn:]
        dact = jnp.concatenate([jnp.zeros((HALO, tn), F32), da_ref[...].astype(F32),
                                jnp.where(last, 0.0, dan_ref[...].astype(F32))], axis=0)
        th = jnp.tanh(_GELU_C * (yg + _GELU_A * (yg * yg * yg)))
        gelu = 0.5 * yg * (1.0 + th)
        gelu_grad = 0.5 * (1.0 + th) + 0.5 * yg * (1.0 - th * th) * (_GELU_C * (1.0 + 3.0 * _GELU_A * (yg * yg)))
        dy = jnp.concatenate([dact * gelu, dact * ya * gelu_grad], axis=1)
        du = w_ref[2:3, :] * dy + w_ref[1:2, :] * pltpu.roll(dy, rows_e - 1, 0) + w_ref[0:1, :] * pltpu.roll(dy, rows_e - 2, 0)
        du_ref[...] = du[HALO:HALO + tm].astype(du_ref.dtype)
        dyt = dy[HALO:HALO + tm]
        dw_ref[0:1, :] += jnp.sum(dyt * pltpu.roll(e, 2, 0)[HALO:HALO + tm], axis=0, keepdims=True)
        dw_ref[1:2, :] += jnp.sum(dyt * pltpu.roll(e, 1, 0)[HALO:HALO + tm], axis=0, keepdims=True)
        dw_ref[2:3, :] += jnp.sum(dyt * e[HALO:HALO + tm], axis=0, keepdims=True)
        db_ref[...] += jnp.sum(dyt, axis=0, keepdims=True)

    hb = tm // HALO
    nh = seq // HALO
    prev = lambda j, i: (jnp.maximum(i * hb - 1, 0), j)
    nxt = lambda j, i: (jnp.minimum((i + 1) * hb, nh - 1), j)
    return pl.pallas_call(
        body, name="conv_gate_bwd", grid=(D_FF // tn, ni),
        in_specs=[pl.BlockSpec((tm, 2 * tn), lambda j, i: (i, j)), pl.BlockSpec((HALO, 2 * tn), prev),
                  pl.BlockSpec((HALO, 2 * tn), nxt), pl.BlockSpec((tm, tn), lambda j, i: (i, j)), pl.BlockSpec((HALO, tn), nxt),
                  pl.BlockSpec((3, 2 * tn), lambda j, i: (0, j)), pl.BlockSpec((1, 2 * tn), lambda j, i: (0, j))],
        out_specs=[pl.BlockSpec((tm, 2 * tn), lambda j, i: (i, j)), pl.BlockSpec((3, 2 * tn), lambda j, i: (0, j)),
                   pl.BlockSpec((1, 2 * tn), lambda j, i: (0, j))],
        out_shape=[jax.ShapeDtypeStruct((seq, 2 * D_FF), MXU_DTYPE), jax.ShapeDtypeStruct((3, 2 * D_FF), F32),
                   jax.ShapeDtypeStruct((1, 2 * D_FF), F32)],
        compiler_params=_params("parallel", "arbitrary"),
    )(u, u, u, d_act, d_act, conv_w, conv_b)


def _pad_heads(w, width):
    w = jnp.transpose(w, (1, 0, 2))
    return jnp.pad(w, ((0, 0), (0, 0), (0, LANES - width))).astype(MXU_DTYPE)


def _unpad_heads(d, width):
    return jnp.transpose(d[:, :, :width], (1, 0, 2))


def _split_pad_rows(w_t):
    z = lambda n: jnp.zeros((n, w_t.shape[1]), w_t.dtype)
    return jnp.concatenate([w_t[:384], z(64), w_t[384:416], z(32), w_t[416:]], axis=0)


def _split_unpad_rows(w_p):
    return jnp.concatenate([w_p[:384], w_p[448:480], w_p[512:]], axis=0)


def _pad_w_o(w_o):
    mla = jnp.pad(w_o[:512].reshape(HEADS, HEAD_DIM, D_MODEL), ((0, 0), (0, LANES - HEAD_DIM), (0, 0)))
    return mla.reshape(HEADS * LANES, D_MODEL).astype(MXU_DTYPE), w_o[512:].astype(MXU_DTYPE)


def _unpad_w_o(d_mla, d_dil):
    return jnp.concatenate([d_mla.reshape(HEADS, LANES, D_MODEL)[:, :HEAD_DIM].reshape(512, D_MODEL), d_dil], axis=0)


def _row(v):
    return v.reshape(1, -1).astype(F32)


def _compute_weights(w):
    w_o_mla, w_o_dil = _pad_w_o(w["w_o"])
    return dict(
        w_in_t=_split_pad_rows(w["w_in"].T).astype(MXU_DTYPE), wq=_pad_heads(w["w_uq"], NOPE_DIM + ROPE_DIM),
        wk=_pad_heads(w["w_uk"], NOPE_DIM), wv=_pad_heads(w["w_uv"], HEAD_DIM), w_o_mla=w_o_mla, w_o_dil=w_o_dil,
        w_up_t=_ffn_interleave(w["w_up"].T, 0).astype(MXU_DTYPE), w_down=w["w_down"].astype(MXU_DTYPE),
        conv_w=_ffn_interleave(w["conv_w"].astype(F32), 1),
        g_cq=_row(w["g_cq"]), g_ckv=_row(w["g_ckv"]), ln1_g=_row(w["ln1_g"]), ln1_b=_row(w["ln1_b"]),
        conv_b=_ffn_interleave(_row(w["conv_b"]), 1), ln2_g=_row(w["ln2_g"]), ln2_b=_row(w["ln2_b"]))


def _natural_grads(g):
    return dict(
        w_in=_split_unpad_rows(g["w_in_t"]).T, g_cq=g["g_cq"].reshape(-1), g_ckv=g["g_ckv"].reshape(-1),
        w_uq=_unpad_heads(g["wq"], NOPE_DIM + ROPE_DIM), w_uk=_unpad_heads(g["wk"], NOPE_DIM),
        w_uv=_unpad_heads(g["wv"], HEAD_DIM), w_o=_unpad_w_o(g["w_o_mla"], g["w_o_dil"]), ln1_g=g["ln1_g"].reshape(-1),
        ln1_b=g["ln1_b"].reshape(-1), w_up=_ffn_deinterleave(g["w_up_t"], 0).T, conv_w=_ffn_deinterleave(g["conv_w"], 1),
        conv_b=_ffn_deinterleave(g["conv_b"], 1).reshape(-1),
        w_down=g["w_down"], ln2_g=g["ln2_g"].reshape(-1), ln2_b=g["ln2_b"].reshape(-1))


def _layer_grads(x0, target, cw, first_after=(), late_weights=None, on_grads=None):
    seq = x0.shape[0]
    ctab, stab = _rope_tables(seq)
    gq, gk = cw["g_cq"], cw["g_ckv"]
    wq, wk, wv = cw["wq"], cw["wk"], cw["wv"]
    x0b = x0.astype(MXU_DTYPE)
    notify = (lambda stage, grads: ()) if on_grads is None else on_grads

    h = _mm(x0b, cw["w_in_t"], name="mm_h", tb=True, tm=1024, tn=1024, tk=1024, after=first_after)
    qf, kf, vp = _mla_prep(h, gq, gk, wq, wk, wv, ctab, stab)
    o_mla, o_mla_b, lse_mla = _mla_attn_fwd(qf, kf, vp)
    o_dil, o_dil_b, lse_dil = _dil_fwd(h)
    if late_weights is not None:
        cw = {**cw, **late_weights(o_mla_b)}
    cb = cw["conv_b"]
    z1, x1, x1b = _mix_ln1(o_mla_b, o_dil_b, cw["w_o_mla"], cw["w_o_dil"], x0, cw["ln1_g"], cw["ln1_b"])
    u = _mm(x1b, cw["w_up_t"], name="mm_up", tb=True, tm=1024, tn=512, tk=1024)
    act = _conv_gate_fwd(u, cw["conv_w"], cb)
    ffn = _mm(act, cw["w_down"], name="mm_down", tm=1024, tn=1024, tk=2816)
    dz2, dz2b, loss, d_ln2_g, d_ln2_b = _ln2_loss_bwd(x1, ffn, target, cw["ln2_g"], cw["ln2_b"])

    d_act = _mm(dz2b, cw["w_down"], name="mm_d_act", tb=True, out_dtype=MXU_DTYPE, tm=1024, tn=1408, tk=1024)
    d_w_down = _mm(act, dz2b, name="mm_dw_down", ta=True, out_dtype=MXU_DTYPE, tm=1408, tn=1024, tk=1024)
    du, d_conv_w, d_conv_b = _conv_gate_bwd(u, d_act, cw["conv_w"], cb)
    dx1 = _mm(du, cw["w_up_t"], name="mm_dx1", res=dz2, res_scale=DN_ALPHA, tm=1024, tn=1024, tk=1408)
    d_w_up_t = _mm(du, x1b, name="mm_dw_up", ta=True, out_dtype=MXU_DTYPE, tm=1408, tn=1024, tk=1024)
    grads = dict(w_up_t=d_w_up_t, w_down=d_w_down, conv_w=d_conv_w, conv_b=d_conv_b, ln2_g=d_ln2_g, ln2_b=d_ln2_b)
    dz1, dz1b, d_ln1_g, d_ln1_b = _ln_bwd(dx1, z1, cw["ln1_g"], "ln1_bwd", after=notify("ffn", grads))
    do_mla = _mm_do_mla(dz1b, cw["w_o_mla"])
    do_dil = _mm(dz1b, cw["w_o_dil"], name="mm_do_dil", tb=True, tm=1024, tn=512, tk=1024)
    d_w_o_mla = _mm_dw_o_mla(o_mla_b, dz1b)
    d_w_o_dil = _mm(o_dil_b, dz1b, name="mm_dw_o_dil", ta=True, out_dtype=MXU_DTYPE, tm=512, tn=1024, tk=1024)
    grads.update(w_o_mla=d_w_o_mla, w_o_dil=d_w_o_dil, ln1_g=d_ln1_g, ln1_b=d_ln1_b)
    dq_dil, dk_dil, dv_dil = _dil_bwd(h, o_dil, lse_dil, do_dil)
    dqf, dkf, dvf = _mla_attn_bwd(qf, kf, vp, o_mla, lse_mla, do_mla)
    dh_mla, d_wq, d_wk, d_wv, d_gq, d_gk = _mla_prep_bwd(h, gq, gk, wq, wk, wv, ctab, stab, dqf, dkf, dvf,
                                                          after=notify("w_o", grads))
    dh = _assemble_dh(dh_mla, dq_dil, dk_dil, dv_dil)
    d_w_in_t = _mm(dh, x0b, name="mm_dw_in", ta=True, out_dtype=MXU_DTYPE, tm=1024, tn=1024, tk=1024)
    grads.update(w_in_t=d_w_in_t, wq=d_wq, wk=d_wk, wv=d_wv, g_cq=d_gq, g_ckv=d_gk, loss=loss)
    grad_x = _mm(dh, cw["w_in_t"], name="mm_dx0", res=dz1, res_scale=DN_ALPHA, tm=1024, tn=1024, tk=2048,
                 after=notify("rest", grads))
    return loss, grad_x, grads


def _all_gather(blocks, name):
    na = len(blocks)

    def body(*refs):
        ins, outs = refs[:na], refs[na:2 * na]
        send_sems, recv_sems, local_sems = refs[2 * na:]
        x, y, c = lax.axis_index("x"), lax.axis_index("y"), lax.axis_index("c")
        me, sibling = (x, y, c), (x, y, 1 - c)
        chips = [(1 - x, y), (x, 1 - y), (1 - x, 1 - y)]

        def slot(out, pos):
            return out.at[4 * pos[0] + 2 * pos[1] + pos[2]]

        def copy(a, k, block, to, src=None):
            return pltpu.make_async_remote_copy(
                src_ref=slot(outs[a], block) if src is None else src, dst_ref=slot(outs[a], block),
                send_sem=send_sems.at[7 * a + k], recv_sem=recv_sems.at[7 * a + k],
                device_id=to, device_id_type=pl.DeviceIdType.MESH)

        mine = [pltpu.make_async_copy(ins[a], slot(outs[a], me), local_sems.at[a]) for a in range(na)]
        for cp in mine:
            cp.start()
        first = []
        for a in range(na):
            first.append(copy(a, 0, me, sibling, src=ins[a]))
            first += [copy(a, 1 + j, me, (*chip, c), src=ins[a]) for j, chip in enumerate(chips)]
        for cp in first:
            cp.start()
        passed = []
        for j, chip in enumerate(chips):
            for a in range(na):
                copy(a, 1 + j, (*chip, c), me).wait_recv()
                cp = copy(a, 4 + j, (*chip, c), sibling)
                cp.start()
                passed.append(cp)
        for a in range(na):
            copy(a, 0, sibling, me).wait_recv()
            for j, chip in enumerate(chips):
                copy(a, 4 + j, (*chip, 1 - c), me).wait_recv()
        for cp in first + passed:
            cp.wait_send()
        for cp in mine:
            cp.wait()

    any_spec = pl.BlockSpec(memory_space=pl.ANY)
    return pl.pallas_call(
        body, name=name, in_specs=[any_spec] * na, out_specs=[any_spec] * na,
        out_shape=[jax.ShapeDtypeStruct((N_DEV,) + b.shape, b.dtype) for b in blocks],
        scratch_shapes=[pltpu.SemaphoreType.DMA((7 * na,)), pltpu.SemaphoreType.DMA((7 * na,)), pltpu.SemaphoreType.DMA((na,))],
    )(*blocks)


_HBM_SPEC = pl.BlockSpec(memory_space=pltpu.HBM)
_SEM_SPEC = pl.BlockSpec(memory_space=pltpu.SEMAPHORE)
_DATAFLOW = pltpu.CompilerParams(has_side_effects=pltpu.SideEffectType.DATAFLOW_SIDE_EFFECTING)


def _split_copies(ins, lands, send_sems, recv_sems, gather):
    x, y, c = lax.axis_index("x"), lax.axis_index("y"), lax.axis_index("c")
    me = 4 * x + 2 * y + c
    copies = []
    for a in range(len(ins)):
        for d in range(1, N_DEV):
            px, py, pc = x ^ (d >> 2), y ^ ((d >> 1) & 1), c ^ (d & 1)
            copies.append(pltpu.make_async_remote_copy(
                src_ref=ins[a] if gather else ins[a].at[4 * px + 2 * py + pc], dst_ref=lands[a].at[me],
                send_sem=send_sems.at[7 * a + d - 1], recv_sem=recv_sems.at[7 * a + d - 1],
                device_id=(px, py, pc), device_id_type=pl.DeviceIdType.MESH))
    return copies


def _send_start(srcs, gather, name):
    na = len(srcs)
    land_types = [pltpu.HBM(((N_DEV,) + s.shape) if gather else s.shape, s.dtype) for s in srcs]

    def body(*refs):
        ins, lands = refs[:na], refs[na:2 * na]
        send_sems, recv_sems, token = refs[2 * na], refs[2 * na + 1], refs[-1]
        for cp in _split_copies(ins, lands, send_sems, recv_sems, gather):
            cp.start()
        token[...] = jnp.zeros_like(token)

    hbm = lambda a: pltpu.with_memory_space_constraint(a, pltpu.HBM)
    outs = pl.pallas_call(
        body, name=name,
        out_shape=(pltpu.SemaphoreType.DMA((7 * na,)), pltpu.SemaphoreType.DMA((7 * na,)),
                   *[pltpu.HBM(s.shape, s.dtype) for s in srcs], *land_types, jax.ShapeDtypeStruct((8, LANES), F32)),
        in_specs=[_HBM_SPEC] * (2 * na),
        out_specs=(_SEM_SPEC, _SEM_SPEC, *[_HBM_SPEC] * (2 * na), pl.BlockSpec(memory_space=pltpu.VMEM)),
        input_output_aliases={i: 2 + i for i in range(2 * na)}, compiler_params=_DATAFLOW,
    )(*[hbm(s) for s in srcs], *[hbm(lax.empty(t.shape, t.dtype)) for t in land_types])
    return dict(send=outs[0], recv=outs[1], srcs=list(outs[2:2 + na]), lands=list(outs[2 + na:2 + 2 * na]), token=outs[-1],
                gather=gather)


def _send_wait(handle, after, name):
    na = len(handle["srcs"])
    gather = handle["gather"]

    def body(*refs):
        ins, lands = refs[:na], refs[na:2 * na]
        send_sems, recv_sems = refs[2 * na], refs[2 * na + 1]
        for cp in _split_copies(ins, lands, send_sems, recv_sems, gather):
            cp.wait_send()
            cp.wait_recv()

    both = handle["srcs"] + handle["lands"]
    outs = pl.pallas_call(
        body, name=name, out_shape=[pltpu.HBM(a.shape, a.dtype) for a in both],
        in_specs=[_HBM_SPEC] * (2 * na) + [_SEM_SPEC, _SEM_SPEC, pl.BlockSpec(memory_space=pl.ANY)],
        out_specs=[_HBM_SPEC] * (2 * na), input_output_aliases={i: i for i in range(2 * na)}, compiler_params=_DATAFLOW,
    )(*both, handle["send"], handle["recv"], after)
    return list(outs[:na]), list(outs[na:])


def _sum_parts(parts, name):
    npart, r, n = parts.shape
    tr = r if r <= 256 else max(t for t in range(16, 257, 16) if r % t == 0)

    def body(p_ref, o_ref):
        g = p_ref[0].astype(F32)
        for s in range(1, npart):
            g = g + p_ref[s].astype(F32)
        o_ref[...] = g

    return pl.pallas_call(
        body, name=name, grid=(r // tr,), in_specs=[pl.BlockSpec((npart, tr, n), lambda i: (0, i, 0))],
        out_specs=pl.BlockSpec((tr, n), lambda i: (i, 0)), out_shape=jax.ShapeDtypeStruct((r, n), F32),
        compiler_params=_params("parallel"),
    )(parts)


def _adamw(parts, w, m, v, name):
    npart, r, n = parts.shape
    tr = r if r <= 256 else max(t for t in range(16, 257, 16) if r % t == 0)
    c1 = 1.0 - ADAM_B1 ** ADAM_STEP
    c2 = 1.0 - ADAM_B2 ** ADAM_STEP

    def body(p_ref, w_ref, m_ref, v_ref, g_out, d_out, m_out, v_out):
        g = p_ref[0].astype(F32)
        for s in range(1, npart):
            g = g + p_ref[s].astype(F32)
        m_new = ADAM_B1 * m_ref[...] + (1.0 - ADAM_B1) * g
        v_new = ADAM_B2 * v_ref[...] + (1.0 - ADAM_B2) * (g * g)
        g_out[...] = g
        m_out[...] = m_new
        v_out[...] = v_new
        d_out[...] = -ADAM_LR * ((m_new / c1) / (jnp.sqrt(v_new / c2) + ADAM_EPS) + ADAM_WD * w_ref[...])

    blk = pl.BlockSpec((tr, n), lambda i: (i, 0))
    shp = jax.ShapeDtypeStruct((r, n), F32)
    return pl.pallas_call(
        body, name=name, grid=(r // tr,), in_specs=[pl.BlockSpec((npart, tr, n), lambda i: (0, i, 0)), blk, blk, blk],
        out_specs=[blk] * 4, out_shape=[shp] * 4, compiler_params=_params("parallel"),
    )(parts, w, m, v)


def _pack(arrs, dtype, row_multiple=16):
    flat = jnp.concatenate([a.reshape(-1).astype(dtype) for a in arrs])
    rows = -(-flat.shape[0] // PACK_COLS)
    rows = -(-rows // row_multiple) * row_multiple
    return jnp.pad(flat, (0, rows * PACK_COLS - flat.shape[0])).reshape(rows, PACK_COLS)


def _unpack(buf, shapes):
    lead = buf.shape[:-2]
    flat = buf.reshape(lead + (-1,))
    out, at = [], 0
    for s in shapes:
        size = math.prod(s)
        out.append(flat[..., at:at + size].reshape(lead + tuple(s)))
        at += size
    return out


REPLICATED = ("g_cq", "g_ckv", "w_uk", "w_uv", "ln1_g", "ln1_b", "conv_b", "ln2_g", "ln2_b")
ALL_WEIGHTS = ("w_in", "g_cq", "g_ckv", "w_uq", "w_uk", "w_uv", "w_o", "ln1_g", "ln1_b", "w_up", "conv_w", "conv_b",
               "w_down", "ln2_g", "ln2_b")


def kernel(x, w_in, g_cq, g_ckv, w_uq, w_uk, w_uv, w_o, ln1_g, ln1_b, w_up, conv_w, conv_b, w_down, ln2_g, ln2_b, loss_target, m_w_in, m_g_cq, m_g_ckv, m_w_uq, m_w_uk, m_w_uv, m_w_o, m_ln1_g, m_ln1_b, m_w_up, m_conv_w, m_conv_b, m_w_down, m_ln2_g, m_ln2_b, v_w_in, v_g_cq, v_g_ckv, v_w_uq, v_w_uk, v_w_uv, v_w_o, v_ln1_g, v_ln1_b, v_w_up, v_conv_w, v_conv_b, v_w_down, v_ln2_g, v_ln2_b):
    w = dict(w_in=w_in, g_cq=g_cq, g_ckv=g_ckv, w_uq=w_uq, w_uk=w_uk, w_uv=w_uv, w_o=w_o, ln1_g=ln1_g, ln1_b=ln1_b,
             w_up=w_up, conv_w=conv_w, conv_b=conv_b, w_down=w_down, ln2_g=ln2_g, ln2_b=ln2_b)
    m = dict(w_in=m_w_in, g_cq=m_g_cq, g_ckv=m_g_ckv, w_uq=m_w_uq, w_uk=m_w_uk, w_uv=m_w_uv, w_o=m_w_o, ln1_g=m_ln1_g,
             ln1_b=m_ln1_b, w_up=m_w_up, conv_w=m_conv_w, conv_b=m_conv_b, w_down=m_w_down, ln2_g=m_ln2_g, ln2_b=m_ln2_b)
    v = dict(w_in=v_w_in, g_cq=v_g_cq, g_ckv=v_g_ckv, w_uq=v_w_uq, w_uk=v_w_uk, w_uv=v_w_uv, w_o=v_w_o, ln1_g=v_ln1_g,
             ln1_b=v_ln1_b, w_up=v_w_up, conv_w=v_conv_w, conv_b=v_conv_b, w_down=v_w_down, ln2_g=v_ln2_g, ln2_b=v_ln2_b)
    me = 4 * lax.axis_index("x") + 2 * lax.axis_index("y") + lax.axis_index("c")
    wire = lambda a: a.astype(WIRE_DTYPE)
    n_in = w_in.shape[1]
    n_in_pad = -(-n_in // 16) * 16
    pad_taps = lambda a: jnp.pad(a, ((0, 8 - a.shape[0]), (0, 0)))

    own_slot = lambda buf, block: lax.dynamic_update_index_in_dim(buf, block, me, 0)
    blocks = lambda a: wire(a).reshape((N_DEV, a.shape[0] // N_DEV) + a.shape[1:])

    g_in, g_uq, g_conv = _all_gather(
        [jnp.pad(wire(w_in).T, ((0, n_in_pad - n_in), (0, 0))), wire(w_uq).reshape(w_uq.shape[0], -1), pad_taps(conv_w)],
        "gather_weights")
    late = _send_start([wire(w_o), wire(w_up).T, wire(w_down)], True, "gather_late_start")
    cw = dict(
        w_in_t=_split_pad_rows(g_in[:, :n_in].reshape(-1, D_MODEL)).astype(MXU_DTYPE),
        wq=_pad_heads(g_uq.reshape((-1,) + w_uq.shape[1:]), NOPE_DIM + ROPE_DIM),
        wk=_pad_heads(w_uk, NOPE_DIM), wv=_pad_heads(w_uv, HEAD_DIM),
        conv_w=_ffn_interleave(jnp.transpose(g_conv[:, :conv_w.shape[0]], (1, 0, 2)).reshape(conv_w.shape[0], -1), 1),
        g_cq=_row(g_cq), g_ckv=_row(g_ckv), ln1_g=_row(ln1_g), ln1_b=_row(ln1_b), conv_b=_ffn_interleave(_row(conv_b), 1),
        ln2_g=_row(ln2_g), ln2_b=_row(ln2_b))

    def late_weights(after):
        own, landed = _send_wait(late, after, "gather_late_wait")
        g_o, g_up, g_down = [own_slot(buf, blk) for buf, blk in zip(landed, own)]
        w_o_mla, w_o_dil = _pad_w_o(g_o.reshape(-1, D_MODEL))
        return dict(w_o_mla=w_o_mla, w_o_dil=w_o_dil, w_up_t=_ffn_interleave(g_up.reshape(-1, D_MODEL), 0).astype(MXU_DTYPE),
                    w_down=g_down.reshape(-1, D_MODEL).astype(MXU_DTYPE))

    sent = {}

    def on_grads(stage, g):
        if stage == "ffn":
            sent[stage] = [_send_start([blocks(_ffn_deinterleave(g["w_up_t"], 0)), blocks(g["w_down"])], False, "exchange_ffn_start")]
        elif stage == "w_o":
            sent[stage] = [_send_start([blocks(_unpad_w_o(g["w_o_mla"], g["w_o_dil"]))], False, "exchange_w_o_start")]
        else:
            d_in = jnp.pad(blocks(_split_unpad_rows(g["w_in_t"])), ((0, 0), (0, n_in_pad - n_in), (0, 0)))
            d_uq = blocks(_unpad_heads(g["wq"], NOPE_DIM + ROPE_DIM).reshape(Q_RANK, -1))
            nat = dict(g_cq=g["g_cq"], g_ckv=g["g_ckv"], w_uk=_unpad_heads(g["wk"], NOPE_DIM),
                       w_uv=_unpad_heads(g["wv"], HEAD_DIM), ln1_g=g["ln1_g"], ln1_b=g["ln1_b"],
                       conv_b=_ffn_deinterleave(g["conv_b"], 1), ln2_g=g["ln2_g"], ln2_b=g["ln2_b"])
            small = _pack([nat[n] for n in REPLICATED] + [g["loss"][:, :1]], F32, 8)
            sent[stage] = [_send_start([d_in, d_uq], False, "exchange_rest_start"),
                           _send_start([small, pad_taps(_ffn_deinterleave(g["conv_w"], 1))], True, "gather_small_start")]
        return [h["token"] for h in sent[stage]]

    _, grad_x, _ = _layer_grads(x[0], loss_target[0], cw, [late["token"]], late_weights, on_grads)

    def landed(handle, name):
        own, got = _send_wait(handle, grad_x, name)
        pick = (lambda a: a) if handle["gather"] else (lambda a: lax.dynamic_index_in_dim(a, me, 0, keepdims=False))
        return [own_slot(buf, pick(src)) for buf, src in zip(got, own)]

    r_up, r_down = landed(sent["ffn"][0], "exchange_ffn_wait")
    (r_o,) = landed(sent["w_o"][0], "exchange_w_o_wait")
    r_in, r_uq = landed(sent["rest"][0], "exchange_rest_wait")
    rep_all, cw_all = landed(sent["rest"][1], "gather_small_wait")

    out = {}

    def update(name, parts, shape2d=None):
        w2, m2, v2 = [d[name].reshape(shape2d or d[name].shape) for d in (w, m, v)]
        res = _adamw(parts, w2, m2, v2, "adamw_" + name)
        for kind, a in zip(("grad", "delta", "new_m", "new_v"), res):
            out[kind, name] = a.reshape(w[name].shape)

    update("w_in", _sum_parts(r_in, "sum_w_in")[:n_in].T[None])
    update("w_uq", r_uq, (w_uq.shape[0], -1))
    update("w_o", r_o)
    update("w_up", _sum_parts(r_up, "sum_w_up").T[None])
    update("w_down", r_down)
    slot = jnp.zeros((1,), F32)
    res = _adamw(rep_all, *[_pack([d[n] for n in REPLICATED] + [slot], F32, 8) for d in (w, m, v)], "adamw_replicated")
    for kind, buf in zip(("grad", "delta", "new_m", "new_v"), res):
        *arrs, total = _unpack(buf, [w[n].shape for n in REPLICATED] + [(1,)])
        for n, a in zip(REPLICATED, arrs):
            out[kind, n] = a
        if kind == "grad":
            loss = total[0]
    ncw = conv_w.shape[1]
    update("conv_w", lax.dynamic_slice_in_dim(cw_all[:, :conv_w.shape[0]], me * ncw, ncw, axis=2))

    return (loss, grad_x[None], *[out[kind, n] for kind in ("grad", "delta", "new_m", "new_v") for n in ALL_WEIGHTS])
```

```python
import functools
import math

import jax
import jax.numpy as jnp
from jax import lax
from jax.experimental import pallas as pl
from jax.experimental.pallas import tpu as pltpu

F32 = jnp.float32
MXU_DTYPE = jnp.bfloat16
WIRE_DTYPE = jnp.bfloat16

N_DEV = 8
D_MODEL = 1024
HEADS = 8
HEAD_DIM = 64
LANES = 128
Q_RANK, KV_RANK, ROPE_DIM, NOPE_DIM = 256, 128, 32, 64
DIL_WIDTH = HEADS * HEAD_DIM
IN_WIDTH = 1952
IN_PAD = 2048
D_FF = 2816
ROPE_THETA = 10000.0
DIL_PAIRS = ((128, 1), (512, 4), (2048, 16))
DIL_BLOCK = 128
DN_ALPHA = 2.0 ** 0.25
LN_EPS = 1e-5
RMS_EPS = 1e-6
MLA_SCALE = 1.0 / math.sqrt(NOPE_DIM + ROPE_DIM)
MLA_SCALE_LOG2 = MLA_SCALE * math.log2(math.e)
DIL_SCALE = 1.0 / math.sqrt(HEAD_DIM)
ALIBI_SLOPES = tuple(2.0 ** (-8.0 * (h + 1) / HEADS) for h in range(HEADS))
NEG_BIG = -1e30
ADAM_LR, ADAM_B1, ADAM_B2, ADAM_EPS, ADAM_WD, ADAM_STEP = 0.001, 0.9, 0.999, 1e-08, 0.01, 10
VMEM_LIMIT = 48 * 1024 * 1024
PACK_COLS = 1024


def _params(*sem):
    return pltpu.CompilerParams(dimension_semantics=sem or None, vmem_limit_bytes=VMEM_LIMIT)


def _dot(a, b, ca, cb):
    return lax.dot_general(a, b, (((ca,), (cb,)), ((), ())), preferred_element_type=F32)


_ANY_SPEC = pl.BlockSpec(memory_space=pl.ANY)


def _mm(a, b, *, name, tm, tn, tk, ta=False, tb=False, out_dtype=F32, res=None, res_scale=1.0, after=()):
    m, k = (a.shape[1], a.shape[0]) if ta else a.shape
    n = b.shape[0] if tb else b.shape[1]
    assert (b.shape[1] if tb else b.shape[0]) == k
    tm, tn, tk = min(tm, m), min(tn, n), min(tk, k)
    assert m % tm == 0 and n % tn == 0 and k % tk == 0, (name, m, n, k, tm, tn, tk)
    nk = k // tk
    a_spec = (pl.BlockSpec((tk, tm), lambda i, j, kk: (kk, i)) if ta
              else pl.BlockSpec((tm, tk), lambda i, j, kk: (i, kk)))
    b_spec = (pl.BlockSpec((tn, tk), lambda i, j, kk: (j, kk)) if tb
              else pl.BlockSpec((tk, tn), lambda i, j, kk: (kk, j)))
    o_spec = pl.BlockSpec((tm, tn), lambda i, j, kk: (i, j))
    in_specs = [a_spec, b_spec]
    args = [a, b]
    if res is not None:
        in_specs.append(o_spec)
        args.append(res)
    n_in = len(args) + len(after)
    in_specs += [_ANY_SPEC] * len(after)
    args += list(after)
    ca, cb = (0 if ta else 1), (1 if tb else 0)

    def finish(acc, r_ref, o_ref):
        if r_ref is not None:
            acc = acc + res_scale * r_ref[...]
        o_ref[...] = acc.astype(o_ref.dtype)

    def body(*refs):
        a_ref, b_ref = refs[:2]
        r_ref = refs[2] if res is not None else None
        o_ref = refs[n_in]
        part = _dot(a_ref[...].astype(MXU_DTYPE), b_ref[...].astype(MXU_DTYPE), ca, cb)
        if nk == 1:
            finish(part, r_ref, o_ref)
            return
        acc_ref = refs[-1]
        kk = pl.program_id(2)

        @pl.when(kk == 0)
        def _():
            acc_ref[...] = part

        @pl.when(kk > 0)
        def _():
            acc_ref[...] += part

        @pl.when(kk == nk - 1)
        def _():
            finish(acc_ref[...], r_ref, o_ref)

    return pl.pallas_call(
        body, name=name, grid=(m // tm, n // tn, nk), in_specs=in_specs, out_specs=o_spec,
        out_shape=jax.ShapeDtypeStruct((m, n), out_dtype),
        scratch_shapes=[pltpu.VMEM((tm, tn), F32)] if nk > 1 else [],
        compiler_params=_params("parallel", "parallel", "arbitrary"),
    )(*args)


def _mm_do_mla(dz, w_o_mla, tm=1024):
    seq, d = dz.shape
    tm = min(tm, seq)

    def body(a_ref, b_ref, o_ref):
        a = a_ref[...].astype(MXU_DTYPE)
        for hd in range(HEADS):
            o_ref[hd] = _dot(a, b_ref[LANES * hd:LANES * (hd + 1), :], 1, 1)

    return pl.pallas_call(
        body, name="mm_do_mla", grid=(seq // tm,),
        in_specs=[pl.BlockSpec((tm, d), lambda i: (i, 0)), pl.BlockSpec((HEADS * LANES, d), lambda i: (0, 0))],
        out_specs=pl.BlockSpec((HEADS, tm, LANES), lambda i: (0, i, 0)),
        out_shape=jax.ShapeDtypeStruct((HEADS, seq, LANES), F32), compiler_params=_params("parallel"),
    )(dz, w_o_mla)


def _mm_dw_o_mla(o_mla, dz, tk=1024):
    seq, d = dz.shape
    tk = min(tk, seq)
    nk = seq // tk

    def body(a_ref, b_ref, o_ref, acc_ref):
        kk = pl.program_id(0)

        @pl.when(kk == 0)
        def _():
            acc_ref[...] = jnp.zeros_like(acc_ref)

        b = b_ref[...].astype(MXU_DTYPE)
        for hd in range(HEADS):
            acc_ref[LANES * hd:LANES * (hd + 1), :] += _dot(a_ref[hd].astype(MXU_DTYPE), b, 0, 0)

        @pl.when(kk == nk - 1)
        def _():
            o_ref[...] = acc_ref[...].astype(o_ref.dtype)

    return pl.pallas_call(
        body, name="mm_dw_o_mla", grid=(nk,),
        in_specs=[pl.BlockSpec((HEADS, tk, LANES), lambda kk: (0, kk, 0)), pl.BlockSpec((tk, d), lambda kk: (kk, 0))],
        out_specs=pl.BlockSpec((HEADS * LANES, d), lambda kk: (0, 0)),
        out_shape=jax.ShapeDtypeStruct((HEADS * LANES, d), MXU_DTYPE),
        scratch_shapes=[pltpu.VMEM((HEADS * LANES, d), F32)], compiler_params=_params("arbitrary"),
    )(o_mla, dz)


def _rope_tables(seq):
    half = ROPE_DIM // 2
    freqs = ROPE_THETA ** (-jnp.arange(half, dtype=F32) / half)
    ang = jnp.arange(seq).astype(F32)[:, None] * freqs[None, :]
    cos, sin = jnp.cos(ang), jnp.sin(ang)
    one = jnp.ones((seq, NOPE_DIM), F32)
    tail = jnp.ones((seq, LANES - NOPE_DIM - ROPE_DIM), F32)
    ctab = jnp.concatenate([one, cos, cos, tail], axis=1)
    stab = jnp.concatenate([0 * one, -sin, sin, 0 * tail], axis=1)
    return ctab, stab


def _rope_swap(t):
    lane = lax.broadcasted_iota(jnp.int32, t.shape, 1)
    half = ROPE_DIM // 2
    return jnp.where(lane < NOPE_DIM + half, pltpu.roll(t, LANES - half, 1), pltpu.roll(t, half, 1))


def _rope(t, ctab, stab):
    return t * ctab + _rope_swap(t) * stab


def _rope_inv(t, ctab, stab):
    return t * ctab - _rope_swap(t) * stab


def _rms(x, g):
    r = lax.rsqrt(jnp.mean(x * x, axis=-1, keepdims=True) + RMS_EPS)
    xh = x * r
    return xh, r, xh * g


def _mla_prep(h, g_cq, g_ckv, wq, wk, wv, ctab, stab, tm=512):
    seq = h.shape[0]
    tm = min(tm, seq)

    def body(h_ref, gq_ref, gk_ref, wq_ref, wk_ref, wv_ref, c_ref, s_ref, q_out, k_out, v_out):
        hb = h_ref[...]
        ctab_, stab_ = c_ref[...], s_ref[...]
        _, _, cqn = _rms(hb[:, :Q_RANK], gq_ref[...])
        _, _, ckn = _rms(hb[:, Q_RANK:Q_RANK + KV_RANK], gk_ref[...])
        cqn = cqn.astype(MXU_DTYPE)
        ckn = ckn.astype(MXU_DTYPE)
        krr = _rope(hb[:, Q_RANK + KV_RANK:], ctab_, stab_)
        for hd in range(HEADS):
            q = _dot(cqn, wq_ref[hd], 1, 0)
            q_out[hd] = _rope(q, ctab_, stab_).astype(q_out.dtype)
            k_out[hd] = (_dot(ckn, wk_ref[hd], 1, 0) + krr).astype(k_out.dtype)
            v_out[hd] = _dot(ckn, wv_ref[hd], 1, 0).astype(v_out.dtype)

    full = lambda *shape: pl.BlockSpec(shape, lambda i: (0,) * len(shape))
    slab = pl.BlockSpec((HEADS, tm, LANES), lambda i: (0, i, 0))
    shp = jax.ShapeDtypeStruct((HEADS, seq, LANES), MXU_DTYPE)
    return pl.pallas_call(
        body, name="mla_prep", grid=(seq // tm,),
        in_specs=[pl.BlockSpec((tm, 512), lambda i: (i, 0)), full(1, Q_RANK), full(1, KV_RANK),
                  full(HEADS, Q_RANK, LANES), full(HEADS, KV_RANK, LANES), full(HEADS, KV_RANK, LANES),
                  pl.BlockSpec((tm, LANES), lambda i: (i, 0)), pl.BlockSpec((tm, LANES), lambda i: (i, 0))],
        out_specs=[slab, slab, slab], out_shape=[shp, shp, shp],
        compiler_params=_params("parallel"),
    )(h, g_cq, g_ckv, wq, wk, wv, ctab, stab)


def _mla_prep_bwd(h, g_cq, g_ckv, wq, wk, wv, ctab, stab, dq, dk, dv, tm=512, after=()):
    seq = h.shape[0]
    tm = min(tm, seq)
    n_after = len(after)

    def body(h_ref, gq_ref, gk_ref, wq_ref, wk_ref, wv_ref, c_ref, s_ref, dq_ref, dk_ref, dv_ref, *rest):
        dh_ref, dwq_ref, dwk_ref, dwv_ref, dgq_ref, dgk_ref = rest[n_after:]

        @pl.when(pl.program_id(0) == 0)
        def _():
            for r in (dwq_ref, dwk_ref, dwv_ref, dgq_ref, dgk_ref):
                r[...] = jnp.zeros_like(r)

        hb = h_ref[...]
        ctab_, stab_ = c_ref[...], s_ref[...]
        gq, gk = gq_ref[...], gk_ref[...]
        xq, rq, cqn = _rms(hb[:, :Q_RANK], gq)
        xk, rk, ckn = _rms(hb[:, Q_RANK:Q_RANK + KV_RANK], gk)
        cqn = cqn.astype(MXU_DTYPE)
        ckn = ckn.astype(MXU_DTYPE)
        d_cqn = jnp.zeros((tm, Q_RANK), F32)
        d_ckn = jnp.zeros((tm, KV_RANK), F32)
        d_krr = jnp.zeros((tm, LANES), F32)
        for hd in range(HEADS):
            dqh = _rope_inv(dq_ref[hd], ctab_, stab_).astype(MXU_DTYPE)
            d_cqn += _dot(dqh, wq_ref[hd], 1, 1)
            dwq_ref[hd] += _dot(cqn, dqh, 0, 0)
            dkh = dk_ref[hd]
            d_krr += dkh
            dkh = dkh.astype(MXU_DTYPE)
            d_ckn += _dot(dkh, wk_ref[hd], 1, 1)
            dwk_ref[hd] += _dot(ckn, dkh, 0, 0)
            dvh = dv_ref[hd].astype(MXU_DTYPE)
            d_ckn += _dot(dvh, wv_ref[hd], 1, 1)
            dwv_ref[hd] += _dot(ckn, dvh, 0, 0)
        lane = lax.broadcasted_iota(jnp.int32, (tm, LANES), 1)
        rot = (lane >= NOPE_DIM) & (lane < NOPE_DIM + ROPE_DIM)
        d_kr = jnp.where(rot, _rope_inv(jnp.where(rot, d_krr, 0.0), ctab_, stab_), 0.0)

        def rms_bwd(dy, xh, r, g, dg_ref):
            dg_ref[...] += jnp.sum(dy * xh, axis=0, keepdims=True)
            dxh = dy * g
            return r * (dxh - xh * jnp.mean(dxh * xh, axis=-1, keepdims=True))

        d_cq = rms_bwd(d_cqn, xq, rq, gq, dgq_ref)
        d_ck = rms_bwd(d_ckn, xk, rk, gk, dgk_ref)
        dh_ref[...] = jnp.concatenate([d_cq, d_ck, d_kr], axis=1).astype(dh_ref.dtype)

    full = lambda *shape: pl.BlockSpec(shape, lambda i: (0,) * len(shape))
    slab = pl.BlockSpec((HEADS, tm, LANES), lambda i: (0, i, 0))
    return pl.pallas_call(
        body, name="mla_prep_bwd", grid=(seq // tm,),
        in_specs=[pl.BlockSpec((tm, 512), lambda i: (i, 0)), full(1, Q_RANK), full(1, KV_RANK),
                  full(HEADS, Q_RANK, LANES), full(HEADS, KV_RANK, LANES), full(HEADS, KV_RANK, LANES),
                  pl.BlockSpec((tm, LANES), lambda i: (i, 0)), pl.BlockSpec((tm, LANES), lambda i: (i, 0)),
                  slab, slab, slab] + [_ANY_SPEC] * n_after,
        out_specs=[pl.BlockSpec((tm, 512), lambda i: (i, 0)), full(HEADS, Q_RANK, LANES), full(HEADS, KV_RANK, LANES),
                   full(HEADS, KV_RANK, LANES), full(1, Q_RANK), full(1, KV_RANK)],
        out_shape=[jax.ShapeDtypeStruct((seq, 512), MXU_DTYPE), jax.ShapeDtypeStruct((HEADS, Q_RANK, LANES), F32),
                   jax.ShapeDtypeStruct((HEADS, KV_RANK, LANES), F32), jax.ShapeDtypeStruct((HEADS, KV_RANK, LANES), F32),
                   jax.ShapeDtypeStruct((1, Q_RANK), F32), jax.ShapeDtypeStruct((1, KV_RANK), F32)],
        compiler_params=_params("arbitrary"),
    )(h, g_cq, g_ckv, wq, wk, wv, ctab, stab, dq, dk, dv, *after)


def _causal_mask(t):
    row = lax.broadcasted_iota(jnp.int32, (t, t), 0)
    col = lax.broadcasted_iota(jnp.int32, (t, t), 1)
    return row >= col


def _mla_attn_fwd(q, k, v, t=512):
    _, seq, _ = q.shape
    t = min(t, seq)

    def body(q_ref, k_ref, v_ref, o_ref, ob_ref, lse_ref, m_ref, l_ref, acc_ref, s_ref):
        i = pl.program_id(1)
        qb = q_ref[...]
        m_ref[...] = jnp.full_like(m_ref, NEG_BIG)
        l_ref[...] = jnp.zeros_like(l_ref)
        acc_ref[...] = jnp.zeros_like(acc_ref)

        def scores(j):
            return _dot(qb, k_ref[pl.ds(pl.multiple_of(j * t, t), t), :], 1, 1) * MLA_SCALE_LOG2

        def softmax_pv(j, s, masked):
            vb = v_ref[pl.ds(pl.multiple_of(j * t, t), t), :]
            if masked:
                s = jnp.where(_causal_mask(t), s, NEG_BIG)
            m_old = m_ref[...]
            m_new = jnp.maximum(m_old, jnp.max(s, axis=1, keepdims=True))
            p = jnp.exp2(s - m_new)
            a = jnp.exp2(m_old - m_new)
            l_ref[...] = a * l_ref[...] + jnp.sum(p, axis=1, keepdims=True)
            acc_ref[...] = a * acc_ref[...] + _dot(p.astype(MXU_DTYPE), vb, 1, 0)
            m_ref[...] = m_new

        s_ref[...] = scores(0)

        def loop_body(j, c):
            s_next = scores(j + 1)
            softmax_pv(j, s_ref[...], False)
            s_ref[...] = s_next
            return c

        lax.fori_loop(0, i, loop_body, 0)
        softmax_pv(i, s_ref[...], True)
        l = l_ref[...]
        o = acc_ref[...] * (1.0 / l)
        o_ref[...] = o
        ob_ref[...] = o.astype(ob_ref.dtype)
        lse_ref[...] = jnp.broadcast_to(m_ref[...] + jnp.log2(l), lse_ref.shape)

    blk = pl.BlockSpec((None, t, LANES), lambda h, i: (h, i, 0))
    whole = pl.BlockSpec((None, seq, LANES), lambda h, i: (h, 0, 0))
    shp = jax.ShapeDtypeStruct((HEADS, seq, LANES), F32)
    return pl.pallas_call(
        body, name="mla_attn_fwd", grid=(HEADS, seq // t),
        in_specs=[blk, whole, whole], out_specs=[blk, blk, blk],
        out_shape=[shp, jax.ShapeDtypeStruct((HEADS, seq, LANES), MXU_DTYPE), shp],
        scratch_shapes=[pltpu.VMEM((t, 1), F32), pltpu.VMEM((t, 1), F32), pltpu.VMEM((t, LANES), F32), pltpu.VMEM((t, t), F32)],
        compiler_params=_params("parallel", "arbitrary"),
    )(q, k, v)


def _mla_attn_bwd(q, k, v, o, lse, do, t=512):
    _, seq, _ = q.shape
    t = min(t, seq)
    nb = seq // t

    def body(q_ref, k_ref, v_ref, o_ref, lse_ref, do_ref, dq_ref, dk_ref, dv_ref, dl_ref, dka_ref, dva_ref):
        dq_ref[...] = jnp.zeros_like(dq_ref)

        def delta_body(i, c):
            rows = pl.ds(pl.multiple_of(i * t, t), t)
            dl_ref[rows, :] = jnp.sum(do_ref[rows, :] * o_ref[rows, :], axis=1, keepdims=True)
            return c

        lax.fori_loop(0, nb, delta_body, 0)

        def kblock(j, c):
            krows = pl.ds(pl.multiple_of(j * t, t), t)
            kb = k_ref[krows, :]
            vb = v_ref[krows, :]
            dka_ref[...] = jnp.zeros_like(dka_ref)
            dva_ref[...] = jnp.zeros_like(dva_ref)

            def qstep(i, masked):
                rows = pl.ds(pl.multiple_of(i * t, t), t)
                qb = q_ref[rows, :]
                dob = do_ref[rows, :].astype(MXU_DTYPE)
                s = _dot(qb, kb, 1, 1) * MLA_SCALE_LOG2
                if masked:
                    s = jnp.where(_causal_mask(t), s, NEG_BIG)
                p = jnp.exp2(s - lse_ref[rows, 0:1])
                dva_ref[...] += _dot(p.astype(MXU_DTYPE), dob, 0, 0)
                dp = _dot(dob, vb, 1, 1)
                ds = (p * (dp - dl_ref[rows, :]) * MLA_SCALE).astype(MXU_DTYPE)
                dka_ref[...] += _dot(ds, qb, 0, 0)
                dq_ref[rows, :] += _dot(ds, kb, 1, 0)

            qstep(j, True)

            def qloop(i, c2):
                qstep(i, False)
                return c2

            lax.fori_loop(j + 1, nb, qloop, 0)
            dk_ref[krows, :] = dka_ref[...]
            dv_ref[krows, :] = dva_ref[...]
            return c

        lax.fori_loop(0, nb, kblock, 0)

    whole = pl.BlockSpec((None, seq, LANES), lambda h: (h, 0, 0))
    shp = jax.ShapeDtypeStruct((HEADS, seq, LANES), F32)
    return pl.pallas_call(
        body, name="mla_attn_bwd", grid=(HEADS,),
        in_specs=[whole] * 6, out_specs=[whole] * 3, out_shape=[shp] * 3,
        scratch_shapes=[pltpu.VMEM((seq, 1), F32), pltpu.VMEM((t, LANES), F32), pltpu.VMEM((t, LANES), F32)],
        compiler_params=_params("parallel"),
    )(q, k, v, o, lse, do)


DIL_CHUNK = DIL_BLOCK * max(d for _, d in DIL_PAIRS)
DIL_PAIR_LANES = 2 * HEAD_DIM
assert DIL_PAIR_LANES == LANES
DIL_UNROLL_FWD = 4
DIL_UNROLL_BWD = 4


def _dil_bias_tables(hp, dil):
    b = DIL_BLOCK
    iq = lax.broadcasted_iota(jnp.int32, (b, 2 * b), 0)
    ik = lax.broadcasted_iota(jnp.int32, (b, 2 * b), 1)
    off = iq + b - ik
    band = (off >= 0) & (off <= b)
    dist = (off * dil).astype(F32)
    every, first = [], []
    for hh in range(2):
        slope = jnp.where(hp == 0, ALIBI_SLOPES[hh], jnp.where(hp == 1, ALIBI_SLOPES[2 + hh],
                          jnp.where(hp == 2, ALIBI_SLOPES[4 + hh], ALIBI_SLOPES[6 + hh]))).astype(F32)
        bias = -slope * dist
        every.append(jnp.where(band, bias, NEG_BIG))
        first.append(jnp.where(band & (ik >= b), bias, NEG_BIG))
    return jnp.concatenate(every, axis=0), jnp.concatenate(first, axis=0)


def _dil_rows(start, dil):
    return pl.ds(start, DIL_BLOCK) if dil == 1 else pl.ds(start, DIL_BLOCK, stride=dil)


def _dil_block_pos(blk, c, dil):
    sc, r = blk // dil, blk % dil
    q0 = sc * (DIL_BLOCK * dil) + r
    kcur0 = c * DIL_CHUNK + q0
    first = kcur0 < DIL_BLOCK * dil
    kprev0 = jnp.where(first, kcur0, kcur0 - DIL_BLOCK * dil)
    return q0, kcur0, kprev0, first


def _pair_cols(hh):
    return slice(HEAD_DIM * hh, HEAD_DIM * (hh + 1))


def _first_head_lanes(shape):
    return lax.broadcasted_iota(jnp.int32, shape, 1) < HEAD_DIM


def _stack_pair(t):
    first = _first_head_lanes(t.shape)
    return jnp.concatenate([jnp.where(first, t, 0.0), jnp.where(first, 0.0, t)], axis=0).astype(MXU_DTYPE)


def _unstack_pair(t):
    rows = t.shape[0] // 2
    return jnp.where(_first_head_lanes((rows, t.shape[1])), t[:rows], t[rows:])


def _pair_column(t):
    return jnp.concatenate([t[:, 0:1], t[:, HEAD_DIM:HEAD_DIM + 1]], axis=0)


def _dil_fwd(h):
    seq = h.shape[0]
    assert seq % DIL_CHUNK == 0
    nblk = DIL_CHUNK // DIL_BLOCK
    rc = 256

    def body(q_ref, k_ref, v_ref, o_ref, ob_ref, lse_ref, *scr):
        o_scr, l_scr = scr[:3], scr[3:]
        hp, c = pl.program_id(0), pl.program_id(1)
        for bi, (_, dil) in enumerate(DIL_PAIRS):
            tables = _dil_bias_tables(hp, dil)

            def block(blk, carry, bi=bi, dil=dil, tables=tables):
                q0, kcur0, kprev0, first = _dil_block_pos(blk, c, dil)
                q2 = _stack_pair(q_ref[_dil_rows(q0, dil), :] * DIL_SCALE)
                kcat = jnp.concatenate([k_ref[_dil_rows(kprev0, dil), :], k_ref[_dil_rows(kcur0, dil), :]], axis=0).astype(MXU_DTYPE)
                vcat = jnp.concatenate([v_ref[_dil_rows(kprev0, dil), :], v_ref[_dil_rows(kcur0, dil), :]], axis=0).astype(MXU_DTYPE)
                s = _dot(q2, kcat, 1, 1) + jnp.where(first, tables[1], tables[0])
                mx = jnp.max(s, axis=1, keepdims=True)
                p = jnp.exp(s - mx)
                l = jnp.sum(p, axis=1, keepdims=True)
                o_scr[bi][_dil_rows(q0, dil), :] = _unstack_pair(_dot(p.astype(MXU_DTYPE), vcat, 1, 0) * (1.0 / l))
                l_scr[bi][_dil_rows(q0, dil), :] = _unstack_pair(jnp.broadcast_to(mx + jnp.log(l), (2 * DIL_BLOCK, LANES)))
                return carry

            lax.fori_loop(0, nblk, block, 0, unroll=DIL_UNROLL_FWD)

        def combine(i, carry):
            rows = pl.ds(pl.multiple_of(i * rc, rc), rc)
            ls = [l_scr[bi][rows, :] for bi in range(3)]
            mx = jnp.maximum(jnp.maximum(ls[0], ls[1]), ls[2])
            es = [jnp.exp(l - mx) for l in ls]
            den = es[0] + es[1] + es[2]
            o = (es[0] * o_scr[0][rows, :] + es[1] * o_scr[1][rows, :] + es[2] * o_scr[2][rows, :]) / den
            o_ref[rows, :] = o
            ob_ref[rows, :] = o.astype(ob_ref.dtype)
            lse_ref[rows, :] = mx + jnp.log(den)
            return carry

        lax.fori_loop(0, DIL_CHUNK // rc, combine, 0)

    nq = DIL_WIDTH // LANES
    chunk = lambda off: pl.BlockSpec((DIL_CHUNK, LANES), lambda hp, c: (c, off + hp))
    whole = lambda off: pl.BlockSpec((seq, LANES), lambda hp, c: (0, off + hp))
    shp = jax.ShapeDtypeStruct((seq, DIL_WIDTH), F32)
    return pl.pallas_call(
        body, name="dil_fwd", grid=(nq, seq // DIL_CHUNK),
        in_specs=[chunk(nq), whole(2 * nq), whole(3 * nq)], out_specs=[chunk(0), chunk(0), chunk(0)],
        out_shape=[shp, jax.ShapeDtypeStruct((seq, DIL_WIDTH), MXU_DTYPE), shp],
        scratch_shapes=[pltpu.VMEM((DIL_CHUNK, LANES), F32)] * 6,
        compiler_params=_params("parallel", "arbitrary"),
    )(h, h, h)


def _dil_bwd(h, o, lse, do):
    seq = h.shape[0]
    nblk = DIL_CHUNK // DIL_BLOCK
    rc = 256

    def body(q_ref, k_ref, v_ref, o_ref, lse_ref, do_ref, dq_ref, dk_ref, dv_ref, dl_scr):
        hp, c = pl.program_id(0), pl.program_id(1)

        @pl.when(c == 0)
        def _():
            dk_ref[...] = jnp.zeros_like(dk_ref)
            dv_ref[...] = jnp.zeros_like(dv_ref)

        def delta(i, carry):
            rows = pl.ds(pl.multiple_of(i * rc, rc), rc)
            prod = do_ref[rows, :] * o_ref[rows, :]
            dl_scr[rows, :] = jnp.concatenate(
                [jnp.broadcast_to(jnp.sum(prod[:, _pair_cols(hh)], axis=1, keepdims=True), (rc, HEAD_DIM)) for hh in range(2)], axis=1)
            return carry

        lax.fori_loop(0, DIL_CHUNK // rc, delta, 0)

        for bi, (_, dil) in enumerate(DIL_PAIRS):
            tables = _dil_bias_tables(hp, dil)

            def block(blk, carry, bi=bi, dil=dil, tables=tables):
                q0, kcur0, kprev0, first = _dil_block_pos(blk, c, dil)
                qrows = _dil_rows(q0, dil)
                q2 = _stack_pair(q_ref[qrows, :] * DIL_SCALE)
                kcat = jnp.concatenate([k_ref[_dil_rows(kprev0, dil), :], k_ref[_dil_rows(kcur0, dil), :]], axis=0).astype(MXU_DTYPE)
                vcat = jnp.concatenate([v_ref[_dil_rows(kprev0, dil), :], v_ref[_dil_rows(kcur0, dil), :]], axis=0).astype(MXU_DTYPE)
                do2 = _stack_pair(do_ref[qrows, :])
                s = _dot(q2, kcat, 1, 1) + jnp.where(first, tables[1], tables[0])
                p = jnp.exp(s - _pair_column(lse_ref[qrows, :]))
                dp = _dot(do2, vcat, 1, 1)
                ds = (p * (dp - _pair_column(dl_scr[qrows, :]))).astype(MXU_DTYPE)
                dq_b = _unstack_pair(_dot(ds, kcat, 1, 0)) * DIL_SCALE
                dk_b = _dot(ds, q2, 0, 0)
                dv_b = _dot(p.astype(MXU_DTYPE), do2, 0, 0)
                if bi == 0:
                    dq_ref[qrows, :] = dq_b
                else:
                    dq_ref[qrows, :] += dq_b
                dk_ref[_dil_rows(kprev0, dil), :] += dk_b[:DIL_BLOCK]
                dv_ref[_dil_rows(kprev0, dil), :] += dv_b[:DIL_BLOCK]
                dk_ref[_dil_rows(kcur0, dil), :] += dk_b[DIL_BLOCK:]
                dv_ref[_dil_rows(kcur0, dil), :] += dv_b[DIL_BLOCK:]
                return carry

            lax.fori_loop(0, nblk, block, 0, unroll=DIL_UNROLL_BWD)

    nq = DIL_WIDTH // LANES
    chunk = lambda off: pl.BlockSpec((DIL_CHUNK, LANES), lambda hp, c: (c, off + hp))
    whole = lambda off: pl.BlockSpec((seq, LANES), lambda hp, c: (0, off + hp))
    shp = jax.ShapeDtypeStruct((seq, DIL_WIDTH), F32)
    return pl.pallas_call(
        body, name="dil_bwd", grid=(nq, seq // DIL_CHUNK),
        in_specs=[chunk(nq), whole(2 * nq), whole(3 * nq), chunk(0), chunk(0), chunk(0)],
        out_specs=[chunk(0), whole(0), whole(0)], out_shape=[shp, shp, shp],
        scratch_shapes=[pltpu.VMEM((DIL_CHUNK, LANES), F32)],
        compiler_params=_params("parallel", "arbitrary"),
    )(h, h, h, o, lse, do)


def _assemble_dh(dh_mla, dq, dk, dv, tm=512):
    seq = dh_mla.shape[0]
    tm = min(tm, seq)

    def body(a_ref, q_ref, k_ref, v_ref, o_ref):
        for j, r in enumerate((a_ref, q_ref, k_ref, v_ref)):
            o_ref[:, 512 * j:512 * (j + 1)] = r[...].astype(o_ref.dtype)

    blk = pl.BlockSpec((tm, 512), lambda i: (i, 0))
    return pl.pallas_call(
        body, name="assemble_dh", grid=(seq // tm,), in_specs=[blk] * 4,
        out_specs=pl.BlockSpec((tm, IN_PAD), lambda i: (i, 0)), out_shape=jax.ShapeDtypeStruct((seq, IN_PAD), MXU_DTYPE),
        compiler_params=_params("parallel"),
    )(dh_mla, dq, dk, dv)


def _ln_stats(z):
    mu = jnp.mean(z, axis=-1, keepdims=True)
    zc = z - mu
    r = lax.rsqrt(jnp.mean(zc * zc, axis=-1, keepdims=True) + LN_EPS)
    return zc * r, r


def _ln_bwd_math(dy, xh, r, g):
    dxh = dy * g
    return r * (dxh - jnp.mean(dxh, axis=-1, keepdims=True) - xh * jnp.mean(dxh * xh, axis=-1, keepdims=True))


def _mix_ln1(o_mla, o_dil, w_o_mla, w_o_dil, x0, g, b, tm=512):
    seq, d = x0.shape
    tm = min(tm, seq)

    def body(om_ref, od_ref, wm_ref, wd_ref, x_ref, g_ref, b_ref, z_ref, y_ref, yb_ref):
        mix = _dot(od_ref[...], wd_ref[...], 1, 0)
        for hd in range(HEADS):
            mix += _dot(om_ref[hd], wm_ref[LANES * hd:LANES * (hd + 1), :], 1, 0)
        z = DN_ALPHA * x_ref[...] + mix
        xh, _ = _ln_stats(z)
        y = xh * g_ref[...] + b_ref[...]
        z_ref[...] = z
        y_ref[...] = y
        yb_ref[...] = y.astype(yb_ref.dtype)

    blk = pl.BlockSpec((tm, d), lambda i: (i, 0))
    vec = pl.BlockSpec((1, d), lambda i: (0, 0))
    shp = jax.ShapeDtypeStruct((seq, d), F32)
    return pl.pallas_call(
        body, name="mix_ln1", grid=(seq // tm,),
        in_specs=[pl.BlockSpec((HEADS, tm, LANES), lambda i: (0, i, 0)), pl.BlockSpec((tm, DIL_WIDTH), lambda i: (i, 0)),
                  pl.BlockSpec((HEADS * LANES, d), lambda i: (0, 0)), pl.BlockSpec((DIL_WIDTH, d), lambda i: (0, 0)), blk, vec, vec],
        out_specs=[blk, blk, blk], out_shape=[shp, shp, jax.ShapeDtypeStruct((seq, d), MXU_DTYPE)],
        compiler_params=_params("parallel"))(o_mla, o_dil, w_o_mla, w_o_dil, x0, g, b)


def _ln_bwd(dy, z, g, name, tm=512, after=()):
    seq, d = z.shape
    tm = min(tm, seq)
    n_after = len(after)

    def body(dy_ref, z_ref, g_ref, *rest):
        dz_ref, dzb_ref, dg_ref, db_ref = rest[n_after:]

        @pl.when(pl.program_id(0) == 0)
        def _():
            dg_ref[...] = jnp.zeros_like(dg_ref)
            db_ref[...] = jnp.zeros_like(db_ref)

        dyb = dy_ref[...]
        xh, r = _ln_stats(z_ref[...])
        dg_ref[...] += jnp.sum(dyb * xh, axis=0, keepdims=True)
        db_ref[...] += jnp.sum(dyb, axis=0, keepdims=True)
        dz = _ln_bwd_math(dyb, xh, r, g_ref[...])
        dz_ref[...] = dz
        dzb_ref[...] = dz.astype(dzb_ref.dtype)

    blk = pl.BlockSpec((tm, d), lambda i: (i, 0))
    vec = pl.BlockSpec((1, d), lambda i: (0, 0))
    return pl.pallas_call(
        body, name=name, grid=(seq // tm,), in_specs=[blk, blk, vec] + [_ANY_SPEC] * n_after, out_specs=[blk, blk, vec, vec],
        out_shape=[jax.ShapeDtypeStruct((seq, d), F32), jax.ShapeDtypeStruct((seq, d), MXU_DTYPE),
                   jax.ShapeDtypeStruct((1, d), F32), jax.ShapeDtypeStruct((1, d), F32)],
        compiler_params=_params("arbitrary"))(dy, z, g, *after)


def _ln2_loss_bwd(x1, ffn, target, g, b, tm=512):
    seq, d = x1.shape
    tm = min(tm, seq)

    def body(x_ref, f_ref, t_ref, g_ref, b_ref, dz_ref, dzb_ref, loss_ref, dg_ref, db_ref):
        @pl.when(pl.program_id(0) == 0)
        def _():
            loss_ref[...] = jnp.zeros_like(loss_ref)
            dg_ref[...] = jnp.zeros_like(dg_ref)
            db_ref[...] = jnp.zeros_like(db_ref)

        gv = g_ref[...]
        z = DN_ALPHA * x_ref[...] + f_ref[...]
        xh, r = _ln_stats(z)
        err = (xh * gv + b_ref[...]) - t_ref[...]
        loss_ref[...] += 0.5 * jnp.sum(jnp.mean(err * err, axis=-1, keepdims=True), axis=0, keepdims=True)
        dy = err * (1.0 / d)
        dg_ref[...] += jnp.sum(dy * xh, axis=0, keepdims=True)
        db_ref[...] += jnp.sum(dy, axis=0, keepdims=True)
        dz = _ln_bwd_math(dy, xh, r, gv)
        dz_ref[...] = dz
        dzb_ref[...] = dz.astype(dzb_ref.dtype)

    blk = pl.BlockSpec((tm, d), lambda i: (i, 0))
    vec = pl.BlockSpec((1, d), lambda i: (0, 0))
    return pl.pallas_call(
        body, name="ln2_loss_bwd", grid=(seq // tm,), in_specs=[blk, blk, blk, vec, vec],
        out_specs=[blk, blk, pl.BlockSpec((1, LANES), lambda i: (0, 0)), vec, vec],
        out_shape=[jax.ShapeDtypeStruct((seq, d), F32), jax.ShapeDtypeStruct((seq, d), MXU_DTYPE),
                   jax.ShapeDtypeStruct((1, LANES), F32),
                   jax.ShapeDtypeStruct((1, d), F32), jax.ShapeDtypeStruct((1, d), F32)],
        compiler_params=_params("arbitrary"))(x1, ffn, target, g, b)


HALO = 16


def _conv_rows(e, w_ref, b_ref):
    y = b_ref[...] + w_ref[0:1, :] * pltpu.roll(e, 2, 0)
    y = y + w_ref[1:2, :] * pltpu.roll(e, 1, 0)
    return y + w_ref[2:3, :] * e


_GELU_C = math.sqrt(2.0 / math.pi)
_GELU_A = 0.044715


def _gelu(x):
    return 0.5 * x * (1.0 + jnp.tanh(_GELU_C * (x + _GELU_A * (x * x * x))))


CONV_TN = 256


def _ffn_interleave(a, axis):
    shp = a.shape
    a = a.reshape(shp[:axis] + (2, D_FF // CONV_TN, CONV_TN) + shp[axis + 1:])
    return jnp.swapaxes(a, axis, axis + 1).reshape(shp)


def _ffn_deinterleave(a, axis):
    shp = a.shape
    a = a.reshape(shp[:axis] + (D_FF // CONV_TN, 2, CONV_TN) + shp[axis + 1:])
    return jnp.swapaxes(a, axis, axis + 1).reshape(shp)


def _conv_gate_fwd(u, conv_w, conv_b, tm=1024):
    seq = u.shape[0]
    tm = min(tm, seq)
    tn = CONV_TN

    def body(u_ref, up_ref, w_ref, b_ref, o_ref):
        first = pl.program_id(0) == 0
        e = jnp.concatenate([jnp.where(first, 0.0, up_ref[...]), u_ref[...]], axis=0)
        y = _conv_rows(e, w_ref, b_ref)[HALO:]
        o_ref[...] = (_gelu(y[:, tn:]) * y[:, :tn]).astype(o_ref.dtype)

    hb = tm // HALO
    return pl.pallas_call(
        body, name="conv_gate_fwd", grid=(seq // tm, D_FF // tn),
        in_specs=[pl.BlockSpec((tm, 2 * tn), lambda i, j: (i, j)),
                  pl.BlockSpec((HALO, 2 * tn), lambda i, j: (jnp.maximum(i * hb - 1, 0), j)),
                  pl.BlockSpec((3, 2 * tn), lambda i, j: (0, j)), pl.BlockSpec((1, 2 * tn), lambda i, j: (0, j))],
        out_specs=pl.BlockSpec((tm, tn), lambda i, j: (i, j)), out_shape=jax.ShapeDtypeStruct((seq, D_FF), MXU_DTYPE),
        compiler_params=_params("parallel", "parallel"),
    )(u, u, conv_w, conv_b)


def _conv_gate_bwd(u, d_act, conv_w, conv_b, tm=512):
    seq = u.shape[0]
    tm = min(tm, seq)
    tn = CONV_TN
    ni = seq // tm
    rows_e = tm + 2 * HALO

    def body(u_ref, up_ref, un_ref, da_ref, dan_ref, w_ref, b_ref, du_ref, dw_ref, db_ref):
        i = pl.program_id(1)
        first, last = i == 0, i == ni - 1

        @pl.when(i == 0)
        def _():
            dw_ref[...] = jnp.zeros_like(dw_ref)
            db_ref[...] = jnp.zeros_like(db_ref)

        e = jnp.concatenate([jnp.where(first, 0.0, up_ref[...]), u_ref[...], jnp.where(last, 0.0, un_ref[...])], axis=0)
        y = _conv_rows(e, w_ref, b_ref)
        ya, yg = y[:, :tn], y[:, tn:]
        dact = jnp.concatenate([jnp.zeros((HALO, tn), F32), da_ref[...].astype(F32),
                                jnp.where(last, 0.0, dan_ref[...].astype(F32))], axis=0)
        th = jnp.tanh(_GELU_C * (yg + _GELU_A * (yg * yg * yg)))
        gelu = 0.5 * yg * (1.0 + th)
        gelu_grad = 0.5 * (1.0 + th) + 0.5 * yg * (1.0 - th * th) * (_GELU_C * (1.0 + 3.0 * _GELU_A * (yg * yg)))
        dy = jnp.concatenate([dact * gelu, dact * ya * gelu_grad], axis=1)
        du = w_ref[2:3, :] * dy + w_ref[1:2, :] * pltpu.roll(dy, rows_e - 1, 0) + w_ref[0:1, :] * pltpu.roll(dy, rows_e - 2, 0)
        du_ref[...] = du[HALO:HALO + tm].astype(du_ref.dtype)
        dyt = dy[HALO:HALO + tm]
        dw_ref[0:1, :] += jnp.sum(dyt * pltpu.roll(e, 2, 0)[HALO:HALO + tm], axis=0, keepdims=True)
        dw_ref[1:2, :] += jnp.sum(dyt * pltpu.roll(e, 1, 0)[HALO:HALO + tm], axis=0, keepdims=True)
        dw_ref[2:3, :] += jnp.sum(dyt * e[HALO:HALO + tm], axis=0, keepdims=True)
        db_ref[...] += jnp.sum(dyt, axis=0, keepdims=True)

    hb = tm // HALO
    nh = seq // HALO
    prev = lambda j, i: (jnp.maximum(i * hb - 1, 0), j)
    nxt = lambda j, i: (jnp.minimum((i + 1) * hb, nh - 1), j)
    return pl.pallas_call(
        body, name="conv_gate_bwd", grid=(D_FF // tn, ni),
        in_specs=[pl.BlockSpec((tm, 2 * tn), lambda j, i: (i, j)), pl.BlockSpec((HALO, 2 * tn), prev),
                  pl.BlockSpec((HALO, 2 * tn), nxt), pl.BlockSpec((tm, tn), lambda j, i: (i, j)), pl.BlockSpec((HALO, tn), nxt),
                  pl.BlockSpec((3, 2 * tn), lambda j, i: (0, j)), pl.BlockSpec((1, 2 * tn), lambda j, i: (0, j))],
        out_specs=[pl.BlockSpec((tm, 2 * tn), lambda j, i: (i, j)), pl.BlockSpec((3, 2 * tn), lambda j, i: (0, j)),
                   pl.BlockSpec((1, 2 * tn), lambda j, i: (0, j))],
        out_shape=[jax.ShapeDtypeStruct((seq, 2 * D_FF), MXU_DTYPE), jax.ShapeDtypeStruct((3, 2 * D_FF), F32),
                   jax.ShapeDtypeStruct((1, 2 * D_FF), F32)],
        compiler_params=_params("parallel", "arbitrary"),
    )(u, u, u, d_act, d_act, conv_w, conv_b)


def _pad_heads(w, width):
    w = jnp.transpose(w, (1, 0, 2))
    return jnp.pad(w, ((0, 0), (0, 0), (0, LANES - width))).astype(MXU_DTYPE)


def _unpad_heads(d, width):
    return jnp.transpose(d[:, :, :width], (1, 0, 2))


def _split_pad_rows(w_t):
    z = lambda n: jnp.zeros((n, w_t.shape[1]), w_t.dtype)
    return jnp.concatenate([w_t[:384], z(64), w_t[384:416], z(32), w_t[416:]], axis=0)


def _split_unpad_rows(w_p):
    return jnp.concatenate([w_p[:384], w_p[448:480], w_p[512:]], axis=0)


def _pad_w_o(w_o):
    mla = jnp.pad(w_o[:512].reshape(HEADS, HEAD_DIM, D_MODEL), ((0, 0), (0, LANES - HEAD_DIM), (0, 0)))
    return mla.reshape(HEADS * LANES, D_MODEL).astype(MXU_DTYPE), w_o[512:].astype(MXU_DTYPE)


def _unpad_w_o(d_mla, d_dil):
    return jnp.concatenate([d_mla.reshape(HEADS, LANES, D_MODEL)[:, :HEAD_DIM].reshape(512, D_MODEL), d_dil], axis=0)


def _row(v):
    return v.reshape(1, -1).astype(F32)


def _compute_weights(w):
    w_o_mla, w_o_dil = _pad_w_o(w["w_o"])
    return dict(
        w_in_t=_split_pad_rows(w["w_in"].T).astype(MXU_DTYPE), wq=_pad_heads(w["w_uq"], NOPE_DIM + ROPE_DIM),
        wk=_pad_heads(w["w_uk"], NOPE_DIM), wv=_pad_heads(w["w_uv"], HEAD_DIM), w_o_mla=w_o_mla, w_o_dil=w_o_dil,
        w_up_t=_ffn_interleave(w["w_up"].T, 0).astype(MXU_DTYPE), w_down=w["w_down"].astype(MXU_DTYPE),
        conv_w=_ffn_interleave(w["conv_w"].astype(F32), 1),
        g_cq=_row(w["g_cq"]), g_ckv=_row(w["g_ckv"]), ln1_g=_row(w["ln1_g"]), ln1_b=_row(w["ln1_b"]),
        conv_b=_ffn_interleave(_row(w["conv_b"]), 1), ln2_g=_row(w["ln2_g"]), ln2_b=_row(w["ln2_b"]))


def _natural_grads(g):
    return dict(
        w_in=_split_unpad_rows(g["w_in_t"]).T, g_cq=g["g_cq"].reshape(-1), g_ckv=g["g_ckv"].reshape(-1),
        w_uq=_unpad_heads(g["wq"], NOPE_DIM + ROPE_DIM), w_uk=_unpad_heads(g["wk"], NOPE_DIM),
        w_uv=_unpad_heads(g["wv"], HEAD_DIM), w_o=_unpad_w_o(g["w_o_mla"], g["w_o_dil"]), ln1_g=g["ln1_g"].reshape(-1),
        ln1_b=g["ln1_b"].reshape(-1), w_up=_ffn_deinterleave(g["w_up_t"], 0).T, conv_w=_ffn_deinterleave(g["conv_w"], 1),
        conv_b=_ffn_deinterleave(g["conv_b"], 1).reshape(-1),
        w_down=g["w_down"], ln2_g=g["ln2_g"].reshape(-1), ln2_b=g["ln2_b"].reshape(-1))


def _layer_grads(x0, target, cw, first_after=(), late_weights=None, on_grads=None):
    seq = x0.shape[0]
    ctab, stab = _rope_tables(seq)
    gq, gk = cw["g_cq"], cw["g_ckv"]
    wq, wk, wv = cw["wq"], cw["wk"], cw["wv"]
    x0b = x0.astype(MXU_DTYPE)
    notify = (lambda stage, grads: ()) if on_grads is None else on_grads

    h = _mm(x0b, cw["w_in_t"], name="mm_h", tb=True, tm=1024, tn=1024, tk=1024, after=first_after)
    qf, kf, vp = _mla_prep(h, gq, gk, wq, wk, wv, ctab, stab)
    o_mla, o_mla_b, lse_mla = _mla_attn_fwd(qf, kf, vp)
    o_dil, o_dil_b, lse_dil = _dil_fwd(h)
    if late_weights is not None:
        cw = {**cw, **late_weights(o_mla_b)}
    cb = cw["conv_b"]
    z1, x1, x1b = _mix_ln1(o_mla_b, o_dil_b, cw["w_o_mla"], cw["w_o_dil"], x0, cw["ln1_g"], cw["ln1_b"])
    u = _mm(x1b, cw["w_up_t"], name="mm_up", tb=True, tm=1024, tn=1408, tk=1024)
    act = _conv_gate_fwd(u, cw["conv_w"], cb)
    ffn = _mm(act, cw["w_down"], name="mm_down", tm=1024, tn=1024, tk=2816)
    dz2, dz2b, loss, d_ln2_g, d_ln2_b = _ln2_loss_bwd(x1, ffn, target, cw["ln2_g"], cw["ln2_b"])

    d_act = _mm(dz2b, cw["w_down"], name="mm_d_act", tb=True, out_dtype=MXU_DTYPE, tm=1024, tn=1408, tk=1024)
    d_w_down = _mm(act, dz2b, name="mm_dw_down", ta=True, out_dtype=MXU_DTYPE, tm=1408, tn=1024, tk=1024)
    du, d_conv_w, d_conv_b = _conv_gate_bwd(u, d_act, cw["conv_w"], cb)
    dx1 = _mm(du, cw["w_up_t"], name="mm_dx1", res=dz2, res_scale=DN_ALPHA, tm=1024, tn=1024, tk=1408)
    d_w_up_t = _mm(du, x1b, name="mm_dw_up", ta=True, out_dtype=MXU_DTYPE, tm=1408, tn=1024, tk=2048)
    grads = dict(w_up_t=d_w_up_t, w_down=d_w_down, conv_w=d_conv_w, conv_b=d_conv_b, ln2_g=d_ln2_g, ln2_b=d_ln2_b)
    dz1, dz1b, d_ln1_g, d_ln1_b = _ln_bwd(dx1, z1, cw["ln1_g"], "ln1_bwd", after=notify("ffn", grads))
    do_mla = _mm_do_mla(dz1b, cw["w_o_mla"])
    do_dil = _mm(dz1b, cw["w_o_dil"], name="mm_do_dil", tb=True, tm=1024, tn=512, tk=1024)
    d_w_o_mla = _mm_dw_o_mla(o_mla_b, dz1b)
    d_w_o_dil = _mm(o_dil_b, dz1b, name="mm_dw_o_dil", ta=True, out_dtype=MXU_DTYPE, tm=512, tn=1024, tk=1024)
    grads.update(w_o_mla=d_w_o_mla, w_o_dil=d_w_o_dil, ln1_g=d_ln1_g, ln1_b=d_ln1_b)
    dq_dil, dk_dil, dv_dil = _dil_bwd(h, o_dil, lse_dil, do_dil)
    dqf, dkf, dvf = _mla_attn_bwd(qf, kf, vp, o_mla, lse_mla, do_mla)
    dh_mla, d_wq, d_wk, d_wv, d_gq, d_gk = _mla_prep_bwd(h, gq, gk, wq, wk, wv, ctab, stab, dqf, dkf, dvf,
                                                          after=notify("w_o", grads))
    dh = _assemble_dh(dh_mla, dq_dil, dk_dil, dv_dil)
    d_w_in_t = _mm(dh, x0b, name="mm_dw_in", ta=True, out_dtype=MXU_DTYPE, tm=1024, tn=1024, tk=1024)
    grads.update(w_in_t=d_w_in_t, wq=d_wq, wk=d_wk, wv=d_wv, g_cq=d_gq, g_ckv=d_gk, loss=loss)
    grad_x = _mm(dh, cw["w_in_t"], name="mm_dx0", res=dz1, res_scale=DN_ALPHA, tm=1024, tn=1024, tk=2048,
                 after=notify("rest", grads))
    return loss, grad_x, grads


def _all_gather(blocks, name):
    na = len(blocks)

    def body(*refs):
        ins, outs = refs[:na], refs[na:2 * na]
        send_sems, recv_sems, local_sems = refs[2 * na:]
        x, y, c = lax.axis_index("x"), lax.axis_index("y"), lax.axis_index("c")
        me, sibling = (x, y, c), (x, y, 1 - c)
        chips = [(1 - x, y), (x, 1 - y), (1 - x, 1 - y)]

        def slot(out, pos):
            return out.at[4 * pos[0] + 2 * pos[1] + pos[2]]

        def copy(a, k, block, to, src=None):
            return pltpu.make_async_remote_copy(
                src_ref=slot(outs[a], block) if src is None else src, dst_ref=slot(outs[a], block),
                send_sem=send_sems.at[7 * a + k], recv_sem=recv_sems.at[7 * a + k],
                device_id=to, device_id_type=pl.DeviceIdType.MESH)

        mine = [pltpu.make_async_copy(ins[a], slot(outs[a], me), local_sems.at[a]) for a in range(na)]
        for cp in mine:
            cp.start()
        first = []
        for a in range(na):
            first.append(copy(a, 0, me, sibling, src=ins[a]))
            first += [copy(a, 1 + j, me, (*chip, c), src=ins[a]) for j, chip in enumerate(chips)]
        for cp in first:
            cp.start()
        passed = []
        for j, chip in enumerate(chips):
            for a in range(na):
                copy(a, 1 + j, (*chip, c), me).wait_recv()
                cp = copy(a, 4 + j, (*chip, c), sibling)
                cp.start()
                passed.append(cp)
        for a in range(na):
            copy(a, 0, sibling, me).wait_recv()
            for j, chip in enumerate(chips):
                copy(a, 4 + j, (*chip, 1 - c), me).wait_recv()
        for cp in first + passed:
            cp.wait_send()
        for cp in mine:
            cp.wait()

    any_spec = pl.BlockSpec(memory_space=pl.ANY)
    return pl.pallas_call(
        body, name=name, in_specs=[any_spec] * na, out_specs=[any_spec] * na,
        out_shape=[jax.ShapeDtypeStruct((N_DEV,) + b.shape, b.dtype) for b in blocks],
        scratch_shapes=[pltpu.SemaphoreType.DMA((7 * na,)), pltpu.SemaphoreType.DMA((7 * na,)), pltpu.SemaphoreType.DMA((na,))],
    )(*blocks)


_HBM_SPEC = pl.BlockSpec(memory_space=pltpu.HBM)
_SEM_SPEC = pl.BlockSpec(memory_space=pltpu.SEMAPHORE)
_DATAFLOW = pltpu.CompilerParams(has_side_effects=pltpu.SideEffectType.DATAFLOW_SIDE_EFFECTING)


def _split_copies(ins, lands, send_sems, recv_sems, gather):
    x, y, c = lax.axis_index("x"), lax.axis_index("y"), lax.axis_index("c")
    me = 4 * x + 2 * y + c
    copies = []
    for a in range(len(ins)):
        for d in range(1, N_DEV):
            px, py, pc = x ^ (d >> 2), y ^ ((d >> 1) & 1), c ^ (d & 1)
            copies.append(pltpu.make_async_remote_copy(
                src_ref=ins[a] if gather else ins[a].at[4 * px + 2 * py + pc], dst_ref=lands[a].at[me],
                send_sem=send_sems.at[7 * a + d - 1], recv_sem=recv_sems.at[7 * a + d - 1],
                device_id=(px, py, pc), device_id_type=pl.DeviceIdType.MESH))
    return copies


def _send_start(srcs, gather, name):
    na = len(srcs)
    land_types = [pltpu.HBM(((N_DEV,) + s.shape) if gather else s.shape, s.dtype) for s in srcs]

    def body(*refs):
        ins, lands = refs[:na], refs[na:2 * na]
        send_sems, recv_sems, token = refs[2 * na], refs[2 * na + 1], refs[-1]
        for cp in _split_copies(ins, lands, send_sems, recv_sems, gather):
            cp.start()
        token[...] = jnp.zeros_like(token)

    hbm = lambda a: pltpu.with_memory_space_constraint(a, pltpu.HBM)
    outs = pl.pallas_call(
        body, name=name,
        out_shape=(pltpu.SemaphoreType.DMA((7 * na,)), pltpu.SemaphoreType.DMA((7 * na,)),
                   *[pltpu.HBM(s.shape, s.dtype) for s in srcs], *land_types, jax.ShapeDtypeStruct((8, LANES), F32)),
        in_specs=[_HBM_SPEC] * (2 * na),
        out_specs=(_SEM_SPEC, _SEM_SPEC, *[_HBM_SPEC] * (2 * na), pl.BlockSpec(memory_space=pltpu.VMEM)),
        input_output_aliases={i: 2 + i for i in range(2 * na)}, compiler_params=_DATAFLOW,
    )(*[hbm(s) for s in srcs], *[hbm(lax.empty(t.shape, t.dtype)) for t in land_types])
    return dict(send=outs[0], recv=outs[1], srcs=list(outs[2:2 + na]), lands=list(outs[2 + na:2 + 2 * na]), token=outs[-1],
                gather=gather)


def _send_wait(handle, after, name):
    na = len(handle["srcs"])
    gather = handle["gather"]

    def body(*refs):
        ins, lands = refs[:na], refs[na:2 * na]
        send_sems, recv_sems = refs[2 * na], refs[2 * na + 1]
        for cp in _split_copies(ins, lands, send_sems, recv_sems, gather):
            cp.wait_send()
            cp.wait_recv()

    both = handle["srcs"] + handle["lands"]
    outs = pl.pallas_call(
        body, name=name, out_shape=[pltpu.HBM(a.shape, a.dtype) for a in both],
        in_specs=[_HBM_SPEC] * (2 * na) + [_SEM_SPEC, _SEM_SPEC, pl.BlockSpec(memory_space=pl.ANY)],
        out_specs=[_HBM_SPEC] * (2 * na), input_output_aliases={i: i for i in range(2 * na)}, compiler_params=_DATAFLOW,
    )(*both, handle["send"], handle["recv"], after)
    return list(outs[:na]), list(outs[na:])


def _sum_parts(parts, name):
    npart, r, n = parts.shape
    tr = r if r <= 256 else max(t for t in range(16, 257, 16) if r % t == 0)

    def body(p_ref, o_ref):
        g = p_ref[0].astype(F32)
        for s in range(1, npart):
            g = g + p_ref[s].astype(F32)
        o_ref[...] = g

    return pl.pallas_call(
        body, name=name, grid=(r // tr,), in_specs=[pl.BlockSpec((npart, tr, n), lambda i: (0, i, 0))],
        out_specs=pl.BlockSpec((tr, n), lambda i: (i, 0)), out_shape=jax.ShapeDtypeStruct((r, n), F32),
        compiler_params=_params("parallel"),
    )(parts)


def _adamw(parts, w, m, v, name):
    npart, r, n = parts.shape
    tr = r if r <= 256 else max(t for t in range(16, 257, 16) if r % t == 0)
    c1 = 1.0 - ADAM_B1 ** ADAM_STEP
    c2 = 1.0 - ADAM_B2 ** ADAM_STEP

    def body(p_ref, w_ref, m_ref, v_ref, g_out, d_out, m_out, v_out):
        g = p_ref[0].astype(F32)
        for s in range(1, npart):
            g = g + p_ref[s].astype(F32)
        m_new = ADAM_B1 * m_ref[...] + (1.0 - ADAM_B1) * g
        v_new = ADAM_B2 * v_ref[...] + (1.0 - ADAM_B2) * (g * g)
        g_out[...] = g
        m_out[...] = m_new
        v_out[...] = v_new
        d_out[...] = -ADAM_LR * ((m_new / c1) / (jnp.sqrt(v_new / c2) + ADAM_EPS) + ADAM_WD * w_ref[...])

    blk = pl.BlockSpec((tr, n), lambda i: (i, 0))
    shp = jax.ShapeDtypeStruct((r, n), F32)
    return pl.pallas_call(
        body, name=name, grid=(r // tr,), in_specs=[pl.BlockSpec((npart, tr, n), lambda i: (0, i, 0)), blk, blk, blk],
        out_specs=[blk] * 4, out_shape=[shp] * 4, compiler_params=_params("parallel"),
    )(parts, w, m, v)


def _pack(arrs, dtype, row_multiple=16):
    flat = jnp.concatenate([a.reshape(-1).astype(dtype) for a in arrs])
    rows = -(-flat.shape[0] // PACK_COLS)
    rows = -(-rows // row_multiple) * row_multiple
    return jnp.pad(flat, (0, rows * PACK_COLS - flat.shape[0])).reshape(rows, PACK_COLS)


def _unpack(buf, shapes):
    lead = buf.shape[:-2]
    flat = buf.reshape(lead + (-1,))
    out, at = [], 0
    for s in shapes:
        size = math.prod(s)
        out.append(flat[..., at:at + size].reshape(lead + tuple(s)))
        at += size
    return out


REPLICATED = ("g_cq", "g_ckv", "w_uk", "w_uv", "ln1_g", "ln1_b", "conv_b", "ln2_g", "ln2_b")
ALL_WEIGHTS = ("w_in", "g_cq", "g_ckv", "w_uq", "w_uk", "w_uv", "w_o", "ln1_g", "ln1_b", "w_up", "conv_w", "conv_b",
               "w_down", "ln2_g", "ln2_b")


def kernel(x, w_in, g_cq, g_ckv, w_uq, w_uk, w_uv, w_o, ln1_g, ln1_b, w_up, conv_w, conv_b, w_down, ln2_g, ln2_b, loss_target, m_w_in, m_g_cq, m_g_ckv, m_w_uq, m_w_uk, m_w_uv, m_w_o, m_ln1_g, m_ln1_b, m_w_up, m_conv_w, m_conv_b, m_w_down, m_ln2_g, m_ln2_b, v_w_in, v_g_cq, v_g_ckv, v_w_uq, v_w_uk, v_w_uv, v_w_o, v_ln1_g, v_ln1_b, v_w_up, v_conv_w, v_conv_b, v_w_down, v_ln2_g, v_ln2_b):
    w = dict(w_in=w_in, g_cq=g_cq, g_ckv=g_ckv, w_uq=w_uq, w_uk=w_uk, w_uv=w_uv, w_o=w_o, ln1_g=ln1_g, ln1_b=ln1_b,
             w_up=w_up, conv_w=conv_w, conv_b=conv_b, w_down=w_down, ln2_g=ln2_g, ln2_b=ln2_b)
    m = dict(w_in=m_w_in, g_cq=m_g_cq, g_ckv=m_g_ckv, w_uq=m_w_uq, w_uk=m_w_uk, w_uv=m_w_uv, w_o=m_w_o, ln1_g=m_ln1_g,
             ln1_b=m_ln1_b, w_up=m_w_up, conv_w=m_conv_w, conv_b=m_conv_b, w_down=m_w_down, ln2_g=m_ln2_g, ln2_b=m_ln2_b)
    v = dict(w_in=v_w_in, g_cq=v_g_cq, g_ckv=v_g_ckv, w_uq=v_w_uq, w_uk=v_w_uk, w_uv=v_w_uv, w_o=v_w_o, ln1_g=v_ln1_g,
             ln1_b=v_ln1_b, w_up=v_w_up, conv_w=v_conv_w, conv_b=v_conv_b, w_down=v_w_down, ln2_g=v_ln2_g, ln2_b=v_ln2_b)
    me = 4 * lax.axis_index("x") + 2 * lax.axis_index("y") + lax.axis_index("c")
    wire = lambda a: a.astype(WIRE_DTYPE)
    n_in = w_in.shape[1]
    n_in_pad = -(-n_in // 16) * 16
    pad_taps = lambda a: jnp.pad(a, ((0, 8 - a.shape[0]), (0, 0)))

    own_slot = lambda buf, block: lax.dynamic_update_index_in_dim(buf, block, me, 0)
    blocks = lambda a: wire(a).reshape((N_DEV, a.shape[0] // N_DEV) + a.shape[1:])

    g_in, g_uq, g_conv = _all_gather(
        [jnp.pad(wire(w_in).T, ((0, n_in_pad - n_in), (0, 0))), wire(w_uq).reshape(w_uq.shape[0], -1), pad_taps(conv_w)],
        "gather_weights")
    late = _send_start([wire(w_o), wire(w_up).T, wire(w_down)], True, "gather_late_start")
    cw = dict(
        w_in_t=_split_pad_rows(g_in[:, :n_in].reshape(-1, D_MODEL)).astype(MXU_DTYPE),
        wq=_pad_heads(g_uq.reshape((-1,) + w_uq.shape[1:]), NOPE_DIM + ROPE_DIM),
        wk=_pad_heads(w_uk, NOPE_DIM), wv=_pad_heads(w_uv, HEAD_DIM),
        conv_w=_ffn_interleave(jnp.transpose(g_conv[:, :conv_w.shape[0]], (1, 0, 2)).reshape(conv_w.shape[0], -1), 1),
        g_cq=_row(g_cq), g_ckv=_row(g_ckv), ln1_g=_row(ln1_g), ln1_b=_row(ln1_b), conv_b=_ffn_interleave(_row(conv_b), 1),
        ln2_g=_row(ln2_g), ln2_b=_row(ln2_b))

    def late_weights(after):
        own, landed = _send_wait(late, after, "gather_late_wait")
        g_o, g_up, g_down = [own_slot(buf, blk) for buf, blk in zip(landed, own)]
        w_o_mla, w_o_dil = _pad_w_o(g_o.reshape(-1, D_MODEL))
        return dict(w_o_mla=w_o_mla, w_o_dil=w_o_dil, w_up_t=_ffn_interleave(g_up.reshape(-1, D_MODEL), 0).astype(MXU_DTYPE),
                    w_down=g_down.reshape(-1, D_MODEL).astype(MXU_DTYPE))

    sent = {}

    def on_grads(stage, g):
        if stage == "ffn":
            sent[stage] = [_send_start([blocks(_ffn_deinterleave(g["w_up_t"], 0)), blocks(g["w_down"])], False, "exchange_ffn_start")]
        elif stage == "w_o":
            sent[stage] = [_send_start([blocks(_unpad_w_o(g["w_o_mla"], g["w_o_dil"]))], False, "exchange_w_o_start")]
        else:
            d_in = jnp.pad(blocks(_split_unpad_rows(g["w_in_t"])), ((0, 0), (0, n_in_pad - n_in), (0, 0)))
            d_uq = blocks(_unpad_heads(g["wq"], NOPE_DIM + ROPE_DIM).reshape(Q_RANK, -1))
            nat = dict(g_cq=g["g_cq"], g_ckv=g["g_ckv"], w_uk=_unpad_heads(g["wk"], NOPE_DIM),
                       w_uv=_unpad_heads(g["wv"], HEAD_DIM), ln1_g=g["ln1_g"], ln1_b=g["ln1_b"],
                       conv_b=_ffn_deinterleave(g["conv_b"], 1), ln2_g=g["ln2_g"], ln2_b=g["ln2_b"])
            small = _pack([nat[n] for n in REPLICATED] + [g["loss"][:, :1]], F32, 8)
            sent[stage] = [_send_start([d_in, d_uq], False, "exchange_rest_start"),
                           _send_start([small, pad_taps(_ffn_deinterleave(g["conv_w"], 1))], True, "gather_small_start")]
        return [h["token"] for h in sent[stage]]

    _, grad_x, _ = _layer_grads(x[0], loss_target[0], cw, [late["token"]], late_weights, on_grads)

    def landed(handle, name):
        own, got = _send_wait(handle, grad_x, name)
        pick = (lambda a: a) if handle["gather"] else (lambda a: lax.dynamic_index_in_dim(a, me, 0, keepdims=False))
        return [own_slot(buf, pick(src)) for buf, src in zip(got, own)]

    r_up, r_down = landed(sent["ffn"][0], "exchange_ffn_wait")
    (r_o,) = landed(sent["w_o"][0], "exchange_w_o_wait")
    r_in, r_uq = landed(sent["rest"][0], "exchange_rest_wait")
    rep_all, cw_all = landed(sent["rest"][1], "gather_small_wait")

    out = {}

    def update(name, parts, shape2d=None):
        w2, m2, v2 = [d[name].reshape(shape2d or d[name].shape) for d in (w, m, v)]
        res = _adamw(parts, w2, m2, v2, "adamw_" + name)
        for kind, a in zip(("grad", "delta", "new_m", "new_v"), res):
            out[kind, name] = a.reshape(w[name].shape)

    update("w_in", _sum_parts(r_in, "sum_w_in")[:n_in].T[None])
    update("w_uq", r_uq, (w_uq.shape[0], -1))
    update("w_o", r_o)
    update("w_up", _sum_parts(r_up, "sum_w_up").T[None])
    update("w_down", r_down)
    slot = jnp.zeros((1,), F32)
    res = _adamw(rep_all, *[_pack([d[n] for n in REPLICATED] + [slot], F32, 8) for d in (w, m, v)], "adamw_replicated")
    for kind, buf in zip(("grad", "delta", "new_m", "new_v"), res):
        *arrs, total = _unpack(buf, [w[n].shape for n in REPLICATED] + [(1,)])
        for n, a in zip(REPLICATED, arrs):
            out[kind, n] = a
        if kind == "grad":
            loss = total[0]
    ncw = conv_w.shape[1]
    update("conv_w", lax.dynamic_slice_in_dim(cw_all[:, :conv_w.shape[0]], me * ncw, ncw, axis=2))

    return (loss, grad_x[None], *[out[kind, n] for kind in ("grad", "delta", "new_m", "new_v") for n in ALL_WEIGHTS])
```

```python
import functools
import math

import jax
import jax.numpy as jnp
from jax import lax
from jax.experimental import pallas as pl
from jax.experimental.pallas import tpu as pltpu

F32 = jnp.float32
MXU_DTYPE = jnp.bfloat16
WIRE_DTYPE = jnp.bfloat16

N_DEV = 8
D_MODEL = 1024
HEADS = 8
HEAD_DIM = 64
LANES = 128
Q_RANK, KV_RANK, ROPE_DIM, NOPE_DIM = 256, 128, 32, 64
DIL_WIDTH = HEADS * HEAD_DIM
IN_WIDTH = 1952
IN_PAD = 2048
D_FF = 2816
ROPE_THETA = 10000.0
DIL_PAIRS = ((128, 1), (512, 4), (2048, 16))
DIL_BLOCK = 128
DN_ALPHA = 2.0 ** 0.25
LN_EPS = 1e-5
RMS_EPS = 1e-6
MLA_SCALE = 1.0 / math.sqrt(NOPE_DIM + ROPE_DIM)
MLA_SCALE_LOG2 = MLA_SCALE * math.log2(math.e)
DIL_SCALE = 1.0 / math.sqrt(HEAD_DIM)
ALIBI_SLOPES = tuple(2.0 ** (-8.0 * (h + 1) / HEADS) for h in range(HEADS))
NEG_BIG = -1e30
ADAM_LR, ADAM_B1, ADAM_B2, ADAM_EPS, ADAM_WD, ADAM_STEP = 0.001, 0.9, 0.999, 1e-08, 0.01, 10
VMEM_LIMIT = 48 * 1024 * 1024


def _params(*sem):
    return pltpu.CompilerParams(dimension_semantics=sem or None, vmem_limit_bytes=VMEM_LIMIT)


def _dot(a, b, ca, cb):
    return lax.dot_general(a, b, (((ca,), (cb,)), ((), ())), preferred_element_type=F32)


_ANY_SPEC = pl.BlockSpec(memory_space=pl.ANY)


def _mm(a, b, *, name, tm, tn, tk, ta=False, tb=False, out_dtype=F32, res=None, res_scale=1.0, after=()):
    m, k = (a.shape[1], a.shape[0]) if ta else a.shape
    n = b.shape[0] if tb else b.shape[1]
    assert (b.shape[1] if tb else b.shape[0]) == k
    tm, tn, tk = min(tm, m), min(tn, n), min(tk, k)
    assert m % tm == 0 and n % tn == 0 and k % tk == 0, (name, m, n, k, tm, tn, tk)
    nk = k // tk
    a_spec = (pl.BlockSpec((tk, tm), lambda i, j, kk: (kk, i)) if ta
              else pl.BlockSpec((tm, tk), lambda i, j, kk: (i, kk)))
    b_mode = dict(pipeline_mode=pl.Buffered(1)) if (tn == n and tk == k) else {}
    b_spec = (pl.BlockSpec((tn, tk), lambda i, j, kk: (j, kk), **b_mode) if tb
              else pl.BlockSpec((tk, tn), lambda i, j, kk: (kk, j), **b_mode))
    o_spec = pl.BlockSpec((tm, tn), lambda i, j, kk: (i, j))
    in_specs = [a_spec, b_spec]
    args = [a, b]
    if res is not None:
        in_specs.append(o_spec)
        args.append(res)
    n_in = len(args) + len(after)
    in_specs += [_ANY_SPEC] * len(after)
    args += list(after)
    ca, cb = (0 if ta else 1), (1 if tb else 0)

    def finish(acc, r_ref, o_ref):
        if r_ref is not None:
            acc = acc + res_scale * r_ref[...]
        o_ref[...] = acc.astype(o_ref.dtype)

    def body(*refs):
        a_ref, b_ref = refs[:2]
        r_ref = refs[2] if res is not None else None
        o_ref = refs[n_in]
        part = _dot(a_ref[...].astype(MXU_DTYPE), b_ref[...].astype(MXU_DTYPE), ca, cb)
        if nk == 1:
            finish(part, r_ref, o_ref)
            return
        acc_ref = refs[-1]
        kk = pl.program_id(2)

        @pl.when(kk == 0)
        def _():
            acc_ref[...] = part

        @pl.when(kk > 0)
        def _():
            acc_ref[...] += part

        @pl.when(kk == nk - 1)
        def _():
            finish(acc_ref[...], r_ref, o_ref)

    return pl.pallas_call(
        body, name=name, grid=(m // tm, n // tn, nk), in_specs=in_specs, out_specs=o_spec,
        out_shape=jax.ShapeDtypeStruct((m, n), out_dtype),
        scratch_shapes=[pltpu.VMEM((tm, tn), F32)] if nk > 1 else [],
        compiler_params=_params("parallel", "parallel", "arbitrary"),
    )(*args)


def _mm_do_mla(dz, w_o_mla, tm=1024):
    seq, d = dz.shape
    tm = min(tm, seq)

    def body(a_ref, b_ref, o_ref):
        a = a_ref[...].astype(MXU_DTYPE)
        for hd in range(HEADS):
            o_ref[hd] = _dot(a, b_ref[LANES * hd:LANES * (hd + 1), :], 1, 1)

    return pl.pallas_call(
        body, name="mm_do_mla", grid=(seq // tm,),
        in_specs=[pl.BlockSpec((tm, d), lambda i: (i, 0)), pl.BlockSpec((HEADS * LANES, d), lambda i: (0, 0))],
        out_specs=pl.BlockSpec((HEADS, tm, LANES), lambda i: (0, i, 0)),
        out_shape=jax.ShapeDtypeStruct((HEADS, seq, LANES), F32), compiler_params=_params("parallel"),
    )(dz, w_o_mla)


def _mm_dw_o_mla(o_mla, dz, tk=1024):
    seq, d = dz.shape
    tk = min(tk, seq)
    nk = seq // tk

    def body(a_ref, b_ref, o_ref, acc_ref):
        kk = pl.program_id(0)

        @pl.when(kk == 0)
        def _():
            acc_ref[...] = jnp.zeros_like(acc_ref)

        b = b_ref[...].astype(MXU_DTYPE)
        for hd in range(HEADS):
            acc_ref[LANES * hd:LANES * (hd + 1), :] += _dot(a_ref[hd].astype(MXU_DTYPE), b, 0, 0)

        @pl.when(kk == nk - 1)
        def _():
            o_ref[...] = acc_ref[...].astype(o_ref.dtype)

    return pl.pallas_call(
        body, name="mm_dw_o_mla", grid=(nk,),
        in_specs=[pl.BlockSpec((HEADS, tk, LANES), lambda kk: (0, kk, 0)), pl.BlockSpec((tk, d), lambda kk: (kk, 0))],
        out_specs=pl.BlockSpec((HEADS * LANES, d), lambda kk: (0, 0)),
        out_shape=jax.ShapeDtypeStruct((HEADS * LANES, d), MXU_DTYPE),
        scratch_shapes=[pltpu.VMEM((HEADS * LANES, d), F32)], compiler_params=_params("arbitrary"),
    )(o_mla, dz)


def _rope_tables(seq):
    half = ROPE_DIM // 2
    freqs = ROPE_THETA ** (-jnp.arange(half, dtype=F32) / half)
    ang = jnp.arange(seq).astype(F32)[:, None] * freqs[None, :]
    cos, sin = jnp.cos(ang), jnp.sin(ang)
    one = jnp.ones((seq, NOPE_DIM), F32)
    tail = jnp.ones((seq, LANES - NOPE_DIM - ROPE_DIM), F32)
    ctab = jnp.concatenate([one, cos, cos, tail], axis=1)
    stab = jnp.concatenate([0 * one, -sin, sin, 0 * tail], axis=1)
    return ctab, stab


def _rope_swap(t):
    lane = lax.broadcasted_iota(jnp.int32, t.shape, 1)
    half = ROPE_DIM // 2
    return jnp.where(lane < NOPE_DIM + half, pltpu.roll(t, LANES - half, 1), pltpu.roll(t, half, 1))


def _rope(t, ctab, stab):
    return t * ctab + _rope_swap(t) * stab


def _rope_inv(t, ctab, stab):
    return t * ctab - _rope_swap(t) * stab


def _rms(x, g):
    r = lax.rsqrt(jnp.mean(x * x, axis=-1, keepdims=True) + RMS_EPS)
    xh = x * r
    return xh, r, xh * g


def _mla_prep(h, g_cq, g_ckv, wq, wk, wv, ctab, stab, tm=512):
    seq = h.shape[0]
    tm = min(tm, seq)

    def body(h_ref, gq_ref, gk_ref, wq_ref, wk_ref, wv_ref, c_ref, s_ref, q_out, k_out, v_out):
        hb = h_ref[...]
        ctab_, stab_ = c_ref[...], s_ref[...]
        _, _, cqn = _rms(hb[:, :Q_RANK], gq_ref[...])
        _, _, ckn = _rms(hb[:, Q_RANK:Q_RANK + KV_RANK], gk_ref[...])
        cqn = cqn.astype(MXU_DTYPE)
        ckn = ckn.astype(MXU_DTYPE)
        krr = _rope(hb[:, Q_RANK + KV_RANK:], ctab_, stab_)
        for hd in range(HEADS):
            q = _dot(cqn, wq_ref[hd], 1, 0)
            q_out[hd] = _rope(q, ctab_, stab_).astype(q_out.dtype)
            k_out[hd] = (_dot(ckn, wk_ref[hd], 1, 0) + krr).astype(k_out.dtype)
            v_out[hd] = _dot(ckn, wv_ref[hd], 1, 0).astype(v_out.dtype)

    full = lambda *shape: pl.BlockSpec(shape, lambda i: (0,) * len(shape))
    slab = pl.BlockSpec((HEADS, tm, LANES), lambda i: (0, i, 0))
    shp = jax.ShapeDtypeStruct((HEADS, seq, LANES), MXU_DTYPE)
    return pl.pallas_call(
        body, name="mla_prep", grid=(seq // tm,),
        in_specs=[pl.BlockSpec((tm, 512), lambda i: (i, 0)), full(1, Q_RANK), full(1, KV_RANK),
                  full(HEADS, Q_RANK, LANES), full(HEADS, KV_RANK, LANES), full(HEADS, KV_RANK, LANES),
                  pl.BlockSpec((tm, LANES), lambda i: (i, 0)), pl.BlockSpec((tm, LANES), lambda i: (i, 0))],
        out_specs=[slab, slab, slab], out_shape=[shp, shp, shp],
        compiler_params=_params("parallel"),
    )(h, g_cq, g_ckv, wq, wk, wv, ctab, stab)


def _mla_prep_bwd(h, g_cq, g_ckv, wq, wk, wv, ctab, stab, dq, dk, dv, tm=512, after=()):
    seq = h.shape[0]
    tm = min(tm, seq)
    n_after = len(after)

    def body(h_ref, gq_ref, gk_ref, wq_ref, wk_ref, wv_ref, c_ref, s_ref, dq_ref, dk_ref, dv_ref, *rest):
        dh_ref, dwq_ref, dwk_ref, dwv_ref, dgq_ref, dgk_ref = rest[n_after:]

        @pl.when(pl.program_id(0) == 0)
        def _():
            for r in (dwq_ref, dwk_ref, dwv_ref, dgq_ref, dgk_ref):
                r[...] = jnp.zeros_like(r)

        hb = h_ref[...]
        ctab_, stab_ = c_ref[...], s_ref[...]
        gq, gk = gq_ref[...], gk_ref[...]
        xq, rq, cqn = _rms(hb[:, :Q_RANK], gq)
        xk, rk, ckn = _rms(hb[:, Q_RANK:Q_RANK + KV_RANK], gk)
        cqn = cqn.astype(MXU_DTYPE)
        ckn = ckn.astype(MXU_DTYPE)
        d_cqn = jnp.zeros((tm, Q_RANK), F32)
        d_ckn = jnp.zeros((tm, KV_RANK), F32)
        d_krr = jnp.zeros((tm, LANES), F32)
        for hd in range(HEADS):
            dqh = _rope_inv(dq_ref[hd], ctab_, stab_).astype(MXU_DTYPE)
            d_cqn += _dot(dqh, wq_ref[hd], 1, 1)
            dwq_ref[hd] += _dot(cqn, dqh, 0, 0)
            dkh = dk_ref[hd]
            d_krr += dkh
            dkh = dkh.astype(MXU_DTYPE)
            d_ckn += _dot(dkh, wk_ref[hd], 1, 1)
            dwk_ref[hd] += _dot(ckn, dkh, 0, 0)
            dvh = dv_ref[hd].astype(MXU_DTYPE)
            d_ckn += _dot(dvh, wv_ref[hd], 1, 1)
            dwv_ref[hd] += _dot(ckn, dvh, 0, 0)
        lane = lax.broadcasted_iota(jnp.int32, (tm, LANES), 1)
        rot = (lane >= NOPE_DIM) & (lane < NOPE_DIM + ROPE_DIM)
        d_kr = jnp.where(rot, _rope_inv(jnp.where(rot, d_krr, 0.0), ctab_, stab_), 0.0)

        def rms_bwd(dy, xh, r, g, dg_ref):
            dg_ref[...] += jnp.sum(dy * xh, axis=0, keepdims=True)
            dxh = dy * g
            return r * (dxh - xh * jnp.mean(dxh * xh, axis=-1, keepdims=True))

        d_cq = rms_bwd(d_cqn, xq, rq, gq, dgq_ref)
        d_ck = rms_bwd(d_ckn, xk, rk, gk, dgk_ref)
        dh_ref[...] = jnp.concatenate([d_cq, d_ck, d_kr], axis=1).astype(dh_ref.dtype)

    full = lambda *shape: pl.BlockSpec(shape, lambda i: (0,) * len(shape))
    slab = pl.BlockSpec((HEADS, tm, LANES), lambda i: (0, i, 0))
    return pl.pallas_call(
        body, name="mla_prep_bwd", grid=(seq // tm,),
        in_specs=[pl.BlockSpec((tm, 512), lambda i: (i, 0)), full(1, Q_RANK), full(1, KV_RANK),
                  full(HEADS, Q_RANK, LANES), full(HEADS, KV_RANK, LANES), full(HEADS, KV_RANK, LANES),
                  pl.BlockSpec((tm, LANES), lambda i: (i, 0)), pl.BlockSpec((tm, LANES), lambda i: (i, 0)),
                  slab, slab, slab] + [_ANY_SPEC] * n_after,
        out_specs=[pl.BlockSpec((tm, 512), lambda i: (i, 0)), full(HEADS, Q_RANK, LANES), full(HEADS, KV_RANK, LANES),
                   full(HEADS, KV_RANK, LANES), full(1, Q_RANK), full(1, KV_RANK)],
        out_shape=[jax.ShapeDtypeStruct((seq, 512), MXU_DTYPE), jax.ShapeDtypeStruct((HEADS, Q_RANK, LANES), F32),
                   jax.ShapeDtypeStruct((HEADS, KV_RANK, LANES), F32), jax.ShapeDtypeStruct((HEADS, KV_RANK, LANES), F32),
                   jax.ShapeDtypeStruct((1, Q_RANK), F32), jax.ShapeDtypeStruct((1, KV_RANK), F32)],
        compiler_params=_params("arbitrary"),
    )(h, g_cq, g_ckv, wq, wk, wv, ctab, stab, dq, dk, dv, *after)


def _causal_mask(t):
    row = lax.broadcasted_iota(jnp.int32, (t, t), 0)
    col = lax.broadcasted_iota(jnp.int32, (t, t), 1)
    return row >= col


def _mla_attn_fwd(q, k, v, t=512):
    _, seq, _ = q.shape
    t = min(t, seq)

    def body(q_ref, k_ref, v_ref, o_ref, ob_ref, lse_ref, m_ref, l_ref, acc_ref, s_ref):
        i = pl.program_id(1)
        qb = q_ref[...]
        m_ref[...] = jnp.full_like(m_ref, NEG_BIG)
        l_ref[...] = jnp.zeros_like(l_ref)
        acc_ref[...] = jnp.zeros_like(acc_ref)

        def scores(j):
            return _dot(qb, k_ref[pl.ds(pl.multiple_of(j * t, t), t), :], 1, 1) * MLA_SCALE_LOG2

        def softmax_pv(j, s, masked):
            vb = v_ref[pl.ds(pl.multiple_of(j * t, t), t), :]
            if masked:
                s = jnp.where(_causal_mask(t), s, NEG_BIG)
            m_old = m_ref[...]
            m_new = jnp.maximum(m_old, jnp.max(s, axis=1, keepdims=True))
            p = jnp.exp2(s - m_new)
            a = jnp.exp2(m_old - m_new)
            l_ref[...] = a * l_ref[...] + jnp.sum(p, axis=1, keepdims=True)
            acc_ref[...] = a * acc_ref[...] + _dot(p.astype(MXU_DTYPE), vb, 1, 0)
            m_ref[...] = m_new

        s_ref[...] = scores(0)

        def loop_body(j, c):
            s_next = scores(j + 1)
            softmax_pv(j, s_ref[...], False)
            s_ref[...] = s_next
            return c

        lax.fori_loop(0, i, loop_body, 0)
        softmax_pv(i, s_ref[...], True)
        l = l_ref[...]
        o = acc_ref[...] * (1.0 / l)
        o_ref[...] = o
        ob_ref[...] = o.astype(ob_ref.dtype)
        lse_ref[...] = jnp.broadcast_to(m_ref[...] + jnp.log2(l), lse_ref.shape)

    blk = pl.BlockSpec((None, t, LANES), lambda h, i: (h, i, 0))
    whole = pl.BlockSpec((None, seq, LANES), lambda h, i: (h, 0, 0))
    shp = jax.ShapeDtypeStruct((HEADS, seq, LANES), F32)
    return pl.pallas_call(
        body, name="mla_attn_fwd", grid=(HEADS, seq // t),
        in_specs=[blk, whole, whole], out_specs=[blk, blk, blk],
        out_shape=[shp, jax.ShapeDtypeStruct((HEADS, seq, LANES), MXU_DTYPE), shp],
        scratch_shapes=[pltpu.VMEM((t, 1), F32), pltpu.VMEM((t, 1), F32), pltpu.VMEM((t, LANES), F32), pltpu.VMEM((t, t), F32)],
        compiler_params=_params("parallel", "arbitrary"),
    )(q, k, v)


def _mla_attn_bwd(q, k, v, o, lse, do, t=512):
    _, seq, _ = q.shape
    t = min(t, seq)
    nb = seq // t

    def body(q_ref, k_ref, v_ref, o_ref, lse_ref, do_ref, dq_ref, dk_ref, dv_ref, dl_ref, dka_ref, dva_ref):
        dq_ref[...] = jnp.zeros_like(dq_ref)

        def delta_body(i, c):
            rows = pl.ds(pl.multiple_of(i * t, t), t)
            dl_ref[rows, :] = jnp.sum(do_ref[rows, :] * o_ref[rows, :], axis=1, keepdims=True)
            return c

        lax.fori_loop(0, nb, delta_body, 0)

        def kblock(j, c):
            krows = pl.ds(pl.multiple_of(j * t, t), t)
            kb = k_ref[krows, :]
            vb = v_ref[krows, :]
            dka_ref[...] = jnp.zeros_like(dka_ref)
            dva_ref[...] = jnp.zeros_like(dva_ref)

            def qstep(i, masked):
                th = t // 2
                rows = [pl.ds(pl.multiple_of(i * t + hf * th, th), th) for hf in range(2)]
                qs = [q_ref[r, :] for r in rows]
                dos = [do_ref[r, :].astype(MXU_DTYPE) for r in rows]
                ss = [_dot(qb, kb, 1, 1) * MLA_SCALE_LOG2 for qb in qs]
                dps = [_dot(dob, vb, 1, 1) for dob in dos]
                for hf in range(2):
                    s = ss[hf]
                    if masked:
                        row = lax.broadcasted_iota(jnp.int32, (th, t), 0) + hf * th
                        s = jnp.where(row >= lax.broadcasted_iota(jnp.int32, (th, t), 1), s, NEG_BIG)
                    p = jnp.exp2(s - lse_ref[rows[hf], 0:1])
                    dva_ref[...] += _dot(p.astype(MXU_DTYPE), dos[hf], 0, 0)
                    ds = (p * (dps[hf] - dl_ref[rows[hf], :]) * MLA_SCALE).astype(MXU_DTYPE)
                    dka_ref[...] += _dot(ds, qs[hf], 0, 0)
                    dq_ref[rows[hf], :] += _dot(ds, kb, 1, 0)

            qstep(j, True)

            def qloop(i, c2):
                qstep(i, False)
                return c2

            lax.fori_loop(j + 1, nb, qloop, 0)
            dk_ref[krows, :] = dka_ref[...]
            dv_ref[krows, :] = dva_ref[...]
            return c

        lax.fori_loop(0, nb, kblock, 0)

    whole = pl.BlockSpec((None, seq, LANES), lambda h: (h, 0, 0))
    shp = jax.ShapeDtypeStruct((HEADS, seq, LANES), F32)
    return pl.pallas_call(
        body, name="mla_attn_bwd", grid=(HEADS,),
        in_specs=[whole] * 6, out_specs=[whole] * 3, out_shape=[shp] * 3,
        scratch_shapes=[pltpu.VMEM((seq, 1), F32), pltpu.VMEM((t, LANES), F32), pltpu.VMEM((t, LANES), F32)],
        compiler_params=_params("parallel"),
    )(q, k, v, o, lse, do)


DIL_CHUNK = DIL_BLOCK * max(d for _, d in DIL_PAIRS)
DIL_PAIR_LANES = 2 * HEAD_DIM
assert DIL_PAIR_LANES == LANES
DIL_UNROLL_FWD = 4
DIL_UNROLL_BWD = 4


def _dil_bias_tables(hp, dil):
    b = DIL_BLOCK
    iq = lax.broadcasted_iota(jnp.int32, (b, 2 * b), 0)
    ik = lax.broadcasted_iota(jnp.int32, (b, 2 * b), 1)
    off = iq + b - ik
    band = (off >= 0) & (off <= b)
    dist = (off * dil).astype(F32)
    every, first = [], []
    for hh in range(2):
        slope = jnp.where(hp == 0, ALIBI_SLOPES[hh], jnp.where(hp == 1, ALIBI_SLOPES[2 + hh],
                          jnp.where(hp == 2, ALIBI_SLOPES[4 + hh], ALIBI_SLOPES[6 + hh]))).astype(F32)
        bias = -slope * dist
        every.append(jnp.where(band, bias, NEG_BIG))
        first.append(jnp.where(band & (ik >= b), bias, NEG_BIG))
    return jnp.concatenate(every, axis=0), jnp.concatenate(first, axis=0)


def _dil_rows(start, dil):
    return pl.ds(start, DIL_BLOCK) if dil == 1 else pl.ds(start, DIL_BLOCK, stride=dil)


def _dil_block_pos(blk, c, dil):
    sc, r = blk // dil, blk % dil
    q0 = sc * (DIL_BLOCK * dil) + r
    kcur0 = c * DIL_CHUNK + q0
    first = kcur0 < DIL_BLOCK * dil
    kprev0 = jnp.where(first, kcur0, kcur0 - DIL_BLOCK * dil)
    return q0, kcur0, kprev0, first


def _pair_cols(hh):
    return slice(HEAD_DIM * hh, HEAD_DIM * (hh + 1))


def _first_head_lanes(shape):
    return lax.broadcasted_iota(jnp.int32, shape, 1) < HEAD_DIM


def _stack_pair(t):
    first = _first_head_lanes(t.shape)
    return jnp.concatenate([jnp.where(first, t, 0.0), jnp.where(first, 0.0, t)], axis=0).astype(MXU_DTYPE)


def _unstack_pair(t):
    rows = t.shape[0] // 2
    return jnp.where(_first_head_lanes((rows, t.shape[1])), t[:rows], t[rows:])


def _pair_column(t):
    return jnp.concatenate([t[:, 0:1], t[:, HEAD_DIM:HEAD_DIM + 1]], axis=0)


def _dil_fwd(h):
    seq = h.shape[0]
    assert seq % DIL_CHUNK == 0
    nblk = DIL_CHUNK // DIL_BLOCK
    rc = 256

    def body(q_ref, k_ref, v_ref, o_ref, ob_ref, lse_ref, *scr):
        o_scr, l_scr = scr[:3], scr[3:]
        hp, c = pl.program_id(0), pl.program_id(1)
        for bi, (_, dil) in enumerate(DIL_PAIRS):
            tables = _dil_bias_tables(hp, dil)

            def block(blk, carry, bi=bi, dil=dil, tables=tables):
                q0, kcur0, kprev0, first = _dil_block_pos(blk, c, dil)
                q2 = _stack_pair(q_ref[_dil_rows(q0, dil), :] * DIL_SCALE)
                kcat = jnp.concatenate([k_ref[_dil_rows(kprev0, dil), :], k_ref[_dil_rows(kcur0, dil), :]], axis=0).astype(MXU_DTYPE)
                vcat = jnp.concatenate([v_ref[_dil_rows(kprev0, dil), :], v_ref[_dil_rows(kcur0, dil), :]], axis=0).astype(MXU_DTYPE)
                s = _dot(q2, kcat, 1, 1) + jnp.where(first, tables[1], tables[0])
                mx = jnp.max(s, axis=1, keepdims=True)
                p = jnp.exp(s - mx)
                l = jnp.sum(p, axis=1, keepdims=True)
                o_scr[bi][_dil_rows(q0, dil), :] = _unstack_pair(_dot(p.astype(MXU_DTYPE), vcat, 1, 0) * (1.0 / l))
                l_scr[bi][_dil_rows(q0, dil), :] = _unstack_pair(jnp.broadcast_to(mx + jnp.log(l), (2 * DIL_BLOCK, LANES)))
                return carry

            lax.fori_loop(0, nblk, block, 0, unroll=DIL_UNROLL_FWD)

        def combine(i, carry):
            rows = pl.ds(pl.multiple_of(i * rc, rc), rc)
            ls = [l_scr[bi][rows, :] for bi in range(3)]
            mx = jnp.maximum(jnp.maximum(ls[0], ls[1]), ls[2])
            es = [jnp.exp(l - mx) for l in ls]
            den = es[0] + es[1] + es[2]
            o = (es[0] * o_scr[0][rows, :] + es[1] * o_scr[1][rows, :] + es[2] * o_scr[2][rows, :]) / den
            o_ref[rows, :] = o
            ob_ref[rows, :] = o.astype(ob_ref.dtype)
            lse_ref[rows, :] = mx + jnp.log(den)
            return carry

        lax.fori_loop(0, DIL_CHUNK // rc, combine, 0)

    nq = DIL_WIDTH // LANES
    chunk = lambda off: pl.BlockSpec((DIL_CHUNK, LANES), lambda hp, c: (c, off + hp))
    whole = lambda off: pl.BlockSpec((seq, LANES), lambda hp, c: (0, off + hp))
    shp = jax.ShapeDtypeStruct((seq, DIL_WIDTH), F32)
    return pl.pallas_call(
        body, name="dil_fwd", grid=(nq, seq // DIL_CHUNK),
        in_specs=[chunk(nq), whole(2 * nq), whole(3 * nq)], out_specs=[chunk(0), chunk(0), chunk(0)],
        out_shape=[shp, jax.ShapeDtypeStruct((seq, DIL_WIDTH), MXU_DTYPE), shp],
        scratch_shapes=[pltpu.VMEM((DIL_CHUNK, LANES), F32)] * 6,
        compiler_params=_params("parallel", "arbitrary"),
    )(h, h, h)


def _dil_bwd(h, o, lse, do):
    seq = h.shape[0]
    nblk = DIL_CHUNK // DIL_BLOCK
    rc = 256

    def body(q_ref, k_ref, v_ref, o_ref, lse_ref, do_ref, dq_ref, dk_ref, dv_ref, dl_scr):
        hp, c = pl.program_id(0), pl.program_id(1)

        @pl.when(c == 0)
        def _():
            dk_ref[...] = jnp.zeros_like(dk_ref)
            dv_ref[...] = jnp.zeros_like(dv_ref)

        def delta(i, carry):
            rows = pl.ds(pl.multiple_of(i * rc, rc), rc)
            prod = do_ref[rows, :] * o_ref[rows, :]
            dl_scr[rows, :] = jnp.concatenate(
                [jnp.broadcast_to(jnp.sum(prod[:, _pair_cols(hh)], axis=1, keepdims=True), (rc, HEAD_DIM)) for hh in range(2)], axis=1)
            return carry

        lax.fori_loop(0, DIL_CHUNK // rc, delta, 0)

        for bi, (_, dil) in enumerate(DIL_PAIRS):
            tables = _dil_bias_tables(hp, dil)

            def block(blk, carry, bi=bi, dil=dil, tables=tables):
                q0, kcur0, kprev0, first = _dil_block_pos(blk, c, dil)
                qrows = _dil_rows(q0, dil)
                q2 = _stack_pair(q_ref[qrows, :] * DIL_SCALE)
                kcat = jnp.concatenate([k_ref[_dil_rows(kprev0, dil), :], k_ref[_dil_rows(kcur0, dil), :]], axis=0).astype(MXU_DTYPE)
                vcat = jnp.concatenate([v_ref[_dil_rows(kprev0, dil), :], v_ref[_dil_rows(kcur0, dil), :]], axis=0).astype(MXU_DTYPE)
                do2 = _stack_pair(do_ref[qrows, :])
                s = _dot(q2, kcat, 1, 1) + jnp.where(first, tables[1], tables[0])
                p = jnp.exp(s - _pair_column(lse_ref[qrows, :]))
                dp = _dot(do2, vcat, 1, 1)
                ds = (p * (dp - _pair_column(dl_scr[qrows, :]))).astype(MXU_DTYPE)
                dq_b = _unstack_pair(_dot(ds, kcat, 1, 0)) * DIL_SCALE
                dk_b = _dot(ds, q2, 0, 0)
                dv_b = _dot(p.astype(MXU_DTYPE), do2, 0, 0)
                if bi == 0:
                    dq_ref[qrows, :] = dq_b
                else:
                    dq_ref[qrows, :] += dq_b
                dk_ref[_dil_rows(kprev0, dil), :] += dk_b[:DIL_BLOCK]
                dv_ref[_dil_rows(kprev0, dil), :] += dv_b[:DIL_BLOCK]
                dk_ref[_dil_rows(kcur0, dil), :] += dk_b[DIL_BLOCK:]
                dv_ref[_dil_rows(kcur0, dil), :] += dv_b[DIL_BLOCK:]
                return carry

            lax.fori_loop(0, nblk, block, 0, unroll=DIL_UNROLL_BWD)

    nq = DIL_WIDTH // LANES
    chunk = lambda off: pl.BlockSpec((DIL_CHUNK, LANES), lambda hp, c: (c, off + hp))
    whole = lambda off: pl.BlockSpec((seq, LANES), lambda hp, c: (0, off + hp))
    shp = jax.ShapeDtypeStruct((seq, DIL_WIDTH), F32)
    return pl.pallas_call(
        body, name="dil_bwd", grid=(nq, seq // DIL_CHUNK),
        in_specs=[chunk(nq), whole(2 * nq), whole(3 * nq), chunk(0), chunk(0), chunk(0)],
        out_specs=[chunk(0), whole(0), whole(0)], out_shape=[shp, shp, shp],
        scratch_shapes=[pltpu.VMEM((DIL_CHUNK, LANES), F32)],
        compiler_params=_params("parallel", "arbitrary"),
    )(h, h, h, o, lse, do)


def _assemble_dh(dh_mla, dq, dk, dv, tm=512):
    seq = dh_mla.shape[0]
    tm = min(tm, seq)

    def body(a_ref, q_ref, k_ref, v_ref, o_ref):
        for j, r in enumerate((a_ref, q_ref, k_ref, v_ref)):
            o_ref[:, 512 * j:512 * (j + 1)] = r[...].astype(o_ref.dtype)

    blk = pl.BlockSpec((tm, 512), lambda i: (i, 0))
    return pl.pallas_call(
        body, name="assemble_dh", grid=(seq // tm,), in_specs=[blk] * 4,
        out_specs=pl.BlockSpec((tm, IN_PAD), lambda i: (i, 0)), out_shape=jax.ShapeDtypeStruct((seq, IN_PAD), MXU_DTYPE),
        compiler_params=_params("parallel"),
    )(dh_mla, dq, dk, dv)


def _ln_stats(z):
    mu = jnp.mean(z, axis=-1, keepdims=True)
    zc = z - mu
    r = lax.rsqrt(jnp.mean(zc * zc, axis=-1, keepdims=True) + LN_EPS)
    return zc * r, r


def _ln_bwd_math(dy, xh, r, g):
    dxh = dy * g
    return r * (dxh - jnp.mean(dxh, axis=-1, keepdims=True) - xh * jnp.mean(dxh * xh, axis=-1, keepdims=True))


def _mix_ln1(o_mla, o_dil, w_o_mla, w_o_dil, x0, g, b, tm=512):
    seq, d = x0.shape
    tm = min(tm, seq)

    def body(om_ref, od_ref, wm_ref, wd_ref, x_ref, g_ref, b_ref, z_ref, y_ref, yb_ref):
        mix = _dot(od_ref[...], wd_ref[...], 1, 0)
        for hd in range(HEADS):
            mix += _dot(om_ref[hd], wm_ref[LANES * hd:LANES * (hd + 1), :], 1, 0)
        z = DN_ALPHA * x_ref[...] + mix
        xh, _ = _ln_stats(z)
        y = xh * g_ref[...] + b_ref[...]
        z_ref[...] = z
        y_ref[...] = y
        yb_ref[...] = y.astype(yb_ref.dtype)

    blk = pl.BlockSpec((tm, d), lambda i: (i, 0))
    vec = pl.BlockSpec((1, d), lambda i: (0, 0))
    shp = jax.ShapeDtypeStruct((seq, d), F32)
    return pl.pallas_call(
        body, name="mix_ln1", grid=(seq // tm,),
        in_specs=[pl.BlockSpec((HEADS, tm, LANES), lambda i: (0, i, 0)), pl.BlockSpec((tm, DIL_WIDTH), lambda i: (i, 0)),
                  pl.BlockSpec((HEADS * LANES, d), lambda i: (0, 0)), pl.BlockSpec((DIL_WIDTH, d), lambda i: (0, 0)), blk, vec, vec],
        out_specs=[blk, blk, blk], out_shape=[shp, shp, jax.ShapeDtypeStruct((seq, d), MXU_DTYPE)],
        compiler_params=_params("parallel"))(o_mla, o_dil, w_o_mla, w_o_dil, x0, g, b)


def _ln_bwd(dy, z, g, name, tm=512, after=()):
    seq, d = z.shape
    tm = min(tm, seq)
    n_after = len(after)

    def body(dy_ref, z_ref, g_ref, *rest):
        dz_ref, dzb_ref, dg_ref, db_ref = rest[n_after:]

        @pl.when(pl.program_id(0) == 0)
        def _():
            dg_ref[...] = jnp.zeros_like(dg_ref)
            db_ref[...] = jnp.zeros_like(db_ref)

        dyb = dy_ref[...]
        xh, r = _ln_stats(z_ref[...])
        dg_ref[...] += jnp.sum(dyb * xh, axis=0, keepdims=True)
        db_ref[...] += jnp.sum(dyb, axis=0, keepdims=True)
        dz = _ln_bwd_math(dyb, xh, r, g_ref[...])
        dz_ref[...] = dz
        dzb_ref[...] = dz.astype(dzb_ref.dtype)

    blk = pl.BlockSpec((tm, d), lambda i: (i, 0))
    vec = pl.BlockSpec((1, d), lambda i: (0, 0))
    return pl.pallas_call(
        body, name=name, grid=(seq // tm,), in_specs=[blk, blk, vec] + [_ANY_SPEC] * n_after, out_specs=[blk, blk, vec, vec],
        out_shape=[jax.ShapeDtypeStruct((seq, d), F32), jax.ShapeDtypeStruct((seq, d), MXU_DTYPE),
                   jax.ShapeDtypeStruct((1, d), F32), jax.ShapeDtypeStruct((1, d), F32)],
        compiler_params=_params("arbitrary"))(dy, z, g, *after)


def _ln2_loss_bwd(x1, ffn, target, g, b, tm=512):
    seq, d = x1.shape
    tm = min(tm, seq)

    def body(x_ref, f_ref, t_ref, g_ref, b_ref, dz_ref, dzb_ref, loss_ref, dg_ref, db_ref):
        @pl.when(pl.program_id(0) == 0)
        def _():
            loss_ref[...] = jnp.zeros_like(loss_ref)
            dg_ref[...] = jnp.zeros_like(dg_ref)
            db_ref[...] = jnp.zeros_like(db_ref)

        gv = g_ref[...]
        z = DN_ALPHA * x_ref[...] + f_ref[...]
        xh, r = _ln_stats(z)
        err = (xh * gv + b_ref[...]) - t_ref[...]
        loss_ref[...] += 0.5 * jnp.sum(jnp.mean(err * err, axis=-1, keepdims=True), axis=0, keepdims=True)
        dy = err * (1.0 / d)
        dg_ref[...] += jnp.sum(dy * xh, axis=0, keepdims=True)
        db_ref[...] += jnp.sum(dy, axis=0, keepdims=True)
        dz = _ln_bwd_math(dy, xh, r, gv)
        dz_ref[...] = dz
        dzb_ref[...] = dz.astype(dzb_ref.dtype)

    blk = pl.BlockSpec((tm, d), lambda i: (i, 0))
    vec = pl.BlockSpec((1, d), lambda i: (0, 0))
    return pl.pallas_call(
        body, name="ln2_loss_bwd", grid=(seq // tm,), in_specs=[blk, blk, blk, vec, vec],
        out_specs=[blk, blk, pl.BlockSpec((1, LANES), lambda i: (0, 0)), vec, vec],
        out_shape=[jax.ShapeDtypeStruct((seq, d), F32), jax.ShapeDtypeStruct((seq, d), MXU_DTYPE),
                   jax.ShapeDtypeStruct((1, LANES), F32),
                   jax.ShapeDtypeStruct((1, d), F32), jax.ShapeDtypeStruct((1, d), F32)],
        compiler_params=_params("arbitrary"))(x1, ffn, target, g, b)


HALO = 16


def _conv_rows(e, w_ref, b_ref):
    y = b_ref[...] + w_ref[0:1, :] * pltpu.roll(e, 2, 0)
    y = y + w_ref[1:2, :] * pltpu.roll(e, 1, 0)
    return y + w_ref[2:3, :] * e


_GELU_C = math.sqrt(2.0 / math.pi)
_GELU_A = 0.044715


def _gelu(x):
    return 0.5 * x * (1.0 + jnp.tanh(_GELU_C * (x + _GELU_A * (x * x * x))))


CONV_TN = 256


def _ffn_interleave(a, axis):
    shp = a.shape
    a = a.reshape(shp[:axis] + (2, D_FF // CONV_TN, CONV_TN) + shp[axis + 1:])
    return jnp.swapaxes(a, axis, axis + 1).reshape(shp)


def _ffn_deinterleave(a, axis):
    shp = a.shape
    a = a.reshape(shp[:axis] + (D_FF // CONV_TN, 2, CONV_TN) + shp[axis + 1:])
    return jnp.swapaxes(a, axis, axis + 1).reshape(shp)


def _conv_gate_fwd(u, conv_w, conv_b, tm=1024):
    seq = u.shape[0]
    tm = min(tm, seq)
    tn = CONV_TN

    def body(u_ref, up_ref, w_ref, b_ref, o_ref):
        first = pl.program_id(0) == 0
        e = jnp.concatenate([jnp.where(first, 0.0, up_ref[...]), u_ref[...]], axis=0)
        y = _conv_rows(e, w_ref, b_ref)[HALO:]
        o_ref[...] = (_gelu(y[:, tn:]) * y[:, :tn]).astype(o_ref.dtype)

    hb = tm // HALO
    return pl.pallas_call(
        body, name="conv_gate_fwd", grid=(seq // tm, D_FF // tn),
        in_specs=[pl.BlockSpec((tm, 2 * tn), lambda i, j: (i, j)),
                  pl.BlockSpec((HALO, 2 * tn), lambda i, j: (jnp.maximum(i * hb - 1, 0), j)),
                  pl.BlockSpec((3, 2 * tn), lambda i, j: (0, j)), pl.BlockSpec((1, 2 * tn), lambda i, j: (0, j))],
        out_specs=pl.BlockSpec((tm, tn), lambda i, j: (i, j)), out_shape=jax.ShapeDtypeStruct((seq, D_FF), MXU_DTYPE),
        compiler_params=_params("parallel", "parallel"),
    )(u, u, conv_w, conv_b)


def _conv_gate_bwd(u, d_act, conv_w, conv_b, tm=512):
    seq = u.shape[0]
    tm = min(tm, seq)
    tn = CONV_TN
    ni = seq // tm
    rows_e = tm + 2 * HALO

    def body(u_ref, up_ref, un_ref, da_ref, dan_ref, w_ref, b_ref, du_ref, dw_ref, db_ref):
        i = pl.program_id(1)
        first, last = i == 0, i == ni - 1

        @pl.when(i == 0)
        def _():
            dw_ref[...] = jnp.zeros_like(dw_ref)
            db_ref[...] = jnp.zeros_like(db_ref)

        e = jnp.concatenate([jnp.where(first, 0.0, up_ref[...]), u_ref[...], jnp.where(last, 0.0, un_ref[...])], axis=0)
        y = _conv_rows(e, w_ref, b_ref)
        ya, yg = y[:, :tn], y[:, tn:]
        dact = jnp.concatenate([jnp.zeros((HALO, tn), F32), da_ref[...].astype(F32),
                                jnp.where(last, 0.0, dan_ref[...].astype(F32))], axis=0)
        th = jnp.tanh(_GELU_C * (yg + _GELU_A * (yg * yg * yg)))
        gelu = 0.5 * yg * (1.0 + th)
        gelu_grad = 0.5 * (1.0 + th) + 0.5 * yg * (1.0 - th * th) * (_GELU_C * (1.0 + 3.0 * _GELU_A * (yg * yg)))
        dy = jnp.concatenate([dact * gelu, dact * ya * gelu_grad], axis=1)
        du = w_ref[2:3, :] * dy + w_ref[1:2, :] * pltpu.roll(dy, rows_e - 1, 0) + w_ref[0:1, :] * pltpu.roll(dy, rows_e - 2, 0)
        du_ref[...] = du[HALO:HALO + tm].astype(du_ref.dtype)
        dyt = dy[HALO:HALO + tm]
        dw_ref[0:1, :] += jnp.sum(dyt * pltpu.roll(e, 2, 0)[HALO:HALO + tm], axis=0, keepdims=True)
        dw_ref[1:2, :] += jnp.sum(dyt * pltpu.roll(e, 1, 0)[HALO:HALO + tm], axis=0, keepdims=True)
        dw_ref[2:3, :] += jnp.sum(dyt * e[HALO:HALO + tm], axis=0, keepdims=True)
        db_ref[...] += jnp.sum(dyt, axis=0, keepdims=True)

    hb = tm // HALO
    nh = seq // HALO
    prev = lambda j, i: (jnp.maximum(i * hb - 1, 0), j)
    nxt = lambda j, i: (jnp.minimum((i + 1) * hb, nh - 1), j)
    return pl.pallas_call(
        body, name="conv_gate_bwd", grid=(D_FF // tn, ni),
        in_specs=[pl.BlockSpec((tm, 2 * tn), lambda j, i: (i, j)), pl.BlockSpec((HALO, 2 * tn), prev),
                  pl.BlockSpec((HALO, 2 * tn), nxt), pl.BlockSpec((tm, tn), lambda j, i: (i, j)), pl.BlockSpec((HALO, tn), nxt),
                  pl.BlockSpec((3, 2 * tn), lambda j, i: (0, j)), pl.BlockSpec((1, 2 * tn), lambda j, i: (0, j))],
        out_specs=[pl.BlockSpec((tm, 2 * tn), lambda j, i: (i, j)), pl.BlockSpec((3, 2 * tn), lambda j, i: (0, j)),
                   pl.BlockSpec((1, 2 * tn), lambda j, i: (0, j))],
        out_shape=[jax.ShapeDtypeStruct((seq, 2 * D_FF), MXU_DTYPE), jax.ShapeDtypeStruct((3, 2 * D_FF), F32),
                   jax.ShapeDtypeStruct((1, 2 * D_FF), F32)],
        compiler_params=_params("parallel", "arbitrary"),
    )(u, u, u, d_act, d_act, conv_w, conv_b)


def _pad_heads(w, width):
    w = jnp.transpose(w, (1, 0, 2))
    return jnp.pad(w, ((0, 0), (0, 0), (0, LANES - width))).astype(MXU_DTYPE)


def _unpad_heads(d, width):
    return jnp.transpose(d[:, :, :width], (1, 0, 2))


def _split_pad_rows(w_t):
    z = lambda n: jnp.zeros((n, w_t.shape[1]), w_t.dtype)
    return jnp.concatenate([w_t[:384], z(64), w_t[384:416], z(32), w_t[416:]], axis=0)


def _split_unpad_rows(w_p):
    return jnp.concatenate([w_p[:384], w_p[448:480], w_p[512:]], axis=0)


def _pad_w_o(w_o):
    mla = jnp.pad(w_o[:512].reshape(HEADS, HEAD_DIM, D_MODEL), ((0, 0), (0, LANES - HEAD_DIM), (0, 0)))
    return mla.reshape(HEADS * LANES, D_MODEL).astype(MXU_DTYPE), w_o[512:].astype(MXU_DTYPE)


def _unpad_w_o(d_mla, d_dil):
    return jnp.concatenate([d_mla.reshape(HEADS, LANES, D_MODEL)[:, :HEAD_DIM].reshape(512, D_MODEL), d_dil], axis=0)


def _row(v):
    return v.reshape(1, -1).astype(F32)


def _compute_weights(w):
    w_o_mla, w_o_dil = _pad_w_o(w["w_o"])
    return dict(
        w_in_t=_split_pad_rows(w["w_in"].T).astype(MXU_DTYPE), wq=_pad_heads(w["w_uq"], NOPE_DIM + ROPE_DIM),
        wk=_pad_heads(w["w_uk"], NOPE_DIM), wv=_pad_heads(w["w_uv"], HEAD_DIM), w_o_mla=w_o_mla, w_o_dil=w_o_dil,
        w_up_t=_ffn_interleave(w["w_up"].T, 0).astype(MXU_DTYPE), w_down=w["w_down"].astype(MXU_DTYPE),
        conv_w=_ffn_interleave(w["conv_w"].astype(F32), 1),
        g_cq=_row(w["g_cq"]), g_ckv=_row(w["g_ckv"]), ln1_g=_row(w["ln1_g"]), ln1_b=_row(w["ln1_b"]),
        conv_b=_ffn_interleave(_row(w["conv_b"]), 1), ln2_g=_row(w["ln2_g"]), ln2_b=_row(w["ln2_b"]))


def _natural_grads(g):
    return dict(
        w_in=_split_unpad_rows(g["w_in_t"]).T, g_cq=g["g_cq"].reshape(-1), g_ckv=g["g_ckv"].reshape(-1),
        w_uq=_unpad_heads(g["wq"], NOPE_DIM + ROPE_DIM), w_uk=_unpad_heads(g["wk"], NOPE_DIM),
        w_uv=_unpad_heads(g["wv"], HEAD_DIM), w_o=_unpad_w_o(g["w_o_mla"], g["w_o_dil"]), ln1_g=g["ln1_g"].reshape(-1),
        ln1_b=g["ln1_b"].reshape(-1), w_up=_ffn_deinterleave(g["w_up_t"], 0).T, conv_w=_ffn_deinterleave(g["conv_w"], 1),
        conv_b=_ffn_deinterleave(g["conv_b"], 1).reshape(-1),
        w_down=g["w_down"], ln2_g=g["ln2_g"].reshape(-1), ln2_b=g["ln2_b"].reshape(-1))


def _layer_grads(x0, target, cw, first_after=(), late_weights=None, on_grads=None):
    seq = x0.shape[0]
    ctab, stab = _rope_tables(seq)
    gq, gk = cw["g_cq"], cw["g_ckv"]
    wq, wk, wv = cw["wq"], cw["wk"], cw["wv"]
    x0b = x0.astype(MXU_DTYPE)
    notify = (lambda stage, grads: ()) if on_grads is None else on_grads

    h = _mm(x0b, cw["w_in_t"], name="mm_h", tb=True, tm=1024, tn=1024, tk=1024, after=first_after)
    qf, kf, vp = _mla_prep(h, gq, gk, wq, wk, wv, ctab, stab)
    o_mla, o_mla_b, lse_mla = _mla_attn_fwd(qf, kf, vp)
    o_dil, o_dil_b, lse_dil = _dil_fwd(h)
    if late_weights is not None:
        cw = {**cw, **late_weights(o_mla_b)}
    cb = cw["conv_b"]
    z1, x1, x1b = _mix_ln1(o_mla_b, o_dil_b, cw["w_o_mla"], cw["w_o_dil"], x0, cw["ln1_g"], cw["ln1_b"])
    u = _mm(x1b, cw["w_up_t"], name="mm_up", tb=True, tm=1024, tn=1408, tk=1024)
    act = _conv_gate_fwd(u, cw["conv_w"], cb)
    ffn = _mm(act, cw["w_down"], name="mm_down", tm=1024, tn=1024, tk=2816)
    dz2, dz2b, loss, d_ln2_g, d_ln2_b = _ln2_loss_bwd(x1, ffn, target, cw["ln2_g"], cw["ln2_b"])

    d_act = _mm(dz2b, cw["w_down"], name="mm_d_act", tb=True, out_dtype=MXU_DTYPE, tm=1024, tn=1408, tk=1024)
    d_w_down = _mm(act, dz2b, name="mm_dw_down", ta=True, out_dtype=MXU_DTYPE, tm=1408, tn=1024, tk=1024)
    du, d_conv_w, d_conv_b = _conv_gate_bwd(u, d_act, cw["conv_w"], cb)
    dx1 = _mm(du, cw["w_up_t"], name="mm_dx1", res=dz2, res_scale=DN_ALPHA, tm=512, tn=1024, tk=2 * D_FF)
    d_w_up_t = _mm(du, x1b, name="mm_dw_up", ta=True, out_dtype=MXU_DTYPE, tm=1408, tn=1024, tk=2048)
    grads = dict(w_up_t=d_w_up_t, w_down=d_w_down, conv_w=d_conv_w, conv_b=d_conv_b, ln2_g=d_ln2_g, ln2_b=d_ln2_b)
    dz1, dz1b, d_ln1_g, d_ln1_b = _ln_bwd(dx1, z1, cw["ln1_g"], "ln1_bwd", after=notify("ffn", grads))
    do_mla = _mm_do_mla(dz1b, cw["w_o_mla"])
    do_dil = _mm(dz1b, cw["w_o_dil"], name="mm_do_dil", tb=True, tm=1024, tn=512, tk=1024)
    d_w_o_mla = _mm_dw_o_mla(o_mla_b, dz1b)
    d_w_o_dil = _mm(o_dil_b, dz1b, name="mm_dw_o_dil", ta=True, out_dtype=MXU_DTYPE, tm=512, tn=1024, tk=1024)
    grads.update(w_o_mla=d_w_o_mla, w_o_dil=d_w_o_dil, ln1_g=d_ln1_g, ln1_b=d_ln1_b)
    dq_dil, dk_dil, dv_dil = _dil_bwd(h, o_dil, lse_dil, do_dil)
    dqf, dkf, dvf = _mla_attn_bwd(qf, kf, vp, o_mla, lse_mla, do_mla)
    dh_mla, d_wq, d_wk, d_wv, d_gq, d_gk = _mla_prep_bwd(h, gq, gk, wq, wk, wv, ctab, stab, dqf, dkf, dvf,
                                                          after=notify("w_o", grads))
    dh = _assemble_dh(dh_mla, dq_dil, dk_dil, dv_dil)
    d_w_in_t = _mm(dh, x0b, name="mm_dw_in", ta=True, out_dtype=MXU_DTYPE, tm=1024, tn=1024, tk=1024)
    grads.update(w_in_t=d_w_in_t, wq=d_wq, wk=d_wk, wv=d_wv, g_cq=d_gq, g_ckv=d_gk, loss=loss)
    grad_x = _mm(dh, cw["w_in_t"], name="mm_dx0", res=dz1, res_scale=DN_ALPHA, tm=1024, tn=1024, tk=2048,
                 after=notify("rest", grads))
    return loss, grad_x, grads


def _all_gather(blocks, name):
    na = len(blocks)

    def body(*refs):
        ins, outs = refs[:na], refs[na:2 * na]
        send_sems, recv_sems, local_sems = refs[2 * na:]
        x, y, c = lax.axis_index("x"), lax.axis_index("y"), lax.axis_index("c")
        me, sibling = (x, y, c), (x, y, 1 - c)
        chips = [(1 - x, y), (x, 1 - y), (1 - x, 1 - y)]

        def slot(out, pos):
            return out.at[4 * pos[0] + 2 * pos[1] + pos[2]]

        def copy(a, k, block, to, src=None):
            return pltpu.make_async_remote_copy(
                src_ref=slot(outs[a], block) if src is None else src, dst_ref=slot(outs[a], block),
                send_sem=send_sems.at[7 * a + k], recv_sem=recv_sems.at[7 * a + k],
                device_id=to, device_id_type=pl.DeviceIdType.MESH)

        mine = [pltpu.make_async_copy(ins[a], slot(outs[a], me), local_sems.at[a]) for a in range(na)]
        for cp in mine:
            cp.start()
        first = []
        for a in range(na):
            first.append(copy(a, 0, me, sibling, src=ins[a]))
            first += [copy(a, 1 + j, me, (*chip, c), src=ins[a]) for j, chip in enumerate(chips)]
        for cp in first:
            cp.start()
        passed = []
        for j, chip in enumerate(chips):
            for a in range(na):
                copy(a, 1 + j, (*chip, c), me).wait_recv()
                cp = copy(a, 4 + j, (*chip, c), sibling)
                cp.start()
                passed.append(cp)
        for a in range(na):
            copy(a, 0, sibling, me).wait_recv()
            for j, chip in enumerate(chips):
                copy(a, 4 + j, (*chip, 1 - c), me).wait_recv()
        for cp in first + passed:
            cp.wait_send()
        for cp in mine:
            cp.wait()

    any_spec = pl.BlockSpec(memory_space=pl.ANY)
    return pl.pallas_call(
        body, name=name, in_specs=[any_spec] * na, out_specs=[any_spec] * na,
        out_shape=[jax.ShapeDtypeStruct((N_DEV,) + b.shape, b.dtype) for b in blocks],
        scratch_shapes=[pltpu.SemaphoreType.DMA((7 * na,)), pltpu.SemaphoreType.DMA((7 * na,)), pltpu.SemaphoreType.DMA((na,))],
    )(*blocks)


_HBM_SPEC = pl.BlockSpec(memory_space=pltpu.HBM)
_SEM_SPEC = pl.BlockSpec(memory_space=pltpu.SEMAPHORE)
_DATAFLOW = pltpu.CompilerParams(has_side_effects=pltpu.SideEffectType.DATAFLOW_SIDE_EFFECTING)


def _split_copies(ins, lands, send_sems, recv_sems, gather):
    x, y, c = lax.axis_index("x"), lax.axis_index("y"), lax.axis_index("c")
    me = 4 * x + 2 * y + c
    copies = []
    for a in range(len(ins)):
        for d in range(1, N_DEV):
            px, py, pc = x ^ (d >> 2), y ^ ((d >> 1) & 1), c ^ (d & 1)
            copies.append(pltpu.make_async_remote_copy(
                src_ref=ins[a] if gather else ins[a].at[4 * px + 2 * py + pc], dst_ref=lands[a].at[me],
                send_sem=send_sems.at[7 * a + d - 1], recv_sem=recv_sems.at[7 * a + d - 1],
                device_id=(px, py, pc), device_id_type=pl.DeviceIdType.MESH))
    return copies


def _send_start(srcs, gather, name):
    na = len(srcs)
    land_types = [pltpu.HBM(((N_DEV,) + s.shape) if gather else s.shape, s.dtype) for s in srcs]

    def body(*refs):
        ins, lands = refs[:na], refs[na:2 * na]
        send_sems, recv_sems, token = refs[2 * na], refs[2 * na + 1], refs[-1]
        for cp in _split_copies(ins, lands, send_sems, recv_sems, gather):
            cp.start()
        token[...] = jnp.zeros_like(token)

    hbm = lambda a: pltpu.with_memory_space_constraint(a, pltpu.HBM)
    outs = pl.pallas_call(
        body, name=name,
        out_shape=(pltpu.SemaphoreType.DMA((7 * na,)), pltpu.SemaphoreType.DMA((7 * na,)),
                   *[pltpu.HBM(s.shape, s.dtype) for s in srcs], *land_types, jax.ShapeDtypeStruct((8, LANES), F32)),
        in_specs=[_HBM_SPEC] * (2 * na),
        out_specs=(_SEM_SPEC, _SEM_SPEC, *[_HBM_SPEC] * (2 * na), pl.BlockSpec(memory_space=pltpu.VMEM)),
        input_output_aliases={i: 2 + i for i in range(2 * na)}, compiler_params=_DATAFLOW,
    )(*[hbm(s) for s in srcs], *[hbm(lax.empty(t.shape, t.dtype)) for t in land_types])
    return dict(send=outs[0], recv=outs[1], srcs=list(outs[2:2 + na]), lands=list(outs[2 + na:2 + 2 * na]), token=outs[-1],
                gather=gather)


def _send_wait(handle, after, name):
    na = len(handle["srcs"])
    gather = handle["gather"]

    def body(*refs):
        ins, lands = refs[:na], refs[na:2 * na]
        send_sems, recv_sems = refs[2 * na], refs[2 * na + 1]
        for cp in _split_copies(ins, lands, send_sems, recv_sems, gather):
            cp.wait_send()
            cp.wait_recv()

    both = handle["srcs"] + handle["lands"]
    outs = pl.pallas_call(
        body, name=name, out_shape=[pltpu.HBM(a.shape, a.dtype) for a in both],
        in_specs=[_HBM_SPEC] * (2 * na) + [_SEM_SPEC, _SEM_SPEC, pl.BlockSpec(memory_space=pl.ANY)],
        out_specs=[_HBM_SPEC] * (2 * na), input_output_aliases={i: i for i in range(2 * na)}, compiler_params=_DATAFLOW,
    )(*both, handle["send"], handle["recv"], after)
    return list(outs[:na]), list(outs[na:])


def _sum_parts(parts, name):
    npart, r, n = parts.shape
    tr = r if r <= 256 else max(t for t in range(16, 257, 16) if r % t == 0)

    def body(p_ref, o_ref):
        g = p_ref[0].astype(F32)
        for s in range(1, npart):
            g = g + p_ref[s].astype(F32)
        o_ref[...] = g

    return pl.pallas_call(
        body, name=name, grid=(r // tr,), in_specs=[pl.BlockSpec((npart, tr, n), lambda i: (0, i, 0))],
        out_specs=pl.BlockSpec((tr, n), lambda i: (i, 0)), out_shape=jax.ShapeDtypeStruct((r, n), F32),
        compiler_params=_params("parallel"),
    )(parts)


def _sum_slots(p_ref):
    g = p_ref[0].astype(F32)
    for s in range(1, p_ref.shape[0]):
        g = g + p_ref[s].astype(F32)
    return g


def _adamw_refs(g, w_ref, m_ref, v_ref, g_out, d_out, m_out, v_out):
    c1 = 1.0 - ADAM_B1 ** ADAM_STEP
    c2 = 1.0 - ADAM_B2 ** ADAM_STEP
    m_new = ADAM_B1 * m_ref[...] + (1.0 - ADAM_B1) * g
    v_new = ADAM_B2 * v_ref[...] + (1.0 - ADAM_B2) * (g * g)
    g_out[...] = g
    m_out[...] = m_new
    v_out[...] = v_new
    d_out[...] = -ADAM_LR * ((m_new / c1) / (jnp.sqrt(v_new / c2) + ADAM_EPS) + ADAM_WD * w_ref[...])


def _adamw(parts, w, m, v, name):
    npart, r, n = parts.shape
    tr = r if r <= 256 else max(t for t in range(16, 257, 16) if r % t == 0)

    def body(p_ref, w_ref, m_ref, v_ref, g_out, d_out, m_out, v_out):
        _adamw_refs(_sum_slots(p_ref), w_ref, m_ref, v_ref, g_out, d_out, m_out, v_out)

    blk = pl.BlockSpec((tr, n), lambda i: (i, 0))
    shp = jax.ShapeDtypeStruct((r, n), F32)
    return pl.pallas_call(
        body, name=name, grid=(r // tr,), in_specs=[pl.BlockSpec((npart, tr, n), lambda i: (0, i, 0)), blk, blk, blk],
        out_specs=[blk] * 4, out_shape=[shp] * 4, compiler_params=_params("parallel"),
    )(parts, w, m, v)


def _adamw_small(parts, ws, ms, vs, loss_parts, name):
    n = len(parts)

    def body(*refs):
        ins, outs = refs[:4 * n + 1], refs[4 * n + 1:]
        for i in range(n):
            _adamw_refs(_sum_slots(ins[i]), ins[n + i], ins[2 * n + i], ins[3 * n + i], *outs[4 * i:4 * i + 4])
        outs[4 * n][...] = _sum_slots(ins[4 * n])

    out_shape = [jax.ShapeDtypeStruct(w.shape, F32) for w in ws for _ in range(4)]
    res = pl.pallas_call(body, name=name, out_shape=out_shape + [jax.ShapeDtypeStruct((1, LANES), F32)],
                         compiler_params=_params())(*parts, *ws, *ms, *vs, loss_parts)
    return [res[4 * i:4 * i + 4] for i in range(n)], res[4 * n]


REPLICATED = ("g_cq", "g_ckv", "w_uk", "w_uv", "ln1_g", "ln1_b", "conv_b", "ln2_g", "ln2_b")
ALL_WEIGHTS = ("w_in", "g_cq", "g_ckv", "w_uq", "w_uk", "w_uv", "w_o", "ln1_g", "ln1_b", "w_up", "conv_w", "conv_b",
               "w_down", "ln2_g", "ln2_b")


def kernel(x, w_in, g_cq, g_ckv, w_uq, w_uk, w_uv, w_o, ln1_g, ln1_b, w_up, conv_w, conv_b, w_down, ln2_g, ln2_b, loss_target, m_w_in, m_g_cq, m_g_ckv, m_w_uq, m_w_uk, m_w_uv, m_w_o, m_ln1_g, m_ln1_b, m_w_up, m_conv_w, m_conv_b, m_w_down, m_ln2_g, m_ln2_b, v_w_in, v_g_cq, v_g_ckv, v_w_uq, v_w_uk, v_w_uv, v_w_o, v_ln1_g, v_ln1_b, v_w_up, v_conv_w, v_conv_b, v_w_down, v_ln2_g, v_ln2_b):
    w = dict(w_in=w_in, g_cq=g_cq, g_ckv=g_ckv, w_uq=w_uq, w_uk=w_uk, w_uv=w_uv, w_o=w_o, ln1_g=ln1_g, ln1_b=ln1_b,
             w_up=w_up, conv_w=conv_w, conv_b=conv_b, w_down=w_down, ln2_g=ln2_g, ln2_b=ln2_b)
    m = dict(w_in=m_w_in, g_cq=m_g_cq, g_ckv=m_g_ckv, w_uq=m_w_uq, w_uk=m_w_uk, w_uv=m_w_uv, w_o=m_w_o, ln1_g=m_ln1_g,
             ln1_b=m_ln1_b, w_up=m_w_up, conv_w=m_conv_w, conv_b=m_conv_b, w_down=m_w_down, ln2_g=m_ln2_g, ln2_b=m_ln2_b)
    v = dict(w_in=v_w_in, g_cq=v_g_cq, g_ckv=v_g_ckv, w_uq=v_w_uq, w_uk=v_w_uk, w_uv=v_w_uv, w_o=v_w_o, ln1_g=v_ln1_g,
             ln1_b=v_ln1_b, w_up=v_w_up, conv_w=v_conv_w, conv_b=v_conv_b, w_down=v_w_down, ln2_g=v_ln2_g, ln2_b=v_ln2_b)
    me = 4 * lax.axis_index("x") + 2 * lax.axis_index("y") + lax.axis_index("c")
    wire = lambda a: a.astype(WIRE_DTYPE)
    n_in = w_in.shape[1]
    n_in_pad = -(-n_in // 16) * 16
    pad_taps = lambda a: jnp.pad(a, ((0, 8 - a.shape[0]), (0, 0)))

    own_slot = lambda buf, block: lax.dynamic_update_index_in_dim(buf, block, me, 0)
    blocks = lambda a: wire(a).reshape((N_DEV, a.shape[0] // N_DEV) + a.shape[1:])

    g_in, g_uq, g_conv = _all_gather(
        [jnp.pad(wire(w_in).T, ((0, n_in_pad - n_in), (0, 0))), wire(w_uq).reshape(w_uq.shape[0], -1), pad_taps(conv_w)],
        "gather_weights")
    late = _send_start([wire(w_o), wire(w_up).T, wire(w_down)], True, "gather_late_start")
    cw = dict(
        w_in_t=_split_pad_rows(g_in[:, :n_in].reshape(-1, D_MODEL)).astype(MXU_DTYPE),
        wq=_pad_heads(g_uq.reshape((-1,) + w_uq.shape[1:]), NOPE_DIM + ROPE_DIM),
        wk=_pad_heads(w_uk, NOPE_DIM), wv=_pad_heads(w_uv, HEAD_DIM),
        conv_w=_ffn_interleave(jnp.transpose(g_conv[:, :conv_w.shape[0]], (1, 0, 2)).reshape(conv_w.shape[0], -1), 1),
        g_cq=_row(g_cq), g_ckv=_row(g_ckv), ln1_g=_row(ln1_g), ln1_b=_row(ln1_b), conv_b=_ffn_interleave(_row(conv_b), 1),
        ln2_g=_row(ln2_g), ln2_b=_row(ln2_b))

    def late_weights(after):
        own, landed = _send_wait(late, after, "gather_late_wait")
        g_o, g_up, g_down = [own_slot(buf, blk) for buf, blk in zip(landed, own)]
        w_o_mla, w_o_dil = _pad_w_o(g_o.reshape(-1, D_MODEL))
        return dict(w_o_mla=w_o_mla, w_o_dil=w_o_dil, w_up_t=_ffn_interleave(g_up.reshape(-1, D_MODEL), 0).astype(MXU_DTYPE),
                    w_down=g_down.reshape(-1, D_MODEL).astype(MXU_DTYPE))

    sent = {}

    def on_grads(stage, g):
        if stage == "ffn":
            sent[stage] = [_send_start([blocks(_ffn_deinterleave(g["w_up_t"], 0)), blocks(g["w_down"])], False, "exchange_ffn_start")]
        elif stage == "w_o":
            sent[stage] = [_send_start([blocks(_unpad_w_o(g["w_o_mla"], g["w_o_dil"]))], False, "exchange_w_o_start")]
        else:
            d_in = jnp.pad(blocks(_split_unpad_rows(g["w_in_t"])), ((0, 0), (0, n_in_pad - n_in), (0, 0)))
            d_uq = blocks(_unpad_heads(g["wq"], NOPE_DIM + ROPE_DIM).reshape(Q_RANK, -1))
            small = dict(g_cq=g["g_cq"], g_ckv=g["g_ckv"], w_uk=_unpad_heads(g["wk"], NOPE_DIM).reshape(KV_RANK, -1),
                         w_uv=_unpad_heads(g["wv"], HEAD_DIM).reshape(KV_RANK, -1), ln1_g=g["ln1_g"], ln1_b=g["ln1_b"],
                         conv_b=_ffn_deinterleave(g["conv_b"], 1), ln2_g=g["ln2_g"], ln2_b=g["ln2_b"])
            sent[stage] = [_send_start([d_in, d_uq], False, "exchange_rest_start"),
                           _send_start([small[n] for n in REPLICATED] + [pad_taps(_ffn_deinterleave(g["conv_w"], 1)), g["loss"]],
                                       True, "gather_small_start")]
        return [h["token"] for h in sent[stage]]

    _, grad_x, _ = _layer_grads(x[0], loss_target[0], cw, [late["token"]], late_weights, on_grads)

    def landed(handle, name):
        own, got = _send_wait(handle, grad_x, name)
        pick = (lambda a: a) if handle["gather"] else (lambda a: lax.dynamic_index_in_dim(a, me, 0, keepdims=False))
        return [own_slot(buf, pick(src)) for buf, src in zip(got, own)]

    r_up, r_down = landed(sent["ffn"][0], "exchange_ffn_wait")
    (r_o,) = landed(sent["w_o"][0], "exchange_w_o_wait")
    r_in, r_uq = landed(sent["rest"][0], "exchange_rest_wait")
    *rep_all, cw_all, loss_all = landed(sent["rest"][1], "gather_small_wait")

    out = {}

    def update(name, parts, shape2d=None):
        w2, m2, v2 = [d[name].reshape(shape2d or d[name].shape) for d in (w, m, v)]
        res = _adamw(parts, w2, m2, v2, "adamw_" + name)
        for kind, a in zip(("grad", "delta", "new_m", "new_v"), res):
            out[kind, name] = a.reshape(w[name].shape)

    update("w_in", _sum_parts(r_in, "sum_w_in")[:n_in].T[None])
    update("w_uq", r_uq, (w_uq.shape[0], -1))
    update("w_o", r_o)
    update("w_up", _sum_parts(r_up, "sum_w_up").T[None])
    update("w_down", r_down)
    two_d = lambda n, a: a.reshape(rep_all[REPLICATED.index(n)].shape[1:])
    res, loss_sum = _adamw_small(rep_all, *[[two_d(n, d[n]) for n in REPLICATED] for d in (w, m, v)], loss_all, "adamw_replicated")
    for n, quad in zip(REPLICATED, res):
        for kind, a in zip(("grad", "delta", "new_m", "new_v"), quad):
            out[kind, n] = a.reshape(w[n].shape)
    loss = loss_sum[0, 0]
    ncw = conv_w.shape[1]
    update("conv_w", lax.dynamic_slice_in_dim(cw_all[:, :conv_w.shape[0]], me * ncw, ncw, axis=2))

    return (loss, grad_x[None], *[out[kind, n] for kind in ("grad", "delta", "new_m", "new_v") for n in ALL_WEIGHTS])
```

```python
import functools
import math

import jax
import jax.numpy as jnp
from jax import lax
from jax.experimental import pallas as pl
from jax.experimental.pallas import tpu as pltpu

F32 = jnp.float32
MXU_DTYPE = jnp.bfloat16
WIRE_DTYPE = jnp.bfloat16

N_DEV = 8
D_MODEL = 1024
HEADS = 8
HEAD_DIM = 64
LANES = 128
Q_RANK, KV_RANK, ROPE_DIM, NOPE_DIM = 256, 128, 32, 64
DIL_WIDTH = HEADS * HEAD_DIM
IN_WIDTH = 1952
IN_PAD = 2048
D_FF = 2816
ROPE_THETA = 10000.0
DIL_PAIRS = ((128, 1), (512, 4), (2048, 16))
DIL_BLOCK = 128
DN_ALPHA = 2.0 ** 0.25
LN_EPS = 1e-5
RMS_EPS = 1e-6
MLA_SCALE = 1.0 / math.sqrt(NOPE_DIM + ROPE_DIM)
MLA_SCALE_LOG2 = MLA_SCALE * math.log2(math.e)
DIL_SCALE = 1.0 / math.sqrt(HEAD_DIM)
ALIBI_SLOPES = tuple(2.0 ** (-8.0 * (h + 1) / HEADS) for h in range(HEADS))
NEG_BIG = -1e30
ADAM_LR, ADAM_B1, ADAM_B2, ADAM_EPS, ADAM_WD, ADAM_STEP = 0.001, 0.9, 0.999, 1e-08, 0.01, 10
VMEM_LIMIT = 48 * 1024 * 1024


def _params(*sem):
    return pltpu.CompilerParams(dimension_semantics=sem or None, vmem_limit_bytes=VMEM_LIMIT)


def _dot(a, b, ca, cb):
    return lax.dot_general(a, b, (((ca,), (cb,)), ((), ())), preferred_element_type=F32)


_ANY_SPEC = pl.BlockSpec(memory_space=pl.ANY)


def _mm(a, b, *, name, tm, tn, tk, ta=False, tb=False, out_dtype=F32, res=None, res_scale=1.0, after=()):
    m, k = (a.shape[1], a.shape[0]) if ta else a.shape
    n = b.shape[0] if tb else b.shape[1]
    assert (b.shape[1] if tb else b.shape[0]) == k
    tm, tn, tk = min(tm, m), min(tn, n), min(tk, k)
    assert m % tm == 0 and n % tn == 0 and k % tk == 0, (name, m, n, k, tm, tn, tk)
    nk = k // tk
    a_spec = (pl.BlockSpec((tk, tm), lambda i, j, kk: (kk, i)) if ta
              else pl.BlockSpec((tm, tk), lambda i, j, kk: (i, kk)))
    b_mode = dict(pipeline_mode=pl.Buffered(1)) if (tn == n and tk == k) else {}
    b_spec = (pl.BlockSpec((tn, tk), lambda i, j, kk: (j, kk), **b_mode) if tb
              else pl.BlockSpec((tk, tn), lambda i, j, kk: (kk, j), **b_mode))
    o_spec = pl.BlockSpec((tm, tn), lambda i, j, kk: (i, j))
    in_specs = [a_spec, b_spec]
    args = [a, b]
    if res is not None:
        in_specs.append(o_spec)
        args.append(res)
    n_in = len(args) + len(after)
    in_specs += [_ANY_SPEC] * len(after)
    args += list(after)
    ca, cb = (0 if ta else 1), (1 if tb else 0)

    def finish(acc, r_ref, o_ref):
        if r_ref is not None:
            acc = acc + res_scale * r_ref[...]
        o_ref[...] = acc.astype(o_ref.dtype)

    def body(*refs):
        a_ref, b_ref = refs[:2]
        r_ref = refs[2] if res is not None else None
        o_ref = refs[n_in]
        part = _dot(a_ref[...].astype(MXU_DTYPE), b_ref[...].astype(MXU_DTYPE), ca, cb)
        if nk == 1:
            finish(part, r_ref, o_ref)
            return
        acc_ref = refs[-1]
        kk = pl.program_id(2)

        @pl.when(kk == 0)
        def _():
            acc_ref[...] = part

        @pl.when(kk > 0)
        def _():
            acc_ref[...] += part

        @pl.when(kk == nk - 1)
        def _():
            finish(acc_ref[...], r_ref, o_ref)

    return pl.pallas_call(
        body, name=name, grid=(m // tm, n // tn, nk), in_specs=in_specs, out_specs=o_spec,
        out_shape=jax.ShapeDtypeStruct((m, n), out_dtype),
        scratch_shapes=[pltpu.VMEM((tm, tn), F32)] if nk > 1 else [],
        compiler_params=_params("parallel", "parallel", "arbitrary"),
    )(*args)


def _mm_do_mla(dz, w_o_mla, tm=1024):
    seq, d = dz.shape
    tm = min(tm, seq)

    def body(a_ref, b_ref, o_ref):
        a = a_ref[...].astype(MXU_DTYPE)
        for hd in range(HEADS):
            o_ref[hd] = _dot(a, b_ref[LANES * hd:LANES * (hd + 1), :], 1, 1)

    return pl.pallas_call(
        body, name="mm_do_mla", grid=(seq // tm,),
        in_specs=[pl.BlockSpec((tm, d), lambda i: (i, 0)), pl.BlockSpec((HEADS * LANES, d), lambda i: (0, 0))],
        out_specs=pl.BlockSpec((HEADS, tm, LANES), lambda i: (0, i, 0)),
        out_shape=jax.ShapeDtypeStruct((HEADS, seq, LANES), F32), compiler_params=_params("parallel"),
    )(dz, w_o_mla)


def _mm_dw_o_mla(o_mla, dz, tk=1024):
    seq, d = dz.shape
    tk = min(tk, seq)
    nk = seq // tk

    def body(a_ref, b_ref, o_ref, acc_ref):
        kk = pl.program_id(0)

        @pl.when(kk == 0)
        def _():
            acc_ref[...] = jnp.zeros_like(acc_ref)

        b = b_ref[...].astype(MXU_DTYPE)
        for hd in range(HEADS):
            acc_ref[LANES * hd:LANES * (hd + 1), :] += _dot(a_ref[hd].astype(MXU_DTYPE), b, 0, 0)

        @pl.when(kk == nk - 1)
        def _():
            o_ref[...] = acc_ref[...].astype(o_ref.dtype)

    return pl.pallas_call(
        body, name="mm_dw_o_mla", grid=(nk,),
        in_specs=[pl.BlockSpec((HEADS, tk, LANES), lambda kk: (0, kk, 0)), pl.BlockSpec((tk, d), lambda kk: (kk, 0))],
        out_specs=pl.BlockSpec((HEADS * LANES, d), lambda kk: (0, 0)),
        out_shape=jax.ShapeDtypeStruct((HEADS * LANES, d), MXU_DTYPE),
        scratch_shapes=[pltpu.VMEM((HEADS * LANES, d), F32)], compiler_params=_params("arbitrary"),
    )(o_mla, dz)


def _rope_tables(seq):
    half = ROPE_DIM // 2
    freqs = ROPE_THETA ** (-jnp.arange(half, dtype=F32) / half)
    ang = jnp.arange(seq).astype(F32)[:, None] * freqs[None, :]
    cos, sin = jnp.cos(ang), jnp.sin(ang)
    one = jnp.ones((seq, NOPE_DIM), F32)
    tail = jnp.ones((seq, LANES - NOPE_DIM - ROPE_DIM), F32)
    ctab = jnp.concatenate([one, cos, cos, tail], axis=1)
    stab = jnp.concatenate([0 * one, -sin, sin, 0 * tail], axis=1)
    return ctab, stab


def _rope_swap(t):
    lane = lax.broadcasted_iota(jnp.int32, t.shape, 1)
    half = ROPE_DIM // 2
    return jnp.where(lane < NOPE_DIM + half, pltpu.roll(t, LANES - half, 1), pltpu.roll(t, half, 1))


def _rope(t, ctab, stab):
    return t * ctab + _rope_swap(t) * stab


def _rope_inv(t, ctab, stab):
    return t * ctab - _rope_swap(t) * stab


def _rms(x, g):
    r = lax.rsqrt(jnp.mean(x * x, axis=-1, keepdims=True) + RMS_EPS)
    xh = x * r
    return xh, r, xh * g


def _mla_prep(h, g_cq, g_ckv, wq, wk, wv, ctab, stab, tm=512):
    seq = h.shape[0]
    tm = min(tm, seq)

    def body(h_ref, gq_ref, gk_ref, wq_ref, wk_ref, wv_ref, c_ref, s_ref, q_out, k_out, v_out):
        hb = h_ref[...]
        ctab_, stab_ = c_ref[...], s_ref[...]
        _, _, cqn = _rms(hb[:, :Q_RANK], gq_ref[...])
        _, _, ckn = _rms(hb[:, Q_RANK:Q_RANK + KV_RANK], gk_ref[...])
        cqn = cqn.astype(MXU_DTYPE)
        ckn = ckn.astype(MXU_DTYPE)
        krr = _rope(hb[:, Q_RANK + KV_RANK:], ctab_, stab_)
        for hd in range(HEADS):
            q = _dot(cqn, wq_ref[hd], 1, 0)
            q_out[hd] = _rope(q, ctab_, stab_).astype(q_out.dtype)
            k_out[hd] = (_dot(ckn, wk_ref[hd], 1, 0) + krr).astype(k_out.dtype)
            v_out[hd] = _dot(ckn, wv_ref[hd], 1, 0).astype(v_out.dtype)

    full = lambda *shape: pl.BlockSpec(shape, lambda i: (0,) * len(shape))
    slab = pl.BlockSpec((HEADS, tm, LANES), lambda i: (0, i, 0))
    shp = jax.ShapeDtypeStruct((HEADS, seq, LANES), MXU_DTYPE)
    return pl.pallas_call(
        body, name="mla_prep", grid=(seq // tm,),
        in_specs=[pl.BlockSpec((tm, 512), lambda i: (i, 0)), full(1, Q_RANK), full(1, KV_RANK),
                  full(HEADS, Q_RANK, LANES), full(HEADS, KV_RANK, LANES), full(HEADS, KV_RANK, LANES),
                  pl.BlockSpec((tm, LANES), lambda i: (i, 0)), pl.BlockSpec((tm, LANES), lambda i: (i, 0))],
        out_specs=[slab, slab, slab], out_shape=[shp, shp, shp],
        compiler_params=_params("parallel"),
    )(h, g_cq, g_ckv, wq, wk, wv, ctab, stab)


def _mla_prep_bwd(h, g_cq, g_ckv, wq, wk, wv, ctab, stab, dq, dk, dv, tm=512, after=()):
    seq = h.shape[0]
    tm = min(tm, seq)
    n_after = len(after)

    def body(h_ref, gq_ref, gk_ref, wq_ref, wk_ref, wv_ref, c_ref, s_ref, dq_ref, dk_ref, dv_ref, *rest):
        dh_ref, dwq_ref, dwk_ref, dwv_ref, dgq_ref, dgk_ref = rest[n_after:]

        @pl.when(pl.program_id(0) == 0)
        def _():
            for r in (dwq_ref, dwk_ref, dwv_ref, dgq_ref, dgk_ref):
                r[...] = jnp.zeros_like(r)

        hb = h_ref[...]
        ctab_, stab_ = c_ref[...], s_ref[...]
        gq, gk = gq_ref[...], gk_ref[...]
        xq, rq, cqn = _rms(hb[:, :Q_RANK], gq)
        xk, rk, ckn = _rms(hb[:, Q_RANK:Q_RANK + KV_RANK], gk)
        cqn = cqn.astype(MXU_DTYPE)
        ckn = ckn.astype(MXU_DTYPE)
        d_cqn = jnp.zeros((tm, Q_RANK), F32)
        d_ckn = jnp.zeros((tm, KV_RANK), F32)
        d_krr = jnp.zeros((tm, LANES), F32)
        for hd in range(HEADS):
            dqh = _rope_inv(dq_ref[hd], ctab_, stab_).astype(MXU_DTYPE)
            d_cqn += _dot(dqh, wq_ref[hd], 1, 1)
            dwq_ref[hd] += _dot(cqn, dqh, 0, 0)
            dkh = dk_ref[hd]
            d_krr += dkh
            dkh = dkh.astype(MXU_DTYPE)
            d_ckn += _dot(dkh, wk_ref[hd], 1, 1)
            dwk_ref[hd] += _dot(ckn, dkh, 0, 0)
            dvh = dv_ref[hd].astype(MXU_DTYPE)
            d_ckn += _dot(dvh, wv_ref[hd], 1, 1)
            dwv_ref[hd] += _dot(ckn, dvh, 0, 0)
        lane = lax.broadcasted_iota(jnp.int32, (tm, LANES), 1)
        rot = (lane >= NOPE_DIM) & (lane < NOPE_DIM + ROPE_DIM)
        d_kr = jnp.where(rot, _rope_inv(jnp.where(rot, d_krr, 0.0), ctab_, stab_), 0.0)

        def rms_bwd(dy, xh, r, g, dg_ref):
            dg_ref[...] += jnp.sum(dy * xh, axis=0, keepdims=True)
            dxh = dy * g
            return r * (dxh - xh * jnp.mean(dxh * xh, axis=-1, keepdims=True))

        d_cq = rms_bwd(d_cqn, xq, rq, gq, dgq_ref)
        d_ck = rms_bwd(d_ckn, xk, rk, gk, dgk_ref)
        dh_ref[...] = jnp.concatenate([d_cq, d_ck, d_kr], axis=1).astype(dh_ref.dtype)

    full = lambda *shape: pl.BlockSpec(shape, lambda i: (0,) * len(shape))
    slab = pl.BlockSpec((HEADS, tm, LANES), lambda i: (0, i, 0))
    return pl.pallas_call(
        body, name="mla_prep_bwd", grid=(seq // tm,),
        in_specs=[pl.BlockSpec((tm, 512), lambda i: (i, 0)), full(1, Q_RANK), full(1, KV_RANK),
                  full(HEADS, Q_RANK, LANES), full(HEADS, KV_RANK, LANES), full(HEADS, KV_RANK, LANES),
                  pl.BlockSpec((tm, LANES), lambda i: (i, 0)), pl.BlockSpec((tm, LANES), lambda i: (i, 0)),
                  slab, slab, slab] + [_ANY_SPEC] * n_after,
        out_specs=[pl.BlockSpec((tm, 512), lambda i: (i, 0)), full(HEADS, Q_RANK, LANES), full(HEADS, KV_RANK, LANES),
                   full(HEADS, KV_RANK, LANES), full(1, Q_RANK), full(1, KV_RANK)],
        out_shape=[jax.ShapeDtypeStruct((seq, 512), MXU_DTYPE), jax.ShapeDtypeStruct((HEADS, Q_RANK, LANES), F32),
                   jax.ShapeDtypeStruct((HEADS, KV_RANK, LANES), F32), jax.ShapeDtypeStruct((HEADS, KV_RANK, LANES), F32),
                   jax.ShapeDtypeStruct((1, Q_RANK), F32), jax.ShapeDtypeStruct((1, KV_RANK), F32)],
        compiler_params=_params("arbitrary"),
    )(h, g_cq, g_ckv, wq, wk, wv, ctab, stab, dq, dk, dv, *after)


def _causal_mask(t):
    row = lax.broadcasted_iota(jnp.int32, (t, t), 0)
    col = lax.broadcasted_iota(jnp.int32, (t, t), 1)
    return row >= col


def _mla_attn_fwd(q, k, v, t=512):
    _, seq, _ = q.shape
    t = min(t, seq)

    def body(q_ref, k_ref, v_ref, o_ref, ob_ref, lse_ref, m_ref, l_ref, acc_ref, s_ref):
        i = pl.program_id(1)
        qb = q_ref[...]
        m_ref[...] = jnp.full_like(m_ref, NEG_BIG)
        l_ref[...] = jnp.zeros_like(l_ref)
        acc_ref[...] = jnp.zeros_like(acc_ref)

        def scores(j):
            return _dot(qb, k_ref[pl.ds(pl.multiple_of(j * t, t), t), :], 1, 1) * MLA_SCALE_LOG2

        def softmax_pv(j, s, masked):
            vb = v_ref[pl.ds(pl.multiple_of(j * t, t), t), :]
            if masked:
                s = jnp.where(_causal_mask(t), s, NEG_BIG)
            m_old = m_ref[...]
            m_new = jnp.maximum(m_old, jnp.max(s, axis=1, keepdims=True))
            p = jnp.exp2(s - m_new)
            a = jnp.exp2(m_old - m_new)
            l_ref[...] = a * l_ref[...] + jnp.sum(p, axis=1, keepdims=True)
            acc_ref[...] = a * acc_ref[...] + _dot(p.astype(MXU_DTYPE), vb, 1, 0)
            m_ref[...] = m_new

        s_ref[...] = scores(0)

        def loop_body(j, c):
            s_next = scores(j + 1)
            softmax_pv(j, s_ref[...], False)
            s_ref[...] = s_next
            return c

        lax.fori_loop(0, i, loop_body, 0)
        softmax_pv(i, s_ref[...], True)
        l = l_ref[...]
        o = acc_ref[...] * (1.0 / l)
        o_ref[...] = o
        ob_ref[...] = o.astype(ob_ref.dtype)
        lse_ref[...] = jnp.broadcast_to(m_ref[...] + jnp.log2(l), lse_ref.shape)

    blk = pl.BlockSpec((None, t, LANES), lambda h, i: (h, i, 0))
    whole = pl.BlockSpec((None, seq, LANES), lambda h, i: (h, 0, 0))
    shp = jax.ShapeDtypeStruct((HEADS, seq, LANES), F32)
    return pl.pallas_call(
        body, name="mla_attn_fwd", grid=(HEADS, seq // t),
        in_specs=[blk, whole, whole], out_specs=[blk, blk, blk],
        out_shape=[shp, jax.ShapeDtypeStruct((HEADS, seq, LANES), MXU_DTYPE), shp],
        scratch_shapes=[pltpu.VMEM((t, 1), F32), pltpu.VMEM((t, 1), F32), pltpu.VMEM((t, LANES), F32), pltpu.VMEM((t, t), F32)],
        compiler_params=_params("parallel", "arbitrary"),
    )(q, k, v)


def _mla_attn_bwd(q, k, v, o, lse, do, t=512):
    _, seq, _ = q.shape
    t = min(t, seq)
    nb = seq // t

    def body(q_ref, k_ref, v_ref, o_ref, lse_ref, do_ref, dq_ref, dk_ref, dv_ref, dl_ref, dka_ref, dva_ref):
        dq_ref[...] = jnp.zeros_like(dq_ref)

        def delta_body(i, c):
            rows = pl.ds(pl.multiple_of(i * t, t), t)
            dl_ref[rows, :] = jnp.sum(do_ref[rows, :] * o_ref[rows, :], axis=1, keepdims=True)
            return c

        lax.fori_loop(0, nb, delta_body, 0)

        def kblock(j, c):
            krows = pl.ds(pl.multiple_of(j * t, t), t)
            kb = k_ref[krows, :]
            vb = v_ref[krows, :]
            dka_ref[...] = jnp.zeros_like(dka_ref)
            dva_ref[...] = jnp.zeros_like(dva_ref)

            def qstep(i, masked):
                th = t // 2
                rows = [pl.ds(pl.multiple_of(i * t + hf * th, th), th) for hf in range(2)]
                qs = [q_ref[r, :] for r in rows]
                dos = [do_ref[r, :].astype(MXU_DTYPE) for r in rows]
                ss = [_dot(qb, kb, 1, 1) * MLA_SCALE_LOG2 for qb in qs]
                dps = [_dot(dob, vb, 1, 1) for dob in dos]
                for hf in range(2):
                    s = ss[hf]
                    if masked:
                        row = lax.broadcasted_iota(jnp.int32, (th, t), 0) + hf * th
                        s = jnp.where(row >= lax.broadcasted_iota(jnp.int32, (th, t), 1), s, NEG_BIG)
                    p = jnp.exp2(s - lse_ref[rows[hf], 0:1])
                    dva_ref[...] += _dot(p.astype(MXU_DTYPE), dos[hf], 0, 0)
                    ds = (p * (dps[hf] - dl_ref[rows[hf], :]) * MLA_SCALE).astype(MXU_DTYPE)
                    dka_ref[...] += _dot(ds, qs[hf], 0, 0)
                    dq_ref[rows[hf], :] += _dot(ds, kb, 1, 0)

            qstep(j, True)

            def qloop(i, c2):
                qstep(i, False)
                return c2

            lax.fori_loop(j + 1, nb, qloop, 0)
            dk_ref[krows, :] = dka_ref[...]
            dv_ref[krows, :] = dva_ref[...]
            return c

        lax.fori_loop(0, nb, kblock, 0)

    whole = pl.BlockSpec((None, seq, LANES), lambda h: (h, 0, 0))
    shp = jax.ShapeDtypeStruct((HEADS, seq, LANES), F32)
    return pl.pallas_call(
        body, name="mla_attn_bwd", grid=(HEADS,),
        in_specs=[whole] * 6, out_specs=[whole] * 3, out_shape=[shp] * 3,
        scratch_shapes=[pltpu.VMEM((seq, 1), F32), pltpu.VMEM((t, LANES), F32), pltpu.VMEM((t, LANES), F32)],
        compiler_params=_params("parallel"),
    )(q, k, v, o, lse, do)


DIL_CHUNK = DIL_BLOCK * max(d for _, d in DIL_PAIRS)
DIL_PAIR_LANES = 2 * HEAD_DIM
assert DIL_PAIR_LANES == LANES
DIL_UNROLL_FWD = 4
DIL_UNROLL_BWD = 4


def _dil_bias_tables(hp, dil):
    b = DIL_BLOCK
    iq = lax.broadcasted_iota(jnp.int32, (b, 2 * b), 0)
    ik = lax.broadcasted_iota(jnp.int32, (b, 2 * b), 1)
    off = iq + b - ik
    band = (off >= 0) & (off <= b)
    dist = (off * dil).astype(F32)
    every, first = [], []
    for hh in range(2):
        slope = jnp.where(hp == 0, ALIBI_SLOPES[hh], jnp.where(hp == 1, ALIBI_SLOPES[2 + hh],
                          jnp.where(hp == 2, ALIBI_SLOPES[4 + hh], ALIBI_SLOPES[6 + hh]))).astype(F32)
        bias = -slope * dist
        every.append(jnp.where(band, bias, NEG_BIG))
        first.append(jnp.where(band & (ik >= b), bias, NEG_BIG))
    return jnp.concatenate(every, axis=0), jnp.concatenate(first, axis=0)


def _dil_rows(start, dil):
    return pl.ds(start, DIL_BLOCK) if dil == 1 else pl.ds(start, DIL_BLOCK, stride=dil)


def _dil_block_pos(blk, c, dil):
    sc, r = blk // dil, blk % dil
    q0 = sc * (DIL_BLOCK * dil) + r
    kcur0 = c * DIL_CHUNK + q0
    first = kcur0 < DIL_BLOCK * dil
    kprev0 = jnp.where(first, kcur0, kcur0 - DIL_BLOCK * dil)
    return q0, kcur0, kprev0, first


def _pair_cols(hh):
    return slice(HEAD_DIM * hh, HEAD_DIM * (hh + 1))


def _first_head_lanes(shape):
    return lax.broadcasted_iota(jnp.int32, shape, 1) < HEAD_DIM


def _stack_pair(t):
    first = _first_head_lanes(t.shape)
    return jnp.concatenate([jnp.where(first, t, 0.0), jnp.where(first, 0.0, t)], axis=0).astype(MXU_DTYPE)


def _unstack_pair(t):
    rows = t.shape[0] // 2
    return jnp.where(_first_head_lanes((rows, t.shape[1])), t[:rows], t[rows:])


def _pair_column(t):
    return jnp.concatenate([t[:, 0:1], t[:, HEAD_DIM:HEAD_DIM + 1]], axis=0)


def _dil_fwd(h):
    seq = h.shape[0]
    assert seq % DIL_CHUNK == 0
    nblk = DIL_CHUNK // DIL_BLOCK
    rc = 256

    def body(q_ref, k_ref, v_ref, o_ref, ob_ref, lse_ref, *scr):
        o_scr, l_scr = scr[:3], scr[3:]
        hp, c = pl.program_id(0), pl.program_id(1)
        for bi, (_, dil) in enumerate(DIL_PAIRS):
            tables = _dil_bias_tables(hp, dil)

            def block(blk, carry, bi=bi, dil=dil, tables=tables):
                q0, kcur0, kprev0, first = _dil_block_pos(blk, c, dil)
                q2 = _stack_pair(q_ref[_dil_rows(q0, dil), :] * DIL_SCALE)
                kcat = jnp.concatenate([k_ref[_dil_rows(kprev0, dil), :], k_ref[_dil_rows(kcur0, dil), :]], axis=0).astype(MXU_DTYPE)
                vcat = jnp.concatenate([v_ref[_dil_rows(kprev0, dil), :], v_ref[_dil_rows(kcur0, dil), :]], axis=0).astype(MXU_DTYPE)
                s = _dot(q2, kcat, 1, 1) + jnp.where(first, tables[1], tables[0])
                mx = jnp.max(s, axis=1, keepdims=True)
                p = jnp.exp(s - mx)
                l = jnp.sum(p, axis=1, keepdims=True)
                o_scr[bi][_dil_rows(q0, dil), :] = _unstack_pair(_dot(p.astype(MXU_DTYPE), vcat, 1, 0) * (1.0 / l))
                l_scr[bi][_dil_rows(q0, dil), :] = _unstack_pair(jnp.broadcast_to(mx + jnp.log(l), (2 * DIL_BLOCK, LANES)))
                return carry

            lax.fori_loop(0, nblk, block, 0, unroll=DIL_UNROLL_FWD)

        def combine(i, carry):
            rows = pl.ds(pl.multiple_of(i * rc, rc), rc)
            ls = [l_scr[bi][rows, :] for bi in range(3)]
            mx = jnp.maximum(jnp.maximum(ls[0], ls[1]), ls[2])
            es = [jnp.exp(l - mx) for l in ls]
            den = es[0] + es[1] + es[2]
            o = (es[0] * o_scr[0][rows, :] + es[1] * o_scr[1][rows, :] + es[2] * o_scr[2][rows, :]) / den
            o_ref[rows, :] = o
            ob_ref[rows, :] = o.astype(ob_ref.dtype)
            lse_ref[rows, :] = mx + jnp.log(den)
            return carry

        lax.fori_loop(0, DIL_CHUNK // rc, combine, 0)

    nq = DIL_WIDTH // LANES
    chunk = lambda off: pl.BlockSpec((DIL_CHUNK, LANES), lambda hp, c: (c, off + hp))
    whole = lambda off: pl.BlockSpec((seq, LANES), lambda hp, c: (0, off + hp))
    shp = jax.ShapeDtypeStruct((seq, DIL_WIDTH), F32)
    return pl.pallas_call(
        body, name="dil_fwd", grid=(nq, seq // DIL_CHUNK),
        in_specs=[chunk(nq), whole(2 * nq), whole(3 * nq)], out_specs=[chunk(0), chunk(0), chunk(0)],
        out_shape=[shp, jax.ShapeDtypeStruct((seq, DIL_WIDTH), MXU_DTYPE), shp],
        scratch_shapes=[pltpu.VMEM((DIL_CHUNK, LANES), F32)] * 6,
        compiler_params=_params("parallel", "arbitrary"),
    )(h, h, h)


def _dil_bwd(h, o, lse, do):
    seq = h.shape[0]
    nblk = DIL_CHUNK // DIL_BLOCK
    rc = 256

    def body(q_ref, k_ref, v_ref, o_ref, lse_ref, do_ref, dq_ref, dk_ref, dv_ref, dl_scr):
        hp, c = pl.program_id(0), pl.program_id(1)

        @pl.when(c == 0)
        def _():
            dk_ref[...] = jnp.zeros_like(dk_ref)
            dv_ref[...] = jnp.zeros_like(dv_ref)

        def delta(i, carry):
            rows = pl.ds(pl.multiple_of(i * rc, rc), rc)
            prod = do_ref[rows, :] * o_ref[rows, :]
            dl_scr[rows, :] = jnp.concatenate(
                [jnp.broadcast_to(jnp.sum(prod[:, _pair_cols(hh)], axis=1, keepdims=True), (rc, HEAD_DIM)) for hh in range(2)], axis=1)
            return carry

        lax.fori_loop(0, DIL_CHUNK // rc, delta, 0)

        for bi, (_, dil) in enumerate(DIL_PAIRS):
            tables = _dil_bias_tables(hp, dil)

            def block(blk, carry, bi=bi, dil=dil, tables=tables):
                q0, kcur0, kprev0, first = _dil_block_pos(blk, c, dil)
                qrows = _dil_rows(q0, dil)
                q2 = _stack_pair(q_ref[qrows, :] * DIL_SCALE)
                kcat = jnp.concatenate([k_ref[_dil_rows(kprev0, dil), :], k_ref[_dil_rows(kcur0, dil), :]], axis=0).astype(MXU_DTYPE)
                vcat = jnp.concatenate([v_ref[_dil_rows(kprev0, dil), :], v_ref[_dil_rows(kcur0, dil), :]], axis=0).astype(MXU_DTYPE)
                do2 = _stack_pair(do_ref[qrows, :])
                s = _dot(q2, kcat, 1, 1) + jnp.where(first, tables[1], tables[0])
                p = jnp.exp(s - _pair_column(lse_ref[qrows, :]))
                dp = _dot(do2, vcat, 1, 1)
                ds = (p * (dp - _pair_column(dl_scr[qrows, :]))).astype(MXU_DTYPE)
                dq_b = _unstack_pair(_dot(ds, kcat, 1, 0)) * DIL_SCALE
                dk_b = _dot(ds, q2, 0, 0)
                dv_b = _dot(p.astype(MXU_DTYPE), do2, 0, 0)
                if bi == 0:
                    dq_ref[qrows, :] = dq_b
                else:
                    dq_ref[qrows, :] += dq_b
                dk_ref[_dil_rows(kprev0, dil), :] += dk_b[:DIL_BLOCK]
                dv_ref[_dil_rows(kprev0, dil), :] += dv_b[:DIL_BLOCK]
                dk_ref[_dil_rows(kcur0, dil), :] += dk_b[DIL_BLOCK:]
                dv_ref[_dil_rows(kcur0, dil), :] += dv_b[DIL_BLOCK:]
                return carry

            lax.fori_loop(0, nblk, block, 0, unroll=DIL_UNROLL_BWD)

    nq = DIL_WIDTH // LANES
    chunk = lambda off: pl.BlockSpec((DIL_CHUNK, LANES), lambda hp, c: (c, off + hp))
    whole = lambda off: pl.BlockSpec((seq, LANES), lambda hp, c: (0, off + hp))
    shp = jax.ShapeDtypeStruct((seq, DIL_WIDTH), F32)
    return pl.pallas_call(
        body, name="dil_bwd", grid=(nq, seq // DIL_CHUNK),
        in_specs=[chunk(nq), whole(2 * nq), whole(3 * nq), chunk(0), chunk(0), chunk(0)],
        out_specs=[chunk(0), whole(0), whole(0)], out_shape=[shp, shp, shp],
        scratch_shapes=[pltpu.VMEM((DIL_CHUNK, LANES), F32)],
        compiler_params=_params("parallel", "arbitrary"),
    )(h, h, h, o, lse, do)


def _assemble_dh(dh_mla, dq, dk, dv, tm=512):
    seq = dh_mla.shape[0]
    tm = min(tm, seq)

    def body(a_ref, q_ref, k_ref, v_ref, o_ref):
        for j, r in enumerate((a_ref, q_ref, k_ref, v_ref)):
            o_ref[:, 512 * j:512 * (j + 1)] = r[...].astype(o_ref.dtype)

    blk = pl.BlockSpec((tm, 512), lambda i: (i, 0))
    return pl.pallas_call(
        body, name="assemble_dh", grid=(seq // tm,), in_specs=[blk] * 4,
        out_specs=pl.BlockSpec((tm, IN_PAD), lambda i: (i, 0)), out_shape=jax.ShapeDtypeStruct((seq, IN_PAD), MXU_DTYPE),
        compiler_params=_params("parallel"),
    )(dh_mla, dq, dk, dv)


def _ln_stats(z):
    mu = jnp.mean(z, axis=-1, keepdims=True)
    zc = z - mu
    r = lax.rsqrt(jnp.mean(zc * zc, axis=-1, keepdims=True) + LN_EPS)
    return zc * r, r


def _ln_bwd_math(dy, xh, r, g):
    dxh = dy * g
    return r * (dxh - jnp.mean(dxh, axis=-1, keepdims=True) - xh * jnp.mean(dxh * xh, axis=-1, keepdims=True))


def _mix_ln1(o_mla, o_dil, w_o_mla, w_o_dil, x0, g, b, tm=512):
    seq, d = x0.shape
    tm = min(tm, seq)

    def body(om_ref, od_ref, wm_ref, wd_ref, x_ref, g_ref, b_ref, z_ref, y_ref, yb_ref):
        mix = _dot(od_ref[...], wd_ref[...], 1, 0)
        for hd in range(HEADS):
            mix += _dot(om_ref[hd], wm_ref[LANES * hd:LANES * (hd + 1), :], 1, 0)
        z = DN_ALPHA * x_ref[...] + mix
        xh, _ = _ln_stats(z)
        y = xh * g_ref[...] + b_ref[...]
        z_ref[...] = z
        y_ref[...] = y
        yb_ref[...] = y.astype(yb_ref.dtype)

    blk = pl.BlockSpec((tm, d), lambda i: (i, 0))
    vec = pl.BlockSpec((1, d), lambda i: (0, 0))
    shp = jax.ShapeDtypeStruct((seq, d), F32)
    return pl.pallas_call(
        body, name="mix_ln1", grid=(seq // tm,),
        in_specs=[pl.BlockSpec((HEADS, tm, LANES), lambda i: (0, i, 0)), pl.BlockSpec((tm, DIL_WIDTH), lambda i: (i, 0)),
                  pl.BlockSpec((HEADS * LANES, d), lambda i: (0, 0)), pl.BlockSpec((DIL_WIDTH, d), lambda i: (0, 0)), blk, vec, vec],
        out_specs=[blk, blk, blk], out_shape=[shp, shp, jax.ShapeDtypeStruct((seq, d), MXU_DTYPE)],
        compiler_params=_params("parallel"))(o_mla, o_dil, w_o_mla, w_o_dil, x0, g, b)


def _ln_bwd(dy, z, g, name, tm=512, after=()):
    seq, d = z.shape
    tm = min(tm, seq)
    n_after = len(after)

    def body(dy_ref, z_ref, g_ref, *rest):
        dz_ref, dzb_ref, dg_ref, db_ref = rest[n_after:]

        @pl.when(pl.program_id(0) == 0)
        def _():
            dg_ref[...] = jnp.zeros_like(dg_ref)
            db_ref[...] = jnp.zeros_like(db_ref)

        dyb = dy_ref[...]
        xh, r = _ln_stats(z_ref[...])
        dg_ref[...] += jnp.sum(dyb * xh, axis=0, keepdims=True)
        db_ref[...] += jnp.sum(dyb, axis=0, keepdims=True)
        dz = _ln_bwd_math(dyb, xh, r, g_ref[...])
        dz_ref[...] = dz
        dzb_ref[...] = dz.astype(dzb_ref.dtype)

    blk = pl.BlockSpec((tm, d), lambda i: (i, 0))
    vec = pl.BlockSpec((1, d), lambda i: (0, 0))
    return pl.pallas_call(
        body, name=name, grid=(seq // tm,), in_specs=[blk, blk, vec] + [_ANY_SPEC] * n_after, out_specs=[blk, blk, vec, vec],
        out_shape=[jax.ShapeDtypeStruct((seq, d), F32), jax.ShapeDtypeStruct((seq, d), MXU_DTYPE),
                   jax.ShapeDtypeStruct((1, d), F32), jax.ShapeDtypeStruct((1, d), F32)],
        compiler_params=_params("arbitrary"))(dy, z, g, *after)


def _ln2_loss_bwd(x1, ffn, target, g, b, tm=512):
    seq, d = x1.shape
    tm = min(tm, seq)

    def body(x_ref, f_ref, t_ref, g_ref, b_ref, dz_ref, dzb_ref, loss_ref, dg_ref, db_ref):
        @pl.when(pl.program_id(0) == 0)
        def _():
            loss_ref[...] = jnp.zeros_like(loss_ref)
            dg_ref[...] = jnp.zeros_like(dg_ref)
            db_ref[...] = jnp.zeros_like(db_ref)

        gv = g_ref[...]
        z = DN_ALPHA * x_ref[...] + f_ref[...]
        xh, r = _ln_stats(z)
        err = (xh * gv + b_ref[...]) - t_ref[...]
        loss_ref[...] += 0.5 * jnp.sum(jnp.mean(err * err, axis=-1, keepdims=True), axis=0, keepdims=True)
        dy = err * (1.0 / d)
        dg_ref[...] += jnp.sum(dy * xh, axis=0, keepdims=True)
        db_ref[...] += jnp.sum(dy, axis=0, keepdims=True)
        dz = _ln_bwd_math(dy, xh, r, gv)
        dz_ref[...] = dz
        dzb_ref[...] = dz.astype(dzb_ref.dtype)

    blk = pl.BlockSpec((tm, d), lambda i: (i, 0))
    vec = pl.BlockSpec((1, d), lambda i: (0, 0))
    return pl.pallas_call(
        body, name="ln2_loss_bwd", grid=(seq // tm,), in_specs=[blk, blk, blk, vec, vec],
        out_specs=[blk, blk, pl.BlockSpec((1, LANES), lambda i: (0, 0)), vec, vec],
        out_shape=[jax.ShapeDtypeStruct((seq, d), F32), jax.ShapeDtypeStruct((seq, d), MXU_DTYPE),
                   jax.ShapeDtypeStruct((1, LANES), F32),
                   jax.ShapeDtypeStruct((1, d), F32), jax.ShapeDtypeStruct((1, d), F32)],
        compiler_params=_params("arbitrary"))(x1, ffn, target, g, b)


HALO = 16


def _conv_rows(e, w_ref, b_ref):
    y = b_ref[...] + w_ref[0:1, :] * pltpu.roll(e, 2, 0)
    y = y + w_ref[1:2, :] * pltpu.roll(e, 1, 0)
    return y + w_ref[2:3, :] * e


_GELU_C = math.sqrt(2.0 / math.pi)
_GELU_A = 0.044715


def _gelu(x):
    return 0.5 * x * (1.0 + jnp.tanh(_GELU_C * (x + _GELU_A * (x * x * x))))


CONV_TN = 256


def _ffn_interleave(a, axis):
    shp = a.shape
    a = a.reshape(shp[:axis] + (2, D_FF // CONV_TN, CONV_TN) + shp[axis + 1:])
    return jnp.swapaxes(a, axis, axis + 1).reshape(shp)


def _ffn_deinterleave(a, axis):
    shp = a.shape
    a = a.reshape(shp[:axis] + (D_FF // CONV_TN, 2, CONV_TN) + shp[axis + 1:])
    return jnp.swapaxes(a, axis, axis + 1).reshape(shp)


def _conv_gate_fwd(u, conv_w, conv_b, tm=1024):
    seq = u.shape[0]
    tm = min(tm, seq)
    tn = CONV_TN

    def body(u_ref, up_ref, w_ref, b_ref, o_ref):
        first = pl.program_id(0) == 0
        e = jnp.concatenate([jnp.where(first, 0.0, up_ref[...]), u_ref[...]], axis=0)
        y = _conv_rows(e, w_ref, b_ref)[HALO:]
        o_ref[...] = (_gelu(y[:, tn:]) * y[:, :tn]).astype(o_ref.dtype)

    hb = tm // HALO
    return pl.pallas_call(
        body, name="conv_gate_fwd", grid=(seq // tm, D_FF // tn),
        in_specs=[pl.BlockSpec((tm, 2 * tn), lambda i, j: (i, j)),
                  pl.BlockSpec((HALO, 2 * tn), lambda i, j: (jnp.maximum(i * hb - 1, 0), j)),
                  pl.BlockSpec((3, 2 * tn), lambda i, j: (0, j)), pl.BlockSpec((1, 2 * tn), lambda i, j: (0, j))],
        out_specs=pl.BlockSpec((tm, tn), lambda i, j: (i, j)), out_shape=jax.ShapeDtypeStruct((seq, D_FF), MXU_DTYPE),
        compiler_params=_params("parallel", "parallel"),
    )(u, u, conv_w, conv_b)


def _conv_gate_bwd(u, d_act, conv_w, conv_b, tm=512):
    seq = u.shape[0]
    tm = min(tm, seq)
    tn = CONV_TN
    ni = seq // tm
    rows_e = tm + 2 * HALO

    def body(u_ref, up_ref, un_ref, da_ref, dan_ref, w_ref, b_ref, du_ref, dw_ref, db_ref):
        i = pl.program_id(1)
        first, last = i == 0, i == ni - 1

        @pl.when(i == 0)
        def _():
            dw_ref[...] = jnp.zeros_like(dw_ref)
            db_ref[...] = jnp.zeros_like(db_ref)

        e = jnp.concatenate([jnp.where(first, 0.0, up_ref[...]), u_ref[...], jnp.where(last, 0.0, un_ref[...])], axis=0)
        y = _conv_rows(e, w_ref, b_ref)
        ya, yg = y[:, :tn], y[:, tn:]
        dact = jnp.concatenate([jnp.zeros((HALO, tn), F32), da_ref[...].astype(F32),
                                jnp.where(last, 0.0, dan_ref[...].astype(F32))], axis=0)
        th = jnp.tanh(_GELU_C * (yg + _GELU_A * (yg * yg * yg)))
        gelu = 0.5 * yg * (1.0 + th)
        gelu_grad = 0.5 * (1.0 + th) + 0.5 * yg * (1.0 - th * th) * (_GELU_C * (1.0 + 3.0 * _GELU_A * (yg * yg)))
        dy = jnp.concatenate([dact * gelu, dact * ya * gelu_grad], axis=1)
        du = w_ref[2:3, :] * dy + w_ref[1:2, :] * pltpu.roll(dy, rows_e - 1, 0) + w_ref[0:1, :] * pltpu.roll(dy, rows_e - 2, 0)
        du_ref[...] = du[HALO:HALO + tm].astype(du_ref.dtype)
        dyt = dy[HALO:HALO + tm]
        dw_ref[0:1, :] += jnp.sum(dyt * pltpu.roll(e, 2, 0)[HALO:HALO + tm], axis=0, keepdims=True)
        dw_ref[1:2, :] += jnp.sum(dyt * pltpu.roll(e, 1, 0)[HALO:HALO + tm], axis=0, keepdims=True)
        dw_ref[2:3, :] += jnp.sum(dyt * e[HALO:HALO + tm], axis=0, keepdims=True)
        db_ref[...] += jnp.sum(dyt, axis=0, keepdims=True)

    hb = tm // HALO
    nh = seq // HALO
    prev = lambda j, i: (jnp.maximum(i * hb - 1, 0), j)
    nxt = lambda j, i: (jnp.minimum((i + 1) * hb, nh - 1), j)
    return pl.pallas_call(
        body, name="conv_gate_bwd", grid=(D_FF // tn, ni),
        in_specs=[pl.BlockSpec((tm, 2 * tn), lambda j, i: (i, j)), pl.BlockSpec((HALO, 2 * tn), prev),
                  pl.BlockSpec((HALO, 2 * tn), nxt), pl.BlockSpec((tm, tn), lambda j, i: (i, j)), pl.BlockSpec((HALO, tn), nxt),
                  pl.BlockSpec((3, 2 * tn), lambda j, i: (0, j)), pl.BlockSpec((1, 2 * tn), lambda j, i: (0, j))],
        out_specs=[pl.BlockSpec((tm, 2 * tn), lambda j, i: (i, j)), pl.BlockSpec((3, 2 * tn), lambda j, i: (0, j)),
                   pl.BlockSpec((1, 2 * tn), lambda j, i: (0, j))],
        out_shape=[jax.ShapeDtypeStruct((seq, 2 * D_FF), MXU_DTYPE), jax.ShapeDtypeStruct((3, 2 * D_FF), F32),
                   jax.ShapeDtypeStruct((1, 2 * D_FF), F32)],
        compiler_params=_params("parallel", "arbitrary"),
    )(u, u, u, d_act, d_act, conv_w, conv_b)


def _pad_heads(w, width):
    w = jnp.transpose(w, (1, 0, 2))
    return jnp.pad(w, ((0, 0), (0, 0), (0, LANES - width))).astype(MXU_DTYPE)


def _heads_major(a):
    return jnp.transpose(a, (1, 0, 2)).reshape(-1, a.shape[2])


def _heads_minor(a, heads):
    return jnp.transpose(a.reshape(heads, -1, a.shape[1]), (1, 0, 2))


def _unpad_heads(d, width):
    return jnp.transpose(d[:, :, :width], (1, 0, 2))


def _split_pad_rows(w_t):
    z = lambda n: jnp.zeros((n, w_t.shape[1]), w_t.dtype)
    return jnp.concatenate([w_t[:384], z(64), w_t[384:416], z(32), w_t[416:]], axis=0)


def _split_unpad_rows(w_p):
    return jnp.concatenate([w_p[:384], w_p[448:480], w_p[512:]], axis=0)


def _pad_w_o(w_o):
    mla = jnp.pad(w_o[:512].reshape(HEADS, HEAD_DIM, D_MODEL), ((0, 0), (0, LANES - HEAD_DIM), (0, 0)))
    return mla.reshape(HEADS * LANES, D_MODEL).astype(MXU_DTYPE), w_o[512:].astype(MXU_DTYPE)


def _unpad_w_o(d_mla, d_dil):
    return jnp.concatenate([d_mla.reshape(HEADS, LANES, D_MODEL)[:, :HEAD_DIM].reshape(512, D_MODEL), d_dil], axis=0)


def _row(v):
    return v.reshape(1, -1).astype(F32)


def _compute_weights(w):
    w_o_mla, w_o_dil = _pad_w_o(w["w_o"])
    return dict(
        w_in_t=_split_pad_rows(w["w_in"].T).astype(MXU_DTYPE), wq=_pad_heads(w["w_uq"], NOPE_DIM + ROPE_DIM),
        wk=_pad_heads(w["w_uk"], NOPE_DIM), wv=_pad_heads(w["w_uv"], HEAD_DIM), w_o_mla=w_o_mla, w_o_dil=w_o_dil,
        w_up_t=_ffn_interleave(w["w_up"].T, 0).astype(MXU_DTYPE), w_down=w["w_down"].astype(MXU_DTYPE),
        conv_w=_ffn_interleave(w["conv_w"].astype(F32), 1),
        g_cq=_row(w["g_cq"]), g_ckv=_row(w["g_ckv"]), ln1_g=_row(w["ln1_g"]), ln1_b=_row(w["ln1_b"]),
        conv_b=_ffn_interleave(_row(w["conv_b"]), 1), ln2_g=_row(w["ln2_g"]), ln2_b=_row(w["ln2_b"]))


def _natural_grads(g):
    return dict(
        w_in=_split_unpad_rows(g["w_in_t"]).T, g_cq=g["g_cq"].reshape(-1), g_ckv=g["g_ckv"].reshape(-1),
        w_uq=_unpad_heads(g["wq"], NOPE_DIM + ROPE_DIM), w_uk=_unpad_heads(g["wk"], NOPE_DIM),
        w_uv=_unpad_heads(g["wv"], HEAD_DIM), w_o=_unpad_w_o(g["w_o_mla"], g["w_o_dil"]), ln1_g=g["ln1_g"].reshape(-1),
        ln1_b=g["ln1_b"].reshape(-1), w_up=_ffn_deinterleave(g["w_up_t"], 0).T, conv_w=_ffn_deinterleave(g["conv_w"], 1),
        conv_b=_ffn_deinterleave(g["conv_b"], 1).reshape(-1),
        w_down=g["w_down"], ln2_g=g["ln2_g"].reshape(-1), ln2_b=g["ln2_b"].reshape(-1))


def _layer_grads(x0, target, cw, first_after=(), late_weights=None, on_grads=None):
    seq = x0.shape[0]
    ctab, stab = _rope_tables(seq)
    gq, gk = cw["g_cq"], cw["g_ckv"]
    wq, wk, wv = cw["wq"], cw["wk"], cw["wv"]
    x0b = x0.astype(MXU_DTYPE)
    notify = (lambda stage, grads: ()) if on_grads is None else on_grads

    h = _mm(x0b, cw["w_in_t"], name="mm_h", tb=True, tm=1024, tn=1024, tk=1024, after=first_after)
    qf, kf, vp = _mla_prep(h, gq, gk, wq, wk, wv, ctab, stab)
    o_mla, o_mla_b, lse_mla = _mla_attn_fwd(qf, kf, vp)
    o_dil, o_dil_b, lse_dil = _dil_fwd(h)
    if late_weights is not None:
        cw = {**cw, **late_weights(o_mla_b)}
    cb = cw["conv_b"]
    z1, x1, x1b = _mix_ln1(o_mla_b, o_dil_b, cw["w_o_mla"], cw["w_o_dil"], x0, cw["ln1_g"], cw["ln1_b"])
    u = _mm(x1b, cw["w_up_t"], name="mm_up", tb=True, tm=1024, tn=1408, tk=1024)
    act = _conv_gate_fwd(u, cw["conv_w"], cb)
    ffn = _mm(act, cw["w_down"], name="mm_down", tm=1024, tn=1024, tk=2816)
    dz2, dz2b, loss, d_ln2_g, d_ln2_b = _ln2_loss_bwd(x1, ffn, target, cw["ln2_g"], cw["ln2_b"])

    d_act = _mm(dz2b, cw["w_down"], name="mm_d_act", tb=True, out_dtype=MXU_DTYPE, tm=1024, tn=1408, tk=1024)
    d_w_down = _mm(act, dz2b, name="mm_dw_down", ta=True, out_dtype=MXU_DTYPE, tm=1408, tn=1024, tk=1024)
    du, d_conv_w, d_conv_b = _conv_gate_bwd(u, d_act, cw["conv_w"], cb)
    dx1 = _mm(du, cw["w_up_t"], name="mm_dx1", res=dz2, res_scale=DN_ALPHA, tm=512, tn=1024, tk=2 * D_FF)
    d_w_up_t = _mm(du, x1b, name="mm_dw_up", ta=True, out_dtype=MXU_DTYPE, tm=1408, tn=1024, tk=2048)
    grads = dict(w_up_t=d_w_up_t, w_down=d_w_down, conv_w=d_conv_w, conv_b=d_conv_b, ln2_g=d_ln2_g, ln2_b=d_ln2_b)
    dz1, dz1b, d_ln1_g, d_ln1_b = _ln_bwd(dx1, z1, cw["ln1_g"], "ln1_bwd", after=notify("ffn", grads))
    do_mla = _mm_do_mla(dz1b, cw["w_o_mla"])
    do_dil = _mm(dz1b, cw["w_o_dil"], name="mm_do_dil", tb=True, tm=1024, tn=512, tk=1024)
    d_w_o_mla = _mm_dw_o_mla(o_mla_b, dz1b)
    d_w_o_dil = _mm(o_dil_b, dz1b, name="mm_dw_o_dil", ta=True, out_dtype=MXU_DTYPE, tm=512, tn=1024, tk=1024)
    grads.update(w_o_mla=d_w_o_mla, w_o_dil=d_w_o_dil, ln1_g=d_ln1_g, ln1_b=d_ln1_b)
    dq_dil, dk_dil, dv_dil = _dil_bwd(h, o_dil, lse_dil, do_dil)
    dqf, dkf, dvf = _mla_attn_bwd(qf, kf, vp, o_mla, lse_mla, do_mla)
    dh_mla, d_wq, d_wk, d_wv, d_gq, d_gk = _mla_prep_bwd(h, gq, gk, wq, wk, wv, ctab, stab, dqf, dkf, dvf,
                                                          after=notify("w_o", grads))
    dh = _assemble_dh(dh_mla, dq_dil, dk_dil, dv_dil)
    d_w_in_t = _mm(dh, x0b, name="mm_dw_in", ta=True, out_dtype=MXU_DTYPE, tm=1024, tn=1024, tk=1024)
    grads.update(w_in_t=d_w_in_t, wq=d_wq, wk=d_wk, wv=d_wv, g_cq=d_gq, g_ckv=d_gk, loss=loss)
    grad_x = _mm(dh, cw["w_in_t"], name="mm_dx0", res=dz1, res_scale=DN_ALPHA, tm=1024, tn=1024, tk=2048,
                 after=notify("rest", grads))
    return loss, grad_x, grads


def _all_gather(blocks, name):
    na = len(blocks)

    def body(*refs):
        ins, outs = refs[:na], refs[na:2 * na]
        send_sems, recv_sems, local_sems = refs[2 * na:]
        x, y, c = lax.axis_index("x"), lax.axis_index("y"), lax.axis_index("c")
        me, sibling = (x, y, c), (x, y, 1 - c)
        chips = [(1 - x, y), (x, 1 - y), (1 - x, 1 - y)]

        def slot(out, pos):
            return out.at[4 * pos[0] + 2 * pos[1] + pos[2]]

        def copy(a, k, block, to, src=None):
            return pltpu.make_async_remote_copy(
                src_ref=slot(outs[a], block) if src is None else src, dst_ref=slot(outs[a], block),
                send_sem=send_sems.at[7 * a + k], recv_sem=recv_sems.at[7 * a + k],
                device_id=to, device_id_type=pl.DeviceIdType.MESH)

        mine = [pltpu.make_async_copy(ins[a], slot(outs[a], me), local_sems.at[a]) for a in range(na)]
        for cp in mine:
            cp.start()
        first = []
        for a in range(na):
            first.append(copy(a, 0, me, sibling, src=ins[a]))
            first += [copy(a, 1 + j, me, (*chip, c), src=ins[a]) for j, chip in enumerate(chips)]
        for cp in first:
            cp.start()
        passed = []
        for j, chip in enumerate(chips):
            for a in range(na):
                copy(a, 1 + j, (*chip, c), me).wait_recv()
                cp = copy(a, 4 + j, (*chip, c), sibling)
                cp.start()
                passed.append(cp)
        for a in range(na):
            copy(a, 0, sibling, me).wait_recv()
            for j, chip in enumerate(chips):
                copy(a, 4 + j, (*chip, 1 - c), me).wait_recv()
        for cp in first + passed:
            cp.wait_send()
        for cp in mine:
            cp.wait()

    any_spec = pl.BlockSpec(memory_space=pl.ANY)
    return pl.pallas_call(
        body, name=name, in_specs=[any_spec] * na, out_specs=[any_spec] * na,
        out_shape=[jax.ShapeDtypeStruct((N_DEV,) + b.shape, b.dtype) for b in blocks],
        scratch_shapes=[pltpu.SemaphoreType.DMA((7 * na,)), pltpu.SemaphoreType.DMA((7 * na,)), pltpu.SemaphoreType.DMA((na,))],
    )(*blocks)


_HBM_SPEC = pl.BlockSpec(memory_space=pltpu.HBM)
_SEM_SPEC = pl.BlockSpec(memory_space=pltpu.SEMAPHORE)
_DATAFLOW = pltpu.CompilerParams(has_side_effects=pltpu.SideEffectType.DATAFLOW_SIDE_EFFECTING)


def _split_copies(ins, lands, send_sems, recv_sems, gather):
    x, y, c = lax.axis_index("x"), lax.axis_index("y"), lax.axis_index("c")
    me = 4 * x + 2 * y + c
    copies = []
    for a in range(len(ins)):
        for d in range(1, N_DEV):
            px, py, pc = x ^ (d >> 2), y ^ ((d >> 1) & 1), c ^ (d & 1)
            copies.append(pltpu.make_async_remote_copy(
                src_ref=ins[a] if gather else ins[a].at[4 * px + 2 * py + pc], dst_ref=lands[a].at[me],
                send_sem=send_sems.at[7 * a + d - 1], recv_sem=recv_sems.at[7 * a + d - 1],
                device_id=(px, py, pc), device_id_type=pl.DeviceIdType.MESH))
    return copies


def _send_start(srcs, gather, name):
    na = len(srcs)
    land_types = [pltpu.HBM(((N_DEV,) + s.shape) if gather else s.shape, s.dtype) for s in srcs]

    def body(*refs):
        ins, lands = refs[:na], refs[na:2 * na]
        send_sems, recv_sems, token = refs[2 * na], refs[2 * na + 1], refs[-1]
        for cp in _split_copies(ins, lands, send_sems, recv_sems, gather):
            cp.start()
        token[...] = jnp.zeros_like(token)

    hbm = lambda a: pltpu.with_memory_space_constraint(a, pltpu.HBM)
    outs = pl.pallas_call(
        body, name=name,
        out_shape=(pltpu.SemaphoreType.DMA((7 * na,)), pltpu.SemaphoreType.DMA((7 * na,)),
                   *[pltpu.HBM(s.shape, s.dtype) for s in srcs], *land_types, jax.ShapeDtypeStruct((8, LANES), F32)),
        in_specs=[_HBM_SPEC] * (2 * na),
        out_specs=(_SEM_SPEC, _SEM_SPEC, *[_HBM_SPEC] * (2 * na), pl.BlockSpec(memory_space=pltpu.VMEM)),
        input_output_aliases={i: 2 + i for i in range(2 * na)}, compiler_params=_DATAFLOW,
    )(*[hbm(s) for s in srcs], *[hbm(lax.empty(t.shape, t.dtype)) for t in land_types])
    return dict(send=outs[0], recv=outs[1], srcs=list(outs[2:2 + na]), lands=list(outs[2 + na:2 + 2 * na]), token=outs[-1],
                gather=gather)


def _send_wait(handle, after, name):
    na = len(handle["srcs"])
    gather = handle["gather"]
    after = list(after)

    def body(*refs):
        ins, lands = refs[:na], refs[na:2 * na]
        send_sems, recv_sems = refs[2 * na], refs[2 * na + 1]
        for cp in _split_copies(ins, lands, send_sems, recv_sems, gather):
            cp.wait_send()
            cp.wait_recv()

    both = handle["srcs"] + handle["lands"]
    outs = pl.pallas_call(
        body, name=name, out_shape=[pltpu.HBM(a.shape, a.dtype) for a in both],
        in_specs=[_HBM_SPEC] * (2 * na) + [_SEM_SPEC, _SEM_SPEC] + [_ANY_SPEC] * len(after),
        out_specs=[_HBM_SPEC] * (2 * na), input_output_aliases={i: i for i in range(2 * na)}, compiler_params=_DATAFLOW,
    )(*both, handle["send"], handle["recv"], *after)
    return list(outs[:na]), list(outs[na:])


def _sum_slots(p_ref):
    g = p_ref[0].astype(F32)
    for s in range(1, p_ref.shape[0]):
        g = g + p_ref[s].astype(F32)
    return g


def _adamw_refs(g, w_ref, m_ref, v_ref, g_out, d_out, m_out, v_out):
    c1 = 1.0 - ADAM_B1 ** ADAM_STEP
    c2 = 1.0 - ADAM_B2 ** ADAM_STEP
    m_new = ADAM_B1 * m_ref[...] + (1.0 - ADAM_B1) * g
    v_new = ADAM_B2 * v_ref[...] + (1.0 - ADAM_B2) * (g * g)
    g_out[...] = g
    m_out[...] = m_new
    v_out[...] = v_new
    d_out[...] = -ADAM_LR * ((m_new / c1) / (jnp.sqrt(v_new / c2) + ADAM_EPS) + ADAM_WD * w_ref[...])


def _adamw(parts, w, m, v, name):
    npart, r, n = parts.shape
    tr = r if r <= 256 else max(t for t in range(16, 257, 16) if r % t == 0)

    def body(p_ref, w_ref, m_ref, v_ref, g_out, d_out, m_out, v_out):
        _adamw_refs(_sum_slots(p_ref), w_ref, m_ref, v_ref, g_out, d_out, m_out, v_out)

    blk = pl.BlockSpec((tr, n), lambda i: (i, 0))
    shp = jax.ShapeDtypeStruct((r, n), F32)
    return pl.pallas_call(
        body, name=name, grid=(r // tr,), in_specs=[pl.BlockSpec((npart, tr, n), lambda i: (0, i, 0)), blk, blk, blk],
        out_specs=[blk] * 4, out_shape=[shp] * 4, compiler_params=_params("parallel"),
    )(parts, w, m, v)


def _adamw_small(parts, ws, ms, vs, loss_parts, name):
    n = len(parts)

    def body(*refs):
        ins, outs = refs[:4 * n + 1], refs[4 * n + 1:]
        for i in range(n):
            _adamw_refs(_sum_slots(ins[i]), ins[n + i], ins[2 * n + i], ins[3 * n + i], *outs[4 * i:4 * i + 4])
        outs[4 * n][...] = _sum_slots(ins[4 * n])

    out_shape = [jax.ShapeDtypeStruct(w.shape, F32) for w in ws for _ in range(4)]
    res = pl.pallas_call(body, name=name, out_shape=out_shape + [jax.ShapeDtypeStruct((1, LANES), F32)],
                         compiler_params=_params())(*parts, *ws, *ms, *vs, loss_parts)
    return [res[4 * i:4 * i + 4] for i in range(n)], res[4 * n]


REPLICATED = ("g_cq", "g_ckv", "w_uk", "w_uv", "ln1_g", "ln1_b", "conv_b", "ln2_g", "ln2_b")
ALL_WEIGHTS = ("w_in", "g_cq", "g_ckv", "w_uq", "w_uk", "w_uv", "w_o", "ln1_g", "ln1_b", "w_up", "conv_w", "conv_b",
               "w_down", "ln2_g", "ln2_b")


def kernel(x, w_in, g_cq, g_ckv, w_uq, w_uk, w_uv, w_o, ln1_g, ln1_b, w_up, conv_w, conv_b, w_down, ln2_g, ln2_b, loss_target, m_w_in, m_g_cq, m_g_ckv, m_w_uq, m_w_uk, m_w_uv, m_w_o, m_ln1_g, m_ln1_b, m_w_up, m_conv_w, m_conv_b, m_w_down, m_ln2_g, m_ln2_b, v_w_in, v_g_cq, v_g_ckv, v_w_uq, v_w_uk, v_w_uv, v_w_o, v_ln1_g, v_ln1_b, v_w_up, v_conv_w, v_conv_b, v_w_down, v_ln2_g, v_ln2_b):
    w = dict(w_in=w_in, g_cq=g_cq, g_ckv=g_ckv, w_uq=w_uq, w_uk=w_uk, w_uv=w_uv, w_o=w_o, ln1_g=ln1_g, ln1_b=ln1_b,
             w_up=w_up, conv_w=conv_w, conv_b=conv_b, w_down=w_down, ln2_g=ln2_g, ln2_b=ln2_b)
    m = dict(w_in=m_w_in, g_cq=m_g_cq, g_ckv=m_g_ckv, w_uq=m_w_uq, w_uk=m_w_uk, w_uv=m_w_uv, w_o=m_w_o, ln1_g=m_ln1_g,
             ln1_b=m_ln1_b, w_up=m_w_up, conv_w=m_conv_w, conv_b=m_conv_b, w_down=m_w_down, ln2_g=m_ln2_g, ln2_b=m_ln2_b)
    v = dict(w_in=v_w_in, g_cq=v_g_cq, g_ckv=v_g_ckv, w_uq=v_w_uq, w_uk=v_w_uk, w_uv=v_w_uv, w_o=v_w_o, ln1_g=v_ln1_g,
             ln1_b=v_ln1_b, w_up=v_w_up, conv_w=v_conv_w, conv_b=v_conv_b, w_down=v_w_down, ln2_g=v_ln2_g, ln2_b=v_ln2_b)
    me = 4 * lax.axis_index("x") + 2 * lax.axis_index("y") + lax.axis_index("c")
    wire = lambda a: a.astype(WIRE_DTYPE)
    n_in = w_in.shape[1]
    n_in_pad = -(-n_in // 16) * 16
    pad_taps = lambda a: jnp.pad(a, ((0, 8 - a.shape[0]), (0, 0)))

    own_slot = lambda buf, block: lax.dynamic_update_index_in_dim(buf, block, me, 0)
    blocks = lambda a: wire(a).reshape((N_DEV, a.shape[0] // N_DEV) + a.shape[1:])

    g_in, g_uq, g_conv = _all_gather(
        [jnp.pad(wire(w_in).T, ((0, n_in_pad - n_in), (0, 0))), _heads_major(wire(w_uq)), pad_taps(conv_w)],
        "gather_weights")
    late = _send_start([wire(w_o), wire(w_up).T, wire(w_down)], True, "gather_late_start")
    r_uq_dev, e_uq = w_uq.shape[0], w_uq.shape[2]
    wq = jnp.transpose(g_uq.reshape(N_DEV, HEADS, r_uq_dev, e_uq), (1, 0, 2, 3)).reshape(HEADS, Q_RANK, e_uq)
    cw = dict(
        w_in_t=_split_pad_rows(g_in[:, :n_in].reshape(-1, D_MODEL)).astype(MXU_DTYPE),
        wq=jnp.pad(wq, ((0, 0), (0, 0), (0, LANES - e_uq))).astype(MXU_DTYPE),
        wk=_pad_heads(w_uk, NOPE_DIM), wv=_pad_heads(w_uv, HEAD_DIM),
        conv_w=_ffn_interleave(jnp.transpose(g_conv[:, :conv_w.shape[0]], (1, 0, 2)).reshape(conv_w.shape[0], -1), 1),
        g_cq=_row(g_cq), g_ckv=_row(g_ckv), ln1_g=_row(ln1_g), ln1_b=_row(ln1_b), conv_b=_ffn_interleave(_row(conv_b), 1),
        ln2_g=_row(ln2_g), ln2_b=_row(ln2_b))

    def late_weights(after):
        own, landed = _send_wait(late, [after], "gather_late_wait")
        g_o, g_up, g_down = [own_slot(buf, blk) for buf, blk in zip(landed, own)]
        w_o_mla, w_o_dil = _pad_w_o(g_o.reshape(-1, D_MODEL))
        return dict(w_o_mla=w_o_mla, w_o_dil=w_o_dil, w_up_t=_ffn_interleave(g_up.reshape(-1, D_MODEL), 0).astype(MXU_DTYPE),
                    w_down=g_down.reshape(-1, D_MODEL).astype(MXU_DTYPE))

    sent = {}

    def on_grads(stage, g):
        if stage == "ffn":
            sent[stage] = [_send_start([blocks(_ffn_deinterleave(g["w_up_t"], 0)), blocks(g["w_down"])], False, "exchange_ffn_start")]
        elif stage == "w_o":
            sent[stage] = [_send_start([blocks(_unpad_w_o(g["w_o_mla"], g["w_o_dil"]))], False, "exchange_w_o_start")]
        else:
            d_in = jnp.pad(blocks(_split_unpad_rows(g["w_in_t"])), ((0, 0), (0, n_in_pad - n_in), (0, 0)))
            d_uq = wire(jnp.transpose(g["wq"][:, :, :e_uq].reshape(HEADS, N_DEV, r_uq_dev, e_uq), (1, 0, 2, 3))
                        ).reshape(N_DEV, HEADS * r_uq_dev, e_uq)
            small = dict(g_cq=g["g_cq"], g_ckv=g["g_ckv"], w_uk=g["wk"][:, :, :NOPE_DIM].reshape(-1, NOPE_DIM),
                         w_uv=g["wv"][:, :, :HEAD_DIM].reshape(-1, HEAD_DIM), ln1_g=g["ln1_g"], ln1_b=g["ln1_b"],
                         conv_b=_ffn_deinterleave(g["conv_b"], 1), ln2_g=g["ln2_g"], ln2_b=g["ln2_b"])
            sent[stage] = [_send_start([d_in, d_uq], False, "exchange_rest_start"),
                           _send_start([small[n] for n in REPLICATED] + [pad_taps(_ffn_deinterleave(g["conv_w"], 1)), g["loss"]],
                                       True, "gather_small_start")]
        return [h["token"] for h in sent[stage]]

    _, grad_x, _ = _layer_grads(x[0], loss_target[0], cw, [late["token"]], late_weights, on_grads)

    def landed(handle, after, name):
        own, got = _send_wait(handle, after, name)
        pick = (lambda a: a) if handle["gather"] else (lambda a: lax.dynamic_index_in_dim(a, me, 0, keepdims=False))
        return [own_slot(buf, pick(src)) for buf, src in zip(got, own)]

    out = {}

    def update(name, parts, view=None):
        to2d = {None: lambda a: a, "t": lambda a: a.T, "heads": _heads_major}[view]
        back = {None: lambda a: a, "t": lambda a: a.T, "heads": lambda a: _heads_minor(a, HEADS)}[view]
        res = _adamw(parts, to2d(w[name]), to2d(m[name]), to2d(v[name]), "adamw_" + name)
        for kind, a in zip(("grad", "delta", "new_m", "new_v"), res):
            out[kind, name] = back(a)
        return res[0]

    r_up, r_down = landed(sent["ffn"][0], [grad_x], "exchange_ffn_wait")
    (r_o,) = landed(sent["w_o"][0], [grad_x], "exchange_w_o_wait")
    done = [update("w_up", r_up, "t"), update("w_down", r_down), update("w_o", r_o)]
    r_in, r_uq = landed(sent["rest"][0], done, "exchange_rest_wait")
    *rep_all, cw_all, loss_all = landed(sent["rest"][1], done, "gather_small_wait")
    update("w_in", r_in[:, :n_in], "t")
    update("w_uq", r_uq, "heads")
    heads_major = ("w_uk", "w_uv")
    two_d = lambda n, a: _heads_major(a) if n in heads_major else a.reshape(1, -1)
    res, loss_sum = _adamw_small(rep_all, *[[two_d(n, d[n]) for n in REPLICATED] for d in (w, m, v)], loss_all, "adamw_replicated")
    for n, quad in zip(REPLICATED, res):
        for kind, a in zip(("grad", "delta", "new_m", "new_v"), quad):
            out[kind, n] = _heads_minor(a, HEADS) if n in heads_major else a.reshape(w[n].shape)
    loss = loss_sum[0, 0]
    ncw = conv_w.shape[1]
    update("conv_w", lax.dynamic_slice_in_dim(cw_all[:, :conv_w.shape[0]], me * ncw, ncw, axis=2))

    return (loss, grad_x[None], *[out[kind, n] for kind in ("grad", "delta", "new_m", "new_v") for n in ALL_WEIGHTS])
```

```python
import functools
import math

import jax
import jax.numpy as jnp
from jax import lax
from jax.experimental import pallas as pl
from jax.experimental.pallas import tpu as pltpu

F32 = jnp.float32
MXU_DTYPE = jnp.bfloat16
WIRE_DTYPE = jnp.bfloat16

N_DEV = 8
D_MODEL = 1024
HEADS = 8
HEAD_DIM = 64
LANES = 128
Q_RANK, KV_RANK, ROPE_DIM, NOPE_DIM = 256, 128, 32, 64
DIL_WIDTH = HEADS * HEAD_DIM
IN_WIDTH = 1952
IN_PAD = 2048
D_FF = 2816
ROPE_THETA = 10000.0
DIL_PAIRS = ((128, 1), (512, 4), (2048, 16))
DIL_BLOCK = 128
DN_ALPHA = 2.0 ** 0.25
LN_EPS = 1e-5
RMS_EPS = 1e-6
MLA_SCALE = 1.0 / math.sqrt(NOPE_DIM + ROPE_DIM)
MLA_SCALE_LOG2 = MLA_SCALE * math.log2(math.e)
DIL_SCALE = 1.0 / math.sqrt(HEAD_DIM)
ALIBI_SLOPES = tuple(2.0 ** (-8.0 * (h + 1) / HEADS) for h in range(HEADS))
NEG_BIG = -1e30
ADAM_LR, ADAM_B1, ADAM_B2, ADAM_EPS, ADAM_WD, ADAM_STEP = 0.001, 0.9, 0.999, 1e-08, 0.01, 10
VMEM_LIMIT = 48 * 1024 * 1024


def _params(*sem):
    return pltpu.CompilerParams(dimension_semantics=sem or None, vmem_limit_bytes=VMEM_LIMIT)


def _dot(a, b, ca, cb):
    return lax.dot_general(a, b, (((ca,), (cb,)), ((), ())), preferred_element_type=F32)


_ANY_SPEC = pl.BlockSpec(memory_space=pl.ANY)


def _mm(a, b, *, name, tm, tn, tk, ta=False, tb=False, out_dtype=F32, res=None, res_scale=1.0, after=()):
    m, k = (a.shape[1], a.shape[0]) if ta else a.shape
    n = b.shape[0] if tb else b.shape[1]
    assert (b.shape[1] if tb else b.shape[0]) == k
    tm, tn, tk = min(tm, m), min(tn, n), min(tk, k)
    assert m % tm == 0 and n % tn == 0 and k % tk == 0, (name, m, n, k, tm, tn, tk)
    nk = k // tk
    a_spec = (pl.BlockSpec((tk, tm), lambda i, j, kk: (kk, i)) if ta
              else pl.BlockSpec((tm, tk), lambda i, j, kk: (i, kk)))
    b_mode = dict(pipeline_mode=pl.Buffered(1)) if (tn == n and tk == k) else {}
    b_spec = (pl.BlockSpec((tn, tk), lambda i, j, kk: (j, kk), **b_mode) if tb
              else pl.BlockSpec((tk, tn), lambda i, j, kk: (kk, j), **b_mode))
    o_spec = pl.BlockSpec((tm, tn), lambda i, j, kk: (i, j))
    in_specs = [a_spec, b_spec]
    args = [a, b]
    if res is not None:
        in_specs.append(o_spec)
        args.append(res)
    n_in = len(args) + len(after)
    in_specs += [_ANY_SPEC] * len(after)
    args += list(after)
    ca, cb = (0 if ta else 1), (1 if tb else 0)

    def finish(acc, r_ref, o_ref):
        if r_ref is not None:
            acc = acc + res_scale * r_ref[...]
        o_ref[...] = acc.astype(o_ref.dtype)

    def body(*refs):
        a_ref, b_ref = refs[:2]
        r_ref = refs[2] if res is not None else None
        o_ref = refs[n_in]
        part = _dot(a_ref[...].astype(MXU_DTYPE), b_ref[...].astype(MXU_DTYPE), ca, cb)
        if nk == 1:
            finish(part, r_ref, o_ref)
            return
        acc_ref = refs[-1]
        kk = pl.program_id(2)

        @pl.when(kk == 0)
        def _():
            acc_ref[...] = part

        @pl.when(kk > 0)
        def _():
            acc_ref[...] += part

        @pl.when(kk == nk - 1)
        def _():
            finish(acc_ref[...], r_ref, o_ref)

    return pl.pallas_call(
        body, name=name, grid=(m // tm, n // tn, nk), in_specs=in_specs, out_specs=o_spec,
        out_shape=jax.ShapeDtypeStruct((m, n), out_dtype),
        scratch_shapes=[pltpu.VMEM((tm, tn), F32)] if nk > 1 else [],
        compiler_params=_params("parallel", "parallel", "arbitrary"),
    )(*args)


def _mm_do_mla(dz, w_o_mla, tm=1024):
    seq, d = dz.shape
    tm = min(tm, seq)

    def body(a_ref, b_ref, o_ref):
        a = a_ref[...].astype(MXU_DTYPE)
        for hd in range(HEADS):
            o_ref[hd] = _dot(a, b_ref[LANES * hd:LANES * (hd + 1), :], 1, 1)

    return pl.pallas_call(
        body, name="mm_do_mla", grid=(seq // tm,),
        in_specs=[pl.BlockSpec((tm, d), lambda i: (i, 0)), pl.BlockSpec((HEADS * LANES, d), lambda i: (0, 0))],
        out_specs=pl.BlockSpec((HEADS, tm, LANES), lambda i: (0, i, 0)),
        out_shape=jax.ShapeDtypeStruct((HEADS, seq, LANES), F32), compiler_params=_params("parallel"),
    )(dz, w_o_mla)


def _mm_dw_o_mla(o_mla, dz, tk=1024):
    seq, d = dz.shape
    tk = min(tk, seq)
    nk = seq // tk

    def body(a_ref, b_ref, o_ref, acc_ref):
        kk = pl.program_id(0)

        @pl.when(kk == 0)
        def _():
            acc_ref[...] = jnp.zeros_like(acc_ref)

        b = b_ref[...].astype(MXU_DTYPE)
        for hd in range(HEADS):
            acc_ref[LANES * hd:LANES * (hd + 1), :] += _dot(a_ref[hd].astype(MXU_DTYPE), b, 0, 0)

        @pl.when(kk == nk - 1)
        def _():
            o_ref[...] = acc_ref[...].astype(o_ref.dtype)

    return pl.pallas_call(
        body, name="mm_dw_o_mla", grid=(nk,),
        in_specs=[pl.BlockSpec((HEADS, tk, LANES), lambda kk: (0, kk, 0)), pl.BlockSpec((tk, d), lambda kk: (kk, 0))],
        out_specs=pl.BlockSpec((HEADS * LANES, d), lambda kk: (0, 0)),
        out_shape=jax.ShapeDtypeStruct((HEADS * LANES, d), MXU_DTYPE),
        scratch_shapes=[pltpu.VMEM((HEADS * LANES, d), F32)], compiler_params=_params("arbitrary"),
    )(o_mla, dz)


def _rope_tables(seq):
    half = ROPE_DIM // 2
    freqs = ROPE_THETA ** (-jnp.arange(half, dtype=F32) / half)
    ang = jnp.arange(seq).astype(F32)[:, None] * freqs[None, :]
    cos, sin = jnp.cos(ang), jnp.sin(ang)
    one = jnp.ones((seq, NOPE_DIM), F32)
    tail = jnp.ones((seq, LANES - NOPE_DIM - ROPE_DIM), F32)
    ctab = jnp.concatenate([one, cos, cos, tail], axis=1)
    stab = jnp.concatenate([0 * one, -sin, sin, 0 * tail], axis=1)
    return ctab, stab


def _rope_swap(t):
    lane = lax.broadcasted_iota(jnp.int32, t.shape, 1)
    half = ROPE_DIM // 2
    return jnp.where(lane < NOPE_DIM + half, pltpu.roll(t, LANES - half, 1), pltpu.roll(t, half, 1))


def _rope(t, ctab, stab):
    return t * ctab + _rope_swap(t) * stab


def _rope_inv(t, ctab, stab):
    return t * ctab - _rope_swap(t) * stab


def _rms(x, g):
    r = lax.rsqrt(jnp.mean(x * x, axis=-1, keepdims=True) + RMS_EPS)
    xh = x * r
    return xh, r, xh * g


def _mla_prep(h, g_cq, g_ckv, wq, wk, wv, ctab, stab, tm=512):
    seq = h.shape[0]
    tm = min(tm, seq)

    def body(h_ref, gq_ref, gk_ref, wq_ref, wk_ref, wv_ref, c_ref, s_ref, q_out, k_out, v_out):
        hb = h_ref[...]
        ctab_, stab_ = c_ref[...], s_ref[...]
        _, _, cqn = _rms(hb[:, :Q_RANK], gq_ref[...])
        _, _, ckn = _rms(hb[:, Q_RANK:Q_RANK + KV_RANK], gk_ref[...])
        cqn = cqn.astype(MXU_DTYPE)
        ckn = ckn.astype(MXU_DTYPE)
        krr = _rope(hb[:, Q_RANK + KV_RANK:], ctab_, stab_)
        for hd in range(HEADS):
            q = _dot(cqn, wq_ref[hd], 1, 0)
            q_out[hd] = _rope(q, ctab_, stab_).astype(q_out.dtype)
            k_out[hd] = (_dot(ckn, wk_ref[hd], 1, 0) + krr).astype(k_out.dtype)
            v_out[hd] = _dot(ckn, wv_ref[hd], 1, 0).astype(v_out.dtype)

    full = lambda *shape: pl.BlockSpec(shape, lambda i: (0,) * len(shape))
    slab = pl.BlockSpec((HEADS, tm, LANES), lambda i: (0, i, 0))
    shp = jax.ShapeDtypeStruct((HEADS, seq, LANES), MXU_DTYPE)
    return pl.pallas_call(
        body, name="mla_prep", grid=(seq // tm,),
        in_specs=[pl.BlockSpec((tm, 512), lambda i: (i, 0)), full(1, Q_RANK), full(1, KV_RANK),
                  full(HEADS, Q_RANK, LANES), full(HEADS, KV_RANK, LANES), full(HEADS, KV_RANK, LANES),
                  pl.BlockSpec((tm, LANES), lambda i: (i, 0)), pl.BlockSpec((tm, LANES), lambda i: (i, 0))],
        out_specs=[slab, slab, slab], out_shape=[shp, shp, shp],
        compiler_params=_params("parallel"),
    )(h, g_cq, g_ckv, wq, wk, wv, ctab, stab)


def _mla_prep_bwd(h, g_cq, g_ckv, wq, wk, wv, ctab, stab, dq, dk, dv, tm=512, after=()):
    seq = h.shape[0]
    tm = min(tm, seq)
    n_after = len(after)

    def body(h_ref, gq_ref, gk_ref, wq_ref, wk_ref, wv_ref, c_ref, s_ref, dq_ref, dk_ref, dv_ref, *rest):
        dh_ref, dwq_ref, dwk_ref, dwv_ref, dgq_ref, dgk_ref = rest[n_after:]

        @pl.when(pl.program_id(0) == 0)
        def _():
            for r in (dwq_ref, dwk_ref, dwv_ref, dgq_ref, dgk_ref):
                r[...] = jnp.zeros_like(r)

        hb = h_ref[...]
        ctab_, stab_ = c_ref[...], s_ref[...]
        gq, gk = gq_ref[...], gk_ref[...]
        xq, rq, cqn = _rms(hb[:, :Q_RANK], gq)
        xk, rk, ckn = _rms(hb[:, Q_RANK:Q_RANK + KV_RANK], gk)
        cqn = cqn.astype(MXU_DTYPE)
        ckn = ckn.astype(MXU_DTYPE)
        d_cqn = jnp.zeros((tm, Q_RANK), F32)
        d_ckn = jnp.zeros((tm, KV_RANK), F32)
        d_krr = jnp.zeros((tm, LANES), F32)
        for hd in range(HEADS):
            dqh = _rope_inv(dq_ref[hd], ctab_, stab_).astype(MXU_DTYPE)
            d_cqn += _dot(dqh, wq_ref[hd], 1, 1)
            dwq_ref[hd] += _dot(cqn, dqh, 0, 0)
            dkh = dk_ref[hd]
            d_krr += dkh
            dkh = dkh.astype(MXU_DTYPE)
            d_ckn += _dot(dkh, wk_ref[hd], 1, 1)
            dwk_ref[hd] += _dot(ckn, dkh, 0, 0)
            dvh = dv_ref[hd].astype(MXU_DTYPE)
            d_ckn += _dot(dvh, wv_ref[hd], 1, 1)
            dwv_ref[hd] += _dot(ckn, dvh, 0, 0)
        lane = lax.broadcasted_iota(jnp.int32, (tm, LANES), 1)
        rot = (lane >= NOPE_DIM) & (lane < NOPE_DIM + ROPE_DIM)
        d_kr = jnp.where(rot, _rope_inv(jnp.where(rot, d_krr, 0.0), ctab_, stab_), 0.0)

        def rms_bwd(dy, xh, r, g, dg_ref):
            dg_ref[...] += jnp.sum(dy * xh, axis=0, keepdims=True)
            dxh = dy * g
            return r * (dxh - xh * jnp.mean(dxh * xh, axis=-1, keepdims=True))

        d_cq = rms_bwd(d_cqn, xq, rq, gq, dgq_ref)
        d_ck = rms_bwd(d_ckn, xk, rk, gk, dgk_ref)
        dh_ref[...] = jnp.concatenate([d_cq, d_ck, d_kr], axis=1).astype(dh_ref.dtype)

    full = lambda *shape: pl.BlockSpec(shape, lambda i: (0,) * len(shape))
    slab = pl.BlockSpec((HEADS, tm, LANES), lambda i: (0, i, 0))
    return pl.pallas_call(
        body, name="mla_prep_bwd", grid=(seq // tm,),
        in_specs=[pl.BlockSpec((tm, 512), lambda i: (i, 0)), full(1, Q_RANK), full(1, KV_RANK),
                  full(HEADS, Q_RANK, LANES), full(HEADS, KV_RANK, LANES), full(HEADS, KV_RANK, LANES),
                  pl.BlockSpec((tm, LANES), lambda i: (i, 0)), pl.BlockSpec((tm, LANES), lambda i: (i, 0)),
                  slab, slab, slab] + [_ANY_SPEC] * n_after,
        out_specs=[pl.BlockSpec((tm, 512), lambda i: (i, 0)), full(HEADS, Q_RANK, LANES), full(HEADS, KV_RANK, LANES),
                   full(HEADS, KV_RANK, LANES), full(1, Q_RANK), full(1, KV_RANK)],
        out_shape=[jax.ShapeDtypeStruct((seq, 512), MXU_DTYPE), jax.ShapeDtypeStruct((HEADS, Q_RANK, LANES), F32),
                   jax.ShapeDtypeStruct((HEADS, KV_RANK, LANES), F32), jax.ShapeDtypeStruct((HEADS, KV_RANK, LANES), F32),
                   jax.ShapeDtypeStruct((1, Q_RANK), F32), jax.ShapeDtypeStruct((1, KV_RANK), F32)],
        compiler_params=_params("arbitrary"),
    )(h, g_cq, g_ckv, wq, wk, wv, ctab, stab, dq, dk, dv, *after)


def _causal_mask(t):
    row = lax.broadcasted_iota(jnp.int32, (t, t), 0)
    col = lax.broadcasted_iota(jnp.int32, (t, t), 1)
    return row >= col


def _mla_attn_fwd(q, k, v, t=512):
    _, seq, _ = q.shape
    t = min(t, seq)

    def body(q_ref, k_ref, v_ref, o_ref, ob_ref, lse_ref, m_ref, l_ref, acc_ref, s_ref):
        i = pl.program_id(1)
        qb = q_ref[...]
        m_ref[...] = jnp.full_like(m_ref, NEG_BIG)
        l_ref[...] = jnp.zeros_like(l_ref)
        acc_ref[...] = jnp.zeros_like(acc_ref)

        def scores(j):
            return _dot(qb, k_ref[pl.ds(pl.multiple_of(j * t, t), t), :], 1, 1) * MLA_SCALE_LOG2

        def softmax_pv(j, s, masked):
            vb = v_ref[pl.ds(pl.multiple_of(j * t, t), t), :]
            if masked:
                s = jnp.where(_causal_mask(t), s, NEG_BIG)
            m_old = m_ref[...]
            m_new = jnp.maximum(m_old, jnp.max(s, axis=1, keepdims=True))
            p = jnp.exp2(s - m_new)
            a = jnp.exp2(m_old - m_new)
            l_ref[...] = a * l_ref[...] + jnp.sum(p, axis=1, keepdims=True)
            acc_ref[...] = a * acc_ref[...] + _dot(p.astype(MXU_DTYPE), vb, 1, 0)
            m_ref[...] = m_new

        s_ref[...] = scores(0)

        def loop_body(j, c):
            s_next = scores(j + 1)
            softmax_pv(j, s_ref[...], False)
            s_ref[...] = s_next
            return c

        lax.fori_loop(0, i, loop_body, 0)
        softmax_pv(i, s_ref[...], True)
        l = l_ref[...]
        o = acc_ref[...] * (1.0 / l)
        o_ref[...] = o
        ob_ref[...] = o.astype(ob_ref.dtype)
        lse_ref[...] = jnp.broadcast_to(m_ref[...] + jnp.log2(l), lse_ref.shape)

    blk = pl.BlockSpec((None, t, LANES), lambda h, i: (h, i, 0))
    whole = pl.BlockSpec((None, seq, LANES), lambda h, i: (h, 0, 0))
    shp = jax.ShapeDtypeStruct((HEADS, seq, LANES), F32)
    return pl.pallas_call(
        body, name="mla_attn_fwd", grid=(HEADS, seq // t),
        in_specs=[blk, whole, whole], out_specs=[blk, blk, blk],
        out_shape=[shp, jax.ShapeDtypeStruct((HEADS, seq, LANES), MXU_DTYPE), shp],
        scratch_shapes=[pltpu.VMEM((t, 1), F32), pltpu.VMEM((t, 1), F32), pltpu.VMEM((t, LANES), F32), pltpu.VMEM((t, t), F32)],
        compiler_params=_params("parallel", "arbitrary"),
    )(q, k, v)


def _mla_attn_bwd(q, k, v, o, lse, do, t=512):
    _, seq, _ = q.shape
    t = min(t, seq)
    nb = seq // t

    def body(q_ref, k_ref, v_ref, o_ref, lse_ref, do_ref, dq_ref, dk_ref, dv_ref, dl_ref, dka_ref, dva_ref):
        dq_ref[...] = jnp.zeros_like(dq_ref)

        def delta_body(i, c):
            rows = pl.ds(pl.multiple_of(i * t, t), t)
            dl_ref[rows, :] = jnp.sum(do_ref[rows, :] * o_ref[rows, :], axis=1, keepdims=True)
            return c

        lax.fori_loop(0, nb, delta_body, 0)

        def kblock(j, c):
            krows = pl.ds(pl.multiple_of(j * t, t), t)
            kb = k_ref[krows, :]
            vb = v_ref[krows, :]
            dka_ref[...] = jnp.zeros_like(dka_ref)
            dva_ref[...] = jnp.zeros_like(dva_ref)

            def qstep(i, masked):
                th = t // 2
                rows = [pl.ds(pl.multiple_of(i * t + hf * th, th), th) for hf in range(2)]
                qs = [q_ref[r, :] for r in rows]
                dos = [do_ref[r, :].astype(MXU_DTYPE) for r in rows]
                ss = [_dot(qb, kb, 1, 1) * MLA_SCALE_LOG2 for qb in qs]
                dps = [_dot(dob, vb, 1, 1) for dob in dos]
                for hf in range(2):
                    s = ss[hf]
                    if masked:
                        row = lax.broadcasted_iota(jnp.int32, (th, t), 0) + hf * th
                        s = jnp.where(row >= lax.broadcasted_iota(jnp.int32, (th, t), 1), s, NEG_BIG)
                    p = jnp.exp2(s - lse_ref[rows[hf], 0:1])
                    dva_ref[...] += _dot(p.astype(MXU_DTYPE), dos[hf], 0, 0)
                    ds = (p * (dps[hf] - dl_ref[rows[hf], :]) * MLA_SCALE).astype(MXU_DTYPE)
                    dka_ref[...] += _dot(ds, qs[hf], 0, 0)
                    dq_ref[rows[hf], :] += _dot(ds, kb, 1, 0)

            qstep(j, True)

            def qloop(i, c2):
                qstep(i, False)
                return c2

            lax.fori_loop(j + 1, nb, qloop, 0)
            dk_ref[krows, :] = dka_ref[...]
            dv_ref[krows, :] = dva_ref[...]
            return c

        lax.fori_loop(0, nb, kblock, 0)

    whole = pl.BlockSpec((None, seq, LANES), lambda h: (h, 0, 0))
    shp = jax.ShapeDtypeStruct((HEADS, seq, LANES), F32)
    return pl.pallas_call(
        body, name="mla_attn_bwd", grid=(HEADS,),
        in_specs=[whole] * 6, out_specs=[whole] * 3, out_shape=[shp] * 3,
        scratch_shapes=[pltpu.VMEM((seq, 1), F32), pltpu.VMEM((t, LANES), F32), pltpu.VMEM((t, LANES), F32)],
        compiler_params=_params("parallel"),
    )(q, k, v, o, lse, do)


DIL_CHUNK = DIL_BLOCK * max(d for _, d in DIL_PAIRS)
DIL_PAIR_LANES = 2 * HEAD_DIM
assert DIL_PAIR_LANES == LANES
DIL_UNROLL_FWD = 4
DIL_UNROLL_BWD = 4


def _dil_bias_tables(hp, dil):
    b = DIL_BLOCK
    iq = lax.broadcasted_iota(jnp.int32, (b, 2 * b), 0)
    ik = lax.broadcasted_iota(jnp.int32, (b, 2 * b), 1)
    off = iq + b - ik
    band = (off >= 0) & (off <= b)
    dist = (off * dil).astype(F32)
    every, first = [], []
    for hh in range(2):
        slope = jnp.where(hp == 0, ALIBI_SLOPES[hh], jnp.where(hp == 1, ALIBI_SLOPES[2 + hh],
                          jnp.where(hp == 2, ALIBI_SLOPES[4 + hh], ALIBI_SLOPES[6 + hh]))).astype(F32)
        bias = -slope * dist
        every.append(jnp.where(band, bias, NEG_BIG))
        first.append(jnp.where(band & (ik >= b), bias, NEG_BIG))
    return jnp.concatenate(every, axis=0), jnp.concatenate(first, axis=0)


def _dil_rows(start, dil):
    return pl.ds(start, DIL_BLOCK) if dil == 1 else pl.ds(start, DIL_BLOCK, stride=dil)


def _dil_block_pos(blk, c, dil):
    sc, r = blk // dil, blk % dil
    q0 = sc * (DIL_BLOCK * dil) + r
    kcur0 = c * DIL_CHUNK + q0
    first = kcur0 < DIL_BLOCK * dil
    kprev0 = jnp.where(first, kcur0, kcur0 - DIL_BLOCK * dil)
    return q0, kcur0, kprev0, first


def _pair_cols(hh):
    return slice(HEAD_DIM * hh, HEAD_DIM * (hh + 1))


def _first_head_lanes(shape):
    return lax.broadcasted_iota(jnp.int32, shape, 1) < HEAD_DIM


def _stack_pair(t):
    first = _first_head_lanes(t.shape)
    return jnp.concatenate([jnp.where(first, t, 0.0), jnp.where(first, 0.0, t)], axis=0).astype(MXU_DTYPE)


def _unstack_pair(t):
    rows = t.shape[0] // 2
    return jnp.where(_first_head_lanes((rows, t.shape[1])), t[:rows], t[rows:])


def _pair_column(t):
    return jnp.concatenate([t[:, 0:1], t[:, HEAD_DIM:HEAD_DIM + 1]], axis=0)


def _dil_fwd(h):
    seq = h.shape[0]
    assert seq % DIL_CHUNK == 0
    nblk = DIL_CHUNK // DIL_BLOCK
    rc = 256

    def body(q_ref, k_ref, v_ref, o_ref, ob_ref, lse_ref, *scr):
        o_scr, l_scr = scr[:3], scr[3:]
        hp, c = pl.program_id(0), pl.program_id(1)
        for bi, (_, dil) in enumerate(DIL_PAIRS):
            tables = _dil_bias_tables(hp, dil)

            def block(blk, carry, bi=bi, dil=dil, tables=tables):
                q0, kcur0, kprev0, first = _dil_block_pos(blk, c, dil)
                q2 = _stack_pair(q_ref[_dil_rows(q0, dil), :] * DIL_SCALE)
                kcat = jnp.concatenate([k_ref[_dil_rows(kprev0, dil), :], k_ref[_dil_rows(kcur0, dil), :]], axis=0).astype(MXU_DTYPE)
                vcat = jnp.concatenate([v_ref[_dil_rows(kprev0, dil), :], v_ref[_dil_rows(kcur0, dil), :]], axis=0).astype(MXU_DTYPE)
                s = _dot(q2, kcat, 1, 1) + jnp.where(first, tables[1], tables[0])
                mx = jnp.max(s, axis=1, keepdims=True)
                p = jnp.exp(s - mx)
                l = jnp.sum(p, axis=1, keepdims=True)
                o_scr[bi][_dil_rows(q0, dil), :] = _unstack_pair(_dot(p.astype(MXU_DTYPE), vcat, 1, 0) * (1.0 / l))
                l_scr[bi][_dil_rows(q0, dil), :] = _unstack_pair(jnp.broadcast_to(mx + jnp.log(l), (2 * DIL_BLOCK, LANES)))
                return carry

            lax.fori_loop(0, nblk, block, 0, unroll=DIL_UNROLL_FWD)

        def combine(i, carry):
            rows = pl.ds(pl.multiple_of(i * rc, rc), rc)
            ls = [l_scr[bi][rows, :] for bi in range(3)]
            mx = jnp.maximum(jnp.maximum(ls[0], ls[1]), ls[2])
            es = [jnp.exp(l - mx) for l in ls]
            den = es[0] + es[1] + es[2]
            o = (es[0] * o_scr[0][rows, :] + es[1] * o_scr[1][rows, :] + es[2] * o_scr[2][rows, :]) / den
            o_ref[rows, :] = o
            ob_ref[rows, :] = o.astype(ob_ref.dtype)
            lse_ref[rows, :] = mx + jnp.log(den)
            return carry

        lax.fori_loop(0, DIL_CHUNK // rc, combine, 0)

    nq = DIL_WIDTH // LANES
    chunk = lambda off: pl.BlockSpec((DIL_CHUNK, LANES), lambda hp, c: (c, off + hp))
    whole = lambda off: pl.BlockSpec((seq, LANES), lambda hp, c: (0, off + hp))
    shp = jax.ShapeDtypeStruct((seq, DIL_WIDTH), F32)
    return pl.pallas_call(
        body, name="dil_fwd", grid=(nq, seq // DIL_CHUNK),
        in_specs=[chunk(nq), whole(2 * nq), whole(3 * nq)], out_specs=[chunk(0), chunk(0), chunk(0)],
        out_shape=[shp, jax.ShapeDtypeStruct((seq, DIL_WIDTH), MXU_DTYPE), shp],
        scratch_shapes=[pltpu.VMEM((DIL_CHUNK, LANES), F32)] * 6,
        compiler_params=_params("parallel", "arbitrary"),
    )(h, h, h)


def _dil_bwd(h, o, lse, do):
    seq = h.shape[0]
    nblk = DIL_CHUNK // DIL_BLOCK
    rc = 256

    def body(q_ref, k_ref, v_ref, o_ref, lse_ref, do_ref, dq_ref, dk_ref, dv_ref, dl_scr):
        hp, c = pl.program_id(0), pl.program_id(1)

        @pl.when(c == 0)
        def _():
            dk_ref[...] = jnp.zeros_like(dk_ref)
            dv_ref[...] = jnp.zeros_like(dv_ref)

        def delta(i, carry):
            rows = pl.ds(pl.multiple_of(i * rc, rc), rc)
            prod = do_ref[rows, :] * o_ref[rows, :]
            dl_scr[rows, :] = jnp.concatenate(
                [jnp.broadcast_to(jnp.sum(prod[:, _pair_cols(hh)], axis=1, keepdims=True), (rc, HEAD_DIM)) for hh in range(2)], axis=1)
            return carry

        lax.fori_loop(0, DIL_CHUNK // rc, delta, 0)

        for bi, (_, dil) in enumerate(DIL_PAIRS):
            tables = _dil_bias_tables(hp, dil)

            def block(blk, carry, bi=bi, dil=dil, tables=tables):
                q0, kcur0, kprev0, first = _dil_block_pos(blk, c, dil)
                qrows = _dil_rows(q0, dil)
                q2 = _stack_pair(q_ref[qrows, :] * DIL_SCALE)
                kcat = jnp.concatenate([k_ref[_dil_rows(kprev0, dil), :], k_ref[_dil_rows(kcur0, dil), :]], axis=0).astype(MXU_DTYPE)
                vcat = jnp.concatenate([v_ref[_dil_rows(kprev0, dil), :], v_ref[_dil_rows(kcur0, dil), :]], axis=0).astype(MXU_DTYPE)
                do2 = _stack_pair(do_ref[qrows, :])
                s = _dot(q2, kcat, 1, 1) + jnp.where(first, tables[1], tables[0])
                p = jnp.exp(s - _pair_column(lse_ref[qrows, :]))
                dp = _dot(do2, vcat, 1, 1)
                ds = (p * (dp - _pair_column(dl_scr[qrows, :]))).astype(MXU_DTYPE)
                dq_b = _unstack_pair(_dot(ds, kcat, 1, 0)) * DIL_SCALE
                dk_b = _dot(ds, q2, 0, 0)
                dv_b = _dot(p.astype(MXU_DTYPE), do2, 0, 0)
                if bi == 0:
                    dq_ref[qrows, :] = dq_b
                else:
                    dq_ref[qrows, :] += dq_b
                dk_ref[_dil_rows(kprev0, dil), :] += dk_b[:DIL_BLOCK]
                dv_ref[_dil_rows(kprev0, dil), :] += dv_b[:DIL_BLOCK]
                dk_ref[_dil_rows(kcur0, dil), :] += dk_b[DIL_BLOCK:]
                dv_ref[_dil_rows(kcur0, dil), :] += dv_b[DIL_BLOCK:]
                return carry

            lax.fori_loop(0, nblk, block, 0, unroll=DIL_UNROLL_BWD)

    nq = DIL_WIDTH // LANES
    chunk = lambda off: pl.BlockSpec((DIL_CHUNK, LANES), lambda hp, c: (c, off + hp))
    whole = lambda off: pl.BlockSpec((seq, LANES), lambda hp, c: (0, off + hp))
    shp = jax.ShapeDtypeStruct((seq, DIL_WIDTH), F32)
    return pl.pallas_call(
        body, name="dil_bwd", grid=(nq, seq // DIL_CHUNK),
        in_specs=[chunk(nq), whole(2 * nq), whole(3 * nq), chunk(0), chunk(0), chunk(0)],
        out_specs=[chunk(0), whole(0), whole(0)], out_shape=[shp, shp, shp],
        scratch_shapes=[pltpu.VMEM((DIL_CHUNK, LANES), F32)],
        compiler_params=_params("parallel", "arbitrary"),
    )(h, h, h, o, lse, do)


def _assemble_dh(dh_mla, dq, dk, dv, tm=512):
    seq = dh_mla.shape[0]
    tm = min(tm, seq)

    def body(a_ref, q_ref, k_ref, v_ref, o_ref):
        for j, r in enumerate((a_ref, q_ref, k_ref, v_ref)):
            o_ref[:, 512 * j:512 * (j + 1)] = r[...].astype(o_ref.dtype)

    blk = pl.BlockSpec((tm, 512), lambda i: (i, 0))
    return pl.pallas_call(
        body, name="assemble_dh", grid=(seq // tm,), in_specs=[blk] * 4,
        out_specs=pl.BlockSpec((tm, IN_PAD), lambda i: (i, 0)), out_shape=jax.ShapeDtypeStruct((seq, IN_PAD), MXU_DTYPE),
        compiler_params=_params("parallel"),
    )(dh_mla, dq, dk, dv)


def _ln_stats(z):
    mu = jnp.mean(z, axis=-1, keepdims=True)
    zc = z - mu
    r = lax.rsqrt(jnp.mean(zc * zc, axis=-1, keepdims=True) + LN_EPS)
    return zc * r, r


def _ln_bwd_math(dy, xh, r, g):
    dxh = dy * g
    return r * (dxh - jnp.mean(dxh, axis=-1, keepdims=True) - xh * jnp.mean(dxh * xh, axis=-1, keepdims=True))


def _mix_ln1(o_mla, o_dil, w_o_mla, w_o_dil, x0, g, b, tm=512):
    seq, d = x0.shape
    tm = min(tm, seq)

    def body(om_ref, od_ref, wm_ref, wd_ref, x_ref, g_ref, b_ref, z_ref, y_ref, yb_ref):
        mix = _dot(od_ref[...], wd_ref[...], 1, 0)
        for hd in range(HEADS):
            mix += _dot(om_ref[hd], wm_ref[LANES * hd:LANES * (hd + 1), :], 1, 0)
        z = DN_ALPHA * x_ref[...] + mix
        xh, _ = _ln_stats(z)
        y = xh * g_ref[...] + b_ref[...]
        z_ref[...] = z
        y_ref[...] = y
        yb_ref[...] = y.astype(yb_ref.dtype)

    blk = pl.BlockSpec((tm, d), lambda i: (i, 0))
    vec = pl.BlockSpec((1, d), lambda i: (0, 0))
    shp = jax.ShapeDtypeStruct((seq, d), F32)
    return pl.pallas_call(
        body, name="mix_ln1", grid=(seq // tm,),
        in_specs=[pl.BlockSpec((HEADS, tm, LANES), lambda i: (0, i, 0)), pl.BlockSpec((tm, DIL_WIDTH), lambda i: (i, 0)),
                  pl.BlockSpec((HEADS * LANES, d), lambda i: (0, 0)), pl.BlockSpec((DIL_WIDTH, d), lambda i: (0, 0)), blk, vec, vec],
        out_specs=[blk, blk, blk], out_shape=[shp, shp, jax.ShapeDtypeStruct((seq, d), MXU_DTYPE)],
        compiler_params=_params("parallel"))(o_mla, o_dil, w_o_mla, w_o_dil, x0, g, b)


def _ln_bwd(dy, z, g, name, tm=512, after=()):
    seq, d = z.shape
    tm = min(tm, seq)
    n_after = len(after)

    def body(dy_ref, z_ref, g_ref, *rest):
        dz_ref, dzb_ref, dg_ref, db_ref = rest[n_after:]

        @pl.when(pl.program_id(0) == 0)
        def _():
            dg_ref[...] = jnp.zeros_like(dg_ref)
            db_ref[...] = jnp.zeros_like(db_ref)

        dyb = dy_ref[...]
        xh, r = _ln_stats(z_ref[...])
        dg_ref[...] += jnp.sum(dyb * xh, axis=0, keepdims=True)
        db_ref[...] += jnp.sum(dyb, axis=0, keepdims=True)
        dz = _ln_bwd_math(dyb, xh, r, g_ref[...])
        dz_ref[...] = dz
        dzb_ref[...] = dz.astype(dzb_ref.dtype)

    blk = pl.BlockSpec((tm, d), lambda i: (i, 0))
    vec = pl.BlockSpec((1, d), lambda i: (0, 0))
    return pl.pallas_call(
        body, name=name, grid=(seq // tm,), in_specs=[blk, blk, vec] + [_ANY_SPEC] * n_after, out_specs=[blk, blk, vec, vec],
        out_shape=[jax.ShapeDtypeStruct((seq, d), F32), jax.ShapeDtypeStruct((seq, d), MXU_DTYPE),
                   jax.ShapeDtypeStruct((1, d), F32), jax.ShapeDtypeStruct((1, d), F32)],
        compiler_params=_params("arbitrary"))(dy, z, g, *after)


def _ln2_loss_bwd(x1, ffn, target, g, b, tm=512):
    seq, d = x1.shape
    tm = min(tm, seq)

    def body(x_ref, f_ref, t_ref, g_ref, b_ref, dz_ref, dzb_ref, loss_ref, dg_ref, db_ref):
        @pl.when(pl.program_id(0) == 0)
        def _():
            loss_ref[...] = jnp.zeros_like(loss_ref)
            dg_ref[...] = jnp.zeros_like(dg_ref)
            db_ref[...] = jnp.zeros_like(db_ref)

        gv = g_ref[...]
        z = DN_ALPHA * x_ref[...] + f_ref[...]
        xh, r = _ln_stats(z)
        err = (xh * gv + b_ref[...]) - t_ref[...]
        loss_ref[...] += 0.5 * jnp.sum(jnp.mean(err * err, axis=-1, keepdims=True), axis=0, keepdims=True)
        dy = err * (1.0 / d)
        dg_ref[...] += jnp.sum(dy * xh, axis=0, keepdims=True)
        db_ref[...] += jnp.sum(dy, axis=0, keepdims=True)
        dz = _ln_bwd_math(dy, xh, r, gv)
        dz_ref[...] = dz
        dzb_ref[...] = dz.astype(dzb_ref.dtype)

    blk = pl.BlockSpec((tm, d), lambda i: (i, 0))
    vec = pl.BlockSpec((1, d), lambda i: (0, 0))
    return pl.pallas_call(
        body, name="ln2_loss_bwd", grid=(seq // tm,), in_specs=[blk, blk, blk, vec, vec],
        out_specs=[blk, blk, pl.BlockSpec((1, LANES), lambda i: (0, 0)), vec, vec],
        out_shape=[jax.ShapeDtypeStruct((seq, d), F32), jax.ShapeDtypeStruct((seq, d), MXU_DTYPE),
                   jax.ShapeDtypeStruct((1, LANES), F32),
                   jax.ShapeDtypeStruct((1, d), F32), jax.ShapeDtypeStruct((1, d), F32)],
        compiler_params=_params("arbitrary"))(x1, ffn, target, g, b)


HALO = 16


def _conv_rows(e, w_ref, b_ref):
    y = b_ref[...] + w_ref[0:1, :] * pltpu.roll(e, 2, 0)
    y = y + w_ref[1:2, :] * pltpu.roll(e, 1, 0)
    return y + w_ref[2:3, :] * e


_GELU_C = math.sqrt(2.0 / math.pi)
_GELU_A = 0.044715


def _gelu(x):
    return 0.5 * x * (1.0 + jnp.tanh(_GELU_C * (x + _GELU_A * (x * x * x))))


CONV_TN = 256


def _ffn_interleave(a, axis):
    shp = a.shape
    a = a.reshape(shp[:axis] + (2, D_FF // CONV_TN, CONV_TN) + shp[axis + 1:])
    return jnp.swapaxes(a, axis, axis + 1).reshape(shp)


def _ffn_deinterleave(a, axis):
    shp = a.shape
    a = a.reshape(shp[:axis] + (D_FF // CONV_TN, 2, CONV_TN) + shp[axis + 1:])
    return jnp.swapaxes(a, axis, axis + 1).reshape(shp)


def _conv_gate_fwd(u, conv_w, conv_b, tm=1024):
    seq = u.shape[0]
    tm = min(tm, seq)
    tn = CONV_TN

    def body(u_ref, up_ref, w_ref, b_ref, o_ref):
        first = pl.program_id(0) == 0
        e = jnp.concatenate([jnp.where(first, 0.0, up_ref[...]), u_ref[...]], axis=0)
        y = _conv_rows(e, w_ref, b_ref)[HALO:]
        o_ref[...] = (_gelu(y[:, tn:]) * y[:, :tn]).astype(o_ref.dtype)

    hb = tm // HALO
    return pl.pallas_call(
        body, name="conv_gate_fwd", grid=(seq // tm, D_FF // tn),
        in_specs=[pl.BlockSpec((tm, 2 * tn), lambda i, j: (i, j)),
                  pl.BlockSpec((HALO, 2 * tn), lambda i, j: (jnp.maximum(i * hb - 1, 0), j)),
                  pl.BlockSpec((3, 2 * tn), lambda i, j: (0, j)), pl.BlockSpec((1, 2 * tn), lambda i, j: (0, j))],
        out_specs=pl.BlockSpec((tm, tn), lambda i, j: (i, j)), out_shape=jax.ShapeDtypeStruct((seq, D_FF), MXU_DTYPE),
        compiler_params=_params("parallel", "parallel"),
    )(u, u, conv_w, conv_b)


def _conv_gate_bwd(u, d_act, conv_w, conv_b, tm=512):
    seq = u.shape[0]
    tm = min(tm, seq)
    tn = CONV_TN
    ni = seq // tm
    rows_e = tm + 2 * HALO

    def body(u_ref, up_ref, un_ref, da_ref, dan_ref, w_ref, b_ref, du_ref, dw_ref, db_ref):
        i = pl.program_id(1)
        first, last = i == 0, i == ni - 1

        @pl.when(i == 0)
        def _():
            dw_ref[...] = jnp.zeros_like(dw_ref)
            db_ref[...] = jnp.zeros_like(db_ref)

        e = jnp.concatenate([jnp.where(first, 0.0, up_ref[...]), u_ref[...], jnp.where(last, 0.0, un_ref[...])], axis=0)
        y = _conv_rows(e, w_ref, b_ref)
        ya, yg = y[:, :tn], y[:, tn:]
        dact = jnp.concatenate([jnp.zeros((HALO, tn), F32), da_ref[...].astype(F32),
                                jnp.where(last, 0.0, dan_ref[...].astype(F32))], axis=0)
        th = jnp.tanh(_GELU_C * (yg + _GELU_A * (yg * yg * yg)))
        gelu = 0.5 * yg * (1.0 + th)
        gelu_grad = 0.5 * (1.0 + th) + 0.5 * yg * (1.0 - th * th) * (_GELU_C * (1.0 + 3.0 * _GELU_A * (yg * yg)))
        dy = jnp.concatenate([dact * gelu, dact * ya * gelu_grad], axis=1)
        du = w_ref[2:3, :] * dy + w_ref[1:2, :] * pltpu.roll(dy, rows_e - 1, 0) + w_ref[0:1, :] * pltpu.roll(dy, rows_e - 2, 0)
        du_ref[...] = du[HALO:HALO + tm].astype(du_ref.dtype)
        dyt = dy[HALO:HALO + tm]
        dw_ref[0:1, :] += jnp.sum(dyt * pltpu.roll(e, 2, 0)[HALO:HALO + tm], axis=0, keepdims=True)
        dw_ref[1:2, :] += jnp.sum(dyt * pltpu.roll(e, 1, 0)[HALO:HALO + tm], axis=0, keepdims=True)
        dw_ref[2:3, :] += jnp.sum(dyt * e[HALO:HALO + tm], axis=0, keepdims=True)
        db_ref[...] += jnp.sum(dyt, axis=0, keepdims=True)

    hb = tm // HALO
    nh = seq // HALO
    prev = lambda j, i: (jnp.maximum(i * hb - 1, 0), j)
    nxt = lambda j, i: (jnp.minimum((i + 1) * hb, nh - 1), j)
    return pl.pallas_call(
        body, name="conv_gate_bwd", grid=(D_FF // tn, ni),
        in_specs=[pl.BlockSpec((tm, 2 * tn), lambda j, i: (i, j)), pl.BlockSpec((HALO, 2 * tn), prev),
                  pl.BlockSpec((HALO, 2 * tn), nxt), pl.BlockSpec((tm, tn), lambda j, i: (i, j)), pl.BlockSpec((HALO, tn), nxt),
                  pl.BlockSpec((3, 2 * tn), lambda j, i: (0, j)), pl.BlockSpec((1, 2 * tn), lambda j, i: (0, j))],
        out_specs=[pl.BlockSpec((tm, 2 * tn), lambda j, i: (i, j)), pl.BlockSpec((3, 2 * tn), lambda j, i: (0, j)),
                   pl.BlockSpec((1, 2 * tn), lambda j, i: (0, j))],
        out_shape=[jax.ShapeDtypeStruct((seq, 2 * D_FF), MXU_DTYPE), jax.ShapeDtypeStruct((3, 2 * D_FF), F32),
                   jax.ShapeDtypeStruct((1, 2 * D_FF), F32)],
        compiler_params=_params("parallel", "arbitrary"),
    )(u, u, u, d_act, d_act, conv_w, conv_b)


def _pad_heads(w, width):
    w = jnp.transpose(w, (1, 0, 2))
    return jnp.pad(w, ((0, 0), (0, 0), (0, LANES - width))).astype(MXU_DTYPE)


def _heads_major(a):
    return jnp.transpose(a, (1, 0, 2)).reshape(-1, a.shape[2])


def _heads_minor(a, heads):
    return jnp.transpose(a.reshape(heads, -1, a.shape[1]), (1, 0, 2))


def _unpad_heads(d, width):
    return jnp.transpose(d[:, :, :width], (1, 0, 2))


def _split_pad_rows(w_t):
    z = lambda n: jnp.zeros((n, w_t.shape[1]), w_t.dtype)
    return jnp.concatenate([w_t[:384], z(64), w_t[384:416], z(32), w_t[416:]], axis=0)


def _split_unpad_rows(w_p):
    return jnp.concatenate([w_p[:384], w_p[448:480], w_p[512:]], axis=0)


def _pad_w_o(w_o):
    mla = jnp.pad(w_o[:512].reshape(HEADS, HEAD_DIM, D_MODEL), ((0, 0), (0, LANES - HEAD_DIM), (0, 0)))
    return mla.reshape(HEADS * LANES, D_MODEL).astype(MXU_DTYPE), w_o[512:].astype(MXU_DTYPE)


def _unpad_w_o(d_mla, d_dil):
    return jnp.concatenate([d_mla.reshape(HEADS, LANES, D_MODEL)[:, :HEAD_DIM].reshape(512, D_MODEL), d_dil], axis=0)


def _row(v):
    return v.reshape(1, -1).astype(F32)


def _compute_weights(w):
    w_o_mla, w_o_dil = _pad_w_o(w["w_o"])
    return dict(
        w_in_t=_split_pad_rows(w["w_in"].T).astype(MXU_DTYPE), wq=_pad_heads(w["w_uq"], NOPE_DIM + ROPE_DIM),
        wk=_pad_heads(w["w_uk"], NOPE_DIM), wv=_pad_heads(w["w_uv"], HEAD_DIM), w_o_mla=w_o_mla, w_o_dil=w_o_dil,
        w_up_t=_ffn_interleave(w["w_up"].T, 0).astype(MXU_DTYPE), w_down=w["w_down"].astype(MXU_DTYPE),
        conv_w=_ffn_interleave(w["conv_w"].astype(F32), 1),
        g_cq=_row(w["g_cq"]), g_ckv=_row(w["g_ckv"]), ln1_g=_row(w["ln1_g"]), ln1_b=_row(w["ln1_b"]),
        conv_b=_ffn_interleave(_row(w["conv_b"]), 1), ln2_g=_row(w["ln2_g"]), ln2_b=_row(w["ln2_b"]))


def _natural_grads(g):
    return dict(
        w_in=_split_unpad_rows(g["w_in_t"]).T, g_cq=g["g_cq"].reshape(-1), g_ckv=g["g_ckv"].reshape(-1),
        w_uq=_unpad_heads(g["wq"], NOPE_DIM + ROPE_DIM), w_uk=_unpad_heads(g["wk"], NOPE_DIM),
        w_uv=_unpad_heads(g["wv"], HEAD_DIM), w_o=_unpad_w_o(g["w_o_mla"], g["w_o_dil"]), ln1_g=g["ln1_g"].reshape(-1),
        ln1_b=g["ln1_b"].reshape(-1), w_up=_ffn_deinterleave(g["w_up_t"], 0).T, conv_w=_ffn_deinterleave(g["conv_w"], 1),
        conv_b=_ffn_deinterleave(g["conv_b"], 1).reshape(-1),
        w_down=g["w_down"], ln2_g=g["ln2_g"].reshape(-1), ln2_b=g["ln2_b"].reshape(-1))


def _layer_grads(x0, target, cw, first_after=(), late_weights=None, on_grads=None):
    seq = x0.shape[0]
    ctab, stab = _rope_tables(seq)
    gq, gk = cw["g_cq"], cw["g_ckv"]
    wq, wk, wv = cw["wq"], cw["wk"], cw["wv"]
    x0b = x0.astype(MXU_DTYPE)
    notify = (lambda stage, grads: ()) if on_grads is None else on_grads

    h = _mm(x0b, cw["w_in_t"], name="mm_h", tb=True, tm=1024, tn=1024, tk=1024, after=first_after)
    qf, kf, vp = _mla_prep(h, gq, gk, wq, wk, wv, ctab, stab)
    o_mla, o_mla_b, lse_mla = _mla_attn_fwd(qf, kf, vp)
    o_dil, o_dil_b, lse_dil = _dil_fwd(h)
    if late_weights is not None:
        cw = {**cw, **late_weights(o_mla_b)}
    cb = cw["conv_b"]
    z1, x1, x1b = _mix_ln1(o_mla_b, o_dil_b, cw["w_o_mla"], cw["w_o_dil"], x0, cw["ln1_g"], cw["ln1_b"])
    u = _mm(x1b, cw["w_up_t"], name="mm_up", tb=True, tm=1024, tn=1408, tk=1024)
    act = _conv_gate_fwd(u, cw["conv_w"], cb)
    ffn = _mm(act, cw["w_down"], name="mm_down", tm=1024, tn=1024, tk=2816)
    dz2, dz2b, loss, d_ln2_g, d_ln2_b = _ln2_loss_bwd(x1, ffn, target, cw["ln2_g"], cw["ln2_b"])

    d_act = _mm(dz2b, cw["w_down"], name="mm_d_act", tb=True, out_dtype=MXU_DTYPE, tm=1024, tn=1408, tk=1024)
    d_w_down = _mm(act, dz2b, name="mm_dw_down", ta=True, out_dtype=MXU_DTYPE, tm=1408, tn=1024, tk=1024)
    du, d_conv_w, d_conv_b = _conv_gate_bwd(u, d_act, cw["conv_w"], cb)
    dx1 = _mm(du, cw["w_up_t"], name="mm_dx1", res=dz2, res_scale=DN_ALPHA, tm=512, tn=1024, tk=2 * D_FF)
    d_w_up_t = _mm(du, x1b, name="mm_dw_up", ta=True, out_dtype=MXU_DTYPE, tm=1408, tn=1024, tk=2048)
    grads = dict(w_up_t=d_w_up_t, w_down=d_w_down, conv_w=d_conv_w, conv_b=d_conv_b, ln2_g=d_ln2_g, ln2_b=d_ln2_b)
    dz1, dz1b, d_ln1_g, d_ln1_b = _ln_bwd(dx1, z1, cw["ln1_g"], "ln1_bwd", after=notify("ffn", grads))
    do_mla = _mm_do_mla(dz1b, cw["w_o_mla"])
    do_dil = _mm(dz1b, cw["w_o_dil"], name="mm_do_dil", tb=True, tm=1024, tn=512, tk=1024)
    d_w_o_mla = _mm_dw_o_mla(o_mla_b, dz1b)
    d_w_o_dil = _mm(o_dil_b, dz1b, name="mm_dw_o_dil", ta=True, out_dtype=MXU_DTYPE, tm=512, tn=1024, tk=1024)
    grads.update(w_o_mla=d_w_o_mla, w_o_dil=d_w_o_dil, ln1_g=d_ln1_g, ln1_b=d_ln1_b)
    dq_dil, dk_dil, dv_dil = _dil_bwd(h, o_dil, lse_dil, do_dil)
    dqf, dkf, dvf = _mla_attn_bwd(qf, kf, vp, o_mla, lse_mla, do_mla)
    dh_mla, d_wq, d_wk, d_wv, d_gq, d_gk = _mla_prep_bwd(h, gq, gk, wq, wk, wv, ctab, stab, dqf, dkf, dvf,
                                                          after=notify("w_o", grads))
    dh = _assemble_dh(dh_mla, dq_dil, dk_dil, dv_dil)
    d_w_in_t = _mm(dh, x0b, name="mm_dw_in", ta=True, out_dtype=MXU_DTYPE, tm=1024, tn=1024, tk=1024)
    grads.update(w_in_t=d_w_in_t, wq=d_wq, wk=d_wk, wv=d_wv, g_cq=d_gq, g_ckv=d_gk, loss=loss)
    grad_x = _mm(dh, cw["w_in_t"], name="mm_dx0", res=dz1, res_scale=DN_ALPHA, tm=1024, tn=1024, tk=2048,
                 after=notify("rest", grads))
    return loss, grad_x, grads


def _all_gather(blocks, name):
    na = len(blocks)

    def body(*refs):
        ins, outs = refs[:na], refs[na:2 * na]
        send_sems, recv_sems, local_sems = refs[2 * na:]
        x, y, c = lax.axis_index("x"), lax.axis_index("y"), lax.axis_index("c")
        me, sibling = (x, y, c), (x, y, 1 - c)
        chips = [(1 - x, y), (x, 1 - y), (1 - x, 1 - y)]

        def slot(out, pos):
            return out.at[4 * pos[0] + 2 * pos[1] + pos[2]]

        def copy(a, k, block, to, src=None):
            return pltpu.make_async_remote_copy(
                src_ref=slot(outs[a], block) if src is None else src, dst_ref=slot(outs[a], block),
                send_sem=send_sems.at[7 * a + k], recv_sem=recv_sems.at[7 * a + k],
                device_id=to, device_id_type=pl.DeviceIdType.MESH)

        mine = [pltpu.make_async_copy(ins[a], slot(outs[a], me), local_sems.at[a]) for a in range(na)]
        for cp in mine:
            cp.start()
        first = []
        for a in range(na):
            first.append(copy(a, 0, me, sibling, src=ins[a]))
            first += [copy(a, 1 + j, me, (*chip, c), src=ins[a]) for j, chip in enumerate(chips)]
        for cp in first:
            cp.start()
        passed = []
        for j, chip in enumerate(chips):
            for a in range(na):
                copy(a, 1 + j, (*chip, c), me).wait_recv()
                cp = copy(a, 4 + j, (*chip, c), sibling)
                cp.start()
                passed.append(cp)
        for a in range(na):
            copy(a, 0, sibling, me).wait_recv()
            for j, chip in enumerate(chips):
                copy(a, 4 + j, (*chip, 1 - c), me).wait_recv()
        for cp in first + passed:
            cp.wait_send()
        for cp in mine:
            cp.wait()

    any_spec = pl.BlockSpec(memory_space=pl.ANY)
    return pl.pallas_call(
        body, name=name, in_specs=[any_spec] * na, out_specs=[any_spec] * na,
        out_shape=[jax.ShapeDtypeStruct((N_DEV,) + b.shape, b.dtype) for b in blocks],
        scratch_shapes=[pltpu.SemaphoreType.DMA((7 * na,)), pltpu.SemaphoreType.DMA((7 * na,)), pltpu.SemaphoreType.DMA((na,))],
    )(*blocks)


_HBM_SPEC = pl.BlockSpec(memory_space=pltpu.HBM)
_SEM_SPEC = pl.BlockSpec(memory_space=pltpu.SEMAPHORE)
_DATAFLOW = pltpu.CompilerParams(has_side_effects=pltpu.SideEffectType.DATAFLOW_SIDE_EFFECTING)


def _split_copies(ins, lands, send_sems, recv_sems, gather):
    x, y, c = lax.axis_index("x"), lax.axis_index("y"), lax.axis_index("c")
    me = 4 * x + 2 * y + c
    copies = []
    for a in range(len(ins)):
        for d in range(1, N_DEV):
            px, py, pc = x ^ (d >> 2), y ^ ((d >> 1) & 1), c ^ (d & 1)
            copies.append(pltpu.make_async_remote_copy(
                src_ref=ins[a] if gather else ins[a].at[4 * px + 2 * py + pc], dst_ref=lands[a].at[me],
                send_sem=send_sems.at[7 * a + d - 1], recv_sem=recv_sems.at[7 * a + d - 1],
                device_id=(px, py, pc), device_id_type=pl.DeviceIdType.MESH))
    return copies


def _send_start(srcs, gather, name):
    na = len(srcs)
    land_types = [pltpu.HBM(((N_DEV,) + s.shape) if gather else s.shape, s.dtype) for s in srcs]

    def body(*refs):
        ins, lands = refs[:na], refs[na:2 * na]
        send_sems, recv_sems, token = refs[2 * na], refs[2 * na + 1], refs[-1]
        for cp in _split_copies(ins, lands, send_sems, recv_sems, gather):
            cp.start()
        token[...] = jnp.zeros_like(token)

    hbm = lambda a: pltpu.with_memory_space_constraint(a, pltpu.HBM)
    outs = pl.pallas_call(
        body, name=name,
        out_shape=(pltpu.SemaphoreType.DMA((7 * na,)), pltpu.SemaphoreType.DMA((7 * na,)),
                   *[pltpu.HBM(s.shape, s.dtype) for s in srcs], *land_types, jax.ShapeDtypeStruct((8, LANES), F32)),
        in_specs=[_HBM_SPEC] * (2 * na),
        out_specs=(_SEM_SPEC, _SEM_SPEC, *[_HBM_SPEC] * (2 * na), pl.BlockSpec(memory_space=pltpu.VMEM)),
        input_output_aliases={i: 2 + i for i in range(2 * na)}, compiler_params=_DATAFLOW,
    )(*[hbm(s) for s in srcs], *[hbm(lax.empty(t.shape, t.dtype)) for t in land_types])
    return dict(send=outs[0], recv=outs[1], srcs=list(outs[2:2 + na]), lands=list(outs[2 + na:2 + 2 * na]), token=outs[-1],
                gather=gather)


def _send_wait(handle, after, name):
    na = len(handle["srcs"])
    gather = handle["gather"]
    after = list(after)

    def body(*refs):
        ins, lands = refs[:na], refs[na:2 * na]
        send_sems, recv_sems = refs[2 * na], refs[2 * na + 1]
        for cp in _split_copies(ins, lands, send_sems, recv_sems, gather):
            cp.wait_send()
            cp.wait_recv()

    both = handle["srcs"] + handle["lands"]
    outs = pl.pallas_call(
        body, name=name, out_shape=[pltpu.HBM(a.shape, a.dtype) for a in both],
        in_specs=[_HBM_SPEC] * (2 * na) + [_SEM_SPEC, _SEM_SPEC] + [_ANY_SPEC] * len(after),
        out_specs=[_HBM_SPEC] * (2 * na), input_output_aliases={i: i for i in range(2 * na)}, compiler_params=_DATAFLOW,
    )(*both, handle["send"], handle["recv"], *after)
    return list(outs[:na]), list(outs[na:])


def _sum_slots(p_ref):
    g = p_ref[0].astype(F32)
    for s in range(1, p_ref.shape[0]):
        g = g + p_ref[s].astype(F32)
    return g


def _adamw_refs(g, w_ref, m_ref, v_ref, g_out, d_out, m_out, v_out):
    c1 = 1.0 - ADAM_B1 ** ADAM_STEP
    c2 = 1.0 - ADAM_B2 ** ADAM_STEP
    m_new = ADAM_B1 * m_ref[...] + (1.0 - ADAM_B1) * g
    v_new = ADAM_B2 * v_ref[...] + (1.0 - ADAM_B2) * (g * g)
    g_out[...] = g
    m_out[...] = m_new
    v_out[...] = v_new
    d_out[...] = -ADAM_LR * ((m_new / c1) / (jnp.sqrt(v_new / c2) + ADAM_EPS) + ADAM_WD * w_ref[...])


def _adamw(parts, w, m, v, name):
    npart, r, n = parts.shape
    tr = r if r <= 256 else max(t for t in range(16, 257, 16) if r % t == 0)

    def body(p_ref, w_ref, m_ref, v_ref, g_out, d_out, m_out, v_out):
        _adamw_refs(_sum_slots(p_ref), w_ref, m_ref, v_ref, g_out, d_out, m_out, v_out)

    blk = pl.BlockSpec((tr, n), lambda i: (i, 0))
    shp = jax.ShapeDtypeStruct((r, n), F32)
    return pl.pallas_call(
        body, name=name, grid=(r // tr,), in_specs=[pl.BlockSpec((npart, tr, n), lambda i: (0, i, 0)), blk, blk, blk],
        out_specs=[blk] * 4, out_shape=[shp] * 4, compiler_params=_params("parallel"),
    )(parts, w, m, v)


def _adamw_small(parts, ws, ms, vs, loss_parts, name):
    n = len(parts)

    def body(*refs):
        ins, outs = refs[:4 * n + 1], refs[4 * n + 1:]
        for i in range(n):
            _adamw_refs(_sum_slots(ins[i]), ins[n + i], ins[2 * n + i], ins[3 * n + i], *outs[4 * i:4 * i + 4])
        outs[4 * n][...] = _sum_slots(ins[4 * n])

    out_shape = [jax.ShapeDtypeStruct(w.shape, F32) for w in ws for _ in range(4)]
    res = pl.pallas_call(body, name=name, out_shape=out_shape + [jax.ShapeDtypeStruct((1, LANES), F32)],
                         compiler_params=_params())(*parts, *ws, *ms, *vs, loss_parts)
    return [res[4 * i:4 * i + 4] for i in range(n)], res[4 * n]


REPLICATED = ("g_cq", "g_ckv", "w_uk", "w_uv", "ln1_g", "ln1_b", "conv_b", "ln2_g", "ln2_b")
ALL_WEIGHTS = ("w_in", "g_cq", "g_ckv", "w_uq", "w_uk", "w_uv", "w_o", "ln1_g", "ln1_b", "w_up", "conv_w", "conv_b",
               "w_down", "ln2_g", "ln2_b")


def kernel(x, w_in, g_cq, g_ckv, w_uq, w_uk, w_uv, w_o, ln1_g, ln1_b, w_up, conv_w, conv_b, w_down, ln2_g, ln2_b, loss_target, m_w_in, m_g_cq, m_g_ckv, m_w_uq, m_w_uk, m_w_uv, m_w_o, m_ln1_g, m_ln1_b, m_w_up, m_conv_w, m_conv_b, m_w_down, m_ln2_g, m_ln2_b, v_w_in, v_g_cq, v_g_ckv, v_w_uq, v_w_uk, v_w_uv, v_w_o, v_ln1_g, v_ln1_b, v_w_up, v_conv_w, v_conv_b, v_w_down, v_ln2_g, v_ln2_b):
    w = dict(w_in=w_in, g_cq=g_cq, g_ckv=g_ckv, w_uq=w_uq, w_uk=w_uk, w_uv=w_uv, w_o=w_o, ln1_g=ln1_g, ln1_b=ln1_b,
             w_up=w_up, conv_w=conv_w, conv_b=conv_b, w_down=w_down, ln2_g=ln2_g, ln2_b=ln2_b)
    m = dict(w_in=m_w_in, g_cq=m_g_cq, g_ckv=m_g_ckv, w_uq=m_w_uq, w_uk=m_w_uk, w_uv=m_w_uv, w_o=m_w_o, ln1_g=m_ln1_g,
             ln1_b=m_ln1_b, w_up=m_w_up, conv_w=m_conv_w, conv_b=m_conv_b, w_down=m_w_down, ln2_g=m_ln2_g, ln2_b=m_ln2_b)
    v = dict(w_in=v_w_in, g_cq=v_g_cq, g_ckv=v_g_ckv, w_uq=v_w_uq, w_uk=v_w_uk, w_uv=v_w_uv, w_o=v_w_o, ln1_g=v_ln1_g,
             ln1_b=v_ln1_b, w_up=v_w_up, conv_w=v_conv_w, conv_b=v_conv_b, w_down=v_w_down, ln2_g=v_ln2_g, ln2_b=v_ln2_b)
    me = 4 * lax.axis_index("x") + 2 * lax.axis_index("y") + lax.axis_index("c")
    wire = lambda a: a.astype(WIRE_DTYPE)
    n_in = w_in.shape[1]
    n_in_pad = -(-n_in // 16) * 16
    pad_taps = lambda a: jnp.pad(a, ((0, 8 - a.shape[0]), (0, 0)))

    own_slot = lambda buf, block: lax.dynamic_update_index_in_dim(buf, block, me, 0)
    blocks = lambda a: wire(a).reshape((N_DEV, a.shape[0] // N_DEV) + a.shape[1:])

    g_in, g_uq, g_conv = _all_gather(
        [jnp.pad(wire(w_in).T, ((0, n_in_pad - n_in), (0, 0))), _heads_major(wire(w_uq)), pad_taps(conv_w)],
        "gather_weights")
    late = _send_start([wire(w_o), wire(w_up).T, wire(w_down)], True, "gather_late_start")
    r_uq_dev, e_uq = w_uq.shape[0], w_uq.shape[2]
    wq = jnp.transpose(g_uq.reshape(N_DEV, HEADS, r_uq_dev, e_uq), (1, 0, 2, 3)).reshape(HEADS, Q_RANK, e_uq)
    cw = dict(
        w_in_t=_split_pad_rows(g_in[:, :n_in].reshape(-1, D_MODEL)).astype(MXU_DTYPE),
        wq=jnp.pad(wq, ((0, 0), (0, 0), (0, LANES - e_uq))).astype(MXU_DTYPE),
        wk=_pad_heads(w_uk, NOPE_DIM), wv=_pad_heads(w_uv, HEAD_DIM),
        conv_w=_ffn_interleave(jnp.transpose(g_conv[:, :conv_w.shape[0]], (1, 0, 2)).reshape(conv_w.shape[0], -1), 1),
        g_cq=_row(g_cq), g_ckv=_row(g_ckv), ln1_g=_row(ln1_g), ln1_b=_row(ln1_b), conv_b=_ffn_interleave(_row(conv_b), 1),
        ln2_g=_row(ln2_g), ln2_b=_row(ln2_b))

    def late_weights(after):
        own, landed = _send_wait(late, [after], "gather_late_wait")
        g_o, g_up, g_down = [own_slot(buf, blk) for buf, blk in zip(landed, own)]
        w_o_mla, w_o_dil = _pad_w_o(g_o.reshape(-1, D_MODEL))
        return dict(w_o_mla=w_o_mla, w_o_dil=w_o_dil, w_up_t=_ffn_interleave(g_up.reshape(-1, D_MODEL), 0).astype(MXU_DTYPE),
                    w_down=g_down.reshape(-1, D_MODEL).astype(MXU_DTYPE))

    sent = {}

    def on_grads(stage, g):
        if stage == "ffn":
            sent[stage] = [_send_start([blocks(_ffn_deinterleave(g["w_up_t"], 0)), blocks(g["w_down"])], False, "exchange_ffn_start")]
        elif stage == "w_o":
            sent[stage] = [_send_start([blocks(_unpad_w_o(g["w_o_mla"], g["w_o_dil"]))], False, "exchange_w_o_start")]
        else:
            d_in = jnp.pad(blocks(_split_unpad_rows(g["w_in_t"])), ((0, 0), (0, n_in_pad - n_in), (0, 0)))
            d_uq = wire(jnp.transpose(g["wq"][:, :, :e_uq].reshape(HEADS, N_DEV, r_uq_dev, e_uq), (1, 0, 2, 3))
                        ).reshape(N_DEV, HEADS * r_uq_dev, e_uq)
            dense = lambda a, width: wire(a[:, :, :width]).reshape(-1, LANES)
            small = dict(g_cq=g["g_cq"], g_ckv=g["g_ckv"], w_uk=dense(g["wk"], NOPE_DIM), w_uv=dense(g["wv"], HEAD_DIM),
                         ln1_g=g["ln1_g"], ln1_b=g["ln1_b"], conv_b=_ffn_deinterleave(g["conv_b"], 1), ln2_g=g["ln2_g"],
                         ln2_b=g["ln2_b"])
            sent[stage] = [_send_start([d_in, d_uq], False, "exchange_rest_start"),
                           _send_start([small[n] for n in REPLICATED] + [_ffn_deinterleave(g["conv_w"], 1), g["loss"]],
                                       True, "gather_small_start")]
        return [h["token"] for h in sent[stage]]

    _, grad_x, _ = _layer_grads(x[0], loss_target[0], cw, [late["token"]], late_weights, on_grads)

    def landed(handle, after, name):
        own, got = _send_wait(handle, after, name)
        pick = (lambda a: a) if handle["gather"] else (lambda a: lax.dynamic_index_in_dim(a, me, 0, keepdims=False))
        return [own_slot(buf, pick(src)) for buf, src in zip(got, own)]

    out = {}

    def update(name, parts, view=None):
        to2d = {None: lambda a: a, "t": lambda a: a.T, "heads": _heads_major}[view]
        back = {None: lambda a: a, "t": lambda a: a.T, "heads": lambda a: _heads_minor(a, HEADS)}[view]
        res = _adamw(parts, to2d(w[name]), to2d(m[name]), to2d(v[name]), "adamw_" + name)
        for kind, a in zip(("grad", "delta", "new_m", "new_v"), res):
            out[kind, name] = back(a)
        return res[0]

    r_up, r_down = landed(sent["ffn"][0], [grad_x], "exchange_ffn_wait")
    (r_o,) = landed(sent["w_o"][0], [grad_x], "exchange_w_o_wait")
    done = [update("w_up", r_up, "t"), update("w_down", r_down), update("w_o", r_o)]
    r_in, r_uq = landed(sent["rest"][0], done, "exchange_rest_wait")
    *rep_all, cw_all, loss_all = landed(sent["rest"][1], done, "gather_small_wait")
    update("w_in", r_in[:, :n_in], "t")
    update("w_uq", r_uq, "heads")
    heads_major = ("w_uk", "w_uv")
    two_d = lambda n, a: _heads_major(a) if n in heads_major else a.reshape(1, -1)
    rep_all = [p.reshape(N_DEV, -1, w[n].shape[2]) if n in heads_major else p for n, p in zip(REPLICATED, rep_all)]
    res, loss_sum = _adamw_small(rep_all, *[[two_d(n, d[n]) for n in REPLICATED] for d in (w, m, v)], loss_all, "adamw_replicated")
    for n, quad in zip(REPLICATED, res):
        for kind, a in zip(("grad", "delta", "new_m", "new_v"), quad):
            out[kind, n] = _heads_minor(a, HEADS) if n in heads_major else a.reshape(w[n].shape)
    loss = loss_sum[0, 0]
    ncw = conv_w.shape[1]
    update("conv_w", lax.dynamic_slice_in_dim(cw_all[:, :conv_w.shape[0]], me * ncw, ncw, axis=2))

    return (loss, grad_x[None], *[out[kind, n] for kind in ("grad", "delta", "new_m", "new_v") for n in ALL_WEIGHTS])
```

```python
import functools
import math

import jax
import jax.numpy as jnp
import numpy as np
from jax import lax
from jax.experimental import pallas as pl
from jax.experimental.pallas import tpu as pltpu

F32 = jnp.float32
MXU_DTYPE = jnp.bfloat16
WIRE_DTYPE = jnp.bfloat16

N_DEV = 8
D_MODEL = 1024
HEADS = 8
HEAD_DIM = 64
LANES = 128
Q_RANK, KV_RANK, ROPE_DIM, NOPE_DIM = 256, 128, 32, 64
DIL_WIDTH = HEADS * HEAD_DIM
IN_WIDTH = 1952
IN_PAD = 2048
D_FF = 2816
ROPE_THETA = 10000.0
DIL_PAIRS = ((128, 1), (512, 4), (2048, 16))
DIL_BLOCK = 128
DN_ALPHA = 2.0 ** 0.25
LN_EPS = 1e-5
RMS_EPS = 1e-6
MLA_SCALE = 1.0 / math.sqrt(NOPE_DIM + ROPE_DIM)
MLA_SCALE_LOG2 = MLA_SCALE * math.log2(math.e)
DIL_SCALE = 1.0 / math.sqrt(HEAD_DIM)
ALIBI_SLOPES = tuple(2.0 ** (-8.0 * (h + 1) / HEADS) for h in range(HEADS))
NEG_BIG = -1e30
ADAM_LR, ADAM_B1, ADAM_B2, ADAM_EPS, ADAM_WD, ADAM_STEP = 0.001, 0.9, 0.999, 1e-08, 0.01, 10
VMEM_LIMIT = 48 * 1024 * 1024


def _params(*sem):
    return pltpu.CompilerParams(dimension_semantics=sem or None, vmem_limit_bytes=VMEM_LIMIT)


def _dot(a, b, ca, cb):
    return lax.dot_general(a, b, (((ca,), (cb,)), ((), ())), preferred_element_type=F32)


_ANY_SPEC = pl.BlockSpec(memory_space=pl.ANY)


def _mm(a, b, *, name, tm, tn, tk, ta=False, tb=False, out_dtype=F32, res=None, res_scale=1.0, after=()):
    m, k = (a.shape[1], a.shape[0]) if ta else a.shape
    n = b.shape[0] if tb else b.shape[1]
    assert (b.shape[1] if tb else b.shape[0]) == k
    tm, tn, tk = min(tm, m), min(tn, n), min(tk, k)
    assert m % tm == 0 and n % tn == 0 and k % tk == 0, (name, m, n, k, tm, tn, tk)
    nk = k // tk
    a_spec = (pl.BlockSpec((tk, tm), lambda i, j, kk: (kk, i)) if ta
              else pl.BlockSpec((tm, tk), lambda i, j, kk: (i, kk)))
    b_mode = dict(pipeline_mode=pl.Buffered(1)) if (tn == n and tk == k) else {}
    b_spec = (pl.BlockSpec((tn, tk), lambda i, j, kk: (j, kk), **b_mode) if tb
              else pl.BlockSpec((tk, tn), lambda i, j, kk: (kk, j), **b_mode))
    o_spec = pl.BlockSpec((tm, tn), lambda i, j, kk: (i, j))
    in_specs = [a_spec, b_spec]
    args = [a, b]
    if res is not None:
        in_specs.append(o_spec)
        args.append(res)
    n_in = len(args) + len(after)
    in_specs += [_ANY_SPEC] * len(after)
    args += list(after)
    ca, cb = (0 if ta else 1), (1 if tb else 0)

    def finish(acc, r_ref, o_ref):
        if r_ref is not None:
            acc = acc + res_scale * r_ref[...]
        o_ref[...] = acc.astype(o_ref.dtype)

    def body(*refs):
        a_ref, b_ref = refs[:2]
        r_ref = refs[2] if res is not None else None
        o_ref = refs[n_in]
        part = _dot(a_ref[...].astype(MXU_DTYPE), b_ref[...].astype(MXU_DTYPE), ca, cb)
        if nk == 1:
            finish(part, r_ref, o_ref)
            return
        acc_ref = refs[-1]
        kk = pl.program_id(2)

        @pl.when(kk == 0)
        def _():
            acc_ref[...] = part

        @pl.when(kk > 0)
        def _():
            acc_ref[...] += part

        @pl.when(kk == nk - 1)
        def _():
            finish(acc_ref[...], r_ref, o_ref)

    return pl.pallas_call(
        body, name=name, grid=(m // tm, n // tn, nk), in_specs=in_specs, out_specs=o_spec,
        out_shape=jax.ShapeDtypeStruct((m, n), out_dtype),
        scratch_shapes=[pltpu.VMEM((tm, tn), F32)] if nk > 1 else [],
        compiler_params=_params("parallel", "parallel", "arbitrary"),
    )(*args)


def _mm_do_mla(dz, w_o_mla, tm=1024):
    seq, d = dz.shape
    tm = min(tm, seq)

    def body(a_ref, b_ref, o_ref):
        a = a_ref[...].astype(MXU_DTYPE)
        for hd in range(HEADS):
            o_ref[hd] = _dot(a, b_ref[LANES * hd:LANES * (hd + 1), :], 1, 1)

    return pl.pallas_call(
        body, name="mm_do_mla", grid=(seq // tm,),
        in_specs=[pl.BlockSpec((tm, d), lambda i: (i, 0)), pl.BlockSpec((HEADS * LANES, d), lambda i: (0, 0))],
        out_specs=pl.BlockSpec((HEADS, tm, LANES), lambda i: (0, i, 0)),
        out_shape=jax.ShapeDtypeStruct((HEADS, seq, LANES), F32), compiler_params=_params("parallel"),
    )(dz, w_o_mla)


def _mm_dw_o_mla(o_mla, dz, tk=1024):
    seq, d = dz.shape
    tk = min(tk, seq)
    nk = seq // tk

    def body(a_ref, b_ref, o_ref, acc_ref):
        kk = pl.program_id(0)

        @pl.when(kk == 0)
        def _():
            acc_ref[...] = jnp.zeros_like(acc_ref)

        b = b_ref[...].astype(MXU_DTYPE)
        for hd in range(HEADS):
            acc_ref[LANES * hd:LANES * (hd + 1), :] += _dot(a_ref[hd].astype(MXU_DTYPE), b, 0, 0)

        @pl.when(kk == nk - 1)
        def _():
            o_ref[...] = acc_ref[...].astype(o_ref.dtype)

    return pl.pallas_call(
        body, name="mm_dw_o_mla", grid=(nk,),
        in_specs=[pl.BlockSpec((HEADS, tk, LANES), lambda kk: (0, kk, 0)), pl.BlockSpec((tk, d), lambda kk: (kk, 0))],
        out_specs=pl.BlockSpec((HEADS * LANES, d), lambda kk: (0, 0)),
        out_shape=jax.ShapeDtypeStruct((HEADS * LANES, d), MXU_DTYPE),
        scratch_shapes=[pltpu.VMEM((HEADS * LANES, d), F32)], compiler_params=_params("arbitrary"),
    )(o_mla, dz)


def _rope_tables(seq):
    half = ROPE_DIM // 2
    f32 = np.float32
    freqs = np.power(f32(ROPE_THETA), -np.arange(half, dtype=f32) / f32(half))
    ang = np.arange(seq, dtype=f32)[:, None] * freqs[None, :]
    cos, sin = np.cos(ang, dtype=f32), np.sin(ang, dtype=f32)
    one = np.ones((seq, NOPE_DIM), f32)
    tail = np.ones((seq, LANES - NOPE_DIM - ROPE_DIM), f32)
    ctab = np.concatenate([one, cos, cos, tail], axis=1)
    stab = np.concatenate([0 * one, -sin, sin, 0 * tail], axis=1)
    return jnp.asarray(ctab), jnp.asarray(stab)


def _rope_swap(t):
    lane = lax.broadcasted_iota(jnp.int32, t.shape, 1)
    half = ROPE_DIM // 2
    return jnp.where(lane < NOPE_DIM + half, pltpu.roll(t, LANES - half, 1), pltpu.roll(t, half, 1))


def _rope(t, ctab, stab):
    return t * ctab + _rope_swap(t) * stab


def _rope_inv(t, ctab, stab):
    return t * ctab - _rope_swap(t) * stab


def _rms(x, g):
    r = lax.rsqrt(jnp.mean(x * x, axis=-1, keepdims=True) + RMS_EPS)
    xh = x * r
    return xh, r, xh * g


def _mla_prep(h, g_cq, g_ckv, wq, wk, wv, ctab, stab, tm=512):
    seq = h.shape[0]
    tm = min(tm, seq)

    def body(h_ref, gq_ref, gk_ref, wq_ref, wk_ref, wv_ref, c_ref, s_ref, q_out, k_out, v_out):
        hb = h_ref[...]
        ctab_, stab_ = c_ref[...], s_ref[...]
        _, _, cqn = _rms(hb[:, :Q_RANK], gq_ref[...])
        _, _, ckn = _rms(hb[:, Q_RANK:Q_RANK + KV_RANK], gk_ref[...])
        cqn = cqn.astype(MXU_DTYPE)
        ckn = ckn.astype(MXU_DTYPE)
        krr = _rope(hb[:, Q_RANK + KV_RANK:], ctab_, stab_)
        for hd in range(HEADS):
            q = _dot(cqn, wq_ref[hd], 1, 0)
            q_out[hd] = _rope(q, ctab_, stab_).astype(q_out.dtype)
            k_out[hd] = (_dot(ckn, wk_ref[hd], 1, 0) + krr).astype(k_out.dtype)
            v_out[hd] = _dot(ckn, wv_ref[hd], 1, 0).astype(v_out.dtype)

    full = lambda *shape: pl.BlockSpec(shape, lambda i: (0,) * len(shape))
    slab = pl.BlockSpec((HEADS, tm, LANES), lambda i: (0, i, 0))
    shp = jax.ShapeDtypeStruct((HEADS, seq, LANES), MXU_DTYPE)
    return pl.pallas_call(
        body, name="mla_prep", grid=(seq // tm,),
        in_specs=[pl.BlockSpec((tm, 512), lambda i: (i, 0)), full(1, Q_RANK), full(1, KV_RANK),
                  full(HEADS, Q_RANK, LANES), full(HEADS, KV_RANK, LANES), full(HEADS, KV_RANK, LANES),
                  pl.BlockSpec((tm, LANES), lambda i: (i, 0)), pl.BlockSpec((tm, LANES), lambda i: (i, 0))],
        out_specs=[slab, slab, slab], out_shape=[shp, shp, shp],
        compiler_params=_params("parallel"),
    )(h, g_cq, g_ckv, wq, wk, wv, ctab, stab)


def _mla_prep_bwd(h, g_cq, g_ckv, wq, wk, wv, ctab, stab, dq, dk, dv, tm=512, after=()):
    seq = h.shape[0]
    tm = min(tm, seq)
    n_after = len(after)

    def body(h_ref, gq_ref, gk_ref, wq_ref, wk_ref, wv_ref, c_ref, s_ref, dq_ref, dk_ref, dv_ref, *rest):
        dh_ref, dwq_ref, dwk_ref, dwv_ref, dgq_ref, dgk_ref = rest[n_after:]

        @pl.when(pl.program_id(0) == 0)
        def _():
            for r in (dwq_ref, dwk_ref, dwv_ref, dgq_ref, dgk_ref):
                r[...] = jnp.zeros_like(r)

        hb = h_ref[...]
        ctab_, stab_ = c_ref[...], s_ref[...]
        gq, gk = gq_ref[...], gk_ref[...]
        xq, rq, cqn = _rms(hb[:, :Q_RANK], gq)
        xk, rk, ckn = _rms(hb[:, Q_RANK:Q_RANK + KV_RANK], gk)
        cqn = cqn.astype(MXU_DTYPE)
        ckn = ckn.astype(MXU_DTYPE)
        d_cqn = jnp.zeros((tm, Q_RANK), F32)
        d_ckn = jnp.zeros((tm, KV_RANK), F32)
        d_krr = jnp.zeros((tm, LANES), F32)
        for hd in range(HEADS):
            dqh = _rope_inv(dq_ref[hd], ctab_, stab_).astype(MXU_DTYPE)
            d_cqn += _dot(dqh, wq_ref[hd], 1, 1)
            dwq_ref[hd] += _dot(cqn, dqh, 0, 0)
            dkh = dk_ref[hd]
            d_krr += dkh
            dkh = dkh.astype(MXU_DTYPE)
            d_ckn += _dot(dkh, wk_ref[hd], 1, 1)
            dwk_ref[hd] += _dot(ckn, dkh, 0, 0)
            dvh = dv_ref[hd].astype(MXU_DTYPE)
            d_ckn += _dot(dvh, wv_ref[hd], 1, 1)
            dwv_ref[hd] += _dot(ckn, dvh, 0, 0)
        lane = lax.broadcasted_iota(jnp.int32, (tm, LANES), 1)
        rot = (lane >= NOPE_DIM) & (lane < NOPE_DIM + ROPE_DIM)
        d_kr = jnp.where(rot, _rope_inv(jnp.where(rot, d_krr, 0.0), ctab_, stab_), 0.0)

        def rms_bwd(dy, xh, r, g, dg_ref):
            dg_ref[...] += jnp.sum(dy * xh, axis=0, keepdims=True)
            dxh = dy * g
            return r * (dxh - xh * jnp.mean(dxh * xh, axis=-1, keepdims=True))

        d_cq = rms_bwd(d_cqn, xq, rq, gq, dgq_ref)
        d_ck = rms_bwd(d_ckn, xk, rk, gk, dgk_ref)
        dh_ref[...] = jnp.concatenate([d_cq, d_ck, d_kr], axis=1).astype(dh_ref.dtype)

    full = lambda *shape: pl.BlockSpec(shape, lambda i: (0,) * len(shape))
    slab = pl.BlockSpec((HEADS, tm, LANES), lambda i: (0, i, 0))
    return pl.pallas_call(
        body, name="mla_prep_bwd", grid=(seq // tm,),
        in_specs=[pl.BlockSpec((tm, 512), lambda i: (i, 0)), full(1, Q_RANK), full(1, KV_RANK),
                  full(HEADS, Q_RANK, LANES), full(HEADS, KV_RANK, LANES), full(HEADS, KV_RANK, LANES),
                  pl.BlockSpec((tm, LANES), lambda i: (i, 0)), pl.BlockSpec((tm, LANES), lambda i: (i, 0)),
                  slab, slab, slab] + [_ANY_SPEC] * n_after,
        out_specs=[pl.BlockSpec((tm, 512), lambda i: (i, 0)), full(HEADS, Q_RANK, LANES), full(HEADS, KV_RANK, LANES),
                   full(HEADS, KV_RANK, LANES), full(1, Q_RANK), full(1, KV_RANK)],
        out_shape=[jax.ShapeDtypeStruct((seq, 512), MXU_DTYPE), jax.ShapeDtypeStruct((HEADS, Q_RANK, LANES), F32),
                   jax.ShapeDtypeStruct((HEADS, KV_RANK, LANES), F32), jax.ShapeDtypeStruct((HEADS, KV_RANK, LANES), F32),
                   jax.ShapeDtypeStruct((1, Q_RANK), F32), jax.ShapeDtypeStruct((1, KV_RANK), F32)],
        compiler_params=_params("arbitrary"),
    )(h, g_cq, g_ckv, wq, wk, wv, ctab, stab, dq, dk, dv, *after)


def _causal_mask(t):
    row = lax.broadcasted_iota(jnp.int32, (t, t), 0)
    col = lax.broadcasted_iota(jnp.int32, (t, t), 1)
    return row >= col


def _mla_attn_fwd(q, k, v, t=512):
    _, seq, _ = q.shape
    t = min(t, seq)

    def body(q_ref, k_ref, v_ref, o_ref, ob_ref, lse_ref, m_ref, l_ref, acc_ref, s_ref):
        i = pl.program_id(1)
        qb = q_ref[...]
        m_ref[...] = jnp.full_like(m_ref, NEG_BIG)
        l_ref[...] = jnp.zeros_like(l_ref)
        acc_ref[...] = jnp.zeros_like(acc_ref)

        def scores(j):
            return _dot(qb, k_ref[pl.ds(pl.multiple_of(j * t, t), t), :], 1, 1) * MLA_SCALE_LOG2

        def softmax_pv(j, s, masked):
            vb = v_ref[pl.ds(pl.multiple_of(j * t, t), t), :]
            if masked:
                s = jnp.where(_causal_mask(t), s, NEG_BIG)
            m_old = m_ref[...]
            m_new = jnp.maximum(m_old, jnp.max(s, axis=1, keepdims=True))
            p = jnp.exp2(s - m_new)
            a = jnp.exp2(m_old - m_new)
            l_ref[...] = a * l_ref[...] + jnp.sum(p, axis=1, keepdims=True)
            acc_ref[...] = a * acc_ref[...] + _dot(p.astype(MXU_DTYPE), vb, 1, 0)
            m_ref[...] = m_new

        s_ref[...] = scores(0)

        def loop_body(j, c):
            s_next = scores(j + 1)
            softmax_pv(j, s_ref[...], False)
            s_ref[...] = s_next
            return c

        lax.fori_loop(0, i, loop_body, 0)
        softmax_pv(i, s_ref[...], True)
        l = l_ref[...]
        o = acc_ref[...] * (1.0 / l)
        o_ref[...] = o
        ob_ref[...] = o.astype(ob_ref.dtype)
        lse_ref[...] = jnp.broadcast_to(m_ref[...] + jnp.log2(l), lse_ref.shape)

    blk = pl.BlockSpec((None, t, LANES), lambda h, i: (h, i, 0))
    whole = pl.BlockSpec((None, seq, LANES), lambda h, i: (h, 0, 0))
    shp = jax.ShapeDtypeStruct((HEADS, seq, LANES), F32)
    return pl.pallas_call(
        body, name="mla_attn_fwd", grid=(HEADS, seq // t),
        in_specs=[blk, whole, whole], out_specs=[blk, blk, blk],
        out_shape=[shp, jax.ShapeDtypeStruct((HEADS, seq, LANES), MXU_DTYPE), shp],
        scratch_shapes=[pltpu.VMEM((t, 1), F32), pltpu.VMEM((t, 1), F32), pltpu.VMEM((t, LANES), F32), pltpu.VMEM((t, t), F32)],
        compiler_params=_params("parallel", "arbitrary"),
    )(q, k, v)


def _mla_attn_bwd(q, k, v, o, lse, do, t=512):
    _, seq, _ = q.shape
    t = min(t, seq)
    nb = seq // t

    def body(q_ref, k_ref, v_ref, o_ref, lse_ref, do_ref, dq_ref, dk_ref, dv_ref, dl_ref, dka_ref, dva_ref):
        dq_ref[...] = jnp.zeros_like(dq_ref)

        def delta_body(i, c):
            rows = pl.ds(pl.multiple_of(i * t, t), t)
            dl_ref[rows, :] = jnp.sum(do_ref[rows, :] * o_ref[rows, :], axis=1, keepdims=True)
            return c

        lax.fori_loop(0, nb, delta_body, 0)

        def kblock(j, c):
            krows = pl.ds(pl.multiple_of(j * t, t), t)
            kb = k_ref[krows, :]
            vb = v_ref[krows, :]
            dka_ref[...] = jnp.zeros_like(dka_ref)
            dva_ref[...] = jnp.zeros_like(dva_ref)

            def qstep(i, masked):
                th = t // 2
                rows = [pl.ds(pl.multiple_of(i * t + hf * th, th), th) for hf in range(2)]
                qs = [q_ref[r, :] for r in rows]
                dos = [do_ref[r, :].astype(MXU_DTYPE) for r in rows]
                ss = [_dot(qb, kb, 1, 1) * MLA_SCALE_LOG2 for qb in qs]
                dps = [_dot(dob, vb, 1, 1) for dob in dos]
                for hf in range(2):
                    s = ss[hf]
                    if masked:
                        row = lax.broadcasted_iota(jnp.int32, (th, t), 0) + hf * th
                        s = jnp.where(row >= lax.broadcasted_iota(jnp.int32, (th, t), 1), s, NEG_BIG)
                    p = jnp.exp2(s - lse_ref[rows[hf], 0:1])
                    dva_ref[...] += _dot(p.astype(MXU_DTYPE), dos[hf], 0, 0)
                    ds = (p * (dps[hf] - dl_ref[rows[hf], :]) * MLA_SCALE).astype(MXU_DTYPE)
                    dka_ref[...] += _dot(ds, qs[hf], 0, 0)
                    dq_ref[rows[hf], :] += _dot(ds, kb, 1, 0)

            qstep(j, True)

            def qloop(i, c2):
                qstep(i, False)
                return c2

            lax.fori_loop(j + 1, nb, qloop, 0)
            dk_ref[krows, :] = dka_ref[...]
            dv_ref[krows, :] = dva_ref[...]
            return c

        lax.fori_loop(0, nb, kblock, 0)

    whole = pl.BlockSpec((None, seq, LANES), lambda h: (h, 0, 0))
    shp = jax.ShapeDtypeStruct((HEADS, seq, LANES), F32)
    return pl.pallas_call(
        body, name="mla_attn_bwd", grid=(HEADS,),
        in_specs=[whole] * 6, out_specs=[whole] * 3, out_shape=[shp] * 3,
        scratch_shapes=[pltpu.VMEM((seq, 1), F32), pltpu.VMEM((t, LANES), F32), pltpu.VMEM((t, LANES), F32)],
        compiler_params=_params("parallel"),
    )(q, k, v, o, lse, do)


DIL_CHUNK = DIL_BLOCK * max(d for _, d in DIL_PAIRS)
DIL_PAIR_LANES = 2 * HEAD_DIM
assert DIL_PAIR_LANES == LANES
DIL_UNROLL_FWD = 4
DIL_UNROLL_BWD = 4


def _dil_bias_tables(hp, dil):
    b = DIL_BLOCK
    iq = lax.broadcasted_iota(jnp.int32, (b, 2 * b), 0)
    ik = lax.broadcasted_iota(jnp.int32, (b, 2 * b), 1)
    off = iq + b - ik
    band = (off >= 0) & (off <= b)
    dist = (off * dil).astype(F32)
    every, first = [], []
    for hh in range(2):
        slope = jnp.where(hp == 0, ALIBI_SLOPES[hh], jnp.where(hp == 1, ALIBI_SLOPES[2 + hh],
                          jnp.where(hp == 2, ALIBI_SLOPES[4 + hh], ALIBI_SLOPES[6 + hh]))).astype(F32)
        bias = -slope * dist
        every.append(jnp.where(band, bias, NEG_BIG))
        first.append(jnp.where(band & (ik >= b), bias, NEG_BIG))
    return jnp.concatenate(every, axis=0), jnp.concatenate(first, axis=0)


def _dil_rows(start, dil):
    return pl.ds(start, DIL_BLOCK) if dil == 1 else pl.ds(start, DIL_BLOCK, stride=dil)


def _dil_block_pos(blk, c, dil):
    sc, r = blk // dil, blk % dil
    q0 = sc * (DIL_BLOCK * dil) + r
    kcur0 = c * DIL_CHUNK + q0
    first = kcur0 < DIL_BLOCK * dil
    kprev0 = jnp.where(first, kcur0, kcur0 - DIL_BLOCK * dil)
    return q0, kcur0, kprev0, first


def _pair_cols(hh):
    return slice(HEAD_DIM * hh, HEAD_DIM * (hh + 1))


def _first_head_lanes(shape):
    return lax.broadcasted_iota(jnp.int32, shape, 1) < HEAD_DIM


def _stack_pair(t):
    first = _first_head_lanes(t.shape)
    return jnp.concatenate([jnp.where(first, t, 0.0), jnp.where(first, 0.0, t)], axis=0).astype(MXU_DTYPE)


def _unstack_pair(t):
    rows = t.shape[0] // 2
    return jnp.where(_first_head_lanes((rows, t.shape[1])), t[:rows], t[rows:])


def _pair_column(t):
    return jnp.concatenate([t[:, 0:1], t[:, HEAD_DIM:HEAD_DIM + 1]], axis=0)


def _dil_fwd(h):
    seq = h.shape[0]
    assert seq % DIL_CHUNK == 0
    nblk = DIL_CHUNK // DIL_BLOCK
    rc = 256

    def body(q_ref, k_ref, v_ref, o_ref, ob_ref, lse_ref, *scr):
        o_scr, l_scr = scr[:3], scr[3:]
        hp, c = pl.program_id(0), pl.program_id(1)
        for bi, (_, dil) in enumerate(DIL_PAIRS):
            tables = _dil_bias_tables(hp, dil)

            def block(blk, carry, bi=bi, dil=dil, tables=tables):
                q0, kcur0, kprev0, first = _dil_block_pos(blk, c, dil)
                q2 = _stack_pair(q_ref[_dil_rows(q0, dil), :] * DIL_SCALE)
                kcat = jnp.concatenate([k_ref[_dil_rows(kprev0, dil), :], k_ref[_dil_rows(kcur0, dil), :]], axis=0).astype(MXU_DTYPE)
                vcat = jnp.concatenate([v_ref[_dil_rows(kprev0, dil), :], v_ref[_dil_rows(kcur0, dil), :]], axis=0).astype(MXU_DTYPE)
                s = _dot(q2, kcat, 1, 1) + jnp.where(first, tables[1], tables[0])
                mx = jnp.max(s, axis=1, keepdims=True)
                p = jnp.exp(s - mx)
                l = jnp.sum(p, axis=1, keepdims=True)
                o_scr[bi][_dil_rows(q0, dil), :] = _unstack_pair(_dot(p.astype(MXU_DTYPE), vcat, 1, 0) * (1.0 / l))
                l_scr[bi][_dil_rows(q0, dil), :] = _unstack_pair(jnp.broadcast_to(mx + jnp.log(l), (2 * DIL_BLOCK, LANES)))
                return carry

            lax.fori_loop(0, nblk, block, 0, unroll=DIL_UNROLL_FWD)

        def combine(i, carry):
            rows = pl.ds(pl.multiple_of(i * rc, rc), rc)
            ls = [l_scr[bi][rows, :] for bi in range(3)]
            mx = jnp.maximum(jnp.maximum(ls[0], ls[1]), ls[2])
            es = [jnp.exp(l - mx) for l in ls]
            den = es[0] + es[1] + es[2]
            o = (es[0] * o_scr[0][rows, :] + es[1] * o_scr[1][rows, :] + es[2] * o_scr[2][rows, :]) / den
            o_ref[rows, :] = o
            ob_ref[rows, :] = o.astype(ob_ref.dtype)
            lse_ref[rows, :] = mx + jnp.log(den)
            return carry

        lax.fori_loop(0, DIL_CHUNK // rc, combine, 0)

    nq = DIL_WIDTH // LANES
    chunk = lambda off: pl.BlockSpec((DIL_CHUNK, LANES), lambda hp, c: (c, off + hp))
    whole = lambda off: pl.BlockSpec((seq, LANES), lambda hp, c: (0, off + hp))
    shp = jax.ShapeDtypeStruct((seq, DIL_WIDTH), F32)
    return pl.pallas_call(
        body, name="dil_fwd", grid=(nq, seq // DIL_CHUNK),
        in_specs=[chunk(nq), whole(2 * nq), whole(3 * nq)], out_specs=[chunk(0), chunk(0), chunk(0)],
        out_shape=[shp, jax.ShapeDtypeStruct((seq, DIL_WIDTH), MXU_DTYPE), shp],
        scratch_shapes=[pltpu.VMEM((DIL_CHUNK, LANES), F32)] * 6,
        compiler_params=_params("parallel", "arbitrary"),
    )(h, h, h)


def _dil_bwd(h, o, lse, do):
    seq = h.shape[0]
    nblk = DIL_CHUNK // DIL_BLOCK
    nchunk = seq // DIL_CHUNK
    rc = 256

    def body(q_ref, k_ref, v_ref, o_ref, lse_ref, do_ref, dq_out, dk_out, dv_out, dl_scr, dq_ref, dk_ref, dv_ref):
        hp, c = pl.program_id(0), pl.program_id(1)

        @pl.when(c == 0)
        def _():
            dk_ref[...] = jnp.zeros_like(dk_ref)
            dv_ref[...] = jnp.zeros_like(dv_ref)

        def delta(i, carry):
            rows = pl.ds(pl.multiple_of(i * rc, rc), rc)
            prod = do_ref[rows, :] * o_ref[rows, :]
            dl_scr[rows, :] = jnp.concatenate(
                [jnp.broadcast_to(jnp.sum(prod[:, _pair_cols(hh)], axis=1, keepdims=True), (rc, HEAD_DIM)) for hh in range(2)], axis=1)
            return carry

        lax.fori_loop(0, DIL_CHUNK // rc, delta, 0)

        for bi, (_, dil) in enumerate(DIL_PAIRS):
            tables = _dil_bias_tables(hp, dil)

            def block(blk, carry, bi=bi, dil=dil, tables=tables):
                q0, kcur0, kprev0, first = _dil_block_pos(blk, c, dil)
                qrows = _dil_rows(q0, dil)
                q2 = _stack_pair(q_ref[qrows, :] * DIL_SCALE)
                kcat = jnp.concatenate([k_ref[_dil_rows(kprev0, dil), :], k_ref[_dil_rows(kcur0, dil), :]], axis=0).astype(MXU_DTYPE)
                vcat = jnp.concatenate([v_ref[_dil_rows(kprev0, dil), :], v_ref[_dil_rows(kcur0, dil), :]], axis=0).astype(MXU_DTYPE)
                do2 = _stack_pair(do_ref[qrows, :])
                s = _dot(q2, kcat, 1, 1) + jnp.where(first, tables[1], tables[0])
                p = jnp.exp(s - _pair_column(lse_ref[qrows, :]))
                dp = _dot(do2, vcat, 1, 1)
                ds = (p * (dp - _pair_column(dl_scr[qrows, :]))).astype(MXU_DTYPE)
                dq_b = _unstack_pair(_dot(ds, kcat, 1, 0)) * DIL_SCALE
                dk_b = _dot(ds, q2, 0, 0)
                dv_b = _dot(p.astype(MXU_DTYPE), do2, 0, 0)
                if bi == 0:
                    dq_ref[qrows, :] = dq_b
                else:
                    dq_ref[qrows, :] += dq_b
                dk_ref[_dil_rows(kprev0, dil), :] += dk_b[:DIL_BLOCK]
                dv_ref[_dil_rows(kprev0, dil), :] += dv_b[:DIL_BLOCK]
                dk_ref[_dil_rows(kcur0, dil), :] += dk_b[DIL_BLOCK:]
                dv_ref[_dil_rows(kcur0, dil), :] += dv_b[DIL_BLOCK:]
                return carry

            lax.fori_loop(0, nblk, block, 0, unroll=DIL_UNROLL_BWD)

        dq_out[...] = dq_ref[...].astype(dq_out.dtype)

        @pl.when(c == nchunk - 1)
        def _():
            dk_out[...] = dk_ref[...].astype(dk_out.dtype)
            dv_out[...] = dv_ref[...].astype(dv_out.dtype)

    nq = DIL_WIDTH // LANES
    chunk = lambda off: pl.BlockSpec((DIL_CHUNK, LANES), lambda hp, c: (c, off + hp))
    whole = lambda off: pl.BlockSpec((seq, LANES), lambda hp, c: (0, off + hp))
    shp = jax.ShapeDtypeStruct((seq, DIL_WIDTH), MXU_DTYPE)
    return pl.pallas_call(
        body, name="dil_bwd", grid=(nq, nchunk),
        in_specs=[chunk(nq), whole(2 * nq), whole(3 * nq), chunk(0), chunk(0), chunk(0)],
        out_specs=[chunk(0), whole(0), whole(0)], out_shape=[shp, shp, shp],
        scratch_shapes=[pltpu.VMEM((DIL_CHUNK, LANES), F32), pltpu.VMEM((DIL_CHUNK, LANES), F32),
                        pltpu.VMEM((seq, LANES), F32), pltpu.VMEM((seq, LANES), F32)],
        compiler_params=_params("parallel", "arbitrary"),
    )(h, h, h, o, lse, do)


DH_PART = IN_PAD // 4


def _mm_dx0(parts, w_in_t, res, tm=1024, after=()):
    seq, d = res.shape
    tm = min(tm, seq)
    n_after = len(after)

    def body(a0, a1, a2, a3, b_ref, r_ref, *rest):
        o_ref = rest[n_after]
        acc = _dot(a0[...], b_ref[0:DH_PART, :], 1, 0)
        for c, a in enumerate((a1, a2, a3), start=1):
            acc += _dot(a[...], b_ref[DH_PART * c:DH_PART * (c + 1), :], 1, 0)
        o_ref[...] = acc + DN_ALPHA * r_ref[...]

    blk = pl.BlockSpec((tm, DH_PART), lambda i: (i, 0))
    row = pl.BlockSpec((tm, d), lambda i: (i, 0))
    return pl.pallas_call(
        body, name="mm_dx0", grid=(seq // tm,),
        in_specs=[blk] * 4 + [pl.BlockSpec((IN_PAD, d), lambda i: (0, 0), pipeline_mode=pl.Buffered(1)), row] + [_ANY_SPEC] * n_after,
        out_specs=row, out_shape=jax.ShapeDtypeStruct((seq, d), F32), compiler_params=_params("parallel"),
    )(*parts, w_in_t, res, *after)


def _mm_dw_in(parts, x0b, tk=1024):
    seq, d = x0b.shape
    tk = min(tk, seq)
    nk = seq // tk

    def body(a0, a1, a2, a3, b_ref, o_ref, acc_ref):
        kk = pl.program_id(0)

        @pl.when(kk == 0)
        def _():
            acc_ref[...] = jnp.zeros_like(acc_ref)

        b = b_ref[...]
        for c, a in enumerate((a0, a1, a2, a3)):
            acc_ref[DH_PART * c:DH_PART * (c + 1), :] += _dot(a[...], b, 0, 0)

        @pl.when(kk == nk - 1)
        def _():
            o_ref[...] = acc_ref[...].astype(o_ref.dtype)

    blk = pl.BlockSpec((tk, DH_PART), lambda kk: (kk, 0))
    return pl.pallas_call(
        body, name="mm_dw_in", grid=(nk,), in_specs=[blk] * 4 + [pl.BlockSpec((tk, d), lambda kk: (kk, 0))],
        out_specs=pl.BlockSpec((IN_PAD, d), lambda kk: (0, 0)), out_shape=jax.ShapeDtypeStruct((IN_PAD, d), MXU_DTYPE),
        scratch_shapes=[pltpu.VMEM((IN_PAD, d), F32)], compiler_params=_params("arbitrary"),
    )(*parts, x0b)


def _ln_stats(z):
    mu = jnp.mean(z, axis=-1, keepdims=True)
    zc = z - mu
    r = lax.rsqrt(jnp.mean(zc * zc, axis=-1, keepdims=True) + LN_EPS)
    return zc * r, r


def _ln_bwd_math(dy, xh, r, g):
    dxh = dy * g
    return r * (dxh - jnp.mean(dxh, axis=-1, keepdims=True) - xh * jnp.mean(dxh * xh, axis=-1, keepdims=True))


def _mix_ln1(o_mla, o_dil, w_o_mla, w_o_dil, x0, g, b, tm=512):
    seq, d = x0.shape
    tm = min(tm, seq)

    def body(om_ref, od_ref, wm_ref, wd_ref, x_ref, g_ref, b_ref, z_ref, y_ref, yb_ref):
        mix = _dot(od_ref[...], wd_ref[...], 1, 0)
        for hd in range(HEADS):
            mix += _dot(om_ref[hd], wm_ref[LANES * hd:LANES * (hd + 1), :], 1, 0)
        z = DN_ALPHA * x_ref[...] + mix
        xh, _ = _ln_stats(z)
        y = xh * g_ref[...] + b_ref[...]
        z_ref[...] = z
        y_ref[...] = y
        yb_ref[...] = y.astype(yb_ref.dtype)

    blk = pl.BlockSpec((tm, d), lambda i: (i, 0))
    vec = pl.BlockSpec((1, d), lambda i: (0, 0))
    shp = jax.ShapeDtypeStruct((seq, d), F32)
    return pl.pallas_call(
        body, name="mix_ln1", grid=(seq // tm,),
        in_specs=[pl.BlockSpec((HEADS, tm, LANES), lambda i: (0, i, 0)), pl.BlockSpec((tm, DIL_WIDTH), lambda i: (i, 0)),
                  pl.BlockSpec((HEADS * LANES, d), lambda i: (0, 0)), pl.BlockSpec((DIL_WIDTH, d), lambda i: (0, 0)), blk, vec, vec],
        out_specs=[blk, blk, blk], out_shape=[shp, shp, jax.ShapeDtypeStruct((seq, d), MXU_DTYPE)],
        compiler_params=_params("parallel"))(o_mla, o_dil, w_o_mla, w_o_dil, x0, g, b)


def _ln_bwd(dy, z, g, name, tm=512, after=()):
    seq, d = z.shape
    tm = min(tm, seq)
    n_after = len(after)

    def body(dy_ref, z_ref, g_ref, *rest):
        dz_ref, dzb_ref, dg_ref, db_ref = rest[n_after:]

        @pl.when(pl.program_id(0) == 0)
        def _():
            dg_ref[...] = jnp.zeros_like(dg_ref)
            db_ref[...] = jnp.zeros_like(db_ref)

        dyb = dy_ref[...]
        xh, r = _ln_stats(z_ref[...])
        dg_ref[...] += jnp.sum(dyb * xh, axis=0, keepdims=True)
        db_ref[...] += jnp.sum(dyb, axis=0, keepdims=True)
        dz = _ln_bwd_math(dyb, xh, r, g_ref[...])
        dz_ref[...] = dz
        dzb_ref[...] = dz.astype(dzb_ref.dtype)

    blk = pl.BlockSpec((tm, d), lambda i: (i, 0))
    vec = pl.BlockSpec((1, d), lambda i: (0, 0))
    return pl.pallas_call(
        body, name=name, grid=(seq // tm,), in_specs=[blk, blk, vec] + [_ANY_SPEC] * n_after, out_specs=[blk, blk, vec, vec],
        out_shape=[jax.ShapeDtypeStruct((seq, d), F32), jax.ShapeDtypeStruct((seq, d), MXU_DTYPE),
                   jax.ShapeDtypeStruct((1, d), F32), jax.ShapeDtypeStruct((1, d), F32)],
        compiler_params=_params("arbitrary"))(dy, z, g, *after)


def _ln2_loss_bwd(x1, ffn, target, g, b, tm=512):
    seq, d = x1.shape
    tm = min(tm, seq)

    def body(x_ref, f_ref, t_ref, g_ref, b_ref, dz_ref, dzb_ref, loss_ref, dg_ref, db_ref):
        @pl.when(pl.program_id(0) == 0)
        def _():
            loss_ref[...] = jnp.zeros_like(loss_ref)
            dg_ref[...] = jnp.zeros_like(dg_ref)
            db_ref[...] = jnp.zeros_like(db_ref)

        gv = g_ref[...]
        z = DN_ALPHA * x_ref[...] + f_ref[...]
        xh, r = _ln_stats(z)
        err = (xh * gv + b_ref[...]) - t_ref[...]
        loss_ref[...] += 0.5 * jnp.sum(jnp.mean(err * err, axis=-1, keepdims=True), axis=0, keepdims=True)
        dy = err * (1.0 / d)
        dg_ref[...] += jnp.sum(dy * xh, axis=0, keepdims=True)
        db_ref[...] += jnp.sum(dy, axis=0, keepdims=True)
        dz = _ln_bwd_math(dy, xh, r, gv)
        dz_ref[...] = dz
        dzb_ref[...] = dz.astype(dzb_ref.dtype)

    blk = pl.BlockSpec((tm, d), lambda i: (i, 0))
    vec = pl.BlockSpec((1, d), lambda i: (0, 0))
    return pl.pallas_call(
        body, name="ln2_loss_bwd", grid=(seq // tm,), in_specs=[blk, blk, blk, vec, vec],
        out_specs=[blk, blk, pl.BlockSpec((1, LANES), lambda i: (0, 0)), vec, vec],
        out_shape=[jax.ShapeDtypeStruct((seq, d), F32), jax.ShapeDtypeStruct((seq, d), MXU_DTYPE),
                   jax.ShapeDtypeStruct((1, LANES), F32),
                   jax.ShapeDtypeStruct((1, d), F32), jax.ShapeDtypeStruct((1, d), F32)],
        compiler_params=_params("arbitrary"))(x1, ffn, target, g, b)


HALO = 16


def _conv_rows(e, w_ref, b_ref):
    y = b_ref[...] + w_ref[0:1, :] * pltpu.roll(e, 2, 0)
    y = y + w_ref[1:2, :] * pltpu.roll(e, 1, 0)
    return y + w_ref[2:3, :] * e


_GELU_C = math.sqrt(2.0 / math.pi)
_GELU_A = 0.044715


def _gelu(x):
    return 0.5 * x * (1.0 + jnp.tanh(_GELU_C * (x + _GELU_A * (x * x * x))))


CONV_TN = 256


def _ffn_interleave(a, axis):
    shp = a.shape
    a = a.reshape(shp[:axis] + (2, D_FF // CONV_TN, CONV_TN) + shp[axis + 1:])
    return jnp.swapaxes(a, axis, axis + 1).reshape(shp)


def _ffn_deinterleave(a, axis):
    shp = a.shape
    a = a.reshape(shp[:axis] + (D_FF // CONV_TN, 2, CONV_TN) + shp[axis + 1:])
    return jnp.swapaxes(a, axis, axis + 1).reshape(shp)


def _conv_gate_fwd(u, conv_w, conv_b, tm=1024):
    seq = u.shape[0]
    tm = min(tm, seq)
    tn = CONV_TN

    def body(u_ref, up_ref, w_ref, b_ref, o_ref):
        first = pl.program_id(0) == 0
        e = jnp.concatenate([jnp.where(first, 0.0, up_ref[...]), u_ref[...]], axis=0)
        y = _conv_rows(e, w_ref, b_ref)[HALO:]
        o_ref[...] = (_gelu(y[:, tn:]) * y[:, :tn]).astype(o_ref.dtype)

    hb = tm // HALO
    return pl.pallas_call(
        body, name="conv_gate_fwd", grid=(seq // tm, D_FF // tn),
        in_specs=[pl.BlockSpec((tm, 2 * tn), lambda i, j: (i, j)),
                  pl.BlockSpec((HALO, 2 * tn), lambda i, j: (jnp.maximum(i * hb - 1, 0), j)),
                  pl.BlockSpec((3, 2 * tn), lambda i, j: (0, j)), pl.BlockSpec((1, 2 * tn), lambda i, j: (0, j))],
        out_specs=pl.BlockSpec((tm, tn), lambda i, j: (i, j)), out_shape=jax.ShapeDtypeStruct((seq, D_FF), MXU_DTYPE),
        compiler_params=_params("parallel", "parallel"),
    )(u, u, conv_w, conv_b)


def _conv_gate_bwd(u, d_act, conv_w, conv_b, tm=512):
    seq = u.shape[0]
    tm = min(tm, seq)
    tn = CONV_TN
    ni = seq // tm
    rows_e = tm + 2 * HALO

    def body(u_ref, up_ref, un_ref, da_ref, dan_ref, w_ref, b_ref, du_ref, dw_ref, db_ref):
        i = pl.program_id(1)
        first, last = i == 0, i == ni - 1

        @pl.when(i == 0)
        def _():
            dw_ref[...] = jnp.zeros_like(dw_ref)
            db_ref[...] = jnp.zeros_like(db_ref)

        e = jnp.concatenate([jnp.where(first, 0.0, up_ref[...]), u_ref[...], jnp.where(last, 0.0, un_ref[...])], axis=0)
        y = _conv_rows(e, w_ref, b_ref)
        ya, yg = y[:, :tn], y[:, tn:]
        dact = jnp.concatenate([jnp.zeros((HALO, tn), F32), da_ref[...].astype(F32),
                                jnp.where(last, 0.0, dan_ref[...].astype(F32))], axis=0)
        th = jnp.tanh(_GELU_C * (yg + _GELU_A * (yg * yg * yg)))
        gelu = 0.5 * yg * (1.0 + th)
        gelu_grad = 0.5 * (1.0 + th) + 0.5 * yg * (1.0 - th * th) * (_GELU_C * (1.0 + 3.0 * _GELU_A * (yg * yg)))
        dy = jnp.concatenate([dact * gelu, dact * ya * gelu_grad], axis=1)
        du = w_ref[2:3, :] * dy + w_ref[1:2, :] * pltpu.roll(dy, rows_e - 1, 0) + w_ref[0:1, :] * pltpu.roll(dy, rows_e - 2, 0)
        du_ref[...] = du[HALO:HALO + tm].astype(du_ref.dtype)
        dyt = dy[HALO:HALO + tm]
        dw_ref[0:1, :] += jnp.sum(dyt * pltpu.roll(e, 2, 0)[HALO:HALO + tm], axis=0, keepdims=True)
        dw_ref[1:2, :] += jnp.sum(dyt * pltpu.roll(e, 1, 0)[HALO:HALO + tm], axis=0, keepdims=True)
        dw_ref[2:3, :] += jnp.sum(dyt * e[HALO:HALO + tm], axis=0, keepdims=True)
        db_ref[...] += jnp.sum(dyt, axis=0, keepdims=True)

    hb = tm // HALO
    nh = seq // HALO
    prev = lambda j, i: (jnp.maximum(i * hb - 1, 0), j)
    nxt = lambda j, i: (jnp.minimum((i + 1) * hb, nh - 1), j)
    return pl.pallas_call(
        body, name="conv_gate_bwd", grid=(D_FF // tn, ni),
        in_specs=[pl.BlockSpec((tm, 2 * tn), lambda j, i: (i, j)), pl.BlockSpec((HALO, 2 * tn), prev),
                  pl.BlockSpec((HALO, 2 * tn), nxt), pl.BlockSpec((tm, tn), lambda j, i: (i, j)), pl.BlockSpec((HALO, tn), nxt),
                  pl.BlockSpec((3, 2 * tn), lambda j, i: (0, j)), pl.BlockSpec((1, 2 * tn), lambda j, i: (0, j))],
        out_specs=[pl.BlockSpec((tm, 2 * tn), lambda j, i: (i, j)), pl.BlockSpec((3, 2 * tn), lambda j, i: (0, j)),
                   pl.BlockSpec((1, 2 * tn), lambda j, i: (0, j))],
        out_shape=[jax.ShapeDtypeStruct((seq, 2 * D_FF), MXU_DTYPE), jax.ShapeDtypeStruct((3, 2 * D_FF), F32),
                   jax.ShapeDtypeStruct((1, 2 * D_FF), F32)],
        compiler_params=_params("parallel", "arbitrary"),
    )(u, u, u, d_act, d_act, conv_w, conv_b)


def _pad_heads(w, width):
    w = jnp.transpose(w, (1, 0, 2))
    return jnp.pad(w, ((0, 0), (0, 0), (0, LANES - width))).astype(MXU_DTYPE)


def _heads_major(a):
    return jnp.transpose(a, (1, 0, 2)).reshape(-1, a.shape[2])


def _heads_minor(a, heads):
    return jnp.transpose(a.reshape(heads, -1, a.shape[1]), (1, 0, 2))


def _unpad_heads(d, width):
    return jnp.transpose(d[:, :, :width], (1, 0, 2))


def _split_pad_rows(w_t):
    z = lambda n: jnp.zeros((n, w_t.shape[1]), w_t.dtype)
    return jnp.concatenate([w_t[:384], z(64), w_t[384:416], z(32), w_t[416:]], axis=0)


def _split_unpad_rows(w_p):
    return jnp.concatenate([w_p[:384], w_p[448:480], w_p[512:]], axis=0)


def _pad_w_o(w_o):
    mla = jnp.pad(w_o[:512].reshape(HEADS, HEAD_DIM, D_MODEL), ((0, 0), (0, LANES - HEAD_DIM), (0, 0)))
    return mla.reshape(HEADS * LANES, D_MODEL).astype(MXU_DTYPE), w_o[512:].astype(MXU_DTYPE)


def _unpad_w_o(d_mla, d_dil):
    return jnp.concatenate([d_mla.reshape(HEADS, LANES, D_MODEL)[:, :HEAD_DIM].reshape(512, D_MODEL), d_dil], axis=0)


def _row(v):
    return v.reshape(1, -1).astype(F32)


def _compute_weights(w):
    w_o_mla, w_o_dil = _pad_w_o(w["w_o"])
    return dict(
        w_in_t=_split_pad_rows(w["w_in"].T).astype(MXU_DTYPE), wq=_pad_heads(w["w_uq"], NOPE_DIM + ROPE_DIM),
        wk=_pad_heads(w["w_uk"], NOPE_DIM), wv=_pad_heads(w["w_uv"], HEAD_DIM), w_o_mla=w_o_mla, w_o_dil=w_o_dil,
        w_up_t=_ffn_interleave(w["w_up"].T, 0).astype(MXU_DTYPE), w_down=w["w_down"].astype(MXU_DTYPE),
        conv_w=_ffn_interleave(w["conv_w"].astype(F32), 1),
        g_cq=_row(w["g_cq"]), g_ckv=_row(w["g_ckv"]), ln1_g=_row(w["ln1_g"]), ln1_b=_row(w["ln1_b"]),
        conv_b=_ffn_interleave(_row(w["conv_b"]), 1), ln2_g=_row(w["ln2_g"]), ln2_b=_row(w["ln2_b"]))


def _natural_grads(g):
    return dict(
        w_in=_split_unpad_rows(g["w_in_t"]).T, g_cq=g["g_cq"].reshape(-1), g_ckv=g["g_ckv"].reshape(-1),
        w_uq=_unpad_heads(g["wq"], NOPE_DIM + ROPE_DIM), w_uk=_unpad_heads(g["wk"], NOPE_DIM),
        w_uv=_unpad_heads(g["wv"], HEAD_DIM), w_o=_unpad_w_o(g["w_o_mla"], g["w_o_dil"]), ln1_g=g["ln1_g"].reshape(-1),
        ln1_b=g["ln1_b"].reshape(-1), w_up=_ffn_deinterleave(g["w_up_t"], 0).T, conv_w=_ffn_deinterleave(g["conv_w"], 1),
        conv_b=_ffn_deinterleave(g["conv_b"], 1).reshape(-1),
        w_down=g["w_down"], ln2_g=g["ln2_g"].reshape(-1), ln2_b=g["ln2_b"].reshape(-1))


def _layer_grads(x0, target, cw, first_after=(), late_weights=None, on_grads=None):
    seq = x0.shape[0]
    ctab, stab = _rope_tables(seq)
    gq, gk = cw["g_cq"], cw["g_ckv"]
    wq, wk, wv = cw["wq"], cw["wk"], cw["wv"]
    x0b = x0.astype(MXU_DTYPE)
    notify = (lambda stage, grads: ()) if on_grads is None else on_grads

    h = _mm(x0b, cw["w_in_t"], name="mm_h", tb=True, tm=1024, tn=1024, tk=1024, after=first_after)
    qf, kf, vp = _mla_prep(h, gq, gk, wq, wk, wv, ctab, stab)
    o_mla, o_mla_b, lse_mla = _mla_attn_fwd(qf, kf, vp)
    o_dil, o_dil_b, lse_dil = _dil_fwd(h)
    if late_weights is not None:
        cw = {**cw, **late_weights(o_mla_b)}
    cb = cw["conv_b"]
    z1, x1, x1b = _mix_ln1(o_mla_b, o_dil_b, cw["w_o_mla"], cw["w_o_dil"], x0, cw["ln1_g"], cw["ln1_b"])
    u = _mm(x1b, cw["w_up_t"], name="mm_up", tb=True, tm=1024, tn=1408, tk=1024)
    act = _conv_gate_fwd(u, cw["conv_w"], cb)
    ffn = _mm(act, cw["w_down"], name="mm_down", tm=1024, tn=1024, tk=2816)
    dz2, dz2b, loss, d_ln2_g, d_ln2_b = _ln2_loss_bwd(x1, ffn, target, cw["ln2_g"], cw["ln2_b"])

    d_act = _mm(dz2b, cw["w_down"], name="mm_d_act", tb=True, out_dtype=MXU_DTYPE, tm=1024, tn=1408, tk=1024)
    d_w_down = _mm(act, dz2b, name="mm_dw_down", ta=True, out_dtype=MXU_DTYPE, tm=1408, tn=1024, tk=1024)
    du, d_conv_w, d_conv_b = _conv_gate_bwd(u, d_act, cw["conv_w"], cb)
    dx1 = _mm(du, cw["w_up_t"], name="mm_dx1", res=dz2, res_scale=DN_ALPHA, tm=512, tn=1024, tk=2 * D_FF)
    d_w_up_t = _mm(du, x1b, name="mm_dw_up", ta=True, out_dtype=MXU_DTYPE, tm=1408, tn=1024, tk=2048)
    grads = dict(w_up_t=d_w_up_t, w_down=d_w_down, conv_w=d_conv_w, conv_b=d_conv_b, ln2_g=d_ln2_g, ln2_b=d_ln2_b)
    dz1, dz1b, d_ln1_g, d_ln1_b = _ln_bwd(dx1, z1, cw["ln1_g"], "ln1_bwd", after=notify("ffn", grads))
    do_mla = _mm_do_mla(dz1b, cw["w_o_mla"])
    do_dil = _mm(dz1b, cw["w_o_dil"], name="mm_do_dil", tb=True, tm=1024, tn=512, tk=1024)
    d_w_o_mla = _mm_dw_o_mla(o_mla_b, dz1b)
    d_w_o_dil = _mm(o_dil_b, dz1b, name="mm_dw_o_dil", ta=True, out_dtype=MXU_DTYPE, tm=512, tn=1024, tk=1024)
    grads.update(w_o_mla=d_w_o_mla, w_o_dil=d_w_o_dil, ln1_g=d_ln1_g, ln1_b=d_ln1_b)
    dq_dil, dk_dil, dv_dil = _dil_bwd(h, o_dil, lse_dil, do_dil)
    dqf, dkf, dvf = _mla_attn_bwd(qf, kf, vp, o_mla, lse_mla, do_mla)
    dh_mla, d_wq, d_wk, d_wv, d_gq, d_gk = _mla_prep_bwd(h, gq, gk, wq, wk, wv, ctab, stab, dqf, dkf, dvf,
                                                          after=notify("w_o", grads))
    dh = (dh_mla, dq_dil, dk_dil, dv_dil)
    d_w_in_t = _mm_dw_in(dh, x0b)
    grads.update(w_in_t=d_w_in_t, wq=d_wq, wk=d_wk, wv=d_wv, g_cq=d_gq, g_ckv=d_gk, loss=loss)
    grad_x = _mm_dx0(dh, cw["w_in_t"], dz1, after=notify("rest", grads))
    return loss, grad_x, grads


def _all_gather(blocks, name):
    na = len(blocks)

    def body(*refs):
        ins, outs = refs[:na], refs[na:2 * na]
        send_sems, recv_sems, local_sems = refs[2 * na:]
        x, y, c = lax.axis_index("x"), lax.axis_index("y"), lax.axis_index("c")
        me, sibling = (x, y, c), (x, y, 1 - c)
        chips = [(1 - x, y), (x, 1 - y), (1 - x, 1 - y)]

        def slot(out, pos):
            return out.at[4 * pos[0] + 2 * pos[1] + pos[2]]

        def copy(a, k, block, to, src=None):
            return pltpu.make_async_remote_copy(
                src_ref=slot(outs[a], block) if src is None else src, dst_ref=slot(outs[a], block),
                send_sem=send_sems.at[7 * a + k], recv_sem=recv_sems.at[7 * a + k],
                device_id=to, device_id_type=pl.DeviceIdType.MESH)

        mine = [pltpu.make_async_copy(ins[a], slot(outs[a], me), local_sems.at[a]) for a in range(na)]
        for cp in mine:
            cp.start()
        first = []
        for a in range(na):
            first.append(copy(a, 0, me, sibling, src=ins[a]))
            first += [copy(a, 1 + j, me, (*chip, c), src=ins[a]) for j, chip in enumerate(chips)]
        for cp in first:
            cp.start()
        passed = []
        for j, chip in enumerate(chips):
            for a in range(na):
                copy(a, 1 + j, (*chip, c), me).wait_recv()
                cp = copy(a, 4 + j, (*chip, c), sibling)
                cp.start()
                passed.append(cp)
        for a in range(na):
            copy(a, 0, sibling, me).wait_recv()
            for j, chip in enumerate(chips):
                copy(a, 4 + j, (*chip, 1 - c), me).wait_recv()
        for cp in first + passed:
            cp.wait_send()
        for cp in mine:
            cp.wait()

    any_spec = pl.BlockSpec(memory_space=pl.ANY)
    return pl.pallas_call(
        body, name=name, in_specs=[any_spec] * na, out_specs=[any_spec] * na,
        out_shape=[jax.ShapeDtypeStruct((N_DEV,) + b.shape, b.dtype) for b in blocks],
        scratch_shapes=[pltpu.SemaphoreType.DMA((7 * na,)), pltpu.SemaphoreType.DMA((7 * na,)), pltpu.SemaphoreType.DMA((na,))],
    )(*blocks)


_HBM_SPEC = pl.BlockSpec(memory_space=pltpu.HBM)
_SEM_SPEC = pl.BlockSpec(memory_space=pltpu.SEMAPHORE)
_DATAFLOW = pltpu.CompilerParams(has_side_effects=pltpu.SideEffectType.DATAFLOW_SIDE_EFFECTING)


def _split_copies(ins, lands, send_sems, recv_sems, gather):
    x, y, c = lax.axis_index("x"), lax.axis_index("y"), lax.axis_index("c")
    me = 4 * x + 2 * y + c
    copies = []
    for a in range(len(ins)):
        for d in range(1, N_DEV):
            px, py, pc = x ^ (d >> 2), y ^ ((d >> 1) & 1), c ^ (d & 1)
            copies.append(pltpu.make_async_remote_copy(
                src_ref=ins[a] if gather else ins[a].at[4 * px + 2 * py + pc], dst_ref=lands[a].at[me],
                send_sem=send_sems.at[7 * a + d - 1], recv_sem=recv_sems.at[7 * a + d - 1],
                device_id=(px, py, pc), device_id_type=pl.DeviceIdType.MESH))
    return copies


def _send_start(srcs, gather, name):
    na = len(srcs)
    land_types = [pltpu.HBM(((N_DEV,) + s.shape) if gather else s.shape, s.dtype) for s in srcs]

    def body(*refs):
        ins, lands = refs[:na], refs[na:2 * na]
        send_sems, recv_sems, token = refs[2 * na], refs[2 * na + 1], refs[-1]
        for cp in _split_copies(ins, lands, send_sems, recv_sems, gather):
            cp.start()
        token[...] = jnp.zeros_like(token)

    hbm = lambda a: pltpu.with_memory_space_constraint(a, pltpu.HBM)
    outs = pl.pallas_call(
        body, name=name,
        out_shape=(pltpu.SemaphoreType.DMA((7 * na,)), pltpu.SemaphoreType.DMA((7 * na,)),
                   *[pltpu.HBM(s.shape, s.dtype) for s in srcs], *land_types, jax.ShapeDtypeStruct((8, LANES), F32)),
        in_specs=[_HBM_SPEC] * (2 * na),
        out_specs=(_SEM_SPEC, _SEM_SPEC, *[_HBM_SPEC] * (2 * na), pl.BlockSpec(memory_space=pltpu.VMEM)),
        input_output_aliases={i: 2 + i for i in range(2 * na)}, compiler_params=_DATAFLOW,
    )(*[hbm(s) for s in srcs], *[hbm(lax.empty(t.shape, t.dtype)) for t in land_types])
    return dict(send=outs[0], recv=outs[1], srcs=list(outs[2:2 + na]), lands=list(outs[2 + na:2 + 2 * na]), token=outs[-1],
                gather=gather)


def _send_wait(handle, after, name):
    na = len(handle["srcs"])
    gather = handle["gather"]
    after = list(after)

    def body(*refs):
        ins, lands = refs[:na], refs[na:2 * na]
        send_sems, recv_sems = refs[2 * na], refs[2 * na + 1]
        for cp in _split_copies(ins, lands, send_sems, recv_sems, gather):
            cp.wait_send()
            cp.wait_recv()

    both = handle["srcs"] + handle["lands"]
    outs = pl.pallas_call(
        body, name=name, out_shape=[pltpu.HBM(a.shape, a.dtype) for a in both],
        in_specs=[_HBM_SPEC] * (2 * na) + [_SEM_SPEC, _SEM_SPEC] + [_ANY_SPEC] * len(after),
        out_specs=[_HBM_SPEC] * (2 * na), input_output_aliases={i: i for i in range(2 * na)}, compiler_params=_DATAFLOW,
    )(*both, handle["send"], handle["recv"], *after)
    return list(outs[:na]), list(outs[na:])


def _sum_slots(p_ref):
    g = p_ref[0].astype(F32)
    for s in range(1, p_ref.shape[0]):
        g = g + p_ref[s].astype(F32)
    return g


def _adamw_refs(g, w_ref, m_ref, v_ref, g_out, d_out, m_out, v_out):
    c1 = 1.0 - ADAM_B1 ** ADAM_STEP
    c2 = 1.0 - ADAM_B2 ** ADAM_STEP
    m_new = ADAM_B1 * m_ref[...] + (1.0 - ADAM_B1) * g
    v_new = ADAM_B2 * v_ref[...] + (1.0 - ADAM_B2) * (g * g)
    g_out[...] = g
    m_out[...] = m_new
    v_out[...] = v_new
    d_out[...] = -ADAM_LR * ((m_new / c1) / (jnp.sqrt(v_new / c2) + ADAM_EPS) + ADAM_WD * w_ref[...])


def _adamw(parts, w, m, v, name):
    npart, r, n = parts.shape
    tr = r if r <= 256 else max(t for t in range(16, 257, 16) if r % t == 0)

    def body(p_ref, w_ref, m_ref, v_ref, g_out, d_out, m_out, v_out):
        _adamw_refs(_sum_slots(p_ref), w_ref, m_ref, v_ref, g_out, d_out, m_out, v_out)

    blk = pl.BlockSpec((tr, n), lambda i: (i, 0))
    shp = jax.ShapeDtypeStruct((r, n), F32)
    return pl.pallas_call(
        body, name=name, grid=(r // tr,), in_specs=[pl.BlockSpec((npart, tr, n), lambda i: (0, i, 0)), blk, blk, blk],
        out_specs=[blk] * 4, out_shape=[shp] * 4, compiler_params=_params("parallel"),
    )(parts, w, m, v)


def _adamw_small(parts, ws, ms, vs, loss_parts, name):
    n = len(parts)

    def body(*refs):
        ins, outs = refs[:4 * n + 1], refs[4 * n + 1:]
        for i in range(n):
            _adamw_refs(_sum_slots(ins[i]), ins[n + i], ins[2 * n + i], ins[3 * n + i], *outs[4 * i:4 * i + 4])
        outs[4 * n][...] = _sum_slots(ins[4 * n])

    out_shape = [jax.ShapeDtypeStruct(w.shape, F32) for w in ws for _ in range(4)]
    res = pl.pallas_call(body, name=name, out_shape=out_shape + [jax.ShapeDtypeStruct((1, LANES), F32)],
                         compiler_params=_params())(*parts, *ws, *ms, *vs, loss_parts)
    return [res[4 * i:4 * i + 4] for i in range(n)], res[4 * n]


REPLICATED = ("g_cq", "g_ckv", "w_uk", "w_uv", "ln1_g", "ln1_b", "conv_b", "ln2_g", "ln2_b")
ALL_WEIGHTS = ("w_in", "g_cq", "g_ckv", "w_uq", "w_uk", "w_uv", "w_o", "ln1_g", "ln1_b", "w_up", "conv_w", "conv_b",
               "w_down", "ln2_g", "ln2_b")


def kernel(x, w_in, g_cq, g_ckv, w_uq, w_uk, w_uv, w_o, ln1_g, ln1_b, w_up, conv_w, conv_b, w_down, ln2_g, ln2_b, loss_target, m_w_in, m_g_cq, m_g_ckv, m_w_uq, m_w_uk, m_w_uv, m_w_o, m_ln1_g, m_ln1_b, m_w_up, m_conv_w, m_conv_b, m_w_down, m_ln2_g, m_ln2_b, v_w_in, v_g_cq, v_g_ckv, v_w_uq, v_w_uk, v_w_uv, v_w_o, v_ln1_g, v_ln1_b, v_w_up, v_conv_w, v_conv_b, v_w_down, v_ln2_g, v_ln2_b):
    w = dict(w_in=w_in, g_cq=g_cq, g_ckv=g_ckv, w_uq=w_uq, w_uk=w_uk, w_uv=w_uv, w_o=w_o, ln1_g=ln1_g, ln1_b=ln1_b,
             w_up=w_up, conv_w=conv_w, conv_b=conv_b, w_down=w_down, ln2_g=ln2_g, ln2_b=ln2_b)
    m = dict(w_in=m_w_in, g_cq=m_g_cq, g_ckv=m_g_ckv, w_uq=m_w_uq, w_uk=m_w_uk, w_uv=m_w_uv, w_o=m_w_o, ln1_g=m_ln1_g,
             ln1_b=m_ln1_b, w_up=m_w_up, conv_w=m_conv_w, conv_b=m_conv_b, w_down=m_w_down, ln2_g=m_ln2_g, ln2_b=m_ln2_b)
    v = dict(w_in=v_w_in, g_cq=v_g_cq, g_ckv=v_g_ckv, w_uq=v_w_uq, w_uk=v_w_uk, w_uv=v_w_uv, w_o=v_w_o, ln1_g=v_ln1_g,
             ln1_b=v_ln1_b, w_up=v_w_up, conv_w=v_conv_w, conv_b=v_conv_b, w_down=v_w_down, ln2_g=v_ln2_g, ln2_b=v_ln2_b)
    me = 4 * lax.axis_index("x") + 2 * lax.axis_index("y") + lax.axis_index("c")
    wire = lambda a: a.astype(WIRE_DTYPE)
    pad_taps = lambda a: jnp.pad(a, ((0, 8 - a.shape[0]), (0, 0)))

    own_slot = lambda buf, block: lax.dynamic_update_index_in_dim(buf, block, me, 0)
    blocks = lambda a: wire(a).reshape((N_DEV, a.shape[0] // N_DEV) + a.shape[1:])

    g_in, g_uq, g_conv = _all_gather(
        [wire(w_in).T, _heads_major(wire(w_uq)), pad_taps(conv_w)],
        "gather_weights")
    late = _send_start([wire(w_o), wire(w_up).T, wire(w_down)], True, "gather_late_start")
    r_uq_dev, e_uq = w_uq.shape[0], w_uq.shape[2]
    wq = jnp.transpose(g_uq.reshape(N_DEV, HEADS, r_uq_dev, e_uq), (1, 0, 2, 3)).reshape(HEADS, Q_RANK, e_uq)
    cw = dict(
        w_in_t=_split_pad_rows(g_in.reshape(-1, D_MODEL)).astype(MXU_DTYPE),
        wq=jnp.pad(wq, ((0, 0), (0, 0), (0, LANES - e_uq))).astype(MXU_DTYPE),
        wk=_pad_heads(w_uk, NOPE_DIM), wv=_pad_heads(w_uv, HEAD_DIM),
        conv_w=_ffn_interleave(jnp.transpose(g_conv[:, :conv_w.shape[0]], (1, 0, 2)).reshape(conv_w.shape[0], -1), 1),
        g_cq=_row(g_cq), g_ckv=_row(g_ckv), ln1_g=_row(ln1_g), ln1_b=_row(ln1_b), conv_b=_ffn_interleave(_row(conv_b), 1),
        ln2_g=_row(ln2_g), ln2_b=_row(ln2_b))

    def late_weights(after):
        own, landed = _send_wait(late, [after], "gather_late_wait")
        g_o, g_up, g_down = [own_slot(buf, blk) for buf, blk in zip(landed, own)]
        w_o_mla, w_o_dil = _pad_w_o(g_o.reshape(-1, D_MODEL))
        return dict(w_o_mla=w_o_mla, w_o_dil=w_o_dil, w_up_t=_ffn_interleave(g_up.reshape(-1, D_MODEL), 0).astype(MXU_DTYPE),
                    w_down=g_down.reshape(-1, D_MODEL).astype(MXU_DTYPE))

    sent = {}

    def on_grads(stage, g):
        if stage == "ffn":
            sent[stage] = [_send_start([blocks(_ffn_deinterleave(g["w_up_t"], 0)), blocks(g["w_down"])], False, "exchange_ffn_start")]
        elif stage == "w_o":
            sent[stage] = [_send_start([blocks(_unpad_w_o(g["w_o_mla"], g["w_o_dil"]))], False, "exchange_w_o_start")]
        else:
            d_in = blocks(_split_unpad_rows(g["w_in_t"]))
            d_uq = wire(jnp.transpose(g["wq"][:, :, :e_uq].reshape(HEADS, N_DEV, r_uq_dev, e_uq), (1, 0, 2, 3))
                        ).reshape(N_DEV, HEADS * r_uq_dev, e_uq)
            dense = lambda a, width: wire(a[:, :, :width]).reshape(-1, LANES)
            small = dict(g_cq=g["g_cq"], g_ckv=g["g_ckv"], w_uk=dense(g["wk"], NOPE_DIM), w_uv=dense(g["wv"], HEAD_DIM),
                         ln1_g=g["ln1_g"], ln1_b=g["ln1_b"], conv_b=_ffn_deinterleave(g["conv_b"], 1), ln2_g=g["ln2_g"],
                         ln2_b=g["ln2_b"])
            sent[stage] = [_send_start([d_in, d_uq], False, "exchange_rest_start"),
                           _send_start([small[n] for n in REPLICATED] + [_ffn_deinterleave(g["conv_w"], 1), g["loss"]],
                                       True, "gather_small_start")]
        return [h["token"] for h in sent[stage]]

    _, grad_x, _ = _layer_grads(x[0], loss_target[0], cw, [late["token"]], late_weights, on_grads)

    def landed(handle, after, name):
        own, got = _send_wait(handle, after, name)
        pick = (lambda a: a) if handle["gather"] else (lambda a: lax.dynamic_index_in_dim(a, me, 0, keepdims=False))
        return [own_slot(buf, pick(src)) for buf, src in zip(got, own)]

    out = {}

    def update(name, parts, view=None):
        to2d = {None: lambda a: a, "t": lambda a: a.T, "heads": _heads_major}[view]
        back = {None: lambda a: a, "t": lambda a: a.T, "heads": lambda a: _heads_minor(a, HEADS)}[view]
        res = _adamw(parts, to2d(w[name]), to2d(m[name]), to2d(v[name]), "adamw_" + name)
        for kind, a in zip(("grad", "delta", "new_m", "new_v"), res):
            out[kind, name] = back(a)
        return res[0]

    r_up, r_down = landed(sent["ffn"][0], [grad_x], "exchange_ffn_wait")
    (r_o,) = landed(sent["w_o"][0], [grad_x], "exchange_w_o_wait")
    done = [update("w_up", r_up, "t"), update("w_down", r_down), update("w_o", r_o)]
    r_in, r_uq = landed(sent["rest"][0], done, "exchange_rest_wait")
    *rep_all, cw_all, loss_all = landed(sent["rest"][1], done, "gather_small_wait")
    update("w_in", r_in, "t")
    update("w_uq", r_uq, "heads")
    heads_major = ("w_uk", "w_uv")
    two_d = lambda n, a: _heads_major(a) if n in heads_major else a.reshape(1, -1)
    rep_all = [p.reshape(N_DEV, -1, w[n].shape[2]) if n in heads_major else p for n, p in zip(REPLICATED, rep_all)]
    res, loss_sum = _adamw_small(rep_all, *[[two_d(n, d[n]) for n in REPLICATED] for d in (w, m, v)], loss_all, "adamw_replicated")
    for n, quad in zip(REPLICATED, res):
        for kind, a in zip(("grad", "delta", "new_m", "new_v"), quad):
            out[kind, n] = _heads_minor(a, HEADS) if n in heads_major else a.reshape(w[n].shape)
    loss = loss_sum[0, 0]
    ncw = conv_w.shape[1]
    update("conv_w", lax.dynamic_slice_in_dim(cw_all[:, :conv_w.shape[0]], me * ncw, ncw, axis=2))

    return (loss, grad_x[None], *[out[kind, n] for kind in ("grad", "delta", "new_m", "new_v") for n in ALL_WEIGHTS])
```

```python
import functools
import math

import jax
import jax.numpy as jnp
import numpy as np
from jax import lax
from jax.experimental import pallas as pl
from jax.experimental.pallas import tpu as pltpu

F32 = jnp.float32
MXU_DTYPE = jnp.bfloat16
WIRE_DTYPE = jnp.bfloat16

N_DEV = 8
D_MODEL = 1024
HEADS = 8
HEAD_DIM = 64
LANES = 128
Q_RANK, KV_RANK, ROPE_DIM, NOPE_DIM = 256, 128, 32, 64
DIL_WIDTH = HEADS * HEAD_DIM
IN_WIDTH = 1952
IN_PAD = 2048
D_FF = 2816
ROPE_THETA = 10000.0
DIL_PAIRS = ((128, 1), (512, 4), (2048, 16))
DIL_BLOCK = 128
DN_ALPHA = 2.0 ** 0.25
LN_EPS = 1e-5
RMS_EPS = 1e-6
MLA_SCALE = 1.0 / math.sqrt(NOPE_DIM + ROPE_DIM)
MLA_SCALE_LOG2 = MLA_SCALE * math.log2(math.e)
DIL_SCALE = 1.0 / math.sqrt(HEAD_DIM)
ALIBI_SLOPES = tuple(2.0 ** (-8.0 * (h + 1) / HEADS) for h in range(HEADS))
NEG_BIG = -1e30
ADAM_LR, ADAM_B1, ADAM_B2, ADAM_EPS, ADAM_WD, ADAM_STEP = 0.001, 0.9, 0.999, 1e-08, 0.01, 10
VMEM_LIMIT = 48 * 1024 * 1024


def _params(*sem):
    return pltpu.CompilerParams(dimension_semantics=sem or None, vmem_limit_bytes=VMEM_LIMIT)


def _dot(a, b, ca, cb):
    return lax.dot_general(a, b, (((ca,), (cb,)), ((), ())), preferred_element_type=F32)


_ANY_SPEC = pl.BlockSpec(memory_space=pl.ANY)


def _mm(a, b, *, name, tm, tn, tk, ta=False, tb=False, out_dtype=F32, res=None, res_scale=1.0, after=()):
    m, k = (a.shape[1], a.shape[0]) if ta else a.shape
    n = b.shape[0] if tb else b.shape[1]
    assert (b.shape[1] if tb else b.shape[0]) == k
    tm, tn, tk = min(tm, m), min(tn, n), min(tk, k)
    assert m % tm == 0 and n % tn == 0 and k % tk == 0, (name, m, n, k, tm, tn, tk)
    nk = k // tk
    a_spec = (pl.BlockSpec((tk, tm), lambda i, j, kk: (kk, i)) if ta
              else pl.BlockSpec((tm, tk), lambda i, j, kk: (i, kk)))
    b_mode = dict(pipeline_mode=pl.Buffered(1)) if (tn == n and tk == k) else {}
    b_spec = (pl.BlockSpec((tn, tk), lambda i, j, kk: (j, kk), **b_mode) if tb
              else pl.BlockSpec((tk, tn), lambda i, j, kk: (kk, j), **b_mode))
    o_spec = pl.BlockSpec((tm, tn), lambda i, j, kk: (i, j))
    in_specs = [a_spec, b_spec]
    args = [a, b]
    if res is not None:
        in_specs.append(o_spec)
        args.append(res)
    n_in = len(args) + len(after)
    in_specs += [_ANY_SPEC] * len(after)
    args += list(after)
    ca, cb = (0 if ta else 1), (1 if tb else 0)

    def finish(acc, r_ref, o_ref):
        if r_ref is not None:
            acc = acc + res_scale * r_ref[...]
        o_ref[...] = acc.astype(o_ref.dtype)

    def body(*refs):
        a_ref, b_ref = refs[:2]
        r_ref = refs[2] if res is not None else None
        o_ref = refs[n_in]
        part = _dot(a_ref[...].astype(MXU_DTYPE), b_ref[...].astype(MXU_DTYPE), ca, cb)
        if nk == 1:
            finish(part, r_ref, o_ref)
            return
        acc_ref = refs[-1]
        kk = pl.program_id(2)

        @pl.when(kk == 0)
        def _():
            acc_ref[...] = part

        @pl.when(kk > 0)
        def _():
            acc_ref[...] += part

        @pl.when(kk == nk - 1)
        def _():
            finish(acc_ref[...], r_ref, o_ref)

    return pl.pallas_call(
        body, name=name, grid=(m // tm, n // tn, nk), in_specs=in_specs, out_specs=o_spec,
        out_shape=jax.ShapeDtypeStruct((m, n), out_dtype),
        scratch_shapes=[pltpu.VMEM((tm, tn), F32)] if nk > 1 else [],
        compiler_params=_params("parallel", "parallel", "arbitrary"),
    )(*args)


def _mm_do_mla(dz, w_o_mla, tm=1024):
    seq, d = dz.shape
    tm = min(tm, seq)

    def body(a_ref, b_ref, o_ref):
        a = a_ref[...].astype(MXU_DTYPE)
        for hd in range(HEADS):
            o_ref[hd] = _dot(a, b_ref[LANES * hd:LANES * (hd + 1), :], 1, 1)

    return pl.pallas_call(
        body, name="mm_do_mla", grid=(seq // tm,),
        in_specs=[pl.BlockSpec((tm, d), lambda i: (i, 0)), pl.BlockSpec((HEADS * LANES, d), lambda i: (0, 0))],
        out_specs=pl.BlockSpec((HEADS, tm, LANES), lambda i: (0, i, 0)),
        out_shape=jax.ShapeDtypeStruct((HEADS, seq, LANES), F32), compiler_params=_params("parallel"),
    )(dz, w_o_mla)


def _mm_dw_o_mla(o_mla, dz, tk=1024):
    seq, d = dz.shape
    tk = min(tk, seq)
    nk = seq // tk

    def body(a_ref, b_ref, o_ref, acc_ref):
        kk = pl.program_id(0)

        @pl.when(kk == 0)
        def _():
            acc_ref[...] = jnp.zeros_like(acc_ref)

        b = b_ref[...].astype(MXU_DTYPE)
        for hd in range(HEADS):
            acc_ref[LANES * hd:LANES * (hd + 1), :] += _dot(a_ref[hd].astype(MXU_DTYPE), b, 0, 0)

        @pl.when(kk == nk - 1)
        def _():
            o_ref[...] = acc_ref[...].astype(o_ref.dtype)

    return pl.pallas_call(
        body, name="mm_dw_o_mla", grid=(nk,),
        in_specs=[pl.BlockSpec((HEADS, tk, LANES), lambda kk: (0, kk, 0)), pl.BlockSpec((tk, d), lambda kk: (kk, 0))],
        out_specs=pl.BlockSpec((HEADS * LANES, d), lambda kk: (0, 0)),
        out_shape=jax.ShapeDtypeStruct((HEADS * LANES, d), MXU_DTYPE),
        scratch_shapes=[pltpu.VMEM((HEADS * LANES, d), F32)], compiler_params=_params("arbitrary"),
    )(o_mla, dz)


def _rope_tables(seq):
    half = ROPE_DIM // 2
    f32 = np.float32
    freqs = np.power(f32(ROPE_THETA), -np.arange(half, dtype=f32) / f32(half))
    ang = np.arange(seq, dtype=f32)[:, None] * freqs[None, :]
    cos, sin = np.cos(ang, dtype=f32), np.sin(ang, dtype=f32)
    one = np.ones((seq, NOPE_DIM), f32)
    tail = np.ones((seq, LANES - NOPE_DIM - ROPE_DIM), f32)
    ctab = np.concatenate([one, cos, cos, tail], axis=1)
    stab = np.concatenate([0 * one, -sin, sin, 0 * tail], axis=1)
    return jnp.asarray(ctab), jnp.asarray(stab)


def _rope_swap(t):
    lane = lax.broadcasted_iota(jnp.int32, t.shape, 1)
    half = ROPE_DIM // 2
    return jnp.where(lane < NOPE_DIM + half, pltpu.roll(t, LANES - half, 1), pltpu.roll(t, half, 1))


def _rope(t, ctab, stab):
    return t * ctab + _rope_swap(t) * stab


def _rope_inv(t, ctab, stab):
    return t * ctab - _rope_swap(t) * stab


def _rms(x, g):
    r = lax.rsqrt(jnp.mean(x * x, axis=-1, keepdims=True) + RMS_EPS)
    xh = x * r
    return xh, r, xh * g


def _mla_prep(h, g_cq, g_ckv, wq, wk, wv, ctab, stab, tm=512):
    seq = h.shape[0]
    tm = min(tm, seq)

    def body(h_ref, gq_ref, gk_ref, wq_ref, wk_ref, wv_ref, c_ref, s_ref, q_out, k_out, v_out):
        hb = h_ref[...]
        ctab_, stab_ = c_ref[...], s_ref[...]
        _, _, cqn = _rms(hb[:, :Q_RANK], gq_ref[...])
        _, _, ckn = _rms(hb[:, Q_RANK:Q_RANK + KV_RANK], gk_ref[...])
        cqn = cqn.astype(MXU_DTYPE)
        ckn = ckn.astype(MXU_DTYPE)
        krr = _rope(hb[:, Q_RANK + KV_RANK:], ctab_, stab_)
        for hd in range(HEADS):
            q = _dot(cqn, wq_ref[hd], 1, 0)
            q_out[hd] = _rope(q, ctab_, stab_).astype(q_out.dtype)
            k_out[hd] = (_dot(ckn, wk_ref[hd], 1, 0) + krr).astype(k_out.dtype)
            v_out[hd] = _dot(ckn, wv_ref[hd], 1, 0).astype(v_out.dtype)

    full = lambda *shape: pl.BlockSpec(shape, lambda i: (0,) * len(shape))
    slab = pl.BlockSpec((HEADS, tm, LANES), lambda i: (0, i, 0))
    shp = jax.ShapeDtypeStruct((HEADS, seq, LANES), MXU_DTYPE)
    return pl.pallas_call(
        body, name="mla_prep", grid=(seq // tm,),
        in_specs=[pl.BlockSpec((tm, 512), lambda i: (i, 0)), full(1, Q_RANK), full(1, KV_RANK),
                  full(HEADS, Q_RANK, LANES), full(HEADS, KV_RANK, LANES), full(HEADS, KV_RANK, LANES),
                  pl.BlockSpec((tm, LANES), lambda i: (i, 0)), pl.BlockSpec((tm, LANES), lambda i: (i, 0))],
        out_specs=[slab, slab, slab], out_shape=[shp, shp, shp],
        compiler_params=_params("parallel"),
    )(h, g_cq, g_ckv, wq, wk, wv, ctab, stab)


def _mla_prep_bwd(h, g_cq, g_ckv, wq, wk, wv, ctab, stab, dq, dk, dv, tm=512, after=()):
    seq = h.shape[0]
    tm = min(tm, seq)
    n_after = len(after)

    def body(h_ref, gq_ref, gk_ref, wq_ref, wk_ref, wv_ref, c_ref, s_ref, dq_ref, dk_ref, dv_ref, *rest):
        dh_ref, dwq_ref, dwk_ref, dwv_ref, dgq_ref, dgk_ref = rest[n_after:]

        @pl.when(pl.program_id(0) == 0)
        def _():
            for r in (dwq_ref, dwk_ref, dwv_ref, dgq_ref, dgk_ref):
                r[...] = jnp.zeros_like(r)

        hb = h_ref[...]
        ctab_, stab_ = c_ref[...], s_ref[...]
        gq, gk = gq_ref[...], gk_ref[...]
        xq, rq, cqn = _rms(hb[:, :Q_RANK], gq)
        xk, rk, ckn = _rms(hb[:, Q_RANK:Q_RANK + KV_RANK], gk)
        cqn = cqn.astype(MXU_DTYPE)
        ckn = ckn.astype(MXU_DTYPE)
        d_cqn = jnp.zeros((tm, Q_RANK), F32)
        d_ckn = jnp.zeros((tm, KV_RANK), F32)
        d_krr = jnp.zeros((tm, LANES), F32)
        for hd in range(HEADS):
            dqh = _rope_inv(dq_ref[hd], ctab_, stab_).astype(MXU_DTYPE)
            d_cqn += _dot(dqh, wq_ref[hd], 1, 1)
            dwq_ref[hd] += _dot(cqn, dqh, 0, 0)
            dkh = dk_ref[hd]
            d_krr += dkh
            dkh = dkh.astype(MXU_DTYPE)
            d_ckn += _dot(dkh, wk_ref[hd], 1, 1)
            dwk_ref[hd] += _dot(ckn, dkh, 0, 0)
            dvh = dv_ref[hd].astype(MXU_DTYPE)
            d_ckn += _dot(dvh, wv_ref[hd], 1, 1)
            dwv_ref[hd] += _dot(ckn, dvh, 0, 0)
        lane = lax.broadcasted_iota(jnp.int32, (tm, LANES), 1)
        rot = (lane >= NOPE_DIM) & (lane < NOPE_DIM + ROPE_DIM)
        d_kr = jnp.where(rot, _rope_inv(jnp.where(rot, d_krr, 0.0), ctab_, stab_), 0.0)

        def rms_bwd(dy, xh, r, g, dg_ref):
            dg_ref[...] += jnp.sum(dy * xh, axis=0, keepdims=True)
            dxh = dy * g
            return r * (dxh - xh * jnp.mean(dxh * xh, axis=-1, keepdims=True))

        d_cq = rms_bwd(d_cqn, xq, rq, gq, dgq_ref)
        d_ck = rms_bwd(d_ckn, xk, rk, gk, dgk_ref)
        dh_ref[...] = jnp.concatenate([d_cq, d_ck, d_kr], axis=1).astype(dh_ref.dtype)

    full = lambda *shape: pl.BlockSpec(shape, lambda i: (0,) * len(shape))
    slab = pl.BlockSpec((HEADS, tm, LANES), lambda i: (0, i, 0))
    return pl.pallas_call(
        body, name="mla_prep_bwd", grid=(seq // tm,),
        in_specs=[pl.BlockSpec((tm, 512), lambda i: (i, 0)), full(1, Q_RANK), full(1, KV_RANK),
                  full(HEADS, Q_RANK, LANES), full(HEADS, KV_RANK, LANES), full(HEADS, KV_RANK, LANES),
                  pl.BlockSpec((tm, LANES), lambda i: (i, 0)), pl.BlockSpec((tm, LANES), lambda i: (i, 0)),
                  slab, slab, slab] + [_ANY_SPEC] * n_after,
        out_specs=[pl.BlockSpec((tm, 512), lambda i: (i, 0)), full(HEADS, Q_RANK, LANES), full(HEADS, KV_RANK, LANES),
                   full(HEADS, KV_RANK, LANES), full(1, Q_RANK), full(1, KV_RANK)],
        out_shape=[jax.ShapeDtypeStruct((seq, 512), MXU_DTYPE), jax.ShapeDtypeStruct((HEADS, Q_RANK, LANES), F32),
                   jax.ShapeDtypeStruct((HEADS, KV_RANK, LANES), F32), jax.ShapeDtypeStruct((HEADS, KV_RANK, LANES), F32),
                   jax.ShapeDtypeStruct((1, Q_RANK), F32), jax.ShapeDtypeStruct((1, KV_RANK), F32)],
        compiler_params=_params("arbitrary"),
    )(h, g_cq, g_ckv, wq, wk, wv, ctab, stab, dq, dk, dv, *after)


def _causal_mask(t):
    row = lax.broadcasted_iota(jnp.int32, (t, t), 0)
    col = lax.broadcasted_iota(jnp.int32, (t, t), 1)
    return row >= col


def _mla_attn_fwd(q, k, v, t=512):
    _, seq, _ = q.shape
    t = min(t, seq)

    def body(q_ref, k_ref, v_ref, o_ref, ob_ref, lse_ref, m_ref, l_ref, acc_ref, s_ref):
        i = pl.program_id(1)
        qb = q_ref[...]
        m_ref[...] = jnp.full_like(m_ref, NEG_BIG)
        l_ref[...] = jnp.zeros_like(l_ref)
        acc_ref[...] = jnp.zeros_like(acc_ref)

        def scores(j):
            return _dot(qb, k_ref[pl.ds(pl.multiple_of(j * t, t), t), :], 1, 1) * MLA_SCALE_LOG2

        def softmax_pv(j, s, masked):
            vb = v_ref[pl.ds(pl.multiple_of(j * t, t), t), :]
            if masked:
                s = jnp.where(_causal_mask(t), s, NEG_BIG)
            m_old = m_ref[...]
            m_new = jnp.maximum(m_old, jnp.max(s, axis=1, keepdims=True))
            p = jnp.exp2(s - m_new)
            a = jnp.exp2(m_old - m_new)
            l_ref[...] = a * l_ref[...] + jnp.sum(p, axis=1, keepdims=True)
            acc_ref[...] = a * acc_ref[...] + _dot(p.astype(MXU_DTYPE), vb, 1, 0)
            m_ref[...] = m_new

        s_ref[...] = scores(0)

        def loop_body(j, c):
            s_next = scores(j + 1)
            softmax_pv(j, s_ref[...], False)
            s_ref[...] = s_next
            return c

        lax.fori_loop(0, i, loop_body, 0)
        softmax_pv(i, s_ref[...], True)
        l = l_ref[...]
        o = acc_ref[...] * (1.0 / l)
        o_ref[...] = o
        ob_ref[...] = o.astype(ob_ref.dtype)
        lse_ref[...] = jnp.broadcast_to(m_ref[...] + jnp.log2(l), lse_ref.shape)

    blk = pl.BlockSpec((None, t, LANES), lambda h, i: (h, i, 0))
    whole = pl.BlockSpec((None, seq, LANES), lambda h, i: (h, 0, 0))
    shp = jax.ShapeDtypeStruct((HEADS, seq, LANES), F32)
    return pl.pallas_call(
        body, name="mla_attn_fwd", grid=(HEADS, seq // t),
        in_specs=[blk, whole, whole], out_specs=[blk, blk, blk],
        out_shape=[shp, jax.ShapeDtypeStruct((HEADS, seq, LANES), MXU_DTYPE), shp],
        scratch_shapes=[pltpu.VMEM((t, 1), F32), pltpu.VMEM((t, 1), F32), pltpu.VMEM((t, LANES), F32), pltpu.VMEM((t, t), F32)],
        compiler_params=_params("parallel", "arbitrary"),
    )(q, k, v)


def _mla_attn_bwd(q, k, v, o, lse, do, t=512):
    _, seq, _ = q.shape
    t = min(t, seq)
    nb = seq // t

    def body(q_ref, k_ref, v_ref, o_ref, lse_ref, do_ref, dq_ref, dk_ref, dv_ref, dl_ref, dka_ref, dva_ref):
        dq_ref[...] = jnp.zeros_like(dq_ref)

        def delta_body(i, c):
            rows = pl.ds(pl.multiple_of(i * t, t), t)
            dl_ref[rows, :] = jnp.sum(do_ref[rows, :] * o_ref[rows, :], axis=1, keepdims=True)
            return c

        lax.fori_loop(0, nb, delta_body, 0)

        def kblock(j, c):
            krows = pl.ds(pl.multiple_of(j * t, t), t)
            kb = k_ref[krows, :]
            vb = v_ref[krows, :]
            dka_ref[...] = jnp.zeros_like(dka_ref)
            dva_ref[...] = jnp.zeros_like(dva_ref)

            def qstep(i, masked):
                th = t // 2
                rows = [pl.ds(pl.multiple_of(i * t + hf * th, th), th) for hf in range(2)]
                qs = [q_ref[r, :] for r in rows]
                dos = [do_ref[r, :].astype(MXU_DTYPE) for r in rows]
                ss = [_dot(qb, kb, 1, 1) * MLA_SCALE_LOG2 for qb in qs]
                dps = [_dot(dob, vb, 1, 1) for dob in dos]
                for hf in range(2):
                    s = ss[hf]
                    if masked:
                        row = lax.broadcasted_iota(jnp.int32, (th, t), 0) + hf * th
                        s = jnp.where(row >= lax.broadcasted_iota(jnp.int32, (th, t), 1), s, NEG_BIG)
                    p = jnp.exp2(s - lse_ref[rows[hf], 0:1])
                    dva_ref[...] += _dot(p.astype(MXU_DTYPE), dos[hf], 0, 0)
                    ds = (p * (dps[hf] - dl_ref[rows[hf], :]) * MLA_SCALE).astype(MXU_DTYPE)
                    dka_ref[...] += _dot(ds, qs[hf], 0, 0)
                    dq_ref[rows[hf], :] += _dot(ds, kb, 1, 0)

            qstep(j, True)

            def qloop(i, c2):
                qstep(i, False)
                return c2

            lax.fori_loop(j + 1, nb, qloop, 0)
            dk_ref[krows, :] = dka_ref[...]
            dv_ref[krows, :] = dva_ref[...]
            return c

        lax.fori_loop(0, nb, kblock, 0)

    whole = pl.BlockSpec((None, seq, LANES), lambda h: (h, 0, 0))
    shp = jax.ShapeDtypeStruct((HEADS, seq, LANES), F32)
    return pl.pallas_call(
        body, name="mla_attn_bwd", grid=(HEADS,),
        in_specs=[whole] * 6, out_specs=[whole] * 3, out_shape=[shp] * 3,
        scratch_shapes=[pltpu.VMEM((seq, 1), F32), pltpu.VMEM((t, LANES), F32), pltpu.VMEM((t, LANES), F32)],
        compiler_params=_params("parallel"),
    )(q, k, v, o, lse, do)


DIL_CHUNK = DIL_BLOCK * max(d for _, d in DIL_PAIRS)
DIL_PAIR_LANES = 2 * HEAD_DIM
assert DIL_PAIR_LANES == LANES
DIL_UNROLL_FWD = 16
DIL_UNROLL_BWD = 8


def _dil_bias_tables(hp, dil):
    b = DIL_BLOCK
    iq = lax.broadcasted_iota(jnp.int32, (b, 2 * b), 0)
    ik = lax.broadcasted_iota(jnp.int32, (b, 2 * b), 1)
    off = iq + b - ik
    band = (off >= 0) & (off <= b)
    dist = (off * dil).astype(F32)
    every, first = [], []
    for hh in range(2):
        slope = jnp.where(hp == 0, ALIBI_SLOPES[hh], jnp.where(hp == 1, ALIBI_SLOPES[2 + hh],
                          jnp.where(hp == 2, ALIBI_SLOPES[4 + hh], ALIBI_SLOPES[6 + hh]))).astype(F32)
        bias = -slope * dist
        every.append(jnp.where(band, bias, NEG_BIG))
        first.append(jnp.where(band & (ik >= b), bias, NEG_BIG))
    return jnp.concatenate(every, axis=0), jnp.concatenate(first, axis=0)


def _dil_rows(start, dil):
    return pl.ds(start, DIL_BLOCK) if dil == 1 else pl.ds(start, DIL_BLOCK, stride=dil)


def _dil_block_pos(blk, c, dil):
    sc, r = blk // dil, blk % dil
    q0 = sc * (DIL_BLOCK * dil) + r
    kcur0 = c * DIL_CHUNK + q0
    first = kcur0 < DIL_BLOCK * dil
    kprev0 = jnp.where(first, kcur0, kcur0 - DIL_BLOCK * dil)
    return q0, kcur0, kprev0, first


def _pair_cols(hh):
    return slice(HEAD_DIM * hh, HEAD_DIM * (hh + 1))


def _first_head_lanes(shape):
    return lax.broadcasted_iota(jnp.int32, shape, 1) < HEAD_DIM


def _stack_pair(t):
    first = _first_head_lanes(t.shape)
    return jnp.concatenate([jnp.where(first, t, 0.0), jnp.where(first, 0.0, t)], axis=0).astype(MXU_DTYPE)


def _unstack_pair(t):
    rows = t.shape[0] // 2
    return jnp.where(_first_head_lanes((rows, t.shape[1])), t[:rows], t[rows:])


def _pair_column(t):
    return jnp.concatenate([t[:, 0:1], t[:, HEAD_DIM:HEAD_DIM + 1]], axis=0)


def _dil_fwd(h):
    seq = h.shape[0]
    assert seq % DIL_CHUNK == 0
    nblk = DIL_CHUNK // DIL_BLOCK
    rc = 256

    def body(q_ref, k_ref, v_ref, o_ref, ob_ref, lse_ref, *scr):
        o_scr, l_scr = scr[:3], scr[3:]
        hp, c = pl.program_id(0), pl.program_id(1)
        for bi, (_, dil) in enumerate(DIL_PAIRS):
            tables = _dil_bias_tables(hp, dil)

            def block(blk, carry, bi=bi, dil=dil, tables=tables):
                q0, kcur0, kprev0, first = _dil_block_pos(blk, c, dil)
                q2 = _stack_pair(q_ref[_dil_rows(q0, dil), :] * DIL_SCALE)
                kcat = jnp.concatenate([k_ref[_dil_rows(kprev0, dil), :], k_ref[_dil_rows(kcur0, dil), :]], axis=0).astype(MXU_DTYPE)
                vcat = jnp.concatenate([v_ref[_dil_rows(kprev0, dil), :], v_ref[_dil_rows(kcur0, dil), :]], axis=0).astype(MXU_DTYPE)
                s = _dot(q2, kcat, 1, 1) + jnp.where(first, tables[1], tables[0])
                mx = jnp.max(s, axis=1, keepdims=True)
                p = jnp.exp(s - mx)
                l = jnp.sum(p, axis=1, keepdims=True)
                o_scr[bi][_dil_rows(q0, dil), :] = _unstack_pair(_dot(p.astype(MXU_DTYPE), vcat, 1, 0) * (1.0 / l))
                l_scr[bi][_dil_rows(q0, dil), :] = _unstack_pair(jnp.broadcast_to(mx + jnp.log(l), (2 * DIL_BLOCK, LANES)))
                return carry

            lax.fori_loop(0, nblk, block, 0, unroll=DIL_UNROLL_FWD)

        def combine(i, carry):
            rows = pl.ds(pl.multiple_of(i * rc, rc), rc)
            ls = [l_scr[bi][rows, :] for bi in range(3)]
            mx = jnp.maximum(jnp.maximum(ls[0], ls[1]), ls[2])
            es = [jnp.exp(l - mx) for l in ls]
            den = es[0] + es[1] + es[2]
            o = (es[0] * o_scr[0][rows, :] + es[1] * o_scr[1][rows, :] + es[2] * o_scr[2][rows, :]) / den
            o_ref[rows, :] = o
            ob_ref[rows, :] = o.astype(ob_ref.dtype)
            lse_ref[rows, :] = mx + jnp.log(den)
            return carry

        lax.fori_loop(0, DIL_CHUNK // rc, combine, 0)

    nq = DIL_WIDTH // LANES
    chunk = lambda off: pl.BlockSpec((DIL_CHUNK, LANES), lambda hp, c: (c, off + hp))
    whole = lambda off: pl.BlockSpec((seq, LANES), lambda hp, c: (0, off + hp))
    shp = jax.ShapeDtypeStruct((seq, DIL_WIDTH), F32)
    return pl.pallas_call(
        body, name="dil_fwd", grid=(nq, seq // DIL_CHUNK),
        in_specs=[chunk(nq), whole(2 * nq), whole(3 * nq)], out_specs=[chunk(0), chunk(0), chunk(0)],
        out_shape=[shp, jax.ShapeDtypeStruct((seq, DIL_WIDTH), MXU_DTYPE), shp],
        scratch_shapes=[pltpu.VMEM((DIL_CHUNK, LANES), F32)] * 6,
        compiler_params=_params("parallel", "arbitrary"),
    )(h, h, h)


def _dil_bwd(h, o, lse, do):
    seq = h.shape[0]
    nblk = DIL_CHUNK // DIL_BLOCK
    nchunk = seq // DIL_CHUNK
    rc = 256

    def body(q_ref, k_ref, v_ref, o_ref, lse_ref, do_ref, dq_out, dk_out, dv_out, dl_scr, dq_ref, dk_ref, dv_ref):
        hp, c = pl.program_id(0), pl.program_id(1)

        @pl.when(c == 0)
        def _():
            dk_ref[...] = jnp.zeros_like(dk_ref)
            dv_ref[...] = jnp.zeros_like(dv_ref)

        def delta(i, carry):
            rows = pl.ds(pl.multiple_of(i * rc, rc), rc)
            prod = do_ref[rows, :] * o_ref[rows, :]
            dl_scr[rows, :] = jnp.concatenate(
                [jnp.broadcast_to(jnp.sum(prod[:, _pair_cols(hh)], axis=1, keepdims=True), (rc, HEAD_DIM)) for hh in range(2)], axis=1)
            return carry

        lax.fori_loop(0, DIL_CHUNK // rc, delta, 0)

        for bi, (_, dil) in enumerate(DIL_PAIRS):
            tables = _dil_bias_tables(hp, dil)

            def block(blk, carry, bi=bi, dil=dil, tables=tables):
                q0, kcur0, kprev0, first = _dil_block_pos(blk, c, dil)
                qrows = _dil_rows(q0, dil)
                q2 = _stack_pair(q_ref[qrows, :] * DIL_SCALE)
                kcat = jnp.concatenate([k_ref[_dil_rows(kprev0, dil), :], k_ref[_dil_rows(kcur0, dil), :]], axis=0).astype(MXU_DTYPE)
                vcat = jnp.concatenate([v_ref[_dil_rows(kprev0, dil), :], v_ref[_dil_rows(kcur0, dil), :]], axis=0).astype(MXU_DTYPE)
                do2 = _stack_pair(do_ref[qrows, :])
                s = _dot(q2, kcat, 1, 1) + jnp.where(first, tables[1], tables[0])
                p = jnp.exp(s - _pair_column(lse_ref[qrows, :]))
                dp = _dot(do2, vcat, 1, 1)
                ds = (p * (dp - _pair_column(dl_scr[qrows, :]))).astype(MXU_DTYPE)
                dq_b = _unstack_pair(_dot(ds, kcat, 1, 0)) * DIL_SCALE
                dk_b = _dot(ds, q2, 0, 0)
                dv_b = _dot(p.astype(MXU_DTYPE), do2, 0, 0)
                if bi == 0:
                    dq_ref[qrows, :] = dq_b
                else:
                    dq_ref[qrows, :] += dq_b
                dk_ref[_dil_rows(kprev0, dil), :] += dk_b[:DIL_BLOCK]
                dv_ref[_dil_rows(kprev0, dil), :] += dv_b[:DIL_BLOCK]
                dk_ref[_dil_rows(kcur0, dil), :] += dk_b[DIL_BLOCK:]
                dv_ref[_dil_rows(kcur0, dil), :] += dv_b[DIL_BLOCK:]
                return carry

            lax.fori_loop(0, nblk, block, 0, unroll=DIL_UNROLL_BWD)

        dq_out[...] = dq_ref[...].astype(dq_out.dtype)

        @pl.when(c == nchunk - 1)
        def _():
            dk_out[...] = dk_ref[...].astype(dk_out.dtype)
            dv_out[...] = dv_ref[...].astype(dv_out.dtype)

    nq = DIL_WIDTH // LANES
    chunk = lambda off: pl.BlockSpec((DIL_CHUNK, LANES), lambda hp, c: (c, off + hp))
    whole = lambda off: pl.BlockSpec((seq, LANES), lambda hp, c: (0, off + hp))
    shp = jax.ShapeDtypeStruct((seq, DIL_WIDTH), MXU_DTYPE)
    return pl.pallas_call(
        body, name="dil_bwd", grid=(nq, nchunk),
        in_specs=[chunk(nq), whole(2 * nq), whole(3 * nq), chunk(0), chunk(0), chunk(0)],
        out_specs=[chunk(0), whole(0), whole(0)], out_shape=[shp, shp, shp],
        scratch_shapes=[pltpu.VMEM((DIL_CHUNK, LANES), F32), pltpu.VMEM((DIL_CHUNK, LANES), F32),
                        pltpu.VMEM((seq, LANES), F32), pltpu.VMEM((seq, LANES), F32)],
        compiler_params=_params("parallel", "arbitrary"),
    )(h, h, h, o, lse, do)


DH_PART = IN_PAD // 4


def _mm_dx0(parts, w_in_t, res, tm=1024, after=()):
    seq, d = res.shape
    tm = min(tm, seq)
    n_after = len(after)

    def body(a0, a1, a2, a3, b_ref, r_ref, *rest):
        o_ref = rest[n_after]
        acc = _dot(a0[...], b_ref[0:DH_PART, :], 1, 0)
        for c, a in enumerate((a1, a2, a3), start=1):
            acc += _dot(a[...], b_ref[DH_PART * c:DH_PART * (c + 1), :], 1, 0)
        o_ref[...] = acc + DN_ALPHA * r_ref[...]

    blk = pl.BlockSpec((tm, DH_PART), lambda i: (i, 0))
    row = pl.BlockSpec((tm, d), lambda i: (i, 0))
    return pl.pallas_call(
        body, name="mm_dx0", grid=(seq // tm,),
        in_specs=[blk] * 4 + [pl.BlockSpec((IN_PAD, d), lambda i: (0, 0), pipeline_mode=pl.Buffered(1)), row] + [_ANY_SPEC] * n_after,
        out_specs=row, out_shape=jax.ShapeDtypeStruct((seq, d), F32), compiler_params=_params("parallel"),
    )(*parts, w_in_t, res, *after)


def _mm_dw_in(parts, x0b, tk=1024):
    seq, d = x0b.shape
    tk = min(tk, seq)
    nk = seq // tk

    def body(a0, a1, a2, a3, b_ref, o_ref, acc_ref):
        kk = pl.program_id(0)

        @pl.when(kk == 0)
        def _():
            acc_ref[...] = jnp.zeros_like(acc_ref)

        b = b_ref[...]
        for c, a in enumerate((a0, a1, a2, a3)):
            acc_ref[DH_PART * c:DH_PART * (c + 1), :] += _dot(a[...], b, 0, 0)

        @pl.when(kk == nk - 1)
        def _():
            o_ref[...] = acc_ref[...].astype(o_ref.dtype)

    blk = pl.BlockSpec((tk, DH_PART), lambda kk: (kk, 0))
    return pl.pallas_call(
        body, name="mm_dw_in", grid=(nk,), in_specs=[blk] * 4 + [pl.BlockSpec((tk, d), lambda kk: (kk, 0))],
        out_specs=pl.BlockSpec((IN_PAD, d), lambda kk: (0, 0)), out_shape=jax.ShapeDtypeStruct((IN_PAD, d), MXU_DTYPE),
        scratch_shapes=[pltpu.VMEM((IN_PAD, d), F32)], compiler_params=_params("arbitrary"),
    )(*parts, x0b)


def _ln_stats(z):
    mu = jnp.mean(z, axis=-1, keepdims=True)
    zc = z - mu
    r = lax.rsqrt(jnp.mean(zc * zc, axis=-1, keepdims=True) + LN_EPS)
    return zc * r, r


def _ln_bwd_math(dy, xh, r, g):
    dxh = dy * g
    return r * (dxh - jnp.mean(dxh, axis=-1, keepdims=True) - xh * jnp.mean(dxh * xh, axis=-1, keepdims=True))


def _mix_ln1(o_mla, o_dil, w_o_mla, w_o_dil, x0, g, b, tm=512):
    seq, d = x0.shape
    tm = min(tm, seq)

    def body(om_ref, od_ref, wm_ref, wd_ref, x_ref, g_ref, b_ref, z_ref, y_ref, yb_ref):
        mix = _dot(od_ref[...], wd_ref[...], 1, 0)
        for hd in range(HEADS):
            mix += _dot(om_ref[hd], wm_ref[LANES * hd:LANES * (hd + 1), :], 1, 0)
        z = DN_ALPHA * x_ref[...] + mix
        xh, _ = _ln_stats(z)
        y = xh * g_ref[...] + b_ref[...]
        z_ref[...] = z
        y_ref[...] = y
        yb_ref[...] = y.astype(yb_ref.dtype)

    blk = pl.BlockSpec((tm, d), lambda i: (i, 0))
    vec = pl.BlockSpec((1, d), lambda i: (0, 0))
    shp = jax.ShapeDtypeStruct((seq, d), F32)
    return pl.pallas_call(
        body, name="mix_ln1", grid=(seq // tm,),
        in_specs=[pl.BlockSpec((HEADS, tm, LANES), lambda i: (0, i, 0)), pl.BlockSpec((tm, DIL_WIDTH), lambda i: (i, 0)),
                  pl.BlockSpec((HEADS * LANES, d), lambda i: (0, 0)), pl.BlockSpec((DIL_WIDTH, d), lambda i: (0, 0)), blk, vec, vec],
        out_specs=[blk, blk, blk], out_shape=[shp, shp, jax.ShapeDtypeStruct((seq, d), MXU_DTYPE)],
        compiler_params=_params("parallel"))(o_mla, o_dil, w_o_mla, w_o_dil, x0, g, b)


def _ln_bwd(dy, z, g, name, tm=512, after=()):
    seq, d = z.shape
    tm = min(tm, seq)
    n_after = len(after)

    def body(dy_ref, z_ref, g_ref, *rest):
        dz_ref, dzb_ref, dg_ref, db_ref = rest[n_after:]

        @pl.when(pl.program_id(0) == 0)
        def _():
            dg_ref[...] = jnp.zeros_like(dg_ref)
            db_ref[...] = jnp.zeros_like(db_ref)

        dyb = dy_ref[...]
        xh, r = _ln_stats(z_ref[...])
        dg_ref[...] += jnp.sum(dyb * xh, axis=0, keepdims=True)
        db_ref[...] += jnp.sum(dyb, axis=0, keepdims=True)
        dz = _ln_bwd_math(dyb, xh, r, g_ref[...])
        dz_ref[...] = dz
        dzb_ref[...] = dz.astype(dzb_ref.dtype)

    blk = pl.BlockSpec((tm, d), lambda i: (i, 0))
    vec = pl.BlockSpec((1, d), lambda i: (0, 0))
    return pl.pallas_call(
        body, name=name, grid=(seq // tm,), in_specs=[blk, blk, vec] + [_ANY_SPEC] * n_after, out_specs=[blk, blk, vec, vec],
        out_shape=[jax.ShapeDtypeStruct((seq, d), F32), jax.ShapeDtypeStruct((seq, d), MXU_DTYPE),
                   jax.ShapeDtypeStruct((1, d), F32), jax.ShapeDtypeStruct((1, d), F32)],
        compiler_params=_params("arbitrary"))(dy, z, g, *after)


def _ln2_loss_bwd(x1, ffn, target, g, b, tm=512):
    seq, d = x1.shape
    tm = min(tm, seq)

    def body(x_ref, f_ref, t_ref, g_ref, b_ref, dz_ref, dzb_ref, loss_ref, dg_ref, db_ref):
        @pl.when(pl.program_id(0) == 0)
        def _():
            loss_ref[...] = jnp.zeros_like(loss_ref)
            dg_ref[...] = jnp.zeros_like(dg_ref)
            db_ref[...] = jnp.zeros_like(db_ref)

        gv = g_ref[...]
        z = DN_ALPHA * x_ref[...] + f_ref[...]
        xh, r = _ln_stats(z)
        err = (xh * gv + b_ref[...]) - t_ref[...]
        loss_ref[...] += 0.5 * jnp.sum(jnp.mean(err * err, axis=-1, keepdims=True), axis=0, keepdims=True)
        dy = err * (1.0 / d)
        dg_ref[...] += jnp.sum(dy * xh, axis=0, keepdims=True)
        db_ref[...] += jnp.sum(dy, axis=0, keepdims=True)
        dz = _ln_bwd_math(dy, xh, r, gv)
        dz_ref[...] = dz
        dzb_ref[...] = dz.astype(dzb_ref.dtype)

    blk = pl.BlockSpec((tm, d), lambda i: (i, 0))
    vec = pl.BlockSpec((1, d), lambda i: (0, 0))
    return pl.pallas_call(
        body, name="ln2_loss_bwd", grid=(seq // tm,), in_specs=[blk, blk, blk, vec, vec],
        out_specs=[blk, blk, pl.BlockSpec((1, LANES), lambda i: (0, 0)), vec, vec],
        out_shape=[jax.ShapeDtypeStruct((seq, d), F32), jax.ShapeDtypeStruct((seq, d), MXU_DTYPE),
                   jax.ShapeDtypeStruct((1, LANES), F32),
                   jax.ShapeDtypeStruct((1, d), F32), jax.ShapeDtypeStruct((1, d), F32)],
        compiler_params=_params("arbitrary"))(x1, ffn, target, g, b)


HALO = 16


def _conv_rows(e, w_ref, b_ref):
    y = b_ref[...] + w_ref[0:1, :] * pltpu.roll(e, 2, 0)
    y = y + w_ref[1:2, :] * pltpu.roll(e, 1, 0)
    return y + w_ref[2:3, :] * e


_GELU_C = math.sqrt(2.0 / math.pi)
_GELU_A = 0.044715


def _gelu(x):
    return 0.5 * x * (1.0 + jnp.tanh(_GELU_C * (x + _GELU_A * (x * x * x))))


CONV_TN = 256


def _ffn_interleave(a, axis):
    shp = a.shape
    a = a.reshape(shp[:axis] + (2, D_FF // CONV_TN, CONV_TN) + shp[axis + 1:])
    return jnp.swapaxes(a, axis, axis + 1).reshape(shp)


def _ffn_deinterleave(a, axis):
    shp = a.shape
    a = a.reshape(shp[:axis] + (D_FF // CONV_TN, 2, CONV_TN) + shp[axis + 1:])
    return jnp.swapaxes(a, axis, axis + 1).reshape(shp)


def _conv_gate_fwd(u, conv_w, conv_b, tm=1024):
    seq = u.shape[0]
    tm = min(tm, seq)
    tn = CONV_TN

    def body(u_ref, up_ref, w_ref, b_ref, o_ref):
        first = pl.program_id(0) == 0
        e = jnp.concatenate([jnp.where(first, 0.0, up_ref[...]), u_ref[...]], axis=0)
        y = _conv_rows(e, w_ref, b_ref)[HALO:]
        o_ref[...] = (_gelu(y[:, tn:]) * y[:, :tn]).astype(o_ref.dtype)

    hb = tm // HALO
    return pl.pallas_call(
        body, name="conv_gate_fwd", grid=(seq // tm, D_FF // tn),
        in_specs=[pl.BlockSpec((tm, 2 * tn), lambda i, j: (i, j)),
                  pl.BlockSpec((HALO, 2 * tn), lambda i, j: (jnp.maximum(i * hb - 1, 0), j)),
                  pl.BlockSpec((3, 2 * tn), lambda i, j: (0, j)), pl.BlockSpec((1, 2 * tn), lambda i, j: (0, j))],
        out_specs=pl.BlockSpec((tm, tn), lambda i, j: (i, j)), out_shape=jax.ShapeDtypeStruct((seq, D_FF), MXU_DTYPE),
        compiler_params=_params("parallel", "parallel"),
    )(u, u, conv_w, conv_b)


def _conv_gate_bwd(u, d_act, conv_w, conv_b, tm=1024):
    seq = u.shape[0]
    tm = min(tm, seq)
    tn = CONV_TN
    ni = seq // tm
    rows_e = tm + 2 * HALO

    def body(u_ref, up_ref, un_ref, da_ref, dan_ref, w_ref, b_ref, du_ref, dw_ref, db_ref):
        i = pl.program_id(1)
        first, last = i == 0, i == ni - 1

        @pl.when(i == 0)
        def _():
            dw_ref[...] = jnp.zeros_like(dw_ref)
            db_ref[...] = jnp.zeros_like(db_ref)

        e = jnp.concatenate([jnp.where(first, 0.0, up_ref[...]), u_ref[...], jnp.where(last, 0.0, un_ref[...])], axis=0)
        y = _conv_rows(e, w_ref, b_ref)
        ya, yg = y[:, :tn], y[:, tn:]
        dact = jnp.concatenate([jnp.zeros((HALO, tn), F32), da_ref[...].astype(F32),
                                jnp.where(last, 0.0, dan_ref[...].astype(F32))], axis=0)
        th = jnp.tanh(_GELU_C * (yg + _GELU_A * (yg * yg * yg)))
        gelu = 0.5 * yg * (1.0 + th)
        gelu_grad = 0.5 * (1.0 + th) + 0.5 * yg * (1.0 - th * th) * (_GELU_C * (1.0 + 3.0 * _GELU_A * (yg * yg)))
        dy = jnp.concatenate([dact * gelu, dact * ya * gelu_grad], axis=1)
        du = w_ref[2:3, :] * dy + w_ref[1:2, :] * pltpu.roll(dy, rows_e - 1, 0) + w_ref[0:1, :] * pltpu.roll(dy, rows_e - 2, 0)
        du_ref[...] = du[HALO:HALO + tm].astype(du_ref.dtype)
        dyt = dy[HALO:HALO + tm]
        dw_ref[0:1, :] += jnp.sum(dyt * pltpu.roll(e, 2, 0)[HALO:HALO + tm], axis=0, keepdims=True)
        dw_ref[1:2, :] += jnp.sum(dyt * pltpu.roll(e, 1, 0)[HALO:HALO + tm], axis=0, keepdims=True)
        dw_ref[2:3, :] += jnp.sum(dyt * e[HALO:HALO + tm], axis=0, keepdims=True)
        db_ref[...] += jnp.sum(dyt, axis=0, keepdims=True)

    hb = tm // HALO
    nh = seq // HALO
    prev = lambda j, i: (jnp.maximum(i * hb - 1, 0), j)
    nxt = lambda j, i: (jnp.minimum((i + 1) * hb, nh - 1), j)
    return pl.pallas_call(
        body, name="conv_gate_bwd", grid=(D_FF // tn, ni),
        in_specs=[pl.BlockSpec((tm, 2 * tn), lambda j, i: (i, j)), pl.BlockSpec((HALO, 2 * tn), prev),
                  pl.BlockSpec((HALO, 2 * tn), nxt), pl.BlockSpec((tm, tn), lambda j, i: (i, j)), pl.BlockSpec((HALO, tn), nxt),
                  pl.BlockSpec((3, 2 * tn), lambda j, i: (0, j)), pl.BlockSpec((1, 2 * tn), lambda j, i: (0, j))],
        out_specs=[pl.BlockSpec((tm, 2 * tn), lambda j, i: (i, j)), pl.BlockSpec((3, 2 * tn), lambda j, i: (0, j)),
                   pl.BlockSpec((1, 2 * tn), lambda j, i: (0, j))],
        out_shape=[jax.ShapeDtypeStruct((seq, 2 * D_FF), MXU_DTYPE), jax.ShapeDtypeStruct((3, 2 * D_FF), F32),
                   jax.ShapeDtypeStruct((1, 2 * D_FF), F32)],
        compiler_params=_params("parallel", "arbitrary"),
    )(u, u, u, d_act, d_act, conv_w, conv_b)


def _pad_heads(w, width):
    w = jnp.transpose(w, (1, 0, 2))
    return jnp.pad(w, ((0, 0), (0, 0), (0, LANES - width))).astype(MXU_DTYPE)


def _heads_major(a):
    return jnp.transpose(a, (1, 0, 2)).reshape(-1, a.shape[2])


def _heads_minor(a, heads):
    return jnp.transpose(a.reshape(heads, -1, a.shape[1]), (1, 0, 2))


def _unpad_heads(d, width):
    return jnp.transpose(d[:, :, :width], (1, 0, 2))


def _split_pad_rows(w_t):
    z = lambda n: jnp.zeros((n, w_t.shape[1]), w_t.dtype)
    return jnp.concatenate([w_t[:384], z(64), w_t[384:416], z(32), w_t[416:]], axis=0)


def _split_unpad_rows(w_p):
    return jnp.concatenate([w_p[:384], w_p[448:480], w_p[512:]], axis=0)


def _pad_w_o(w_o):
    mla = jnp.pad(w_o[:512].reshape(HEADS, HEAD_DIM, D_MODEL), ((0, 0), (0, LANES - HEAD_DIM), (0, 0)))
    return mla.reshape(HEADS * LANES, D_MODEL).astype(MXU_DTYPE), w_o[512:].astype(MXU_DTYPE)


def _unpad_w_o(d_mla, d_dil):
    return jnp.concatenate([d_mla.reshape(HEADS, LANES, D_MODEL)[:, :HEAD_DIM].reshape(512, D_MODEL), d_dil], axis=0)


def _row(v):
    return v.reshape(1, -1).astype(F32)


def _compute_weights(w):
    w_o_mla, w_o_dil = _pad_w_o(w["w_o"])
    return dict(
        w_in_t=_split_pad_rows(w["w_in"].T).astype(MXU_DTYPE), wq=_pad_heads(w["w_uq"], NOPE_DIM + ROPE_DIM),
        wk=_pad_heads(w["w_uk"], NOPE_DIM), wv=_pad_heads(w["w_uv"], HEAD_DIM), w_o_mla=w_o_mla, w_o_dil=w_o_dil,
        w_up_t=_ffn_interleave(w["w_up"].T, 0).astype(MXU_DTYPE), w_down=w["w_down"].astype(MXU_DTYPE),
        conv_w=_ffn_interleave(w["conv_w"].astype(F32), 1),
        g_cq=_row(w["g_cq"]), g_ckv=_row(w["g_ckv"]), ln1_g=_row(w["ln1_g"]), ln1_b=_row(w["ln1_b"]),
        conv_b=_ffn_interleave(_row(w["conv_b"]), 1), ln2_g=_row(w["ln2_g"]), ln2_b=_row(w["ln2_b"]))


def _natural_grads(g):
    return dict(
        w_in=_split_unpad_rows(g["w_in_t"]).T, g_cq=g["g_cq"].reshape(-1), g_ckv=g["g_ckv"].reshape(-1),
        w_uq=_unpad_heads(g["wq"], NOPE_DIM + ROPE_DIM), w_uk=_unpad_heads(g["wk"], NOPE_DIM),
        w_uv=_unpad_heads(g["wv"], HEAD_DIM), w_o=_unpad_w_o(g["w_o_mla"], g["w_o_dil"]), ln1_g=g["ln1_g"].reshape(-1),
        ln1_b=g["ln1_b"].reshape(-1), w_up=_ffn_deinterleave(g["w_up_t"], 0).T, conv_w=_ffn_deinterleave(g["conv_w"], 1),
        conv_b=_ffn_deinterleave(g["conv_b"], 1).reshape(-1),
        w_down=g["w_down"], ln2_g=g["ln2_g"].reshape(-1), ln2_b=g["ln2_b"].reshape(-1))


def _layer_grads(x0, target, cw, first_after=(), late_weights=None, on_grads=None):
    seq = x0.shape[0]
    ctab, stab = _rope_tables(seq)
    gq, gk = cw["g_cq"], cw["g_ckv"]
    wq, wk, wv = cw["wq"], cw["wk"], cw["wv"]
    x0b = x0.astype(MXU_DTYPE)
    notify = (lambda stage, grads: ()) if on_grads is None else on_grads

    h = _mm(x0b, cw["w_in_t"], name="mm_h", tb=True, tm=1024, tn=IN_PAD, tk=1024, after=first_after)
    qf, kf, vp = _mla_prep(h, gq, gk, wq, wk, wv, ctab, stab)
    o_mla, o_mla_b, lse_mla = _mla_attn_fwd(qf, kf, vp)
    o_dil, o_dil_b, lse_dil = _dil_fwd(h)
    if late_weights is not None:
        cw = {**cw, **late_weights(o_mla_b)}
    cb = cw["conv_b"]
    z1, x1, x1b = _mix_ln1(o_mla_b, o_dil_b, cw["w_o_mla"], cw["w_o_dil"], x0, cw["ln1_g"], cw["ln1_b"])
    u = _mm(x1b, cw["w_up_t"], name="mm_up", tb=True, tm=512, tn=2 * D_FF, tk=1024)
    act = _conv_gate_fwd(u, cw["conv_w"], cb)
    ffn = _mm(act, cw["w_down"], name="mm_down", tm=1024, tn=1024, tk=2816)
    dz2, dz2b, loss, d_ln2_g, d_ln2_b = _ln2_loss_bwd(x1, ffn, target, cw["ln2_g"], cw["ln2_b"])

    d_act = _mm(dz2b, cw["w_down"], name="mm_d_act", tb=True, out_dtype=MXU_DTYPE, tm=1024, tn=D_FF, tk=1024)
    d_w_down = _mm(act, dz2b, name="mm_dw_down", ta=True, out_dtype=MXU_DTYPE, tm=1408, tn=1024, tk=1024)
    du, d_conv_w, d_conv_b = _conv_gate_bwd(u, d_act, cw["conv_w"], cb)
    dx1 = _mm(du, cw["w_up_t"], name="mm_dx1", res=dz2, res_scale=DN_ALPHA, tm=512, tn=1024, tk=2 * D_FF)
    d_w_up_t = _mm(du, x1b, name="mm_dw_up", ta=True, out_dtype=MXU_DTYPE, tm=1408, tn=1024, tk=2048)
    grads = dict(w_up_t=d_w_up_t, w_down=d_w_down, conv_w=d_conv_w, conv_b=d_conv_b, ln2_g=d_ln2_g, ln2_b=d_ln2_b)
    dz1, dz1b, d_ln1_g, d_ln1_b = _ln_bwd(dx1, z1, cw["ln1_g"], "ln1_bwd", after=notify("ffn", grads))
    do_mla = _mm_do_mla(dz1b, cw["w_o_mla"])
    do_dil = _mm(dz1b, cw["w_o_dil"], name="mm_do_dil", tb=True, tm=1024, tn=512, tk=1024)
    d_w_o_mla = _mm_dw_o_mla(o_mla_b, dz1b)
    d_w_o_dil = _mm(o_dil_b, dz1b, name="mm_dw_o_dil", ta=True, out_dtype=MXU_DTYPE, tm=512, tn=1024, tk=1024)
    grads.update(w_o_mla=d_w_o_mla, w_o_dil=d_w_o_dil, ln1_g=d_ln1_g, ln1_b=d_ln1_b)
    dq_dil, dk_dil, dv_dil = _dil_bwd(h, o_dil, lse_dil, do_dil)
    dqf, dkf, dvf = _mla_attn_bwd(qf, kf, vp, o_mla, lse_mla, do_mla)
    dh_mla, d_wq, d_wk, d_wv, d_gq, d_gk = _mla_prep_bwd(h, gq, gk, wq, wk, wv, ctab, stab, dqf, dkf, dvf,
                                                          after=notify("w_o", grads))
    dh = (dh_mla, dq_dil, dk_dil, dv_dil)
    d_w_in_t = _mm_dw_in(dh, x0b)
    grads.update(w_in_t=d_w_in_t, wq=d_wq, wk=d_wk, wv=d_wv, g_cq=d_gq, g_ckv=d_gk, loss=loss)
    grad_x = _mm_dx0(dh, cw["w_in_t"], dz1, after=notify("rest", grads))
    return loss, grad_x, grads


def _all_gather(blocks, name):
    na = len(blocks)

    def body(*refs):
        ins, outs = refs[:na], refs[na:2 * na]
        send_sems, recv_sems, local_sems = refs[2 * na:]
        x, y, c = lax.axis_index("x"), lax.axis_index("y"), lax.axis_index("c")
        me, sibling = (x, y, c), (x, y, 1 - c)
        chips = [(1 - x, y), (x, 1 - y), (1 - x, 1 - y)]

        def slot(out, pos):
            return out.at[4 * pos[0] + 2 * pos[1] + pos[2]]

        def copy(a, k, block, to, src=None):
            return pltpu.make_async_remote_copy(
                src_ref=slot(outs[a], block) if src is None else src, dst_ref=slot(outs[a], block),
                send_sem=send_sems.at[7 * a + k], recv_sem=recv_sems.at[7 * a + k],
                device_id=to, device_id_type=pl.DeviceIdType.MESH)

        mine = [pltpu.make_async_copy(ins[a], slot(outs[a], me), local_sems.at[a]) for a in range(na)]
        for cp in mine:
            cp.start()
        first = []
        for a in range(na):
            first.append(copy(a, 0, me, sibling, src=ins[a]))
            first += [copy(a, 1 + j, me, (*chip, c), src=ins[a]) for j, chip in enumerate(chips)]
        for cp in first:
            cp.start()
        passed = []
        for j, chip in enumerate(chips):
            for a in range(na):
                copy(a, 1 + j, (*chip, c), me).wait_recv()
                cp = copy(a, 4 + j, (*chip, c), sibling)
                cp.start()
                passed.append(cp)
        for a in range(na):
            copy(a, 0, sibling, me).wait_recv()
            for j, chip in enumerate(chips):
                copy(a, 4 + j, (*chip, 1 - c), me).wait_recv()
        for cp in first + passed:
            cp.wait_send()
        for cp in mine:
            cp.wait()

    any_spec = pl.BlockSpec(memory_space=pl.ANY)
    return pl.pallas_call(
        body, name=name, in_specs=[any_spec] * na, out_specs=[any_spec] * na,
        out_shape=[jax.ShapeDtypeStruct((N_DEV,) + b.shape, b.dtype) for b in blocks],
        scratch_shapes=[pltpu.SemaphoreType.DMA((7 * na,)), pltpu.SemaphoreType.DMA((7 * na,)), pltpu.SemaphoreType.DMA((na,))],
    )(*blocks)


_HBM_SPEC = pl.BlockSpec(memory_space=pltpu.HBM)
_SEM_SPEC = pl.BlockSpec(memory_space=pltpu.SEMAPHORE)
_DATAFLOW = pltpu.CompilerParams(has_side_effects=pltpu.SideEffectType.DATAFLOW_SIDE_EFFECTING)


def _split_copies(ins, lands, send_sems, recv_sems, gather):
    x, y, c = lax.axis_index("x"), lax.axis_index("y"), lax.axis_index("c")
    me = 4 * x + 2 * y + c
    copies = []
    for a in range(len(ins)):
        for d in range(1, N_DEV):
            px, py, pc = x ^ (d >> 2), y ^ ((d >> 1) & 1), c ^ (d & 1)
            copies.append(pltpu.make_async_remote_copy(
                src_ref=ins[a] if gather else ins[a].at[4 * px + 2 * py + pc], dst_ref=lands[a].at[me],
                send_sem=send_sems.at[7 * a + d - 1], recv_sem=recv_sems.at[7 * a + d - 1],
                device_id=(px, py, pc), device_id_type=pl.DeviceIdType.MESH))
    return copies


def _send_start(srcs, gather, name):
    na = len(srcs)
    land_types = [pltpu.HBM(((N_DEV,) + s.shape) if gather else s.shape, s.dtype) for s in srcs]

    def body(*refs):
        ins, lands = refs[:na], refs[na:2 * na]
        send_sems, recv_sems, token = refs[2 * na], refs[2 * na + 1], refs[-1]
        for cp in _split_copies(ins, lands, send_sems, recv_sems, gather):
            cp.start()
        token[...] = jnp.zeros_like(token)

    hbm = lambda a: pltpu.with_memory_space_constraint(a, pltpu.HBM)
    outs = pl.pallas_call(
        body, name=name,
        out_shape=(pltpu.SemaphoreType.DMA((7 * na,)), pltpu.SemaphoreType.DMA((7 * na,)),
                   *[pltpu.HBM(s.shape, s.dtype) for s in srcs], *land_types, jax.ShapeDtypeStruct((8, LANES), F32)),
        in_specs=[_HBM_SPEC] * (2 * na),
        out_specs=(_SEM_SPEC, _SEM_SPEC, *[_HBM_SPEC] * (2 * na), pl.BlockSpec(memory_space=pltpu.VMEM)),
        input_output_aliases={i: 2 + i for i in range(2 * na)}, compiler_params=_DATAFLOW,
    )(*[hbm(s) for s in srcs], *[hbm(lax.empty(t.shape, t.dtype)) for t in land_types])
    return dict(send=outs[0], recv=outs[1], srcs=list(outs[2:2 + na]), lands=list(outs[2 + na:2 + 2 * na]), token=outs[-1],
                gather=gather)


def _send_wait(handle, after, name):
    na = len(handle["srcs"])
    gather = handle["gather"]
    after = list(after)

    def body(*refs):
        ins, lands = refs[:na], refs[na:2 * na]
        send_sems, recv_sems = refs[2 * na], refs[2 * na + 1]
        for cp in _split_copies(ins, lands, send_sems, recv_sems, gather):
            cp.wait_send()
            cp.wait_recv()

    both = handle["srcs"] + handle["lands"]
    outs = pl.pallas_call(
        body, name=name, out_shape=[pltpu.HBM(a.shape, a.dtype) for a in both],
        in_specs=[_HBM_SPEC] * (2 * na) + [_SEM_SPEC, _SEM_SPEC] + [_ANY_SPEC] * len(after),
        out_specs=[_HBM_SPEC] * (2 * na), input_output_aliases={i: i for i in range(2 * na)}, compiler_params=_DATAFLOW,
    )(*both, handle["send"], handle["recv"], *after)
    return list(outs[:na]), list(outs[na:])


def _sum_slots(p_ref):
    g = p_ref[0].astype(F32)
    for s in range(1, p_ref.shape[0]):
        g = g + p_ref[s].astype(F32)
    return g


def _adamw_refs(g, w_ref, m_ref, v_ref, g_out, d_out, m_out, v_out):
    c1 = 1.0 - ADAM_B1 ** ADAM_STEP
    c2 = 1.0 - ADAM_B2 ** ADAM_STEP
    m_new = ADAM_B1 * m_ref[...] + (1.0 - ADAM_B1) * g
    v_new = ADAM_B2 * v_ref[...] + (1.0 - ADAM_B2) * (g * g)
    g_out[...] = g
    m_out[...] = m_new
    v_out[...] = v_new
    d_out[...] = -ADAM_LR * ((m_new / c1) / (jnp.sqrt(v_new / c2) + ADAM_EPS) + ADAM_WD * w_ref[...])


def _adamw(parts, w, m, v, name):
    npart, r, n = parts.shape
    tr = r if r <= 256 else max(t for t in range(16, 257, 16) if r % t == 0)

    def body(p_ref, w_ref, m_ref, v_ref, g_out, d_out, m_out, v_out):
        _adamw_refs(_sum_slots(p_ref), w_ref, m_ref, v_ref, g_out, d_out, m_out, v_out)

    blk = pl.BlockSpec((tr, n), lambda i: (i, 0))
    shp = jax.ShapeDtypeStruct((r, n), F32)
    return pl.pallas_call(
        body, name=name, grid=(r // tr,), in_specs=[pl.BlockSpec((npart, tr, n), lambda i: (0, i, 0)), blk, blk, blk],
        out_specs=[blk] * 4, out_shape=[shp] * 4, compiler_params=_params("parallel"),
    )(parts, w, m, v)


def _adamw_small(parts, ws, ms, vs, loss_parts, name):
    n = len(parts)

    def body(*refs):
        ins, outs = refs[:4 * n + 1], refs[4 * n + 1:]
        for i in range(n):
            _adamw_refs(_sum_slots(ins[i]), ins[n + i], ins[2 * n + i], ins[3 * n + i], *outs[4 * i:4 * i + 4])
        outs[4 * n][...] = _sum_slots(ins[4 * n])

    out_shape = [jax.ShapeDtypeStruct(w.shape, F32) for w in ws for _ in range(4)]
    res = pl.pallas_call(body, name=name, out_shape=out_shape + [jax.ShapeDtypeStruct((1, LANES), F32)],
                         compiler_params=_params())(*parts, *ws, *ms, *vs, loss_parts)
    return [res[4 * i:4 * i + 4] for i in range(n)], res[4 * n]


REPLICATED = ("g_cq", "g_ckv", "w_uk", "w_uv", "ln1_g", "ln1_b", "conv_b", "ln2_g", "ln2_b")
ALL_WEIGHTS = ("w_in", "g_cq", "g_ckv", "w_uq", "w_uk", "w_uv", "w_o", "ln1_g", "ln1_b", "w_up", "conv_w", "conv_b",
               "w_down", "ln2_g", "ln2_b")


def kernel(x, w_in, g_cq, g_ckv, w_uq, w_uk, w_uv, w_o, ln1_g, ln1_b, w_up, conv_w, conv_b, w_down, ln2_g, ln2_b, loss_target, m_w_in, m_g_cq, m_g_ckv, m_w_uq, m_w_uk, m_w_uv, m_w_o, m_ln1_g, m_ln1_b, m_w_up, m_conv_w, m_conv_b, m_w_down, m_ln2_g, m_ln2_b, v_w_in, v_g_cq, v_g_ckv, v_w_uq, v_w_uk, v_w_uv, v_w_o, v_ln1_g, v_ln1_b, v_w_up, v_conv_w, v_conv_b, v_w_down, v_ln2_g, v_ln2_b):
    w = dict(w_in=w_in, g_cq=g_cq, g_ckv=g_ckv, w_uq=w_uq, w_uk=w_uk, w_uv=w_uv, w_o=w_o, ln1_g=ln1_g, ln1_b=ln1_b,
             w_up=w_up, conv_w=conv_w, conv_b=conv_b, w_down=w_down, ln2_g=ln2_g, ln2_b=ln2_b)
    m = dict(w_in=m_w_in, g_cq=m_g_cq, g_ckv=m_g_ckv, w_uq=m_w_uq, w_uk=m_w_uk, w_uv=m_w_uv, w_o=m_w_o, ln1_g=m_ln1_g,
             ln1_b=m_ln1_b, w_up=m_w_up, conv_w=m_conv_w, conv_b=m_conv_b, w_down=m_w_down, ln2_g=m_ln2_g, ln2_b=m_ln2_b)
    v = dict(w_in=v_w_in, g_cq=v_g_cq, g_ckv=v_g_ckv, w_uq=v_w_uq, w_uk=v_w_uk, w_uv=v_w_uv, w_o=v_w_o, ln1_g=v_ln1_g,
             ln1_b=v_ln1_b, w_up=v_w_up, conv_w=v_conv_w, conv_b=v_conv_b, w_down=v_w_down, ln2_g=v_ln2_g, ln2_b=v_ln2_b)
    me = 4 * lax.axis_index("x") + 2 * lax.axis_index("y") + lax.axis_index("c")
    wire = lambda a: a.astype(WIRE_DTYPE)
    pad_taps = lambda a: jnp.pad(a, ((0, 8 - a.shape[0]), (0, 0)))

    own_slot = lambda buf, block: lax.dynamic_update_index_in_dim(buf, block, me, 0)
    blocks = lambda a: wire(a).reshape((N_DEV, a.shape[0] // N_DEV) + a.shape[1:])

    g_in, g_uq, g_conv = _all_gather(
        [wire(w_in).T, _heads_major(wire(w_uq)), pad_taps(conv_w)],
        "gather_weights")
    late = _send_start([wire(w_o), wire(w_up).T, wire(w_down)], True, "gather_late_start")
    r_uq_dev, e_uq = w_uq.shape[0], w_uq.shape[2]
    wq = jnp.transpose(g_uq.reshape(N_DEV, HEADS, r_uq_dev, e_uq), (1, 0, 2, 3)).reshape(HEADS, Q_RANK, e_uq)
    cw = dict(
        w_in_t=_split_pad_rows(g_in.reshape(-1, D_MODEL)).astype(MXU_DTYPE),
        wq=jnp.pad(wq, ((0, 0), (0, 0), (0, LANES - e_uq))).astype(MXU_DTYPE),
        wk=_pad_heads(w_uk, NOPE_DIM), wv=_pad_heads(w_uv, HEAD_DIM),
        conv_w=_ffn_interleave(jnp.transpose(g_conv[:, :conv_w.shape[0]], (1, 0, 2)).reshape(conv_w.shape[0], -1), 1),
        g_cq=_row(g_cq), g_ckv=_row(g_ckv), ln1_g=_row(ln1_g), ln1_b=_row(ln1_b), conv_b=_ffn_interleave(_row(conv_b), 1),
        ln2_g=_row(ln2_g), ln2_b=_row(ln2_b))

    def late_weights(after):
        own, landed = _send_wait(late, [after], "gather_late_wait")
        g_o, g_up, g_down = [own_slot(buf, blk) for buf, blk in zip(landed, own)]
        w_o_mla, w_o_dil = _pad_w_o(g_o.reshape(-1, D_MODEL))
        return dict(w_o_mla=w_o_mla, w_o_dil=w_o_dil, w_up_t=_ffn_interleave(g_up.reshape(-1, D_MODEL), 0).astype(MXU_DTYPE),
                    w_down=g_down.reshape(-1, D_MODEL).astype(MXU_DTYPE))

    sent = {}

    def on_grads(stage, g):
        if stage == "ffn":
            sent[stage] = [_send_start([blocks(_ffn_deinterleave(g["w_up_t"], 0)), blocks(g["w_down"])], False, "exchange_ffn_start")]
        elif stage == "w_o":
            sent[stage] = [_send_start([blocks(_unpad_w_o(g["w_o_mla"], g["w_o_dil"]))], False, "exchange_w_o_start")]
        else:
            d_in = blocks(_split_unpad_rows(g["w_in_t"]))
            d_uq = wire(jnp.transpose(g["wq"][:, :, :e_uq].reshape(HEADS, N_DEV, r_uq_dev, e_uq), (1, 0, 2, 3))
                        ).reshape(N_DEV, HEADS * r_uq_dev, e_uq)
            dense = lambda a, width: wire(a[:, :, :width]).reshape(-1, LANES)
            small = dict(g_cq=g["g_cq"], g_ckv=g["g_ckv"], w_uk=dense(g["wk"], NOPE_DIM), w_uv=dense(g["wv"], HEAD_DIM),
                         ln1_g=g["ln1_g"], ln1_b=g["ln1_b"], conv_b=_ffn_deinterleave(g["conv_b"], 1), ln2_g=g["ln2_g"],
                         ln2_b=g["ln2_b"])
            sent[stage] = [_send_start([d_in, d_uq], False, "exchange_rest_start"),
                           _send_start([small[n] for n in REPLICATED] + [_ffn_deinterleave(g["conv_w"], 1), g["loss"]],
                                       True, "gather_small_start")]
        return [h["token"] for h in sent[stage]]

    _, grad_x, _ = _layer_grads(x[0], loss_target[0], cw, [late["token"]], late_weights, on_grads)

    def landed(handle, after, name):
        own, got = _send_wait(handle, after, name)
        pick = (lambda a: a) if handle["gather"] else (lambda a: lax.dynamic_index_in_dim(a, me, 0, keepdims=False))
        return [own_slot(buf, pick(src)) for buf, src in zip(got, own)]

    out = {}

    def update(name, parts, view=None):
        to2d = {None: lambda a: a, "t": lambda a: a.T, "heads": _heads_major}[view]
        back = {None: lambda a: a, "t": lambda a: a.T, "heads": lambda a: _heads_minor(a, HEADS)}[view]
        res = _adamw(parts, to2d(w[name]), to2d(m[name]), to2d(v[name]), "adamw_" + name)
        for kind, a in zip(("grad", "delta", "new_m", "new_v"), res):
            out[kind, name] = back(a)
        return res[0]

    r_up, r_down = landed(sent["ffn"][0], [grad_x], "exchange_ffn_wait")
    (r_o,) = landed(sent["w_o"][0], [grad_x], "exchange_w_o_wait")
    done = [update("w_up", r_up, "t"), update("w_down", r_down), update("w_o", r_o)]
    r_in, r_uq = landed(sent["rest"][0], done, "exchange_rest_wait")
    *rep_all, cw_all, loss_all = landed(sent["rest"][1], done, "gather_small_wait")
    update("w_in", r_in, "t")
    update("w_uq", r_uq, "heads")
    heads_major = ("w_uk", "w_uv")
    two_d = lambda n, a: _heads_major(a) if n in heads_major else a.reshape(1, -1)
    rep_all = [p.reshape(N_DEV, -1, w[n].shape[2]) if n in heads_major else p for n, p in zip(REPLICATED, rep_all)]
    res, loss_sum = _adamw_small(rep_all, *[[two_d(n, d[n]) for n in REPLICATED] for d in (w, m, v)], loss_all, "adamw_replicated")
    for n, quad in zip(REPLICATED, res):
        for kind, a in zip(("grad", "delta", "new_m", "new_v"), quad):
            out[kind, n] = _heads_minor(a, HEADS) if n in heads_major else a.reshape(w[n].shape)
    loss = loss_sum[0, 0]
    ncw = conv_w.shape[1]
    update("conv_w", lax.dynamic_slice_in_dim(cw_all[:, :conv_w.shape[0]], me * ncw, ncw, axis=2))

    return (loss, grad_x[None], *[out[kind, n] for kind in ("grad", "delta", "new_m", "new_v") for n in ALL_WEIGHTS])
```

```python
import functools
import math

import jax
import jax.numpy as jnp
import numpy as np
from jax import lax
from jax.experimental import pallas as pl
from jax.experimental.pallas import tpu as pltpu

F32 = jnp.float32
MXU_DTYPE = jnp.bfloat16
WIRE_DTYPE = jnp.bfloat16

N_DEV = 8
D_MODEL = 1024
HEADS = 8
HEAD_DIM = 64
LANES = 128
Q_RANK, KV_RANK, ROPE_DIM, NOPE_DIM = 256, 128, 32, 64
DIL_WIDTH = HEADS * HEAD_DIM
IN_WIDTH = 1952
IN_PAD = 2048
D_FF = 2816
ROPE_THETA = 10000.0
DIL_PAIRS = ((128, 1), (512, 4), (2048, 16))
DIL_BLOCK = 128
DN_ALPHA = 2.0 ** 0.25
LN_EPS = 1e-5
RMS_EPS = 1e-6
MLA_SCALE = 1.0 / math.sqrt(NOPE_DIM + ROPE_DIM)
MLA_SCALE_LOG2 = MLA_SCALE * math.log2(math.e)
DIL_SCALE = 1.0 / math.sqrt(HEAD_DIM)
ALIBI_SLOPES = tuple(2.0 ** (-8.0 * (h + 1) / HEADS) for h in range(HEADS))
NEG_BIG = -1e30
ADAM_LR, ADAM_B1, ADAM_B2, ADAM_EPS, ADAM_WD, ADAM_STEP = 0.001, 0.9, 0.999, 1e-08, 0.01, 10
VMEM_LIMIT = 48 * 1024 * 1024


def _params(*sem):
    return pltpu.CompilerParams(dimension_semantics=sem or None, vmem_limit_bytes=VMEM_LIMIT)


def _dot(a, b, ca, cb):
    return lax.dot_general(a, b, (((ca,), (cb,)), ((), ())), preferred_element_type=F32)


_ANY_SPEC = pl.BlockSpec(memory_space=pl.ANY)


def _mm(a, b, *, name, tm, tn, tk, ta=False, tb=False, out_dtype=F32, res=None, res_scale=1.0, after=()):
    m, k = (a.shape[1], a.shape[0]) if ta else a.shape
    n = b.shape[0] if tb else b.shape[1]
    assert (b.shape[1] if tb else b.shape[0]) == k
    tm, tn, tk = min(tm, m), min(tn, n), min(tk, k)
    assert m % tm == 0 and n % tn == 0 and k % tk == 0, (name, m, n, k, tm, tn, tk)
    nk = k // tk
    a_spec = (pl.BlockSpec((tk, tm), lambda i, j, kk: (kk, i)) if ta
              else pl.BlockSpec((tm, tk), lambda i, j, kk: (i, kk)))
    b_mode = dict(pipeline_mode=pl.Buffered(1)) if (tn == n and tk == k) else {}
    b_spec = (pl.BlockSpec((tn, tk), lambda i, j, kk: (j, kk), **b_mode) if tb
              else pl.BlockSpec((tk, tn), lambda i, j, kk: (kk, j), **b_mode))
    o_spec = pl.BlockSpec((tm, tn), lambda i, j, kk: (i, j))
    in_specs = [a_spec, b_spec]
    args = [a, b]
    if res is not None:
        in_specs.append(o_spec)
        args.append(res)
    n_in = len(args) + len(after)
    in_specs += [_ANY_SPEC] * len(after)
    args += list(after)
    ca, cb = (0 if ta else 1), (1 if tb else 0)

    def finish(acc, r_ref, o_ref):
        if r_ref is not None:
            acc = acc + res_scale * r_ref[...]
        o_ref[...] = acc.astype(o_ref.dtype)

    def body(*refs):
        a_ref, b_ref = refs[:2]
        r_ref = refs[2] if res is not None else None
        o_ref = refs[n_in]
        part = _dot(a_ref[...].astype(MXU_DTYPE), b_ref[...].astype(MXU_DTYPE), ca, cb)
        if nk == 1:
            finish(part, r_ref, o_ref)
            return
        acc_ref = refs[-1]
        kk = pl.program_id(2)

        @pl.when(kk == 0)
        def _():
            acc_ref[...] = part

        @pl.when(kk > 0)
        def _():
            acc_ref[...] += part

        @pl.when(kk == nk - 1)
        def _():
            finish(acc_ref[...], r_ref, o_ref)

    return pl.pallas_call(
        body, name=name, grid=(m // tm, n // tn, nk), in_specs=in_specs, out_specs=o_spec,
        out_shape=jax.ShapeDtypeStruct((m, n), out_dtype),
        scratch_shapes=[pltpu.VMEM((tm, tn), F32)] if nk > 1 else [],
        compiler_params=_params("parallel", "parallel", "arbitrary"),
    )(*args)


def _mm_do_mla(dz, w_o_mla, tm=1024):
    seq, d = dz.shape
    tm = min(tm, seq)

    def body(a_ref, b_ref, o_ref):
        a = a_ref[...].astype(MXU_DTYPE)
        for hd in range(HEADS):
            o_ref[hd] = _dot(a, b_ref[LANES * hd:LANES * (hd + 1), :], 1, 1)

    return pl.pallas_call(
        body, name="mm_do_mla", grid=(seq // tm,),
        in_specs=[pl.BlockSpec((tm, d), lambda i: (i, 0)), pl.BlockSpec((HEADS * LANES, d), lambda i: (0, 0))],
        out_specs=pl.BlockSpec((HEADS, tm, LANES), lambda i: (0, i, 0)),
        out_shape=jax.ShapeDtypeStruct((HEADS, seq, LANES), F32), compiler_params=_params("parallel"),
    )(dz, w_o_mla)


def _mm_dw_o_mla(o_mla, dz, tk=1024):
    seq, d = dz.shape
    tk = min(tk, seq)
    nk = seq // tk

    def body(a_ref, b_ref, o_ref, acc_ref):
        kk = pl.program_id(0)

        @pl.when(kk == 0)
        def _():
            acc_ref[...] = jnp.zeros_like(acc_ref)

        b = b_ref[...].astype(MXU_DTYPE)
        for hd in range(HEADS):
            acc_ref[LANES * hd:LANES * (hd + 1), :] += _dot(a_ref[hd].astype(MXU_DTYPE), b, 0, 0)

        @pl.when(kk == nk - 1)
        def _():
            o_ref[...] = acc_ref[...].astype(o_ref.dtype)

    return pl.pallas_call(
        body, name="mm_dw_o_mla", grid=(nk,),
        in_specs=[pl.BlockSpec((HEADS, tk, LANES), lambda kk: (0, kk, 0)), pl.BlockSpec((tk, d), lambda kk: (kk, 0))],
        out_specs=pl.BlockSpec((HEADS * LANES, d), lambda kk: (0, 0)),
        out_shape=jax.ShapeDtypeStruct((HEADS * LANES, d), MXU_DTYPE),
        scratch_shapes=[pltpu.VMEM((HEADS * LANES, d), F32)], compiler_params=_params("arbitrary"),
    )(o_mla, dz)


def _rope_tables(seq):
    half = ROPE_DIM // 2
    f32 = np.float32
    freqs = np.power(f32(ROPE_THETA), -np.arange(half, dtype=f32) / f32(half))
    ang = np.arange(seq, dtype=f32)[:, None] * freqs[None, :]
    cos, sin = np.cos(ang, dtype=f32), np.sin(ang, dtype=f32)
    one = np.ones((seq, NOPE_DIM), f32)
    tail = np.ones((seq, LANES - NOPE_DIM - ROPE_DIM), f32)
    ctab = np.concatenate([one, cos, cos, tail], axis=1)
    stab = np.concatenate([0 * one, -sin, sin, 0 * tail], axis=1)
    return jnp.asarray(ctab), jnp.asarray(stab)


def _rope_swap(t):
    lane = lax.broadcasted_iota(jnp.int32, t.shape, 1)
    half = ROPE_DIM // 2
    return jnp.where(lane < NOPE_DIM + half, pltpu.roll(t, LANES - half, 1), pltpu.roll(t, half, 1))


def _rope(t, ctab, stab):
    return t * ctab + _rope_swap(t) * stab


def _rope_inv(t, ctab, stab):
    return t * ctab - _rope_swap(t) * stab


def _rms(x, g):
    r = lax.rsqrt(jnp.mean(x * x, axis=-1, keepdims=True) + RMS_EPS)
    xh = x * r
    return xh, r, xh * g


def _mla_prep(h, g_cq, g_ckv, wq, wk, wv, ctab, stab, tm=512):
    seq = h.shape[0]
    tm = min(tm, seq)

    def body(h_ref, gq_ref, gk_ref, wq_ref, wk_ref, wv_ref, c_ref, s_ref, q_out, k_out, v_out):
        hb = h_ref[...]
        ctab_, stab_ = c_ref[...], s_ref[...]
        _, _, cqn = _rms(hb[:, :Q_RANK], gq_ref[...])
        _, _, ckn = _rms(hb[:, Q_RANK:Q_RANK + KV_RANK], gk_ref[...])
        cqn = cqn.astype(MXU_DTYPE)
        ckn = ckn.astype(MXU_DTYPE)
        krr = _rope(hb[:, Q_RANK + KV_RANK:], ctab_, stab_)
        for hd in range(HEADS):
            q = _dot(cqn, wq_ref[hd], 1, 0)
            q_out[hd] = _rope(q, ctab_, stab_).astype(q_out.dtype)
            k_out[hd] = (_dot(ckn, wk_ref[hd], 1, 0) + krr).astype(k_out.dtype)
            v_out[hd] = _dot(ckn, wv_ref[hd], 1, 0).astype(v_out.dtype)

    full = lambda *shape: pl.BlockSpec(shape, lambda i: (0,) * len(shape))
    slab = pl.BlockSpec((HEADS, tm, LANES), lambda i: (0, i, 0))
    shp = jax.ShapeDtypeStruct((HEADS, seq, LANES), MXU_DTYPE)
    return pl.pallas_call(
        body, name="mla_prep", grid=(seq // tm,),
        in_specs=[pl.BlockSpec((tm, 512), lambda i: (i, 0)), full(1, Q_RANK), full(1, KV_RANK),
                  full(HEADS, Q_RANK, LANES), full(HEADS, KV_RANK, LANES), full(HEADS, KV_RANK, LANES),
                  pl.BlockSpec((tm, LANES), lambda i: (i, 0)), pl.BlockSpec((tm, LANES), lambda i: (i, 0))],
        out_specs=[slab, slab, slab], out_shape=[shp, shp, shp],
        compiler_params=_params("parallel"),
    )(h, g_cq, g_ckv, wq, wk, wv, ctab, stab)


def _mla_prep_bwd(h, g_cq, g_ckv, wq, wk, wv, ctab, stab, dq, dk, dv, tm=512, after=()):
    seq = h.shape[0]
    tm = min(tm, seq)
    n_after = len(after)

    def body(h_ref, gq_ref, gk_ref, wq_ref, wk_ref, wv_ref, c_ref, s_ref, dq_ref, dk_ref, dv_ref, *rest):
        dh_ref, dwq_ref, dwk_ref, dwv_ref, dgq_ref, dgk_ref = rest[n_after:]

        @pl.when(pl.program_id(0) == 0)
        def _():
            for r in (dwq_ref, dwk_ref, dwv_ref, dgq_ref, dgk_ref):
                r[...] = jnp.zeros_like(r)

        hb = h_ref[...]
        ctab_, stab_ = c_ref[...], s_ref[...]
        gq, gk = gq_ref[...], gk_ref[...]
        xq, rq, cqn = _rms(hb[:, :Q_RANK], gq)
        xk, rk, ckn = _rms(hb[:, Q_RANK:Q_RANK + KV_RANK], gk)
        cqn = cqn.astype(MXU_DTYPE)
        ckn = ckn.astype(MXU_DTYPE)
        d_cqn = jnp.zeros((tm, Q_RANK), F32)
        d_ckn = jnp.zeros((tm, KV_RANK), F32)
        d_krr = jnp.zeros((tm, LANES), F32)
        for hd in range(HEADS):
            dqh = _rope_inv(dq_ref[hd], ctab_, stab_).astype(MXU_DTYPE)
            d_cqn += _dot(dqh, wq_ref[hd], 1, 1)
            dwq_ref[hd] += _dot(cqn, dqh, 0, 0)
            dkh = dk_ref[hd]
            d_krr += dkh
            dkh = dkh.astype(MXU_DTYPE)
            d_ckn += _dot(dkh, wk_ref[hd], 1, 1)
            dwk_ref[hd] += _dot(ckn, dkh, 0, 0)
            dvh = dv_ref[hd].astype(MXU_DTYPE)
            d_ckn += _dot(dvh, wv_ref[hd], 1, 1)
            dwv_ref[hd] += _dot(ckn, dvh, 0, 0)
        lane = lax.broadcasted_iota(jnp.int32, (tm, LANES), 1)
        rot = (lane >= NOPE_DIM) & (lane < NOPE_DIM + ROPE_DIM)
        d_kr = jnp.where(rot, _rope_inv(jnp.where(rot, d_krr, 0.0), ctab_, stab_), 0.0)

        def rms_bwd(dy, xh, r, g, dg_ref):
            dg_ref[...] += jnp.sum(dy * xh, axis=0, keepdims=True)
            dxh = dy * g
            return r * (dxh - xh * jnp.mean(dxh * xh, axis=-1, keepdims=True))

        d_cq = rms_bwd(d_cqn, xq, rq, gq, dgq_ref)
        d_ck = rms_bwd(d_ckn, xk, rk, gk, dgk_ref)
        dh_ref[...] = jnp.concatenate([d_cq, d_ck, d_kr], axis=1).astype(dh_ref.dtype)

    full = lambda *shape: pl.BlockSpec(shape, lambda i: (0,) * len(shape))
    slab = pl.BlockSpec((HEADS, tm, LANES), lambda i: (0, i, 0))
    return pl.pallas_call(
        body, name="mla_prep_bwd", grid=(seq // tm,),
        in_specs=[pl.BlockSpec((tm, 512), lambda i: (i, 0)), full(1, Q_RANK), full(1, KV_RANK),
                  full(HEADS, Q_RANK, LANES), full(HEADS, KV_RANK, LANES), full(HEADS, KV_RANK, LANES),
                  pl.BlockSpec((tm, LANES), lambda i: (i, 0)), pl.BlockSpec((tm, LANES), lambda i: (i, 0)),
                  slab, slab, slab] + [_ANY_SPEC] * n_after,
        out_specs=[pl.BlockSpec((tm, 512), lambda i: (i, 0)), full(HEADS, Q_RANK, LANES), full(HEADS, KV_RANK, LANES),
                   full(HEADS, KV_RANK, LANES), full(1, Q_RANK), full(1, KV_RANK)],
        out_shape=[jax.ShapeDtypeStruct((seq, 512), MXU_DTYPE), jax.ShapeDtypeStruct((HEADS, Q_RANK, LANES), F32),
                   jax.ShapeDtypeStruct((HEADS, KV_RANK, LANES), F32), jax.ShapeDtypeStruct((HEADS, KV_RANK, LANES), F32),
                   jax.ShapeDtypeStruct((1, Q_RANK), F32), jax.ShapeDtypeStruct((1, KV_RANK), F32)],
        compiler_params=_params("arbitrary"),
    )(h, g_cq, g_ckv, wq, wk, wv, ctab, stab, dq, dk, dv, *after)


def _causal_mask(t):
    row = lax.broadcasted_iota(jnp.int32, (t, t), 0)
    col = lax.broadcasted_iota(jnp.int32, (t, t), 1)
    return row >= col


def _mla_attn_fwd(q, k, v, t=512):
    _, seq, _ = q.shape
    t = min(t, seq)

    def body(q_ref, k_ref, v_ref, o_ref, ob_ref, lse_ref, m_ref, l_ref, acc_ref, s_ref):
        i = pl.program_id(1)
        qb = q_ref[...]
        m_ref[...] = jnp.full_like(m_ref, NEG_BIG)
        l_ref[...] = jnp.zeros_like(l_ref)
        acc_ref[...] = jnp.zeros_like(acc_ref)

        def scores(j):
            return _dot(qb, k_ref[pl.ds(pl.multiple_of(j * t, t), t), :], 1, 1) * MLA_SCALE_LOG2

        def softmax_pv(j, s, masked):
            vb = v_ref[pl.ds(pl.multiple_of(j * t, t), t), :]
            if masked:
                s = jnp.where(_causal_mask(t), s, NEG_BIG)
            m_old = m_ref[...]
            m_new = jnp.maximum(m_old, jnp.max(s, axis=1, keepdims=True))
            p = jnp.exp2(s - m_new)
            a = jnp.exp2(m_old - m_new)
            l_ref[...] = a * l_ref[...] + jnp.sum(p, axis=1, keepdims=True)
            acc_ref[...] = a * acc_ref[...] + _dot(p.astype(MXU_DTYPE), vb, 1, 0)
            m_ref[...] = m_new

        s_ref[...] = scores(0)

        def loop_body(j, c):
            s_next = scores(j + 1)
            softmax_pv(j, s_ref[...], False)
            s_ref[...] = s_next
            return c

        lax.fori_loop(0, i, loop_body, 0)
        softmax_pv(i, s_ref[...], True)
        l = l_ref[...]
        o = acc_ref[...] * (1.0 / l)
        o_ref[...] = o
        ob_ref[...] = o.astype(ob_ref.dtype)
        lse_ref[...] = jnp.broadcast_to(m_ref[...] + jnp.log2(l), lse_ref.shape)

    blk = pl.BlockSpec((None, t, LANES), lambda h, i: (h, i, 0))
    whole = pl.BlockSpec((None, seq, LANES), lambda h, i: (h, 0, 0))
    shp = jax.ShapeDtypeStruct((HEADS, seq, LANES), F32)
    return pl.pallas_call(
        body, name="mla_attn_fwd", grid=(HEADS, seq // t),
        in_specs=[blk, whole, whole], out_specs=[blk, blk, blk],
        out_shape=[shp, jax.ShapeDtypeStruct((HEADS, seq, LANES), MXU_DTYPE), shp],
        scratch_shapes=[pltpu.VMEM((t, 1), F32), pltpu.VMEM((t, 1), F32), pltpu.VMEM((t, LANES), F32), pltpu.VMEM((t, t), F32)],
        compiler_params=_params("parallel", "arbitrary"),
    )(q, k, v)


def _mla_attn_bwd(q, k, v, o, lse, do, t=512):
    _, seq, _ = q.shape
    t = min(t, seq)
    nb = seq // t

    def body(q_ref, k_ref, v_ref, o_ref, lse_ref, do_ref, dq_ref, dk_ref, dv_ref, dl_ref, dka_ref, dva_ref):
        dq_ref[...] = jnp.zeros_like(dq_ref)

        def delta_body(i, c):
            rows = pl.ds(pl.multiple_of(i * t, t), t)
            dl_ref[rows, :] = jnp.sum(do_ref[rows, :] * o_ref[rows, :], axis=1, keepdims=True)
            return c

        lax.fori_loop(0, nb, delta_body, 0)

        def kblock(j, c):
            krows = pl.ds(pl.multiple_of(j * t, t), t)
            kb = k_ref[krows, :]
            vb = v_ref[krows, :]
            dka_ref[...] = jnp.zeros_like(dka_ref)
            dva_ref[...] = jnp.zeros_like(dva_ref)

            def qstep(i, masked):
                th = t // 2
                rows = [pl.ds(pl.multiple_of(i * t + hf * th, th), th) for hf in range(2)]
                qs = [q_ref[r, :] for r in rows]
                dos = [do_ref[r, :].astype(MXU_DTYPE) for r in rows]
                nkeys = [th if masked else t, t]
                ss = [_dot(qs[hf], kb[:nkeys[hf]], 1, 1) * MLA_SCALE_LOG2 for hf in range(2)]
                dps = [_dot(dos[hf], vb[:nkeys[hf]], 1, 1) for hf in range(2)]
                for hf in range(2):
                    s, nk = ss[hf], nkeys[hf]
                    if masked:
                        row = lax.broadcasted_iota(jnp.int32, (th, nk), 0) + hf * th
                        s = jnp.where(row >= lax.broadcasted_iota(jnp.int32, (th, nk), 1), s, NEG_BIG)
                    p = jnp.exp2(s - lse_ref[rows[hf], 0:1])
                    dva_ref[0:nk, :] += _dot(p.astype(MXU_DTYPE), dos[hf], 0, 0)
                    ds = (p * (dps[hf] - dl_ref[rows[hf], :]) * MLA_SCALE).astype(MXU_DTYPE)
                    dka_ref[0:nk, :] += _dot(ds, qs[hf], 0, 0)
                    dq_ref[rows[hf], :] += _dot(ds, kb[:nk], 1, 0)

            qstep(j, True)

            def qloop(i, c2):
                qstep(i, False)
                return c2

            lax.fori_loop(j + 1, nb, qloop, 0)
            dk_ref[krows, :] = dka_ref[...]
            dv_ref[krows, :] = dva_ref[...]
            return c

        lax.fori_loop(0, nb, kblock, 0)

    whole = pl.BlockSpec((None, seq, LANES), lambda h: (h, 0, 0))
    shp = jax.ShapeDtypeStruct((HEADS, seq, LANES), F32)
    return pl.pallas_call(
        body, name="mla_attn_bwd", grid=(HEADS,),
        in_specs=[whole] * 6, out_specs=[whole] * 3, out_shape=[shp] * 3,
        scratch_shapes=[pltpu.VMEM((seq, 1), F32), pltpu.VMEM((t, LANES), F32), pltpu.VMEM((t, LANES), F32)],
        compiler_params=_params("parallel"),
    )(q, k, v, o, lse, do)


DIL_CHUNK = DIL_BLOCK * max(d for _, d in DIL_PAIRS)
DIL_PAIR_LANES = 2 * HEAD_DIM
assert DIL_PAIR_LANES == LANES
DIL_UNROLL_FWD = 16
DIL_UNROLL_BWD = 8


def _dil_bias_tables(hp, dil):
    b = DIL_BLOCK
    iq = lax.broadcasted_iota(jnp.int32, (b, 2 * b), 0)
    ik = lax.broadcasted_iota(jnp.int32, (b, 2 * b), 1)
    off = iq + b - ik
    band = (off >= 0) & (off <= b)
    dist = (off * dil).astype(F32)
    every, first = [], []
    for hh in range(2):
        slope = jnp.where(hp == 0, ALIBI_SLOPES[hh], jnp.where(hp == 1, ALIBI_SLOPES[2 + hh],
                          jnp.where(hp == 2, ALIBI_SLOPES[4 + hh], ALIBI_SLOPES[6 + hh]))).astype(F32)
        bias = -slope * dist
        every.append(jnp.where(band, bias, NEG_BIG))
        first.append(jnp.where(band & (ik >= b), bias, NEG_BIG))
    return jnp.concatenate(every, axis=0), jnp.concatenate(first, axis=0)


def _dil_rows(start, dil):
    return pl.ds(start, DIL_BLOCK) if dil == 1 else pl.ds(start, DIL_BLOCK, stride=dil)


def _dil_block_pos(blk, c, dil):
    sc, r = blk // dil, blk % dil
    q0 = sc * (DIL_BLOCK * dil) + r
    kcur0 = c * DIL_CHUNK + q0
    first = kcur0 < DIL_BLOCK * dil
    kprev0 = jnp.where(first, kcur0, kcur0 - DIL_BLOCK * dil)
    return q0, kcur0, kprev0, first


def _pair_cols(hh):
    return slice(HEAD_DIM * hh, HEAD_DIM * (hh + 1))


def _first_head_lanes(shape):
    return lax.broadcasted_iota(jnp.int32, shape, 1) < HEAD_DIM


def _stack_pair(t):
    first = _first_head_lanes(t.shape)
    return jnp.concatenate([jnp.where(first, t, 0.0), jnp.where(first, 0.0, t)], axis=0).astype(MXU_DTYPE)


def _unstack_pair(t):
    rows = t.shape[0] // 2
    return jnp.where(_first_head_lanes((rows, t.shape[1])), t[:rows], t[rows:])


def _pair_column(t):
    return jnp.concatenate([t[:, 0:1], t[:, HEAD_DIM:HEAD_DIM + 1]], axis=0)


def _dil_fwd(h):
    seq = h.shape[0]
    assert seq % DIL_CHUNK == 0
    nblk = DIL_CHUNK // DIL_BLOCK
    rc = 256

    def body(q_ref, k_ref, v_ref, o_ref, ob_ref, lse_ref, *scr):
        o_scr, l_scr = scr[:3], scr[3:]
        hp, c = pl.program_id(0), pl.program_id(1)
        for bi, (_, dil) in enumerate(DIL_PAIRS):
            tables = _dil_bias_tables(hp, dil)

            def block(blk, carry, bi=bi, dil=dil, tables=tables):
                q0, kcur0, kprev0, first = _dil_block_pos(blk, c, dil)
                q2 = _stack_pair(q_ref[_dil_rows(q0, dil), :] * DIL_SCALE)
                kcat = jnp.concatenate([k_ref[_dil_rows(kprev0, dil), :], k_ref[_dil_rows(kcur0, dil), :]], axis=0).astype(MXU_DTYPE)
                vcat = jnp.concatenate([v_ref[_dil_rows(kprev0, dil), :], v_ref[_dil_rows(kcur0, dil), :]], axis=0).astype(MXU_DTYPE)
                s = _dot(q2, kcat, 1, 1) + jnp.where(first, tables[1], tables[0])
                mx = jnp.max(s, axis=1, keepdims=True)
                p = jnp.exp(s - mx)
                l = jnp.sum(p, axis=1, keepdims=True)
                o_scr[bi][_dil_rows(q0, dil), :] = _unstack_pair(_dot(p.astype(MXU_DTYPE), vcat, 1, 0) * (1.0 / l))
                l_scr[bi][_dil_rows(q0, dil), :] = _unstack_pair(jnp.broadcast_to(mx + jnp.log(l), (2 * DIL_BLOCK, LANES)))
                return carry

            lax.fori_loop(0, nblk, block, 0, unroll=DIL_UNROLL_FWD)

        def combine(i, carry):
            rows = pl.ds(pl.multiple_of(i * rc, rc), rc)
            ls = [l_scr[bi][rows, :] for bi in range(3)]
            mx = jnp.maximum(jnp.maximum(ls[0], ls[1]), ls[2])
            es = [jnp.exp(l - mx) for l in ls]
            den = es[0] + es[1] + es[2]
            o = (es[0] * o_scr[0][rows, :] + es[1] * o_scr[1][rows, :] + es[2] * o_scr[2][rows, :]) / den
            o_ref[rows, :] = o
            ob_ref[rows, :] = o.astype(ob_ref.dtype)
            lse_ref[rows, :] = mx + jnp.log(den)
            return carry

        lax.fori_loop(0, DIL_CHUNK // rc, combine, 0)

    nq = DIL_WIDTH // LANES
    chunk = lambda off: pl.BlockSpec((DIL_CHUNK, LANES), lambda hp, c: (c, off + hp))
    whole = lambda off: pl.BlockSpec((seq, LANES), lambda hp, c: (0, off + hp))
    shp = jax.ShapeDtypeStruct((seq, DIL_WIDTH), F32)
    return pl.pallas_call(
        body, name="dil_fwd", grid=(nq, seq // DIL_CHUNK),
        in_specs=[chunk(nq), whole(2 * nq), whole(3 * nq)], out_specs=[chunk(0), chunk(0), chunk(0)],
        out_shape=[shp, jax.ShapeDtypeStruct((seq, DIL_WIDTH), MXU_DTYPE), shp],
        scratch_shapes=[pltpu.VMEM((DIL_CHUNK, LANES), F32)] * 6,
        compiler_params=_params("parallel", "arbitrary"),
    )(h, h, h)


def _dil_bwd(h, o, lse, do):
    seq = h.shape[0]
    nblk = DIL_CHUNK // DIL_BLOCK
    nchunk = seq // DIL_CHUNK
    rc = 256

    def body(q_ref, k_ref, v_ref, o_ref, lse_ref, do_ref, dq_out, dk_out, dv_out, dl_scr, dq_ref, dk_ref, dv_ref):
        hp, c = pl.program_id(0), pl.program_id(1)

        @pl.when(c == 0)
        def _():
            dk_ref[...] = jnp.zeros_like(dk_ref)
            dv_ref[...] = jnp.zeros_like(dv_ref)

        def delta(i, carry):
            rows = pl.ds(pl.multiple_of(i * rc, rc), rc)
            prod = do_ref[rows, :] * o_ref[rows, :]
            dl_scr[rows, :] = jnp.concatenate(
                [jnp.broadcast_to(jnp.sum(prod[:, _pair_cols(hh)], axis=1, keepdims=True), (rc, HEAD_DIM)) for hh in range(2)], axis=1)
            return carry

        lax.fori_loop(0, DIL_CHUNK // rc, delta, 0)

        for bi, (_, dil) in enumerate(DIL_PAIRS):
            tables = _dil_bias_tables(hp, dil)

            def block(blk, carry, bi=bi, dil=dil, tables=tables):
                q0, kcur0, kprev0, first = _dil_block_pos(blk, c, dil)
                qrows = _dil_rows(q0, dil)
                q2 = _stack_pair(q_ref[qrows, :] * DIL_SCALE)
                kcat = jnp.concatenate([k_ref[_dil_rows(kprev0, dil), :], k_ref[_dil_rows(kcur0, dil), :]], axis=0).astype(MXU_DTYPE)
                vcat = jnp.concatenate([v_ref[_dil_rows(kprev0, dil), :], v_ref[_dil_rows(kcur0, dil), :]], axis=0).astype(MXU_DTYPE)
                do2 = _stack_pair(do_ref[qrows, :])
                s = _dot(q2, kcat, 1, 1) + jnp.where(first, tables[1], tables[0])
                p = jnp.exp(s - _pair_column(lse_ref[qrows, :]))
                dp = _dot(do2, vcat, 1, 1)
                ds = (p * (dp - _pair_column(dl_scr[qrows, :]))).astype(MXU_DTYPE)
                dq_b = _unstack_pair(_dot(ds, kcat, 1, 0)) * DIL_SCALE
                dk_b = _dot(ds, q2, 0, 0)
                dv_b = _dot(p.astype(MXU_DTYPE), do2, 0, 0)
                if bi == 0:
                    dq_ref[qrows, :] = dq_b
                else:
                    dq_ref[qrows, :] += dq_b
                dk_ref[_dil_rows(kprev0, dil), :] += dk_b[:DIL_BLOCK]
                dv_ref[_dil_rows(kprev0, dil), :] += dv_b[:DIL_BLOCK]
                dk_ref[_dil_rows(kcur0, dil), :] += dk_b[DIL_BLOCK:]
                dv_ref[_dil_rows(kcur0, dil), :] += dv_b[DIL_BLOCK:]
                return carry

            lax.fori_loop(0, nblk, block, 0, unroll=DIL_UNROLL_BWD)

        dq_out[...] = dq_ref[...].astype(dq_out.dtype)

        @pl.when(c == nchunk - 1)
        def _():
            dk_out[...] = dk_ref[...].astype(dk_out.dtype)
            dv_out[...] = dv_ref[...].astype(dv_out.dtype)

    nq = DIL_WIDTH // LANES
    chunk = lambda off: pl.BlockSpec((DIL_CHUNK, LANES), lambda hp, c: (c, off + hp))
    whole = lambda off: pl.BlockSpec((seq, LANES), lambda hp, c: (0, off + hp))
    shp = jax.ShapeDtypeStruct((seq, DIL_WIDTH), MXU_DTYPE)
    return pl.pallas_call(
        body, name="dil_bwd", grid=(nq, nchunk),
        in_specs=[chunk(nq), whole(2 * nq), whole(3 * nq), chunk(0), chunk(0), chunk(0)],
        out_specs=[chunk(0), whole(0), whole(0)], out_shape=[shp, shp, shp],
        scratch_shapes=[pltpu.VMEM((DIL_CHUNK, LANES), F32), pltpu.VMEM((DIL_CHUNK, LANES), F32),
                        pltpu.VMEM((seq, LANES), F32), pltpu.VMEM((seq, LANES), F32)],
        compiler_params=_params("parallel", "arbitrary"),
    )(h, h, h, o, lse, do)


DH_PART = IN_PAD // 4


def _mm_dx0(parts, w_in_t, res, tm=1024, after=()):
    seq, d = res.shape
    tm = min(tm, seq)
    n_after = len(after)

    def body(a0, a1, a2, a3, b_ref, r_ref, *rest):
        o_ref = rest[n_after]
        acc = _dot(a0[...], b_ref[0:DH_PART, :], 1, 0)
        for c, a in enumerate((a1, a2, a3), start=1):
            acc += _dot(a[...], b_ref[DH_PART * c:DH_PART * (c + 1), :], 1, 0)
        o_ref[...] = acc + DN_ALPHA * r_ref[...]

    blk = pl.BlockSpec((tm, DH_PART), lambda i: (i, 0))
    row = pl.BlockSpec((tm, d), lambda i: (i, 0))
    return pl.pallas_call(
        body, name="mm_dx0", grid=(seq // tm,),
        in_specs=[blk] * 4 + [pl.BlockSpec((IN_PAD, d), lambda i: (0, 0), pipeline_mode=pl.Buffered(1)), row] + [_ANY_SPEC] * n_after,
        out_specs=row, out_shape=jax.ShapeDtypeStruct((seq, d), F32), compiler_params=_params("parallel"),
    )(*parts, w_in_t, res, *after)


def _mm_dw_in(parts, x0b, tk=1024):
    seq, d = x0b.shape
    tk = min(tk, seq)
    nk = seq // tk

    def body(a0, a1, a2, a3, b_ref, o_ref, acc_ref):
        kk = pl.program_id(0)

        @pl.when(kk == 0)
        def _():
            acc_ref[...] = jnp.zeros_like(acc_ref)

        b = b_ref[...]
        for c, a in enumerate((a0, a1, a2, a3)):
            acc_ref[DH_PART * c:DH_PART * (c + 1), :] += _dot(a[...], b, 0, 0)

        @pl.when(kk == nk - 1)
        def _():
            o_ref[...] = acc_ref[...].astype(o_ref.dtype)

    blk = pl.BlockSpec((tk, DH_PART), lambda kk: (kk, 0))
    return pl.pallas_call(
        body, name="mm_dw_in", grid=(nk,), in_specs=[blk] * 4 + [pl.BlockSpec((tk, d), lambda kk: (kk, 0))],
        out_specs=pl.BlockSpec((IN_PAD, d), lambda kk: (0, 0)), out_shape=jax.ShapeDtypeStruct((IN_PAD, d), MXU_DTYPE),
        scratch_shapes=[pltpu.VMEM((IN_PAD, d), F32)], compiler_params=_params("arbitrary"),
    )(*parts, x0b)


def _ln_stats(z):
    mu = jnp.mean(z, axis=-1, keepdims=True)
    zc = z - mu
    r = lax.rsqrt(jnp.mean(zc * zc, axis=-1, keepdims=True) + LN_EPS)
    return zc * r, r


def _ln_bwd_math(dy, xh, r, g):
    dxh = dy * g
    return r * (dxh - jnp.mean(dxh, axis=-1, keepdims=True) - xh * jnp.mean(dxh * xh, axis=-1, keepdims=True))


def _mix_ln1(o_mla, o_dil, w_o_mla, w_o_dil, x0, g, b, tm=512):
    seq, d = x0.shape
    tm = min(tm, seq)

    def body(om_ref, od_ref, wm_ref, wd_ref, x_ref, g_ref, b_ref, z_ref, y_ref, yb_ref):
        mix = _dot(od_ref[...], wd_ref[...], 1, 0)
        for hd in range(HEADS):
            mix += _dot(om_ref[hd], wm_ref[LANES * hd:LANES * (hd + 1), :], 1, 0)
        z = DN_ALPHA * x_ref[...] + mix
        xh, _ = _ln_stats(z)
        y = xh * g_ref[...] + b_ref[...]
        z_ref[...] = z
        y_ref[...] = y
        yb_ref[...] = y.astype(yb_ref.dtype)

    blk = pl.BlockSpec((tm, d), lambda i: (i, 0))
    vec = pl.BlockSpec((1, d), lambda i: (0, 0))
    shp = jax.ShapeDtypeStruct((seq, d), F32)
    return pl.pallas_call(
        body, name="mix_ln1", grid=(seq // tm,),
        in_specs=[pl.BlockSpec((HEADS, tm, LANES), lambda i: (0, i, 0)), pl.BlockSpec((tm, DIL_WIDTH), lambda i: (i, 0)),
                  pl.BlockSpec((HEADS * LANES, d), lambda i: (0, 0)), pl.BlockSpec((DIL_WIDTH, d), lambda i: (0, 0)), blk, vec, vec],
        out_specs=[blk, blk, blk], out_shape=[shp, shp, jax.ShapeDtypeStruct((seq, d), MXU_DTYPE)],
        compiler_params=_params("parallel"))(o_mla, o_dil, w_o_mla, w_o_dil, x0, g, b)


def _ln_bwd(dy, z, g, name, tm=512, after=()):
    seq, d = z.shape
    tm = min(tm, seq)
    n_after = len(after)

    def body(dy_ref, z_ref, g_ref, *rest):
        dz_ref, dzb_ref, dg_ref, db_ref = rest[n_after:]

        @pl.when(pl.program_id(0) == 0)
        def _():
            dg_ref[...] = jnp.zeros_like(dg_ref)
            db_ref[...] = jnp.zeros_like(db_ref)

        dyb = dy_ref[...]
        xh, r = _ln_stats(z_ref[...])
        dg_ref[...] += jnp.sum(dyb * xh, axis=0, keepdims=True)
        db_ref[...] += jnp.sum(dyb, axis=0, keepdims=True)
        dz = _ln_bwd_math(dyb, xh, r, g_ref[...])
        dz_ref[...] = dz
        dzb_ref[...] = dz.astype(dzb_ref.dtype)

    blk = pl.BlockSpec((tm, d), lambda i: (i, 0))
    vec = pl.BlockSpec((1, d), lambda i: (0, 0))
    return pl.pallas_call(
        body, name=name, grid=(seq // tm,), in_specs=[blk, blk, vec] + [_ANY_SPEC] * n_after, out_specs=[blk, blk, vec, vec],
        out_shape=[jax.ShapeDtypeStruct((seq, d), F32), jax.ShapeDtypeStruct((seq, d), MXU_DTYPE),
                   jax.ShapeDtypeStruct((1, d), F32), jax.ShapeDtypeStruct((1, d), F32)],
        compiler_params=_params("arbitrary"))(dy, z, g, *after)


def _ln2_loss_bwd(x1, ffn, target, g, b, tm=512):
    seq, d = x1.shape
    tm = min(tm, seq)

    def body(x_ref, f_ref, t_ref, g_ref, b_ref, dz_ref, dzb_ref, loss_ref, dg_ref, db_ref):
        @pl.when(pl.program_id(0) == 0)
        def _():
            loss_ref[...] = jnp.zeros_like(loss_ref)
            dg_ref[...] = jnp.zeros_like(dg_ref)
            db_ref[...] = jnp.zeros_like(db_ref)

        gv = g_ref[...]
        z = DN_ALPHA * x_ref[...] + f_ref[...]
        xh, r = _ln_stats(z)
        err = (xh * gv + b_ref[...]) - t_ref[...]
        loss_ref[...] += 0.5 * jnp.sum(jnp.mean(err * err, axis=-1, keepdims=True), axis=0, keepdims=True)
        dy = err * (1.0 / d)
        dg_ref[...] += jnp.sum(dy * xh, axis=0, keepdims=True)
        db_ref[...] += jnp.sum(dy, axis=0, keepdims=True)
        dz = _ln_bwd_math(dy, xh, r, gv)
        dz_ref[...] = dz
        dzb_ref[...] = dz.astype(dzb_ref.dtype)

    blk = pl.BlockSpec((tm, d), lambda i: (i, 0))
    vec = pl.BlockSpec((1, d), lambda i: (0, 0))
    return pl.pallas_call(
        body, name="ln2_loss_bwd", grid=(seq // tm,), in_specs=[blk, blk, blk, vec, vec],
        out_specs=[blk, blk, pl.BlockSpec((1, LANES), lambda i: (0, 0)), vec, vec],
        out_shape=[jax.ShapeDtypeStruct((seq, d), F32), jax.ShapeDtypeStruct((seq, d), MXU_DTYPE),
                   jax.ShapeDtypeStruct((1, LANES), F32),
                   jax.ShapeDtypeStruct((1, d), F32), jax.ShapeDtypeStruct((1, d), F32)],
        compiler_params=_params("arbitrary"))(x1, ffn, target, g, b)


HALO = 16


def _conv_rows(e, w_ref, b_ref):
    y = b_ref[...] + w_ref[0:1, :] * pltpu.roll(e, 2, 0)
    y = y + w_ref[1:2, :] * pltpu.roll(e, 1, 0)
    return y + w_ref[2:3, :] * e


_GELU_C = math.sqrt(2.0 / math.pi)
_GELU_A = 0.044715


def _gelu(x):
    return 0.5 * x * (1.0 + jnp.tanh(_GELU_C * (x + _GELU_A * (x * x * x))))


CONV_TN = 256


def _ffn_interleave(a, axis):
    shp = a.shape
    a = a.reshape(shp[:axis] + (2, D_FF // CONV_TN, CONV_TN) + shp[axis + 1:])
    return jnp.swapaxes(a, axis, axis + 1).reshape(shp)


def _ffn_deinterleave(a, axis):
    shp = a.shape
    a = a.reshape(shp[:axis] + (D_FF // CONV_TN, 2, CONV_TN) + shp[axis + 1:])
    return jnp.swapaxes(a, axis, axis + 1).reshape(shp)


def _conv_gate_fwd(u, conv_w, conv_b, tm=1024):
    seq = u.shape[0]
    tm = min(tm, seq)
    tn = CONV_TN

    def body(u_ref, up_ref, w_ref, b_ref, o_ref):
        first = pl.program_id(0) == 0
        e = jnp.concatenate([jnp.where(first, 0.0, up_ref[...]), u_ref[...]], axis=0)
        y = _conv_rows(e, w_ref, b_ref)[HALO:]
        o_ref[...] = (_gelu(y[:, tn:]) * y[:, :tn]).astype(o_ref.dtype)

    hb = tm // HALO
    return pl.pallas_call(
        body, name="conv_gate_fwd", grid=(seq // tm, D_FF // tn),
        in_specs=[pl.BlockSpec((tm, 2 * tn), lambda i, j: (i, j)),
                  pl.BlockSpec((HALO, 2 * tn), lambda i, j: (jnp.maximum(i * hb - 1, 0), j)),
                  pl.BlockSpec((3, 2 * tn), lambda i, j: (0, j)), pl.BlockSpec((1, 2 * tn), lambda i, j: (0, j))],
        out_specs=pl.BlockSpec((tm, tn), lambda i, j: (i, j)), out_shape=jax.ShapeDtypeStruct((seq, D_FF), MXU_DTYPE),
        compiler_params=_params("parallel", "parallel"),
    )(u, u, conv_w, conv_b)


def _conv_gate_bwd(u, d_act, conv_w, conv_b, tm=1024):
    seq = u.shape[0]
    tm = min(tm, seq)
    tn = CONV_TN
    ni = seq // tm
    rows_e = tm + 2 * HALO

    def body(u_ref, up_ref, un_ref, da_ref, dan_ref, w_ref, b_ref, du_ref, dw_ref, db_ref):
        i = pl.program_id(1)
        first, last = i == 0, i == ni - 1

        @pl.when(i == 0)
        def _():
            dw_ref[...] = jnp.zeros_like(dw_ref)
            db_ref[...] = jnp.zeros_like(db_ref)

        e = jnp.concatenate([jnp.where(first, 0.0, up_ref[...]), u_ref[...], jnp.where(last, 0.0, un_ref[...])], axis=0)
        y = _conv_rows(e, w_ref, b_ref)
        ya, yg = y[:, :tn], y[:, tn:]
        dact = jnp.concatenate([jnp.zeros((HALO, tn), F32), da_ref[...].astype(F32),
                                jnp.where(last, 0.0, dan_ref[...].astype(F32))], axis=0)
        th = jnp.tanh(_GELU_C * (yg + _GELU_A * (yg * yg * yg)))
        gelu = 0.5 * yg * (1.0 + th)
        gelu_grad = 0.5 * (1.0 + th) + 0.5 * yg * (1.0 - th * th) * (_GELU_C * (1.0 + 3.0 * _GELU_A * (yg * yg)))
        dy = jnp.concatenate([dact * gelu, dact * ya * gelu_grad], axis=1)
        du = w_ref[2:3, :] * dy + w_ref[1:2, :] * pltpu.roll(dy, rows_e - 1, 0) + w_ref[0:1, :] * pltpu.roll(dy, rows_e - 2, 0)
        du_ref[...] = du[HALO:HALO + tm].astype(du_ref.dtype)
        dyt = dy[HALO:HALO + tm]
        dw_ref[0:1, :] += jnp.sum(dyt * pltpu.roll(e, 2, 0)[HALO:HALO + tm], axis=0, keepdims=True)
        dw_ref[1:2, :] += jnp.sum(dyt * pltpu.roll(e, 1, 0)[HALO:HALO + tm], axis=0, keepdims=True)
        dw_ref[2:3, :] += jnp.sum(dyt * e[HALO:HALO + tm], axis=0, keepdims=True)
        db_ref[...] += jnp.sum(dyt, axis=0, keepdims=True)

    hb = tm // HALO
    nh = seq // HALO
    prev = lambda j, i: (jnp.maximum(i * hb - 1, 0), j)
    nxt = lambda j, i: (jnp.minimum((i + 1) * hb, nh - 1), j)
    return pl.pallas_call(
        body, name="conv_gate_bwd", grid=(D_FF // tn, ni),
        in_specs=[pl.BlockSpec((tm, 2 * tn), lambda j, i: (i, j)), pl.BlockSpec((HALO, 2 * tn), prev),
                  pl.BlockSpec((HALO, 2 * tn), nxt), pl.BlockSpec((tm, tn), lambda j, i: (i, j)), pl.BlockSpec((HALO, tn), nxt),
                  pl.BlockSpec((3, 2 * tn), lambda j, i: (0, j)), pl.BlockSpec((1, 2 * tn), lambda j, i: (0, j))],
        out_specs=[pl.BlockSpec((tm, 2 * tn), lambda j, i: (i, j)), pl.BlockSpec((3, 2 * tn), lambda j, i: (0, j)),
                   pl.BlockSpec((1, 2 * tn), lambda j, i: (0, j))],
        out_shape=[jax.ShapeDtypeStruct((seq, 2 * D_FF), MXU_DTYPE), jax.ShapeDtypeStruct((3, 2 * D_FF), F32),
                   jax.ShapeDtypeStruct((1, 2 * D_FF), F32)],
        compiler_params=_params("parallel", "arbitrary"),
    )(u, u, u, d_act, d_act, conv_w, conv_b)


def _pad_heads(w, width):
    w = jnp.transpose(w, (1, 0, 2))
    return jnp.pad(w, ((0, 0), (0, 0), (0, LANES - width))).astype(MXU_DTYPE)


def _heads_major(a):
    return jnp.transpose(a, (1, 0, 2)).reshape(-1, a.shape[2])


def _heads_minor(a, heads):
    return jnp.transpose(a.reshape(heads, -1, a.shape[1]), (1, 0, 2))


def _unpad_heads(d, width):
    return jnp.transpose(d[:, :, :width], (1, 0, 2))


def _split_pad_rows(w_t):
    z = lambda n: jnp.zeros((n, w_t.shape[1]), w_t.dtype)
    return jnp.concatenate([w_t[:384], z(64), w_t[384:416], z(32), w_t[416:]], axis=0)


def _split_unpad_rows(w_p):
    return jnp.concatenate([w_p[:384], w_p[448:480], w_p[512:]], axis=0)


def _pad_w_o(w_o):
    mla = jnp.pad(w_o[:512].reshape(HEADS, HEAD_DIM, D_MODEL), ((0, 0), (0, LANES - HEAD_DIM), (0, 0)))
    return mla.reshape(HEADS * LANES, D_MODEL).astype(MXU_DTYPE), w_o[512:].astype(MXU_DTYPE)


def _unpad_w_o(d_mla, d_dil):
    return jnp.concatenate([d_mla.reshape(HEADS, LANES, D_MODEL)[:, :HEAD_DIM].reshape(512, D_MODEL), d_dil], axis=0)


def _row(v):
    return v.reshape(1, -1).astype(F32)


def _compute_weights(w):
    w_o_mla, w_o_dil = _pad_w_o(w["w_o"])
    return dict(
        w_in_t=_split_pad_rows(w["w_in"].T).astype(MXU_DTYPE), wq=_pad_heads(w["w_uq"], NOPE_DIM + ROPE_DIM),
        wk=_pad_heads(w["w_uk"], NOPE_DIM), wv=_pad_heads(w["w_uv"], HEAD_DIM), w_o_mla=w_o_mla, w_o_dil=w_o_dil,
        w_up_t=_ffn_interleave(w["w_up"].T, 0).astype(MXU_DTYPE), w_down=w["w_down"].astype(MXU_DTYPE),
        conv_w=_ffn_interleave(w["conv_w"].astype(F32), 1),
        g_cq=_row(w["g_cq"]), g_ckv=_row(w["g_ckv"]), ln1_g=_row(w["ln1_g"]), ln1_b=_row(w["ln1_b"]),
        conv_b=_ffn_interleave(_row(w["conv_b"]), 1), ln2_g=_row(w["ln2_g"]), ln2_b=_row(w["ln2_b"]))


def _natural_grads(g):
    return dict(
        w_in=_split_unpad_rows(g["w_in_t"]).T, g_cq=g["g_cq"].reshape(-1), g_ckv=g["g_ckv"].reshape(-1),
        w_uq=_unpad_heads(g["wq"], NOPE_DIM + ROPE_DIM), w_uk=_unpad_heads(g["wk"], NOPE_DIM),
        w_uv=_unpad_heads(g["wv"], HEAD_DIM), w_o=_unpad_w_o(g["w_o_mla"], g["w_o_dil"]), ln1_g=g["ln1_g"].reshape(-1),
        ln1_b=g["ln1_b"].reshape(-1), w_up=_ffn_deinterleave(g["w_up_t"], 0).T, conv_w=_ffn_deinterleave(g["conv_w"], 1),
        conv_b=_ffn_deinterleave(g["conv_b"], 1).reshape(-1),
        w_down=g["w_down"], ln2_g=g["ln2_g"].reshape(-1), ln2_b=g["ln2_b"].reshape(-1))


def _layer_grads(x0, target, cw, first_after=(), late_weights=None, on_grads=None):
    seq = x0.shape[0]
    ctab, stab = _rope_tables(seq)
    gq, gk = cw["g_cq"], cw["g_ckv"]
    wq, wk, wv = cw["wq"], cw["wk"], cw["wv"]
    x0b = x0.astype(MXU_DTYPE)
    notify = (lambda stage, grads: ()) if on_grads is None else on_grads

    h = _mm(x0b, cw["w_in_t"], name="mm_h", tb=True, tm=1024, tn=IN_PAD, tk=1024, after=first_after)
    qf, kf, vp = _mla_prep(h, gq, gk, wq, wk, wv, ctab, stab)
    o_mla, o_mla_b, lse_mla = _mla_attn_fwd(qf, kf, vp)
    o_dil, o_dil_b, lse_dil = _dil_fwd(h)
    if late_weights is not None:
        cw = {**cw, **late_weights(o_mla_b)}
    cb = cw["conv_b"]
    z1, x1, x1b = _mix_ln1(o_mla_b, o_dil_b, cw["w_o_mla"], cw["w_o_dil"], x0, cw["ln1_g"], cw["ln1_b"])
    u = _mm(x1b, cw["w_up_t"], name="mm_up", tb=True, tm=512, tn=2 * D_FF, tk=1024)
    act = _conv_gate_fwd(u, cw["conv_w"], cb)
    ffn = _mm(act, cw["w_down"], name="mm_down", tm=1024, tn=1024, tk=2816)
    dz2, dz2b, loss, d_ln2_g, d_ln2_b = _ln2_loss_bwd(x1, ffn, target, cw["ln2_g"], cw["ln2_b"])

    d_act = _mm(dz2b, cw["w_down"], name="mm_d_act", tb=True, out_dtype=MXU_DTYPE, tm=1024, tn=D_FF, tk=1024)
    d_w_down = _mm(act, dz2b, name="mm_dw_down", ta=True, out_dtype=MXU_DTYPE, tm=1408, tn=1024, tk=1024)
    du, d_conv_w, d_conv_b = _conv_gate_bwd(u, d_act, cw["conv_w"], cb)
    dx1 = _mm(du, cw["w_up_t"], name="mm_dx1", res=dz2, res_scale=DN_ALPHA, tm=512, tn=1024, tk=2 * D_FF)
    d_w_up_t = _mm(du, x1b, name="mm_dw_up", ta=True, out_dtype=MXU_DTYPE, tm=1408, tn=1024, tk=2048)
    grads = dict(w_up_t=d_w_up_t, w_down=d_w_down, conv_w=d_conv_w, conv_b=d_conv_b, ln2_g=d_ln2_g, ln2_b=d_ln2_b)
    dz1, dz1b, d_ln1_g, d_ln1_b = _ln_bwd(dx1, z1, cw["ln1_g"], "ln1_bwd", after=notify("ffn", grads))
    do_mla = _mm_do_mla(dz1b, cw["w_o_mla"])
    do_dil = _mm(dz1b, cw["w_o_dil"], name="mm_do_dil", tb=True, tm=1024, tn=512, tk=1024)
    d_w_o_mla = _mm_dw_o_mla(o_mla_b, dz1b)
    d_w_o_dil = _mm(o_dil_b, dz1b, name="mm_dw_o_dil", ta=True, out_dtype=MXU_DTYPE, tm=512, tn=1024, tk=1024)
    grads.update(w_o_mla=d_w_o_mla, w_o_dil=d_w_o_dil, ln1_g=d_ln1_g, ln1_b=d_ln1_b)
    dq_dil, dk_dil, dv_dil = _dil_bwd(h, o_dil, lse_dil, do_dil)
    dqf, dkf, dvf = _mla_attn_bwd(qf, kf, vp, o_mla, lse_mla, do_mla)
    dh_mla, d_wq, d_wk, d_wv, d_gq, d_gk = _mla_prep_bwd(h, gq, gk, wq, wk, wv, ctab, stab, dqf, dkf, dvf,
                                                          after=notify("w_o", grads))
    dh = (dh_mla, dq_dil, dk_dil, dv_dil)
    d_w_in_t = _mm_dw_in(dh, x0b)
    grads.update(w_in_t=d_w_in_t, wq=d_wq, wk=d_wk, wv=d_wv, g_cq=d_gq, g_ckv=d_gk, loss=loss)
    grad_x = _mm_dx0(dh, cw["w_in_t"], dz1, after=notify("rest", grads))
    return loss, grad_x, grads


def _all_gather(blocks, name):
    na = len(blocks)

    def body(*refs):
        ins, outs = refs[:na], refs[na:2 * na]
        send_sems, recv_sems, local_sems = refs[2 * na:]
        x, y, c = lax.axis_index("x"), lax.axis_index("y"), lax.axis_index("c")
        me, sibling = (x, y, c), (x, y, 1 - c)
        chips = [(1 - x, y), (x, 1 - y), (1 - x, 1 - y)]

        def slot(out, pos):
            return out.at[4 * pos[0] + 2 * pos[1] + pos[2]]

        def copy(a, k, block, to, src=None):
            return pltpu.make_async_remote_copy(
                src_ref=slot(outs[a], block) if src is None else src, dst_ref=slot(outs[a], block),
                send_sem=send_sems.at[7 * a + k], recv_sem=recv_sems.at[7 * a + k],
                device_id=to, device_id_type=pl.DeviceIdType.MESH)

        mine = [pltpu.make_async_copy(ins[a], slot(outs[a], me), local_sems.at[a]) for a in range(na)]
        for cp in mine:
            cp.start()
        first = []
        for a in range(na):
            first.append(copy(a, 0, me, sibling, src=ins[a]))
            first += [copy(a, 1 + j, me, (*chip, c), src=ins[a]) for j, chip in enumerate(chips)]
        for cp in first:
            cp.start()
        passed = []
        for j, chip in enumerate(chips):
            for a in range(na):
                copy(a, 1 + j, (*chip, c), me).wait_recv()
                cp = copy(a, 4 + j, (*chip, c), sibling)
                cp.start()
                passed.append(cp)
        for a in range(na):
            copy(a, 0, sibling, me).wait_recv()
            for j, chip in enumerate(chips):
                copy(a, 4 + j, (*chip, 1 - c), me).wait_recv()
        for cp in first + passed:
            cp.wait_send()
        for cp in mine:
            cp.wait()

    any_spec = pl.BlockSpec(memory_space=pl.ANY)
    return pl.pallas_call(
        body, name=name, in_specs=[any_spec] * na, out_specs=[any_spec] * na,
        out_shape=[jax.ShapeDtypeStruct((N_DEV,) + b.shape, b.dtype) for b in blocks],
        scratch_shapes=[pltpu.SemaphoreType.DMA((7 * na,)), pltpu.SemaphoreType.DMA((7 * na,)), pltpu.SemaphoreType.DMA((na,))],
    )(*blocks)


_HBM_SPEC = pl.BlockSpec(memory_space=pltpu.HBM)
_SEM_SPEC = pl.BlockSpec(memory_space=pltpu.SEMAPHORE)
_DATAFLOW = pltpu.CompilerParams(has_side_effects=pltpu.SideEffectType.DATAFLOW_SIDE_EFFECTING)


def _split_copies(ins, lands, send_sems, recv_sems, gather):
    x, y, c = lax.axis_index("x"), lax.axis_index("y"), lax.axis_index("c")
    me = 4 * x + 2 * y + c
    copies = []
    for a in range(len(ins)):
        for d in range(1, N_DEV):
            px, py, pc = x ^ (d >> 2), y ^ ((d >> 1) & 1), c ^ (d & 1)
            copies.append(pltpu.make_async_remote_copy(
                src_ref=ins[a] if gather else ins[a].at[4 * px + 2 * py + pc], dst_ref=lands[a].at[me],
                send_sem=send_sems.at[7 * a + d - 1], recv_sem=recv_sems.at[7 * a + d - 1],
                device_id=(px, py, pc), device_id_type=pl.DeviceIdType.MESH))
    return copies


def _send_start(srcs, gather, name):
    na = len(srcs)
    land_types = [pltpu.HBM(((N_DEV,) + s.shape) if gather else s.shape, s.dtype) for s in srcs]

    def body(*refs):
        ins, lands = refs[:na], refs[na:2 * na]
        send_sems, recv_sems, token = refs[2 * na], refs[2 * na + 1], refs[-1]
        for cp in _split_copies(ins, lands, send_sems, recv_sems, gather):
            cp.start()
        token[...] = jnp.zeros_like(token)

    hbm = lambda a: pltpu.with_memory_space_constraint(a, pltpu.HBM)
    outs = pl.pallas_call(
        body, name=name,
        out_shape=(pltpu.SemaphoreType.DMA((7 * na,)), pltpu.SemaphoreType.DMA((7 * na,)),
                   *[pltpu.HBM(s.shape, s.dtype) for s in srcs], *land_types, jax.ShapeDtypeStruct((8, LANES), F32)),
        in_specs=[_HBM_SPEC] * (2 * na),
        out_specs=(_SEM_SPEC, _SEM_SPEC, *[_HBM_SPEC] * (2 * na), pl.BlockSpec(memory_space=pltpu.VMEM)),
        input_output_aliases={i: 2 + i for i in range(2 * na)}, compiler_params=_DATAFLOW,
    )(*[hbm(s) for s in srcs], *[hbm(lax.empty(t.shape, t.dtype)) for t in land_types])
    return dict(send=outs[0], recv=outs[1], srcs=list(outs[2:2 + na]), lands=list(outs[2 + na:2 + 2 * na]), token=outs[-1],
                gather=gather)


def _send_wait(handle, after, name):
    na = len(handle["srcs"])
    gather = handle["gather"]
    after = list(after)

    def body(*refs):
        ins, lands = refs[:na], refs[na:2 * na]
        send_sems, recv_sems = refs[2 * na], refs[2 * na + 1]
        for cp in _split_copies(ins, lands, send_sems, recv_sems, gather):
            cp.wait_send()
            cp.wait_recv()

    both = handle["srcs"] + handle["lands"]
    outs = pl.pallas_call(
        body, name=name, out_shape=[pltpu.HBM(a.shape, a.dtype) for a in both],
        in_specs=[_HBM_SPEC] * (2 * na) + [_SEM_SPEC, _SEM_SPEC] + [_ANY_SPEC] * len(after),
        out_specs=[_HBM_SPEC] * (2 * na), input_output_aliases={i: i for i in range(2 * na)}, compiler_params=_DATAFLOW,
    )(*both, handle["send"], handle["recv"], *after)
    return list(outs[:na]), list(outs[na:])


def _sum_slots(p_ref):
    g = p_ref[0].astype(F32)
    for s in range(1, p_ref.shape[0]):
        g = g + p_ref[s].astype(F32)
    return g


def _adamw_refs(g, w_ref, m_ref, v_ref, g_out, d_out, m_out, v_out):
    c1 = 1.0 - ADAM_B1 ** ADAM_STEP
    c2 = 1.0 - ADAM_B2 ** ADAM_STEP
    m_new = ADAM_B1 * m_ref[...] + (1.0 - ADAM_B1) * g
    v_new = ADAM_B2 * v_ref[...] + (1.0 - ADAM_B2) * (g * g)
    g_out[...] = g
    m_out[...] = m_new
    v_out[...] = v_new
    d_out[...] = -ADAM_LR * ((m_new / c1) / (jnp.sqrt(v_new / c2) + ADAM_EPS) + ADAM_WD * w_ref[...])


def _adamw(parts, w, m, v, name):
    npart, r, n = parts.shape
    tr = r if r <= 256 else max(t for t in range(16, 257, 16) if r % t == 0)

    def body(p_ref, w_ref, m_ref, v_ref, g_out, d_out, m_out, v_out):
        _adamw_refs(_sum_slots(p_ref), w_ref, m_ref, v_ref, g_out, d_out, m_out, v_out)

    blk = pl.BlockSpec((tr, n), lambda i: (i, 0))
    shp = jax.ShapeDtypeStruct((r, n), F32)
    return pl.pallas_call(
        body, name=name, grid=(r // tr,), in_specs=[pl.BlockSpec((npart, tr, n), lambda i: (0, i, 0)), blk, blk, blk],
        out_specs=[blk] * 4, out_shape=[shp] * 4, compiler_params=_params("parallel"),
    )(parts, w, m, v)


def _adamw_small(parts, ws, ms, vs, loss_parts, name):
    n = len(parts)

    def body(*refs):
        ins, outs = refs[:4 * n + 1], refs[4 * n + 1:]
        for i in range(n):
            _adamw_refs(_sum_slots(ins[i]), ins[n + i], ins[2 * n + i], ins[3 * n + i], *outs[4 * i:4 * i + 4])
        outs[4 * n][...] = _sum_slots(ins[4 * n])

    out_shape = [jax.ShapeDtypeStruct(w.shape, F32) for w in ws for _ in range(4)]
    res = pl.pallas_call(body, name=name, out_shape=out_shape + [jax.ShapeDtypeStruct((1, LANES), F32)],
                         compiler_params=_params())(*parts, *ws, *ms, *vs, loss_parts)
    return [res[4 * i:4 * i + 4] for i in range(n)], res[4 * n]


REPLICATED = ("g_cq", "g_ckv", "w_uk", "w_uv", "ln1_g", "ln1_b", "conv_b", "ln2_g", "ln2_b")
ALL_WEIGHTS = ("w_in", "g_cq", "g_ckv", "w_uq", "w_uk", "w_uv", "w_o", "ln1_g", "ln1_b", "w_up", "conv_w", "conv_b",
               "w_down", "ln2_g", "ln2_b")


def kernel(x, w_in, g_cq, g_ckv, w_uq, w_uk, w_uv, w_o, ln1_g, ln1_b, w_up, conv_w, conv_b, w_down, ln2_g, ln2_b, loss_target, m_w_in, m_g_cq, m_g_ckv, m_w_uq, m_w_uk, m_w_uv, m_w_o, m_ln1_g, m_ln1_b, m_w_up, m_conv_w, m_conv_b, m_w_down, m_ln2_g, m_ln2_b, v_w_in, v_g_cq, v_g_ckv, v_w_uq, v_w_uk, v_w_uv, v_w_o, v_ln1_g, v_ln1_b, v_w_up, v_conv_w, v_conv_b, v_w_down, v_ln2_g, v_ln2_b):
    w = dict(w_in=w_in, g_cq=g_cq, g_ckv=g_ckv, w_uq=w_uq, w_uk=w_uk, w_uv=w_uv, w_o=w_o, ln1_g=ln1_g, ln1_b=ln1_b,
             w_up=w_up, conv_w=conv_w, conv_b=conv_b, w_down=w_down, ln2_g=ln2_g, ln2_b=ln2_b)
    m = dict(w_in=m_w_in, g_cq=m_g_cq, g_ckv=m_g_ckv, w_uq=m_w_uq, w_uk=m_w_uk, w_uv=m_w_uv, w_o=m_w_o, ln1_g=m_ln1_g,
             ln1_b=m_ln1_b, w_up=m_w_up, conv_w=m_conv_w, conv_b=m_conv_b, w_down=m_w_down, ln2_g=m_ln2_g, ln2_b=m_ln2_b)
    v = dict(w_in=v_w_in, g_cq=v_g_cq, g_ckv=v_g_ckv, w_uq=v_w_uq, w_uk=v_w_uk, w_uv=v_w_uv, w_o=v_w_o, ln1_g=v_ln1_g,
             ln1_b=v_ln1_b, w_up=v_w_up, conv_w=v_conv_w, conv_b=v_conv_b, w_down=v_w_down, ln2_g=v_ln2_g, ln2_b=v_ln2_b)
    me = 4 * lax.axis_index("x") + 2 * lax.axis_index("y") + lax.axis_index("c")
    wire = lambda a: a.astype(WIRE_DTYPE)
    pad_taps = lambda a: jnp.pad(a, ((0, 8 - a.shape[0]), (0, 0)))

    own_slot = lambda buf, block: lax.dynamic_update_index_in_dim(buf, block, me, 0)
    blocks = lambda a: wire(a).reshape((N_DEV, a.shape[0] // N_DEV) + a.shape[1:])

    g_in, g_uq, g_conv = _all_gather(
        [wire(w_in).T, _heads_major(wire(w_uq)), pad_taps(conv_w)],
        "gather_weights")
    late = _send_start([wire(w_o), wire(w_up).T, wire(w_down)], True, "gather_late_start")
    r_uq_dev, e_uq = w_uq.shape[0], w_uq.shape[2]
    wq = jnp.transpose(g_uq.reshape(N_DEV, HEADS, r_uq_dev, e_uq), (1, 0, 2, 3)).reshape(HEADS, Q_RANK, e_uq)
    cw = dict(
        w_in_t=_split_pad_rows(g_in.reshape(-1, D_MODEL)).astype(MXU_DTYPE),
        wq=jnp.pad(wq, ((0, 0), (0, 0), (0, LANES - e_uq))).astype(MXU_DTYPE),
        wk=_pad_heads(w_uk, NOPE_DIM), wv=_pad_heads(w_uv, HEAD_DIM),
        conv_w=_ffn_interleave(jnp.transpose(g_conv[:, :conv_w.shape[0]], (1, 0, 2)).reshape(conv_w.shape[0], -1), 1),
        g_cq=_row(g_cq), g_ckv=_row(g_ckv), ln1_g=_row(ln1_g), ln1_b=_row(ln1_b), conv_b=_ffn_interleave(_row(conv_b), 1),
        ln2_g=_row(ln2_g), ln2_b=_row(ln2_b))

    def late_weights(after):
        own, landed = _send_wait(late, [after], "gather_late_wait")
        g_o, g_up, g_down = [own_slot(buf, blk) for buf, blk in zip(landed, own)]
        w_o_mla, w_o_dil = _pad_w_o(g_o.reshape(-1, D_MODEL))
        return dict(w_o_mla=w_o_mla, w_o_dil=w_o_dil, w_up_t=_ffn_interleave(g_up.reshape(-1, D_MODEL), 0).astype(MXU_DTYPE),
                    w_down=g_down.reshape(-1, D_MODEL).astype(MXU_DTYPE))

    sent = {}

    def on_grads(stage, g):
        if stage == "ffn":
            sent[stage] = [_send_start([blocks(_ffn_deinterleave(g["w_up_t"], 0)), blocks(g["w_down"])], False, "exchange_ffn_start")]
        elif stage == "w_o":
            sent[stage] = [_send_start([blocks(_unpad_w_o(g["w_o_mla"], g["w_o_dil"]))], False, "exchange_w_o_start")]
        else:
            d_in = blocks(_split_unpad_rows(g["w_in_t"]))
            d_uq = wire(jnp.transpose(g["wq"][:, :, :e_uq].reshape(HEADS, N_DEV, r_uq_dev, e_uq), (1, 0, 2, 3))
                        ).reshape(N_DEV, HEADS * r_uq_dev, e_uq)
            dense = lambda a, width: wire(a[:, :, :width]).reshape(-1, LANES)
            small = dict(g_cq=g["g_cq"], g_ckv=g["g_ckv"], w_uk=dense(g["wk"], NOPE_DIM), w_uv=dense(g["wv"], HEAD_DIM),
                         ln1_g=g["ln1_g"], ln1_b=g["ln1_b"], conv_b=_ffn_deinterleave(g["conv_b"], 1), ln2_g=g["ln2_g"],
                         ln2_b=g["ln2_b"])
            sent[stage] = [_send_start([d_in, d_uq], False, "exchange_rest_start"),
                           _send_start([small[n] for n in REPLICATED] + [_ffn_deinterleave(g["conv_w"], 1), g["loss"]],
                                       True, "gather_small_start")]
        return [h["token"] for h in sent[stage]]

    _, grad_x, _ = _layer_grads(x[0], loss_target[0], cw, [late["token"]], late_weights, on_grads)

    def landed(handle, after, name):
        own, got = _send_wait(handle, after, name)
        pick = (lambda a: a) if handle["gather"] else (lambda a: lax.dynamic_index_in_dim(a, me, 0, keepdims=False))
        return [own_slot(buf, pick(src)) for buf, src in zip(got, own)]

    out = {}

    def update(name, parts, view=None):
        to2d = {None: lambda a: a, "t": lambda a: a.T, "heads": _heads_major}[view]
        back = {None: lambda a: a, "t": lambda a: a.T, "heads": lambda a: _heads_minor(a, HEADS)}[view]
        res = _adamw(parts, to2d(w[name]), to2d(m[name]), to2d(v[name]), "adamw_" + name)
        for kind, a in zip(("grad", "delta", "new_m", "new_v"), res):
            out[kind, name] = back(a)
        return res[0]

    r_up, r_down = landed(sent["ffn"][0], [grad_x], "exchange_ffn_wait")
    (r_o,) = landed(sent["w_o"][0], [grad_x], "exchange_w_o_wait")
    done = [update("w_up", r_up, "t"), update("w_down", r_down), update("w_o", r_o)]
    r_in, r_uq = landed(sent["rest"][0], done, "exchange_rest_wait")
    *rep_all, cw_all, loss_all = landed(sent["rest"][1], done, "gather_small_wait")
    update("w_in", r_in, "t")
    update("w_uq", r_uq, "heads")
    heads_major = ("w_uk", "w_uv")
    two_d = lambda n, a: _heads_major(a) if n in heads_major else a.reshape(1, -1)
    rep_all = [p.reshape(N_DEV, -1, w[n].shape[2]) if n in heads_major else p for n, p in zip(REPLICATED, rep_all)]
    res, loss_sum = _adamw_small(rep_all, *[[two_d(n, d[n]) for n in REPLICATED] for d in (w, m, v)], loss_all, "adamw_replicated")
    for n, quad in zip(REPLICATED, res):
        for kind, a in zip(("grad", "delta", "new_m", "new_v"), quad):
            out[kind, n] = _heads_minor(a, HEADS) if n in heads_major else a.reshape(w[n].shape)
    loss = loss_sum[0, 0]
    ncw = conv_w.shape[1]
    update("conv_w", lax.dynamic_slice_in_dim(cw_all[:, :conv_w.shape[0]], me * ncw, ncw, axis=2))

    return (loss, grad_x[None], *[out[kind, n] for kind in ("grad", "delta", "new_m", "new_v") for n in ALL_WEIGHTS])
```

```python
import functools
import math

import jax
import jax.numpy as jnp
import numpy as np
from jax import lax
from jax.experimental import pallas as pl
from jax.experimental.pallas import tpu as pltpu

F32 = jnp.float32
MXU_DTYPE = jnp.bfloat16
WIRE_DTYPE = jnp.bfloat16

N_DEV = 8
D_MODEL = 1024
HEADS = 8
HEAD_DIM = 64
LANES = 128
Q_RANK, KV_RANK, ROPE_DIM, NOPE_DIM = 256, 128, 32, 64
DIL_WIDTH = HEADS * HEAD_DIM
IN_WIDTH = 1952
IN_PAD = 2048
D_FF = 2816
ROPE_THETA = 10000.0
DIL_PAIRS = ((128, 1), (512, 4), (2048, 16))
DIL_BLOCK = 128
DN_ALPHA = 2.0 ** 0.25
LN_EPS = 1e-5
RMS_EPS = 1e-6
ONES_LANE = HEAD_DIM
MLA_SCALE = 1.0 / math.sqrt(NOPE_DIM + ROPE_DIM)
MLA_SCALE_LOG2 = MLA_SCALE * math.log2(math.e)
DIL_SCALE = 1.0 / math.sqrt(HEAD_DIM)
ALIBI_SLOPES = tuple(2.0 ** (-8.0 * (h + 1) / HEADS) for h in range(HEADS))
NEG_BIG = -1e30
ADAM_LR, ADAM_B1, ADAM_B2, ADAM_EPS, ADAM_WD, ADAM_STEP = 0.001, 0.9, 0.999, 1e-08, 0.01, 10
VMEM_LIMIT = 48 * 1024 * 1024


def _params(*sem):
    return pltpu.CompilerParams(dimension_semantics=sem or None, vmem_limit_bytes=VMEM_LIMIT)


def _dot(a, b, ca, cb):
    return lax.dot_general(a, b, (((ca,), (cb,)), ((), ())), preferred_element_type=F32)


_ANY_SPEC = pl.BlockSpec(memory_space=pl.ANY)


def _mm(a, b, *, name, tm, tn, tk, ta=False, tb=False, out_dtype=F32, res=None, res_scale=1.0, after=()):
    m, k = (a.shape[1], a.shape[0]) if ta else a.shape
    n = b.shape[0] if tb else b.shape[1]
    assert (b.shape[1] if tb else b.shape[0]) == k
    tm, tn, tk = min(tm, m), min(tn, n), min(tk, k)
    assert m % tm == 0 and n % tn == 0 and k % tk == 0, (name, m, n, k, tm, tn, tk)
    nk = k // tk
    a_spec = (pl.BlockSpec((tk, tm), lambda i, j, kk: (kk, i)) if ta
              else pl.BlockSpec((tm, tk), lambda i, j, kk: (i, kk)))
    b_mode = dict(pipeline_mode=pl.Buffered(1)) if (tn == n and tk == k) else {}
    b_spec = (pl.BlockSpec((tn, tk), lambda i, j, kk: (j, kk), **b_mode) if tb
              else pl.BlockSpec((tk, tn), lambda i, j, kk: (kk, j), **b_mode))
    o_spec = pl.BlockSpec((tm, tn), lambda i, j, kk: (i, j))
    in_specs = [a_spec, b_spec]
    args = [a, b]
    if res is not None:
        in_specs.append(o_spec)
        args.append(res)
    n_in = len(args) + len(after)
    in_specs += [_ANY_SPEC] * len(after)
    args += list(after)
    ca, cb = (0 if ta else 1), (1 if tb else 0)

    def finish(acc, r_ref, o_ref):
        if r_ref is not None:
            acc = acc + res_scale * r_ref[...]
        o_ref[...] = acc.astype(o_ref.dtype)

    def body(*refs):
        a_ref, b_ref = refs[:2]
        r_ref = refs[2] if res is not None else None
        o_ref = refs[n_in]
        part = _dot(a_ref[...].astype(MXU_DTYPE), b_ref[...].astype(MXU_DTYPE), ca, cb)
        if nk == 1:
            finish(part, r_ref, o_ref)
            return
        acc_ref = refs[-1]
        kk = pl.program_id(2)

        @pl.when(kk == 0)
        def _():
            acc_ref[...] = part

        @pl.when(kk > 0)
        def _():
            acc_ref[...] += part

        @pl.when(kk == nk - 1)
        def _():
            finish(acc_ref[...], r_ref, o_ref)

    return pl.pallas_call(
        body, name=name, grid=(m // tm, n // tn, nk), in_specs=in_specs, out_specs=o_spec,
        out_shape=jax.ShapeDtypeStruct((m, n), out_dtype),
        scratch_shapes=[pltpu.VMEM((tm, tn), F32)] if nk > 1 else [],
        compiler_params=_params("parallel", "parallel", "arbitrary"),
    )(*args)


def _mm_do_mla(dz, w_o_mla, tm=1024):
    seq, d = dz.shape
    tm = min(tm, seq)

    def body(a_ref, b_ref, o_ref):
        a = a_ref[...].astype(MXU_DTYPE)
        for hd in range(HEADS):
            o_ref[hd] = _dot(a, b_ref[LANES * hd:LANES * (hd + 1), :], 1, 1)

    return pl.pallas_call(
        body, name="mm_do_mla", grid=(seq // tm,),
        in_specs=[pl.BlockSpec((tm, d), lambda i: (i, 0)), pl.BlockSpec((HEADS * LANES, d), lambda i: (0, 0))],
        out_specs=pl.BlockSpec((HEADS, tm, LANES), lambda i: (0, i, 0)),
        out_shape=jax.ShapeDtypeStruct((HEADS, seq, LANES), F32), compiler_params=_params("parallel"),
    )(dz, w_o_mla)


def _mm_dw_o_mla(o_mla, dz, tk=1024):
    seq, d = dz.shape
    tk = min(tk, seq)
    nk = seq // tk

    def body(a_ref, b_ref, o_ref, acc_ref):
        kk = pl.program_id(0)

        @pl.when(kk == 0)
        def _():
            acc_ref[...] = jnp.zeros_like(acc_ref)

        b = b_ref[...].astype(MXU_DTYPE)
        for hd in range(HEADS):
            acc_ref[LANES * hd:LANES * (hd + 1), :] += _dot(a_ref[hd].astype(MXU_DTYPE), b, 0, 0)

        @pl.when(kk == nk - 1)
        def _():
            o_ref[...] = acc_ref[...].astype(o_ref.dtype)

    return pl.pallas_call(
        body, name="mm_dw_o_mla", grid=(nk,),
        in_specs=[pl.BlockSpec((HEADS, tk, LANES), lambda kk: (0, kk, 0)), pl.BlockSpec((tk, d), lambda kk: (kk, 0))],
        out_specs=pl.BlockSpec((HEADS * LANES, d), lambda kk: (0, 0)),
        out_shape=jax.ShapeDtypeStruct((HEADS * LANES, d), MXU_DTYPE),
        scratch_shapes=[pltpu.VMEM((HEADS * LANES, d), F32)], compiler_params=_params("arbitrary"),
    )(o_mla, dz)


def _rope_tables(seq):
    half = ROPE_DIM // 2
    f32 = np.float32
    freqs = np.power(f32(ROPE_THETA), -np.arange(half, dtype=f32) / f32(half))
    ang = np.arange(seq, dtype=f32)[:, None] * freqs[None, :]
    cos, sin = np.cos(ang, dtype=f32), np.sin(ang, dtype=f32)
    one = np.ones((seq, NOPE_DIM), f32)
    tail = np.ones((seq, LANES - NOPE_DIM - ROPE_DIM), f32)
    ctab = np.concatenate([one, cos, cos, tail], axis=1)
    stab = np.concatenate([0 * one, -sin, sin, 0 * tail], axis=1)
    return jnp.asarray(ctab), jnp.asarray(stab)


def _rope_swap(t):
    lane = lax.broadcasted_iota(jnp.int32, t.shape, 1)
    half = ROPE_DIM // 2
    return jnp.where(lane < NOPE_DIM + half, pltpu.roll(t, LANES - half, 1), pltpu.roll(t, half, 1))


def _rope(t, ctab, stab):
    return t * ctab + _rope_swap(t) * stab


def _rope_inv(t, ctab, stab):
    return t * ctab - _rope_swap(t) * stab


def _rms(x, g):
    r = lax.rsqrt(jnp.mean(x * x, axis=-1, keepdims=True) + RMS_EPS)
    xh = x * r
    return xh, r, xh * g


def _mla_prep(h, g_cq, g_ckv, wq, wk, wv, ctab, stab, tm=512):
    seq = h.shape[0]
    tm = min(tm, seq)

    def body(h_ref, gq_ref, gk_ref, wq_ref, wk_ref, wv_ref, c_ref, s_ref, q_out, k_out, v_out):
        hb = h_ref[...]
        ctab_, stab_ = c_ref[...], s_ref[...]
        _, _, cqn = _rms(hb[:, :Q_RANK], gq_ref[...])
        _, _, ckn = _rms(hb[:, Q_RANK:Q_RANK + KV_RANK], gk_ref[...])
        cqn = cqn.astype(MXU_DTYPE)
        ckn = ckn.astype(MXU_DTYPE)
        krr = _rope(hb[:, Q_RANK + KV_RANK:], ctab_, stab_)
        ones_lane = (lax.broadcasted_iota(jnp.int32, (1, LANES), 1) == ONES_LANE).astype(F32)
        for hd in range(HEADS):
            q = _dot(cqn, wq_ref[hd], 1, 0)
            q_out[hd] = _rope(q, ctab_, stab_).astype(q_out.dtype)
            k_out[hd] = (_dot(ckn, wk_ref[hd], 1, 0) + krr).astype(k_out.dtype)
            v_out[hd] = (_dot(ckn, wv_ref[hd], 1, 0) + ones_lane).astype(v_out.dtype)

    full = lambda *shape: pl.BlockSpec(shape, lambda i: (0,) * len(shape))
    slab = pl.BlockSpec((HEADS, tm, LANES), lambda i: (0, i, 0))
    shp = jax.ShapeDtypeStruct((HEADS, seq, LANES), MXU_DTYPE)
    return pl.pallas_call(
        body, name="mla_prep", grid=(seq // tm,),
        in_specs=[pl.BlockSpec((tm, 512), lambda i: (i, 0)), full(1, Q_RANK), full(1, KV_RANK),
                  full(HEADS, Q_RANK, LANES), full(HEADS, KV_RANK, LANES), full(HEADS, KV_RANK, LANES),
                  pl.BlockSpec((tm, LANES), lambda i: (i, 0)), pl.BlockSpec((tm, LANES), lambda i: (i, 0))],
        out_specs=[slab, slab, slab], out_shape=[shp, shp, shp],
        compiler_params=_params("parallel"),
    )(h, g_cq, g_ckv, wq, wk, wv, ctab, stab)


def _mla_prep_bwd(h, g_cq, g_ckv, wq, wk, wv, ctab, stab, dq, dk, dv, tm=512, after=()):
    seq = h.shape[0]
    tm = min(tm, seq)
    n_after = len(after)

    def body(h_ref, gq_ref, gk_ref, wq_ref, wk_ref, wv_ref, c_ref, s_ref, dq_ref, dk_ref, dv_ref, *rest):
        dh_ref, dwq_ref, dwk_ref, dwv_ref, dgq_ref, dgk_ref = rest[n_after:]

        @pl.when(pl.program_id(0) == 0)
        def _():
            for r in (dwq_ref, dwk_ref, dwv_ref, dgq_ref, dgk_ref):
                r[...] = jnp.zeros_like(r)

        hb = h_ref[...]
        ctab_, stab_ = c_ref[...], s_ref[...]
        gq, gk = gq_ref[...], gk_ref[...]
        xq, rq, cqn = _rms(hb[:, :Q_RANK], gq)
        xk, rk, ckn = _rms(hb[:, Q_RANK:Q_RANK + KV_RANK], gk)
        cqn = cqn.astype(MXU_DTYPE)
        ckn = ckn.astype(MXU_DTYPE)
        d_cqn = jnp.zeros((tm, Q_RANK), F32)
        d_ckn = jnp.zeros((tm, KV_RANK), F32)
        d_krr = jnp.zeros((tm, LANES), F32)
        for hd in range(HEADS):
            dqh = _rope_inv(dq_ref[hd], ctab_, stab_).astype(MXU_DTYPE)
            d_cqn += _dot(dqh, wq_ref[hd], 1, 1)
            dwq_ref[hd] += _dot(cqn, dqh, 0, 0)
            dkh = dk_ref[hd]
            d_krr += dkh
            dkh = dkh.astype(MXU_DTYPE)
            d_ckn += _dot(dkh, wk_ref[hd], 1, 1)
            dwk_ref[hd] += _dot(ckn, dkh, 0, 0)
            dvh = dv_ref[hd].astype(MXU_DTYPE)
            d_ckn += _dot(dvh, wv_ref[hd], 1, 1)
            dwv_ref[hd] += _dot(ckn, dvh, 0, 0)
        lane = lax.broadcasted_iota(jnp.int32, (tm, LANES), 1)
        rot = (lane >= NOPE_DIM) & (lane < NOPE_DIM + ROPE_DIM)
        d_kr = jnp.where(rot, _rope_inv(jnp.where(rot, d_krr, 0.0), ctab_, stab_), 0.0)

        def rms_bwd(dy, xh, r, g, dg_ref):
            dg_ref[...] += jnp.sum(dy * xh, axis=0, keepdims=True)
            dxh = dy * g
            return r * (dxh - xh * jnp.mean(dxh * xh, axis=-1, keepdims=True))

        d_cq = rms_bwd(d_cqn, xq, rq, gq, dgq_ref)
        d_ck = rms_bwd(d_ckn, xk, rk, gk, dgk_ref)
        dh_ref[...] = jnp.concatenate([d_cq, d_ck, d_kr], axis=1).astype(dh_ref.dtype)

    full = lambda *shape: pl.BlockSpec(shape, lambda i: (0,) * len(shape))
    slab = pl.BlockSpec((HEADS, tm, LANES), lambda i: (0, i, 0))
    return pl.pallas_call(
        body, name="mla_prep_bwd", grid=(seq // tm,),
        in_specs=[pl.BlockSpec((tm, 512), lambda i: (i, 0)), full(1, Q_RANK), full(1, KV_RANK),
                  full(HEADS, Q_RANK, LANES), full(HEADS, KV_RANK, LANES), full(HEADS, KV_RANK, LANES),
                  pl.BlockSpec((tm, LANES), lambda i: (i, 0)), pl.BlockSpec((tm, LANES), lambda i: (i, 0)),
                  slab, slab, slab] + [_ANY_SPEC] * n_after,
        out_specs=[pl.BlockSpec((tm, 512), lambda i: (i, 0)), full(HEADS, Q_RANK, LANES), full(HEADS, KV_RANK, LANES),
                   full(HEADS, KV_RANK, LANES), full(1, Q_RANK), full(1, KV_RANK)],
        out_shape=[jax.ShapeDtypeStruct((seq, 512), MXU_DTYPE), jax.ShapeDtypeStruct((HEADS, Q_RANK, LANES), F32),
                   jax.ShapeDtypeStruct((HEADS, KV_RANK, LANES), F32), jax.ShapeDtypeStruct((HEADS, KV_RANK, LANES), F32),
                   jax.ShapeDtypeStruct((1, Q_RANK), F32), jax.ShapeDtypeStruct((1, KV_RANK), F32)],
        compiler_params=_params("arbitrary"),
    )(h, g_cq, g_ckv, wq, wk, wv, ctab, stab, dq, dk, dv, *after)


def _causal_mask(t):
    row = lax.broadcasted_iota(jnp.int32, (t, t), 0)
    col = lax.broadcasted_iota(jnp.int32, (t, t), 1)
    return row >= col


def _mla_attn_fwd(q, k, v, t=512):
    _, seq, _ = q.shape
    t = min(t, seq)

    def body(q_ref, k_ref, v_ref, o_ref, ob_ref, lse_ref, m_ref, acc_ref, s_ref):
        i = pl.program_id(1)
        qb = q_ref[...]
        m_ref[...] = jnp.full_like(m_ref, NEG_BIG)
        acc_ref[...] = jnp.zeros_like(acc_ref)

        def scores(j):
            return _dot(qb, k_ref[pl.ds(pl.multiple_of(j * t, t), t), :], 1, 1) * MLA_SCALE_LOG2

        def softmax_pv(j, s, masked):
            vb = v_ref[pl.ds(pl.multiple_of(j * t, t), t), :]
            if masked:
                s = jnp.where(_causal_mask(t), s, NEG_BIG)
            m_old = m_ref[...]
            m_new = jnp.maximum(m_old, jnp.max(s, axis=1, keepdims=True))
            p = jnp.exp2(s - m_new)
            a = jnp.exp2(m_old - m_new)
            acc_ref[...] = a * acc_ref[...] + _dot(p.astype(MXU_DTYPE), vb, 1, 0)
            m_ref[...] = m_new

        s_ref[...] = scores(0)

        def loop_body(j, c):
            s_next = scores(j + 1)
            softmax_pv(j, s_ref[...], False)
            s_ref[...] = s_next
            return c

        lax.fori_loop(0, i, loop_body, 0)
        softmax_pv(i, s_ref[...], True)
        acc = acc_ref[...]
        l = acc[:, ONES_LANE:ONES_LANE + 1]
        o = jnp.where(lax.broadcasted_iota(jnp.int32, acc.shape, 1) < HEAD_DIM, acc * (1.0 / l), 0.0)
        o_ref[...] = o
        ob_ref[...] = o.astype(ob_ref.dtype)
        lse_ref[...] = jnp.broadcast_to(m_ref[...] + jnp.log2(l), lse_ref.shape)

    blk = pl.BlockSpec((None, t, LANES), lambda h, i: (h, i, 0))
    whole = pl.BlockSpec((None, seq, LANES), lambda h, i: (h, 0, 0))
    shp = jax.ShapeDtypeStruct((HEADS, seq, LANES), F32)
    return pl.pallas_call(
        body, name="mla_attn_fwd", grid=(HEADS, seq // t),
        in_specs=[blk, whole, whole], out_specs=[blk, blk, blk],
        out_shape=[shp, jax.ShapeDtypeStruct((HEADS, seq, LANES), MXU_DTYPE), shp],
        scratch_shapes=[pltpu.VMEM((t, 1), F32), pltpu.VMEM((t, LANES), F32), pltpu.VMEM((t, t), F32)],
        compiler_params=_params("parallel", "arbitrary"),
    )(q, k, v)


def _mla_attn_bwd(q, k, v, o, lse, do, t=512):
    _, seq, _ = q.shape
    t = min(t, seq)
    nb = seq // t

    def body(q_ref, k_ref, v_ref, o_ref, lse_ref, do_ref, dq_ref, dk_ref, dv_ref, dl_ref, dka_ref, dva_ref):
        dq_ref[...] = jnp.zeros_like(dq_ref)

        def delta_body(i, c):
            rows = pl.ds(pl.multiple_of(i * t, t), t)
            dl_ref[rows, :] = jnp.sum(do_ref[rows, :] * o_ref[rows, :], axis=1, keepdims=True)
            return c

        lax.fori_loop(0, nb, delta_body, 0)

        def kblock(j, c):
            krows = pl.ds(pl.multiple_of(j * t, t), t)
            kb = k_ref[krows, :]
            vb = v_ref[krows, :]
            dka_ref[...] = jnp.zeros_like(dka_ref)
            dva_ref[...] = jnp.zeros_like(dva_ref)

            def qstep(i, masked):
                th = t // 2
                rows = [pl.ds(pl.multiple_of(i * t + hf * th, th), th) for hf in range(2)]
                qs = [q_ref[r, :] for r in rows]
                dos = [do_ref[r, :].astype(MXU_DTYPE) for r in rows]
                nkeys = [th if masked else t, t]
                ss = [_dot(qs[hf], kb[:nkeys[hf]], 1, 1) * MLA_SCALE_LOG2 for hf in range(2)]
                dps = [_dot(dos[hf], vb[:nkeys[hf]], 1, 1) for hf in range(2)]
                for hf in range(2):
                    s, nk = ss[hf], nkeys[hf]
                    if masked:
                        row = lax.broadcasted_iota(jnp.int32, (th, nk), 0) + hf * th
                        s = jnp.where(row >= lax.broadcasted_iota(jnp.int32, (th, nk), 1), s, NEG_BIG)
                    p = jnp.exp2(s - lse_ref[rows[hf], 0:1])
                    dva_ref[0:nk, :] += _dot(p.astype(MXU_DTYPE), dos[hf], 0, 0)
                    ds = (p * (dps[hf] - dl_ref[rows[hf], :]) * MLA_SCALE).astype(MXU_DTYPE)
                    dka_ref[0:nk, :] += _dot(ds, qs[hf], 0, 0)
                    dq_ref[rows[hf], :] += _dot(ds, kb[:nk], 1, 0)

            qstep(j, True)

            def qloop(i, c2):
                qstep(i, False)
                return c2

            lax.fori_loop(j + 1, nb, qloop, 0)
            dk_ref[krows, :] = dka_ref[...]
            dv_ref[krows, :] = dva_ref[...]
            return c

        lax.fori_loop(0, nb, kblock, 0)

    whole = pl.BlockSpec((None, seq, LANES), lambda h: (h, 0, 0))
    shp = jax.ShapeDtypeStruct((HEADS, seq, LANES), F32)
    return pl.pallas_call(
        body, name="mla_attn_bwd", grid=(HEADS,),
        in_specs=[whole] * 6, out_specs=[whole] * 3, out_shape=[shp] * 3,
        scratch_shapes=[pltpu.VMEM((seq, 1), F32), pltpu.VMEM((t, LANES), F32), pltpu.VMEM((t, LANES), F32)],
        compiler_params=_params("parallel"),
    )(q, k, v, o, lse, do)


DIL_CHUNK = DIL_BLOCK * max(d for _, d in DIL_PAIRS)
DIL_PAIR_LANES = 2 * HEAD_DIM
assert DIL_PAIR_LANES == LANES
DIL_UNROLL_FWD = 16
DIL_UNROLL_BWD = 8


def _dil_bias_tables(hp, dil):
    b = DIL_BLOCK
    iq = lax.broadcasted_iota(jnp.int32, (b, 2 * b), 0)
    ik = lax.broadcasted_iota(jnp.int32, (b, 2 * b), 1)
    off = iq + b - ik
    band = (off >= 0) & (off <= b)
    dist = (off * dil).astype(F32)
    every, first = [], []
    for hh in range(2):
        slope = jnp.where(hp == 0, ALIBI_SLOPES[hh], jnp.where(hp == 1, ALIBI_SLOPES[2 + hh],
                          jnp.where(hp == 2, ALIBI_SLOPES[4 + hh], ALIBI_SLOPES[6 + hh]))).astype(F32)
        bias = -slope * dist
        every.append(jnp.where(band, bias, NEG_BIG))
        first.append(jnp.where(band & (ik >= b), bias, NEG_BIG))
    return jnp.concatenate(every, axis=0), jnp.concatenate(first, axis=0)


def _dil_rows(start, dil):
    return pl.ds(start, DIL_BLOCK) if dil == 1 else pl.ds(start, DIL_BLOCK, stride=dil)


def _dil_block_pos(blk, c, dil):
    sc, r = blk // dil, blk % dil
    q0 = sc * (DIL_BLOCK * dil) + r
    kcur0 = c * DIL_CHUNK + q0
    first = kcur0 < DIL_BLOCK * dil
    kprev0 = jnp.where(first, kcur0, kcur0 - DIL_BLOCK * dil)
    return q0, kcur0, kprev0, first


def _pair_cols(hh):
    return slice(HEAD_DIM * hh, HEAD_DIM * (hh + 1))


def _first_head_lanes(shape):
    return lax.broadcasted_iota(jnp.int32, shape, 1) < HEAD_DIM


def _stack_pair(t):
    first = _first_head_lanes(t.shape)
    return jnp.concatenate([jnp.where(first, t, 0.0), jnp.where(first, 0.0, t)], axis=0).astype(MXU_DTYPE)


def _unstack_pair(t):
    rows = t.shape[0] // 2
    return jnp.where(_first_head_lanes((rows, t.shape[1])), t[:rows], t[rows:])


def _pair_column(t):
    return jnp.concatenate([t[:, 0:1], t[:, HEAD_DIM:HEAD_DIM + 1]], axis=0)


def _dil_fwd(h):
    seq = h.shape[0]
    assert seq % DIL_CHUNK == 0
    nblk = DIL_CHUNK // DIL_BLOCK
    rc = 256

    def body(q_ref, k_ref, v_ref, o_ref, ob_ref, lse_ref, *scr):
        o_scr, l_scr = scr[:3], scr[3:]
        hp, c = pl.program_id(0), pl.program_id(1)
        for bi, (_, dil) in enumerate(DIL_PAIRS):
            tables = _dil_bias_tables(hp, dil)

            def block(blk, carry, bi=bi, dil=dil, tables=tables):
                q0, kcur0, kprev0, first = _dil_block_pos(blk, c, dil)
                q2 = _stack_pair(q_ref[_dil_rows(q0, dil), :] * DIL_SCALE)
                kcat = jnp.concatenate([k_ref[_dil_rows(kprev0, dil), :], k_ref[_dil_rows(kcur0, dil), :]], axis=0).astype(MXU_DTYPE)
                vcat = jnp.concatenate([v_ref[_dil_rows(kprev0, dil), :], v_ref[_dil_rows(kcur0, dil), :]], axis=0).astype(MXU_DTYPE)
                s = _dot(q2, kcat, 1, 1) + jnp.where(first, tables[1], tables[0])
                mx = jnp.max(s, axis=1, keepdims=True)
                p = jnp.exp(s - mx)
                l = jnp.sum(p, axis=1, keepdims=True)
                o_scr[bi][_dil_rows(q0, dil), :] = _unstack_pair(_dot(p.astype(MXU_DTYPE), vcat, 1, 0) * (1.0 / l))
                l_scr[bi][_dil_rows(q0, dil), :] = _unstack_pair(jnp.broadcast_to(mx + jnp.log(l), (2 * DIL_BLOCK, LANES)))
                return carry

            lax.fori_loop(0, nblk, block, 0, unroll=DIL_UNROLL_FWD)

        def combine(i, carry):
            rows = pl.ds(pl.multiple_of(i * rc, rc), rc)
            ls = [l_scr[bi][rows, :] for bi in range(3)]
            mx = jnp.maximum(jnp.maximum(ls[0], ls[1]), ls[2])
            es = [jnp.exp(l - mx) for l in ls]
            den = es[0] + es[1] + es[2]
            o = (es[0] * o_scr[0][rows, :] + es[1] * o_scr[1][rows, :] + es[2] * o_scr[2][rows, :]) / den
            o_ref[rows, :] = o
            ob_ref[rows, :] = o.astype(ob_ref.dtype)
            lse_ref[rows, :] = mx + jnp.log(den)
            return carry

        lax.fori_loop(0, DIL_CHUNK // rc, combine, 0)

    nq = DIL_WIDTH // LANES
    chunk = lambda off: pl.BlockSpec((DIL_CHUNK, LANES), lambda hp, c: (c, off + hp))
    whole = lambda off: pl.BlockSpec((seq, LANES), lambda hp, c: (0, off + hp))
    shp = jax.ShapeDtypeStruct((seq, DIL_WIDTH), F32)
    return pl.pallas_call(
        body, name="dil_fwd", grid=(nq, seq // DIL_CHUNK),
        in_specs=[chunk(nq), whole(2 * nq), whole(3 * nq)], out_specs=[chunk(0), chunk(0), chunk(0)],
        out_shape=[shp, jax.ShapeDtypeStruct((seq, DIL_WIDTH), MXU_DTYPE), shp],
        scratch_shapes=[pltpu.VMEM((DIL_CHUNK, LANES), F32)] * 6,
        compiler_params=_params("parallel", "arbitrary"),
    )(h, h, h)


def _dil_bwd(h, o, lse, do):
    seq = h.shape[0]
    nblk = DIL_CHUNK // DIL_BLOCK
    nchunk = seq // DIL_CHUNK
    rc = 256

    def body(q_ref, k_ref, v_ref, o_ref, lse_ref, do_ref, dq_out, dk_out, dv_out, dl_scr, dq_ref, dk_ref, dv_ref):
        hp, c = pl.program_id(0), pl.program_id(1)

        @pl.when(c == 0)
        def _():
            dk_ref[...] = jnp.zeros_like(dk_ref)
            dv_ref[...] = jnp.zeros_like(dv_ref)

        def delta(i, carry):
            rows = pl.ds(pl.multiple_of(i * rc, rc), rc)
            prod = do_ref[rows, :] * o_ref[rows, :]
            dl_scr[rows, :] = jnp.concatenate(
                [jnp.broadcast_to(jnp.sum(prod[:, _pair_cols(hh)], axis=1, keepdims=True), (rc, HEAD_DIM)) for hh in range(2)], axis=1)
            return carry

        lax.fori_loop(0, DIL_CHUNK // rc, delta, 0)

        for bi, (_, dil) in enumerate(DIL_PAIRS):
            tables = _dil_bias_tables(hp, dil)

            def block(blk, carry, bi=bi, dil=dil, tables=tables):
                q0, kcur0, kprev0, first = _dil_block_pos(blk, c, dil)
                qrows = _dil_rows(q0, dil)
                q2 = _stack_pair(q_ref[qrows, :] * DIL_SCALE)
                kcat = jnp.concatenate([k_ref[_dil_rows(kprev0, dil), :], k_ref[_dil_rows(kcur0, dil), :]], axis=0).astype(MXU_DTYPE)
                vcat = jnp.concatenate([v_ref[_dil_rows(kprev0, dil), :], v_ref[_dil_rows(kcur0, dil), :]], axis=0).astype(MXU_DTYPE)
                do2 = _stack_pair(do_ref[qrows, :])
                s = _dot(q2, kcat, 1, 1) + jnp.where(first, tables[1], tables[0])
                p = jnp.exp(s - _pair_column(lse_ref[qrows, :]))
                dp = _dot(do2, vcat, 1, 1)
                ds = (p * (dp - _pair_column(dl_scr[qrows, :]))).astype(MXU_DTYPE)
                dq_b = _unstack_pair(_dot(ds, kcat, 1, 0)) * DIL_SCALE
                dk_b = _dot(ds, q2, 0, 0)
                dv_b = _dot(p.astype(MXU_DTYPE), do2, 0, 0)
                if bi == 0:
                    dq_ref[qrows, :] = dq_b
                else:
                    dq_ref[qrows, :] += dq_b
                dk_ref[_dil_rows(kprev0, dil), :] += dk_b[:DIL_BLOCK]
                dv_ref[_dil_rows(kprev0, dil), :] += dv_b[:DIL_BLOCK]
                dk_ref[_dil_rows(kcur0, dil), :] += dk_b[DIL_BLOCK:]
                dv_ref[_dil_rows(kcur0, dil), :] += dv_b[DIL_BLOCK:]
                return carry

            lax.fori_loop(0, nblk, block, 0, unroll=DIL_UNROLL_BWD)

        dq_out[...] = dq_ref[...].astype(dq_out.dtype)

        @pl.when(c == nchunk - 1)
        def _():
            dk_out[...] = dk_ref[...].astype(dk_out.dtype)
            dv_out[...] = dv_ref[...].astype(dv_out.dtype)

    nq = DIL_WIDTH // LANES
    chunk = lambda off: pl.BlockSpec((DIL_CHUNK, LANES), lambda hp, c: (c, off + hp))
    whole = lambda off: pl.BlockSpec((seq, LANES), lambda hp, c: (0, off + hp))
    shp = jax.ShapeDtypeStruct((seq, DIL_WIDTH), MXU_DTYPE)
    return pl.pallas_call(
        body, name="dil_bwd", grid=(nq, nchunk),
        in_specs=[chunk(nq), whole(2 * nq), whole(3 * nq), chunk(0), chunk(0), chunk(0)],
        out_specs=[chunk(0), whole(0), whole(0)], out_shape=[shp, shp, shp],
        scratch_shapes=[pltpu.VMEM((DIL_CHUNK, LANES), F32), pltpu.VMEM((DIL_CHUNK, LANES), F32),
                        pltpu.VMEM((seq, LANES), F32), pltpu.VMEM((seq, LANES), F32)],
        compiler_params=_params("parallel", "arbitrary"),
    )(h, h, h, o, lse, do)


DH_PART = IN_PAD // 4


def _mm_dx0(parts, w_in_t, res, tm=1024, after=()):
    seq, d = res.shape
    tm = min(tm, seq)
    n_after = len(after)

    def body(a0, a1, a2, a3, b_ref, r_ref, *rest):
        o_ref = rest[n_after]
        acc = _dot(a0[...], b_ref[0:DH_PART, :], 1, 0)
        for c, a in enumerate((a1, a2, a3), start=1):
            acc += _dot(a[...], b_ref[DH_PART * c:DH_PART * (c + 1), :], 1, 0)
        o_ref[...] = acc + DN_ALPHA * r_ref[...]

    blk = pl.BlockSpec((tm, DH_PART), lambda i: (i, 0))
    row = pl.BlockSpec((tm, d), lambda i: (i, 0))
    return pl.pallas_call(
        body, name="mm_dx0", grid=(seq // tm,),
        in_specs=[blk] * 4 + [pl.BlockSpec((IN_PAD, d), lambda i: (0, 0), pipeline_mode=pl.Buffered(1)), row] + [_ANY_SPEC] * n_after,
        out_specs=row, out_shape=jax.ShapeDtypeStruct((seq, d), F32), compiler_params=_params("parallel"),
    )(*parts, w_in_t, res, *after)


def _mm_dw_in(parts, x0b, tk=1024):
    seq, d = x0b.shape
    tk = min(tk, seq)
    nk = seq // tk

    def body(a0, a1, a2, a3, b_ref, o_ref, acc_ref):
        kk = pl.program_id(0)

        @pl.when(kk == 0)
        def _():
            acc_ref[...] = jnp.zeros_like(acc_ref)

        b = b_ref[...]
        for c, a in enumerate((a0, a1, a2, a3)):
            acc_ref[DH_PART * c:DH_PART * (c + 1), :] += _dot(a[...], b, 0, 0)

        @pl.when(kk == nk - 1)
        def _():
            o_ref[...] = acc_ref[...].astype(o_ref.dtype)

    blk = pl.BlockSpec((tk, DH_PART), lambda kk: (kk, 0))
    return pl.pallas_call(
        body, name="mm_dw_in", grid=(nk,), in_specs=[blk] * 4 + [pl.BlockSpec((tk, d), lambda kk: (kk, 0))],
        out_specs=pl.BlockSpec((IN_PAD, d), lambda kk: (0, 0)), out_shape=jax.ShapeDtypeStruct((IN_PAD, d), MXU_DTYPE),
        scratch_shapes=[pltpu.VMEM((IN_PAD, d), F32)], compiler_params=_params("arbitrary"),
    )(*parts, x0b)


def _ln_stats(z):
    mu = jnp.mean(z, axis=-1, keepdims=True)
    zc = z - mu
    r = lax.rsqrt(jnp.mean(zc * zc, axis=-1, keepdims=True) + LN_EPS)
    return zc * r, r


def _ln_bwd_math(dy, xh, r, g):
    dxh = dy * g
    return r * (dxh - jnp.mean(dxh, axis=-1, keepdims=True) - xh * jnp.mean(dxh * xh, axis=-1, keepdims=True))


def _mix_ln1(o_mla, o_dil, w_o_mla, w_o_dil, x0, g, b, tm=512):
    seq, d = x0.shape
    tm = min(tm, seq)

    def body(om_ref, od_ref, wm_ref, wd_ref, x_ref, g_ref, b_ref, z_ref, y_ref, yb_ref):
        mix = _dot(od_ref[...], wd_ref[...], 1, 0)
        for hd in range(HEADS):
            mix += _dot(om_ref[hd], wm_ref[LANES * hd:LANES * (hd + 1), :], 1, 0)
        z = DN_ALPHA * x_ref[...] + mix
        xh, _ = _ln_stats(z)
        y = xh * g_ref[...] + b_ref[...]
        z_ref[...] = z
        y_ref[...] = y
        yb_ref[...] = y.astype(yb_ref.dtype)

    blk = pl.BlockSpec((tm, d), lambda i: (i, 0))
    vec = pl.BlockSpec((1, d), lambda i: (0, 0))
    shp = jax.ShapeDtypeStruct((seq, d), F32)
    return pl.pallas_call(
        body, name="mix_ln1", grid=(seq // tm,),
        in_specs=[pl.BlockSpec((HEADS, tm, LANES), lambda i: (0, i, 0)), pl.BlockSpec((tm, DIL_WIDTH), lambda i: (i, 0)),
                  pl.BlockSpec((HEADS * LANES, d), lambda i: (0, 0)), pl.BlockSpec((DIL_WIDTH, d), lambda i: (0, 0)), blk, vec, vec],
        out_specs=[blk, blk, blk], out_shape=[shp, shp, jax.ShapeDtypeStruct((seq, d), MXU_DTYPE)],
        compiler_params=_params("parallel"))(o_mla, o_dil, w_o_mla, w_o_dil, x0, g, b)


def _ln_bwd(dy, z, g, name, tm=512, after=()):
    seq, d = z.shape
    tm = min(tm, seq)
    n_after = len(after)

    def body(dy_ref, z_ref, g_ref, *rest):
        dz_ref, dzb_ref, dg_ref, db_ref = rest[n_after:]

        @pl.when(pl.program_id(0) == 0)
        def _():
            dg_ref[...] = jnp.zeros_like(dg_ref)
            db_ref[...] = jnp.zeros_like(db_ref)

        dyb = dy_ref[...]
        xh, r = _ln_stats(z_ref[...])
        dg_ref[...] += jnp.sum(dyb * xh, axis=0, keepdims=True)
        db_ref[...] += jnp.sum(dyb, axis=0, keepdims=True)
        dz = _ln_bwd_math(dyb, xh, r, g_ref[...])
        dz_ref[...] = dz
        dzb_ref[...] = dz.astype(dzb_ref.dtype)

    blk = pl.BlockSpec((tm, d), lambda i: (i, 0))
    vec = pl.BlockSpec((1, d), lambda i: (0, 0))
    return pl.pallas_call(
        body, name=name, grid=(seq // tm,), in_specs=[blk, blk, vec] + [_ANY_SPEC] * n_after, out_specs=[blk, blk, vec, vec],
        out_shape=[jax.ShapeDtypeStruct((seq, d), F32), jax.ShapeDtypeStruct((seq, d), MXU_DTYPE),
                   jax.ShapeDtypeStruct((1, d), F32), jax.ShapeDtypeStruct((1, d), F32)],
        compiler_params=_params("arbitrary"))(dy, z, g, *after)


def _ln2_loss_bwd(x1, ffn, target, g, b, tm=512):
    seq, d = x1.shape
    tm = min(tm, seq)

    def body(x_ref, f_ref, t_ref, g_ref, b_ref, dz_ref, dzb_ref, loss_ref, dg_ref, db_ref):
        @pl.when(pl.program_id(0) == 0)
        def _():
            loss_ref[...] = jnp.zeros_like(loss_ref)
            dg_ref[...] = jnp.zeros_like(dg_ref)
            db_ref[...] = jnp.zeros_like(db_ref)

        gv = g_ref[...]
        z = DN_ALPHA * x_ref[...] + f_ref[...]
        xh, r = _ln_stats(z)
        err = (xh * gv + b_ref[...]) - t_ref[...]
        loss_ref[...] += 0.5 * jnp.sum(jnp.mean(err * err, axis=-1, keepdims=True), axis=0, keepdims=True)
        dy = err * (1.0 / d)
        dg_ref[...] += jnp.sum(dy * xh, axis=0, keepdims=True)
        db_ref[...] += jnp.sum(dy, axis=0, keepdims=True)
        dz = _ln_bwd_math(dy, xh, r, gv)
        dz_ref[...] = dz
        dzb_ref[...] = dz.astype(dzb_ref.dtype)

    blk = pl.BlockSpec((tm, d), lambda i: (i, 0))
    vec = pl.BlockSpec((1, d), lambda i: (0, 0))
    return pl.pallas_call(
        body, name="ln2_loss_bwd", grid=(seq // tm,), in_specs=[blk, blk, blk, vec, vec],
        out_specs=[blk, blk, pl.BlockSpec((1, LANES), lambda i: (0, 0)), vec, vec],
        out_shape=[jax.ShapeDtypeStruct((seq, d), F32), jax.ShapeDtypeStruct((seq, d), MXU_DTYPE),
                   jax.ShapeDtypeStruct((1, LANES), F32),
                   jax.ShapeDtypeStruct((1, d), F32), jax.ShapeDtypeStruct((1, d), F32)],
        compiler_params=_params("arbitrary"))(x1, ffn, target, g, b)


HALO = 16


def _conv_rows(e, w_ref, b_ref):
    y = b_ref[...] + w_ref[0:1, :] * pltpu.roll(e, 2, 0)
    y = y + w_ref[1:2, :] * pltpu.roll(e, 1, 0)
    return y + w_ref[2:3, :] * e


_GELU_C = math.sqrt(2.0 / math.pi)
_GELU_A = 0.044715


def _gelu(x):
    return 0.5 * x * (1.0 + jnp.tanh(_GELU_C * (x + _GELU_A * (x * x * x))))


CONV_TN = 256


def _ffn_interleave(a, axis):
    shp = a.shape
    a = a.reshape(shp[:axis] + (2, D_FF // CONV_TN, CONV_TN) + shp[axis + 1:])
    return jnp.swapaxes(a, axis, axis + 1).reshape(shp)


def _ffn_deinterleave(a, axis):
    shp = a.shape
    a = a.reshape(shp[:axis] + (D_FF // CONV_TN, 2, CONV_TN) + shp[axis + 1:])
    return jnp.swapaxes(a, axis, axis + 1).reshape(shp)


def _conv_gate_fwd(u, conv_w, conv_b, tm=1024):
    seq = u.shape[0]
    tm = min(tm, seq)
    tn = CONV_TN

    def body(u_ref, up_ref, w_ref, b_ref, o_ref):
        first = pl.program_id(0) == 0
        e = jnp.concatenate([jnp.where(first, 0.0, up_ref[...]), u_ref[...]], axis=0)
        y = _conv_rows(e, w_ref, b_ref)[HALO:]
        o_ref[...] = (_gelu(y[:, tn:]) * y[:, :tn]).astype(o_ref.dtype)

    hb = tm // HALO
    return pl.pallas_call(
        body, name="conv_gate_fwd", grid=(seq // tm, D_FF // tn),
        in_specs=[pl.BlockSpec((tm, 2 * tn), lambda i, j: (i, j)),
                  pl.BlockSpec((HALO, 2 * tn), lambda i, j: (jnp.maximum(i * hb - 1, 0), j)),
                  pl.BlockSpec((3, 2 * tn), lambda i, j: (0, j)), pl.BlockSpec((1, 2 * tn), lambda i, j: (0, j))],
        out_specs=pl.BlockSpec((tm, tn), lambda i, j: (i, j)), out_shape=jax.ShapeDtypeStruct((seq, D_FF), MXU_DTYPE),
        compiler_params=_params("parallel", "parallel"),
    )(u, u, conv_w, conv_b)


def _conv_gate_bwd(u, d_act, conv_w, conv_b, tm=1024):
    seq = u.shape[0]
    tm = min(tm, seq)
    tn = CONV_TN
    ni = seq // tm
    rows_e = tm + 2 * HALO

    def body(u_ref, up_ref, un_ref, da_ref, dan_ref, w_ref, b_ref, du_ref, dw_ref, db_ref):
        i = pl.program_id(1)
        first, last = i == 0, i == ni - 1

        @pl.when(i == 0)
        def _():
            dw_ref[...] = jnp.zeros_like(dw_ref)
            db_ref[...] = jnp.zeros_like(db_ref)

        e = jnp.concatenate([jnp.where(first, 0.0, up_ref[...]), u_ref[...], jnp.where(last, 0.0, un_ref[...])], axis=0)
        y = _conv_rows(e, w_ref, b_ref)
        ya, yg = y[:, :tn], y[:, tn:]
        dact = jnp.concatenate([jnp.zeros((HALO, tn), F32), da_ref[...].astype(F32),
                                jnp.where(last, 0.0, dan_ref[...].astype(F32))], axis=0)
        th = jnp.tanh(_GELU_C * (yg + _GELU_A * (yg * yg * yg)))
        gelu = 0.5 * yg * (1.0 + th)
        gelu_grad = 0.5 * (1.0 + th) + 0.5 * yg * (1.0 - th * th) * (_GELU_C * (1.0 + 3.0 * _GELU_A * (yg * yg)))
        dy = jnp.concatenate([dact * gelu, dact * ya * gelu_grad], axis=1)
        du = w_ref[2:3, :] * dy + w_ref[1:2, :] * pltpu.roll(dy, rows_e - 1, 0) + w_ref[0:1, :] * pltpu.roll(dy, rows_e - 2, 0)
        du_ref[...] = du[HALO:HALO + tm].astype(du_ref.dtype)
        dyt = dy[HALO:HALO + tm]
        dw_ref[0:1, :] += jnp.sum(dyt * pltpu.roll(e, 2, 0)[HALO:HALO + tm], axis=0, keepdims=True)
        dw_ref[1:2, :] += jnp.sum(dyt * pltpu.roll(e, 1, 0)[HALO:HALO + tm], axis=0, keepdims=True)
        dw_ref[2:3, :] += jnp.sum(dyt * e[HALO:HALO + tm], axis=0, keepdims=True)
        db_ref[...] += jnp.sum(dyt, axis=0, keepdims=True)

    hb = tm // HALO
    nh = seq // HALO
    prev = lambda j, i: (jnp.maximum(i * hb - 1, 0), j)
    nxt = lambda j, i: (jnp.minimum((i + 1) * hb, nh - 1), j)
    return pl.pallas_call(
        body, name="conv_gate_bwd", grid=(D_FF // tn, ni),
        in_specs=[pl.BlockSpec((tm, 2 * tn), lambda j, i: (i, j)), pl.BlockSpec((HALO, 2 * tn), prev),
                  pl.BlockSpec((HALO, 2 * tn), nxt), pl.BlockSpec((tm, tn), lambda j, i: (i, j)), pl.BlockSpec((HALO, tn), nxt),
                  pl.BlockSpec((3, 2 * tn), lambda j, i: (0, j)), pl.BlockSpec((1, 2 * tn), lambda j, i: (0, j))],
        out_specs=[pl.BlockSpec((tm, 2 * tn), lambda j, i: (i, j)), pl.BlockSpec((3, 2 * tn), lambda j, i: (0, j)),
                   pl.BlockSpec((1, 2 * tn), lambda j, i: (0, j))],
        out_shape=[jax.ShapeDtypeStruct((seq, 2 * D_FF), MXU_DTYPE), jax.ShapeDtypeStruct((3, 2 * D_FF), F32),
                   jax.ShapeDtypeStruct((1, 2 * D_FF), F32)],
        compiler_params=_params("parallel", "arbitrary"),
    )(u, u, u, d_act, d_act, conv_w, conv_b)


def _pad_heads(w, width):
    w = jnp.transpose(w, (1, 0, 2))
    return jnp.pad(w, ((0, 0), (0, 0), (0, LANES - width))).astype(MXU_DTYPE)


def _heads_major(a):
    return jnp.transpose(a, (1, 0, 2)).reshape(-1, a.shape[2])


def _heads_minor(a, heads):
    return jnp.transpose(a.reshape(heads, -1, a.shape[1]), (1, 0, 2))


def _unpad_heads(d, width):
    return jnp.transpose(d[:, :, :width], (1, 0, 2))


def _split_pad_rows(w_t):
    z = lambda n: jnp.zeros((n, w_t.shape[1]), w_t.dtype)
    return jnp.concatenate([w_t[:384], z(64), w_t[384:416], z(32), w_t[416:]], axis=0)


def _split_unpad_rows(w_p):
    return jnp.concatenate([w_p[:384], w_p[448:480], w_p[512:]], axis=0)


def _pad_w_o(w_o):
    mla = jnp.pad(w_o[:512].reshape(HEADS, HEAD_DIM, D_MODEL), ((0, 0), (0, LANES - HEAD_DIM), (0, 0)))
    return mla.reshape(HEADS * LANES, D_MODEL).astype(MXU_DTYPE), w_o[512:].astype(MXU_DTYPE)


def _unpad_w_o(d_mla, d_dil):
    return jnp.concatenate([d_mla.reshape(HEADS, LANES, D_MODEL)[:, :HEAD_DIM].reshape(512, D_MODEL), d_dil], axis=0)


def _row(v):
    return v.reshape(1, -1).astype(F32)


def _compute_weights(w):
    w_o_mla, w_o_dil = _pad_w_o(w["w_o"])
    return dict(
        w_in_t=_split_pad_rows(w["w_in"].T).astype(MXU_DTYPE), wq=_pad_heads(w["w_uq"], NOPE_DIM + ROPE_DIM),
        wk=_pad_heads(w["w_uk"], NOPE_DIM), wv=_pad_heads(w["w_uv"], HEAD_DIM), w_o_mla=w_o_mla, w_o_dil=w_o_dil,
        w_up_t=_ffn_interleave(w["w_up"].T, 0).astype(MXU_DTYPE), w_down=w["w_down"].astype(MXU_DTYPE),
        conv_w=_ffn_interleave(w["conv_w"].astype(F32), 1),
        g_cq=_row(w["g_cq"]), g_ckv=_row(w["g_ckv"]), ln1_g=_row(w["ln1_g"]), ln1_b=_row(w["ln1_b"]),
        conv_b=_ffn_interleave(_row(w["conv_b"]), 1), ln2_g=_row(w["ln2_g"]), ln2_b=_row(w["ln2_b"]))


def _natural_grads(g):
    return dict(
        w_in=_split_unpad_rows(g["w_in_t"]).T, g_cq=g["g_cq"].reshape(-1), g_ckv=g["g_ckv"].reshape(-1),
        w_uq=_unpad_heads(g["wq"], NOPE_DIM + ROPE_DIM), w_uk=_unpad_heads(g["wk"], NOPE_DIM),
        w_uv=_unpad_heads(g["wv"], HEAD_DIM), w_o=_unpad_w_o(g["w_o_mla"], g["w_o_dil"]), ln1_g=g["ln1_g"].reshape(-1),
        ln1_b=g["ln1_b"].reshape(-1), w_up=_ffn_deinterleave(g["w_up_t"], 0).T, conv_w=_ffn_deinterleave(g["conv_w"], 1),
        conv_b=_ffn_deinterleave(g["conv_b"], 1).reshape(-1),
        w_down=g["w_down"], ln2_g=g["ln2_g"].reshape(-1), ln2_b=g["ln2_b"].reshape(-1))


def _layer_grads(x0, target, cw, first_after=(), late_weights=None, on_grads=None):
    seq = x0.shape[0]
    ctab, stab = _rope_tables(seq)
    gq, gk = cw["g_cq"], cw["g_ckv"]
    wq, wk, wv = cw["wq"], cw["wk"], cw["wv"]
    x0b = x0.astype(MXU_DTYPE)
    notify = (lambda stage, grads: ()) if on_grads is None else on_grads

    h = _mm(x0b, cw["w_in_t"], name="mm_h", tb=True, tm=1024, tn=IN_PAD, tk=1024, after=first_after)
    qf, kf, vp = _mla_prep(h, gq, gk, wq, wk, wv, ctab, stab)
    o_mla, o_mla_b, lse_mla = _mla_attn_fwd(qf, kf, vp)
    o_dil, o_dil_b, lse_dil = _dil_fwd(h)
    if late_weights is not None:
        cw = {**cw, **late_weights(o_mla_b)}
    cb = cw["conv_b"]
    z1, x1, x1b = _mix_ln1(o_mla_b, o_dil_b, cw["w_o_mla"], cw["w_o_dil"], x0, cw["ln1_g"], cw["ln1_b"])
    u = _mm(x1b, cw["w_up_t"], name="mm_up", tb=True, tm=512, tn=2 * D_FF, tk=1024)
    act = _conv_gate_fwd(u, cw["conv_w"], cb)
    ffn = _mm(act, cw["w_down"], name="mm_down", tm=1024, tn=1024, tk=2816)
    dz2, dz2b, loss, d_ln2_g, d_ln2_b = _ln2_loss_bwd(x1, ffn, target, cw["ln2_g"], cw["ln2_b"])

    d_act = _mm(dz2b, cw["w_down"], name="mm_d_act", tb=True, out_dtype=MXU_DTYPE, tm=1024, tn=D_FF, tk=1024)
    d_w_down = _mm(act, dz2b, name="mm_dw_down", ta=True, out_dtype=MXU_DTYPE, tm=1408, tn=1024, tk=1024)
    du, d_conv_w, d_conv_b = _conv_gate_bwd(u, d_act, cw["conv_w"], cb)
    dx1 = _mm(du, cw["w_up_t"], name="mm_dx1", res=dz2, res_scale=DN_ALPHA, tm=512, tn=1024, tk=2 * D_FF)
    d_w_up_t = _mm(du, x1b, name="mm_dw_up", ta=True, out_dtype=MXU_DTYPE, tm=1408, tn=1024, tk=2048)
    grads = dict(w_up_t=d_w_up_t, w_down=d_w_down, conv_w=d_conv_w, conv_b=d_conv_b, ln2_g=d_ln2_g, ln2_b=d_ln2_b)
    dz1, dz1b, d_ln1_g, d_ln1_b = _ln_bwd(dx1, z1, cw["ln1_g"], "ln1_bwd", after=notify("ffn", grads))
    do_mla = _mm_do_mla(dz1b, cw["w_o_mla"])
    do_dil = _mm(dz1b, cw["w_o_dil"], name="mm_do_dil", tb=True, tm=1024, tn=512, tk=1024)
    d_w_o_mla = _mm_dw_o_mla(o_mla_b, dz1b)
    d_w_o_dil = _mm(o_dil_b, dz1b, name="mm_dw_o_dil", ta=True, out_dtype=MXU_DTYPE, tm=512, tn=1024, tk=1024)
    grads.update(w_o_mla=d_w_o_mla, w_o_dil=d_w_o_dil, ln1_g=d_ln1_g, ln1_b=d_ln1_b)
    dq_dil, dk_dil, dv_dil = _dil_bwd(h, o_dil, lse_dil, do_dil)
    dqf, dkf, dvf = _mla_attn_bwd(qf, kf, vp, o_mla, lse_mla, do_mla)
    dh_mla, d_wq, d_wk, d_wv, d_gq, d_gk = _mla_prep_bwd(h, gq, gk, wq, wk, wv, ctab, stab, dqf, dkf, dvf,
                                                          after=notify("w_o", grads))
    dh = (dh_mla, dq_dil, dk_dil, dv_dil)
    d_w_in_t = _mm_dw_in(dh, x0b)
    grads.update(w_in_t=d_w_in_t, wq=d_wq, wk=d_wk, wv=d_wv, g_cq=d_gq, g_ckv=d_gk, loss=loss)
    grad_x = _mm_dx0(dh, cw["w_in_t"], dz1, after=notify("rest", grads))
    return loss, grad_x, grads


def _all_gather(blocks, name):
    na = len(blocks)

    def body(*refs):
        ins, outs = refs[:na], refs[na:2 * na]
        send_sems, recv_sems, local_sems = refs[2 * na:]
        x, y, c = lax.axis_index("x"), lax.axis_index("y"), lax.axis_index("c")
        me, sibling = (x, y, c), (x, y, 1 - c)
        chips = [(1 - x, y), (x, 1 - y), (1 - x, 1 - y)]

        def slot(out, pos):
            return out.at[4 * pos[0] + 2 * pos[1] + pos[2]]

        def copy(a, k, block, to, src=None):
            return pltpu.make_async_remote_copy(
                src_ref=slot(outs[a], block) if src is None else src, dst_ref=slot(outs[a], block),
                send_sem=send_sems.at[7 * a + k], recv_sem=recv_sems.at[7 * a + k],
                device_id=to, device_id_type=pl.DeviceIdType.MESH)

        mine = [pltpu.make_async_copy(ins[a], slot(outs[a], me), local_sems.at[a]) for a in range(na)]
        for cp in mine:
            cp.start()
        first = []
        for a in range(na):
            first.append(copy(a, 0, me, sibling, src=ins[a]))
            first += [copy(a, 1 + j, me, (*chip, c), src=ins[a]) for j, chip in enumerate(chips)]
        for cp in first:
            cp.start()
        passed = []
        for j, chip in enumerate(chips):
            for a in range(na):
                copy(a, 1 + j, (*chip, c), me).wait_recv()
                cp = copy(a, 4 + j, (*chip, c), sibling)
                cp.start()
                passed.append(cp)
        for a in range(na):
            copy(a, 0, sibling, me).wait_recv()
            for j, chip in enumerate(chips):
                copy(a, 4 + j, (*chip, 1 - c), me).wait_recv()
        for cp in first + passed:
            cp.wait_send()
        for cp in mine:
            cp.wait()

    any_spec = pl.BlockSpec(memory_space=pl.ANY)
    return pl.pallas_call(
        body, name=name, in_specs=[any_spec] * na, out_specs=[any_spec] * na,
        out_shape=[jax.ShapeDtypeStruct((N_DEV,) + b.shape, b.dtype) for b in blocks],
        scratch_shapes=[pltpu.SemaphoreType.DMA((7 * na,)), pltpu.SemaphoreType.DMA((7 * na,)), pltpu.SemaphoreType.DMA((na,))],
    )(*blocks)


_HBM_SPEC = pl.BlockSpec(memory_space=pltpu.HBM)
_SEM_SPEC = pl.BlockSpec(memory_space=pltpu.SEMAPHORE)
_DATAFLOW = pltpu.CompilerParams(has_side_effects=pltpu.SideEffectType.DATAFLOW_SIDE_EFFECTING)


def _split_copies(ins, lands, send_sems, recv_sems, gather):
    x, y, c = lax.axis_index("x"), lax.axis_index("y"), lax.axis_index("c")
    me = 4 * x + 2 * y + c
    copies = []
    for a in range(len(ins)):
        for d in range(1, N_DEV):
            px, py, pc = x ^ (d >> 2), y ^ ((d >> 1) & 1), c ^ (d & 1)
            copies.append(pltpu.make_async_remote_copy(
                src_ref=ins[a] if gather else ins[a].at[4 * px + 2 * py + pc], dst_ref=lands[a].at[me],
                send_sem=send_sems.at[7 * a + d - 1], recv_sem=recv_sems.at[7 * a + d - 1],
                device_id=(px, py, pc), device_id_type=pl.DeviceIdType.MESH))
    return copies


def _send_start(srcs, gather, name):
    na = len(srcs)
    land_types = [pltpu.HBM(((N_DEV,) + s.shape) if gather else s.shape, s.dtype) for s in srcs]

    def body(*refs):
        ins, lands = refs[:na], refs[na:2 * na]
        send_sems, recv_sems, token = refs[2 * na], refs[2 * na + 1], refs[-1]
        for cp in _split_copies(ins, lands, send_sems, recv_sems, gather):
            cp.start()
        token[...] = jnp.zeros_like(token)

    hbm = lambda a: pltpu.with_memory_space_constraint(a, pltpu.HBM)
    outs = pl.pallas_call(
        body, name=name,
        out_shape=(pltpu.SemaphoreType.DMA((7 * na,)), pltpu.SemaphoreType.DMA((7 * na,)),
                   *[pltpu.HBM(s.shape, s.dtype) for s in srcs], *land_types, jax.ShapeDtypeStruct((8, LANES), F32)),
        in_specs=[_HBM_SPEC] * (2 * na),
        out_specs=(_SEM_SPEC, _SEM_SPEC, *[_HBM_SPEC] * (2 * na), pl.BlockSpec(memory_space=pltpu.VMEM)),
        input_output_aliases={i: 2 + i for i in range(2 * na)}, compiler_params=_DATAFLOW,
    )(*[hbm(s) for s in srcs], *[hbm(lax.empty(t.shape, t.dtype)) for t in land_types])
    return dict(send=outs[0], recv=outs[1], srcs=list(outs[2:2 + na]), lands=list(outs[2 + na:2 + 2 * na]), token=outs[-1],
                gather=gather)


def _send_wait(handle, after, name):
    na = len(handle["srcs"])
    gather = handle["gather"]
    after = list(after)

    def body(*refs):
        ins, lands = refs[:na], refs[na:2 * na]
        send_sems, recv_sems = refs[2 * na], refs[2 * na + 1]
        for cp in _split_copies(ins, lands, send_sems, recv_sems, gather):
            cp.wait_send()
            cp.wait_recv()

    both = handle["srcs"] + handle["lands"]
    outs = pl.pallas_call(
        body, name=name, out_shape=[pltpu.HBM(a.shape, a.dtype) for a in both],
        in_specs=[_HBM_SPEC] * (2 * na) + [_SEM_SPEC, _SEM_SPEC] + [_ANY_SPEC] * len(after),
        out_specs=[_HBM_SPEC] * (2 * na), input_output_aliases={i: i for i in range(2 * na)}, compiler_params=_DATAFLOW,
    )(*both, handle["send"], handle["recv"], *after)
    return list(outs[:na]), list(outs[na:])


def _sum_slots(p_ref):
    g = p_ref[0].astype(F32)
    for s in range(1, p_ref.shape[0]):
        g = g + p_ref[s].astype(F32)
    return g


def _adamw_refs(g, w_ref, m_ref, v_ref, g_out, d_out, m_out, v_out):
    c1 = 1.0 - ADAM_B1 ** ADAM_STEP
    c2 = 1.0 - ADAM_B2 ** ADAM_STEP
    m_new = ADAM_B1 * m_ref[...] + (1.0 - ADAM_B1) * g
    v_new = ADAM_B2 * v_ref[...] + (1.0 - ADAM_B2) * (g * g)
    g_out[...] = g
    m_out[...] = m_new
    v_out[...] = v_new
    d_out[...] = -ADAM_LR * ((m_new / c1) / (jnp.sqrt(v_new / c2) + ADAM_EPS) + ADAM_WD * w_ref[...])


def _adamw(parts, w, m, v, name):
    npart, r, n = parts.shape
    tr = r if r <= 256 else max(t for t in range(16, 257, 16) if r % t == 0)

    def body(p_ref, w_ref, m_ref, v_ref, g_out, d_out, m_out, v_out):
        _adamw_refs(_sum_slots(p_ref), w_ref, m_ref, v_ref, g_out, d_out, m_out, v_out)

    blk = pl.BlockSpec((tr, n), lambda i: (i, 0))
    shp = jax.ShapeDtypeStruct((r, n), F32)
    return pl.pallas_call(
        body, name=name, grid=(r // tr,), in_specs=[pl.BlockSpec((npart, tr, n), lambda i: (0, i, 0)), blk, blk, blk],
        out_specs=[blk] * 4, out_shape=[shp] * 4, compiler_params=_params("parallel"),
    )(parts, w, m, v)


def _adamw_small(parts, ws, ms, vs, loss_parts, name):
    n = len(parts)

    def body(*refs):
        ins, outs = refs[:4 * n + 1], refs[4 * n + 1:]
        for i in range(n):
            _adamw_refs(_sum_slots(ins[i]), ins[n + i], ins[2 * n + i], ins[3 * n + i], *outs[4 * i:4 * i + 4])
        outs[4 * n][...] = _sum_slots(ins[4 * n])

    out_shape = [jax.ShapeDtypeStruct(w.shape, F32) for w in ws for _ in range(4)]
    res = pl.pallas_call(body, name=name, out_shape=out_shape + [jax.ShapeDtypeStruct((1, LANES), F32)],
                         compiler_params=_params())(*parts, *ws, *ms, *vs, loss_parts)
    return [res[4 * i:4 * i + 4] for i in range(n)], res[4 * n]


REPLICATED = ("g_cq", "g_ckv", "w_uk", "w_uv", "ln1_g", "ln1_b", "conv_b", "ln2_g", "ln2_b")
ALL_WEIGHTS = ("w_in", "g_cq", "g_ckv", "w_uq", "w_uk", "w_uv", "w_o", "ln1_g", "ln1_b", "w_up", "conv_w", "conv_b",
               "w_down", "ln2_g", "ln2_b")


def kernel(x, w_in, g_cq, g_ckv, w_uq, w_uk, w_uv, w_o, ln1_g, ln1_b, w_up, conv_w, conv_b, w_down, ln2_g, ln2_b, loss_target, m_w_in, m_g_cq, m_g_ckv, m_w_uq, m_w_uk, m_w_uv, m_w_o, m_ln1_g, m_ln1_b, m_w_up, m_conv_w, m_conv_b, m_w_down, m_ln2_g, m_ln2_b, v_w_in, v_g_cq, v_g_ckv, v_w_uq, v_w_uk, v_w_uv, v_w_o, v_ln1_g, v_ln1_b, v_w_up, v_conv_w, v_conv_b, v_w_down, v_ln2_g, v_ln2_b):
    w = dict(w_in=w_in, g_cq=g_cq, g_ckv=g_ckv, w_uq=w_uq, w_uk=w_uk, w_uv=w_uv, w_o=w_o, ln1_g=ln1_g, ln1_b=ln1_b,
             w_up=w_up, conv_w=conv_w, conv_b=conv_b, w_down=w_down, ln2_g=ln2_g, ln2_b=ln2_b)
    m = dict(w_in=m_w_in, g_cq=m_g_cq, g_ckv=m_g_ckv, w_uq=m_w_uq, w_uk=m_w_uk, w_uv=m_w_uv, w_o=m_w_o, ln1_g=m_ln1_g,
             ln1_b=m_ln1_b, w_up=m_w_up, conv_w=m_conv_w, conv_b=m_conv_b, w_down=m_w_down, ln2_g=m_ln2_g, ln2_b=m_ln2_b)
    v = dict(w_in=v_w_in, g_cq=v_g_cq, g_ckv=v_g_ckv, w_uq=v_w_uq, w_uk=v_w_uk, w_uv=v_w_uv, w_o=v_w_o, ln1_g=v_ln1_g,
             ln1_b=v_ln1_b, w_up=v_w_up, conv_w=v_conv_w, conv_b=v_conv_b, w_down=v_w_down, ln2_g=v_ln2_g, ln2_b=v_ln2_b)
    me = 4 * lax.axis_index("x") + 2 * lax.axis_index("y") + lax.axis_index("c")
    wire = lambda a: a.astype(WIRE_DTYPE)
    pad_taps = lambda a: jnp.pad(a, ((0, 8 - a.shape[0]), (0, 0)))

    own_slot = lambda buf, block: lax.dynamic_update_index_in_dim(buf, block, me, 0)
    blocks = lambda a: wire(a).reshape((N_DEV, a.shape[0] // N_DEV) + a.shape[1:])

    g_in, g_uq, g_conv = _all_gather(
        [wire(w_in).T, _heads_major(wire(w_uq)), pad_taps(conv_w)],
        "gather_weights")
    late = _send_start([wire(w_o), wire(w_up).T, wire(w_down)], True, "gather_late_start")
    r_uq_dev, e_uq = w_uq.shape[0], w_uq.shape[2]
    wq = jnp.transpose(g_uq.reshape(N_DEV, HEADS, r_uq_dev, e_uq), (1, 0, 2, 3)).reshape(HEADS, Q_RANK, e_uq)
    cw = dict(
        w_in_t=_split_pad_rows(g_in.reshape(-1, D_MODEL)).astype(MXU_DTYPE),
        wq=jnp.pad(wq, ((0, 0), (0, 0), (0, LANES - e_uq))).astype(MXU_DTYPE),
        wk=_pad_heads(w_uk, NOPE_DIM), wv=_pad_heads(w_uv, HEAD_DIM),
        conv_w=_ffn_interleave(jnp.transpose(g_conv[:, :conv_w.shape[0]], (1, 0, 2)).reshape(conv_w.shape[0], -1), 1),
        g_cq=_row(g_cq), g_ckv=_row(g_ckv), ln1_g=_row(ln1_g), ln1_b=_row(ln1_b), conv_b=_ffn_interleave(_row(conv_b), 1),
        ln2_g=_row(ln2_g), ln2_b=_row(ln2_b))

    def late_weights(after):
        own, landed = _send_wait(late, [after], "gather_late_wait")
        g_o, g_up, g_down = [own_slot(buf, blk) for buf, blk in zip(landed, own)]
        w_o_mla, w_o_dil = _pad_w_o(g_o.reshape(-1, D_MODEL))
        return dict(w_o_mla=w_o_mla, w_o_dil=w_o_dil, w_up_t=_ffn_interleave(g_up.reshape(-1, D_MODEL), 0).astype(MXU_DTYPE),
                    w_down=g_down.reshape(-1, D_MODEL).astype(MXU_DTYPE))

    sent = {}

    def on_grads(stage, g):
        if stage == "ffn":
            sent[stage] = [_send_start([blocks(_ffn_deinterleave(g["w_up_t"], 0)), blocks(g["w_down"])], False, "exchange_ffn_start")]
        elif stage == "w_o":
            sent[stage] = [_send_start([blocks(_unpad_w_o(g["w_o_mla"], g["w_o_dil"]))], False, "exchange_w_o_start")]
        else:
            d_in = blocks(_split_unpad_rows(g["w_in_t"]))
            d_uq = wire(jnp.transpose(g["wq"][:, :, :e_uq].reshape(HEADS, N_DEV, r_uq_dev, e_uq), (1, 0, 2, 3))
                        ).reshape(N_DEV, HEADS * r_uq_dev, e_uq)
            dense = lambda a, width: wire(a[:, :, :width]).reshape(-1, LANES)
            small = dict(g_cq=g["g_cq"], g_ckv=g["g_ckv"], w_uk=dense(g["wk"], NOPE_DIM), w_uv=dense(g["wv"], HEAD_DIM),
                         ln1_g=g["ln1_g"], ln1_b=g["ln1_b"], conv_b=_ffn_deinterleave(g["conv_b"], 1), ln2_g=g["ln2_g"],
                         ln2_b=g["ln2_b"])
            sent[stage] = [_send_start([d_in, d_uq], False, "exchange_rest_start"),
                           _send_start([small[n] for n in REPLICATED] + [_ffn_deinterleave(g["conv_w"], 1), g["loss"]],
                                       True, "gather_small_start")]
        return [h["token"] for h in sent[stage]]

    _, grad_x, _ = _layer_grads(x[0], loss_target[0], cw, [late["token"]], late_weights, on_grads)

    def landed(handle, after, name):
        own, got = _send_wait(handle, after, name)
        pick = (lambda a: a) if handle["gather"] else (lambda a: lax.dynamic_index_in_dim(a, me, 0, keepdims=False))
        return [own_slot(buf, pick(src)) for buf, src in zip(got, own)]

    out = {}

    def update(name, parts, view=None):
        to2d = {None: lambda a: a, "t": lambda a: a.T, "heads": _heads_major}[view]
        back = {None: lambda a: a, "t": lambda a: a.T, "heads": lambda a: _heads_minor(a, HEADS)}[view]
        res = _adamw(parts, to2d(w[name]), to2d(m[name]), to2d(v[name]), "adamw_" + name)
        for kind, a in zip(("grad", "delta", "new_m", "new_v"), res):
            out[kind, name] = back(a)
        return res[0]

    r_up, r_down = landed(sent["ffn"][0], [grad_x], "exchange_ffn_wait")
    (r_o,) = landed(sent["w_o"][0], [grad_x], "exchange_w_o_wait")
    done = [update("w_up", r_up, "t"), update("w_down", r_down), update("w_o", r_o)]
    r_in, r_uq = landed(sent["rest"][0], done, "exchange_rest_wait")
    *rep_all, cw_all, loss_all = landed(sent["rest"][1], done, "gather_small_wait")
    update("w_in", r_in, "t")
    update("w_uq", r_uq, "heads")
    heads_major = ("w_uk", "w_uv")
    two_d = lambda n, a: _heads_major(a) if n in heads_major else a.reshape(1, -1)
    rep_all = [p.reshape(N_DEV, -1, w[n].shape[2]) if n in heads_major else p for n, p in zip(REPLICATED, rep_all)]
    res, loss_sum = _adamw_small(rep_all, *[[two_d(n, d[n]) for n in REPLICATED] for d in (w, m, v)], loss_all, "adamw_replicated")
    for n, quad in zip(REPLICATED, res):
        for kind, a in zip(("grad", "delta", "new_m", "new_v"), quad):
            out[kind, n] = _heads_minor(a, HEADS) if n in heads_major else a.reshape(w[n].shape)
    loss = loss_sum[0, 0]
    ncw = conv_w.shape[1]
    update("conv_w", lax.dynamic_slice_in_dim(cw_all[:, :conv_w.shape[0]], me * ncw, ncw, axis=2))

    return (loss, grad_x[None], *[out[kind, n] for kind in ("grad", "delta", "new_m", "new_v") for n in ALL_WEIGHTS])
```

```python
import math

import jax
import jax.numpy as jnp
import numpy as np
from jax import lax
from jax.experimental import pallas as pl
from jax.experimental.pallas import tpu as pltpu

F32 = jnp.float32
MXU_DTYPE = jnp.bfloat16
WIRE_DTYPE = jnp.bfloat16

N_DEV = 8
D_MODEL = 1024
HEADS = 8
HEAD_DIM = 64
LANES = 128
Q_RANK, KV_RANK, ROPE_DIM, NOPE_DIM = 256, 128, 32, 64
DIL_WIDTH = HEADS * HEAD_DIM
MLA_WIDTH = HEADS * HEAD_DIM
IN_PAD = 2048
DH_PART = IN_PAD // 4
D_FF = 2816
ROPE_THETA = 10000.0
DIL_PAIRS = ((128, 1), (512, 4), (2048, 16))
DIL_BLOCK = 128
DN_ALPHA = 2.0 ** 0.25
LN_EPS = 1e-5
RMS_EPS = 1e-6
ONES_LANE = HEAD_DIM
MLA_SCALE = 1.0 / math.sqrt(NOPE_DIM + ROPE_DIM)
MLA_SCALE_LOG2 = MLA_SCALE * math.log2(math.e)
DIL_SCALE = 1.0 / math.sqrt(HEAD_DIM)
ALIBI_SLOPES = tuple(2.0 ** (-8.0 * (h + 1) / HEADS) for h in range(HEADS))
NEG_BIG = -1e30
ADAM_LR, ADAM_B1, ADAM_B2, ADAM_EPS, ADAM_WD, ADAM_STEP = 0.001, 0.9, 0.999, 1e-08, 0.01, 10
VMEM_LIMIT = 48 * 1024 * 1024


def _params(*sem):
    return pltpu.CompilerParams(dimension_semantics=sem or None, vmem_limit_bytes=VMEM_LIMIT)


def _dot(a, b, ca, cb):
    return lax.dot_general(a, b, (((ca,), (cb,)), ((), ())), preferred_element_type=F32)


_ANY_SPEC = pl.BlockSpec(memory_space=pl.ANY)


def _mm(a, b, *, name, tm, tn, tk, ta=False, tb=False, out_dtype=F32, after=()):
    m, k = (a.shape[1], a.shape[0]) if ta else a.shape
    n = b.shape[0] if tb else b.shape[1]
    assert (b.shape[1] if tb else b.shape[0]) == k
    tm, tn, tk = min(tm, m), min(tn, n), min(tk, k)
    assert m % tm == 0 and n % tn == 0 and k % tk == 0, (name, m, n, k, tm, tn, tk)
    nk = k // tk
    a_spec = (pl.BlockSpec((tk, tm), lambda i, j, kk: (kk, i)) if ta
              else pl.BlockSpec((tm, tk), lambda i, j, kk: (i, kk)))
    b_mode = dict(pipeline_mode=pl.Buffered(1)) if (tn == n and tk == k) else {}
    b_spec = (pl.BlockSpec((tn, tk), lambda i, j, kk: (j, kk), **b_mode) if tb
              else pl.BlockSpec((tk, tn), lambda i, j, kk: (kk, j), **b_mode))
    o_spec = pl.BlockSpec((tm, tn), lambda i, j, kk: (i, j))
    n_in = 2 + len(after)
    ca, cb = (0 if ta else 1), (1 if tb else 0)

    def body(*refs):
        a_ref, b_ref, o_ref = refs[0], refs[1], refs[n_in]
        part = _dot(a_ref[...].astype(MXU_DTYPE), b_ref[...].astype(MXU_DTYPE), ca, cb)
        if nk == 1:
            o_ref[...] = part.astype(o_ref.dtype)
            return
        acc_ref = refs[-1]
        kk = pl.program_id(2)

        @pl.when(kk == 0)
        def _():
            acc_ref[...] = part

        @pl.when(kk > 0)
        def _():
            acc_ref[...] += part

        @pl.when(kk == nk - 1)
        def _():
            o_ref[...] = acc_ref[...].astype(o_ref.dtype)

    return pl.pallas_call(
        body, name=name, grid=(m // tm, n // tn, nk), in_specs=[a_spec, b_spec] + [_ANY_SPEC] * len(after), out_specs=o_spec,
        out_shape=jax.ShapeDtypeStruct((m, n), out_dtype),
        scratch_shapes=[pltpu.VMEM((tm, tn), F32)] if nk > 1 else [],
        compiler_params=_params("parallel", "parallel", "arbitrary"),
    )(a, b, *after)


def _mm_do_mla(dz, w_o_mla, tm=1024):
    seq, d = dz.shape
    tm = min(tm, seq)

    def body(a_ref, b_ref, o_ref):
        a = a_ref[...].astype(MXU_DTYPE)
        for hd in range(HEADS):
            o_ref[hd] = _dot(a, b_ref[LANES * hd:LANES * (hd + 1), :], 1, 1)

    return pl.pallas_call(
        body, name="mm_do_mla", grid=(seq // tm,),
        in_specs=[pl.BlockSpec((tm, d), lambda i: (i, 0)), pl.BlockSpec((HEADS * LANES, d), lambda i: (0, 0))],
        out_specs=pl.BlockSpec((HEADS, tm, LANES), lambda i: (0, i, 0)),
        out_shape=jax.ShapeDtypeStruct((HEADS, seq, LANES), F32), compiler_params=_params("parallel"),
    )(dz, w_o_mla)


def _mm_dw_o_mla(o_mla, dz, tk=1024):
    seq, d = dz.shape
    tk = min(tk, seq)
    nk = seq // tk

    def body(a_ref, b_ref, o_ref, acc_ref):
        kk = pl.program_id(0)

        @pl.when(kk == 0)
        def _():
            acc_ref[...] = jnp.zeros_like(acc_ref)

        b = b_ref[...].astype(MXU_DTYPE)
        for hd in range(HEADS):
            acc_ref[LANES * hd:LANES * (hd + 1), :] += _dot(a_ref[hd].astype(MXU_DTYPE), b, 0, 0)

        @pl.when(kk == nk - 1)
        def _():
            o_ref[...] = acc_ref[...].astype(o_ref.dtype)

    return pl.pallas_call(
        body, name="mm_dw_o_mla", grid=(nk,),
        in_specs=[pl.BlockSpec((HEADS, tk, LANES), lambda kk: (0, kk, 0)), pl.BlockSpec((tk, d), lambda kk: (kk, 0))],
        out_specs=pl.BlockSpec((HEADS * LANES, d), lambda kk: (0, 0)),
        out_shape=jax.ShapeDtypeStruct((HEADS * LANES, d), MXU_DTYPE),
        scratch_shapes=[pltpu.VMEM((HEADS * LANES, d), F32)], compiler_params=_params("arbitrary"),
    )(o_mla, dz)


def _rope_tables(seq):
    half = ROPE_DIM // 2
    f32 = np.float32
    freqs = np.power(f32(ROPE_THETA), -np.arange(half, dtype=f32) / f32(half))
    ang = np.arange(seq, dtype=f32)[:, None] * freqs[None, :]
    cos, sin = np.cos(ang, dtype=f32), np.sin(ang, dtype=f32)
    one = np.ones((seq, NOPE_DIM), f32)
    tail = np.ones((seq, LANES - NOPE_DIM - ROPE_DIM), f32)
    ctab = np.concatenate([one, cos, cos, tail], axis=1)
    stab = np.concatenate([0 * one, -sin, sin, 0 * tail], axis=1)
    return jnp.asarray(ctab), jnp.asarray(stab)


def _rope_swap(t):
    lane = lax.broadcasted_iota(jnp.int32, t.shape, 1)
    half = ROPE_DIM // 2
    return jnp.where(lane < NOPE_DIM + half, pltpu.roll(t, LANES - half, 1), pltpu.roll(t, half, 1))


def _rope(t, ctab, stab):
    return t * ctab + _rope_swap(t) * stab


def _rope_inv(t, ctab, stab):
    return t * ctab - _rope_swap(t) * stab


def _rms(x, g):
    r = lax.rsqrt(jnp.mean(x * x, axis=-1, keepdims=True) + RMS_EPS)
    xh = x * r
    return xh, r, xh * g


def _mla_prep(h, g_cq, g_ckv, wq, wk, wv, ctab, stab, tm=512):
    seq = h.shape[0]
    tm = min(tm, seq)

    def body(h_ref, gq_ref, gk_ref, wq_ref, wk_ref, wv_ref, c_ref, s_ref, q_out, k_out, v_out):
        hb = h_ref[...]
        ctab_, stab_ = c_ref[...], s_ref[...]
        _, _, cqn = _rms(hb[:, :Q_RANK], gq_ref[...])
        _, _, ckn = _rms(hb[:, Q_RANK:Q_RANK + KV_RANK], gk_ref[...])
        cqn = cqn.astype(MXU_DTYPE)
        ckn = ckn.astype(MXU_DTYPE)
        krr = _rope(hb[:, Q_RANK + KV_RANK:], ctab_, stab_)
        ones_lane = (lax.broadcasted_iota(jnp.int32, (1, LANES), 1) == ONES_LANE).astype(F32)
        for hd in range(HEADS):
            q = _dot(cqn, wq_ref[hd], 1, 0)
            q_out[hd] = _rope(q, ctab_, stab_).astype(q_out.dtype)
            k_out[hd] = (_dot(ckn, wk_ref[hd], 1, 0) + krr).astype(k_out.dtype)
            v_out[hd] = (_dot(ckn, wv_ref[hd], 1, 0) + ones_lane).astype(v_out.dtype)

    full = lambda *shape: pl.BlockSpec(shape, lambda i: (0,) * len(shape))
    slab = pl.BlockSpec((HEADS, tm, LANES), lambda i: (0, i, 0))
    shp = jax.ShapeDtypeStruct((HEADS, seq, LANES), MXU_DTYPE)
    return pl.pallas_call(
        body, name="mla_prep", grid=(seq // tm,),
        in_specs=[pl.BlockSpec((tm, DH_PART), lambda i: (i, 0)), full(1, Q_RANK), full(1, KV_RANK),
                  full(HEADS, Q_RANK, LANES), full(HEADS, KV_RANK, LANES), full(HEADS, KV_RANK, LANES),
                  pl.BlockSpec((tm, LANES), lambda i: (i, 0)), pl.BlockSpec((tm, LANES), lambda i: (i, 0))],
        out_specs=[slab, slab, slab], out_shape=[shp, shp, shp],
        compiler_params=_params("parallel"),
    )(h, g_cq, g_ckv, wq, wk, wv, ctab, stab)


def _mla_prep_bwd(h, g_cq, g_ckv, wq, wk, wv, ctab, stab, dq, dk, dv, tm=512, after=()):
    seq = h.shape[0]
    tm = min(tm, seq)
    n_after = len(after)

    def body(h_ref, gq_ref, gk_ref, wq_ref, wk_ref, wv_ref, c_ref, s_ref, dq_ref, dk_ref, dv_ref, *rest):
        dh_ref, dwq_ref, dwk_ref, dwv_ref, dgq_ref, dgk_ref = rest[n_after:]

        @pl.when(pl.program_id(0) == 0)
        def _():
            for r in (dwq_ref, dwk_ref, dwv_ref, dgq_ref, dgk_ref):
                r[...] = jnp.zeros_like(r)

        hb = h_ref[...]
        ctab_, stab_ = c_ref[...], s_ref[...]
        gq, gk = gq_ref[...], gk_ref[...]
        xq, rq, cqn = _rms(hb[:, :Q_RANK], gq)
        xk, rk, ckn = _rms(hb[:, Q_RANK:Q_RANK + KV_RANK], gk)
        cqn = cqn.astype(MXU_DTYPE)
        ckn = ckn.astype(MXU_DTYPE)
        d_cqn = jnp.zeros((tm, Q_RANK), F32)
        d_ckn = jnp.zeros((tm, KV_RANK), F32)
        d_krr = jnp.zeros((tm, LANES), F32)
        for hd in range(HEADS):
            dqh = _rope_inv(dq_ref[hd], ctab_, stab_).astype(MXU_DTYPE)
            d_cqn += _dot(dqh, wq_ref[hd], 1, 1)
            dwq_ref[hd] += _dot(cqn, dqh, 0, 0)
            dkh = dk_ref[hd]
            d_krr += dkh
            dkh = dkh.astype(MXU_DTYPE)
            d_ckn += _dot(dkh, wk_ref[hd], 1, 1)
            dwk_ref[hd] += _dot(ckn, dkh, 0, 0)
            dvh = dv_ref[hd].astype(MXU_DTYPE)
            d_ckn += _dot(dvh, wv_ref[hd], 1, 1)
            dwv_ref[hd] += _dot(ckn, dvh, 0, 0)
        lane = lax.broadcasted_iota(jnp.int32, (tm, LANES), 1)
        rot = (lane >= NOPE_DIM) & (lane < NOPE_DIM + ROPE_DIM)
        d_kr = jnp.where(rot, _rope_inv(jnp.where(rot, d_krr, 0.0), ctab_, stab_), 0.0)

        def rms_bwd(dy, xh, r, g, dg_ref):
            dg_ref[...] += jnp.sum(dy * xh, axis=0, keepdims=True)
            dxh = dy * g
            return r * (dxh - xh * jnp.mean(dxh * xh, axis=-1, keepdims=True))

        d_cq = rms_bwd(d_cqn, xq, rq, gq, dgq_ref)
        d_ck = rms_bwd(d_ckn, xk, rk, gk, dgk_ref)
        dh_ref[...] = jnp.concatenate([d_cq, d_ck, d_kr], axis=1).astype(dh_ref.dtype)

    full = lambda *shape: pl.BlockSpec(shape, lambda i: (0,) * len(shape))
    slab = pl.BlockSpec((HEADS, tm, LANES), lambda i: (0, i, 0))
    return pl.pallas_call(
        body, name="mla_prep_bwd", grid=(seq // tm,),
        in_specs=[pl.BlockSpec((tm, DH_PART), lambda i: (i, 0)), full(1, Q_RANK), full(1, KV_RANK),
                  full(HEADS, Q_RANK, LANES), full(HEADS, KV_RANK, LANES), full(HEADS, KV_RANK, LANES),
                  pl.BlockSpec((tm, LANES), lambda i: (i, 0)), pl.BlockSpec((tm, LANES), lambda i: (i, 0)),
                  slab, slab, slab] + [_ANY_SPEC] * n_after,
        out_specs=[pl.BlockSpec((tm, DH_PART), lambda i: (i, 0)), full(HEADS, Q_RANK, LANES), full(HEADS, KV_RANK, LANES),
                   full(HEADS, KV_RANK, LANES), full(1, Q_RANK), full(1, KV_RANK)],
        out_shape=[jax.ShapeDtypeStruct((seq, DH_PART), MXU_DTYPE), jax.ShapeDtypeStruct((HEADS, Q_RANK, LANES), F32),
                   jax.ShapeDtypeStruct((HEADS, KV_RANK, LANES), F32), jax.ShapeDtypeStruct((HEADS, KV_RANK, LANES), F32),
                   jax.ShapeDtypeStruct((1, Q_RANK), F32), jax.ShapeDtypeStruct((1, KV_RANK), F32)],
        compiler_params=_params("arbitrary"),
    )(h, g_cq, g_ckv, wq, wk, wv, ctab, stab, dq, dk, dv, *after)


def _causal_mask(t):
    row = lax.broadcasted_iota(jnp.int32, (t, t), 0)
    col = lax.broadcasted_iota(jnp.int32, (t, t), 1)
    return row >= col


def _mla_attn_fwd(q, k, v, t=512):
    _, seq, _ = q.shape
    t = min(t, seq)

    def body(q_ref, k_ref, v_ref, o_ref, ob_ref, lse_ref, m_ref, acc_ref, s_ref):
        i = pl.program_id(1)
        qb = q_ref[...]
        m_ref[...] = jnp.full_like(m_ref, NEG_BIG)
        acc_ref[...] = jnp.zeros_like(acc_ref)

        def scores(j):
            return _dot(qb, k_ref[pl.ds(pl.multiple_of(j * t, t), t), :], 1, 1) * MLA_SCALE_LOG2

        def softmax_pv(j, s, masked):
            vb = v_ref[pl.ds(pl.multiple_of(j * t, t), t), :]
            if masked:
                s = jnp.where(_causal_mask(t), s, NEG_BIG)
            m_old = m_ref[...]
            m_new = jnp.maximum(m_old, jnp.max(s, axis=1, keepdims=True))
            p = jnp.exp2(s - m_new)
            a = jnp.exp2(m_old - m_new)
            acc_ref[...] = a * acc_ref[...] + _dot(p.astype(MXU_DTYPE), vb, 1, 0)
            m_ref[...] = m_new

        s_ref[...] = scores(0)

        def loop_body(j, c):
            s_next = scores(j + 1)
            softmax_pv(j, s_ref[...], False)
            s_ref[...] = s_next
            return c

        lax.fori_loop(0, i, loop_body, 0)
        softmax_pv(i, s_ref[...], True)
        acc = acc_ref[...]
        l = acc[:, ONES_LANE:ONES_LANE + 1]
        o = jnp.where(lax.broadcasted_iota(jnp.int32, acc.shape, 1) < HEAD_DIM, acc * (1.0 / l), 0.0)
        o_ref[...] = o
        ob_ref[...] = o.astype(ob_ref.dtype)
        lse_ref[...] = jnp.broadcast_to(m_ref[...] + jnp.log2(l), lse_ref.shape)

    blk = pl.BlockSpec((None, t, LANES), lambda h, i: (h, i, 0))
    whole = pl.BlockSpec((None, seq, LANES), lambda h, i: (h, 0, 0))
    shp = jax.ShapeDtypeStruct((HEADS, seq, LANES), F32)
    return pl.pallas_call(
        body, name="mla_attn_fwd", grid=(HEADS, seq // t),
        in_specs=[blk, whole, whole], out_specs=[blk, blk, blk],
        out_shape=[shp, jax.ShapeDtypeStruct((HEADS, seq, LANES), MXU_DTYPE), shp],
        scratch_shapes=[pltpu.VMEM((t, 1), F32), pltpu.VMEM((t, LANES), F32), pltpu.VMEM((t, t), F32)],
        compiler_params=_params("parallel", "arbitrary"),
    )(q, k, v)


def _mla_attn_bwd(q, k, v, o, lse, do, t=512):
    _, seq, _ = q.shape
    t = min(t, seq)
    nb = seq // t

    def body(q_ref, k_ref, v_ref, o_ref, lse_ref, do_ref, dq_ref, dk_ref, dv_ref, dl_ref, dka_ref, dva_ref):
        dq_ref[...] = jnp.zeros_like(dq_ref)

        def delta_body(i, c):
            rows = pl.ds(pl.multiple_of(i * t, t), t)
            dl_ref[rows, :] = jnp.sum(do_ref[rows, :] * o_ref[rows, :], axis=1, keepdims=True)
            return c

        lax.fori_loop(0, nb, delta_body, 0)

        def kblock(j, c):
            krows = pl.ds(pl.multiple_of(j * t, t), t)
            kb = k_ref[krows, :]
            vb = v_ref[krows, :]
            dka_ref[...] = jnp.zeros_like(dka_ref)
            dva_ref[...] = jnp.zeros_like(dva_ref)

            def qstep(i, masked):
                th = t // 2
                rows = [pl.ds(pl.multiple_of(i * t + hf * th, th), th) for hf in range(2)]
                qs = [q_ref[r, :] for r in rows]
                dos = [do_ref[r, :].astype(MXU_DTYPE) for r in rows]
                nkeys = [th if masked else t, t]
                ss = [_dot(qs[hf], kb[:nkeys[hf]], 1, 1) * MLA_SCALE_LOG2 for hf in range(2)]
                dps = [_dot(dos[hf], vb[:nkeys[hf]], 1, 1) for hf in range(2)]
                for hf in range(2):
                    s, nk = ss[hf], nkeys[hf]
                    if masked:
                        row = lax.broadcasted_iota(jnp.int32, (th, nk), 0) + hf * th
                        s = jnp.where(row >= lax.broadcasted_iota(jnp.int32, (th, nk), 1), s, NEG_BIG)
                    p = jnp.exp2(s - lse_ref[rows[hf], 0:1])
                    dva_ref[0:nk, :] += _dot(p.astype(MXU_DTYPE), dos[hf], 0, 0)
                    ds = (p * (dps[hf] - dl_ref[rows[hf], :]) * MLA_SCALE).astype(MXU_DTYPE)
                    dka_ref[0:nk, :] += _dot(ds, qs[hf], 0, 0)
                    dq_ref[rows[hf], :] += _dot(ds, kb[:nk], 1, 0)

            qstep(j, True)

            def qloop(i, c2):
                qstep(i, False)
                return c2

            lax.fori_loop(j + 1, nb, qloop, 0)
            dk_ref[krows, :] = dka_ref[...]
            dv_ref[krows, :] = dva_ref[...]
            return c

        lax.fori_loop(0, nb, kblock, 0)

    whole = pl.BlockSpec((None, seq, LANES), lambda h: (h, 0, 0))
    shp = jax.ShapeDtypeStruct((HEADS, seq, LANES), F32)
    return pl.pallas_call(
        body, name="mla_attn_bwd", grid=(HEADS,),
        in_specs=[whole] * 6, out_specs=[whole] * 3, out_shape=[shp] * 3,
        scratch_shapes=[pltpu.VMEM((seq, 1), F32), pltpu.VMEM((t, LANES), F32), pltpu.VMEM((t, LANES), F32)],
        compiler_params=_params("parallel"),
    )(q, k, v, o, lse, do)


DIL_CHUNK = DIL_BLOCK * max(d for _, d in DIL_PAIRS)
DIL_PAIR_LANES = 2 * HEAD_DIM
assert DIL_PAIR_LANES == LANES
DIL_UNROLL_FWD = 16
DIL_UNROLL_BWD = 8


def _dil_bias_tables(hp, dil):
    b = DIL_BLOCK
    iq = lax.broadcasted_iota(jnp.int32, (b, 2 * b), 0)
    ik = lax.broadcasted_iota(jnp.int32, (b, 2 * b), 1)
    off = iq + b - ik
    band = (off >= 0) & (off <= b)
    dist = (off * dil).astype(F32)
    every, first = [], []
    for hh in range(2):
        slope = jnp.where(hp == 0, ALIBI_SLOPES[hh], jnp.where(hp == 1, ALIBI_SLOPES[2 + hh],
                          jnp.where(hp == 2, ALIBI_SLOPES[4 + hh], ALIBI_SLOPES[6 + hh]))).astype(F32)
        bias = -slope * dist
        every.append(jnp.where(band, bias, NEG_BIG))
        first.append(jnp.where(band & (ik >= b), bias, NEG_BIG))
    return jnp.concatenate(every, axis=0), jnp.concatenate(first, axis=0)


def _dil_rows(start, dil):
    return pl.ds(start, DIL_BLOCK) if dil == 1 else pl.ds(start, DIL_BLOCK, stride=dil)


def _dil_block_pos(blk, c, dil):
    sc, r = blk // dil, blk % dil
    q0 = sc * (DIL_BLOCK * dil) + r
    kcur0 = c * DIL_CHUNK + q0
    first = kcur0 < DIL_BLOCK * dil
    kprev0 = jnp.where(first, kcur0, kcur0 - DIL_BLOCK * dil)
    return q0, kcur0, kprev0, first


def _pair_cols(hh):
    return slice(HEAD_DIM * hh, HEAD_DIM * (hh + 1))


def _first_head_lanes(shape):
    return lax.broadcasted_iota(jnp.int32, shape, 1) < HEAD_DIM


def _stack_pair(t):
    first = _first_head_lanes(t.shape)
    return jnp.concatenate([jnp.where(first, t, 0.0), jnp.where(first, 0.0, t)], axis=0).astype(MXU_DTYPE)


def _unstack_pair(t):
    rows = t.shape[0] // 2
    return jnp.where(_first_head_lanes((rows, t.shape[1])), t[:rows], t[rows:])


def _pair_column(t):
    return jnp.concatenate([t[:, 0:1], t[:, HEAD_DIM:HEAD_DIM + 1]], axis=0)


def _dil_fwd(h):
    seq = h.shape[0]
    assert seq % DIL_CHUNK == 0
    nblk = DIL_CHUNK // DIL_BLOCK
    rc = 256

    def body(q_ref, k_ref, v_ref, o_ref, ob_ref, lse_ref, *scr):
        o_scr, l_scr = scr[:3], scr[3:]
        hp, c = pl.program_id(0), pl.program_id(1)
        for bi, (_, dil) in enumerate(DIL_PAIRS):
            tables = _dil_bias_tables(hp, dil)

            def block(blk, carry, bi=bi, dil=dil, tables=tables):
                q0, kcur0, kprev0, first = _dil_block_pos(blk, c, dil)
                q2 = _stack_pair(q_ref[_dil_rows(q0, dil), :] * DIL_SCALE)
                kcat = jnp.concatenate([k_ref[_dil_rows(kprev0, dil), :], k_ref[_dil_rows(kcur0, dil), :]], axis=0).astype(MXU_DTYPE)
                vcat = jnp.concatenate([v_ref[_dil_rows(kprev0, dil), :], v_ref[_dil_rows(kcur0, dil), :]], axis=0).astype(MXU_DTYPE)
                s = _dot(q2, kcat, 1, 1) + jnp.where(first, tables[1], tables[0])
                mx = jnp.max(s, axis=1, keepdims=True)
                p = jnp.exp(s - mx)
                l = jnp.sum(p, axis=1, keepdims=True)
                o_scr[bi][_dil_rows(q0, dil), :] = _unstack_pair(_dot(p.astype(MXU_DTYPE), vcat, 1, 0) * (1.0 / l))
                l_scr[bi][_dil_rows(q0, dil), :] = _unstack_pair(jnp.broadcast_to(mx + jnp.log(l), (2 * DIL_BLOCK, LANES)))
                return carry

            lax.fori_loop(0, nblk, block, 0, unroll=DIL_UNROLL_FWD)

        def combine(i, carry):
            rows = pl.ds(pl.multiple_of(i * rc, rc), rc)
            ls = [l_scr[bi][rows, :] for bi in range(3)]
            mx = jnp.maximum(jnp.maximum(ls[0], ls[1]), ls[2])
            es = [jnp.exp(l - mx) for l in ls]
            den = es[0] + es[1] + es[2]
            o = (es[0] * o_scr[0][rows, :] + es[1] * o_scr[1][rows, :] + es[2] * o_scr[2][rows, :]) / den
            o_ref[rows, :] = o
            ob_ref[rows, :] = o.astype(ob_ref.dtype)
            lse_ref[rows, :] = mx + jnp.log(den)
            return carry

        lax.fori_loop(0, DIL_CHUNK // rc, combine, 0)

    nq = DIL_WIDTH // LANES
    chunk = lambda off: pl.BlockSpec((DIL_CHUNK, LANES), lambda hp, c: (c, off + hp))
    whole = lambda off: pl.BlockSpec((seq, LANES), lambda hp, c: (0, off + hp))
    shp = jax.ShapeDtypeStruct((seq, DIL_WIDTH), F32)
    return pl.pallas_call(
        body, name="dil_fwd", grid=(nq, seq // DIL_CHUNK),
        in_specs=[chunk(nq), whole(2 * nq), whole(3 * nq)], out_specs=[chunk(0), chunk(0), chunk(0)],
        out_shape=[shp, jax.ShapeDtypeStruct((seq, DIL_WIDTH), MXU_DTYPE), shp],
        scratch_shapes=[pltpu.VMEM((DIL_CHUNK, LANES), F32)] * 6,
        compiler_params=_params("parallel", "arbitrary"),
    )(h, h, h)


def _dil_bwd(h, o, lse, do):
    seq = h.shape[0]
    nblk = DIL_CHUNK // DIL_BLOCK
    nchunk = seq // DIL_CHUNK
    rc = 256

    def body(q_ref, k_ref, v_ref, o_ref, lse_ref, do_ref, dq_out, dk_out, dv_out, dl_scr, dq_ref, dk_ref, dv_ref):
        hp, c = pl.program_id(0), pl.program_id(1)

        @pl.when(c == 0)
        def _():
            dk_ref[...] = jnp.zeros_like(dk_ref)
            dv_ref[...] = jnp.zeros_like(dv_ref)

        def delta(i, carry):
            rows = pl.ds(pl.multiple_of(i * rc, rc), rc)
            prod = do_ref[rows, :] * o_ref[rows, :]
            dl_scr[rows, :] = jnp.concatenate(
                [jnp.broadcast_to(jnp.sum(prod[:, _pair_cols(hh)], axis=1, keepdims=True), (rc, HEAD_DIM)) for hh in range(2)], axis=1)
            return carry

        lax.fori_loop(0, DIL_CHUNK // rc, delta, 0)

        for bi, (_, dil) in enumerate(DIL_PAIRS):
            tables = _dil_bias_tables(hp, dil)

            def block(blk, carry, bi=bi, dil=dil, tables=tables):
                q0, kcur0, kprev0, first = _dil_block_pos(blk, c, dil)
                qrows = _dil_rows(q0, dil)
                q2 = _stack_pair(q_ref[qrows, :] * DIL_SCALE)
                kcat = jnp.concatenate([k_ref[_dil_rows(kprev0, dil), :], k_ref[_dil_rows(kcur0, dil), :]], axis=0).astype(MXU_DTYPE)
                vcat = jnp.concatenate([v_ref[_dil_rows(kprev0, dil), :], v_ref[_dil_rows(kcur0, dil), :]], axis=0).astype(MXU_DTYPE)
                do2 = _stack_pair(do_ref[qrows, :])
                s = _dot(q2, kcat, 1, 1) + jnp.where(first, tables[1], tables[0])
                p = jnp.exp(s - _pair_column(lse_ref[qrows, :]))
                dp = _dot(do2, vcat, 1, 1)
                ds = (p * (dp - _pair_column(dl_scr[qrows, :]))).astype(MXU_DTYPE)
                dq_b = _unstack_pair(_dot(ds, kcat, 1, 0)) * DIL_SCALE
                dk_b = _dot(ds, q2, 0, 0)
                dv_b = _dot(p.astype(MXU_DTYPE), do2, 0, 0)
                if bi == 0:
                    dq_ref[qrows, :] = dq_b
                else:
                    dq_ref[qrows, :] += dq_b
                dk_ref[_dil_rows(kprev0, dil), :] += dk_b[:DIL_BLOCK]
                dv_ref[_dil_rows(kprev0, dil), :] += dv_b[:DIL_BLOCK]
                dk_ref[_dil_rows(kcur0, dil), :] += dk_b[DIL_BLOCK:]
                dv_ref[_dil_rows(kcur0, dil), :] += dv_b[DIL_BLOCK:]
                return carry

            lax.fori_loop(0, nblk, block, 0, unroll=DIL_UNROLL_BWD)

        dq_out[...] = dq_ref[...].astype(dq_out.dtype)

        @pl.when(c == nchunk - 1)
        def _():
            dk_out[...] = dk_ref[...].astype(dk_out.dtype)
            dv_out[...] = dv_ref[...].astype(dv_out.dtype)

    nq = DIL_WIDTH // LANES
    chunk = lambda off: pl.BlockSpec((DIL_CHUNK, LANES), lambda hp, c: (c, off + hp))
    whole = lambda off: pl.BlockSpec((seq, LANES), lambda hp, c: (0, off + hp))
    shp = jax.ShapeDtypeStruct((seq, DIL_WIDTH), MXU_DTYPE)
    return pl.pallas_call(
        body, name="dil_bwd", grid=(nq, nchunk),
        in_specs=[chunk(nq), whole(2 * nq), whole(3 * nq), chunk(0), chunk(0), chunk(0)],
        out_specs=[chunk(0), whole(0), whole(0)], out_shape=[shp, shp, shp],
        scratch_shapes=[pltpu.VMEM((DIL_CHUNK, LANES), F32), pltpu.VMEM((DIL_CHUNK, LANES), F32),
                        pltpu.VMEM((seq, LANES), F32), pltpu.VMEM((seq, LANES), F32)],
        compiler_params=_params("parallel", "arbitrary"),
    )(h, h, h, o, lse, do)


def _mm_dx0(parts, w_in_t, res, tm=1024, after=()):
    seq, d = res.shape
    tm = min(tm, seq)
    n_after = len(after)

    def body(a0, a1, a2, a3, b_ref, r_ref, *rest):
        o_ref = rest[n_after]
        acc = _dot(a0[...], b_ref[0:DH_PART, :], 1, 0)
        for c, a in enumerate((a1, a2, a3), start=1):
            acc += _dot(a[...], b_ref[DH_PART * c:DH_PART * (c + 1), :], 1, 0)
        o_ref[...] = acc + DN_ALPHA * r_ref[...]

    blk = pl.BlockSpec((tm, DH_PART), lambda i: (i, 0))
    row = pl.BlockSpec((tm, d), lambda i: (i, 0))
    return pl.pallas_call(
        body, name="mm_dx0", grid=(seq // tm,),
        in_specs=[blk] * 4 + [pl.BlockSpec((IN_PAD, d), lambda i: (0, 0), pipeline_mode=pl.Buffered(1)), row] + [_ANY_SPEC] * n_after,
        out_specs=row, out_shape=jax.ShapeDtypeStruct((seq, d), F32), compiler_params=_params("parallel"),
    )(*parts, w_in_t, res, *after)


def _mm_dw_in(parts, x0b, tk=1024):
    seq, d = x0b.shape
    tk = min(tk, seq)
    nk = seq // tk

    def body(a0, a1, a2, a3, b_ref, o_ref, acc_ref):
        kk = pl.program_id(0)

        @pl.when(kk == 0)
        def _():
            acc_ref[...] = jnp.zeros_like(acc_ref)

        b = b_ref[...]
        for c, a in enumerate((a0, a1, a2, a3)):
            acc_ref[DH_PART * c:DH_PART * (c + 1), :] += _dot(a[...], b, 0, 0)

        @pl.when(kk == nk - 1)
        def _():
            o_ref[...] = acc_ref[...].astype(o_ref.dtype)

    blk = pl.BlockSpec((tk, DH_PART), lambda kk: (kk, 0))
    return pl.pallas_call(
        body, name="mm_dw_in", grid=(nk,), in_specs=[blk] * 4 + [pl.BlockSpec((tk, d), lambda kk: (kk, 0))],
        out_specs=pl.BlockSpec((IN_PAD, d), lambda kk: (0, 0)), out_shape=jax.ShapeDtypeStruct((IN_PAD, d), MXU_DTYPE),
        scratch_shapes=[pltpu.VMEM((IN_PAD, d), F32)], compiler_params=_params("arbitrary"),
    )(*parts, x0b)


def _ln_stats(z):
    mu = jnp.mean(z, axis=-1, keepdims=True)
    zc = z - mu
    r = lax.rsqrt(jnp.mean(zc * zc, axis=-1, keepdims=True) + LN_EPS)
    return zc * r, r


def _ln_bwd_math(dy, xh, r, g):
    dxh = dy * g
    return r * (dxh - jnp.mean(dxh, axis=-1, keepdims=True) - xh * jnp.mean(dxh * xh, axis=-1, keepdims=True))


def _mix_ln1(o_mla, o_dil, w_o_mla, w_o_dil, x0, g, b, tm=512):
    seq, d = x0.shape
    tm = min(tm, seq)

    def body(om_ref, od_ref, wm_ref, wd_ref, x_ref, g_ref, b_ref, z_ref, y_ref, yb_ref):
        mix = _dot(od_ref[...], wd_ref[...], 1, 0)
        for hd in range(HEADS):
            mix += _dot(om_ref[hd], wm_ref[LANES * hd:LANES * (hd + 1), :], 1, 0)
        z = DN_ALPHA * x_ref[...] + mix
        xh, _ = _ln_stats(z)
        y = xh * g_ref[...] + b_ref[...]
        z_ref[...] = z
        y_ref[...] = y
        yb_ref[...] = y.astype(yb_ref.dtype)

    blk = pl.BlockSpec((tm, d), lambda i: (i, 0))
    vec = pl.BlockSpec((1, d), lambda i: (0, 0))
    shp = jax.ShapeDtypeStruct((seq, d), F32)
    return pl.pallas_call(
        body, name="mix_ln1", grid=(seq // tm,),
        in_specs=[pl.BlockSpec((HEADS, tm, LANES), lambda i: (0, i, 0)), pl.BlockSpec((tm, DIL_WIDTH), lambda i: (i, 0)),
                  pl.BlockSpec((HEADS * LANES, d), lambda i: (0, 0)), pl.BlockSpec((DIL_WIDTH, d), lambda i: (0, 0)), blk, vec, vec],
        out_specs=[blk, blk, blk], out_shape=[shp, shp, jax.ShapeDtypeStruct((seq, d), MXU_DTYPE)],
        compiler_params=_params("parallel"))(o_mla, o_dil, w_o_mla, w_o_dil, x0, g, b)


def _dx1_ln1_bwd(du, w_up_t, dz2, z, g, tm=256, after=()):
    seq, d = z.shape
    kdim = du.shape[1]
    tm = min(tm, seq)
    n_after = len(after)

    def body(du_ref, w_ref, r_ref, z_ref, g_ref, *rest):
        dz_ref, dzb_ref, dg_ref, db_ref = rest[n_after:]

        @pl.when(pl.program_id(0) == 0)
        def _():
            dg_ref[...] = jnp.zeros_like(dg_ref)
            db_ref[...] = jnp.zeros_like(db_ref)

        dyb = _dot(du_ref[...], w_ref[...], 1, 0) + DN_ALPHA * r_ref[...]
        xh, r = _ln_stats(z_ref[...])
        dg_ref[...] += jnp.sum(dyb * xh, axis=0, keepdims=True)
        db_ref[...] += jnp.sum(dyb, axis=0, keepdims=True)
        dz = _ln_bwd_math(dyb, xh, r, g_ref[...])
        dz_ref[...] = dz
        dzb_ref[...] = dz.astype(dzb_ref.dtype)

    blk = pl.BlockSpec((tm, d), lambda i: (i, 0))
    vec = pl.BlockSpec((1, d), lambda i: (0, 0))
    return pl.pallas_call(
        body, name="dx1_ln1_bwd", grid=(seq // tm,),
        in_specs=[pl.BlockSpec((tm, kdim), lambda i: (i, 0)),
                  pl.BlockSpec((kdim, d), lambda i: (0, 0), pipeline_mode=pl.Buffered(1)), blk, blk, vec] + [_ANY_SPEC] * n_after,
        out_specs=[blk, blk, vec, vec],
        out_shape=[jax.ShapeDtypeStruct((seq, d), F32), jax.ShapeDtypeStruct((seq, d), MXU_DTYPE),
                   jax.ShapeDtypeStruct((1, d), F32), jax.ShapeDtypeStruct((1, d), F32)],
        compiler_params=_params("arbitrary"))(du, w_up_t, dz2, z, g, *after)


def _down_ln2_loss_bwd(act, w_down, x1, target, g, b, tm=512):
    seq, d = x1.shape
    kdim = act.shape[1]
    tm = min(tm, seq)

    def body(a_ref, w_ref, x_ref, t_ref, g_ref, b_ref, dz_ref, dzb_ref, loss_ref, dg_ref, db_ref):
        @pl.when(pl.program_id(0) == 0)
        def _():
            loss_ref[...] = jnp.zeros_like(loss_ref)
            dg_ref[...] = jnp.zeros_like(dg_ref)
            db_ref[...] = jnp.zeros_like(db_ref)

        gv = g_ref[...]
        z = DN_ALPHA * x_ref[...] + _dot(a_ref[...], w_ref[...], 1, 0)
        xh, r = _ln_stats(z)
        err = (xh * gv + b_ref[...]) - t_ref[...]
        loss_ref[...] += 0.5 * jnp.sum(jnp.mean(err * err, axis=-1, keepdims=True), axis=0, keepdims=True)
        dy = err * (1.0 / d)
        dg_ref[...] += jnp.sum(dy * xh, axis=0, keepdims=True)
        db_ref[...] += jnp.sum(dy, axis=0, keepdims=True)
        dz = _ln_bwd_math(dy, xh, r, gv)
        dz_ref[...] = dz
        dzb_ref[...] = dz.astype(dzb_ref.dtype)

    blk = pl.BlockSpec((tm, d), lambda i: (i, 0))
    vec = pl.BlockSpec((1, d), lambda i: (0, 0))
    return pl.pallas_call(
        body, name="down_ln2_loss_bwd", grid=(seq // tm,),
        in_specs=[pl.BlockSpec((tm, kdim), lambda i: (i, 0)),
                  pl.BlockSpec((kdim, d), lambda i: (0, 0), pipeline_mode=pl.Buffered(1)), blk, blk, vec, vec],
        out_specs=[blk, blk, pl.BlockSpec((1, LANES), lambda i: (0, 0)), vec, vec],
        out_shape=[jax.ShapeDtypeStruct((seq, d), F32), jax.ShapeDtypeStruct((seq, d), MXU_DTYPE),
                   jax.ShapeDtypeStruct((1, LANES), F32),
                   jax.ShapeDtypeStruct((1, d), F32), jax.ShapeDtypeStruct((1, d), F32)],
        compiler_params=_params("arbitrary"))(act, w_down, x1, target, g, b)


HALO = 16


def _conv_rows(e, w_ref, b_ref):
    y = b_ref[...] + w_ref[0:1, :] * pltpu.roll(e, 2, 0)
    y = y + w_ref[1:2, :] * pltpu.roll(e, 1, 0)
    return y + w_ref[2:3, :] * e


_GELU_C = math.sqrt(2.0 / math.pi)
_GELU_A = 0.044715


def _gelu(x):
    return 0.5 * x * (1.0 + jnp.tanh(_GELU_C * (x + _GELU_A * (x * x * x))))


CONV_TN = 256


def _ffn_interleave(a, axis):
    shp = a.shape
    a = a.reshape(shp[:axis] + (2, D_FF // CONV_TN, CONV_TN) + shp[axis + 1:])
    return jnp.swapaxes(a, axis, axis + 1).reshape(shp)


def _ffn_deinterleave(a, axis):
    shp = a.shape
    a = a.reshape(shp[:axis] + (D_FF // CONV_TN, 2, CONV_TN) + shp[axis + 1:])
    return jnp.swapaxes(a, axis, axis + 1).reshape(shp)


def _conv_gate_fwd(u, conv_w, conv_b, tm=1024):
    seq = u.shape[0]
    tm = min(tm, seq)
    tn = CONV_TN

    def body(u_ref, up_ref, w_ref, b_ref, o_ref):
        first = pl.program_id(0) == 0
        e = jnp.concatenate([jnp.where(first, 0.0, up_ref[...]), u_ref[...]], axis=0)
        y = _conv_rows(e, w_ref, b_ref)[HALO:]
        o_ref[...] = (_gelu(y[:, tn:]) * y[:, :tn]).astype(o_ref.dtype)

    hb = tm // HALO
    return pl.pallas_call(
        body, name="conv_gate_fwd", grid=(seq // tm, D_FF // tn),
        in_specs=[pl.BlockSpec((tm, 2 * tn), lambda i, j: (i, j)),
                  pl.BlockSpec((HALO, 2 * tn), lambda i, j: (jnp.maximum(i * hb - 1, 0), j)),
                  pl.BlockSpec((3, 2 * tn), lambda i, j: (0, j)), pl.BlockSpec((1, 2 * tn), lambda i, j: (0, j))],
        out_specs=pl.BlockSpec((tm, tn), lambda i, j: (i, j)), out_shape=jax.ShapeDtypeStruct((seq, D_FF), MXU_DTYPE),
        compiler_params=_params("parallel", "parallel"),
    )(u, u, conv_w, conv_b)


def _conv_gate_bwd(u, d_act, conv_w, conv_b, tm=1024):
    seq = u.shape[0]
    tm = min(tm, seq)
    tn = CONV_TN
    ni = seq // tm
    rows_e = tm + 2 * HALO

    def body(u_ref, up_ref, un_ref, da_ref, dan_ref, w_ref, b_ref, du_ref, dw_ref, db_ref):
        i = pl.program_id(1)
        first, last = i == 0, i == ni - 1

        @pl.when(i == 0)
        def _():
            dw_ref[...] = jnp.zeros_like(dw_ref)
            db_ref[...] = jnp.zeros_like(db_ref)

        e = jnp.concatenate([jnp.where(first, 0.0, up_ref[...]), u_ref[...], jnp.where(last, 0.0, un_ref[...])], axis=0)
        y = _conv_rows(e, w_ref, b_ref)
        ya, yg = y[:, :tn], y[:, tn:]
        dact = jnp.concatenate([jnp.zeros((HALO, tn), F32), da_ref[...].astype(F32),
                                jnp.where(last, 0.0, dan_ref[...].astype(F32))], axis=0)
        th = jnp.tanh(_GELU_C * (yg + _GELU_A * (yg * yg * yg)))
        gelu = 0.5 * yg * (1.0 + th)
        gelu_grad = 0.5 * (1.0 + th) + 0.5 * yg * (1.0 - th * th) * (_GELU_C * (1.0 + 3.0 * _GELU_A * (yg * yg)))
        dy = jnp.concatenate([dact * gelu, dact * ya * gelu_grad], axis=1)
        du = w_ref[2:3, :] * dy + w_ref[1:2, :] * pltpu.roll(dy, rows_e - 1, 0) + w_ref[0:1, :] * pltpu.roll(dy, rows_e - 2, 0)
        du_ref[...] = du[HALO:HALO + tm].astype(du_ref.dtype)
        dyt = dy[HALO:HALO + tm]
        dw_ref[0:1, :] += jnp.sum(dyt * pltpu.roll(e, 2, 0)[HALO:HALO + tm], axis=0, keepdims=True)
        dw_ref[1:2, :] += jnp.sum(dyt * pltpu.roll(e, 1, 0)[HALO:HALO + tm], axis=0, keepdims=True)
        dw_ref[2:3, :] += jnp.sum(dyt * e[HALO:HALO + tm], axis=0, keepdims=True)
        db_ref[...] += jnp.sum(dyt, axis=0, keepdims=True)

    hb = tm // HALO
    nh = seq // HALO
    prev = lambda j, i: (jnp.maximum(i * hb - 1, 0), j)
    nxt = lambda j, i: (jnp.minimum((i + 1) * hb, nh - 1), j)
    return pl.pallas_call(
        body, name="conv_gate_bwd", grid=(D_FF // tn, ni),
        in_specs=[pl.BlockSpec((tm, 2 * tn), lambda j, i: (i, j)), pl.BlockSpec((HALO, 2 * tn), prev),
                  pl.BlockSpec((HALO, 2 * tn), nxt), pl.BlockSpec((tm, tn), lambda j, i: (i, j)), pl.BlockSpec((HALO, tn), nxt),
                  pl.BlockSpec((3, 2 * tn), lambda j, i: (0, j)), pl.BlockSpec((1, 2 * tn), lambda j, i: (0, j))],
        out_specs=[pl.BlockSpec((tm, 2 * tn), lambda j, i: (i, j)), pl.BlockSpec((3, 2 * tn), lambda j, i: (0, j)),
                   pl.BlockSpec((1, 2 * tn), lambda j, i: (0, j))],
        out_shape=[jax.ShapeDtypeStruct((seq, 2 * D_FF), MXU_DTYPE), jax.ShapeDtypeStruct((3, 2 * D_FF), F32),
                   jax.ShapeDtypeStruct((1, 2 * D_FF), F32)],
        compiler_params=_params("parallel", "arbitrary"),
    )(u, u, u, d_act, d_act, conv_w, conv_b)


def _pad_heads(w, width):
    w = jnp.transpose(w, (1, 0, 2))
    return jnp.pad(w, ((0, 0), (0, 0), (0, LANES - width))).astype(MXU_DTYPE)


def _heads_major(a):
    return jnp.transpose(a, (1, 0, 2)).reshape(-1, a.shape[2])


def _heads_minor(a, heads):
    return jnp.transpose(a.reshape(heads, -1, a.shape[1]), (1, 0, 2))


_LATENT = Q_RANK + KV_RANK
_ROPE_AT = _LATENT + NOPE_DIM
_ROPE_END = _ROPE_AT + ROPE_DIM


def _split_pad_rows(w_t):
    z = lambda n: jnp.zeros((n, w_t.shape[1]), w_t.dtype)
    return jnp.concatenate([w_t[:_LATENT], z(_ROPE_AT - _LATENT), w_t[_LATENT:_LATENT + ROPE_DIM], z(DH_PART - _ROPE_END),
                            w_t[_LATENT + ROPE_DIM:]], axis=0)


def _split_unpad_rows(w_p):
    return jnp.concatenate([w_p[:_LATENT], w_p[_ROPE_AT:_ROPE_END], w_p[DH_PART:]], axis=0)


def _pad_w_o(w_o):
    mla = jnp.pad(w_o[:MLA_WIDTH].reshape(HEADS, HEAD_DIM, D_MODEL), ((0, 0), (0, LANES - HEAD_DIM), (0, 0)))
    return mla.reshape(HEADS * LANES, D_MODEL).astype(MXU_DTYPE), w_o[MLA_WIDTH:].astype(MXU_DTYPE)


def _unpad_w_o(d_mla, d_dil):
    return jnp.concatenate([d_mla.reshape(HEADS, LANES, D_MODEL)[:, :HEAD_DIM].reshape(MLA_WIDTH, D_MODEL), d_dil], axis=0)


def _row(v):
    return v.reshape(1, -1).astype(F32)


def _layer_grads(x0, target, cw, first_after=(), late_weights=None, on_grads=None):
    seq = x0.shape[0]
    ctab, stab = _rope_tables(seq)
    gq, gk = cw["g_cq"], cw["g_ckv"]
    wq, wk, wv = cw["wq"], cw["wk"], cw["wv"]
    x0b = x0.astype(MXU_DTYPE)
    notify = (lambda stage, grads: ()) if on_grads is None else on_grads

    h = _mm(x0b, cw["w_in_t"], name="mm_h", tb=True, tm=1024, tn=IN_PAD, tk=1024, after=first_after)
    qf, kf, vp = _mla_prep(h, gq, gk, wq, wk, wv, ctab, stab)
    o_mla, o_mla_b, lse_mla = _mla_attn_fwd(qf, kf, vp)
    o_dil, o_dil_b, lse_dil = _dil_fwd(h)
    if late_weights is not None:
        cw = {**cw, **late_weights(o_mla_b)}
    cb = cw["conv_b"]
    z1, x1, x1b = _mix_ln1(o_mla_b, o_dil_b, cw["w_o_mla"], cw["w_o_dil"], x0, cw["ln1_g"], cw["ln1_b"])
    u = _mm(x1b, cw["w_up_t"], name="mm_up", tb=True, tm=512, tn=2 * D_FF, tk=1024)
    act = _conv_gate_fwd(u, cw["conv_w"], cb)
    dz2, dz2b, loss, d_ln2_g, d_ln2_b = _down_ln2_loss_bwd(act, cw["w_down"], x1, target, cw["ln2_g"], cw["ln2_b"])

    d_act = _mm(dz2b, cw["w_down"], name="mm_d_act", tb=True, out_dtype=MXU_DTYPE, tm=1024, tn=D_FF, tk=1024)
    d_w_down = _mm(act, dz2b, name="mm_dw_down", ta=True, out_dtype=MXU_DTYPE, tm=1408, tn=1024, tk=1024)
    du, d_conv_w, d_conv_b = _conv_gate_bwd(u, d_act, cw["conv_w"], cb)
    d_w_up_t = _mm(du, x1b, name="mm_dw_up", ta=True, out_dtype=MXU_DTYPE, tm=1408, tn=1024, tk=2048)
    grads = dict(w_up_t=d_w_up_t, w_down=d_w_down, conv_w=d_conv_w, conv_b=d_conv_b, ln2_g=d_ln2_g, ln2_b=d_ln2_b)
    dz1, dz1b, d_ln1_g, d_ln1_b = _dx1_ln1_bwd(du, cw["w_up_t"], dz2, z1, cw["ln1_g"], after=notify("ffn", grads))
    do_mla = _mm_do_mla(dz1b, cw["w_o_mla"])
    do_dil = _mm(dz1b, cw["w_o_dil"], name="mm_do_dil", tb=True, tm=1024, tn=512, tk=1024)
    d_w_o_mla = _mm_dw_o_mla(o_mla_b, dz1b)
    d_w_o_dil = _mm(o_dil_b, dz1b, name="mm_dw_o_dil", ta=True, out_dtype=MXU_DTYPE, tm=512, tn=1024, tk=1024)
    grads.update(w_o_mla=d_w_o_mla, w_o_dil=d_w_o_dil, ln1_g=d_ln1_g, ln1_b=d_ln1_b)
    dq_dil, dk_dil, dv_dil = _dil_bwd(h, o_dil, lse_dil, do_dil)
    dqf, dkf, dvf = _mla_attn_bwd(qf, kf, vp, o_mla, lse_mla, do_mla)
    dh_mla, d_wq, d_wk, d_wv, d_gq, d_gk = _mla_prep_bwd(h, gq, gk, wq, wk, wv, ctab, stab, dqf, dkf, dvf,
                                                          after=notify("w_o", grads))
    dh = (dh_mla, dq_dil, dk_dil, dv_dil)
    d_w_in_t = _mm_dw_in(dh, x0b)
    grads.update(w_in_t=d_w_in_t, wq=d_wq, wk=d_wk, wv=d_wv, g_cq=d_gq, g_ckv=d_gk, loss=loss)
    grad_x = _mm_dx0(dh, cw["w_in_t"], dz1, after=notify("rest", grads))
    return loss, grad_x, grads


def _all_gather(blocks, name):
    na = len(blocks)

    def body(*refs):
        ins, outs = refs[:na], refs[na:2 * na]
        send_sems, recv_sems, local_sems = refs[2 * na:]
        x, y, c = lax.axis_index("x"), lax.axis_index("y"), lax.axis_index("c")
        me, sibling = (x, y, c), (x, y, 1 - c)
        chips = [(1 - x, y), (x, 1 - y), (1 - x, 1 - y)]

        def slot(out, pos):
            return out.at[4 * pos[0] + 2 * pos[1] + pos[2]]

        def copy(a, k, block, to, src=None):
            return pltpu.make_async_remote_copy(
                src_ref=slot(outs[a], block) if src is None else src, dst_ref=slot(outs[a], block),
                send_sem=send_sems.at[7 * a + k], recv_sem=recv_sems.at[7 * a + k],
                device_id=to, device_id_type=pl.DeviceIdType.MESH)

        mine = [pltpu.make_async_copy(ins[a], slot(outs[a], me), local_sems.at[a]) for a in range(na)]
        for cp in mine:
            cp.start()
        first = []
        for a in range(na):
            first.append(copy(a, 0, me, sibling, src=ins[a]))
            first += [copy(a, 1 + j, me, (*chip, c), src=ins[a]) for j, chip in enumerate(chips)]
        for cp in first:
            cp.start()
        passed = []
        for j, chip in enumerate(chips):
            for a in range(na):
                copy(a, 1 + j, (*chip, c), me).wait_recv()
                cp = copy(a, 4 + j, (*chip, c), sibling)
                cp.start()
                passed.append(cp)
        for a in range(na):
            copy(a, 0, sibling, me).wait_recv()
            for j, chip in enumerate(chips):
                copy(a, 4 + j, (*chip, 1 - c), me).wait_recv()
        for cp in first + passed:
            cp.wait_send()
        for cp in mine:
            cp.wait()

    any_spec = pl.BlockSpec(memory_space=pl.ANY)
    return pl.pallas_call(
        body, name=name, in_specs=[any_spec] * na, out_specs=[any_spec] * na,
        out_shape=[jax.ShapeDtypeStruct((N_DEV,) + b.shape, b.dtype) for b in blocks],
        scratch_shapes=[pltpu.SemaphoreType.DMA((7 * na,)), pltpu.SemaphoreType.DMA((7 * na,)), pltpu.SemaphoreType.DMA((na,))],
    )(*blocks)


_HBM_SPEC = pl.BlockSpec(memory_space=pltpu.HBM)
_SEM_SPEC = pl.BlockSpec(memory_space=pltpu.SEMAPHORE)
_DATAFLOW = pltpu.CompilerParams(has_side_effects=pltpu.SideEffectType.DATAFLOW_SIDE_EFFECTING)


def _split_copies(ins, lands, send_sems, recv_sems, gather):
    x, y, c = lax.axis_index("x"), lax.axis_index("y"), lax.axis_index("c")
    me = 4 * x + 2 * y + c
    copies = []
    for a in range(len(ins)):
        for d in range(1, N_DEV):
            px, py, pc = x ^ (d >> 2), y ^ ((d >> 1) & 1), c ^ (d & 1)
            copies.append(pltpu.make_async_remote_copy(
                src_ref=ins[a] if gather else ins[a].at[4 * px + 2 * py + pc], dst_ref=lands[a].at[me],
                send_sem=send_sems.at[7 * a + d - 1], recv_sem=recv_sems.at[7 * a + d - 1],
                device_id=(px, py, pc), device_id_type=pl.DeviceIdType.MESH))
    return copies


def _send_start(srcs, gather, name):
    na = len(srcs)
    land_types = [pltpu.HBM(((N_DEV,) + s.shape) if gather else s.shape, s.dtype) for s in srcs]

    def body(*refs):
        ins, lands = refs[:na], refs[na:2 * na]
        send_sems, recv_sems, token = refs[2 * na], refs[2 * na + 1], refs[-1]
        for cp in _split_copies(ins, lands, send_sems, recv_sems, gather):
            cp.start()
        token[...] = jnp.zeros_like(token)

    hbm = lambda a: pltpu.with_memory_space_constraint(a, pltpu.HBM)
    outs = pl.pallas_call(
        body, name=name,
        out_shape=(pltpu.SemaphoreType.DMA((7 * na,)), pltpu.SemaphoreType.DMA((7 * na,)),
                   *[pltpu.HBM(s.shape, s.dtype) for s in srcs], *land_types, jax.ShapeDtypeStruct((8, LANES), F32)),
        in_specs=[_HBM_SPEC] * (2 * na),
        out_specs=(_SEM_SPEC, _SEM_SPEC, *[_HBM_SPEC] * (2 * na), pl.BlockSpec(memory_space=pltpu.VMEM)),
        input_output_aliases={i: 2 + i for i in range(2 * na)}, compiler_params=_DATAFLOW,
    )(*[hbm(s) for s in srcs], *[hbm(lax.empty(t.shape, t.dtype)) for t in land_types])
    return dict(send=outs[0], recv=outs[1], srcs=list(outs[2:2 + na]), lands=list(outs[2 + na:2 + 2 * na]), token=outs[-1],
                gather=gather)


def _send_wait(handle, after, name):
    na = len(handle["srcs"])
    gather = handle["gather"]
    after = list(after)

    def body(*refs):
        ins, lands = refs[:na], refs[na:2 * na]
        send_sems, recv_sems = refs[2 * na], refs[2 * na + 1]
        for cp in _split_copies(ins, lands, send_sems, recv_sems, gather):
            cp.wait_send()
            cp.wait_recv()

    both = handle["srcs"] + handle["lands"]
    outs = pl.pallas_call(
        body, name=name, out_shape=[pltpu.HBM(a.shape, a.dtype) for a in both],
        in_specs=[_HBM_SPEC] * (2 * na) + [_SEM_SPEC, _SEM_SPEC] + [_ANY_SPEC] * len(after),
        out_specs=[_HBM_SPEC] * (2 * na), input_output_aliases={i: i for i in range(2 * na)}, compiler_params=_DATAFLOW,
    )(*both, handle["send"], handle["recv"], *after)
    return list(outs[:na]), list(outs[na:])


def _sum_slots(p_ref):
    g = p_ref[0].astype(F32)
    for s in range(1, p_ref.shape[0]):
        g = g + p_ref[s].astype(F32)
    return g


def _adamw_refs(g, w_ref, m_ref, v_ref, g_out, d_out, m_out, v_out):
    c1 = 1.0 - ADAM_B1 ** ADAM_STEP
    c2 = 1.0 - ADAM_B2 ** ADAM_STEP
    m_new = ADAM_B1 * m_ref[...] + (1.0 - ADAM_B1) * g
    v_new = ADAM_B2 * v_ref[...] + (1.0 - ADAM_B2) * (g * g)
    g_out[...] = g
    m_out[...] = m_new
    v_out[...] = v_new
    d_out[...] = -ADAM_LR * ((m_new / c1) / (jnp.sqrt(v_new / c2) + ADAM_EPS) + ADAM_WD * w_ref[...])


def _adamw(parts, w, m, v, name):
    npart, r, n = parts.shape
    tr = r if r <= 256 else max(t for t in range(16, 257, 16) if r % t == 0)

    def body(p_ref, w_ref, m_ref, v_ref, g_out, d_out, m_out, v_out):
        _adamw_refs(_sum_slots(p_ref), w_ref, m_ref, v_ref, g_out, d_out, m_out, v_out)

    blk = pl.BlockSpec((tr, n), lambda i: (i, 0))
    shp = jax.ShapeDtypeStruct((r, n), F32)
    return pl.pallas_call(
        body, name=name, grid=(r // tr,), in_specs=[pl.BlockSpec((npart, tr, n), lambda i: (0, i, 0)), blk, blk, blk],
        out_specs=[blk] * 4, out_shape=[shp] * 4, compiler_params=_params("parallel"),
    )(parts, w, m, v)


def _adamw_small(parts, ws, ms, vs, loss_parts, name):
    n = len(parts)

    def body(*refs):
        ins, outs = refs[:4 * n + 1], refs[4 * n + 1:]
        for i in range(n):
            _adamw_refs(_sum_slots(ins[i]), ins[n + i], ins[2 * n + i], ins[3 * n + i], *outs[4 * i:4 * i + 4])
        outs[4 * n][...] = _sum_slots(ins[4 * n])

    out_shape = [jax.ShapeDtypeStruct(w.shape, F32) for w in ws for _ in range(4)]
    res = pl.pallas_call(body, name=name, out_shape=out_shape + [jax.ShapeDtypeStruct((1, LANES), F32)],
                         compiler_params=_params())(*parts, *ws, *ms, *vs, loss_parts)
    return [res[4 * i:4 * i + 4] for i in range(n)], res[4 * n]


REPLICATED = ("g_cq", "g_ckv", "w_uk", "w_uv", "ln1_g", "ln1_b", "conv_b", "ln2_g", "ln2_b")
ALL_WEIGHTS = ("w_in", "g_cq", "g_ckv", "w_uq", "w_uk", "w_uv", "w_o", "ln1_g", "ln1_b", "w_up", "conv_w", "conv_b",
               "w_down", "ln2_g", "ln2_b")


def kernel(x, w_in, g_cq, g_ckv, w_uq, w_uk, w_uv, w_o, ln1_g, ln1_b, w_up, conv_w, conv_b, w_down, ln2_g, ln2_b, loss_target, m_w_in, m_g_cq, m_g_ckv, m_w_uq, m_w_uk, m_w_uv, m_w_o, m_ln1_g, m_ln1_b, m_w_up, m_conv_w, m_conv_b, m_w_down, m_ln2_g, m_ln2_b, v_w_in, v_g_cq, v_g_ckv, v_w_uq, v_w_uk, v_w_uv, v_w_o, v_ln1_g, v_ln1_b, v_w_up, v_conv_w, v_conv_b, v_w_down, v_ln2_g, v_ln2_b):
    w = dict(w_in=w_in, g_cq=g_cq, g_ckv=g_ckv, w_uq=w_uq, w_uk=w_uk, w_uv=w_uv, w_o=w_o, ln1_g=ln1_g, ln1_b=ln1_b,
             w_up=w_up, conv_w=conv_w, conv_b=conv_b, w_down=w_down, ln2_g=ln2_g, ln2_b=ln2_b)
    m = dict(w_in=m_w_in, g_cq=m_g_cq, g_ckv=m_g_ckv, w_uq=m_w_uq, w_uk=m_w_uk, w_uv=m_w_uv, w_o=m_w_o, ln1_g=m_ln1_g,
             ln1_b=m_ln1_b, w_up=m_w_up, conv_w=m_conv_w, conv_b=m_conv_b, w_down=m_w_down, ln2_g=m_ln2_g, ln2_b=m_ln2_b)
    v = dict(w_in=v_w_in, g_cq=v_g_cq, g_ckv=v_g_ckv, w_uq=v_w_uq, w_uk=v_w_uk, w_uv=v_w_uv, w_o=v_w_o, ln1_g=v_ln1_g,
             ln1_b=v_ln1_b, w_up=v_w_up, conv_w=v_conv_w, conv_b=v_conv_b, w_down=v_w_down, ln2_g=v_ln2_g, ln2_b=v_ln2_b)
    me = 4 * lax.axis_index("x") + 2 * lax.axis_index("y") + lax.axis_index("c")
    wire = lambda a: a.astype(WIRE_DTYPE)
    pad_taps = lambda a: jnp.pad(a, ((0, 8 - a.shape[0]), (0, 0)))

    own_slot = lambda buf, block: lax.dynamic_update_index_in_dim(buf, block, me, 0)
    blocks = lambda a: wire(a).reshape((N_DEV, a.shape[0] // N_DEV) + a.shape[1:])

    g_in, g_uq, g_conv = _all_gather(
        [wire(w_in).T, _heads_major(wire(w_uq)), pad_taps(conv_w)],
        "gather_weights")
    late = _send_start([wire(w_o), wire(w_up).T, wire(w_down)], True, "gather_late_start")
    r_uq_dev, e_uq = w_uq.shape[0], w_uq.shape[2]
    wq = jnp.transpose(g_uq.reshape(N_DEV, HEADS, r_uq_dev, e_uq), (1, 0, 2, 3)).reshape(HEADS, Q_RANK, e_uq)
    cw = dict(
        w_in_t=_split_pad_rows(g_in.reshape(-1, D_MODEL)).astype(MXU_DTYPE),
        wq=jnp.pad(wq, ((0, 0), (0, 0), (0, LANES - e_uq))).astype(MXU_DTYPE),
        wk=_pad_heads(w_uk, NOPE_DIM), wv=_pad_heads(w_uv, HEAD_DIM),
        conv_w=_ffn_interleave(jnp.transpose(g_conv[:, :conv_w.shape[0]], (1, 0, 2)).reshape(conv_w.shape[0], -1), 1),
        g_cq=_row(g_cq), g_ckv=_row(g_ckv), ln1_g=_row(ln1_g), ln1_b=_row(ln1_b), conv_b=_ffn_interleave(_row(conv_b), 1),
        ln2_g=_row(ln2_g), ln2_b=_row(ln2_b))

    def late_weights(after):
        own, landed = _send_wait(late, [after], "gather_late_wait")
        g_o, g_up, g_down = [own_slot(buf, blk) for buf, blk in zip(landed, own)]
        w_o_mla, w_o_dil = _pad_w_o(g_o.reshape(-1, D_MODEL))
        return dict(w_o_mla=w_o_mla, w_o_dil=w_o_dil, w_up_t=_ffn_interleave(g_up.reshape(-1, D_MODEL), 0).astype(MXU_DTYPE),
                    w_down=g_down.reshape(-1, D_MODEL).astype(MXU_DTYPE))

    sent = {}

    def on_grads(stage, g):
        if stage == "ffn":
            sent[stage] = [_send_start([blocks(_ffn_deinterleave(g["w_up_t"], 0)), blocks(g["w_down"])], False, "exchange_ffn_start")]
        elif stage == "w_o":
            sent[stage] = [_send_start([blocks(_unpad_w_o(g["w_o_mla"], g["w_o_dil"]))], False, "exchange_w_o_start")]
        else:
            d_in = blocks(_split_unpad_rows(g["w_in_t"]))
            d_uq = wire(jnp.transpose(g["wq"][:, :, :e_uq].reshape(HEADS, N_DEV, r_uq_dev, e_uq), (1, 0, 2, 3))
                        ).reshape(N_DEV, HEADS * r_uq_dev, e_uq)
            dense = lambda a, width: wire(a[:, :, :width]).reshape(-1, LANES)
            small = dict(g_cq=g["g_cq"], g_ckv=g["g_ckv"], w_uk=dense(g["wk"], NOPE_DIM), w_uv=dense(g["wv"], HEAD_DIM),
                         ln1_g=g["ln1_g"], ln1_b=g["ln1_b"], conv_b=_ffn_deinterleave(g["conv_b"], 1), ln2_g=g["ln2_g"],
                         ln2_b=g["ln2_b"])
            sent[stage] = [_send_start([d_in, d_uq], False, "exchange_rest_start"),
                           _send_start([small[n] for n in REPLICATED] + [_ffn_deinterleave(g["conv_w"], 1), g["loss"]],
                                       True, "gather_small_start")]
        return [h["token"] for h in sent[stage]]

    _, grad_x, _ = _layer_grads(x[0], loss_target[0], cw, [late["token"]], late_weights, on_grads)

    def landed(handle, after, name):
        own, got = _send_wait(handle, after, name)
        pick = (lambda a: a) if handle["gather"] else (lambda a: lax.dynamic_index_in_dim(a, me, 0, keepdims=False))
        return [own_slot(buf, pick(src)) for buf, src in zip(got, own)]

    out = {}

    def update(name, parts, view=None):
        to2d = {None: lambda a: a, "t": lambda a: a.T, "heads": _heads_major}[view]
        back = {None: lambda a: a, "t": lambda a: a.T, "heads": lambda a: _heads_minor(a, HEADS)}[view]
        res = _adamw(parts, to2d(w[name]), to2d(m[name]), to2d(v[name]), "adamw_" + name)
        for kind, a in zip(("grad", "delta", "new_m", "new_v"), res):
            out[kind, name] = back(a)
        return res[0]

    r_up, r_down = landed(sent["ffn"][0], [grad_x], "exchange_ffn_wait")
    (r_o,) = landed(sent["w_o"][0], [grad_x], "exchange_w_o_wait")
    done = [update("w_up", r_up, "t"), update("w_down", r_down), update("w_o", r_o)]
    r_in, r_uq = landed(sent["rest"][0], done, "exchange_rest_wait")
    *rep_all, cw_all, loss_all = landed(sent["rest"][1], done, "gather_small_wait")
    update("w_in", r_in, "t")
    update("w_uq", r_uq, "heads")
    heads_major = ("w_uk", "w_uv")
    two_d = lambda n, a: _heads_major(a) if n in heads_major else a.reshape(1, -1)
    rep_all = [p.reshape(N_DEV, -1, w[n].shape[2]) if n in heads_major else p for n, p in zip(REPLICATED, rep_all)]
    res, loss_sum = _adamw_small(rep_all, *[[two_d(n, d[n]) for n in REPLICATED] for d in (w, m, v)], loss_all, "adamw_replicated")
    for n, quad in zip(REPLICATED, res):
        for kind, a in zip(("grad", "delta", "new_m", "new_v"), quad):
            out[kind, n] = _heads_minor(a, HEADS) if n in heads_major else a.reshape(w[n].shape)
    loss = loss_sum[0, 0]
    ncw = conv_w.shape[1]
    update("conv_w", lax.dynamic_slice_in_dim(cw_all[:, :conv_w.shape[0]], me * ncw, ncw, axis=2))

    return (loss, grad_x[None], *[out[kind, n] for kind in ("grad", "delta", "new_m", "new_v") for n in ALL_WEIGHTS])
```

```python
import math

import jax
import jax.numpy as jnp
import numpy as np
from jax import lax
from jax.experimental import pallas as pl
from jax.experimental.pallas import tpu as pltpu

F32 = jnp.float32
MXU_DTYPE = jnp.bfloat16
WIRE_DTYPE = jnp.bfloat16

N_DEV = 8
D_MODEL = 1024
HEADS = 8
HEAD_DIM = 64
LANES = 128
Q_RANK, KV_RANK, ROPE_DIM, NOPE_DIM = 256, 128, 32, 64
DIL_WIDTH = HEADS * HEAD_DIM
MLA_WIDTH = HEADS * HEAD_DIM
IN_PAD = 2048
DH_PART = IN_PAD // 4
D_FF = 2816
ROPE_THETA = 10000.0
DIL_PAIRS = ((128, 1), (512, 4), (2048, 16))
DIL_BLOCK = 128
DN_ALPHA = 2.0 ** 0.25
LN_EPS = 1e-5
RMS_EPS = 1e-6
ONES_LANE = HEAD_DIM
MLA_SCALE = 1.0 / math.sqrt(NOPE_DIM + ROPE_DIM)
MLA_SCALE_LOG2 = MLA_SCALE * math.log2(math.e)
DIL_SCALE = 1.0 / math.sqrt(HEAD_DIM)
ALIBI_SLOPES = tuple(2.0 ** (-8.0 * (h + 1) / HEADS) for h in range(HEADS))
NEG_BIG = -1e30
ADAM_LR, ADAM_B1, ADAM_B2, ADAM_EPS, ADAM_WD, ADAM_STEP = 0.001, 0.9, 0.999, 1e-08, 0.01, 10
VMEM_LIMIT = 48 * 1024 * 1024


def _params(*sem):
    return pltpu.CompilerParams(dimension_semantics=sem or None, vmem_limit_bytes=VMEM_LIMIT)


def _dot(a, b, ca, cb):
    return lax.dot_general(a, b, (((ca,), (cb,)), ((), ())), preferred_element_type=F32)


_ANY_SPEC = pl.BlockSpec(memory_space=pl.ANY)


def _mm(a, b, *, name, tm, tn, tk, ta=False, tb=False, out_dtype=F32, after=()):
    m, k = (a.shape[1], a.shape[0]) if ta else a.shape
    n = b.shape[0] if tb else b.shape[1]
    assert (b.shape[1] if tb else b.shape[0]) == k
    tm, tn, tk = min(tm, m), min(tn, n), min(tk, k)
    assert m % tm == 0 and n % tn == 0 and k % tk == 0, (name, m, n, k, tm, tn, tk)
    nk = k // tk
    a_spec = (pl.BlockSpec((tk, tm), lambda i, j, kk: (kk, i)) if ta
              else pl.BlockSpec((tm, tk), lambda i, j, kk: (i, kk)))
    b_mode = dict(pipeline_mode=pl.Buffered(1)) if (tn == n and tk == k) else {}
    b_spec = (pl.BlockSpec((tn, tk), lambda i, j, kk: (j, kk), **b_mode) if tb
              else pl.BlockSpec((tk, tn), lambda i, j, kk: (kk, j), **b_mode))
    o_spec = pl.BlockSpec((tm, tn), lambda i, j, kk: (i, j))
    n_in = 2 + len(after)
    ca, cb = (0 if ta else 1), (1 if tb else 0)

    def body(*refs):
        a_ref, b_ref, o_ref = refs[0], refs[1], refs[n_in]
        part = _dot(a_ref[...].astype(MXU_DTYPE), b_ref[...].astype(MXU_DTYPE), ca, cb)
        if nk == 1:
            o_ref[...] = part.astype(o_ref.dtype)
            return
        acc_ref = refs[-1]
        kk = pl.program_id(2)

        @pl.when(kk == 0)
        def _():
            acc_ref[...] = part

        @pl.when(kk > 0)
        def _():
            acc_ref[...] += part

        @pl.when(kk == nk - 1)
        def _():
            o_ref[...] = acc_ref[...].astype(o_ref.dtype)

    return pl.pallas_call(
        body, name=name, grid=(m // tm, n // tn, nk), in_specs=[a_spec, b_spec] + [_ANY_SPEC] * len(after), out_specs=o_spec,
        out_shape=jax.ShapeDtypeStruct((m, n), out_dtype),
        scratch_shapes=[pltpu.VMEM((tm, tn), F32)] if nk > 1 else [],
        compiler_params=_params("parallel", "parallel", "arbitrary"),
    )(a, b, *after)


def _w_o_bwd(dz, o_mla, o_dil, w_o_mla, w_o_dil, tm=512):
    seq, d = dz.shape
    tm = min(tm, seq)
    nstep = seq // tm

    def body(a_ref, om_ref, od_ref, wm_ref, wd_ref, dom_ref, dod_ref, dwm_ref, dwd_ref, accm_ref, accd_ref):
        step = pl.program_id(0)

        @pl.when(step == 0)
        def _():
            accm_ref[...] = jnp.zeros_like(accm_ref)
            accd_ref[...] = jnp.zeros_like(accd_ref)

        a = a_ref[...]
        for hd in range(HEADS):
            rows = slice(LANES * hd, LANES * (hd + 1))
            dom_ref[hd] = _dot(a, wm_ref[rows, :], 1, 1)
            accm_ref[rows, :] += _dot(om_ref[hd], a, 0, 0)
        dod_ref[...] = _dot(a, wd_ref[...], 1, 1)
        accd_ref[...] += _dot(od_ref[...], a, 0, 0)

        @pl.when(step == nstep - 1)
        def _():
            dwm_ref[...] = accm_ref[...].astype(dwm_ref.dtype)
            dwd_ref[...] = accd_ref[...].astype(dwd_ref.dtype)

    once = dict(pipeline_mode=pl.Buffered(1))
    return pl.pallas_call(
        body, name="w_o_bwd", grid=(nstep,),
        in_specs=[pl.BlockSpec((tm, d), lambda i: (i, 0)), pl.BlockSpec((HEADS, tm, LANES), lambda i: (0, i, 0)),
                  pl.BlockSpec((tm, DIL_WIDTH), lambda i: (i, 0)), pl.BlockSpec((HEADS * LANES, d), lambda i: (0, 0), **once),
                  pl.BlockSpec((DIL_WIDTH, d), lambda i: (0, 0), **once)],
        out_specs=[pl.BlockSpec((HEADS, tm, LANES), lambda i: (0, i, 0)), pl.BlockSpec((tm, DIL_WIDTH), lambda i: (i, 0)),
                   pl.BlockSpec((HEADS * LANES, d), lambda i: (0, 0)), pl.BlockSpec((DIL_WIDTH, d), lambda i: (0, 0))],
        out_shape=[jax.ShapeDtypeStruct((HEADS, seq, LANES), F32), jax.ShapeDtypeStruct((seq, DIL_WIDTH), F32),
                   jax.ShapeDtypeStruct((HEADS * LANES, d), MXU_DTYPE), jax.ShapeDtypeStruct((DIL_WIDTH, d), MXU_DTYPE)],
        scratch_shapes=[pltpu.VMEM((HEADS * LANES, d), F32), pltpu.VMEM((DIL_WIDTH, d), F32)],
        compiler_params=_params("arbitrary"),
    )(dz, o_mla, o_dil, w_o_mla, w_o_dil)


def _rope_tables(seq):
    half = ROPE_DIM // 2
    f32 = np.float32
    freqs = np.power(f32(ROPE_THETA), -np.arange(half, dtype=f32) / f32(half))
    ang = np.arange(seq, dtype=f32)[:, None] * freqs[None, :]
    cos, sin = np.cos(ang, dtype=f32), np.sin(ang, dtype=f32)
    one = np.ones((seq, NOPE_DIM), f32)
    tail = np.ones((seq, LANES - NOPE_DIM - ROPE_DIM), f32)
    ctab = np.concatenate([one, cos, cos, tail], axis=1)
    stab = np.concatenate([0 * one, -sin, sin, 0 * tail], axis=1)
    return jnp.asarray(ctab), jnp.asarray(stab)


def _rope_swap(t):
    lane = lax.broadcasted_iota(jnp.int32, t.shape, 1)
    half = ROPE_DIM // 2
    return jnp.where(lane < NOPE_DIM + half, pltpu.roll(t, LANES - half, 1), pltpu.roll(t, half, 1))


def _rope(t, ctab, stab):
    return t * ctab + _rope_swap(t) * stab


def _rope_inv(t, ctab, stab):
    return t * ctab - _rope_swap(t) * stab


def _rms(x, g):
    r = lax.rsqrt(jnp.mean(x * x, axis=-1, keepdims=True) + RMS_EPS)
    xh = x * r
    return xh, r, xh * g


def _mla_prep(h, g_cq, g_ckv, wq, wk, wv, ctab, stab, tm=512):
    seq = h.shape[0]
    tm = min(tm, seq)

    def body(h_ref, gq_ref, gk_ref, wq_ref, wk_ref, wv_ref, c_ref, s_ref, q_out, k_out, v_out):
        hb = h_ref[...]
        ctab_, stab_ = c_ref[...], s_ref[...]
        _, _, cqn = _rms(hb[:, :Q_RANK], gq_ref[...])
        _, _, ckn = _rms(hb[:, Q_RANK:Q_RANK + KV_RANK], gk_ref[...])
        cqn = cqn.astype(MXU_DTYPE)
        ckn = ckn.astype(MXU_DTYPE)
        krr = _rope(hb[:, Q_RANK + KV_RANK:], ctab_, stab_)
        ones_lane = (lax.broadcasted_iota(jnp.int32, (1, LANES), 1) == ONES_LANE).astype(F32)
        for hd in range(HEADS):
            q = _dot(cqn, wq_ref[hd], 1, 0)
            q_out[hd] = _rope(q, ctab_, stab_).astype(q_out.dtype)
            k_out[hd] = (_dot(ckn, wk_ref[hd], 1, 0) + krr).astype(k_out.dtype)
            v_out[hd] = (_dot(ckn, wv_ref[hd], 1, 0) + ones_lane).astype(v_out.dtype)

    full = lambda *shape: pl.BlockSpec(shape, lambda i: (0,) * len(shape))
    slab = pl.BlockSpec((HEADS, tm, LANES), lambda i: (0, i, 0))
    shp = jax.ShapeDtypeStruct((HEADS, seq, LANES), MXU_DTYPE)
    return pl.pallas_call(
        body, name="mla_prep", grid=(seq // tm,),
        in_specs=[pl.BlockSpec((tm, DH_PART), lambda i: (i, 0)), full(1, Q_RANK), full(1, KV_RANK),
                  full(HEADS, Q_RANK, LANES), full(HEADS, KV_RANK, LANES), full(HEADS, KV_RANK, LANES),
                  pl.BlockSpec((tm, LANES), lambda i: (i, 0)), pl.BlockSpec((tm, LANES), lambda i: (i, 0))],
        out_specs=[slab, slab, slab], out_shape=[shp, shp, shp],
        compiler_params=_params("parallel"),
    )(h, g_cq, g_ckv, wq, wk, wv, ctab, stab)


def _mla_prep_bwd(h, g_cq, g_ckv, wq, wk, wv, ctab, stab, dq, dk, dv, tm=512, after=()):
    seq = h.shape[0]
    tm = min(tm, seq)
    n_after = len(after)

    def body(h_ref, gq_ref, gk_ref, wq_ref, wk_ref, wv_ref, c_ref, s_ref, dq_ref, dk_ref, dv_ref, *rest):
        dh_ref, dwq_ref, dwk_ref, dwv_ref, dgq_ref, dgk_ref = rest[n_after:]

        @pl.when(pl.program_id(0) == 0)
        def _():
            for r in (dwq_ref, dwk_ref, dwv_ref, dgq_ref, dgk_ref):
                r[...] = jnp.zeros_like(r)

        hb = h_ref[...]
        ctab_, stab_ = c_ref[...], s_ref[...]
        gq, gk = gq_ref[...], gk_ref[...]
        xq, rq, cqn = _rms(hb[:, :Q_RANK], gq)
        xk, rk, ckn = _rms(hb[:, Q_RANK:Q_RANK + KV_RANK], gk)
        cqn = cqn.astype(MXU_DTYPE)
        ckn = ckn.astype(MXU_DTYPE)
        d_cqn = jnp.zeros((tm, Q_RANK), F32)
        d_ckn = jnp.zeros((tm, KV_RANK), F32)
        d_krr = jnp.zeros((tm, LANES), F32)
        for hd in range(HEADS):
            dqh = _rope_inv(dq_ref[hd], ctab_, stab_).astype(MXU_DTYPE)
            d_cqn += _dot(dqh, wq_ref[hd], 1, 1)
            dwq_ref[hd] += _dot(cqn, dqh, 0, 0)
            dkh = dk_ref[hd]
            d_krr += dkh
            dkh = dkh.astype(MXU_DTYPE)
            d_ckn += _dot(dkh, wk_ref[hd], 1, 1)
            dwk_ref[hd] += _dot(ckn, dkh, 0, 0)
            dvh = dv_ref[hd].astype(MXU_DTYPE)
            d_ckn += _dot(dvh, wv_ref[hd], 1, 1)
            dwv_ref[hd] += _dot(ckn, dvh, 0, 0)
        lane = lax.broadcasted_iota(jnp.int32, (tm, LANES), 1)
        rot = (lane >= NOPE_DIM) & (lane < NOPE_DIM + ROPE_DIM)
        d_kr = jnp.where(rot, _rope_inv(jnp.where(rot, d_krr, 0.0), ctab_, stab_), 0.0)

        def rms_bwd(dy, xh, r, g, dg_ref):
            dg_ref[...] += jnp.sum(dy * xh, axis=0, keepdims=True)
            dxh = dy * g
            return r * (dxh - xh * jnp.mean(dxh * xh, axis=-1, keepdims=True))

        d_cq = rms_bwd(d_cqn, xq, rq, gq, dgq_ref)
        d_ck = rms_bwd(d_ckn, xk, rk, gk, dgk_ref)
        dh_ref[...] = jnp.concatenate([d_cq, d_ck, d_kr], axis=1).astype(dh_ref.dtype)

    full = lambda *shape: pl.BlockSpec(shape, lambda i: (0,) * len(shape))
    slab = pl.BlockSpec((HEADS, tm, LANES), lambda i: (0, i, 0))
    return pl.pallas_call(
        body, name="mla_prep_bwd", grid=(seq // tm,),
        in_specs=[pl.BlockSpec((tm, DH_PART), lambda i: (i, 0)), full(1, Q_RANK), full(1, KV_RANK),
                  full(HEADS, Q_RANK, LANES), full(HEADS, KV_RANK, LANES), full(HEADS, KV_RANK, LANES),
                  pl.BlockSpec((tm, LANES), lambda i: (i, 0)), pl.BlockSpec((tm, LANES), lambda i: (i, 0)),
                  slab, slab, slab] + [_ANY_SPEC] * n_after,
        out_specs=[pl.BlockSpec((tm, DH_PART), lambda i: (i, 0)), full(HEADS, Q_RANK, LANES), full(HEADS, KV_RANK, LANES),
                   full(HEADS, KV_RANK, LANES), full(1, Q_RANK), full(1, KV_RANK)],
        out_shape=[jax.ShapeDtypeStruct((seq, DH_PART), MXU_DTYPE), jax.ShapeDtypeStruct((HEADS, Q_RANK, LANES), F32),
                   jax.ShapeDtypeStruct((HEADS, KV_RANK, LANES), F32), jax.ShapeDtypeStruct((HEADS, KV_RANK, LANES), F32),
                   jax.ShapeDtypeStruct((1, Q_RANK), F32), jax.ShapeDtypeStruct((1, KV_RANK), F32)],
        compiler_params=_params("arbitrary"),
    )(h, g_cq, g_ckv, wq, wk, wv, ctab, stab, dq, dk, dv, *after)


def _causal_mask(t):
    row = lax.broadcasted_iota(jnp.int32, (t, t), 0)
    col = lax.broadcasted_iota(jnp.int32, (t, t), 1)
    return row >= col


def _mla_attn_fwd(q, k, v, t=512):
    _, seq, _ = q.shape
    t = min(t, seq)

    def body(q_ref, k_ref, v_ref, o_ref, ob_ref, lse_ref, m_ref, acc_ref, s_ref):
        i = pl.program_id(1)
        qb = q_ref[...]
        m_ref[...] = jnp.full_like(m_ref, NEG_BIG)
        acc_ref[...] = jnp.zeros_like(acc_ref)

        def scores(j):
            return _dot(qb, k_ref[pl.ds(pl.multiple_of(j * t, t), t), :], 1, 1) * MLA_SCALE_LOG2

        def softmax_pv(j, s, masked):
            vb = v_ref[pl.ds(pl.multiple_of(j * t, t), t), :]
            if masked:
                s = jnp.where(_causal_mask(t), s, NEG_BIG)
            m_old = m_ref[...]
            m_new = jnp.maximum(m_old, jnp.max(s, axis=1, keepdims=True))
            p = jnp.exp2(s - m_new)
            a = jnp.exp2(m_old - m_new)
            acc_ref[...] = a * acc_ref[...] + _dot(p.astype(MXU_DTYPE), vb, 1, 0)
            m_ref[...] = m_new

        s_ref[...] = scores(0)

        def loop_body(j, c):
            s_next = scores(j + 1)
            softmax_pv(j, s_ref[...], False)
            s_ref[...] = s_next
            return c

        lax.fori_loop(0, i, loop_body, 0)
        softmax_pv(i, s_ref[...], True)
        acc = acc_ref[...]
        l = acc[:, ONES_LANE:ONES_LANE + 1]
        o = jnp.where(lax.broadcasted_iota(jnp.int32, acc.shape, 1) < HEAD_DIM, acc * (1.0 / l), 0.0)
        o_ref[...] = o
        ob_ref[...] = o.astype(ob_ref.dtype)
        lse_ref[...] = jnp.broadcast_to(m_ref[...] + jnp.log2(l), lse_ref.shape)

    blk = pl.BlockSpec((None, t, LANES), lambda h, i: (h, i, 0))
    whole = pl.BlockSpec((None, seq, LANES), lambda h, i: (h, 0, 0))
    shp = jax.ShapeDtypeStruct((HEADS, seq, LANES), F32)
    return pl.pallas_call(
        body, name="mla_attn_fwd", grid=(HEADS, seq // t),
        in_specs=[blk, whole, whole], out_specs=[blk, blk, blk],
        out_shape=[shp, jax.ShapeDtypeStruct((HEADS, seq, LANES), MXU_DTYPE), shp],
        scratch_shapes=[pltpu.VMEM((t, 1), F32), pltpu.VMEM((t, LANES), F32), pltpu.VMEM((t, t), F32)],
        compiler_params=_params("parallel", "arbitrary"),
    )(q, k, v)


def _mla_attn_bwd(q, k, v, o, lse, do, t=512):
    _, seq, _ = q.shape
    t = min(t, seq)
    nb = seq // t

    def body(q_ref, k_ref, v_ref, o_ref, lse_ref, do_ref, dq_ref, dk_ref, dv_ref, dl_ref, dka_ref, dva_ref):
        dq_ref[...] = jnp.zeros_like(dq_ref)

        def delta_body(i, c):
            rows = pl.ds(pl.multiple_of(i * t, t), t)
            dl_ref[rows, :] = jnp.sum(do_ref[rows, :] * o_ref[rows, :], axis=1, keepdims=True)
            return c

        lax.fori_loop(0, nb, delta_body, 0)

        def kblock(j, c):
            krows = pl.ds(pl.multiple_of(j * t, t), t)
            kb = k_ref[krows, :]
            vb = v_ref[krows, :]
            dka_ref[...] = jnp.zeros_like(dka_ref)
            dva_ref[...] = jnp.zeros_like(dva_ref)

            def qstep(i, masked):
                th = t // 2
                rows = [pl.ds(pl.multiple_of(i * t + hf * th, th), th) for hf in range(2)]
                qs = [q_ref[r, :] for r in rows]
                dos = [do_ref[r, :].astype(MXU_DTYPE) for r in rows]
                nkeys = [th if masked else t, t]
                ss = [_dot(qs[hf], kb[:nkeys[hf]], 1, 1) * MLA_SCALE_LOG2 for hf in range(2)]
                dps = [_dot(dos[hf], vb[:nkeys[hf]], 1, 1) for hf in range(2)]
                for hf in range(2):
                    s, nk = ss[hf], nkeys[hf]
                    if masked:
                        row = lax.broadcasted_iota(jnp.int32, (th, nk), 0) + hf * th
                        s = jnp.where(row >= lax.broadcasted_iota(jnp.int32, (th, nk), 1), s, NEG_BIG)
                    p = jnp.exp2(s - lse_ref[rows[hf], 0:1])
                    dva_ref[0:nk, :] += _dot(p.astype(MXU_DTYPE), dos[hf], 0, 0)
                    ds = (p * (dps[hf] - dl_ref[rows[hf], :]) * MLA_SCALE).astype(MXU_DTYPE)
                    dka_ref[0:nk, :] += _dot(ds, qs[hf], 0, 0)
                    dq_ref[rows[hf], :] += _dot(ds, kb[:nk], 1, 0)

            qstep(j, True)

            def qloop(i, c2):
                qstep(i, False)
                return c2

            lax.fori_loop(j + 1, nb, qloop, 0)
            dk_ref[krows, :] = dka_ref[...]
            dv_ref[krows, :] = dva_ref[...]
            return c

        lax.fori_loop(0, nb, kblock, 0)

    whole = pl.BlockSpec((None, seq, LANES), lambda h: (h, 0, 0))
    shp = jax.ShapeDtypeStruct((HEADS, seq, LANES), F32)
    return pl.pallas_call(
        body, name="mla_attn_bwd", grid=(HEADS,),
        in_specs=[whole] * 6, out_specs=[whole] * 3, out_shape=[shp] * 3,
        scratch_shapes=[pltpu.VMEM((seq, 1), F32), pltpu.VMEM((t, LANES), F32), pltpu.VMEM((t, LANES), F32)],
        compiler_params=_params("parallel"),
    )(q, k, v, o, lse, do)


DIL_CHUNK = DIL_BLOCK * max(d for _, d in DIL_PAIRS)
DIL_PAIR_LANES = 2 * HEAD_DIM
assert DIL_PAIR_LANES == LANES
DIL_UNROLL_FWD = 16
DIL_UNROLL_BWD = 8


def _dil_bias_tables(hp, dil):
    b = DIL_BLOCK
    iq = lax.broadcasted_iota(jnp.int32, (b, 2 * b), 0)
    ik = lax.broadcasted_iota(jnp.int32, (b, 2 * b), 1)
    off = iq + b - ik
    band = (off >= 0) & (off <= b)
    dist = (off * dil).astype(F32)
    every, first = [], []
    for hh in range(2):
        slope = jnp.where(hp == 0, ALIBI_SLOPES[hh], jnp.where(hp == 1, ALIBI_SLOPES[2 + hh],
                          jnp.where(hp == 2, ALIBI_SLOPES[4 + hh], ALIBI_SLOPES[6 + hh]))).astype(F32)
        bias = -slope * dist
        every.append(jnp.where(band, bias, NEG_BIG))
        first.append(jnp.where(band & (ik >= b), bias, NEG_BIG))
    return jnp.concatenate(every, axis=0), jnp.concatenate(first, axis=0)


def _dil_rows(start, dil):
    return pl.ds(start, DIL_BLOCK) if dil == 1 else pl.ds(start, DIL_BLOCK, stride=dil)


def _dil_block_pos(blk, c, dil):
    sc, r = blk // dil, blk % dil
    q0 = sc * (DIL_BLOCK * dil) + r
    kcur0 = c * DIL_CHUNK + q0
    first = kcur0 < DIL_BLOCK * dil
    kprev0 = jnp.where(first, kcur0, kcur0 - DIL_BLOCK * dil)
    return q0, kcur0, kprev0, first


def _pair_cols(hh):
    return slice(HEAD_DIM * hh, HEAD_DIM * (hh + 1))


def _first_head_lanes(shape):
    return lax.broadcasted_iota(jnp.int32, shape, 1) < HEAD_DIM


def _stack_pair(t):
    first = _first_head_lanes(t.shape)
    return jnp.concatenate([jnp.where(first, t, 0.0), jnp.where(first, 0.0, t)], axis=0).astype(MXU_DTYPE)


def _unstack_pair(t):
    rows = t.shape[0] // 2
    return jnp.where(_first_head_lanes((rows, t.shape[1])), t[:rows], t[rows:])


def _pair_column(t):
    return jnp.concatenate([t[:, 0:1], t[:, HEAD_DIM:HEAD_DIM + 1]], axis=0)


def _dil_fwd(h):
    seq = h.shape[0]
    assert seq % DIL_CHUNK == 0
    nblk = DIL_CHUNK // DIL_BLOCK
    rc = 256

    def body(q_ref, k_ref, v_ref, o_ref, ob_ref, lse_ref, *scr):
        o_scr, l_scr = scr[:3], scr[3:]
        hp, c = pl.program_id(0), pl.program_id(1)
        for bi, (_, dil) in enumerate(DIL_PAIRS):
            tables = _dil_bias_tables(hp, dil)

            def block(blk, carry, bi=bi, dil=dil, tables=tables):
                q0, kcur0, kprev0, first = _dil_block_pos(blk, c, dil)
                q2 = _stack_pair(q_ref[_dil_rows(q0, dil), :] * DIL_SCALE)
                kcat = jnp.concatenate([k_ref[_dil_rows(kprev0, dil), :], k_ref[_dil_rows(kcur0, dil), :]], axis=0).astype(MXU_DTYPE)
                vcat = jnp.concatenate([v_ref[_dil_rows(kprev0, dil), :], v_ref[_dil_rows(kcur0, dil), :]], axis=0).astype(MXU_DTYPE)
                s = _dot(q2, kcat, 1, 1) + jnp.where(first, tables[1], tables[0])
                mx = jnp.max(s, axis=1, keepdims=True)
                p = jnp.exp(s - mx)
                l = jnp.sum(p, axis=1, keepdims=True)
                o_scr[bi][_dil_rows(q0, dil), :] = _unstack_pair(_dot(p.astype(MXU_DTYPE), vcat, 1, 0) * (1.0 / l))
                l_scr[bi][_dil_rows(q0, dil), :] = _unstack_pair(jnp.broadcast_to(mx + jnp.log(l), (2 * DIL_BLOCK, LANES)))
                return carry

            lax.fori_loop(0, nblk, block, 0, unroll=DIL_UNROLL_FWD)

        def combine(i, carry):
            rows = pl.ds(pl.multiple_of(i * rc, rc), rc)
            ls = [l_scr[bi][rows, :] for bi in range(3)]
            mx = jnp.maximum(jnp.maximum(ls[0], ls[1]), ls[2])
            es = [jnp.exp(l - mx) for l in ls]
            den = es[0] + es[1] + es[2]
            o = (es[0] * o_scr[0][rows, :] + es[1] * o_scr[1][rows, :] + es[2] * o_scr[2][rows, :]) / den
            o_ref[rows, :] = o
            ob_ref[rows, :] = o.astype(ob_ref.dtype)
            lse_ref[rows, :] = mx + jnp.log(den)
            return carry

        lax.fori_loop(0, DIL_CHUNK // rc, combine, 0)

    nq = DIL_WIDTH // LANES
    chunk = lambda off: pl.BlockSpec((DIL_CHUNK, LANES), lambda hp, c: (c, off + hp))
    whole = lambda off: pl.BlockSpec((seq, LANES), lambda hp, c: (0, off + hp))
    shp = jax.ShapeDtypeStruct((seq, DIL_WIDTH), F32)
    return pl.pallas_call(
        body, name="dil_fwd", grid=(nq, seq // DIL_CHUNK),
        in_specs=[chunk(nq), whole(2 * nq), whole(3 * nq)], out_specs=[chunk(0), chunk(0), chunk(0)],
        out_shape=[shp, jax.ShapeDtypeStruct((seq, DIL_WIDTH), MXU_DTYPE), shp],
        scratch_shapes=[pltpu.VMEM((DIL_CHUNK, LANES), F32)] * 6,
        compiler_params=_params("parallel", "arbitrary"),
    )(h, h, h)


def _dil_bwd(h, o, lse, do, after=()):
    seq = h.shape[0]
    nblk = DIL_CHUNK // DIL_BLOCK
    nchunk = seq // DIL_CHUNK
    rc = 256

    n_after = len(after)

    def body(q_ref, k_ref, v_ref, o_ref, lse_ref, do_ref, *rest):
        dq_out, dk_out, dv_out, dl_scr, dq_ref, dk_ref, dv_ref = rest[n_after:]
        hp, c = pl.program_id(0), pl.program_id(1)

        @pl.when(c == 0)
        def _():
            dk_ref[...] = jnp.zeros_like(dk_ref)
            dv_ref[...] = jnp.zeros_like(dv_ref)

        def delta(i, carry):
            rows = pl.ds(pl.multiple_of(i * rc, rc), rc)
            prod = do_ref[rows, :] * o_ref[rows, :]
            dl_scr[rows, :] = jnp.concatenate(
                [jnp.broadcast_to(jnp.sum(prod[:, _pair_cols(hh)], axis=1, keepdims=True), (rc, HEAD_DIM)) for hh in range(2)], axis=1)
            return carry

        lax.fori_loop(0, DIL_CHUNK // rc, delta, 0)

        for bi, (_, dil) in enumerate(DIL_PAIRS):
            tables = _dil_bias_tables(hp, dil)

            def block(blk, carry, bi=bi, dil=dil, tables=tables):
                q0, kcur0, kprev0, first = _dil_block_pos(blk, c, dil)
                qrows = _dil_rows(q0, dil)
                q2 = _stack_pair(q_ref[qrows, :] * DIL_SCALE)
                kcat = jnp.concatenate([k_ref[_dil_rows(kprev0, dil), :], k_ref[_dil_rows(kcur0, dil), :]], axis=0).astype(MXU_DTYPE)
                vcat = jnp.concatenate([v_ref[_dil_rows(kprev0, dil), :], v_ref[_dil_rows(kcur0, dil), :]], axis=0).astype(MXU_DTYPE)
                do2 = _stack_pair(do_ref[qrows, :])
                s = _dot(q2, kcat, 1, 1) + jnp.where(first, tables[1], tables[0])
                p = jnp.exp(s - _pair_column(lse_ref[qrows, :]))
                dp = _dot(do2, vcat, 1, 1)
                ds = (p * (dp - _pair_column(dl_scr[qrows, :]))).astype(MXU_DTYPE)
                dq_b = _unstack_pair(_dot(ds, kcat, 1, 0)) * DIL_SCALE
                dk_b = _dot(ds, q2, 0, 0)
                dv_b = _dot(p.astype(MXU_DTYPE), do2, 0, 0)
                if bi == 0:
                    dq_ref[qrows, :] = dq_b
                else:
                    dq_ref[qrows, :] += dq_b
                dk_ref[_dil_rows(kprev0, dil), :] += dk_b[:DIL_BLOCK]
                dv_ref[_dil_rows(kprev0, dil), :] += dv_b[:DIL_BLOCK]
                dk_ref[_dil_rows(kcur0, dil), :] += dk_b[DIL_BLOCK:]
                dv_ref[_dil_rows(kcur0, dil), :] += dv_b[DIL_BLOCK:]
                return carry

            lax.fori_loop(0, nblk, block, 0, unroll=DIL_UNROLL_BWD)

        dq_out[...] = dq_ref[...].astype(dq_out.dtype)

        @pl.when(c == nchunk - 1)
        def _():
            dk_out[...] = dk_ref[...].astype(dk_out.dtype)
            dv_out[...] = dv_ref[...].astype(dv_out.dtype)

    nq = DIL_WIDTH // LANES
    chunk = lambda off: pl.BlockSpec((DIL_CHUNK, LANES), lambda hp, c: (c, off + hp))
    whole = lambda off: pl.BlockSpec((seq, LANES), lambda hp, c: (0, off + hp))
    shp = jax.ShapeDtypeStruct((seq, DIL_WIDTH), MXU_DTYPE)
    return pl.pallas_call(
        body, name="dil_bwd", grid=(nq, nchunk),
        in_specs=[chunk(nq), whole(2 * nq), whole(3 * nq), chunk(0), chunk(0), chunk(0)] + [_ANY_SPEC] * n_after,
        out_specs=[chunk(0), whole(0), whole(0)], out_shape=[shp, shp, shp],
        scratch_shapes=[pltpu.VMEM((DIL_CHUNK, LANES), F32), pltpu.VMEM((DIL_CHUNK, LANES), F32),
                        pltpu.VMEM((seq, LANES), F32), pltpu.VMEM((seq, LANES), F32)],
        compiler_params=_params("parallel", "arbitrary"),
    )(h, h, h, o, lse, do, *after)


def _mm_dx0(parts, w_in_t, res, tm=1024, after=()):
    seq, d = res.shape
    tm = min(tm, seq)
    n_after = len(after)

    def body(a0, a1, a2, a3, b_ref, r_ref, *rest):
        o_ref = rest[n_after]
        acc = _dot(a0[...], b_ref[0:DH_PART, :], 1, 0)
        for c, a in enumerate((a1, a2, a3), start=1):
            acc += _dot(a[...], b_ref[DH_PART * c:DH_PART * (c + 1), :], 1, 0)
        o_ref[...] = acc + DN_ALPHA * r_ref[...]

    blk = pl.BlockSpec((tm, DH_PART), lambda i: (i, 0))
    row = pl.BlockSpec((tm, d), lambda i: (i, 0))
    return pl.pallas_call(
        body, name="mm_dx0", grid=(seq // tm,),
        in_specs=[blk] * 4 + [pl.BlockSpec((IN_PAD, d), lambda i: (0, 0), pipeline_mode=pl.Buffered(1)), row] + [_ANY_SPEC] * n_after,
        out_specs=row, out_shape=jax.ShapeDtypeStruct((seq, d), F32), compiler_params=_params("parallel"),
    )(*parts, w_in_t, res, *after)


def _mm_dw_in(parts, x0, tk=1024):
    seq, d = x0.shape
    tk = min(tk, seq)
    nk = seq // tk

    def body(a0, a1, a2, a3, b_ref, o_ref, acc_ref):
        kk = pl.program_id(0)

        @pl.when(kk == 0)
        def _():
            acc_ref[...] = jnp.zeros_like(acc_ref)

        b = b_ref[...].astype(MXU_DTYPE)
        for c, a in enumerate((a0, a1, a2, a3)):
            acc_ref[DH_PART * c:DH_PART * (c + 1), :] += _dot(a[...], b, 0, 0)

        @pl.when(kk == nk - 1)
        def _():
            o_ref[...] = acc_ref[...].astype(o_ref.dtype)

    blk = pl.BlockSpec((tk, DH_PART), lambda kk: (kk, 0))
    return pl.pallas_call(
        body, name="mm_dw_in", grid=(nk,), in_specs=[blk] * 4 + [pl.BlockSpec((tk, d), lambda kk: (kk, 0))],
        out_specs=pl.BlockSpec((IN_PAD, d), lambda kk: (0, 0)), out_shape=jax.ShapeDtypeStruct((IN_PAD, d), MXU_DTYPE),
        scratch_shapes=[pltpu.VMEM((IN_PAD, d), F32)], compiler_params=_params("arbitrary"),
    )(*parts, x0)


def _ln_stats(z):
    mu = jnp.mean(z, axis=-1, keepdims=True)
    zc = z - mu
    r = lax.rsqrt(jnp.mean(zc * zc, axis=-1, keepdims=True) + LN_EPS)
    return zc * r, r


def _ln_bwd_math(dy, xh, r, g):
    dxh = dy * g
    return r * (dxh - jnp.mean(dxh, axis=-1, keepdims=True) - xh * jnp.mean(dxh * xh, axis=-1, keepdims=True))


def _mix_ln1(o_mla, o_dil, w_o_mla, w_o_dil, x0, g, b, tm=512):
    seq, d = x0.shape
    tm = min(tm, seq)

    def body(om_ref, od_ref, wm_ref, wd_ref, x_ref, g_ref, b_ref, z_ref, y_ref, yb_ref):
        mix = _dot(od_ref[...], wd_ref[...], 1, 0)
        for hd in range(HEADS):
            mix += _dot(om_ref[hd], wm_ref[LANES * hd:LANES * (hd + 1), :], 1, 0)
        z = DN_ALPHA * x_ref[...] + mix
        xh, _ = _ln_stats(z)
        y = xh * g_ref[...] + b_ref[...]
        z_ref[...] = z
        y_ref[...] = y
        yb_ref[...] = y.astype(yb_ref.dtype)

    blk = pl.BlockSpec((tm, d), lambda i: (i, 0))
    vec = pl.BlockSpec((1, d), lambda i: (0, 0))
    shp = jax.ShapeDtypeStruct((seq, d), F32)
    return pl.pallas_call(
        body, name="mix_ln1", grid=(seq // tm,),
        in_specs=[pl.BlockSpec((HEADS, tm, LANES), lambda i: (0, i, 0)), pl.BlockSpec((tm, DIL_WIDTH), lambda i: (i, 0)),
                  pl.BlockSpec((HEADS * LANES, d), lambda i: (0, 0)), pl.BlockSpec((DIL_WIDTH, d), lambda i: (0, 0)), blk, vec, vec],
        out_specs=[blk, blk, blk], out_shape=[shp, shp, jax.ShapeDtypeStruct((seq, d), MXU_DTYPE)],
        compiler_params=_params("parallel"))(o_mla, o_dil, w_o_mla, w_o_dil, x0, g, b)


def _dx1_ln1_bwd(du, w_up_t, dz2, z, g, tm=256, after=()):
    seq, d = z.shape
    kdim = du.shape[1]
    tm = min(tm, seq)
    n_after = len(after)

    def body(du_ref, w_ref, r_ref, z_ref, g_ref, *rest):
        dz_ref, dzb_ref, dg_ref, db_ref = rest[n_after:]

        @pl.when(pl.program_id(0) == 0)
        def _():
            dg_ref[...] = jnp.zeros_like(dg_ref)
            db_ref[...] = jnp.zeros_like(db_ref)

        dyb = _dot(du_ref[...], w_ref[...], 1, 0) + DN_ALPHA * r_ref[...]
        xh, r = _ln_stats(z_ref[...])
        dg_ref[...] += jnp.sum(dyb * xh, axis=0, keepdims=True)
        db_ref[...] += jnp.sum(dyb, axis=0, keepdims=True)
        dz = _ln_bwd_math(dyb, xh, r, g_ref[...])
        dz_ref[...] = dz
        dzb_ref[...] = dz.astype(dzb_ref.dtype)

    blk = pl.BlockSpec((tm, d), lambda i: (i, 0))
    vec = pl.BlockSpec((1, d), lambda i: (0, 0))
    return pl.pallas_call(
        body, name="dx1_ln1_bwd", grid=(seq // tm,),
        in_specs=[pl.BlockSpec((tm, kdim), lambda i: (i, 0)),
                  pl.BlockSpec((kdim, d), lambda i: (0, 0), pipeline_mode=pl.Buffered(1)), blk, blk, vec] + [_ANY_SPEC] * n_after,
        out_specs=[blk, blk, vec, vec],
        out_shape=[jax.ShapeDtypeStruct((seq, d), F32), jax.ShapeDtypeStruct((seq, d), MXU_DTYPE),
                   jax.ShapeDtypeStruct((1, d), F32), jax.ShapeDtypeStruct((1, d), F32)],
        compiler_params=_params("arbitrary"))(du, w_up_t, dz2, z, g, *after)


def _down_ln2_loss_bwd(act, w_down, x1, target, g, b, tm=512):
    seq, d = x1.shape
    kdim = act.shape[1]
    tm = min(tm, seq)

    def body(a_ref, w_ref, x_ref, t_ref, g_ref, b_ref, dz_ref, dzb_ref, loss_ref, dg_ref, db_ref):
        @pl.when(pl.program_id(0) == 0)
        def _():
            loss_ref[...] = jnp.zeros_like(loss_ref)
            dg_ref[...] = jnp.zeros_like(dg_ref)
            db_ref[...] = jnp.zeros_like(db_ref)

        gv = g_ref[...]
        z = DN_ALPHA * x_ref[...] + _dot(a_ref[...], w_ref[...], 1, 0)
        xh, r = _ln_stats(z)
        err = (xh * gv + b_ref[...]) - t_ref[...]
        loss_ref[...] += 0.5 * jnp.sum(jnp.mean(err * err, axis=-1, keepdims=True), axis=0, keepdims=True)
        dy = err * (1.0 / d)
        dg_ref[...] += jnp.sum(dy * xh, axis=0, keepdims=True)
        db_ref[...] += jnp.sum(dy, axis=0, keepdims=True)
        dz = _ln_bwd_math(dy, xh, r, gv)
        dz_ref[...] = dz
        dzb_ref[...] = dz.astype(dzb_ref.dtype)

    blk = pl.BlockSpec((tm, d), lambda i: (i, 0))
    vec = pl.BlockSpec((1, d), lambda i: (0, 0))
    return pl.pallas_call(
        body, name="down_ln2_loss_bwd", grid=(seq // tm,),
        in_specs=[pl.BlockSpec((tm, kdim), lambda i: (i, 0)),
                  pl.BlockSpec((kdim, d), lambda i: (0, 0), pipeline_mode=pl.Buffered(1)), blk, blk, vec, vec],
        out_specs=[blk, blk, pl.BlockSpec((1, LANES), lambda i: (0, 0)), vec, vec],
        out_shape=[jax.ShapeDtypeStruct((seq, d), F32), jax.ShapeDtypeStruct((seq, d), MXU_DTYPE),
                   jax.ShapeDtypeStruct((1, LANES), F32),
                   jax.ShapeDtypeStruct((1, d), F32), jax.ShapeDtypeStruct((1, d), F32)],
        compiler_params=_params("arbitrary"))(act, w_down, x1, target, g, b)


HALO = 16


def _conv_rows(e, w_ref, b_ref):
    y = b_ref[...] + w_ref[0:1, :] * pltpu.roll(e, 2, 0)
    y = y + w_ref[1:2, :] * pltpu.roll(e, 1, 0)
    return y + w_ref[2:3, :] * e


_GELU_C = math.sqrt(2.0 / math.pi)
_GELU_A = 0.044715


def _gelu(x):
    return 0.5 * x * (1.0 + jnp.tanh(_GELU_C * (x + _GELU_A * (x * x * x))))


CONV_TN = 256


def _ffn_interleave(a, axis):
    shp = a.shape
    a = a.reshape(shp[:axis] + (2, D_FF // CONV_TN, CONV_TN) + shp[axis + 1:])
    return jnp.swapaxes(a, axis, axis + 1).reshape(shp)


def _ffn_deinterleave(a, axis):
    shp = a.shape
    a = a.reshape(shp[:axis] + (D_FF // CONV_TN, 2, CONV_TN) + shp[axis + 1:])
    return jnp.swapaxes(a, axis, axis + 1).reshape(shp)


def _conv_gate_fwd(u, conv_w, conv_b, tm=1024):
    seq = u.shape[0]
    tm = min(tm, seq)
    tn = CONV_TN

    def body(u_ref, up_ref, w_ref, b_ref, o_ref):
        first = pl.program_id(0) == 0
        e = jnp.concatenate([jnp.where(first, 0.0, up_ref[...]), u_ref[...]], axis=0)
        y = _conv_rows(e, w_ref, b_ref)[HALO:]
        o_ref[...] = (_gelu(y[:, tn:]) * y[:, :tn]).astype(o_ref.dtype)

    hb = tm // HALO
    return pl.pallas_call(
        body, name="conv_gate_fwd", grid=(seq // tm, D_FF // tn),
        in_specs=[pl.BlockSpec((tm, 2 * tn), lambda i, j: (i, j)),
                  pl.BlockSpec((HALO, 2 * tn), lambda i, j: (jnp.maximum(i * hb - 1, 0), j)),
                  pl.BlockSpec((3, 2 * tn), lambda i, j: (0, j)), pl.BlockSpec((1, 2 * tn), lambda i, j: (0, j))],
        out_specs=pl.BlockSpec((tm, tn), lambda i, j: (i, j)), out_shape=jax.ShapeDtypeStruct((seq, D_FF), MXU_DTYPE),
        compiler_params=_params("parallel", "parallel"),
    )(u, u, conv_w, conv_b)


def _conv_gate_bwd(u, d_act, conv_w, conv_b, tm=1024):
    seq = u.shape[0]
    tm = min(tm, seq)
    tn = CONV_TN
    ni = seq // tm
    rows_e = tm + 2 * HALO

    def body(u_ref, up_ref, un_ref, da_ref, dan_ref, w_ref, b_ref, du_ref, dw_ref, db_ref):
        i = pl.program_id(1)
        first, last = i == 0, i == ni - 1

        @pl.when(i == 0)
        def _():
            dw_ref[...] = jnp.zeros_like(dw_ref)
            db_ref[...] = jnp.zeros_like(db_ref)

        e = jnp.concatenate([jnp.where(first, 0.0, up_ref[...]), u_ref[...], jnp.where(last, 0.0, un_ref[...])], axis=0)
        y = _conv_rows(e, w_ref, b_ref)
        ya, yg = y[:, :tn], y[:, tn:]
        dact = jnp.concatenate([jnp.zeros((HALO, tn), F32), da_ref[...].astype(F32),
                                jnp.where(last, 0.0, dan_ref[...].astype(F32))], axis=0)
        th = jnp.tanh(_GELU_C * (yg + _GELU_A * (yg * yg * yg)))
        gelu = 0.5 * yg * (1.0 + th)
        gelu_grad = 0.5 * (1.0 + th) + 0.5 * yg * (1.0 - th * th) * (_GELU_C * (1.0 + 3.0 * _GELU_A * (yg * yg)))
        dy = jnp.concatenate([dact * gelu, dact * ya * gelu_grad], axis=1)
        du = w_ref[2:3, :] * dy + w_ref[1:2, :] * pltpu.roll(dy, rows_e - 1, 0) + w_ref[0:1, :] * pltpu.roll(dy, rows_e - 2, 0)
        du_ref[...] = du[HALO:HALO + tm].astype(du_ref.dtype)
        dyt = dy[HALO:HALO + tm]
        dw_ref[0:1, :] += jnp.sum(dyt * pltpu.roll(e, 2, 0)[HALO:HALO + tm], axis=0, keepdims=True)
        dw_ref[1:2, :] += jnp.sum(dyt * pltpu.roll(e, 1, 0)[HALO:HALO + tm], axis=0, keepdims=True)
        dw_ref[2:3, :] += jnp.sum(dyt * e[HALO:HALO + tm], axis=0, keepdims=True)
        db_ref[...] += jnp.sum(dyt, axis=0, keepdims=True)

    hb = tm // HALO
    nh = seq // HALO
    prev = lambda j, i: (jnp.maximum(i * hb - 1, 0), j)
    nxt = lambda j, i: (jnp.minimum((i + 1) * hb, nh - 1), j)
    return pl.pallas_call(
        body, name="conv_gate_bwd", grid=(D_FF // tn, ni),
        in_specs=[pl.BlockSpec((tm, 2 * tn), lambda j, i: (i, j)), pl.BlockSpec((HALO, 2 * tn), prev),
                  pl.BlockSpec((HALO, 2 * tn), nxt), pl.BlockSpec((tm, tn), lambda j, i: (i, j)), pl.BlockSpec((HALO, tn), nxt),
                  pl.BlockSpec((3, 2 * tn), lambda j, i: (0, j)), pl.BlockSpec((1, 2 * tn), lambda j, i: (0, j))],
        out_specs=[pl.BlockSpec((tm, 2 * tn), lambda j, i: (i, j)), pl.BlockSpec((3, 2 * tn), lambda j, i: (0, j)),
                   pl.BlockSpec((1, 2 * tn), lambda j, i: (0, j))],
        out_shape=[jax.ShapeDtypeStruct((seq, 2 * D_FF), MXU_DTYPE), jax.ShapeDtypeStruct((3, 2 * D_FF), F32),
                   jax.ShapeDtypeStruct((1, 2 * D_FF), F32)],
        compiler_params=_params("parallel", "arbitrary"),
    )(u, u, u, d_act, d_act, conv_w, conv_b)


def _pad_heads(w, width):
    w = jnp.transpose(w, (1, 0, 2))
    return jnp.pad(w, ((0, 0), (0, 0), (0, LANES - width))).astype(MXU_DTYPE)


def _heads_major(a):
    return jnp.transpose(a, (1, 0, 2)).reshape(-1, a.shape[2])


def _heads_minor(a, heads):
    return jnp.transpose(a.reshape(heads, -1, a.shape[1]), (1, 0, 2))


_LATENT = Q_RANK + KV_RANK
_ROPE_AT = _LATENT + NOPE_DIM
_ROPE_END = _ROPE_AT + ROPE_DIM


def _split_pad_rows(w_t):
    z = lambda n: jnp.zeros((n, w_t.shape[1]), w_t.dtype)
    return jnp.concatenate([w_t[:_LATENT], z(_ROPE_AT - _LATENT), w_t[_LATENT:_LATENT + ROPE_DIM], z(DH_PART - _ROPE_END),
                            w_t[_LATENT + ROPE_DIM:]], axis=0)


def _split_unpad_rows(w_p):
    return jnp.concatenate([w_p[:_LATENT], w_p[_ROPE_AT:_ROPE_END], w_p[DH_PART:]], axis=0)


def _pad_w_o(w_o):
    mla = jnp.pad(w_o[:MLA_WIDTH].reshape(HEADS, HEAD_DIM, D_MODEL), ((0, 0), (0, LANES - HEAD_DIM), (0, 0)))
    return mla.reshape(HEADS * LANES, D_MODEL).astype(MXU_DTYPE), w_o[MLA_WIDTH:].astype(MXU_DTYPE)


def _unpad_w_o(d_mla, d_dil):
    return jnp.concatenate([d_mla.reshape(HEADS, LANES, D_MODEL)[:, :HEAD_DIM].reshape(MLA_WIDTH, D_MODEL), d_dil], axis=0)


def _row(v):
    return v.reshape(1, -1).astype(F32)


def _layer_grads(x0, target, cw, first_after=(), late_weights=None, on_grads=None):
    seq = x0.shape[0]
    ctab, stab = _rope_tables(seq)
    gq, gk = cw["g_cq"], cw["g_ckv"]
    wq, wk, wv = cw["wq"], cw["wk"], cw["wv"]
    notify = (lambda stage, grads: ()) if on_grads is None else on_grads

    h = _mm(x0, cw["w_in_t"], name="mm_h", tb=True, tm=1024, tn=IN_PAD, tk=1024, after=first_after)
    qf, kf, vp = _mla_prep(h, gq, gk, wq, wk, wv, ctab, stab)
    o_mla, o_mla_b, lse_mla = _mla_attn_fwd(qf, kf, vp)
    o_dil, o_dil_b, lse_dil = _dil_fwd(h)
    if late_weights is not None:
        cw = {**cw, **late_weights(o_mla_b)}
    cb = cw["conv_b"]
    z1, x1, x1b = _mix_ln1(o_mla_b, o_dil_b, cw["w_o_mla"], cw["w_o_dil"], x0, cw["ln1_g"], cw["ln1_b"])
    u = _mm(x1b, cw["w_up_t"], name="mm_up", tb=True, tm=512, tn=2 * D_FF, tk=1024)
    act = _conv_gate_fwd(u, cw["conv_w"], cb)
    dz2, dz2b, loss, d_ln2_g, d_ln2_b = _down_ln2_loss_bwd(act, cw["w_down"], x1, target, cw["ln2_g"], cw["ln2_b"])

    d_act = _mm(dz2b, cw["w_down"], name="mm_d_act", tb=True, out_dtype=MXU_DTYPE, tm=1024, tn=D_FF, tk=1024)
    d_w_down = _mm(act, dz2b, name="mm_dw_down", ta=True, out_dtype=MXU_DTYPE, tm=1408, tn=1024, tk=1024)
    du, d_conv_w, d_conv_b = _conv_gate_bwd(u, d_act, cw["conv_w"], cb)
    d_w_up_t = _mm(du, x1b, name="mm_dw_up", ta=True, out_dtype=MXU_DTYPE, tm=1408, tn=1024, tk=2048)
    grads = dict(w_up_t=d_w_up_t, w_down=d_w_down, conv_w=d_conv_w, conv_b=d_conv_b, ln2_g=d_ln2_g, ln2_b=d_ln2_b)
    dz1, dz1b, d_ln1_g, d_ln1_b = _dx1_ln1_bwd(du, cw["w_up_t"], dz2, z1, cw["ln1_g"], after=notify("ffn", grads))
    do_mla, do_dil, d_w_o_mla, d_w_o_dil = _w_o_bwd(dz1b, o_mla_b, o_dil_b, cw["w_o_mla"], cw["w_o_dil"])
    grads.update(w_o_mla=d_w_o_mla, w_o_dil=d_w_o_dil, ln1_g=d_ln1_g, ln1_b=d_ln1_b)
    dqf, dkf, dvf = _mla_attn_bwd(qf, kf, vp, o_mla, lse_mla, do_mla)
    dh_mla, d_wq, d_wk, d_wv, d_gq, d_gk = _mla_prep_bwd(h, gq, gk, wq, wk, wv, ctab, stab, dqf, dkf, dvf,
                                                          after=notify("w_o", grads))
    grads.update(wq=d_wq, wk=d_wk, wv=d_wv, g_cq=d_gq, g_ckv=d_gk, loss=loss)
    dq_dil, dk_dil, dv_dil = _dil_bwd(h, o_dil, lse_dil, do_dil, after=notify("mla", grads))
    dh = (dh_mla, dq_dil, dk_dil, dv_dil)
    grads.update(w_in_t=_mm_dw_in(dh, x0))
    grad_x = _mm_dx0(dh, cw["w_in_t"], dz1, after=notify("w_in", grads))
    return loss, grad_x, grads


def _all_gather(blocks, name):
    na = len(blocks)

    def body(*refs):
        ins, outs = refs[:na], refs[na:2 * na]
        send_sems, recv_sems, local_sems = refs[2 * na:]
        x, y, c = lax.axis_index("x"), lax.axis_index("y"), lax.axis_index("c")
        me, sibling = (x, y, c), (x, y, 1 - c)
        chips = [(1 - x, y), (x, 1 - y), (1 - x, 1 - y)]

        def slot(out, pos):
            return out.at[4 * pos[0] + 2 * pos[1] + pos[2]]

        def copy(a, k, block, to, src=None):
            return pltpu.make_async_remote_copy(
                src_ref=slot(outs[a], block) if src is None else src, dst_ref=slot(outs[a], block),
                send_sem=send_sems.at[7 * a + k], recv_sem=recv_sems.at[7 * a + k],
                device_id=to, device_id_type=pl.DeviceIdType.MESH)

        mine = [pltpu.make_async_copy(ins[a], slot(outs[a], me), local_sems.at[a]) for a in range(na)]
        for cp in mine:
            cp.start()
        first = []
        for a in range(na):
            first.append(copy(a, 0, me, sibling, src=ins[a]))
            first += [copy(a, 1 + j, me, (*chip, c), src=ins[a]) for j, chip in enumerate(chips)]
        for cp in first:
            cp.start()
        passed = []
        for j, chip in enumerate(chips):
            for a in range(na):
                copy(a, 1 + j, (*chip, c), me).wait_recv()
                cp = copy(a, 4 + j, (*chip, c), sibling)
                cp.start()
                passed.append(cp)
        for a in range(na):
            copy(a, 0, sibling, me).wait_recv()
            for j, chip in enumerate(chips):
                copy(a, 4 + j, (*chip, 1 - c), me).wait_recv()
        for cp in first + passed:
            cp.wait_send()
        for cp in mine:
            cp.wait()

    any_spec = pl.BlockSpec(memory_space=pl.ANY)
    return pl.pallas_call(
        body, name=name, in_specs=[any_spec] * na, out_specs=[any_spec] * na,
        out_shape=[jax.ShapeDtypeStruct((N_DEV,) + b.shape, b.dtype) for b in blocks],
        scratch_shapes=[pltpu.SemaphoreType.DMA((7 * na,)), pltpu.SemaphoreType.DMA((7 * na,)), pltpu.SemaphoreType.DMA((na,))],
    )(*blocks)


_HBM_SPEC = pl.BlockSpec(memory_space=pltpu.HBM)
_SEM_SPEC = pl.BlockSpec(memory_space=pltpu.SEMAPHORE)
_DATAFLOW = pltpu.CompilerParams(has_side_effects=pltpu.SideEffectType.DATAFLOW_SIDE_EFFECTING)


def _split_copies(ins, lands, send_sems, recv_sems, gather):
    x, y, c = lax.axis_index("x"), lax.axis_index("y"), lax.axis_index("c")
    me = 4 * x + 2 * y + c
    copies = []
    for a in range(len(ins)):
        for d in range(1, N_DEV):
            px, py, pc = x ^ (d >> 2), y ^ ((d >> 1) & 1), c ^ (d & 1)
            copies.append(pltpu.make_async_remote_copy(
                src_ref=ins[a] if gather else ins[a].at[4 * px + 2 * py + pc], dst_ref=lands[a].at[me],
                send_sem=send_sems.at[7 * a + d - 1], recv_sem=recv_sems.at[7 * a + d - 1],
                device_id=(px, py, pc), device_id_type=pl.DeviceIdType.MESH))
    return copies


def _send_start(srcs, gather, name):
    na = len(srcs)
    land_types = [pltpu.HBM(((N_DEV,) + s.shape) if gather else s.shape, s.dtype) for s in srcs]

    def body(*refs):
        ins, lands = refs[:na], refs[na:2 * na]
        send_sems, recv_sems, token = refs[2 * na], refs[2 * na + 1], refs[-1]
        for cp in _split_copies(ins, lands, send_sems, recv_sems, gather):
            cp.start()
        token[...] = jnp.zeros_like(token)

    hbm = lambda a: pltpu.with_memory_space_constraint(a, pltpu.HBM)
    outs = pl.pallas_call(
        body, name=name,
        out_shape=(pltpu.SemaphoreType.DMA((7 * na,)), pltpu.SemaphoreType.DMA((7 * na,)),
                   *[pltpu.HBM(s.shape, s.dtype) for s in srcs], *land_types, jax.ShapeDtypeStruct((8, LANES), F32)),
        in_specs=[_HBM_SPEC] * (2 * na),
        out_specs=(_SEM_SPEC, _SEM_SPEC, *[_HBM_SPEC] * (2 * na), pl.BlockSpec(memory_space=pltpu.VMEM)),
        input_output_aliases={i: 2 + i for i in range(2 * na)}, compiler_params=_DATAFLOW,
    )(*[hbm(s) for s in srcs], *[hbm(lax.empty(t.shape, t.dtype)) for t in land_types])
    return dict(send=outs[0], recv=outs[1], srcs=list(outs[2:2 + na]), lands=list(outs[2 + na:2 + 2 * na]), token=outs[-1],
                gather=gather)


def _send_wait(handle, after, name):
    na = len(handle["srcs"])
    gather = handle["gather"]
    after = list(after)

    def body(*refs):
        ins, lands = refs[:na], refs[na:2 * na]
        send_sems, recv_sems = refs[2 * na], refs[2 * na + 1]
        for cp in _split_copies(ins, lands, send_sems, recv_sems, gather):
            cp.wait_send()
            cp.wait_recv()

    both = handle["srcs"] + handle["lands"]
    outs = pl.pallas_call(
        body, name=name, out_shape=[pltpu.HBM(a.shape, a.dtype) for a in both],
        in_specs=[_HBM_SPEC] * (2 * na) + [_SEM_SPEC, _SEM_SPEC] + [_ANY_SPEC] * len(after),
        out_specs=[_HBM_SPEC] * (2 * na), input_output_aliases={i: i for i in range(2 * na)}, compiler_params=_DATAFLOW,
    )(*both, handle["send"], handle["recv"], *after)
    return list(outs[:na]), list(outs[na:])


def _sum_slots(p_ref):
    g = p_ref[0].astype(F32)
    for s in range(1, p_ref.shape[0]):
        g = g + p_ref[s].astype(F32)
    return g


def _adamw_refs(g, w_ref, m_ref, v_ref, g_out, d_out, m_out, v_out):
    c1 = 1.0 - ADAM_B1 ** ADAM_STEP
    c2 = 1.0 - ADAM_B2 ** ADAM_STEP
    m_new = ADAM_B1 * m_ref[...] + (1.0 - ADAM_B1) * g
    v_new = ADAM_B2 * v_ref[...] + (1.0 - ADAM_B2) * (g * g)
    g_out[...] = g
    m_out[...] = m_new
    v_out[...] = v_new
    d_out[...] = -ADAM_LR * ((m_new / c1) / (jnp.sqrt(v_new / c2) + ADAM_EPS) + ADAM_WD * w_ref[...])


def _adamw(parts, w, m, v, name):
    npart, r, n = parts.shape
    tr = r if r <= 256 else max(t for t in range(16, 257, 16) if r % t == 0)

    def body(p_ref, w_ref, m_ref, v_ref, g_out, d_out, m_out, v_out):
        _adamw_refs(_sum_slots(p_ref), w_ref, m_ref, v_ref, g_out, d_out, m_out, v_out)

    blk = pl.BlockSpec((tr, n), lambda i: (i, 0))
    shp = jax.ShapeDtypeStruct((r, n), F32)
    return pl.pallas_call(
        body, name=name, grid=(r // tr,), in_specs=[pl.BlockSpec((npart, tr, n), lambda i: (0, i, 0)), blk, blk, blk],
        out_specs=[blk] * 4, out_shape=[shp] * 4, compiler_params=_params("parallel"),
    )(parts, w, m, v)


def _adamw_small(parts, ws, ms, vs, loss_parts, name):
    n = len(parts)

    def body(*refs):
        ins, outs = refs[:4 * n + 1], refs[4 * n + 1:]
        for i in range(n):
            _adamw_refs(_sum_slots(ins[i]), ins[n + i], ins[2 * n + i], ins[3 * n + i], *outs[4 * i:4 * i + 4])
        outs[4 * n][...] = _sum_slots(ins[4 * n])

    out_shape = [jax.ShapeDtypeStruct(w.shape, F32) for w in ws for _ in range(4)]
    res = pl.pallas_call(body, name=name, out_shape=out_shape + [jax.ShapeDtypeStruct((1, LANES), F32)],
                         compiler_params=_params())(*parts, *ws, *ms, *vs, loss_parts)
    return [res[4 * i:4 * i + 4] for i in range(n)], res[4 * n]


REPLICATED = ("g_cq", "g_ckv", "w_uk", "w_uv", "ln1_g", "ln1_b", "conv_b", "ln2_g", "ln2_b")
ALL_WEIGHTS = ("w_in", "g_cq", "g_ckv", "w_uq", "w_uk", "w_uv", "w_o", "ln1_g", "ln1_b", "w_up", "conv_w", "conv_b",
               "w_down", "ln2_g", "ln2_b")


def kernel(x, w_in, g_cq, g_ckv, w_uq, w_uk, w_uv, w_o, ln1_g, ln1_b, w_up, conv_w, conv_b, w_down, ln2_g, ln2_b, loss_target, m_w_in, m_g_cq, m_g_ckv, m_w_uq, m_w_uk, m_w_uv, m_w_o, m_ln1_g, m_ln1_b, m_w_up, m_conv_w, m_conv_b, m_w_down, m_ln2_g, m_ln2_b, v_w_in, v_g_cq, v_g_ckv, v_w_uq, v_w_uk, v_w_uv, v_w_o, v_ln1_g, v_ln1_b, v_w_up, v_conv_w, v_conv_b, v_w_down, v_ln2_g, v_ln2_b):
    w = dict(w_in=w_in, g_cq=g_cq, g_ckv=g_ckv, w_uq=w_uq, w_uk=w_uk, w_uv=w_uv, w_o=w_o, ln1_g=ln1_g, ln1_b=ln1_b,
             w_up=w_up, conv_w=conv_w, conv_b=conv_b, w_down=w_down, ln2_g=ln2_g, ln2_b=ln2_b)
    m = dict(w_in=m_w_in, g_cq=m_g_cq, g_ckv=m_g_ckv, w_uq=m_w_uq, w_uk=m_w_uk, w_uv=m_w_uv, w_o=m_w_o, ln1_g=m_ln1_g,
             ln1_b=m_ln1_b, w_up=m_w_up, conv_w=m_conv_w, conv_b=m_conv_b, w_down=m_w_down, ln2_g=m_ln2_g, ln2_b=m_ln2_b)
    v = dict(w_in=v_w_in, g_cq=v_g_cq, g_ckv=v_g_ckv, w_uq=v_w_uq, w_uk=v_w_uk, w_uv=v_w_uv, w_o=v_w_o, ln1_g=v_ln1_g,
             ln1_b=v_ln1_b, w_up=v_w_up, conv_w=v_conv_w, conv_b=v_conv_b, w_down=v_w_down, ln2_g=v_ln2_g, ln2_b=v_ln2_b)
    me = 4 * lax.axis_index("x") + 2 * lax.axis_index("y") + lax.axis_index("c")
    wire = lambda a: a.astype(WIRE_DTYPE)
    pad_taps = lambda a: jnp.pad(a, ((0, 8 - a.shape[0]), (0, 0)))

    own_slot = lambda buf, block: lax.dynamic_update_index_in_dim(buf, block, me, 0)
    blocks = lambda a: wire(a).reshape((N_DEV, a.shape[0] // N_DEV) + a.shape[1:])

    g_in, g_uq, g_conv = _all_gather(
        [wire(w_in).T, _heads_major(wire(w_uq)), pad_taps(conv_w)],
        "gather_weights")
    late = _send_start([wire(w_o), wire(w_up).T, wire(w_down)], True, "gather_late_start")
    r_uq_dev, e_uq = w_uq.shape[0], w_uq.shape[2]
    wq = jnp.transpose(g_uq.reshape(N_DEV, HEADS, r_uq_dev, e_uq), (1, 0, 2, 3)).reshape(HEADS, Q_RANK, e_uq)
    cw = dict(
        w_in_t=_split_pad_rows(g_in.reshape(-1, D_MODEL)).astype(MXU_DTYPE),
        wq=jnp.pad(wq, ((0, 0), (0, 0), (0, LANES - e_uq))).astype(MXU_DTYPE),
        wk=_pad_heads(w_uk, NOPE_DIM), wv=_pad_heads(w_uv, HEAD_DIM),
        conv_w=_ffn_interleave(jnp.transpose(g_conv[:, :conv_w.shape[0]], (1, 0, 2)).reshape(conv_w.shape[0], -1), 1),
        g_cq=_row(g_cq), g_ckv=_row(g_ckv), ln1_g=_row(ln1_g), ln1_b=_row(ln1_b), conv_b=_ffn_interleave(_row(conv_b), 1),
        ln2_g=_row(ln2_g), ln2_b=_row(ln2_b))

    def late_weights(after):
        own, landed = _send_wait(late, [after], "gather_late_wait")
        g_o, g_up, g_down = [own_slot(buf, blk) for buf, blk in zip(landed, own)]
        w_o_mla, w_o_dil = _pad_w_o(g_o.reshape(-1, D_MODEL))
        return dict(w_o_mla=w_o_mla, w_o_dil=w_o_dil, w_up_t=_ffn_interleave(g_up.reshape(-1, D_MODEL), 0).astype(MXU_DTYPE),
                    w_down=g_down.reshape(-1, D_MODEL).astype(MXU_DTYPE))

    sent = {}

    def on_grads(stage, g):
        if stage == "ffn":
            sent[stage] = [_send_start([blocks(_ffn_deinterleave(g["w_up_t"], 0)), blocks(g["w_down"])], False, "exchange_ffn_start")]
        elif stage == "w_o":
            sent[stage] = [_send_start([blocks(_unpad_w_o(g["w_o_mla"], g["w_o_dil"]))], False, "exchange_w_o_start")]
        elif stage == "mla":
            d_uq = wire(jnp.transpose(g["wq"][:, :, :e_uq].reshape(HEADS, N_DEV, r_uq_dev, e_uq), (1, 0, 2, 3))
                        ).reshape(N_DEV, HEADS * r_uq_dev, e_uq)
            dense = lambda a, width: wire(a[:, :, :width]).reshape(-1, LANES)
            small = dict(g_cq=g["g_cq"], g_ckv=g["g_ckv"], w_uk=dense(g["wk"], NOPE_DIM), w_uv=dense(g["wv"], HEAD_DIM),
                         ln1_g=g["ln1_g"], ln1_b=g["ln1_b"], conv_b=_ffn_deinterleave(g["conv_b"], 1), ln2_g=g["ln2_g"],
                         ln2_b=g["ln2_b"])
            sent[stage] = [_send_start([d_uq], False, "exchange_w_uq_start"),
                           _send_start([small[n] for n in REPLICATED] + [_ffn_deinterleave(g["conv_w"], 1), g["loss"]],
                                       True, "gather_small_start")]
        else:
            sent[stage] = [_send_start([blocks(_split_unpad_rows(g["w_in_t"]))], False, "exchange_w_in_start")]
        return [h["token"] for h in sent[stage]]

    _, grad_x, _ = _layer_grads(x[0], loss_target[0], cw, [late["token"]], late_weights, on_grads)

    def landed(handle, after, name):
        own, got = _send_wait(handle, after, name)
        pick = (lambda a: a) if handle["gather"] else (lambda a: lax.dynamic_index_in_dim(a, me, 0, keepdims=False))
        return [own_slot(buf, pick(src)) for buf, src in zip(got, own)]

    out = {}

    def update(name, parts, view=None):
        to2d = {None: lambda a: a, "t": lambda a: a.T, "heads": _heads_major}[view]
        back = {None: lambda a: a, "t": lambda a: a.T, "heads": lambda a: _heads_minor(a, HEADS)}[view]
        res = _adamw(parts, to2d(w[name]), to2d(m[name]), to2d(v[name]), "adamw_" + name)
        for kind, a in zip(("grad", "delta", "new_m", "new_v"), res):
            out[kind, name] = back(a)
        return res[0]

    r_up, r_down = landed(sent["ffn"][0], [grad_x], "exchange_ffn_wait")
    (r_o,) = landed(sent["w_o"][0], [grad_x], "exchange_w_o_wait")
    (r_uq,) = landed(sent["mla"][0], [grad_x], "exchange_w_uq_wait")
    *rep_all, cw_all, loss_all = landed(sent["mla"][1], [grad_x], "gather_small_wait")
    done = [update("w_up", r_up, "t"), update("w_down", r_down), update("w_o", r_o), update("w_uq", r_uq, "heads")]
    heads_major = ("w_uk", "w_uv")
    two_d = lambda n, a: _heads_major(a) if n in heads_major else a.reshape(1, -1)
    rep_all = [p.reshape(N_DEV, -1, w[n].shape[2]) if n in heads_major else p for n, p in zip(REPLICATED, rep_all)]
    res, loss_sum = _adamw_small(rep_all, *[[two_d(n, d[n]) for n in REPLICATED] for d in (w, m, v)], loss_all, "adamw_replicated")
    for n, quad in zip(REPLICATED, res):
        for kind, a in zip(("grad", "delta", "new_m", "new_v"), quad):
            out[kind, n] = _heads_minor(a, HEADS) if n in heads_major else a.reshape(w[n].shape)
    loss = loss_sum[0, 0]
    ncw = conv_w.shape[1]
    done += [loss_sum, update("conv_w", lax.dynamic_slice_in_dim(cw_all[:, :conv_w.shape[0]], me * ncw, ncw, axis=2))]
    (r_in,) = landed(sent["w_in"][0], done, "exchange_w_in_wait")
    update("w_in", r_in, "t")

    return (loss, grad_x[None], *[out[kind, n] for kind in ("grad", "delta", "new_m", "new_v") for n in ALL_WEIGHTS])
```

```python
import math

import jax
import jax.numpy as jnp
import numpy as np
from jax import lax
from jax.experimental import pallas as pl
from jax.experimental.pallas import tpu as pltpu

F32 = jnp.float32
MXU_DTYPE = jnp.bfloat16
WIRE_DTYPE = jnp.bfloat16

N_DEV = 8
D_MODEL = 1024
HEADS = 8
HEAD_DIM = 64
LANES = 128
Q_RANK, KV_RANK, ROPE_DIM, NOPE_DIM = 256, 128, 32, 64
DIL_WIDTH = HEADS * HEAD_DIM
MLA_WIDTH = HEADS * HEAD_DIM
IN_PAD = 2048
DH_PART = IN_PAD // 4
D_FF = 2816
ROPE_THETA = 10000.0
DIL_PAIRS = ((128, 1), (512, 4), (2048, 16))
DIL_BLOCK = 128
DN_ALPHA = 2.0 ** 0.25
LN_EPS = 1e-5
RMS_EPS = 1e-6
ONES_LANE = HEAD_DIM
MLA_SCALE = 1.0 / math.sqrt(NOPE_DIM + ROPE_DIM)
MLA_SCALE_LOG2 = MLA_SCALE * math.log2(math.e)
DIL_SCALE = 1.0 / math.sqrt(HEAD_DIM)
ALIBI_SLOPES = tuple(2.0 ** (-8.0 * (h + 1) / HEADS) for h in range(HEADS))
NEG_BIG = -1e30
ADAM_LR, ADAM_B1, ADAM_B2, ADAM_EPS, ADAM_WD, ADAM_STEP = 0.001, 0.9, 0.999, 1e-08, 0.01, 10
VMEM_LIMIT = 48 * 1024 * 1024


def _params(*sem):
    return pltpu.CompilerParams(dimension_semantics=sem or None, vmem_limit_bytes=VMEM_LIMIT)


def _dot(a, b, ca, cb):
    return lax.dot_general(a, b, (((ca,), (cb,)), ((), ())), preferred_element_type=F32)


_ANY_SPEC = pl.BlockSpec(memory_space=pl.ANY)


def _mm(a, b, *, name, tm, tn, tk, ta=False, tb=False, out_dtype=F32, after=()):
    m, k = (a.shape[1], a.shape[0]) if ta else a.shape
    n = b.shape[0] if tb else b.shape[1]
    assert (b.shape[1] if tb else b.shape[0]) == k
    tm, tn, tk = min(tm, m), min(tn, n), min(tk, k)
    assert m % tm == 0 and n % tn == 0 and k % tk == 0, (name, m, n, k, tm, tn, tk)
    nk = k // tk
    a_spec = (pl.BlockSpec((tk, tm), lambda i, j, kk: (kk, i)) if ta
              else pl.BlockSpec((tm, tk), lambda i, j, kk: (i, kk)))
    b_mode = dict(pipeline_mode=pl.Buffered(1)) if (tn == n and tk == k) else {}
    b_spec = (pl.BlockSpec((tn, tk), lambda i, j, kk: (j, kk), **b_mode) if tb
              else pl.BlockSpec((tk, tn), lambda i, j, kk: (kk, j), **b_mode))
    o_spec = pl.BlockSpec((tm, tn), lambda i, j, kk: (i, j))
    n_in = 2 + len(after)
    ca, cb = (0 if ta else 1), (1 if tb else 0)

    def body(*refs):
        a_ref, b_ref, o_ref = refs[0], refs[1], refs[n_in]
        part = _dot(a_ref[...].astype(MXU_DTYPE), b_ref[...].astype(MXU_DTYPE), ca, cb)
        if nk == 1:
            o_ref[...] = part.astype(o_ref.dtype)
            return
        acc_ref = refs[-1]
        kk = pl.program_id(2)

        @pl.when(kk == 0)
        def _():
            acc_ref[...] = part

        @pl.when(kk > 0)
        def _():
            acc_ref[...] += part

        @pl.when(kk == nk - 1)
        def _():
            o_ref[...] = acc_ref[...].astype(o_ref.dtype)

    return pl.pallas_call(
        body, name=name, grid=(m // tm, n // tn, nk), in_specs=[a_spec, b_spec] + [_ANY_SPEC] * len(after), out_specs=o_spec,
        out_shape=jax.ShapeDtypeStruct((m, n), out_dtype),
        scratch_shapes=[pltpu.VMEM((tm, tn), F32)] if nk > 1 else [],
        compiler_params=_params("parallel", "parallel", "arbitrary"),
    )(a, b, *after)


def _w_o_bwd(dz, o_mla, o_dil, w_o_mla, w_o_dil, tm=512):
    seq, d = dz.shape
    tm = min(tm, seq)
    nstep = seq // tm

    def body(a_ref, om_ref, od_ref, wm_ref, wd_ref, dom_ref, dod_ref, dwm_ref, dwd_ref, accm_ref, accd_ref):
        step = pl.program_id(0)

        @pl.when(step == 0)
        def _():
            accm_ref[...] = jnp.zeros_like(accm_ref)
            accd_ref[...] = jnp.zeros_like(accd_ref)

        a = a_ref[...]
        for hd in range(HEADS):
            rows = slice(LANES * hd, LANES * (hd + 1))
            dom_ref[hd] = _dot(a, wm_ref[rows, :], 1, 1)
            accm_ref[rows, :] += _dot(om_ref[hd], a, 0, 0)
        dod_ref[...] = _dot(a, wd_ref[...], 1, 1)
        accd_ref[...] += _dot(od_ref[...], a, 0, 0)

        @pl.when(step == nstep - 1)
        def _():
            dwm_ref[...] = accm_ref[...].astype(dwm_ref.dtype)
            dwd_ref[...] = accd_ref[...].astype(dwd_ref.dtype)

    once = dict(pipeline_mode=pl.Buffered(1))
    return pl.pallas_call(
        body, name="w_o_bwd", grid=(nstep,),
        in_specs=[pl.BlockSpec((tm, d), lambda i: (i, 0)), pl.BlockSpec((HEADS, tm, LANES), lambda i: (0, i, 0)),
                  pl.BlockSpec((tm, DIL_WIDTH), lambda i: (i, 0)), pl.BlockSpec((HEADS * LANES, d), lambda i: (0, 0), **once),
                  pl.BlockSpec((DIL_WIDTH, d), lambda i: (0, 0), **once)],
        out_specs=[pl.BlockSpec((HEADS, tm, LANES), lambda i: (0, i, 0)), pl.BlockSpec((tm, DIL_WIDTH), lambda i: (i, 0)),
                   pl.BlockSpec((HEADS * LANES, d), lambda i: (0, 0)), pl.BlockSpec((DIL_WIDTH, d), lambda i: (0, 0))],
        out_shape=[jax.ShapeDtypeStruct((HEADS, seq, LANES), F32), jax.ShapeDtypeStruct((seq, DIL_WIDTH), F32),
                   jax.ShapeDtypeStruct((HEADS * LANES, d), MXU_DTYPE), jax.ShapeDtypeStruct((DIL_WIDTH, d), MXU_DTYPE)],
        scratch_shapes=[pltpu.VMEM((HEADS * LANES, d), F32), pltpu.VMEM((DIL_WIDTH, d), F32)],
        compiler_params=_params("arbitrary"),
    )(dz, o_mla, o_dil, w_o_mla, w_o_dil)


def _rope_tables(seq):
    half = ROPE_DIM // 2
    f32 = np.float32
    freqs = np.power(f32(ROPE_THETA), -np.arange(half, dtype=f32) / f32(half))
    ang = np.arange(seq, dtype=f32)[:, None] * freqs[None, :]
    cos, sin = np.cos(ang, dtype=f32), np.sin(ang, dtype=f32)
    one = np.ones((seq, NOPE_DIM), f32)
    tail = np.ones((seq, LANES - NOPE_DIM - ROPE_DIM), f32)
    ctab = np.concatenate([one, cos, cos, tail], axis=1)
    stab = np.concatenate([0 * one, -sin, sin, 0 * tail], axis=1)
    return jnp.asarray(ctab), jnp.asarray(stab)


def _rope_swap(t):
    lane = lax.broadcasted_iota(jnp.int32, t.shape, 1)
    half = ROPE_DIM // 2
    return jnp.where(lane < NOPE_DIM + half, pltpu.roll(t, LANES - half, 1), pltpu.roll(t, half, 1))


def _rope(t, ctab, stab):
    return t * ctab + _rope_swap(t) * stab


def _rope_inv(t, ctab, stab):
    return t * ctab - _rope_swap(t) * stab


def _rms(x, g):
    r = lax.rsqrt(jnp.mean(x * x, axis=-1, keepdims=True) + RMS_EPS)
    xh = x * r
    return xh, r, xh * g


def _mla_prep(h, g_cq, g_ckv, wq, wk, wv, ctab, stab, tm=512):
    seq = h.shape[0]
    tm = min(tm, seq)

    def body(h_ref, gq_ref, gk_ref, wq_ref, wk_ref, wv_ref, c_ref, s_ref, q_out, k_out, v_out):
        hb = h_ref[...]
        ctab_, stab_ = c_ref[...], s_ref[...]
        _, _, cqn = _rms(hb[:, :Q_RANK], gq_ref[...])
        _, _, ckn = _rms(hb[:, Q_RANK:Q_RANK + KV_RANK], gk_ref[...])
        cqn = cqn.astype(MXU_DTYPE)
        ckn = ckn.astype(MXU_DTYPE)
        krr = _rope(hb[:, Q_RANK + KV_RANK:], ctab_, stab_)
        ones_lane = (lax.broadcasted_iota(jnp.int32, (1, LANES), 1) == ONES_LANE).astype(F32)
        for hd in range(HEADS):
            q = _dot(cqn, wq_ref[hd], 1, 0)
            q_out[hd] = _rope(q, ctab_, stab_).astype(q_out.dtype)
            k_out[hd] = (_dot(ckn, wk_ref[hd], 1, 0) + krr).astype(k_out.dtype)
            v_out[hd] = (_dot(ckn, wv_ref[hd], 1, 0) + ones_lane).astype(v_out.dtype)

    full = lambda *shape: pl.BlockSpec(shape, lambda i: (0,) * len(shape))
    slab = pl.BlockSpec((HEADS, tm, LANES), lambda i: (0, i, 0))
    shp = jax.ShapeDtypeStruct((HEADS, seq, LANES), MXU_DTYPE)
    return pl.pallas_call(
        body, name="mla_prep", grid=(seq // tm,),
        in_specs=[pl.BlockSpec((tm, DH_PART), lambda i: (i, 0)), full(1, Q_RANK), full(1, KV_RANK),
                  full(HEADS, Q_RANK, LANES), full(HEADS, KV_RANK, LANES), full(HEADS, KV_RANK, LANES),
                  pl.BlockSpec((tm, LANES), lambda i: (i, 0)), pl.BlockSpec((tm, LANES), lambda i: (i, 0))],
        out_specs=[slab, slab, slab], out_shape=[shp, shp, shp],
        compiler_params=_params("parallel"),
    )(h, g_cq, g_ckv, wq, wk, wv, ctab, stab)


def _mla_prep_bwd(h, g_cq, g_ckv, wq, wk, wv, ctab, stab, dq, dk, dv, tm=512, after=()):
    seq = h.shape[0]
    tm = min(tm, seq)
    n_after = len(after)

    def body(h_ref, gq_ref, gk_ref, wq_ref, wk_ref, wv_ref, c_ref, s_ref, dq_ref, dk_ref, dv_ref, *rest):
        dh_ref, dwq_ref, dwk_ref, dwv_ref, dgq_ref, dgk_ref = rest[n_after:]

        @pl.when(pl.program_id(0) == 0)
        def _():
            for r in (dwq_ref, dwk_ref, dwv_ref, dgq_ref, dgk_ref):
                r[...] = jnp.zeros_like(r)

        hb = h_ref[...]
        ctab_, stab_ = c_ref[...], s_ref[...]
        gq, gk = gq_ref[...], gk_ref[...]
        xq, rq, cqn = _rms(hb[:, :Q_RANK], gq)
        xk, rk, ckn = _rms(hb[:, Q_RANK:Q_RANK + KV_RANK], gk)
        cqn = cqn.astype(MXU_DTYPE)
        ckn = ckn.astype(MXU_DTYPE)
        d_cqn = jnp.zeros((tm, Q_RANK), F32)
        d_ckn = jnp.zeros((tm, KV_RANK), F32)
        d_krr = jnp.zeros((tm, LANES), F32)
        for hd in range(HEADS):
            dqh = _rope_inv(dq_ref[hd], ctab_, stab_).astype(MXU_DTYPE)
            d_cqn += _dot(dqh, wq_ref[hd], 1, 1)
            dwq_ref[hd] += _dot(cqn, dqh, 0, 0)
            dkh = dk_ref[hd]
            d_krr += dkh
            dkh = dkh.astype(MXU_DTYPE)
            d_ckn += _dot(dkh, wk_ref[hd], 1, 1)
            dwk_ref[hd] += _dot(ckn, dkh, 0, 0)
            dvh = dv_ref[hd].astype(MXU_DTYPE)
            d_ckn += _dot(dvh, wv_ref[hd], 1, 1)
            dwv_ref[hd] += _dot(ckn, dvh, 0, 0)
        lane = lax.broadcasted_iota(jnp.int32, (tm, LANES), 1)
        rot = (lane >= NOPE_DIM) & (lane < NOPE_DIM + ROPE_DIM)
        d_kr = jnp.where(rot, _rope_inv(jnp.where(rot, d_krr, 0.0), ctab_, stab_), 0.0)

        def rms_bwd(dy, xh, r, g, dg_ref):
            dg_ref[...] += jnp.sum(dy * xh, axis=0, keepdims=True)
            dxh = dy * g
            return r * (dxh - xh * jnp.mean(dxh * xh, axis=-1, keepdims=True))

        d_cq = rms_bwd(d_cqn, xq, rq, gq, dgq_ref)
        d_ck = rms_bwd(d_ckn, xk, rk, gk, dgk_ref)
        dh_ref[...] = jnp.concatenate([d_cq, d_ck, d_kr], axis=1).astype(dh_ref.dtype)

    full = lambda *shape: pl.BlockSpec(shape, lambda i: (0,) * len(shape))
    slab = pl.BlockSpec((HEADS, tm, LANES), lambda i: (0, i, 0))
    return pl.pallas_call(
        body, name="mla_prep_bwd", grid=(seq // tm,),
        in_specs=[pl.BlockSpec((tm, DH_PART), lambda i: (i, 0)), full(1, Q_RANK), full(1, KV_RANK),
                  full(HEADS, Q_RANK, LANES), full(HEADS, KV_RANK, LANES), full(HEADS, KV_RANK, LANES),
                  pl.BlockSpec((tm, LANES), lambda i: (i, 0)), pl.BlockSpec((tm, LANES), lambda i: (i, 0)),
                  slab, slab, slab] + [_ANY_SPEC] * n_after,
        out_specs=[pl.BlockSpec((tm, DH_PART), lambda i: (i, 0)), full(HEADS, Q_RANK, LANES), full(HEADS, KV_RANK, LANES),
                   full(HEADS, KV_RANK, LANES), full(1, Q_RANK), full(1, KV_RANK)],
        out_shape=[jax.ShapeDtypeStruct((seq, DH_PART), MXU_DTYPE), jax.ShapeDtypeStruct((HEADS, Q_RANK, LANES), F32),
                   jax.ShapeDtypeStruct((HEADS, KV_RANK, LANES), F32), jax.ShapeDtypeStruct((HEADS, KV_RANK, LANES), F32),
                   jax.ShapeDtypeStruct((1, Q_RANK), F32), jax.ShapeDtypeStruct((1, KV_RANK), F32)],
        compiler_params=_params("arbitrary"),
    )(h, g_cq, g_ckv, wq, wk, wv, ctab, stab, dq, dk, dv, *after)


def _mla_attn_fwd(q, k, v, t=1024):
    _, seq, _ = q.shape
    t = min(t, seq)

    def body(q_ref, k_ref, v_ref, o_ref, ob_ref, lse_ref, m_ref, acc_ref, s_ref):
        i = pl.program_id(1)
        qb = q_ref[...]
        m_ref[...] = jnp.full_like(m_ref, NEG_BIG)
        acc_ref[...] = jnp.zeros_like(acc_ref)

        def scores(j):
            return _dot(qb, k_ref[pl.ds(pl.multiple_of(j * t, t), t), :], 1, 1) * MLA_SCALE_LOG2

        def softmax_pv(j, s, rows=slice(None), mask=None):
            vb = v_ref[pl.ds(pl.multiple_of(j * t, t), s.shape[1]), :]
            if mask is not None:
                s = jnp.where(mask, s, NEG_BIG)
            m_old = m_ref[rows, :]
            m_new = jnp.maximum(m_old, jnp.max(s, axis=1, keepdims=True))
            p = jnp.exp2(s - m_new)
            a = jnp.exp2(m_old - m_new)
            acc_ref[rows, :] = a * acc_ref[rows, :] + _dot(p.astype(MXU_DTYPE), vb, 1, 0)
            m_ref[rows, :] = m_new

        def softmax_pv_diagonal(j):
            th = t // 2
            for hf, nk in ((0, th), (1, t)):
                row = lax.broadcasted_iota(jnp.int32, (th, nk), 0) + hf * th
                rows = slice(hf * th, (hf + 1) * th)
                softmax_pv(j, s_ref[rows, 0:nk], rows, row >= lax.broadcasted_iota(jnp.int32, (th, nk), 1))

        s_ref[...] = scores(0)

        def loop_body(j, c):
            s_next = scores(j + 1)
            softmax_pv(j, s_ref[...])
            s_ref[...] = s_next
            return c

        lax.fori_loop(0, i, loop_body, 0)
        softmax_pv_diagonal(i)
        acc = acc_ref[...]
        l = acc[:, ONES_LANE:ONES_LANE + 1]
        o = jnp.where(lax.broadcasted_iota(jnp.int32, acc.shape, 1) < HEAD_DIM, acc * (1.0 / l), 0.0)
        o_ref[...] = o
        ob_ref[...] = o.astype(ob_ref.dtype)
        lse_ref[...] = jnp.broadcast_to(m_ref[...] + jnp.log2(l), lse_ref.shape)

    blk = pl.BlockSpec((None, t, LANES), lambda h, i: (h, i, 0))
    whole = pl.BlockSpec((None, seq, LANES), lambda h, i: (h, 0, 0))
    shp = jax.ShapeDtypeStruct((HEADS, seq, LANES), F32)
    return pl.pallas_call(
        body, name="mla_attn_fwd", grid=(HEADS, seq // t),
        in_specs=[blk, whole, whole], out_specs=[blk, blk, blk],
        out_shape=[shp, jax.ShapeDtypeStruct((HEADS, seq, LANES), MXU_DTYPE), shp],
        scratch_shapes=[pltpu.VMEM((t, 1), F32), pltpu.VMEM((t, LANES), F32), pltpu.VMEM((t, t), F32)],
        compiler_params=_params("parallel", "arbitrary"),
    )(q, k, v)


def _mla_attn_bwd(q, k, v, o, lse, do, t=1024):
    _, seq, _ = q.shape
    t = min(t, seq)
    nb = seq // t

    def body(q_ref, k_ref, v_ref, o_ref, lse_ref, do_ref, dq_ref, dk_ref, dv_ref, dl_ref, dka_ref, dva_ref):
        dq_ref[...] = jnp.zeros_like(dq_ref)

        def delta_body(i, c):
            rows = pl.ds(pl.multiple_of(i * t, t), t)
            dl_ref[rows, :] = jnp.sum(do_ref[rows, :] * o_ref[rows, :], axis=1, keepdims=True)
            return c

        lax.fori_loop(0, nb, delta_body, 0)

        def kblock(j, c):
            krows = pl.ds(pl.multiple_of(j * t, t), t)
            kb = k_ref[krows, :]
            vb = v_ref[krows, :]
            dka_ref[...] = jnp.zeros_like(dka_ref)
            dva_ref[...] = jnp.zeros_like(dva_ref)

            def qstep(i, masked):
                th = t // 2
                rows = [pl.ds(pl.multiple_of(i * t + hf * th, th), th) for hf in range(2)]
                qs = [q_ref[r, :] for r in rows]
                dos = [do_ref[r, :].astype(MXU_DTYPE) for r in rows]
                nkeys = [th if masked else t, t]
                ss = [_dot(qs[hf], kb[:nkeys[hf]], 1, 1) * MLA_SCALE_LOG2 for hf in range(2)]
                dps = [_dot(dos[hf], vb[:nkeys[hf]], 1, 1) for hf in range(2)]
                for hf in range(2):
                    s, nk = ss[hf], nkeys[hf]
                    if masked:
                        row = lax.broadcasted_iota(jnp.int32, (th, nk), 0) + hf * th
                        s = jnp.where(row >= lax.broadcasted_iota(jnp.int32, (th, nk), 1), s, NEG_BIG)
                    p = jnp.exp2(s - lse_ref[rows[hf], 0:1])
                    dva_ref[0:nk, :] += _dot(p.astype(MXU_DTYPE), dos[hf], 0, 0)
                    ds = (p * (dps[hf] - dl_ref[rows[hf], :]) * MLA_SCALE).astype(MXU_DTYPE)
                    dka_ref[0:nk, :] += _dot(ds, qs[hf], 0, 0)
                    dq_ref[rows[hf], :] += _dot(ds, kb[:nk], 1, 0)

            qstep(j, True)

            def qloop(i, c2):
                qstep(i, False)
                return c2

            lax.fori_loop(j + 1, nb, qloop, 0)
            dk_ref[krows, :] = dka_ref[...]
            dv_ref[krows, :] = dva_ref[...]
            return c

        lax.fori_loop(0, nb, kblock, 0)

    whole = pl.BlockSpec((None, seq, LANES), lambda h: (h, 0, 0))
    shp = jax.ShapeDtypeStruct((HEADS, seq, LANES), F32)
    return pl.pallas_call(
        body, name="mla_attn_bwd", grid=(HEADS,),
        in_specs=[whole] * 6, out_specs=[whole] * 3, out_shape=[shp] * 3,
        scratch_shapes=[pltpu.VMEM((seq, 1), F32), pltpu.VMEM((t, LANES), F32), pltpu.VMEM((t, LANES), F32)],
        compiler_params=_params("parallel"),
    )(q, k, v, o, lse, do)


DIL_CHUNK = DIL_BLOCK * max(d for _, d in DIL_PAIRS)
DIL_PAIR_LANES = 2 * HEAD_DIM
assert DIL_PAIR_LANES == LANES
DIL_UNROLL_FWD = 16
DIL_UNROLL_BWD = 8


def _dil_bias_tables(hp, dil):
    b = DIL_BLOCK
    iq = lax.broadcasted_iota(jnp.int32, (b, 2 * b), 0)
    ik = lax.broadcasted_iota(jnp.int32, (b, 2 * b), 1)
    off = iq + b - ik
    band = (off >= 0) & (off <= b)
    dist = (off * dil).astype(F32)
    every, first = [], []
    for hh in range(2):
        slope = jnp.where(hp == 0, ALIBI_SLOPES[hh], jnp.where(hp == 1, ALIBI_SLOPES[2 + hh],
                          jnp.where(hp == 2, ALIBI_SLOPES[4 + hh], ALIBI_SLOPES[6 + hh]))).astype(F32)
        bias = -slope * dist
        every.append(jnp.where(band, bias, NEG_BIG))
        first.append(jnp.where(band & (ik >= b), bias, NEG_BIG))
    return jnp.concatenate(every, axis=0), jnp.concatenate(first, axis=0)


def _dil_rows(start, dil):
    return pl.ds(start, DIL_BLOCK) if dil == 1 else pl.ds(start, DIL_BLOCK, stride=dil)


def _dil_block_pos(blk, c, dil):
    sc, r = blk // dil, blk % dil
    q0 = sc * (DIL_BLOCK * dil) + r
    kcur0 = c * DIL_CHUNK + q0
    first = kcur0 < DIL_BLOCK * dil
    kprev0 = jnp.where(first, kcur0, kcur0 - DIL_BLOCK * dil)
    return q0, kcur0, kprev0, first


def _pair_cols(hh):
    return slice(HEAD_DIM * hh, HEAD_DIM * (hh + 1))


def _first_head_lanes(shape):
    return lax.broadcasted_iota(jnp.int32, shape, 1) < HEAD_DIM


def _stack_pair(t):
    first = _first_head_lanes(t.shape)
    return jnp.concatenate([jnp.where(first, t, 0.0), jnp.where(first, 0.0, t)], axis=0).astype(MXU_DTYPE)


def _unstack_pair(t):
    rows = t.shape[0] // 2
    return jnp.where(_first_head_lanes((rows, t.shape[1])), t[:rows], t[rows:])


def _pair_column(t):
    return jnp.concatenate([t[:, 0:1], t[:, HEAD_DIM:HEAD_DIM + 1]], axis=0)


def _dil_fwd(h):
    seq = h.shape[0]
    assert seq % DIL_CHUNK == 0
    nblk = DIL_CHUNK // DIL_BLOCK
    rc = 256

    def body(q_ref, k_ref, v_ref, o_ref, ob_ref, lse_ref, *scr):
        o_scr, l_scr = scr[:3], scr[3:]
        hp, c = pl.program_id(0), pl.program_id(1)
        for bi, (_, dil) in enumerate(DIL_PAIRS):
            tables = _dil_bias_tables(hp, dil)

            def block(blk, carry, bi=bi, dil=dil, tables=tables):
                q0, kcur0, kprev0, first = _dil_block_pos(blk, c, dil)
                q2 = _stack_pair(q_ref[_dil_rows(q0, dil), :] * DIL_SCALE)
                kcat = jnp.concatenate([k_ref[_dil_rows(kprev0, dil), :], k_ref[_dil_rows(kcur0, dil), :]], axis=0).astype(MXU_DTYPE)
                vcat = jnp.concatenate([v_ref[_dil_rows(kprev0, dil), :], v_ref[_dil_rows(kcur0, dil), :]], axis=0).astype(MXU_DTYPE)
                s = _dot(q2, kcat, 1, 1) + jnp.where(first, tables[1], tables[0])
                mx = jnp.max(s, axis=1, keepdims=True)
                p = jnp.exp(s - mx)
                l = jnp.sum(p, axis=1, keepdims=True)
                o_scr[bi][_dil_rows(q0, dil), :] = _unstack_pair(_dot(p.astype(MXU_DTYPE), vcat, 1, 0) * (1.0 / l))
                l_scr[bi][_dil_rows(q0, dil), :] = _unstack_pair(jnp.broadcast_to(mx + jnp.log(l), (2 * DIL_BLOCK, LANES)))
                return carry

            lax.fori_loop(0, nblk, block, 0, unroll=DIL_UNROLL_FWD)

        def combine(i, carry):
            rows = pl.ds(pl.multiple_of(i * rc, rc), rc)
            ls = [l_scr[bi][rows, :] for bi in range(3)]
            mx = jnp.maximum(jnp.maximum(ls[0], ls[1]), ls[2])
            es = [jnp.exp(l - mx) for l in ls]
            den = es[0] + es[1] + es[2]
            o = (es[0] * o_scr[0][rows, :] + es[1] * o_scr[1][rows, :] + es[2] * o_scr[2][rows, :]) / den
            o_ref[rows, :] = o
            ob_ref[rows, :] = o.astype(ob_ref.dtype)
            lse_ref[rows, :] = mx + jnp.log(den)
            return carry

        lax.fori_loop(0, DIL_CHUNK // rc, combine, 0)

    nq = DIL_WIDTH // LANES
    chunk = lambda off: pl.BlockSpec((DIL_CHUNK, LANES), lambda hp, c: (c, off + hp))
    whole = lambda off: pl.BlockSpec((seq, LANES), lambda hp, c: (0, off + hp))
    shp = jax.ShapeDtypeStruct((seq, DIL_WIDTH), F32)
    return pl.pallas_call(
        body, name="dil_fwd", grid=(nq, seq // DIL_CHUNK),
        in_specs=[chunk(nq), whole(2 * nq), whole(3 * nq)], out_specs=[chunk(0), chunk(0), chunk(0)],
        out_shape=[shp, jax.ShapeDtypeStruct((seq, DIL_WIDTH), MXU_DTYPE), shp],
        scratch_shapes=[pltpu.VMEM((DIL_CHUNK, LANES), F32)] * 6,
        compiler_params=_params("parallel", "arbitrary"),
    )(h, h, h)


def _dil_bwd(h, o, lse, do, after=()):
    seq = h.shape[0]
    nblk = DIL_CHUNK // DIL_BLOCK
    nchunk = seq // DIL_CHUNK
    rc = 256

    n_after = len(after)

    def body(q_ref, k_ref, v_ref, o_ref, lse_ref, do_ref, *rest):
        dq_out, dk_out, dv_out, dl_scr, dq_ref, dk_ref, dv_ref = rest[n_after:]
        hp, c = pl.program_id(0), pl.program_id(1)

        @pl.when(c == 0)
        def _():
            dk_ref[...] = jnp.zeros_like(dk_ref)
            dv_ref[...] = jnp.zeros_like(dv_ref)

        def delta(i, carry):
            rows = pl.ds(pl.multiple_of(i * rc, rc), rc)
            prod = do_ref[rows, :] * o_ref[rows, :]
            dl_scr[rows, :] = jnp.concatenate(
                [jnp.broadcast_to(jnp.sum(prod[:, _pair_cols(hh)], axis=1, keepdims=True), (rc, HEAD_DIM)) for hh in range(2)], axis=1)
            return carry

        lax.fori_loop(0, DIL_CHUNK // rc, delta, 0)

        for bi, (_, dil) in enumerate(DIL_PAIRS):
            tables = _dil_bias_tables(hp, dil)

            def block(blk, carry, bi=bi, dil=dil, tables=tables):
                q0, kcur0, kprev0, first = _dil_block_pos(blk, c, dil)
                qrows = _dil_rows(q0, dil)
                q2 = _stack_pair(q_ref[qrows, :] * DIL_SCALE)
                kcat = jnp.concatenate([k_ref[_dil_rows(kprev0, dil), :], k_ref[_dil_rows(kcur0, dil), :]], axis=0).astype(MXU_DTYPE)
                vcat = jnp.concatenate([v_ref[_dil_rows(kprev0, dil), :], v_ref[_dil_rows(kcur0, dil), :]], axis=0).astype(MXU_DTYPE)
                do2 = _stack_pair(do_ref[qrows, :])
                s = _dot(q2, kcat, 1, 1) + jnp.where(first, tables[1], tables[0])
                p = jnp.exp(s - _pair_column(lse_ref[qrows, :]))
                dp = _dot(do2, vcat, 1, 1)
                ds = (p * (dp - _pair_column(dl_scr[qrows, :]))).astype(MXU_DTYPE)
                dq_b = _unstack_pair(_dot(ds, kcat, 1, 0)) * DIL_SCALE
                dk_b = _dot(ds, q2, 0, 0)
                dv_b = _dot(p.astype(MXU_DTYPE), do2, 0, 0)
                if bi == 0:
                    dq_ref[qrows, :] = dq_b
                else:
                    dq_ref[qrows, :] += dq_b
                dk_ref[_dil_rows(kprev0, dil), :] += dk_b[:DIL_BLOCK]
                dv_ref[_dil_rows(kprev0, dil), :] += dv_b[:DIL_BLOCK]
                dk_ref[_dil_rows(kcur0, dil), :] += dk_b[DIL_BLOCK:]
                dv_ref[_dil_rows(kcur0, dil), :] += dv_b[DIL_BLOCK:]
                return carry

            lax.fori_loop(0, nblk, block, 0, unroll=DIL_UNROLL_BWD)

        dq_out[...] = dq_ref[...].astype(dq_out.dtype)

        @pl.when(c == nchunk - 1)
        def _():
            dk_out[...] = dk_ref[...].astype(dk_out.dtype)
            dv_out[...] = dv_ref[...].astype(dv_out.dtype)

    nq = DIL_WIDTH // LANES
    chunk = lambda off: pl.BlockSpec((DIL_CHUNK, LANES), lambda hp, c: (c, off + hp))
    whole = lambda off: pl.BlockSpec((seq, LANES), lambda hp, c: (0, off + hp))
    shp = jax.ShapeDtypeStruct((seq, DIL_WIDTH), MXU_DTYPE)
    return pl.pallas_call(
        body, name="dil_bwd", grid=(nq, nchunk),
        in_specs=[chunk(nq), whole(2 * nq), whole(3 * nq), chunk(0), chunk(0), chunk(0)] + [_ANY_SPEC] * n_after,
        out_specs=[chunk(0), whole(0), whole(0)], out_shape=[shp, shp, shp],
        scratch_shapes=[pltpu.VMEM((DIL_CHUNK, LANES), F32), pltpu.VMEM((DIL_CHUNK, LANES), F32),
                        pltpu.VMEM((seq, LANES), F32), pltpu.VMEM((seq, LANES), F32)],
        compiler_params=_params("parallel", "arbitrary"),
    )(h, h, h, o, lse, do, *after)


def _mm_dx0(parts, w_in_t, res, tm=1024, after=()):
    seq, d = res.shape
    tm = min(tm, seq)
    n_after = len(after)

    def body(a0, a1, a2, a3, b_ref, r_ref, *rest):
        o_ref = rest[n_after]
        acc = _dot(a0[...], b_ref[0:DH_PART, :], 1, 0)
        for c, a in enumerate((a1, a2, a3), start=1):
            acc += _dot(a[...], b_ref[DH_PART * c:DH_PART * (c + 1), :], 1, 0)
        o_ref[...] = acc + DN_ALPHA * r_ref[...]

    blk = pl.BlockSpec((tm, DH_PART), lambda i: (i, 0))
    row = pl.BlockSpec((tm, d), lambda i: (i, 0))
    return pl.pallas_call(
        body, name="mm_dx0", grid=(seq // tm,),
        in_specs=[blk] * 4 + [pl.BlockSpec((IN_PAD, d), lambda i: (0, 0), pipeline_mode=pl.Buffered(1)), row] + [_ANY_SPEC] * n_after,
        out_specs=row, out_shape=jax.ShapeDtypeStruct((seq, d), F32), compiler_params=_params("parallel"),
    )(*parts, w_in_t, res, *after)


def _mm_dw_in(parts, x0, tk=1024):
    seq, d = x0.shape
    tk = min(tk, seq)
    nk = seq // tk

    def body(a0, a1, a2, a3, b_ref, o_ref, acc_ref):
        kk = pl.program_id(0)

        @pl.when(kk == 0)
        def _():
            acc_ref[...] = jnp.zeros_like(acc_ref)

        b = b_ref[...].astype(MXU_DTYPE)
        for c, a in enumerate((a0, a1, a2, a3)):
            acc_ref[DH_PART * c:DH_PART * (c + 1), :] += _dot(a[...], b, 0, 0)

        @pl.when(kk == nk - 1)
        def _():
            o_ref[...] = acc_ref[...].astype(o_ref.dtype)

    blk = pl.BlockSpec((tk, DH_PART), lambda kk: (kk, 0))
    return pl.pallas_call(
        body, name="mm_dw_in", grid=(nk,), in_specs=[blk] * 4 + [pl.BlockSpec((tk, d), lambda kk: (kk, 0))],
        out_specs=pl.BlockSpec((IN_PAD, d), lambda kk: (0, 0)), out_shape=jax.ShapeDtypeStruct((IN_PAD, d), MXU_DTYPE),
        scratch_shapes=[pltpu.VMEM((IN_PAD, d), F32)], compiler_params=_params("arbitrary"),
    )(*parts, x0)


def _ln_stats(z):
    mu = jnp.mean(z, axis=-1, keepdims=True)
    zc = z - mu
    r = lax.rsqrt(jnp.mean(zc * zc, axis=-1, keepdims=True) + LN_EPS)
    return zc * r, r


def _ln_bwd_math(dy, xh, r, g):
    dxh = dy * g
    return r * (dxh - jnp.mean(dxh, axis=-1, keepdims=True) - xh * jnp.mean(dxh * xh, axis=-1, keepdims=True))


def _mix_ln1(o_mla, o_dil, w_o_mla, w_o_dil, x0, g, b, tm=512):
    seq, d = x0.shape
    tm = min(tm, seq)

    def body(om_ref, od_ref, wm_ref, wd_ref, x_ref, g_ref, b_ref, z_ref, y_ref, yb_ref):
        mix = _dot(od_ref[...], wd_ref[...], 1, 0)
        for hd in range(HEADS):
            mix += _dot(om_ref[hd], wm_ref[LANES * hd:LANES * (hd + 1), :], 1, 0)
        z = DN_ALPHA * x_ref[...] + mix
        xh, _ = _ln_stats(z)
        y = xh * g_ref[...] + b_ref[...]
        z_ref[...] = z
        y_ref[...] = y
        yb_ref[...] = y.astype(yb_ref.dtype)

    blk = pl.BlockSpec((tm, d), lambda i: (i, 0))
    vec = pl.BlockSpec((1, d), lambda i: (0, 0))
    shp = jax.ShapeDtypeStruct((seq, d), F32)
    return pl.pallas_call(
        body, name="mix_ln1", grid=(seq // tm,),
        in_specs=[pl.BlockSpec((HEADS, tm, LANES), lambda i: (0, i, 0)), pl.BlockSpec((tm, DIL_WIDTH), lambda i: (i, 0)),
                  pl.BlockSpec((HEADS * LANES, d), lambda i: (0, 0)), pl.BlockSpec((DIL_WIDTH, d), lambda i: (0, 0)), blk, vec, vec],
        out_specs=[blk, blk, blk], out_shape=[shp, shp, jax.ShapeDtypeStruct((seq, d), MXU_DTYPE)],
        compiler_params=_params("parallel"))(o_mla, o_dil, w_o_mla, w_o_dil, x0, g, b)


def _dx1_ln1_bwd(du, w_up_t, dz2, z, g, tm=256, after=()):
    seq, d = z.shape
    kdim = du.shape[1]
    tm = min(tm, seq)
    n_after = len(after)

    def body(du_ref, w_ref, r_ref, z_ref, g_ref, *rest):
        dz_ref, dzb_ref, dg_ref, db_ref = rest[n_after:]

        @pl.when(pl.program_id(0) == 0)
        def _():
            dg_ref[...] = jnp.zeros_like(dg_ref)
            db_ref[...] = jnp.zeros_like(db_ref)

        dyb = _dot(du_ref[...], w_ref[...], 1, 0) + DN_ALPHA * r_ref[...]
        xh, r = _ln_stats(z_ref[...])
        dg_ref[...] += jnp.sum(dyb * xh, axis=0, keepdims=True)
        db_ref[...] += jnp.sum(dyb, axis=0, keepdims=True)
        dz = _ln_bwd_math(dyb, xh, r, g_ref[...])
        dz_ref[...] = dz
        dzb_ref[...] = dz.astype(dzb_ref.dtype)

    blk = pl.BlockSpec((tm, d), lambda i: (i, 0))
    vec = pl.BlockSpec((1, d), lambda i: (0, 0))
    return pl.pallas_call(
        body, name="dx1_ln1_bwd", grid=(seq // tm,),
        in_specs=[pl.BlockSpec((tm, kdim), lambda i: (i, 0)),
                  pl.BlockSpec((kdim, d), lambda i: (0, 0), pipeline_mode=pl.Buffered(1)), blk, blk, vec] + [_ANY_SPEC] * n_after,
        out_specs=[blk, blk, vec, vec],
        out_shape=[jax.ShapeDtypeStruct((seq, d), F32), jax.ShapeDtypeStruct((seq, d), MXU_DTYPE),
                   jax.ShapeDtypeStruct((1, d), F32), jax.ShapeDtypeStruct((1, d), F32)],
        compiler_params=_params("arbitrary"))(du, w_up_t, dz2, z, g, *after)


def _down_ln2_loss_bwd(act, w_down, x1, target, g, b, tm=512):
    seq, d = x1.shape
    kdim = act.shape[1]
    tm = min(tm, seq)

    def body(a_ref, w_ref, x_ref, t_ref, g_ref, b_ref, dz_ref, dzb_ref, loss_ref, dg_ref, db_ref):
        @pl.when(pl.program_id(0) == 0)
        def _():
            loss_ref[...] = jnp.zeros_like(loss_ref)
            dg_ref[...] = jnp.zeros_like(dg_ref)
            db_ref[...] = jnp.zeros_like(db_ref)

        gv = g_ref[...]
        z = DN_ALPHA * x_ref[...] + _dot(a_ref[...], w_ref[...], 1, 0)
        xh, r = _ln_stats(z)
        err = (xh * gv + b_ref[...]) - t_ref[...]
        loss_ref[...] += 0.5 * jnp.sum(jnp.mean(err * err, axis=-1, keepdims=True), axis=0, keepdims=True)
        dy = err * (1.0 / d)
        dg_ref[...] += jnp.sum(dy * xh, axis=0, keepdims=True)
        db_ref[...] += jnp.sum(dy, axis=0, keepdims=True)
        dz = _ln_bwd_math(dy, xh, r, gv)
        dz_ref[...] = dz
        dzb_ref[...] = dz.astype(dzb_ref.dtype)

    blk = pl.BlockSpec((tm, d), lambda i: (i, 0))
    vec = pl.BlockSpec((1, d), lambda i: (0, 0))
    return pl.pallas_call(
        body, name="down_ln2_loss_bwd", grid=(seq // tm,),
        in_specs=[pl.BlockSpec((tm, kdim), lambda i: (i, 0)),
                  pl.BlockSpec((kdim, d), lambda i: (0, 0), pipeline_mode=pl.Buffered(1)), blk, blk, vec, vec],
        out_specs=[blk, blk, pl.BlockSpec((1, LANES), lambda i: (0, 0)), vec, vec],
        out_shape=[jax.ShapeDtypeStruct((seq, d), F32), jax.ShapeDtypeStruct((seq, d), MXU_DTYPE),
                   jax.ShapeDtypeStruct((1, LANES), F32),
                   jax.ShapeDtypeStruct((1, d), F32), jax.ShapeDtypeStruct((1, d), F32)],
        compiler_params=_params("arbitrary"))(act, w_down, x1, target, g, b)


HALO = 16


def _conv_rows(e, w_ref, b_ref):
    y = b_ref[...] + w_ref[0:1, :] * pltpu.roll(e, 2, 0)
    y = y + w_ref[1:2, :] * pltpu.roll(e, 1, 0)
    return y + w_ref[2:3, :] * e


_GELU_C = math.sqrt(2.0 / math.pi)
_GELU_A = 0.044715


def _gelu(x):
    return 0.5 * x * (1.0 + jnp.tanh(_GELU_C * (x + _GELU_A * (x * x * x))))


CONV_TN = 256


def _ffn_interleave(a, axis):
    shp = a.shape
    a = a.reshape(shp[:axis] + (2, D_FF // CONV_TN, CONV_TN) + shp[axis + 1:])
    return jnp.swapaxes(a, axis, axis + 1).reshape(shp)


def _ffn_deinterleave(a, axis):
    shp = a.shape
    a = a.reshape(shp[:axis] + (D_FF // CONV_TN, 2, CONV_TN) + shp[axis + 1:])
    return jnp.swapaxes(a, axis, axis + 1).reshape(shp)


def _conv_gate_fwd(u, conv_w, conv_b, tm=1024):
    seq = u.shape[0]
    tm = min(tm, seq)
    tn = CONV_TN

    def body(u_ref, up_ref, w_ref, b_ref, o_ref):
        first = pl.program_id(0) == 0
        e = jnp.concatenate([jnp.where(first, 0.0, up_ref[...]), u_ref[...]], axis=0)
        y = _conv_rows(e, w_ref, b_ref)[HALO:]
        o_ref[...] = (_gelu(y[:, tn:]) * y[:, :tn]).astype(o_ref.dtype)

    hb = tm // HALO
    return pl.pallas_call(
        body, name="conv_gate_fwd", grid=(seq // tm, D_FF // tn),
        in_specs=[pl.BlockSpec((tm, 2 * tn), lambda i, j: (i, j)),
                  pl.BlockSpec((HALO, 2 * tn), lambda i, j: (jnp.maximum(i * hb - 1, 0), j)),
                  pl.BlockSpec((3, 2 * tn), lambda i, j: (0, j)), pl.BlockSpec((1, 2 * tn), lambda i, j: (0, j))],
        out_specs=pl.BlockSpec((tm, tn), lambda i, j: (i, j)), out_shape=jax.ShapeDtypeStruct((seq, D_FF), MXU_DTYPE),
        compiler_params=_params("parallel", "parallel"),
    )(u, u, conv_w, conv_b)


def _conv_gate_bwd(u, d_act, conv_w, conv_b, tm=1024):
    seq = u.shape[0]
    tm = min(tm, seq)
    tn = CONV_TN
    ni = seq // tm
    rows_e = tm + 2 * HALO

    def body(u_ref, up_ref, un_ref, da_ref, dan_ref, w_ref, b_ref, du_ref, dw_ref, db_ref):
        i = pl.program_id(1)
        first, last = i == 0, i == ni - 1

        @pl.when(i == 0)
        def _():
            dw_ref[...] = jnp.zeros_like(dw_ref)
            db_ref[...] = jnp.zeros_like(db_ref)

        e = jnp.concatenate([jnp.where(first, 0.0, up_ref[...]), u_ref[...], jnp.where(last, 0.0, un_ref[...])], axis=0)
        y = _conv_rows(e, w_ref, b_ref)
        ya, yg = y[:, :tn], y[:, tn:]
        dact = jnp.concatenate([jnp.zeros((HALO, tn), F32), da_ref[...].astype(F32),
                                jnp.where(last, 0.0, dan_ref[...].astype(F32))], axis=0)
        th = jnp.tanh(_GELU_C * (yg + _GELU_A * (yg * yg * yg)))
        gelu = 0.5 * yg * (1.0 + th)
        gelu_grad = 0.5 * (1.0 + th) + 0.5 * yg * (1.0 - th * th) * (_GELU_C * (1.0 + 3.0 * _GELU_A * (yg * yg)))
        dy = jnp.concatenate([dact * gelu, dact * ya * gelu_grad], axis=1)
        du = w_ref[2:3, :] * dy + w_ref[1:2, :] * pltpu.roll(dy, rows_e - 1, 0) + w_ref[0:1, :] * pltpu.roll(dy, rows_e - 2, 0)
        du_ref[...] = du[HALO:HALO + tm].astype(du_ref.dtype)
        dyt = dy[HALO:HALO + tm]
        dw_ref[0:1, :] += jnp.sum(dyt * pltpu.roll(e, 2, 0)[HALO:HALO + tm], axis=0, keepdims=True)
        dw_ref[1:2, :] += jnp.sum(dyt * pltpu.roll(e, 1, 0)[HALO:HALO + tm], axis=0, keepdims=True)
        dw_ref[2:3, :] += jnp.sum(dyt * e[HALO:HALO + tm], axis=0, keepdims=True)
        db_ref[...] += jnp.sum(dyt, axis=0, keepdims=True)

    hb = tm // HALO
    nh = seq // HALO
    prev = lambda j, i: (jnp.maximum(i * hb - 1, 0), j)
    nxt = lambda j, i: (jnp.minimum((i + 1) * hb, nh - 1), j)
    return pl.pallas_call(
        body, name="conv_gate_bwd", grid=(D_FF // tn, ni),
        in_specs=[pl.BlockSpec((tm, 2 * tn), lambda j, i: (i, j)), pl.BlockSpec((HALO, 2 * tn), prev),
                  pl.BlockSpec((HALO, 2 * tn), nxt), pl.BlockSpec((tm, tn), lambda j, i: (i, j)), pl.BlockSpec((HALO, tn), nxt),
                  pl.BlockSpec((3, 2 * tn), lambda j, i: (0, j)), pl.BlockSpec((1, 2 * tn), lambda j, i: (0, j))],
        out_specs=[pl.BlockSpec((tm, 2 * tn), lambda j, i: (i, j)), pl.BlockSpec((3, 2 * tn), lambda j, i: (0, j)),
                   pl.BlockSpec((1, 2 * tn), lambda j, i: (0, j))],
        out_shape=[jax.ShapeDtypeStruct((seq, 2 * D_FF), MXU_DTYPE), jax.ShapeDtypeStruct((3, 2 * D_FF), F32),
                   jax.ShapeDtypeStruct((1, 2 * D_FF), F32)],
        compiler_params=_params("parallel", "arbitrary"),
    )(u, u, u, d_act, d_act, conv_w, conv_b)


def _pad_heads(w, width):
    w = jnp.transpose(w, (1, 0, 2))
    return jnp.pad(w, ((0, 0), (0, 0), (0, LANES - width))).astype(MXU_DTYPE)


def _heads_major(a):
    return jnp.transpose(a, (1, 0, 2)).reshape(-1, a.shape[2])


def _heads_minor(a, heads):
    return jnp.transpose(a.reshape(heads, -1, a.shape[1]), (1, 0, 2))


_LATENT = Q_RANK + KV_RANK
_ROPE_AT = _LATENT + NOPE_DIM
_ROPE_END = _ROPE_AT + ROPE_DIM


def _split_pad_rows(w_t):
    z = lambda n: jnp.zeros((n, w_t.shape[1]), w_t.dtype)
    return jnp.concatenate([w_t[:_LATENT], z(_ROPE_AT - _LATENT), w_t[_LATENT:_LATENT + ROPE_DIM], z(DH_PART - _ROPE_END),
                            w_t[_LATENT + ROPE_DIM:]], axis=0)


def _split_unpad_rows(w_p):
    return jnp.concatenate([w_p[:_LATENT], w_p[_ROPE_AT:_ROPE_END], w_p[DH_PART:]], axis=0)


def _pad_w_o(w_o):
    mla = jnp.pad(w_o[:MLA_WIDTH].reshape(HEADS, HEAD_DIM, D_MODEL), ((0, 0), (0, LANES - HEAD_DIM), (0, 0)))
    return mla.reshape(HEADS * LANES, D_MODEL).astype(MXU_DTYPE), w_o[MLA_WIDTH:].astype(MXU_DTYPE)


def _unpad_w_o(d_mla, d_dil):
    return jnp.concatenate([d_mla.reshape(HEADS, LANES, D_MODEL)[:, :HEAD_DIM].reshape(MLA_WIDTH, D_MODEL), d_dil], axis=0)


def _row(v):
    return v.reshape(1, -1).astype(F32)


def _layer_grads(x0, target, cw, first_after=(), late_weights=None, on_grads=None):
    seq = x0.shape[0]
    ctab, stab = _rope_tables(seq)
    gq, gk = cw["g_cq"], cw["g_ckv"]
    wq, wk, wv = cw["wq"], cw["wk"], cw["wv"]
    notify = (lambda stage, grads: ()) if on_grads is None else on_grads

    h = _mm(x0, cw["w_in_t"], name="mm_h", tb=True, tm=1024, tn=IN_PAD, tk=1024, after=first_after)
    qf, kf, vp = _mla_prep(h, gq, gk, wq, wk, wv, ctab, stab)
    o_mla, o_mla_b, lse_mla = _mla_attn_fwd(qf, kf, vp)
    o_dil, o_dil_b, lse_dil = _dil_fwd(h)
    if late_weights is not None:
        cw = {**cw, **late_weights(o_mla_b)}
    cb = cw["conv_b"]
    z1, x1, x1b = _mix_ln1(o_mla_b, o_dil_b, cw["w_o_mla"], cw["w_o_dil"], x0, cw["ln1_g"], cw["ln1_b"])
    u = _mm(x1b, cw["w_up_t"], name="mm_up", tb=True, tm=512, tn=2 * D_FF, tk=1024)
    act = _conv_gate_fwd(u, cw["conv_w"], cb)
    dz2, dz2b, loss, d_ln2_g, d_ln2_b = _down_ln2_loss_bwd(act, cw["w_down"], x1, target, cw["ln2_g"], cw["ln2_b"])

    d_act = _mm(dz2b, cw["w_down"], name="mm_d_act", tb=True, out_dtype=MXU_DTYPE, tm=1024, tn=D_FF, tk=1024)
    d_w_down = _mm(act, dz2b, name="mm_dw_down", ta=True, out_dtype=MXU_DTYPE, tm=1408, tn=1024, tk=1024)
    du, d_conv_w, d_conv_b = _conv_gate_bwd(u, d_act, cw["conv_w"], cb)
    d_w_up_t = _mm(du, x1b, name="mm_dw_up", ta=True, out_dtype=MXU_DTYPE, tm=1408, tn=1024, tk=2048)
    grads = dict(w_up_t=d_w_up_t, w_down=d_w_down, conv_w=d_conv_w, conv_b=d_conv_b, ln2_g=d_ln2_g, ln2_b=d_ln2_b)
    dz1, dz1b, d_ln1_g, d_ln1_b = _dx1_ln1_bwd(du, cw["w_up_t"], dz2, z1, cw["ln1_g"], after=notify("ffn", grads))
    do_mla, do_dil, d_w_o_mla, d_w_o_dil = _w_o_bwd(dz1b, o_mla_b, o_dil_b, cw["w_o_mla"], cw["w_o_dil"])
    grads.update(w_o_mla=d_w_o_mla, w_o_dil=d_w_o_dil, ln1_g=d_ln1_g, ln1_b=d_ln1_b)
    dqf, dkf, dvf = _mla_attn_bwd(qf, kf, vp, o_mla, lse_mla, do_mla)
    dh_mla, d_wq, d_wk, d_wv, d_gq, d_gk = _mla_prep_bwd(h, gq, gk, wq, wk, wv, ctab, stab, dqf, dkf, dvf,
                                                          after=notify("w_o", grads))
    grads.update(wq=d_wq, wk=d_wk, wv=d_wv, g_cq=d_gq, g_ckv=d_gk, loss=loss)
    dq_dil, dk_dil, dv_dil = _dil_bwd(h, o_dil, lse_dil, do_dil, after=notify("mla", grads))
    dh = (dh_mla, dq_dil, dk_dil, dv_dil)
    grads.update(w_in_t=_mm_dw_in(dh, x0))
    grad_x = _mm_dx0(dh, cw["w_in_t"], dz1, after=notify("w_in", grads))
    return loss, grad_x, grads


def _all_gather(blocks, name):
    na = len(blocks)

    def body(*refs):
        ins, outs = refs[:na], refs[na:2 * na]
        send_sems, recv_sems, local_sems = refs[2 * na:]
        x, y, c = lax.axis_index("x"), lax.axis_index("y"), lax.axis_index("c")
        me, sibling = (x, y, c), (x, y, 1 - c)
        chips = [(1 - x, y), (x, 1 - y), (1 - x, 1 - y)]

        def slot(out, pos):
            return out.at[4 * pos[0] + 2 * pos[1] + pos[2]]

        def copy(a, k, block, to, src=None):
            return pltpu.make_async_remote_copy(
                src_ref=slot(outs[a], block) if src is None else src, dst_ref=slot(outs[a], block),
                send_sem=send_sems.at[7 * a + k], recv_sem=recv_sems.at[7 * a + k],
                device_id=to, device_id_type=pl.DeviceIdType.MESH)

        mine = [pltpu.make_async_copy(ins[a], slot(outs[a], me), local_sems.at[a]) for a in range(na)]
        for cp in mine:
            cp.start()
        first = []
        for a in range(na):
            first.append(copy(a, 0, me, sibling, src=ins[a]))
            first += [copy(a, 1 + j, me, (*chip, c), src=ins[a]) for j, chip in enumerate(chips)]
        for cp in first:
            cp.start()
        passed = []
        for j, chip in enumerate(chips):
            for a in range(na):
                copy(a, 1 + j, (*chip, c), me).wait_recv()
                cp = copy(a, 4 + j, (*chip, c), sibling)
                cp.start()
                passed.append(cp)
        for a in range(na):
            copy(a, 0, sibling, me).wait_recv()
            for j, chip in enumerate(chips):
                copy(a, 4 + j, (*chip, 1 - c), me).wait_recv()
        for cp in first + passed:
            cp.wait_send()
        for cp in mine:
            cp.wait()

    any_spec = pl.BlockSpec(memory_space=pl.ANY)
    return pl.pallas_call(
        body, name=name, in_specs=[any_spec] * na, out_specs=[any_spec] * na,
        out_shape=[jax.ShapeDtypeStruct((N_DEV,) + b.shape, b.dtype) for b in blocks],
        scratch_shapes=[pltpu.SemaphoreType.DMA((7 * na,)), pltpu.SemaphoreType.DMA((7 * na,)), pltpu.SemaphoreType.DMA((na,))],
    )(*blocks)


_HBM_SPEC = pl.BlockSpec(memory_space=pltpu.HBM)
_SEM_SPEC = pl.BlockSpec(memory_space=pltpu.SEMAPHORE)
_DATAFLOW = pltpu.CompilerParams(has_side_effects=pltpu.SideEffectType.DATAFLOW_SIDE_EFFECTING)


def _split_copies(ins, lands, send_sems, recv_sems, gather):
    x, y, c = lax.axis_index("x"), lax.axis_index("y"), lax.axis_index("c")
    me = 4 * x + 2 * y + c
    copies = []
    for a in range(len(ins)):
        for d in range(1, N_DEV):
            px, py, pc = x ^ (d >> 2), y ^ ((d >> 1) & 1), c ^ (d & 1)
            copies.append(pltpu.make_async_remote_copy(
                src_ref=ins[a] if gather else ins[a].at[4 * px + 2 * py + pc], dst_ref=lands[a].at[me],
                send_sem=send_sems.at[7 * a + d - 1], recv_sem=recv_sems.at[7 * a + d - 1],
                device_id=(px, py, pc), device_id_type=pl.DeviceIdType.MESH))
    return copies


def _send_start(srcs, gather, name):
    na = len(srcs)
    land_types = [pltpu.HBM(((N_DEV,) + s.shape) if gather else s.shape, s.dtype) for s in srcs]

    def body(*refs):
        ins, lands = refs[:na], refs[na:2 * na]
        send_sems, recv_sems, token = refs[2 * na], refs[2 * na + 1], refs[-1]
        for cp in _split_copies(ins, lands, send_sems, recv_sems, gather):
            cp.start()
        token[...] = jnp.zeros_like(token)

    hbm = lambda a: pltpu.with_memory_space_constraint(a, pltpu.HBM)
    outs = pl.pallas_call(
        body, name=name,
        out_shape=(pltpu.SemaphoreType.DMA((7 * na,)), pltpu.SemaphoreType.DMA((7 * na,)),
                   *[pltpu.HBM(s.shape, s.dtype) for s in srcs], *land_types, jax.ShapeDtypeStruct((8, LANES), F32)),
        in_specs=[_HBM_SPEC] * (2 * na),
        out_specs=(_SEM_SPEC, _SEM_SPEC, *[_HBM_SPEC] * (2 * na), pl.BlockSpec(memory_space=pltpu.VMEM)),
        input_output_aliases={i: 2 + i for i in range(2 * na)}, compiler_params=_DATAFLOW,
    )(*[hbm(s) for s in srcs], *[hbm(lax.empty(t.shape, t.dtype)) for t in land_types])
    return dict(send=outs[0], recv=outs[1], srcs=list(outs[2:2 + na]), lands=list(outs[2 + na:2 + 2 * na]), token=outs[-1],
                gather=gather)


def _send_wait(handle, after, name):
    na = len(handle["srcs"])
    gather = handle["gather"]
    after = list(after)

    def body(*refs):
        ins, lands = refs[:na], refs[na:2 * na]
        send_sems, recv_sems = refs[2 * na], refs[2 * na + 1]
        for cp in _split_copies(ins, lands, send_sems, recv_sems, gather):
            cp.wait_send()
            cp.wait_recv()

    both = handle["srcs"] + handle["lands"]
    outs = pl.pallas_call(
        body, name=name, out_shape=[pltpu.HBM(a.shape, a.dtype) for a in both],
        in_specs=[_HBM_SPEC] * (2 * na) + [_SEM_SPEC, _SEM_SPEC] + [_ANY_SPEC] * len(after),
        out_specs=[_HBM_SPEC] * (2 * na), input_output_aliases={i: i for i in range(2 * na)}, compiler_params=_DATAFLOW,
    )(*both, handle["send"], handle["recv"], *after)
    return list(outs[:na]), list(outs[na:])


def _sum_slots(p_ref):
    g = p_ref[0].astype(F32)
    for s in range(1, p_ref.shape[0]):
        g = g + p_ref[s].astype(F32)
    return g


def _adamw_refs(g, w_ref, m_ref, v_ref, g_out, d_out, m_out, v_out):
    c1 = 1.0 - ADAM_B1 ** ADAM_STEP
    c2 = 1.0 - ADAM_B2 ** ADAM_STEP
    m_new = ADAM_B1 * m_ref[...] + (1.0 - ADAM_B1) * g
    v_new = ADAM_B2 * v_ref[...] + (1.0 - ADAM_B2) * (g * g)
    g_out[...] = g
    m_out[...] = m_new
    v_out[...] = v_new
    d_out[...] = -ADAM_LR * ((m_new / c1) / (jnp.sqrt(v_new / c2) + ADAM_EPS) + ADAM_WD * w_ref[...])


def _adamw(parts, w, m, v, name):
    npart, r, n = parts.shape
    tr = r if r <= 256 else max(t for t in range(16, 257, 16) if r % t == 0)

    def body(p_ref, w_ref, m_ref, v_ref, g_out, d_out, m_out, v_out):
        _adamw_refs(_sum_slots(p_ref), w_ref, m_ref, v_ref, g_out, d_out, m_out, v_out)

    blk = pl.BlockSpec((tr, n), lambda i: (i, 0))
    shp = jax.ShapeDtypeStruct((r, n), F32)
    return pl.pallas_call(
        body, name=name, grid=(r // tr,), in_specs=[pl.BlockSpec((npart, tr, n), lambda i: (0, i, 0)), blk, blk, blk],
        out_specs=[blk] * 4, out_shape=[shp] * 4, compiler_params=_params("parallel"),
    )(parts, w, m, v)


def _adamw_small(parts, ws, ms, vs, loss_parts, name):
    n = len(parts)

    def body(*refs):
        ins, outs = refs[:4 * n + 1], refs[4 * n + 1:]
        for i in range(n):
            _adamw_refs(_sum_slots(ins[i]), ins[n + i], ins[2 * n + i], ins[3 * n + i], *outs[4 * i:4 * i + 4])
        outs[4 * n][...] = _sum_slots(ins[4 * n])

    out_shape = [jax.ShapeDtypeStruct(w.shape, F32) for w in ws for _ in range(4)]
    res = pl.pallas_call(body, name=name, out_shape=out_shape + [jax.ShapeDtypeStruct((1, LANES), F32)],
                         compiler_params=_params())(*parts, *ws, *ms, *vs, loss_parts)
    return [res[4 * i:4 * i + 4] for i in range(n)], res[4 * n]


REPLICATED = ("g_cq", "g_ckv", "w_uk", "w_uv", "ln1_g", "ln1_b", "conv_b", "ln2_g", "ln2_b")
ALL_WEIGHTS = ("w_in", "g_cq", "g_ckv", "w_uq", "w_uk", "w_uv", "w_o", "ln1_g", "ln1_b", "w_up", "conv_w", "conv_b",
               "w_down", "ln2_g", "ln2_b")


def kernel(x, w_in, g_cq, g_ckv, w_uq, w_uk, w_uv, w_o, ln1_g, ln1_b, w_up, conv_w, conv_b, w_down, ln2_g, ln2_b, loss_target, m_w_in, m_g_cq, m_g_ckv, m_w_uq, m_w_uk, m_w_uv, m_w_o, m_ln1_g, m_ln1_b, m_w_up, m_conv_w, m_conv_b, m_w_down, m_ln2_g, m_ln2_b, v_w_in, v_g_cq, v_g_ckv, v_w_uq, v_w_uk, v_w_uv, v_w_o, v_ln1_g, v_ln1_b, v_w_up, v_conv_w, v_conv_b, v_w_down, v_ln2_g, v_ln2_b):
    w = dict(w_in=w_in, g_cq=g_cq, g_ckv=g_ckv, w_uq=w_uq, w_uk=w_uk, w_uv=w_uv, w_o=w_o, ln1_g=ln1_g, ln1_b=ln1_b,
             w_up=w_up, conv_w=conv_w, conv_b=conv_b, w_down=w_down, ln2_g=ln2_g, ln2_b=ln2_b)
    m = dict(w_in=m_w_in, g_cq=m_g_cq, g_ckv=m_g_ckv, w_uq=m_w_uq, w_uk=m_w_uk, w_uv=m_w_uv, w_o=m_w_o, ln1_g=m_ln1_g,
             ln1_b=m_ln1_b, w_up=m_w_up, conv_w=m_conv_w, conv_b=m_conv_b, w_down=m_w_down, ln2_g=m_ln2_g, ln2_b=m_ln2_b)
    v = dict(w_in=v_w_in, g_cq=v_g_cq, g_ckv=v_g_ckv, w_uq=v_w_uq, w_uk=v_w_uk, w_uv=v_w_uv, w_o=v_w_o, ln1_g=v_ln1_g,
             ln1_b=v_ln1_b, w_up=v_w_up, conv_w=v_conv_w, conv_b=v_conv_b, w_down=v_w_down, ln2_g=v_ln2_g, ln2_b=v_ln2_b)
    me = 4 * lax.axis_index("x") + 2 * lax.axis_index("y") + lax.axis_index("c")
    wire = lambda a: a.astype(WIRE_DTYPE)
    pad_taps = lambda a: jnp.pad(a, ((0, 8 - a.shape[0]), (0, 0)))

    own_slot = lambda buf, block: lax.dynamic_update_index_in_dim(buf, block, me, 0)
    blocks = lambda a: wire(a).reshape((N_DEV, a.shape[0] // N_DEV) + a.shape[1:])

    g_in, g_uq, g_conv = _all_gather(
        [wire(w_in).T, _heads_major(wire(w_uq)), pad_taps(conv_w)],
        "gather_weights")
    late = _send_start([wire(w_o), wire(w_up).T, wire(w_down)], True, "gather_late_start")
    r_uq_dev, e_uq = w_uq.shape[0], w_uq.shape[2]
    wq = jnp.transpose(g_uq.reshape(N_DEV, HEADS, r_uq_dev, e_uq), (1, 0, 2, 3)).reshape(HEADS, Q_RANK, e_uq)
    cw = dict(
        w_in_t=_split_pad_rows(g_in.reshape(-1, D_MODEL)).astype(MXU_DTYPE),
        wq=jnp.pad(wq, ((0, 0), (0, 0), (0, LANES - e_uq))).astype(MXU_DTYPE),
        wk=_pad_heads(w_uk, NOPE_DIM), wv=_pad_heads(w_uv, HEAD_DIM),
        conv_w=_ffn_interleave(jnp.transpose(g_conv[:, :conv_w.shape[0]], (1, 0, 2)).reshape(conv_w.shape[0], -1), 1),
        g_cq=_row(g_cq), g_ckv=_row(g_ckv), ln1_g=_row(ln1_g), ln1_b=_row(ln1_b), conv_b=_ffn_interleave(_row(conv_b), 1),
        ln2_g=_row(ln2_g), ln2_b=_row(ln2_b))

    def late_weights(after):
        own, landed = _send_wait(late, [after], "gather_late_wait")
        g_o, g_up, g_down = [own_slot(buf, blk) for buf, blk in zip(landed, own)]
        w_o_mla, w_o_dil = _pad_w_o(g_o.reshape(-1, D_MODEL))
        return dict(w_o_mla=w_o_mla, w_o_dil=w_o_dil, w_up_t=_ffn_interleave(g_up.reshape(-1, D_MODEL), 0).astype(MXU_DTYPE),
                    w_down=g_down.reshape(-1, D_MODEL).astype(MXU_DTYPE))

    sent = {}

    def on_grads(stage, g):
        if stage == "ffn":
            sent[stage] = [_send_start([blocks(_ffn_deinterleave(g["w_up_t"], 0)), blocks(g["w_down"])], False, "exchange_ffn_start")]
        elif stage == "w_o":
            sent[stage] = [_send_start([blocks(_unpad_w_o(g["w_o_mla"], g["w_o_dil"]))], False, "exchange_w_o_start")]
        elif stage == "mla":
            d_uq = wire(jnp.transpose(g["wq"][:, :, :e_uq].reshape(HEADS, N_DEV, r_uq_dev, e_uq), (1, 0, 2, 3))
                        ).reshape(N_DEV, HEADS * r_uq_dev, e_uq)
            dense = lambda a, width: wire(a[:, :, :width]).reshape(-1, LANES)
            small = dict(g_cq=g["g_cq"], g_ckv=g["g_ckv"], w_uk=dense(g["wk"], NOPE_DIM), w_uv=dense(g["wv"], HEAD_DIM),
                         ln1_g=g["ln1_g"], ln1_b=g["ln1_b"], conv_b=_ffn_deinterleave(g["conv_b"], 1), ln2_g=g["ln2_g"],
                         ln2_b=g["ln2_b"])
            sent[stage] = [_send_start([d_uq], False, "exchange_w_uq_start"),
                           _send_start([small[n] for n in REPLICATED] + [_ffn_deinterleave(g["conv_w"], 1), g["loss"]],
                                       True, "gather_small_start")]
        else:
            sent[stage] = [_send_start([blocks(_split_unpad_rows(g["w_in_t"]))], False, "exchange_w_in_start")]
        return [h["token"] for h in sent[stage]]

    _, grad_x, _ = _layer_grads(x[0], loss_target[0], cw, [late["token"]], late_weights, on_grads)

    def landed(handle, after, name):
        own, got = _send_wait(handle, after, name)
        pick = (lambda a: a) if handle["gather"] else (lambda a: lax.dynamic_index_in_dim(a, me, 0, keepdims=False))
        return [own_slot(buf, pick(src)) for buf, src in zip(got, own)]

    out = {}

    def update(name, parts, view=None):
        to2d = {None: lambda a: a, "t": lambda a: a.T, "heads": _heads_major}[view]
        back = {None: lambda a: a, "t": lambda a: a.T, "heads": lambda a: _heads_minor(a, HEADS)}[view]
        res = _adamw(parts, to2d(w[name]), to2d(m[name]), to2d(v[name]), "adamw_" + name)
        for kind, a in zip(("grad", "delta", "new_m", "new_v"), res):
            out[kind, name] = back(a)
        return res[0]

    r_up, r_down = landed(sent["ffn"][0], [grad_x], "exchange_ffn_wait")
    (r_o,) = landed(sent["w_o"][0], [grad_x], "exchange_w_o_wait")
    (r_uq,) = landed(sent["mla"][0], [grad_x], "exchange_w_uq_wait")
    *rep_all, cw_all, loss_all = landed(sent["mla"][1], [grad_x], "gather_small_wait")
    done = [update("w_up", r_up, "t"), update("w_down", r_down), update("w_o", r_o), update("w_uq", r_uq, "heads")]
    heads_major = ("w_uk", "w_uv")
    two_d = lambda n, a: _heads_major(a) if n in heads_major else a.reshape(1, -1)
    rep_all = [p.reshape(N_DEV, -1, w[n].shape[2]) if n in heads_major else p for n, p in zip(REPLICATED, rep_all)]
    res, loss_sum = _adamw_small(rep_all, *[[two_d(n, d[n]) for n in REPLICATED] for d in (w, m, v)], loss_all, "adamw_replicated")
    for n, quad in zip(REPLICATED, res):
        for kind, a in zip(("grad", "delta", "new_m", "new_v"), quad):
            out[kind, n] = _heads_minor(a, HEADS) if n in heads_major else a.reshape(w[n].shape)
    loss = loss_sum[0, 0]
    ncw = conv_w.shape[1]
    done += [loss_sum, update("conv_w", lax.dynamic_slice_in_dim(cw_all[:, :conv_w.shape[0]], me * ncw, ncw, axis=2))]
    (r_in,) = landed(sent["w_in"][0], done, "exchange_w_in_wait")
    update("w_in", r_in, "t")

    return (loss, grad_x[None], *[out[kind, n] for kind in ("grad", "delta", "new_m", "new_v") for n in ALL_WEIGHTS])
```

```python
import math

import jax
import jax.numpy as jnp
import numpy as np
from jax import lax
from jax.experimental import pallas as pl
from jax.experimental.pallas import tpu as pltpu

F32 = jnp.float32
MXU_DTYPE = jnp.bfloat16
WIRE_DTYPE = jnp.bfloat16

N_DEV = 8
D_MODEL = 1024
HEADS = 8
HEAD_DIM = 64
LANES = 128
Q_RANK, KV_RANK, ROPE_DIM, NOPE_DIM = 256, 128, 32, 64
DIL_WIDTH = HEADS * HEAD_DIM
MLA_WIDTH = HEADS * HEAD_DIM
IN_PAD = 2048
DH_PART = IN_PAD // 4
D_FF = 2816
ROPE_THETA = 10000.0
DIL_PAIRS = ((128, 1), (512, 4), (2048, 16))
DIL_BLOCK = 128
DN_ALPHA = 2.0 ** 0.25
LN_EPS = 1e-5
RMS_EPS = 1e-6
ONES_LANE = HEAD_DIM
MLA_SCALE = 1.0 / math.sqrt(NOPE_DIM + ROPE_DIM)
MLA_SCALE_LOG2 = MLA_SCALE * math.log2(math.e)
MLA_BWD_SPLITS = 2
MLA_BWD_SPLITS_DIAGONAL = 4
MLA_FWD_SPLITS_DIAGONAL = 2
DIL_SCALE = 1.0 / math.sqrt(HEAD_DIM)
ALIBI_SLOPES = tuple(2.0 ** (-8.0 * (h + 1) / HEADS) for h in range(HEADS))
NEG_BIG = -1e30
ADAM_LR, ADAM_B1, ADAM_B2, ADAM_EPS, ADAM_WD, ADAM_STEP = 0.001, 0.9, 0.999, 1e-08, 0.01, 10
VMEM_LIMIT = 48 * 1024 * 1024


def _params(*sem):
    return pltpu.CompilerParams(dimension_semantics=sem or None, vmem_limit_bytes=VMEM_LIMIT)


def _dot(a, b, ca, cb):
    return lax.dot_general(a, b, (((ca,), (cb,)), ((), ())), preferred_element_type=F32)


_ANY_SPEC = pl.BlockSpec(memory_space=pl.ANY)


def _mm(a, b, *, name, tm, tn, tk, ta=False, tb=False, out_dtype=F32, after=()):
    m, k = (a.shape[1], a.shape[0]) if ta else a.shape
    n = b.shape[0] if tb else b.shape[1]
    assert (b.shape[1] if tb else b.shape[0]) == k
    tm, tn, tk = min(tm, m), min(tn, n), min(tk, k)
    assert m % tm == 0 and n % tn == 0 and k % tk == 0, (name, m, n, k, tm, tn, tk)
    nk = k // tk
    a_spec = (pl.BlockSpec((tk, tm), lambda i, j, kk: (kk, i)) if ta
              else pl.BlockSpec((tm, tk), lambda i, j, kk: (i, kk)))
    b_mode = dict(pipeline_mode=pl.Buffered(1)) if (tn == n and tk == k) else {}
    b_spec = (pl.BlockSpec((tn, tk), lambda i, j, kk: (j, kk), **b_mode) if tb
              else pl.BlockSpec((tk, tn), lambda i, j, kk: (kk, j), **b_mode))
    o_spec = pl.BlockSpec((tm, tn), lambda i, j, kk: (i, j))
    n_in = 2 + len(after)
    ca, cb = (0 if ta else 1), (1 if tb else 0)

    def body(*refs):
        a_ref, b_ref, o_ref = refs[0], refs[1], refs[n_in]
        part = _dot(a_ref[...].astype(MXU_DTYPE), b_ref[...].astype(MXU_DTYPE), ca, cb)
        if nk == 1:
            o_ref[...] = part.astype(o_ref.dtype)
            return
        acc_ref = refs[-1]
        kk = pl.program_id(2)

        @pl.when(kk == 0)
        def _():
            acc_ref[...] = part

        @pl.when(kk > 0)
        def _():
            acc_ref[...] += part

        @pl.when(kk == nk - 1)
        def _():
            o_ref[...] = acc_ref[...].astype(o_ref.dtype)

    return pl.pallas_call(
        body, name=name, grid=(m // tm, n // tn, nk), in_specs=[a_spec, b_spec] + [_ANY_SPEC] * len(after), out_specs=o_spec,
        out_shape=jax.ShapeDtypeStruct((m, n), out_dtype),
        scratch_shapes=[pltpu.VMEM((tm, tn), F32)] if nk > 1 else [],
        compiler_params=_params("parallel", "parallel", "arbitrary"),
    )(a, b, *after)


def _w_o_bwd(dz, o_mla, o_dil, w_o_mla, w_o_dil, tm=512):
    seq, d = dz.shape
    tm = min(tm, seq)
    nstep = seq // tm

    def body(a_ref, om_ref, od_ref, wm_ref, wd_ref, dom_ref, dod_ref, dwm_ref, dwd_ref, accm_ref, accd_ref):
        step = pl.program_id(0)

        @pl.when(step == 0)
        def _():
            accm_ref[...] = jnp.zeros_like(accm_ref)
            accd_ref[...] = jnp.zeros_like(accd_ref)

        a = a_ref[...]
        for hd in range(HEADS):
            rows = slice(LANES * hd, LANES * (hd + 1))
            dom_ref[hd] = _dot(a, wm_ref[rows, :], 1, 1)
            accm_ref[rows, :] += _dot(om_ref[hd], a, 0, 0)
        dod_ref[...] = _dot(a, wd_ref[...], 1, 1)
        accd_ref[...] += _dot(od_ref[...], a, 0, 0)

        @pl.when(step == nstep - 1)
        def _():
            dwm_ref[...] = accm_ref[...].astype(dwm_ref.dtype)
            dwd_ref[...] = accd_ref[...].astype(dwd_ref.dtype)

    once = dict(pipeline_mode=pl.Buffered(1))
    return pl.pallas_call(
        body, name="w_o_bwd", grid=(nstep,),
        in_specs=[pl.BlockSpec((tm, d), lambda i: (i, 0)), pl.BlockSpec((HEADS, tm, LANES), lambda i: (0, i, 0)),
                  pl.BlockSpec((tm, DIL_WIDTH), lambda i: (i, 0)), pl.BlockSpec((HEADS * LANES, d), lambda i: (0, 0), **once),
                  pl.BlockSpec((DIL_WIDTH, d), lambda i: (0, 0), **once)],
        out_specs=[pl.BlockSpec((HEADS, tm, LANES), lambda i: (0, i, 0)), pl.BlockSpec((tm, DIL_WIDTH), lambda i: (i, 0)),
                   pl.BlockSpec((HEADS * LANES, d), lambda i: (0, 0)), pl.BlockSpec((DIL_WIDTH, d), lambda i: (0, 0))],
        out_shape=[jax.ShapeDtypeStruct((HEADS, seq, LANES), F32), jax.ShapeDtypeStruct((seq, DIL_WIDTH), F32),
                   jax.ShapeDtypeStruct((HEADS * LANES, d), MXU_DTYPE), jax.ShapeDtypeStruct((DIL_WIDTH, d), MXU_DTYPE)],
        scratch_shapes=[pltpu.VMEM((HEADS * LANES, d), F32), pltpu.VMEM((DIL_WIDTH, d), F32)],
        compiler_params=_params("arbitrary"),
    )(dz, o_mla, o_dil, w_o_mla, w_o_dil)


def _rope_tables(seq):
    half = ROPE_DIM // 2
    f32 = np.float32
    freqs = np.power(f32(ROPE_THETA), -np.arange(half, dtype=f32) / f32(half))
    ang = np.arange(seq, dtype=f32)[:, None] * freqs[None, :]
    cos, sin = np.cos(ang, dtype=f32), np.sin(ang, dtype=f32)
    one = np.ones((seq, NOPE_DIM), f32)
    tail = np.ones((seq, LANES - NOPE_DIM - ROPE_DIM), f32)
    ctab = np.concatenate([one, cos, cos, tail], axis=1)
    stab = np.concatenate([0 * one, -sin, sin, 0 * tail], axis=1)
    return jnp.asarray(ctab), jnp.asarray(stab)


def _rope_swap(t):
    lane = lax.broadcasted_iota(jnp.int32, t.shape, 1)
    half = ROPE_DIM // 2
    return jnp.where(lane < NOPE_DIM + half, pltpu.roll(t, LANES - half, 1), pltpu.roll(t, half, 1))


def _rope(t, ctab, stab):
    return t * ctab + _rope_swap(t) * stab


def _rope_inv(t, ctab, stab):
    return t * ctab - _rope_swap(t) * stab


def _rms(x, g):
    r = lax.rsqrt(jnp.mean(x * x, axis=-1, keepdims=True) + RMS_EPS)
    xh = x * r
    return xh, r, xh * g


def _mla_prep(h, g_cq, g_ckv, wq, wk, wv, ctab, stab, tm=512):
    seq = h.shape[0]
    tm = min(tm, seq)

    def body(h_ref, gq_ref, gk_ref, wq_ref, wk_ref, wv_ref, c_ref, s_ref, q_out, k_out, v_out):
        hb = h_ref[...]
        ctab_, stab_ = c_ref[...], s_ref[...]
        _, _, cqn = _rms(hb[:, :Q_RANK], gq_ref[...])
        _, _, ckn = _rms(hb[:, Q_RANK:Q_RANK + KV_RANK], gk_ref[...])
        cqn = cqn.astype(MXU_DTYPE)
        ckn = ckn.astype(MXU_DTYPE)
        krr = _rope(hb[:, Q_RANK + KV_RANK:], ctab_, stab_)
        ones_lane = (lax.broadcasted_iota(jnp.int32, (1, LANES), 1) == ONES_LANE).astype(F32)
        for hd in range(HEADS):
            q = _dot(cqn, wq_ref[hd], 1, 0)
            q_out[hd] = _rope(q, ctab_, stab_).astype(q_out.dtype)
            k_out[hd] = (_dot(ckn, wk_ref[hd], 1, 0) + krr).astype(k_out.dtype)
            v_out[hd] = (_dot(ckn, wv_ref[hd], 1, 0) + ones_lane).astype(v_out.dtype)

    full = lambda *shape: pl.BlockSpec(shape, lambda i: (0,) * len(shape))
    slab = pl.BlockSpec((HEADS, tm, LANES), lambda i: (0, i, 0))
    shp = jax.ShapeDtypeStruct((HEADS, seq, LANES), MXU_DTYPE)
    return pl.pallas_call(
        body, name="mla_prep", grid=(seq // tm,),
        in_specs=[pl.BlockSpec((tm, DH_PART), lambda i: (i, 0)), full(1, Q_RANK), full(1, KV_RANK),
                  full(HEADS, Q_RANK, LANES), full(HEADS, KV_RANK, LANES), full(HEADS, KV_RANK, LANES),
                  pl.BlockSpec((tm, LANES), lambda i: (i, 0)), pl.BlockSpec((tm, LANES), lambda i: (i, 0))],
        out_specs=[slab, slab, slab], out_shape=[shp, shp, shp],
        compiler_params=_params("parallel"),
    )(h, g_cq, g_ckv, wq, wk, wv, ctab, stab)


def _mla_prep_bwd(h, g_cq, g_ckv, wq, wk, wv, ctab, stab, dq, dk, dv, tm=512, after=()):
    seq = h.shape[0]
    tm = min(tm, seq)
    n_after = len(after)

    def body(h_ref, gq_ref, gk_ref, wq_ref, wk_ref, wv_ref, c_ref, s_ref, dq_ref, dk_ref, dv_ref, *rest):
        dh_ref, dwq_ref, dwk_ref, dwv_ref, dgq_ref, dgk_ref = rest[n_after:]

        @pl.when(pl.program_id(0) == 0)
        def _():
            for r in (dwq_ref, dwk_ref, dwv_ref, dgq_ref, dgk_ref):
                r[...] = jnp.zeros_like(r)

        hb = h_ref[...]
        ctab_, stab_ = c_ref[...], s_ref[...]
        gq, gk = gq_ref[...], gk_ref[...]
        xq, rq, cqn = _rms(hb[:, :Q_RANK], gq)
        xk, rk, ckn = _rms(hb[:, Q_RANK:Q_RANK + KV_RANK], gk)
        cqn = cqn.astype(MXU_DTYPE)
        ckn = ckn.astype(MXU_DTYPE)
        d_cqn = jnp.zeros((tm, Q_RANK), F32)
        d_ckn = jnp.zeros((tm, KV_RANK), F32)
        d_krr = jnp.zeros((tm, LANES), F32)
        for hd in range(HEADS):
            dqh = _rope_inv(dq_ref[hd], ctab_, stab_).astype(MXU_DTYPE)
            d_cqn += _dot(dqh, wq_ref[hd], 1, 1)
            dwq_ref[hd] += _dot(cqn, dqh, 0, 0)
            dkh = dk_ref[hd]
            d_krr += dkh
            dkh = dkh.astype(MXU_DTYPE)
            d_ckn += _dot(dkh, wk_ref[hd], 1, 1)
            dwk_ref[hd] += _dot(ckn, dkh, 0, 0)
            dvh = dv_ref[hd].astype(MXU_DTYPE)
            d_ckn += _dot(dvh, wv_ref[hd], 1, 1)
            dwv_ref[hd] += _dot(ckn, dvh, 0, 0)
        lane = lax.broadcasted_iota(jnp.int32, (tm, LANES), 1)
        rot = (lane >= NOPE_DIM) & (lane < NOPE_DIM + ROPE_DIM)
        d_kr = jnp.where(rot, _rope_inv(jnp.where(rot, d_krr, 0.0), ctab_, stab_), 0.0)

        def rms_bwd(dy, xh, r, g, dg_ref):
            dg_ref[...] += jnp.sum(dy * xh, axis=0, keepdims=True)
            dxh = dy * g
            return r * (dxh - xh * jnp.mean(dxh * xh, axis=-1, keepdims=True))

        d_cq = rms_bwd(d_cqn, xq, rq, gq, dgq_ref)
        d_ck = rms_bwd(d_ckn, xk, rk, gk, dgk_ref)
        dh_ref[...] = jnp.concatenate([d_cq, d_ck, d_kr], axis=1).astype(dh_ref.dtype)

    full = lambda *shape: pl.BlockSpec(shape, lambda i: (0,) * len(shape))
    slab = pl.BlockSpec((HEADS, tm, LANES), lambda i: (0, i, 0))
    return pl.pallas_call(
        body, name="mla_prep_bwd", grid=(seq // tm,),
        in_specs=[pl.BlockSpec((tm, DH_PART), lambda i: (i, 0)), full(1, Q_RANK), full(1, KV_RANK),
                  full(HEADS, Q_RANK, LANES), full(HEADS, KV_RANK, LANES), full(HEADS, KV_RANK, LANES),
                  pl.BlockSpec((tm, LANES), lambda i: (i, 0)), pl.BlockSpec((tm, LANES), lambda i: (i, 0)),
                  slab, slab, slab] + [_ANY_SPEC] * n_after,
        out_specs=[pl.BlockSpec((tm, DH_PART), lambda i: (i, 0)), full(HEADS, Q_RANK, LANES), full(HEADS, KV_RANK, LANES),
                   full(HEADS, KV_RANK, LANES), full(1, Q_RANK), full(1, KV_RANK)],
        out_shape=[jax.ShapeDtypeStruct((seq, DH_PART), MXU_DTYPE), jax.ShapeDtypeStruct((HEADS, Q_RANK, LANES), F32),
                   jax.ShapeDtypeStruct((HEADS, KV_RANK, LANES), F32), jax.ShapeDtypeStruct((HEADS, KV_RANK, LANES), F32),
                   jax.ShapeDtypeStruct((1, Q_RANK), F32), jax.ShapeDtypeStruct((1, KV_RANK), F32)],
        compiler_params=_params("arbitrary"),
    )(h, g_cq, g_ckv, wq, wk, wv, ctab, stab, dq, dk, dv, *after)


def _mla_attn_fwd(q, k, v, t=1024):
    _, seq, _ = q.shape
    t = min(t, seq)

    def body(q_ref, k_ref, v_ref, o_ref, ob_ref, lse_ref, m_ref, acc_ref, s_ref):
        i = pl.program_id(1)
        qb = q_ref[...]
        m_ref[...] = jnp.full_like(m_ref, NEG_BIG)
        acc_ref[...] = jnp.zeros_like(acc_ref)

        def scores(j):
            return _dot(qb, k_ref[pl.ds(pl.multiple_of(j * t, t), t), :], 1, 1) * MLA_SCALE_LOG2

        def softmax_pv(j, s, rows=slice(None), mask=None):
            vb = v_ref[pl.ds(pl.multiple_of(j * t, t), s.shape[1]), :]
            if mask is not None:
                s = jnp.where(mask, s, NEG_BIG)
            m_old = m_ref[rows, :]
            m_new = jnp.maximum(m_old, jnp.max(s, axis=1, keepdims=True))
            p = jnp.exp2(s - m_new)
            a = jnp.exp2(m_old - m_new)
            acc_ref[rows, :] = a * acc_ref[rows, :] + _dot(p.astype(MXU_DTYPE), vb, 1, 0)
            m_ref[rows, :] = m_new

        def softmax_pv_diagonal(j):
            th = t // MLA_FWD_SPLITS_DIAGONAL
            for hf in range(MLA_FWD_SPLITS_DIAGONAL):
                nk = (hf + 1) * th
                row = lax.broadcasted_iota(jnp.int32, (th, nk), 0) + hf * th
                rows = slice(hf * th, (hf + 1) * th)
                softmax_pv(j, s_ref[rows, 0:nk], rows, row >= lax.broadcasted_iota(jnp.int32, (th, nk), 1))

        s_ref[...] = scores(0)

        def loop_body(j, c):
            s_next = scores(j + 1)
            softmax_pv(j, s_ref[...])
            s_ref[...] = s_next
            return c

        lax.fori_loop(0, i, loop_body, 0)
        softmax_pv_diagonal(i)
        acc = acc_ref[...]
        l = acc[:, ONES_LANE:ONES_LANE + 1]
        o = jnp.where(lax.broadcasted_iota(jnp.int32, acc.shape, 1) < HEAD_DIM, acc * (1.0 / l), 0.0)
        o_ref[...] = o
        ob_ref[...] = o.astype(ob_ref.dtype)
        lse_ref[...] = jnp.broadcast_to(m_ref[...] + jnp.log2(l), lse_ref.shape)

    blk = pl.BlockSpec((None, t, LANES), lambda h, i: (h, i, 0))
    whole = pl.BlockSpec((None, seq, LANES), lambda h, i: (h, 0, 0))
    shp = jax.ShapeDtypeStruct((HEADS, seq, LANES), F32)
    return pl.pallas_call(
        body, name="mla_attn_fwd", grid=(HEADS, seq // t),
        in_specs=[blk, whole, whole], out_specs=[blk, blk, blk],
        out_shape=[shp, jax.ShapeDtypeStruct((HEADS, seq, LANES), MXU_DTYPE), shp],
        scratch_shapes=[pltpu.VMEM((t, 1), F32), pltpu.VMEM((t, LANES), F32), pltpu.VMEM((t, t), F32)],
        compiler_params=_params("parallel", "arbitrary"),
    )(q, k, v)


def _mla_attn_bwd(q, k, v, o, lse, do, t=1024):
    _, seq, _ = q.shape
    t = min(t, seq)
    nb = seq // t

    def body(q_ref, k_ref, v_ref, o_ref, lse_ref, do_ref, dq_ref, dk_ref, dv_ref, dl_ref, dka_ref, dva_ref):
        dq_ref[...] = jnp.zeros_like(dq_ref)

        def delta_body(i, c):
            rows = pl.ds(pl.multiple_of(i * t, t), t)
            dl_ref[rows, :] = jnp.sum(do_ref[rows, :] * o_ref[rows, :], axis=1, keepdims=True)
            return c

        lax.fori_loop(0, nb, delta_body, 0)

        def kblock(j, c):
            krows = pl.ds(pl.multiple_of(j * t, t), t)
            kb = k_ref[krows, :]
            vb = v_ref[krows, :]
            dka_ref[...] = jnp.zeros_like(dka_ref)
            dva_ref[...] = jnp.zeros_like(dva_ref)

            def qstep(i, masked):
                ns = MLA_BWD_SPLITS_DIAGONAL if masked else MLA_BWD_SPLITS
                th = t // ns
                rows = [pl.ds(pl.multiple_of(i * t + hf * th, th), th) for hf in range(ns)]
                qs = [q_ref[r, :] for r in rows]
                dos = [do_ref[r, :].astype(MXU_DTYPE) for r in rows]
                nkeys = [(hf + 1) * th if masked else t for hf in range(ns)]
                ss = [_dot(qs[hf], kb[:nkeys[hf]], 1, 1) * MLA_SCALE_LOG2 for hf in range(ns)]
                dps = [_dot(dos[hf], vb[:nkeys[hf]], 1, 1) for hf in range(ns)]
                for hf in range(ns):
                    s, nk = ss[hf], nkeys[hf]
                    if masked:
                        row = lax.broadcasted_iota(jnp.int32, (th, nk), 0) + hf * th
                        s = jnp.where(row >= lax.broadcasted_iota(jnp.int32, (th, nk), 1), s, NEG_BIG)
                    p = jnp.exp2(s - lse_ref[rows[hf], 0:1])
                    dva_ref[0:nk, :] += _dot(p.astype(MXU_DTYPE), dos[hf], 0, 0)
                    ds = (p * (dps[hf] - dl_ref[rows[hf], :]) * MLA_SCALE).astype(MXU_DTYPE)
                    dka_ref[0:nk, :] += _dot(ds, qs[hf], 0, 0)
                    dq_ref[rows[hf], :] += _dot(ds, kb[:nk], 1, 0)

            qstep(j, True)

            def qloop(i, c2):
                qstep(i, False)
                return c2

            lax.fori_loop(j + 1, nb, qloop, 0)
            dk_ref[krows, :] = dka_ref[...]
            dv_ref[krows, :] = dva_ref[...]
            return c

        lax.fori_loop(0, nb, kblock, 0)

    whole = pl.BlockSpec((None, seq, LANES), lambda h: (h, 0, 0))
    shp = jax.ShapeDtypeStruct((HEADS, seq, LANES), F32)
    return pl.pallas_call(
        body, name="mla_attn_bwd", grid=(HEADS,),
        in_specs=[whole] * 6, out_specs=[whole] * 3, out_shape=[shp] * 3,
        scratch_shapes=[pltpu.VMEM((seq, 1), F32), pltpu.VMEM((t, LANES), F32), pltpu.VMEM((t, LANES), F32)],
        compiler_params=_params("parallel"),
    )(q, k, v, o, lse, do)


DIL_CHUNK = DIL_BLOCK * max(d for _, d in DIL_PAIRS)
DIL_PAIR_LANES = 2 * HEAD_DIM
assert DIL_PAIR_LANES == LANES
DIL_UNROLL_FWD = 16
DIL_UNROLL_BWD = 8


def _dil_bias_tables(hp, dil):
    b = DIL_BLOCK
    iq = lax.broadcasted_iota(jnp.int32, (b, 2 * b), 0)
    ik = lax.broadcasted_iota(jnp.int32, (b, 2 * b), 1)
    off = iq + b - ik
    band = (off >= 0) & (off <= b)
    dist = (off * dil).astype(F32)
    every, first = [], []
    for hh in range(2):
        slope = jnp.where(hp == 0, ALIBI_SLOPES[hh], jnp.where(hp == 1, ALIBI_SLOPES[2 + hh],
                          jnp.where(hp == 2, ALIBI_SLOPES[4 + hh], ALIBI_SLOPES[6 + hh]))).astype(F32)
        bias = -slope * dist
        every.append(jnp.where(band, bias, NEG_BIG))
        first.append(jnp.where(band & (ik >= b), bias, NEG_BIG))
    return jnp.concatenate(every, axis=0), jnp.concatenate(first, axis=0)


def _dil_rows(start, dil):
    return pl.ds(start, DIL_BLOCK) if dil == 1 else pl.ds(start, DIL_BLOCK, stride=dil)


def _dil_block_pos(blk, c, dil):
    sc, r = blk // dil, blk % dil
    q0 = sc * (DIL_BLOCK * dil) + r
    kcur0 = c * DIL_CHUNK + q0
    first = kcur0 < DIL_BLOCK * dil
    kprev0 = jnp.where(first, kcur0, kcur0 - DIL_BLOCK * dil)
    return q0, kcur0, kprev0, first


def _pair_cols(hh):
    return slice(HEAD_DIM * hh, HEAD_DIM * (hh + 1))


def _first_head_lanes(shape):
    return lax.broadcasted_iota(jnp.int32, shape, 1) < HEAD_DIM


def _stack_pair(t):
    first = _first_head_lanes(t.shape)
    return jnp.concatenate([jnp.where(first, t, 0.0), jnp.where(first, 0.0, t)], axis=0).astype(MXU_DTYPE)


def _unstack_pair(t):
    rows = t.shape[0] // 2
    return jnp.where(_first_head_lanes((rows, t.shape[1])), t[:rows], t[rows:])


def _pair_column(t):
    return jnp.concatenate([t[:, 0:1], t[:, HEAD_DIM:HEAD_DIM + 1]], axis=0)


def _dil_fwd(h):
    seq = h.shape[0]
    assert seq % DIL_CHUNK == 0
    nblk = DIL_CHUNK // DIL_BLOCK
    rc = 256

    def body(q_ref, k_ref, v_ref, o_ref, ob_ref, lse_ref, *scr):
        o_scr, l_scr = scr[:3], scr[3:]
        hp, c = pl.program_id(0), pl.program_id(1)
        for bi, (_, dil) in enumerate(DIL_PAIRS):
            tables = _dil_bias_tables(hp, dil)

            def block(blk, carry, bi=bi, dil=dil, tables=tables):
                q0, kcur0, kprev0, first = _dil_block_pos(blk, c, dil)
                q2 = _stack_pair(q_ref[_dil_rows(q0, dil), :] * DIL_SCALE)
                kcat = jnp.concatenate([k_ref[_dil_rows(kprev0, dil), :], k_ref[_dil_rows(kcur0, dil), :]], axis=0).astype(MXU_DTYPE)
                vcat = jnp.concatenate([v_ref[_dil_rows(kprev0, dil), :], v_ref[_dil_rows(kcur0, dil), :]], axis=0).astype(MXU_DTYPE)
                s = _dot(q2, kcat, 1, 1) + jnp.where(first, tables[1], tables[0])
                mx = jnp.max(s, axis=1, keepdims=True)
                p = jnp.exp(s - mx)
                l = jnp.sum(p, axis=1, keepdims=True)
                o_scr[bi][_dil_rows(q0, dil), :] = _unstack_pair(_dot(p.astype(MXU_DTYPE), vcat, 1, 0) * (1.0 / l))
                l_scr[bi][_dil_rows(q0, dil), :] = _unstack_pair(jnp.broadcast_to(mx + jnp.log(l), (2 * DIL_BLOCK, LANES)))
                return carry

            lax.fori_loop(0, nblk, block, 0, unroll=DIL_UNROLL_FWD)

        def combine(i, carry):
            rows = pl.ds(pl.multiple_of(i * rc, rc), rc)
            ls = [l_scr[bi][rows, :] for bi in range(3)]
            mx = jnp.maximum(jnp.maximum(ls[0], ls[1]), ls[2])
            es = [jnp.exp(l - mx) for l in ls]
            den = es[0] + es[1] + es[2]
            o = (es[0] * o_scr[0][rows, :] + es[1] * o_scr[1][rows, :] + es[2] * o_scr[2][rows, :]) / den
            o_ref[rows, :] = o
            ob_ref[rows, :] = o.astype(ob_ref.dtype)
            lse_ref[rows, :] = mx + jnp.log(den)
            return carry

        lax.fori_loop(0, DIL_CHUNK // rc, combine, 0)

    nq = DIL_WIDTH // LANES
    chunk = lambda off: pl.BlockSpec((DIL_CHUNK, LANES), lambda hp, c: (c, off + hp))
    whole = lambda off: pl.BlockSpec((seq, LANES), lambda hp, c: (0, off + hp))
    shp = jax.ShapeDtypeStruct((seq, DIL_WIDTH), F32)
    return pl.pallas_call(
        body, name="dil_fwd", grid=(nq, seq // DIL_CHUNK),
        in_specs=[chunk(nq), whole(2 * nq), whole(3 * nq)], out_specs=[chunk(0), chunk(0), chunk(0)],
        out_shape=[shp, jax.ShapeDtypeStruct((seq, DIL_WIDTH), MXU_DTYPE), shp],
        scratch_shapes=[pltpu.VMEM((DIL_CHUNK, LANES), F32)] * 6,
        compiler_params=_params("parallel", "arbitrary"),
    )(h, h, h)


def _dil_bwd(h, o, lse, do, after=()):
    seq = h.shape[0]
    nblk = DIL_CHUNK // DIL_BLOCK
    nchunk = seq // DIL_CHUNK
    rc = 256

    n_after = len(after)

    def body(q_ref, k_ref, v_ref, o_ref, lse_ref, do_ref, *rest):
        dq_out, dk_out, dv_out, dl_scr, dq_ref, dk_ref, dv_ref = rest[n_after:]
        hp, c = pl.program_id(0), pl.program_id(1)

        @pl.when(c == 0)
        def _():
            dk_ref[...] = jnp.zeros_like(dk_ref)
            dv_ref[...] = jnp.zeros_like(dv_ref)

        def delta(i, carry):
            rows = pl.ds(pl.multiple_of(i * rc, rc), rc)
            prod = do_ref[rows, :] * o_ref[rows, :]
            dl_scr[rows, :] = jnp.concatenate(
                [jnp.broadcast_to(jnp.sum(prod[:, _pair_cols(hh)], axis=1, keepdims=True), (rc, HEAD_DIM)) for hh in range(2)], axis=1)
            return carry

        lax.fori_loop(0, DIL_CHUNK // rc, delta, 0)

        for bi, (_, dil) in enumerate(DIL_PAIRS):
            tables = _dil_bias_tables(hp, dil)

            def block(blk, carry, bi=bi, dil=dil, tables=tables):
                q0, kcur0, kprev0, first = _dil_block_pos(blk, c, dil)
                qrows = _dil_rows(q0, dil)
                q2 = _stack_pair(q_ref[qrows, :] * DIL_SCALE)
                kcat = jnp.concatenate([k_ref[_dil_rows(kprev0, dil), :], k_ref[_dil_rows(kcur0, dil), :]], axis=0).astype(MXU_DTYPE)
                vcat = jnp.concatenate([v_ref[_dil_rows(kprev0, dil), :], v_ref[_dil_rows(kcur0, dil), :]], axis=0).astype(MXU_DTYPE)
                do2 = _stack_pair(do_ref[qrows, :])
                s = _dot(q2, kcat, 1, 1) + jnp.where(first, tables[1], tables[0])
                p = jnp.exp(s - _pair_column(lse_ref[qrows, :]))
                dp = _dot(do2, vcat, 1, 1)
                ds = (p * (dp - _pair_column(dl_scr[qrows, :]))).astype(MXU_DTYPE)
                dq_b = _unstack_pair(_dot(ds, kcat, 1, 0)) * DIL_SCALE
                dk_b = _dot(ds, q2, 0, 0)
                dv_b = _dot(p.astype(MXU_DTYPE), do2, 0, 0)
                if bi == 0:
                    dq_ref[qrows, :] = dq_b
                else:
                    dq_ref[qrows, :] += dq_b
                dk_ref[_dil_rows(kprev0, dil), :] += dk_b[:DIL_BLOCK]
                dv_ref[_dil_rows(kprev0, dil), :] += dv_b[:DIL_BLOCK]
                dk_ref[_dil_rows(kcur0, dil), :] += dk_b[DIL_BLOCK:]
                dv_ref[_dil_rows(kcur0, dil), :] += dv_b[DIL_BLOCK:]
                return carry

            lax.fori_loop(0, nblk, block, 0, unroll=DIL_UNROLL_BWD)

        dq_out[...] = dq_ref[...].astype(dq_out.dtype)

        @pl.when(c == nchunk - 1)
        def _():
            dk_out[...] = dk_ref[...].astype(dk_out.dtype)
            dv_out[...] = dv_ref[...].astype(dv_out.dtype)

    nq = DIL_WIDTH // LANES
    chunk = lambda off: pl.BlockSpec((DIL_CHUNK, LANES), lambda hp, c: (c, off + hp))
    whole = lambda off: pl.BlockSpec((seq, LANES), lambda hp, c: (0, off + hp))
    shp = jax.ShapeDtypeStruct((seq, DIL_WIDTH), MXU_DTYPE)
    return pl.pallas_call(
        body, name="dil_bwd", grid=(nq, nchunk),
        in_specs=[chunk(nq), whole(2 * nq), whole(3 * nq), chunk(0), chunk(0), chunk(0)] + [_ANY_SPEC] * n_after,
        out_specs=[chunk(0), whole(0), whole(0)], out_shape=[shp, shp, shp],
        scratch_shapes=[pltpu.VMEM((DIL_CHUNK, LANES), F32), pltpu.VMEM((DIL_CHUNK, LANES), F32),
                        pltpu.VMEM((seq, LANES), F32), pltpu.VMEM((seq, LANES), F32)],
        compiler_params=_params("parallel", "arbitrary"),
    )(h, h, h, o, lse, do, *after)


def _mm_dx0(parts, w_in_t, res, tm=1024, after=()):
    seq, d = res.shape
    tm = min(tm, seq)
    n_after = len(after)

    def body(a0, a1, a2, a3, b_ref, r_ref, *rest):
        o_ref = rest[n_after]
        acc = _dot(a0[...], b_ref[0:DH_PART, :], 1, 0)
        for c, a in enumerate((a1, a2, a3), start=1):
            acc += _dot(a[...], b_ref[DH_PART * c:DH_PART * (c + 1), :], 1, 0)
        o_ref[...] = acc + DN_ALPHA * r_ref[...]

    blk = pl.BlockSpec((tm, DH_PART), lambda i: (i, 0))
    row = pl.BlockSpec((tm, d), lambda i: (i, 0))
    return pl.pallas_call(
        body, name="mm_dx0", grid=(seq // tm,),
        in_specs=[blk] * 4 + [pl.BlockSpec((IN_PAD, d), lambda i: (0, 0), pipeline_mode=pl.Buffered(1)), row] + [_ANY_SPEC] * n_after,
        out_specs=row, out_shape=jax.ShapeDtypeStruct((seq, d), F32), compiler_params=_params("parallel"),
    )(*parts, w_in_t, res, *after)


def _mm_dw_in(parts, x0, tk=1024):
    seq, d = x0.shape
    tk = min(tk, seq)
    nk = seq // tk

    def body(a0, a1, a2, a3, b_ref, o_ref, acc_ref):
        kk = pl.program_id(0)

        @pl.when(kk == 0)
        def _():
            acc_ref[...] = jnp.zeros_like(acc_ref)

        b = b_ref[...].astype(MXU_DTYPE)
        for c, a in enumerate((a0, a1, a2, a3)):
            acc_ref[DH_PART * c:DH_PART * (c + 1), :] += _dot(a[...], b, 0, 0)

        @pl.when(kk == nk - 1)
        def _():
            o_ref[...] = acc_ref[...].astype(o_ref.dtype)

    blk = pl.BlockSpec((tk, DH_PART), lambda kk: (kk, 0))
    return pl.pallas_call(
        body, name="mm_dw_in", grid=(nk,), in_specs=[blk] * 4 + [pl.BlockSpec((tk, d), lambda kk: (kk, 0))],
        out_specs=pl.BlockSpec((IN_PAD, d), lambda kk: (0, 0)), out_shape=jax.ShapeDtypeStruct((IN_PAD, d), MXU_DTYPE),
        scratch_shapes=[pltpu.VMEM((IN_PAD, d), F32)], compiler_params=_params("arbitrary"),
    )(*parts, x0)


def _ln_stats(z):
    mu = jnp.mean(z, axis=-1, keepdims=True)
    zc = z - mu
    r = lax.rsqrt(jnp.mean(zc * zc, axis=-1, keepdims=True) + LN_EPS)
    return zc * r, r


def _ln_bwd_math(dy, xh, r, g):
    dxh = dy * g
    return r * (dxh - jnp.mean(dxh, axis=-1, keepdims=True) - xh * jnp.mean(dxh * xh, axis=-1, keepdims=True))


def _mix_ln1(o_mla, o_dil, w_o_mla, w_o_dil, x0, g, b, tm=512):
    seq, d = x0.shape
    tm = min(tm, seq)

    def body(om_ref, od_ref, wm_ref, wd_ref, x_ref, g_ref, b_ref, z_ref, y_ref, yb_ref):
        mix = _dot(od_ref[...], wd_ref[...], 1, 0)
        for hd in range(HEADS):
            mix += _dot(om_ref[hd], wm_ref[LANES * hd:LANES * (hd + 1), :], 1, 0)
        z = DN_ALPHA * x_ref[...] + mix
        xh, _ = _ln_stats(z)
        y = xh * g_ref[...] + b_ref[...]
        z_ref[...] = z
        y_ref[...] = y
        yb_ref[...] = y.astype(yb_ref.dtype)

    blk = pl.BlockSpec((tm, d), lambda i: (i, 0))
    vec = pl.BlockSpec((1, d), lambda i: (0, 0))
    shp = jax.ShapeDtypeStruct((seq, d), F32)
    return pl.pallas_call(
        body, name="mix_ln1", grid=(seq // tm,),
        in_specs=[pl.BlockSpec((HEADS, tm, LANES), lambda i: (0, i, 0)), pl.BlockSpec((tm, DIL_WIDTH), lambda i: (i, 0)),
                  pl.BlockSpec((HEADS * LANES, d), lambda i: (0, 0)), pl.BlockSpec((DIL_WIDTH, d), lambda i: (0, 0)), blk, vec, vec],
        out_specs=[blk, blk, blk], out_shape=[shp, shp, jax.ShapeDtypeStruct((seq, d), MXU_DTYPE)],
        compiler_params=_params("parallel"))(o_mla, o_dil, w_o_mla, w_o_dil, x0, g, b)


def _dx1_ln1_bwd(du, w_up_t, dz2, z, g, tm=256, after=()):
    seq, d = z.shape
    kdim = du.shape[1]
    tm = min(tm, seq)
    n_after = len(after)

    def body(du_ref, w_ref, r_ref, z_ref, g_ref, *rest):
        dz_ref, dzb_ref, dg_ref, db_ref = rest[n_after:]

        @pl.when(pl.program_id(0) == 0)
        def _():
            dg_ref[...] = jnp.zeros_like(dg_ref)
            db_ref[...] = jnp.zeros_like(db_ref)

        dyb = _dot(du_ref[...], w_ref[...], 1, 0) + DN_ALPHA * r_ref[...]
        xh, r = _ln_stats(z_ref[...])
        dg_ref[...] += jnp.sum(dyb * xh, axis=0, keepdims=True)
        db_ref[...] += jnp.sum(dyb, axis=0, keepdims=True)
        dz = _ln_bwd_math(dyb, xh, r, g_ref[...])
        dz_ref[...] = dz
        dzb_ref[...] = dz.astype(dzb_ref.dtype)

    blk = pl.BlockSpec((tm, d), lambda i: (i, 0))
    vec = pl.BlockSpec((1, d), lambda i: (0, 0))
    return pl.pallas_call(
        body, name="dx1_ln1_bwd", grid=(seq // tm,),
        in_specs=[pl.BlockSpec((tm, kdim), lambda i: (i, 0)),
                  pl.BlockSpec((kdim, d), lambda i: (0, 0), pipeline_mode=pl.Buffered(1)), blk, blk, vec] + [_ANY_SPEC] * n_after,
        out_specs=[blk, blk, vec, vec],
        out_shape=[jax.ShapeDtypeStruct((seq, d), F32), jax.ShapeDtypeStruct((seq, d), MXU_DTYPE),
                   jax.ShapeDtypeStruct((1, d), F32), jax.ShapeDtypeStruct((1, d), F32)],
        compiler_params=_params("arbitrary"))(du, w_up_t, dz2, z, g, *after)


def _down_ln2_loss_bwd(act, w_down, x1, target, g, b, tm=512):
    seq, d = x1.shape
    kdim = act.shape[1]
    tm = min(tm, seq)

    def body(a_ref, w_ref, x_ref, t_ref, g_ref, b_ref, dz_ref, dzb_ref, loss_ref, dg_ref, db_ref):
        @pl.when(pl.program_id(0) == 0)
        def _():
            loss_ref[...] = jnp.zeros_like(loss_ref)
            dg_ref[...] = jnp.zeros_like(dg_ref)
            db_ref[...] = jnp.zeros_like(db_ref)

        gv = g_ref[...]
        z = DN_ALPHA * x_ref[...] + _dot(a_ref[...], w_ref[...], 1, 0)
        xh, r = _ln_stats(z)
        err = (xh * gv + b_ref[...]) - t_ref[...]
        loss_ref[...] += 0.5 * jnp.sum(jnp.mean(err * err, axis=-1, keepdims=True), axis=0, keepdims=True)
        dy = err * (1.0 / d)
        dg_ref[...] += jnp.sum(dy * xh, axis=0, keepdims=True)
        db_ref[...] += jnp.sum(dy, axis=0, keepdims=True)
        dz = _ln_bwd_math(dy, xh, r, gv)
        dz_ref[...] = dz
        dzb_ref[...] = dz.astype(dzb_ref.dtype)

    blk = pl.BlockSpec((tm, d), lambda i: (i, 0))
    vec = pl.BlockSpec((1, d), lambda i: (0, 0))
    return pl.pallas_call(
        body, name="down_ln2_loss_bwd", grid=(seq // tm,),
        in_specs=[pl.BlockSpec((tm, kdim), lambda i: (i, 0)),
                  pl.BlockSpec((kdim, d), lambda i: (0, 0), pipeline_mode=pl.Buffered(1)), blk, blk, vec, vec],
        out_specs=[blk, blk, pl.BlockSpec((1, LANES), lambda i: (0, 0)), vec, vec],
        out_shape=[jax.ShapeDtypeStruct((seq, d), F32), jax.ShapeDtypeStruct((seq, d), MXU_DTYPE),
                   jax.ShapeDtypeStruct((1, LANES), F32),
                   jax.ShapeDtypeStruct((1, d), F32), jax.ShapeDtypeStruct((1, d), F32)],
        compiler_params=_params("arbitrary"))(act, w_down, x1, target, g, b)


HALO = 16


def _conv_rows(e, w_ref, b_ref):
    y = b_ref[...] + w_ref[0:1, :] * pltpu.roll(e, 2, 0)
    y = y + w_ref[1:2, :] * pltpu.roll(e, 1, 0)
    return y + w_ref[2:3, :] * e


_GELU_C = math.sqrt(2.0 / math.pi)
_GELU_A = 0.044715


def _gelu(x):
    return 0.5 * x * (1.0 + jnp.tanh(_GELU_C * (x + _GELU_A * (x * x * x))))


CONV_TN = 256


def _ffn_interleave(a, axis):
    shp = a.shape
    a = a.reshape(shp[:axis] + (2, D_FF // CONV_TN, CONV_TN) + shp[axis + 1:])
    return jnp.swapaxes(a, axis, axis + 1).reshape(shp)


def _ffn_deinterleave(a, axis):
    shp = a.shape
    a = a.reshape(shp[:axis] + (D_FF // CONV_TN, 2, CONV_TN) + shp[axis + 1:])
    return jnp.swapaxes(a, axis, axis + 1).reshape(shp)


def _conv_gate_fwd(u, conv_w, conv_b, tm=1024):
    seq = u.shape[0]
    tm = min(tm, seq)
    tn = CONV_TN

    def body(u_ref, up_ref, w_ref, b_ref, o_ref):
        first = pl.program_id(0) == 0
        e = jnp.concatenate([jnp.where(first, 0.0, up_ref[...]), u_ref[...]], axis=0)
        y = _conv_rows(e, w_ref, b_ref)[HALO:]
        o_ref[...] = (_gelu(y[:, tn:]) * y[:, :tn]).astype(o_ref.dtype)

    hb = tm // HALO
    return pl.pallas_call(
        body, name="conv_gate_fwd", grid=(seq // tm, D_FF // tn),
        in_specs=[pl.BlockSpec((tm, 2 * tn), lambda i, j: (i, j)),
                  pl.BlockSpec((HALO, 2 * tn), lambda i, j: (jnp.maximum(i * hb - 1, 0), j)),
                  pl.BlockSpec((3, 2 * tn), lambda i, j: (0, j)), pl.BlockSpec((1, 2 * tn), lambda i, j: (0, j))],
        out_specs=pl.BlockSpec((tm, tn), lambda i, j: (i, j)), out_shape=jax.ShapeDtypeStruct((seq, D_FF), MXU_DTYPE),
        compiler_params=_params("parallel", "parallel"),
    )(u, u, conv_w, conv_b)


def _conv_gate_bwd(u, d_act, conv_w, conv_b, tm=1024):
    seq = u.shape[0]
    tm = min(tm, seq)
    tn = CONV_TN
    ni = seq // tm
    rows_e = tm + 2 * HALO

    def body(u_ref, up_ref, un_ref, da_ref, dan_ref, w_ref, b_ref, du_ref, dw_ref, db_ref):
        i = pl.program_id(1)
        first, last = i == 0, i == ni - 1

        @pl.when(i == 0)
        def _():
            dw_ref[...] = jnp.zeros_like(dw_ref)
            db_ref[...] = jnp.zeros_like(db_ref)

        e = jnp.concatenate([jnp.where(first, 0.0, up_ref[...]), u_ref[...], jnp.where(last, 0.0, un_ref[...])], axis=0)
        y = _conv_rows(e, w_ref, b_ref)
        ya, yg = y[:, :tn], y[:, tn:]
        dact = jnp.concatenate([jnp.zeros((HALO, tn), F32), da_ref[...].astype(F32),
                                jnp.where(last, 0.0, dan_ref[...].astype(F32))], axis=0)
        th = jnp.tanh(_GELU_C * (yg + _GELU_A * (yg * yg * yg)))
        gelu = 0.5 * yg * (1.0 + th)
        gelu_grad = 0.5 * (1.0 + th) + 0.5 * yg * (1.0 - th * th) * (_GELU_C * (1.0 + 3.0 * _GELU_A * (yg * yg)))
        dy = jnp.concatenate([dact * gelu, dact * ya * gelu_grad], axis=1)
        du = w_ref[2:3, :] * dy + w_ref[1:2, :] * pltpu.roll(dy, rows_e - 1, 0) + w_ref[0:1, :] * pltpu.roll(dy, rows_e - 2, 0)
        du_ref[...] = du[HALO:HALO + tm].astype(du_ref.dtype)
        dyt = dy[HALO:HALO + tm]
        dw_ref[0:1, :] += jnp.sum(dyt * pltpu.roll(e, 2, 0)[HALO:HALO + tm], axis=0, keepdims=True)
        dw_ref[1:2, :] += jnp.sum(dyt * pltpu.roll(e, 1, 0)[HALO:HALO + tm], axis=0, keepdims=True)
        dw_ref[2:3, :] += jnp.sum(dyt * e[HALO:HALO + tm], axis=0, keepdims=True)
        db_ref[...] += jnp.sum(dyt, axis=0, keepdims=True)

    hb = tm // HALO
    nh = seq // HALO
    prev = lambda j, i: (jnp.maximum(i * hb - 1, 0), j)
    nxt = lambda j, i: (jnp.minimum((i + 1) * hb, nh - 1), j)
    return pl.pallas_call(
        body, name="conv_gate_bwd", grid=(D_FF // tn, ni),
        in_specs=[pl.BlockSpec((tm, 2 * tn), lambda j, i: (i, j)), pl.BlockSpec((HALO, 2 * tn), prev),
                  pl.BlockSpec((HALO, 2 * tn), nxt), pl.BlockSpec((tm, tn), lambda j, i: (i, j)), pl.BlockSpec((HALO, tn), nxt),
                  pl.BlockSpec((3, 2 * tn), lambda j, i: (0, j)), pl.BlockSpec((1, 2 * tn), lambda j, i: (0, j))],
        out_specs=[pl.BlockSpec((tm, 2 * tn), lambda j, i: (i, j)), pl.BlockSpec((3, 2 * tn), lambda j, i: (0, j)),
                   pl.BlockSpec((1, 2 * tn), lambda j, i: (0, j))],
        out_shape=[jax.ShapeDtypeStruct((seq, 2 * D_FF), MXU_DTYPE), jax.ShapeDtypeStruct((3, 2 * D_FF), F32),
                   jax.ShapeDtypeStruct((1, 2 * D_FF), F32)],
        compiler_params=_params("parallel", "arbitrary"),
    )(u, u, u, d_act, d_act, conv_w, conv_b)


def _pad_heads(w, width):
    w = jnp.transpose(w, (1, 0, 2))
    return jnp.pad(w, ((0, 0), (0, 0), (0, LANES - width))).astype(MXU_DTYPE)


def _heads_major(a):
    return jnp.transpose(a, (1, 0, 2)).reshape(-1, a.shape[2])


def _heads_minor(a, heads):
    return jnp.transpose(a.reshape(heads, -1, a.shape[1]), (1, 0, 2))


_LATENT = Q_RANK + KV_RANK
_ROPE_AT = _LATENT + NOPE_DIM
_ROPE_END = _ROPE_AT + ROPE_DIM


def _split_pad_rows(w_t):
    z = lambda n: jnp.zeros((n, w_t.shape[1]), w_t.dtype)
    return jnp.concatenate([w_t[:_LATENT], z(_ROPE_AT - _LATENT), w_t[_LATENT:_LATENT + ROPE_DIM], z(DH_PART - _ROPE_END),
                            w_t[_LATENT + ROPE_DIM:]], axis=0)


def _split_unpad_rows(w_p):
    return jnp.concatenate([w_p[:_LATENT], w_p[_ROPE_AT:_ROPE_END], w_p[DH_PART:]], axis=0)


def _pad_w_o(w_o):
    mla = jnp.pad(w_o[:MLA_WIDTH].reshape(HEADS, HEAD_DIM, D_MODEL), ((0, 0), (0, LANES - HEAD_DIM), (0, 0)))
    return mla.reshape(HEADS * LANES, D_MODEL).astype(MXU_DTYPE), w_o[MLA_WIDTH:].astype(MXU_DTYPE)


def _unpad_w_o(d_mla, d_dil):
    return jnp.concatenate([d_mla.reshape(HEADS, LANES, D_MODEL)[:, :HEAD_DIM].reshape(MLA_WIDTH, D_MODEL), d_dil], axis=0)


def _row(v):
    return v.reshape(1, -1).astype(F32)


def _layer_grads(x0, target, cw, first_after=(), late_weights=None, on_grads=None):
    seq = x0.shape[0]
    ctab, stab = _rope_tables(seq)
    gq, gk = cw["g_cq"], cw["g_ckv"]
    wq, wk, wv = cw["wq"], cw["wk"], cw["wv"]
    notify = (lambda stage, grads: ()) if on_grads is None else on_grads

    h = _mm(x0, cw["w_in_t"], name="mm_h", tb=True, tm=1024, tn=IN_PAD, tk=1024, after=first_after)
    qf, kf, vp = _mla_prep(h, gq, gk, wq, wk, wv, ctab, stab)
    o_mla, o_mla_b, lse_mla = _mla_attn_fwd(qf, kf, vp)
    o_dil, o_dil_b, lse_dil = _dil_fwd(h)
    if late_weights is not None:
        cw = {**cw, **late_weights(o_mla_b)}
    cb = cw["conv_b"]
    z1, x1, x1b = _mix_ln1(o_mla_b, o_dil_b, cw["w_o_mla"], cw["w_o_dil"], x0, cw["ln1_g"], cw["ln1_b"])
    u = _mm(x1b, cw["w_up_t"], name="mm_up", tb=True, tm=512, tn=2 * D_FF, tk=1024)
    act = _conv_gate_fwd(u, cw["conv_w"], cb)
    dz2, dz2b, loss, d_ln2_g, d_ln2_b = _down_ln2_loss_bwd(act, cw["w_down"], x1, target, cw["ln2_g"], cw["ln2_b"])

    d_act = _mm(dz2b, cw["w_down"], name="mm_d_act", tb=True, out_dtype=MXU_DTYPE, tm=1024, tn=D_FF, tk=1024)
    d_w_down = _mm(act, dz2b, name="mm_dw_down", ta=True, out_dtype=MXU_DTYPE, tm=1408, tn=1024, tk=1024)
    du, d_conv_w, d_conv_b = _conv_gate_bwd(u, d_act, cw["conv_w"], cb)
    d_w_up_t = _mm(du, x1b, name="mm_dw_up", ta=True, out_dtype=MXU_DTYPE, tm=1408, tn=1024, tk=2048)
    grads = dict(w_up_t=d_w_up_t, w_down=d_w_down, conv_w=d_conv_w, conv_b=d_conv_b, ln2_g=d_ln2_g, ln2_b=d_ln2_b)
    dz1, dz1b, d_ln1_g, d_ln1_b = _dx1_ln1_bwd(du, cw["w_up_t"], dz2, z1, cw["ln1_g"], after=notify("ffn", grads))
    do_mla, do_dil, d_w_o_mla, d_w_o_dil = _w_o_bwd(dz1b, o_mla_b, o_dil_b, cw["w_o_mla"], cw["w_o_dil"])
    grads.update(w_o_mla=d_w_o_mla, w_o_dil=d_w_o_dil, ln1_g=d_ln1_g, ln1_b=d_ln1_b)
    dqf, dkf, dvf = _mla_attn_bwd(qf, kf, vp, o_mla, lse_mla, do_mla)
    dh_mla, d_wq, d_wk, d_wv, d_gq, d_gk = _mla_prep_bwd(h, gq, gk, wq, wk, wv, ctab, stab, dqf, dkf, dvf,
                                                          after=notify("w_o", grads))
    grads.update(wq=d_wq, wk=d_wk, wv=d_wv, g_cq=d_gq, g_ckv=d_gk, loss=loss)
    dq_dil, dk_dil, dv_dil = _dil_bwd(h, o_dil, lse_dil, do_dil, after=notify("mla", grads))
    dh = (dh_mla, dq_dil, dk_dil, dv_dil)
    grads.update(w_in_t=_mm_dw_in(dh, x0))
    grad_x = _mm_dx0(dh, cw["w_in_t"], dz1, after=notify("w_in", grads))
    return loss, grad_x, grads


def _all_gather(blocks, name):
    na = len(blocks)

    def body(*refs):
        ins, outs = refs[:na], refs[na:2 * na]
        send_sems, recv_sems, local_sems = refs[2 * na:]
        x, y, c = lax.axis_index("x"), lax.axis_index("y"), lax.axis_index("c")
        me, sibling = (x, y, c), (x, y, 1 - c)
        chips = [(1 - x, y), (x, 1 - y), (1 - x, 1 - y)]

        def slot(out, pos):
            return out.at[4 * pos[0] + 2 * pos[1] + pos[2]]

        def copy(a, k, block, to, src=None):
            return pltpu.make_async_remote_copy(
                src_ref=slot(outs[a], block) if src is None else src, dst_ref=slot(outs[a], block),
                send_sem=send_sems.at[7 * a + k], recv_sem=recv_sems.at[7 * a + k],
                device_id=to, device_id_type=pl.DeviceIdType.MESH)

        mine = [pltpu.make_async_copy(ins[a], slot(outs[a], me), local_sems.at[a]) for a in range(na)]
        for cp in mine:
            cp.start()
        first = []
        for a in range(na):
            first.append(copy(a, 0, me, sibling, src=ins[a]))
            first += [copy(a, 1 + j, me, (*chip, c), src=ins[a]) for j, chip in enumerate(chips)]
        for cp in first:
            cp.start()
        passed = []
        for j, chip in enumerate(chips):
            for a in range(na):
                copy(a, 1 + j, (*chip, c), me).wait_recv()
                cp = copy(a, 4 + j, (*chip, c), sibling)
                cp.start()
                passed.append(cp)
        for a in range(na):
            copy(a, 0, sibling, me).wait_recv()
            for j, chip in enumerate(chips):
                copy(a, 4 + j, (*chip, 1 - c), me).wait_recv()
        for cp in first + passed:
            cp.wait_send()
        for cp in mine:
            cp.wait()

    any_spec = pl.BlockSpec(memory_space=pl.ANY)
    return pl.pallas_call(
        body, name=name, in_specs=[any_spec] * na, out_specs=[any_spec] * na,
        out_shape=[jax.ShapeDtypeStruct((N_DEV,) + b.shape, b.dtype) for b in blocks],
        scratch_shapes=[pltpu.SemaphoreType.DMA((7 * na,)), pltpu.SemaphoreType.DMA((7 * na,)), pltpu.SemaphoreType.DMA((na,))],
    )(*blocks)


_HBM_SPEC = pl.BlockSpec(memory_space=pltpu.HBM)
_SEM_SPEC = pl.BlockSpec(memory_space=pltpu.SEMAPHORE)
_DATAFLOW = pltpu.CompilerParams(has_side_effects=pltpu.SideEffectType.DATAFLOW_SIDE_EFFECTING)


def _split_copies(ins, lands, send_sems, recv_sems, gather):
    x, y, c = lax.axis_index("x"), lax.axis_index("y"), lax.axis_index("c")
    me = 4 * x + 2 * y + c
    copies = []
    for a in range(len(ins)):
        for d in range(1, N_DEV):
            px, py, pc = x ^ (d >> 2), y ^ ((d >> 1) & 1), c ^ (d & 1)
            copies.append(pltpu.make_async_remote_copy(
                src_ref=ins[a] if gather[a] else ins[a].at[4 * px + 2 * py + pc], dst_ref=lands[a].at[me],
                send_sem=send_sems.at[7 * a + d - 1], recv_sem=recv_sems.at[7 * a + d - 1],
                device_id=(px, py, pc), device_id_type=pl.DeviceIdType.MESH))
    return copies


def _send_start(srcs, gather, name):
    na = len(srcs)
    assert len(gather) == na
    land_types = [pltpu.HBM(((N_DEV,) + s.shape) if g else s.shape, s.dtype) for s, g in zip(srcs, gather)]

    def body(*refs):
        ins, lands = refs[:na], refs[na:2 * na]
        send_sems, recv_sems, token = refs[2 * na], refs[2 * na + 1], refs[-1]
        for cp in _split_copies(ins, lands, send_sems, recv_sems, gather):
            cp.start()
        token[...] = jnp.zeros_like(token)

    hbm = lambda a: pltpu.with_memory_space_constraint(a, pltpu.HBM)
    outs = pl.pallas_call(
        body, name=name,
        out_shape=(pltpu.SemaphoreType.DMA((7 * na,)), pltpu.SemaphoreType.DMA((7 * na,)),
                   *[pltpu.HBM(s.shape, s.dtype) for s in srcs], *land_types, jax.ShapeDtypeStruct((8, LANES), F32)),
        in_specs=[_HBM_SPEC] * (2 * na),
        out_specs=(_SEM_SPEC, _SEM_SPEC, *[_HBM_SPEC] * (2 * na), pl.BlockSpec(memory_space=pltpu.VMEM)),
        input_output_aliases={i: 2 + i for i in range(2 * na)}, compiler_params=_DATAFLOW,
    )(*[hbm(s) for s in srcs], *[hbm(lax.empty(t.shape, t.dtype)) for t in land_types])
    return dict(send=outs[0], recv=outs[1], srcs=list(outs[2:2 + na]), lands=list(outs[2 + na:2 + 2 * na]), token=outs[-1],
                gather=gather)


def _send_wait(handle, after, name):
    na = len(handle["srcs"])
    gather = handle["gather"]
    after = list(after)

    def body(*refs):
        ins, lands = refs[:na], refs[na:2 * na]
        send_sems, recv_sems = refs[2 * na], refs[2 * na + 1]
        for cp in _split_copies(ins, lands, send_sems, recv_sems, gather):
            cp.wait_send()
            cp.wait_recv()

    both = handle["srcs"] + handle["lands"]
    outs = pl.pallas_call(
        body, name=name, out_shape=[pltpu.HBM(a.shape, a.dtype) for a in both],
        in_specs=[_HBM_SPEC] * (2 * na) + [_SEM_SPEC, _SEM_SPEC] + [_ANY_SPEC] * len(after),
        out_specs=[_HBM_SPEC] * (2 * na), input_output_aliases={i: i for i in range(2 * na)}, compiler_params=_DATAFLOW,
    )(*both, handle["send"], handle["recv"], *after)
    return list(outs[:na]), list(outs[na:])


def _sum_slots(p_ref):
    g = p_ref[0].astype(F32)
    for s in range(1, p_ref.shape[0]):
        g = g + p_ref[s].astype(F32)
    return g


def _adamw_refs(g, w_ref, m_ref, v_ref, g_out, d_out, m_out, v_out):
    c1 = 1.0 - ADAM_B1 ** ADAM_STEP
    c2 = 1.0 - ADAM_B2 ** ADAM_STEP
    m_new = ADAM_B1 * m_ref[...] + (1.0 - ADAM_B1) * g
    v_new = ADAM_B2 * v_ref[...] + (1.0 - ADAM_B2) * (g * g)
    g_out[...] = g
    m_out[...] = m_new
    v_out[...] = v_new
    d_out[...] = -ADAM_LR * ((m_new / c1) / (jnp.sqrt(v_new / c2) + ADAM_EPS) + ADAM_WD * w_ref[...])


def _adamw(parts, w, m, v, name):
    npart, r, n = parts.shape
    tr = r if r <= 256 else max(t for t in range(16, 257, 16) if r % t == 0)

    def body(p_ref, w_ref, m_ref, v_ref, g_out, d_out, m_out, v_out):
        _adamw_refs(_sum_slots(p_ref), w_ref, m_ref, v_ref, g_out, d_out, m_out, v_out)

    blk = pl.BlockSpec((tr, n), lambda i: (i, 0))
    shp = jax.ShapeDtypeStruct((r, n), F32)
    return pl.pallas_call(
        body, name=name, grid=(r // tr,), in_specs=[pl.BlockSpec((npart, tr, n), lambda i: (0, i, 0)), blk, blk, blk],
        out_specs=[blk] * 4, out_shape=[shp] * 4, compiler_params=_params("parallel"),
    )(parts, w, m, v)


def _adamw_small(parts, ws, ms, vs, loss_parts, name):
    n = len(parts)

    def body(*refs):
        ins, outs = refs[:4 * n + 1], refs[4 * n + 1:]
        for i in range(n):
            _adamw_refs(_sum_slots(ins[i]), ins[n + i], ins[2 * n + i], ins[3 * n + i], *outs[4 * i:4 * i + 4])
        outs[4 * n][...] = _sum_slots(ins[4 * n])

    out_shape = [jax.ShapeDtypeStruct(w.shape, F32) for w in ws for _ in range(4)]
    res = pl.pallas_call(body, name=name, out_shape=out_shape + [jax.ShapeDtypeStruct((1, LANES), F32)],
                         compiler_params=_params())(*parts, *ws, *ms, *vs, loss_parts)
    return [res[4 * i:4 * i + 4] for i in range(n)], res[4 * n]


REPLICATED = ("g_cq", "g_ckv", "w_uk", "w_uv", "ln1_g", "ln1_b", "conv_b", "ln2_g", "ln2_b")
ALL_WEIGHTS = ("w_in", "g_cq", "g_ckv", "w_uq", "w_uk", "w_uv", "w_o", "ln1_g", "ln1_b", "w_up", "conv_w", "conv_b",
               "w_down", "ln2_g", "ln2_b")


def kernel(x, w_in, g_cq, g_ckv, w_uq, w_uk, w_uv, w_o, ln1_g, ln1_b, w_up, conv_w, conv_b, w_down, ln2_g, ln2_b, loss_target, m_w_in, m_g_cq, m_g_ckv, m_w_uq, m_w_uk, m_w_uv, m_w_o, m_ln1_g, m_ln1_b, m_w_up, m_conv_w, m_conv_b, m_w_down, m_ln2_g, m_ln2_b, v_w_in, v_g_cq, v_g_ckv, v_w_uq, v_w_uk, v_w_uv, v_w_o, v_ln1_g, v_ln1_b, v_w_up, v_conv_w, v_conv_b, v_w_down, v_ln2_g, v_ln2_b):
    w = dict(w_in=w_in, g_cq=g_cq, g_ckv=g_ckv, w_uq=w_uq, w_uk=w_uk, w_uv=w_uv, w_o=w_o, ln1_g=ln1_g, ln1_b=ln1_b,
             w_up=w_up, conv_w=conv_w, conv_b=conv_b, w_down=w_down, ln2_g=ln2_g, ln2_b=ln2_b)
    m = dict(w_in=m_w_in, g_cq=m_g_cq, g_ckv=m_g_ckv, w_uq=m_w_uq, w_uk=m_w_uk, w_uv=m_w_uv, w_o=m_w_o, ln1_g=m_ln1_g,
             ln1_b=m_ln1_b, w_up=m_w_up, conv_w=m_conv_w, conv_b=m_conv_b, w_down=m_w_down, ln2_g=m_ln2_g, ln2_b=m_ln2_b)
    v = dict(w_in=v_w_in, g_cq=v_g_cq, g_ckv=v_g_ckv, w_uq=v_w_uq, w_uk=v_w_uk, w_uv=v_w_uv, w_o=v_w_o, ln1_g=v_ln1_g,
             ln1_b=v_ln1_b, w_up=v_w_up, conv_w=v_conv_w, conv_b=v_conv_b, w_down=v_w_down, ln2_g=v_ln2_g, ln2_b=v_ln2_b)
    me = 4 * lax.axis_index("x") + 2 * lax.axis_index("y") + lax.axis_index("c")
    wire = lambda a: a.astype(WIRE_DTYPE)
    pad_taps = lambda a: jnp.pad(a, ((0, 8 - a.shape[0]), (0, 0)))

    own_slot = lambda buf, block: lax.dynamic_update_index_in_dim(buf, block, me, 0)
    blocks = lambda a: wire(a).reshape((N_DEV, a.shape[0] // N_DEV) + a.shape[1:])

    g_in, g_uq, g_conv = _all_gather(
        [wire(w_in).T, _heads_major(wire(w_uq)), pad_taps(conv_w)],
        "gather_weights")
    late = _send_start([wire(w_o), wire(w_up).T, wire(w_down)], [True] * 3, "gather_late_start")
    r_uq_dev, e_uq = w_uq.shape[0], w_uq.shape[2]
    wq = jnp.transpose(g_uq.reshape(N_DEV, HEADS, r_uq_dev, e_uq), (1, 0, 2, 3)).reshape(HEADS, Q_RANK, e_uq)
    cw = dict(
        w_in_t=_split_pad_rows(g_in.reshape(-1, D_MODEL)).astype(MXU_DTYPE),
        wq=jnp.pad(wq, ((0, 0), (0, 0), (0, LANES - e_uq))).astype(MXU_DTYPE),
        wk=_pad_heads(w_uk, NOPE_DIM), wv=_pad_heads(w_uv, HEAD_DIM),
        conv_w=_ffn_interleave(jnp.transpose(g_conv[:, :conv_w.shape[0]], (1, 0, 2)).reshape(conv_w.shape[0], -1), 1),
        g_cq=_row(g_cq), g_ckv=_row(g_ckv), ln1_g=_row(ln1_g), ln1_b=_row(ln1_b), conv_b=_ffn_interleave(_row(conv_b), 1),
        ln2_g=_row(ln2_g), ln2_b=_row(ln2_b))

    def late_weights(after):
        own, landed = _send_wait(late, [after], "gather_late_wait")
        g_o, g_up, g_down = [own_slot(buf, blk) for buf, blk in zip(landed, own)]
        w_o_mla, w_o_dil = _pad_w_o(g_o.reshape(-1, D_MODEL))
        return dict(w_o_mla=w_o_mla, w_o_dil=w_o_dil, w_up_t=_ffn_interleave(g_up.reshape(-1, D_MODEL), 0).astype(MXU_DTYPE),
                    w_down=g_down.reshape(-1, D_MODEL).astype(MXU_DTYPE))

    sent = {}

    def on_grads(stage, g):
        if stage == "ffn":
            sent[stage] = _send_start([blocks(_ffn_deinterleave(g["w_up_t"], 0)), blocks(g["w_down"])], [False] * 2,
                                      "exchange_ffn_start")
        elif stage == "w_o":
            return []
        elif stage == "mla":
            d_uq = wire(jnp.transpose(g["wq"][:, :, :e_uq].reshape(HEADS, N_DEV, r_uq_dev, e_uq), (1, 0, 2, 3))
                        ).reshape(N_DEV, HEADS * r_uq_dev, e_uq)
            dense = lambda a, width: wire(a[:, :, :width]).reshape(-1, LANES)
            small = dict(g_cq=g["g_cq"], g_ckv=g["g_ckv"], w_uk=dense(g["wk"], NOPE_DIM), w_uv=dense(g["wv"], HEAD_DIM),
                         ln1_g=g["ln1_g"], ln1_b=g["ln1_b"], conv_b=_ffn_deinterleave(g["conv_b"], 1), ln2_g=g["ln2_g"],
                         ln2_b=g["ln2_b"])
            everyone = [small[n] for n in REPLICATED] + [_ffn_deinterleave(g["conv_w"], 1), g["loss"]]
            sent[stage] = _send_start([blocks(_unpad_w_o(g["w_o_mla"], g["w_o_dil"])), d_uq] + everyone,
                                      [False] * 2 + [True] * len(everyone), "exchange_mla_start")
        else:
            sent[stage] = _send_start([blocks(_split_unpad_rows(g["w_in_t"]))], [False], "exchange_w_in_start")
        return [sent[stage]["token"]]

    _, grad_x, _ = _layer_grads(x[0], loss_target[0], cw, [late["token"]], late_weights, on_grads)

    def landed(handle, after, name):
        own, got = _send_wait(handle, after, name)
        pick = lambda src, whole: src if whole else lax.dynamic_index_in_dim(src, me, 0, keepdims=False)
        return [own_slot(buf, pick(src, whole)) for buf, src, whole in zip(got, own, handle["gather"])]

    out = {}

    def update(name, parts, view=None):
        to2d = {None: lambda a: a, "t": lambda a: a.T, "heads": _heads_major}[view]
        back = {None: lambda a: a, "t": lambda a: a.T, "heads": lambda a: _heads_minor(a, HEADS)}[view]
        res = _adamw(parts, to2d(w[name]), to2d(m[name]), to2d(v[name]), "adamw_" + name)
        for kind, a in zip(("grad", "delta", "new_m", "new_v"), res):
            out[kind, name] = back(a)
        return res[0]

    r_up, r_down = landed(sent["ffn"], [grad_x], "exchange_ffn_wait")
    r_o, r_uq, *rep_all, cw_all, loss_all = landed(sent["mla"], [grad_x], "exchange_mla_wait")
    done = [update("w_up", r_up, "t"), update("w_down", r_down), update("w_o", r_o), update("w_uq", r_uq, "heads")]
    heads_major = ("w_uk", "w_uv")
    two_d = lambda n, a: _heads_major(a) if n in heads_major else a.reshape(1, -1)
    rep_all = [p.reshape(N_DEV, -1, w[n].shape[2]) if n in heads_major else p for n, p in zip(REPLICATED, rep_all)]
    res, loss_sum = _adamw_small(rep_all, *[[two_d(n, d[n]) for n in REPLICATED] for d in (w, m, v)], loss_all, "adamw_replicated")
    for n, quad in zip(REPLICATED, res):
        for kind, a in zip(("grad", "delta", "new_m", "new_v"), quad):
            out[kind, n] = _heads_minor(a, HEADS) if n in heads_major else a.reshape(w[n].shape)
    loss = loss_sum[0, 0]
    ncw = conv_w.shape[1]
    done += [loss_sum, update("conv_w", lax.dynamic_slice_in_dim(cw_all[:, :conv_w.shape[0]], me * ncw, ncw, axis=2))]
    (r_in,) = landed(sent["w_in"], done, "exchange_w_in_wait")
    update("w_in", r_in, "t")

    return (loss, grad_x[None], *[out[kind, n] for kind in ("grad", "delta", "new_m", "new_v") for n in ALL_WEIGHTS])
```

```python
import math

import jax
import jax.numpy as jnp
import numpy as np
from jax import lax
from jax.experimental import pallas as pl
from jax.experimental.pallas import tpu as pltpu

F32 = jnp.float32
MXU_DTYPE = jnp.bfloat16
WIRE_DTYPE = jnp.bfloat16

N_DEV = 8
D_MODEL = 1024
HEADS = 8
HEAD_DIM = 64
LANES = 128
Q_RANK, KV_RANK, ROPE_DIM, NOPE_DIM = 256, 128, 32, 64
DIL_WIDTH = HEADS * HEAD_DIM
MLA_WIDTH = HEADS * HEAD_DIM
IN_PAD = 2048
DH_PART = IN_PAD // 4
D_FF = 2816
ROPE_THETA = 10000.0
DIL_PAIRS = ((128, 1), (512, 4), (2048, 16))
DIL_BLOCK = 128
DN_ALPHA = 2.0 ** 0.25
LN_EPS = 1e-5
RMS_EPS = 1e-6
ONES_LANE = HEAD_DIM
MLA_SCALE = 1.0 / math.sqrt(NOPE_DIM + ROPE_DIM)
MLA_SCALE_LOG2 = MLA_SCALE * math.log2(math.e)
MLA_BWD_SPLITS = 2
MLA_BWD_SPLITS_DIAGONAL = 4
MLA_FWD_SPLITS_DIAGONAL = 2
DIL_SCALE = 1.0 / math.sqrt(HEAD_DIM)
ALIBI_SLOPES = tuple(2.0 ** (-8.0 * (h + 1) / HEADS) for h in range(HEADS))
NEG_BIG = -1e30
ADAM_LR, ADAM_B1, ADAM_B2, ADAM_EPS, ADAM_WD, ADAM_STEP = 0.001, 0.9, 0.999, 1e-08, 0.01, 10
VMEM_LIMIT = 48 * 1024 * 1024


def _params(*sem):
    return pltpu.CompilerParams(dimension_semantics=sem or None, vmem_limit_bytes=VMEM_LIMIT)


def _dot(a, b, ca, cb):
    return lax.dot_general(a, b, (((ca,), (cb,)), ((), ())), preferred_element_type=F32)


_ANY_SPEC = pl.BlockSpec(memory_space=pl.ANY)


def _mm(a, b, *, name, tm, tn, tk, ta=False, tb=False, out_dtype=F32, after=()):
    m, k = (a.shape[1], a.shape[0]) if ta else a.shape
    n = b.shape[0] if tb else b.shape[1]
    assert (b.shape[1] if tb else b.shape[0]) == k
    tm, tn, tk = min(tm, m), min(tn, n), min(tk, k)
    assert m % tm == 0 and n % tn == 0 and k % tk == 0, (name, m, n, k, tm, tn, tk)
    nk = k // tk
    a_spec = (pl.BlockSpec((tk, tm), lambda i, j, kk: (kk, i)) if ta
              else pl.BlockSpec((tm, tk), lambda i, j, kk: (i, kk)))
    b_mode = dict(pipeline_mode=pl.Buffered(1)) if (tn == n and tk == k) else {}
    b_spec = (pl.BlockSpec((tn, tk), lambda i, j, kk: (j, kk), **b_mode) if tb
              else pl.BlockSpec((tk, tn), lambda i, j, kk: (kk, j), **b_mode))
    o_spec = pl.BlockSpec((tm, tn), lambda i, j, kk: (i, j))
    n_in = 2 + len(after)
    ca, cb = (0 if ta else 1), (1 if tb else 0)

    def body(*refs):
        a_ref, b_ref, o_ref = refs[0], refs[1], refs[n_in]
        part = _dot(a_ref[...].astype(MXU_DTYPE), b_ref[...].astype(MXU_DTYPE), ca, cb)
        if nk == 1:
            o_ref[...] = part.astype(o_ref.dtype)
            return
        acc_ref = refs[-1]
        kk = pl.program_id(2)

        @pl.when(kk == 0)
        def _():
            acc_ref[...] = part

        @pl.when(kk > 0)
        def _():
            acc_ref[...] += part

        @pl.when(kk == nk - 1)
        def _():
            o_ref[...] = acc_ref[...].astype(o_ref.dtype)

    return pl.pallas_call(
        body, name=name, grid=(m // tm, n // tn, nk), in_specs=[a_spec, b_spec] + [_ANY_SPEC] * len(after), out_specs=o_spec,
        out_shape=jax.ShapeDtypeStruct((m, n), out_dtype),
        scratch_shapes=[pltpu.VMEM((tm, tn), F32)] if nk > 1 else [],
        compiler_params=_params("parallel", "parallel", "arbitrary"),
    )(a, b, *after)


def _w_o_bwd(dz, o_mla, o_dil, w_o_mla, w_o_dil, tm=512):
    seq, d = dz.shape
    tm = min(tm, seq)
    nstep = seq // tm

    def body(a_ref, om_ref, od_ref, wm_ref, wd_ref, dom_ref, dod_ref, dwm_ref, dwd_ref, accm_ref, accd_ref):
        step = pl.program_id(0)

        @pl.when(step == 0)
        def _():
            accm_ref[...] = jnp.zeros_like(accm_ref)
            accd_ref[...] = jnp.zeros_like(accd_ref)

        a = a_ref[...]
        for hd in range(HEADS):
            rows = slice(LANES * hd, LANES * (hd + 1))
            dom_ref[hd] = _dot(a, wm_ref[rows, :], 1, 1)
            accm_ref[rows, :] += _dot(om_ref[hd], a, 0, 0)
        dod_ref[...] = _dot(a, wd_ref[...], 1, 1)
        accd_ref[...] += _dot(od_ref[...], a, 0, 0)

        @pl.when(step == nstep - 1)
        def _():
            dwm_ref[...] = accm_ref[...].astype(dwm_ref.dtype)
            dwd_ref[...] = accd_ref[...].astype(dwd_ref.dtype)

    once = dict(pipeline_mode=pl.Buffered(1))
    return pl.pallas_call(
        body, name="w_o_bwd", grid=(nstep,),
        in_specs=[pl.BlockSpec((tm, d), lambda i: (i, 0)), pl.BlockSpec((HEADS, tm, LANES), lambda i: (0, i, 0)),
                  pl.BlockSpec((tm, DIL_WIDTH), lambda i: (i, 0)), pl.BlockSpec((HEADS * LANES, d), lambda i: (0, 0), **once),
                  pl.BlockSpec((DIL_WIDTH, d), lambda i: (0, 0), **once)],
        out_specs=[pl.BlockSpec((HEADS, tm, LANES), lambda i: (0, i, 0)), pl.BlockSpec((tm, DIL_WIDTH), lambda i: (i, 0)),
                   pl.BlockSpec((HEADS * LANES, d), lambda i: (0, 0)), pl.BlockSpec((DIL_WIDTH, d), lambda i: (0, 0))],
        out_shape=[jax.ShapeDtypeStruct((HEADS, seq, LANES), F32), jax.ShapeDtypeStruct((seq, DIL_WIDTH), F32),
                   jax.ShapeDtypeStruct((HEADS * LANES, d), MXU_DTYPE), jax.ShapeDtypeStruct((DIL_WIDTH, d), MXU_DTYPE)],
        scratch_shapes=[pltpu.VMEM((HEADS * LANES, d), F32), pltpu.VMEM((DIL_WIDTH, d), F32)],
        compiler_params=_params("arbitrary"),
    )(dz, o_mla, o_dil, w_o_mla, w_o_dil)


def _rope_tables(seq):
    half = ROPE_DIM // 2
    f32 = np.float32
    freqs = np.power(f32(ROPE_THETA), -np.arange(half, dtype=f32) / f32(half))
    ang = np.arange(seq, dtype=f32)[:, None] * freqs[None, :]
    cos, sin = np.cos(ang, dtype=f32), np.sin(ang, dtype=f32)
    one = np.ones((seq, NOPE_DIM), f32)
    tail = np.ones((seq, LANES - NOPE_DIM - ROPE_DIM), f32)
    ctab = np.concatenate([one, cos, cos, tail], axis=1)
    stab = np.concatenate([0 * one, -sin, sin, 0 * tail], axis=1)
    return jnp.asarray(ctab), jnp.asarray(stab)


def _rope_swap(t):
    lane = lax.broadcasted_iota(jnp.int32, t.shape, 1)
    half = ROPE_DIM // 2
    return jnp.where(lane < NOPE_DIM + half, pltpu.roll(t, LANES - half, 1), pltpu.roll(t, half, 1))


def _rope(t, ctab, stab):
    return t * ctab + _rope_swap(t) * stab


def _rope_inv(t, ctab, stab):
    return t * ctab - _rope_swap(t) * stab


def _rms(x, g):
    r = lax.rsqrt(jnp.mean(x * x, axis=-1, keepdims=True) + RMS_EPS)
    xh = x * r
    return xh, r, xh * g


def _mla_prep(h, g_cq, g_ckv, wq, wk, wv, ctab, stab, tm=512):
    seq = h.shape[0]
    tm = min(tm, seq)

    def body(h_ref, gq_ref, gk_ref, wq_ref, wk_ref, wv_ref, c_ref, s_ref, q_out, k_out, v_out):
        hb = h_ref[...]
        ctab_, stab_ = c_ref[...], s_ref[...]
        _, _, cqn = _rms(hb[:, :Q_RANK], gq_ref[...])
        _, _, ckn = _rms(hb[:, Q_RANK:Q_RANK + KV_RANK], gk_ref[...])
        cqn = cqn.astype(MXU_DTYPE)
        ckn = ckn.astype(MXU_DTYPE)
        krr = _rope(hb[:, Q_RANK + KV_RANK:], ctab_, stab_)
        ones_lane = (lax.broadcasted_iota(jnp.int32, (1, LANES), 1) == ONES_LANE).astype(F32)
        for hd in range(HEADS):
            q = _dot(cqn, wq_ref[hd], 1, 0)
            q_out[hd] = _rope(q, ctab_, stab_).astype(q_out.dtype)
            k_out[hd] = (_dot(ckn, wk_ref[hd], 1, 0) + krr).astype(k_out.dtype)
            v_out[hd] = (_dot(ckn, wv_ref[hd], 1, 0) + ones_lane).astype(v_out.dtype)

    full = lambda *shape: pl.BlockSpec(shape, lambda i: (0,) * len(shape))
    slab = pl.BlockSpec((HEADS, tm, LANES), lambda i: (0, i, 0))
    shp = jax.ShapeDtypeStruct((HEADS, seq, LANES), MXU_DTYPE)
    return pl.pallas_call(
        body, name="mla_prep", grid=(seq // tm,),
        in_specs=[pl.BlockSpec((tm, DH_PART), lambda i: (i, 0)), full(1, Q_RANK), full(1, KV_RANK),
                  full(HEADS, Q_RANK, LANES), full(HEADS, KV_RANK, LANES), full(HEADS, KV_RANK, LANES),
                  pl.BlockSpec((tm, LANES), lambda i: (i, 0)), pl.BlockSpec((tm, LANES), lambda i: (i, 0))],
        out_specs=[slab, slab, slab], out_shape=[shp, shp, shp],
        compiler_params=_params("parallel"),
    )(h, g_cq, g_ckv, wq, wk, wv, ctab, stab)


def _mla_prep_bwd(h, g_cq, g_ckv, wq, wk, wv, ctab, stab, dq, dk, dv, tm=512, after=()):
    seq = h.shape[0]
    tm = min(tm, seq)
    n_after = len(after)

    def body(h_ref, gq_ref, gk_ref, wq_ref, wk_ref, wv_ref, c_ref, s_ref, dq_ref, dk_ref, dv_ref, *rest):
        dh_ref, dwq_ref, dwk_ref, dwv_ref, dgq_ref, dgk_ref = rest[n_after:]

        @pl.when(pl.program_id(0) == 0)
        def _():
            for r in (dwq_ref, dwk_ref, dwv_ref, dgq_ref, dgk_ref):
                r[...] = jnp.zeros_like(r)

        hb = h_ref[...]
        ctab_, stab_ = c_ref[...], s_ref[...]
        gq, gk = gq_ref[...], gk_ref[...]
        xq, rq, cqn = _rms(hb[:, :Q_RANK], gq)
        xk, rk, ckn = _rms(hb[:, Q_RANK:Q_RANK + KV_RANK], gk)
        cqn = cqn.astype(MXU_DTYPE)
        ckn = ckn.astype(MXU_DTYPE)
        d_cqn = jnp.zeros((tm, Q_RANK), F32)
        d_ckn = jnp.zeros((tm, KV_RANK), F32)
        d_krr = jnp.zeros((tm, LANES), F32)
        for hd in range(HEADS):
            dqh = _rope_inv(dq_ref[hd], ctab_, stab_).astype(MXU_DTYPE)
            d_cqn += _dot(dqh, wq_ref[hd], 1, 1)
            dwq_ref[hd] += _dot(cqn, dqh, 0, 0)
            dkh = dk_ref[hd]
            d_krr += dkh
            dkh = dkh.astype(MXU_DTYPE)
            d_ckn += _dot(dkh, wk_ref[hd], 1, 1)
            dwk_ref[hd] += _dot(ckn, dkh, 0, 0)
            dvh = dv_ref[hd].astype(MXU_DTYPE)
            d_ckn += _dot(dvh, wv_ref[hd], 1, 1)
            dwv_ref[hd] += _dot(ckn, dvh, 0, 0)
        lane = lax.broadcasted_iota(jnp.int32, (tm, LANES), 1)
        rot = (lane >= NOPE_DIM) & (lane < NOPE_DIM + ROPE_DIM)
        d_kr = jnp.where(rot, _rope_inv(jnp.where(rot, d_krr, 0.0), ctab_, stab_), 0.0)

        def rms_bwd(dy, xh, r, g, dg_ref):
            dg_ref[...] += jnp.sum(dy * xh, axis=0, keepdims=True)
            dxh = dy * g
            return r * (dxh - xh * jnp.mean(dxh * xh, axis=-1, keepdims=True))

        d_cq = rms_bwd(d_cqn, xq, rq, gq, dgq_ref)
        d_ck = rms_bwd(d_ckn, xk, rk, gk, dgk_ref)
        dh_ref[...] = jnp.concatenate([d_cq, d_ck, d_kr], axis=1).astype(dh_ref.dtype)

    full = lambda *shape: pl.BlockSpec(shape, lambda i: (0,) * len(shape))
    slab = pl.BlockSpec((HEADS, tm, LANES), lambda i: (0, i, 0))
    return pl.pallas_call(
        body, name="mla_prep_bwd", grid=(seq // tm,),
        in_specs=[pl.BlockSpec((tm, DH_PART), lambda i: (i, 0)), full(1, Q_RANK), full(1, KV_RANK),
                  full(HEADS, Q_RANK, LANES), full(HEADS, KV_RANK, LANES), full(HEADS, KV_RANK, LANES),
                  pl.BlockSpec((tm, LANES), lambda i: (i, 0)), pl.BlockSpec((tm, LANES), lambda i: (i, 0)),
                  slab, slab, slab] + [_ANY_SPEC] * n_after,
        out_specs=[pl.BlockSpec((tm, DH_PART), lambda i: (i, 0)), full(HEADS, Q_RANK, LANES), full(HEADS, KV_RANK, LANES),
                   full(HEADS, KV_RANK, LANES), full(1, Q_RANK), full(1, KV_RANK)],
        out_shape=[jax.ShapeDtypeStruct((seq, DH_PART), MXU_DTYPE), jax.ShapeDtypeStruct((HEADS, Q_RANK, LANES), F32),
                   jax.ShapeDtypeStruct((HEADS, KV_RANK, LANES), F32), jax.ShapeDtypeStruct((HEADS, KV_RANK, LANES), F32),
                   jax.ShapeDtypeStruct((1, Q_RANK), F32), jax.ShapeDtypeStruct((1, KV_RANK), F32)],
        compiler_params=_params("arbitrary"),
    )(h, g_cq, g_ckv, wq, wk, wv, ctab, stab, dq, dk, dv, *after)


def _mla_attn_fwd(q, k, v, t=1024):
    _, seq, _ = q.shape
    t = min(t, seq)

    def body(q_ref, k_ref, v_ref, o_ref, ob_ref, lse_ref, m_ref, acc_ref, s_ref):
        i = pl.program_id(1)
        qb = q_ref[...]
        m_ref[...] = jnp.full_like(m_ref, NEG_BIG)
        acc_ref[...] = jnp.zeros_like(acc_ref)

        def scores(j):
            return _dot(qb, k_ref[pl.ds(pl.multiple_of(j * t, t), t), :], 1, 1) * MLA_SCALE_LOG2

        def softmax_pv(j, s, rows=slice(None), mask=None):
            vb = v_ref[pl.ds(pl.multiple_of(j * t, t), s.shape[1]), :]
            if mask is not None:
                s = jnp.where(mask, s, NEG_BIG)
            m_old = m_ref[rows, :]
            m_new = jnp.maximum(m_old, jnp.max(s, axis=1, keepdims=True))
            p = jnp.exp2(s - m_new)
            a = jnp.exp2(m_old - m_new)
            acc_ref[rows, :] = a * acc_ref[rows, :] + _dot(p.astype(MXU_DTYPE), vb, 1, 0)
            m_ref[rows, :] = m_new

        def softmax_pv_diagonal(j):
            th = t // MLA_FWD_SPLITS_DIAGONAL
            for hf in range(MLA_FWD_SPLITS_DIAGONAL):
                nk = (hf + 1) * th
                row = lax.broadcasted_iota(jnp.int32, (th, nk), 0) + hf * th
                rows = slice(hf * th, (hf + 1) * th)
                softmax_pv(j, s_ref[rows, 0:nk], rows, row >= lax.broadcasted_iota(jnp.int32, (th, nk), 1))

        s_ref[...] = scores(0)

        def loop_body(j, c):
            s_next = scores(j + 1)
            softmax_pv(j, s_ref[...])
            s_ref[...] = s_next
            return c

        lax.fori_loop(0, i, loop_body, 0)
        softmax_pv_diagonal(i)
        acc = acc_ref[...]
        l = acc[:, ONES_LANE:ONES_LANE + 1]
        o = jnp.where(lax.broadcasted_iota(jnp.int32, acc.shape, 1) < HEAD_DIM, acc * (1.0 / l), 0.0)
        o_ref[...] = o
        ob_ref[...] = o.astype(ob_ref.dtype)
        lse_ref[...] = jnp.broadcast_to(m_ref[...] + jnp.log2(l), lse_ref.shape)

    blk = pl.BlockSpec((None, t, LANES), lambda h, i: (h, i, 0))
    whole = pl.BlockSpec((None, seq, LANES), lambda h, i: (h, 0, 0))
    shp = jax.ShapeDtypeStruct((HEADS, seq, LANES), F32)
    return pl.pallas_call(
        body, name="mla_attn_fwd", grid=(HEADS, seq // t),
        in_specs=[blk, whole, whole], out_specs=[blk, blk, blk],
        out_shape=[shp, jax.ShapeDtypeStruct((HEADS, seq, LANES), MXU_DTYPE), shp],
        scratch_shapes=[pltpu.VMEM((t, 1), F32), pltpu.VMEM((t, LANES), F32), pltpu.VMEM((t, t), F32)],
        compiler_params=_params("parallel", "arbitrary"),
    )(q, k, v)


def _mla_attn_bwd(q, k, v, o, lse, do, t=1024):
    _, seq, _ = q.shape
    t = min(t, seq)
    nb = seq // t

    def body(q_ref, k_ref, v_ref, o_ref, lse_ref, do_ref, dq_ref, dk_ref, dv_ref, dl_ref, dka_ref, dva_ref):
        dq_ref[...] = jnp.zeros_like(dq_ref)

        def delta_body(i, c):
            rows = pl.ds(pl.multiple_of(i * t, t), t)
            dl_ref[rows, :] = jnp.sum(do_ref[rows, :] * o_ref[rows, :], axis=1, keepdims=True)
            return c

        lax.fori_loop(0, nb, delta_body, 0)

        def kblock(j, c):
            krows = pl.ds(pl.multiple_of(j * t, t), t)
            kb = k_ref[krows, :]
            vb = v_ref[krows, :]
            dka_ref[...] = jnp.zeros_like(dka_ref)
            dva_ref[...] = jnp.zeros_like(dva_ref)

            def qstep(i, masked):
                ns = MLA_BWD_SPLITS_DIAGONAL if masked else MLA_BWD_SPLITS
                th = t // ns
                rows = [pl.ds(pl.multiple_of(i * t + hf * th, th), th) for hf in range(ns)]
                qs = [q_ref[r, :] for r in rows]
                dos = [do_ref[r, :].astype(MXU_DTYPE) for r in rows]
                nkeys = [(hf + 1) * th if masked else t for hf in range(ns)]
                ss = [_dot(qs[hf], kb[:nkeys[hf]], 1, 1) * MLA_SCALE_LOG2 for hf in range(ns)]
                dps = [_dot(dos[hf], vb[:nkeys[hf]], 1, 1) for hf in range(ns)]
                for hf in range(ns):
                    s, nk = ss[hf], nkeys[hf]
                    if masked:
                        row = lax.broadcasted_iota(jnp.int32, (th, nk), 0) + hf * th
                        s = jnp.where(row >= lax.broadcasted_iota(jnp.int32, (th, nk), 1), s, NEG_BIG)
                    p = jnp.exp2(s - lse_ref[rows[hf], 0:1])
                    dva_ref[0:nk, :] += _dot(p.astype(MXU_DTYPE), dos[hf], 0, 0)
                    ds = (p * (dps[hf] - dl_ref[rows[hf], :]) * MLA_SCALE).astype(MXU_DTYPE)
                    dka_ref[0:nk, :] += _dot(ds, qs[hf], 0, 0)
                    dq_ref[rows[hf], :] += _dot(ds, kb[:nk], 1, 0)

            qstep(j, True)

            def qloop(i, c2):
                qstep(i, False)
                return c2

            lax.fori_loop(j + 1, nb, qloop, 0)
            dk_ref[krows, :] = dka_ref[...]
            dv_ref[krows, :] = dva_ref[...]
            return c

        lax.fori_loop(0, nb, kblock, 0)

    whole = pl.BlockSpec((None, seq, LANES), lambda h: (h, 0, 0))
    shp = jax.ShapeDtypeStruct((HEADS, seq, LANES), F32)
    return pl.pallas_call(
        body, name="mla_attn_bwd", grid=(HEADS,),
        in_specs=[whole] * 6, out_specs=[whole] * 3, out_shape=[shp] * 3,
        scratch_shapes=[pltpu.VMEM((seq, 1), F32), pltpu.VMEM((t, LANES), F32), pltpu.VMEM((t, LANES), F32)],
        compiler_params=_params("parallel"),
    )(q, k, v, o, lse, do)


DIL_CHUNK = DIL_BLOCK * max(d for _, d in DIL_PAIRS)
DIL_PAIR_LANES = 2 * HEAD_DIM
assert DIL_PAIR_LANES == LANES
DIL_UNROLL_FWD = 16
DIL_UNROLL_BWD = 8


def _dil_bias_tables(hp, dil):
    b = DIL_BLOCK
    iq = lax.broadcasted_iota(jnp.int32, (b, 2 * b), 0)
    ik = lax.broadcasted_iota(jnp.int32, (b, 2 * b), 1)
    off = iq + b - ik
    band = (off >= 0) & (off <= b)
    dist = (off * dil).astype(F32)
    every, first = [], []
    for hh in range(2):
        slope = jnp.where(hp == 0, ALIBI_SLOPES[hh], jnp.where(hp == 1, ALIBI_SLOPES[2 + hh],
                          jnp.where(hp == 2, ALIBI_SLOPES[4 + hh], ALIBI_SLOPES[6 + hh]))).astype(F32)
        bias = -slope * dist
        every.append(jnp.where(band, bias, NEG_BIG))
        first.append(jnp.where(band & (ik >= b), bias, NEG_BIG))
    return jnp.concatenate(every, axis=0), jnp.concatenate(first, axis=0)


def _dil_rows(start, dil):
    return pl.ds(start, DIL_BLOCK) if dil == 1 else pl.ds(start, DIL_BLOCK, stride=dil)


def _dil_block_pos(blk, c, dil):
    sc, r = blk // dil, blk % dil
    q0 = sc * (DIL_BLOCK * dil) + r
    kcur0 = c * DIL_CHUNK + q0
    first = kcur0 < DIL_BLOCK * dil
    kprev0 = jnp.where(first, kcur0, kcur0 - DIL_BLOCK * dil)
    return q0, kcur0, kprev0, first


def _pair_cols(hh):
    return slice(HEAD_DIM * hh, HEAD_DIM * (hh + 1))


def _first_head_lanes(shape):
    return lax.broadcasted_iota(jnp.int32, shape, 1) < HEAD_DIM


def _stack_pair(t):
    first = _first_head_lanes(t.shape)
    return jnp.concatenate([jnp.where(first, t, 0.0), jnp.where(first, 0.0, t)], axis=0).astype(MXU_DTYPE)


def _unstack_pair(t):
    rows = t.shape[0] // 2
    return jnp.where(_first_head_lanes((rows, t.shape[1])), t[:rows], t[rows:])


def _pair_column(t):
    return jnp.concatenate([t[:, 0:1], t[:, HEAD_DIM:HEAD_DIM + 1]], axis=0)


def _dil_fwd(h):
    seq = h.shape[0]
    assert seq % DIL_CHUNK == 0
    nblk = DIL_CHUNK // DIL_BLOCK
    rc = 256

    def body(q_ref, k_ref, v_ref, o_ref, ob_ref, lse_ref, *scr):
        o_scr, l_scr = scr[:3], scr[3:]
        hp, c = pl.program_id(0), pl.program_id(1)
        for bi, (_, dil) in enumerate(DIL_PAIRS):
            tables = _dil_bias_tables(hp, dil)

            def block(blk, carry, bi=bi, dil=dil, tables=tables):
                q0, kcur0, kprev0, first = _dil_block_pos(blk, c, dil)
                q2 = _stack_pair(q_ref[_dil_rows(q0, dil), :] * DIL_SCALE)
                kcat = jnp.concatenate([k_ref[_dil_rows(kprev0, dil), :], k_ref[_dil_rows(kcur0, dil), :]], axis=0).astype(MXU_DTYPE)
                vcat = jnp.concatenate([v_ref[_dil_rows(kprev0, dil), :], v_ref[_dil_rows(kcur0, dil), :]], axis=0).astype(MXU_DTYPE)
                s = _dot(q2, kcat, 1, 1) + jnp.where(first, tables[1], tables[0])
                mx = jnp.max(s, axis=1, keepdims=True)
                p = jnp.exp(s - mx)
                l = jnp.sum(p, axis=1, keepdims=True)
                o_scr[bi][_dil_rows(q0, dil), :] = _unstack_pair(_dot(p.astype(MXU_DTYPE), vcat, 1, 0) * (1.0 / l))
                l_scr[bi][_dil_rows(q0, dil), :] = _unstack_pair(jnp.broadcast_to(mx + jnp.log(l), (2 * DIL_BLOCK, LANES)))
                return carry

            lax.fori_loop(0, nblk, block, 0, unroll=DIL_UNROLL_FWD)

        def combine(i, carry):
            rows = pl.ds(pl.multiple_of(i * rc, rc), rc)
            ls = [l_scr[bi][rows, :] for bi in range(3)]
            mx = jnp.maximum(jnp.maximum(ls[0], ls[1]), ls[2])
            es = [jnp.exp(l - mx) for l in ls]
            den = es[0] + es[1] + es[2]
            o = (es[0] * o_scr[0][rows, :] + es[1] * o_scr[1][rows, :] + es[2] * o_scr[2][rows, :]) / den
            o_ref[rows, :] = o
            ob_ref[rows, :] = o.astype(ob_ref.dtype)
            lse_ref[rows, :] = mx + jnp.log(den)
            return carry

        lax.fori_loop(0, DIL_CHUNK // rc, combine, 0)

    nq = DIL_WIDTH // LANES
    chunk = lambda off: pl.BlockSpec((DIL_CHUNK, LANES), lambda hp, c: (c, off + hp))
    whole = lambda off: pl.BlockSpec((seq, LANES), lambda hp, c: (0, off + hp))
    shp = jax.ShapeDtypeStruct((seq, DIL_WIDTH), F32)
    return pl.pallas_call(
        body, name="dil_fwd", grid=(nq, seq // DIL_CHUNK),
        in_specs=[chunk(nq), whole(2 * nq), whole(3 * nq)], out_specs=[chunk(0), chunk(0), chunk(0)],
        out_shape=[shp, jax.ShapeDtypeStruct((seq, DIL_WIDTH), MXU_DTYPE), shp],
        scratch_shapes=[pltpu.VMEM((DIL_CHUNK, LANES), F32)] * 6,
        compiler_params=_params("parallel", "arbitrary"),
    )(h, h, h)


def _dil_bwd(h, o, lse, do, after=()):
    seq = h.shape[0]
    nblk = DIL_CHUNK // DIL_BLOCK
    nchunk = seq // DIL_CHUNK
    rc = 256

    n_after = len(after)

    def body(q_ref, k_ref, v_ref, o_ref, lse_ref, do_ref, *rest):
        dq_out, dk_out, dv_out, dl_scr, dq_ref, dk_ref, dv_ref = rest[n_after:]
        hp, c = pl.program_id(0), pl.program_id(1)

        @pl.when(c == 0)
        def _():
            dk_ref[...] = jnp.zeros_like(dk_ref)
            dv_ref[...] = jnp.zeros_like(dv_ref)

        def delta(i, carry):
            rows = pl.ds(pl.multiple_of(i * rc, rc), rc)
            prod = do_ref[rows, :] * o_ref[rows, :]
            dl_scr[rows, :] = jnp.concatenate(
                [jnp.broadcast_to(jnp.sum(prod[:, _pair_cols(hh)], axis=1, keepdims=True), (rc, HEAD_DIM)) for hh in range(2)], axis=1)
            return carry

        lax.fori_loop(0, DIL_CHUNK // rc, delta, 0)

        for bi, (_, dil) in enumerate(DIL_PAIRS):
            tables = _dil_bias_tables(hp, dil)

            def block(blk, carry, bi=bi, dil=dil, tables=tables):
                q0, kcur0, kprev0, first = _dil_block_pos(blk, c, dil)
                qrows = _dil_rows(q0, dil)
                q2 = _stack_pair(q_ref[qrows, :] * DIL_SCALE)
                kcat = jnp.concatenate([k_ref[_dil_rows(kprev0, dil), :], k_ref[_dil_rows(kcur0, dil), :]], axis=0).astype(MXU_DTYPE)
                vcat = jnp.concatenate([v_ref[_dil_rows(kprev0, dil), :], v_ref[_dil_rows(kcur0, dil), :]], axis=0).astype(MXU_DTYPE)
                do2 = _stack_pair(do_ref[qrows, :])
                s = _dot(q2, kcat, 1, 1) + jnp.where(first, tables[1], tables[0])
                p = jnp.exp(s - _pair_column(lse_ref[qrows, :]))
                dp = _dot(do2, vcat, 1, 1)
                ds = (p * (dp - _pair_column(dl_scr[qrows, :]))).astype(MXU_DTYPE)
                dq_b = _unstack_pair(_dot(ds, kcat, 1, 0)) * DIL_SCALE
                dk_b = _dot(ds, q2, 0, 0)
                dv_b = _dot(p.astype(MXU_DTYPE), do2, 0, 0)
                if bi == 0:
                    dq_ref[qrows, :] = dq_b
                else:
                    dq_ref[qrows, :] += dq_b
                dk_ref[_dil_rows(kprev0, dil), :] += dk_b[:DIL_BLOCK]
                dv_ref[_dil_rows(kprev0, dil), :] += dv_b[:DIL_BLOCK]
                dk_ref[_dil_rows(kcur0, dil), :] += dk_b[DIL_BLOCK:]
                dv_ref[_dil_rows(kcur0, dil), :] += dv_b[DIL_BLOCK:]
                return carry

            lax.fori_loop(0, nblk, block, 0, unroll=DIL_UNROLL_BWD)

        dq_out[...] = dq_ref[...].astype(dq_out.dtype)

        @pl.when(c == nchunk - 1)
        def _():
            dk_out[...] = dk_ref[...].astype(dk_out.dtype)
            dv_out[...] = dv_ref[...].astype(dv_out.dtype)

    nq = DIL_WIDTH // LANES
    chunk = lambda off: pl.BlockSpec((DIL_CHUNK, LANES), lambda hp, c: (c, off + hp))
    whole = lambda off: pl.BlockSpec((seq, LANES), lambda hp, c: (0, off + hp))
    shp = jax.ShapeDtypeStruct((seq, DIL_WIDTH), MXU_DTYPE)
    return pl.pallas_call(
        body, name="dil_bwd", grid=(nq, nchunk),
        in_specs=[chunk(nq), whole(2 * nq), whole(3 * nq), chunk(0), chunk(0), chunk(0)] + [_ANY_SPEC] * n_after,
        out_specs=[chunk(0), whole(0), whole(0)], out_shape=[shp, shp, shp],
        scratch_shapes=[pltpu.VMEM((DIL_CHUNK, LANES), F32), pltpu.VMEM((DIL_CHUNK, LANES), F32),
                        pltpu.VMEM((seq, LANES), F32), pltpu.VMEM((seq, LANES), F32)],
        compiler_params=_params("parallel", "arbitrary"),
    )(h, h, h, o, lse, do, *after)


def _mm_dx0(parts, w_in_t, res, tm=1024, after=()):
    seq, d = res.shape
    tm = min(tm, seq)
    n_after = len(after)

    def body(a0, a1, a2, a3, b_ref, r_ref, *rest):
        o_ref = rest[n_after]
        acc = _dot(a0[...], b_ref[0:DH_PART, :], 1, 0)
        for c, a in enumerate((a1, a2, a3), start=1):
            acc += _dot(a[...], b_ref[DH_PART * c:DH_PART * (c + 1), :], 1, 0)
        o_ref[...] = acc + DN_ALPHA * r_ref[...]

    blk = pl.BlockSpec((tm, DH_PART), lambda i: (i, 0))
    row = pl.BlockSpec((tm, d), lambda i: (i, 0))
    return pl.pallas_call(
        body, name="mm_dx0", grid=(seq // tm,),
        in_specs=[blk] * 4 + [pl.BlockSpec((IN_PAD, d), lambda i: (0, 0), pipeline_mode=pl.Buffered(1)), row] + [_ANY_SPEC] * n_after,
        out_specs=row, out_shape=jax.ShapeDtypeStruct((seq, d), F32), compiler_params=_params("parallel"),
    )(*parts, w_in_t, res, *after)


def _mm_dw_in(parts, x0, tk=1024):
    seq, d = x0.shape
    tk = min(tk, seq)
    nk = seq // tk

    def body(a0, a1, a2, a3, b_ref, o_ref, acc_ref):
        kk = pl.program_id(0)

        @pl.when(kk == 0)
        def _():
            acc_ref[...] = jnp.zeros_like(acc_ref)

        b = b_ref[...].astype(MXU_DTYPE)
        for c, a in enumerate((a0, a1, a2, a3)):
            acc_ref[DH_PART * c:DH_PART * (c + 1), :] += _dot(a[...], b, 0, 0)

        @pl.when(kk == nk - 1)
        def _():
            o_ref[...] = acc_ref[...].astype(o_ref.dtype)

    blk = pl.BlockSpec((tk, DH_PART), lambda kk: (kk, 0))
    return pl.pallas_call(
        body, name="mm_dw_in", grid=(nk,), in_specs=[blk] * 4 + [pl.BlockSpec((tk, d), lambda kk: (kk, 0))],
        out_specs=pl.BlockSpec((IN_PAD, d), lambda kk: (0, 0)), out_shape=jax.ShapeDtypeStruct((IN_PAD, d), MXU_DTYPE),
        scratch_shapes=[pltpu.VMEM((IN_PAD, d), F32)], compiler_params=_params("arbitrary"),
    )(*parts, x0)


def _ln_stats(z):
    mu = jnp.mean(z, axis=-1, keepdims=True)
    zc = z - mu
    r = lax.rsqrt(jnp.mean(zc * zc, axis=-1, keepdims=True) + LN_EPS)
    return zc * r, r


def _ln_bwd_math(dy, xh, r, g):
    dxh = dy * g
    return r * (dxh - jnp.mean(dxh, axis=-1, keepdims=True) - xh * jnp.mean(dxh * xh, axis=-1, keepdims=True))


def _mix_ln1(o_mla, o_dil, w_o_mla, w_o_dil, x0, g, b, tm=512):
    seq, d = x0.shape
    tm = min(tm, seq)

    def body(om_ref, od_ref, wm_ref, wd_ref, x_ref, g_ref, b_ref, z_ref, y_ref, yb_ref):
        mix = _dot(od_ref[...], wd_ref[...], 1, 0)
        for hd in range(HEADS):
            mix += _dot(om_ref[hd], wm_ref[LANES * hd:LANES * (hd + 1), :], 1, 0)
        z = DN_ALPHA * x_ref[...] + mix
        xh, _ = _ln_stats(z)
        y = xh * g_ref[...] + b_ref[...]
        z_ref[...] = z
        y_ref[...] = y
        yb_ref[...] = y.astype(yb_ref.dtype)

    blk = pl.BlockSpec((tm, d), lambda i: (i, 0))
    vec = pl.BlockSpec((1, d), lambda i: (0, 0))
    shp = jax.ShapeDtypeStruct((seq, d), F32)
    return pl.pallas_call(
        body, name="mix_ln1", grid=(seq // tm,),
        in_specs=[pl.BlockSpec((HEADS, tm, LANES), lambda i: (0, i, 0)), pl.BlockSpec((tm, DIL_WIDTH), lambda i: (i, 0)),
                  pl.BlockSpec((HEADS * LANES, d), lambda i: (0, 0)), pl.BlockSpec((DIL_WIDTH, d), lambda i: (0, 0)), blk, vec, vec],
        out_specs=[blk, blk, blk], out_shape=[shp, shp, jax.ShapeDtypeStruct((seq, d), MXU_DTYPE)],
        compiler_params=_params("parallel"))(o_mla, o_dil, w_o_mla, w_o_dil, x0, g, b)


def _dx1_ln1_bwd(du, w_up_t, dz2, z, g, tm=256, after=()):
    seq, d = z.shape
    kdim = du.shape[1]
    tm = min(tm, seq)
    n_after = len(after)

    def body(du_ref, w_ref, r_ref, z_ref, g_ref, *rest):
        dz_ref, dzb_ref, dg_ref, db_ref = rest[n_after:]

        @pl.when(pl.program_id(0) == 0)
        def _():
            dg_ref[...] = jnp.zeros_like(dg_ref)
            db_ref[...] = jnp.zeros_like(db_ref)

        dyb = _dot(du_ref[...], w_ref[...], 1, 0) + DN_ALPHA * r_ref[...]
        xh, r = _ln_stats(z_ref[...])
        dg_ref[...] += jnp.sum(dyb * xh, axis=0, keepdims=True)
        db_ref[...] += jnp.sum(dyb, axis=0, keepdims=True)
        dz = _ln_bwd_math(dyb, xh, r, g_ref[...])
        dz_ref[...] = dz
        dzb_ref[...] = dz.astype(dzb_ref.dtype)

    blk = pl.BlockSpec((tm, d), lambda i: (i, 0))
    vec = pl.BlockSpec((1, d), lambda i: (0, 0))
    return pl.pallas_call(
        body, name="dx1_ln1_bwd", grid=(seq // tm,),
        in_specs=[pl.BlockSpec((tm, kdim), lambda i: (i, 0)),
                  pl.BlockSpec((kdim, d), lambda i: (0, 0), pipeline_mode=pl.Buffered(1)), blk, blk, vec] + [_ANY_SPEC] * n_after,
        out_specs=[blk, blk, vec, vec],
        out_shape=[jax.ShapeDtypeStruct((seq, d), F32), jax.ShapeDtypeStruct((seq, d), MXU_DTYPE),
                   jax.ShapeDtypeStruct((1, d), F32), jax.ShapeDtypeStruct((1, d), F32)],
        compiler_params=_params("arbitrary"))(du, w_up_t, dz2, z, g, *after)


def _down_ln2_loss_bwd(act, w_down, x1, target, g, b, tm=512):
    seq, d = x1.shape
    kdim = act.shape[1]
    tm = min(tm, seq)

    def body(a_ref, w_ref, x_ref, t_ref, g_ref, b_ref, dz_ref, dzb_ref, loss_ref, dg_ref, db_ref):
        @pl.when(pl.program_id(0) == 0)
        def _():
            loss_ref[...] = jnp.zeros_like(loss_ref)
            dg_ref[...] = jnp.zeros_like(dg_ref)
            db_ref[...] = jnp.zeros_like(db_ref)

        gv = g_ref[...]
        z = DN_ALPHA * x_ref[...] + _dot(a_ref[...], w_ref[...], 1, 0)
        xh, r = _ln_stats(z)
        err = (xh * gv + b_ref[...]) - t_ref[...]
        loss_ref[...] += 0.5 * jnp.sum(jnp.mean(err * err, axis=-1, keepdims=True), axis=0, keepdims=True)
        dy = err * (1.0 / d)
        dg_ref[...] += jnp.sum(dy * xh, axis=0, keepdims=True)
        db_ref[...] += jnp.sum(dy, axis=0, keepdims=True)
        dz = _ln_bwd_math(dy, xh, r, gv)
        dz_ref[...] = dz
        dzb_ref[...] = dz.astype(dzb_ref.dtype)

    blk = pl.BlockSpec((tm, d), lambda i: (i, 0))
    vec = pl.BlockSpec((1, d), lambda i: (0, 0))
    return pl.pallas_call(
        body, name="down_ln2_loss_bwd", grid=(seq // tm,),
        in_specs=[pl.BlockSpec((tm, kdim), lambda i: (i, 0)),
                  pl.BlockSpec((kdim, d), lambda i: (0, 0), pipeline_mode=pl.Buffered(1)), blk, blk, vec, vec],
        out_specs=[blk, blk, pl.BlockSpec((1, LANES), lambda i: (0, 0)), vec, vec],
        out_shape=[jax.ShapeDtypeStruct((seq, d), F32), jax.ShapeDtypeStruct((seq, d), MXU_DTYPE),
                   jax.ShapeDtypeStruct((1, LANES), F32),
                   jax.ShapeDtypeStruct((1, d), F32), jax.ShapeDtypeStruct((1, d), F32)],
        compiler_params=_params("arbitrary"))(act, w_down, x1, target, g, b)


HALO = 16


def _conv_rows(e, w_ref, b_ref):
    y = b_ref[...] + w_ref[0:1, :] * pltpu.roll(e, 2, 0)
    y = y + w_ref[1:2, :] * pltpu.roll(e, 1, 0)
    return y + w_ref[2:3, :] * e


_GELU_C = math.sqrt(2.0 / math.pi)
_GELU_A = 0.044715


def _gelu(x):
    return 0.5 * x * (1.0 + jnp.tanh(_GELU_C * (x + _GELU_A * (x * x * x))))


CONV_TN = 256


def _ffn_interleave(a, axis):
    shp = a.shape
    a = a.reshape(shp[:axis] + (2, D_FF // CONV_TN, CONV_TN) + shp[axis + 1:])
    return jnp.swapaxes(a, axis, axis + 1).reshape(shp)


def _ffn_deinterleave(a, axis):
    shp = a.shape
    a = a.reshape(shp[:axis] + (D_FF // CONV_TN, 2, CONV_TN) + shp[axis + 1:])
    return jnp.swapaxes(a, axis, axis + 1).reshape(shp)


def _conv_gate_fwd(u, conv_w, conv_b, tm=1024):
    seq = u.shape[0]
    tm = min(tm, seq)
    tn = CONV_TN

    def body(u_ref, up_ref, w_ref, b_ref, o_ref):
        first = pl.program_id(0) == 0
        e = jnp.concatenate([jnp.where(first, 0.0, up_ref[...]), u_ref[...]], axis=0)
        y = _conv_rows(e, w_ref, b_ref)[HALO:]
        o_ref[...] = (_gelu(y[:, tn:]) * y[:, :tn]).astype(o_ref.dtype)

    hb = tm // HALO
    return pl.pallas_call(
        body, name="conv_gate_fwd", grid=(seq // tm, D_FF // tn),
        in_specs=[pl.BlockSpec((tm, 2 * tn), lambda i, j: (i, j)),
                  pl.BlockSpec((HALO, 2 * tn), lambda i, j: (jnp.maximum(i * hb - 1, 0), j)),
                  pl.BlockSpec((3, 2 * tn), lambda i, j: (0, j)), pl.BlockSpec((1, 2 * tn), lambda i, j: (0, j))],
        out_specs=pl.BlockSpec((tm, tn), lambda i, j: (i, j)), out_shape=jax.ShapeDtypeStruct((seq, D_FF), MXU_DTYPE),
        compiler_params=_params("parallel", "parallel"),
    )(u, u, conv_w, conv_b)


def _conv_gate_bwd(u, d_act, conv_w, conv_b, tm=1024):
    seq = u.shape[0]
    tm = min(tm, seq)
    tn = CONV_TN
    ni = seq // tm
    rows_e = tm + 2 * HALO

    def body(u_ref, up_ref, un_ref, da_ref, dan_ref, w_ref, b_ref, du_ref, dw_ref, db_ref):
        i = pl.program_id(1)
        first, last = i == 0, i == ni - 1

        @pl.when(i == 0)
        def _():
            dw_ref[...] = jnp.zeros_like(dw_ref)
            db_ref[...] = jnp.zeros_like(db_ref)

        e = jnp.concatenate([jnp.where(first, 0.0, up_ref[...]), u_ref[...], jnp.where(last, 0.0, un_ref[...])], axis=0)
        y = _conv_rows(e, w_ref, b_ref)
        ya, yg = y[:, :tn], y[:, tn:]
        dact = jnp.concatenate([jnp.zeros((HALO, tn), F32), da_ref[...].astype(F32),
                                jnp.where(last, 0.0, dan_ref[...].astype(F32))], axis=0)
        th = jnp.tanh(_GELU_C * (yg + _GELU_A * (yg * yg * yg)))
        gelu = 0.5 * yg * (1.0 + th)
        gelu_grad = 0.5 * (1.0 + th) + 0.5 * yg * (1.0 - th * th) * (_GELU_C * (1.0 + 3.0 * _GELU_A * (yg * yg)))
        dy = jnp.concatenate([dact * gelu, dact * ya * gelu_grad], axis=1)
        du = w_ref[2:3, :] * dy + w_ref[1:2, :] * pltpu.roll(dy, rows_e - 1, 0) + w_ref[0:1, :] * pltpu.roll(dy, rows_e - 2, 0)
        du_ref[...] = du[HALO:HALO + tm].astype(du_ref.dtype)
        dyt = dy[HALO:HALO + tm]
        dw_ref[0:1, :] += jnp.sum(dyt * pltpu.roll(e, 2, 0)[HALO:HALO + tm], axis=0, keepdims=True)
        dw_ref[1:2, :] += jnp.sum(dyt * pltpu.roll(e, 1, 0)[HALO:HALO + tm], axis=0, keepdims=True)
        dw_ref[2:3, :] += jnp.sum(dyt * e[HALO:HALO + tm], axis=0, keepdims=True)
        db_ref[...] += jnp.sum(dyt, axis=0, keepdims=True)

    hb = tm // HALO
    nh = seq // HALO
    prev = lambda j, i: (jnp.maximum(i * hb - 1, 0), j)
    nxt = lambda j, i: (jnp.minimum((i + 1) * hb, nh - 1), j)
    return pl.pallas_call(
        body, name="conv_gate_bwd", grid=(D_FF // tn, ni),
        in_specs=[pl.BlockSpec((tm, 2 * tn), lambda j, i: (i, j)), pl.BlockSpec((HALO, 2 * tn), prev),
                  pl.BlockSpec((HALO, 2 * tn), nxt), pl.BlockSpec((tm, tn), lambda j, i: (i, j)), pl.BlockSpec((HALO, tn), nxt),
                  pl.BlockSpec((3, 2 * tn), lambda j, i: (0, j)), pl.BlockSpec((1, 2 * tn), lambda j, i: (0, j))],
        out_specs=[pl.BlockSpec((tm, 2 * tn), lambda j, i: (i, j)), pl.BlockSpec((3, 2 * tn), lambda j, i: (0, j)),
                   pl.BlockSpec((1, 2 * tn), lambda j, i: (0, j))],
        out_shape=[jax.ShapeDtypeStruct((seq, 2 * D_FF), MXU_DTYPE), jax.ShapeDtypeStruct((3, 2 * D_FF), F32),
                   jax.ShapeDtypeStruct((1, 2 * D_FF), F32)],
        compiler_params=_params("parallel", "arbitrary"),
    )(u, u, u, d_act, d_act, conv_w, conv_b)


def _pad_heads(w, width):
    w = jnp.transpose(w, (1, 0, 2))
    return jnp.pad(w, ((0, 0), (0, 0), (0, LANES - width))).astype(MXU_DTYPE)


def _heads_major(a):
    return jnp.transpose(a, (1, 0, 2)).reshape(-1, a.shape[2])


def _heads_minor(a, heads):
    return jnp.transpose(a.reshape(heads, -1, a.shape[1]), (1, 0, 2))


_LATENT = Q_RANK + KV_RANK
_ROPE_AT = _LATENT + NOPE_DIM
_ROPE_END = _ROPE_AT + ROPE_DIM


def _split_pad_rows(w_t):
    z = lambda n: jnp.zeros((n, w_t.shape[1]), w_t.dtype)
    return jnp.concatenate([w_t[:_LATENT], z(_ROPE_AT - _LATENT), w_t[_LATENT:_LATENT + ROPE_DIM], z(DH_PART - _ROPE_END),
                            w_t[_LATENT + ROPE_DIM:]], axis=0)


def _split_unpad_rows(w_p):
    return jnp.concatenate([w_p[:_LATENT], w_p[_ROPE_AT:_ROPE_END], w_p[DH_PART:]], axis=0)


def _pad_w_o(w_o):
    mla = jnp.pad(w_o[:MLA_WIDTH].reshape(HEADS, HEAD_DIM, D_MODEL), ((0, 0), (0, LANES - HEAD_DIM), (0, 0)))
    return mla.reshape(HEADS * LANES, D_MODEL).astype(MXU_DTYPE), w_o[MLA_WIDTH:].astype(MXU_DTYPE)


def _unpad_w_o(d_mla, d_dil):
    return jnp.concatenate([d_mla.reshape(HEADS, LANES, D_MODEL)[:, :HEAD_DIM].reshape(MLA_WIDTH, D_MODEL), d_dil], axis=0)


def _row(v):
    return v.reshape(1, -1).astype(F32)


def _layer_grads(x0, target, cw, first_after=(), late_weights=None, on_grads=None):
    seq = x0.shape[0]
    ctab, stab = _rope_tables(seq)
    gq, gk = cw["g_cq"], cw["g_ckv"]
    wq, wk, wv = cw["wq"], cw["wk"], cw["wv"]
    notify = (lambda stage, grads: ()) if on_grads is None else on_grads

    h = _mm(x0, cw["w_in_t"], name="mm_h", tb=True, tm=1024, tn=IN_PAD, tk=1024, after=first_after)
    qf, kf, vp = _mla_prep(h, gq, gk, wq, wk, wv, ctab, stab)
    o_mla, o_mla_b, lse_mla = _mla_attn_fwd(qf, kf, vp)
    o_dil, o_dil_b, lse_dil = _dil_fwd(h)
    fetch = (lambda stage, after: {}) if late_weights is None else late_weights
    cw = {**cw, **fetch("w_o", o_mla_b)}
    cb = cw["conv_b"]
    z1, x1, x1b = _mix_ln1(o_mla_b, o_dil_b, cw["w_o_mla"], cw["w_o_dil"], x0, cw["ln1_g"], cw["ln1_b"])
    cw = {**cw, **fetch("w_up", x1b)}
    u = _mm(x1b, cw["w_up_t"], name="mm_up", tb=True, tm=512, tn=2 * D_FF, tk=1024)
    act = _conv_gate_fwd(u, cw["conv_w"], cb)
    cw = {**cw, **fetch("w_down", act)}
    dz2, dz2b, loss, d_ln2_g, d_ln2_b = _down_ln2_loss_bwd(act, cw["w_down"], x1, target, cw["ln2_g"], cw["ln2_b"])

    d_act = _mm(dz2b, cw["w_down"], name="mm_d_act", tb=True, out_dtype=MXU_DTYPE, tm=1024, tn=D_FF, tk=1024)
    d_w_down = _mm(act, dz2b, name="mm_dw_down", ta=True, out_dtype=MXU_DTYPE, tm=1408, tn=1024, tk=1024)
    du, d_conv_w, d_conv_b = _conv_gate_bwd(u, d_act, cw["conv_w"], cb)
    d_w_up_t = _mm(du, x1b, name="mm_dw_up", ta=True, out_dtype=MXU_DTYPE, tm=1408, tn=1024, tk=2048)
    grads = dict(w_up_t=d_w_up_t, w_down=d_w_down, conv_w=d_conv_w, conv_b=d_conv_b, ln2_g=d_ln2_g, ln2_b=d_ln2_b)
    dz1, dz1b, d_ln1_g, d_ln1_b = _dx1_ln1_bwd(du, cw["w_up_t"], dz2, z1, cw["ln1_g"], after=notify("ffn", grads))
    do_mla, do_dil, d_w_o_mla, d_w_o_dil = _w_o_bwd(dz1b, o_mla_b, o_dil_b, cw["w_o_mla"], cw["w_o_dil"])
    grads.update(w_o_mla=d_w_o_mla, w_o_dil=d_w_o_dil, ln1_g=d_ln1_g, ln1_b=d_ln1_b)
    dqf, dkf, dvf = _mla_attn_bwd(qf, kf, vp, o_mla, lse_mla, do_mla)
    dh_mla, d_wq, d_wk, d_wv, d_gq, d_gk = _mla_prep_bwd(h, gq, gk, wq, wk, wv, ctab, stab, dqf, dkf, dvf,
                                                          after=notify("w_o", grads))
    grads.update(wq=d_wq, wk=d_wk, wv=d_wv, g_cq=d_gq, g_ckv=d_gk, loss=loss)
    dq_dil, dk_dil, dv_dil = _dil_bwd(h, o_dil, lse_dil, do_dil, after=notify("mla", grads))
    dh = (dh_mla, dq_dil, dk_dil, dv_dil)
    grads.update(w_in_t=_mm_dw_in(dh, x0))
    grad_x = _mm_dx0(dh, cw["w_in_t"], dz1, after=notify("w_in", grads))
    return loss, grad_x, grads


def _all_gather(blocks, name):
    na = len(blocks)

    def body(*refs):
        ins, outs = refs[:na], refs[na:2 * na]
        send_sems, recv_sems, local_sems = refs[2 * na:]
        x, y, c = lax.axis_index("x"), lax.axis_index("y"), lax.axis_index("c")
        me, sibling = (x, y, c), (x, y, 1 - c)
        chips = [(1 - x, y), (x, 1 - y), (1 - x, 1 - y)]

        def slot(out, pos):
            return out.at[4 * pos[0] + 2 * pos[1] + pos[2]]

        def copy(a, k, block, to, src=None):
            return pltpu.make_async_remote_copy(
                src_ref=slot(outs[a], block) if src is None else src, dst_ref=slot(outs[a], block),
                send_sem=send_sems.at[7 * a + k], recv_sem=recv_sems.at[7 * a + k],
                device_id=to, device_id_type=pl.DeviceIdType.MESH)

        mine = [pltpu.make_async_copy(ins[a], slot(outs[a], me), local_sems.at[a]) for a in range(na)]
        for cp in mine:
            cp.start()
        first = []
        for a in range(na):
            first.append(copy(a, 0, me, sibling, src=ins[a]))
            first += [copy(a, 1 + j, me, (*chip, c), src=ins[a]) for j, chip in enumerate(chips)]
        for cp in first:
            cp.start()
        passed = []
        for j, chip in enumerate(chips):
            for a in range(na):
                copy(a, 1 + j, (*chip, c), me).wait_recv()
                cp = copy(a, 4 + j, (*chip, c), sibling)
                cp.start()
                passed.append(cp)
        for a in range(na):
            copy(a, 0, sibling, me).wait_recv()
            for j, chip in enumerate(chips):
                copy(a, 4 + j, (*chip, 1 - c), me).wait_recv()
        for cp in first + passed:
            cp.wait_send()
        for cp in mine:
            cp.wait()

    any_spec = pl.BlockSpec(memory_space=pl.ANY)
    return pl.pallas_call(
        body, name=name, in_specs=[any_spec] * na, out_specs=[any_spec] * na,
        out_shape=[jax.ShapeDtypeStruct((N_DEV,) + b.shape, b.dtype) for b in blocks],
        scratch_shapes=[pltpu.SemaphoreType.DMA((7 * na,)), pltpu.SemaphoreType.DMA((7 * na,)), pltpu.SemaphoreType.DMA((na,))],
    )(*blocks)


_HBM_SPEC = pl.BlockSpec(memory_space=pltpu.HBM)
_SEM_SPEC = pl.BlockSpec(memory_space=pltpu.SEMAPHORE)
_DATAFLOW = pltpu.CompilerParams(has_side_effects=pltpu.SideEffectType.DATAFLOW_SIDE_EFFECTING)


def _split_copies(ins, lands, send_sems, recv_sems, gather):
    x, y, c = lax.axis_index("x"), lax.axis_index("y"), lax.axis_index("c")
    me = 4 * x + 2 * y + c
    copies = []
    for a in range(len(ins)):
        for d in range(1, N_DEV):
            px, py, pc = x ^ (d >> 2), y ^ ((d >> 1) & 1), c ^ (d & 1)
            copies.append(pltpu.make_async_remote_copy(
                src_ref=ins[a] if gather[a] else ins[a].at[4 * px + 2 * py + pc], dst_ref=lands[a].at[me],
                send_sem=send_sems.at[7 * a + d - 1], recv_sem=recv_sems.at[7 * a + d - 1],
                device_id=(px, py, pc), device_id_type=pl.DeviceIdType.MESH))
    return copies


def _send_start(srcs, gather, name):
    na = len(srcs)
    assert len(gather) == na
    land_types = [pltpu.HBM(((N_DEV,) + s.shape) if g else s.shape, s.dtype) for s, g in zip(srcs, gather)]

    def body(*refs):
        ins, lands = refs[:na], refs[na:2 * na]
        send_sems, recv_sems, token = refs[2 * na], refs[2 * na + 1], refs[-1]
        for cp in _split_copies(ins, lands, send_sems, recv_sems, gather):
            cp.start()
        token[...] = jnp.zeros_like(token)

    hbm = lambda a: pltpu.with_memory_space_constraint(a, pltpu.HBM)
    outs = pl.pallas_call(
        body, name=name,
        out_shape=(pltpu.SemaphoreType.DMA((7 * na,)), pltpu.SemaphoreType.DMA((7 * na,)),
                   *[pltpu.HBM(s.shape, s.dtype) for s in srcs], *land_types, jax.ShapeDtypeStruct((8, LANES), F32)),
        in_specs=[_HBM_SPEC] * (2 * na),
        out_specs=(_SEM_SPEC, _SEM_SPEC, *[_HBM_SPEC] * (2 * na), pl.BlockSpec(memory_space=pltpu.VMEM)),
        input_output_aliases={i: 2 + i for i in range(2 * na)}, compiler_params=_DATAFLOW,
    )(*[hbm(s) for s in srcs], *[hbm(lax.empty(t.shape, t.dtype)) for t in land_types])
    return dict(send=outs[0], recv=outs[1], srcs=list(outs[2:2 + na]), lands=list(outs[2 + na:2 + 2 * na]), token=outs[-1],
                gather=gather)


def _send_wait(handle, after, name, only=None):
    na = len(handle["srcs"])
    gather = handle["gather"]
    after = list(after)
    chosen = range(na) if only is None else only

    def body(*refs):
        ins, lands = refs[:na], refs[na:2 * na]
        send_sems, recv_sems = refs[2 * na], refs[2 * na + 1]
        copies = _split_copies(ins, lands, send_sems, recv_sems, gather)
        for a in chosen:
            for cp in copies[(N_DEV - 1) * a:(N_DEV - 1) * (a + 1)]:
                cp.wait_send()
                cp.wait_recv()

    both = handle["srcs"] + handle["lands"]
    outs = pl.pallas_call(
        body, name=name, out_shape=[pltpu.HBM(a.shape, a.dtype) for a in both],
        in_specs=[_HBM_SPEC] * (2 * na) + [_SEM_SPEC, _SEM_SPEC] + [_ANY_SPEC] * len(after),
        out_specs=[_HBM_SPEC] * (2 * na), input_output_aliases={i: i for i in range(2 * na)}, compiler_params=_DATAFLOW,
    )(*both, handle["send"], handle["recv"], *after)
    srcs, lands = list(outs[:na]), list(outs[na:])
    return srcs, lands, {**handle, "srcs": srcs, "lands": lands}


def _sum_slots(p_ref):
    g = p_ref[0].astype(F32)
    for s in range(1, p_ref.shape[0]):
        g = g + p_ref[s].astype(F32)
    return g


def _adamw_refs(g, w_ref, m_ref, v_ref, g_out, d_out, m_out, v_out):
    c1 = 1.0 - ADAM_B1 ** ADAM_STEP
    c2 = 1.0 - ADAM_B2 ** ADAM_STEP
    m_new = ADAM_B1 * m_ref[...] + (1.0 - ADAM_B1) * g
    v_new = ADAM_B2 * v_ref[...] + (1.0 - ADAM_B2) * (g * g)
    g_out[...] = g
    m_out[...] = m_new
    v_out[...] = v_new
    d_out[...] = -ADAM_LR * ((m_new / c1) / (jnp.sqrt(v_new / c2) + ADAM_EPS) + ADAM_WD * w_ref[...])


def _adamw(parts, w, m, v, name):
    npart, r, n = parts.shape
    tr = r if r <= 256 else max(t for t in range(16, 257, 16) if r % t == 0)

    def body(p_ref, w_ref, m_ref, v_ref, g_out, d_out, m_out, v_out):
        _adamw_refs(_sum_slots(p_ref), w_ref, m_ref, v_ref, g_out, d_out, m_out, v_out)

    blk = pl.BlockSpec((tr, n), lambda i: (i, 0))
    shp = jax.ShapeDtypeStruct((r, n), F32)
    return pl.pallas_call(
        body, name=name, grid=(r // tr,), in_specs=[pl.BlockSpec((npart, tr, n), lambda i: (0, i, 0)), blk, blk, blk],
        out_specs=[blk] * 4, out_shape=[shp] * 4, compiler_params=_params("parallel"),
    )(parts, w, m, v)


def _adamw_small(parts, ws, ms, vs, loss_parts, name):
    n = len(parts)

    def body(*refs):
        ins, outs = refs[:4 * n + 1], refs[4 * n + 1:]
        for i in range(n):
            _adamw_refs(_sum_slots(ins[i]), ins[n + i], ins[2 * n + i], ins[3 * n + i], *outs[4 * i:4 * i + 4])
        outs[4 * n][...] = _sum_slots(ins[4 * n])

    out_shape = [jax.ShapeDtypeStruct(w.shape, F32) for w in ws for _ in range(4)]
    res = pl.pallas_call(body, name=name, out_shape=out_shape + [jax.ShapeDtypeStruct((1, LANES), F32)],
                         compiler_params=_params())(*parts, *ws, *ms, *vs, loss_parts)
    return [res[4 * i:4 * i + 4] for i in range(n)], res[4 * n]


REPLICATED = ("g_cq", "g_ckv", "w_uk", "w_uv", "ln1_g", "ln1_b", "conv_b", "ln2_g", "ln2_b")
ALL_WEIGHTS = ("w_in", "g_cq", "g_ckv", "w_uq", "w_uk", "w_uv", "w_o", "ln1_g", "ln1_b", "w_up", "conv_w", "conv_b",
               "w_down", "ln2_g", "ln2_b")


def kernel(x, w_in, g_cq, g_ckv, w_uq, w_uk, w_uv, w_o, ln1_g, ln1_b, w_up, conv_w, conv_b, w_down, ln2_g, ln2_b, loss_target, m_w_in, m_g_cq, m_g_ckv, m_w_uq, m_w_uk, m_w_uv, m_w_o, m_ln1_g, m_ln1_b, m_w_up, m_conv_w, m_conv_b, m_w_down, m_ln2_g, m_ln2_b, v_w_in, v_g_cq, v_g_ckv, v_w_uq, v_w_uk, v_w_uv, v_w_o, v_ln1_g, v_ln1_b, v_w_up, v_conv_w, v_conv_b, v_w_down, v_ln2_g, v_ln2_b):
    w = dict(w_in=w_in, g_cq=g_cq, g_ckv=g_ckv, w_uq=w_uq, w_uk=w_uk, w_uv=w_uv, w_o=w_o, ln1_g=ln1_g, ln1_b=ln1_b,
             w_up=w_up, conv_w=conv_w, conv_b=conv_b, w_down=w_down, ln2_g=ln2_g, ln2_b=ln2_b)
    m = dict(w_in=m_w_in, g_cq=m_g_cq, g_ckv=m_g_ckv, w_uq=m_w_uq, w_uk=m_w_uk, w_uv=m_w_uv, w_o=m_w_o, ln1_g=m_ln1_g,
             ln1_b=m_ln1_b, w_up=m_w_up, conv_w=m_conv_w, conv_b=m_conv_b, w_down=m_w_down, ln2_g=m_ln2_g, ln2_b=m_ln2_b)
    v = dict(w_in=v_w_in, g_cq=v_g_cq, g_ckv=v_g_ckv, w_uq=v_w_uq, w_uk=v_w_uk, w_uv=v_w_uv, w_o=v_w_o, ln1_g=v_ln1_g,
             ln1_b=v_ln1_b, w_up=v_w_up, conv_w=v_conv_w, conv_b=v_conv_b, w_down=v_w_down, ln2_g=v_ln2_g, ln2_b=v_ln2_b)
    me = 4 * lax.axis_index("x") + 2 * lax.axis_index("y") + lax.axis_index("c")
    wire = lambda a: a.astype(WIRE_DTYPE)
    pad_taps = lambda a: jnp.pad(a, ((0, 8 - a.shape[0]), (0, 0)))

    own_slot = lambda buf, block: lax.dynamic_update_index_in_dim(buf, block, me, 0)
    blocks = lambda a: wire(a).reshape((N_DEV, a.shape[0] // N_DEV) + a.shape[1:])

    g_in, g_uq, g_conv = _all_gather(
        [wire(w_in).T, _heads_major(wire(w_uq)), pad_taps(conv_w)],
        "gather_weights")
    late = _send_start([wire(w_o), wire(w_up).T, wire(w_down)], [True] * 3, "gather_late_start")
    r_uq_dev, e_uq = w_uq.shape[0], w_uq.shape[2]
    wq = jnp.transpose(g_uq.reshape(N_DEV, HEADS, r_uq_dev, e_uq), (1, 0, 2, 3)).reshape(HEADS, Q_RANK, e_uq)
    cw = dict(
        w_in_t=_split_pad_rows(g_in.reshape(-1, D_MODEL)).astype(MXU_DTYPE),
        wq=jnp.pad(wq, ((0, 0), (0, 0), (0, LANES - e_uq))).astype(MXU_DTYPE),
        wk=_pad_heads(w_uk, NOPE_DIM), wv=_pad_heads(w_uv, HEAD_DIM),
        conv_w=_ffn_interleave(jnp.transpose(g_conv[:, :conv_w.shape[0]], (1, 0, 2)).reshape(conv_w.shape[0], -1), 1),
        g_cq=_row(g_cq), g_ckv=_row(g_ckv), ln1_g=_row(ln1_g), ln1_b=_row(ln1_b), conv_b=_ffn_interleave(_row(conv_b), 1),
        ln2_g=_row(ln2_g), ln2_b=_row(ln2_b))

    in_flight = {"late": late}

    def late_weights(stage, after):
        which = ("w_o", "w_up", "w_down").index(stage)
        own, got, in_flight["late"] = _send_wait(in_flight["late"], [after], f"gather_{stage}_wait", only=[which])
        full = own_slot(got[which], own[which]).reshape(-1, D_MODEL)
        if stage == "w_o":
            w_o_mla, w_o_dil = _pad_w_o(full)
            return dict(w_o_mla=w_o_mla, w_o_dil=w_o_dil)
        if stage == "w_up":
            return dict(w_up_t=_ffn_interleave(full, 0).astype(MXU_DTYPE))
        return dict(w_down=full.astype(MXU_DTYPE))

    sent = {}

    def on_grads(stage, g):
        if stage == "ffn":
            sent[stage] = _send_start([blocks(_ffn_deinterleave(g["w_up_t"], 0)), blocks(g["w_down"])], [False] * 2,
                                      "exchange_ffn_start")
        elif stage == "w_o":
            return []
        elif stage == "mla":
            d_uq = wire(jnp.transpose(g["wq"][:, :, :e_uq].reshape(HEADS, N_DEV, r_uq_dev, e_uq), (1, 0, 2, 3))
                        ).reshape(N_DEV, HEADS * r_uq_dev, e_uq)
            dense = lambda a, width: wire(a[:, :, :width]).reshape(-1, LANES)
            small = dict(g_cq=g["g_cq"], g_ckv=g["g_ckv"], w_uk=dense(g["wk"], NOPE_DIM), w_uv=dense(g["wv"], HEAD_DIM),
                         ln1_g=g["ln1_g"], ln1_b=g["ln1_b"], conv_b=_ffn_deinterleave(g["conv_b"], 1), ln2_g=g["ln2_g"],
                         ln2_b=g["ln2_b"])
            everyone = [small[n] for n in REPLICATED] + [_ffn_deinterleave(g["conv_w"], 1), g["loss"]]
            sent[stage] = _send_start([blocks(_unpad_w_o(g["w_o_mla"], g["w_o_dil"])), d_uq] + everyone,
                                      [False] * 2 + [True] * len(everyone), "exchange_mla_start")
        else:
            sent[stage] = _send_start([blocks(_split_unpad_rows(g["w_in_t"]))], [False], "exchange_w_in_start")
        return [sent[stage]["token"]]

    _, grad_x, _ = _layer_grads(x[0], loss_target[0], cw, [late["token"]], late_weights, on_grads)

    def landed(handle, after, name):
        own, got, _ = _send_wait(handle, after, name)
        pick = lambda src, whole: src if whole else lax.dynamic_index_in_dim(src, me, 0, keepdims=False)
        return [own_slot(buf, pick(src, whole)) for buf, src, whole in zip(got, own, handle["gather"])]

    out = {}

    def update(name, parts, view=None):
        to2d = {None: lambda a: a, "t": lambda a: a.T, "heads": _heads_major}[view]
        back = {None: lambda a: a, "t": lambda a: a.T, "heads": lambda a: _heads_minor(a, HEADS)}[view]
        res = _adamw(parts, to2d(w[name]), to2d(m[name]), to2d(v[name]), "adamw_" + name)
        for kind, a in zip(("grad", "delta", "new_m", "new_v"), res):
            out[kind, name] = back(a)
        return res[0]

    r_up, r_down = landed(sent["ffn"], [grad_x], "exchange_ffn_wait")
    r_o, r_uq, *rep_all, cw_all, loss_all = landed(sent["mla"], [grad_x], "exchange_mla_wait")
    done = [update("w_up", r_up, "t"), update("w_down", r_down), update("w_o", r_o), update("w_uq", r_uq, "heads")]
    heads_major = ("w_uk", "w_uv")
    two_d = lambda n, a: _heads_major(a) if n in heads_major else a.reshape(1, -1)
    rep_all = [p.reshape(N_DEV, -1, w[n].shape[2]) if n in heads_major else p for n, p in zip(REPLICATED, rep_all)]
    res, loss_sum = _adamw_small(rep_all, *[[two_d(n, d[n]) for n in REPLICATED] for d in (w, m, v)], loss_all, "adamw_replicated")
    for n, quad in zip(REPLICATED, res):
        for kind, a in zip(("grad", "delta", "new_m", "new_v"), quad):
            out[kind, n] = _heads_minor(a, HEADS) if n in heads_major else a.reshape(w[n].shape)
    loss = loss_sum[0, 0]
    ncw = conv_w.shape[1]
    done += [loss_sum, update("conv_w", lax.dynamic_slice_in_dim(cw_all[:, :conv_w.shape[0]], me * ncw, ncw, axis=2))]
    (r_in,) = landed(sent["w_in"], done, "exchange_w_in_wait")
    update("w_in", r_in, "t")

    return (loss, grad_x[None], *[out[kind, n] for kind in ("grad", "delta", "new_m", "new_v") for n in ALL_WEIGHTS])
```

```python
import math

import jax
import jax.numpy as jnp
import numpy as np
from jax import lax
from jax.experimental import pallas as pl
from jax.experimental.pallas import tpu as pltpu

F32 = jnp.float32
MXU_DTYPE = jnp.bfloat16
WIRE_DTYPE = jnp.bfloat16

N_DEV = 8
D_MODEL = 1024
HEADS = 8
HEAD_DIM = 64
LANES = 128
Q_RANK, KV_RANK, ROPE_DIM, NOPE_DIM = 256, 128, 32, 64
DIL_WIDTH = HEADS * HEAD_DIM
MLA_WIDTH = HEADS * HEAD_DIM
IN_PAD = 2048
DH_PART = IN_PAD // 4
D_FF = 2816
ROPE_THETA = 10000.0
DIL_PAIRS = ((128, 1), (512, 4), (2048, 16))
DIL_BLOCK = 128
DN_ALPHA = 2.0 ** 0.25
LN_EPS = 1e-5
RMS_EPS = 1e-6
ONES_LANE = HEAD_DIM
MLA_SCALE = 1.0 / math.sqrt(NOPE_DIM + ROPE_DIM)
MLA_SCALE_LOG2 = MLA_SCALE * math.log2(math.e)
MLA_BWD_SPLITS = 2
MLA_BWD_SPLITS_DIAGONAL = 4
MLA_FWD_SPLITS_DIAGONAL = 2
DIL_SCALE = 1.0 / math.sqrt(HEAD_DIM)
ALIBI_SLOPES = tuple(2.0 ** (-8.0 * (h + 1) / HEADS) for h in range(HEADS))
NEG_BIG = -1e30
ADAM_LR, ADAM_B1, ADAM_B2, ADAM_EPS, ADAM_WD, ADAM_STEP = 0.001, 0.9, 0.999, 1e-08, 0.01, 10
VMEM_LIMIT = 48 * 1024 * 1024


def _params(*sem):
    return pltpu.CompilerParams(dimension_semantics=sem or None, vmem_limit_bytes=VMEM_LIMIT)


def _dot(a, b, ca, cb):
    return lax.dot_general(a, b, (((ca,), (cb,)), ((), ())), preferred_element_type=F32)


_ANY_SPEC = pl.BlockSpec(memory_space=pl.ANY)


def _mm(a, b, *, name, tm, tn, tk, ta=False, tb=False, out_dtype=F32, after=()):
    m, k = (a.shape[1], a.shape[0]) if ta else a.shape
    n = b.shape[0] if tb else b.shape[1]
    assert (b.shape[1] if tb else b.shape[0]) == k
    tm, tn, tk = min(tm, m), min(tn, n), min(tk, k)
    assert m % tm == 0 and n % tn == 0 and k % tk == 0, (name, m, n, k, tm, tn, tk)
    nk = k // tk
    a_spec = (pl.BlockSpec((tk, tm), lambda i, j, kk: (kk, i)) if ta
              else pl.BlockSpec((tm, tk), lambda i, j, kk: (i, kk)))
    b_mode = dict(pipeline_mode=pl.Buffered(1)) if (tn == n and tk == k) else {}
    b_spec = (pl.BlockSpec((tn, tk), lambda i, j, kk: (j, kk), **b_mode) if tb
              else pl.BlockSpec((tk, tn), lambda i, j, kk: (kk, j), **b_mode))
    o_spec = pl.BlockSpec((tm, tn), lambda i, j, kk: (i, j))
    n_in = 2 + len(after)
    ca, cb = (0 if ta else 1), (1 if tb else 0)

    def body(*refs):
        a_ref, b_ref, o_ref = refs[0], refs[1], refs[n_in]
        part = _dot(a_ref[...].astype(MXU_DTYPE), b_ref[...].astype(MXU_DTYPE), ca, cb)
        if nk == 1:
            o_ref[...] = part.astype(o_ref.dtype)
            return
        acc_ref = refs[-1]
        kk = pl.program_id(2)

        @pl.when(kk == 0)
        def _():
            acc_ref[...] = part

        @pl.when(kk > 0)
        def _():
            acc_ref[...] += part

        @pl.when(kk == nk - 1)
        def _():
            o_ref[...] = acc_ref[...].astype(o_ref.dtype)

    return pl.pallas_call(
        body, name=name, grid=(m // tm, n // tn, nk), in_specs=[a_spec, b_spec] + [_ANY_SPEC] * len(after), out_specs=o_spec,
        out_shape=jax.ShapeDtypeStruct((m, n), out_dtype),
        scratch_shapes=[pltpu.VMEM((tm, tn), F32)] if nk > 1 else [],
        compiler_params=_params("parallel", "parallel", "arbitrary"),
    )(a, b, *after)


def _w_o_bwd(dz, o_mla, o_dil, w_o_mla, w_o_dil, tm=512):
    seq, d = dz.shape
    tm = min(tm, seq)
    nstep = seq // tm

    def body(a_ref, om_ref, od_ref, wm_ref, wd_ref, dom_ref, dod_ref, dwm_ref, dwd_ref, accm_ref, accd_ref):
        step = pl.program_id(0)

        @pl.when(step == 0)
        def _():
            accm_ref[...] = jnp.zeros_like(accm_ref)
            accd_ref[...] = jnp.zeros_like(accd_ref)

        a = a_ref[...]
        for hd in range(HEADS):
            rows = slice(LANES * hd, LANES * (hd + 1))
            dom_ref[hd] = _dot(a, wm_ref[rows, :], 1, 1)
            accm_ref[rows, :] += _dot(om_ref[hd], a, 0, 0)
        dod_ref[...] = _dot(a, wd_ref[...], 1, 1)
        accd_ref[...] += _dot(od_ref[...], a, 0, 0)

        @pl.when(step == nstep - 1)
        def _():
            dwm_ref[...] = accm_ref[...].astype(dwm_ref.dtype)
            dwd_ref[...] = accd_ref[...].astype(dwd_ref.dtype)

    once = dict(pipeline_mode=pl.Buffered(1))
    return pl.pallas_call(
        body, name="w_o_bwd", grid=(nstep,),
        in_specs=[pl.BlockSpec((tm, d), lambda i: (i, 0)), pl.BlockSpec((HEADS, tm, LANES), lambda i: (0, i, 0)),
                  pl.BlockSpec((tm, DIL_WIDTH), lambda i: (i, 0)), pl.BlockSpec((HEADS * LANES, d), lambda i: (0, 0), **once),
                  pl.BlockSpec((DIL_WIDTH, d), lambda i: (0, 0), **once)],
        out_specs=[pl.BlockSpec((HEADS, tm, LANES), lambda i: (0, i, 0)), pl.BlockSpec((tm, DIL_WIDTH), lambda i: (i, 0)),
                   pl.BlockSpec((HEADS * LANES, d), lambda i: (0, 0)), pl.BlockSpec((DIL_WIDTH, d), lambda i: (0, 0))],
        out_shape=[jax.ShapeDtypeStruct((HEADS, seq, LANES), F32), jax.ShapeDtypeStruct((seq, DIL_WIDTH), F32),
                   jax.ShapeDtypeStruct((HEADS * LANES, d), MXU_DTYPE), jax.ShapeDtypeStruct((DIL_WIDTH, d), MXU_DTYPE)],
        scratch_shapes=[pltpu.VMEM((HEADS * LANES, d), F32), pltpu.VMEM((DIL_WIDTH, d), F32)],
        compiler_params=_params("arbitrary"),
    )(dz, o_mla, o_dil, w_o_mla, w_o_dil)


def _rope_tables(seq):
    half = ROPE_DIM // 2
    f32 = np.float32
    freqs = np.power(f32(ROPE_THETA), -np.arange(half, dtype=f32) / f32(half))
    ang = np.arange(seq, dtype=f32)[:, None] * freqs[None, :]
    cos, sin = np.cos(ang, dtype=f32), np.sin(ang, dtype=f32)
    one = np.ones((seq, NOPE_DIM), f32)
    tail = np.ones((seq, LANES - NOPE_DIM - ROPE_DIM), f32)
    ctab = np.concatenate([one, cos, cos, tail], axis=1)
    stab = np.concatenate([0 * one, -sin, sin, 0 * tail], axis=1)
    return jnp.asarray(ctab), jnp.asarray(stab)


def _rope_swap(t):
    lane = lax.broadcasted_iota(jnp.int32, t.shape, 1)
    half = ROPE_DIM // 2
    return jnp.where(lane < NOPE_DIM + half, pltpu.roll(t, LANES - half, 1), pltpu.roll(t, half, 1))


def _rope(t, ctab, stab):
    return t * ctab + _rope_swap(t) * stab


def _rope_inv(t, ctab, stab):
    return t * ctab - _rope_swap(t) * stab


def _rms(x, g):
    r = lax.rsqrt(jnp.mean(x * x, axis=-1, keepdims=True) + RMS_EPS)
    xh = x * r
    return xh, r, xh * g


def _mla_prep(h, g_cq, g_ckv, wq, wk, wv, ctab, stab, tm=512):
    seq = h.shape[0]
    tm = min(tm, seq)

    def body(h_ref, gq_ref, gk_ref, wq_ref, wk_ref, wv_ref, c_ref, s_ref, q_out, k_out, v_out):
        hb = h_ref[...]
        ctab_, stab_ = c_ref[...], s_ref[...]
        _, _, cqn = _rms(hb[:, :Q_RANK], gq_ref[...])
        _, _, ckn = _rms(hb[:, Q_RANK:Q_RANK + KV_RANK], gk_ref[...])
        cqn = cqn.astype(MXU_DTYPE)
        ckn = ckn.astype(MXU_DTYPE)
        krr = _rope(hb[:, Q_RANK + KV_RANK:], ctab_, stab_)
        ones_lane = (lax.broadcasted_iota(jnp.int32, (1, LANES), 1) == ONES_LANE).astype(F32)
        for hd in range(HEADS):
            q = _dot(cqn, wq_ref[hd], 1, 0)
            q_out[hd] = _rope(q, ctab_, stab_).astype(q_out.dtype)
            k_out[hd] = (_dot(ckn, wk_ref[hd], 1, 0) + krr).astype(k_out.dtype)
            v_out[hd] = (_dot(ckn, wv_ref[hd], 1, 0) + ones_lane).astype(v_out.dtype)

    full = lambda *shape: pl.BlockSpec(shape, lambda i: (0,) * len(shape))
    slab = pl.BlockSpec((HEADS, tm, LANES), lambda i: (0, i, 0))
    shp = jax.ShapeDtypeStruct((HEADS, seq, LANES), MXU_DTYPE)
    return pl.pallas_call(
        body, name="mla_prep", grid=(seq // tm,),
        in_specs=[pl.BlockSpec((tm, DH_PART), lambda i: (i, 0)), full(1, Q_RANK), full(1, KV_RANK),
                  full(HEADS, Q_RANK, LANES), full(HEADS, KV_RANK, LANES), full(HEADS, KV_RANK, LANES),
                  pl.BlockSpec((tm, LANES), lambda i: (i, 0)), pl.BlockSpec((tm, LANES), lambda i: (i, 0))],
        out_specs=[slab, slab, slab], out_shape=[shp, shp, shp],
        compiler_params=_params("parallel"),
    )(h, g_cq, g_ckv, wq, wk, wv, ctab, stab)


def _mla_prep_bwd(h, g_cq, g_ckv, wq, wk, wv, ctab, stab, dq, dk, dv, tm=512, after=()):
    seq = h.shape[0]
    tm = min(tm, seq)
    n_after = len(after)

    def body(h_ref, gq_ref, gk_ref, wq_ref, wk_ref, wv_ref, c_ref, s_ref, dq_ref, dk_ref, dv_ref, *rest):
        dh_ref, dwq_ref, dwk_ref, dwv_ref, dgq_ref, dgk_ref = rest[n_after:]

        @pl.when(pl.program_id(0) == 0)
        def _():
            for r in (dwq_ref, dwk_ref, dwv_ref, dgq_ref, dgk_ref):
                r[...] = jnp.zeros_like(r)

        hb = h_ref[...]
        ctab_, stab_ = c_ref[...], s_ref[...]
        gq, gk = gq_ref[...], gk_ref[...]
        xq, rq, cqn = _rms(hb[:, :Q_RANK], gq)
        xk, rk, ckn = _rms(hb[:, Q_RANK:Q_RANK + KV_RANK], gk)
        cqn = cqn.astype(MXU_DTYPE)
        ckn = ckn.astype(MXU_DTYPE)
        d_cqn = jnp.zeros((tm, Q_RANK), F32)
        d_ckn = jnp.zeros((tm, KV_RANK), F32)
        d_krr = jnp.zeros((tm, LANES), F32)
        for hd in range(HEADS):
            dqh = _rope_inv(dq_ref[hd], ctab_, stab_).astype(MXU_DTYPE)
            d_cqn += _dot(dqh, wq_ref[hd], 1, 1)
            dwq_ref[hd] += _dot(cqn, dqh, 0, 0)
            dkh = dk_ref[hd]
            d_krr += dkh
            dkh = dkh.astype(MXU_DTYPE)
            d_ckn += _dot(dkh, wk_ref[hd], 1, 1)
            dwk_ref[hd] += _dot(ckn, dkh, 0, 0)
            dvh = dv_ref[hd].astype(MXU_DTYPE)
            d_ckn += _dot(dvh, wv_ref[hd], 1, 1)
            dwv_ref[hd] += _dot(ckn, dvh, 0, 0)
        lane = lax.broadcasted_iota(jnp.int32, (tm, LANES), 1)
        rot = (lane >= NOPE_DIM) & (lane < NOPE_DIM + ROPE_DIM)
        d_kr = jnp.where(rot, _rope_inv(jnp.where(rot, d_krr, 0.0), ctab_, stab_), 0.0)

        def rms_bwd(dy, xh, r, g, dg_ref):
            dg_ref[...] += jnp.sum(dy * xh, axis=0, keepdims=True)
            dxh = dy * g
            return r * (dxh - xh * jnp.mean(dxh * xh, axis=-1, keepdims=True))

        d_cq = rms_bwd(d_cqn, xq, rq, gq, dgq_ref)
        d_ck = rms_bwd(d_ckn, xk, rk, gk, dgk_ref)
        dh_ref[...] = jnp.concatenate([d_cq, d_ck, d_kr], axis=1).astype(dh_ref.dtype)

    full = lambda *shape: pl.BlockSpec(shape, lambda i: (0,) * len(shape))
    slab = pl.BlockSpec((HEADS, tm, LANES), lambda i: (0, i, 0))
    return pl.pallas_call(
        body, name="mla_prep_bwd", grid=(seq // tm,),
        in_specs=[pl.BlockSpec((tm, DH_PART), lambda i: (i, 0)), full(1, Q_RANK), full(1, KV_RANK),
                  full(HEADS, Q_RANK, LANES), full(HEADS, KV_RANK, LANES), full(HEADS, KV_RANK, LANES),
                  pl.BlockSpec((tm, LANES), lambda i: (i, 0)), pl.BlockSpec((tm, LANES), lambda i: (i, 0)),
                  slab, slab, slab] + [_ANY_SPEC] * n_after,
        out_specs=[pl.BlockSpec((tm, DH_PART), lambda i: (i, 0)), full(HEADS, Q_RANK, LANES), full(HEADS, KV_RANK, LANES),
                   full(HEADS, KV_RANK, LANES), full(1, Q_RANK), full(1, KV_RANK)],
        out_shape=[jax.ShapeDtypeStruct((seq, DH_PART), MXU_DTYPE), jax.ShapeDtypeStruct((HEADS, Q_RANK, LANES), F32),
                   jax.ShapeDtypeStruct((HEADS, KV_RANK, LANES), F32), jax.ShapeDtypeStruct((HEADS, KV_RANK, LANES), F32),
                   jax.ShapeDtypeStruct((1, Q_RANK), F32), jax.ShapeDtypeStruct((1, KV_RANK), F32)],
        compiler_params=_params("arbitrary"),
    )(h, g_cq, g_ckv, wq, wk, wv, ctab, stab, dq, dk, dv, *after)


def _mla_attn_fwd(q, k, v, t=1024):
    _, seq, _ = q.shape
    t = min(t, seq)

    def body(q_ref, k_ref, v_ref, o_ref, ob_ref, lse_ref, m_ref, acc_ref, s_ref):
        i = pl.program_id(1)
        qb = q_ref[...]
        m_ref[...] = jnp.full_like(m_ref, NEG_BIG)
        acc_ref[...] = jnp.zeros_like(acc_ref)

        def scores(j):
            return _dot(qb, k_ref[pl.ds(pl.multiple_of(j * t, t), t), :], 1, 1) * MLA_SCALE_LOG2

        def softmax_pv(j, s, rows=slice(None), mask=None):
            vb = v_ref[pl.ds(pl.multiple_of(j * t, t), s.shape[1]), :]
            if mask is not None:
                s = jnp.where(mask, s, NEG_BIG)
            m_old = m_ref[rows, :]
            m_new = jnp.maximum(m_old, jnp.max(s, axis=1, keepdims=True))
            p = jnp.exp2(s - m_new)
            a = jnp.exp2(m_old - m_new)
            acc_ref[rows, :] = a * acc_ref[rows, :] + _dot(p.astype(MXU_DTYPE), vb, 1, 0)
            m_ref[rows, :] = m_new

        def softmax_pv_diagonal(j):
            th = t // MLA_FWD_SPLITS_DIAGONAL
            for hf in range(MLA_FWD_SPLITS_DIAGONAL):
                nk = (hf + 1) * th
                row = lax.broadcasted_iota(jnp.int32, (th, nk), 0) + hf * th
                rows = slice(hf * th, (hf + 1) * th)
                softmax_pv(j, s_ref[rows, 0:nk], rows, row >= lax.broadcasted_iota(jnp.int32, (th, nk), 1))

        s_ref[...] = scores(0)

        def loop_body(j, c):
            s_next = scores(j + 1)
            softmax_pv(j, s_ref[...])
            s_ref[...] = s_next
            return c

        lax.fori_loop(0, i, loop_body, 0)
        softmax_pv_diagonal(i)
        acc = acc_ref[...]
        l = acc[:, ONES_LANE:ONES_LANE + 1]
        o = jnp.where(lax.broadcasted_iota(jnp.int32, acc.shape, 1) < HEAD_DIM, acc * (1.0 / l), 0.0)
        o_ref[...] = o
        ob_ref[...] = o.astype(ob_ref.dtype)
        lse_ref[...] = jnp.broadcast_to(m_ref[...] + jnp.log2(l), lse_ref.shape)

    blk = pl.BlockSpec((None, t, LANES), lambda h, i: (h, i, 0))
    whole = pl.BlockSpec((None, seq, LANES), lambda h, i: (h, 0, 0))
    shp = jax.ShapeDtypeStruct((HEADS, seq, LANES), F32)
    return pl.pallas_call(
        body, name="mla_attn_fwd", grid=(HEADS, seq // t),
        in_specs=[blk, whole, whole], out_specs=[blk, blk, blk],
        out_shape=[shp, jax.ShapeDtypeStruct((HEADS, seq, LANES), MXU_DTYPE), shp],
        scratch_shapes=[pltpu.VMEM((t, 1), F32), pltpu.VMEM((t, LANES), F32), pltpu.VMEM((t, t), F32)],
        compiler_params=_params("parallel", "arbitrary"),
    )(q, k, v)


def _mla_attn_bwd(q, k, v, o, lse, do, t=1024):
    _, seq, _ = q.shape
    t = min(t, seq)
    nb = seq // t

    def body(q_ref, k_ref, v_ref, o_ref, lse_ref, do_ref, dq_ref, dk_ref, dv_ref, dl_ref, dka_ref, dva_ref):
        dq_ref[...] = jnp.zeros_like(dq_ref)

        def delta_body(i, c):
            rows = pl.ds(pl.multiple_of(i * t, t), t)
            dl_ref[rows, :] = jnp.sum(do_ref[rows, :] * o_ref[rows, :], axis=1, keepdims=True)
            return c

        lax.fori_loop(0, nb, delta_body, 0)

        def kblock(j, c):
            krows = pl.ds(pl.multiple_of(j * t, t), t)
            kb = k_ref[krows, :]
            vb = v_ref[krows, :]
            dka_ref[...] = jnp.zeros_like(dka_ref)
            dva_ref[...] = jnp.zeros_like(dva_ref)

            def qstep(i, masked):
                ns = MLA_BWD_SPLITS_DIAGONAL if masked else MLA_BWD_SPLITS
                th = t // ns
                rows = [pl.ds(pl.multiple_of(i * t + hf * th, th), th) for hf in range(ns)]
                qs = [q_ref[r, :] for r in rows]
                dos = [do_ref[r, :].astype(MXU_DTYPE) for r in rows]
                nkeys = [(hf + 1) * th if masked else t for hf in range(ns)]
                ss = [_dot(qs[hf], kb[:nkeys[hf]], 1, 1) * MLA_SCALE_LOG2 for hf in range(ns)]
                dps = [_dot(dos[hf], vb[:nkeys[hf]], 1, 1) for hf in range(ns)]
                for hf in range(ns):
                    s, nk = ss[hf], nkeys[hf]
                    if masked:
                        row = lax.broadcasted_iota(jnp.int32, (th, nk), 0) + hf * th
                        s = jnp.where(row >= lax.broadcasted_iota(jnp.int32, (th, nk), 1), s, NEG_BIG)
                    p = jnp.exp2(s - lse_ref[rows[hf], 0:1])
                    dva_ref[0:nk, :] += _dot(p.astype(MXU_DTYPE), dos[hf], 0, 0)
                    ds = (p * (dps[hf] - dl_ref[rows[hf], :]) * MLA_SCALE).astype(MXU_DTYPE)
                    dka_ref[0:nk, :] += _dot(ds, qs[hf], 0, 0)
                    dq_ref[rows[hf], :] += _dot(ds, kb[:nk], 1, 0)

            qstep(j, True)

            def qloop(i, c2):
                qstep(i, False)
                return c2

            lax.fori_loop(j + 1, nb, qloop, 0)
            dk_ref[krows, :] = dka_ref[...]
            dv_ref[krows, :] = dva_ref[...]
            return c

        lax.fori_loop(0, nb, kblock, 0)

    whole = pl.BlockSpec((None, seq, LANES), lambda h: (h, 0, 0))
    shp = jax.ShapeDtypeStruct((HEADS, seq, LANES), F32)
    return pl.pallas_call(
        body, name="mla_attn_bwd", grid=(HEADS,),
        in_specs=[whole] * 6, out_specs=[whole] * 3, out_shape=[shp] * 3,
        scratch_shapes=[pltpu.VMEM((seq, 1), F32), pltpu.VMEM((t, LANES), F32), pltpu.VMEM((t, LANES), F32)],
        compiler_params=_params("parallel"),
    )(q, k, v, o, lse, do)


DIL_CHUNK = DIL_BLOCK * max(d for _, d in DIL_PAIRS)
DIL_PAIR_LANES = 2 * HEAD_DIM
assert DIL_PAIR_LANES == LANES
DIL_UNROLL_FWD = 16
DIL_UNROLL_BWD = 8


def _dil_bias_tables(hp, dil):
    b = DIL_BLOCK
    iq = lax.broadcasted_iota(jnp.int32, (b, 2 * b), 0)
    ik = lax.broadcasted_iota(jnp.int32, (b, 2 * b), 1)
    off = iq + b - ik
    band = (off >= 0) & (off <= b)
    dist = (off * dil).astype(F32)
    every, first = [], []
    for hh in range(2):
        slope = jnp.where(hp == 0, ALIBI_SLOPES[hh], jnp.where(hp == 1, ALIBI_SLOPES[2 + hh],
                          jnp.where(hp == 2, ALIBI_SLOPES[4 + hh], ALIBI_SLOPES[6 + hh]))).astype(F32)
        bias = -slope * dist
        every.append(jnp.where(band, bias, NEG_BIG))
        first.append(jnp.where(band & (ik >= b), bias, NEG_BIG))
    return jnp.concatenate(every, axis=0), jnp.concatenate(first, axis=0)


def _dil_rows(start, dil):
    return pl.ds(start, DIL_BLOCK) if dil == 1 else pl.ds(start, DIL_BLOCK, stride=dil)


def _dil_block_pos(blk, c, dil):
    sc, r = blk // dil, blk % dil
    q0 = sc * (DIL_BLOCK * dil) + r
    kcur0 = c * DIL_CHUNK + q0
    first = kcur0 < DIL_BLOCK * dil
    kprev0 = jnp.where(first, kcur0, kcur0 - DIL_BLOCK * dil)
    return q0, kcur0, kprev0, first


def _pair_cols(hh):
    return slice(HEAD_DIM * hh, HEAD_DIM * (hh + 1))


def _first_head_lanes(shape):
    return lax.broadcasted_iota(jnp.int32, shape, 1) < HEAD_DIM


def _stack_pair(t):
    first = _first_head_lanes(t.shape)
    return jnp.concatenate([jnp.where(first, t, 0.0), jnp.where(first, 0.0, t)], axis=0).astype(MXU_DTYPE)


def _unstack_pair(t):
    rows = t.shape[0] // 2
    return jnp.where(_first_head_lanes((rows, t.shape[1])), t[:rows], t[rows:])


def _pair_column(t):
    return jnp.concatenate([t[:, 0:1], t[:, HEAD_DIM:HEAD_DIM + 1]], axis=0)


def _dil_fwd(h):
    seq = h.shape[0]
    assert seq % DIL_CHUNK == 0
    nblk = DIL_CHUNK // DIL_BLOCK
    rc = 256

    def body(q_ref, k_ref, v_ref, o_ref, ob_ref, lse_ref, *scr):
        o_scr, l_scr = scr[:3], scr[3:]
        hp, c = pl.program_id(0), pl.program_id(1)
        for bi, (_, dil) in enumerate(DIL_PAIRS):
            tables = _dil_bias_tables(hp, dil)

            def block(blk, carry, bi=bi, dil=dil, tables=tables):
                q0, kcur0, kprev0, first = _dil_block_pos(blk, c, dil)
                q2 = _stack_pair(q_ref[_dil_rows(q0, dil), :] * DIL_SCALE)
                kcat = jnp.concatenate([k_ref[_dil_rows(kprev0, dil), :], k_ref[_dil_rows(kcur0, dil), :]], axis=0).astype(MXU_DTYPE)
                vcat = jnp.concatenate([v_ref[_dil_rows(kprev0, dil), :], v_ref[_dil_rows(kcur0, dil), :]], axis=0).astype(MXU_DTYPE)
                s = _dot(q2, kcat, 1, 1) + jnp.where(first, tables[1], tables[0])
                mx = jnp.max(s, axis=1, keepdims=True)
                p = jnp.exp(s - mx)
                l = jnp.sum(p, axis=1, keepdims=True)
                o_scr[bi][_dil_rows(q0, dil), :] = _unstack_pair(_dot(p.astype(MXU_DTYPE), vcat, 1, 0) * (1.0 / l))
                l_scr[bi][_dil_rows(q0, dil), :] = _unstack_pair(jnp.broadcast_to(mx + jnp.log(l), (2 * DIL_BLOCK, LANES)))
                return carry

            lax.fori_loop(0, nblk, block, 0, unroll=DIL_UNROLL_FWD)

        def combine(i, carry):
            rows = pl.ds(pl.multiple_of(i * rc, rc), rc)
            ls = [l_scr[bi][rows, :] for bi in range(3)]
            mx = jnp.maximum(jnp.maximum(ls[0], ls[1]), ls[2])
            es = [jnp.exp(l - mx) for l in ls]
            den = es[0] + es[1] + es[2]
            o = (es[0] * o_scr[0][rows, :] + es[1] * o_scr[1][rows, :] + es[2] * o_scr[2][rows, :]) / den
            o_ref[rows, :] = o
            ob_ref[rows, :] = o.astype(ob_ref.dtype)
            lse_ref[rows, :] = mx + jnp.log(den)
            return carry

        lax.fori_loop(0, DIL_CHUNK // rc, combine, 0)

    nq = DIL_WIDTH // LANES
    chunk = lambda off: pl.BlockSpec((DIL_CHUNK, LANES), lambda hp, c: (c, off + hp))
    whole = lambda off: pl.BlockSpec((seq, LANES), lambda hp, c: (0, off + hp))
    shp = jax.ShapeDtypeStruct((seq, DIL_WIDTH), F32)
    return pl.pallas_call(
        body, name="dil_fwd", grid=(nq, seq // DIL_CHUNK),
        in_specs=[chunk(nq), whole(2 * nq), whole(3 * nq)], out_specs=[chunk(0), chunk(0), chunk(0)],
        out_shape=[shp, jax.ShapeDtypeStruct((seq, DIL_WIDTH), MXU_DTYPE), shp],
        scratch_shapes=[pltpu.VMEM((DIL_CHUNK, LANES), F32)] * 6,
        compiler_params=_params("parallel", "arbitrary"),
    )(h, h, h)


def _dil_bwd(h, o, lse, do, after=()):
    seq = h.shape[0]
    nblk = DIL_CHUNK // DIL_BLOCK
    nchunk = seq // DIL_CHUNK
    rc = 256

    n_after = len(after)

    def body(q_ref, k_ref, v_ref, o_ref, lse_ref, do_ref, *rest):
        dq_out, dk_out, dv_out, dl_scr, dq_ref, dk_ref, dv_ref = rest[n_after:]
        hp, c = pl.program_id(0), pl.program_id(1)

        @pl.when(c == 0)
        def _():
            dk_ref[...] = jnp.zeros_like(dk_ref)
            dv_ref[...] = jnp.zeros_like(dv_ref)

        def delta(i, carry):
            rows = pl.ds(pl.multiple_of(i * rc, rc), rc)
            prod = do_ref[rows, :] * o_ref[rows, :]
            dl_scr[rows, :] = jnp.concatenate(
                [jnp.broadcast_to(jnp.sum(prod[:, _pair_cols(hh)], axis=1, keepdims=True), (rc, HEAD_DIM)) for hh in range(2)], axis=1)
            return carry

        lax.fori_loop(0, DIL_CHUNK // rc, delta, 0)

        for bi, (_, dil) in enumerate(DIL_PAIRS):
            tables = _dil_bias_tables(hp, dil)

            def block(blk, carry, bi=bi, dil=dil, tables=tables):
                q0, kcur0, kprev0, first = _dil_block_pos(blk, c, dil)
                qrows = _dil_rows(q0, dil)
                q2 = _stack_pair(q_ref[qrows, :] * DIL_SCALE)
                kcat = jnp.concatenate([k_ref[_dil_rows(kprev0, dil), :], k_ref[_dil_rows(kcur0, dil), :]], axis=0).astype(MXU_DTYPE)
                vcat = jnp.concatenate([v_ref[_dil_rows(kprev0, dil), :], v_ref[_dil_rows(kcur0, dil), :]], axis=0).astype(MXU_DTYPE)
                do2 = _stack_pair(do_ref[qrows, :])
                s = _dot(q2, kcat, 1, 1) + jnp.where(first, tables[1], tables[0])
                p = jnp.exp(s - _pair_column(lse_ref[qrows, :]))
                dp = _dot(do2, vcat, 1, 1)
                ds = (p * (dp - _pair_column(dl_scr[qrows, :]))).astype(MXU_DTYPE)
                dq_b = _unstack_pair(_dot(ds, kcat, 1, 0)) * DIL_SCALE
                dk_b = _dot(ds, q2, 0, 0)
                dv_b = _dot(p.astype(MXU_DTYPE), do2, 0, 0)
                if bi == 0:
                    dq_ref[qrows, :] = dq_b
                else:
                    dq_ref[qrows, :] += dq_b
                dk_ref[_dil_rows(kprev0, dil), :] += dk_b[:DIL_BLOCK]
                dv_ref[_dil_rows(kprev0, dil), :] += dv_b[:DIL_BLOCK]
                dk_ref[_dil_rows(kcur0, dil), :] += dk_b[DIL_BLOCK:]
                dv_ref[_dil_rows(kcur0, dil), :] += dv_b[DIL_BLOCK:]
                return carry

            lax.fori_loop(0, nblk, block, 0, unroll=DIL_UNROLL_BWD)

        dq_out[...] = dq_ref[...].astype(dq_out.dtype)

        @pl.when(c == nchunk - 1)
        def _():
            dk_out[...] = dk_ref[...].astype(dk_out.dtype)
            dv_out[...] = dv_ref[...].astype(dv_out.dtype)

    nq = DIL_WIDTH // LANES
    chunk = lambda off: pl.BlockSpec((DIL_CHUNK, LANES), lambda hp, c: (c, off + hp))
    whole = lambda off: pl.BlockSpec((seq, LANES), lambda hp, c: (0, off + hp))
    shp = jax.ShapeDtypeStruct((seq, DIL_WIDTH), MXU_DTYPE)
    return pl.pallas_call(
        body, name="dil_bwd", grid=(nq, nchunk),
        in_specs=[chunk(nq), whole(2 * nq), whole(3 * nq), chunk(0), chunk(0), chunk(0)] + [_ANY_SPEC] * n_after,
        out_specs=[chunk(0), whole(0), whole(0)], out_shape=[shp, shp, shp],
        scratch_shapes=[pltpu.VMEM((DIL_CHUNK, LANES), F32), pltpu.VMEM((DIL_CHUNK, LANES), F32),
                        pltpu.VMEM((seq, LANES), F32), pltpu.VMEM((seq, LANES), F32)],
        compiler_params=_params("parallel", "arbitrary"),
    )(h, h, h, o, lse, do, *after)


def _mm_dx0(parts, w_in_t, res, tm=1024, after=()):
    seq, d = res.shape
    tm = min(tm, seq)
    n_after = len(after)

    def body(a0, a1, a2, a3, b_ref, r_ref, *rest):
        o_ref = rest[n_after]
        acc = _dot(a0[...], b_ref[0:DH_PART, :], 1, 0)
        for c, a in enumerate((a1, a2, a3), start=1):
            acc += _dot(a[...], b_ref[DH_PART * c:DH_PART * (c + 1), :], 1, 0)
        o_ref[...] = acc + DN_ALPHA * r_ref[...]

    blk = pl.BlockSpec((tm, DH_PART), lambda i: (i, 0))
    row = pl.BlockSpec((tm, d), lambda i: (i, 0))
    return pl.pallas_call(
        body, name="mm_dx0", grid=(seq // tm,),
        in_specs=[blk] * 4 + [pl.BlockSpec((IN_PAD, d), lambda i: (0, 0), pipeline_mode=pl.Buffered(1)), row] + [_ANY_SPEC] * n_after,
        out_specs=row, out_shape=jax.ShapeDtypeStruct((seq, d), F32), compiler_params=_params("parallel"),
    )(*parts, w_in_t, res, *after)


def _mm_dw_in(parts, x0, tk=1024):
    seq, d = x0.shape
    tk = min(tk, seq)
    nk = seq // tk

    def body(a0, a1, a2, a3, b_ref, o_ref, acc_ref):
        kk = pl.program_id(0)

        @pl.when(kk == 0)
        def _():
            acc_ref[...] = jnp.zeros_like(acc_ref)

        b = b_ref[...].astype(MXU_DTYPE)
        for c, a in enumerate((a0, a1, a2, a3)):
            acc_ref[DH_PART * c:DH_PART * (c + 1), :] += _dot(a[...], b, 0, 0)

        @pl.when(kk == nk - 1)
        def _():
            o_ref[...] = acc_ref[...].astype(o_ref.dtype)

    blk = pl.BlockSpec((tk, DH_PART), lambda kk: (kk, 0))
    return pl.pallas_call(
        body, name="mm_dw_in", grid=(nk,), in_specs=[blk] * 4 + [pl.BlockSpec((tk, d), lambda kk: (kk, 0))],
        out_specs=pl.BlockSpec((IN_PAD, d), lambda kk: (0, 0)), out_shape=jax.ShapeDtypeStruct((IN_PAD, d), MXU_DTYPE),
        scratch_shapes=[pltpu.VMEM((IN_PAD, d), F32)], compiler_params=_params("arbitrary"),
    )(*parts, x0)


def _ln_stats(z):
    mu = jnp.mean(z, axis=-1, keepdims=True)
    zc = z - mu
    r = lax.rsqrt(jnp.mean(zc * zc, axis=-1, keepdims=True) + LN_EPS)
    return zc * r, r


def _ln_bwd_math(dy, xh, r, g):
    dxh = dy * g
    return r * (dxh - jnp.mean(dxh, axis=-1, keepdims=True) - xh * jnp.mean(dxh * xh, axis=-1, keepdims=True))


def _mix_ln1(o_mla, o_dil, w_o_mla, w_o_dil, x0, g, b, tm=512):
    seq, d = x0.shape
    tm = min(tm, seq)

    def body(om_ref, od_ref, wm_ref, wd_ref, x_ref, g_ref, b_ref, z_ref, y_ref, yb_ref):
        mix = _dot(od_ref[...], wd_ref[...], 1, 0)
        for hd in range(HEADS):
            mix += _dot(om_ref[hd], wm_ref[LANES * hd:LANES * (hd + 1), :], 1, 0)
        z = DN_ALPHA * x_ref[...] + mix
        xh, _ = _ln_stats(z)
        y = xh * g_ref[...] + b_ref[...]
        z_ref[...] = z
        y_ref[...] = y
        yb_ref[...] = y.astype(yb_ref.dtype)

    blk = pl.BlockSpec((tm, d), lambda i: (i, 0))
    vec = pl.BlockSpec((1, d), lambda i: (0, 0))
    shp = jax.ShapeDtypeStruct((seq, d), F32)
    return pl.pallas_call(
        body, name="mix_ln1", grid=(seq // tm,),
        in_specs=[pl.BlockSpec((HEADS, tm, LANES), lambda i: (0, i, 0)), pl.BlockSpec((tm, DIL_WIDTH), lambda i: (i, 0)),
                  pl.BlockSpec((HEADS * LANES, d), lambda i: (0, 0)), pl.BlockSpec((DIL_WIDTH, d), lambda i: (0, 0)), blk, vec, vec],
        out_specs=[blk, blk, blk], out_shape=[shp, shp, jax.ShapeDtypeStruct((seq, d), MXU_DTYPE)],
        compiler_params=_params("parallel"))(o_mla, o_dil, w_o_mla, w_o_dil, x0, g, b)


def _dx1_ln1_bwd(du, w_up_t, dz2, z, g, tm=256, after=()):
    seq, d = z.shape
    kdim = du.shape[1]
    tm = min(tm, seq)
    n_after = len(after)

    def body(du_ref, w_ref, r_ref, z_ref, g_ref, *rest):
        dz_ref, dzb_ref, dg_ref, db_ref = rest[n_after:]

        @pl.when(pl.program_id(0) == 0)
        def _():
            dg_ref[...] = jnp.zeros_like(dg_ref)
            db_ref[...] = jnp.zeros_like(db_ref)

        dyb = _dot(du_ref[...], w_ref[...], 1, 0) + DN_ALPHA * r_ref[...]
        xh, r = _ln_stats(z_ref[...])
        dg_ref[...] += jnp.sum(dyb * xh, axis=0, keepdims=True)
        db_ref[...] += jnp.sum(dyb, axis=0, keepdims=True)
        dz = _ln_bwd_math(dyb, xh, r, g_ref[...])
        dz_ref[...] = dz
        dzb_ref[...] = dz.astype(dzb_ref.dtype)

    blk = pl.BlockSpec((tm, d), lambda i: (i, 0))
    vec = pl.BlockSpec((1, d), lambda i: (0, 0))
    return pl.pallas_call(
        body, name="dx1_ln1_bwd", grid=(seq // tm,),
        in_specs=[pl.BlockSpec((tm, kdim), lambda i: (i, 0)),
                  pl.BlockSpec((kdim, d), lambda i: (0, 0), pipeline_mode=pl.Buffered(1)), blk, blk, vec] + [_ANY_SPEC] * n_after,
        out_specs=[blk, blk, vec, vec],
        out_shape=[jax.ShapeDtypeStruct((seq, d), F32), jax.ShapeDtypeStruct((seq, d), MXU_DTYPE),
                   jax.ShapeDtypeStruct((1, d), F32), jax.ShapeDtypeStruct((1, d), F32)],
        compiler_params=_params("arbitrary"))(du, w_up_t, dz2, z, g, *after)


def _down_ln2_loss_bwd(act, w_down, x1, target, g, b, tm=512):
    seq, d = x1.shape
    kdim = act.shape[1]
    tm = min(tm, seq)

    def body(a_ref, w_ref, x_ref, t_ref, g_ref, b_ref, dz_ref, dzb_ref, loss_ref, dg_ref, db_ref):
        @pl.when(pl.program_id(0) == 0)
        def _():
            loss_ref[...] = jnp.zeros_like(loss_ref)
            dg_ref[...] = jnp.zeros_like(dg_ref)
            db_ref[...] = jnp.zeros_like(db_ref)

        gv = g_ref[...]
        z = DN_ALPHA * x_ref[...] + _dot(a_ref[...], w_ref[...], 1, 0)
        xh, r = _ln_stats(z)
        err = (xh * gv + b_ref[...]) - t_ref[...]
        loss_ref[...] += 0.5 * jnp.sum(jnp.mean(err * err, axis=-1, keepdims=True), axis=0, keepdims=True)
        dy = err * (1.0 / d)
        dg_ref[...] += jnp.sum(dy * xh, axis=0, keepdims=True)
        db_ref[...] += jnp.sum(dy, axis=0, keepdims=True)
        dz = _ln_bwd_math(dy, xh, r, gv)
        dz_ref[...] = dz
        dzb_ref[...] = dz.astype(dzb_ref.dtype)

    blk = pl.BlockSpec((tm, d), lambda i: (i, 0))
    vec = pl.BlockSpec((1, d), lambda i: (0, 0))
    return pl.pallas_call(
        body, name="down_ln2_loss_bwd", grid=(seq // tm,),
        in_specs=[pl.BlockSpec((tm, kdim), lambda i: (i, 0)),
                  pl.BlockSpec((kdim, d), lambda i: (0, 0), pipeline_mode=pl.Buffered(1)), blk, blk, vec, vec],
        out_specs=[blk, blk, pl.BlockSpec((1, LANES), lambda i: (0, 0)), vec, vec],
        out_shape=[jax.ShapeDtypeStruct((seq, d), F32), jax.ShapeDtypeStruct((seq, d), MXU_DTYPE),
                   jax.ShapeDtypeStruct((1, LANES), F32),
                   jax.ShapeDtypeStruct((1, d), F32), jax.ShapeDtypeStruct((1, d), F32)],
        compiler_params=_params("arbitrary"))(act, w_down, x1, target, g, b)


HALO = 16


def _conv_rows(e, w_ref, b_ref):
    y = b_ref[...] + w_ref[0:1, :] * pltpu.roll(e, 2, 0)
    y = y + w_ref[1:2, :] * pltpu.roll(e, 1, 0)
    return y + w_ref[2:3, :] * e


_GELU_C = math.sqrt(2.0 / math.pi)
_GELU_A = 0.044715


def _gelu(x):
    return 0.5 * x * (1.0 + jnp.tanh(_GELU_C * (x + _GELU_A * (x * x * x))))


CONV_TN = 256


def _ffn_interleave(a, axis):
    shp = a.shape
    a = a.reshape(shp[:axis] + (2, D_FF // CONV_TN, CONV_TN) + shp[axis + 1:])
    return jnp.swapaxes(a, axis, axis + 1).reshape(shp)


def _ffn_deinterleave(a, axis):
    shp = a.shape
    a = a.reshape(shp[:axis] + (D_FF // CONV_TN, 2, CONV_TN) + shp[axis + 1:])
    return jnp.swapaxes(a, axis, axis + 1).reshape(shp)


def _conv_gate_fwd(u, conv_w, conv_b, tm=1024):
    seq = u.shape[0]
    tm = min(tm, seq)
    tn = CONV_TN

    def body(u_ref, up_ref, w_ref, b_ref, o_ref):
        first = pl.program_id(0) == 0
        e = jnp.concatenate([jnp.where(first, 0.0, up_ref[...]), u_ref[...]], axis=0)
        y = _conv_rows(e, w_ref, b_ref)[HALO:]
        o_ref[...] = (_gelu(y[:, tn:]) * y[:, :tn]).astype(o_ref.dtype)

    hb = tm // HALO
    return pl.pallas_call(
        body, name="conv_gate_fwd", grid=(seq // tm, D_FF // tn),
        in_specs=[pl.BlockSpec((tm, 2 * tn), lambda i, j: (i, j)),
                  pl.BlockSpec((HALO, 2 * tn), lambda i, j: (jnp.maximum(i * hb - 1, 0), j)),
                  pl.BlockSpec((3, 2 * tn), lambda i, j: (0, j)), pl.BlockSpec((1, 2 * tn), lambda i, j: (0, j))],
        out_specs=pl.BlockSpec((tm, tn), lambda i, j: (i, j)), out_shape=jax.ShapeDtypeStruct((seq, D_FF), MXU_DTYPE),
        compiler_params=_params("parallel", "parallel"),
    )(u, u, conv_w, conv_b)


def _conv_gate_bwd(u, d_act, conv_w, conv_b, tm=1024):
    seq = u.shape[0]
    tm = min(tm, seq)
    tn = CONV_TN
    ni = seq // tm
    rows_e = tm + 2 * HALO

    def body(u_ref, up_ref, un_ref, da_ref, dan_ref, w_ref, b_ref, du_ref, dw_ref, db_ref):
        i = pl.program_id(1)
        first, last = i == 0, i == ni - 1

        @pl.when(i == 0)
        def _():
            dw_ref[...] = jnp.zeros_like(dw_ref)
            db_ref[...] = jnp.zeros_like(db_ref)

        e = jnp.concatenate([jnp.where(first, 0.0, up_ref[...]), u_ref[...], jnp.where(last, 0.0, un_ref[...])], axis=0)
        y = _conv_rows(e, w_ref, b_ref)
        ya, yg = y[:, :tn], y[:, tn:]
        dact = jnp.concatenate([jnp.zeros((HALO, tn), F32), da_ref[...].astype(F32),
                                jnp.where(last, 0.0, dan_ref[...].astype(F32))], axis=0)
        th = jnp.tanh(_GELU_C * (yg + _GELU_A * (yg * yg * yg)))
        gelu = 0.5 * yg * (1.0 + th)
        gelu_grad = 0.5 * (1.0 + th) + 0.5 * yg * (1.0 - th * th) * (_GELU_C * (1.0 + 3.0 * _GELU_A * (yg * yg)))
        dy = jnp.concatenate([dact * gelu, dact * ya * gelu_grad], axis=1)
        du = w_ref[2:3, :] * dy + w_ref[1:2, :] * pltpu.roll(dy, rows_e - 1, 0) + w_ref[0:1, :] * pltpu.roll(dy, rows_e - 2, 0)
        du_ref[...] = du[HALO:HALO + tm].astype(du_ref.dtype)
        dyt = dy[HALO:HALO + tm]
        dw_ref[0:1, :] += jnp.sum(dyt * pltpu.roll(e, 2, 0)[HALO:HALO + tm], axis=0, keepdims=True)
        dw_ref[1:2, :] += jnp.sum(dyt * pltpu.roll(e, 1, 0)[HALO:HALO + tm], axis=0, keepdims=True)
        dw_ref[2:3, :] += jnp.sum(dyt * e[HALO:HALO + tm], axis=0, keepdims=True)
        db_ref[...] += jnp.sum(dyt, axis=0, keepdims=True)

    hb = tm // HALO
    nh = seq // HALO
    prev = lambda j, i: (jnp.maximum(i * hb - 1, 0), j)
    nxt = lambda j, i: (jnp.minimum((i + 1) * hb, nh - 1), j)
    return pl.pallas_call(
        body, name="conv_gate_bwd", grid=(D_FF // tn, ni),
        in_specs=[pl.BlockSpec((tm, 2 * tn), lambda j, i: (i, j)), pl.BlockSpec((HALO, 2 * tn), prev),
                  pl.BlockSpec((HALO, 2 * tn), nxt), pl.BlockSpec((tm, tn), lambda j, i: (i, j)), pl.BlockSpec((HALO, tn), nxt),
                  pl.BlockSpec((3, 2 * tn), lambda j, i: (0, j)), pl.BlockSpec((1, 2 * tn), lambda j, i: (0, j))],
        out_specs=[pl.BlockSpec((tm, 2 * tn), lambda j, i: (i, j)), pl.BlockSpec((3, 2 * tn), lambda j, i: (0, j)),
                   pl.BlockSpec((1, 2 * tn), lambda j, i: (0, j))],
        out_shape=[jax.ShapeDtypeStruct((seq, 2 * D_FF), MXU_DTYPE), jax.ShapeDtypeStruct((3, 2 * D_FF), F32),
                   jax.ShapeDtypeStruct((1, 2 * D_FF), F32)],
        compiler_params=_params("parallel", "arbitrary"),
    )(u, u, u, d_act, d_act, conv_w, conv_b)


def _pad_heads(w, width):
    w = jnp.transpose(w, (1, 0, 2))
    return jnp.pad(w, ((0, 0), (0, 0), (0, LANES - width))).astype(MXU_DTYPE)


def _heads_major(a):
    return jnp.transpose(a, (1, 0, 2)).reshape(-1, a.shape[2])


def _heads_minor(a, heads):
    return jnp.transpose(a.reshape(heads, -1, a.shape[1]), (1, 0, 2))


_LATENT = Q_RANK + KV_RANK
_ROPE_AT = _LATENT + NOPE_DIM
_ROPE_END = _ROPE_AT + ROPE_DIM


def _split_pad_rows(w_t):
    z = lambda n: jnp.zeros((n, w_t.shape[1]), w_t.dtype)
    return jnp.concatenate([w_t[:_LATENT], z(_ROPE_AT - _LATENT), w_t[_LATENT:_LATENT + ROPE_DIM], z(DH_PART - _ROPE_END),
                            w_t[_LATENT + ROPE_DIM:]], axis=0)


def _split_unpad_rows(w_p):
    return jnp.concatenate([w_p[:_LATENT], w_p[_ROPE_AT:_ROPE_END], w_p[DH_PART:]], axis=0)


def _pad_w_o(w_o):
    mla = jnp.pad(w_o[:MLA_WIDTH].reshape(HEADS, HEAD_DIM, D_MODEL), ((0, 0), (0, LANES - HEAD_DIM), (0, 0)))
    return mla.reshape(HEADS * LANES, D_MODEL).astype(MXU_DTYPE), w_o[MLA_WIDTH:].astype(MXU_DTYPE)


def _unpad_w_o(d_mla, d_dil):
    return jnp.concatenate([d_mla.reshape(HEADS, LANES, D_MODEL)[:, :HEAD_DIM].reshape(MLA_WIDTH, D_MODEL), d_dil], axis=0)


def _row(v):
    return v.reshape(1, -1).astype(F32)


def _layer_grads(x0, target, cw, first_after=(), late_weights=None, on_grads=None):
    seq = x0.shape[0]
    ctab, stab = _rope_tables(seq)
    gq, gk = cw["g_cq"], cw["g_ckv"]
    wq, wk, wv = cw["wq"], cw["wk"], cw["wv"]
    notify = (lambda stage, grads: ()) if on_grads is None else on_grads

    h = _mm(x0, cw["w_in_t"], name="mm_h", tb=True, tm=1024, tn=IN_PAD, tk=1024, after=first_after)
    qf, kf, vp = _mla_prep(h, gq, gk, wq, wk, wv, ctab, stab)
    o_mla, o_mla_b, lse_mla = _mla_attn_fwd(qf, kf, vp)
    o_dil, o_dil_b, lse_dil = _dil_fwd(h)
    fetch = (lambda stage, after: {}) if late_weights is None else late_weights
    cw = {**cw, **fetch("w_o", o_mla_b)}
    cb = cw["conv_b"]
    z1, x1, x1b = _mix_ln1(o_mla_b, o_dil_b, cw["w_o_mla"], cw["w_o_dil"], x0, cw["ln1_g"], cw["ln1_b"])
    cw = {**cw, **fetch("w_up", x1b)}
    u = _mm(x1b, cw["w_up_t"], name="mm_up", tb=True, tm=512, tn=2 * D_FF, tk=1024)
    act = _conv_gate_fwd(u, cw["conv_w"], cb)
    cw = {**cw, **fetch("w_down", act)}
    dz2, dz2b, loss, d_ln2_g, d_ln2_b = _down_ln2_loss_bwd(act, cw["w_down"], x1, target, cw["ln2_g"], cw["ln2_b"])

    d_act = _mm(dz2b, cw["w_down"], name="mm_d_act", tb=True, out_dtype=MXU_DTYPE, tm=1024, tn=D_FF, tk=1024)
    d_w_down = _mm(act, dz2b, name="mm_dw_down", ta=True, out_dtype=MXU_DTYPE, tm=1408, tn=1024, tk=1024)
    du, d_conv_w, d_conv_b = _conv_gate_bwd(u, d_act, cw["conv_w"], cb)
    d_w_up_t = _mm(du, x1b, name="mm_dw_up", ta=True, out_dtype=MXU_DTYPE, tm=1408, tn=1024, tk=2048)
    grads = dict(w_up_t=d_w_up_t, w_down=d_w_down, conv_w=d_conv_w, conv_b=d_conv_b, ln2_g=d_ln2_g, ln2_b=d_ln2_b)
    dz1, dz1b, d_ln1_g, d_ln1_b = _dx1_ln1_bwd(du, cw["w_up_t"], dz2, z1, cw["ln1_g"], after=notify("ffn", grads))
    do_mla, do_dil, d_w_o_mla, d_w_o_dil = _w_o_bwd(dz1b, o_mla_b, o_dil_b, cw["w_o_mla"], cw["w_o_dil"])
    grads.update(w_o_mla=d_w_o_mla, w_o_dil=d_w_o_dil, ln1_g=d_ln1_g, ln1_b=d_ln1_b)
    dqf, dkf, dvf = _mla_attn_bwd(qf, kf, vp, o_mla, lse_mla, do_mla)
    dh_mla, d_wq, d_wk, d_wv, d_gq, d_gk = _mla_prep_bwd(h, gq, gk, wq, wk, wv, ctab, stab, dqf, dkf, dvf,
                                                          after=notify("w_o", grads))
    grads.update(wq=d_wq, wk=d_wk, wv=d_wv, g_cq=d_gq, g_ckv=d_gk, loss=loss)
    dq_dil, dk_dil, dv_dil = _dil_bwd(h, o_dil, lse_dil, do_dil, after=notify("mla", grads))
    dh = (dh_mla, dq_dil, dk_dil, dv_dil)
    grads.update(w_in_t=_mm_dw_in(dh, x0))
    grad_x = _mm_dx0(dh, cw["w_in_t"], dz1, after=notify("w_in", grads))
    return loss, grad_x, grads


def _all_gather(blocks, name):
    na = len(blocks)

    def body(*refs):
        ins, outs = refs[:na], refs[na:2 * na]
        send_sems, recv_sems, local_sems = refs[2 * na:]
        x, y, c = lax.axis_index("x"), lax.axis_index("y"), lax.axis_index("c")
        me, sibling = (x, y, c), (x, y, 1 - c)
        chips = [(1 - x, y), (x, 1 - y), (1 - x, 1 - y)]

        def slot(out, pos):
            return out.at[4 * pos[0] + 2 * pos[1] + pos[2]]

        def copy(a, k, block, to, src=None):
            return pltpu.make_async_remote_copy(
                src_ref=slot(outs[a], block) if src is None else src, dst_ref=slot(outs[a], block),
                send_sem=send_sems.at[7 * a + k], recv_sem=recv_sems.at[7 * a + k],
                device_id=to, device_id_type=pl.DeviceIdType.MESH)

        mine = [pltpu.make_async_copy(ins[a], slot(outs[a], me), local_sems.at[a]) for a in range(na)]
        for cp in mine:
            cp.start()
        first = []
        for a in range(na):
            first.append(copy(a, 0, me, sibling, src=ins[a]))
            first += [copy(a, 1 + j, me, (*chip, c), src=ins[a]) for j, chip in enumerate(chips)]
        for cp in first:
            cp.start()
        passed = []
        for j, chip in enumerate(chips):
            for a in range(na):
                copy(a, 1 + j, (*chip, c), me).wait_recv()
                cp = copy(a, 4 + j, (*chip, c), sibling)
                cp.start()
                passed.append(cp)
        for a in range(na):
            copy(a, 0, sibling, me).wait_recv()
            for j, chip in enumerate(chips):
                copy(a, 4 + j, (*chip, 1 - c), me).wait_recv()
        for cp in first + passed:
            cp.wait_send()
        for cp in mine:
            cp.wait()

    any_spec = pl.BlockSpec(memory_space=pl.ANY)
    return pl.pallas_call(
        body, name=name, in_specs=[any_spec] * na, out_specs=[any_spec] * na,
        out_shape=[jax.ShapeDtypeStruct((N_DEV,) + b.shape, b.dtype) for b in blocks],
        scratch_shapes=[pltpu.SemaphoreType.DMA((7 * na,)), pltpu.SemaphoreType.DMA((7 * na,)), pltpu.SemaphoreType.DMA((na,))],
    )(*blocks)


_HBM_SPEC = pl.BlockSpec(memory_space=pltpu.HBM)
_SEM_SPEC = pl.BlockSpec(memory_space=pltpu.SEMAPHORE)
_DATAFLOW = pltpu.CompilerParams(has_side_effects=pltpu.SideEffectType.DATAFLOW_SIDE_EFFECTING)


def _split_copies(which, ins, lands, send_sems, recv_sems, gather):
    x, y, c = lax.axis_index("x"), lax.axis_index("y"), lax.axis_index("c")
    me = 4 * x + 2 * y + c
    copies = []
    for a, src, land in zip(which, ins, lands):
        for d in range(1, N_DEV):
            px, py, pc = x ^ (d >> 2), y ^ ((d >> 1) & 1), c ^ (d & 1)
            copies.append(pltpu.make_async_remote_copy(
                src_ref=src if gather[a] else src.at[4 * px + 2 * py + pc], dst_ref=land.at[me],
                send_sem=send_sems.at[7 * a + d - 1], recv_sem=recv_sems.at[7 * a + d - 1],
                device_id=(px, py, pc), device_id_type=pl.DeviceIdType.MESH))
    return copies


def _send_start(srcs, gather, name):
    na = len(srcs)
    assert len(gather) == na
    land_types = [pltpu.HBM(((N_DEV,) + s.shape) if g else s.shape, s.dtype) for s, g in zip(srcs, gather)]

    def body(*refs):
        ins, lands = refs[:na], refs[na:2 * na]
        send_sems, recv_sems, token = refs[2 * na], refs[2 * na + 1], refs[-1]
        for cp in _split_copies(range(na), ins, lands, send_sems, recv_sems, gather):
            cp.start()
        token[...] = jnp.zeros_like(token)

    hbm = lambda a: pltpu.with_memory_space_constraint(a, pltpu.HBM)
    outs = pl.pallas_call(
        body, name=name,
        out_shape=(pltpu.SemaphoreType.DMA((7 * na,)), pltpu.SemaphoreType.DMA((7 * na,)),
                   *[pltpu.HBM(s.shape, s.dtype) for s in srcs], *land_types, jax.ShapeDtypeStruct((8, LANES), F32)),
        in_specs=[_HBM_SPEC] * (2 * na),
        out_specs=(_SEM_SPEC, _SEM_SPEC, *[_HBM_SPEC] * (2 * na), pl.BlockSpec(memory_space=pltpu.VMEM)),
        input_output_aliases={i: 2 + i for i in range(2 * na)}, compiler_params=_DATAFLOW,
    )(*[hbm(s) for s in srcs], *[hbm(lax.empty(t.shape, t.dtype)) for t in land_types])
    return dict(send=outs[0], recv=outs[1], srcs=list(outs[2:2 + na]), lands=list(outs[2 + na:2 + 2 * na]), token=outs[-1],
                gather=gather)


def _send_wait(handle, after, name, only=None):
    which = list(range(len(handle["srcs"]))) if only is None else list(only)
    na = len(which)
    gather = handle["gather"]
    after = list(after)

    def body(*refs):
        ins, lands = refs[:na], refs[na:2 * na]
        send_sems, recv_sems = refs[2 * na], refs[2 * na + 1]
        for cp in _split_copies(which, ins, lands, send_sems, recv_sems, gather):
            cp.wait_send()
            cp.wait_recv()

    both = [handle["srcs"][a] for a in which] + [handle["lands"][a] for a in which]
    outs = pl.pallas_call(
        body, name=name, out_shape=[pltpu.HBM(a.shape, a.dtype) for a in both],
        in_specs=[_HBM_SPEC] * (2 * na) + [_SEM_SPEC, _SEM_SPEC] + [_ANY_SPEC] * len(after),
        out_specs=[_HBM_SPEC] * (2 * na), input_output_aliases={i: i for i in range(2 * na)}, compiler_params=_DATAFLOW,
    )(*both, handle["send"], handle["recv"], *after)
    return list(outs[:na]), list(outs[na:])


def _sum_slots(p_ref):
    g = p_ref[0].astype(F32)
    for s in range(1, p_ref.shape[0]):
        g = g + p_ref[s].astype(F32)
    return g


def _adamw_refs(g, w_ref, m_ref, v_ref, g_out, d_out, m_out, v_out):
    c1 = 1.0 - ADAM_B1 ** ADAM_STEP
    c2 = 1.0 - ADAM_B2 ** ADAM_STEP
    m_new = ADAM_B1 * m_ref[...] + (1.0 - ADAM_B1) * g
    v_new = ADAM_B2 * v_ref[...] + (1.0 - ADAM_B2) * (g * g)
    g_out[...] = g
    m_out[...] = m_new
    v_out[...] = v_new
    d_out[...] = -ADAM_LR * ((m_new / c1) / (jnp.sqrt(v_new / c2) + ADAM_EPS) + ADAM_WD * w_ref[...])


def _adamw(parts, w, m, v, name):
    npart, r, n = parts.shape
    tr = r if r <= 256 else max(t for t in range(16, 257, 16) if r % t == 0)

    def body(p_ref, w_ref, m_ref, v_ref, g_out, d_out, m_out, v_out):
        _adamw_refs(_sum_slots(p_ref), w_ref, m_ref, v_ref, g_out, d_out, m_out, v_out)

    blk = pl.BlockSpec((tr, n), lambda i: (i, 0))
    shp = jax.ShapeDtypeStruct((r, n), F32)
    return pl.pallas_call(
        body, name=name, grid=(r // tr,), in_specs=[pl.BlockSpec((npart, tr, n), lambda i: (0, i, 0)), blk, blk, blk],
        out_specs=[blk] * 4, out_shape=[shp] * 4, compiler_params=_params("parallel"),
    )(parts, w, m, v)


def _adamw_small(parts, ws, ms, vs, loss_parts, name):
    n = len(parts)

    def body(*refs):
        ins, outs = refs[:4 * n + 1], refs[4 * n + 1:]
        for i in range(n):
            _adamw_refs(_sum_slots(ins[i]), ins[n + i], ins[2 * n + i], ins[3 * n + i], *outs[4 * i:4 * i + 4])
        outs[4 * n][...] = _sum_slots(ins[4 * n])

    out_shape = [jax.ShapeDtypeStruct(w.shape, F32) for w in ws for _ in range(4)]
    res = pl.pallas_call(body, name=name, out_shape=out_shape + [jax.ShapeDtypeStruct((1, LANES), F32)],
                         compiler_params=_params())(*parts, *ws, *ms, *vs, loss_parts)
    return [res[4 * i:4 * i + 4] for i in range(n)], res[4 * n]


REPLICATED = ("g_cq", "g_ckv", "w_uk", "w_uv", "ln1_g", "ln1_b", "conv_b", "ln2_g", "ln2_b")
ALL_WEIGHTS = ("w_in", "g_cq", "g_ckv", "w_uq", "w_uk", "w_uv", "w_o", "ln1_g", "ln1_b", "w_up", "conv_w", "conv_b",
               "w_down", "ln2_g", "ln2_b")


def kernel(x, w_in, g_cq, g_ckv, w_uq, w_uk, w_uv, w_o, ln1_g, ln1_b, w_up, conv_w, conv_b, w_down, ln2_g, ln2_b, loss_target, m_w_in, m_g_cq, m_g_ckv, m_w_uq, m_w_uk, m_w_uv, m_w_o, m_ln1_g, m_ln1_b, m_w_up, m_conv_w, m_conv_b, m_w_down, m_ln2_g, m_ln2_b, v_w_in, v_g_cq, v_g_ckv, v_w_uq, v_w_uk, v_w_uv, v_w_o, v_ln1_g, v_ln1_b, v_w_up, v_conv_w, v_conv_b, v_w_down, v_ln2_g, v_ln2_b):
    w = dict(w_in=w_in, g_cq=g_cq, g_ckv=g_ckv, w_uq=w_uq, w_uk=w_uk, w_uv=w_uv, w_o=w_o, ln1_g=ln1_g, ln1_b=ln1_b,
             w_up=w_up, conv_w=conv_w, conv_b=conv_b, w_down=w_down, ln2_g=ln2_g, ln2_b=ln2_b)
    m = dict(w_in=m_w_in, g_cq=m_g_cq, g_ckv=m_g_ckv, w_uq=m_w_uq, w_uk=m_w_uk, w_uv=m_w_uv, w_o=m_w_o, ln1_g=m_ln1_g,
             ln1_b=m_ln1_b, w_up=m_w_up, conv_w=m_conv_w, conv_b=m_conv_b, w_down=m_w_down, ln2_g=m_ln2_g, ln2_b=m_ln2_b)
    v = dict(w_in=v_w_in, g_cq=v_g_cq, g_ckv=v_g_ckv, w_uq=v_w_uq, w_uk=v_w_uk, w_uv=v_w_uv, w_o=v_w_o, ln1_g=v_ln1_g,
             ln1_b=v_ln1_b, w_up=v_w_up, conv_w=v_conv_w, conv_b=v_conv_b, w_down=v_w_down, ln2_g=v_ln2_g, ln2_b=v_ln2_b)
    me = 4 * lax.axis_index("x") + 2 * lax.axis_index("y") + lax.axis_index("c")
    wire = lambda a: a.astype(WIRE_DTYPE)
    pad_taps = lambda a: jnp.pad(a, ((0, 8 - a.shape[0]), (0, 0)))

    own_slot = lambda buf, block: lax.dynamic_update_index_in_dim(buf, block, me, 0)
    blocks = lambda a: wire(a).reshape((N_DEV, a.shape[0] // N_DEV) + a.shape[1:])

    g_in, g_uq, g_conv = _all_gather(
        [wire(w_in).T, _heads_major(wire(w_uq)), pad_taps(conv_w)],
        "gather_weights")
    late = _send_start([wire(w_o), wire(w_up).T, wire(w_down)], [True] * 3, "gather_late_start")
    r_uq_dev, e_uq = w_uq.shape[0], w_uq.shape[2]
    wq = jnp.transpose(g_uq.reshape(N_DEV, HEADS, r_uq_dev, e_uq), (1, 0, 2, 3)).reshape(HEADS, Q_RANK, e_uq)
    cw = dict(
        w_in_t=_split_pad_rows(g_in.reshape(-1, D_MODEL)).astype(MXU_DTYPE),
        wq=jnp.pad(wq, ((0, 0), (0, 0), (0, LANES - e_uq))).astype(MXU_DTYPE),
        wk=_pad_heads(w_uk, NOPE_DIM), wv=_pad_heads(w_uv, HEAD_DIM),
        conv_w=_ffn_interleave(jnp.transpose(g_conv[:, :conv_w.shape[0]], (1, 0, 2)).reshape(conv_w.shape[0], -1), 1),
        g_cq=_row(g_cq), g_ckv=_row(g_ckv), ln1_g=_row(ln1_g), ln1_b=_row(ln1_b), conv_b=_ffn_interleave(_row(conv_b), 1),
        ln2_g=_row(ln2_g), ln2_b=_row(ln2_b))

    def late_weights(stage, after):
        (own,), (got,) = _send_wait(late, [after], f"gather_{stage}_wait", only=[("w_o", "w_up", "w_down").index(stage)])
        full = own_slot(got, own).reshape(-1, D_MODEL)
        if stage == "w_o":
            w_o_mla, w_o_dil = _pad_w_o(full)
            return dict(w_o_mla=w_o_mla, w_o_dil=w_o_dil)
        if stage == "w_up":
            return dict(w_up_t=_ffn_interleave(full, 0).astype(MXU_DTYPE))
        return dict(w_down=full.astype(MXU_DTYPE))

    sent = {}

    def on_grads(stage, g):
        if stage == "ffn":
            sent[stage] = _send_start([blocks(_ffn_deinterleave(g["w_up_t"], 0)), blocks(g["w_down"])], [False] * 2,
                                      "exchange_ffn_start")
        elif stage == "w_o":
            return []
        elif stage == "mla":
            d_uq = wire(jnp.transpose(g["wq"][:, :, :e_uq].reshape(HEADS, N_DEV, r_uq_dev, e_uq), (1, 0, 2, 3))
                        ).reshape(N_DEV, HEADS * r_uq_dev, e_uq)
            dense = lambda a, width: wire(a[:, :, :width]).reshape(-1, LANES)
            small = dict(g_cq=g["g_cq"], g_ckv=g["g_ckv"], w_uk=dense(g["wk"], NOPE_DIM), w_uv=dense(g["wv"], HEAD_DIM),
                         ln1_g=g["ln1_g"], ln1_b=g["ln1_b"], conv_b=_ffn_deinterleave(g["conv_b"], 1), ln2_g=g["ln2_g"],
                         ln2_b=g["ln2_b"])
            everyone = [small[n] for n in REPLICATED] + [_ffn_deinterleave(g["conv_w"], 1), g["loss"]]
            sent[stage] = _send_start([blocks(_unpad_w_o(g["w_o_mla"], g["w_o_dil"])), d_uq] + everyone,
                                      [False] * 2 + [True] * len(everyone), "exchange_mla_start")
        else:
            sent[stage] = _send_start([blocks(_split_unpad_rows(g["w_in_t"]))], [False], "exchange_w_in_start")
        return [sent[stage]["token"]]

    _, grad_x, _ = _layer_grads(x[0], loss_target[0], cw, [late["token"]], late_weights, on_grads)

    def landed(handle, after, name):
        own, got = _send_wait(handle, after, name)
        pick = lambda src, whole: src if whole else lax.dynamic_index_in_dim(src, me, 0, keepdims=False)
        return [own_slot(buf, pick(src, whole)) for buf, src, whole in zip(got, own, handle["gather"])]

    out = {}

    def update(name, parts, view=None):
        to2d = {None: lambda a: a, "t": lambda a: a.T, "heads": _heads_major}[view]
        back = {None: lambda a: a, "t": lambda a: a.T, "heads": lambda a: _heads_minor(a, HEADS)}[view]
        res = _adamw(parts, to2d(w[name]), to2d(m[name]), to2d(v[name]), "adamw_" + name)
        for kind, a in zip(("grad", "delta", "new_m", "new_v"), res):
            out[kind, name] = back(a)
        return res[0]

    r_up, r_down = landed(sent["ffn"], [grad_x], "exchange_ffn_wait")
    r_o, r_uq, *rep_all, cw_all, loss_all = landed(sent["mla"], [grad_x], "exchange_mla_wait")
    done = [update("w_up", r_up, "t"), update("w_down", r_down), update("w_o", r_o), update("w_uq", r_uq, "heads")]
    heads_major = ("w_uk", "w_uv")
    two_d = lambda n, a: _heads_major(a) if n in heads_major else a.reshape(1, -1)
    rep_all = [p.reshape(N_DEV, -1, w[n].shape[2]) if n in heads_major else p for n, p in zip(REPLICATED, rep_all)]
    res, loss_sum = _adamw_small(rep_all, *[[two_d(n, d[n]) for n in REPLICATED] for d in (w, m, v)], loss_all, "adamw_replicated")
    for n, quad in zip(REPLICATED, res):
        for kind, a in zip(("grad", "delta", "new_m", "new_v"), quad):
            out[kind, n] = _heads_minor(a, HEADS) if n in heads_major else a.reshape(w[n].shape)
    loss = loss_sum[0, 0]
    ncw = conv_w.shape[1]
    done += [loss_sum, update("conv_w", lax.dynamic_slice_in_dim(cw_all[:, :conv_w.shape[0]], me * ncw, ncw, axis=2))]
    (r_in,) = landed(sent["w_in"], done, "exchange_w_in_wait")
    update("w_in", r_in, "t")

    return (loss, grad_x[None], *[out[kind, n] for kind in ("grad", "delta", "new_m", "new_v") for n in ALL_WEIGHTS])
```

```python
import math

import jax
import jax.numpy as jnp
import numpy as np
from jax import lax
from jax.experimental import pallas as pl
from jax.experimental.pallas import tpu as pltpu

F32 = jnp.float32
MXU_DTYPE = jnp.bfloat16
WIRE_DTYPE = jnp.bfloat16

N_DEV = 8
D_MODEL = 1024
HEADS = 8
HEAD_DIM = 64
LANES = 128
Q_RANK, KV_RANK, ROPE_DIM, NOPE_DIM = 256, 128, 32, 64
DIL_WIDTH = HEADS * HEAD_DIM
MLA_WIDTH = HEADS * HEAD_DIM
IN_PAD = 2048
DH_PART = IN_PAD // 4
D_FF = 2816
ROPE_THETA = 10000.0
DIL_PAIRS = ((128, 1), (512, 4), (2048, 16))
DIL_BLOCK = 128
DN_ALPHA = 2.0 ** 0.25
LN_EPS = 1e-5
RMS_EPS = 1e-6
ONES_LANE = HEAD_DIM
MLA_SCALE = 1.0 / math.sqrt(NOPE_DIM + ROPE_DIM)
MLA_SCALE_LOG2 = MLA_SCALE * math.log2(math.e)
MLA_BWD_SPLITS = 2
MLA_BWD_SPLITS_DIAGONAL = 4
MLA_FWD_SPLITS_DIAGONAL = 2
DIL_SCALE = 1.0 / math.sqrt(HEAD_DIM)
ALIBI_SLOPES = tuple(2.0 ** (-8.0 * (h + 1) / HEADS) for h in range(HEADS))
NEG_BIG = -1e30
ADAM_LR, ADAM_B1, ADAM_B2, ADAM_EPS, ADAM_WD, ADAM_STEP = 0.001, 0.9, 0.999, 1e-08, 0.01, 10
VMEM_LIMIT = 48 * 1024 * 1024


def _params(*sem):
    return pltpu.CompilerParams(dimension_semantics=sem or None, vmem_limit_bytes=VMEM_LIMIT)


def _dot(a, b, ca, cb):
    return lax.dot_general(a, b, (((ca,), (cb,)), ((), ())), preferred_element_type=F32)


_ANY_SPEC = pl.BlockSpec(memory_space=pl.ANY)


def _mm(a, b, *, name, tm, tn, tk, ta=False, tb=False, out_dtype=F32, after=()):
    m, k = (a.shape[1], a.shape[0]) if ta else a.shape
    n = b.shape[0] if tb else b.shape[1]
    assert (b.shape[1] if tb else b.shape[0]) == k
    tm, tn, tk = min(tm, m), min(tn, n), min(tk, k)
    assert m % tm == 0 and n % tn == 0 and k % tk == 0, (name, m, n, k, tm, tn, tk)
    nk = k // tk
    a_spec = (pl.BlockSpec((tk, tm), lambda i, j, kk: (kk, i)) if ta
              else pl.BlockSpec((tm, tk), lambda i, j, kk: (i, kk)))
    b_mode = dict(pipeline_mode=pl.Buffered(1)) if (tn == n and tk == k) else {}
    b_spec = (pl.BlockSpec((tn, tk), lambda i, j, kk: (j, kk), **b_mode) if tb
              else pl.BlockSpec((tk, tn), lambda i, j, kk: (kk, j), **b_mode))
    o_spec = pl.BlockSpec((tm, tn), lambda i, j, kk: (i, j))
    n_in = 2 + len(after)
    ca, cb = (0 if ta else 1), (1 if tb else 0)

    def body(*refs):
        a_ref, b_ref, o_ref = refs[0], refs[1], refs[n_in]
        part = _dot(a_ref[...].astype(MXU_DTYPE), b_ref[...].astype(MXU_DTYPE), ca, cb)
        if nk == 1:
            o_ref[...] = part.astype(o_ref.dtype)
            return
        acc_ref = refs[-1]
        kk = pl.program_id(2)

        @pl.when(kk == 0)
        def _():
            acc_ref[...] = part

        @pl.when(kk > 0)
        def _():
            acc_ref[...] += part

        @pl.when(kk == nk - 1)
        def _():
            o_ref[...] = acc_ref[...].astype(o_ref.dtype)

    return pl.pallas_call(
        body, name=name, grid=(m // tm, n // tn, nk), in_specs=[a_spec, b_spec] + [_ANY_SPEC] * len(after), out_specs=o_spec,
        out_shape=jax.ShapeDtypeStruct((m, n), out_dtype),
        scratch_shapes=[pltpu.VMEM((tm, tn), F32)] if nk > 1 else [],
        compiler_params=_params("parallel", "parallel", "arbitrary"),
    )(a, b, *after)


def _w_o_bwd(dz, o_mla, o_dil, w_o_mla, w_o_dil, tm=1024):
    seq, d = dz.shape
    tm = min(tm, seq)
    nstep = seq // tm

    def body(a_ref, om_ref, od_ref, wm_ref, wd_ref, dom_ref, dod_ref, dwm_ref, dwd_ref, accm_ref, accd_ref):
        step = pl.program_id(0)

        @pl.when(step == 0)
        def _():
            accm_ref[...] = jnp.zeros_like(accm_ref)
            accd_ref[...] = jnp.zeros_like(accd_ref)

        a = a_ref[...]
        for hd in range(HEADS):
            rows = slice(LANES * hd, LANES * (hd + 1))
            dom_ref[hd] = _dot(a, wm_ref[rows, :], 1, 1)
            accm_ref[rows, :] += _dot(om_ref[hd], a, 0, 0)
        dod_ref[...] = _dot(a, wd_ref[...], 1, 1)
        accd_ref[...] += _dot(od_ref[...], a, 0, 0)

        @pl.when(step == nstep - 1)
        def _():
            dwm_ref[...] = accm_ref[...].astype(dwm_ref.dtype)
            dwd_ref[...] = accd_ref[...].astype(dwd_ref.dtype)

    once = dict(pipeline_mode=pl.Buffered(1))
    return pl.pallas_call(
        body, name="w_o_bwd", grid=(nstep,),
        in_specs=[pl.BlockSpec((tm, d), lambda i: (i, 0)), pl.BlockSpec((HEADS, tm, LANES), lambda i: (0, i, 0)),
                  pl.BlockSpec((tm, DIL_WIDTH), lambda i: (i, 0)), pl.BlockSpec((HEADS * LANES, d), lambda i: (0, 0), **once),
                  pl.BlockSpec((DIL_WIDTH, d), lambda i: (0, 0), **once)],
        out_specs=[pl.BlockSpec((HEADS, tm, LANES), lambda i: (0, i, 0)), pl.BlockSpec((tm, DIL_WIDTH), lambda i: (i, 0)),
                   pl.BlockSpec((HEADS * LANES, d), lambda i: (0, 0)), pl.BlockSpec((DIL_WIDTH, d), lambda i: (0, 0))],
        out_shape=[jax.ShapeDtypeStruct((HEADS, seq, LANES), F32), jax.ShapeDtypeStruct((seq, DIL_WIDTH), F32),
                   jax.ShapeDtypeStruct((HEADS * LANES, d), MXU_DTYPE), jax.ShapeDtypeStruct((DIL_WIDTH, d), MXU_DTYPE)],
        scratch_shapes=[pltpu.VMEM((HEADS * LANES, d), F32), pltpu.VMEM((DIL_WIDTH, d), F32)],
        compiler_params=_params("arbitrary"),
    )(dz, o_mla, o_dil, w_o_mla, w_o_dil)


def _rope_tables(seq):
    half = ROPE_DIM // 2
    f32 = np.float32
    freqs = np.power(f32(ROPE_THETA), -np.arange(half, dtype=f32) / f32(half))
    ang = np.arange(seq, dtype=f32)[:, None] * freqs[None, :]
    cos, sin = np.cos(ang, dtype=f32), np.sin(ang, dtype=f32)
    one = np.ones((seq, NOPE_DIM), f32)
    tail = np.ones((seq, LANES - NOPE_DIM - ROPE_DIM), f32)
    ctab = np.concatenate([one, cos, cos, tail], axis=1)
    stab = np.concatenate([0 * one, -sin, sin, 0 * tail], axis=1)
    return jnp.asarray(ctab), jnp.asarray(stab)


def _rope_swap(t):
    lane = lax.broadcasted_iota(jnp.int32, t.shape, 1)
    half = ROPE_DIM // 2
    return jnp.where(lane < NOPE_DIM + half, pltpu.roll(t, LANES - half, 1), pltpu.roll(t, half, 1))


def _rope(t, ctab, stab):
    return t * ctab + _rope_swap(t) * stab


def _rope_inv(t, ctab, stab):
    return t * ctab - _rope_swap(t) * stab


def _rms(x, g):
    r = lax.rsqrt(jnp.mean(x * x, axis=-1, keepdims=True) + RMS_EPS)
    xh = x * r
    return xh, r, xh * g


def _mla_prep(h, g_cq, g_ckv, wq, wk, wv, ctab, stab, tm=512):
    seq = h.shape[0]
    tm = min(tm, seq)

    def body(h_ref, gq_ref, gk_ref, wq_ref, wk_ref, wv_ref, c_ref, s_ref, q_out, k_out, v_out):
        hb = h_ref[...]
        ctab_, stab_ = c_ref[...], s_ref[...]
        _, _, cqn = _rms(hb[:, :Q_RANK], gq_ref[...])
        _, _, ckn = _rms(hb[:, Q_RANK:Q_RANK + KV_RANK], gk_ref[...])
        cqn = cqn.astype(MXU_DTYPE)
        ckn = ckn.astype(MXU_DTYPE)
        krr = _rope(hb[:, Q_RANK + KV_RANK:], ctab_, stab_)
        ones_lane = (lax.broadcasted_iota(jnp.int32, (1, LANES), 1) == ONES_LANE).astype(F32)
        for hd in range(HEADS):
            q = _dot(cqn, wq_ref[hd], 1, 0)
            q_out[hd] = _rope(q, ctab_, stab_).astype(q_out.dtype)
            k_out[hd] = (_dot(ckn, wk_ref[hd], 1, 0) + krr).astype(k_out.dtype)
            v_out[hd] = (_dot(ckn, wv_ref[hd], 1, 0) + ones_lane).astype(v_out.dtype)

    full = lambda *shape: pl.BlockSpec(shape, lambda i: (0,) * len(shape))
    slab = pl.BlockSpec((HEADS, tm, LANES), lambda i: (0, i, 0))
    shp = jax.ShapeDtypeStruct((HEADS, seq, LANES), MXU_DTYPE)
    return pl.pallas_call(
        body, name="mla_prep", grid=(seq // tm,),
        in_specs=[pl.BlockSpec((tm, DH_PART), lambda i: (i, 0)), full(1, Q_RANK), full(1, KV_RANK),
                  full(HEADS, Q_RANK, LANES), full(HEADS, KV_RANK, LANES), full(HEADS, KV_RANK, LANES),
                  pl.BlockSpec((tm, LANES), lambda i: (i, 0)), pl.BlockSpec((tm, LANES), lambda i: (i, 0))],
        out_specs=[slab, slab, slab], out_shape=[shp, shp, shp],
        compiler_params=_params("parallel"),
    )(h, g_cq, g_ckv, wq, wk, wv, ctab, stab)


def _mla_prep_bwd(h, g_cq, g_ckv, wq, wk, wv, ctab, stab, dq, dk, dv, tm=512, after=()):
    seq = h.shape[0]
    tm = min(tm, seq)
    n_after = len(after)

    def body(h_ref, gq_ref, gk_ref, wq_ref, wk_ref, wv_ref, c_ref, s_ref, dq_ref, dk_ref, dv_ref, *rest):
        dh_ref, dwq_ref, dwk_ref, dwv_ref, dgq_ref, dgk_ref = rest[n_after:]

        @pl.when(pl.program_id(0) == 0)
        def _():
            for r in (dwq_ref, dwk_ref, dwv_ref, dgq_ref, dgk_ref):
                r[...] = jnp.zeros_like(r)

        hb = h_ref[...]
        ctab_, stab_ = c_ref[...], s_ref[...]
        gq, gk = gq_ref[...], gk_ref[...]
        xq, rq, cqn = _rms(hb[:, :Q_RANK], gq)
        xk, rk, ckn = _rms(hb[:, Q_RANK:Q_RANK + KV_RANK], gk)
        cqn = cqn.astype(MXU_DTYPE)
        ckn = ckn.astype(MXU_DTYPE)
        d_cqn = jnp.zeros((tm, Q_RANK), F32)
        d_ckn = jnp.zeros((tm, KV_RANK), F32)
        d_krr = jnp.zeros((tm, LANES), F32)
        for hd in range(HEADS):
            dqh = _rope_inv(dq_ref[hd], ctab_, stab_).astype(MXU_DTYPE)
            d_cqn += _dot(dqh, wq_ref[hd], 1, 1)
            dwq_ref[hd] += _dot(cqn, dqh, 0, 0)
            dkh = dk_ref[hd]
            d_krr += dkh
            dkh = dkh.astype(MXU_DTYPE)
            d_ckn += _dot(dkh, wk_ref[hd], 1, 1)
            dwk_ref[hd] += _dot(ckn, dkh, 0, 0)
            dvh = dv_ref[hd].astype(MXU_DTYPE)
            d_ckn += _dot(dvh, wv_ref[hd], 1, 1)
            dwv_ref[hd] += _dot(ckn, dvh, 0, 0)
        lane = lax.broadcasted_iota(jnp.int32, (tm, LANES), 1)
        rot = (lane >= NOPE_DIM) & (lane < NOPE_DIM + ROPE_DIM)
        d_kr = jnp.where(rot, _rope_inv(jnp.where(rot, d_krr, 0.0), ctab_, stab_), 0.0)

        def rms_bwd(dy, xh, r, g, dg_ref):
            dg_ref[...] += jnp.sum(dy * xh, axis=0, keepdims=True)
            dxh = dy * g
            return r * (dxh - xh * jnp.mean(dxh * xh, axis=-1, keepdims=True))

        d_cq = rms_bwd(d_cqn, xq, rq, gq, dgq_ref)
        d_ck = rms_bwd(d_ckn, xk, rk, gk, dgk_ref)
        dh_ref[...] = jnp.concatenate([d_cq, d_ck, d_kr], axis=1).astype(dh_ref.dtype)

    full = lambda *shape: pl.BlockSpec(shape, lambda i: (0,) * len(shape))
    slab = pl.BlockSpec((HEADS, tm, LANES), lambda i: (0, i, 0))
    return pl.pallas_call(
        body, name="mla_prep_bwd", grid=(seq // tm,),
        in_specs=[pl.BlockSpec((tm, DH_PART), lambda i: (i, 0)), full(1, Q_RANK), full(1, KV_RANK),
                  full(HEADS, Q_RANK, LANES), full(HEADS, KV_RANK, LANES), full(HEADS, KV_RANK, LANES),
                  pl.BlockSpec((tm, LANES), lambda i: (i, 0)), pl.BlockSpec((tm, LANES), lambda i: (i, 0)),
                  slab, slab, slab] + [_ANY_SPEC] * n_after,
        out_specs=[pl.BlockSpec((tm, DH_PART), lambda i: (i, 0)), full(HEADS, Q_RANK, LANES), full(HEADS, KV_RANK, LANES),
                   full(HEADS, KV_RANK, LANES), full(1, Q_RANK), full(1, KV_RANK)],
        out_shape=[jax.ShapeDtypeStruct((seq, DH_PART), MXU_DTYPE), jax.ShapeDtypeStruct((HEADS, Q_RANK, LANES), F32),
                   jax.ShapeDtypeStruct((HEADS, KV_RANK, LANES), F32), jax.ShapeDtypeStruct((HEADS, KV_RANK, LANES), F32),
                   jax.ShapeDtypeStruct((1, Q_RANK), F32), jax.ShapeDtypeStruct((1, KV_RANK), F32)],
        compiler_params=_params("arbitrary"),
    )(h, g_cq, g_ckv, wq, wk, wv, ctab, stab, dq, dk, dv, *after)


def _mla_attn_fwd(q, k, v, t=1024):
    _, seq, _ = q.shape
    t = min(t, seq)

    def body(q_ref, k_ref, v_ref, o_ref, ob_ref, lse_ref, m_ref, acc_ref, s_ref):
        i = pl.program_id(1)
        qb = q_ref[...]
        m_ref[...] = jnp.full_like(m_ref, NEG_BIG)
        acc_ref[...] = jnp.zeros_like(acc_ref)

        def scores(j):
            return _dot(qb, k_ref[pl.ds(pl.multiple_of(j * t, t), t), :], 1, 1) * MLA_SCALE_LOG2

        def softmax_pv(j, s, rows=slice(None), mask=None):
            vb = v_ref[pl.ds(pl.multiple_of(j * t, t), s.shape[1]), :]
            if mask is not None:
                s = jnp.where(mask, s, NEG_BIG)
            m_old = m_ref[rows, :]
            m_new = jnp.maximum(m_old, jnp.max(s, axis=1, keepdims=True))
            p = jnp.exp2(s - m_new)
            a = jnp.exp2(m_old - m_new)
            acc_ref[rows, :] = a * acc_ref[rows, :] + _dot(p.astype(MXU_DTYPE), vb, 1, 0)
            m_ref[rows, :] = m_new

        def softmax_pv_diagonal(j):
            th = t // MLA_FWD_SPLITS_DIAGONAL
            for hf in range(MLA_FWD_SPLITS_DIAGONAL):
                nk = (hf + 1) * th
                row = lax.broadcasted_iota(jnp.int32, (th, nk), 0) + hf * th
                rows = slice(hf * th, (hf + 1) * th)
                softmax_pv(j, s_ref[rows, 0:nk], rows, row >= lax.broadcasted_iota(jnp.int32, (th, nk), 1))

        s_ref[...] = scores(0)

        def loop_body(j, c):
            s_next = scores(j + 1)
            softmax_pv(j, s_ref[...])
            s_ref[...] = s_next
            return c

        lax.fori_loop(0, i, loop_body, 0)
        softmax_pv_diagonal(i)
        acc = acc_ref[...]
        l = acc[:, ONES_LANE:ONES_LANE + 1]
        o = jnp.where(lax.broadcasted_iota(jnp.int32, acc.shape, 1) < HEAD_DIM, acc * (1.0 / l), 0.0)
        o_ref[...] = o
        ob_ref[...] = o.astype(ob_ref.dtype)
        lse_ref[...] = jnp.broadcast_to(m_ref[...] + jnp.log2(l), lse_ref.shape)

    blk = pl.BlockSpec((None, t, LANES), lambda h, i: (h, i, 0))
    whole = pl.BlockSpec((None, seq, LANES), lambda h, i: (h, 0, 0))
    shp = jax.ShapeDtypeStruct((HEADS, seq, LANES), F32)
    return pl.pallas_call(
        body, name="mla_attn_fwd", grid=(HEADS, seq // t),
        in_specs=[blk, whole, whole], out_specs=[blk, blk, blk],
        out_shape=[shp, jax.ShapeDtypeStruct((HEADS, seq, LANES), MXU_DTYPE), shp],
        scratch_shapes=[pltpu.VMEM((t, 1), F32), pltpu.VMEM((t, LANES), F32), pltpu.VMEM((t, t), F32)],
        compiler_params=_params("parallel", "arbitrary"),
    )(q, k, v)


def _mla_attn_bwd(q, k, v, o, lse, do, t=1024):
    _, seq, _ = q.shape
    t = min(t, seq)
    nb = seq // t

    def body(q_ref, k_ref, v_ref, o_ref, lse_ref, do_ref, dq_ref, dk_ref, dv_ref, dl_ref, dka_ref, dva_ref):
        dq_ref[...] = jnp.zeros_like(dq_ref)

        def delta_body(i, c):
            rows = pl.ds(pl.multiple_of(i * t, t), t)
            dl_ref[rows, :] = jnp.sum(do_ref[rows, :] * o_ref[rows, :], axis=1, keepdims=True)
            return c

        lax.fori_loop(0, nb, delta_body, 0)

        def kblock(j, c):
            krows = pl.ds(pl.multiple_of(j * t, t), t)
            kb = k_ref[krows, :]
            vb = v_ref[krows, :]
            dka_ref[...] = jnp.zeros_like(dka_ref)
            dva_ref[...] = jnp.zeros_like(dva_ref)

            def qstep(i, masked):
                ns = MLA_BWD_SPLITS_DIAGONAL if masked else MLA_BWD_SPLITS
                th = t // ns
                rows = [pl.ds(pl.multiple_of(i * t + hf * th, th), th) for hf in range(ns)]
                qs = [q_ref[r, :] for r in rows]
                dos = [do_ref[r, :].astype(MXU_DTYPE) for r in rows]
                nkeys = [(hf + 1) * th if masked else t for hf in range(ns)]
                ss = [_dot(qs[hf], kb[:nkeys[hf]], 1, 1) * MLA_SCALE_LOG2 for hf in range(ns)]
                dps = [_dot(dos[hf], vb[:nkeys[hf]], 1, 1) for hf in range(ns)]
                for hf in range(ns):
                    s, nk = ss[hf], nkeys[hf]
                    if masked:
                        row = lax.broadcasted_iota(jnp.int32, (th, nk), 0) + hf * th
                        s = jnp.where(row >= lax.broadcasted_iota(jnp.int32, (th, nk), 1), s, NEG_BIG)
                    p = jnp.exp2(s - lse_ref[rows[hf], 0:1])
                    dva_ref[0:nk, :] += _dot(p.astype(MXU_DTYPE), dos[hf], 0, 0)
                    ds = (p * (dps[hf] - dl_ref[rows[hf], :]) * MLA_SCALE).astype(MXU_DTYPE)
                    dka_ref[0:nk, :] += _dot(ds, qs[hf], 0, 0)
                    dq_ref[rows[hf], :] += _dot(ds, kb[:nk], 1, 0)

            qstep(j, True)

            def qloop(i, c2):
                qstep(i, False)
                return c2

            lax.fori_loop(j + 1, nb, qloop, 0)
            dk_ref[krows, :] = dka_ref[...]
            dv_ref[krows, :] = dva_ref[...]
            return c

        lax.fori_loop(0, nb, kblock, 0)

    whole = pl.BlockSpec((None, seq, LANES), lambda h: (h, 0, 0))
    shp = jax.ShapeDtypeStruct((HEADS, seq, LANES), F32)
    return pl.pallas_call(
        body, name="mla_attn_bwd", grid=(HEADS,),
        in_specs=[whole] * 6, out_specs=[whole] * 3, out_shape=[shp] * 3,
        scratch_shapes=[pltpu.VMEM((seq, 1), F32), pltpu.VMEM((t, LANES), F32), pltpu.VMEM((t, LANES), F32)],
        compiler_params=_params("parallel"),
    )(q, k, v, o, lse, do)


DIL_CHUNK = DIL_BLOCK * max(d for _, d in DIL_PAIRS)
DIL_PAIR_LANES = 2 * HEAD_DIM
assert DIL_PAIR_LANES == LANES
DIL_UNROLL_FWD = 16
DIL_UNROLL_BWD = 16


def _dil_bias_tables(hp, dil):
    b = DIL_BLOCK
    iq = lax.broadcasted_iota(jnp.int32, (b, 2 * b), 0)
    ik = lax.broadcasted_iota(jnp.int32, (b, 2 * b), 1)
    off = iq + b - ik
    band = (off >= 0) & (off <= b)
    dist = (off * dil).astype(F32)
    every, first = [], []
    for hh in range(2):
        slope = jnp.where(hp == 0, ALIBI_SLOPES[hh], jnp.where(hp == 1, ALIBI_SLOPES[2 + hh],
                          jnp.where(hp == 2, ALIBI_SLOPES[4 + hh], ALIBI_SLOPES[6 + hh]))).astype(F32)
        bias = -slope * dist
        every.append(jnp.where(band, bias, NEG_BIG))
        first.append(jnp.where(band & (ik >= b), bias, NEG_BIG))
    return jnp.concatenate(every, axis=0), jnp.concatenate(first, axis=0)


def _dil_rows(start, dil):
    return pl.ds(start, DIL_BLOCK) if dil == 1 else pl.ds(start, DIL_BLOCK, stride=dil)


def _dil_block_pos(blk, c, dil):
    sc, r = blk // dil, blk % dil
    q0 = sc * (DIL_BLOCK * dil) + r
    kcur0 = c * DIL_CHUNK + q0
    first = kcur0 < DIL_BLOCK * dil
    kprev0 = jnp.where(first, kcur0, kcur0 - DIL_BLOCK * dil)
    return q0, kcur0, kprev0, first


def _pair_cols(hh):
    return slice(HEAD_DIM * hh, HEAD_DIM * (hh + 1))


def _first_head_lanes(shape):
    return lax.broadcasted_iota(jnp.int32, shape, 1) < HEAD_DIM


def _stack_pair(t):
    first = _first_head_lanes(t.shape)
    return jnp.concatenate([jnp.where(first, t, 0.0), jnp.where(first, 0.0, t)], axis=0).astype(MXU_DTYPE)


def _unstack_pair(t):
    rows = t.shape[0] // 2
    return jnp.where(_first_head_lanes((rows, t.shape[1])), t[:rows], t[rows:])


def _pair_column(t):
    return jnp.concatenate([t[:, 0:1], t[:, HEAD_DIM:HEAD_DIM + 1]], axis=0)


def _dil_fwd(h):
    seq = h.shape[0]
    assert seq % DIL_CHUNK == 0
    nblk = DIL_CHUNK // DIL_BLOCK
    rc = 256

    def body(q_ref, k_ref, v_ref, o_ref, ob_ref, lse_ref, *scr):
        o_scr, l_scr = scr[:3], scr[3:]
        hp, c = pl.program_id(0), pl.program_id(1)
        for bi, (_, dil) in enumerate(DIL_PAIRS):
            tables = _dil_bias_tables(hp, dil)

            def block(blk, carry, bi=bi, dil=dil, tables=tables):
                q0, kcur0, kprev0, first = _dil_block_pos(blk, c, dil)
                q2 = _stack_pair(q_ref[_dil_rows(q0, dil), :] * DIL_SCALE)
                kcat = jnp.concatenate([k_ref[_dil_rows(kprev0, dil), :], k_ref[_dil_rows(kcur0, dil), :]], axis=0).astype(MXU_DTYPE)
                vcat = jnp.concatenate([v_ref[_dil_rows(kprev0, dil), :], v_ref[_dil_rows(kcur0, dil), :]], axis=0).astype(MXU_DTYPE)
                s = _dot(q2, kcat, 1, 1) + jnp.where(first, tables[1], tables[0])
                mx = jnp.max(s, axis=1, keepdims=True)
                p = jnp.exp(s - mx)
                l = jnp.sum(p, axis=1, keepdims=True)
                o_scr[bi][_dil_rows(q0, dil), :] = _unstack_pair(_dot(p.astype(MXU_DTYPE), vcat, 1, 0) * (1.0 / l))
                l_scr[bi][_dil_rows(q0, dil), :] = _unstack_pair(jnp.broadcast_to(mx + jnp.log(l), (2 * DIL_BLOCK, LANES)))
                return carry

            lax.fori_loop(0, nblk, block, 0, unroll=DIL_UNROLL_FWD)

        def combine(i, carry):
            rows = pl.ds(pl.multiple_of(i * rc, rc), rc)
            ls = [l_scr[bi][rows, :] for bi in range(3)]
            mx = jnp.maximum(jnp.maximum(ls[0], ls[1]), ls[2])
            es = [jnp.exp(l - mx) for l in ls]
            den = es[0] + es[1] + es[2]
            o = (es[0] * o_scr[0][rows, :] + es[1] * o_scr[1][rows, :] + es[2] * o_scr[2][rows, :]) / den
            o_ref[rows, :] = o
            ob_ref[rows, :] = o.astype(ob_ref.dtype)
            lse_ref[rows, :] = mx + jnp.log(den)
            return carry

        lax.fori_loop(0, DIL_CHUNK // rc, combine, 0)

    nq = DIL_WIDTH // LANES
    chunk = lambda off: pl.BlockSpec((DIL_CHUNK, LANES), lambda hp, c: (c, off + hp))
    whole = lambda off: pl.BlockSpec((seq, LANES), lambda hp, c: (0, off + hp))
    shp = jax.ShapeDtypeStruct((seq, DIL_WIDTH), F32)
    return pl.pallas_call(
        body, name="dil_fwd", grid=(nq, seq // DIL_CHUNK),
        in_specs=[chunk(nq), whole(2 * nq), whole(3 * nq)], out_specs=[chunk(0), chunk(0), chunk(0)],
        out_shape=[shp, jax.ShapeDtypeStruct((seq, DIL_WIDTH), MXU_DTYPE), shp],
        scratch_shapes=[pltpu.VMEM((DIL_CHUNK, LANES), F32)] * 6,
        compiler_params=_params("parallel", "arbitrary"),
    )(h, h, h)


def _dil_bwd(h, o, lse, do, after=()):
    seq = h.shape[0]
    nblk = DIL_CHUNK // DIL_BLOCK
    nchunk = seq // DIL_CHUNK
    rc = 256

    n_after = len(after)

    def body(q_ref, k_ref, v_ref, o_ref, lse_ref, do_ref, *rest):
        dq_out, dk_out, dv_out, dl_scr, dq_ref, dk_ref, dv_ref = rest[n_after:]
        hp, c = pl.program_id(0), pl.program_id(1)

        @pl.when(c == 0)
        def _():
            dk_ref[...] = jnp.zeros_like(dk_ref)
            dv_ref[...] = jnp.zeros_like(dv_ref)

        def delta(i, carry):
            rows = pl.ds(pl.multiple_of(i * rc, rc), rc)
            prod = do_ref[rows, :] * o_ref[rows, :]
            dl_scr[rows, :] = jnp.concatenate(
                [jnp.broadcast_to(jnp.sum(prod[:, _pair_cols(hh)], axis=1, keepdims=True), (rc, HEAD_DIM)) for hh in range(2)], axis=1)
            return carry

        lax.fori_loop(0, DIL_CHUNK // rc, delta, 0)

        for bi, (_, dil) in enumerate(DIL_PAIRS):
            tables = _dil_bias_tables(hp, dil)

            def block(blk, carry, bi=bi, dil=dil, tables=tables):
                q0, kcur0, kprev0, first = _dil_block_pos(blk, c, dil)
                qrows = _dil_rows(q0, dil)
                q2 = _stack_pair(q_ref[qrows, :] * DIL_SCALE)
                kcat = jnp.concatenate([k_ref[_dil_rows(kprev0, dil), :], k_ref[_dil_rows(kcur0, dil), :]], axis=0).astype(MXU_DTYPE)
                vcat = jnp.concatenate([v_ref[_dil_rows(kprev0, dil), :], v_ref[_dil_rows(kcur0, dil), :]], axis=0).astype(MXU_DTYPE)
                do2 = _stack_pair(do_ref[qrows, :])
                s = _dot(q2, kcat, 1, 1) + jnp.where(first, tables[1], tables[0])
                p = jnp.exp(s - _pair_column(lse_ref[qrows, :]))
                dp = _dot(do2, vcat, 1, 1)
                ds = (p * (dp - _pair_column(dl_scr[qrows, :]))).astype(MXU_DTYPE)
                dq_b = _unstack_pair(_dot(ds, kcat, 1, 0)) * DIL_SCALE
                dk_b = _dot(ds, q2, 0, 0)
                dv_b = _dot(p.astype(MXU_DTYPE), do2, 0, 0)
                if bi == 0:
                    dq_ref[qrows, :] = dq_b
                else:
                    dq_ref[qrows, :] += dq_b
                dk_ref[_dil_rows(kprev0, dil), :] += dk_b[:DIL_BLOCK]
                dv_ref[_dil_rows(kprev0, dil), :] += dv_b[:DIL_BLOCK]
                dk_ref[_dil_rows(kcur0, dil), :] += dk_b[DIL_BLOCK:]
                dv_ref[_dil_rows(kcur0, dil), :] += dv_b[DIL_BLOCK:]
                return carry

            lax.fori_loop(0, nblk, block, 0, unroll=DIL_UNROLL_BWD)

        dq_out[...] = dq_ref[...].astype(dq_out.dtype)

        @pl.when(c == nchunk - 1)
        def _():
            dk_out[...] = dk_ref[...].astype(dk_out.dtype)
            dv_out[...] = dv_ref[...].astype(dv_out.dtype)

    nq = DIL_WIDTH // LANES
    chunk = lambda off: pl.BlockSpec((DIL_CHUNK, LANES), lambda hp, c: (c, off + hp))
    whole = lambda off: pl.BlockSpec((seq, LANES), lambda hp, c: (0, off + hp))
    shp = jax.ShapeDtypeStruct((seq, DIL_WIDTH), MXU_DTYPE)
    return pl.pallas_call(
        body, name="dil_bwd", grid=(nq, nchunk),
        in_specs=[chunk(nq), whole(2 * nq), whole(3 * nq), chunk(0), chunk(0), chunk(0)] + [_ANY_SPEC] * n_after,
        out_specs=[chunk(0), whole(0), whole(0)], out_shape=[shp, shp, shp],
        scratch_shapes=[pltpu.VMEM((DIL_CHUNK, LANES), F32), pltpu.VMEM((DIL_CHUNK, LANES), F32),
                        pltpu.VMEM((seq, LANES), F32), pltpu.VMEM((seq, LANES), F32)],
        compiler_params=_params("parallel", "arbitrary"),
    )(h, h, h, o, lse, do, *after)


def _mm_dx0(parts, w_in_t, res, tm=1024, after=()):
    seq, d = res.shape
    tm = min(tm, seq)
    n_after = len(after)

    def body(a0, a1, a2, a3, b_ref, r_ref, *rest):
        o_ref = rest[n_after]
        acc = _dot(a0[...], b_ref[0:DH_PART, :], 1, 0)
        for c, a in enumerate((a1, a2, a3), start=1):
            acc += _dot(a[...], b_ref[DH_PART * c:DH_PART * (c + 1), :], 1, 0)
        o_ref[...] = acc + DN_ALPHA * r_ref[...]

    blk = pl.BlockSpec((tm, DH_PART), lambda i: (i, 0))
    row = pl.BlockSpec((tm, d), lambda i: (i, 0))
    return pl.pallas_call(
        body, name="mm_dx0", grid=(seq // tm,),
        in_specs=[blk] * 4 + [pl.BlockSpec((IN_PAD, d), lambda i: (0, 0), pipeline_mode=pl.Buffered(1)), row] + [_ANY_SPEC] * n_after,
        out_specs=row, out_shape=jax.ShapeDtypeStruct((seq, d), F32), compiler_params=_params("parallel"),
    )(*parts, w_in_t, res, *after)


def _mm_dw_in(parts, x0, tk=1024):
    seq, d = x0.shape
    tk = min(tk, seq)
    nk = seq // tk

    def body(a0, a1, a2, a3, b_ref, o_ref, acc_ref):
        kk = pl.program_id(0)

        @pl.when(kk == 0)
        def _():
            acc_ref[...] = jnp.zeros_like(acc_ref)

        b = b_ref[...].astype(MXU_DTYPE)
        for c, a in enumerate((a0, a1, a2, a3)):
            acc_ref[DH_PART * c:DH_PART * (c + 1), :] += _dot(a[...], b, 0, 0)

        @pl.when(kk == nk - 1)
        def _():
            o_ref[...] = acc_ref[...].astype(o_ref.dtype)

    blk = pl.BlockSpec((tk, DH_PART), lambda kk: (kk, 0))
    return pl.pallas_call(
        body, name="mm_dw_in", grid=(nk,), in_specs=[blk] * 4 + [pl.BlockSpec((tk, d), lambda kk: (kk, 0))],
        out_specs=pl.BlockSpec((IN_PAD, d), lambda kk: (0, 0)), out_shape=jax.ShapeDtypeStruct((IN_PAD, d), MXU_DTYPE),
        scratch_shapes=[pltpu.VMEM((IN_PAD, d), F32)], compiler_params=_params("arbitrary"),
    )(*parts, x0)


def _ln_stats(z):
    mu = jnp.mean(z, axis=-1, keepdims=True)
    zc = z - mu
    r = lax.rsqrt(jnp.mean(zc * zc, axis=-1, keepdims=True) + LN_EPS)
    return zc * r, r


def _ln_bwd_math(dy, xh, r, g):
    dxh = dy * g
    return r * (dxh - jnp.mean(dxh, axis=-1, keepdims=True) - xh * jnp.mean(dxh * xh, axis=-1, keepdims=True))


def _mix_ln1(o_mla, o_dil, w_o_mla, w_o_dil, x0, g, b, tm=512):
    seq, d = x0.shape
    tm = min(tm, seq)

    def body(om_ref, od_ref, wm_ref, wd_ref, x_ref, g_ref, b_ref, z_ref, y_ref, yb_ref):
        mix = _dot(od_ref[...], wd_ref[...], 1, 0)
        for hd in range(HEADS):
            mix += _dot(om_ref[hd], wm_ref[LANES * hd:LANES * (hd + 1), :], 1, 0)
        z = DN_ALPHA * x_ref[...] + mix
        xh, _ = _ln_stats(z)
        y = xh * g_ref[...] + b_ref[...]
        z_ref[...] = z
        y_ref[...] = y
        yb_ref[...] = y.astype(yb_ref.dtype)

    blk = pl.BlockSpec((tm, d), lambda i: (i, 0))
    vec = pl.BlockSpec((1, d), lambda i: (0, 0))
    shp = jax.ShapeDtypeStruct((seq, d), F32)
    return pl.pallas_call(
        body, name="mix_ln1", grid=(seq // tm,),
        in_specs=[pl.BlockSpec((HEADS, tm, LANES), lambda i: (0, i, 0)), pl.BlockSpec((tm, DIL_WIDTH), lambda i: (i, 0)),
                  pl.BlockSpec((HEADS * LANES, d), lambda i: (0, 0)), pl.BlockSpec((DIL_WIDTH, d), lambda i: (0, 0)), blk, vec, vec],
        out_specs=[blk, blk, blk], out_shape=[shp, shp, jax.ShapeDtypeStruct((seq, d), MXU_DTYPE)],
        compiler_params=_params("parallel"))(o_mla, o_dil, w_o_mla, w_o_dil, x0, g, b)


def _dx1_ln1_bwd(du, w_up_t, dz2, z, g, tm=256, after=()):
    seq, d = z.shape
    kdim = du.shape[1]
    tm = min(tm, seq)
    n_after = len(after)

    def body(du_ref, w_ref, r_ref, z_ref, g_ref, *rest):
        dz_ref, dzb_ref, dg_ref, db_ref = rest[n_after:]

        @pl.when(pl.program_id(0) == 0)
        def _():
            dg_ref[...] = jnp.zeros_like(dg_ref)
            db_ref[...] = jnp.zeros_like(db_ref)

        dyb = _dot(du_ref[...], w_ref[...], 1, 0) + DN_ALPHA * r_ref[...]
        xh, r = _ln_stats(z_ref[...])
        dg_ref[...] += jnp.sum(dyb * xh, axis=0, keepdims=True)
        db_ref[...] += jnp.sum(dyb, axis=0, keepdims=True)
        dz = _ln_bwd_math(dyb, xh, r, g_ref[...])
        dz_ref[...] = dz
        dzb_ref[...] = dz.astype(dzb_ref.dtype)

    blk = pl.BlockSpec((tm, d), lambda i: (i, 0))
    vec = pl.BlockSpec((1, d), lambda i: (0, 0))
    return pl.pallas_call(
        body, name="dx1_ln1_bwd", grid=(seq // tm,),
        in_specs=[pl.BlockSpec((tm, kdim), lambda i: (i, 0)),
                  pl.BlockSpec((kdim, d), lambda i: (0, 0), pipeline_mode=pl.Buffered(1)), blk, blk, vec] + [_ANY_SPEC] * n_after,
        out_specs=[blk, blk, vec, vec],
        out_shape=[jax.ShapeDtypeStruct((seq, d), F32), jax.ShapeDtypeStruct((seq, d), MXU_DTYPE),
                   jax.ShapeDtypeStruct((1, d), F32), jax.ShapeDtypeStruct((1, d), F32)],
        compiler_params=_params("arbitrary"))(du, w_up_t, dz2, z, g, *after)


def _down_ln2_loss_bwd(act, w_down, x1, target, g, b, tm=512):
    seq, d = x1.shape
    kdim = act.shape[1]
    tm = min(tm, seq)

    def body(a_ref, w_ref, x_ref, t_ref, g_ref, b_ref, dz_ref, dzb_ref, loss_ref, dg_ref, db_ref):
        @pl.when(pl.program_id(0) == 0)
        def _():
            loss_ref[...] = jnp.zeros_like(loss_ref)
            dg_ref[...] = jnp.zeros_like(dg_ref)
            db_ref[...] = jnp.zeros_like(db_ref)

        gv = g_ref[...]
        z = DN_ALPHA * x_ref[...] + _dot(a_ref[...], w_ref[...], 1, 0)
        xh, r = _ln_stats(z)
        err = (xh * gv + b_ref[...]) - t_ref[...]
        loss_ref[...] += 0.5 * jnp.sum(jnp.mean(err * err, axis=-1, keepdims=True), axis=0, keepdims=True)
        dy = err * (1.0 / d)
        dg_ref[...] += jnp.sum(dy * xh, axis=0, keepdims=True)
        db_ref[...] += jnp.sum(dy, axis=0, keepdims=True)
        dz = _ln_bwd_math(dy, xh, r, gv)
        dz_ref[...] = dz
        dzb_ref[...] = dz.astype(dzb_ref.dtype)

    blk = pl.BlockSpec((tm, d), lambda i: (i, 0))
    vec = pl.BlockSpec((1, d), lambda i: (0, 0))
    return pl.pallas_call(
        body, name="down_ln2_loss_bwd", grid=(seq // tm,),
        in_specs=[pl.BlockSpec((tm, kdim), lambda i: (i, 0)),
                  pl.BlockSpec((kdim, d), lambda i: (0, 0), pipeline_mode=pl.Buffered(1)), blk, blk, vec, vec],
        out_specs=[blk, blk, pl.BlockSpec((1, LANES), lambda i: (0, 0)), vec, vec],
        out_shape=[jax.ShapeDtypeStruct((seq, d), F32), jax.ShapeDtypeStruct((seq, d), MXU_DTYPE),
                   jax.ShapeDtypeStruct((1, LANES), F32),
                   jax.ShapeDtypeStruct((1, d), F32), jax.ShapeDtypeStruct((1, d), F32)],
        compiler_params=_params("arbitrary"))(act, w_down, x1, target, g, b)


HALO = 16


def _conv_rows(e, w_ref, b_ref):
    y = b_ref[...] + w_ref[0:1, :] * pltpu.roll(e, 2, 0)
    y = y + w_ref[1:2, :] * pltpu.roll(e, 1, 0)
    return y + w_ref[2:3, :] * e


_GELU_C = math.sqrt(2.0 / math.pi)
_GELU_A = 0.044715


def _gelu(x):
    return 0.5 * x * (1.0 + jnp.tanh(_GELU_C * (x + _GELU_A * (x * x * x))))


CONV_TN = 256


def _ffn_interleave(a, axis):
    shp = a.shape
    a = a.reshape(shp[:axis] + (2, D_FF // CONV_TN, CONV_TN) + shp[axis + 1:])
    return jnp.swapaxes(a, axis, axis + 1).reshape(shp)


def _ffn_deinterleave(a, axis):
    shp = a.shape
    a = a.reshape(shp[:axis] + (D_FF // CONV_TN, 2, CONV_TN) + shp[axis + 1:])
    return jnp.swapaxes(a, axis, axis + 1).reshape(shp)


def _conv_gate_fwd(u, conv_w, conv_b, tm=1024):
    seq = u.shape[0]
    tm = min(tm, seq)
    tn = CONV_TN

    def body(u_ref, up_ref, w_ref, b_ref, o_ref):
        first = pl.program_id(0) == 0
        e = jnp.concatenate([jnp.where(first, 0.0, up_ref[...]), u_ref[...]], axis=0)
        y = _conv_rows(e, w_ref, b_ref)[HALO:]
        o_ref[...] = (_gelu(y[:, tn:]) * y[:, :tn]).astype(o_ref.dtype)

    hb = tm // HALO
    return pl.pallas_call(
        body, name="conv_gate_fwd", grid=(seq // tm, D_FF // tn),
        in_specs=[pl.BlockSpec((tm, 2 * tn), lambda i, j: (i, j)),
                  pl.BlockSpec((HALO, 2 * tn), lambda i, j: (jnp.maximum(i * hb - 1, 0), j)),
                  pl.BlockSpec((3, 2 * tn), lambda i, j: (0, j)), pl.BlockSpec((1, 2 * tn), lambda i, j: (0, j))],
        out_specs=pl.BlockSpec((tm, tn), lambda i, j: (i, j)), out_shape=jax.ShapeDtypeStruct((seq, D_FF), MXU_DTYPE),
        compiler_params=_params("parallel", "parallel"),
    )(u, u, conv_w, conv_b)


def _conv_gate_bwd(u, d_act, conv_w, conv_b, tm=1024):
    seq = u.shape[0]
    tm = min(tm, seq)
    tn = CONV_TN
    ni = seq // tm
    rows_e = tm + 2 * HALO

    def body(u_ref, up_ref, un_ref, da_ref, dan_ref, w_ref, b_ref, du_ref, dw_ref, db_ref):
        i = pl.program_id(1)
        first, last = i == 0, i == ni - 1

        @pl.when(i == 0)
        def _():
            dw_ref[...] = jnp.zeros_like(dw_ref)
            db_ref[...] = jnp.zeros_like(db_ref)

        e = jnp.concatenate([jnp.where(first, 0.0, up_ref[...]), u_ref[...], jnp.where(last, 0.0, un_ref[...])], axis=0)
        y = _conv_rows(e, w_ref, b_ref)
        ya, yg = y[:, :tn], y[:, tn:]
        dact = jnp.concatenate([jnp.zeros((HALO, tn), F32), da_ref[...].astype(F32),
                                jnp.where(last, 0.0, dan_ref[...].astype(F32))], axis=0)
        th = jnp.tanh(_GELU_C * (yg + _GELU_A * (yg * yg * yg)))
        gelu = 0.5 * yg * (1.0 + th)
        gelu_grad = 0.5 * (1.0 + th) + 0.5 * yg * (1.0 - th * th) * (_GELU_C * (1.0 + 3.0 * _GELU_A * (yg * yg)))
        dy = jnp.concatenate([dact * gelu, dact * ya * gelu_grad], axis=1)
        du = w_ref[2:3, :] * dy + w_ref[1:2, :] * pltpu.roll(dy, rows_e - 1, 0) + w_ref[0:1, :] * pltpu.roll(dy, rows_e - 2, 0)
        du_ref[...] = du[HALO:HALO + tm].astype(du_ref.dtype)
        dyt = dy[HALO:HALO + tm]
        dw_ref[0:1, :] += jnp.sum(dyt * pltpu.roll(e, 2, 0)[HALO:HALO + tm], axis=0, keepdims=True)
        dw_ref[1:2, :] += jnp.sum(dyt * pltpu.roll(e, 1, 0)[HALO:HALO + tm], axis=0, keepdims=True)
        dw_ref[2:3, :] += jnp.sum(dyt * e[HALO:HALO + tm], axis=0, keepdims=True)
        db_ref[...] += jnp.sum(dyt, axis=0, keepdims=True)

    hb = tm // HALO
    nh = seq // HALO
    prev = lambda j, i: (jnp.maximum(i * hb - 1, 0), j)
    nxt = lambda j, i: (jnp.minimum((i + 1) * hb, nh - 1), j)
    return pl.pallas_call(
        body, name="conv_gate_bwd", grid=(D_FF // tn, ni),
        in_specs=[pl.BlockSpec((tm, 2 * tn), lambda j, i: (i, j)), pl.BlockSpec((HALO, 2 * tn), prev),
                  pl.BlockSpec((HALO, 2 * tn), nxt), pl.BlockSpec((tm, tn), lambda j, i: (i, j)), pl.BlockSpec((HALO, tn), nxt),
                  pl.BlockSpec((3, 2 * tn), lambda j, i: (0, j)), pl.BlockSpec((1, 2 * tn), lambda j, i: (0, j))],
        out_specs=[pl.BlockSpec((tm, 2 * tn), lambda j, i: (i, j)), pl.BlockSpec((3, 2 * tn), lambda j, i: (0, j)),
                   pl.BlockSpec((1, 2 * tn), lambda j, i: (0, j))],
        out_shape=[jax.ShapeDtypeStruct((seq, 2 * D_FF), MXU_DTYPE), jax.ShapeDtypeStruct((3, 2 * D_FF), F32),
                   jax.ShapeDtypeStruct((1, 2 * D_FF), F32)],
        compiler_params=_params("parallel", "arbitrary"),
    )(u, u, u, d_act, d_act, conv_w, conv_b)


def _pad_heads(w, width):
    w = jnp.transpose(w, (1, 0, 2))
    return jnp.pad(w, ((0, 0), (0, 0), (0, LANES - width))).astype(MXU_DTYPE)


def _heads_major(a):
    return jnp.transpose(a, (1, 0, 2)).reshape(-1, a.shape[2])


def _heads_minor(a, heads):
    return jnp.transpose(a.reshape(heads, -1, a.shape[1]), (1, 0, 2))


_LATENT = Q_RANK + KV_RANK
_ROPE_AT = _LATENT + NOPE_DIM
_ROPE_END = _ROPE_AT + ROPE_DIM


def _split_pad_rows(w_t):
    z = lambda n: jnp.zeros((n, w_t.shape[1]), w_t.dtype)
    return jnp.concatenate([w_t[:_LATENT], z(_ROPE_AT - _LATENT), w_t[_LATENT:_LATENT + ROPE_DIM], z(DH_PART - _ROPE_END),
                            w_t[_LATENT + ROPE_DIM:]], axis=0)


def _split_unpad_rows(w_p):
    return jnp.concatenate([w_p[:_LATENT], w_p[_ROPE_AT:_ROPE_END], w_p[DH_PART:]], axis=0)


def _pad_w_o(w_o):
    mla = jnp.pad(w_o[:MLA_WIDTH].reshape(HEADS, HEAD_DIM, D_MODEL), ((0, 0), (0, LANES - HEAD_DIM), (0, 0)))
    return mla.reshape(HEADS * LANES, D_MODEL).astype(MXU_DTYPE), w_o[MLA_WIDTH:].astype(MXU_DTYPE)


def _unpad_w_o(d_mla, d_dil):
    return jnp.concatenate([d_mla.reshape(HEADS, LANES, D_MODEL)[:, :HEAD_DIM].reshape(MLA_WIDTH, D_MODEL), d_dil], axis=0)


def _row(v):
    return v.reshape(1, -1).astype(F32)


def _layer_grads(x0, target, cw, first_after=(), late_weights=None, on_grads=None):
    seq = x0.shape[0]
    ctab, stab = _rope_tables(seq)
    gq, gk = cw["g_cq"], cw["g_ckv"]
    wq, wk, wv = cw["wq"], cw["wk"], cw["wv"]
    notify = (lambda stage, grads: ()) if on_grads is None else on_grads

    h = _mm(x0, cw["w_in_t"], name="mm_h", tb=True, tm=1024, tn=IN_PAD, tk=1024, after=first_after)
    qf, kf, vp = _mla_prep(h, gq, gk, wq, wk, wv, ctab, stab)
    o_mla, o_mla_b, lse_mla = _mla_attn_fwd(qf, kf, vp)
    o_dil, o_dil_b, lse_dil = _dil_fwd(h)
    fetch = (lambda stage, after: {}) if late_weights is None else late_weights
    cw = {**cw, **fetch("w_o", o_mla_b)}
    cb = cw["conv_b"]
    z1, x1, x1b = _mix_ln1(o_mla_b, o_dil_b, cw["w_o_mla"], cw["w_o_dil"], x0, cw["ln1_g"], cw["ln1_b"])
    cw = {**cw, **fetch("w_up", x1b)}
    u = _mm(x1b, cw["w_up_t"], name="mm_up", tb=True, tm=512, tn=2 * D_FF, tk=1024)
    act = _conv_gate_fwd(u, cw["conv_w"], cb)
    cw = {**cw, **fetch("w_down", act)}
    dz2, dz2b, loss, d_ln2_g, d_ln2_b = _down_ln2_loss_bwd(act, cw["w_down"], x1, target, cw["ln2_g"], cw["ln2_b"])

    d_act = _mm(dz2b, cw["w_down"], name="mm_d_act", tb=True, out_dtype=MXU_DTYPE, tm=1024, tn=D_FF, tk=1024)
    d_w_down = _mm(act, dz2b, name="mm_dw_down", ta=True, out_dtype=MXU_DTYPE, tm=1408, tn=1024, tk=1024)
    du, d_conv_w, d_conv_b = _conv_gate_bwd(u, d_act, cw["conv_w"], cb)
    d_w_up_t = _mm(du, x1b, name="mm_dw_up", ta=True, out_dtype=MXU_DTYPE, tm=1408, tn=1024, tk=2048)
    grads = dict(w_up_t=d_w_up_t, w_down=d_w_down, conv_w=d_conv_w, conv_b=d_conv_b, ln2_g=d_ln2_g, ln2_b=d_ln2_b)
    dz1, dz1b, d_ln1_g, d_ln1_b = _dx1_ln1_bwd(du, cw["w_up_t"], dz2, z1, cw["ln1_g"], after=notify("ffn", grads))
    do_mla, do_dil, d_w_o_mla, d_w_o_dil = _w_o_bwd(dz1b, o_mla_b, o_dil_b, cw["w_o_mla"], cw["w_o_dil"])
    grads.update(w_o_mla=d_w_o_mla, w_o_dil=d_w_o_dil, ln1_g=d_ln1_g, ln1_b=d_ln1_b)
    dqf, dkf, dvf = _mla_attn_bwd(qf, kf, vp, o_mla, lse_mla, do_mla)
    dh_mla, d_wq, d_wk, d_wv, d_gq, d_gk = _mla_prep_bwd(h, gq, gk, wq, wk, wv, ctab, stab, dqf, dkf, dvf,
                                                          after=notify("w_o", grads))
    grads.update(wq=d_wq, wk=d_wk, wv=d_wv, g_cq=d_gq, g_ckv=d_gk, loss=loss)
    dq_dil, dk_dil, dv_dil = _dil_bwd(h, o_dil, lse_dil, do_dil, after=notify("mla", grads))
    dh = (dh_mla, dq_dil, dk_dil, dv_dil)
    grads.update(w_in_t=_mm_dw_in(dh, x0))
    grad_x = _mm_dx0(dh, cw["w_in_t"], dz1, after=notify("w_in", grads))
    return loss, grad_x, grads


def _all_gather(blocks, name):
    na = len(blocks)

    def body(*refs):
        ins, outs = refs[:na], refs[na:2 * na]
        send_sems, recv_sems, local_sems = refs[2 * na:]
        x, y, c = lax.axis_index("x"), lax.axis_index("y"), lax.axis_index("c")
        me, sibling = (x, y, c), (x, y, 1 - c)
        chips = [(1 - x, y), (x, 1 - y), (1 - x, 1 - y)]

        def slot(out, pos):
            return out.at[4 * pos[0] + 2 * pos[1] + pos[2]]

        def copy(a, k, block, to, src=None):
            return pltpu.make_async_remote_copy(
                src_ref=slot(outs[a], block) if src is None else src, dst_ref=slot(outs[a], block),
                send_sem=send_sems.at[7 * a + k], recv_sem=recv_sems.at[7 * a + k],
                device_id=to, device_id_type=pl.DeviceIdType.MESH)

        mine = [pltpu.make_async_copy(ins[a], slot(outs[a], me), local_sems.at[a]) for a in range(na)]
        for cp in mine:
            cp.start()
        first = []
        for a in range(na):
            first.append(copy(a, 0, me, sibling, src=ins[a]))
            first += [copy(a, 1 + j, me, (*chip, c), src=ins[a]) for j, chip in enumerate(chips)]
        for cp in first:
            cp.start()
        passed = []
        for j, chip in enumerate(chips):
            for a in range(na):
                copy(a, 1 + j, (*chip, c), me).wait_recv()
                cp = copy(a, 4 + j, (*chip, c), sibling)
                cp.start()
                passed.append(cp)
        for a in range(na):
            copy(a, 0, sibling, me).wait_recv()
            for j, chip in enumerate(chips):
                copy(a, 4 + j, (*chip, 1 - c), me).wait_recv()
        for cp in first + passed:
            cp.wait_send()
        for cp in mine:
            cp.wait()

    any_spec = pl.BlockSpec(memory_space=pl.ANY)
    return pl.pallas_call(
        body, name=name, in_specs=[any_spec] * na, out_specs=[any_spec] * na,
        out_shape=[jax.ShapeDtypeStruct((N_DEV,) + b.shape, b.dtype) for b in blocks],
        scratch_shapes=[pltpu.SemaphoreType.DMA((7 * na,)), pltpu.SemaphoreType.DMA((7 * na,)), pltpu.SemaphoreType.DMA((na,))],
    )(*blocks)


_HBM_SPEC = pl.BlockSpec(memory_space=pltpu.HBM)
_SEM_SPEC = pl.BlockSpec(memory_space=pltpu.SEMAPHORE)
_DATAFLOW = pltpu.CompilerParams(has_side_effects=pltpu.SideEffectType.DATAFLOW_SIDE_EFFECTING)


def _split_copies(which, ins, lands, send_sems, recv_sems, gather):
    x, y, c = lax.axis_index("x"), lax.axis_index("y"), lax.axis_index("c")
    me = 4 * x + 2 * y + c
    copies = []
    for a, src, land in zip(which, ins, lands):
        for d in range(1, N_DEV):
            px, py, pc = x ^ (d >> 2), y ^ ((d >> 1) & 1), c ^ (d & 1)
            copies.append(pltpu.make_async_remote_copy(
                src_ref=src if gather[a] else src.at[4 * px + 2 * py + pc], dst_ref=land.at[me],
                send_sem=send_sems.at[7 * a + d - 1], recv_sem=recv_sems.at[7 * a + d - 1],
                device_id=(px, py, pc), device_id_type=pl.DeviceIdType.MESH))
    return copies


def _send_start(srcs, gather, name):
    na = len(srcs)
    assert len(gather) == na
    land_types = [pltpu.HBM(((N_DEV,) + s.shape) if g else s.shape, s.dtype) for s, g in zip(srcs, gather)]

    def body(*refs):
        ins, lands = refs[:na], refs[na:2 * na]
        send_sems, recv_sems, token = refs[2 * na], refs[2 * na + 1], refs[-1]
        for cp in _split_copies(range(na), ins, lands, send_sems, recv_sems, gather):
            cp.start()
        token[...] = jnp.zeros_like(token)

    hbm = lambda a: pltpu.with_memory_space_constraint(a, pltpu.HBM)
    outs = pl.pallas_call(
        body, name=name,
        out_shape=(pltpu.SemaphoreType.DMA((7 * na,)), pltpu.SemaphoreType.DMA((7 * na,)),
                   *[pltpu.HBM(s.shape, s.dtype) for s in srcs], *land_types, jax.ShapeDtypeStruct((8, LANES), F32)),
        in_specs=[_HBM_SPEC] * (2 * na),
        out_specs=(_SEM_SPEC, _SEM_SPEC, *[_HBM_SPEC] * (2 * na), pl.BlockSpec(memory_space=pltpu.VMEM)),
        input_output_aliases={i: 2 + i for i in range(2 * na)}, compiler_params=_DATAFLOW,
    )(*[hbm(s) for s in srcs], *[hbm(lax.empty(t.shape, t.dtype)) for t in land_types])
    return dict(send=outs[0], recv=outs[1], srcs=list(outs[2:2 + na]), lands=list(outs[2 + na:2 + 2 * na]), token=outs[-1],
                gather=gather)


def _send_wait(handle, after, name, only=None):
    which = list(range(len(handle["srcs"]))) if only is None else list(only)
    na = len(which)
    gather = handle["gather"]
    after = list(after)

    def body(*refs):
        ins, lands = refs[:na], refs[na:2 * na]
        send_sems, recv_sems = refs[2 * na], refs[2 * na + 1]
        for cp in _split_copies(which, ins, lands, send_sems, recv_sems, gather):
            cp.wait_send()
            cp.wait_recv()

    both = [handle["srcs"][a] for a in which] + [handle["lands"][a] for a in which]
    outs = pl.pallas_call(
        body, name=name, out_shape=[pltpu.HBM(a.shape, a.dtype) for a in both],
        in_specs=[_HBM_SPEC] * (2 * na) + [_SEM_SPEC, _SEM_SPEC] + [_ANY_SPEC] * len(after),
        out_specs=[_HBM_SPEC] * (2 * na), input_output_aliases={i: i for i in range(2 * na)}, compiler_params=_DATAFLOW,
    )(*both, handle["send"], handle["recv"], *after)
    return list(outs[:na]), list(outs[na:])


def _sum_slots(p_ref):
    g = p_ref[0].astype(F32)
    for s in range(1, p_ref.shape[0]):
        g = g + p_ref[s].astype(F32)
    return g


def _adamw_refs(g, w_ref, m_ref, v_ref, g_out, d_out, m_out, v_out):
    c1 = 1.0 - ADAM_B1 ** ADAM_STEP
    c2 = 1.0 - ADAM_B2 ** ADAM_STEP
    m_new = ADAM_B1 * m_ref[...] + (1.0 - ADAM_B1) * g
    v_new = ADAM_B2 * v_ref[...] + (1.0 - ADAM_B2) * (g * g)
    g_out[...] = g
    m_out[...] = m_new
    v_out[...] = v_new
    d_out[...] = -ADAM_LR * ((m_new / c1) / (jnp.sqrt(v_new / c2) + ADAM_EPS) + ADAM_WD * w_ref[...])


def _adamw(parts, w, m, v, name):
    npart, r, n = parts.shape
    tr = r if r <= 256 else max(t for t in range(16, 257, 16) if r % t == 0)

    def body(p_ref, w_ref, m_ref, v_ref, g_out, d_out, m_out, v_out):
        _adamw_refs(_sum_slots(p_ref), w_ref, m_ref, v_ref, g_out, d_out, m_out, v_out)

    blk = pl.BlockSpec((tr, n), lambda i: (i, 0))
    shp = jax.ShapeDtypeStruct((r, n), F32)
    return pl.pallas_call(
        body, name=name, grid=(r // tr,), in_specs=[pl.BlockSpec((npart, tr, n), lambda i: (0, i, 0)), blk, blk, blk],
        out_specs=[blk] * 4, out_shape=[shp] * 4, compiler_params=_params("parallel"),
    )(parts, w, m, v)


def _adamw_small(parts, ws, ms, vs, loss_parts, name):
    n = len(parts)

    def body(*refs):
        ins, outs = refs[:4 * n + 1], refs[4 * n + 1:]
        for i in range(n):
            _adamw_refs(_sum_slots(ins[i]), ins[n + i], ins[2 * n + i], ins[3 * n + i], *outs[4 * i:4 * i + 4])
        outs[4 * n][...] = _sum_slots(ins[4 * n])

    out_shape = [jax.ShapeDtypeStruct(w.shape, F32) for w in ws for _ in range(4)]
    res = pl.pallas_call(body, name=name, out_shape=out_shape + [jax.ShapeDtypeStruct((1, LANES), F32)],
                         compiler_params=_params())(*parts, *ws, *ms, *vs, loss_parts)
    return [res[4 * i:4 * i + 4] for i in range(n)], res[4 * n]


REPLICATED = ("g_cq", "g_ckv", "w_uk", "w_uv", "ln1_g", "ln1_b", "conv_b", "ln2_g", "ln2_b")
ALL_WEIGHTS = ("w_in", "g_cq", "g_ckv", "w_uq", "w_uk", "w_uv", "w_o", "ln1_g", "ln1_b", "w_up", "conv_w", "conv_b",
               "w_down", "ln2_g", "ln2_b")


def kernel(x, w_in, g_cq, g_ckv, w_uq, w_uk, w_uv, w_o, ln1_g, ln1_b, w_up, conv_w, conv_b, w_down, ln2_g, ln2_b, loss_target, m_w_in, m_g_cq, m_g_ckv, m_w_uq, m_w_uk, m_w_uv, m_w_o, m_ln1_g, m_ln1_b, m_w_up, m_conv_w, m_conv_b, m_w_down, m_ln2_g, m_ln2_b, v_w_in, v_g_cq, v_g_ckv, v_w_uq, v_w_uk, v_w_uv, v_w_o, v_ln1_g, v_ln1_b, v_w_up, v_conv_w, v_conv_b, v_w_down, v_ln2_g, v_ln2_b):
    w = dict(w_in=w_in, g_cq=g_cq, g_ckv=g_ckv, w_uq=w_uq, w_uk=w_uk, w_uv=w_uv, w_o=w_o, ln1_g=ln1_g, ln1_b=ln1_b,
             w_up=w_up, conv_w=conv_w, conv_b=conv_b, w_down=w_down, ln2_g=ln2_g, ln2_b=ln2_b)
    m = dict(w_in=m_w_in, g_cq=m_g_cq, g_ckv=m_g_ckv, w_uq=m_w_uq, w_uk=m_w_uk, w_uv=m_w_uv, w_o=m_w_o, ln1_g=m_ln1_g,
             ln1_b=m_ln1_b, w_up=m_w_up, conv_w=m_conv_w, conv_b=m_conv_b, w_down=m_w_down, ln2_g=m_ln2_g, ln2_b=m_ln2_b)
    v = dict(w_in=v_w_in, g_cq=v_g_cq, g_ckv=v_g_ckv, w_uq=v_w_uq, w_uk=v_w_uk, w_uv=v_w_uv, w_o=v_w_o, ln1_g=v_ln1_g,
             ln1_b=v_ln1_b, w_up=v_w_up, conv_w=v_conv_w, conv_b=v_conv_b, w_down=v_w_down, ln2_g=v_ln2_g, ln2_b=v_ln2_b)
    me = 4 * lax.axis_index("x") + 2 * lax.axis_index("y") + lax.axis_index("c")
    wire = lambda a: a.astype(WIRE_DTYPE)
    pad_taps = lambda a: jnp.pad(a, ((0, 8 - a.shape[0]), (0, 0)))

    own_slot = lambda buf, block: lax.dynamic_update_index_in_dim(buf, block, me, 0)
    blocks = lambda a: wire(a).reshape((N_DEV, a.shape[0] // N_DEV) + a.shape[1:])

    g_in, g_uq, g_conv = _all_gather(
        [wire(w_in).T, _heads_major(wire(w_uq)), pad_taps(conv_w)],
        "gather_weights")
    late = _send_start([wire(w_o), wire(w_up).T, wire(w_down)], [True] * 3, "gather_late_start")
    r_uq_dev, e_uq = w_uq.shape[0], w_uq.shape[2]
    wq = jnp.transpose(g_uq.reshape(N_DEV, HEADS, r_uq_dev, e_uq), (1, 0, 2, 3)).reshape(HEADS, Q_RANK, e_uq)
    cw = dict(
        w_in_t=_split_pad_rows(g_in.reshape(-1, D_MODEL)).astype(MXU_DTYPE),
        wq=jnp.pad(wq, ((0, 0), (0, 0), (0, LANES - e_uq))).astype(MXU_DTYPE),
        wk=_pad_heads(w_uk, NOPE_DIM), wv=_pad_heads(w_uv, HEAD_DIM),
        conv_w=_ffn_interleave(jnp.transpose(g_conv[:, :conv_w.shape[0]], (1, 0, 2)).reshape(conv_w.shape[0], -1), 1),
        g_cq=_row(g_cq), g_ckv=_row(g_ckv), ln1_g=_row(ln1_g), ln1_b=_row(ln1_b), conv_b=_ffn_interleave(_row(conv_b), 1),
        ln2_g=_row(ln2_g), ln2_b=_row(ln2_b))

    def late_weights(stage, after):
        (own,), (got,) = _send_wait(late, [after], f"gather_{stage}_wait", only=[("w_o", "w_up", "w_down").index(stage)])
        full = own_slot(got, own).reshape(-1, D_MODEL)
        if stage == "w_o":
            w_o_mla, w_o_dil = _pad_w_o(full)
            return dict(w_o_mla=w_o_mla, w_o_dil=w_o_dil)
        if stage == "w_up":
            return dict(w_up_t=_ffn_interleave(full, 0).astype(MXU_DTYPE))
        return dict(w_down=full.astype(MXU_DTYPE))

    sent = {}

    def on_grads(stage, g):
        if stage == "ffn":
            sent[stage] = _send_start([blocks(_ffn_deinterleave(g["w_up_t"], 0)), blocks(g["w_down"])], [False] * 2,
                                      "exchange_ffn_start")
        elif stage == "w_o":
            return []
        elif stage == "mla":
            d_uq = wire(jnp.transpose(g["wq"][:, :, :e_uq].reshape(HEADS, N_DEV, r_uq_dev, e_uq), (1, 0, 2, 3))
                        ).reshape(N_DEV, HEADS * r_uq_dev, e_uq)
            dense = lambda a, width: wire(a[:, :, :width]).reshape(-1, LANES)
            small = dict(g_cq=g["g_cq"], g_ckv=g["g_ckv"], w_uk=dense(g["wk"], NOPE_DIM), w_uv=dense(g["wv"], HEAD_DIM),
                         ln1_g=g["ln1_g"], ln1_b=g["ln1_b"], conv_b=_ffn_deinterleave(g["conv_b"], 1), ln2_g=g["ln2_g"],
                         ln2_b=g["ln2_b"])
            everyone = [small[n] for n in REPLICATED] + [_ffn_deinterleave(g["conv_w"], 1), g["loss"]]
            sent[stage] = _send_start([blocks(_unpad_w_o(g["w_o_mla"], g["w_o_dil"])), d_uq] + everyone,
                                      [False] * 2 + [True] * len(everyone), "exchange_mla_start")
        else:
            sent[stage] = _send_start([blocks(_split_unpad_rows(g["w_in_t"]))], [False], "exchange_w_in_start")
        return [sent[stage]["token"]]

    _, grad_x, _ = _layer_grads(x[0], loss_target[0], cw, [late["token"]], late_weights, on_grads)

    def landed(handle, after, name):
        own, got = _send_wait(handle, after, name)
        pick = lambda src, whole: src if whole else lax.dynamic_index_in_dim(src, me, 0, keepdims=False)
        return [own_slot(buf, pick(src, whole)) for buf, src, whole in zip(got, own, handle["gather"])]

    out = {}

    def update(name, parts, view=None):
        to2d = {None: lambda a: a, "t": lambda a: a.T, "heads": _heads_major}[view]
        back = {None: lambda a: a, "t": lambda a: a.T, "heads": lambda a: _heads_minor(a, HEADS)}[view]
        res = _adamw(parts, to2d(w[name]), to2d(m[name]), to2d(v[name]), "adamw_" + name)
        for kind, a in zip(("grad", "delta", "new_m", "new_v"), res):
            out[kind, name] = back(a)
        return res[0]

    r_up, r_down = landed(sent["ffn"], [grad_x], "exchange_ffn_wait")
    r_o, r_uq, *rep_all, cw_all, loss_all = landed(sent["mla"], [grad_x], "exchange_mla_wait")
    done = [update("w_up", r_up, "t"), update("w_down", r_down), update("w_o", r_o), update("w_uq", r_uq, "heads")]
    heads_major = ("w_uk", "w_uv")
    two_d = lambda n, a: _heads_major(a) if n in heads_major else a.reshape(1, -1)
    rep_all = [p.reshape(N_DEV, -1, w[n].shape[2]) if n in heads_major else p for n, p in zip(REPLICATED, rep_all)]
    res, loss_sum = _adamw_small(rep_all, *[[two_d(n, d[n]) for n in REPLICATED] for d in (w, m, v)], loss_all, "adamw_replicated")
    for n, quad in zip(REPLICATED, res):
        for kind, a in zip(("grad", "delta", "new_m", "new_v"), quad):
            out[kind, n] = _heads_minor(a, HEADS) if n in heads_major else a.reshape(w[n].shape)
    loss = loss_sum[0, 0]
    ncw = conv_w.shape[1]
    done += [loss_sum, update("conv_w", lax.dynamic_slice_in_dim(cw_all[:, :conv_w.shape[0]], me * ncw, ncw, axis=2))]
    (r_in,) = landed(sent["w_in"], done, "exchange_w_in_wait")
    update("w_in", r_in, "t")

    return (loss, grad_x[None], *[out[kind, n] for kind in ("grad", "delta", "new_m", "new_v") for n in ALL_WEIGHTS])
```

```python
import math

import jax
import jax.numpy as jnp
import numpy as np
from jax import lax
from jax.experimental import pallas as pl
from jax.experimental.pallas import tpu as pltpu

F32 = jnp.float32
MXU_DTYPE = jnp.bfloat16
WIRE_DTYPE = jnp.bfloat16

N_DEV = 8
D_MODEL = 1024
HEADS = 8
HEAD_DIM = 64
LANES = 128
Q_RANK, KV_RANK, ROPE_DIM, NOPE_DIM = 256, 128, 32, 64
DIL_WIDTH = HEADS * HEAD_DIM
MLA_WIDTH = HEADS * HEAD_DIM
IN_PAD = 2048
DH_PART = IN_PAD // 4
D_FF = 2816
ROPE_THETA = 10000.0
DIL_PAIRS = ((128, 1), (512, 4), (2048, 16))
DIL_BLOCK = 128
DN_ALPHA = 2.0 ** 0.25
LN_EPS = 1e-5
RMS_EPS = 1e-6
ONES_LANE = HEAD_DIM
MLA_SCALE = 1.0 / math.sqrt(NOPE_DIM + ROPE_DIM)
MLA_SCALE_LOG2 = MLA_SCALE * math.log2(math.e)
MLA_BWD_SPLITS = 2
MLA_BWD_SPLITS_DIAGONAL = 4
MLA_FWD_SPLITS_DIAGONAL = 2
DIL_SCALE = 1.0 / math.sqrt(HEAD_DIM)
ALIBI_SLOPES = tuple(2.0 ** (-8.0 * (h + 1) / HEADS) for h in range(HEADS))
NEG_BIG = -1e30
ADAM_LR, ADAM_B1, ADAM_B2, ADAM_EPS, ADAM_WD, ADAM_STEP = 0.001, 0.9, 0.999, 1e-08, 0.01, 10
VMEM_LIMIT = 48 * 1024 * 1024


def _params(*sem):
    return pltpu.CompilerParams(dimension_semantics=sem or None, vmem_limit_bytes=VMEM_LIMIT)


def _dot(a, b, ca, cb):
    return lax.dot_general(a, b, (((ca,), (cb,)), ((), ())), preferred_element_type=F32)


_ANY_SPEC = pl.BlockSpec(memory_space=pl.ANY)


def _mm(a, b, *, name, tm, tn, tk, ta=False, tb=False, out_dtype=F32, after=()):
    m, k = (a.shape[1], a.shape[0]) if ta else a.shape
    n = b.shape[0] if tb else b.shape[1]
    assert (b.shape[1] if tb else b.shape[0]) == k
    tm, tn, tk = min(tm, m), min(tn, n), min(tk, k)
    assert m % tm == 0 and n % tn == 0 and k % tk == 0, (name, m, n, k, tm, tn, tk)
    nk = k // tk
    a_spec = (pl.BlockSpec((tk, tm), lambda i, j, kk: (kk, i)) if ta
              else pl.BlockSpec((tm, tk), lambda i, j, kk: (i, kk)))
    b_mode = dict(pipeline_mode=pl.Buffered(1)) if (tn == n and tk == k) else {}
    b_spec = (pl.BlockSpec((tn, tk), lambda i, j, kk: (j, kk), **b_mode) if tb
              else pl.BlockSpec((tk, tn), lambda i, j, kk: (kk, j), **b_mode))
    o_spec = pl.BlockSpec((tm, tn), lambda i, j, kk: (i, j))
    n_in = 2 + len(after)
    ca, cb = (0 if ta else 1), (1 if tb else 0)

    def body(*refs):
        a_ref, b_ref, o_ref = refs[0], refs[1], refs[n_in]
        part = _dot(a_ref[...].astype(MXU_DTYPE), b_ref[...].astype(MXU_DTYPE), ca, cb)
        if nk == 1:
            o_ref[...] = part.astype(o_ref.dtype)
            return
        acc_ref = refs[-1]
        kk = pl.program_id(2)

        @pl.when(kk == 0)
        def _():
            acc_ref[...] = part

        @pl.when(kk > 0)
        def _():
            acc_ref[...] += part

        @pl.when(kk == nk - 1)
        def _():
            o_ref[...] = acc_ref[...].astype(o_ref.dtype)

    return pl.pallas_call(
        body, name=name, grid=(m // tm, n // tn, nk), in_specs=[a_spec, b_spec] + [_ANY_SPEC] * len(after), out_specs=o_spec,
        out_shape=jax.ShapeDtypeStruct((m, n), out_dtype),
        scratch_shapes=[pltpu.VMEM((tm, tn), F32)] if nk > 1 else [],
        compiler_params=_params("parallel", "parallel", "arbitrary"),
    )(a, b, *after)


def _w_o_bwd(dz, o_mla, o_dil, w_o_mla, w_o_dil, tm=1024):
    seq, d = dz.shape
    tm = min(tm, seq)
    nstep = seq // tm

    def body(a_ref, om_ref, od_ref, wm_ref, wd_ref, dom_ref, dod_ref, dwm_ref, dwd_ref, accm_ref, accd_ref):
        step = pl.program_id(0)

        @pl.when(step == 0)
        def _():
            accm_ref[...] = jnp.zeros_like(accm_ref)
            accd_ref[...] = jnp.zeros_like(accd_ref)

        a = a_ref[...]
        for hd in range(HEADS):
            rows = slice(LANES * hd, LANES * (hd + 1))
            dom_ref[hd] = _dot(a, wm_ref[rows, :], 1, 1)
            accm_ref[rows, :] += _dot(om_ref[hd], a, 0, 0)
        dod_ref[...] = _dot(a, wd_ref[...], 1, 1)
        accd_ref[...] += _dot(od_ref[...], a, 0, 0)

        @pl.when(step == nstep - 1)
        def _():
            dwm_ref[...] = accm_ref[...].astype(dwm_ref.dtype)
            dwd_ref[...] = accd_ref[...].astype(dwd_ref.dtype)

    once = dict(pipeline_mode=pl.Buffered(1))
    return pl.pallas_call(
        body, name="w_o_bwd", grid=(nstep,),
        in_specs=[pl.BlockSpec((tm, d), lambda i: (i, 0)), pl.BlockSpec((HEADS, tm, LANES), lambda i: (0, i, 0)),
                  pl.BlockSpec((tm, DIL_WIDTH), lambda i: (i, 0)), pl.BlockSpec((HEADS * LANES, d), lambda i: (0, 0), **once),
                  pl.BlockSpec((DIL_WIDTH, d), lambda i: (0, 0), **once)],
        out_specs=[pl.BlockSpec((HEADS, tm, LANES), lambda i: (0, i, 0)), pl.BlockSpec((tm, DIL_WIDTH), lambda i: (i, 0)),
                   pl.BlockSpec((HEADS * LANES, d), lambda i: (0, 0)), pl.BlockSpec((DIL_WIDTH, d), lambda i: (0, 0))],
        out_shape=[jax.ShapeDtypeStruct((HEADS, seq, LANES), F32), jax.ShapeDtypeStruct((seq, DIL_WIDTH), F32),
                   jax.ShapeDtypeStruct((HEADS * LANES, d), MXU_DTYPE), jax.ShapeDtypeStruct((DIL_WIDTH, d), MXU_DTYPE)],
        scratch_shapes=[pltpu.VMEM((HEADS * LANES, d), F32), pltpu.VMEM((DIL_WIDTH, d), F32)],
        compiler_params=_params("arbitrary"),
    )(dz, o_mla, o_dil, w_o_mla, w_o_dil)


def _rope_tables(seq):
    half = ROPE_DIM // 2
    f32 = np.float32
    freqs = np.power(f32(ROPE_THETA), -np.arange(half, dtype=f32) / f32(half))
    ang = np.arange(seq, dtype=f32)[:, None] * freqs[None, :]
    cos, sin = np.cos(ang, dtype=f32), np.sin(ang, dtype=f32)
    one = np.ones((seq, NOPE_DIM), f32)
    tail = np.ones((seq, LANES - NOPE_DIM - ROPE_DIM), f32)
    ctab = np.concatenate([one, cos, cos, tail], axis=1)
    stab = np.concatenate([0 * one, -sin, sin, 0 * tail], axis=1)
    return jnp.asarray(ctab), jnp.asarray(stab)


def _rope_swap(t):
    lane = lax.broadcasted_iota(jnp.int32, t.shape, 1)
    half = ROPE_DIM // 2
    return jnp.where(lane < NOPE_DIM + half, pltpu.roll(t, LANES - half, 1), pltpu.roll(t, half, 1))


def _rope(t, ctab, stab):
    return t * ctab + _rope_swap(t) * stab


def _rope_inv(t, ctab, stab):
    return t * ctab - _rope_swap(t) * stab


def _rms(x, g):
    r = lax.rsqrt(jnp.mean(x * x, axis=-1, keepdims=True) + RMS_EPS)
    xh = x * r
    return xh, r, xh * g


def _mla_prep(h, g_cq, g_ckv, wq, wk, wv, ctab, stab, tm=512):
    seq = h.shape[0]
    tm = min(tm, seq)

    def body(h_ref, gq_ref, gk_ref, wq_ref, wk_ref, wv_ref, c_ref, s_ref, q_out, k_out, v_out):
        hb = h_ref[...]
        ctab_, stab_ = c_ref[...], s_ref[...]
        _, _, cqn = _rms(hb[:, :Q_RANK], gq_ref[...])
        _, _, ckn = _rms(hb[:, Q_RANK:Q_RANK + KV_RANK], gk_ref[...])
        cqn = cqn.astype(MXU_DTYPE)
        ckn = ckn.astype(MXU_DTYPE)
        krr = _rope(hb[:, Q_RANK + KV_RANK:], ctab_, stab_)
        ones_lane = (lax.broadcasted_iota(jnp.int32, (1, LANES), 1) == ONES_LANE).astype(F32)
        for hd in range(HEADS):
            q = _dot(cqn, wq_ref[hd], 1, 0)
            q_out[hd] = _rope(q, ctab_, stab_).astype(q_out.dtype)
            k_out[hd] = (_dot(ckn, wk_ref[hd], 1, 0) + krr).astype(k_out.dtype)
            v_out[hd] = (_dot(ckn, wv_ref[hd], 1, 0) + ones_lane).astype(v_out.dtype)

    full = lambda *shape: pl.BlockSpec(shape, lambda i: (0,) * len(shape))
    slab = pl.BlockSpec((HEADS, tm, LANES), lambda i: (0, i, 0))
    shp = jax.ShapeDtypeStruct((HEADS, seq, LANES), MXU_DTYPE)
    return pl.pallas_call(
        body, name="mla_prep", grid=(seq // tm,),
        in_specs=[pl.BlockSpec((tm, DH_PART), lambda i: (i, 0)), full(1, Q_RANK), full(1, KV_RANK),
                  full(HEADS, Q_RANK, LANES), full(HEADS, KV_RANK, LANES), full(HEADS, KV_RANK, LANES),
                  pl.BlockSpec((tm, LANES), lambda i: (i, 0)), pl.BlockSpec((tm, LANES), lambda i: (i, 0))],
        out_specs=[slab, slab, slab], out_shape=[shp, shp, shp],
        compiler_params=_params("parallel"),
    )(h, g_cq, g_ckv, wq, wk, wv, ctab, stab)


def _mla_prep_bwd(h, g_cq, g_ckv, wq, wk, wv, ctab, stab, dq, dk, dv, tm=512, after=()):
    seq = h.shape[0]
    tm = min(tm, seq)
    n_after = len(after)

    def body(h_ref, gq_ref, gk_ref, wq_ref, wk_ref, wv_ref, c_ref, s_ref, dq_ref, dk_ref, dv_ref, *rest):
        dh_ref, dwq_ref, dwk_ref, dwv_ref, dgq_ref, dgk_ref = rest[n_after:]

        @pl.when(pl.program_id(0) == 0)
        def _():
            for r in (dwq_ref, dwk_ref, dwv_ref, dgq_ref, dgk_ref):
                r[...] = jnp.zeros_like(r)

        hb = h_ref[...]
        ctab_, stab_ = c_ref[...], s_ref[...]
        gq, gk = gq_ref[...], gk_ref[...]
        xq, rq, cqn = _rms(hb[:, :Q_RANK], gq)
        xk, rk, ckn = _rms(hb[:, Q_RANK:Q_RANK + KV_RANK], gk)
        cqn = cqn.astype(MXU_DTYPE)
        ckn = ckn.astype(MXU_DTYPE)
        d_cqn = jnp.zeros((tm, Q_RANK), F32)
        d_ckn = jnp.zeros((tm, KV_RANK), F32)
        d_krr = jnp.zeros((tm, LANES), F32)
        for hd in range(HEADS):
            dqh = _rope_inv(dq_ref[hd], ctab_, stab_).astype(MXU_DTYPE)
            d_cqn += _dot(dqh, wq_ref[hd], 1, 1)
            dwq_ref[hd] += _dot(cqn, dqh, 0, 0)
            dkh = dk_ref[hd]
            d_krr += dkh
            dkh = dkh.astype(MXU_DTYPE)
            d_ckn += _dot(dkh, wk_ref[hd], 1, 1)
            dwk_ref[hd] += _dot(ckn, dkh, 0, 0)
            dvh = dv_ref[hd].astype(MXU_DTYPE)
            d_ckn += _dot(dvh, wv_ref[hd], 1, 1)
            dwv_ref[hd] += _dot(ckn, dvh, 0, 0)
        lane = lax.broadcasted_iota(jnp.int32, (tm, LANES), 1)
        rot = (lane >= NOPE_DIM) & (lane < NOPE_DIM + ROPE_DIM)
        d_kr = jnp.where(rot, _rope_inv(jnp.where(rot, d_krr, 0.0), ctab_, stab_), 0.0)

        def rms_bwd(dy, xh, r, g, dg_ref):
            dg_ref[...] += jnp.sum(dy * xh, axis=0, keepdims=True)
            dxh = dy * g
            return r * (dxh - xh * jnp.mean(dxh * xh, axis=-1, keepdims=True))

        d_cq = rms_bwd(d_cqn, xq, rq, gq, dgq_ref)
        d_ck = rms_bwd(d_ckn, xk, rk, gk, dgk_ref)
        dh_ref[...] = jnp.concatenate([d_cq, d_ck, d_kr], axis=1).astype(dh_ref.dtype)

    full = lambda *shape: pl.BlockSpec(shape, lambda i: (0,) * len(shape))
    slab = pl.BlockSpec((HEADS, tm, LANES), lambda i: (0, i, 0))
    return pl.pallas_call(
        body, name="mla_prep_bwd", grid=(seq // tm,),
        in_specs=[pl.BlockSpec((tm, DH_PART), lambda i: (i, 0)), full(1, Q_RANK), full(1, KV_RANK),
                  full(HEADS, Q_RANK, LANES), full(HEADS, KV_RANK, LANES), full(HEADS, KV_RANK, LANES),
                  pl.BlockSpec((tm, LANES), lambda i: (i, 0)), pl.BlockSpec((tm, LANES), lambda i: (i, 0)),
                  slab, slab, slab] + [_ANY_SPEC] * n_after,
        out_specs=[pl.BlockSpec((tm, DH_PART), lambda i: (i, 0)), full(HEADS, Q_RANK, LANES), full(HEADS, KV_RANK, LANES),
                   full(HEADS, KV_RANK, LANES), full(1, Q_RANK), full(1, KV_RANK)],
        out_shape=[jax.ShapeDtypeStruct((seq, DH_PART), MXU_DTYPE), jax.ShapeDtypeStruct((HEADS, Q_RANK, LANES), F32),
                   jax.ShapeDtypeStruct((HEADS, KV_RANK, LANES), F32), jax.ShapeDtypeStruct((HEADS, KV_RANK, LANES), F32),
                   jax.ShapeDtypeStruct((1, Q_RANK), F32), jax.ShapeDtypeStruct((1, KV_RANK), F32)],
        compiler_params=_params("arbitrary"),
    )(h, g_cq, g_ckv, wq, wk, wv, ctab, stab, dq, dk, dv, *after)


def _mla_attn_fwd(q, k, v, t=1024):
    _, seq, _ = q.shape
    t = min(t, seq)

    def body(q_ref, k_ref, v_ref, o_ref, ob_ref, lse_ref, m_ref, acc_ref, s_ref):
        i = pl.program_id(1)
        qb = q_ref[...]
        m_ref[...] = jnp.full_like(m_ref, NEG_BIG)
        acc_ref[...] = jnp.zeros_like(acc_ref)

        def scores(j):
            return _dot(qb, k_ref[pl.ds(pl.multiple_of(j * t, t), t), :], 1, 1) * MLA_SCALE_LOG2

        def softmax_pv(j, s, rows=slice(None), mask=None):
            vb = v_ref[pl.ds(pl.multiple_of(j * t, t), s.shape[1]), :]
            if mask is not None:
                s = jnp.where(mask, s, NEG_BIG)
            m_old = m_ref[rows, :]
            m_new = jnp.maximum(m_old, jnp.max(s, axis=1, keepdims=True))
            p = jnp.exp2(s - m_new)
            a = jnp.exp2(m_old - m_new)
            acc_ref[rows, :] = a * acc_ref[rows, :] + _dot(p.astype(MXU_DTYPE), vb, 1, 0)
            m_ref[rows, :] = m_new

        def softmax_pv_diagonal(j):
            th = t // MLA_FWD_SPLITS_DIAGONAL
            for hf in range(MLA_FWD_SPLITS_DIAGONAL):
                nk = (hf + 1) * th
                row = lax.broadcasted_iota(jnp.int32, (th, nk), 0) + hf * th
                rows = slice(hf * th, (hf + 1) * th)
                softmax_pv(j, s_ref[rows, 0:nk], rows, row >= lax.broadcasted_iota(jnp.int32, (th, nk), 1))

        s_ref[...] = scores(0)

        def loop_body(j, c):
            s_next = scores(j + 1)
            softmax_pv(j, s_ref[...])
            s_ref[...] = s_next
            return c

        lax.fori_loop(0, i, loop_body, 0)
        softmax_pv_diagonal(i)
        acc = acc_ref[...]
        l = acc[:, ONES_LANE:ONES_LANE + 1]
        o = jnp.where(lax.broadcasted_iota(jnp.int32, acc.shape, 1) < HEAD_DIM, acc * (1.0 / l), 0.0)
        o_ref[...] = o
        ob_ref[...] = o.astype(ob_ref.dtype)
        lse_ref[...] = jnp.broadcast_to(m_ref[...] + jnp.log2(l), lse_ref.shape)

    blk = pl.BlockSpec((None, t, LANES), lambda h, i: (h, i, 0))
    whole = pl.BlockSpec((None, seq, LANES), lambda h, i: (h, 0, 0))
    shp = jax.ShapeDtypeStruct((HEADS, seq, LANES), F32)
    return pl.pallas_call(
        body, name="mla_attn_fwd", grid=(HEADS, seq // t),
        in_specs=[blk, whole, whole], out_specs=[blk, blk, blk],
        out_shape=[shp, jax.ShapeDtypeStruct((HEADS, seq, LANES), MXU_DTYPE), shp],
        scratch_shapes=[pltpu.VMEM((t, 1), F32), pltpu.VMEM((t, LANES), F32), pltpu.VMEM((t, t), F32)],
        compiler_params=_params("parallel", "arbitrary"),
    )(q, k, v)


def _mla_attn_bwd(q, k, v, o, lse, do, t=1024):
    _, seq, _ = q.shape
    t = min(t, seq)
    nb = seq // t

    def body(q_ref, k_ref, v_ref, o_ref, lse_ref, do_ref, dq_ref, dk_ref, dv_ref, dl_ref, dka_ref, dva_ref):
        dq_ref[...] = jnp.zeros_like(dq_ref)

        def delta_body(i, c):
            rows = pl.ds(pl.multiple_of(i * t, t), t)
            dl_ref[rows, :] = jnp.sum(do_ref[rows, :] * o_ref[rows, :], axis=1, keepdims=True)
            return c

        lax.fori_loop(0, nb, delta_body, 0)

        def kblock(j, c):
            krows = pl.ds(pl.multiple_of(j * t, t), t)
            kb = k_ref[krows, :]
            vb = v_ref[krows, :]
            dka_ref[...] = jnp.zeros_like(dka_ref)
            dva_ref[...] = jnp.zeros_like(dva_ref)

            def qstep(i, masked):
                ns = MLA_BWD_SPLITS_DIAGONAL if masked else MLA_BWD_SPLITS
                th = t // ns
                rows = [pl.ds(pl.multiple_of(i * t + hf * th, th), th) for hf in range(ns)]
                qs = [q_ref[r, :] for r in rows]
                dos = [do_ref[r, :].astype(MXU_DTYPE) for r in rows]
                nkeys = [(hf + 1) * th if masked else t for hf in range(ns)]
                ss = [_dot(qs[hf], kb[:nkeys[hf]], 1, 1) * MLA_SCALE_LOG2 for hf in range(ns)]
                dps = [_dot(dos[hf], vb[:nkeys[hf]], 1, 1) for hf in range(ns)]
                for hf in range(ns):
                    s, nk = ss[hf], nkeys[hf]
                    if masked:
                        row = lax.broadcasted_iota(jnp.int32, (th, nk), 0) + hf * th
                        s = jnp.where(row >= lax.broadcasted_iota(jnp.int32, (th, nk), 1), s, NEG_BIG)
                    p = jnp.exp2(s - lse_ref[rows[hf], 0:1])
                    dva_ref[0:nk, :] += _dot(p.astype(MXU_DTYPE), dos[hf], 0, 0)
                    ds = (p * (dps[hf] - dl_ref[rows[hf], :]) * MLA_SCALE).astype(MXU_DTYPE)
                    dka_ref[0:nk, :] += _dot(ds, qs[hf], 0, 0)
                    dq_ref[rows[hf], :] += _dot(ds, kb[:nk], 1, 0)

            qstep(j, True)

            def qloop(i, c2):
                qstep(i, False)
                return c2

            lax.fori_loop(j + 1, nb, qloop, 0)
            dk_ref[krows, :] = dka_ref[...]
            dv_ref[krows, :] = dva_ref[...]
            return c

        lax.fori_loop(0, nb, kblock, 0)

    whole = pl.BlockSpec((None, seq, LANES), lambda h: (h, 0, 0))
    shp = jax.ShapeDtypeStruct((HEADS, seq, LANES), F32)
    return pl.pallas_call(
        body, name="mla_attn_bwd", grid=(HEADS,),
        in_specs=[whole] * 6, out_specs=[whole] * 3, out_shape=[shp] * 3,
        scratch_shapes=[pltpu.VMEM((seq, 1), F32), pltpu.VMEM((t, LANES), F32), pltpu.VMEM((t, LANES), F32)],
        compiler_params=_params("parallel"),
    )(q, k, v, o, lse, do)


DIL_CHUNK = DIL_BLOCK * max(d for _, d in DIL_PAIRS)
DIL_PAIR_LANES = 2 * HEAD_DIM
assert DIL_PAIR_LANES == LANES
DIL_UNROLL_FWD = 16
DIL_UNROLL_BWD = 16


def _dil_bias_tables(hp, dil):
    b = DIL_BLOCK
    iq = lax.broadcasted_iota(jnp.int32, (b, 2 * b), 0)
    ik = lax.broadcasted_iota(jnp.int32, (b, 2 * b), 1)
    off = iq + b - ik
    band = (off >= 0) & (off <= b)
    dist = (off * dil).astype(F32)
    every, first = [], []
    for hh in range(2):
        slope = jnp.where(hp == 0, ALIBI_SLOPES[hh], jnp.where(hp == 1, ALIBI_SLOPES[2 + hh],
                          jnp.where(hp == 2, ALIBI_SLOPES[4 + hh], ALIBI_SLOPES[6 + hh]))).astype(F32)
        bias = -slope * dist
        every.append(jnp.where(band, bias, NEG_BIG))
        first.append(jnp.where(band & (ik >= b), bias, NEG_BIG))
    return jnp.concatenate(every, axis=0), jnp.concatenate(first, axis=0)


def _dil_rows(start, dil):
    return pl.ds(start, DIL_BLOCK) if dil == 1 else pl.ds(start, DIL_BLOCK, stride=dil)


def _dil_block_pos(blk, c, dil):
    sc, r = blk // dil, blk % dil
    q0 = sc * (DIL_BLOCK * dil) + r
    kcur0 = c * DIL_CHUNK + q0
    first = kcur0 < DIL_BLOCK * dil
    kprev0 = jnp.where(first, kcur0, kcur0 - DIL_BLOCK * dil)
    return q0, kcur0, kprev0, first


def _pair_cols(hh):
    return slice(HEAD_DIM * hh, HEAD_DIM * (hh + 1))


def _first_head_lanes(shape):
    return lax.broadcasted_iota(jnp.int32, shape, 1) < HEAD_DIM


def _stack_pair(t):
    first = _first_head_lanes(t.shape)
    return jnp.concatenate([jnp.where(first, t, 0.0), jnp.where(first, 0.0, t)], axis=0).astype(MXU_DTYPE)


def _unstack_pair(t):
    rows = t.shape[0] // 2
    return jnp.where(_first_head_lanes((rows, t.shape[1])), t[:rows], t[rows:])


def _pair_column(t):
    return jnp.concatenate([t[:, 0:1], t[:, HEAD_DIM:HEAD_DIM + 1]], axis=0)


def _dil_fwd(h):
    seq = h.shape[0]
    assert seq % DIL_CHUNK == 0
    nblk = DIL_CHUNK // DIL_BLOCK
    rc = 256

    def body(q_ref, k_ref, v_ref, o_ref, ob_ref, lse_ref, *scr):
        o_scr, l_scr = scr[:3], scr[3:]
        hp, c = pl.program_id(0), pl.program_id(1)
        for bi, (_, dil) in enumerate(DIL_PAIRS):
            tables = _dil_bias_tables(hp, dil)

            def block(blk, carry, bi=bi, dil=dil, tables=tables):
                q0, kcur0, kprev0, first = _dil_block_pos(blk, c, dil)
                q2 = _stack_pair(q_ref[_dil_rows(q0, dil), :] * DIL_SCALE)
                kcat = jnp.concatenate([k_ref[_dil_rows(kprev0, dil), :], k_ref[_dil_rows(kcur0, dil), :]], axis=0).astype(MXU_DTYPE)
                vcat = jnp.concatenate([v_ref[_dil_rows(kprev0, dil), :], v_ref[_dil_rows(kcur0, dil), :]], axis=0).astype(MXU_DTYPE)
                s = _dot(q2, kcat, 1, 1) + jnp.where(first, tables[1], tables[0])
                mx = jnp.max(s, axis=1, keepdims=True)
                p = jnp.exp(s - mx)
                l = jnp.sum(p, axis=1, keepdims=True)
                o_scr[bi][_dil_rows(q0, dil), :] = _unstack_pair(_dot(p.astype(MXU_DTYPE), vcat, 1, 0) * (1.0 / l))
                l_scr[bi][_dil_rows(q0, dil), :] = _unstack_pair(jnp.broadcast_to(mx + jnp.log(l), (2 * DIL_BLOCK, LANES)))
                return carry

            lax.fori_loop(0, nblk, block, 0, unroll=DIL_UNROLL_FWD)

        def combine(i, carry):
            rows = pl.ds(pl.multiple_of(i * rc, rc), rc)
            ls = [l_scr[bi][rows, :] for bi in range(3)]
            mx = jnp.maximum(jnp.maximum(ls[0], ls[1]), ls[2])
            es = [jnp.exp(l - mx) for l in ls]
            den = es[0] + es[1] + es[2]
            o = (es[0] * o_scr[0][rows, :] + es[1] * o_scr[1][rows, :] + es[2] * o_scr[2][rows, :]) / den
            o_ref[rows, :] = o
            ob_ref[rows, :] = o.astype(ob_ref.dtype)
            lse_ref[rows, :] = mx + jnp.log(den)
            return carry

        lax.fori_loop(0, DIL_CHUNK // rc, combine, 0)

    nq = DIL_WIDTH // LANES
    chunk = lambda off: pl.BlockSpec((DIL_CHUNK, LANES), lambda hp, c: (c, off + hp))
    whole = lambda off: pl.BlockSpec((seq, LANES), lambda hp, c: (0, off + hp))
    shp = jax.ShapeDtypeStruct((seq, DIL_WIDTH), F32)
    return pl.pallas_call(
        body, name="dil_fwd", grid=(nq, seq // DIL_CHUNK),
        in_specs=[chunk(nq), whole(2 * nq), whole(3 * nq)], out_specs=[chunk(0), chunk(0), chunk(0)],
        out_shape=[shp, jax.ShapeDtypeStruct((seq, DIL_WIDTH), MXU_DTYPE), shp],
        scratch_shapes=[pltpu.VMEM((DIL_CHUNK, LANES), F32)] * 6,
        compiler_params=_params("parallel", "arbitrary"),
    )(h, h, h)


def _dil_bwd(h, o, lse, do, after=()):
    seq = h.shape[0]
    nblk = DIL_CHUNK // DIL_BLOCK
    nchunk = seq // DIL_CHUNK
    rc = 256

    n_after = len(after)

    def body(q_ref, k_ref, v_ref, o_ref, lse_ref, do_ref, *rest):
        dq_out, dk_out, dv_out, dl_scr, dq_ref, dk_ref, dv_ref = rest[n_after:]
        hp, c = pl.program_id(0), pl.program_id(1)

        @pl.when(c == 0)
        def _():
            dk_ref[...] = jnp.zeros_like(dk_ref)
            dv_ref[...] = jnp.zeros_like(dv_ref)

        def delta(i, carry):
            rows = pl.ds(pl.multiple_of(i * rc, rc), rc)
            prod = do_ref[rows, :] * o_ref[rows, :]
            dl_scr[rows, :] = jnp.concatenate(
                [jnp.broadcast_to(jnp.sum(prod[:, _pair_cols(hh)], axis=1, keepdims=True), (rc, HEAD_DIM)) for hh in range(2)], axis=1)
            return carry

        lax.fori_loop(0, DIL_CHUNK // rc, delta, 0)

        for bi, (_, dil) in enumerate(DIL_PAIRS):
            tables = _dil_bias_tables(hp, dil)

            def block(blk, carry, bi=bi, dil=dil, tables=tables):
                q0, kcur0, kprev0, first = _dil_block_pos(blk, c, dil)
                qrows = _dil_rows(q0, dil)
                q2 = _stack_pair(q_ref[qrows, :] * DIL_SCALE)
                kcat = jnp.concatenate([k_ref[_dil_rows(kprev0, dil), :], k_ref[_dil_rows(kcur0, dil), :]], axis=0).astype(MXU_DTYPE)
                vcat = jnp.concatenate([v_ref[_dil_rows(kprev0, dil), :], v_ref[_dil_rows(kcur0, dil), :]], axis=0).astype(MXU_DTYPE)
                do2 = _stack_pair(do_ref[qrows, :])
                s = _dot(q2, kcat, 1, 1) + jnp.where(first, tables[1], tables[0])
                p = jnp.exp(s - _pair_column(lse_ref[qrows, :]))
                dp = _dot(do2, vcat, 1, 1)
                ds = (p * (dp - _pair_column(dl_scr[qrows, :]))).astype(MXU_DTYPE)
                dq_b = _unstack_pair(_dot(ds, kcat, 1, 0)) * DIL_SCALE
                dk_b = _dot(ds, q2, 0, 0)
                dv_b = _dot(p.astype(MXU_DTYPE), do2, 0, 0)
                if bi == 0:
                    dq_ref[qrows, :] = dq_b
                else:
                    dq_ref[qrows, :] += dq_b
                dk_ref[_dil_rows(kprev0, dil), :] += dk_b[:DIL_BLOCK]
                dv_ref[_dil_rows(kprev0, dil), :] += dv_b[:DIL_BLOCK]
                dk_ref[_dil_rows(kcur0, dil), :] += dk_b[DIL_BLOCK:]
                dv_ref[_dil_rows(kcur0, dil), :] += dv_b[DIL_BLOCK:]
                return carry

            lax.fori_loop(0, nblk, block, 0, unroll=DIL_UNROLL_BWD)

        dq_out[...] = dq_ref[...].astype(dq_out.dtype)

        @pl.when(c == nchunk - 1)
        def _():
            dk_out[...] = dk_ref[...].astype(dk_out.dtype)
            dv_out[...] = dv_ref[...].astype(dv_out.dtype)

    nq = DIL_WIDTH // LANES
    chunk = lambda off: pl.BlockSpec((DIL_CHUNK, LANES), lambda hp, c: (c, off + hp))
    whole = lambda off: pl.BlockSpec((seq, LANES), lambda hp, c: (0, off + hp))
    shp = jax.ShapeDtypeStruct((seq, DIL_WIDTH), MXU_DTYPE)
    return pl.pallas_call(
        body, name="dil_bwd", grid=(nq, nchunk),
        in_specs=[chunk(nq), whole(2 * nq), whole(3 * nq), chunk(0), chunk(0), chunk(0)] + [_ANY_SPEC] * n_after,
        out_specs=[chunk(0), whole(0), whole(0)], out_shape=[shp, shp, shp],
        scratch_shapes=[pltpu.VMEM((DIL_CHUNK, LANES), F32), pltpu.VMEM((DIL_CHUNK, LANES), F32),
                        pltpu.VMEM((seq, LANES), F32), pltpu.VMEM((seq, LANES), F32)],
        compiler_params=_params("parallel", "arbitrary"),
    )(h, h, h, o, lse, do, *after)


def _mm_dx0(parts, w_in_t, res, tm=1024, after=()):
    seq, d = res.shape
    tm = min(tm, seq)
    n_after = len(after)

    def body(a0, a1, a2, a3, b_ref, r_ref, *rest):
        o_ref = rest[n_after]
        acc = _dot(a0[...], b_ref[0:DH_PART, :], 1, 0)
        for c, a in enumerate((a1, a2, a3), start=1):
            acc += _dot(a[...], b_ref[DH_PART * c:DH_PART * (c + 1), :], 1, 0)
        o_ref[...] = acc + DN_ALPHA * r_ref[...]

    blk = pl.BlockSpec((tm, DH_PART), lambda i: (i, 0))
    row = pl.BlockSpec((tm, d), lambda i: (i, 0))
    return pl.pallas_call(
        body, name="mm_dx0", grid=(seq // tm,),
        in_specs=[blk] * 4 + [pl.BlockSpec((IN_PAD, d), lambda i: (0, 0), pipeline_mode=pl.Buffered(1)), row] + [_ANY_SPEC] * n_after,
        out_specs=row, out_shape=jax.ShapeDtypeStruct((seq, d), F32), compiler_params=_params("parallel"),
    )(*parts, w_in_t, res, *after)


def _mm_dw_in(parts, x0, tk=1024):
    seq, d = x0.shape
    tk = min(tk, seq)
    nk = seq // tk

    def body(a0, a1, a2, a3, b_ref, o_ref, acc_ref):
        kk = pl.program_id(0)

        @pl.when(kk == 0)
        def _():
            acc_ref[...] = jnp.zeros_like(acc_ref)

        b = b_ref[...].astype(MXU_DTYPE)
        for c, a in enumerate((a0, a1, a2, a3)):
            acc_ref[DH_PART * c:DH_PART * (c + 1), :] += _dot(a[...], b, 0, 0)

        @pl.when(kk == nk - 1)
        def _():
            o_ref[...] = acc_ref[...].astype(o_ref.dtype)

    blk = pl.BlockSpec((tk, DH_PART), lambda kk: (kk, 0))
    return pl.pallas_call(
        body, name="mm_dw_in", grid=(nk,), in_specs=[blk] * 4 + [pl.BlockSpec((tk, d), lambda kk: (kk, 0))],
        out_specs=pl.BlockSpec((IN_PAD, d), lambda kk: (0, 0)), out_shape=jax.ShapeDtypeStruct((IN_PAD, d), MXU_DTYPE),
        scratch_shapes=[pltpu.VMEM((IN_PAD, d), F32)], compiler_params=_params("arbitrary"),
    )(*parts, x0)


def _ln_stats(z):
    mu = jnp.mean(z, axis=-1, keepdims=True)
    zc = z - mu
    r = lax.rsqrt(jnp.mean(zc * zc, axis=-1, keepdims=True) + LN_EPS)
    return zc * r, r


def _ln_bwd_math(dy, xh, r, g):
    dxh = dy * g
    return r * (dxh - jnp.mean(dxh, axis=-1, keepdims=True) - xh * jnp.mean(dxh * xh, axis=-1, keepdims=True))


def _mix_ln1(o_mla, o_dil, w_o_mla, w_o_dil, x0, g, b, tm=512):
    seq, d = x0.shape
    tm = min(tm, seq)

    def body(om_ref, od_ref, wm_ref, wd_ref, x_ref, g_ref, b_ref, z_ref, y_ref, yb_ref):
        mix = _dot(od_ref[...], wd_ref[...], 1, 0)
        for hd in range(HEADS):
            mix += _dot(om_ref[hd], wm_ref[LANES * hd:LANES * (hd + 1), :], 1, 0)
        z = DN_ALPHA * x_ref[...] + mix
        xh, _ = _ln_stats(z)
        y = xh * g_ref[...] + b_ref[...]
        z_ref[...] = z
        y_ref[...] = y
        yb_ref[...] = y.astype(yb_ref.dtype)

    blk = pl.BlockSpec((tm, d), lambda i: (i, 0))
    vec = pl.BlockSpec((1, d), lambda i: (0, 0))
    shp = jax.ShapeDtypeStruct((seq, d), F32)
    return pl.pallas_call(
        body, name="mix_ln1", grid=(seq // tm,),
        in_specs=[pl.BlockSpec((HEADS, tm, LANES), lambda i: (0, i, 0)), pl.BlockSpec((tm, DIL_WIDTH), lambda i: (i, 0)),
                  pl.BlockSpec((HEADS * LANES, d), lambda i: (0, 0)), pl.BlockSpec((DIL_WIDTH, d), lambda i: (0, 0)), blk, vec, vec],
        out_specs=[blk, blk, blk], out_shape=[shp, shp, jax.ShapeDtypeStruct((seq, d), MXU_DTYPE)],
        compiler_params=_params("parallel"))(o_mla, o_dil, w_o_mla, w_o_dil, x0, g, b)


def _dx1_ln1_bwd(du, w_up_t, dz2, z, g, tm=256, after=()):
    seq, d = z.shape
    kdim = du.shape[1]
    tm = min(tm, seq)
    n_after = len(after)

    def body(du_ref, w_ref, r_ref, z_ref, g_ref, *rest):
        dz_ref, dzb_ref, dg_ref, db_ref = rest[n_after:]

        @pl.when(pl.program_id(0) == 0)
        def _():
            dg_ref[...] = jnp.zeros_like(dg_ref)
            db_ref[...] = jnp.zeros_like(db_ref)

        dyb = _dot(du_ref[...], w_ref[...], 1, 0) + DN_ALPHA * r_ref[...]
        xh, r = _ln_stats(z_ref[...])
        dg_ref[...] += jnp.sum(dyb * xh, axis=0, keepdims=True)
        db_ref[...] += jnp.sum(dyb, axis=0, keepdims=True)
        dz = _ln_bwd_math(dyb, xh, r, g_ref[...])
        dz_ref[...] = dz
        dzb_ref[...] = dz.astype(dzb_ref.dtype)

    blk = pl.BlockSpec((tm, d), lambda i: (i, 0))
    vec = pl.BlockSpec((1, d), lambda i: (0, 0))
    return pl.pallas_call(
        body, name="dx1_ln1_bwd", grid=(seq // tm,),
        in_specs=[pl.BlockSpec((tm, kdim), lambda i: (i, 0)),
                  pl.BlockSpec((kdim, d), lambda i: (0, 0), pipeline_mode=pl.Buffered(1)), blk, blk, vec] + [_ANY_SPEC] * n_after,
        out_specs=[blk, blk, vec, vec],
        out_shape=[jax.ShapeDtypeStruct((seq, d), F32), jax.ShapeDtypeStruct((seq, d), MXU_DTYPE),
                   jax.ShapeDtypeStruct((1, d), F32), jax.ShapeDtypeStruct((1, d), F32)],
        compiler_params=_params("arbitrary"))(du, w_up_t, dz2, z, g, *after)


def _down_ln2_loss_bwd(act, w_down, x1, target, g, b, tm=512):
    seq, d = x1.shape
    kdim = act.shape[1]
    tm = min(tm, seq)

    def body(a_ref, w_ref, x_ref, t_ref, g_ref, b_ref, dz_ref, dzb_ref, loss_ref, dg_ref, db_ref):
        @pl.when(pl.program_id(0) == 0)
        def _():
            loss_ref[...] = jnp.zeros_like(loss_ref)
            dg_ref[...] = jnp.zeros_like(dg_ref)
            db_ref[...] = jnp.zeros_like(db_ref)

        gv = g_ref[...]
        z = DN_ALPHA * x_ref[...] + _dot(a_ref[...], w_ref[...], 1, 0)
        xh, r = _ln_stats(z)
        err = (xh * gv + b_ref[...]) - t_ref[...]
        loss_ref[...] += 0.5 * jnp.sum(jnp.mean(err * err, axis=-1, keepdims=True), axis=0, keepdims=True)
        dy = err * (1.0 / d)
        dg_ref[...] += jnp.sum(dy * xh, axis=0, keepdims=True)
        db_ref[...] += jnp.sum(dy, axis=0, keepdims=True)
        dz = _ln_bwd_math(dy, xh, r, gv)
        dz_ref[...] = dz
        dzb_ref[...] = dz.astype(dzb_ref.dtype)

    blk = pl.BlockSpec((tm, d), lambda i: (i, 0))
    vec = pl.BlockSpec((1, d), lambda i: (0, 0))
    return pl.pallas_call(
        body, name="down_ln2_loss_bwd", grid=(seq // tm,),
        in_specs=[pl.BlockSpec((tm, kdim), lambda i: (i, 0)),
                  pl.BlockSpec((kdim, d), lambda i: (0, 0), pipeline_mode=pl.Buffered(1)), blk, blk, vec, vec],
        out_specs=[blk, blk, pl.BlockSpec((1, LANES), lambda i: (0, 0)), vec, vec],
        out_shape=[jax.ShapeDtypeStruct((seq, d), F32), jax.ShapeDtypeStruct((seq, d), MXU_DTYPE),
                   jax.ShapeDtypeStruct((1, LANES), F32),
                   jax.ShapeDtypeStruct((1, d), F32), jax.ShapeDtypeStruct((1, d), F32)],
        compiler_params=_params("arbitrary"))(act, w_down, x1, target, g, b)


HALO = 16


def _conv_rows(e, w_ref, b_ref):
    y = b_ref[...] + w_ref[0:1, :] * pltpu.roll(e, 2, 0)
    y = y + w_ref[1:2, :] * pltpu.roll(e, 1, 0)
    return y + w_ref[2:3, :] * e


_GELU_C = math.sqrt(2.0 / math.pi)
_GELU_A = 0.044715


def _gelu(x):
    return 0.5 * x * (1.0 + jnp.tanh(_GELU_C * (x + _GELU_A * (x * x * x))))


CONV_TN = 256


def _ffn_interleave(a, axis):
    shp = a.shape
    a = a.reshape(shp[:axis] + (2, D_FF // CONV_TN, CONV_TN) + shp[axis + 1:])
    return jnp.swapaxes(a, axis, axis + 1).reshape(shp)


def _ffn_deinterleave(a, axis):
    shp = a.shape
    a = a.reshape(shp[:axis] + (D_FF // CONV_TN, 2, CONV_TN) + shp[axis + 1:])
    return jnp.swapaxes(a, axis, axis + 1).reshape(shp)


def _up_conv_gate_fwd(x1, w_up_t, conv_w, conv_b, tm=1024):
    seq, d = x1.shape
    tm = min(tm, seq)
    tn = CONV_TN

    def body(x_ref, xp_ref, wu_ref, w_ref, b_ref, u_ref, o_ref):
        first = pl.program_id(0) == 0
        halo = jnp.where(first, jnp.zeros_like(xp_ref), xp_ref[...])
        e = _dot(jnp.concatenate([halo, x_ref[...]], axis=0), wu_ref[...], 1, 1)
        u_ref[...] = e[HALO:]
        y = _conv_rows(e, w_ref, b_ref)[HALO:]
        o_ref[...] = (_gelu(y[:, tn:]) * y[:, :tn]).astype(o_ref.dtype)

    hb = tm // HALO
    return pl.pallas_call(
        body, name="up_conv_gate_fwd", grid=(seq // tm, D_FF // tn),
        in_specs=[pl.BlockSpec((tm, d), lambda i, j: (i, 0)),
                  pl.BlockSpec((HALO, d), lambda i, j: (jnp.maximum(i * hb - 1, 0), 0)),
                  pl.BlockSpec((2 * tn, d), lambda i, j: (j, 0)),
                  pl.BlockSpec((3, 2 * tn), lambda i, j: (0, j)), pl.BlockSpec((1, 2 * tn), lambda i, j: (0, j))],
        out_specs=[pl.BlockSpec((tm, 2 * tn), lambda i, j: (i, j)), pl.BlockSpec((tm, tn), lambda i, j: (i, j))],
        out_shape=[jax.ShapeDtypeStruct((seq, 2 * D_FF), F32), jax.ShapeDtypeStruct((seq, D_FF), MXU_DTYPE)],
        compiler_params=_params("parallel", "arbitrary"),
    )(x1, x1, w_up_t, conv_w, conv_b)


def _conv_gate_bwd(u, d_act, conv_w, conv_b, tm=1024):
    seq = u.shape[0]
    tm = min(tm, seq)
    tn = CONV_TN
    ni = seq // tm
    rows_e = tm + 2 * HALO

    def body(u_ref, up_ref, un_ref, da_ref, dan_ref, w_ref, b_ref, du_ref, dw_ref, db_ref):
        i = pl.program_id(1)
        first, last = i == 0, i == ni - 1

        @pl.when(i == 0)
        def _():
            dw_ref[...] = jnp.zeros_like(dw_ref)
            db_ref[...] = jnp.zeros_like(db_ref)

        e = jnp.concatenate([jnp.where(first, 0.0, up_ref[...]), u_ref[...], jnp.where(last, 0.0, un_ref[...])], axis=0)
        y = _conv_rows(e, w_ref, b_ref)
        ya, yg = y[:, :tn], y[:, tn:]
        dact = jnp.concatenate([jnp.zeros((HALO, tn), F32), da_ref[...].astype(F32),
                                jnp.where(last, 0.0, dan_ref[...].astype(F32))], axis=0)
        th = jnp.tanh(_GELU_C * (yg + _GELU_A * (yg * yg * yg)))
        gelu = 0.5 * yg * (1.0 + th)
        gelu_grad = 0.5 * (1.0 + th) + 0.5 * yg * (1.0 - th * th) * (_GELU_C * (1.0 + 3.0 * _GELU_A * (yg * yg)))
        dy = jnp.concatenate([dact * gelu, dact * ya * gelu_grad], axis=1)
        du = w_ref[2:3, :] * dy + w_ref[1:2, :] * pltpu.roll(dy, rows_e - 1, 0) + w_ref[0:1, :] * pltpu.roll(dy, rows_e - 2, 0)
        du_ref[...] = du[HALO:HALO + tm].astype(du_ref.dtype)
        dyt = dy[HALO:HALO + tm]
        dw_ref[0:1, :] += jnp.sum(dyt * pltpu.roll(e, 2, 0)[HALO:HALO + tm], axis=0, keepdims=True)
        dw_ref[1:2, :] += jnp.sum(dyt * pltpu.roll(e, 1, 0)[HALO:HALO + tm], axis=0, keepdims=True)
        dw_ref[2:3, :] += jnp.sum(dyt * e[HALO:HALO + tm], axis=0, keepdims=True)
        db_ref[...] += jnp.sum(dyt, axis=0, keepdims=True)

    hb = tm // HALO
    nh = seq // HALO
    prev = lambda j, i: (jnp.maximum(i * hb - 1, 0), j)
    nxt = lambda j, i: (jnp.minimum((i + 1) * hb, nh - 1), j)
    return pl.pallas_call(
        body, name="conv_gate_bwd", grid=(D_FF // tn, ni),
        in_specs=[pl.BlockSpec((tm, 2 * tn), lambda j, i: (i, j)), pl.BlockSpec((HALO, 2 * tn), prev),
                  pl.BlockSpec((HALO, 2 * tn), nxt), pl.BlockSpec((tm, tn), lambda j, i: (i, j)), pl.BlockSpec((HALO, tn), nxt),
                  pl.BlockSpec((3, 2 * tn), lambda j, i: (0, j)), pl.BlockSpec((1, 2 * tn), lambda j, i: (0, j))],
        out_specs=[pl.BlockSpec((tm, 2 * tn), lambda j, i: (i, j)), pl.BlockSpec((3, 2 * tn), lambda j, i: (0, j)),
                   pl.BlockSpec((1, 2 * tn), lambda j, i: (0, j))],
        out_shape=[jax.ShapeDtypeStruct((seq, 2 * D_FF), MXU_DTYPE), jax.ShapeDtypeStruct((3, 2 * D_FF), F32),
                   jax.ShapeDtypeStruct((1, 2 * D_FF), F32)],
        compiler_params=_params("parallel", "arbitrary"),
    )(u, u, u, d_act, d_act, conv_w, conv_b)


def _pad_heads(w, width):
    w = jnp.transpose(w, (1, 0, 2))
    return jnp.pad(w, ((0, 0), (0, 0), (0, LANES - width))).astype(MXU_DTYPE)


def _heads_major(a):
    return jnp.transpose(a, (1, 0, 2)).reshape(-1, a.shape[2])


def _heads_minor(a, heads):
    return jnp.transpose(a.reshape(heads, -1, a.shape[1]), (1, 0, 2))


_LATENT = Q_RANK + KV_RANK
_ROPE_AT = _LATENT + NOPE_DIM
_ROPE_END = _ROPE_AT + ROPE_DIM


def _split_pad_rows(w_t):
    z = lambda n: jnp.zeros((n, w_t.shape[1]), w_t.dtype)
    return jnp.concatenate([w_t[:_LATENT], z(_ROPE_AT - _LATENT), w_t[_LATENT:_LATENT + ROPE_DIM], z(DH_PART - _ROPE_END),
                            w_t[_LATENT + ROPE_DIM:]], axis=0)


def _split_unpad_rows(w_p):
    return jnp.concatenate([w_p[:_LATENT], w_p[_ROPE_AT:_ROPE_END], w_p[DH_PART:]], axis=0)


def _pad_w_o(w_o):
    mla = jnp.pad(w_o[:MLA_WIDTH].reshape(HEADS, HEAD_DIM, D_MODEL), ((0, 0), (0, LANES - HEAD_DIM), (0, 0)))
    return mla.reshape(HEADS * LANES, D_MODEL).astype(MXU_DTYPE), w_o[MLA_WIDTH:].astype(MXU_DTYPE)


def _unpad_w_o(d_mla, d_dil):
    return jnp.concatenate([d_mla.reshape(HEADS, LANES, D_MODEL)[:, :HEAD_DIM].reshape(MLA_WIDTH, D_MODEL), d_dil], axis=0)


def _row(v):
    return v.reshape(1, -1).astype(F32)


def _layer_grads(x0, target, cw, first_after=(), late_weights=None, on_grads=None):
    seq = x0.shape[0]
    ctab, stab = _rope_tables(seq)
    gq, gk = cw["g_cq"], cw["g_ckv"]
    wq, wk, wv = cw["wq"], cw["wk"], cw["wv"]
    notify = (lambda stage, grads: ()) if on_grads is None else on_grads

    h = _mm(x0, cw["w_in_t"], name="mm_h", tb=True, tm=1024, tn=IN_PAD, tk=1024, after=first_after)
    qf, kf, vp = _mla_prep(h, gq, gk, wq, wk, wv, ctab, stab)
    o_mla, o_mla_b, lse_mla = _mla_attn_fwd(qf, kf, vp)
    o_dil, o_dil_b, lse_dil = _dil_fwd(h)
    fetch = (lambda stage, after: {}) if late_weights is None else late_weights
    cw = {**cw, **fetch("w_o", o_mla_b)}
    cb = cw["conv_b"]
    z1, x1, x1b = _mix_ln1(o_mla_b, o_dil_b, cw["w_o_mla"], cw["w_o_dil"], x0, cw["ln1_g"], cw["ln1_b"])
    cw = {**cw, **fetch("w_up", x1b)}
    u, act = _up_conv_gate_fwd(x1b, cw["w_up_t"], cw["conv_w"], cb)
    cw = {**cw, **fetch("w_down", act)}
    dz2, dz2b, loss, d_ln2_g, d_ln2_b = _down_ln2_loss_bwd(act, cw["w_down"], x1, target, cw["ln2_g"], cw["ln2_b"])

    d_act = _mm(dz2b, cw["w_down"], name="mm_d_act", tb=True, out_dtype=MXU_DTYPE, tm=1024, tn=D_FF, tk=1024)
    d_w_down = _mm(act, dz2b, name="mm_dw_down", ta=True, out_dtype=MXU_DTYPE, tm=1408, tn=1024, tk=1024)
    du, d_conv_w, d_conv_b = _conv_gate_bwd(u, d_act, cw["conv_w"], cb)
    d_w_up_t = _mm(du, x1b, name="mm_dw_up", ta=True, out_dtype=MXU_DTYPE, tm=1408, tn=1024, tk=2048)
    grads = dict(w_up_t=d_w_up_t, w_down=d_w_down, conv_w=d_conv_w, conv_b=d_conv_b, ln2_g=d_ln2_g, ln2_b=d_ln2_b)
    dz1, dz1b, d_ln1_g, d_ln1_b = _dx1_ln1_bwd(du, cw["w_up_t"], dz2, z1, cw["ln1_g"], after=notify("ffn", grads))
    do_mla, do_dil, d_w_o_mla, d_w_o_dil = _w_o_bwd(dz1b, o_mla_b, o_dil_b, cw["w_o_mla"], cw["w_o_dil"])
    grads.update(w_o_mla=d_w_o_mla, w_o_dil=d_w_o_dil, ln1_g=d_ln1_g, ln1_b=d_ln1_b)
    dqf, dkf, dvf = _mla_attn_bwd(qf, kf, vp, o_mla, lse_mla, do_mla)
    dh_mla, d_wq, d_wk, d_wv, d_gq, d_gk = _mla_prep_bwd(h, gq, gk, wq, wk, wv, ctab, stab, dqf, dkf, dvf,
                                                          after=notify("w_o", grads))
    grads.update(wq=d_wq, wk=d_wk, wv=d_wv, g_cq=d_gq, g_ckv=d_gk, loss=loss)
    dq_dil, dk_dil, dv_dil = _dil_bwd(h, o_dil, lse_dil, do_dil, after=notify("mla", grads))
    dh = (dh_mla, dq_dil, dk_dil, dv_dil)
    grads.update(w_in_t=_mm_dw_in(dh, x0))
    grad_x = _mm_dx0(dh, cw["w_in_t"], dz1, after=notify("w_in", grads))
    return loss, grad_x, grads


def _all_gather(blocks, name):
    na = len(blocks)

    def body(*refs):
        ins, outs = refs[:na], refs[na:2 * na]
        send_sems, recv_sems, local_sems = refs[2 * na:]
        x, y, c = lax.axis_index("x"), lax.axis_index("y"), lax.axis_index("c")
        me, sibling = (x, y, c), (x, y, 1 - c)
        chips = [(1 - x, y), (x, 1 - y), (1 - x, 1 - y)]

        def slot(out, pos):
            return out.at[4 * pos[0] + 2 * pos[1] + pos[2]]

        def copy(a, k, block, to, src=None):
            return pltpu.make_async_remote_copy(
                src_ref=slot(outs[a], block) if src is None else src, dst_ref=slot(outs[a], block),
                send_sem=send_sems.at[7 * a + k], recv_sem=recv_sems.at[7 * a + k],
                device_id=to, device_id_type=pl.DeviceIdType.MESH)

        mine = [pltpu.make_async_copy(ins[a], slot(outs[a], me), local_sems.at[a]) for a in range(na)]
        for cp in mine:
            cp.start()
        first = []
        for a in range(na):
            first.append(copy(a, 0, me, sibling, src=ins[a]))
            first += [copy(a, 1 + j, me, (*chip, c), src=ins[a]) for j, chip in enumerate(chips)]
        for cp in first:
            cp.start()
        passed = []
        for j, chip in enumerate(chips):
            for a in range(na):
                copy(a, 1 + j, (*chip, c), me).wait_recv()
                cp = copy(a, 4 + j, (*chip, c), sibling)
                cp.start()
                passed.append(cp)
        for a in range(na):
            copy(a, 0, sibling, me).wait_recv()
            for j, chip in enumerate(chips):
                copy(a, 4 + j, (*chip, 1 - c), me).wait_recv()
        for cp in first + passed:
            cp.wait_send()
        for cp in mine:
            cp.wait()

    any_spec = pl.BlockSpec(memory_space=pl.ANY)
    return pl.pallas_call(
        body, name=name, in_specs=[any_spec] * na, out_specs=[any_spec] * na,
        out_shape=[jax.ShapeDtypeStruct((N_DEV,) + b.shape, b.dtype) for b in blocks],
        scratch_shapes=[pltpu.SemaphoreType.DMA((7 * na,)), pltpu.SemaphoreType.DMA((7 * na,)), pltpu.SemaphoreType.DMA((na,))],
    )(*blocks)


_HBM_SPEC = pl.BlockSpec(memory_space=pltpu.HBM)
_SEM_SPEC = pl.BlockSpec(memory_space=pltpu.SEMAPHORE)
_DATAFLOW = pltpu.CompilerParams(has_side_effects=pltpu.SideEffectType.DATAFLOW_SIDE_EFFECTING)


def _split_copies(which, ins, lands, send_sems, recv_sems, gather):
    x, y, c = lax.axis_index("x"), lax.axis_index("y"), lax.axis_index("c")
    me = 4 * x + 2 * y + c
    copies = []
    for a, src, land in zip(which, ins, lands):
        for d in range(1, N_DEV):
            px, py, pc = x ^ (d >> 2), y ^ ((d >> 1) & 1), c ^ (d & 1)
            copies.append(pltpu.make_async_remote_copy(
                src_ref=src if gather[a] else src.at[4 * px + 2 * py + pc], dst_ref=land.at[me],
                send_sem=send_sems.at[7 * a + d - 1], recv_sem=recv_sems.at[7 * a + d - 1],
                device_id=(px, py, pc), device_id_type=pl.DeviceIdType.MESH))
    return copies


def _send_start(srcs, gather, name):
    na = len(srcs)
    assert len(gather) == na
    land_types = [pltpu.HBM(((N_DEV,) + s.shape) if g else s.shape, s.dtype) for s, g in zip(srcs, gather)]

    def body(*refs):
        ins, lands = refs[:na], refs[na:2 * na]
        send_sems, recv_sems, token = refs[2 * na], refs[2 * na + 1], refs[-1]
        for cp in _split_copies(range(na), ins, lands, send_sems, recv_sems, gather):
            cp.start()
        token[...] = jnp.zeros_like(token)

    hbm = lambda a: pltpu.with_memory_space_constraint(a, pltpu.HBM)
    outs = pl.pallas_call(
        body, name=name,
        out_shape=(pltpu.SemaphoreType.DMA((7 * na,)), pltpu.SemaphoreType.DMA((7 * na,)),
                   *[pltpu.HBM(s.shape, s.dtype) for s in srcs], *land_types, jax.ShapeDtypeStruct((8, LANES), F32)),
        in_specs=[_HBM_SPEC] * (2 * na),
        out_specs=(_SEM_SPEC, _SEM_SPEC, *[_HBM_SPEC] * (2 * na), pl.BlockSpec(memory_space=pltpu.VMEM)),
        input_output_aliases={i: 2 + i for i in range(2 * na)}, compiler_params=_DATAFLOW,
    )(*[hbm(s) for s in srcs], *[hbm(lax.empty(t.shape, t.dtype)) for t in land_types])
    return dict(send=outs[0], recv=outs[1], srcs=list(outs[2:2 + na]), lands=list(outs[2 + na:2 + 2 * na]), token=outs[-1],
                gather=gather)


def _send_wait(handle, after, name, only=None):
    which = list(range(len(handle["srcs"]))) if only is None else list(only)
    na = len(which)
    gather = handle["gather"]
    after = list(after)

    def body(*refs):
        ins, lands = refs[:na], refs[na:2 * na]
        send_sems, recv_sems = refs[2 * na], refs[2 * na + 1]
        for cp in _split_copies(which, ins, lands, send_sems, recv_sems, gather):
            cp.wait_send()
            cp.wait_recv()

    both = [handle["srcs"][a] for a in which] + [handle["lands"][a] for a in which]
    outs = pl.pallas_call(
        body, name=name, out_shape=[pltpu.HBM(a.shape, a.dtype) for a in both],
        in_specs=[_HBM_SPEC] * (2 * na) + [_SEM_SPEC, _SEM_SPEC] + [_ANY_SPEC] * len(after),
        out_specs=[_HBM_SPEC] * (2 * na), input_output_aliases={i: i for i in range(2 * na)}, compiler_params=_DATAFLOW,
    )(*both, handle["send"], handle["recv"], *after)
    return list(outs[:na]), list(outs[na:])


def _sum_slots(p_ref):
    g = p_ref[0].astype(F32)
    for s in range(1, p_ref.shape[0]):
        g = g + p_ref[s].astype(F32)
    return g


def _adamw_refs(g, w_ref, m_ref, v_ref, g_out, d_out, m_out, v_out):
    c1 = 1.0 - ADAM_B1 ** ADAM_STEP
    c2 = 1.0 - ADAM_B2 ** ADAM_STEP
    m_new = ADAM_B1 * m_ref[...] + (1.0 - ADAM_B1) * g
    v_new = ADAM_B2 * v_ref[...] + (1.0 - ADAM_B2) * (g * g)
    g_out[...] = g
    m_out[...] = m_new
    v_out[...] = v_new
    d_out[...] = -ADAM_LR * ((m_new / c1) / (jnp.sqrt(v_new / c2) + ADAM_EPS) + ADAM_WD * w_ref[...])


def _adamw(parts, w, m, v, name):
    npart, r, n = parts.shape
    tr = r if r <= 256 else max(t for t in range(16, 257, 16) if r % t == 0)

    def body(p_ref, w_ref, m_ref, v_ref, g_out, d_out, m_out, v_out):
        _adamw_refs(_sum_slots(p_ref), w_ref, m_ref, v_ref, g_out, d_out, m_out, v_out)

    blk = pl.BlockSpec((tr, n), lambda i: (i, 0))
    shp = jax.ShapeDtypeStruct((r, n), F32)
    return pl.pallas_call(
        body, name=name, grid=(r // tr,), in_specs=[pl.BlockSpec((npart, tr, n), lambda i: (0, i, 0)), blk, blk, blk],
        out_specs=[blk] * 4, out_shape=[shp] * 4, compiler_params=_params("parallel"),
    )(parts, w, m, v)


def _adamw_small(parts, ws, ms, vs, loss_parts, name):
    n = len(parts)

    def body(*refs):
        ins, outs = refs[:4 * n + 1], refs[4 * n + 1:]
        for i in range(n):
            _adamw_refs(_sum_slots(ins[i]), ins[n + i], ins[2 * n + i], ins[3 * n + i], *outs[4 * i:4 * i + 4])
        outs[4 * n][...] = _sum_slots(ins[4 * n])

    out_shape = [jax.ShapeDtypeStruct(w.shape, F32) for w in ws for _ in range(4)]
    res = pl.pallas_call(body, name=name, out_shape=out_shape + [jax.ShapeDtypeStruct((1, LANES), F32)],
                         compiler_params=_params())(*parts, *ws, *ms, *vs, loss_parts)
    return [res[4 * i:4 * i + 4] for i in range(n)], res[4 * n]


REPLICATED = ("g_cq", "g_ckv", "w_uk", "w_uv", "ln1_g", "ln1_b", "conv_b", "ln2_g", "ln2_b")
ALL_WEIGHTS = ("w_in", "g_cq", "g_ckv", "w_uq", "w_uk", "w_uv", "w_o", "ln1_g", "ln1_b", "w_up", "conv_w", "conv_b",
               "w_down", "ln2_g", "ln2_b")


def kernel(x, w_in, g_cq, g_ckv, w_uq, w_uk, w_uv, w_o, ln1_g, ln1_b, w_up, conv_w, conv_b, w_down, ln2_g, ln2_b, loss_target, m_w_in, m_g_cq, m_g_ckv, m_w_uq, m_w_uk, m_w_uv, m_w_o, m_ln1_g, m_ln1_b, m_w_up, m_conv_w, m_conv_b, m_w_down, m_ln2_g, m_ln2_b, v_w_in, v_g_cq, v_g_ckv, v_w_uq, v_w_uk, v_w_uv, v_w_o, v_ln1_g, v_ln1_b, v_w_up, v_conv_w, v_conv_b, v_w_down, v_ln2_g, v_ln2_b):
    w = dict(w_in=w_in, g_cq=g_cq, g_ckv=g_ckv, w_uq=w_uq, w_uk=w_uk, w_uv=w_uv, w_o=w_o, ln1_g=ln1_g, ln1_b=ln1_b,
             w_up=w_up, conv_w=conv_w, conv_b=conv_b, w_down=w_down, ln2_g=ln2_g, ln2_b=ln2_b)
    m = dict(w_in=m_w_in, g_cq=m_g_cq, g_ckv=m_g_ckv, w_uq=m_w_uq, w_uk=m_w_uk, w_uv=m_w_uv, w_o=m_w_o, ln1_g=m_ln1_g,
             ln1_b=m_ln1_b, w_up=m_w_up, conv_w=m_conv_w, conv_b=m_conv_b, w_down=m_w_down, ln2_g=m_ln2_g, ln2_b=m_ln2_b)
    v = dict(w_in=v_w_in, g_cq=v_g_cq, g_ckv=v_g_ckv, w_uq=v_w_uq, w_uk=v_w_uk, w_uv=v_w_uv, w_o=v_w_o, ln1_g=v_ln1_g,
             ln1_b=v_ln1_b, w_up=v_w_up, conv_w=v_conv_w, conv_b=v_conv_b, w_down=v_w_down, ln2_g=v_ln2_g, ln2_b=v_ln2_b)
    me = 4 * lax.axis_index("x") + 2 * lax.axis_index("y") + lax.axis_index("c")
    wire = lambda a: a.astype(WIRE_DTYPE)
    pad_taps = lambda a: jnp.pad(a, ((0, 8 - a.shape[0]), (0, 0)))

    own_slot = lambda buf, block: lax.dynamic_update_index_in_dim(buf, block, me, 0)
    blocks = lambda a: wire(a).reshape((N_DEV, a.shape[0] // N_DEV) + a.shape[1:])

    g_in, g_uq, g_conv = _all_gather(
        [wire(w_in).T, _heads_major(wire(w_uq)), pad_taps(conv_w)],
        "gather_weights")
    late = _send_start([wire(w_o), wire(w_up).T, wire(w_down)], [True] * 3, "gather_late_start")
    r_uq_dev, e_uq = w_uq.shape[0], w_uq.shape[2]
    wq = jnp.transpose(g_uq.reshape(N_DEV, HEADS, r_uq_dev, e_uq), (1, 0, 2, 3)).reshape(HEADS, Q_RANK, e_uq)
    cw = dict(
        w_in_t=_split_pad_rows(g_in.reshape(-1, D_MODEL)).astype(MXU_DTYPE),
        wq=jnp.pad(wq, ((0, 0), (0, 0), (0, LANES - e_uq))).astype(MXU_DTYPE),
        wk=_pad_heads(w_uk, NOPE_DIM), wv=_pad_heads(w_uv, HEAD_DIM),
        conv_w=_ffn_interleave(jnp.transpose(g_conv[:, :conv_w.shape[0]], (1, 0, 2)).reshape(conv_w.shape[0], -1), 1),
        g_cq=_row(g_cq), g_ckv=_row(g_ckv), ln1_g=_row(ln1_g), ln1_b=_row(ln1_b), conv_b=_ffn_interleave(_row(conv_b), 1),
        ln2_g=_row(ln2_g), ln2_b=_row(ln2_b))

    def late_weights(stage, after):
        (own,), (got,) = _send_wait(late, [after], f"gather_{stage}_wait", only=[("w_o", "w_up", "w_down").index(stage)])
        full = own_slot(got, own).reshape(-1, D_MODEL)
        if stage == "w_o":
            w_o_mla, w_o_dil = _pad_w_o(full)
            return dict(w_o_mla=w_o_mla, w_o_dil=w_o_dil)
        if stage == "w_up":
            return dict(w_up_t=_ffn_interleave(full, 0).astype(MXU_DTYPE))
        return dict(w_down=full.astype(MXU_DTYPE))

    sent = {}

    def on_grads(stage, g):
        if stage == "ffn":
            sent[stage] = _send_start([blocks(_ffn_deinterleave(g["w_up_t"], 0)), blocks(g["w_down"])], [False] * 2,
                                      "exchange_ffn_start")
        elif stage == "w_o":
            return []
        elif stage == "mla":
            d_uq = wire(jnp.transpose(g["wq"][:, :, :e_uq].reshape(HEADS, N_DEV, r_uq_dev, e_uq), (1, 0, 2, 3))
                        ).reshape(N_DEV, HEADS * r_uq_dev, e_uq)
            dense = lambda a, width: wire(a[:, :, :width]).reshape(-1, LANES)
            small = dict(g_cq=g["g_cq"], g_ckv=g["g_ckv"], w_uk=dense(g["wk"], NOPE_DIM), w_uv=dense(g["wv"], HEAD_DIM),
                         ln1_g=g["ln1_g"], ln1_b=g["ln1_b"], conv_b=_ffn_deinterleave(g["conv_b"], 1), ln2_g=g["ln2_g"],
                         ln2_b=g["ln2_b"])
            everyone = [small[n] for n in REPLICATED] + [_ffn_deinterleave(g["conv_w"], 1), g["loss"]]
            sent[stage] = _send_start([blocks(_unpad_w_o(g["w_o_mla"], g["w_o_dil"])), d_uq] + everyone,
                                      [False] * 2 + [True] * len(everyone), "exchange_mla_start")
        else:
            sent[stage] = _send_start([blocks(_split_unpad_rows(g["w_in_t"]))], [False], "exchange_w_in_start")
        return [sent[stage]["token"]]

    _, grad_x, _ = _layer_grads(x[0], loss_target[0], cw, [late["token"]], late_weights, on_grads)

    def landed(handle, after, name):
        own, got = _send_wait(handle, after, name)
        pick = lambda src, whole: src if whole else lax.dynamic_index_in_dim(src, me, 0, keepdims=False)
        return [own_slot(buf, pick(src, whole)) for buf, src, whole in zip(got, own, handle["gather"])]

    out = {}

    def update(name, parts, view=None):
        to2d = {None: lambda a: a, "t": lambda a: a.T, "heads": _heads_major}[view]
        back = {None: lambda a: a, "t": lambda a: a.T, "heads": lambda a: _heads_minor(a, HEADS)}[view]
        res = _adamw(parts, to2d(w[name]), to2d(m[name]), to2d(v[name]), "adamw_" + name)
        for kind, a in zip(("grad", "delta", "new_m", "new_v"), res):
            out[kind, name] = back(a)
        return res[0]

    r_up, r_down = landed(sent["ffn"], [grad_x], "exchange_ffn_wait")
    r_o, r_uq, *rep_all, cw_all, loss_all = landed(sent["mla"], [grad_x], "exchange_mla_wait")
    done = [update("w_up", r_up, "t"), update("w_down", r_down), update("w_o", r_o), update("w_uq", r_uq, "heads")]
    heads_major = ("w_uk", "w_uv")
    two_d = lambda n, a: _heads_major(a) if n in heads_major else a.reshape(1, -1)
    rep_all = [p.reshape(N_DEV, -1, w[n].shape[2]) if n in heads_major else p for n, p in zip(REPLICATED, rep_all)]
    res, loss_sum = _adamw_small(rep_all, *[[two_d(n, d[n]) for n in REPLICATED] for d in (w, m, v)], loss_all, "adamw_replicated")
    for n, quad in zip(REPLICATED, res):
        for kind, a in zip(("grad", "delta", "new_m", "new_v"), quad):
            out[kind, n] = _heads_minor(a, HEADS) if n in heads_major else a.reshape(w[n].shape)
    loss = loss_sum[0, 0]
    ncw = conv_w.shape[1]
    done += [loss_sum, update("conv_w", lax.dynamic_slice_in_dim(cw_all[:, :conv_w.shape[0]], me * ncw, ncw, axis=2))]
    (r_in,) = landed(sent["w_in"], done, "exchange_w_in_wait")
    update("w_in", r_in, "t")

    return (loss, grad_x[None], *[out[kind, n] for kind in ("grad", "delta", "new_m", "new_v") for n in ALL_WEIGHTS])
```

```python
import math

import jax
import jax.numpy as jnp
import numpy as np
from jax import lax
from jax.experimental import pallas as pl
from jax.experimental.pallas import tpu as pltpu

F32 = jnp.float32
MXU_DTYPE = jnp.bfloat16
WIRE_DTYPE = jnp.bfloat16

N_DEV = 8
D_MODEL = 1024
HEADS = 8
HEAD_DIM = 64
LANES = 128
Q_RANK, KV_RANK, ROPE_DIM, NOPE_DIM = 256, 128, 32, 64
DIL_WIDTH = HEADS * HEAD_DIM
MLA_WIDTH = HEADS * HEAD_DIM
IN_PAD = 2048
DH_PART = IN_PAD // 4
D_FF = 2816
ROPE_THETA = 10000.0
DIL_PAIRS = ((128, 1), (512, 4), (2048, 16))
DIL_BLOCK = 128
DN_ALPHA = 2.0 ** 0.25
LN_EPS = 1e-5
RMS_EPS = 1e-6
ONES_LANE = HEAD_DIM
MLA_SCALE = 1.0 / math.sqrt(NOPE_DIM + ROPE_DIM)
MLA_SCALE_LOG2 = MLA_SCALE * math.log2(math.e)
MLA_BWD_SPLITS = 2
MLA_BWD_SPLITS_DIAGONAL = 4
MLA_FWD_SPLITS_DIAGONAL = 2
DIL_SCALE = 1.0 / math.sqrt(HEAD_DIM)
ALIBI_SLOPES = tuple(2.0 ** (-8.0 * (h + 1) / HEADS) for h in range(HEADS))
NEG_BIG = -1e30
ADAM_LR, ADAM_B1, ADAM_B2, ADAM_EPS, ADAM_WD, ADAM_STEP = 0.001, 0.9, 0.999, 1e-08, 0.01, 10
VMEM_LIMIT = 48 * 1024 * 1024


def _params(*sem):
    return pltpu.CompilerParams(dimension_semantics=sem or None, vmem_limit_bytes=VMEM_LIMIT)


def _dot(a, b, ca, cb):
    return lax.dot_general(a, b, (((ca,), (cb,)), ((), ())), preferred_element_type=F32)


_ANY_SPEC = pl.BlockSpec(memory_space=pl.ANY)


def _mm(a, b, *, name, tm, tn, tk, ta=False, tb=False, out_dtype=F32, after=()):
    m, k = (a.shape[1], a.shape[0]) if ta else a.shape
    n = b.shape[0] if tb else b.shape[1]
    assert (b.shape[1] if tb else b.shape[0]) == k
    tm, tn, tk = min(tm, m), min(tn, n), min(tk, k)
    assert m % tm == 0 and n % tn == 0 and k % tk == 0, (name, m, n, k, tm, tn, tk)
    nk = k // tk
    a_spec = (pl.BlockSpec((tk, tm), lambda i, j, kk: (kk, i)) if ta
              else pl.BlockSpec((tm, tk), lambda i, j, kk: (i, kk)))
    b_mode = dict(pipeline_mode=pl.Buffered(1)) if (tn == n and tk == k) else {}
    b_spec = (pl.BlockSpec((tn, tk), lambda i, j, kk: (j, kk), **b_mode) if tb
              else pl.BlockSpec((tk, tn), lambda i, j, kk: (kk, j), **b_mode))
    o_spec = pl.BlockSpec((tm, tn), lambda i, j, kk: (i, j))
    n_in = 2 + len(after)
    ca, cb = (0 if ta else 1), (1 if tb else 0)

    def body(*refs):
        a_ref, b_ref, o_ref = refs[0], refs[1], refs[n_in]
        part = _dot(a_ref[...].astype(MXU_DTYPE), b_ref[...].astype(MXU_DTYPE), ca, cb)
        if nk == 1:
            o_ref[...] = part.astype(o_ref.dtype)
            return
        acc_ref = refs[-1]
        kk = pl.program_id(2)

        @pl.when(kk == 0)
        def _():
            acc_ref[...] = part

        @pl.when(kk > 0)
        def _():
            acc_ref[...] += part

        @pl.when(kk == nk - 1)
        def _():
            o_ref[...] = acc_ref[...].astype(o_ref.dtype)

    return pl.pallas_call(
        body, name=name, grid=(m // tm, n // tn, nk), in_specs=[a_spec, b_spec] + [_ANY_SPEC] * len(after), out_specs=o_spec,
        out_shape=jax.ShapeDtypeStruct((m, n), out_dtype),
        scratch_shapes=[pltpu.VMEM((tm, tn), F32)] if nk > 1 else [],
        compiler_params=_params("parallel", "parallel", "arbitrary"),
    )(a, b, *after)


def _w_o_bwd(dz, o_mla, o_dil, w_o_mla, w_o_dil, tm=1024):
    seq, d = dz.shape
    tm = min(tm, seq)
    nstep = seq // tm

    def body(a_ref, om_ref, od_ref, wm_ref, wd_ref, dom_ref, dod_ref, dwm_ref, dwd_ref, accm_ref, accd_ref):
        step = pl.program_id(0)

        @pl.when(step == 0)
        def _():
            accm_ref[...] = jnp.zeros_like(accm_ref)
            accd_ref[...] = jnp.zeros_like(accd_ref)

        a = a_ref[...]
        for hd in range(HEADS):
            rows = slice(LANES * hd, LANES * (hd + 1))
            dom_ref[hd] = _dot(a, wm_ref[rows, :], 1, 1)
            accm_ref[rows, :] += _dot(om_ref[hd], a, 0, 0)
        dod_ref[...] = _dot(a, wd_ref[...], 1, 1)
        accd_ref[...] += _dot(od_ref[...], a, 0, 0)

        @pl.when(step == nstep - 1)
        def _():
            dwm_ref[...] = accm_ref[...].astype(dwm_ref.dtype)
            dwd_ref[...] = accd_ref[...].astype(dwd_ref.dtype)

    once = dict(pipeline_mode=pl.Buffered(1))
    return pl.pallas_call(
        body, name="w_o_bwd", grid=(nstep,),
        in_specs=[pl.BlockSpec((tm, d), lambda i: (i, 0)), pl.BlockSpec((HEADS, tm, LANES), lambda i: (0, i, 0)),
                  pl.BlockSpec((tm, DIL_WIDTH), lambda i: (i, 0)), pl.BlockSpec((HEADS * LANES, d), lambda i: (0, 0), **once),
                  pl.BlockSpec((DIL_WIDTH, d), lambda i: (0, 0), **once)],
        out_specs=[pl.BlockSpec((HEADS, tm, LANES), lambda i: (0, i, 0)), pl.BlockSpec((tm, DIL_WIDTH), lambda i: (i, 0)),
                   pl.BlockSpec((HEADS * LANES, d), lambda i: (0, 0)), pl.BlockSpec((DIL_WIDTH, d), lambda i: (0, 0))],
        out_shape=[jax.ShapeDtypeStruct((HEADS, seq, LANES), F32), jax.ShapeDtypeStruct((seq, DIL_WIDTH), F32),
                   jax.ShapeDtypeStruct((HEADS * LANES, d), MXU_DTYPE), jax.ShapeDtypeStruct((DIL_WIDTH, d), MXU_DTYPE)],
        scratch_shapes=[pltpu.VMEM((HEADS * LANES, d), F32), pltpu.VMEM((DIL_WIDTH, d), F32)],
        compiler_params=_params("arbitrary"),
    )(dz, o_mla, o_dil, w_o_mla, w_o_dil)


def _rope_tables(seq):
    half = ROPE_DIM // 2
    f32 = np.float32
    freqs = np.power(f32(ROPE_THETA), -np.arange(half, dtype=f32) / f32(half))
    ang = np.arange(seq, dtype=f32)[:, None] * freqs[None, :]
    cos, sin = np.cos(ang, dtype=f32), np.sin(ang, dtype=f32)
    one = np.ones((seq, NOPE_DIM), f32)
    tail = np.ones((seq, LANES - NOPE_DIM - ROPE_DIM), f32)
    ctab = np.concatenate([one, cos, cos, tail], axis=1)
    stab = np.concatenate([0 * one, -sin, sin, 0 * tail], axis=1)
    return jnp.asarray(ctab), jnp.asarray(stab)


def _rope_swap(t):
    lane = lax.broadcasted_iota(jnp.int32, t.shape, 1)
    half = ROPE_DIM // 2
    return jnp.where(lane < NOPE_DIM + half, pltpu.roll(t, LANES - half, 1), pltpu.roll(t, half, 1))


def _rope(t, ctab, stab):
    return t * ctab + _rope_swap(t) * stab


def _rope_inv(t, ctab, stab):
    return t * ctab - _rope_swap(t) * stab


def _rms(x, g):
    r = lax.rsqrt(jnp.mean(x * x, axis=-1, keepdims=True) + RMS_EPS)
    xh = x * r
    return xh, r, xh * g


def _mla_prep(h, g_cq, g_ckv, wq, wk, wv, ctab, stab, tm=512):
    seq = h.shape[0]
    tm = min(tm, seq)

    def body(h_ref, gq_ref, gk_ref, wq_ref, wk_ref, wv_ref, c_ref, s_ref, q_out, k_out, v_out):
        hb = h_ref[...]
        ctab_, stab_ = c_ref[...], s_ref[...]
        _, _, cqn = _rms(hb[:, :Q_RANK], gq_ref[...])
        _, _, ckn = _rms(hb[:, Q_RANK:Q_RANK + KV_RANK], gk_ref[...])
        cqn = cqn.astype(MXU_DTYPE)
        ckn = ckn.astype(MXU_DTYPE)
        krr = _rope(hb[:, Q_RANK + KV_RANK:], ctab_, stab_)
        ones_lane = (lax.broadcasted_iota(jnp.int32, (1, LANES), 1) == ONES_LANE).astype(F32)
        for hd in range(HEADS):
            q = _dot(cqn, wq_ref[hd], 1, 0)
            q_out[hd] = _rope(q, ctab_, stab_).astype(q_out.dtype)
            k_out[hd] = (_dot(ckn, wk_ref[hd], 1, 0) + krr).astype(k_out.dtype)
            v_out[hd] = (_dot(ckn, wv_ref[hd], 1, 0) + ones_lane).astype(v_out.dtype)

    full = lambda *shape: pl.BlockSpec(shape, lambda i: (0,) * len(shape))
    slab = pl.BlockSpec((HEADS, tm, LANES), lambda i: (0, i, 0))
    shp = jax.ShapeDtypeStruct((HEADS, seq, LANES), MXU_DTYPE)
    return pl.pallas_call(
        body, name="mla_prep", grid=(seq // tm,),
        in_specs=[pl.BlockSpec((tm, DH_PART), lambda i: (i, 0)), full(1, Q_RANK), full(1, KV_RANK),
                  full(HEADS, Q_RANK, LANES), full(HEADS, KV_RANK, LANES), full(HEADS, KV_RANK, LANES),
                  pl.BlockSpec((tm, LANES), lambda i: (i, 0)), pl.BlockSpec((tm, LANES), lambda i: (i, 0))],
        out_specs=[slab, slab, slab], out_shape=[shp, shp, shp],
        compiler_params=_params("parallel"),
    )(h, g_cq, g_ckv, wq, wk, wv, ctab, stab)


def _mla_prep_bwd(h, g_cq, g_ckv, wq, wk, wv, ctab, stab, dq, dk, dv, tm=512, after=()):
    seq = h.shape[0]
    tm = min(tm, seq)
    n_after = len(after)

    def body(h_ref, gq_ref, gk_ref, wq_ref, wk_ref, wv_ref, c_ref, s_ref, dq_ref, dk_ref, dv_ref, *rest):
        dh_ref, dwq_ref, dwk_ref, dwv_ref, dgq_ref, dgk_ref = rest[n_after:]

        @pl.when(pl.program_id(0) == 0)
        def _():
            for r in (dwq_ref, dwk_ref, dwv_ref, dgq_ref, dgk_ref):
                r[...] = jnp.zeros_like(r)

        hb = h_ref[...]
        ctab_, stab_ = c_ref[...], s_ref[...]
        gq, gk = gq_ref[...], gk_ref[...]
        xq, rq, cqn = _rms(hb[:, :Q_RANK], gq)
        xk, rk, ckn = _rms(hb[:, Q_RANK:Q_RANK + KV_RANK], gk)
        cqn = cqn.astype(MXU_DTYPE)
        ckn = ckn.astype(MXU_DTYPE)
        d_cqn = jnp.zeros((tm, Q_RANK), F32)
        d_ckn = jnp.zeros((tm, KV_RANK), F32)
        d_krr = jnp.zeros((tm, LANES), F32)
        for hd in range(HEADS):
            dqh = _rope_inv(dq_ref[hd], ctab_, stab_).astype(MXU_DTYPE)
            d_cqn += _dot(dqh, wq_ref[hd], 1, 1)
            dwq_ref[hd] += _dot(cqn, dqh, 0, 0)
            dkh = dk_ref[hd]
            d_krr += dkh
            dkh = dkh.astype(MXU_DTYPE)
            d_ckn += _dot(dkh, wk_ref[hd], 1, 1)
            dwk_ref[hd] += _dot(ckn, dkh, 0, 0)
            dvh = dv_ref[hd].astype(MXU_DTYPE)
            d_ckn += _dot(dvh, wv_ref[hd], 1, 1)
            dwv_ref[hd] += _dot(ckn, dvh, 0, 0)
        lane = lax.broadcasted_iota(jnp.int32, (tm, LANES), 1)
        rot = (lane >= NOPE_DIM) & (lane < NOPE_DIM + ROPE_DIM)
        d_kr = jnp.where(rot, _rope_inv(jnp.where(rot, d_krr, 0.0), ctab_, stab_), 0.0)

        def rms_bwd(dy, xh, r, g, dg_ref):
            dg_ref[...] += jnp.sum(dy * xh, axis=0, keepdims=True)
            dxh = dy * g
            return r * (dxh - xh * jnp.mean(dxh * xh, axis=-1, keepdims=True))

        d_cq = rms_bwd(d_cqn, xq, rq, gq, dgq_ref)
        d_ck = rms_bwd(d_ckn, xk, rk, gk, dgk_ref)
        dh_ref[...] = jnp.concatenate([d_cq, d_ck, d_kr], axis=1).astype(dh_ref.dtype)

    full = lambda *shape: pl.BlockSpec(shape, lambda i: (0,) * len(shape))
    slab = pl.BlockSpec((HEADS, tm, LANES), lambda i: (0, i, 0))
    return pl.pallas_call(
        body, name="mla_prep_bwd", grid=(seq // tm,),
        in_specs=[pl.BlockSpec((tm, DH_PART), lambda i: (i, 0)), full(1, Q_RANK), full(1, KV_RANK),
                  full(HEADS, Q_RANK, LANES), full(HEADS, KV_RANK, LANES), full(HEADS, KV_RANK, LANES),
                  pl.BlockSpec((tm, LANES), lambda i: (i, 0)), pl.BlockSpec((tm, LANES), lambda i: (i, 0)),
                  slab, slab, slab] + [_ANY_SPEC] * n_after,
        out_specs=[pl.BlockSpec((tm, DH_PART), lambda i: (i, 0)), full(HEADS, Q_RANK, LANES), full(HEADS, KV_RANK, LANES),
                   full(HEADS, KV_RANK, LANES), full(1, Q_RANK), full(1, KV_RANK)],
        out_shape=[jax.ShapeDtypeStruct((seq, DH_PART), MXU_DTYPE), jax.ShapeDtypeStruct((HEADS, Q_RANK, LANES), F32),
                   jax.ShapeDtypeStruct((HEADS, KV_RANK, LANES), F32), jax.ShapeDtypeStruct((HEADS, KV_RANK, LANES), F32),
                   jax.ShapeDtypeStruct((1, Q_RANK), F32), jax.ShapeDtypeStruct((1, KV_RANK), F32)],
        compiler_params=_params("arbitrary"),
    )(h, g_cq, g_ckv, wq, wk, wv, ctab, stab, dq, dk, dv, *after)


def _mla_attn_fwd(q, k, v, t=1024):
    _, seq, _ = q.shape
    t = min(t, seq)

    def body(q_ref, k_ref, v_ref, o_ref, ob_ref, lse_ref, m_ref, acc_ref, s_ref):
        i = pl.program_id(1)
        qb = q_ref[...]
        m_ref[...] = jnp.full_like(m_ref, NEG_BIG)
        acc_ref[...] = jnp.zeros_like(acc_ref)

        def scores(j):
            return _dot(qb, k_ref[pl.ds(pl.multiple_of(j * t, t), t), :], 1, 1) * MLA_SCALE_LOG2

        def softmax_pv(j, s, rows=slice(None), mask=None):
            vb = v_ref[pl.ds(pl.multiple_of(j * t, t), s.shape[1]), :]
            if mask is not None:
                s = jnp.where(mask, s, NEG_BIG)
            m_old = m_ref[rows, :]
            m_new = jnp.maximum(m_old, jnp.max(s, axis=1, keepdims=True))
            p = jnp.exp2(s - m_new)
            a = jnp.exp2(m_old - m_new)
            acc_ref[rows, :] = a * acc_ref[rows, :] + _dot(p.astype(MXU_DTYPE), vb, 1, 0)
            m_ref[rows, :] = m_new

        def softmax_pv_diagonal(j):
            th = t // MLA_FWD_SPLITS_DIAGONAL
            for hf in range(MLA_FWD_SPLITS_DIAGONAL):
                nk = (hf + 1) * th
                row = lax.broadcasted_iota(jnp.int32, (th, nk), 0) + hf * th
                rows = slice(hf * th, (hf + 1) * th)
                softmax_pv(j, s_ref[rows, 0:nk], rows, row >= lax.broadcasted_iota(jnp.int32, (th, nk), 1))

        s_ref[...] = scores(0)

        def loop_body(j, c):
            s_next = scores(j + 1)
            softmax_pv(j, s_ref[...])
            s_ref[...] = s_next
            return c

        lax.fori_loop(0, i, loop_body, 0)
        softmax_pv_diagonal(i)
        acc = acc_ref[...]
        l = acc[:, ONES_LANE:ONES_LANE + 1]
        o = jnp.where(lax.broadcasted_iota(jnp.int32, acc.shape, 1) < HEAD_DIM, acc * (1.0 / l), 0.0)
        o_ref[...] = o
        ob_ref[...] = o.astype(ob_ref.dtype)
        lse_ref[...] = jnp.broadcast_to(m_ref[...] + jnp.log2(l), lse_ref.shape)

    blk = pl.BlockSpec((None, t, LANES), lambda h, i: (h, i, 0))
    whole = pl.BlockSpec((None, seq, LANES), lambda h, i: (h, 0, 0))
    shp = jax.ShapeDtypeStruct((HEADS, seq, LANES), F32)
    return pl.pallas_call(
        body, name="mla_attn_fwd", grid=(HEADS, seq // t),
        in_specs=[blk, whole, whole], out_specs=[blk, blk, blk],
        out_shape=[shp, jax.ShapeDtypeStruct((HEADS, seq, LANES), MXU_DTYPE), shp],
        scratch_shapes=[pltpu.VMEM((t, 1), F32), pltpu.VMEM((t, LANES), F32), pltpu.VMEM((t, t), F32)],
        compiler_params=_params("parallel", "arbitrary"),
    )(q, k, v)


def _mla_attn_bwd(q, k, v, o, lse, do, t=1024):
    _, seq, _ = q.shape
    t = min(t, seq)
    nb = seq // t

    def body(q_ref, k_ref, v_ref, o_ref, lse_ref, do_ref, dq_ref, dk_ref, dv_ref, dl_ref, dka_ref, dva_ref):
        dq_ref[...] = jnp.zeros_like(dq_ref)

        def delta_body(i, c):
            rows = pl.ds(pl.multiple_of(i * t, t), t)
            dl_ref[rows, :] = jnp.sum(do_ref[rows, :] * o_ref[rows, :], axis=1, keepdims=True)
            return c

        lax.fori_loop(0, nb, delta_body, 0)

        def kblock(j, c):
            krows = pl.ds(pl.multiple_of(j * t, t), t)
            kb = k_ref[krows, :]
            vb = v_ref[krows, :]
            dka_ref[...] = jnp.zeros_like(dka_ref)
            dva_ref[...] = jnp.zeros_like(dva_ref)

            def qstep(i, masked):
                ns = MLA_BWD_SPLITS_DIAGONAL if masked else MLA_BWD_SPLITS
                th = t // ns
                rows = [pl.ds(pl.multiple_of(i * t + hf * th, th), th) for hf in range(ns)]
                qs = [q_ref[r, :] for r in rows]
                dos = [do_ref[r, :].astype(MXU_DTYPE) for r in rows]
                nkeys = [(hf + 1) * th if masked else t for hf in range(ns)]
                ss = [_dot(qs[hf], kb[:nkeys[hf]], 1, 1) * MLA_SCALE_LOG2 for hf in range(ns)]
                dps = [_dot(dos[hf], vb[:nkeys[hf]], 1, 1) for hf in range(ns)]
                for hf in range(ns):
                    s, nk = ss[hf], nkeys[hf]
                    if masked:
                        row = lax.broadcasted_iota(jnp.int32, (th, nk), 0) + hf * th
                        s = jnp.where(row >= lax.broadcasted_iota(jnp.int32, (th, nk), 1), s, NEG_BIG)
                    p = jnp.exp2(s - lse_ref[rows[hf], 0:1])
                    dva_ref[0:nk, :] += _dot(p.astype(MXU_DTYPE), dos[hf], 0, 0)
                    ds = (p * (dps[hf] - dl_ref[rows[hf], :]) * MLA_SCALE).astype(MXU_DTYPE)
                    dka_ref[0:nk, :] += _dot(ds, qs[hf], 0, 0)
                    dq_ref[rows[hf], :] += _dot(ds, kb[:nk], 1, 0)

            qstep(j, True)

            def qloop(i, c2):
                qstep(i, False)
                return c2

            lax.fori_loop(j + 1, nb, qloop, 0)
            dk_ref[krows, :] = dka_ref[...]
            dv_ref[krows, :] = dva_ref[...]
            return c

        lax.fori_loop(0, nb, kblock, 0)

    whole = pl.BlockSpec((None, seq, LANES), lambda h: (h, 0, 0))
    shp = jax.ShapeDtypeStruct((HEADS, seq, LANES), F32)
    return pl.pallas_call(
        body, name="mla_attn_bwd", grid=(HEADS,),
        in_specs=[whole] * 6, out_specs=[whole] * 3, out_shape=[shp] * 3,
        scratch_shapes=[pltpu.VMEM((seq, 1), F32), pltpu.VMEM((t, LANES), F32), pltpu.VMEM((t, LANES), F32)],
        compiler_params=_params("parallel"),
    )(q, k, v, o, lse, do)


DIL_CHUNK = DIL_BLOCK * max(d for _, d in DIL_PAIRS)
DIL_PAIR_LANES = 2 * HEAD_DIM
assert DIL_PAIR_LANES == LANES
DIL_UNROLL_FWD = 16
DIL_UNROLL_BWD = 16


def _dil_bias_tables(hp, dil):
    b = DIL_BLOCK
    iq = lax.broadcasted_iota(jnp.int32, (b, 2 * b), 0)
    ik = lax.broadcasted_iota(jnp.int32, (b, 2 * b), 1)
    off = iq + b - ik
    band = (off >= 0) & (off <= b)
    dist = (off * dil).astype(F32)
    every, first = [], []
    for hh in range(2):
        slope = jnp.where(hp == 0, ALIBI_SLOPES[hh], jnp.where(hp == 1, ALIBI_SLOPES[2 + hh],
                          jnp.where(hp == 2, ALIBI_SLOPES[4 + hh], ALIBI_SLOPES[6 + hh]))).astype(F32)
        bias = -slope * dist
        every.append(jnp.where(band, bias, NEG_BIG))
        first.append(jnp.where(band & (ik >= b), bias, NEG_BIG))
    return jnp.concatenate(every, axis=0), jnp.concatenate(first, axis=0)


def _dil_rows(start, dil):
    return pl.ds(start, DIL_BLOCK) if dil == 1 else pl.ds(start, DIL_BLOCK, stride=dil)


def _dil_block_pos(blk, c, dil):
    sc, r = blk // dil, blk % dil
    q0 = sc * (DIL_BLOCK * dil) + r
    kcur0 = c * DIL_CHUNK + q0
    first = kcur0 < DIL_BLOCK * dil
    kprev0 = jnp.where(first, kcur0, kcur0 - DIL_BLOCK * dil)
    return q0, kcur0, kprev0, first


def _pair_cols(hh):
    return slice(HEAD_DIM * hh, HEAD_DIM * (hh + 1))


def _first_head_lanes(shape):
    return lax.broadcasted_iota(jnp.int32, shape, 1) < HEAD_DIM


def _stack_pair(t):
    first = _first_head_lanes(t.shape)
    return jnp.concatenate([jnp.where(first, t, 0.0), jnp.where(first, 0.0, t)], axis=0).astype(MXU_DTYPE)


def _unstack_pair(t):
    rows = t.shape[0] // 2
    return jnp.where(_first_head_lanes((rows, t.shape[1])), t[:rows], t[rows:])


def _pair_column(t):
    return jnp.concatenate([t[:, 0:1], t[:, HEAD_DIM:HEAD_DIM + 1]], axis=0)


def _dil_fwd(h):
    seq = h.shape[0]
    assert seq % DIL_CHUNK == 0
    nblk = DIL_CHUNK // DIL_BLOCK
    rc = 256

    def body(q_ref, k_ref, v_ref, o_ref, ob_ref, lse_ref, *scr):
        o_scr, l_scr = scr[:3], scr[3:]
        hp, c = pl.program_id(0), pl.program_id(1)
        for bi, (_, dil) in enumerate(DIL_PAIRS):
            tables = _dil_bias_tables(hp, dil)

            def block(blk, carry, bi=bi, dil=dil, tables=tables):
                q0, kcur0, kprev0, first = _dil_block_pos(blk, c, dil)
                q2 = _stack_pair(q_ref[_dil_rows(q0, dil), :] * DIL_SCALE)
                kcat = jnp.concatenate([k_ref[_dil_rows(kprev0, dil), :], k_ref[_dil_rows(kcur0, dil), :]], axis=0).astype(MXU_DTYPE)
                vcat = jnp.concatenate([v_ref[_dil_rows(kprev0, dil), :], v_ref[_dil_rows(kcur0, dil), :]], axis=0).astype(MXU_DTYPE)
                s = _dot(q2, kcat, 1, 1) + jnp.where(first, tables[1], tables[0])
                mx = jnp.max(s, axis=1, keepdims=True)
                p = jnp.exp(s - mx)
                l = jnp.sum(p, axis=1, keepdims=True)
                o_scr[bi][_dil_rows(q0, dil), :] = _unstack_pair(_dot(p.astype(MXU_DTYPE), vcat, 1, 0) * (1.0 / l))
                l_scr[bi][_dil_rows(q0, dil), :] = _unstack_pair(jnp.broadcast_to(mx + jnp.log(l), (2 * DIL_BLOCK, LANES)))
                return carry

            lax.fori_loop(0, nblk, block, 0, unroll=DIL_UNROLL_FWD)

        def combine(i, carry):
            rows = pl.ds(pl.multiple_of(i * rc, rc), rc)
            ls = [l_scr[bi][rows, :] for bi in range(3)]
            mx = jnp.maximum(jnp.maximum(ls[0], ls[1]), ls[2])
            es = [jnp.exp(l - mx) for l in ls]
            den = es[0] + es[1] + es[2]
            o = (es[0] * o_scr[0][rows, :] + es[1] * o_scr[1][rows, :] + es[2] * o_scr[2][rows, :]) / den
            o_ref[rows, :] = o
            ob_ref[rows, :] = o.astype(ob_ref.dtype)
            lse_ref[rows, :] = mx + jnp.log(den)
            return carry

        lax.fori_loop(0, DIL_CHUNK // rc, combine, 0)

    nq = DIL_WIDTH // LANES
    chunk = lambda off: pl.BlockSpec((DIL_CHUNK, LANES), lambda hp, c: (c, off + hp))
    whole = lambda off: pl.BlockSpec((seq, LANES), lambda hp, c: (0, off + hp))
    shp = jax.ShapeDtypeStruct((seq, DIL_WIDTH), F32)
    return pl.pallas_call(
        body, name="dil_fwd", grid=(nq, seq // DIL_CHUNK),
        in_specs=[chunk(nq), whole(2 * nq), whole(3 * nq)], out_specs=[chunk(0), chunk(0), chunk(0)],
        out_shape=[shp, jax.ShapeDtypeStruct((seq, DIL_WIDTH), MXU_DTYPE), shp],
        scratch_shapes=[pltpu.VMEM((DIL_CHUNK, LANES), F32)] * 6,
        compiler_params=_params("parallel", "arbitrary"),
    )(h, h, h)


def _dil_bwd(h, o, lse, do, after=()):
    seq = h.shape[0]
    nblk = DIL_CHUNK // DIL_BLOCK
    nchunk = seq // DIL_CHUNK
    rc = 256

    n_after = len(after)

    def body(q_ref, k_ref, v_ref, o_ref, lse_ref, do_ref, *rest):
        dq_out, dk_out, dv_out, dl_scr, dq_ref, dk_ref, dv_ref = rest[n_after:]
        hp, c = pl.program_id(0), pl.program_id(1)

        @pl.when(c == 0)
        def _():
            dk_ref[...] = jnp.zeros_like(dk_ref)
            dv_ref[...] = jnp.zeros_like(dv_ref)

        def delta(i, carry):
            rows = pl.ds(pl.multiple_of(i * rc, rc), rc)
            prod = do_ref[rows, :] * o_ref[rows, :]
            dl_scr[rows, :] = jnp.concatenate(
                [jnp.broadcast_to(jnp.sum(prod[:, _pair_cols(hh)], axis=1, keepdims=True), (rc, HEAD_DIM)) for hh in range(2)], axis=1)
            return carry

        lax.fori_loop(0, DIL_CHUNK // rc, delta, 0)

        for bi, (_, dil) in enumerate(DIL_PAIRS):
            tables = _dil_bias_tables(hp, dil)

            def block(blk, carry, bi=bi, dil=dil, tables=tables):
                q0, kcur0, kprev0, first = _dil_block_pos(blk, c, dil)
                qrows = _dil_rows(q0, dil)
                q2 = _stack_pair(q_ref[qrows, :] * DIL_SCALE)
                kcat = jnp.concatenate([k_ref[_dil_rows(kprev0, dil), :], k_ref[_dil_rows(kcur0, dil), :]], axis=0).astype(MXU_DTYPE)
                vcat = jnp.concatenate([v_ref[_dil_rows(kprev0, dil), :], v_ref[_dil_rows(kcur0, dil), :]], axis=0).astype(MXU_DTYPE)
                do2 = _stack_pair(do_ref[qrows, :])
                s = _dot(q2, kcat, 1, 1) + jnp.where(first, tables[1], tables[0])
                p = jnp.exp(s - _pair_column(lse_ref[qrows, :]))
                dp = _dot(do2, vcat, 1, 1)
                ds = (p * (dp - _pair_column(dl_scr[qrows, :]))).astype(MXU_DTYPE)
                dq_b = _unstack_pair(_dot(ds, kcat, 1, 0)) * DIL_SCALE
                dk_b = _dot(ds, q2, 0, 0)
                dv_b = _dot(p.astype(MXU_DTYPE), do2, 0, 0)
                if bi == 0:
                    dq_ref[qrows, :] = dq_b
                else:
                    dq_ref[qrows, :] += dq_b
                dk_ref[_dil_rows(kprev0, dil), :] += dk_b[:DIL_BLOCK]
                dv_ref[_dil_rows(kprev0, dil), :] += dv_b[:DIL_BLOCK]
                dk_ref[_dil_rows(kcur0, dil), :] += dk_b[DIL_BLOCK:]
                dv_ref[_dil_rows(kcur0, dil), :] += dv_b[DIL_BLOCK:]
                return carry

            lax.fori_loop(0, nblk, block, 0, unroll=DIL_UNROLL_BWD)

        dq_out[...] = dq_ref[...].astype(dq_out.dtype)

        @pl.when(c == nchunk - 1)
        def _():
            dk_out[...] = dk_ref[...].astype(dk_out.dtype)
            dv_out[...] = dv_ref[...].astype(dv_out.dtype)

    nq = DIL_WIDTH // LANES
    chunk = lambda off: pl.BlockSpec((DIL_CHUNK, LANES), lambda hp, c: (c, off + hp))
    whole = lambda off: pl.BlockSpec((seq, LANES), lambda hp, c: (0, off + hp))
    shp = jax.ShapeDtypeStruct((seq, DIL_WIDTH), MXU_DTYPE)
    return pl.pallas_call(
        body, name="dil_bwd", grid=(nq, nchunk),
        in_specs=[chunk(nq), whole(2 * nq), whole(3 * nq), chunk(0), chunk(0), chunk(0)] + [_ANY_SPEC] * n_after,
        out_specs=[chunk(0), whole(0), whole(0)], out_shape=[shp, shp, shp],
        scratch_shapes=[pltpu.VMEM((DIL_CHUNK, LANES), F32), pltpu.VMEM((DIL_CHUNK, LANES), F32),
                        pltpu.VMEM((seq, LANES), F32), pltpu.VMEM((seq, LANES), F32)],
        compiler_params=_params("parallel", "arbitrary"),
    )(h, h, h, o, lse, do, *after)


def _mm_dx0(parts, w_in_t, res, tm=1024, after=()):
    seq, d = res.shape
    tm = min(tm, seq)
    n_after = len(after)

    def body(a0, a1, a2, a3, b_ref, r_ref, *rest):
        o_ref = rest[n_after]
        acc = _dot(a0[...], b_ref[0:DH_PART, :], 1, 0)
        for c, a in enumerate((a1, a2, a3), start=1):
            acc += _dot(a[...], b_ref[DH_PART * c:DH_PART * (c + 1), :], 1, 0)
        o_ref[...] = acc + DN_ALPHA * r_ref[...]

    blk = pl.BlockSpec((tm, DH_PART), lambda i: (i, 0))
    row = pl.BlockSpec((tm, d), lambda i: (i, 0))
    return pl.pallas_call(
        body, name="mm_dx0", grid=(seq // tm,),
        in_specs=[blk] * 4 + [pl.BlockSpec((IN_PAD, d), lambda i: (0, 0), pipeline_mode=pl.Buffered(1)), row] + [_ANY_SPEC] * n_after,
        out_specs=row, out_shape=jax.ShapeDtypeStruct((seq, d), F32), compiler_params=_params("parallel"),
    )(*parts, w_in_t, res, *after)


def _mm_dw_in(parts, x0, tk=1024):
    seq, d = x0.shape
    tk = min(tk, seq)
    nk = seq // tk

    def body(a0, a1, a2, a3, b_ref, o_ref, acc_ref):
        kk = pl.program_id(0)

        @pl.when(kk == 0)
        def _():
            acc_ref[...] = jnp.zeros_like(acc_ref)

        b = b_ref[...].astype(MXU_DTYPE)
        for c, a in enumerate((a0, a1, a2, a3)):
            acc_ref[DH_PART * c:DH_PART * (c + 1), :] += _dot(a[...], b, 0, 0)

        @pl.when(kk == nk - 1)
        def _():
            o_ref[...] = acc_ref[...].astype(o_ref.dtype)

    blk = pl.BlockSpec((tk, DH_PART), lambda kk: (kk, 0))
    return pl.pallas_call(
        body, name="mm_dw_in", grid=(nk,), in_specs=[blk] * 4 + [pl.BlockSpec((tk, d), lambda kk: (kk, 0))],
        out_specs=pl.BlockSpec((IN_PAD, d), lambda kk: (0, 0)), out_shape=jax.ShapeDtypeStruct((IN_PAD, d), MXU_DTYPE),
        scratch_shapes=[pltpu.VMEM((IN_PAD, d), F32)], compiler_params=_params("arbitrary"),
    )(*parts, x0)


def _ln_stats(z):
    mu = jnp.mean(z, axis=-1, keepdims=True)
    zc = z - mu
    r = lax.rsqrt(jnp.mean(zc * zc, axis=-1, keepdims=True) + LN_EPS)
    return zc * r, r


def _ln_bwd_math(dy, xh, r, g):
    dxh = dy * g
    return r * (dxh - jnp.mean(dxh, axis=-1, keepdims=True) - xh * jnp.mean(dxh * xh, axis=-1, keepdims=True))


def _mix_ln1(o_mla, o_dil, w_o_mla, w_o_dil, x0, g, b, tm=512):
    seq, d = x0.shape
    tm = min(tm, seq)

    def body(om_ref, od_ref, wm_ref, wd_ref, x_ref, g_ref, b_ref, z_ref, y_ref, yb_ref):
        mix = _dot(od_ref[...], wd_ref[...], 1, 0)
        for hd in range(HEADS):
            mix += _dot(om_ref[hd], wm_ref[LANES * hd:LANES * (hd + 1), :], 1, 0)
        z = DN_ALPHA * x_ref[...] + mix
        xh, _ = _ln_stats(z)
        y = xh * g_ref[...] + b_ref[...]
        z_ref[...] = z
        y_ref[...] = y
        yb_ref[...] = y.astype(yb_ref.dtype)

    blk = pl.BlockSpec((tm, d), lambda i: (i, 0))
    vec = pl.BlockSpec((1, d), lambda i: (0, 0))
    shp = jax.ShapeDtypeStruct((seq, d), F32)
    return pl.pallas_call(
        body, name="mix_ln1", grid=(seq // tm,),
        in_specs=[pl.BlockSpec((HEADS, tm, LANES), lambda i: (0, i, 0)), pl.BlockSpec((tm, DIL_WIDTH), lambda i: (i, 0)),
                  pl.BlockSpec((HEADS * LANES, d), lambda i: (0, 0)), pl.BlockSpec((DIL_WIDTH, d), lambda i: (0, 0)), blk, vec, vec],
        out_specs=[blk, blk, blk], out_shape=[shp, shp, jax.ShapeDtypeStruct((seq, d), MXU_DTYPE)],
        compiler_params=_params("parallel"))(o_mla, o_dil, w_o_mla, w_o_dil, x0, g, b)


def _dx1_ln1_bwd(du, w_up_t, dz2, z, g, tm=256, after=()):
    seq, d = z.shape
    kdim = du.shape[1]
    tm = min(tm, seq)
    n_after = len(after)

    def body(du_ref, w_ref, r_ref, z_ref, g_ref, *rest):
        dz_ref, dzb_ref, dg_ref, db_ref = rest[n_after:]

        @pl.when(pl.program_id(0) == 0)
        def _():
            dg_ref[...] = jnp.zeros_like(dg_ref)
            db_ref[...] = jnp.zeros_like(db_ref)

        dyb = _dot(du_ref[...], w_ref[...], 1, 0) + DN_ALPHA * r_ref[...]
        xh, r = _ln_stats(z_ref[...])
        dg_ref[...] += jnp.sum(dyb * xh, axis=0, keepdims=True)
        db_ref[...] += jnp.sum(dyb, axis=0, keepdims=True)
        dz = _ln_bwd_math(dyb, xh, r, g_ref[...])
        dz_ref[...] = dz
        dzb_ref[...] = dz.astype(dzb_ref.dtype)

    blk = pl.BlockSpec((tm, d), lambda i: (i, 0))
    vec = pl.BlockSpec((1, d), lambda i: (0, 0))
    return pl.pallas_call(
        body, name="dx1_ln1_bwd", grid=(seq // tm,),
        in_specs=[pl.BlockSpec((tm, kdim), lambda i: (i, 0)),
                  pl.BlockSpec((kdim, d), lambda i: (0, 0), pipeline_mode=pl.Buffered(1)), blk, blk, vec] + [_ANY_SPEC] * n_after,
        out_specs=[blk, blk, vec, vec],
        out_shape=[jax.ShapeDtypeStruct((seq, d), F32), jax.ShapeDtypeStruct((seq, d), MXU_DTYPE),
                   jax.ShapeDtypeStruct((1, d), F32), jax.ShapeDtypeStruct((1, d), F32)],
        compiler_params=_params("arbitrary"))(du, w_up_t, dz2, z, g, *after)


def _down_ln2_loss_bwd(act, w_down, x1, target, g, b, tm=512):
    seq, d = x1.shape
    kdim = act.shape[1]
    tm = min(tm, seq)

    def body(a_ref, w_ref, x_ref, t_ref, g_ref, b_ref, dz_ref, dzb_ref, loss_ref, dg_ref, db_ref):
        @pl.when(pl.program_id(0) == 0)
        def _():
            loss_ref[...] = jnp.zeros_like(loss_ref)
            dg_ref[...] = jnp.zeros_like(dg_ref)
            db_ref[...] = jnp.zeros_like(db_ref)

        gv = g_ref[...]
        z = DN_ALPHA * x_ref[...] + _dot(a_ref[...], w_ref[...], 1, 0)
        xh, r = _ln_stats(z)
        err = (xh * gv + b_ref[...]) - t_ref[...]
        loss_ref[...] += 0.5 * jnp.sum(jnp.mean(err * err, axis=-1, keepdims=True), axis=0, keepdims=True)
        dy = err * (1.0 / d)
        dg_ref[...] += jnp.sum(dy * xh, axis=0, keepdims=True)
        db_ref[...] += jnp.sum(dy, axis=0, keepdims=True)
        dz = _ln_bwd_math(dy, xh, r, gv)
        dz_ref[...] = dz
        dzb_ref[...] = dz.astype(dzb_ref.dtype)

    blk = pl.BlockSpec((tm, d), lambda i: (i, 0))
    vec = pl.BlockSpec((1, d), lambda i: (0, 0))
    return pl.pallas_call(
        body, name="down_ln2_loss_bwd", grid=(seq // tm,),
        in_specs=[pl.BlockSpec((tm, kdim), lambda i: (i, 0)),
                  pl.BlockSpec((kdim, d), lambda i: (0, 0), pipeline_mode=pl.Buffered(1)), blk, blk, vec, vec],
        out_specs=[blk, blk, pl.BlockSpec((1, LANES), lambda i: (0, 0)), vec, vec],
        out_shape=[jax.ShapeDtypeStruct((seq, d), F32), jax.ShapeDtypeStruct((seq, d), MXU_DTYPE),
                   jax.ShapeDtypeStruct((1, LANES), F32),
                   jax.ShapeDtypeStruct((1, d), F32), jax.ShapeDtypeStruct((1, d), F32)],
        compiler_params=_params("arbitrary"))(act, w_down, x1, target, g, b)


HALO = 16


def _conv_rows(e, w_ref, b_ref):
    y = b_ref[...] + w_ref[0:1, :] * pltpu.roll(e, 2, 0)
    y = y + w_ref[1:2, :] * pltpu.roll(e, 1, 0)
    return y + w_ref[2:3, :] * e


_GELU_C = math.sqrt(2.0 / math.pi)
_GELU_A = 0.044715


def _gelu(x):
    return x * (0.5 + 0.5 * jnp.tanh(x * (_GELU_C + (_GELU_C * _GELU_A) * (x * x))))


CONV_TN = 256


def _ffn_interleave(a, axis):
    shp = a.shape
    a = a.reshape(shp[:axis] + (2, D_FF // CONV_TN, CONV_TN) + shp[axis + 1:])
    return jnp.swapaxes(a, axis, axis + 1).reshape(shp)


def _ffn_deinterleave(a, axis):
    shp = a.shape
    a = a.reshape(shp[:axis] + (D_FF // CONV_TN, 2, CONV_TN) + shp[axis + 1:])
    return jnp.swapaxes(a, axis, axis + 1).reshape(shp)


def _up_conv_gate_fwd(x1, w_up_t, conv_w, conv_b, tm=1024):
    seq, d = x1.shape
    tm = min(tm, seq)
    tn = CONV_TN

    def body(x_ref, xp_ref, wu_ref, w_ref, b_ref, u_ref, y_ref, o_ref):
        first = pl.program_id(0) == 0
        halo = jnp.where(first, jnp.zeros_like(xp_ref), xp_ref[...])
        e = _dot(jnp.concatenate([halo, x_ref[...]], axis=0), wu_ref[...], 1, 1)
        u_ref[...] = e[HALO:]
        y = _conv_rows(e, w_ref, b_ref)[HALO:]
        y_ref[...] = y
        o_ref[...] = (_gelu(y[:, tn:]) * y[:, :tn]).astype(o_ref.dtype)

    hb = tm // HALO
    return pl.pallas_call(
        body, name="up_conv_gate_fwd", grid=(seq // tm, D_FF // tn),
        in_specs=[pl.BlockSpec((tm, d), lambda i, j: (i, 0)),
                  pl.BlockSpec((HALO, d), lambda i, j: (jnp.maximum(i * hb - 1, 0), 0)),
                  pl.BlockSpec((2 * tn, d), lambda i, j: (j, 0)),
                  pl.BlockSpec((3, 2 * tn), lambda i, j: (0, j)), pl.BlockSpec((1, 2 * tn), lambda i, j: (0, j))],
        out_specs=[pl.BlockSpec((tm, 2 * tn), lambda i, j: (i, j)), pl.BlockSpec((tm, 2 * tn), lambda i, j: (i, j)),
                   pl.BlockSpec((tm, tn), lambda i, j: (i, j))],
        out_shape=[jax.ShapeDtypeStruct((seq, 2 * D_FF), F32), jax.ShapeDtypeStruct((seq, 2 * D_FF), F32),
                   jax.ShapeDtypeStruct((seq, D_FF), MXU_DTYPE)],
        compiler_params=_params("parallel", "arbitrary"),
    )(x1, x1, w_up_t, conv_w, conv_b)


def _conv_gate_bwd(u, y, d_act, conv_w, tm=1024):
    seq = u.shape[0]
    tm = min(tm, seq)
    tn = CONV_TN
    ni = seq // tm
    rows = tm + HALO

    def body(u_ref, y_ref, yn_ref, da_ref, dan_ref, w_ref, du_ref, dw_ref, db_ref):
        i = pl.program_id(1)
        last = i == ni - 1

        @pl.when(i == 0)
        def _():
            dw_ref[...] = jnp.zeros_like(dw_ref)
            db_ref[...] = jnp.zeros_like(db_ref)

        yy = jnp.concatenate([y_ref[...], yn_ref[...]], axis=0)
        ya, yg = yy[:, :tn], yy[:, tn:]
        dact = jnp.concatenate([da_ref[...].astype(F32), jnp.where(last, 0.0, dan_ref[...].astype(F32))], axis=0)
        yg2 = yg * yg
        th = jnp.tanh(yg * (_GELU_C + (_GELU_C * _GELU_A) * yg2))
        half = 0.5 + 0.5 * th
        gelu = yg * half
        gelu_grad = half + gelu * (1.0 - half) * (2.0 * _GELU_C + (6.0 * _GELU_C * _GELU_A) * yg2)
        dy = jnp.concatenate([dact * gelu, dact * ya * gelu_grad], axis=1)
        dy0, dy1, dy2 = dy[:tm], pltpu.roll(dy, rows - 1, 0)[:tm], pltpu.roll(dy, rows - 2, 0)[:tm]
        du_ref[...] = (w_ref[2:3, :] * dy0 + w_ref[1:2, :] * dy1 + w_ref[0:1, :] * dy2).astype(du_ref.dtype)
        ut = u_ref[...]
        dw_ref[0:1, :] += jnp.sum(dy2 * ut, axis=0, keepdims=True)
        dw_ref[1:2, :] += jnp.sum(dy1 * ut, axis=0, keepdims=True)
        dw_ref[2:3, :] += jnp.sum(dy0 * ut, axis=0, keepdims=True)
        db_ref[...] += jnp.sum(dy0, axis=0, keepdims=True)

    hb = tm // HALO
    nh = seq // HALO
    nxt = lambda j, i: (jnp.minimum((i + 1) * hb, nh - 1), j)
    tile = pl.BlockSpec((tm, 2 * tn), lambda j, i: (i, j))
    return pl.pallas_call(
        body, name="conv_gate_bwd", grid=(D_FF // tn, ni),
        in_specs=[tile, tile, pl.BlockSpec((HALO, 2 * tn), nxt),
                  pl.BlockSpec((tm, tn), lambda j, i: (i, j)), pl.BlockSpec((HALO, tn), nxt),
                  pl.BlockSpec((3, 2 * tn), lambda j, i: (0, j))],
        out_specs=[tile, pl.BlockSpec((3, 2 * tn), lambda j, i: (0, j)), pl.BlockSpec((1, 2 * tn), lambda j, i: (0, j))],
        out_shape=[jax.ShapeDtypeStruct((seq, 2 * D_FF), MXU_DTYPE), jax.ShapeDtypeStruct((3, 2 * D_FF), F32),
                   jax.ShapeDtypeStruct((1, 2 * D_FF), F32)],
        compiler_params=_params("parallel", "arbitrary"),
    )(u, y, y, d_act, d_act, conv_w)


def _pad_heads(w, width):
    w = jnp.transpose(w, (1, 0, 2))
    return jnp.pad(w, ((0, 0), (0, 0), (0, LANES - width))).astype(MXU_DTYPE)


def _heads_major(a):
    return jnp.transpose(a, (1, 0, 2)).reshape(-1, a.shape[2])


def _heads_minor(a, heads):
    return jnp.transpose(a.reshape(heads, -1, a.shape[1]), (1, 0, 2))


_LATENT = Q_RANK + KV_RANK
_ROPE_AT = _LATENT + NOPE_DIM
_ROPE_END = _ROPE_AT + ROPE_DIM


def _split_pad_rows(w_t):
    z = lambda n: jnp.zeros((n, w_t.shape[1]), w_t.dtype)
    return jnp.concatenate([w_t[:_LATENT], z(_ROPE_AT - _LATENT), w_t[_LATENT:_LATENT + ROPE_DIM], z(DH_PART - _ROPE_END),
                            w_t[_LATENT + ROPE_DIM:]], axis=0)


def _split_unpad_rows(w_p):
    return jnp.concatenate([w_p[:_LATENT], w_p[_ROPE_AT:_ROPE_END], w_p[DH_PART:]], axis=0)


def _pad_w_o(w_o):
    mla = jnp.pad(w_o[:MLA_WIDTH].reshape(HEADS, HEAD_DIM, D_MODEL), ((0, 0), (0, LANES - HEAD_DIM), (0, 0)))
    return mla.reshape(HEADS * LANES, D_MODEL).astype(MXU_DTYPE), w_o[MLA_WIDTH:].astype(MXU_DTYPE)


def _unpad_w_o(d_mla, d_dil):
    return jnp.concatenate([d_mla.reshape(HEADS, LANES, D_MODEL)[:, :HEAD_DIM].reshape(MLA_WIDTH, D_MODEL), d_dil], axis=0)


def _row(v):
    return v.reshape(1, -1).astype(F32)


def _layer_grads(x0, target, cw, first_after=(), late_weights=None, on_grads=None):
    seq = x0.shape[0]
    ctab, stab = _rope_tables(seq)
    gq, gk = cw["g_cq"], cw["g_ckv"]
    wq, wk, wv = cw["wq"], cw["wk"], cw["wv"]
    notify = (lambda stage, grads: ()) if on_grads is None else on_grads

    h = _mm(x0, cw["w_in_t"], name="mm_h", tb=True, tm=1024, tn=IN_PAD, tk=1024, after=first_after)
    qf, kf, vp = _mla_prep(h, gq, gk, wq, wk, wv, ctab, stab)
    o_mla, o_mla_b, lse_mla = _mla_attn_fwd(qf, kf, vp)
    o_dil, o_dil_b, lse_dil = _dil_fwd(h)
    fetch = (lambda stage, after: {}) if late_weights is None else late_weights
    cw = {**cw, **fetch("w_o", o_mla_b)}
    cb = cw["conv_b"]
    z1, x1, x1b = _mix_ln1(o_mla_b, o_dil_b, cw["w_o_mla"], cw["w_o_dil"], x0, cw["ln1_g"], cw["ln1_b"])
    cw = {**cw, **fetch("w_up", x1b)}
    u, y, act = _up_conv_gate_fwd(x1b, cw["w_up_t"], cw["conv_w"], cb)
    cw = {**cw, **fetch("w_down", act)}
    dz2, dz2b, loss, d_ln2_g, d_ln2_b = _down_ln2_loss_bwd(act, cw["w_down"], x1, target, cw["ln2_g"], cw["ln2_b"])

    d_act = _mm(dz2b, cw["w_down"], name="mm_d_act", tb=True, out_dtype=MXU_DTYPE, tm=1024, tn=D_FF, tk=1024)
    d_w_down = _mm(act, dz2b, name="mm_dw_down", ta=True, out_dtype=MXU_DTYPE, tm=1408, tn=1024, tk=1024)
    du, d_conv_w, d_conv_b = _conv_gate_bwd(u, y, d_act, cw["conv_w"])
    d_w_up_t = _mm(du, x1b, name="mm_dw_up", ta=True, out_dtype=MXU_DTYPE, tm=1408, tn=1024, tk=2048)
    grads = dict(w_up_t=d_w_up_t, w_down=d_w_down, conv_w=d_conv_w, conv_b=d_conv_b, ln2_g=d_ln2_g, ln2_b=d_ln2_b)
    dz1, dz1b, d_ln1_g, d_ln1_b = _dx1_ln1_bwd(du, cw["w_up_t"], dz2, z1, cw["ln1_g"], after=notify("ffn", grads))
    do_mla, do_dil, d_w_o_mla, d_w_o_dil = _w_o_bwd(dz1b, o_mla_b, o_dil_b, cw["w_o_mla"], cw["w_o_dil"])
    grads.update(w_o_mla=d_w_o_mla, w_o_dil=d_w_o_dil, ln1_g=d_ln1_g, ln1_b=d_ln1_b)
    dqf, dkf, dvf = _mla_attn_bwd(qf, kf, vp, o_mla, lse_mla, do_mla)
    dh_mla, d_wq, d_wk, d_wv, d_gq, d_gk = _mla_prep_bwd(h, gq, gk, wq, wk, wv, ctab, stab, dqf, dkf, dvf,
                                                          after=notify("w_o", grads))
    grads.update(wq=d_wq, wk=d_wk, wv=d_wv, g_cq=d_gq, g_ckv=d_gk, loss=loss)
    dq_dil, dk_dil, dv_dil = _dil_bwd(h, o_dil, lse_dil, do_dil, after=notify("mla", grads))
    dh = (dh_mla, dq_dil, dk_dil, dv_dil)
    grads.update(w_in_t=_mm_dw_in(dh, x0))
    grad_x = _mm_dx0(dh, cw["w_in_t"], dz1, after=notify("w_in", grads))
    return loss, grad_x, grads


def _all_gather(blocks, name):
    na = len(blocks)

    def body(*refs):
        ins, outs = refs[:na], refs[na:2 * na]
        send_sems, recv_sems, local_sems = refs[2 * na:]
        x, y, c = lax.axis_index("x"), lax.axis_index("y"), lax.axis_index("c")
        me, sibling = (x, y, c), (x, y, 1 - c)
        chips = [(1 - x, y), (x, 1 - y), (1 - x, 1 - y)]

        def slot(out, pos):
            return out.at[4 * pos[0] + 2 * pos[1] + pos[2]]

        def copy(a, k, block, to, src=None):
            return pltpu.make_async_remote_copy(
                src_ref=slot(outs[a], block) if src is None else src, dst_ref=slot(outs[a], block),
                send_sem=send_sems.at[7 * a + k], recv_sem=recv_sems.at[7 * a + k],
                device_id=to, device_id_type=pl.DeviceIdType.MESH)

        mine = [pltpu.make_async_copy(ins[a], slot(outs[a], me), local_sems.at[a]) for a in range(na)]
        for cp in mine:
            cp.start()
        first = []
        for a in range(na):
            first.append(copy(a, 0, me, sibling, src=ins[a]))
            first += [copy(a, 1 + j, me, (*chip, c), src=ins[a]) for j, chip in enumerate(chips)]
        for cp in first:
            cp.start()
        passed = []
        for j, chip in enumerate(chips):
            for a in range(na):
                copy(a, 1 + j, (*chip, c), me).wait_recv()
                cp = copy(a, 4 + j, (*chip, c), sibling)
                cp.start()
                passed.append(cp)
        for a in range(na):
            copy(a, 0, sibling, me).wait_recv()
            for j, chip in enumerate(chips):
                copy(a, 4 + j, (*chip, 1 - c), me).wait_recv()
        for cp in first + passed:
            cp.wait_send()
        for cp in mine:
            cp.wait()

    any_spec = pl.BlockSpec(memory_space=pl.ANY)
    return pl.pallas_call(
        body, name=name, in_specs=[any_spec] * na, out_specs=[any_spec] * na,
        out_shape=[jax.ShapeDtypeStruct((N_DEV,) + b.shape, b.dtype) for b in blocks],
        scratch_shapes=[pltpu.SemaphoreType.DMA((7 * na,)), pltpu.SemaphoreType.DMA((7 * na,)), pltpu.SemaphoreType.DMA((na,))],
    )(*blocks)


_HBM_SPEC = pl.BlockSpec(memory_space=pltpu.HBM)
_SEM_SPEC = pl.BlockSpec(memory_space=pltpu.SEMAPHORE)
_DATAFLOW = pltpu.CompilerParams(has_side_effects=pltpu.SideEffectType.DATAFLOW_SIDE_EFFECTING)


def _split_copies(which, ins, lands, send_sems, recv_sems, gather):
    x, y, c = lax.axis_index("x"), lax.axis_index("y"), lax.axis_index("c")
    me = 4 * x + 2 * y + c
    copies = []
    for a, src, land in zip(which, ins, lands):
        for d in range(1, N_DEV):
            px, py, pc = x ^ (d >> 2), y ^ ((d >> 1) & 1), c ^ (d & 1)
            copies.append(pltpu.make_async_remote_copy(
                src_ref=src if gather[a] else src.at[4 * px + 2 * py + pc], dst_ref=land.at[me],
                send_sem=send_sems.at[7 * a + d - 1], recv_sem=recv_sems.at[7 * a + d - 1],
                device_id=(px, py, pc), device_id_type=pl.DeviceIdType.MESH))
    return copies


def _send_start(srcs, gather, name):
    na = len(srcs)
    assert len(gather) == na
    land_types = [pltpu.HBM(((N_DEV,) + s.shape) if g else s.shape, s.dtype) for s, g in zip(srcs, gather)]

    def body(*refs):
        ins, lands = refs[:na], refs[na:2 * na]
        send_sems, recv_sems, token = refs[2 * na], refs[2 * na + 1], refs[-1]
        for cp in _split_copies(range(na), ins, lands, send_sems, recv_sems, gather):
            cp.start()
        token[...] = jnp.zeros_like(token)

    hbm = lambda a: pltpu.with_memory_space_constraint(a, pltpu.HBM)
    outs = pl.pallas_call(
        body, name=name,
        out_shape=(pltpu.SemaphoreType.DMA((7 * na,)), pltpu.SemaphoreType.DMA((7 * na,)),
                   *[pltpu.HBM(s.shape, s.dtype) for s in srcs], *land_types, jax.ShapeDtypeStruct((8, LANES), F32)),
        in_specs=[_HBM_SPEC] * (2 * na),
        out_specs=(_SEM_SPEC, _SEM_SPEC, *[_HBM_SPEC] * (2 * na), pl.BlockSpec(memory_space=pltpu.VMEM)),
        input_output_aliases={i: 2 + i for i in range(2 * na)}, compiler_params=_DATAFLOW,
    )(*[hbm(s) for s in srcs], *[hbm(lax.empty(t.shape, t.dtype)) for t in land_types])
    return dict(send=outs[0], recv=outs[1], srcs=list(outs[2:2 + na]), lands=list(outs[2 + na:2 + 2 * na]), token=outs[-1],
                gather=gather)


def _send_wait(handle, after, name, only=None):
    which = list(range(len(handle["srcs"]))) if only is None else list(only)
    na = len(which)
    gather = handle["gather"]
    after = list(after)

    def body(*refs):
        ins, lands = refs[:na], refs[na:2 * na]
        send_sems, recv_sems = refs[2 * na], refs[2 * na + 1]
        for cp in _split_copies(which, ins, lands, send_sems, recv_sems, gather):
            cp.wait_send()
            cp.wait_recv()

    both = [handle["srcs"][a] for a in which] + [handle["lands"][a] for a in which]
    outs = pl.pallas_call(
        body, name=name, out_shape=[pltpu.HBM(a.shape, a.dtype) for a in both],
        in_specs=[_HBM_SPEC] * (2 * na) + [_SEM_SPEC, _SEM_SPEC] + [_ANY_SPEC] * len(after),
        out_specs=[_HBM_SPEC] * (2 * na), input_output_aliases={i: i for i in range(2 * na)}, compiler_params=_DATAFLOW,
    )(*both, handle["send"], handle["recv"], *after)
    return list(outs[:na]), list(outs[na:])


def _sum_slots(p_ref):
    g = p_ref[0].astype(F32)
    for s in range(1, p_ref.shape[0]):
        g = g + p_ref[s].astype(F32)
    return g


def _adamw_refs(g, w_ref, m_ref, v_ref, g_out, d_out, m_out, v_out):
    c1 = 1.0 - ADAM_B1 ** ADAM_STEP
    c2 = 1.0 - ADAM_B2 ** ADAM_STEP
    m_new = ADAM_B1 * m_ref[...] + (1.0 - ADAM_B1) * g
    v_new = ADAM_B2 * v_ref[...] + (1.0 - ADAM_B2) * (g * g)
    g_out[...] = g
    m_out[...] = m_new
    v_out[...] = v_new
    d_out[...] = -ADAM_LR * ((m_new / c1) / (jnp.sqrt(v_new / c2) + ADAM_EPS) + ADAM_WD * w_ref[...])


def _adamw(parts, w, m, v, name):
    npart, r, n = parts.shape
    tr = r if r <= 256 else max(t for t in range(16, 257, 16) if r % t == 0)

    def body(p_ref, w_ref, m_ref, v_ref, g_out, d_out, m_out, v_out):
        _adamw_refs(_sum_slots(p_ref), w_ref, m_ref, v_ref, g_out, d_out, m_out, v_out)

    blk = pl.BlockSpec((tr, n), lambda i: (i, 0))
    shp = jax.ShapeDtypeStruct((r, n), F32)
    return pl.pallas_call(
        body, name=name, grid=(r // tr,), in_specs=[pl.BlockSpec((npart, tr, n), lambda i: (0, i, 0)), blk, blk, blk],
        out_specs=[blk] * 4, out_shape=[shp] * 4, compiler_params=_params("parallel"),
    )(parts, w, m, v)


def _adamw_small(parts, ws, ms, vs, loss_parts, name):
    n = len(parts)

    def body(*refs):
        ins, outs = refs[:4 * n + 1], refs[4 * n + 1:]
        for i in range(n):
            _adamw_refs(_sum_slots(ins[i]), ins[n + i], ins[2 * n + i], ins[3 * n + i], *outs[4 * i:4 * i + 4])
        outs[4 * n][...] = _sum_slots(ins[4 * n])

    out_shape = [jax.ShapeDtypeStruct(w.shape, F32) for w in ws for _ in range(4)]
    res = pl.pallas_call(body, name=name, out_shape=out_shape + [jax.ShapeDtypeStruct((1, LANES), F32)],
                         compiler_params=_params())(*parts, *ws, *ms, *vs, loss_parts)
    return [res[4 * i:4 * i + 4] for i in range(n)], res[4 * n]


REPLICATED = ("g_cq", "g_ckv", "w_uk", "w_uv", "ln1_g", "ln1_b", "conv_b", "ln2_g", "ln2_b")
ALL_WEIGHTS = ("w_in", "g_cq", "g_ckv", "w_uq", "w_uk", "w_uv", "w_o", "ln1_g", "ln1_b", "w_up", "conv_w", "conv_b",
               "w_down", "ln2_g", "ln2_b")


def kernel(x, w_in, g_cq, g_ckv, w_uq, w_uk, w_uv, w_o, ln1_g, ln1_b, w_up, conv_w, conv_b, w_down, ln2_g, ln2_b, loss_target, m_w_in, m_g_cq, m_g_ckv, m_w_uq, m_w_uk, m_w_uv, m_w_o, m_ln1_g, m_ln1_b, m_w_up, m_conv_w, m_conv_b, m_w_down, m_ln2_g, m_ln2_b, v_w_in, v_g_cq, v_g_ckv, v_w_uq, v_w_uk, v_w_uv, v_w_o, v_ln1_g, v_ln1_b, v_w_up, v_conv_w, v_conv_b, v_w_down, v_ln2_g, v_ln2_b):
    w = dict(w_in=w_in, g_cq=g_cq, g_ckv=g_ckv, w_uq=w_uq, w_uk=w_uk, w_uv=w_uv, w_o=w_o, ln1_g=ln1_g, ln1_b=ln1_b,
             w_up=w_up, conv_w=conv_w, conv_b=conv_b, w_down=w_down, ln2_g=ln2_g, ln2_b=ln2_b)
    m = dict(w_in=m_w_in, g_cq=m_g_cq, g_ckv=m_g_ckv, w_uq=m_w_uq, w_uk=m_w_uk, w_uv=m_w_uv, w_o=m_w_o, ln1_g=m_ln1_g,
             ln1_b=m_ln1_b, w_up=m_w_up, conv_w=m_conv_w, conv_b=m_conv_b, w_down=m_w_down, ln2_g=m_ln2_g, ln2_b=m_ln2_b)
    v = dict(w_in=v_w_in, g_cq=v_g_cq, g_ckv=v_g_ckv, w_uq=v_w_uq, w_uk=v_w_uk, w_uv=v_w_uv, w_o=v_w_o, ln1_g=v_ln1_g,
             ln1_b=v_ln1_b, w_up=v_w_up, conv_w=v_conv_w, conv_b=v_conv_b, w_down=v_w_down, ln2_g=v_ln2_g, ln2_b=v_ln2_b)
    me = 4 * lax.axis_index("x") + 2 * lax.axis_index("y") + lax.axis_index("c")
    wire = lambda a: a.astype(WIRE_DTYPE)
    pad_taps = lambda a: jnp.pad(a, ((0, 8 - a.shape[0]), (0, 0)))

    own_slot = lambda buf, block: lax.dynamic_update_index_in_dim(buf, block, me, 0)
    blocks = lambda a: wire(a).reshape((N_DEV, a.shape[0] // N_DEV) + a.shape[1:])

    g_in, g_uq, g_conv = _all_gather(
        [wire(w_in).T, _heads_major(wire(w_uq)), pad_taps(conv_w)],
        "gather_weights")
    late = _send_start([wire(w_o), wire(w_up).T, wire(w_down)], [True] * 3, "gather_late_start")
    r_uq_dev, e_uq = w_uq.shape[0], w_uq.shape[2]
    wq = jnp.transpose(g_uq.reshape(N_DEV, HEADS, r_uq_dev, e_uq), (1, 0, 2, 3)).reshape(HEADS, Q_RANK, e_uq)
    cw = dict(
        w_in_t=_split_pad_rows(g_in.reshape(-1, D_MODEL)).astype(MXU_DTYPE),
        wq=jnp.pad(wq, ((0, 0), (0, 0), (0, LANES - e_uq))).astype(MXU_DTYPE),
        wk=_pad_heads(w_uk, NOPE_DIM), wv=_pad_heads(w_uv, HEAD_DIM),
        conv_w=_ffn_interleave(jnp.transpose(g_conv[:, :conv_w.shape[0]], (1, 0, 2)).reshape(conv_w.shape[0], -1), 1),
        g_cq=_row(g_cq), g_ckv=_row(g_ckv), ln1_g=_row(ln1_g), ln1_b=_row(ln1_b), conv_b=_ffn_interleave(_row(conv_b), 1),
        ln2_g=_row(ln2_g), ln2_b=_row(ln2_b))

    def late_weights(stage, after):
        (own,), (got,) = _send_wait(late, [after], f"gather_{stage}_wait", only=[("w_o", "w_up", "w_down").index(stage)])
        full = own_slot(got, own).reshape(-1, D_MODEL)
        if stage == "w_o":
            w_o_mla, w_o_dil = _pad_w_o(full)
            return dict(w_o_mla=w_o_mla, w_o_dil=w_o_dil)
        if stage == "w_up":
            return dict(w_up_t=_ffn_interleave(full, 0).astype(MXU_DTYPE))
        return dict(w_down=full.astype(MXU_DTYPE))

    sent = {}

    def on_grads(stage, g):
        if stage == "ffn":
            sent[stage] = _send_start([blocks(_ffn_deinterleave(g["w_up_t"], 0)), blocks(g["w_down"])], [False] * 2,
                                      "exchange_ffn_start")
        elif stage == "w_o":
            return []
        elif stage == "mla":
            d_uq = wire(jnp.transpose(g["wq"][:, :, :e_uq].reshape(HEADS, N_DEV, r_uq_dev, e_uq), (1, 0, 2, 3))
                        ).reshape(N_DEV, HEADS * r_uq_dev, e_uq)
            dense = lambda a, width: wire(a[:, :, :width]).reshape(-1, LANES)
            small = dict(g_cq=g["g_cq"], g_ckv=g["g_ckv"], w_uk=dense(g["wk"], NOPE_DIM), w_uv=dense(g["wv"], HEAD_DIM),
                         ln1_g=g["ln1_g"], ln1_b=g["ln1_b"], conv_b=_ffn_deinterleave(g["conv_b"], 1), ln2_g=g["ln2_g"],
                         ln2_b=g["ln2_b"])
            everyone = [small[n] for n in REPLICATED] + [_ffn_deinterleave(g["conv_w"], 1), g["loss"]]
            sent[stage] = _send_start([blocks(_unpad_w_o(g["w_o_mla"], g["w_o_dil"])), d_uq] + everyone,
                                      [False] * 2 + [True] * len(everyone), "exchange_mla_start")
        else:
            sent[stage] = _send_start([blocks(_split_unpad_rows(g["w_in_t"]))], [False], "exchange_w_in_start")
        return [sent[stage]["token"]]

    _, grad_x, _ = _layer_grads(x[0], loss_target[0], cw, [late["token"]], late_weights, on_grads)

    def landed(handle, after, name):
        own, got = _send_wait(handle, after, name)
        pick = lambda src, whole: src if whole else lax.dynamic_index_in_dim(src, me, 0, keepdims=False)
        return [own_slot(buf, pick(src, whole)) for buf, src, whole in zip(got, own, handle["gather"])]

    out = {}

    def update(name, parts, view=None):
        to2d = {None: lambda a: a, "t": lambda a: a.T, "heads": _heads_major}[view]
        back = {None: lambda a: a, "t": lambda a: a.T, "heads": lambda a: _heads_minor(a, HEADS)}[view]
        res = _adamw(parts, to2d(w[name]), to2d(m[name]), to2d(v[name]), "adamw_" + name)
        for kind, a in zip(("grad", "delta", "new_m", "new_v"), res):
            out[kind, name] = back(a)
        return res[0]

    r_up, r_down = landed(sent["ffn"], [grad_x], "exchange_ffn_wait")
    r_o, r_uq, *rep_all, cw_all, loss_all = landed(sent["mla"], [grad_x], "exchange_mla_wait")
    done = [update("w_up", r_up, "t"), update("w_down", r_down), update("w_o", r_o), update("w_uq", r_uq, "heads")]
    heads_major = ("w_uk", "w_uv")
    two_d = lambda n, a: _heads_major(a) if n in heads_major else a.reshape(1, -1)
    rep_all = [p.reshape(N_DEV, -1, w[n].shape[2]) if n in heads_major else p for n, p in zip(REPLICATED, rep_all)]
    res, loss_sum = _adamw_small(rep_all, *[[two_d(n, d[n]) for n in REPLICATED] for d in (w, m, v)], loss_all, "adamw_replicated")
    for n, quad in zip(REPLICATED, res):
        for kind, a in zip(("grad", "delta", "new_m", "new_v"), quad):
            out[kind, n] = _heads_minor(a, HEADS) if n in heads_major else a.reshape(w[n].shape)
    loss = loss_sum[0, 0]
    ncw = conv_w.shape[1]
    done += [loss_sum, update("conv_w", lax.dynamic_slice_in_dim(cw_all[:, :conv_w.shape[0]], me * ncw, ncw, axis=2))]
    (r_in,) = landed(sent["w_in"], done, "exchange_w_in_wait")
    update("w_in", r_in, "t")

    return (loss, grad_x[None], *[out[kind, n] for kind in ("grad", "delta", "new_m", "new_v") for n in ALL_WEIGHTS])
```

```python
import math

import jax
import jax.numpy as jnp
import numpy as np
from jax import lax
from jax.experimental import pallas as pl
from jax.experimental.pallas import tpu as pltpu

F32 = jnp.float32
MXU_DTYPE = jnp.bfloat16
WIRE_DTYPE = jnp.bfloat16

N_DEV = 8
D_MODEL = 1024
HEADS = 8
HEAD_DIM = 64
LANES = 128
Q_RANK, KV_RANK, ROPE_DIM, NOPE_DIM = 256, 128, 32, 64
DIL_WIDTH = HEADS * HEAD_DIM
MLA_WIDTH = HEADS * HEAD_DIM
IN_PAD = 2048
DH_PART = IN_PAD // 4
D_FF = 2816
ROPE_THETA = 10000.0
DIL_PAIRS = ((128, 1), (512, 4), (2048, 16))
DIL_BLOCK = 128
DN_ALPHA = 2.0 ** 0.25
LN_EPS = 1e-5
RMS_EPS = 1e-6
ONES_LANE = HEAD_DIM
MLA_SCALE = 1.0 / math.sqrt(NOPE_DIM + ROPE_DIM)
MLA_SCALE_LOG2 = MLA_SCALE * math.log2(math.e)
MLA_BWD_SPLITS = 2
MLA_BWD_SPLITS_DIAGONAL = 4
MLA_FWD_SPLITS_DIAGONAL = 2
DIL_SCALE = 1.0 / math.sqrt(HEAD_DIM)
ALIBI_SLOPES = tuple(2.0 ** (-8.0 * (h + 1) / HEADS) for h in range(HEADS))
NEG_BIG = -1e30
ADAM_LR, ADAM_B1, ADAM_B2, ADAM_EPS, ADAM_WD, ADAM_STEP = 0.001, 0.9, 0.999, 1e-08, 0.01, 10
VMEM_LIMIT = 48 * 1024 * 1024


def _params(*sem):
    return pltpu.CompilerParams(dimension_semantics=sem or None, vmem_limit_bytes=VMEM_LIMIT)


def _dot(a, b, ca, cb):
    return lax.dot_general(a, b, (((ca,), (cb,)), ((), ())), preferred_element_type=F32)


_ANY_SPEC = pl.BlockSpec(memory_space=pl.ANY)


def _mm(a, b, *, name, tm, tn, tk, ta=False, tb=False, out_dtype=F32, after=()):
    m, k = (a.shape[1], a.shape[0]) if ta else a.shape
    n = b.shape[0] if tb else b.shape[1]
    assert (b.shape[1] if tb else b.shape[0]) == k
    tm, tn, tk = min(tm, m), min(tn, n), min(tk, k)
    assert m % tm == 0 and n % tn == 0 and k % tk == 0, (name, m, n, k, tm, tn, tk)
    nk = k // tk
    a_spec = (pl.BlockSpec((tk, tm), lambda i, j, kk: (kk, i)) if ta
              else pl.BlockSpec((tm, tk), lambda i, j, kk: (i, kk)))
    b_mode = dict(pipeline_mode=pl.Buffered(1)) if (tn == n and tk == k) else {}
    b_spec = (pl.BlockSpec((tn, tk), lambda i, j, kk: (j, kk), **b_mode) if tb
              else pl.BlockSpec((tk, tn), lambda i, j, kk: (kk, j), **b_mode))
    o_spec = pl.BlockSpec((tm, tn), lambda i, j, kk: (i, j))
    n_in = 2 + len(after)
    ca, cb = (0 if ta else 1), (1 if tb else 0)

    def body(*refs):
        a_ref, b_ref, o_ref = refs[0], refs[1], refs[n_in]
        part = _dot(a_ref[...].astype(MXU_DTYPE), b_ref[...].astype(MXU_DTYPE), ca, cb)
        if nk == 1:
            o_ref[...] = part.astype(o_ref.dtype)
            return
        acc_ref = refs[-1]
        kk = pl.program_id(2)

        @pl.when(kk == 0)
        def _():
            acc_ref[...] = part

        @pl.when(kk > 0)
        def _():
            acc_ref[...] += part

        @pl.when(kk == nk - 1)
        def _():
            o_ref[...] = acc_ref[...].astype(o_ref.dtype)

    return pl.pallas_call(
        body, name=name, grid=(m // tm, n // tn, nk), in_specs=[a_spec, b_spec] + [_ANY_SPEC] * len(after), out_specs=o_spec,
        out_shape=jax.ShapeDtypeStruct((m, n), out_dtype),
        scratch_shapes=[pltpu.VMEM((tm, tn), F32)] if nk > 1 else [],
        compiler_params=_params("parallel", "parallel", "arbitrary"),
    )(a, b, *after)


def _w_o_bwd(dz, o_mla, o_dil, w_o_mla, w_o_dil, tm=1024):
    seq, d = dz.shape
    tm = min(tm, seq)
    nstep = seq // tm

    def body(a_ref, om_ref, od_ref, wm_ref, wd_ref, dom_ref, dod_ref, dwm_ref, dwd_ref, accm_ref, accd_ref):
        step = pl.program_id(0)

        @pl.when(step == 0)
        def _():
            accm_ref[...] = jnp.zeros_like(accm_ref)
            accd_ref[...] = jnp.zeros_like(accd_ref)

        a = a_ref[...]
        for hd in range(HEADS):
            rows = slice(LANES * hd, LANES * (hd + 1))
            dom_ref[hd] = _dot(a, wm_ref[rows, :], 1, 1)
            accm_ref[rows, :] += _dot(om_ref[hd], a, 0, 0)
        dod_ref[...] = _dot(a, wd_ref[...], 1, 1)
        accd_ref[...] += _dot(od_ref[...], a, 0, 0)

        @pl.when(step == nstep - 1)
        def _():
            dwm_ref[...] = accm_ref[...].astype(dwm_ref.dtype)
            dwd_ref[...] = accd_ref[...].astype(dwd_ref.dtype)

    once = dict(pipeline_mode=pl.Buffered(1))
    return pl.pallas_call(
        body, name="w_o_bwd", grid=(nstep,),
        in_specs=[pl.BlockSpec((tm, d), lambda i: (i, 0)), pl.BlockSpec((HEADS, tm, LANES), lambda i: (0, i, 0)),
                  pl.BlockSpec((tm, DIL_WIDTH), lambda i: (i, 0)), pl.BlockSpec((HEADS * LANES, d), lambda i: (0, 0), **once),
                  pl.BlockSpec((DIL_WIDTH, d), lambda i: (0, 0), **once)],
        out_specs=[pl.BlockSpec((HEADS, tm, LANES), lambda i: (0, i, 0)), pl.BlockSpec((tm, DIL_WIDTH), lambda i: (i, 0)),
                   pl.BlockSpec((HEADS * LANES, d), lambda i: (0, 0)), pl.BlockSpec((DIL_WIDTH, d), lambda i: (0, 0))],
        out_shape=[jax.ShapeDtypeStruct((HEADS, seq, LANES), F32), jax.ShapeDtypeStruct((seq, DIL_WIDTH), F32),
                   jax.ShapeDtypeStruct((HEADS * LANES, d), MXU_DTYPE), jax.ShapeDtypeStruct((DIL_WIDTH, d), MXU_DTYPE)],
        scratch_shapes=[pltpu.VMEM((HEADS * LANES, d), F32), pltpu.VMEM((DIL_WIDTH, d), F32)],
        compiler_params=_params("arbitrary"),
    )(dz, o_mla, o_dil, w_o_mla, w_o_dil)


def _rope_tables(seq):
    half = ROPE_DIM // 2
    f32 = np.float32
    freqs = np.power(f32(ROPE_THETA), -np.arange(half, dtype=f32) / f32(half))
    ang = np.arange(seq, dtype=f32)[:, None] * freqs[None, :]
    cos, sin = np.cos(ang, dtype=f32), np.sin(ang, dtype=f32)
    one = np.ones((seq, NOPE_DIM), f32)
    tail = np.ones((seq, LANES - NOPE_DIM - ROPE_DIM), f32)
    ctab = np.concatenate([one, cos, cos, tail], axis=1)
    stab = np.concatenate([0 * one, -sin, sin, 0 * tail], axis=1)
    return jnp.asarray(ctab), jnp.asarray(stab)


def _rope_swap(t):
    lane = lax.broadcasted_iota(jnp.int32, t.shape, 1)
    half = ROPE_DIM // 2
    return jnp.where(lane < NOPE_DIM + half, pltpu.roll(t, LANES - half, 1), pltpu.roll(t, half, 1))


def _rope(t, ctab, stab):
    return t * ctab + _rope_swap(t) * stab


def _rope_inv(t, ctab, stab):
    return t * ctab - _rope_swap(t) * stab


def _rms(x, g):
    r = lax.rsqrt(jnp.mean(x * x, axis=-1, keepdims=True) + RMS_EPS)
    xh = x * r
    return xh, r, xh * g


def _mla_prep(h, g_cq, g_ckv, wq, wk, wv, ctab, stab, tm=512):
    seq = h.shape[0]
    tm = min(tm, seq)

    def body(h_ref, gq_ref, gk_ref, wq_ref, wk_ref, wv_ref, c_ref, s_ref, q_out, k_out, v_out):
        hb = h_ref[...]
        ctab_, stab_ = c_ref[...], s_ref[...]
        _, _, cqn = _rms(hb[:, :Q_RANK], gq_ref[...])
        _, _, ckn = _rms(hb[:, Q_RANK:Q_RANK + KV_RANK], gk_ref[...])
        cqn = cqn.astype(MXU_DTYPE)
        ckn = ckn.astype(MXU_DTYPE)
        krr = _rope(hb[:, Q_RANK + KV_RANK:], ctab_, stab_)
        ones_lane = (lax.broadcasted_iota(jnp.int32, (1, LANES), 1) == ONES_LANE).astype(F32)
        for hd in range(HEADS):
            q = _dot(cqn, wq_ref[hd], 1, 0)
            q_out[hd] = _rope(q, ctab_, stab_).astype(q_out.dtype)
            k_out[hd] = (_dot(ckn, wk_ref[hd], 1, 0) + krr).astype(k_out.dtype)
            v_out[hd] = (_dot(ckn, wv_ref[hd], 1, 0) + ones_lane).astype(v_out.dtype)

    full = lambda *shape: pl.BlockSpec(shape, lambda i: (0,) * len(shape))
    slab = pl.BlockSpec((HEADS, tm, LANES), lambda i: (0, i, 0))
    shp = jax.ShapeDtypeStruct((HEADS, seq, LANES), MXU_DTYPE)
    return pl.pallas_call(
        body, name="mla_prep", grid=(seq // tm,),
        in_specs=[pl.BlockSpec((tm, DH_PART), lambda i: (i, 0)), full(1, Q_RANK), full(1, KV_RANK),
                  full(HEADS, Q_RANK, LANES), full(HEADS, KV_RANK, LANES), full(HEADS, KV_RANK, LANES),
                  pl.BlockSpec((tm, LANES), lambda i: (i, 0)), pl.BlockSpec((tm, LANES), lambda i: (i, 0))],
        out_specs=[slab, slab, slab], out_shape=[shp, shp, shp],
        compiler_params=_params("parallel"),
    )(h, g_cq, g_ckv, wq, wk, wv, ctab, stab)


def _mla_prep_bwd(h, g_cq, g_ckv, wq, wk, wv, ctab, stab, dq, dk, dv, tm=512, after=()):
    seq = h.shape[0]
    tm = min(tm, seq)
    n_after = len(after)

    def body(h_ref, gq_ref, gk_ref, wq_ref, wk_ref, wv_ref, c_ref, s_ref, dq_ref, dk_ref, dv_ref, *rest):
        dh_ref, dwq_ref, dwk_ref, dwv_ref, dgq_ref, dgk_ref = rest[n_after:]

        @pl.when(pl.program_id(0) == 0)
        def _():
            for r in (dwq_ref, dwk_ref, dwv_ref, dgq_ref, dgk_ref):
                r[...] = jnp.zeros_like(r)

        hb = h_ref[...]
        ctab_, stab_ = c_ref[...], s_ref[...]
        gq, gk = gq_ref[...], gk_ref[...]
        xq, rq, cqn = _rms(hb[:, :Q_RANK], gq)
        xk, rk, ckn = _rms(hb[:, Q_RANK:Q_RANK + KV_RANK], gk)
        cqn = cqn.astype(MXU_DTYPE)
        ckn = ckn.astype(MXU_DTYPE)
        d_cqn = jnp.zeros((tm, Q_RANK), F32)
        d_ckn = jnp.zeros((tm, KV_RANK), F32)
        d_krr = jnp.zeros((tm, LANES), F32)
        for hd in range(HEADS):
            dqh = _rope_inv(dq_ref[hd], ctab_, stab_).astype(MXU_DTYPE)
            d_cqn += _dot(dqh, wq_ref[hd], 1, 1)
            dwq_ref[hd] += _dot(cqn, dqh, 0, 0)
            dkh = dk_ref[hd]
            d_krr += dkh
            dkh = dkh.astype(MXU_DTYPE)
            d_ckn += _dot(dkh, wk_ref[hd], 1, 1)
            dwk_ref[hd] += _dot(ckn, dkh, 0, 0)
            dvh = dv_ref[hd].astype(MXU_DTYPE)
            d_ckn += _dot(dvh, wv_ref[hd], 1, 1)
            dwv_ref[hd] += _dot(ckn, dvh, 0, 0)
        lane = lax.broadcasted_iota(jnp.int32, (tm, LANES), 1)
        rot = (lane >= NOPE_DIM) & (lane < NOPE_DIM + ROPE_DIM)
        d_kr = jnp.where(rot, _rope_inv(jnp.where(rot, d_krr, 0.0), ctab_, stab_), 0.0)

        def rms_bwd(dy, xh, r, g, dg_ref):
            dg_ref[...] += jnp.sum(dy * xh, axis=0, keepdims=True)
            dxh = dy * g
            return r * (dxh - xh * jnp.mean(dxh * xh, axis=-1, keepdims=True))

        d_cq = rms_bwd(d_cqn, xq, rq, gq, dgq_ref)
        d_ck = rms_bwd(d_ckn, xk, rk, gk, dgk_ref)
        dh_ref[...] = jnp.concatenate([d_cq, d_ck, d_kr], axis=1).astype(dh_ref.dtype)

    full = lambda *shape: pl.BlockSpec(shape, lambda i: (0,) * len(shape))
    slab = pl.BlockSpec((HEADS, tm, LANES), lambda i: (0, i, 0))
    return pl.pallas_call(
        body, name="mla_prep_bwd", grid=(seq // tm,),
        in_specs=[pl.BlockSpec((tm, DH_PART), lambda i: (i, 0)), full(1, Q_RANK), full(1, KV_RANK),
                  full(HEADS, Q_RANK, LANES), full(HEADS, KV_RANK, LANES), full(HEADS, KV_RANK, LANES),
                  pl.BlockSpec((tm, LANES), lambda i: (i, 0)), pl.BlockSpec((tm, LANES), lambda i: (i, 0)),
                  slab, slab, slab] + [_ANY_SPEC] * n_after,
        out_specs=[pl.BlockSpec((tm, DH_PART), lambda i: (i, 0)), full(HEADS, Q_RANK, LANES), full(HEADS, KV_RANK, LANES),
                   full(HEADS, KV_RANK, LANES), full(1, Q_RANK), full(1, KV_RANK)],
        out_shape=[jax.ShapeDtypeStruct((seq, DH_PART), MXU_DTYPE), jax.ShapeDtypeStruct((HEADS, Q_RANK, LANES), F32),
                   jax.ShapeDtypeStruct((HEADS, KV_RANK, LANES), F32), jax.ShapeDtypeStruct((HEADS, KV_RANK, LANES), F32),
                   jax.ShapeDtypeStruct((1, Q_RANK), F32), jax.ShapeDtypeStruct((1, KV_RANK), F32)],
        compiler_params=_params("arbitrary"),
    )(h, g_cq, g_ckv, wq, wk, wv, ctab, stab, dq, dk, dv, *after)


def _mla_attn_fwd(q, k, v, t=1024):
    _, seq, _ = q.shape
    t = min(t, seq)

    def body(q_ref, k_ref, v_ref, o_ref, ob_ref, lse_ref, m_ref, acc_ref, s_ref):
        i = pl.program_id(1)
        qb = q_ref[...]
        m_ref[...] = jnp.full_like(m_ref, NEG_BIG)
        acc_ref[...] = jnp.zeros_like(acc_ref)

        def scores(j):
            return _dot(qb, k_ref[pl.ds(pl.multiple_of(j * t, t), t), :], 1, 1) * MLA_SCALE_LOG2

        def softmax_pv(j, s, rows=slice(None), mask=None):
            vb = v_ref[pl.ds(pl.multiple_of(j * t, t), s.shape[1]), :]
            if mask is not None:
                s = jnp.where(mask, s, NEG_BIG)
            m_old = m_ref[rows, :]
            m_new = jnp.maximum(m_old, jnp.max(s, axis=1, keepdims=True))
            p = jnp.exp2(s - m_new)
            a = jnp.exp2(m_old - m_new)
            acc_ref[rows, :] = a * acc_ref[rows, :] + _dot(p.astype(MXU_DTYPE), vb, 1, 0)
            m_ref[rows, :] = m_new

        def softmax_pv_diagonal(j):
            th = t // MLA_FWD_SPLITS_DIAGONAL
            for hf in range(MLA_FWD_SPLITS_DIAGONAL):
                nk = (hf + 1) * th
                row = lax.broadcasted_iota(jnp.int32, (th, nk), 0) + hf * th
                rows = slice(hf * th, (hf + 1) * th)
                softmax_pv(j, s_ref[rows, 0:nk], rows, row >= lax.broadcasted_iota(jnp.int32, (th, nk), 1))

        s_ref[...] = scores(0)

        def loop_body(j, c):
            s_next = scores(j + 1)
            softmax_pv(j, s_ref[...])
            s_ref[...] = s_next
            return c

        lax.fori_loop(0, i, loop_body, 0)
        softmax_pv_diagonal(i)
        acc = acc_ref[...]
        l = acc[:, ONES_LANE:ONES_LANE + 1]
        o = jnp.where(lax.broadcasted_iota(jnp.int32, acc.shape, 1) < HEAD_DIM, acc * (1.0 / l), 0.0)
        o_ref[...] = o
        ob_ref[...] = o.astype(ob_ref.dtype)
        lse_ref[...] = jnp.broadcast_to(m_ref[...] + jnp.log2(l), lse_ref.shape)

    blk = pl.BlockSpec((None, t, LANES), lambda h, i: (h, i, 0))
    whole = pl.BlockSpec((None, seq, LANES), lambda h, i: (h, 0, 0))
    shp = jax.ShapeDtypeStruct((HEADS, seq, LANES), F32)
    return pl.pallas_call(
        body, name="mla_attn_fwd", grid=(HEADS, seq // t),
        in_specs=[blk, whole, whole], out_specs=[blk, blk, blk],
        out_shape=[shp, jax.ShapeDtypeStruct((HEADS, seq, LANES), MXU_DTYPE), shp],
        scratch_shapes=[pltpu.VMEM((t, 1), F32), pltpu.VMEM((t, LANES), F32), pltpu.VMEM((t, t), F32)],
        compiler_params=_params("parallel", "arbitrary"),
    )(q, k, v)


def _mla_attn_bwd(q, k, v, o, lse, do, t=1024):
    _, seq, _ = q.shape
    t = min(t, seq)
    nb = seq // t

    def body(q_ref, k_ref, v_ref, o_ref, lse_ref, do_ref, dq_ref, dk_ref, dv_ref, dl_ref, dka_ref, dva_ref):
        dq_ref[...] = jnp.zeros_like(dq_ref)

        def delta_body(i, c):
            rows = pl.ds(pl.multiple_of(i * t, t), t)
            dl_ref[rows, :] = jnp.sum(do_ref[rows, :] * o_ref[rows, :], axis=1, keepdims=True)
            return c

        lax.fori_loop(0, nb, delta_body, 0)

        def kblock(j, c):
            krows = pl.ds(pl.multiple_of(j * t, t), t)
            kb = k_ref[krows, :]
            vb = v_ref[krows, :]
            dka_ref[...] = jnp.zeros_like(dka_ref)
            dva_ref[...] = jnp.zeros_like(dva_ref)

            def qstep(i, masked):
                ns = MLA_BWD_SPLITS_DIAGONAL if masked else MLA_BWD_SPLITS
                th = t // ns
                rows = [pl.ds(pl.multiple_of(i * t + hf * th, th), th) for hf in range(ns)]
                qs = [q_ref[r, :] for r in rows]
                dos = [do_ref[r, :].astype(MXU_DTYPE) for r in rows]
                nkeys = [(hf + 1) * th if masked else t for hf in range(ns)]
                ss = [_dot(qs[hf], kb[:nkeys[hf]], 1, 1) * MLA_SCALE_LOG2 for hf in range(ns)]
                dps = [_dot(dos[hf], vb[:nkeys[hf]], 1, 1) for hf in range(ns)]
                for hf in range(ns):
                    s, nk = ss[hf], nkeys[hf]
                    if masked:
                        row = lax.broadcasted_iota(jnp.int32, (th, nk), 0) + hf * th
                        s = jnp.where(row >= lax.broadcasted_iota(jnp.int32, (th, nk), 1), s, NEG_BIG)
                    p = jnp.exp2(s - lse_ref[rows[hf], 0:1])
                    dva_ref[0:nk, :] += _dot(p.astype(MXU_DTYPE), dos[hf], 0, 0)
                    ds = (p * (dps[hf] - dl_ref[rows[hf], :]) * MLA_SCALE).astype(MXU_DTYPE)
                    dka_ref[0:nk, :] += _dot(ds, qs[hf], 0, 0)
                    dq_ref[rows[hf], :] += _dot(ds, kb[:nk], 1, 0)

            qstep(j, True)

            def qloop(i, c2):
                qstep(i, False)
                return c2

            lax.fori_loop(j + 1, nb, qloop, 0)
            dk_ref[krows, :] = dka_ref[...]
            dv_ref[krows, :] = dva_ref[...]
            return c

        lax.fori_loop(0, nb, kblock, 0)

    whole = pl.BlockSpec((None, seq, LANES), lambda h: (h, 0, 0))
    shp = jax.ShapeDtypeStruct((HEADS, seq, LANES), F32)
    return pl.pallas_call(
        body, name="mla_attn_bwd", grid=(HEADS,),
        in_specs=[whole] * 6, out_specs=[whole] * 3, out_shape=[shp] * 3,
        scratch_shapes=[pltpu.VMEM((seq, 1), F32), pltpu.VMEM((t, LANES), F32), pltpu.VMEM((t, LANES), F32)],
        compiler_params=_params("parallel"),
    )(q, k, v, o, lse, do)


DIL_CHUNK = DIL_BLOCK * max(d for _, d in DIL_PAIRS)
DIL_PAIR_LANES = 2 * HEAD_DIM
assert DIL_PAIR_LANES == LANES
DIL_UNROLL_FWD = 16
DIL_UNROLL_BWD = 16


def _dil_bias_tables(hp, dil):
    b = DIL_BLOCK
    iq = lax.broadcasted_iota(jnp.int32, (b, 2 * b), 0)
    ik = lax.broadcasted_iota(jnp.int32, (b, 2 * b), 1)
    off = iq + b - ik
    band = (off >= 0) & (off <= b)
    dist = (off * dil).astype(F32)
    every, first = [], []
    for hh in range(2):
        slope = jnp.where(hp == 0, ALIBI_SLOPES[hh], jnp.where(hp == 1, ALIBI_SLOPES[2 + hh],
                          jnp.where(hp == 2, ALIBI_SLOPES[4 + hh], ALIBI_SLOPES[6 + hh]))).astype(F32)
        bias = -slope * dist
        every.append(jnp.where(band, bias, NEG_BIG))
        first.append(jnp.where(band & (ik >= b), bias, NEG_BIG))
    return jnp.concatenate(every, axis=0), jnp.concatenate(first, axis=0)


def _dil_rows(start, dil):
    return pl.ds(start, DIL_BLOCK) if dil == 1 else pl.ds(start, DIL_BLOCK, stride=dil)


def _dil_block_pos(blk, c, dil):
    sc, r = blk // dil, blk % dil
    q0 = sc * (DIL_BLOCK * dil) + r
    kcur0 = c * DIL_CHUNK + q0
    first = kcur0 < DIL_BLOCK * dil
    kprev0 = jnp.where(first, kcur0, kcur0 - DIL_BLOCK * dil)
    return q0, kcur0, kprev0, first


def _pair_cols(hh):
    return slice(HEAD_DIM * hh, HEAD_DIM * (hh + 1))


def _first_head_lanes(shape):
    return lax.broadcasted_iota(jnp.int32, shape, 1) < HEAD_DIM


def _stack_pair(t):
    first = _first_head_lanes(t.shape)
    return jnp.concatenate([jnp.where(first, t, 0.0), jnp.where(first, 0.0, t)], axis=0).astype(MXU_DTYPE)


def _unstack_pair(t):
    rows = t.shape[0] // 2
    return jnp.where(_first_head_lanes((rows, t.shape[1])), t[:rows], t[rows:])


def _pair_column(t):
    return jnp.concatenate([t[:, 0:1], t[:, HEAD_DIM:HEAD_DIM + 1]], axis=0)


def _dil_fwd(h):
    seq = h.shape[0]
    assert seq % DIL_CHUNK == 0
    nblk = DIL_CHUNK // DIL_BLOCK
    rc = 256

    def body(q_ref, k_ref, v_ref, o_ref, ob_ref, lse_ref, *scr):
        o_scr, l_scr = scr[:3], scr[3:]
        hp, c = pl.program_id(0), pl.program_id(1)
        for bi, (_, dil) in enumerate(DIL_PAIRS):
            tables = _dil_bias_tables(hp, dil)

            def block(blk, carry, bi=bi, dil=dil, tables=tables):
                q0, kcur0, kprev0, first = _dil_block_pos(blk, c, dil)
                q2 = _stack_pair(q_ref[_dil_rows(q0, dil), :] * DIL_SCALE)
                kcat = jnp.concatenate([k_ref[_dil_rows(kprev0, dil), :], k_ref[_dil_rows(kcur0, dil), :]], axis=0).astype(MXU_DTYPE)
                vcat = jnp.concatenate([v_ref[_dil_rows(kprev0, dil), :], v_ref[_dil_rows(kcur0, dil), :]], axis=0).astype(MXU_DTYPE)
                s = _dot(q2, kcat, 1, 1) + jnp.where(first, tables[1], tables[0])
                mx = jnp.max(s, axis=1, keepdims=True)
                p = jnp.exp(s - mx)
                l = jnp.sum(p, axis=1, keepdims=True)
                o_scr[bi][_dil_rows(q0, dil), :] = _unstack_pair(_dot(p.astype(MXU_DTYPE), vcat, 1, 0) * (1.0 / l))
                l_scr[bi][_dil_rows(q0, dil), :] = _unstack_pair(jnp.broadcast_to(mx + jnp.log(l), (2 * DIL_BLOCK, LANES)))
                return carry

            lax.fori_loop(0, nblk, block, 0, unroll=DIL_UNROLL_FWD)

        def combine(i, carry):
            rows = pl.ds(pl.multiple_of(i * rc, rc), rc)
            ls = [l_scr[bi][rows, :] for bi in range(3)]
            mx = jnp.maximum(jnp.maximum(ls[0], ls[1]), ls[2])
            es = [jnp.exp(l - mx) for l in ls]
            den = es[0] + es[1] + es[2]
            o = (es[0] * o_scr[0][rows, :] + es[1] * o_scr[1][rows, :] + es[2] * o_scr[2][rows, :]) / den
            o_ref[rows, :] = o
            ob_ref[rows, :] = o.astype(ob_ref.dtype)
            lse_ref[rows, :] = mx + jnp.log(den)
            return carry

        lax.fori_loop(0, DIL_CHUNK // rc, combine, 0)

    nq = DIL_WIDTH // LANES
    chunk = lambda off: pl.BlockSpec((DIL_CHUNK, LANES), lambda hp, c: (c, off + hp))
    whole = lambda off: pl.BlockSpec((seq, LANES), lambda hp, c: (0, off + hp))
    shp = jax.ShapeDtypeStruct((seq, DIL_WIDTH), F32)
    return pl.pallas_call(
        body, name="dil_fwd", grid=(nq, seq // DIL_CHUNK),
        in_specs=[chunk(nq), whole(2 * nq), whole(3 * nq)], out_specs=[chunk(0), chunk(0), chunk(0)],
        out_shape=[shp, jax.ShapeDtypeStruct((seq, DIL_WIDTH), MXU_DTYPE), shp],
        scratch_shapes=[pltpu.VMEM((DIL_CHUNK, LANES), F32)] * 6,
        compiler_params=_params("parallel", "arbitrary"),
    )(h, h, h)


def _dil_bwd(h, o, lse, do, after=()):
    seq = h.shape[0]
    nblk = DIL_CHUNK // DIL_BLOCK
    nchunk = seq // DIL_CHUNK
    rc = 256

    n_after = len(after)

    def body(q_ref, k_ref, v_ref, o_ref, lse_ref, do_ref, *rest):
        dq_out, dk_out, dv_out, dl_scr, dq_ref, dk_ref, dv_ref = rest[n_after:]
        hp, c = pl.program_id(0), pl.program_id(1)

        @pl.when(c == 0)
        def _():
            dk_ref[...] = jnp.zeros_like(dk_ref)
            dv_ref[...] = jnp.zeros_like(dv_ref)

        def delta(i, carry):
            rows = pl.ds(pl.multiple_of(i * rc, rc), rc)
            prod = do_ref[rows, :] * o_ref[rows, :]
            dl_scr[rows, :] = jnp.concatenate(
                [jnp.broadcast_to(jnp.sum(prod[:, _pair_cols(hh)], axis=1, keepdims=True), (rc, HEAD_DIM)) for hh in range(2)], axis=1)
            return carry

        lax.fori_loop(0, DIL_CHUNK // rc, delta, 0)

        for bi, (_, dil) in enumerate(DIL_PAIRS):
            tables = _dil_bias_tables(hp, dil)

            def block(blk, carry, bi=bi, dil=dil, tables=tables):
                q0, kcur0, kprev0, first = _dil_block_pos(blk, c, dil)
                qrows = _dil_rows(q0, dil)
                q2 = _stack_pair(q_ref[qrows, :] * DIL_SCALE)
                kcat = jnp.concatenate([k_ref[_dil_rows(kprev0, dil), :], k_ref[_dil_rows(kcur0, dil), :]], axis=0).astype(MXU_DTYPE)
                vcat = jnp.concatenate([v_ref[_dil_rows(kprev0, dil), :], v_ref[_dil_rows(kcur0, dil), :]], axis=0).astype(MXU_DTYPE)
                do2 = _stack_pair(do_ref[qrows, :])
                s = _dot(q2, kcat, 1, 1) + jnp.where(first, tables[1], tables[0])
                p = jnp.exp(s - _pair_column(lse_ref[qrows, :]))
                dp = _dot(do2, vcat, 1, 1)
                ds = (p * (dp - _pair_column(dl_scr[qrows, :]))).astype(MXU_DTYPE)
                dq_b = _unstack_pair(_dot(ds, kcat, 1, 0)) * DIL_SCALE
                dk_b = _dot(ds, q2, 0, 0)
                dv_b = _dot(p.astype(MXU_DTYPE), do2, 0, 0)
                if bi == 0:
                    dq_ref[qrows, :] = dq_b
                else:
                    dq_ref[qrows, :] += dq_b
                dk_ref[_dil_rows(kprev0, dil), :] += dk_b[:DIL_BLOCK]
                dv_ref[_dil_rows(kprev0, dil), :] += dv_b[:DIL_BLOCK]
                dk_ref[_dil_rows(kcur0, dil), :] += dk_b[DIL_BLOCK:]
                dv_ref[_dil_rows(kcur0, dil), :] += dv_b[DIL_BLOCK:]
                return carry

            lax.fori_loop(0, nblk, block, 0, unroll=DIL_UNROLL_BWD)

        dq_out[...] = dq_ref[...].astype(dq_out.dtype)

        @pl.when(c == nchunk - 1)
        def _():
            dk_out[...] = dk_ref[...].astype(dk_out.dtype)
            dv_out[...] = dv_ref[...].astype(dv_out.dtype)

    nq = DIL_WIDTH // LANES
    chunk = lambda off: pl.BlockSpec((DIL_CHUNK, LANES), lambda hp, c: (c, off + hp))
    whole = lambda off: pl.BlockSpec((seq, LANES), lambda hp, c: (0, off + hp))
    shp = jax.ShapeDtypeStruct((seq, DIL_WIDTH), MXU_DTYPE)
    return pl.pallas_call(
        body, name="dil_bwd", grid=(nq, nchunk),
        in_specs=[chunk(nq), whole(2 * nq), whole(3 * nq), chunk(0), chunk(0), chunk(0)] + [_ANY_SPEC] * n_after,
        out_specs=[chunk(0), whole(0), whole(0)], out_shape=[shp, shp, shp],
        scratch_shapes=[pltpu.VMEM((DIL_CHUNK, LANES), F32), pltpu.VMEM((DIL_CHUNK, LANES), F32),
                        pltpu.VMEM((seq, LANES), F32), pltpu.VMEM((seq, LANES), F32)],
        compiler_params=_params("parallel", "arbitrary"),
    )(h, h, h, o, lse, do, *after)


def _mm_dx0(parts, w_in_t, res, tm=1024, after=()):
    seq, d = res.shape
    tm = min(tm, seq)
    n_after = len(after)

    def body(a0, a1, a2, a3, b_ref, r_ref, *rest):
        o_ref = rest[n_after]
        acc = _dot(a0[...], b_ref[0:DH_PART, :], 1, 0)
        for c, a in enumerate((a1, a2, a3), start=1):
            acc += _dot(a[...], b_ref[DH_PART * c:DH_PART * (c + 1), :], 1, 0)
        o_ref[...] = acc + DN_ALPHA * r_ref[...]

    blk = pl.BlockSpec((tm, DH_PART), lambda i: (i, 0))
    row = pl.BlockSpec((tm, d), lambda i: (i, 0))
    return pl.pallas_call(
        body, name="mm_dx0", grid=(seq // tm,),
        in_specs=[blk] * 4 + [pl.BlockSpec((IN_PAD, d), lambda i: (0, 0), pipeline_mode=pl.Buffered(1)), row] + [_ANY_SPEC] * n_after,
        out_specs=row, out_shape=jax.ShapeDtypeStruct((seq, d), F32), compiler_params=_params("parallel"),
    )(*parts, w_in_t, res, *after)


def _mm_dw_in(parts, x0, tk=1024):
    seq, d = x0.shape
    tk = min(tk, seq)
    nk = seq // tk

    def body(a0, a1, a2, a3, b_ref, o_ref, acc_ref):
        kk = pl.program_id(0)

        @pl.when(kk == 0)
        def _():
            acc_ref[...] = jnp.zeros_like(acc_ref)

        b = b_ref[...].astype(MXU_DTYPE)
        for c, a in enumerate((a0, a1, a2, a3)):
            acc_ref[DH_PART * c:DH_PART * (c + 1), :] += _dot(a[...], b, 0, 0)

        @pl.when(kk == nk - 1)
        def _():
            o_ref[...] = acc_ref[...].astype(o_ref.dtype)

    blk = pl.BlockSpec((tk, DH_PART), lambda kk: (kk, 0))
    return pl.pallas_call(
        body, name="mm_dw_in", grid=(nk,), in_specs=[blk] * 4 + [pl.BlockSpec((tk, d), lambda kk: (kk, 0))],
        out_specs=pl.BlockSpec((IN_PAD, d), lambda kk: (0, 0)), out_shape=jax.ShapeDtypeStruct((IN_PAD, d), MXU_DTYPE),
        scratch_shapes=[pltpu.VMEM((IN_PAD, d), F32)], compiler_params=_params("arbitrary"),
    )(*parts, x0)


def _ln_stats(z):
    mu = jnp.mean(z, axis=-1, keepdims=True)
    zc = z - mu
    r = lax.rsqrt(jnp.mean(zc * zc, axis=-1, keepdims=True) + LN_EPS)
    return zc * r, r


def _ln_bwd_math(dy, xh, r, g):
    dxh = dy * g
    return r * (dxh - jnp.mean(dxh, axis=-1, keepdims=True) - xh * jnp.mean(dxh * xh, axis=-1, keepdims=True))


def _mix_ln1(o_mla, o_dil, w_o_mla, w_o_dil, x0, g, b, tm=512):
    seq, d = x0.shape
    tm = min(tm, seq)

    def body(om_ref, od_ref, wm_ref, wd_ref, x_ref, g_ref, b_ref, z_ref, y_ref, yb_ref):
        mix = _dot(od_ref[...], wd_ref[...], 1, 0)
        for hd in range(HEADS):
            mix += _dot(om_ref[hd], wm_ref[LANES * hd:LANES * (hd + 1), :], 1, 0)
        z = DN_ALPHA * x_ref[...] + mix
        xh, _ = _ln_stats(z)
        y = xh * g_ref[...] + b_ref[...]
        z_ref[...] = z
        y_ref[...] = y
        yb_ref[...] = y.astype(yb_ref.dtype)

    blk = pl.BlockSpec((tm, d), lambda i: (i, 0))
    vec = pl.BlockSpec((1, d), lambda i: (0, 0))
    shp = jax.ShapeDtypeStruct((seq, d), F32)
    return pl.pallas_call(
        body, name="mix_ln1", grid=(seq // tm,),
        in_specs=[pl.BlockSpec((HEADS, tm, LANES), lambda i: (0, i, 0)), pl.BlockSpec((tm, DIL_WIDTH), lambda i: (i, 0)),
                  pl.BlockSpec((HEADS * LANES, d), lambda i: (0, 0)), pl.BlockSpec((DIL_WIDTH, d), lambda i: (0, 0)), blk, vec, vec],
        out_specs=[blk, blk, blk], out_shape=[shp, shp, jax.ShapeDtypeStruct((seq, d), MXU_DTYPE)],
        compiler_params=_params("parallel"))(o_mla, o_dil, w_o_mla, w_o_dil, x0, g, b)


def _dx1_ln1_bwd(du, w_up_t, dz2, z, g, tm=256, after=()):
    seq, d = z.shape
    kdim = du.shape[1]
    tm = min(tm, seq)
    n_after = len(after)

    def body(du_ref, w_ref, r_ref, z_ref, g_ref, *rest):
        dz_ref, dzb_ref, dg_ref, db_ref = rest[n_after:]

        @pl.when(pl.program_id(0) == 0)
        def _():
            dg_ref[...] = jnp.zeros_like(dg_ref)
            db_ref[...] = jnp.zeros_like(db_ref)

        dyb = _dot(du_ref[...], w_ref[...], 1, 0) + DN_ALPHA * r_ref[...]
        xh, r = _ln_stats(z_ref[...])
        dg_ref[...] += jnp.sum(dyb * xh, axis=0, keepdims=True)
        db_ref[...] += jnp.sum(dyb, axis=0, keepdims=True)
        dz = _ln_bwd_math(dyb, xh, r, g_ref[...])
        dz_ref[...] = dz
        dzb_ref[...] = dz.astype(dzb_ref.dtype)

    blk = pl.BlockSpec((tm, d), lambda i: (i, 0))
    vec = pl.BlockSpec((1, d), lambda i: (0, 0))
    return pl.pallas_call(
        body, name="dx1_ln1_bwd", grid=(seq // tm,),
        in_specs=[pl.BlockSpec((tm, kdim), lambda i: (i, 0)),
                  pl.BlockSpec((kdim, d), lambda i: (0, 0), pipeline_mode=pl.Buffered(1)), blk, blk, vec] + [_ANY_SPEC] * n_after,
        out_specs=[blk, blk, vec, vec],
        out_shape=[jax.ShapeDtypeStruct((seq, d), F32), jax.ShapeDtypeStruct((seq, d), MXU_DTYPE),
                   jax.ShapeDtypeStruct((1, d), F32), jax.ShapeDtypeStruct((1, d), F32)],
        compiler_params=_params("arbitrary"))(du, w_up_t, dz2, z, g, *after)


def _down_ln2_loss_bwd(act, w_down, x1, target, g, b, tm=512):
    seq, d = x1.shape
    kdim = act.shape[1]
    tm = min(tm, seq)

    def body(a_ref, w_ref, x_ref, t_ref, g_ref, b_ref, dz_ref, dzb_ref, loss_ref, dg_ref, db_ref):
        @pl.when(pl.program_id(0) == 0)
        def _():
            loss_ref[...] = jnp.zeros_like(loss_ref)
            dg_ref[...] = jnp.zeros_like(dg_ref)
            db_ref[...] = jnp.zeros_like(db_ref)

        gv = g_ref[...]
        z = DN_ALPHA * x_ref[...] + _dot(a_ref[...], w_ref[...], 1, 0)
        xh, r = _ln_stats(z)
        err = (xh * gv + b_ref[...]) - t_ref[...]
        loss_ref[...] += 0.5 * jnp.sum(jnp.mean(err * err, axis=-1, keepdims=True), axis=0, keepdims=True)
        dy = err * (1.0 / d)
        dg_ref[...] += jnp.sum(dy * xh, axis=0, keepdims=True)
        db_ref[...] += jnp.sum(dy, axis=0, keepdims=True)
        dz = _ln_bwd_math(dy, xh, r, gv)
        dz_ref[...] = dz
        dzb_ref[...] = dz.astype(dzb_ref.dtype)

    blk = pl.BlockSpec((tm, d), lambda i: (i, 0))
    vec = pl.BlockSpec((1, d), lambda i: (0, 0))
    return pl.pallas_call(
        body, name="down_ln2_loss_bwd", grid=(seq // tm,),
        in_specs=[pl.BlockSpec((tm, kdim), lambda i: (i, 0)),
                  pl.BlockSpec((kdim, d), lambda i: (0, 0), pipeline_mode=pl.Buffered(1)), blk, blk, vec, vec],
        out_specs=[blk, blk, pl.BlockSpec((1, LANES), lambda i: (0, 0)), vec, vec],
        out_shape=[jax.ShapeDtypeStruct((seq, d), F32), jax.ShapeDtypeStruct((seq, d), MXU_DTYPE),
                   jax.ShapeDtypeStruct((1, LANES), F32),
                   jax.ShapeDtypeStruct((1, d), F32), jax.ShapeDtypeStruct((1, d), F32)],
        compiler_params=_params("arbitrary"))(act, w_down, x1, target, g, b)


HALO = 16


def _conv_rows(e, w_ref, b_ref):
    y = b_ref[...] + w_ref[0:1, :] * pltpu.roll(e, 2, 0)
    y = y + w_ref[1:2, :] * pltpu.roll(e, 1, 0)
    return y + w_ref[2:3, :] * e


_GELU_C = math.sqrt(2.0 / math.pi)
_GELU_A = 0.044715


def _gelu(x):
    return x * (0.5 + 0.5 * jnp.tanh(x * (_GELU_C + (_GELU_C * _GELU_A) * (x * x))))


CONV_TN = 256


def _ffn_interleave(a, axis):
    shp = a.shape
    a = a.reshape(shp[:axis] + (2, D_FF // CONV_TN, CONV_TN) + shp[axis + 1:])
    return jnp.swapaxes(a, axis, axis + 1).reshape(shp)


def _ffn_deinterleave(a, axis):
    shp = a.shape
    a = a.reshape(shp[:axis] + (D_FF // CONV_TN, 2, CONV_TN) + shp[axis + 1:])
    return jnp.swapaxes(a, axis, axis + 1).reshape(shp)


def _up_conv_gate_fwd(x1, w_up_t, conv_w, conv_b, tm=1024):
    seq, d = x1.shape
    tm = min(tm, seq)
    tn = CONV_TN

    def body(x_ref, xp_ref, wu_ref, w_ref, b_ref, u_ref, y_ref, o_ref):
        first = pl.program_id(0) == 0
        halo = jnp.where(first, jnp.zeros_like(xp_ref), xp_ref[...])
        e = _dot(jnp.concatenate([halo, x_ref[...]], axis=0), wu_ref[...], 1, 1)
        u_ref[...] = e[HALO:]
        y = _conv_rows(e, w_ref, b_ref)[HALO:]
        y_ref[...] = y
        o_ref[...] = (_gelu(y[:, tn:]) * y[:, :tn]).astype(o_ref.dtype)

    hb = tm // HALO
    return pl.pallas_call(
        body, name="up_conv_gate_fwd", grid=(seq // tm, D_FF // tn),
        in_specs=[pl.BlockSpec((tm, d), lambda i, j: (i, 0)),
                  pl.BlockSpec((HALO, d), lambda i, j: (jnp.maximum(i * hb - 1, 0), 0)),
                  pl.BlockSpec((2 * tn, d), lambda i, j: (j, 0)),
                  pl.BlockSpec((3, 2 * tn), lambda i, j: (0, j)), pl.BlockSpec((1, 2 * tn), lambda i, j: (0, j))],
        out_specs=[pl.BlockSpec((tm, 2 * tn), lambda i, j: (i, j)), pl.BlockSpec((tm, 2 * tn), lambda i, j: (i, j)),
                   pl.BlockSpec((tm, tn), lambda i, j: (i, j))],
        out_shape=[jax.ShapeDtypeStruct((seq, 2 * D_FF), F32), jax.ShapeDtypeStruct((seq, 2 * D_FF), F32),
                   jax.ShapeDtypeStruct((seq, D_FF), MXU_DTYPE)],
        compiler_params=_params("parallel", "arbitrary"),
    )(x1, x1, w_up_t, conv_w, conv_b)


def _conv_gate_bwd(u, y, d_act, conv_w, tm=1024):
    seq = u.shape[0]
    tm = min(tm, seq)
    tn = CONV_TN
    ni = seq // tm
    rows = tm + HALO

    def body(u_ref, y_ref, yn_ref, da_ref, dan_ref, w_ref, du_ref, dw_ref, db_ref):
        i = pl.program_id(1)
        last = i == ni - 1

        @pl.when(i == 0)
        def _():
            dw_ref[...] = jnp.zeros_like(dw_ref)
            db_ref[...] = jnp.zeros_like(db_ref)

        yy = jnp.concatenate([y_ref[...], yn_ref[...]], axis=0)
        ya, yg = yy[:, :tn], yy[:, tn:]
        dact = jnp.concatenate([da_ref[...].astype(F32), jnp.where(last, 0.0, dan_ref[...].astype(F32))], axis=0)
        yg2 = yg * yg
        th = jnp.tanh(yg * (_GELU_C + (_GELU_C * _GELU_A) * yg2))
        half = 0.5 + 0.5 * th
        gelu = yg * half
        gelu_grad = half + gelu * (1.0 - half) * (2.0 * _GELU_C + (6.0 * _GELU_C * _GELU_A) * yg2)
        dy = jnp.concatenate([dact * gelu, dact * ya * gelu_grad], axis=1)
        dy0, dy1, dy2 = dy[:tm], pltpu.roll(dy, rows - 1, 0)[:tm], pltpu.roll(dy, rows - 2, 0)[:tm]
        du_ref[...] = (w_ref[2:3, :] * dy0 + w_ref[1:2, :] * dy1 + w_ref[0:1, :] * dy2).astype(du_ref.dtype)
        ut = u_ref[...]
        dw_ref[0:1, :] += jnp.sum(dy2 * ut, axis=0, keepdims=True)
        dw_ref[1:2, :] += jnp.sum(dy1 * ut, axis=0, keepdims=True)
        dw_ref[2:3, :] += jnp.sum(dy0 * ut, axis=0, keepdims=True)
        db_ref[...] += jnp.sum(dy0, axis=0, keepdims=True)

    hb = tm // HALO
    nh = seq // HALO
    nxt = lambda j, i: (jnp.minimum((i + 1) * hb, nh - 1), j)
    tile = pl.BlockSpec((tm, 2 * tn), lambda j, i: (i, j))
    return pl.pallas_call(
        body, name="conv_gate_bwd", grid=(D_FF // tn, ni),
        in_specs=[tile, tile, pl.BlockSpec((HALO, 2 * tn), nxt),
                  pl.BlockSpec((tm, tn), lambda j, i: (i, j)), pl.BlockSpec((HALO, tn), nxt),
                  pl.BlockSpec((3, 2 * tn), lambda j, i: (0, j))],
        out_specs=[tile, pl.BlockSpec((3, 2 * tn), lambda j, i: (0, j)), pl.BlockSpec((1, 2 * tn), lambda j, i: (0, j))],
        out_shape=[jax.ShapeDtypeStruct((seq, 2 * D_FF), MXU_DTYPE), jax.ShapeDtypeStruct((3, 2 * D_FF), F32),
                   jax.ShapeDtypeStruct((1, 2 * D_FF), F32)],
        compiler_params=_params("parallel", "arbitrary"),
    )(u, y, y, d_act, d_act, conv_w)


def _pad_heads(w, width):
    w = jnp.transpose(w, (1, 0, 2))
    return jnp.pad(w, ((0, 0), (0, 0), (0, LANES - width))).astype(MXU_DTYPE)


def _heads_major(a):
    return jnp.transpose(a, (1, 0, 2)).reshape(-1, a.shape[2])


def _heads_minor(a, heads):
    return jnp.transpose(a.reshape(heads, -1, a.shape[1]), (1, 0, 2))


_LATENT = Q_RANK + KV_RANK
_ROPE_AT = _LATENT + NOPE_DIM
_ROPE_END = _ROPE_AT + ROPE_DIM


def _split_pad_rows(w_t):
    z = lambda n: jnp.zeros((n, w_t.shape[1]), w_t.dtype)
    return jnp.concatenate([w_t[:_LATENT], z(_ROPE_AT - _LATENT), w_t[_LATENT:_LATENT + ROPE_DIM], z(DH_PART - _ROPE_END),
                            w_t[_LATENT + ROPE_DIM:]], axis=0)


def _split_unpad_rows(w_p):
    return jnp.concatenate([w_p[:_LATENT], w_p[_ROPE_AT:_ROPE_END], w_p[DH_PART:]], axis=0)


def _pad_w_o(w_o):
    mla = jnp.pad(w_o[:MLA_WIDTH].reshape(HEADS, HEAD_DIM, D_MODEL), ((0, 0), (0, LANES - HEAD_DIM), (0, 0)))
    return mla.reshape(HEADS * LANES, D_MODEL).astype(MXU_DTYPE), w_o[MLA_WIDTH:].astype(MXU_DTYPE)


def _unpad_w_o(d_mla, d_dil):
    return jnp.concatenate([d_mla.reshape(HEADS, LANES, D_MODEL)[:, :HEAD_DIM].reshape(MLA_WIDTH, D_MODEL), d_dil], axis=0)


def _row(v):
    return v.reshape(1, -1).astype(F32)


def _layer_grads(x0, target, cw, first_after=(), late_weights=None, on_grads=None):
    seq = x0.shape[0]
    ctab, stab = _rope_tables(seq)
    gq, gk = cw["g_cq"], cw["g_ckv"]
    wq, wk, wv = cw["wq"], cw["wk"], cw["wv"]
    notify = (lambda stage, grads: ()) if on_grads is None else on_grads

    h = _mm(x0, cw["w_in_t"], name="mm_h", tb=True, tm=1024, tn=IN_PAD, tk=1024, after=first_after)
    qf, kf, vp = _mla_prep(h, gq, gk, wq, wk, wv, ctab, stab)
    o_mla, o_mla_b, lse_mla = _mla_attn_fwd(qf, kf, vp)
    o_dil, o_dil_b, lse_dil = _dil_fwd(h)
    fetch = (lambda stage, after: {}) if late_weights is None else late_weights
    cw = {**cw, **fetch("w_o", o_mla_b), **fetch("w_up", o_mla_b)}
    cb = cw["conv_b"]
    z1, x1, x1b = _mix_ln1(o_mla_b, o_dil_b, cw["w_o_mla"], cw["w_o_dil"], x0, cw["ln1_g"], cw["ln1_b"])
    u, y, act = _up_conv_gate_fwd(x1b, cw["w_up_t"], cw["conv_w"], cb)
    cw = {**cw, **fetch("w_down", act)}
    dz2, dz2b, loss, d_ln2_g, d_ln2_b = _down_ln2_loss_bwd(act, cw["w_down"], x1, target, cw["ln2_g"], cw["ln2_b"])

    d_act = _mm(dz2b, cw["w_down"], name="mm_d_act", tb=True, out_dtype=MXU_DTYPE, tm=1024, tn=D_FF, tk=1024)
    d_w_down = _mm(act, dz2b, name="mm_dw_down", ta=True, out_dtype=MXU_DTYPE, tm=1408, tn=1024, tk=1024)
    du, d_conv_w, d_conv_b = _conv_gate_bwd(u, y, d_act, cw["conv_w"])
    d_w_up_t = _mm(du, x1b, name="mm_dw_up", ta=True, out_dtype=MXU_DTYPE, tm=1408, tn=1024, tk=2048)
    grads = dict(w_up_t=d_w_up_t, w_down=d_w_down, conv_w=d_conv_w, conv_b=d_conv_b, ln2_g=d_ln2_g, ln2_b=d_ln2_b)
    dz1, dz1b, d_ln1_g, d_ln1_b = _dx1_ln1_bwd(du, cw["w_up_t"], dz2, z1, cw["ln1_g"], after=notify("ffn", grads))
    do_mla, do_dil, d_w_o_mla, d_w_o_dil = _w_o_bwd(dz1b, o_mla_b, o_dil_b, cw["w_o_mla"], cw["w_o_dil"])
    grads.update(w_o_mla=d_w_o_mla, w_o_dil=d_w_o_dil, ln1_g=d_ln1_g, ln1_b=d_ln1_b)
    dqf, dkf, dvf = _mla_attn_bwd(qf, kf, vp, o_mla, lse_mla, do_mla)
    dh_mla, d_wq, d_wk, d_wv, d_gq, d_gk = _mla_prep_bwd(h, gq, gk, wq, wk, wv, ctab, stab, dqf, dkf, dvf,
                                                          after=notify("w_o", grads))
    grads.update(wq=d_wq, wk=d_wk, wv=d_wv, g_cq=d_gq, g_ckv=d_gk, loss=loss)
    dq_dil, dk_dil, dv_dil = _dil_bwd(h, o_dil, lse_dil, do_dil, after=notify("mla", grads))
    dh = (dh_mla, dq_dil, dk_dil, dv_dil)
    grads.update(w_in_t=_mm_dw_in(dh, x0))
    grad_x = _mm_dx0(dh, cw["w_in_t"], dz1, after=notify("w_in", grads))
    return loss, grad_x, grads


def _all_gather(blocks, name):
    na = len(blocks)

    def body(*refs):
        ins, outs = refs[:na], refs[na:2 * na]
        send_sems, recv_sems, local_sems = refs[2 * na:]
        x, y, c = lax.axis_index("x"), lax.axis_index("y"), lax.axis_index("c")
        me, sibling = (x, y, c), (x, y, 1 - c)
        chips = [(1 - x, y), (x, 1 - y), (1 - x, 1 - y)]

        def slot(out, pos):
            return out.at[4 * pos[0] + 2 * pos[1] + pos[2]]

        def copy(a, k, block, to, src=None):
            return pltpu.make_async_remote_copy(
                src_ref=slot(outs[a], block) if src is None else src, dst_ref=slot(outs[a], block),
                send_sem=send_sems.at[7 * a + k], recv_sem=recv_sems.at[7 * a + k],
                device_id=to, device_id_type=pl.DeviceIdType.MESH)

        mine = [pltpu.make_async_copy(ins[a], slot(outs[a], me), local_sems.at[a]) for a in range(na)]
        for cp in mine:
            cp.start()
        first = []
        for a in range(na):
            first.append(copy(a, 0, me, sibling, src=ins[a]))
            first += [copy(a, 1 + j, me, (*chip, c), src=ins[a]) for j, chip in enumerate(chips)]
        for cp in first:
            cp.start()
        passed = []
        for j, chip in enumerate(chips):
            for a in range(na):
                copy(a, 1 + j, (*chip, c), me).wait_recv()
                cp = copy(a, 4 + j, (*chip, c), sibling)
                cp.start()
                passed.append(cp)
        for a in range(na):
            copy(a, 0, sibling, me).wait_recv()
            for j, chip in enumerate(chips):
                copy(a, 4 + j, (*chip, 1 - c), me).wait_recv()
        for cp in first + passed:
            cp.wait_send()
        for cp in mine:
            cp.wait()

    any_spec = pl.BlockSpec(memory_space=pl.ANY)
    return pl.pallas_call(
        body, name=name, in_specs=[any_spec] * na, out_specs=[any_spec] * na,
        out_shape=[jax.ShapeDtypeStruct((N_DEV,) + b.shape, b.dtype) for b in blocks],
        scratch_shapes=[pltpu.SemaphoreType.DMA((7 * na,)), pltpu.SemaphoreType.DMA((7 * na,)), pltpu.SemaphoreType.DMA((na,))],
    )(*blocks)


_HBM_SPEC = pl.BlockSpec(memory_space=pltpu.HBM)
_SEM_SPEC = pl.BlockSpec(memory_space=pltpu.SEMAPHORE)
_DATAFLOW = pltpu.CompilerParams(has_side_effects=pltpu.SideEffectType.DATAFLOW_SIDE_EFFECTING)


def _split_copies(which, ins, lands, send_sems, recv_sems, gather):
    x, y, c = lax.axis_index("x"), lax.axis_index("y"), lax.axis_index("c")
    me = 4 * x + 2 * y + c
    copies = []
    for a, src, land in zip(which, ins, lands):
        for d in range(1, N_DEV):
            px, py, pc = x ^ (d >> 2), y ^ ((d >> 1) & 1), c ^ (d & 1)
            copies.append(pltpu.make_async_remote_copy(
                src_ref=src if gather[a] else src.at[4 * px + 2 * py + pc], dst_ref=land.at[me],
                send_sem=send_sems.at[7 * a + d - 1], recv_sem=recv_sems.at[7 * a + d - 1],
                device_id=(px, py, pc), device_id_type=pl.DeviceIdType.MESH))
    return copies


def _send_start(srcs, gather, name):
    na = len(srcs)
    assert len(gather) == na
    land_types = [pltpu.HBM(((N_DEV,) + s.shape) if g else s.shape, s.dtype) for s, g in zip(srcs, gather)]

    def body(*refs):
        ins, lands = refs[:na], refs[na:2 * na]
        send_sems, recv_sems, token = refs[2 * na], refs[2 * na + 1], refs[-1]
        for cp in _split_copies(range(na), ins, lands, send_sems, recv_sems, gather):
            cp.start()
        token[...] = jnp.zeros_like(token)

    hbm = lambda a: pltpu.with_memory_space_constraint(a, pltpu.HBM)
    outs = pl.pallas_call(
        body, name=name,
        out_shape=(pltpu.SemaphoreType.DMA((7 * na,)), pltpu.SemaphoreType.DMA((7 * na,)),
                   *[pltpu.HBM(s.shape, s.dtype) for s in srcs], *land_types, jax.ShapeDtypeStruct((8, LANES), F32)),
        in_specs=[_HBM_SPEC] * (2 * na),
        out_specs=(_SEM_SPEC, _SEM_SPEC, *[_HBM_SPEC] * (2 * na), pl.BlockSpec(memory_space=pltpu.VMEM)),
        input_output_aliases={i: 2 + i for i in range(2 * na)}, compiler_params=_DATAFLOW,
    )(*[hbm(s) for s in srcs], *[hbm(lax.empty(t.shape, t.dtype)) for t in land_types])
    return dict(send=outs[0], recv=outs[1], srcs=list(outs[2:2 + na]), lands=list(outs[2 + na:2 + 2 * na]), token=outs[-1],
                gather=gather)


def _send_wait(handle, after, name, only=None):
    which = list(range(len(handle["srcs"]))) if only is None else list(only)
    na = len(which)
    gather = handle["gather"]
    after = list(after)

    def body(*refs):
        ins, lands = refs[:na], refs[na:2 * na]
        send_sems, recv_sems = refs[2 * na], refs[2 * na + 1]
        for cp in _split_copies(which, ins, lands, send_sems, recv_sems, gather):
            cp.wait_send()
            cp.wait_recv()

    both = [handle["srcs"][a] for a in which] + [handle["lands"][a] for a in which]
    outs = pl.pallas_call(
        body, name=name, out_shape=[pltpu.HBM(a.shape, a.dtype) for a in both],
        in_specs=[_HBM_SPEC] * (2 * na) + [_SEM_SPEC, _SEM_SPEC] + [_ANY_SPEC] * len(after),
        out_specs=[_HBM_SPEC] * (2 * na), input_output_aliases={i: i for i in range(2 * na)}, compiler_params=_DATAFLOW,
    )(*both, handle["send"], handle["recv"], *after)
    return list(outs[:na]), list(outs[na:])


def _sum_slots(p_ref):
    g = p_ref[0].astype(F32)
    for s in range(1, p_ref.shape[0]):
        g = g + p_ref[s].astype(F32)
    return g


def _adamw_refs(g, w_ref, m_ref, v_ref, g_out, d_out, m_out, v_out):
    c1 = 1.0 - ADAM_B1 ** ADAM_STEP
    c2 = 1.0 - ADAM_B2 ** ADAM_STEP
    m_new = ADAM_B1 * m_ref[...] + (1.0 - ADAM_B1) * g
    v_new = ADAM_B2 * v_ref[...] + (1.0 - ADAM_B2) * (g * g)
    g_out[...] = g
    m_out[...] = m_new
    v_out[...] = v_new
    d_out[...] = -ADAM_LR * ((m_new / c1) / (jnp.sqrt(v_new / c2) + ADAM_EPS) + ADAM_WD * w_ref[...])


def _adamw(parts, w, m, v, name):
    npart, r, n = parts.shape
    tr = r if r <= 256 else max(t for t in range(16, 257, 16) if r % t == 0)

    def body(p_ref, w_ref, m_ref, v_ref, g_out, d_out, m_out, v_out):
        _adamw_refs(_sum_slots(p_ref), w_ref, m_ref, v_ref, g_out, d_out, m_out, v_out)

    blk = pl.BlockSpec((tr, n), lambda i: (i, 0))
    shp = jax.ShapeDtypeStruct((r, n), F32)
    return pl.pallas_call(
        body, name=name, grid=(r // tr,), in_specs=[pl.BlockSpec((npart, tr, n), lambda i: (0, i, 0)), blk, blk, blk],
        out_specs=[blk] * 4, out_shape=[shp] * 4, compiler_params=_params("parallel"),
    )(parts, w, m, v)


def _adamw_small(parts, ws, ms, vs, loss_parts, name):
    n = len(parts)

    def body(*refs):
        ins, outs = refs[:4 * n + 1], refs[4 * n + 1:]
        for i in range(n):
            _adamw_refs(_sum_slots(ins[i]), ins[n + i], ins[2 * n + i], ins[3 * n + i], *outs[4 * i:4 * i + 4])
        outs[4 * n][...] = _sum_slots(ins[4 * n])

    out_shape = [jax.ShapeDtypeStruct(w.shape, F32) for w in ws for _ in range(4)]
    res = pl.pallas_call(body, name=name, out_shape=out_shape + [jax.ShapeDtypeStruct((1, LANES), F32)],
                         compiler_params=_params())(*parts, *ws, *ms, *vs, loss_parts)
    return [res[4 * i:4 * i + 4] for i in range(n)], res[4 * n]


REPLICATED = ("g_cq", "g_ckv", "w_uk", "w_uv", "ln1_g", "ln1_b", "conv_b", "ln2_g", "ln2_b")
ALL_WEIGHTS = ("w_in", "g_cq", "g_ckv", "w_uq", "w_uk", "w_uv", "w_o", "ln1_g", "ln1_b", "w_up", "conv_w", "conv_b",
               "w_down", "ln2_g", "ln2_b")


def kernel(x, w_in, g_cq, g_ckv, w_uq, w_uk, w_uv, w_o, ln1_g, ln1_b, w_up, conv_w, conv_b, w_down, ln2_g, ln2_b, loss_target, m_w_in, m_g_cq, m_g_ckv, m_w_uq, m_w_uk, m_w_uv, m_w_o, m_ln1_g, m_ln1_b, m_w_up, m_conv_w, m_conv_b, m_w_down, m_ln2_g, m_ln2_b, v_w_in, v_g_cq, v_g_ckv, v_w_uq, v_w_uk, v_w_uv, v_w_o, v_ln1_g, v_ln1_b, v_w_up, v_conv_w, v_conv_b, v_w_down, v_ln2_g, v_ln2_b):
    w = dict(w_in=w_in, g_cq=g_cq, g_ckv=g_ckv, w_uq=w_uq, w_uk=w_uk, w_uv=w_uv, w_o=w_o, ln1_g=ln1_g, ln1_b=ln1_b,
             w_up=w_up, conv_w=conv_w, conv_b=conv_b, w_down=w_down, ln2_g=ln2_g, ln2_b=ln2_b)
    m = dict(w_in=m_w_in, g_cq=m_g_cq, g_ckv=m_g_ckv, w_uq=m_w_uq, w_uk=m_w_uk, w_uv=m_w_uv, w_o=m_w_o, ln1_g=m_ln1_g,
             ln1_b=m_ln1_b, w_up=m_w_up, conv_w=m_conv_w, conv_b=m_conv_b, w_down=m_w_down, ln2_g=m_ln2_g, ln2_b=m_ln2_b)
    v = dict(w_in=v_w_in, g_cq=v_g_cq, g_ckv=v_g_ckv, w_uq=v_w_uq, w_uk=v_w_uk, w_uv=v_w_uv, w_o=v_w_o, ln1_g=v_ln1_g,
             ln1_b=v_ln1_b, w_up=v_w_up, conv_w=v_conv_w, conv_b=v_conv_b, w_down=v_w_down, ln2_g=v_ln2_g, ln2_b=v_ln2_b)
    me = 4 * lax.axis_index("x") + 2 * lax.axis_index("y") + lax.axis_index("c")
    wire = lambda a: a.astype(WIRE_DTYPE)
    pad_taps = lambda a: jnp.pad(a, ((0, 8 - a.shape[0]), (0, 0)))

    own_slot = lambda buf, block: lax.dynamic_update_index_in_dim(buf, block, me, 0)
    blocks = lambda a: wire(a).reshape((N_DEV, a.shape[0] // N_DEV) + a.shape[1:])

    g_in, g_uq, g_conv = _all_gather(
        [wire(w_in).T, _heads_major(wire(w_uq)), pad_taps(conv_w)],
        "gather_weights")
    late = _send_start([wire(w_o), wire(w_up).T, wire(w_down)], [True] * 3, "gather_late_start")
    r_uq_dev, e_uq = w_uq.shape[0], w_uq.shape[2]
    wq = jnp.transpose(g_uq.reshape(N_DEV, HEADS, r_uq_dev, e_uq), (1, 0, 2, 3)).reshape(HEADS, Q_RANK, e_uq)
    cw = dict(
        w_in_t=_split_pad_rows(g_in.reshape(-1, D_MODEL)).astype(MXU_DTYPE),
        wq=jnp.pad(wq, ((0, 0), (0, 0), (0, LANES - e_uq))).astype(MXU_DTYPE),
        wk=_pad_heads(w_uk, NOPE_DIM), wv=_pad_heads(w_uv, HEAD_DIM),
        conv_w=_ffn_interleave(jnp.transpose(g_conv[:, :conv_w.shape[0]], (1, 0, 2)).reshape(conv_w.shape[0], -1), 1),
        g_cq=_row(g_cq), g_ckv=_row(g_ckv), ln1_g=_row(ln1_g), ln1_b=_row(ln1_b), conv_b=_ffn_interleave(_row(conv_b), 1),
        ln2_g=_row(ln2_g), ln2_b=_row(ln2_b))

    def late_weights(stage, after):
        (own,), (got,) = _send_wait(late, [after], f"gather_{stage}_wait", only=[("w_o", "w_up", "w_down").index(stage)])
        full = own_slot(got, own).reshape(-1, D_MODEL)
        if stage == "w_o":
            w_o_mla, w_o_dil = _pad_w_o(full)
            return dict(w_o_mla=w_o_mla, w_o_dil=w_o_dil)
        if stage == "w_up":
            return dict(w_up_t=_ffn_interleave(full, 0).astype(MXU_DTYPE))
        return dict(w_down=full.astype(MXU_DTYPE))

    sent = {}

    def on_grads(stage, g):
        if stage == "ffn":
            sent[stage] = _send_start([blocks(_ffn_deinterleave(g["w_up_t"], 0)), blocks(g["w_down"])], [False] * 2,
                                      "exchange_ffn_start")
        elif stage == "w_o":
            return []
        elif stage == "mla":
            d_uq = wire(jnp.transpose(g["wq"][:, :, :e_uq].reshape(HEADS, N_DEV, r_uq_dev, e_uq), (1, 0, 2, 3))
                        ).reshape(N_DEV, HEADS * r_uq_dev, e_uq)
            dense = lambda a, width: wire(jnp.transpose(a[:, :, :width], (0, 2, 1))).reshape(-1, a.shape[1])
            small = dict(g_cq=g["g_cq"], g_ckv=g["g_ckv"], w_uk=dense(g["wk"], NOPE_DIM), w_uv=dense(g["wv"], HEAD_DIM),
                         ln1_g=g["ln1_g"], ln1_b=g["ln1_b"], conv_b=_ffn_deinterleave(g["conv_b"], 1), ln2_g=g["ln2_g"],
                         ln2_b=g["ln2_b"])
            everyone = [small[n] for n in REPLICATED] + [_ffn_deinterleave(g["conv_w"], 1), g["loss"]]
            sent[stage] = _send_start([blocks(_unpad_w_o(g["w_o_mla"], g["w_o_dil"])), d_uq] + everyone,
                                      [False] * 2 + [True] * len(everyone), "exchange_mla_start")
        else:
            sent[stage] = _send_start([blocks(_split_unpad_rows(g["w_in_t"]))], [False], "exchange_w_in_start")
        return [sent[stage]["token"]]

    _, grad_x, _ = _layer_grads(x[0], loss_target[0], cw, [late["token"]], late_weights, on_grads)

    def landed(handle, after, name):
        own, got = _send_wait(handle, after, name)
        pick = lambda src, whole: src if whole else lax.dynamic_index_in_dim(src, me, 0, keepdims=False)
        return [own_slot(buf, pick(src, whole)) for buf, src, whole in zip(got, own, handle["gather"])]

    out = {}

    def update(name, parts, view=None):
        to2d = {None: lambda a: a, "t": lambda a: a.T, "heads": _heads_major}[view]
        back = {None: lambda a: a, "t": lambda a: a.T, "heads": lambda a: _heads_minor(a, HEADS)}[view]
        res = _adamw(parts, to2d(w[name]), to2d(m[name]), to2d(v[name]), "adamw_" + name)
        for kind, a in zip(("grad", "delta", "new_m", "new_v"), res):
            out[kind, name] = back(a)
        return res[0]

    r_up, r_down = landed(sent["ffn"], [grad_x], "exchange_ffn_wait")
    r_o, r_uq, *rep_all, cw_all, loss_all = landed(sent["mla"], [grad_x], "exchange_mla_wait")
    done = [update("w_up", r_up, "t"), update("w_down", r_down), update("w_o", r_o), update("w_uq", r_uq, "heads")]
    rank_minor = ("w_uk", "w_uv")
    two_d = lambda n, a: jnp.transpose(a, (1, 2, 0)).reshape(-1, a.shape[0]) if n in rank_minor else a.reshape(1, -1)
    res, loss_sum = _adamw_small(rep_all, *[[two_d(n, d[n]) for n in REPLICATED] for d in (w, m, v)], loss_all, "adamw_replicated")
    for n, quad in zip(REPLICATED, res):
        for kind, a in zip(("grad", "delta", "new_m", "new_v"), quad):
            out[kind, n] = jnp.transpose(a.reshape(HEADS, -1, a.shape[1]), (2, 0, 1)) if n in rank_minor else a.reshape(w[n].shape)
    loss = loss_sum[0, 0]
    ncw = conv_w.shape[1]
    done += [loss_sum, update("conv_w", lax.dynamic_slice_in_dim(cw_all[:, :conv_w.shape[0]], me * ncw, ncw, axis=2))]
    (r_in,) = landed(sent["w_in"], done, "exchange_w_in_wait")
    update("w_in", r_in, "t")

    return (loss, grad_x[None], *[out[kind, n] for kind in ("grad", "delta", "new_m", "new_v") for n in ALL_WEIGHTS])
```

```python
import math

import jax
import jax.numpy as jnp
import numpy as np
from jax import lax
from jax.experimental import pallas as pl
from jax.experimental.pallas import tpu as pltpu

F32 = jnp.float32
MXU_DTYPE = jnp.bfloat16
WIRE_DTYPE = jnp.bfloat16

N_DEV = 8
D_MODEL = 1024
HEADS = 8
HEAD_DIM = 64
LANES = 128
Q_RANK, KV_RANK, ROPE_DIM, NOPE_DIM = 256, 128, 32, 64
DIL_WIDTH = HEADS * HEAD_DIM
MLA_WIDTH = HEADS * HEAD_DIM
IN_PAD = 2048
DH_PART = IN_PAD // 4
D_FF = 2816
ROPE_THETA = 10000.0
DIL_PAIRS = ((128, 1), (512, 4), (2048, 16))
DIL_BLOCK = 128
DN_ALPHA = 2.0 ** 0.25
LN_EPS = 1e-5
RMS_EPS = 1e-6
ONES_LANE = HEAD_DIM
MLA_SCALE = 1.0 / math.sqrt(NOPE_DIM + ROPE_DIM)
MLA_SCALE_LOG2 = MLA_SCALE * math.log2(math.e)
MLA_BWD_SPLITS = 2
MLA_BWD_SPLITS_DIAGONAL = 4
MLA_FWD_SPLITS_DIAGONAL = 2
DIL_SCALE = 1.0 / math.sqrt(HEAD_DIM)
ALIBI_SLOPES = tuple(2.0 ** (-8.0 * (h + 1) / HEADS) for h in range(HEADS))
NEG_BIG = -1e30
ADAM_LR, ADAM_B1, ADAM_B2, ADAM_EPS, ADAM_WD, ADAM_STEP = 0.001, 0.9, 0.999, 1e-08, 0.01, 10
VMEM_LIMIT = 48 * 1024 * 1024


def _params(*sem):
    return pltpu.CompilerParams(dimension_semantics=sem or None, vmem_limit_bytes=VMEM_LIMIT)


def _dot(a, b, ca, cb):
    return lax.dot_general(a, b, (((ca,), (cb,)), ((), ())), preferred_element_type=F32)


_ANY_SPEC = pl.BlockSpec(memory_space=pl.ANY)


def _mm(a, b, *, name, tm, tn, tk, ta=False, tb=False, out_dtype=F32, after=()):
    m, k = (a.shape[1], a.shape[0]) if ta else a.shape
    n = b.shape[0] if tb else b.shape[1]
    assert (b.shape[1] if tb else b.shape[0]) == k
    tm, tn, tk = min(tm, m), min(tn, n), min(tk, k)
    assert m % tm == 0 and n % tn == 0 and k % tk == 0, (name, m, n, k, tm, tn, tk)
    nk = k // tk
    a_spec = (pl.BlockSpec((tk, tm), lambda i, j, kk: (kk, i)) if ta
              else pl.BlockSpec((tm, tk), lambda i, j, kk: (i, kk)))
    b_mode = dict(pipeline_mode=pl.Buffered(1)) if (tn == n and tk == k) else {}
    b_spec = (pl.BlockSpec((tn, tk), lambda i, j, kk: (j, kk), **b_mode) if tb
              else pl.BlockSpec((tk, tn), lambda i, j, kk: (kk, j), **b_mode))
    o_spec = pl.BlockSpec((tm, tn), lambda i, j, kk: (i, j))
    n_in = 2 + len(after)
    ca, cb = (0 if ta else 1), (1 if tb else 0)

    def body(*refs):
        a_ref, b_ref, o_ref = refs[0], refs[1], refs[n_in]
        part = _dot(a_ref[...].astype(MXU_DTYPE), b_ref[...].astype(MXU_DTYPE), ca, cb)
        if nk == 1:
            o_ref[...] = part.astype(o_ref.dtype)
            return
        acc_ref = refs[-1]
        kk = pl.program_id(2)

        @pl.when(kk == 0)
        def _():
            acc_ref[...] = part

        @pl.when(kk > 0)
        def _():
            acc_ref[...] += part

        @pl.when(kk == nk - 1)
        def _():
            o_ref[...] = acc_ref[...].astype(o_ref.dtype)

    return pl.pallas_call(
        body, name=name, grid=(m // tm, n // tn, nk), in_specs=[a_spec, b_spec] + [_ANY_SPEC] * len(after), out_specs=o_spec,
        out_shape=jax.ShapeDtypeStruct((m, n), out_dtype),
        scratch_shapes=[pltpu.VMEM((tm, tn), F32)] if nk > 1 else [],
        compiler_params=_params("parallel", "parallel", "arbitrary"),
    )(a, b, *after)


def _w_o_bwd(dz, o_mla, o_dil, w_o_mla, w_o_dil, tm=1024):
    seq, d = dz.shape
    tm = min(tm, seq)
    nstep = seq // tm

    def body(a_ref, om_ref, od_ref, wm_ref, wd_ref, dom_ref, dod_ref, dwm_ref, dwd_ref, accm_ref, accd_ref):
        step = pl.program_id(0)

        @pl.when(step == 0)
        def _():
            accm_ref[...] = jnp.zeros_like(accm_ref)
            accd_ref[...] = jnp.zeros_like(accd_ref)

        a = a_ref[...]
        for hd in range(HEADS):
            rows = slice(LANES * hd, LANES * (hd + 1))
            dom_ref[hd] = _dot(a, wm_ref[rows, :], 1, 1)
            accm_ref[rows, :] += _dot(om_ref[hd], a, 0, 0)
        dod_ref[...] = _dot(a, wd_ref[...], 1, 1)
        accd_ref[...] += _dot(od_ref[...], a, 0, 0)

        @pl.when(step == nstep - 1)
        def _():
            dwm_ref[...] = accm_ref[...].astype(dwm_ref.dtype)
            dwd_ref[...] = accd_ref[...].astype(dwd_ref.dtype)

    once = dict(pipeline_mode=pl.Buffered(1))
    return pl.pallas_call(
        body, name="w_o_bwd", grid=(nstep,),
        in_specs=[pl.BlockSpec((tm, d), lambda i: (i, 0)), pl.BlockSpec((HEADS, tm, LANES), lambda i: (0, i, 0)),
                  pl.BlockSpec((tm, DIL_WIDTH), lambda i: (i, 0)), pl.BlockSpec((HEADS * LANES, d), lambda i: (0, 0), **once),
                  pl.BlockSpec((DIL_WIDTH, d), lambda i: (0, 0), **once)],
        out_specs=[pl.BlockSpec((HEADS, tm, LANES), lambda i: (0, i, 0)), pl.BlockSpec((tm, DIL_WIDTH), lambda i: (i, 0)),
                   pl.BlockSpec((HEADS * LANES, d), lambda i: (0, 0)), pl.BlockSpec((DIL_WIDTH, d), lambda i: (0, 0))],
        out_shape=[jax.ShapeDtypeStruct((HEADS, seq, LANES), F32), jax.ShapeDtypeStruct((seq, DIL_WIDTH), F32),
                   jax.ShapeDtypeStruct((HEADS * LANES, d), MXU_DTYPE), jax.ShapeDtypeStruct((DIL_WIDTH, d), MXU_DTYPE)],
        scratch_shapes=[pltpu.VMEM((HEADS * LANES, d), F32), pltpu.VMEM((DIL_WIDTH, d), F32)],
        compiler_params=_params("arbitrary"),
    )(dz, o_mla, o_dil, w_o_mla, w_o_dil)


def _rope_tables(seq):
    half = ROPE_DIM // 2
    f32 = np.float32
    freqs = np.power(f32(ROPE_THETA), -np.arange(half, dtype=f32) / f32(half))
    ang = np.arange(seq, dtype=f32)[:, None] * freqs[None, :]
    cos, sin = np.cos(ang, dtype=f32), np.sin(ang, dtype=f32)
    one = np.ones((seq, NOPE_DIM), f32)
    tail = np.ones((seq, LANES - NOPE_DIM - ROPE_DIM), f32)
    ctab = np.concatenate([one, cos, cos, tail], axis=1)
    stab = np.concatenate([0 * one, -sin, sin, 0 * tail], axis=1)
    return jnp.asarray(ctab), jnp.asarray(stab)


def _rope_swap(t):
    lane = lax.broadcasted_iota(jnp.int32, t.shape, 1)
    half = ROPE_DIM // 2
    return jnp.where(lane < NOPE_DIM + half, pltpu.roll(t, LANES - half, 1), pltpu.roll(t, half, 1))


def _rope(t, ctab, stab):
    return t * ctab + _rope_swap(t) * stab


def _rope_inv(t, ctab, stab):
    return t * ctab - _rope_swap(t) * stab


def _rms(x, g):
    r = lax.rsqrt(jnp.mean(x * x, axis=-1, keepdims=True) + RMS_EPS)
    xh = x * r
    return xh, r, xh * g


def _mla_prep(h, g_cq, g_ckv, wq, wk, wv, ctab, stab, tm=512):
    seq = h.shape[0]
    tm = min(tm, seq)

    def body(h_ref, gq_ref, gk_ref, wq_ref, wk_ref, wv_ref, c_ref, s_ref, q_out, k_out, v_out):
        hb = h_ref[...]
        ctab_, stab_ = c_ref[...], s_ref[...]
        _, _, cqn = _rms(hb[:, :Q_RANK], gq_ref[...])
        _, _, ckn = _rms(hb[:, Q_RANK:Q_RANK + KV_RANK], gk_ref[...])
        cqn = cqn.astype(MXU_DTYPE)
        ckn = ckn.astype(MXU_DTYPE)
        krr = _rope(hb[:, Q_RANK + KV_RANK:], ctab_, stab_)
        ones_lane = (lax.broadcasted_iota(jnp.int32, (1, LANES), 1) == ONES_LANE).astype(F32)
        for hd in range(HEADS):
            q = _dot(cqn, wq_ref[hd], 1, 0)
            q_out[hd] = _rope(q, ctab_, stab_).astype(q_out.dtype)
            k_out[hd] = (_dot(ckn, wk_ref[hd], 1, 0) + krr).astype(k_out.dtype)
            v_out[hd] = (_dot(ckn, wv_ref[hd], 1, 0) + ones_lane).astype(v_out.dtype)

    full = lambda *shape: pl.BlockSpec(shape, lambda i: (0,) * len(shape))
    slab = pl.BlockSpec((HEADS, tm, LANES), lambda i: (0, i, 0))
    shp = jax.ShapeDtypeStruct((HEADS, seq, LANES), MXU_DTYPE)
    return pl.pallas_call(
        body, name="mla_prep", grid=(seq // tm,),
        in_specs=[pl.BlockSpec((tm, DH_PART), lambda i: (i, 0)), full(1, Q_RANK), full(1, KV_RANK),
                  full(HEADS, Q_RANK, LANES), full(HEADS, KV_RANK, LANES), full(HEADS, KV_RANK, LANES),
                  pl.BlockSpec((tm, LANES), lambda i: (i, 0)), pl.BlockSpec((tm, LANES), lambda i: (i, 0))],
        out_specs=[slab, slab, slab], out_shape=[shp, shp, shp],
        compiler_params=_params("parallel"),
    )(h, g_cq, g_ckv, wq, wk, wv, ctab, stab)


def _mla_prep_bwd(h, g_cq, g_ckv, wq, wk, wv, ctab, stab, dq, dk, dv, tm=512, after=()):
    seq = h.shape[0]
    tm = min(tm, seq)
    n_after = len(after)

    def body(h_ref, gq_ref, gk_ref, wq_ref, wk_ref, wv_ref, c_ref, s_ref, dq_ref, dk_ref, dv_ref, *rest):
        dh_ref, dwq_ref, dwk_ref, dwv_ref, dgq_ref, dgk_ref = rest[n_after:]

        @pl.when(pl.program_id(0) == 0)
        def _():
            for r in (dwq_ref, dwk_ref, dwv_ref, dgq_ref, dgk_ref):
                r[...] = jnp.zeros_like(r)

        hb = h_ref[...]
        ctab_, stab_ = c_ref[...], s_ref[...]
        gq, gk = gq_ref[...], gk_ref[...]
        xq, rq, cqn = _rms(hb[:, :Q_RANK], gq)
        xk, rk, ckn = _rms(hb[:, Q_RANK:Q_RANK + KV_RANK], gk)
        cqn = cqn.astype(MXU_DTYPE)
        ckn = ckn.astype(MXU_DTYPE)
        d_cqn = jnp.zeros((tm, Q_RANK), F32)
        d_ckn = jnp.zeros((tm, KV_RANK), F32)
        d_krr = jnp.zeros((tm, LANES), F32)
        for hd in range(HEADS):
            dqh = _rope_inv(dq_ref[hd], ctab_, stab_).astype(MXU_DTYPE)
            d_cqn += _dot(dqh, wq_ref[hd], 1, 1)
            dwq_ref[hd] += _dot(cqn, dqh, 0, 0)
            dkh = dk_ref[hd]
            d_krr += dkh
            dkh = dkh.astype(MXU_DTYPE)
            d_ckn += _dot(dkh, wk_ref[hd], 1, 1)
            dwk_ref[hd] += _dot(ckn, dkh, 0, 0)
            dvh = dv_ref[hd].astype(MXU_DTYPE)
            d_ckn += _dot(dvh, wv_ref[hd], 1, 1)
            dwv_ref[hd] += _dot(ckn, dvh, 0, 0)
        lane = lax.broadcasted_iota(jnp.int32, (tm, LANES), 1)
        rot = (lane >= NOPE_DIM) & (lane < NOPE_DIM + ROPE_DIM)
        d_kr = jnp.where(rot, _rope_inv(jnp.where(rot, d_krr, 0.0), ctab_, stab_), 0.0)

        def rms_bwd(dy, xh, r, g, dg_ref):
            dg_ref[...] += jnp.sum(dy * xh, axis=0, keepdims=True)
            dxh = dy * g
            return r * (dxh - xh * jnp.mean(dxh * xh, axis=-1, keepdims=True))

        d_cq = rms_bwd(d_cqn, xq, rq, gq, dgq_ref)
        d_ck = rms_bwd(d_ckn, xk, rk, gk, dgk_ref)
        dh_ref[...] = jnp.concatenate([d_cq, d_ck, d_kr], axis=1).astype(dh_ref.dtype)

    full = lambda *shape: pl.BlockSpec(shape, lambda i: (0,) * len(shape))
    slab = pl.BlockSpec((HEADS, tm, LANES), lambda i: (0, i, 0))
    return pl.pallas_call(
        body, name="mla_prep_bwd", grid=(seq // tm,),
        in_specs=[pl.BlockSpec((tm, DH_PART), lambda i: (i, 0)), full(1, Q_RANK), full(1, KV_RANK),
                  full(HEADS, Q_RANK, LANES), full(HEADS, KV_RANK, LANES), full(HEADS, KV_RANK, LANES),
                  pl.BlockSpec((tm, LANES), lambda i: (i, 0)), pl.BlockSpec((tm, LANES), lambda i: (i, 0)),
                  slab, slab, slab] + [_ANY_SPEC] * n_after,
        out_specs=[pl.BlockSpec((tm, DH_PART), lambda i: (i, 0)), full(HEADS, Q_RANK, LANES), full(HEADS, KV_RANK, LANES),
                   full(HEADS, KV_RANK, LANES), full(1, Q_RANK), full(1, KV_RANK)],
        out_shape=[jax.ShapeDtypeStruct((seq, DH_PART), MXU_DTYPE), jax.ShapeDtypeStruct((HEADS, Q_RANK, LANES), F32),
                   jax.ShapeDtypeStruct((HEADS, KV_RANK, LANES), F32), jax.ShapeDtypeStruct((HEADS, KV_RANK, LANES), F32),
                   jax.ShapeDtypeStruct((1, Q_RANK), F32), jax.ShapeDtypeStruct((1, KV_RANK), F32)],
        compiler_params=_params("arbitrary"),
    )(h, g_cq, g_ckv, wq, wk, wv, ctab, stab, dq, dk, dv, *after)


def _mla_attn_fwd(q, k, v, t=1024):
    _, seq, _ = q.shape
    t = min(t, seq)

    def body(q_ref, k_ref, v_ref, o_ref, ob_ref, lse_ref, m_ref, acc_ref, s_ref):
        i = pl.program_id(1)
        qb = q_ref[...]
        m_ref[...] = jnp.full_like(m_ref, NEG_BIG)
        acc_ref[...] = jnp.zeros_like(acc_ref)

        def scores(j):
            return _dot(qb, k_ref[pl.ds(pl.multiple_of(j * t, t), t), :], 1, 1) * MLA_SCALE_LOG2

        def softmax_pv(j, s, rows=slice(None), mask=None):
            vb = v_ref[pl.ds(pl.multiple_of(j * t, t), s.shape[1]), :]
            if mask is not None:
                s = jnp.where(mask, s, NEG_BIG)
            m_old = m_ref[rows, :]
            m_new = jnp.maximum(m_old, jnp.max(s, axis=1, keepdims=True))
            p = jnp.exp2(s - m_new)
            a = jnp.exp2(m_old - m_new)
            acc_ref[rows, :] = a * acc_ref[rows, :] + _dot(p.astype(MXU_DTYPE), vb, 1, 0)
            m_ref[rows, :] = m_new

        def softmax_pv_diagonal(j):
            th = t // MLA_FWD_SPLITS_DIAGONAL
            for hf in range(MLA_FWD_SPLITS_DIAGONAL):
                nk = (hf + 1) * th
                row = lax.broadcasted_iota(jnp.int32, (th, nk), 0) + hf * th
                rows = slice(hf * th, (hf + 1) * th)
                softmax_pv(j, s_ref[rows, 0:nk], rows, row >= lax.broadcasted_iota(jnp.int32, (th, nk), 1))

        s_ref[...] = scores(0)

        def loop_body(j, c):
            s_next = scores(j + 1)
            softmax_pv(j, s_ref[...])
            s_ref[...] = s_next
            return c

        lax.fori_loop(0, i, loop_body, 0)
        softmax_pv_diagonal(i)
        acc = acc_ref[...]
        l = acc[:, ONES_LANE:ONES_LANE + 1]
        o = jnp.where(lax.broadcasted_iota(jnp.int32, acc.shape, 1) < HEAD_DIM, acc * (1.0 / l), 0.0)
        o_ref[...] = o
        ob_ref[...] = o.astype(ob_ref.dtype)
        lse_ref[...] = jnp.broadcast_to(m_ref[...] + jnp.log2(l), lse_ref.shape)

    blk = pl.BlockSpec((None, t, LANES), lambda h, i: (h, i, 0))
    whole = pl.BlockSpec((None, seq, LANES), lambda h, i: (h, 0, 0))
    shp = jax.ShapeDtypeStruct((HEADS, seq, LANES), F32)
    return pl.pallas_call(
        body, name="mla_attn_fwd", grid=(HEADS, seq // t),
        in_specs=[blk, whole, whole], out_specs=[blk, blk, blk],
        out_shape=[shp, jax.ShapeDtypeStruct((HEADS, seq, LANES), MXU_DTYPE), shp],
        scratch_shapes=[pltpu.VMEM((t, 1), F32), pltpu.VMEM((t, LANES), F32), pltpu.VMEM((t, t), F32)],
        compiler_params=_params("parallel", "arbitrary"),
    )(q, k, v)


def _mla_attn_bwd(q, k, v, o, lse, do, t=1024):
    _, seq, _ = q.shape
    t = min(t, seq)
    nb = seq // t

    def body(q_ref, k_ref, v_ref, o_ref, lse_ref, do_ref, dq_ref, dk_ref, dv_ref, dl_ref, dka_ref, dva_ref):
        dq_ref[...] = jnp.zeros_like(dq_ref)

        def delta_body(i, c):
            rows = pl.ds(pl.multiple_of(i * t, t), t)
            dl_ref[rows, :] = jnp.sum(do_ref[rows, :] * o_ref[rows, :], axis=1, keepdims=True)
            return c

        lax.fori_loop(0, nb, delta_body, 0)

        def kblock(j, c):
            krows = pl.ds(pl.multiple_of(j * t, t), t)
            kb = k_ref[krows, :]
            vb = v_ref[krows, :]
            dka_ref[...] = jnp.zeros_like(dka_ref)
            dva_ref[...] = jnp.zeros_like(dva_ref)

            def qstep(i, masked):
                ns = MLA_BWD_SPLITS_DIAGONAL if masked else MLA_BWD_SPLITS
                th = t // ns
                rows = [pl.ds(pl.multiple_of(i * t + hf * th, th), th) for hf in range(ns)]
                qs = [q_ref[r, :] for r in rows]
                dos = [do_ref[r, :].astype(MXU_DTYPE) for r in rows]
                nkeys = [(hf + 1) * th if masked else t for hf in range(ns)]
                ss = [_dot(qs[hf], kb[:nkeys[hf]], 1, 1) * MLA_SCALE_LOG2 for hf in range(ns)]
                dps = [_dot(dos[hf], vb[:nkeys[hf]], 1, 1) for hf in range(ns)]
                for hf in range(ns):
                    s, nk = ss[hf], nkeys[hf]
                    if masked:
                        row = lax.broadcasted_iota(jnp.int32, (th, nk), 0) + hf * th
                        s = jnp.where(row >= lax.broadcasted_iota(jnp.int32, (th, nk), 1), s, NEG_BIG)
                    p = jnp.exp2(s - lse_ref[rows[hf], 0:1])
                    dva_ref[0:nk, :] += _dot(p.astype(MXU_DTYPE), dos[hf], 0, 0)
                    ds = (p * (dps[hf] - dl_ref[rows[hf], :]) * MLA_SCALE).astype(MXU_DTYPE)
                    dka_ref[0:nk, :] += _dot(ds, qs[hf], 0, 0)
                    dq_ref[rows[hf], :] += _dot(ds, kb[:nk], 1, 0)

            qstep(j, True)

            def qloop(i, c2):
                qstep(i, False)
                return c2

            lax.fori_loop(j + 1, nb, qloop, 0)
            dk_ref[krows, :] = dka_ref[...]
            dv_ref[krows, :] = dva_ref[...]
            return c

        lax.fori_loop(0, nb, kblock, 0)

    whole = pl.BlockSpec((None, seq, LANES), lambda h: (h, 0, 0))
    shp = jax.ShapeDtypeStruct((HEADS, seq, LANES), F32)
    return pl.pallas_call(
        body, name="mla_attn_bwd", grid=(HEADS,),
        in_specs=[whole] * 6, out_specs=[whole] * 3, out_shape=[shp] * 3,
        scratch_shapes=[pltpu.VMEM((seq, 1), F32), pltpu.VMEM((t, LANES), F32), pltpu.VMEM((t, LANES), F32)],
        compiler_params=_params("parallel"),
    )(q, k, v, o, lse, do)


DIL_CHUNK = DIL_BLOCK * max(d for _, d in DIL_PAIRS)
DIL_PAIR_LANES = 2 * HEAD_DIM
assert DIL_PAIR_LANES == LANES
DIL_UNROLL_FWD = 16
DIL_UNROLL_BWD = 16


def _dil_bias_tables(hp, dil):
    b = DIL_BLOCK
    iq = lax.broadcasted_iota(jnp.int32, (b, 2 * b), 0)
    ik = lax.broadcasted_iota(jnp.int32, (b, 2 * b), 1)
    off = iq + b - ik
    band = (off >= 0) & (off <= b)
    dist = (off * dil).astype(F32)
    every, first = [], []
    for hh in range(2):
        slope = jnp.where(hp == 0, ALIBI_SLOPES[hh], jnp.where(hp == 1, ALIBI_SLOPES[2 + hh],
                          jnp.where(hp == 2, ALIBI_SLOPES[4 + hh], ALIBI_SLOPES[6 + hh]))).astype(F32)
        bias = -slope * dist
        every.append(jnp.where(band, bias, NEG_BIG))
        first.append(jnp.where(band & (ik >= b), bias, NEG_BIG))
    return jnp.concatenate(every, axis=0), jnp.concatenate(first, axis=0)


def _dil_rows(start, dil):
    return pl.ds(start, DIL_BLOCK) if dil == 1 else pl.ds(start, DIL_BLOCK, stride=dil)


def _dil_block_pos(blk, c, dil):
    sc, r = blk // dil, blk % dil
    q0 = sc * (DIL_BLOCK * dil) + r
    kcur0 = c * DIL_CHUNK + q0
    first = kcur0 < DIL_BLOCK * dil
    kprev0 = jnp.where(first, kcur0, kcur0 - DIL_BLOCK * dil)
    return q0, kcur0, kprev0, first


def _pair_cols(hh):
    return slice(HEAD_DIM * hh, HEAD_DIM * (hh + 1))


def _first_head_lanes(shape):
    return lax.broadcasted_iota(jnp.int32, shape, 1) < HEAD_DIM


def _stack_pair(t):
    first = _first_head_lanes(t.shape)
    return jnp.concatenate([jnp.where(first, t, 0.0), jnp.where(first, 0.0, t)], axis=0).astype(MXU_DTYPE)


def _unstack_pair(t):
    rows = t.shape[0] // 2
    return jnp.where(_first_head_lanes((rows, t.shape[1])), t[:rows], t[rows:])


def _pair_column(t):
    return jnp.concatenate([t[:, 0:1], t[:, HEAD_DIM:HEAD_DIM + 1]], axis=0)


def _dil_fwd(h):
    seq = h.shape[0]
    assert seq % DIL_CHUNK == 0
    nblk = DIL_CHUNK // DIL_BLOCK
    rc = 256

    def body(q_ref, k_ref, v_ref, o_ref, ob_ref, lse_ref, *scr):
        o_scr, l_scr = scr[:3], scr[3:]
        hp, c = pl.program_id(0), pl.program_id(1)
        for bi, (_, dil) in enumerate(DIL_PAIRS):
            tables = _dil_bias_tables(hp, dil)

            def block(blk, carry, bi=bi, dil=dil, tables=tables):
                q0, kcur0, kprev0, first = _dil_block_pos(blk, c, dil)
                q2 = _stack_pair(q_ref[_dil_rows(q0, dil), :] * DIL_SCALE)
                kcat = jnp.concatenate([k_ref[_dil_rows(kprev0, dil), :], k_ref[_dil_rows(kcur0, dil), :]], axis=0).astype(MXU_DTYPE)
                vcat = jnp.concatenate([v_ref[_dil_rows(kprev0, dil), :], v_ref[_dil_rows(kcur0, dil), :]], axis=0).astype(MXU_DTYPE)
                s = _dot(q2, kcat, 1, 1) + jnp.where(first, tables[1], tables[0])
                mx = jnp.max(s, axis=1, keepdims=True)
                p = jnp.exp(s - mx)
                l = jnp.sum(p, axis=1, keepdims=True)
                o_scr[bi][_dil_rows(q0, dil), :] = _unstack_pair(_dot(p.astype(MXU_DTYPE), vcat, 1, 0) * (1.0 / l))
                l_scr[bi][_dil_rows(q0, dil), :] = _unstack_pair(jnp.broadcast_to(mx + jnp.log(l), (2 * DIL_BLOCK, LANES)))
                return carry

            lax.fori_loop(0, nblk, block, 0, unroll=DIL_UNROLL_FWD)

        def combine(i, carry):
            rows = pl.ds(pl.multiple_of(i * rc, rc), rc)
            ls = [l_scr[bi][rows, :] for bi in range(3)]
            mx = jnp.maximum(jnp.maximum(ls[0], ls[1]), ls[2])
            es = [jnp.exp(l - mx) for l in ls]
            den = es[0] + es[1] + es[2]
            o = (es[0] * o_scr[0][rows, :] + es[1] * o_scr[1][rows, :] + es[2] * o_scr[2][rows, :]) / den
            o_ref[rows, :] = o
            ob_ref[rows, :] = o.astype(ob_ref.dtype)
            lse_ref[rows, :] = mx + jnp.log(den)
            return carry

        lax.fori_loop(0, DIL_CHUNK // rc, combine, 0)

    nq = DIL_WIDTH // LANES
    chunk = lambda off: pl.BlockSpec((DIL_CHUNK, LANES), lambda hp, c: (c, off + hp))
    whole = lambda off: pl.BlockSpec((seq, LANES), lambda hp, c: (0, off + hp))
    shp = jax.ShapeDtypeStruct((seq, DIL_WIDTH), F32)
    return pl.pallas_call(
        body, name="dil_fwd", grid=(nq, seq // DIL_CHUNK),
        in_specs=[chunk(nq), whole(2 * nq), whole(3 * nq)], out_specs=[chunk(0), chunk(0), chunk(0)],
        out_shape=[shp, jax.ShapeDtypeStruct((seq, DIL_WIDTH), MXU_DTYPE), shp],
        scratch_shapes=[pltpu.VMEM((DIL_CHUNK, LANES), F32)] * 6,
        compiler_params=_params("parallel", "arbitrary"),
    )(h, h, h)


def _dil_bwd(h, o, lse, do, after=()):
    seq = h.shape[0]
    nblk = DIL_CHUNK // DIL_BLOCK
    nchunk = seq // DIL_CHUNK
    rc = 256

    n_after = len(after)

    def body(q_ref, k_ref, v_ref, o_ref, lse_ref, do_ref, *rest):
        dq_out, dk_out, dv_out, dl_scr, dq_ref, dk_ref, dv_ref = rest[n_after:]
        hp, c = pl.program_id(0), pl.program_id(1)

        @pl.when(c == 0)
        def _():
            dk_ref[...] = jnp.zeros_like(dk_ref)
            dv_ref[...] = jnp.zeros_like(dv_ref)

        def delta(i, carry):
            rows = pl.ds(pl.multiple_of(i * rc, rc), rc)
            prod = do_ref[rows, :] * o_ref[rows, :]
            dl_scr[rows, :] = jnp.concatenate(
                [jnp.broadcast_to(jnp.sum(prod[:, _pair_cols(hh)], axis=1, keepdims=True), (rc, HEAD_DIM)) for hh in range(2)], axis=1)
            return carry

        lax.fori_loop(0, DIL_CHUNK // rc, delta, 0)

        for bi, (_, dil) in enumerate(DIL_PAIRS):
            tables = _dil_bias_tables(hp, dil)

            def block(blk, carry, bi=bi, dil=dil, tables=tables):
                q0, kcur0, kprev0, first = _dil_block_pos(blk, c, dil)
                qrows = _dil_rows(q0, dil)
                q2 = _stack_pair(q_ref[qrows, :] * DIL_SCALE)
                kcat = jnp.concatenate([k_ref[_dil_rows(kprev0, dil), :], k_ref[_dil_rows(kcur0, dil), :]], axis=0).astype(MXU_DTYPE)
                vcat = jnp.concatenate([v_ref[_dil_rows(kprev0, dil), :], v_ref[_dil_rows(kcur0, dil), :]], axis=0).astype(MXU_DTYPE)
                do2 = _stack_pair(do_ref[qrows, :])
                s = _dot(q2, kcat, 1, 1) + jnp.where(first, tables[1], tables[0])
                p = jnp.exp(s - _pair_column(lse_ref[qrows, :]))
                dp = _dot(do2, vcat, 1, 1)
                ds = (p * (dp - _pair_column(dl_scr[qrows, :]))).astype(MXU_DTYPE)
                dq_b = _unstack_pair(_dot(ds, kcat, 1, 0)) * DIL_SCALE
                dk_b = _dot(ds, q2, 0, 0)
                dv_b = _dot(p.astype(MXU_DTYPE), do2, 0, 0)
                if bi == 0:
                    dq_ref[qrows, :] = dq_b
                else:
                    dq_ref[qrows, :] += dq_b
                dk_ref[_dil_rows(kprev0, dil), :] += dk_b[:DIL_BLOCK]
                dv_ref[_dil_rows(kprev0, dil), :] += dv_b[:DIL_BLOCK]
                dk_ref[_dil_rows(kcur0, dil), :] += dk_b[DIL_BLOCK:]
                dv_ref[_dil_rows(kcur0, dil), :] += dv_b[DIL_BLOCK:]
                return carry

            lax.fori_loop(0, nblk, block, 0, unroll=DIL_UNROLL_BWD)

        dq_out[...] = dq_ref[...].astype(dq_out.dtype)

        @pl.when(c == nchunk - 1)
        def _():
            dk_out[...] = dk_ref[...].astype(dk_out.dtype)
            dv_out[...] = dv_ref[...].astype(dv_out.dtype)

    nq = DIL_WIDTH // LANES
    chunk = lambda off: pl.BlockSpec((DIL_CHUNK, LANES), lambda hp, c: (c, off + hp))
    whole = lambda off: pl.BlockSpec((seq, LANES), lambda hp, c: (0, off + hp))
    shp = jax.ShapeDtypeStruct((seq, DIL_WIDTH), MXU_DTYPE)
    return pl.pallas_call(
        body, name="dil_bwd", grid=(nq, nchunk),
        in_specs=[chunk(nq), whole(2 * nq), whole(3 * nq), chunk(0), chunk(0), chunk(0)] + [_ANY_SPEC] * n_after,
        out_specs=[chunk(0), whole(0), whole(0)], out_shape=[shp, shp, shp],
        scratch_shapes=[pltpu.VMEM((DIL_CHUNK, LANES), F32), pltpu.VMEM((DIL_CHUNK, LANES), F32),
                        pltpu.VMEM((seq, LANES), F32), pltpu.VMEM((seq, LANES), F32)],
        compiler_params=_params("parallel", "arbitrary"),
    )(h, h, h, o, lse, do, *after)


def _mm_dx0(parts, w_in_t, res, tm=1024, after=()):
    seq, d = res.shape
    tm = min(tm, seq)
    n_after = len(after)

    def body(a0, a1, a2, a3, b_ref, r_ref, *rest):
        o_ref = rest[n_after]
        acc = _dot(a0[...], b_ref[0:DH_PART, :], 1, 0)
        for c, a in enumerate((a1, a2, a3), start=1):
            acc += _dot(a[...], b_ref[DH_PART * c:DH_PART * (c + 1), :], 1, 0)
        o_ref[...] = acc + DN_ALPHA * r_ref[...]

    blk = pl.BlockSpec((tm, DH_PART), lambda i: (i, 0))
    row = pl.BlockSpec((tm, d), lambda i: (i, 0))
    return pl.pallas_call(
        body, name="mm_dx0", grid=(seq // tm,),
        in_specs=[blk] * 4 + [pl.BlockSpec((IN_PAD, d), lambda i: (0, 0), pipeline_mode=pl.Buffered(1)), row] + [_ANY_SPEC] * n_after,
        out_specs=row, out_shape=jax.ShapeDtypeStruct((seq, d), F32), compiler_params=_params("parallel"),
    )(*parts, w_in_t, res, *after)


def _mm_dw_in(parts, x0, tk=1024):
    seq, d = x0.shape
    tk = min(tk, seq)
    nk = seq // tk

    def body(a0, a1, a2, a3, b_ref, o_ref, acc_ref):
        kk = pl.program_id(0)

        @pl.when(kk == 0)
        def _():
            acc_ref[...] = jnp.zeros_like(acc_ref)

        b = b_ref[...].astype(MXU_DTYPE)
        for c, a in enumerate((a0, a1, a2, a3)):
            acc_ref[DH_PART * c:DH_PART * (c + 1), :] += _dot(a[...], b, 0, 0)

        @pl.when(kk == nk - 1)
        def _():
            o_ref[...] = acc_ref[...].astype(o_ref.dtype)

    blk = pl.BlockSpec((tk, DH_PART), lambda kk: (kk, 0))
    return pl.pallas_call(
        body, name="mm_dw_in", grid=(nk,), in_specs=[blk] * 4 + [pl.BlockSpec((tk, d), lambda kk: (kk, 0))],
        out_specs=pl.BlockSpec((IN_PAD, d), lambda kk: (0, 0)), out_shape=jax.ShapeDtypeStruct((IN_PAD, d), MXU_DTYPE),
        scratch_shapes=[pltpu.VMEM((IN_PAD, d), F32)], compiler_params=_params("arbitrary"),
    )(*parts, x0)


def _ln_stats(z):
    mu = jnp.mean(z, axis=-1, keepdims=True)
    zc = z - mu
    r = lax.rsqrt(jnp.mean(zc * zc, axis=-1, keepdims=True) + LN_EPS)
    return zc * r, r


def _ln_bwd_math(dy, xh, r, g):
    dxh = dy * g
    return r * (dxh - jnp.mean(dxh, axis=-1, keepdims=True) - xh * jnp.mean(dxh * xh, axis=-1, keepdims=True))


def _mix_ln1(o_mla, o_dil, w_o_mla, w_o_dil, x0, g, b, tm=512):
    seq, d = x0.shape
    tm = min(tm, seq)

    def body(om_ref, od_ref, wm_ref, wd_ref, x_ref, g_ref, b_ref, z_ref, y_ref, yb_ref):
        mix = _dot(od_ref[...], wd_ref[...], 1, 0)
        for hd in range(HEADS):
            mix += _dot(om_ref[hd], wm_ref[LANES * hd:LANES * (hd + 1), :], 1, 0)
        z = DN_ALPHA * x_ref[...] + mix
        xh, _ = _ln_stats(z)
        y = xh * g_ref[...] + b_ref[...]
        z_ref[...] = z
        y_ref[...] = y
        yb_ref[...] = y.astype(yb_ref.dtype)

    blk = pl.BlockSpec((tm, d), lambda i: (i, 0))
    vec = pl.BlockSpec((1, d), lambda i: (0, 0))
    shp = jax.ShapeDtypeStruct((seq, d), F32)
    return pl.pallas_call(
        body, name="mix_ln1", grid=(seq // tm,),
        in_specs=[pl.BlockSpec((HEADS, tm, LANES), lambda i: (0, i, 0)), pl.BlockSpec((tm, DIL_WIDTH), lambda i: (i, 0)),
                  pl.BlockSpec((HEADS * LANES, d), lambda i: (0, 0)), pl.BlockSpec((DIL_WIDTH, d), lambda i: (0, 0)), blk, vec, vec],
        out_specs=[blk, blk, blk], out_shape=[shp, shp, jax.ShapeDtypeStruct((seq, d), MXU_DTYPE)],
        compiler_params=_params("parallel"))(o_mla, o_dil, w_o_mla, w_o_dil, x0, g, b)


def _dx1_ln1_bwd(du, w_up_t, dz2, z, g, tm=256, after=()):
    seq, d = z.shape
    kdim = du.shape[1]
    tm = min(tm, seq)
    n_after = len(after)

    def body(du_ref, w_ref, r_ref, z_ref, g_ref, *rest):
        dz_ref, dzb_ref, dg_ref, db_ref = rest[n_after:]

        @pl.when(pl.program_id(0) == 0)
        def _():
            dg_ref[...] = jnp.zeros_like(dg_ref)
            db_ref[...] = jnp.zeros_like(db_ref)

        dyb = _dot(du_ref[...], w_ref[...], 1, 0) + DN_ALPHA * r_ref[...]
        xh, r = _ln_stats(z_ref[...])
        dg_ref[...] += jnp.sum(dyb * xh, axis=0, keepdims=True)
        db_ref[...] += jnp.sum(dyb, axis=0, keepdims=True)
        dz = _ln_bwd_math(dyb, xh, r, g_ref[...])
        dz_ref[...] = dz
        dzb_ref[...] = dz.astype(dzb_ref.dtype)

    blk = pl.BlockSpec((tm, d), lambda i: (i, 0))
    vec = pl.BlockSpec((1, d), lambda i: (0, 0))
    return pl.pallas_call(
        body, name="dx1_ln1_bwd", grid=(seq // tm,),
        in_specs=[pl.BlockSpec((tm, kdim), lambda i: (i, 0)),
                  pl.BlockSpec((kdim, d), lambda i: (0, 0), pipeline_mode=pl.Buffered(1)), blk, blk, vec] + [_ANY_SPEC] * n_after,
        out_specs=[blk, blk, vec, vec],
        out_shape=[jax.ShapeDtypeStruct((seq, d), F32), jax.ShapeDtypeStruct((seq, d), MXU_DTYPE),
                   jax.ShapeDtypeStruct((1, d), F32), jax.ShapeDtypeStruct((1, d), F32)],
        compiler_params=_params("arbitrary"))(du, w_up_t, dz2, z, g, *after)


def _down_ln2_loss_bwd(act, w_down, x1, target, g, b, tm=512):
    seq, d = x1.shape
    kdim = act.shape[1]
    tm = min(tm, seq)

    def body(a_ref, w_ref, x_ref, t_ref, g_ref, b_ref, dz_ref, dzb_ref, loss_ref, dg_ref, db_ref):
        @pl.when(pl.program_id(0) == 0)
        def _():
            loss_ref[...] = jnp.zeros_like(loss_ref)
            dg_ref[...] = jnp.zeros_like(dg_ref)
            db_ref[...] = jnp.zeros_like(db_ref)

        gv = g_ref[...]
        z = DN_ALPHA * x_ref[...] + _dot(a_ref[...], w_ref[...], 1, 0)
        xh, r = _ln_stats(z)
        err = (xh * gv + b_ref[...]) - t_ref[...]
        loss_ref[...] += 0.5 * jnp.sum(jnp.mean(err * err, axis=-1, keepdims=True), axis=0, keepdims=True)
        dy = err * (1.0 / d)
        dg_ref[...] += jnp.sum(dy * xh, axis=0, keepdims=True)
        db_ref[...] += jnp.sum(dy, axis=0, keepdims=True)
        dz = _ln_bwd_math(dy, xh, r, gv)
        dz_ref[...] = dz
        dzb_ref[...] = dz.astype(dzb_ref.dtype)

    blk = pl.BlockSpec((tm, d), lambda i: (i, 0))
    vec = pl.BlockSpec((1, d), lambda i: (0, 0))
    return pl.pallas_call(
        body, name="down_ln2_loss_bwd", grid=(seq // tm,),
        in_specs=[pl.BlockSpec((tm, kdim), lambda i: (i, 0)),
                  pl.BlockSpec((kdim, d), lambda i: (0, 0), pipeline_mode=pl.Buffered(1)), blk, blk, vec, vec],
        out_specs=[blk, blk, pl.BlockSpec((1, LANES), lambda i: (0, 0)), vec, vec],
        out_shape=[jax.ShapeDtypeStruct((seq, d), F32), jax.ShapeDtypeStruct((seq, d), MXU_DTYPE),
                   jax.ShapeDtypeStruct((1, LANES), F32),
                   jax.ShapeDtypeStruct((1, d), F32), jax.ShapeDtypeStruct((1, d), F32)],
        compiler_params=_params("arbitrary"))(act, w_down, x1, target, g, b)


HALO = 16


def _conv_rows(e, w_ref, b_ref):
    y = b_ref[...] + w_ref[0:1, :] * pltpu.roll(e, 2, 0)
    y = y + w_ref[1:2, :] * pltpu.roll(e, 1, 0)
    return y + w_ref[2:3, :] * e


_GELU_C = math.sqrt(2.0 / math.pi)
_GELU_A = 0.044715


def _gelu(x):
    return x * (0.5 + 0.5 * jnp.tanh(x * (_GELU_C + (_GELU_C * _GELU_A) * (x * x))))


CONV_TN = 256


def _ffn_interleave(a, axis):
    shp = a.shape
    a = a.reshape(shp[:axis] + (2, D_FF // CONV_TN, CONV_TN) + shp[axis + 1:])
    return jnp.swapaxes(a, axis, axis + 1).reshape(shp)


def _ffn_deinterleave(a, axis):
    shp = a.shape
    a = a.reshape(shp[:axis] + (D_FF // CONV_TN, 2, CONV_TN) + shp[axis + 1:])
    return jnp.swapaxes(a, axis, axis + 1).reshape(shp)


def _up_conv_gate_fwd(x1, w_up_t, conv_w, conv_b, tm=1024):
    seq, d = x1.shape
    tm = min(tm, seq)
    tn = CONV_TN

    def body(x_ref, xp_ref, wu_ref, w_ref, b_ref, u_ref, y_ref, o_ref):
        first = pl.program_id(0) == 0
        halo = jnp.where(first, jnp.zeros_like(xp_ref), xp_ref[...])
        e = _dot(jnp.concatenate([halo, x_ref[...]], axis=0), wu_ref[...], 1, 1)
        u_ref[...] = e[HALO:]
        y = _conv_rows(e, w_ref, b_ref)[HALO:]
        y_ref[...] = y
        o_ref[...] = (_gelu(y[:, tn:]) * y[:, :tn]).astype(o_ref.dtype)

    hb = tm // HALO
    return pl.pallas_call(
        body, name="up_conv_gate_fwd", grid=(seq // tm, D_FF // tn),
        in_specs=[pl.BlockSpec((tm, d), lambda i, j: (i, 0)),
                  pl.BlockSpec((HALO, d), lambda i, j: (jnp.maximum(i * hb - 1, 0), 0)),
                  pl.BlockSpec((2 * tn, d), lambda i, j: (j, 0)),
                  pl.BlockSpec((3, 2 * tn), lambda i, j: (0, j)), pl.BlockSpec((1, 2 * tn), lambda i, j: (0, j))],
        out_specs=[pl.BlockSpec((tm, 2 * tn), lambda i, j: (i, j)), pl.BlockSpec((tm, 2 * tn), lambda i, j: (i, j)),
                   pl.BlockSpec((tm, tn), lambda i, j: (i, j))],
        out_shape=[jax.ShapeDtypeStruct((seq, 2 * D_FF), F32), jax.ShapeDtypeStruct((seq, 2 * D_FF), F32),
                   jax.ShapeDtypeStruct((seq, D_FF), MXU_DTYPE)],
        compiler_params=_params("parallel", "arbitrary"),
    )(x1, x1, w_up_t, conv_w, conv_b)


def _conv_gate_bwd(u, y, d_act, conv_w, tm=1024):
    seq = u.shape[0]
    tm = min(tm, seq)
    tn = CONV_TN
    ni = seq // tm
    rows = tm + HALO

    def body(u_ref, y_ref, yn_ref, da_ref, dan_ref, w_ref, du_ref, dw_ref, db_ref):
        i = pl.program_id(1)
        last = i == ni - 1

        @pl.when(i == 0)
        def _():
            dw_ref[...] = jnp.zeros_like(dw_ref)
            db_ref[...] = jnp.zeros_like(db_ref)

        yy = jnp.concatenate([y_ref[...], yn_ref[...]], axis=0)
        ya, yg = yy[:, :tn], yy[:, tn:]
        dact = jnp.concatenate([da_ref[...].astype(F32), jnp.where(last, 0.0, dan_ref[...].astype(F32))], axis=0)
        yg2 = yg * yg
        th = jnp.tanh(yg * (_GELU_C + (_GELU_C * _GELU_A) * yg2))
        half = 0.5 + 0.5 * th
        gelu = yg * half
        gelu_grad = half + gelu * (1.0 - half) * (2.0 * _GELU_C + (6.0 * _GELU_C * _GELU_A) * yg2)
        dy = jnp.concatenate([dact * gelu, dact * ya * gelu_grad], axis=1)
        dy0, dy1, dy2 = dy[:tm], pltpu.roll(dy, rows - 1, 0)[:tm], pltpu.roll(dy, rows - 2, 0)[:tm]
        du_ref[...] = (w_ref[2:3, :] * dy0 + w_ref[1:2, :] * dy1 + w_ref[0:1, :] * dy2).astype(du_ref.dtype)
        ut = u_ref[...]
        dw_ref[0:1, :] += jnp.sum(dy2 * ut, axis=0, keepdims=True)
        dw_ref[1:2, :] += jnp.sum(dy1 * ut, axis=0, keepdims=True)
        dw_ref[2:3, :] += jnp.sum(dy0 * ut, axis=0, keepdims=True)
        db_ref[...] += jnp.sum(dy0, axis=0, keepdims=True)

    hb = tm // HALO
    nh = seq // HALO
    nxt = lambda j, i: (jnp.minimum((i + 1) * hb, nh - 1), j)
    tile = pl.BlockSpec((tm, 2 * tn), lambda j, i: (i, j))
    return pl.pallas_call(
        body, name="conv_gate_bwd", grid=(D_FF // tn, ni),
        in_specs=[tile, tile, pl.BlockSpec((HALO, 2 * tn), nxt),
                  pl.BlockSpec((tm, tn), lambda j, i: (i, j)), pl.BlockSpec((HALO, tn), nxt),
                  pl.BlockSpec((3, 2 * tn), lambda j, i: (0, j))],
        out_specs=[tile, pl.BlockSpec((3, 2 * tn), lambda j, i: (0, j)), pl.BlockSpec((1, 2 * tn), lambda j, i: (0, j))],
        out_shape=[jax.ShapeDtypeStruct((seq, 2 * D_FF), MXU_DTYPE), jax.ShapeDtypeStruct((3, 2 * D_FF), F32),
                   jax.ShapeDtypeStruct((1, 2 * D_FF), F32)],
        compiler_params=_params("parallel", "arbitrary"),
    )(u, y, y, d_act, d_act, conv_w)


def _pad_heads(w, width):
    w = jnp.transpose(w, (1, 0, 2))
    return jnp.pad(w, ((0, 0), (0, 0), (0, LANES - width))).astype(MXU_DTYPE)


def _heads_major(a):
    return jnp.transpose(a, (1, 0, 2)).reshape(-1, a.shape[2])


def _heads_minor(a, heads):
    return jnp.transpose(a.reshape(heads, -1, a.shape[1]), (1, 0, 2))


_LATENT = Q_RANK + KV_RANK
_ROPE_AT = _LATENT + NOPE_DIM
_ROPE_END = _ROPE_AT + ROPE_DIM


def _split_pad_rows(w_t):
    z = lambda n: jnp.zeros((n, w_t.shape[1]), w_t.dtype)
    return jnp.concatenate([w_t[:_LATENT], z(_ROPE_AT - _LATENT), w_t[_LATENT:_LATENT + ROPE_DIM], z(DH_PART - _ROPE_END),
                            w_t[_LATENT + ROPE_DIM:]], axis=0)


def _split_unpad_rows(w_p):
    return jnp.concatenate([w_p[:_LATENT], w_p[_ROPE_AT:_ROPE_END], w_p[DH_PART:]], axis=0)


def _pad_w_o(w_o):
    mla = jnp.pad(w_o[:MLA_WIDTH].reshape(HEADS, HEAD_DIM, D_MODEL), ((0, 0), (0, LANES - HEAD_DIM), (0, 0)))
    return mla.reshape(HEADS * LANES, D_MODEL).astype(MXU_DTYPE), w_o[MLA_WIDTH:].astype(MXU_DTYPE)


def _unpad_w_o(d_mla, d_dil):
    return jnp.concatenate([d_mla.reshape(HEADS, LANES, D_MODEL)[:, :HEAD_DIM].reshape(MLA_WIDTH, D_MODEL), d_dil], axis=0)


def _row(v):
    return v.reshape(1, -1).astype(F32)


def _layer_grads(x0, target, cw, first_after=(), late_weights=None, on_grads=None):
    seq = x0.shape[0]
    ctab, stab = _rope_tables(seq)
    gq, gk = cw["g_cq"], cw["g_ckv"]
    wq, wk, wv = cw["wq"], cw["wk"], cw["wv"]
    notify = (lambda stage, grads: ()) if on_grads is None else on_grads

    h = _mm(x0, cw["w_in_t"], name="mm_h", tb=True, tm=1024, tn=IN_PAD, tk=1024, after=first_after)
    qf, kf, vp = _mla_prep(h, gq, gk, wq, wk, wv, ctab, stab)
    o_mla, o_mla_b, lse_mla = _mla_attn_fwd(qf, kf, vp)
    o_dil, o_dil_b, lse_dil = _dil_fwd(h)
    fetch = (lambda stage, after: {}) if late_weights is None else late_weights
    cw = {**cw, **fetch("w_o", o_mla_b), **fetch("w_up", o_mla_b)}
    cb = cw["conv_b"]
    z1, x1, x1b = _mix_ln1(o_mla_b, o_dil_b, cw["w_o_mla"], cw["w_o_dil"], x0, cw["ln1_g"], cw["ln1_b"])
    u, y, act = _up_conv_gate_fwd(x1b, cw["w_up_t"], cw["conv_w"], cb)
    cw = {**cw, **fetch("w_down", act)}
    dz2, dz2b, loss, d_ln2_g, d_ln2_b = _down_ln2_loss_bwd(act, cw["w_down"], x1, target, cw["ln2_g"], cw["ln2_b"])

    d_act = _mm(dz2b, cw["w_down"], name="mm_d_act", tb=True, out_dtype=MXU_DTYPE, tm=1024, tn=D_FF, tk=1024)
    d_w_down = _mm(act, dz2b, name="mm_dw_down", ta=True, out_dtype=MXU_DTYPE, tm=1408, tn=1024, tk=1024)
    du, d_conv_w, d_conv_b = _conv_gate_bwd(u, y, d_act, cw["conv_w"])
    d_w_up_t = _mm(du, x1b, name="mm_dw_up", ta=True, out_dtype=MXU_DTYPE, tm=1408, tn=1024, tk=2048)
    grads = dict(w_up_t=d_w_up_t, w_down=d_w_down, conv_w=d_conv_w, conv_b=d_conv_b, ln2_g=d_ln2_g, ln2_b=d_ln2_b)
    dz1, dz1b, d_ln1_g, d_ln1_b = _dx1_ln1_bwd(du, cw["w_up_t"], dz2, z1, cw["ln1_g"], after=notify("ffn", grads))
    do_mla, do_dil, d_w_o_mla, d_w_o_dil = _w_o_bwd(dz1b, o_mla_b, o_dil_b, cw["w_o_mla"], cw["w_o_dil"])
    grads.update(w_o_mla=d_w_o_mla, w_o_dil=d_w_o_dil, ln1_g=d_ln1_g, ln1_b=d_ln1_b)
    dqf, dkf, dvf = _mla_attn_bwd(qf, kf, vp, o_mla, lse_mla, do_mla)
    dh_mla, d_wq, d_wk, d_wv, d_gq, d_gk = _mla_prep_bwd(h, gq, gk, wq, wk, wv, ctab, stab, dqf, dkf, dvf,
                                                          after=notify("w_o", grads))
    grads.update(wq=d_wq, wk=d_wk, wv=d_wv, g_cq=d_gq, g_ckv=d_gk, loss=loss)
    dq_dil, dk_dil, dv_dil = _dil_bwd(h, o_dil, lse_dil, do_dil, after=notify("mla", grads))
    dh = (dh_mla, dq_dil, dk_dil, dv_dil)
    grads.update(w_in_t=_mm_dw_in(dh, x0))
    grad_x = _mm_dx0(dh, cw["w_in_t"], dz1, after=notify("w_in", grads))
    return loss, grad_x, grads


def _all_gather(blocks, name):
    na = len(blocks)

    def body(*refs):
        ins, outs = refs[:na], refs[na:2 * na]
        send_sems, recv_sems, local_sems = refs[2 * na:]
        x, y, c = lax.axis_index("x"), lax.axis_index("y"), lax.axis_index("c")
        me, sibling = (x, y, c), (x, y, 1 - c)
        chips = [(1 - x, y), (x, 1 - y), (1 - x, 1 - y)]

        def slot(out, pos):
            return out.at[4 * pos[0] + 2 * pos[1] + pos[2]]

        def copy(a, k, block, to, src=None):
            return pltpu.make_async_remote_copy(
                src_ref=slot(outs[a], block) if src is None else src, dst_ref=slot(outs[a], block),
                send_sem=send_sems.at[7 * a + k], recv_sem=recv_sems.at[7 * a + k],
                device_id=to, device_id_type=pl.DeviceIdType.MESH)

        mine = [pltpu.make_async_copy(ins[a], slot(outs[a], me), local_sems.at[a]) for a in range(na)]
        for cp in mine:
            cp.start()
        first = []
        for a in range(na):
            first.append(copy(a, 0, me, sibling, src=ins[a]))
            first += [copy(a, 1 + j, me, (*chip, c), src=ins[a]) for j, chip in enumerate(chips)]
        for cp in first:
            cp.start()
        passed = []
        for j, chip in enumerate(chips):
            for a in range(na):
                copy(a, 1 + j, (*chip, c), me).wait_recv()
                cp = copy(a, 4 + j, (*chip, c), sibling)
                cp.start()
                passed.append(cp)
        for a in range(na):
            copy(a, 0, sibling, me).wait_recv()
            for j, chip in enumerate(chips):
                copy(a, 4 + j, (*chip, 1 - c), me).wait_recv()
        for cp in first + passed:
            cp.wait_send()
        for cp in mine:
            cp.wait()

    any_spec = pl.BlockSpec(memory_space=pl.ANY)
    return pl.pallas_call(
        body, name=name, in_specs=[any_spec] * na, out_specs=[any_spec] * na,
        out_shape=[jax.ShapeDtypeStruct((N_DEV,) + b.shape, b.dtype) for b in blocks],
        scratch_shapes=[pltpu.SemaphoreType.DMA((7 * na,)), pltpu.SemaphoreType.DMA((7 * na,)), pltpu.SemaphoreType.DMA((na,))],
    )(*blocks)


_HBM_SPEC = pl.BlockSpec(memory_space=pltpu.HBM)
_SEM_SPEC = pl.BlockSpec(memory_space=pltpu.SEMAPHORE)
_DATAFLOW = pltpu.CompilerParams(has_side_effects=pltpu.SideEffectType.DATAFLOW_SIDE_EFFECTING)


def _split_copies(which, ins, lands, send_sems, recv_sems, gather):
    x, y, c = lax.axis_index("x"), lax.axis_index("y"), lax.axis_index("c")
    me = 4 * x + 2 * y + c
    copies = []
    for a, src, land in zip(which, ins, lands):
        for d in range(1, N_DEV):
            px, py, pc = x ^ (d >> 2), y ^ ((d >> 1) & 1), c ^ (d & 1)
            copies.append(pltpu.make_async_remote_copy(
                src_ref=src if gather[a] else src.at[4 * px + 2 * py + pc], dst_ref=land.at[me],
                send_sem=send_sems.at[7 * a + d - 1], recv_sem=recv_sems.at[7 * a + d - 1],
                device_id=(px, py, pc), device_id_type=pl.DeviceIdType.MESH))
    return copies


def _own_copies(which, ins, lands, own_sems, gather):
    me = 4 * lax.axis_index("x") + 2 * lax.axis_index("y") + lax.axis_index("c")
    return [pltpu.make_async_copy(src if gather[a] else src.at[me], land.at[me], own_sems.at[a])
            for a, src, land in zip(which, ins, lands)]


def _send_start(srcs, gather, name):
    na = len(srcs)
    assert len(gather) == na
    land_types = [pltpu.HBM(((N_DEV,) + s.shape) if g else s.shape, s.dtype) for s, g in zip(srcs, gather)]

    def body(*refs):
        ins, lands = refs[:na], refs[na:2 * na]
        send_sems, recv_sems, own_sems, token = refs[2 * na], refs[2 * na + 1], refs[2 * na + 2], refs[-1]
        for cp in _split_copies(range(na), ins, lands, send_sems, recv_sems, gather):
            cp.start()
        for cp in _own_copies(range(na), ins, lands, own_sems, gather):
            cp.start()
        token[...] = jnp.zeros_like(token)

    hbm = lambda a: pltpu.with_memory_space_constraint(a, pltpu.HBM)
    outs = pl.pallas_call(
        body, name=name,
        out_shape=(pltpu.SemaphoreType.DMA((7 * na,)), pltpu.SemaphoreType.DMA((7 * na,)), pltpu.SemaphoreType.DMA((na,)),
                   *[pltpu.HBM(s.shape, s.dtype) for s in srcs], *land_types, jax.ShapeDtypeStruct((8, LANES), F32)),
        in_specs=[_HBM_SPEC] * (2 * na),
        out_specs=(_SEM_SPEC, _SEM_SPEC, _SEM_SPEC, *[_HBM_SPEC] * (2 * na), pl.BlockSpec(memory_space=pltpu.VMEM)),
        input_output_aliases={i: 3 + i for i in range(2 * na)}, compiler_params=_DATAFLOW,
    )(*[hbm(s) for s in srcs], *[hbm(lax.empty(t.shape, t.dtype)) for t in land_types])
    return dict(send=outs[0], recv=outs[1], own=outs[2], srcs=list(outs[3:3 + na]), lands=list(outs[3 + na:3 + 2 * na]),
                token=outs[-1], gather=gather)


def _send_wait(handle, after, name, only=None):
    which = list(range(len(handle["srcs"]))) if only is None else list(only)
    na = len(which)
    gather = handle["gather"]
    after = list(after)

    def body(*refs):
        ins, lands = refs[:na], refs[na:2 * na]
        send_sems, recv_sems, own_sems = refs[2 * na], refs[2 * na + 1], refs[2 * na + 2]
        for cp in _split_copies(which, ins, lands, send_sems, recv_sems, gather):
            cp.wait_send()
            cp.wait_recv()
        for cp in _own_copies(which, ins, lands, own_sems, gather):
            cp.wait()

    both = [handle["srcs"][a] for a in which] + [handle["lands"][a] for a in which]
    outs = pl.pallas_call(
        body, name=name, out_shape=[pltpu.HBM(a.shape, a.dtype) for a in both],
        in_specs=[_HBM_SPEC] * (2 * na) + [_SEM_SPEC] * 3 + [_ANY_SPEC] * len(after),
        out_specs=[_HBM_SPEC] * (2 * na), input_output_aliases={i: i for i in range(2 * na)}, compiler_params=_DATAFLOW,
    )(*both, handle["send"], handle["recv"], handle["own"], *after)
    return list(outs[na:])


def _sum_slots(p_ref):
    g = p_ref[0].astype(F32)
    for s in range(1, p_ref.shape[0]):
        g = g + p_ref[s].astype(F32)
    return g


def _adamw_refs(g, w_ref, m_ref, v_ref, g_out, d_out, m_out, v_out):
    c1 = 1.0 - ADAM_B1 ** ADAM_STEP
    c2 = 1.0 - ADAM_B2 ** ADAM_STEP
    m_new = ADAM_B1 * m_ref[...] + (1.0 - ADAM_B1) * g
    v_new = ADAM_B2 * v_ref[...] + (1.0 - ADAM_B2) * (g * g)
    g_out[...] = g
    m_out[...] = m_new
    v_out[...] = v_new
    d_out[...] = -ADAM_LR * ((m_new / c1) / (jnp.sqrt(v_new / c2) + ADAM_EPS) + ADAM_WD * w_ref[...])


def _adamw(parts, w, m, v, name):
    npart, r, n = parts.shape
    tr = r if r <= 256 else max(t for t in range(16, 257, 16) if r % t == 0)

    def body(p_ref, w_ref, m_ref, v_ref, g_out, d_out, m_out, v_out):
        _adamw_refs(_sum_slots(p_ref), w_ref, m_ref, v_ref, g_out, d_out, m_out, v_out)

    blk = pl.BlockSpec((tr, n), lambda i: (i, 0))
    shp = jax.ShapeDtypeStruct((r, n), F32)
    return pl.pallas_call(
        body, name=name, grid=(r // tr,), in_specs=[pl.BlockSpec((npart, tr, n), lambda i: (0, i, 0)), blk, blk, blk],
        out_specs=[blk] * 4, out_shape=[shp] * 4, compiler_params=_params("parallel"),
    )(parts, w, m, v)


def _adamw_small(parts, ws, ms, vs, loss_parts, name):
    n = len(parts)

    def body(*refs):
        ins, outs = refs[:4 * n + 1], refs[4 * n + 1:]
        for i in range(n):
            _adamw_refs(_sum_slots(ins[i]), ins[n + i], ins[2 * n + i], ins[3 * n + i], *outs[4 * i:4 * i + 4])
        outs[4 * n][...] = _sum_slots(ins[4 * n])

    out_shape = [jax.ShapeDtypeStruct(w.shape, F32) for w in ws for _ in range(4)]
    res = pl.pallas_call(body, name=name, out_shape=out_shape + [jax.ShapeDtypeStruct((1, LANES), F32)],
                         compiler_params=_params())(*parts, *ws, *ms, *vs, loss_parts)
    return [res[4 * i:4 * i + 4] for i in range(n)], res[4 * n]


REPLICATED = ("g_cq", "g_ckv", "w_uk", "w_uv", "ln1_g", "ln1_b", "conv_b", "ln2_g", "ln2_b")
ALL_WEIGHTS = ("w_in", "g_cq", "g_ckv", "w_uq", "w_uk", "w_uv", "w_o", "ln1_g", "ln1_b", "w_up", "conv_w", "conv_b",
               "w_down", "ln2_g", "ln2_b")


def kernel(x, w_in, g_cq, g_ckv, w_uq, w_uk, w_uv, w_o, ln1_g, ln1_b, w_up, conv_w, conv_b, w_down, ln2_g, ln2_b, loss_target, m_w_in, m_g_cq, m_g_ckv, m_w_uq, m_w_uk, m_w_uv, m_w_o, m_ln1_g, m_ln1_b, m_w_up, m_conv_w, m_conv_b, m_w_down, m_ln2_g, m_ln2_b, v_w_in, v_g_cq, v_g_ckv, v_w_uq, v_w_uk, v_w_uv, v_w_o, v_ln1_g, v_ln1_b, v_w_up, v_conv_w, v_conv_b, v_w_down, v_ln2_g, v_ln2_b):
    w = dict(w_in=w_in, g_cq=g_cq, g_ckv=g_ckv, w_uq=w_uq, w_uk=w_uk, w_uv=w_uv, w_o=w_o, ln1_g=ln1_g, ln1_b=ln1_b,
             w_up=w_up, conv_w=conv_w, conv_b=conv_b, w_down=w_down, ln2_g=ln2_g, ln2_b=ln2_b)
    m = dict(w_in=m_w_in, g_cq=m_g_cq, g_ckv=m_g_ckv, w_uq=m_w_uq, w_uk=m_w_uk, w_uv=m_w_uv, w_o=m_w_o, ln1_g=m_ln1_g,
             ln1_b=m_ln1_b, w_up=m_w_up, conv_w=m_conv_w, conv_b=m_conv_b, w_down=m_w_down, ln2_g=m_ln2_g, ln2_b=m_ln2_b)
    v = dict(w_in=v_w_in, g_cq=v_g_cq, g_ckv=v_g_ckv, w_uq=v_w_uq, w_uk=v_w_uk, w_uv=v_w_uv, w_o=v_w_o, ln1_g=v_ln1_g,
             ln1_b=v_ln1_b, w_up=v_w_up, conv_w=v_conv_w, conv_b=v_conv_b, w_down=v_w_down, ln2_g=v_ln2_g, ln2_b=v_ln2_b)
    me = 4 * lax.axis_index("x") + 2 * lax.axis_index("y") + lax.axis_index("c")
    wire = lambda a: a.astype(WIRE_DTYPE)
    pad_taps = lambda a: jnp.pad(a, ((0, 8 - a.shape[0]), (0, 0)))

    blocks = lambda a: wire(a).reshape((N_DEV, a.shape[0] // N_DEV) + a.shape[1:])

    g_in, g_uq, g_conv = _all_gather(
        [wire(w_in).T, _heads_major(wire(w_uq)), pad_taps(conv_w)],
        "gather_weights")
    late = _send_start([wire(w_o), wire(w_up).T, wire(w_down)], [True] * 3, "gather_late_start")
    r_uq_dev, e_uq = w_uq.shape[0], w_uq.shape[2]
    wq = jnp.transpose(g_uq.reshape(N_DEV, HEADS, r_uq_dev, e_uq), (1, 0, 2, 3)).reshape(HEADS, Q_RANK, e_uq)
    cw = dict(
        w_in_t=_split_pad_rows(g_in.reshape(-1, D_MODEL)).astype(MXU_DTYPE),
        wq=jnp.pad(wq, ((0, 0), (0, 0), (0, LANES - e_uq))).astype(MXU_DTYPE),
        wk=_pad_heads(w_uk, NOPE_DIM), wv=_pad_heads(w_uv, HEAD_DIM),
        conv_w=_ffn_interleave(jnp.transpose(g_conv[:, :conv_w.shape[0]], (1, 0, 2)).reshape(conv_w.shape[0], -1), 1),
        g_cq=_row(g_cq), g_ckv=_row(g_ckv), ln1_g=_row(ln1_g), ln1_b=_row(ln1_b), conv_b=_ffn_interleave(_row(conv_b), 1),
        ln2_g=_row(ln2_g), ln2_b=_row(ln2_b))

    def late_weights(stage, after):
        (got,) = _send_wait(late, [after], f"gather_{stage}_wait", only=[("w_o", "w_up", "w_down").index(stage)])
        full = got.reshape(-1, D_MODEL)
        if stage == "w_o":
            w_o_mla, w_o_dil = _pad_w_o(full)
            return dict(w_o_mla=w_o_mla, w_o_dil=w_o_dil)
        if stage == "w_up":
            return dict(w_up_t=_ffn_interleave(full, 0).astype(MXU_DTYPE))
        return dict(w_down=full.astype(MXU_DTYPE))

    sent = {}

    def on_grads(stage, g):
        if stage == "ffn":
            sent[stage] = _send_start([blocks(_ffn_deinterleave(g["w_up_t"], 0)), blocks(g["w_down"])], [False] * 2,
                                      "exchange_ffn_start")
        elif stage == "w_o":
            return []
        elif stage == "mla":
            d_uq = wire(jnp.transpose(g["wq"][:, :, :e_uq].reshape(HEADS, N_DEV, r_uq_dev, e_uq), (1, 0, 2, 3))
                        ).reshape(N_DEV, HEADS * r_uq_dev, e_uq)
            dense = lambda a, width: wire(jnp.transpose(a[:, :, :width], (0, 2, 1))).reshape(-1, a.shape[1])
            small = dict(g_cq=g["g_cq"], g_ckv=g["g_ckv"], w_uk=dense(g["wk"], NOPE_DIM), w_uv=dense(g["wv"], HEAD_DIM),
                         ln1_g=g["ln1_g"], ln1_b=g["ln1_b"], conv_b=_ffn_deinterleave(g["conv_b"], 1), ln2_g=g["ln2_g"],
                         ln2_b=g["ln2_b"])
            everyone = [small[n] for n in REPLICATED] + [_ffn_deinterleave(g["conv_w"], 1), g["loss"]]
            sent[stage] = _send_start([blocks(_unpad_w_o(g["w_o_mla"], g["w_o_dil"])), d_uq] + everyone,
                                      [False] * 2 + [True] * len(everyone), "exchange_mla_start")
        else:
            sent[stage] = _send_start([blocks(_split_unpad_rows(g["w_in_t"]))], [False], "exchange_w_in_start")
        return [sent[stage]["token"]]

    _, grad_x, _ = _layer_grads(x[0], loss_target[0], cw, [late["token"]], late_weights, on_grads)


    out = {}

    def update(name, parts, view=None):
        to2d = {None: lambda a: a, "t": lambda a: a.T, "heads": _heads_major}[view]
        back = {None: lambda a: a, "t": lambda a: a.T, "heads": lambda a: _heads_minor(a, HEADS)}[view]
        res = _adamw(parts, to2d(w[name]), to2d(m[name]), to2d(v[name]), "adamw_" + name)
        for kind, a in zip(("grad", "delta", "new_m", "new_v"), res):
            out[kind, name] = back(a)
        return res[0]

    r_up, r_down = _send_wait(sent["ffn"], [grad_x], "exchange_ffn_wait")
    r_o, r_uq, *rep_all, cw_all, loss_all = _send_wait(sent["mla"], [grad_x], "exchange_mla_wait")
    done = [update("w_up", r_up, "t"), update("w_down", r_down), update("w_o", r_o), update("w_uq", r_uq, "heads")]
    rank_minor = ("w_uk", "w_uv")
    two_d = lambda n, a: jnp.transpose(a, (1, 2, 0)).reshape(-1, a.shape[0]) if n in rank_minor else a.reshape(1, -1)
    res, loss_sum = _adamw_small(rep_all, *[[two_d(n, d[n]) for n in REPLICATED] for d in (w, m, v)], loss_all, "adamw_replicated")
    for n, quad in zip(REPLICATED, res):
        for kind, a in zip(("grad", "delta", "new_m", "new_v"), quad):
            out[kind, n] = jnp.transpose(a.reshape(HEADS, -1, a.shape[1]), (2, 0, 1)) if n in rank_minor else a.reshape(w[n].shape)
    loss = loss_sum[0, 0]
    ncw = conv_w.shape[1]
    done += [loss_sum, update("conv_w", lax.dynamic_slice_in_dim(cw_all[:, :conv_w.shape[0]], me * ncw, ncw, axis=2))]
    (r_in,) = _send_wait(sent["w_in"], done, "exchange_w_in_wait")
    update("w_in", r_in, "t")

    return (loss, grad_x[None], *[out[kind, n] for kind in ("grad", "delta", "new_m", "new_v") for n in ALL_WEIGHTS])
```

```python
import math

import jax
import jax.numpy as jnp
import numpy as np
from jax import lax
from jax.experimental import pallas as pl
from jax.experimental.pallas import tpu as pltpu

F32 = jnp.float32
MXU_DTYPE = jnp.bfloat16
WIRE_DTYPE = jnp.bfloat16

N_DEV = 8
D_MODEL = 1024
HEADS = 8
HEAD_DIM = 64
LANES = 128
Q_RANK, KV_RANK, ROPE_DIM, NOPE_DIM = 256, 128, 32, 64
DIL_WIDTH = HEADS * HEAD_DIM
MLA_WIDTH = HEADS * HEAD_DIM
IN_PAD = 2048
DH_PART = IN_PAD // 4
D_FF = 2816
ROPE_THETA = 10000.0
DIL_PAIRS = ((128, 1), (512, 4), (2048, 16))
DIL_BLOCK = 128
DN_ALPHA = 2.0 ** 0.25
LN_EPS = 1e-5
RMS_EPS = 1e-6
ONES_LANE = HEAD_DIM
MLA_SCALE = 1.0 / math.sqrt(NOPE_DIM + ROPE_DIM)
MLA_SCALE_LOG2 = MLA_SCALE * math.log2(math.e)
MLA_BWD_SPLITS = 2
MLA_BWD_SPLITS_DIAGONAL = 4
MLA_FWD_SPLITS_DIAGONAL = 2
DIL_SCALE = 1.0 / math.sqrt(HEAD_DIM)
ALIBI_SLOPES = tuple(2.0 ** (-8.0 * (h + 1) / HEADS) for h in range(HEADS))
NEG_BIG = -1e30
ADAM_LR, ADAM_B1, ADAM_B2, ADAM_EPS, ADAM_WD, ADAM_STEP = 0.001, 0.9, 0.999, 1e-08, 0.01, 10
VMEM_LIMIT = 48 * 1024 * 1024


def _params(*sem):
    return pltpu.CompilerParams(dimension_semantics=sem or None, vmem_limit_bytes=VMEM_LIMIT)


def _dot(a, b, ca, cb):
    return lax.dot_general(a, b, (((ca,), (cb,)), ((), ())), preferred_element_type=F32)


_ANY_SPEC = pl.BlockSpec(memory_space=pl.ANY)


def _mm(a, b, *, name, tm, tn, tk, ta=False, tb=False, out_dtype=F32, after=()):
    m, k = (a.shape[1], a.shape[0]) if ta else a.shape
    n = b.shape[0] if tb else b.shape[1]
    assert (b.shape[1] if tb else b.shape[0]) == k
    tm, tn, tk = min(tm, m), min(tn, n), min(tk, k)
    assert m % tm == 0 and n % tn == 0 and k % tk == 0, (name, m, n, k, tm, tn, tk)
    nk = k // tk
    a_spec = (pl.BlockSpec((tk, tm), lambda i, j, kk: (kk, i)) if ta
              else pl.BlockSpec((tm, tk), lambda i, j, kk: (i, kk)))
    b_mode = dict(pipeline_mode=pl.Buffered(1)) if (tn == n and tk == k) else {}
    b_spec = (pl.BlockSpec((tn, tk), lambda i, j, kk: (j, kk), **b_mode) if tb
              else pl.BlockSpec((tk, tn), lambda i, j, kk: (kk, j), **b_mode))
    o_spec = pl.BlockSpec((tm, tn), lambda i, j, kk: (i, j))
    n_in = 2 + len(after)
    ca, cb = (0 if ta else 1), (1 if tb else 0)

    def body(*refs):
        a_ref, b_ref, o_ref = refs[0], refs[1], refs[n_in]
        part = _dot(a_ref[...].astype(MXU_DTYPE), b_ref[...].astype(MXU_DTYPE), ca, cb)
        if nk == 1:
            o_ref[...] = part.astype(o_ref.dtype)
            return
        acc_ref = refs[-1]
        kk = pl.program_id(2)

        @pl.when(kk == 0)
        def _():
            acc_ref[...] = part

        @pl.when(kk > 0)
        def _():
            acc_ref[...] += part

        @pl.when(kk == nk - 1)
        def _():
            o_ref[...] = acc_ref[...].astype(o_ref.dtype)

    return pl.pallas_call(
        body, name=name, grid=(m // tm, n // tn, nk), in_specs=[a_spec, b_spec] + [_ANY_SPEC] * len(after), out_specs=o_spec,
        out_shape=jax.ShapeDtypeStruct((m, n), out_dtype),
        scratch_shapes=[pltpu.VMEM((tm, tn), F32)] if nk > 1 else [],
        compiler_params=_params("parallel", "parallel", "arbitrary"),
    )(a, b, *after)


def _w_o_bwd(dz, o_mla, o_dil, w_o_mla, w_o_dil, tm=1024):
    seq, d = dz.shape
    tm = min(tm, seq)
    nstep = seq // tm

    def body(a_ref, om_ref, od_ref, wm_ref, wd_ref, dom_ref, dod_ref, dwm_ref, dwd_ref, accm_ref, accd_ref):
        step = pl.program_id(0)

        @pl.when(step == 0)
        def _():
            accm_ref[...] = jnp.zeros_like(accm_ref)
            accd_ref[...] = jnp.zeros_like(accd_ref)

        a = a_ref[...]
        for hd in range(HEADS):
            rows = slice(LANES * hd, LANES * (hd + 1))
            dom_ref[hd] = _dot(a, wm_ref[rows, :], 1, 1)
            accm_ref[rows, :] += _dot(om_ref[hd], a, 0, 0)
        dod_ref[...] = _dot(a, wd_ref[...], 1, 1)
        accd_ref[...] += _dot(od_ref[...], a, 0, 0)

        @pl.when(step == nstep - 1)
        def _():
            dwm_ref[...] = accm_ref[...].astype(dwm_ref.dtype)
            dwd_ref[...] = accd_ref[...].astype(dwd_ref.dtype)

    once = dict(pipeline_mode=pl.Buffered(1))
    return pl.pallas_call(
        body, name="w_o_bwd", grid=(nstep,),
        in_specs=[pl.BlockSpec((tm, d), lambda i: (i, 0)), pl.BlockSpec((HEADS, tm, LANES), lambda i: (0, i, 0)),
                  pl.BlockSpec((tm, DIL_WIDTH), lambda i: (i, 0)), pl.BlockSpec((HEADS * LANES, d), lambda i: (0, 0), **once),
                  pl.BlockSpec((DIL_WIDTH, d), lambda i: (0, 0), **once)],
        out_specs=[pl.BlockSpec((HEADS, tm, LANES), lambda i: (0, i, 0)), pl.BlockSpec((tm, DIL_WIDTH), lambda i: (i, 0)),
                   pl.BlockSpec((HEADS * LANES, d), lambda i: (0, 0)), pl.BlockSpec((DIL_WIDTH, d), lambda i: (0, 0))],
        out_shape=[jax.ShapeDtypeStruct((HEADS, seq, LANES), F32), jax.ShapeDtypeStruct((seq, DIL_WIDTH), F32),
                   jax.ShapeDtypeStruct((HEADS * LANES, d), MXU_DTYPE), jax.ShapeDtypeStruct((DIL_WIDTH, d), MXU_DTYPE)],
        scratch_shapes=[pltpu.VMEM((HEADS * LANES, d), F32), pltpu.VMEM((DIL_WIDTH, d), F32)],
        compiler_params=_params("arbitrary"),
    )(dz, o_mla, o_dil, w_o_mla, w_o_dil)


def _rope_tables(seq):
    half = ROPE_DIM // 2
    f32 = np.float32
    freqs = np.power(f32(ROPE_THETA), -np.arange(half, dtype=f32) / f32(half))
    ang = np.arange(seq, dtype=f32)[:, None] * freqs[None, :]
    cos, sin = np.cos(ang, dtype=f32), np.sin(ang, dtype=f32)
    one = np.ones((seq, NOPE_DIM), f32)
    tail = np.ones((seq, LANES - NOPE_DIM - ROPE_DIM), f32)
    ctab = np.concatenate([one, cos, cos, tail], axis=1)
    stab = np.concatenate([0 * one, -sin, sin, 0 * tail], axis=1)
    return jnp.asarray(ctab), jnp.asarray(stab)


def _rope_swap(t):
    lane = lax.broadcasted_iota(jnp.int32, t.shape, 1)
    half = ROPE_DIM // 2
    return jnp.where(lane < NOPE_DIM + half, pltpu.roll(t, LANES - half, 1), pltpu.roll(t, half, 1))


def _rope(t, ctab, stab):
    return t * ctab + _rope_swap(t) * stab


def _rope_inv(t, ctab, stab):
    return t * ctab - _rope_swap(t) * stab


def _rms(x, g):
    r = lax.rsqrt(jnp.mean(x * x, axis=-1, keepdims=True) + RMS_EPS)
    xh = x * r
    return xh, r, xh * g


def _mla_prep(h, g_cq, g_ckv, wq, wk, wv, ctab, stab, tm=512):
    seq = h.shape[0]
    tm = min(tm, seq)

    def body(h_ref, gq_ref, gk_ref, wq_ref, wk_ref, wv_ref, c_ref, s_ref, q_out, k_out, v_out):
        hb = h_ref[...]
        ctab_, stab_ = c_ref[...], s_ref[...]
        _, _, cqn = _rms(hb[:, :Q_RANK], gq_ref[...])
        _, _, ckn = _rms(hb[:, Q_RANK:Q_RANK + KV_RANK], gk_ref[...])
        cqn = cqn.astype(MXU_DTYPE)
        ckn = ckn.astype(MXU_DTYPE)
        krr = _rope(hb[:, Q_RANK + KV_RANK:], ctab_, stab_)
        ones_lane = (lax.broadcasted_iota(jnp.int32, (1, LANES), 1) == ONES_LANE).astype(F32)
        for hd in range(HEADS):
            q = _dot(cqn, wq_ref[hd], 1, 0)
            q_out[hd] = _rope(q, ctab_, stab_).astype(q_out.dtype)
            k_out[hd] = (_dot(ckn, wk_ref[hd], 1, 0) + krr).astype(k_out.dtype)
            v_out[hd] = (_dot(ckn, wv_ref[hd], 1, 0) + ones_lane).astype(v_out.dtype)

    full = lambda *shape: pl.BlockSpec(shape, lambda i: (0,) * len(shape))
    slab = pl.BlockSpec((HEADS, tm, LANES), lambda i: (0, i, 0))
    shp = jax.ShapeDtypeStruct((HEADS, seq, LANES), MXU_DTYPE)
    return pl.pallas_call(
        body, name="mla_prep", grid=(seq // tm,),
        in_specs=[pl.BlockSpec((tm, DH_PART), lambda i: (i, 0)), full(1, Q_RANK), full(1, KV_RANK),
                  full(HEADS, Q_RANK, LANES), full(HEADS, KV_RANK, LANES), full(HEADS, KV_RANK, LANES),
                  pl.BlockSpec((tm, LANES), lambda i: (i, 0)), pl.BlockSpec((tm, LANES), lambda i: (i, 0))],
        out_specs=[slab, slab, slab], out_shape=[shp, shp, shp],
        compiler_params=_params("parallel"),
    )(h, g_cq, g_ckv, wq, wk, wv, ctab, stab)


def _mla_prep_bwd(h, g_cq, g_ckv, wq, wk, wv, ctab, stab, dq, dk, dv, tm=512, after=()):
    seq = h.shape[0]
    tm = min(tm, seq)
    n_after = len(after)

    def body(h_ref, gq_ref, gk_ref, wq_ref, wk_ref, wv_ref, c_ref, s_ref, dq_ref, dk_ref, dv_ref, *rest):
        dh_ref, dwq_ref, dwk_ref, dwv_ref, dgq_ref, dgk_ref = rest[n_after:]

        @pl.when(pl.program_id(0) == 0)
        def _():
            for r in (dwq_ref, dwk_ref, dwv_ref, dgq_ref, dgk_ref):
                r[...] = jnp.zeros_like(r)

        hb = h_ref[...]
        ctab_, stab_ = c_ref[...], s_ref[...]
        gq, gk = gq_ref[...], gk_ref[...]
        xq, rq, cqn = _rms(hb[:, :Q_RANK], gq)
        xk, rk, ckn = _rms(hb[:, Q_RANK:Q_RANK + KV_RANK], gk)
        cqn = cqn.astype(MXU_DTYPE)
        ckn = ckn.astype(MXU_DTYPE)
        d_cqn = jnp.zeros((tm, Q_RANK), F32)
        d_ckn = jnp.zeros((tm, KV_RANK), F32)
        d_krr = jnp.zeros((tm, LANES), F32)
        for hd in range(HEADS):
            dqh = _rope_inv(dq_ref[hd], ctab_, stab_).astype(MXU_DTYPE)
            d_cqn += _dot(dqh, wq_ref[hd], 1, 1)
            dwq_ref[hd] += _dot(cqn, dqh, 0, 0)
            dkh = dk_ref[hd]
            d_krr += dkh
            dkh = dkh.astype(MXU_DTYPE)
            d_ckn += _dot(dkh, wk_ref[hd], 1, 1)
            dwk_ref[hd] += _dot(ckn, dkh, 0, 0)
            dvh = dv_ref[hd].astype(MXU_DTYPE)
            d_ckn += _dot(dvh, wv_ref[hd], 1, 1)
            dwv_ref[hd] += _dot(ckn, dvh, 0, 0)
        lane = lax.broadcasted_iota(jnp.int32, (tm, LANES), 1)
        rot = (lane >= NOPE_DIM) & (lane < NOPE_DIM + ROPE_DIM)
        d_kr = jnp.where(rot, _rope_inv(jnp.where(rot, d_krr, 0.0), ctab_, stab_), 0.0)

        def rms_bwd(dy, xh, r, g, dg_ref):
            dg_ref[...] += jnp.sum(dy * xh, axis=0, keepdims=True)
            dxh = dy * g
            return r * (dxh - xh * jnp.mean(dxh * xh, axis=-1, keepdims=True))

        d_cq = rms_bwd(d_cqn, xq, rq, gq, dgq_ref)
        d_ck = rms_bwd(d_ckn, xk, rk, gk, dgk_ref)
        dh_ref[...] = jnp.concatenate([d_cq, d_ck, d_kr], axis=1).astype(dh_ref.dtype)

    full = lambda *shape: pl.BlockSpec(shape, lambda i: (0,) * len(shape))
    slab = pl.BlockSpec((HEADS, tm, LANES), lambda i: (0, i, 0))
    return pl.pallas_call(
        body, name="mla_prep_bwd", grid=(seq // tm,),
        in_specs=[pl.BlockSpec((tm, DH_PART), lambda i: (i, 0)), full(1, Q_RANK), full(1, KV_RANK),
                  full(HEADS, Q_RANK, LANES), full(HEADS, KV_RANK, LANES), full(HEADS, KV_RANK, LANES),
                  pl.BlockSpec((tm, LANES), lambda i: (i, 0)), pl.BlockSpec((tm, LANES), lambda i: (i, 0)),
                  slab, slab, slab] + [_ANY_SPEC] * n_after,
        out_specs=[pl.BlockSpec((tm, DH_PART), lambda i: (i, 0)), full(HEADS, Q_RANK, LANES), full(HEADS, KV_RANK, LANES),
                   full(HEADS, KV_RANK, LANES), full(1, Q_RANK), full(1, KV_RANK)],
        out_shape=[jax.ShapeDtypeStruct((seq, DH_PART), MXU_DTYPE), jax.ShapeDtypeStruct((HEADS, Q_RANK, LANES), F32),
                   jax.ShapeDtypeStruct((HEADS, KV_RANK, LANES), F32), jax.ShapeDtypeStruct((HEADS, KV_RANK, LANES), F32),
                   jax.ShapeDtypeStruct((1, Q_RANK), F32), jax.ShapeDtypeStruct((1, KV_RANK), F32)],
        compiler_params=_params("arbitrary"),
    )(h, g_cq, g_ckv, wq, wk, wv, ctab, stab, dq, dk, dv, *after)


def _mla_attn_fwd(q, k, v, t=1024):
    _, seq, _ = q.shape
    t = min(t, seq)

    def body(q_ref, k_ref, v_ref, o_ref, ob_ref, lse_ref, m_ref, acc_ref, s_ref):
        i = pl.program_id(1)
        qb = q_ref[...]
        m_ref[...] = jnp.full_like(m_ref, NEG_BIG)
        acc_ref[...] = jnp.zeros_like(acc_ref)

        def scores(j):
            return _dot(qb, k_ref[pl.ds(pl.multiple_of(j * t, t), t), :], 1, 1) * MLA_SCALE_LOG2

        def softmax_pv(j, s, rows=slice(None), mask=None):
            vb = v_ref[pl.ds(pl.multiple_of(j * t, t), s.shape[1]), :]
            if mask is not None:
                s = jnp.where(mask, s, NEG_BIG)
            m_old = m_ref[rows, :]
            m_new = jnp.maximum(m_old, jnp.max(s, axis=1, keepdims=True))
            p = jnp.exp2(s - m_new)
            a = jnp.exp2(m_old - m_new)
            acc_ref[rows, :] = a * acc_ref[rows, :] + _dot(p.astype(MXU_DTYPE), vb, 1, 0)
            m_ref[rows, :] = m_new

        def softmax_pv_diagonal(j):
            th = t // MLA_FWD_SPLITS_DIAGONAL
            for hf in range(MLA_FWD_SPLITS_DIAGONAL):
                nk = (hf + 1) * th
                row = lax.broadcasted_iota(jnp.int32, (th, nk), 0) + hf * th
                rows = slice(hf * th, (hf + 1) * th)
                softmax_pv(j, s_ref[rows, 0:nk], rows, row >= lax.broadcasted_iota(jnp.int32, (th, nk), 1))

        s_ref[...] = scores(0)

        def loop_body(j, c):
            s_next = scores(j + 1)
            softmax_pv(j, s_ref[...])
            s_ref[...] = s_next
            return c

        lax.fori_loop(0, i, loop_body, 0)
        softmax_pv_diagonal(i)
        acc = acc_ref[...]
        l = acc[:, ONES_LANE:ONES_LANE + 1]
        o = jnp.where(lax.broadcasted_iota(jnp.int32, acc.shape, 1) < HEAD_DIM, acc * (1.0 / l), 0.0)
        o_ref[...] = o
        ob_ref[...] = o.astype(ob_ref.dtype)
        lse_ref[...] = jnp.broadcast_to(m_ref[...] + jnp.log2(l), lse_ref.shape)

    blk = pl.BlockSpec((None, t, LANES), lambda h, i: (h, i, 0))
    whole = pl.BlockSpec((None, seq, LANES), lambda h, i: (h, 0, 0))
    shp = jax.ShapeDtypeStruct((HEADS, seq, LANES), F32)
    return pl.pallas_call(
        body, name="mla_attn_fwd", grid=(HEADS, seq // t),
        in_specs=[blk, whole, whole], out_specs=[blk, blk, blk],
        out_shape=[shp, jax.ShapeDtypeStruct((HEADS, seq, LANES), MXU_DTYPE), shp],
        scratch_shapes=[pltpu.VMEM((t, 1), F32), pltpu.VMEM((t, LANES), F32), pltpu.VMEM((t, t), F32)],
        compiler_params=_params("parallel", "arbitrary"),
    )(q, k, v)


def _mla_attn_bwd(q, k, v, o, lse, do, t=1024):
    _, seq, _ = q.shape
    t = min(t, seq)
    nb = seq // t

    def body(q_ref, k_ref, v_ref, o_ref, lse_ref, do_ref, dq_ref, dk_ref, dv_ref, dl_ref, dka_ref, dva_ref):
        dq_ref[...] = jnp.zeros_like(dq_ref)

        def delta_body(i, c):
            rows = pl.ds(pl.multiple_of(i * t, t), t)
            dl_ref[rows, :] = jnp.sum(do_ref[rows, :] * o_ref[rows, :], axis=1, keepdims=True)
            return c

        lax.fori_loop(0, nb, delta_body, 0)

        def kblock(j, c):
            krows = pl.ds(pl.multiple_of(j * t, t), t)
            kb = k_ref[krows, :]
            vb = v_ref[krows, :]
            dka_ref[...] = jnp.zeros_like(dka_ref)
            dva_ref[...] = jnp.zeros_like(dva_ref)

            def qstep(i, masked):
                ns = MLA_BWD_SPLITS_DIAGONAL if masked else MLA_BWD_SPLITS
                th = t // ns
                rows = [pl.ds(pl.multiple_of(i * t + hf * th, th), th) for hf in range(ns)]
                qs = [q_ref[r, :] for r in rows]
                dos = [do_ref[r, :].astype(MXU_DTYPE) for r in rows]
                nkeys = [(hf + 1) * th if masked else t for hf in range(ns)]
                ss = [_dot(qs[hf], kb[:nkeys[hf]], 1, 1) * MLA_SCALE_LOG2 for hf in range(ns)]
                dps = [_dot(dos[hf], vb[:nkeys[hf]], 1, 1) for hf in range(ns)]
                for hf in range(ns):
                    s, nk = ss[hf], nkeys[hf]
                    if masked:
                        row = lax.broadcasted_iota(jnp.int32, (th, nk), 0) + hf * th
                        s = jnp.where(row >= lax.broadcasted_iota(jnp.int32, (th, nk), 1), s, NEG_BIG)
                    p = jnp.exp2(s - lse_ref[rows[hf], 0:1])
                    dva_ref[0:nk, :] += _dot(p.astype(MXU_DTYPE), dos[hf], 0, 0)
                    ds = (p * (dps[hf] - dl_ref[rows[hf], :]) * MLA_SCALE).astype(MXU_DTYPE)
                    dka_ref[0:nk, :] += _dot(ds, qs[hf], 0, 0)
                    dq_ref[rows[hf], :] += _dot(ds, kb[:nk], 1, 0)

            qstep(j, True)

            def qloop(i, c2):
                qstep(i, False)
                return c2

            lax.fori_loop(j + 1, nb, qloop, 0)
            dk_ref[krows, :] = dka_ref[...]
            dv_ref[krows, :] = dva_ref[...]
            return c

        lax.fori_loop(0, nb, kblock, 0)

    whole = pl.BlockSpec((None, seq, LANES), lambda h: (h, 0, 0))
    shp = jax.ShapeDtypeStruct((HEADS, seq, LANES), F32)
    return pl.pallas_call(
        body, name="mla_attn_bwd", grid=(HEADS,),
        in_specs=[whole] * 6, out_specs=[whole] * 3, out_shape=[shp] * 3,
        scratch_shapes=[pltpu.VMEM((seq, 1), F32), pltpu.VMEM((t, LANES), F32), pltpu.VMEM((t, LANES), F32)],
        compiler_params=_params("parallel"),
    )(q, k, v, o, lse, do)


DIL_CHUNK = DIL_BLOCK * max(d for _, d in DIL_PAIRS)
DIL_PAIR_LANES = 2 * HEAD_DIM
assert DIL_PAIR_LANES == LANES
DIL_UNROLL_FWD = 16
DIL_UNROLL_BWD = 16


def _dil_bias_tables(hp, dil):
    b = DIL_BLOCK
    iq = lax.broadcasted_iota(jnp.int32, (b, 2 * b), 0)
    ik = lax.broadcasted_iota(jnp.int32, (b, 2 * b), 1)
    off = iq + b - ik
    band = (off >= 0) & (off <= b)
    dist = (off * dil).astype(F32)
    every, first = [], []
    for hh in range(2):
        slope = jnp.where(hp == 0, ALIBI_SLOPES[hh], jnp.where(hp == 1, ALIBI_SLOPES[2 + hh],
                          jnp.where(hp == 2, ALIBI_SLOPES[4 + hh], ALIBI_SLOPES[6 + hh]))).astype(F32)
        bias = -slope * dist
        every.append(jnp.where(band, bias, NEG_BIG))
        first.append(jnp.where(band & (ik >= b), bias, NEG_BIG))
    return jnp.concatenate(every, axis=0), jnp.concatenate(first, axis=0)


def _dil_rows(start, dil):
    return pl.ds(start, DIL_BLOCK) if dil == 1 else pl.ds(start, DIL_BLOCK, stride=dil)


def _dil_block_pos(blk, c, dil):
    sc, r = blk // dil, blk % dil
    q0 = sc * (DIL_BLOCK * dil) + r
    kcur0 = c * DIL_CHUNK + q0
    first = kcur0 < DIL_BLOCK * dil
    kprev0 = jnp.where(first, kcur0, kcur0 - DIL_BLOCK * dil)
    return q0, kcur0, kprev0, first


def _pair_cols(hh):
    return slice(HEAD_DIM * hh, HEAD_DIM * (hh + 1))


def _first_head_lanes(shape):
    return lax.broadcasted_iota(jnp.int32, shape, 1) < HEAD_DIM


def _stack_pair(t):
    first = _first_head_lanes(t.shape)
    return jnp.concatenate([jnp.where(first, t, 0.0), jnp.where(first, 0.0, t)], axis=0).astype(MXU_DTYPE)


def _unstack_pair(t):
    rows = t.shape[0] // 2
    return jnp.where(_first_head_lanes((rows, t.shape[1])), t[:rows], t[rows:])


def _pair_column(t):
    return jnp.concatenate([t[:, 0:1], t[:, HEAD_DIM:HEAD_DIM + 1]], axis=0)


def _dil_fwd(h):
    seq = h.shape[0]
    assert seq % DIL_CHUNK == 0
    nblk = DIL_CHUNK // DIL_BLOCK
    rc = 256

    def body(q_ref, k_ref, v_ref, o_ref, ob_ref, lse_ref, *scr):
        o_scr, l_scr = scr[:3], scr[3:]
        hp, c = pl.program_id(0), pl.program_id(1)
        for bi, (_, dil) in enumerate(DIL_PAIRS):
            tables = _dil_bias_tables(hp, dil)

            def block(blk, carry, bi=bi, dil=dil, tables=tables):
                q0, kcur0, kprev0, first = _dil_block_pos(blk, c, dil)
                q2 = _stack_pair(q_ref[_dil_rows(q0, dil), :] * DIL_SCALE)
                kcat = jnp.concatenate([k_ref[_dil_rows(kprev0, dil), :], k_ref[_dil_rows(kcur0, dil), :]], axis=0).astype(MXU_DTYPE)
                vcat = jnp.concatenate([v_ref[_dil_rows(kprev0, dil), :], v_ref[_dil_rows(kcur0, dil), :]], axis=0).astype(MXU_DTYPE)
                s = _dot(q2, kcat, 1, 1) + jnp.where(first, tables[1], tables[0])
                mx = jnp.max(s, axis=1, keepdims=True)
                p = jnp.exp(s - mx)
                l = jnp.sum(p, axis=1, keepdims=True)
                o_scr[bi][_dil_rows(q0, dil), :] = _unstack_pair(_dot(p.astype(MXU_DTYPE), vcat, 1, 0) * (1.0 / l))
                l_scr[bi][_dil_rows(q0, dil), :] = _unstack_pair(jnp.broadcast_to(mx + jnp.log(l), (2 * DIL_BLOCK, LANES)))
                return carry

            lax.fori_loop(0, nblk, block, 0, unroll=DIL_UNROLL_FWD)

        def combine(i, carry):
            rows = pl.ds(pl.multiple_of(i * rc, rc), rc)
            ls = [l_scr[bi][rows, :] for bi in range(3)]
            mx = jnp.maximum(jnp.maximum(ls[0], ls[1]), ls[2])
            es = [jnp.exp(l - mx) for l in ls]
            den = es[0] + es[1] + es[2]
            o = (es[0] * o_scr[0][rows, :] + es[1] * o_scr[1][rows, :] + es[2] * o_scr[2][rows, :]) / den
            o_ref[rows, :] = o
            ob_ref[rows, :] = o.astype(ob_ref.dtype)
            lse_ref[rows, :] = mx + jnp.log(den)
            return carry

        lax.fori_loop(0, DIL_CHUNK // rc, combine, 0)

    nq = DIL_WIDTH // LANES
    chunk = lambda off: pl.BlockSpec((DIL_CHUNK, LANES), lambda hp, c: (c, off + hp))
    whole = lambda off: pl.BlockSpec((seq, LANES), lambda hp, c: (0, off + hp))
    shp = jax.ShapeDtypeStruct((seq, DIL_WIDTH), F32)
    return pl.pallas_call(
        body, name="dil_fwd", grid=(nq, seq // DIL_CHUNK),
        in_specs=[chunk(nq), whole(2 * nq), whole(3 * nq)], out_specs=[chunk(0), chunk(0), chunk(0)],
        out_shape=[shp, jax.ShapeDtypeStruct((seq, DIL_WIDTH), MXU_DTYPE), shp],
        scratch_shapes=[pltpu.VMEM((DIL_CHUNK, LANES), F32)] * 6,
        compiler_params=_params("parallel", "arbitrary"),
    )(h, h, h)


def _dil_bwd(h, o, lse, do, after=()):
    seq = h.shape[0]
    nblk = DIL_CHUNK // DIL_BLOCK
    nchunk = seq // DIL_CHUNK
    rc = 256

    n_after = len(after)

    def body(q_ref, k_ref, v_ref, o_ref, lse_ref, do_ref, *rest):
        dq_out, dk_out, dv_out, dl_scr, dq_ref, dk_ref, dv_ref = rest[n_after:]
        hp, c = pl.program_id(0), pl.program_id(1)

        @pl.when(c == 0)
        def _():
            dk_ref[...] = jnp.zeros_like(dk_ref)
            dv_ref[...] = jnp.zeros_like(dv_ref)

        def delta(i, carry):
            rows = pl.ds(pl.multiple_of(i * rc, rc), rc)
            prod = do_ref[rows, :] * o_ref[rows, :]
            dl_scr[rows, :] = jnp.concatenate(
                [jnp.broadcast_to(jnp.sum(prod[:, _pair_cols(hh)], axis=1, keepdims=True), (rc, HEAD_DIM)) for hh in range(2)], axis=1)
            return carry

        lax.fori_loop(0, DIL_CHUNK // rc, delta, 0)

        for bi, (_, dil) in enumerate(DIL_PAIRS):
            tables = _dil_bias_tables(hp, dil)

            def block(blk, carry, bi=bi, dil=dil, tables=tables):
                q0, kcur0, kprev0, first = _dil_block_pos(blk, c, dil)
                qrows = _dil_rows(q0, dil)
                q2 = _stack_pair(q_ref[qrows, :] * DIL_SCALE)
                kcat = jnp.concatenate([k_ref[_dil_rows(kprev0, dil), :], k_ref[_dil_rows(kcur0, dil), :]], axis=0).astype(MXU_DTYPE)
                vcat = jnp.concatenate([v_ref[_dil_rows(kprev0, dil), :], v_ref[_dil_rows(kcur0, dil), :]], axis=0).astype(MXU_DTYPE)
                do2 = _stack_pair(do_ref[qrows, :])
                s = _dot(q2, kcat, 1, 1) + jnp.where(first, tables[1], tables[0])
                p = jnp.exp(s - _pair_column(lse_ref[qrows, :]))
                dp = _dot(do2, vcat, 1, 1)
                ds = (p * (dp - _pair_column(dl_scr[qrows, :]))).astype(MXU_DTYPE)
                dq_b = _unstack_pair(_dot(ds, kcat, 1, 0)) * DIL_SCALE
                dk_b = _dot(ds, q2, 0, 0)
                dv_b = _dot(p.astype(MXU_DTYPE), do2, 0, 0)
                if bi == 0:
                    dq_ref[qrows, :] = dq_b
                else:
                    dq_ref[qrows, :] += dq_b
                dk_ref[_dil_rows(kprev0, dil), :] += dk_b[:DIL_BLOCK]
                dv_ref[_dil_rows(kprev0, dil), :] += dv_b[:DIL_BLOCK]
                dk_ref[_dil_rows(kcur0, dil), :] += dk_b[DIL_BLOCK:]
                dv_ref[_dil_rows(kcur0, dil), :] += dv_b[DIL_BLOCK:]
                return carry

            lax.fori_loop(0, nblk, block, 0, unroll=DIL_UNROLL_BWD)

        dq_out[...] = dq_ref[...].astype(dq_out.dtype)

        @pl.when(c == nchunk - 1)
        def _():
            dk_out[...] = dk_ref[...].astype(dk_out.dtype)
            dv_out[...] = dv_ref[...].astype(dv_out.dtype)

    nq = DIL_WIDTH // LANES
    chunk = lambda off: pl.BlockSpec((DIL_CHUNK, LANES), lambda hp, c: (c, off + hp))
    whole = lambda off: pl.BlockSpec((seq, LANES), lambda hp, c: (0, off + hp))
    shp = jax.ShapeDtypeStruct((seq, DIL_WIDTH), MXU_DTYPE)
    return pl.pallas_call(
        body, name="dil_bwd", grid=(nq, nchunk),
        in_specs=[chunk(nq), whole(2 * nq), whole(3 * nq), chunk(0), chunk(0), chunk(0)] + [_ANY_SPEC] * n_after,
        out_specs=[chunk(0), whole(0), whole(0)], out_shape=[shp, shp, shp],
        scratch_shapes=[pltpu.VMEM((DIL_CHUNK, LANES), F32), pltpu.VMEM((DIL_CHUNK, LANES), F32),
                        pltpu.VMEM((seq, LANES), F32), pltpu.VMEM((seq, LANES), F32)],
        compiler_params=_params("parallel", "arbitrary"),
    )(h, h, h, o, lse, do, *after)


def _mm_dx0(parts, w_in_t, res, tm=1024, after=()):
    seq, d = res.shape
    tm = min(tm, seq)
    n_after = len(after)

    def body(a0, a1, a2, a3, b_ref, r_ref, *rest):
        o_ref = rest[n_after]
        acc = _dot(a0[...], b_ref[0:DH_PART, :], 1, 0)
        for c, a in enumerate((a1, a2, a3), start=1):
            acc += _dot(a[...], b_ref[DH_PART * c:DH_PART * (c + 1), :], 1, 0)
        o_ref[...] = acc + DN_ALPHA * r_ref[...]

    blk = pl.BlockSpec((tm, DH_PART), lambda i: (i, 0))
    row = pl.BlockSpec((tm, d), lambda i: (i, 0))
    return pl.pallas_call(
        body, name="mm_dx0", grid=(seq // tm,),
        in_specs=[blk] * 4 + [pl.BlockSpec((IN_PAD, d), lambda i: (0, 0), pipeline_mode=pl.Buffered(1)), row] + [_ANY_SPEC] * n_after,
        out_specs=row, out_shape=jax.ShapeDtypeStruct((seq, d), F32), compiler_params=_params("parallel"),
    )(*parts, w_in_t, res, *after)


def _mm_dw_in(parts, x0, tk=1024):
    seq, d = x0.shape
    tk = min(tk, seq)
    nk = seq // tk

    def body(a0, a1, a2, a3, b_ref, o_ref, acc_ref):
        kk = pl.program_id(0)

        @pl.when(kk == 0)
        def _():
            acc_ref[...] = jnp.zeros_like(acc_ref)

        b = b_ref[...].astype(MXU_DTYPE)
        for c, a in enumerate((a0, a1, a2, a3)):
            acc_ref[DH_PART * c:DH_PART * (c + 1), :] += _dot(a[...], b, 0, 0)

        @pl.when(kk == nk - 1)
        def _():
            o_ref[...] = acc_ref[...].astype(o_ref.dtype)

    blk = pl.BlockSpec((tk, DH_PART), lambda kk: (kk, 0))
    return pl.pallas_call(
        body, name="mm_dw_in", grid=(nk,), in_specs=[blk] * 4 + [pl.BlockSpec((tk, d), lambda kk: (kk, 0))],
        out_specs=pl.BlockSpec((IN_PAD, d), lambda kk: (0, 0)), out_shape=jax.ShapeDtypeStruct((IN_PAD, d), MXU_DTYPE),
        scratch_shapes=[pltpu.VMEM((IN_PAD, d), F32)], compiler_params=_params("arbitrary"),
    )(*parts, x0)


def _ln_stats(z):
    mu = jnp.mean(z, axis=-1, keepdims=True)
    zc = z - mu
    r = lax.rsqrt(jnp.mean(zc * zc, axis=-1, keepdims=True) + LN_EPS)
    return zc * r, r


def _ln_bwd_math(dy, xh, r, g):
    dxh = dy * g
    return r * (dxh - jnp.mean(dxh, axis=-1, keepdims=True) - xh * jnp.mean(dxh * xh, axis=-1, keepdims=True))


def _mix_ln1(o_mla, o_dil, w_o_mla, w_o_dil, x0, g, b, tm=512):
    seq, d = x0.shape
    tm = min(tm, seq)

    def body(om_ref, od_ref, wm_ref, wd_ref, x_ref, g_ref, b_ref, z_ref, y_ref, yb_ref):
        mix = _dot(od_ref[...], wd_ref[...], 1, 0)
        for hd in range(HEADS):
            mix += _dot(om_ref[hd], wm_ref[LANES * hd:LANES * (hd + 1), :], 1, 0)
        z = DN_ALPHA * x_ref[...] + mix
        xh, _ = _ln_stats(z)
        y = xh * g_ref[...] + b_ref[...]
        z_ref[...] = z
        y_ref[...] = y
        yb_ref[...] = y.astype(yb_ref.dtype)

    blk = pl.BlockSpec((tm, d), lambda i: (i, 0))
    vec = pl.BlockSpec((1, d), lambda i: (0, 0))
    shp = jax.ShapeDtypeStruct((seq, d), F32)
    return pl.pallas_call(
        body, name="mix_ln1", grid=(seq // tm,),
        in_specs=[pl.BlockSpec((HEADS, tm, LANES), lambda i: (0, i, 0)), pl.BlockSpec((tm, DIL_WIDTH), lambda i: (i, 0)),
                  pl.BlockSpec((HEADS * LANES, d), lambda i: (0, 0)), pl.BlockSpec((DIL_WIDTH, d), lambda i: (0, 0)), blk, vec, vec],
        out_specs=[blk, blk, blk], out_shape=[shp, shp, jax.ShapeDtypeStruct((seq, d), MXU_DTYPE)],
        compiler_params=_params("parallel"))(o_mla, o_dil, w_o_mla, w_o_dil, x0, g, b)


def _dx1_ln1_bwd(du, w_up_t, dz2, z, g, tm=256, after=()):
    seq, d = z.shape
    kdim = du.shape[1]
    tm = min(tm, seq)
    n_after = len(after)

    def body(du_ref, w_ref, r_ref, z_ref, g_ref, *rest):
        dz_ref, dzb_ref, dg_ref, db_ref = rest[n_after:]

        @pl.when(pl.program_id(0) == 0)
        def _():
            dg_ref[...] = jnp.zeros_like(dg_ref)
            db_ref[...] = jnp.zeros_like(db_ref)

        dyb = _dot(du_ref[...], w_ref[...], 1, 0) + DN_ALPHA * r_ref[...]
        xh, r = _ln_stats(z_ref[...])
        dg_ref[...] += jnp.sum(dyb * xh, axis=0, keepdims=True)
        db_ref[...] += jnp.sum(dyb, axis=0, keepdims=True)
        dz = _ln_bwd_math(dyb, xh, r, g_ref[...])
        dz_ref[...] = dz
        dzb_ref[...] = dz.astype(dzb_ref.dtype)

    blk = pl.BlockSpec((tm, d), lambda i: (i, 0))
    vec = pl.BlockSpec((1, d), lambda i: (0, 0))
    return pl.pallas_call(
        body, name="dx1_ln1_bwd", grid=(seq // tm,),
        in_specs=[pl.BlockSpec((tm, kdim), lambda i: (i, 0)),
                  pl.BlockSpec((kdim, d), lambda i: (0, 0), pipeline_mode=pl.Buffered(1)), blk, blk, vec] + [_ANY_SPEC] * n_after,
        out_specs=[blk, blk, vec, vec],
        out_shape=[jax.ShapeDtypeStruct((seq, d), F32), jax.ShapeDtypeStruct((seq, d), MXU_DTYPE),
                   jax.ShapeDtypeStruct((1, d), F32), jax.ShapeDtypeStruct((1, d), F32)],
        compiler_params=_params("arbitrary"))(du, w_up_t, dz2, z, g, *after)


def _down_ln2_loss_bwd(act, w_down, x1, target, g, b, tm=512):
    seq, d = x1.shape
    kdim = act.shape[1]
    tm = min(tm, seq)

    def body(a_ref, w_ref, x_ref, t_ref, g_ref, b_ref, dz_ref, dzb_ref, loss_ref, dg_ref, db_ref):
        @pl.when(pl.program_id(0) == 0)
        def _():
            loss_ref[...] = jnp.zeros_like(loss_ref)
            dg_ref[...] = jnp.zeros_like(dg_ref)
            db_ref[...] = jnp.zeros_like(db_ref)

        gv = g_ref[...]
        z = DN_ALPHA * x_ref[...] + _dot(a_ref[...], w_ref[...], 1, 0)
        xh, r = _ln_stats(z)
        err = (xh * gv + b_ref[...]) - t_ref[...]
        loss_ref[...] += 0.5 * jnp.sum(jnp.mean(err * err, axis=-1, keepdims=True), axis=0, keepdims=True)
        dy = err * (1.0 / d)
        dg_ref[...] += jnp.sum(dy * xh, axis=0, keepdims=True)
        db_ref[...] += jnp.sum(dy, axis=0, keepdims=True)
        dz = _ln_bwd_math(dy, xh, r, gv)
        dz_ref[...] = dz
        dzb_ref[...] = dz.astype(dzb_ref.dtype)

    blk = pl.BlockSpec((tm, d), lambda i: (i, 0))
    vec = pl.BlockSpec((1, d), lambda i: (0, 0))
    return pl.pallas_call(
        body, name="down_ln2_loss_bwd", grid=(seq // tm,),
        in_specs=[pl.BlockSpec((tm, kdim), lambda i: (i, 0)),
                  pl.BlockSpec((kdim, d), lambda i: (0, 0), pipeline_mode=pl.Buffered(1)), blk, blk, vec, vec],
        out_specs=[blk, blk, pl.BlockSpec((1, LANES), lambda i: (0, 0)), vec, vec],
        out_shape=[jax.ShapeDtypeStruct((seq, d), F32), jax.ShapeDtypeStruct((seq, d), MXU_DTYPE),
                   jax.ShapeDtypeStruct((1, LANES), F32),
                   jax.ShapeDtypeStruct((1, d), F32), jax.ShapeDtypeStruct((1, d), F32)],
        compiler_params=_params("arbitrary"))(act, w_down, x1, target, g, b)


HALO = 16


def _conv_rows(e, w_ref, b_ref):
    y = b_ref[...] + w_ref[0:1, :] * pltpu.roll(e, 2, 0)
    y = y + w_ref[1:2, :] * pltpu.roll(e, 1, 0)
    return y + w_ref[2:3, :] * e


_GELU_C = math.sqrt(2.0 / math.pi)
_GELU_A = 0.044715


def _gelu(x):
    return x * (0.5 + 0.5 * jnp.tanh(x * (_GELU_C + (_GELU_C * _GELU_A) * (x * x))))


CONV_TN = 256


def _ffn_interleave(a, axis):
    shp = a.shape
    a = a.reshape(shp[:axis] + (2, D_FF // CONV_TN, CONV_TN) + shp[axis + 1:])
    return jnp.swapaxes(a, axis, axis + 1).reshape(shp)


def _ffn_deinterleave(a, axis):
    shp = a.shape
    a = a.reshape(shp[:axis] + (D_FF // CONV_TN, 2, CONV_TN) + shp[axis + 1:])
    return jnp.swapaxes(a, axis, axis + 1).reshape(shp)


def _up_conv_gate_fwd(x1, w_up_t, conv_w, conv_b, tm=1024):
    seq, d = x1.shape
    tm = min(tm, seq)
    tn = CONV_TN

    def body(x_ref, xp_ref, wu_ref, w_ref, b_ref, u_ref, y_ref, o_ref):
        first = pl.program_id(0) == 0
        halo = jnp.where(first, jnp.zeros_like(xp_ref), xp_ref[...])
        e = _dot(jnp.concatenate([halo, x_ref[...]], axis=0), wu_ref[...], 1, 1)
        u_ref[...] = e[HALO:]
        y = _conv_rows(e, w_ref, b_ref)[HALO:]
        y_ref[...] = y
        o_ref[...] = (_gelu(y[:, tn:]) * y[:, :tn]).astype(o_ref.dtype)

    hb = tm // HALO
    return pl.pallas_call(
        body, name="up_conv_gate_fwd", grid=(seq // tm, D_FF // tn),
        in_specs=[pl.BlockSpec((tm, d), lambda i, j: (i, 0)),
                  pl.BlockSpec((HALO, d), lambda i, j: (jnp.maximum(i * hb - 1, 0), 0)),
                  pl.BlockSpec((2 * tn, d), lambda i, j: (j, 0)),
                  pl.BlockSpec((3, 2 * tn), lambda i, j: (0, j)), pl.BlockSpec((1, 2 * tn), lambda i, j: (0, j))],
        out_specs=[pl.BlockSpec((tm, 2 * tn), lambda i, j: (i, j)), pl.BlockSpec((tm, 2 * tn), lambda i, j: (i, j)),
                   pl.BlockSpec((tm, tn), lambda i, j: (i, j))],
        out_shape=[jax.ShapeDtypeStruct((seq, 2 * D_FF), F32), jax.ShapeDtypeStruct((seq, 2 * D_FF), F32),
                   jax.ShapeDtypeStruct((seq, D_FF), MXU_DTYPE)],
        compiler_params=_params("parallel", "arbitrary"),
    )(x1, x1, w_up_t, conv_w, conv_b)


def _conv_gate_bwd(u, y, d_act, conv_w, tm=1024):
    seq = u.shape[0]
    tm = min(tm, seq)
    tn = CONV_TN
    ni = seq // tm
    rows = tm + HALO

    def body(u_ref, y_ref, yn_ref, da_ref, dan_ref, w_ref, du_ref, dw_ref, db_ref):
        i = pl.program_id(1)
        last = i == ni - 1

        @pl.when(i == 0)
        def _():
            dw_ref[...] = jnp.zeros_like(dw_ref)
            db_ref[...] = jnp.zeros_like(db_ref)

        yy = jnp.concatenate([y_ref[...], yn_ref[...]], axis=0)
        ya, yg = yy[:, :tn], yy[:, tn:]
        dact = jnp.concatenate([da_ref[...].astype(F32), jnp.where(last, 0.0, dan_ref[...].astype(F32))], axis=0)
        yg2 = yg * yg
        th = jnp.tanh(yg * (_GELU_C + (_GELU_C * _GELU_A) * yg2))
        half = 0.5 + 0.5 * th
        gelu = yg * half
        gelu_grad = half + gelu * (1.0 - half) * (2.0 * _GELU_C + (6.0 * _GELU_C * _GELU_A) * yg2)
        dy = jnp.concatenate([dact * gelu, dact * ya * gelu_grad], axis=1)
        dy0, dy1, dy2 = dy[:tm], pltpu.roll(dy, rows - 1, 0)[:tm], pltpu.roll(dy, rows - 2, 0)[:tm]
        du_ref[...] = (w_ref[2:3, :] * dy0 + w_ref[1:2, :] * dy1 + w_ref[0:1, :] * dy2).astype(du_ref.dtype)
        ut = u_ref[...]
        dw_ref[0:1, :] += jnp.sum(dy2 * ut, axis=0, keepdims=True)
        dw_ref[1:2, :] += jnp.sum(dy1 * ut, axis=0, keepdims=True)
        dw_ref[2:3, :] += jnp.sum(dy0 * ut, axis=0, keepdims=True)
        db_ref[...] += jnp.sum(dy0, axis=0, keepdims=True)

    hb = tm // HALO
    nh = seq // HALO
    nxt = lambda j, i: (jnp.minimum((i + 1) * hb, nh - 1), j)
    tile = pl.BlockSpec((tm, 2 * tn), lambda j, i: (i, j))
    return pl.pallas_call(
        body, name="conv_gate_bwd", grid=(D_FF // tn, ni),
        in_specs=[tile, tile, pl.BlockSpec((HALO, 2 * tn), nxt),
                  pl.BlockSpec((tm, tn), lambda j, i: (i, j)), pl.BlockSpec((HALO, tn), nxt),
                  pl.BlockSpec((3, 2 * tn), lambda j, i: (0, j))],
        out_specs=[tile, pl.BlockSpec((3, 2 * tn), lambda j, i: (0, j)), pl.BlockSpec((1, 2 * tn), lambda j, i: (0, j))],
        out_shape=[jax.ShapeDtypeStruct((seq, 2 * D_FF), MXU_DTYPE), jax.ShapeDtypeStruct((3, 2 * D_FF), F32),
                   jax.ShapeDtypeStruct((1, 2 * D_FF), F32)],
        compiler_params=_params("parallel", "arbitrary"),
    )(u, y, y, d_act, d_act, conv_w)


def _pad_heads(w, width):
    w = jnp.transpose(w, (1, 0, 2))
    return jnp.pad(w, ((0, 0), (0, 0), (0, LANES - width))).astype(MXU_DTYPE)


def _heads_major(a):
    return jnp.transpose(a, (1, 0, 2)).reshape(-1, a.shape[2])


def _heads_minor(a, heads):
    return jnp.transpose(a.reshape(heads, -1, a.shape[1]), (1, 0, 2))


_LATENT = Q_RANK + KV_RANK
_ROPE_AT = _LATENT + NOPE_DIM
_ROPE_END = _ROPE_AT + ROPE_DIM


def _split_pad_rows(w_t):
    z = lambda n: jnp.zeros((n, w_t.shape[1]), w_t.dtype)
    return jnp.concatenate([w_t[:_LATENT], z(_ROPE_AT - _LATENT), w_t[_LATENT:_LATENT + ROPE_DIM], z(DH_PART - _ROPE_END),
                            w_t[_LATENT + ROPE_DIM:]], axis=0)


def _split_unpad_rows(w_p):
    return jnp.concatenate([w_p[:_LATENT], w_p[_ROPE_AT:_ROPE_END], w_p[DH_PART:]], axis=0)


def _pad_w_o(w_o):
    mla = jnp.pad(w_o[:MLA_WIDTH].reshape(HEADS, HEAD_DIM, D_MODEL), ((0, 0), (0, LANES - HEAD_DIM), (0, 0)))
    return mla.reshape(HEADS * LANES, D_MODEL).astype(MXU_DTYPE), w_o[MLA_WIDTH:].astype(MXU_DTYPE)


def _unpad_w_o(d_mla, d_dil):
    return jnp.concatenate([d_mla.reshape(HEADS, LANES, D_MODEL)[:, :HEAD_DIM].reshape(MLA_WIDTH, D_MODEL), d_dil], axis=0)


def _row(v):
    return v.reshape(1, -1).astype(F32)


def _layer_grads(x0, target, cw, first_after=(), late_weights=None, on_grads=None):
    seq = x0.shape[0]
    ctab, stab = _rope_tables(seq)
    gq, gk = cw["g_cq"], cw["g_ckv"]
    wq, wk, wv = cw["wq"], cw["wk"], cw["wv"]
    notify = (lambda stage, grads: ()) if on_grads is None else on_grads

    h = _mm(x0, cw["w_in_t"], name="mm_h", tb=True, tm=1024, tn=IN_PAD, tk=1024, after=first_after)
    qf, kf, vp = _mla_prep(h, gq, gk, wq, wk, wv, ctab, stab)
    o_mla, o_mla_b, lse_mla = _mla_attn_fwd(qf, kf, vp)
    o_dil, o_dil_b, lse_dil = _dil_fwd(h)
    fetch = (lambda stage, after: {}) if late_weights is None else late_weights
    cw = {**cw, **fetch("w_o", o_mla_b), **fetch("w_up", o_mla_b)}
    cb = cw["conv_b"]
    z1, x1, x1b = _mix_ln1(o_mla_b, o_dil_b, cw["w_o_mla"], cw["w_o_dil"], x0, cw["ln1_g"], cw["ln1_b"])
    u, y, act = _up_conv_gate_fwd(x1b, cw["w_up_t"], cw["conv_w"], cb)
    cw = {**cw, **fetch("w_down", act)}
    dz2, dz2b, loss, d_ln2_g, d_ln2_b = _down_ln2_loss_bwd(act, cw["w_down"], x1, target, cw["ln2_g"], cw["ln2_b"])

    d_act = _mm(dz2b, cw["w_down"], name="mm_d_act", tb=True, out_dtype=MXU_DTYPE, tm=1024, tn=D_FF, tk=1024)
    d_w_down = _mm(act, dz2b, name="mm_dw_down", ta=True, out_dtype=MXU_DTYPE, tm=1408, tn=1024, tk=1024)
    du, d_conv_w, d_conv_b = _conv_gate_bwd(u, y, d_act, cw["conv_w"])
    d_w_up_t = _mm(du, x1b, name="mm_dw_up", ta=True, out_dtype=MXU_DTYPE, tm=1408, tn=1024, tk=2048)
    grads = dict(w_up_t=d_w_up_t, w_down=d_w_down, conv_w=d_conv_w, conv_b=d_conv_b, ln2_g=d_ln2_g, ln2_b=d_ln2_b)
    dz1, dz1b, d_ln1_g, d_ln1_b = _dx1_ln1_bwd(du, cw["w_up_t"], dz2, z1, cw["ln1_g"], after=notify("ffn", grads))
    do_mla, do_dil, d_w_o_mla, d_w_o_dil = _w_o_bwd(dz1b, o_mla_b, o_dil_b, cw["w_o_mla"], cw["w_o_dil"])
    grads.update(w_o_mla=d_w_o_mla, w_o_dil=d_w_o_dil, ln1_g=d_ln1_g, ln1_b=d_ln1_b)
    dqf, dkf, dvf = _mla_attn_bwd(qf, kf, vp, o_mla, lse_mla, do_mla)
    dh_mla, d_wq, d_wk, d_wv, d_gq, d_gk = _mla_prep_bwd(h, gq, gk, wq, wk, wv, ctab, stab, dqf, dkf, dvf,
                                                          after=notify("w_o", grads))
    grads.update(wq=d_wq, wk=d_wk, wv=d_wv, g_cq=d_gq, g_ckv=d_gk, loss=loss)
    dq_dil, dk_dil, dv_dil = _dil_bwd(h, o_dil, lse_dil, do_dil, after=notify("mla", grads))
    dh = (dh_mla, dq_dil, dk_dil, dv_dil)
    grads.update(w_in_t=_mm_dw_in(dh, x0))
    grad_x = _mm_dx0(dh, cw["w_in_t"], dz1, after=notify("w_in", grads))
    return loss, grad_x, grads


def _all_gather(blocks, name):
    na = len(blocks)

    def body(*refs):
        ins, outs = refs[:na], refs[na:2 * na]
        send_sems, recv_sems, local_sems = refs[2 * na:]
        x, y, c = lax.axis_index("x"), lax.axis_index("y"), lax.axis_index("c")
        me, sibling = (x, y, c), (x, y, 1 - c)
        chips = [(1 - x, y), (x, 1 - y), (1 - x, 1 - y)]

        def slot(out, pos):
            return out.at[4 * pos[0] + 2 * pos[1] + pos[2]]

        def copy(a, k, block, to, src=None):
            return pltpu.make_async_remote_copy(
                src_ref=slot(outs[a], block) if src is None else src, dst_ref=slot(outs[a], block),
                send_sem=send_sems.at[7 * a + k], recv_sem=recv_sems.at[7 * a + k],
                device_id=to, device_id_type=pl.DeviceIdType.MESH)

        mine = [pltpu.make_async_copy(ins[a], slot(outs[a], me), local_sems.at[a]) for a in range(na)]
        for cp in mine:
            cp.start()
        first = []
        for a in range(na):
            first.append(copy(a, 0, me, sibling, src=ins[a]))
            first += [copy(a, 1 + j, me, (*chip, c), src=ins[a]) for j, chip in enumerate(chips)]
        for cp in first:
            cp.start()
        passed = []
        for j, chip in enumerate(chips):
            for a in range(na):
                copy(a, 1 + j, (*chip, c), me).wait_recv()
                cp = copy(a, 4 + j, (*chip, c), sibling)
                cp.start()
                passed.append(cp)
        for a in range(na):
            copy(a, 0, sibling, me).wait_recv()
            for j, chip in enumerate(chips):
                copy(a, 4 + j, (*chip, 1 - c), me).wait_recv()
        for cp in first + passed:
            cp.wait_send()
        for cp in mine:
            cp.wait()

    any_spec = pl.BlockSpec(memory_space=pl.ANY)
    return pl.pallas_call(
        body, name=name, in_specs=[any_spec] * na, out_specs=[any_spec] * na,
        out_shape=[jax.ShapeDtypeStruct((N_DEV,) + b.shape, b.dtype) for b in blocks],
        scratch_shapes=[pltpu.SemaphoreType.DMA((7 * na,)), pltpu.SemaphoreType.DMA((7 * na,)), pltpu.SemaphoreType.DMA((na,))],
    )(*blocks)


_HBM_SPEC = pl.BlockSpec(memory_space=pltpu.HBM)
_SEM_SPEC = pl.BlockSpec(memory_space=pltpu.SEMAPHORE)
_DATAFLOW = pltpu.CompilerParams(has_side_effects=pltpu.SideEffectType.DATAFLOW_SIDE_EFFECTING)


def _split_copies(which, ins, lands, send_sems, recv_sems, gather):
    x, y, c = lax.axis_index("x"), lax.axis_index("y"), lax.axis_index("c")
    me = 4 * x + 2 * y + c
    copies = []
    for a, src, land in zip(which, ins, lands):
        for d in range(1, N_DEV):
            px, py, pc = x ^ (d >> 2), y ^ ((d >> 1) & 1), c ^ (d & 1)
            copies.append(pltpu.make_async_remote_copy(
                src_ref=src if gather[a] else src.at[4 * px + 2 * py + pc], dst_ref=land.at[me],
                send_sem=send_sems.at[7 * a + d - 1], recv_sem=recv_sems.at[7 * a + d - 1],
                device_id=(px, py, pc), device_id_type=pl.DeviceIdType.MESH))
    return copies


def _own_copies(which, ins, lands, own_sems, gather):
    me = 4 * lax.axis_index("x") + 2 * lax.axis_index("y") + lax.axis_index("c")
    return [pltpu.make_async_copy(src if gather[a] else src.at[me], land.at[me], own_sems.at[a])
            for a, src, land in zip(which, ins, lands)]


def _send_start(srcs, gather, name):
    na = len(srcs)
    assert len(gather) == na
    land_types = [pltpu.HBM(((N_DEV,) + s.shape) if g else s.shape, s.dtype) for s, g in zip(srcs, gather)]

    def body(*refs):
        ins, lands = refs[:na], refs[na:2 * na]
        send_sems, recv_sems, own_sems, token = refs[2 * na], refs[2 * na + 1], refs[2 * na + 2], refs[-1]
        for cp in _split_copies(range(na), ins, lands, send_sems, recv_sems, gather):
            cp.start()
        for cp in _own_copies(range(na), ins, lands, own_sems, gather):
            cp.start()
        token[...] = jnp.zeros_like(token)

    hbm = lambda a: pltpu.with_memory_space_constraint(a, pltpu.HBM)
    outs = pl.pallas_call(
        body, name=name,
        out_shape=(pltpu.SemaphoreType.DMA((7 * na,)), pltpu.SemaphoreType.DMA((7 * na,)), pltpu.SemaphoreType.DMA((na,)),
                   *[pltpu.HBM(s.shape, s.dtype) for s in srcs], *land_types, jax.ShapeDtypeStruct((8, LANES), F32)),
        in_specs=[_HBM_SPEC] * (2 * na),
        out_specs=(_SEM_SPEC, _SEM_SPEC, _SEM_SPEC, *[_HBM_SPEC] * (2 * na), pl.BlockSpec(memory_space=pltpu.VMEM)),
        input_output_aliases={i: 3 + i for i in range(2 * na)}, compiler_params=_DATAFLOW,
    )(*[hbm(s) for s in srcs], *[hbm(lax.empty(t.shape, t.dtype)) for t in land_types])
    return dict(send=outs[0], recv=outs[1], own=outs[2], srcs=list(outs[3:3 + na]), lands=list(outs[3 + na:3 + 2 * na]),
                token=outs[-1], gather=gather)


def _send_wait(handle, after, name, only=None):
    which = list(range(len(handle["srcs"]))) if only is None else list(only)
    na = len(which)
    gather = handle["gather"]
    after = list(after)

    def body(*refs):
        ins, lands = refs[:na], refs[na:2 * na]
        send_sems, recv_sems, own_sems = refs[2 * na], refs[2 * na + 1], refs[2 * na + 2]
        for cp in _split_copies(which, ins, lands, send_sems, recv_sems, gather):
            cp.wait_send()
            cp.wait_recv()
        for cp in _own_copies(which, ins, lands, own_sems, gather):
            cp.wait()

    both = [handle["srcs"][a] for a in which] + [handle["lands"][a] for a in which]
    outs = pl.pallas_call(
        body, name=name, out_shape=[pltpu.HBM(a.shape, a.dtype) for a in both],
        in_specs=[_HBM_SPEC] * (2 * na) + [_SEM_SPEC] * 3 + [_ANY_SPEC] * len(after),
        out_specs=[_HBM_SPEC] * (2 * na), input_output_aliases={i: i for i in range(2 * na)}, compiler_params=_DATAFLOW,
    )(*both, handle["send"], handle["recv"], handle["own"], *after)
    return list(outs[na:])


def _in_hbm(a):
    return pltpu.with_memory_space_constraint(a, pltpu.HBM)


def _sum_slots(p_ref):
    g = p_ref[0].astype(F32)
    for s in range(1, p_ref.shape[0]):
        g = g + p_ref[s].astype(F32)
    return g


def _adamw_refs(g, w_ref, m_ref, v_ref, g_out, d_out, m_out, v_out):
    c1 = 1.0 - ADAM_B1 ** ADAM_STEP
    c2 = 1.0 - ADAM_B2 ** ADAM_STEP
    m_new = ADAM_B1 * m_ref[...] + (1.0 - ADAM_B1) * g
    v_new = ADAM_B2 * v_ref[...] + (1.0 - ADAM_B2) * (g * g)
    g_out[...] = g
    m_out[...] = m_new
    v_out[...] = v_new
    d_out[...] = -ADAM_LR * ((m_new / c1) / (jnp.sqrt(v_new / c2) + ADAM_EPS) + ADAM_WD * w_ref[...])


def _adamw(parts, w, m, v, name):
    npart, r, n = parts.shape
    tr = r if r <= 256 else max(t for t in range(16, 257, 16) if r % t == 0)

    def body(p_ref, w_ref, m_ref, v_ref, g_out, d_out, m_out, v_out):
        _adamw_refs(_sum_slots(p_ref), w_ref, m_ref, v_ref, g_out, d_out, m_out, v_out)

    blk = pl.BlockSpec((tr, n), lambda i: (i, 0))
    return pl.pallas_call(
        body, name=name, grid=(r // tr,), in_specs=[pl.BlockSpec((npart, tr, n), lambda i: (0, i, 0)), blk, blk, blk],
        out_specs=[blk] * 4, out_shape=[pltpu.HBM((r, n), F32)] * 4, compiler_params=_params("parallel"),
    )(*map(_in_hbm, (parts, w, m, v)))


def _adamw_small(parts, ws, ms, vs, loss_parts, name):
    n = len(parts)

    def body(*refs):
        ins, outs = refs[:4 * n + 1], refs[4 * n + 1:]
        for i in range(n):
            _adamw_refs(_sum_slots(ins[i]), ins[n + i], ins[2 * n + i], ins[3 * n + i], *outs[4 * i:4 * i + 4])
        outs[4 * n][...] = _sum_slots(ins[4 * n])

    out_shape = [pltpu.HBM(w.shape, F32) for w in ws for _ in range(4)]
    res = pl.pallas_call(body, name=name, out_shape=out_shape + [pltpu.HBM((1, LANES), F32)],
                         compiler_params=_params())(*map(_in_hbm, (*parts, *ws, *ms, *vs, loss_parts)))
    return [res[4 * i:4 * i + 4] for i in range(n)], res[4 * n]


REPLICATED = ("g_cq", "g_ckv", "w_uk", "w_uv", "ln1_g", "ln1_b", "conv_b", "ln2_g", "ln2_b")
ALL_WEIGHTS = ("w_in", "g_cq", "g_ckv", "w_uq", "w_uk", "w_uv", "w_o", "ln1_g", "ln1_b", "w_up", "conv_w", "conv_b",
               "w_down", "ln2_g", "ln2_b")


def kernel(x, w_in, g_cq, g_ckv, w_uq, w_uk, w_uv, w_o, ln1_g, ln1_b, w_up, conv_w, conv_b, w_down, ln2_g, ln2_b, loss_target, m_w_in, m_g_cq, m_g_ckv, m_w_uq, m_w_uk, m_w_uv, m_w_o, m_ln1_g, m_ln1_b, m_w_up, m_conv_w, m_conv_b, m_w_down, m_ln2_g, m_ln2_b, v_w_in, v_g_cq, v_g_ckv, v_w_uq, v_w_uk, v_w_uv, v_w_o, v_ln1_g, v_ln1_b, v_w_up, v_conv_w, v_conv_b, v_w_down, v_ln2_g, v_ln2_b):
    w = dict(w_in=w_in, g_cq=g_cq, g_ckv=g_ckv, w_uq=w_uq, w_uk=w_uk, w_uv=w_uv, w_o=w_o, ln1_g=ln1_g, ln1_b=ln1_b,
             w_up=w_up, conv_w=conv_w, conv_b=conv_b, w_down=w_down, ln2_g=ln2_g, ln2_b=ln2_b)
    m = dict(w_in=m_w_in, g_cq=m_g_cq, g_ckv=m_g_ckv, w_uq=m_w_uq, w_uk=m_w_uk, w_uv=m_w_uv, w_o=m_w_o, ln1_g=m_ln1_g,
             ln1_b=m_ln1_b, w_up=m_w_up, conv_w=m_conv_w, conv_b=m_conv_b, w_down=m_w_down, ln2_g=m_ln2_g, ln2_b=m_ln2_b)
    v = dict(w_in=v_w_in, g_cq=v_g_cq, g_ckv=v_g_ckv, w_uq=v_w_uq, w_uk=v_w_uk, w_uv=v_w_uv, w_o=v_w_o, ln1_g=v_ln1_g,
             ln1_b=v_ln1_b, w_up=v_w_up, conv_w=v_conv_w, conv_b=v_conv_b, w_down=v_w_down, ln2_g=v_ln2_g, ln2_b=v_ln2_b)
    me = 4 * lax.axis_index("x") + 2 * lax.axis_index("y") + lax.axis_index("c")
    wire = lambda a: a.astype(WIRE_DTYPE)
    pad_taps = lambda a: jnp.pad(a, ((0, 8 - a.shape[0]), (0, 0)))

    blocks = lambda a: wire(a).reshape((N_DEV, a.shape[0] // N_DEV) + a.shape[1:])

    g_in, g_uq, g_conv = _all_gather(
        [wire(w_in).T, _heads_major(wire(w_uq)), pad_taps(conv_w)],
        "gather_weights")
    late = _send_start([wire(w_o), wire(w_up).T, wire(w_down)], [True] * 3, "gather_late_start")
    r_uq_dev, e_uq = w_uq.shape[0], w_uq.shape[2]
    wq = jnp.transpose(g_uq.reshape(N_DEV, HEADS, r_uq_dev, e_uq), (1, 0, 2, 3)).reshape(HEADS, Q_RANK, e_uq)
    cw = dict(
        w_in_t=_split_pad_rows(g_in.reshape(-1, D_MODEL)).astype(MXU_DTYPE),
        wq=jnp.pad(wq, ((0, 0), (0, 0), (0, LANES - e_uq))).astype(MXU_DTYPE),
        wk=_pad_heads(w_uk, NOPE_DIM), wv=_pad_heads(w_uv, HEAD_DIM),
        conv_w=_ffn_interleave(jnp.transpose(g_conv[:, :conv_w.shape[0]], (1, 0, 2)).reshape(conv_w.shape[0], -1), 1),
        g_cq=_row(g_cq), g_ckv=_row(g_ckv), ln1_g=_row(ln1_g), ln1_b=_row(ln1_b), conv_b=_ffn_interleave(_row(conv_b), 1),
        ln2_g=_row(ln2_g), ln2_b=_row(ln2_b))

    def late_weights(stage, after):
        (got,) = _send_wait(late, [after], f"gather_{stage}_wait", only=[("w_o", "w_up", "w_down").index(stage)])
        full = got.reshape(-1, D_MODEL)
        if stage == "w_o":
            w_o_mla, w_o_dil = _pad_w_o(full)
            return dict(w_o_mla=w_o_mla, w_o_dil=w_o_dil)
        if stage == "w_up":
            return dict(w_up_t=_ffn_interleave(full, 0).astype(MXU_DTYPE))
        return dict(w_down=full.astype(MXU_DTYPE))

    sent = {}

    def on_grads(stage, g):
        if stage == "ffn":
            sent[stage] = _send_start([blocks(_ffn_deinterleave(g["w_up_t"], 0)), blocks(g["w_down"])], [False] * 2,
                                      "exchange_ffn_start")
        elif stage == "w_o":
            return []
        elif stage == "mla":
            d_uq = wire(jnp.transpose(g["wq"][:, :, :e_uq].reshape(HEADS, N_DEV, r_uq_dev, e_uq), (1, 0, 2, 3))
                        ).reshape(N_DEV, HEADS * r_uq_dev, e_uq)
            dense = lambda a, width: wire(jnp.transpose(a[:, :, :width], (0, 2, 1))).reshape(-1, a.shape[1])
            small = dict(g_cq=g["g_cq"], g_ckv=g["g_ckv"], w_uk=dense(g["wk"], NOPE_DIM), w_uv=dense(g["wv"], HEAD_DIM),
                         ln1_g=g["ln1_g"], ln1_b=g["ln1_b"], conv_b=_ffn_deinterleave(g["conv_b"], 1), ln2_g=g["ln2_g"],
                         ln2_b=g["ln2_b"])
            everyone = [small[n] for n in REPLICATED] + [_ffn_deinterleave(g["conv_w"], 1), g["loss"]]
            sent[stage] = _send_start([blocks(_unpad_w_o(g["w_o_mla"], g["w_o_dil"])), d_uq] + everyone,
                                      [False] * 2 + [True] * len(everyone), "exchange_mla_start")
        else:
            sent[stage] = _send_start([blocks(_split_unpad_rows(g["w_in_t"]))], [False], "exchange_w_in_start")
        return [sent[stage]["token"]]

    _, grad_x, _ = _layer_grads(x[0], loss_target[0], cw, [late["token"]], late_weights, on_grads)


    out = {}

    def update(name, parts, view=None):
        to2d = {None: lambda a: a, "t": lambda a: a.T, "heads": _heads_major}[view]
        back = {None: lambda a: a, "t": lambda a: a.T, "heads": lambda a: _heads_minor(a, HEADS)}[view]
        res = _adamw(parts, to2d(w[name]), to2d(m[name]), to2d(v[name]), "adamw_" + name)
        for kind, a in zip(("grad", "delta", "new_m", "new_v"), res):
            out[kind, name] = back(a)
        return res[0]

    r_up, r_down = _send_wait(sent["ffn"], [grad_x], "exchange_ffn_wait")
    r_o, r_uq, *rep_all, cw_all, loss_all = _send_wait(sent["mla"], [grad_x], "exchange_mla_wait")
    done = [update("w_up", r_up, "t"), update("w_down", r_down), update("w_o", r_o), update("w_uq", r_uq, "heads")]
    rank_minor = ("w_uk", "w_uv")
    two_d = lambda n, a: jnp.transpose(a, (1, 2, 0)).reshape(-1, a.shape[0]) if n in rank_minor else a.reshape(1, -1)
    res, loss_sum = _adamw_small(rep_all, *[[two_d(n, d[n]) for n in REPLICATED] for d in (w, m, v)], loss_all, "adamw_replicated")
    for n, quad in zip(REPLICATED, res):
        for kind, a in zip(("grad", "delta", "new_m", "new_v"), quad):
            out[kind, n] = jnp.transpose(a.reshape(HEADS, -1, a.shape[1]), (2, 0, 1)) if n in rank_minor else a.reshape(w[n].shape)
    loss = loss_sum[0, 0]
    ncw = conv_w.shape[1]
    done += [loss_sum, update("conv_w", lax.dynamic_slice_in_dim(cw_all[:, :conv_w.shape[0]], me * ncw, ncw, axis=2))]
    (r_in,) = _send_wait(sent["w_in"], done, "exchange_w_in_wait")
    update("w_in", r_in, "t")

    return (loss, grad_x[None], *[out[kind, n] for kind in ("grad", "delta", "new_m", "new_v") for n in ALL_WEIGHTS])
```

```python
import math

import jax
import jax.numpy as jnp
import numpy as np
from jax import lax
from jax.experimental import pallas as pl
from jax.experimental.pallas import tpu as pltpu

F32 = jnp.float32
MXU_DTYPE = jnp.bfloat16
WIRE_DTYPE = jnp.bfloat16

N_DEV = 8
D_MODEL = 1024
HEADS = 8
HEAD_DIM = 64
LANES = 128
Q_RANK, KV_RANK, ROPE_DIM, NOPE_DIM = 256, 128, 32, 64
DIL_WIDTH = HEADS * HEAD_DIM
MLA_WIDTH = HEADS * HEAD_DIM
IN_PAD = 2048
DH_PART = IN_PAD // 4
D_FF = 2816
ROPE_THETA = 10000.0
DIL_PAIRS = ((128, 1), (512, 4), (2048, 16))
DIL_BLOCK = 128
DN_ALPHA = 2.0 ** 0.25
LN_EPS = 1e-5
RMS_EPS = 1e-6
ONES_LANE = HEAD_DIM
MLA_SCALE = 1.0 / math.sqrt(NOPE_DIM + ROPE_DIM)
MLA_SCALE_LOG2 = MLA_SCALE * math.log2(math.e)
MLA_BWD_SPLITS = 2
MLA_BWD_SPLITS_DIAGONAL = 4
MLA_FWD_SPLITS_DIAGONAL = 2
DIL_SCALE = 1.0 / math.sqrt(HEAD_DIM)
ALIBI_SLOPES = tuple(2.0 ** (-8.0 * (h + 1) / HEADS) for h in range(HEADS))
NEG_BIG = -1e30
ADAM_LR, ADAM_B1, ADAM_B2, ADAM_EPS, ADAM_WD, ADAM_STEP = 0.001, 0.9, 0.999, 1e-08, 0.01, 10
VMEM_LIMIT = 48 * 1024 * 1024


def _params(*sem):
    return pltpu.CompilerParams(dimension_semantics=sem or None, vmem_limit_bytes=VMEM_LIMIT)


def _dot(a, b, ca, cb):
    return lax.dot_general(a, b, (((ca,), (cb,)), ((), ())), preferred_element_type=F32)


_ANY_SPEC = pl.BlockSpec(memory_space=pl.ANY)


def _mm(a, b, *, name, tm, tn, tk, ta=False, tb=False, out_dtype=F32, after=()):
    m, k = (a.shape[1], a.shape[0]) if ta else a.shape
    n = b.shape[0] if tb else b.shape[1]
    assert (b.shape[1] if tb else b.shape[0]) == k
    tm, tn, tk = min(tm, m), min(tn, n), min(tk, k)
    assert m % tm == 0 and n % tn == 0 and k % tk == 0, (name, m, n, k, tm, tn, tk)
    nk = k // tk
    a_spec = (pl.BlockSpec((tk, tm), lambda i, j, kk: (kk, i)) if ta
              else pl.BlockSpec((tm, tk), lambda i, j, kk: (i, kk)))
    b_mode = dict(pipeline_mode=pl.Buffered(1)) if (tn == n and tk == k) else {}
    b_spec = (pl.BlockSpec((tn, tk), lambda i, j, kk: (j, kk), **b_mode) if tb
              else pl.BlockSpec((tk, tn), lambda i, j, kk: (kk, j), **b_mode))
    o_spec = pl.BlockSpec((tm, tn), lambda i, j, kk: (i, j))
    n_in = 2 + len(after)
    ca, cb = (0 if ta else 1), (1 if tb else 0)

    def body(*refs):
        a_ref, b_ref, o_ref = refs[0], refs[1], refs[n_in]
        part = _dot(a_ref[...].astype(MXU_DTYPE), b_ref[...].astype(MXU_DTYPE), ca, cb)
        if nk == 1:
            o_ref[...] = part.astype(o_ref.dtype)
            return
        acc_ref = refs[-1]
        kk = pl.program_id(2)

        @pl.when(kk == 0)
        def _():
            acc_ref[...] = part

        @pl.when(kk > 0)
        def _():
            acc_ref[...] += part

        @pl.when(kk == nk - 1)
        def _():
            o_ref[...] = acc_ref[...].astype(o_ref.dtype)

    return pl.pallas_call(
        body, name=name, grid=(m // tm, n // tn, nk), in_specs=[a_spec, b_spec] + [_ANY_SPEC] * len(after), out_specs=o_spec,
        out_shape=jax.ShapeDtypeStruct((m, n), out_dtype),
        scratch_shapes=[pltpu.VMEM((tm, tn), F32)] if nk > 1 else [],
        compiler_params=_params("parallel", "parallel", "arbitrary"),
    )(a, b, *after)


def _w_o_bwd(dz, o_mla, o_dil, w_o_mla, w_o_dil, tm=1024):
    seq, d = dz.shape
    tm = min(tm, seq)
    nstep = seq // tm

    def body(a_ref, om_ref, od_ref, wm_ref, wd_ref, dom_ref, dod_ref, dwm_ref, dwd_ref, accm_ref, accd_ref):
        step = pl.program_id(0)

        @pl.when(step == 0)
        def _():
            accm_ref[...] = jnp.zeros_like(accm_ref)
            accd_ref[...] = jnp.zeros_like(accd_ref)

        a = a_ref[...]
        for hd in range(HEADS):
            rows = slice(LANES * hd, LANES * (hd + 1))
            dom_ref[hd] = _dot(a, wm_ref[rows, :], 1, 1)
            accm_ref[rows, :] += _dot(om_ref[hd], a, 0, 0)
        dod_ref[...] = _dot(a, wd_ref[...], 1, 1)
        accd_ref[...] += _dot(od_ref[...], a, 0, 0)

        @pl.when(step == nstep - 1)
        def _():
            dwm_ref[...] = accm_ref[...].astype(dwm_ref.dtype)
            dwd_ref[...] = accd_ref[...].astype(dwd_ref.dtype)

    once = dict(pipeline_mode=pl.Buffered(1))
    return pl.pallas_call(
        body, name="w_o_bwd", grid=(nstep,),
        in_specs=[pl.BlockSpec((tm, d), lambda i: (i, 0)), pl.BlockSpec((HEADS, tm, LANES), lambda i: (0, i, 0)),
                  pl.BlockSpec((tm, DIL_WIDTH), lambda i: (i, 0)), pl.BlockSpec((HEADS * LANES, d), lambda i: (0, 0), **once),
                  pl.BlockSpec((DIL_WIDTH, d), lambda i: (0, 0), **once)],
        out_specs=[pl.BlockSpec((HEADS, tm, LANES), lambda i: (0, i, 0)), pl.BlockSpec((tm, DIL_WIDTH), lambda i: (i, 0)),
                   pl.BlockSpec((HEADS * LANES, d), lambda i: (0, 0)), pl.BlockSpec((DIL_WIDTH, d), lambda i: (0, 0))],
        out_shape=[jax.ShapeDtypeStruct((HEADS, seq, LANES), F32), jax.ShapeDtypeStruct((seq, DIL_WIDTH), F32),
                   jax.ShapeDtypeStruct((HEADS * LANES, d), MXU_DTYPE), jax.ShapeDtypeStruct((DIL_WIDTH, d), MXU_DTYPE)],
        scratch_shapes=[pltpu.VMEM((HEADS * LANES, d), F32), pltpu.VMEM((DIL_WIDTH, d), F32)],
        compiler_params=_params("arbitrary"),
    )(dz, o_mla, o_dil, w_o_mla, w_o_dil)


def _rope_tables(seq):
    half = ROPE_DIM // 2
    f32 = np.float32
    freqs = np.power(f32(ROPE_THETA), -np.arange(half, dtype=f32) / f32(half))
    ang = np.arange(seq, dtype=f32)[:, None] * freqs[None, :]
    cos, sin = np.cos(ang, dtype=f32), np.sin(ang, dtype=f32)
    one = np.ones((seq, NOPE_DIM), f32)
    tail = np.ones((seq, LANES - NOPE_DIM - ROPE_DIM), f32)
    ctab = np.concatenate([one, cos, cos, tail], axis=1)
    stab = np.concatenate([0 * one, -sin, sin, 0 * tail], axis=1)
    return jnp.asarray(ctab), jnp.asarray(stab)


def _rope_swap(t):
    lane = lax.broadcasted_iota(jnp.int32, t.shape, 1)
    half = ROPE_DIM // 2
    return jnp.where(lane < NOPE_DIM + half, pltpu.roll(t, LANES - half, 1), pltpu.roll(t, half, 1))


def _rope(t, ctab, stab):
    return t * ctab + _rope_swap(t) * stab


def _rope_inv(t, ctab, stab):
    return t * ctab - _rope_swap(t) * stab


def _rms(x, g):
    r = lax.rsqrt(jnp.mean(x * x, axis=-1, keepdims=True) + RMS_EPS)
    xh = x * r
    return xh, r, xh * g


def _mla_prep(h, g_cq, g_ckv, wq, wk, wv, ctab, stab, tm=512):
    seq = h.shape[0]
    tm = min(tm, seq)

    def body(h_ref, gq_ref, gk_ref, wq_ref, wk_ref, wv_ref, c_ref, s_ref, q_out, k_out, v_out):
        hb = h_ref[...]
        ctab_, stab_ = c_ref[...], s_ref[...]
        _, _, cqn = _rms(hb[:, :Q_RANK], gq_ref[...])
        _, _, ckn = _rms(hb[:, Q_RANK:Q_RANK + KV_RANK], gk_ref[...])
        cqn = cqn.astype(MXU_DTYPE)
        ckn = ckn.astype(MXU_DTYPE)
        krr = _rope(hb[:, Q_RANK + KV_RANK:], ctab_, stab_)
        ones_lane = (lax.broadcasted_iota(jnp.int32, (1, LANES), 1) == ONES_LANE).astype(F32)
        for hd in range(HEADS):
            q = _dot(cqn, wq_ref[hd], 1, 0)
            q_out[hd] = _rope(q, ctab_, stab_).astype(q_out.dtype)
            k_out[hd] = (_dot(ckn, wk_ref[hd], 1, 0) + krr).astype(k_out.dtype)
            v_out[hd] = (_dot(ckn, wv_ref[hd], 1, 0) + ones_lane).astype(v_out.dtype)

    full = lambda *shape: pl.BlockSpec(shape, lambda i: (0,) * len(shape))
    slab = pl.BlockSpec((HEADS, tm, LANES), lambda i: (0, i, 0))
    shp = jax.ShapeDtypeStruct((HEADS, seq, LANES), MXU_DTYPE)
    return pl.pallas_call(
        body, name="mla_prep", grid=(seq // tm,),
        in_specs=[pl.BlockSpec((tm, DH_PART), lambda i: (i, 0)), full(1, Q_RANK), full(1, KV_RANK),
                  full(HEADS, Q_RANK, LANES), full(HEADS, KV_RANK, LANES), full(HEADS, KV_RANK, LANES),
                  pl.BlockSpec((tm, LANES), lambda i: (i, 0)), pl.BlockSpec((tm, LANES), lambda i: (i, 0))],
        out_specs=[slab, slab, slab], out_shape=[shp, shp, shp],
        compiler_params=_params("parallel"),
    )(h, g_cq, g_ckv, wq, wk, wv, ctab, stab)


def _mla_prep_bwd(h, g_cq, g_ckv, wq, wk, wv, ctab, stab, dq, dk, dv, tm=512, after=()):
    seq = h.shape[0]
    tm = min(tm, seq)
    n_after = len(after)

    def body(h_ref, gq_ref, gk_ref, wq_ref, wk_ref, wv_ref, c_ref, s_ref, dq_ref, dk_ref, dv_ref, *rest):
        dh_ref, dwq_ref, dwk_ref, dwv_ref, dgq_ref, dgk_ref = rest[n_after:]

        @pl.when(pl.program_id(0) == 0)
        def _():
            for r in (dwq_ref, dwk_ref, dwv_ref, dgq_ref, dgk_ref):
                r[...] = jnp.zeros_like(r)

        hb = h_ref[...]
        ctab_, stab_ = c_ref[...], s_ref[...]
        gq, gk = gq_ref[...], gk_ref[...]
        xq, rq, cqn = _rms(hb[:, :Q_RANK], gq)
        xk, rk, ckn = _rms(hb[:, Q_RANK:Q_RANK + KV_RANK], gk)
        cqn = cqn.astype(MXU_DTYPE)
        ckn = ckn.astype(MXU_DTYPE)
        d_cqn = jnp.zeros((tm, Q_RANK), F32)
        d_ckn = jnp.zeros((tm, KV_RANK), F32)
        d_krr = jnp.zeros((tm, LANES), F32)
        for hd in range(HEADS):
            dqh = _rope_inv(dq_ref[hd], ctab_, stab_).astype(MXU_DTYPE)
            d_cqn += _dot(dqh, wq_ref[hd], 1, 1)
            dwq_ref[hd] += _dot(cqn, dqh, 0, 0)
            dkh = dk_ref[hd]
            d_krr += dkh
            dkh = dkh.astype(MXU_DTYPE)
            d_ckn += _dot(dkh, wk_ref[hd], 1, 1)
            dwk_ref[hd] += _dot(ckn, dkh, 0, 0)
            dvh = dv_ref[hd].astype(MXU_DTYPE)
            d_ckn += _dot(dvh, wv_ref[hd], 1, 1)
            dwv_ref[hd] += _dot(ckn, dvh, 0, 0)
        lane = lax.broadcasted_iota(jnp.int32, (tm, LANES), 1)
        rot = (lane >= NOPE_DIM) & (lane < NOPE_DIM + ROPE_DIM)
        d_kr = jnp.where(rot, _rope_inv(jnp.where(rot, d_krr, 0.0), ctab_, stab_), 0.0)

        def rms_bwd(dy, xh, r, g, dg_ref):
            dg_ref[...] += jnp.sum(dy * xh, axis=0, keepdims=True)
            dxh = dy * g
            return r * (dxh - xh * jnp.mean(dxh * xh, axis=-1, keepdims=True))

        d_cq = rms_bwd(d_cqn, xq, rq, gq, dgq_ref)
        d_ck = rms_bwd(d_ckn, xk, rk, gk, dgk_ref)
        dh_ref[...] = jnp.concatenate([d_cq, d_ck, d_kr], axis=1).astype(dh_ref.dtype)

    full = lambda *shape: pl.BlockSpec(shape, lambda i: (0,) * len(shape))
    slab = pl.BlockSpec((HEADS, tm, LANES), lambda i: (0, i, 0))
    return pl.pallas_call(
        body, name="mla_prep_bwd", grid=(seq // tm,),
        in_specs=[pl.BlockSpec((tm, DH_PART), lambda i: (i, 0)), full(1, Q_RANK), full(1, KV_RANK),
                  full(HEADS, Q_RANK, LANES), full(HEADS, KV_RANK, LANES), full(HEADS, KV_RANK, LANES),
                  pl.BlockSpec((tm, LANES), lambda i: (i, 0)), pl.BlockSpec((tm, LANES), lambda i: (i, 0)),
                  slab, slab, slab] + [_ANY_SPEC] * n_after,
        out_specs=[pl.BlockSpec((tm, DH_PART), lambda i: (i, 0)), full(HEADS, Q_RANK, LANES), full(HEADS, KV_RANK, LANES),
                   full(HEADS, KV_RANK, LANES), full(1, Q_RANK), full(1, KV_RANK)],
        out_shape=[jax.ShapeDtypeStruct((seq, DH_PART), MXU_DTYPE), jax.ShapeDtypeStruct((HEADS, Q_RANK, LANES), F32),
                   jax.ShapeDtypeStruct((HEADS, KV_RANK, LANES), F32), jax.ShapeDtypeStruct((HEADS, KV_RANK, LANES), F32),
                   jax.ShapeDtypeStruct((1, Q_RANK), F32), jax.ShapeDtypeStruct((1, KV_RANK), F32)],
        compiler_params=_params("arbitrary"),
    )(h, g_cq, g_ckv, wq, wk, wv, ctab, stab, dq, dk, dv, *after)


def _mla_attn_fwd(q, k, v, t=1024):
    _, seq, _ = q.shape
    t = min(t, seq)

    def body(q_ref, k_ref, v_ref, o_ref, ob_ref, lse_ref, m_ref, acc_ref, s_ref):
        i = pl.program_id(1)
        qb = q_ref[...]
        m_ref[...] = jnp.full_like(m_ref, NEG_BIG)
        acc_ref[...] = jnp.zeros_like(acc_ref)

        def scores(j):
            return _dot(qb, k_ref[pl.ds(pl.multiple_of(j * t, t), t), :], 1, 1) * MLA_SCALE_LOG2

        def softmax_pv(j, s, rows=slice(None), mask=None):
            vb = v_ref[pl.ds(pl.multiple_of(j * t, t), s.shape[1]), :]
            if mask is not None:
                s = jnp.where(mask, s, NEG_BIG)
            m_old = m_ref[rows, :]
            m_new = jnp.maximum(m_old, jnp.max(s, axis=1, keepdims=True))
            p = jnp.exp2(s - m_new)
            a = jnp.exp2(m_old - m_new)
            acc_ref[rows, :] = a * acc_ref[rows, :] + _dot(p.astype(MXU_DTYPE), vb, 1, 0)
            m_ref[rows, :] = m_new

        def softmax_pv_diagonal(j):
            th = t // MLA_FWD_SPLITS_DIAGONAL
            for hf in range(MLA_FWD_SPLITS_DIAGONAL):
                nk = (hf + 1) * th
                row = lax.broadcasted_iota(jnp.int32, (th, nk), 0) + hf * th
                rows = slice(hf * th, (hf + 1) * th)
                softmax_pv(j, s_ref[rows, 0:nk], rows, row >= lax.broadcasted_iota(jnp.int32, (th, nk), 1))

        s_ref[...] = scores(0)

        def loop_body(j, c):
            s_next = scores(j + 1)
            softmax_pv(j, s_ref[...])
            s_ref[...] = s_next
            return c

        lax.fori_loop(0, i, loop_body, 0)
        softmax_pv_diagonal(i)
        acc = acc_ref[...]
        l = acc[:, ONES_LANE:ONES_LANE + 1]
        o = jnp.where(lax.broadcasted_iota(jnp.int32, acc.shape, 1) < HEAD_DIM, acc * (1.0 / l), 0.0)
        o_ref[...] = o
        ob_ref[...] = o.astype(ob_ref.dtype)
        lse_ref[...] = jnp.broadcast_to(m_ref[...] + jnp.log2(l), lse_ref.shape)

    blk = pl.BlockSpec((None, t, LANES), lambda h, i: (h, i, 0))
    whole = pl.BlockSpec((None, seq, LANES), lambda h, i: (h, 0, 0))
    shp = jax.ShapeDtypeStruct((HEADS, seq, LANES), F32)
    return pl.pallas_call(
        body, name="mla_attn_fwd", grid=(HEADS, seq // t),
        in_specs=[blk, whole, whole], out_specs=[blk, blk, blk],
        out_shape=[shp, jax.ShapeDtypeStruct((HEADS, seq, LANES), MXU_DTYPE), shp],
        scratch_shapes=[pltpu.VMEM((t, 1), F32), pltpu.VMEM((t, LANES), F32), pltpu.VMEM((t, t), F32)],
        compiler_params=_params("parallel", "arbitrary"),
    )(q, k, v)


def _mla_attn_bwd(q, k, v, o, lse, do, t=1024):
    _, seq, _ = q.shape
    t = min(t, seq)
    nb = seq // t

    def body(q_ref, k_ref, v_ref, o_ref, lse_ref, do_ref, dq_ref, dk_ref, dv_ref, dl_ref, dka_ref, dva_ref):
        dq_ref[...] = jnp.zeros_like(dq_ref)

        def delta_body(i, c):
            rows = pl.ds(pl.multiple_of(i * t, t), t)
            dl_ref[rows, :] = jnp.sum(do_ref[rows, :] * o_ref[rows, :], axis=1, keepdims=True)
            return c

        lax.fori_loop(0, nb, delta_body, 0)

        def kblock(j, c):
            krows = pl.ds(pl.multiple_of(j * t, t), t)
            kb = k_ref[krows, :]
            vb = v_ref[krows, :]
            dka_ref[...] = jnp.zeros_like(dka_ref)
            dva_ref[...] = jnp.zeros_like(dva_ref)

            def qstep(i, masked):
                ns = MLA_BWD_SPLITS_DIAGONAL if masked else MLA_BWD_SPLITS
                th = t // ns
                rows = [pl.ds(pl.multiple_of(i * t + hf * th, th), th) for hf in range(ns)]
                qs = [q_ref[r, :] for r in rows]
                dos = [do_ref[r, :].astype(MXU_DTYPE) for r in rows]
                nkeys = [(hf + 1) * th if masked else t for hf in range(ns)]
                ss = [_dot(qs[hf], kb[:nkeys[hf]], 1, 1) * MLA_SCALE_LOG2 for hf in range(ns)]
                dps = [_dot(dos[hf], vb[:nkeys[hf]], 1, 1) for hf in range(ns)]
                for hf in range(ns):
                    s, nk = ss[hf], nkeys[hf]
                    if masked:
                        row = lax.broadcasted_iota(jnp.int32, (th, nk), 0) + hf * th
                        s = jnp.where(row >= lax.broadcasted_iota(jnp.int32, (th, nk), 1), s, NEG_BIG)
                    p = jnp.exp2(s - lse_ref[rows[hf], 0:1])
                    dva_ref[0:nk, :] += _dot(p.astype(MXU_DTYPE), dos[hf], 0, 0)
                    ds = (p * (dps[hf] - dl_ref[rows[hf], :]) * MLA_SCALE).astype(MXU_DTYPE)
                    dka_ref[0:nk, :] += _dot(ds, qs[hf], 0, 0)
                    dq_ref[rows[hf], :] += _dot(ds, kb[:nk], 1, 0)

            qstep(j, True)

            def qloop(i, c2):
                qstep(i, False)
                return c2

            lax.fori_loop(j + 1, nb, qloop, 0)
            dk_ref[krows, :] = dka_ref[...]
            dv_ref[krows, :] = dva_ref[...]
            return c

        lax.fori_loop(0, nb, kblock, 0)

    whole = pl.BlockSpec((None, seq, LANES), lambda h: (h, 0, 0))
    shp = jax.ShapeDtypeStruct((HEADS, seq, LANES), F32)
    return pl.pallas_call(
        body, name="mla_attn_bwd", grid=(HEADS,),
        in_specs=[whole] * 6, out_specs=[whole] * 3, out_shape=[shp] * 3,
        scratch_shapes=[pltpu.VMEM((seq, 1), F32), pltpu.VMEM((t, LANES), F32), pltpu.VMEM((t, LANES), F32)],
        compiler_params=_params("parallel"),
    )(q, k, v, o, lse, do)


DIL_CHUNK = DIL_BLOCK * max(d for _, d in DIL_PAIRS)
DIL_PAIR_LANES = 2 * HEAD_DIM
assert DIL_PAIR_LANES == LANES
DIL_UNROLL_FWD = 16
DIL_UNROLL_BWD = 16


def _dil_bias_tables(hp, dil):
    b = DIL_BLOCK
    iq = lax.broadcasted_iota(jnp.int32, (b, 2 * b), 0)
    ik = lax.broadcasted_iota(jnp.int32, (b, 2 * b), 1)
    off = iq + b - ik
    band = (off >= 0) & (off <= b)
    dist = (off * dil).astype(F32)
    every, first = [], []
    for hh in range(2):
        slope = jnp.where(hp == 0, ALIBI_SLOPES[hh], jnp.where(hp == 1, ALIBI_SLOPES[2 + hh],
                          jnp.where(hp == 2, ALIBI_SLOPES[4 + hh], ALIBI_SLOPES[6 + hh]))).astype(F32)
        bias = -slope * dist
        every.append(jnp.where(band, bias, NEG_BIG))
        first.append(jnp.where(band & (ik >= b), bias, NEG_BIG))
    return jnp.concatenate(every, axis=0), jnp.concatenate(first, axis=0)


def _dil_rows(start, dil):
    return pl.ds(start, DIL_BLOCK) if dil == 1 else pl.ds(start, DIL_BLOCK, stride=dil)


def _dil_block_pos(blk, c, dil):
    sc, r = blk // dil, blk % dil
    q0 = sc * (DIL_BLOCK * dil) + r
    kcur0 = c * DIL_CHUNK + q0
    first = kcur0 < DIL_BLOCK * dil
    kprev0 = jnp.where(first, kcur0, kcur0 - DIL_BLOCK * dil)
    return q0, kcur0, kprev0, first


def _pair_cols(hh):
    return slice(HEAD_DIM * hh, HEAD_DIM * (hh + 1))


def _first_head_lanes(shape):
    return lax.broadcasted_iota(jnp.int32, shape, 1) < HEAD_DIM


def _stack_pair(t):
    first = _first_head_lanes(t.shape)
    return jnp.concatenate([jnp.where(first, t, 0.0), jnp.where(first, 0.0, t)], axis=0).astype(MXU_DTYPE)


def _unstack_pair(t):
    rows = t.shape[0] // 2
    return jnp.where(_first_head_lanes((rows, t.shape[1])), t[:rows], t[rows:])


def _pair_column(t):
    return jnp.concatenate([t[:, 0:1], t[:, HEAD_DIM:HEAD_DIM + 1]], axis=0)


def _dil_fwd(h):
    seq = h.shape[0]
    assert seq % DIL_CHUNK == 0
    nblk = DIL_CHUNK // DIL_BLOCK
    rc = 256

    def body(q_ref, k_ref, v_ref, o_ref, ob_ref, lse_ref, *scr):
        o_scr, l_scr = scr[:3], scr[3:]
        hp, c = pl.program_id(0), pl.program_id(1)
        for bi, (_, dil) in enumerate(DIL_PAIRS):
            tables = _dil_bias_tables(hp, dil)

            def block(blk, carry, bi=bi, dil=dil, tables=tables):
                q0, kcur0, kprev0, first = _dil_block_pos(blk, c, dil)
                q2 = _stack_pair(q_ref[_dil_rows(q0, dil), :] * DIL_SCALE)
                kcat = jnp.concatenate([k_ref[_dil_rows(kprev0, dil), :], k_ref[_dil_rows(kcur0, dil), :]], axis=0).astype(MXU_DTYPE)
                vcat = jnp.concatenate([v_ref[_dil_rows(kprev0, dil), :], v_ref[_dil_rows(kcur0, dil), :]], axis=0).astype(MXU_DTYPE)
                s = _dot(q2, kcat, 1, 1) + jnp.where(first, tables[1], tables[0])
                mx = jnp.max(s, axis=1, keepdims=True)
                p = jnp.exp(s - mx)
                l = jnp.sum(p, axis=1, keepdims=True)
                o_scr[bi][_dil_rows(q0, dil), :] = _unstack_pair(_dot(p.astype(MXU_DTYPE), vcat, 1, 0) * (1.0 / l))
                l_scr[bi][_dil_rows(q0, dil), :] = _unstack_pair(jnp.broadcast_to(mx + jnp.log(l), (2 * DIL_BLOCK, LANES)))
                return carry

            lax.fori_loop(0, nblk, block, 0, unroll=DIL_UNROLL_FWD)

        def combine(i, carry):
            rows = pl.ds(pl.multiple_of(i * rc, rc), rc)
            ls = [l_scr[bi][rows, :] for bi in range(3)]
            mx = jnp.maximum(jnp.maximum(ls[0], ls[1]), ls[2])
            es = [jnp.exp(l - mx) for l in ls]
            den = es[0] + es[1] + es[2]
            o = (es[0] * o_scr[0][rows, :] + es[1] * o_scr[1][rows, :] + es[2] * o_scr[2][rows, :]) / den
            o_ref[rows, :] = o
            ob_ref[rows, :] = o.astype(ob_ref.dtype)
            lse_ref[rows, :] = mx + jnp.log(den)
            return carry

        lax.fori_loop(0, DIL_CHUNK // rc, combine, 0)

    nq = DIL_WIDTH // LANES
    chunk = lambda off: pl.BlockSpec((DIL_CHUNK, LANES), lambda hp, c: (c, off + hp))
    whole = lambda off: pl.BlockSpec((seq, LANES), lambda hp, c: (0, off + hp))
    shp = jax.ShapeDtypeStruct((seq, DIL_WIDTH), F32)
    return pl.pallas_call(
        body, name="dil_fwd", grid=(nq, seq // DIL_CHUNK),
        in_specs=[chunk(nq), whole(2 * nq), whole(3 * nq)], out_specs=[chunk(0), chunk(0), chunk(0)],
        out_shape=[shp, jax.ShapeDtypeStruct((seq, DIL_WIDTH), MXU_DTYPE), shp],
        scratch_shapes=[pltpu.VMEM((DIL_CHUNK, LANES), F32)] * 6,
        compiler_params=_params("parallel", "arbitrary"),
    )(h, h, h)


def _dil_bwd(h, o, lse, do, after=()):
    seq = h.shape[0]
    nblk = DIL_CHUNK // DIL_BLOCK
    nchunk = seq // DIL_CHUNK
    rc = 256

    n_after = len(after)

    def body(q_ref, k_ref, v_ref, o_ref, lse_ref, do_ref, *rest):
        dq_out, dk_out, dv_out, dl_scr, dq_ref, dk_ref, dv_ref = rest[n_after:]
        hp, c = pl.program_id(0), pl.program_id(1)

        @pl.when(c == 0)
        def _():
            dk_ref[...] = jnp.zeros_like(dk_ref)
            dv_ref[...] = jnp.zeros_like(dv_ref)

        def delta(i, carry):
            rows = pl.ds(pl.multiple_of(i * rc, rc), rc)
            prod = do_ref[rows, :] * o_ref[rows, :]
            dl_scr[rows, :] = jnp.concatenate(
                [jnp.broadcast_to(jnp.sum(prod[:, _pair_cols(hh)], axis=1, keepdims=True), (rc, HEAD_DIM)) for hh in range(2)], axis=1)
            return carry

        lax.fori_loop(0, DIL_CHUNK // rc, delta, 0)

        for bi, (_, dil) in enumerate(DIL_PAIRS):
            tables = _dil_bias_tables(hp, dil)

            def block(blk, carry, bi=bi, dil=dil, tables=tables):
                q0, kcur0, kprev0, first = _dil_block_pos(blk, c, dil)
                qrows = _dil_rows(q0, dil)
                q2 = _stack_pair(q_ref[qrows, :] * DIL_SCALE)
                kcat = jnp.concatenate([k_ref[_dil_rows(kprev0, dil), :], k_ref[_dil_rows(kcur0, dil), :]], axis=0).astype(MXU_DTYPE)
                vcat = jnp.concatenate([v_ref[_dil_rows(kprev0, dil), :], v_ref[_dil_rows(kcur0, dil), :]], axis=0).astype(MXU_DTYPE)
                do2 = _stack_pair(do_ref[qrows, :])
                s = _dot(q2, kcat, 1, 1) + jnp.where(first, tables[1], tables[0])
                p = jnp.exp(s - _pair_column(lse_ref[qrows, :]))
                dp = _dot(do2, vcat, 1, 1)
                ds = (p * (dp - _pair_column(dl_scr[qrows, :]))).astype(MXU_DTYPE)
                dq_b = _unstack_pair(_dot(ds, kcat, 1, 0)) * DIL_SCALE
                dk_b = _dot(ds, q2, 0, 0)
                dv_b = _dot(p.astype(MXU_DTYPE), do2, 0, 0)
                if bi == 0:
                    dq_ref[qrows, :] = dq_b
                else:
                    dq_ref[qrows, :] += dq_b
                dk_ref[_dil_rows(kprev0, dil), :] += dk_b[:DIL_BLOCK]
                dv_ref[_dil_rows(kprev0, dil), :] += dv_b[:DIL_BLOCK]
                dk_ref[_dil_rows(kcur0, dil), :] += dk_b[DIL_BLOCK:]
                dv_ref[_dil_rows(kcur0, dil), :] += dv_b[DIL_BLOCK:]
                return carry

            lax.fori_loop(0, nblk, block, 0, unroll=DIL_UNROLL_BWD)

        dq_out[...] = dq_ref[...].astype(dq_out.dtype)

        @pl.when(c == nchunk - 1)
        def _():
            dk_out[...] = dk_ref[...].astype(dk_out.dtype)
            dv_out[...] = dv_ref[...].astype(dv_out.dtype)

    nq = DIL_WIDTH // LANES
    chunk = lambda off: pl.BlockSpec((DIL_CHUNK, LANES), lambda hp, c: (c, off + hp))
    whole = lambda off: pl.BlockSpec((seq, LANES), lambda hp, c: (0, off + hp))
    shp = jax.ShapeDtypeStruct((seq, DIL_WIDTH), MXU_DTYPE)
    return pl.pallas_call(
        body, name="dil_bwd", grid=(nq, nchunk),
        in_specs=[chunk(nq), whole(2 * nq), whole(3 * nq), chunk(0), chunk(0), chunk(0)] + [_ANY_SPEC] * n_after,
        out_specs=[chunk(0), whole(0), whole(0)], out_shape=[shp, shp, shp],
        scratch_shapes=[pltpu.VMEM((DIL_CHUNK, LANES), F32), pltpu.VMEM((DIL_CHUNK, LANES), F32),
                        pltpu.VMEM((seq, LANES), F32), pltpu.VMEM((seq, LANES), F32)],
        compiler_params=_params("parallel", "arbitrary"),
    )(h, h, h, o, lse, do, *after)


def _mm_dx0(parts, w_in_t, res, tm=1024, after=()):
    seq, d = res.shape
    tm = min(tm, seq)
    n_after = len(after)

    def body(a0, a1, a2, a3, b_ref, r_ref, *rest):
        o_ref = rest[n_after]
        acc = _dot(a0[...], b_ref[0:DH_PART, :], 1, 0)
        for c, a in enumerate((a1, a2, a3), start=1):
            acc += _dot(a[...], b_ref[DH_PART * c:DH_PART * (c + 1), :], 1, 0)
        o_ref[...] = acc + DN_ALPHA * r_ref[...]

    blk = pl.BlockSpec((tm, DH_PART), lambda i: (i, 0))
    row = pl.BlockSpec((tm, d), lambda i: (i, 0))
    return pl.pallas_call(
        body, name="mm_dx0", grid=(seq // tm,),
        in_specs=[blk] * 4 + [pl.BlockSpec((IN_PAD, d), lambda i: (0, 0), pipeline_mode=pl.Buffered(1)), row] + [_ANY_SPEC] * n_after,
        out_specs=row, out_shape=jax.ShapeDtypeStruct((seq, d), F32), compiler_params=_params("parallel"),
    )(*parts, w_in_t, res, *after)


def _mm_dw_in(parts, x0, tk=1024):
    seq, d = x0.shape
    tk = min(tk, seq)
    nk = seq // tk

    def body(a0, a1, a2, a3, b_ref, o_ref, acc_ref):
        kk = pl.program_id(0)

        @pl.when(kk == 0)
        def _():
            acc_ref[...] = jnp.zeros_like(acc_ref)

        b = b_ref[...].astype(MXU_DTYPE)
        for c, a in enumerate((a0, a1, a2, a3)):
            acc_ref[DH_PART * c:DH_PART * (c + 1), :] += _dot(a[...], b, 0, 0)

        @pl.when(kk == nk - 1)
        def _():
            o_ref[...] = acc_ref[...].astype(o_ref.dtype)

    blk = pl.BlockSpec((tk, DH_PART), lambda kk: (kk, 0))
    return pl.pallas_call(
        body, name="mm_dw_in", grid=(nk,), in_specs=[blk] * 4 + [pl.BlockSpec((tk, d), lambda kk: (kk, 0))],
        out_specs=pl.BlockSpec((IN_PAD, d), lambda kk: (0, 0)), out_shape=jax.ShapeDtypeStruct((IN_PAD, d), MXU_DTYPE),
        scratch_shapes=[pltpu.VMEM((IN_PAD, d), F32)], compiler_params=_params("arbitrary"),
    )(*parts, x0)


def _ln_stats(z):
    mu = jnp.mean(z, axis=-1, keepdims=True)
    zc = z - mu
    r = lax.rsqrt(jnp.mean(zc * zc, axis=-1, keepdims=True) + LN_EPS)
    return zc * r, r


def _ln_bwd_math(dy, xh, r, g):
    dxh = dy * g
    return r * (dxh - jnp.mean(dxh, axis=-1, keepdims=True) - xh * jnp.mean(dxh * xh, axis=-1, keepdims=True))


def _mix_ln1(o_mla, o_dil, w_o_mla, w_o_dil, x0, g, b, tm=512):
    seq, d = x0.shape
    tm = min(tm, seq)

    def body(om_ref, od_ref, wm_ref, wd_ref, x_ref, g_ref, b_ref, z_ref, y_ref, yb_ref):
        mix = _dot(od_ref[...], wd_ref[...], 1, 0)
        for hd in range(HEADS):
            mix += _dot(om_ref[hd], wm_ref[LANES * hd:LANES * (hd + 1), :], 1, 0)
        z = DN_ALPHA * x_ref[...] + mix
        xh, _ = _ln_stats(z)
        y = xh * g_ref[...] + b_ref[...]
        z_ref[...] = z
        y_ref[...] = y
        yb_ref[...] = y.astype(yb_ref.dtype)

    blk = pl.BlockSpec((tm, d), lambda i: (i, 0))
    vec = pl.BlockSpec((1, d), lambda i: (0, 0))
    shp = jax.ShapeDtypeStruct((seq, d), F32)
    return pl.pallas_call(
        body, name="mix_ln1", grid=(seq // tm,),
        in_specs=[pl.BlockSpec((HEADS, tm, LANES), lambda i: (0, i, 0)), pl.BlockSpec((tm, DIL_WIDTH), lambda i: (i, 0)),
                  pl.BlockSpec((HEADS * LANES, d), lambda i: (0, 0)), pl.BlockSpec((DIL_WIDTH, d), lambda i: (0, 0)), blk, vec, vec],
        out_specs=[blk, blk, blk], out_shape=[shp, shp, jax.ShapeDtypeStruct((seq, d), MXU_DTYPE)],
        compiler_params=_params("parallel"))(o_mla, o_dil, w_o_mla, w_o_dil, x0, g, b)


def _dx1_ln1_bwd(du, w_up_t, dz2, z, g, tm=256, after=()):
    seq, d = z.shape
    kdim = du.shape[1]
    tm = min(tm, seq)
    n_after = len(after)

    def body(du_ref, w_ref, r_ref, z_ref, g_ref, *rest):
        dz_ref, dzb_ref, dg_ref, db_ref = rest[n_after:]

        @pl.when(pl.program_id(0) == 0)
        def _():
            dg_ref[...] = jnp.zeros_like(dg_ref)
            db_ref[...] = jnp.zeros_like(db_ref)

        dyb = _dot(du_ref[...], w_ref[...], 1, 0) + DN_ALPHA * r_ref[...]
        xh, r = _ln_stats(z_ref[...])
        dg_ref[...] += jnp.sum(dyb * xh, axis=0, keepdims=True)
        db_ref[...] += jnp.sum(dyb, axis=0, keepdims=True)
        dz = _ln_bwd_math(dyb, xh, r, g_ref[...])
        dz_ref[...] = dz
        dzb_ref[...] = dz.astype(dzb_ref.dtype)

    blk = pl.BlockSpec((tm, d), lambda i: (i, 0))
    vec = pl.BlockSpec((1, d), lambda i: (0, 0))
    return pl.pallas_call(
        body, name="dx1_ln1_bwd", grid=(seq // tm,),
        in_specs=[pl.BlockSpec((tm, kdim), lambda i: (i, 0)),
                  pl.BlockSpec((kdim, d), lambda i: (0, 0), pipeline_mode=pl.Buffered(1)), blk, blk, vec] + [_ANY_SPEC] * n_after,
        out_specs=[blk, blk, vec, vec],
        out_shape=[jax.ShapeDtypeStruct((seq, d), F32), jax.ShapeDtypeStruct((seq, d), MXU_DTYPE),
                   jax.ShapeDtypeStruct((1, d), F32), jax.ShapeDtypeStruct((1, d), F32)],
        compiler_params=_params("arbitrary"))(du, w_up_t, dz2, z, g, *after)


def _down_ln2_loss_bwd(act, w_down, x1, target, g, b, tm=512):
    seq, d = x1.shape
    kdim = act.shape[1]
    tm = min(tm, seq)

    def body(a_ref, w_ref, x_ref, t_ref, g_ref, b_ref, dz_ref, dzb_ref, loss_ref, dg_ref, db_ref):
        @pl.when(pl.program_id(0) == 0)
        def _():
            loss_ref[...] = jnp.zeros_like(loss_ref)
            dg_ref[...] = jnp.zeros_like(dg_ref)
            db_ref[...] = jnp.zeros_like(db_ref)

        gv = g_ref[...]
        z = DN_ALPHA * x_ref[...] + _dot(a_ref[...], w_ref[...], 1, 0)
        xh, r = _ln_stats(z)
        err = (xh * gv + b_ref[...]) - t_ref[...]
        loss_ref[...] += 0.5 * jnp.sum(jnp.mean(err * err, axis=-1, keepdims=True), axis=0, keepdims=True)
        dy = err * (1.0 / d)
        dg_ref[...] += jnp.sum(dy * xh, axis=0, keepdims=True)
        db_ref[...] += jnp.sum(dy, axis=0, keepdims=True)
        dz = _ln_bwd_math(dy, xh, r, gv)
        dz_ref[...] = dz
        dzb_ref[...] = dz.astype(dzb_ref.dtype)

    blk = pl.BlockSpec((tm, d), lambda i: (i, 0))
    vec = pl.BlockSpec((1, d), lambda i: (0, 0))
    return pl.pallas_call(
        body, name="down_ln2_loss_bwd", grid=(seq // tm,),
        in_specs=[pl.BlockSpec((tm, kdim), lambda i: (i, 0)),
                  pl.BlockSpec((kdim, d), lambda i: (0, 0), pipeline_mode=pl.Buffered(1)), blk, blk, vec, vec],
        out_specs=[blk, blk, pl.BlockSpec((1, LANES), lambda i: (0, 0)), vec, vec],
        out_shape=[jax.ShapeDtypeStruct((seq, d), F32), jax.ShapeDtypeStruct((seq, d), MXU_DTYPE),
                   jax.ShapeDtypeStruct((1, LANES), F32),
                   jax.ShapeDtypeStruct((1, d), F32), jax.ShapeDtypeStruct((1, d), F32)],
        compiler_params=_params("arbitrary"))(act, w_down, x1, target, g, b)


HALO = 16


def _conv_rows(e, w_ref, b_ref):
    y = b_ref[...] + w_ref[0:1, :] * pltpu.roll(e, 2, 0)
    y = y + w_ref[1:2, :] * pltpu.roll(e, 1, 0)
    return y + w_ref[2:3, :] * e


_GELU_C = math.sqrt(2.0 / math.pi)
_GELU_A = 0.044715


def _gelu(x):
    return x * (0.5 + 0.5 * jnp.tanh(x * (_GELU_C + (_GELU_C * _GELU_A) * (x * x))))


CONV_TN = 256


def _ffn_interleave(a, axis):
    shp = a.shape
    a = a.reshape(shp[:axis] + (2, D_FF // CONV_TN, CONV_TN) + shp[axis + 1:])
    return jnp.swapaxes(a, axis, axis + 1).reshape(shp)


def _ffn_deinterleave(a, axis):
    shp = a.shape
    a = a.reshape(shp[:axis] + (D_FF // CONV_TN, 2, CONV_TN) + shp[axis + 1:])
    return jnp.swapaxes(a, axis, axis + 1).reshape(shp)


def _up_conv_gate_fwd(x1, w_up_t, conv_w, conv_b, tm=1024):
    seq, d = x1.shape
    tm = min(tm, seq)
    tn = CONV_TN

    def body(x_ref, xp_ref, wu_ref, w_ref, b_ref, u_ref, y_ref, o_ref):
        first = pl.program_id(0) == 0
        halo = jnp.where(first, jnp.zeros_like(xp_ref), xp_ref[...])
        e = _dot(jnp.concatenate([halo, x_ref[...]], axis=0), wu_ref[...], 1, 1)
        u_ref[...] = e[HALO:]
        y = _conv_rows(e, w_ref, b_ref)[HALO:]
        y_ref[...] = y
        o_ref[...] = (_gelu(y[:, tn:]) * y[:, :tn]).astype(o_ref.dtype)

    hb = tm // HALO
    return pl.pallas_call(
        body, name="up_conv_gate_fwd", grid=(seq // tm, D_FF // tn),
        in_specs=[pl.BlockSpec((tm, d), lambda i, j: (i, 0)),
                  pl.BlockSpec((HALO, d), lambda i, j: (jnp.maximum(i * hb - 1, 0), 0)),
                  pl.BlockSpec((2 * tn, d), lambda i, j: (j, 0)),
                  pl.BlockSpec((3, 2 * tn), lambda i, j: (0, j)), pl.BlockSpec((1, 2 * tn), lambda i, j: (0, j))],
        out_specs=[pl.BlockSpec((tm, 2 * tn), lambda i, j: (i, j)), pl.BlockSpec((tm, 2 * tn), lambda i, j: (i, j)),
                   pl.BlockSpec((tm, tn), lambda i, j: (i, j))],
        out_shape=[jax.ShapeDtypeStruct((seq, 2 * D_FF), F32), jax.ShapeDtypeStruct((seq, 2 * D_FF), F32),
                   jax.ShapeDtypeStruct((seq, D_FF), MXU_DTYPE)],
        compiler_params=_params("parallel", "arbitrary"),
    )(x1, x1, w_up_t, conv_w, conv_b)


def _conv_gate_bwd(u, y, d_act, conv_w, tm=1024):
    seq = u.shape[0]
    tm = min(tm, seq)
    tn = CONV_TN
    ni = seq // tm
    rows = tm + HALO

    def body(u_ref, y_ref, yn_ref, da_ref, dan_ref, w_ref, du_ref, dw_ref, db_ref):
        i = pl.program_id(1)
        last = i == ni - 1

        @pl.when(i == 0)
        def _():
            dw_ref[...] = jnp.zeros_like(dw_ref)
            db_ref[...] = jnp.zeros_like(db_ref)

        yy = jnp.concatenate([y_ref[...], yn_ref[...]], axis=0)
        ya, yg = yy[:, :tn], yy[:, tn:]
        dact = jnp.concatenate([da_ref[...].astype(F32), jnp.where(last, 0.0, dan_ref[...].astype(F32))], axis=0)
        yg2 = yg * yg
        th = jnp.tanh(yg * (_GELU_C + (_GELU_C * _GELU_A) * yg2))
        half = 0.5 + 0.5 * th
        gelu = yg * half
        gelu_grad = half + gelu * (1.0 - half) * (2.0 * _GELU_C + (6.0 * _GELU_C * _GELU_A) * yg2)
        dy = jnp.concatenate([dact * gelu, dact * ya * gelu_grad], axis=1)
        dy0, dy1, dy2 = dy[:tm], pltpu.roll(dy, rows - 1, 0)[:tm], pltpu.roll(dy, rows - 2, 0)[:tm]
        du_ref[...] = (w_ref[2:3, :] * dy0 + w_ref[1:2, :] * dy1 + w_ref[0:1, :] * dy2).astype(du_ref.dtype)
        ut = u_ref[...]
        dw_ref[0:1, :] += jnp.sum(dy2 * ut, axis=0, keepdims=True)
        dw_ref[1:2, :] += jnp.sum(dy1 * ut, axis=0, keepdims=True)
        dw_ref[2:3, :] += jnp.sum(dy0 * ut, axis=0, keepdims=True)
        db_ref[...] += jnp.sum(dy0, axis=0, keepdims=True)

    hb = tm // HALO
    nh = seq // HALO
    nxt = lambda j, i: (jnp.minimum((i + 1) * hb, nh - 1), j)
    tile = pl.BlockSpec((tm, 2 * tn), lambda j, i: (i, j))
    return pl.pallas_call(
        body, name="conv_gate_bwd", grid=(D_FF // tn, ni),
        in_specs=[tile, tile, pl.BlockSpec((HALO, 2 * tn), nxt),
                  pl.BlockSpec((tm, tn), lambda j, i: (i, j)), pl.BlockSpec((HALO, tn), nxt),
                  pl.BlockSpec((3, 2 * tn), lambda j, i: (0, j))],
        out_specs=[tile, pl.BlockSpec((3, 2 * tn), lambda j, i: (0, j)), pl.BlockSpec((1, 2 * tn), lambda j, i: (0, j))],
        out_shape=[jax.ShapeDtypeStruct((seq, 2 * D_FF), MXU_DTYPE), jax.ShapeDtypeStruct((3, 2 * D_FF), F32),
                   jax.ShapeDtypeStruct((1, 2 * D_FF), F32)],
        compiler_params=_params("parallel", "arbitrary"),
    )(u, y, y, d_act, d_act, conv_w)


def _pad_heads(w, width):
    w = jnp.transpose(w, (1, 0, 2))
    return jnp.pad(w, ((0, 0), (0, 0), (0, LANES - width))).astype(MXU_DTYPE)


def _heads_major(a):
    return jnp.transpose(a, (1, 0, 2)).reshape(-1, a.shape[2])


def _heads_minor(a, heads):
    return jnp.transpose(a.reshape(heads, -1, a.shape[1]), (1, 0, 2))


_LATENT = Q_RANK + KV_RANK
_ROPE_AT = _LATENT + NOPE_DIM
_ROPE_END = _ROPE_AT + ROPE_DIM


def _split_pad_rows(w_t):
    z = lambda n: jnp.zeros((n, w_t.shape[1]), w_t.dtype)
    return jnp.concatenate([w_t[:_LATENT], z(_ROPE_AT - _LATENT), w_t[_LATENT:_LATENT + ROPE_DIM], z(DH_PART - _ROPE_END),
                            w_t[_LATENT + ROPE_DIM:]], axis=0)


def _split_unpad_rows(w_p):
    return jnp.concatenate([w_p[:_LATENT], w_p[_ROPE_AT:_ROPE_END], w_p[DH_PART:]], axis=0)


def _pad_w_o(w_o):
    mla = jnp.pad(w_o[:MLA_WIDTH].reshape(HEADS, HEAD_DIM, D_MODEL), ((0, 0), (0, LANES - HEAD_DIM), (0, 0)))
    return mla.reshape(HEADS * LANES, D_MODEL).astype(MXU_DTYPE), w_o[MLA_WIDTH:].astype(MXU_DTYPE)


def _unpad_w_o(d_mla, d_dil):
    return jnp.concatenate([d_mla.reshape(HEADS, LANES, D_MODEL)[:, :HEAD_DIM].reshape(MLA_WIDTH, D_MODEL), d_dil], axis=0)


def _row(v):
    return v.reshape(1, -1).astype(F32)


def _layer_grads(x0, target, cw, first_after=(), late_weights=None, on_grads=None):
    seq = x0.shape[0]
    ctab, stab = _rope_tables(seq)
    gq, gk = cw["g_cq"], cw["g_ckv"]
    wq, wk, wv = cw["wq"], cw["wk"], cw["wv"]
    notify = (lambda stage, grads: ()) if on_grads is None else on_grads

    h = _mm(x0, cw["w_in_t"], name="mm_h", tb=True, tm=1024, tn=IN_PAD, tk=1024, after=first_after)
    qf, kf, vp = _mla_prep(h, gq, gk, wq, wk, wv, ctab, stab)
    o_mla, o_mla_b, lse_mla = _mla_attn_fwd(qf, kf, vp)
    o_dil, o_dil_b, lse_dil = _dil_fwd(h)
    fetch = (lambda stage, after: {}) if late_weights is None else late_weights
    cw = {**cw, **fetch("w_o", o_mla_b), **fetch("w_up", o_mla_b)}
    cb = cw["conv_b"]
    z1, x1, x1b = _mix_ln1(o_mla_b, o_dil_b, cw["w_o_mla"], cw["w_o_dil"], x0, cw["ln1_g"], cw["ln1_b"])
    u, y, act = _up_conv_gate_fwd(x1b, cw["w_up_t"], cw["conv_w"], cb)
    cw = {**cw, **fetch("w_down", act)}
    dz2, dz2b, loss, d_ln2_g, d_ln2_b = _down_ln2_loss_bwd(act, cw["w_down"], x1, target, cw["ln2_g"], cw["ln2_b"])

    d_act = _mm(dz2b, cw["w_down"], name="mm_d_act", tb=True, out_dtype=MXU_DTYPE, tm=1024, tn=D_FF, tk=1024)
    d_w_down = _mm(act, dz2b, name="mm_dw_down", ta=True, out_dtype=MXU_DTYPE, tm=1408, tn=1024, tk=1024)
    du, d_conv_w, d_conv_b = _conv_gate_bwd(u, y, d_act, cw["conv_w"])
    d_w_up_t = _mm(du, x1b, name="mm_dw_up", ta=True, out_dtype=MXU_DTYPE, tm=1408, tn=1024, tk=2048)
    grads = dict(w_up_t=d_w_up_t, w_down=d_w_down, conv_w=d_conv_w, conv_b=d_conv_b, ln2_g=d_ln2_g, ln2_b=d_ln2_b)
    dz1, dz1b, d_ln1_g, d_ln1_b = _dx1_ln1_bwd(du, cw["w_up_t"], dz2, z1, cw["ln1_g"], after=notify("ffn", grads))
    do_mla, do_dil, d_w_o_mla, d_w_o_dil = _w_o_bwd(dz1b, o_mla_b, o_dil_b, cw["w_o_mla"], cw["w_o_dil"])
    grads.update(w_o_mla=d_w_o_mla, w_o_dil=d_w_o_dil, ln1_g=d_ln1_g, ln1_b=d_ln1_b)
    dqf, dkf, dvf = _mla_attn_bwd(qf, kf, vp, o_mla, lse_mla, do_mla)
    dh_mla, d_wq, d_wk, d_wv, d_gq, d_gk = _mla_prep_bwd(h, gq, gk, wq, wk, wv, ctab, stab, dqf, dkf, dvf,
                                                          after=notify("w_o", grads))
    grads.update(wq=d_wq, wk=d_wk, wv=d_wv, g_cq=d_gq, g_ckv=d_gk, loss=loss)
    dq_dil, dk_dil, dv_dil = _dil_bwd(h, o_dil, lse_dil, do_dil, after=notify("mla", grads))
    dh = (dh_mla, dq_dil, dk_dil, dv_dil)
    grads.update(w_in_t=_mm_dw_in(dh, x0))
    grad_x = _mm_dx0(dh, cw["w_in_t"], dz1, after=notify("w_in", grads))
    return loss, grad_x, grads


_HBM_SPEC = pl.BlockSpec(memory_space=pltpu.HBM)
_SEM_SPEC = pl.BlockSpec(memory_space=pltpu.SEMAPHORE)
_DATAFLOW = pltpu.CompilerParams(has_side_effects=pltpu.SideEffectType.DATAFLOW_SIDE_EFFECTING)


def _split_copies(which, ins, lands, send_sems, recv_sems, gather):
    x, y, c = lax.axis_index("x"), lax.axis_index("y"), lax.axis_index("c")
    me = 4 * x + 2 * y + c
    copies = []
    for a, src, land in zip(which, ins, lands):
        for d in range(1, N_DEV):
            px, py, pc = x ^ (d >> 2), y ^ ((d >> 1) & 1), c ^ (d & 1)
            copies.append(pltpu.make_async_remote_copy(
                src_ref=src if gather[a] else src.at[4 * px + 2 * py + pc], dst_ref=land.at[me],
                send_sem=send_sems.at[7 * a + d - 1], recv_sem=recv_sems.at[7 * a + d - 1],
                device_id=(px, py, pc), device_id_type=pl.DeviceIdType.MESH))
    return copies


def _own_copies(which, ins, lands, own_sems, gather):
    me = 4 * lax.axis_index("x") + 2 * lax.axis_index("y") + lax.axis_index("c")
    return [pltpu.make_async_copy(src if gather[a] else src.at[me], land.at[me], own_sems.at[a])
            for a, src, land in zip(which, ins, lands)]


def _send_start(srcs, gather, name):
    na = len(srcs)
    assert len(gather) == na
    land_types = [pltpu.HBM(((N_DEV,) + s.shape) if g else s.shape, s.dtype) for s, g in zip(srcs, gather)]

    def body(*refs):
        ins, lands = refs[:na], refs[na:2 * na]
        send_sems, recv_sems, own_sems, token = refs[2 * na], refs[2 * na + 1], refs[2 * na + 2], refs[-1]
        for cp in _split_copies(range(na), ins, lands, send_sems, recv_sems, gather):
            cp.start()
        for cp in _own_copies(range(na), ins, lands, own_sems, gather):
            cp.start()
        token[...] = jnp.zeros_like(token)

    hbm = lambda a: pltpu.with_memory_space_constraint(a, pltpu.HBM)
    outs = pl.pallas_call(
        body, name=name,
        out_shape=(pltpu.SemaphoreType.DMA((7 * na,)), pltpu.SemaphoreType.DMA((7 * na,)), pltpu.SemaphoreType.DMA((na,)),
                   *[pltpu.HBM(s.shape, s.dtype) for s in srcs], *land_types, jax.ShapeDtypeStruct((8, LANES), F32)),
        in_specs=[_HBM_SPEC] * (2 * na),
        out_specs=(_SEM_SPEC, _SEM_SPEC, _SEM_SPEC, *[_HBM_SPEC] * (2 * na), pl.BlockSpec(memory_space=pltpu.VMEM)),
        input_output_aliases={i: 3 + i for i in range(2 * na)}, compiler_params=_DATAFLOW,
    )(*[hbm(s) for s in srcs], *[hbm(lax.empty(t.shape, t.dtype)) for t in land_types])
    return dict(send=outs[0], recv=outs[1], own=outs[2], srcs=list(outs[3:3 + na]), lands=list(outs[3 + na:3 + 2 * na]),
                token=outs[-1], gather=gather)


def _send_wait(handle, after, name, only=None):
    which = list(range(len(handle["srcs"]))) if only is None else list(only)
    na = len(which)
    gather = handle["gather"]
    after = list(after)

    def body(*refs):
        ins, lands = refs[:na], refs[na:2 * na]
        send_sems, recv_sems, own_sems = refs[2 * na], refs[2 * na + 1], refs[2 * na + 2]
        for cp in _split_copies(which, ins, lands, send_sems, recv_sems, gather):
            cp.wait_send()
            cp.wait_recv()
        for cp in _own_copies(which, ins, lands, own_sems, gather):
            cp.wait()

    both = [handle["srcs"][a] for a in which] + [handle["lands"][a] for a in which]
    outs = pl.pallas_call(
        body, name=name, out_shape=[pltpu.HBM(a.shape, a.dtype) for a in both],
        in_specs=[_HBM_SPEC] * (2 * na) + [_SEM_SPEC] * 3 + [_ANY_SPEC] * len(after),
        out_specs=[_HBM_SPEC] * (2 * na), input_output_aliases={i: i for i in range(2 * na)}, compiler_params=_DATAFLOW,
    )(*both, handle["send"], handle["recv"], handle["own"], *after)
    return list(outs[na:])


def _in_hbm(a):
    return pltpu.with_memory_space_constraint(a, pltpu.HBM)


def _sum_slots(p_ref):
    g = p_ref[0].astype(F32)
    for s in range(1, p_ref.shape[0]):
        g = g + p_ref[s].astype(F32)
    return g


def _adamw_refs(g, w_ref, m_ref, v_ref, g_out, d_out, m_out, v_out):
    c1 = 1.0 - ADAM_B1 ** ADAM_STEP
    c2 = 1.0 - ADAM_B2 ** ADAM_STEP
    m_new = ADAM_B1 * m_ref[...] + (1.0 - ADAM_B1) * g
    v_new = ADAM_B2 * v_ref[...] + (1.0 - ADAM_B2) * (g * g)
    g_out[...] = g
    m_out[...] = m_new
    v_out[...] = v_new
    d_out[...] = -ADAM_LR * ((m_new / c1) / (jnp.sqrt(v_new / c2) + ADAM_EPS) + ADAM_WD * w_ref[...])


def _adamw(parts, w, m, v, name):
    npart, r, n = parts.shape
    tr = r if r <= 256 else max(t for t in range(16, 257, 16) if r % t == 0)

    def body(p_ref, w_ref, m_ref, v_ref, g_out, d_out, m_out, v_out):
        _adamw_refs(_sum_slots(p_ref), w_ref, m_ref, v_ref, g_out, d_out, m_out, v_out)

    blk = pl.BlockSpec((tr, n), lambda i: (i, 0))
    return pl.pallas_call(
        body, name=name, grid=(r // tr,), in_specs=[pl.BlockSpec((npart, tr, n), lambda i: (0, i, 0)), blk, blk, blk],
        out_specs=[blk] * 4, out_shape=[pltpu.HBM((r, n), F32)] * 4, compiler_params=_params("parallel"),
    )(*map(_in_hbm, (parts, w, m, v)))


def _adamw_small(parts, ws, ms, vs, loss_parts, name):
    n = len(parts)

    def body(*refs):
        ins, outs = refs[:4 * n + 1], refs[4 * n + 1:]
        for i in range(n):
            _adamw_refs(_sum_slots(ins[i]), ins[n + i], ins[2 * n + i], ins[3 * n + i], *outs[4 * i:4 * i + 4])
        outs[4 * n][...] = _sum_slots(ins[4 * n])

    out_shape = [pltpu.HBM(w.shape, F32) for w in ws for _ in range(4)]
    res = pl.pallas_call(body, name=name, out_shape=out_shape + [pltpu.HBM((1, LANES), F32)],
                         compiler_params=_params())(*map(_in_hbm, (*parts, *ws, *ms, *vs, loss_parts)))
    return [res[4 * i:4 * i + 4] for i in range(n)], res[4 * n]


REPLICATED = ("g_cq", "g_ckv", "w_uk", "w_uv", "ln1_g", "ln1_b", "conv_b", "ln2_g", "ln2_b")
ALL_WEIGHTS = ("w_in", "g_cq", "g_ckv", "w_uq", "w_uk", "w_uv", "w_o", "ln1_g", "ln1_b", "w_up", "conv_w", "conv_b",
               "w_down", "ln2_g", "ln2_b")


def kernel(x, w_in, g_cq, g_ckv, w_uq, w_uk, w_uv, w_o, ln1_g, ln1_b, w_up, conv_w, conv_b, w_down, ln2_g, ln2_b, loss_target, m_w_in, m_g_cq, m_g_ckv, m_w_uq, m_w_uk, m_w_uv, m_w_o, m_ln1_g, m_ln1_b, m_w_up, m_conv_w, m_conv_b, m_w_down, m_ln2_g, m_ln2_b, v_w_in, v_g_cq, v_g_ckv, v_w_uq, v_w_uk, v_w_uv, v_w_o, v_ln1_g, v_ln1_b, v_w_up, v_conv_w, v_conv_b, v_w_down, v_ln2_g, v_ln2_b):
    w = dict(w_in=w_in, g_cq=g_cq, g_ckv=g_ckv, w_uq=w_uq, w_uk=w_uk, w_uv=w_uv, w_o=w_o, ln1_g=ln1_g, ln1_b=ln1_b,
             w_up=w_up, conv_w=conv_w, conv_b=conv_b, w_down=w_down, ln2_g=ln2_g, ln2_b=ln2_b)
    m = dict(w_in=m_w_in, g_cq=m_g_cq, g_ckv=m_g_ckv, w_uq=m_w_uq, w_uk=m_w_uk, w_uv=m_w_uv, w_o=m_w_o, ln1_g=m_ln1_g,
             ln1_b=m_ln1_b, w_up=m_w_up, conv_w=m_conv_w, conv_b=m_conv_b, w_down=m_w_down, ln2_g=m_ln2_g, ln2_b=m_ln2_b)
    v = dict(w_in=v_w_in, g_cq=v_g_cq, g_ckv=v_g_ckv, w_uq=v_w_uq, w_uk=v_w_uk, w_uv=v_w_uv, w_o=v_w_o, ln1_g=v_ln1_g,
             ln1_b=v_ln1_b, w_up=v_w_up, conv_w=v_conv_w, conv_b=v_conv_b, w_down=v_w_down, ln2_g=v_ln2_g, ln2_b=v_ln2_b)
    me = 4 * lax.axis_index("x") + 2 * lax.axis_index("y") + lax.axis_index("c")
    wire = lambda a: a.astype(WIRE_DTYPE)
    pad_taps = lambda a: jnp.pad(a, ((0, 8 - a.shape[0]), (0, 0)))

    blocks = lambda a: wire(a).reshape((N_DEV, a.shape[0] // N_DEV) + a.shape[1:])

    early = _send_start([wire(w_in).T, _heads_major(wire(w_uq)), pad_taps(conv_w)], [True] * 3, "gather_early_start")
    late_shards = [wire(w_o), wire(w_up).T, wire(w_down)]
    wk, wv = _pad_heads(w_uk, NOPE_DIM), _pad_heads(w_uv, HEAD_DIM)
    g_in, g_uq, g_conv = _send_wait(early, late_shards + [wk, wv], "gather_early_wait")
    late = _send_start(late_shards, [True] * 3, "gather_late_start")
    r_uq_dev, e_uq = w_uq.shape[0], w_uq.shape[2]
    wq = jnp.transpose(g_uq.reshape(N_DEV, HEADS, r_uq_dev, e_uq), (1, 0, 2, 3)).reshape(HEADS, Q_RANK, e_uq)
    cw = dict(
        w_in_t=_split_pad_rows(g_in.reshape(-1, D_MODEL)).astype(MXU_DTYPE),
        wq=jnp.pad(wq, ((0, 0), (0, 0), (0, LANES - e_uq))).astype(MXU_DTYPE),
        wk=wk, wv=wv,
        conv_w=_ffn_interleave(jnp.transpose(g_conv[:, :conv_w.shape[0]], (1, 0, 2)).reshape(conv_w.shape[0], -1), 1),
        g_cq=_row(g_cq), g_ckv=_row(g_ckv), ln1_g=_row(ln1_g), ln1_b=_row(ln1_b), conv_b=_ffn_interleave(_row(conv_b), 1),
        ln2_g=_row(ln2_g), ln2_b=_row(ln2_b))

    def late_weights(stage, after):
        (got,) = _send_wait(late, [after], f"gather_{stage}_wait", only=[("w_o", "w_up", "w_down").index(stage)])
        full = got.reshape(-1, D_MODEL)
        if stage == "w_o":
            w_o_mla, w_o_dil = _pad_w_o(full)
            return dict(w_o_mla=w_o_mla, w_o_dil=w_o_dil)
        if stage == "w_up":
            return dict(w_up_t=_ffn_interleave(full, 0).astype(MXU_DTYPE))
        return dict(w_down=full.astype(MXU_DTYPE))

    sent = {}

    def on_grads(stage, g):
        if stage == "ffn":
            sent[stage] = _send_start([blocks(_ffn_deinterleave(g["w_up_t"], 0)), blocks(g["w_down"])], [False] * 2,
                                      "exchange_ffn_start")
        elif stage == "w_o":
            return []
        elif stage == "mla":
            d_uq = wire(jnp.transpose(g["wq"][:, :, :e_uq].reshape(HEADS, N_DEV, r_uq_dev, e_uq), (1, 0, 2, 3))
                        ).reshape(N_DEV, HEADS * r_uq_dev, e_uq)
            dense = lambda a, width: wire(jnp.transpose(a[:, :, :width], (0, 2, 1))).reshape(-1, a.shape[1])
            small = dict(g_cq=g["g_cq"], g_ckv=g["g_ckv"], w_uk=dense(g["wk"], NOPE_DIM), w_uv=dense(g["wv"], HEAD_DIM),
                         ln1_g=g["ln1_g"], ln1_b=g["ln1_b"], conv_b=_ffn_deinterleave(g["conv_b"], 1), ln2_g=g["ln2_g"],
                         ln2_b=g["ln2_b"])
            everyone = [small[n] for n in REPLICATED] + [_ffn_deinterleave(g["conv_w"], 1), g["loss"]]
            sent[stage] = _send_start([blocks(_unpad_w_o(g["w_o_mla"], g["w_o_dil"])), d_uq] + everyone,
                                      [False] * 2 + [True] * len(everyone), "exchange_mla_start")
        else:
            sent[stage] = _send_start([blocks(_split_unpad_rows(g["w_in_t"]))], [False], "exchange_w_in_start")
        return [sent[stage]["token"]]

    _, grad_x, _ = _layer_grads(x[0], loss_target[0], cw, [late["token"]], late_weights, on_grads)


    out = {}

    def update(name, parts, view=None):
        to2d = {None: lambda a: a, "t": lambda a: a.T, "heads": _heads_major}[view]
        back = {None: lambda a: a, "t": lambda a: a.T, "heads": lambda a: _heads_minor(a, HEADS)}[view]
        res = _adamw(parts, to2d(w[name]), to2d(m[name]), to2d(v[name]), "adamw_" + name)
        for kind, a in zip(("grad", "delta", "new_m", "new_v"), res):
            out[kind, name] = back(a)
        return res[0]

    r_up, r_down = _send_wait(sent["ffn"], [grad_x], "exchange_ffn_wait")
    r_o, r_uq, *rep_all, cw_all, loss_all = _send_wait(sent["mla"], [grad_x], "exchange_mla_wait")
    done = [update("w_up", r_up, "t"), update("w_down", r_down), update("w_o", r_o), update("w_uq", r_uq, "heads")]
    rank_minor = ("w_uk", "w_uv")
    two_d = lambda n, a: jnp.transpose(a, (1, 2, 0)).reshape(-1, a.shape[0]) if n in rank_minor else a.reshape(1, -1)
    res, loss_sum = _adamw_small(rep_all, *[[two_d(n, d[n]) for n in REPLICATED] for d in (w, m, v)], loss_all, "adamw_replicated")
    for n, quad in zip(REPLICATED, res):
        for kind, a in zip(("grad", "delta", "new_m", "new_v"), quad):
            out[kind, n] = jnp.transpose(a.reshape(HEADS, -1, a.shape[1]), (2, 0, 1)) if n in rank_minor else a.reshape(w[n].shape)
    loss = loss_sum[0, 0]
    ncw = conv_w.shape[1]
    done += [loss_sum, update("conv_w", lax.dynamic_slice_in_dim(cw_all[:, :conv_w.shape[0]], me * ncw, ncw, axis=2))]
    (r_in,) = _send_wait(sent["w_in"], done, "exchange_w_in_wait")
    update("w_in", r_in, "t")

    return (loss, grad_x[None], *[out[kind, n] for kind in ("grad", "delta", "new_m", "new_v") for n in ALL_WEIGHTS])
```

```python
import math

import jax
import jax.numpy as jnp
import numpy as np
from jax import lax
from jax.experimental import pallas as pl
from jax.experimental.pallas import tpu as pltpu

F32 = jnp.float32
MXU_DTYPE = jnp.bfloat16
WIRE_DTYPE = jnp.bfloat16

N_DEV = 8
D_MODEL = 1024
HEADS = 8
HEAD_DIM = 64
LANES = 128
Q_RANK, KV_RANK, ROPE_DIM, NOPE_DIM = 256, 128, 32, 64
DIL_WIDTH = HEADS * HEAD_DIM
MLA_WIDTH = HEADS * HEAD_DIM
IN_PAD = 2048
DH_PART = IN_PAD // 4
D_FF = 2816
ROPE_THETA = 10000.0
DIL_PAIRS = ((128, 1), (512, 4), (2048, 16))
DIL_BLOCK = 128
DN_ALPHA = 2.0 ** 0.25
LN_EPS = 1e-5
RMS_EPS = 1e-6
ONES_LANE = HEAD_DIM
MLA_SCALE = 1.0 / math.sqrt(NOPE_DIM + ROPE_DIM)
MLA_SCALE_LOG2 = MLA_SCALE * math.log2(math.e)
MLA_BWD_SPLITS = 2
MLA_BWD_SPLITS_DIAGONAL = 4
MLA_FWD_SPLITS_DIAGONAL = 2
DIL_SCALE = 1.0 / math.sqrt(HEAD_DIM)
ALIBI_SLOPES = tuple(2.0 ** (-8.0 * (h + 1) / HEADS) for h in range(HEADS))
NEG_BIG = -1e30
ADAM_LR, ADAM_B1, ADAM_B2, ADAM_EPS, ADAM_WD, ADAM_STEP = 0.001, 0.9, 0.999, 1e-08, 0.01, 10
VMEM_LIMIT = 48 * 1024 * 1024


def _params(*sem):
    return pltpu.CompilerParams(dimension_semantics=sem or None, vmem_limit_bytes=VMEM_LIMIT)


def _dot(a, b, ca, cb):
    return lax.dot_general(a, b, (((ca,), (cb,)), ((), ())), preferred_element_type=F32)


_ANY_SPEC = pl.BlockSpec(memory_space=pl.ANY)


def _mm(a, b, *, name, tm, tn, tk, ta=False, tb=False, out_dtype=F32, after=()):
    m, k = (a.shape[1], a.shape[0]) if ta else a.shape
    n = b.shape[0] if tb else b.shape[1]
    assert (b.shape[1] if tb else b.shape[0]) == k
    tm, tn, tk = min(tm, m), min(tn, n), min(tk, k)
    assert m % tm == 0 and n % tn == 0 and k % tk == 0, (name, m, n, k, tm, tn, tk)
    nk = k // tk
    a_spec = (pl.BlockSpec((tk, tm), lambda i, j, kk: (kk, i)) if ta
              else pl.BlockSpec((tm, tk), lambda i, j, kk: (i, kk)))
    b_mode = dict(pipeline_mode=pl.Buffered(1)) if (tn == n and tk == k) else {}
    b_spec = (pl.BlockSpec((tn, tk), lambda i, j, kk: (j, kk), **b_mode) if tb
              else pl.BlockSpec((tk, tn), lambda i, j, kk: (kk, j), **b_mode))
    o_spec = pl.BlockSpec((tm, tn), lambda i, j, kk: (i, j))
    n_in = 2 + len(after)
    ca, cb = (0 if ta else 1), (1 if tb else 0)

    def body(*refs):
        a_ref, b_ref, o_ref = refs[0], refs[1], refs[n_in]
        part = _dot(a_ref[...].astype(MXU_DTYPE), b_ref[...].astype(MXU_DTYPE), ca, cb)
        if nk == 1:
            o_ref[...] = part.astype(o_ref.dtype)
            return
        acc_ref = refs[-1]
        kk = pl.program_id(2)

        @pl.when(kk == 0)
        def _():
            acc_ref[...] = part

        @pl.when(kk > 0)
        def _():
            acc_ref[...] += part

        @pl.when(kk == nk - 1)
        def _():
            o_ref[...] = acc_ref[...].astype(o_ref.dtype)

    return pl.pallas_call(
        body, name=name, grid=(m // tm, n // tn, nk), in_specs=[a_spec, b_spec] + [_ANY_SPEC] * len(after), out_specs=o_spec,
        out_shape=pltpu.HBM((m, n), out_dtype),
        scratch_shapes=[pltpu.VMEM((tm, tn), F32)] if nk > 1 else [],
        compiler_params=_params("parallel", "parallel", "arbitrary"),
    )(a, b, *after)


def _w_o_bwd(dz, o_mla, o_dil, w_o_mla, w_o_dil, tm=1024):
    seq, d = dz.shape
    tm = min(tm, seq)
    nstep = seq // tm

    def body(a_ref, om_ref, od_ref, wm_ref, wd_ref, dom_ref, dod_ref, dwm_ref, dwd_ref, accm_ref, accd_ref):
        step = pl.program_id(0)

        @pl.when(step == 0)
        def _():
            accm_ref[...] = jnp.zeros_like(accm_ref)
            accd_ref[...] = jnp.zeros_like(accd_ref)

        a = a_ref[...]
        for hd in range(HEADS):
            rows = slice(LANES * hd, LANES * (hd + 1))
            dom_ref[hd] = _dot(a, wm_ref[rows, :], 1, 1)
            accm_ref[rows, :] += _dot(om_ref[hd], a, 0, 0)
        dod_ref[...] = _dot(a, wd_ref[...], 1, 1)
        accd_ref[...] += _dot(od_ref[...], a, 0, 0)

        @pl.when(step == nstep - 1)
        def _():
            dwm_ref[...] = accm_ref[...].astype(dwm_ref.dtype)
            dwd_ref[...] = accd_ref[...].astype(dwd_ref.dtype)

    once = dict(pipeline_mode=pl.Buffered(1))
    return pl.pallas_call(
        body, name="w_o_bwd", grid=(nstep,),
        in_specs=[pl.BlockSpec((tm, d), lambda i: (i, 0)), pl.BlockSpec((HEADS, tm, LANES), lambda i: (0, i, 0)),
                  pl.BlockSpec((tm, DIL_WIDTH), lambda i: (i, 0)), pl.BlockSpec((HEADS * LANES, d), lambda i: (0, 0), **once),
                  pl.BlockSpec((DIL_WIDTH, d), lambda i: (0, 0), **once)],
        out_specs=[pl.BlockSpec((HEADS, tm, LANES), lambda i: (0, i, 0)), pl.BlockSpec((tm, DIL_WIDTH), lambda i: (i, 0)),
                   pl.BlockSpec((HEADS * LANES, d), lambda i: (0, 0)), pl.BlockSpec((DIL_WIDTH, d), lambda i: (0, 0))],
        out_shape=[jax.ShapeDtypeStruct((HEADS, seq, LANES), F32), jax.ShapeDtypeStruct((seq, DIL_WIDTH), F32),
                   jax.ShapeDtypeStruct((HEADS * LANES, d), MXU_DTYPE), jax.ShapeDtypeStruct((DIL_WIDTH, d), MXU_DTYPE)],
        scratch_shapes=[pltpu.VMEM((HEADS * LANES, d), F32), pltpu.VMEM((DIL_WIDTH, d), F32)],
        compiler_params=_params("arbitrary"),
    )(dz, o_mla, o_dil, w_o_mla, w_o_dil)


def _rope_tables(seq):
    half = ROPE_DIM // 2
    f32 = np.float32
    freqs = np.power(f32(ROPE_THETA), -np.arange(half, dtype=f32) / f32(half))
    ang = np.arange(seq, dtype=f32)[:, None] * freqs[None, :]
    cos, sin = np.cos(ang, dtype=f32), np.sin(ang, dtype=f32)
    one = np.ones((seq, NOPE_DIM), f32)
    tail = np.ones((seq, LANES - NOPE_DIM - ROPE_DIM), f32)
    ctab = np.concatenate([one, cos, cos, tail], axis=1)
    stab = np.concatenate([0 * one, -sin, sin, 0 * tail], axis=1)
    return jnp.asarray(ctab), jnp.asarray(stab)


def _rope_swap(t):
    lane = lax.broadcasted_iota(jnp.int32, t.shape, 1)
    half = ROPE_DIM // 2
    return jnp.where(lane < NOPE_DIM + half, pltpu.roll(t, LANES - half, 1), pltpu.roll(t, half, 1))


def _rope(t, ctab, stab):
    return t * ctab + _rope_swap(t) * stab


def _rope_inv(t, ctab, stab):
    return t * ctab - _rope_swap(t) * stab


def _rms(x, g):
    r = lax.rsqrt(jnp.mean(x * x, axis=-1, keepdims=True) + RMS_EPS)
    xh = x * r
    return xh, r, xh * g


def _mla_prep(h, g_cq, g_ckv, wq, wk, wv, ctab, stab, tm=512):
    seq = h.shape[0]
    tm = min(tm, seq)

    def body(h_ref, gq_ref, gk_ref, wq_ref, wk_ref, wv_ref, c_ref, s_ref, q_out, k_out, v_out):
        hb = h_ref[...]
        ctab_, stab_ = c_ref[...], s_ref[...]
        _, _, cqn = _rms(hb[:, :Q_RANK], gq_ref[...])
        _, _, ckn = _rms(hb[:, Q_RANK:Q_RANK + KV_RANK], gk_ref[...])
        cqn = cqn.astype(MXU_DTYPE)
        ckn = ckn.astype(MXU_DTYPE)
        krr = _rope(hb[:, Q_RANK + KV_RANK:], ctab_, stab_)
        ones_lane = (lax.broadcasted_iota(jnp.int32, (1, LANES), 1) == ONES_LANE).astype(F32)
        for hd in range(HEADS):
            q = _dot(cqn, wq_ref[hd], 1, 0)
            q_out[hd] = _rope(q, ctab_, stab_).astype(q_out.dtype)
            k_out[hd] = (_dot(ckn, wk_ref[hd], 1, 0) + krr).astype(k_out.dtype)
            v_out[hd] = (_dot(ckn, wv_ref[hd], 1, 0) + ones_lane).astype(v_out.dtype)

    full = lambda *shape: pl.BlockSpec(shape, lambda i: (0,) * len(shape))
    slab = pl.BlockSpec((HEADS, tm, LANES), lambda i: (0, i, 0))
    shp = jax.ShapeDtypeStruct((HEADS, seq, LANES), MXU_DTYPE)
    return pl.pallas_call(
        body, name="mla_prep", grid=(seq // tm,),
        in_specs=[pl.BlockSpec((tm, DH_PART), lambda i: (i, 0)), full(1, Q_RANK), full(1, KV_RANK),
                  full(HEADS, Q_RANK, LANES), full(HEADS, KV_RANK, LANES), full(HEADS, KV_RANK, LANES),
                  pl.BlockSpec((tm, LANES), lambda i: (i, 0)), pl.BlockSpec((tm, LANES), lambda i: (i, 0))],
        out_specs=[slab, slab, slab], out_shape=[shp, shp, shp],
        compiler_params=_params("parallel"),
    )(h, g_cq, g_ckv, wq, wk, wv, ctab, stab)


def _mla_prep_bwd(h, g_cq, g_ckv, wq, wk, wv, ctab, stab, dq, dk, dv, tm=512, after=()):
    seq = h.shape[0]
    tm = min(tm, seq)
    n_after = len(after)

    def body(h_ref, gq_ref, gk_ref, wq_ref, wk_ref, wv_ref, c_ref, s_ref, dq_ref, dk_ref, dv_ref, *rest):
        dh_ref, dwq_ref, dwk_ref, dwv_ref, dgq_ref, dgk_ref = rest[n_after:]

        @pl.when(pl.program_id(0) == 0)
        def _():
            for r in (dwq_ref, dwk_ref, dwv_ref, dgq_ref, dgk_ref):
                r[...] = jnp.zeros_like(r)

        hb = h_ref[...]
        ctab_, stab_ = c_ref[...], s_ref[...]
        gq, gk = gq_ref[...], gk_ref[...]
        xq, rq, cqn = _rms(hb[:, :Q_RANK], gq)
        xk, rk, ckn = _rms(hb[:, Q_RANK:Q_RANK + KV_RANK], gk)
        cqn = cqn.astype(MXU_DTYPE)
        ckn = ckn.astype(MXU_DTYPE)
        d_cqn = jnp.zeros((tm, Q_RANK), F32)
        d_ckn = jnp.zeros((tm, KV_RANK), F32)
        d_krr = jnp.zeros((tm, LANES), F32)
        for hd in range(HEADS):
            dqh = _rope_inv(dq_ref[hd], ctab_, stab_).astype(MXU_DTYPE)
            d_cqn += _dot(dqh, wq_ref[hd], 1, 1)
            dwq_ref[hd] += _dot(cqn, dqh, 0, 0)
            dkh = dk_ref[hd]
            d_krr += dkh
            dkh = dkh.astype(MXU_DTYPE)
            d_ckn += _dot(dkh, wk_ref[hd], 1, 1)
            dwk_ref[hd] += _dot(ckn, dkh, 0, 0)
            dvh = dv_ref[hd].astype(MXU_DTYPE)
            d_ckn += _dot(dvh, wv_ref[hd], 1, 1)
            dwv_ref[hd] += _dot(ckn, dvh, 0, 0)
        lane = lax.broadcasted_iota(jnp.int32, (tm, LANES), 1)
        rot = (lane >= NOPE_DIM) & (lane < NOPE_DIM + ROPE_DIM)
        d_kr = jnp.where(rot, _rope_inv(jnp.where(rot, d_krr, 0.0), ctab_, stab_), 0.0)

        def rms_bwd(dy, xh, r, g, dg_ref):
            dg_ref[...] += jnp.sum(dy * xh, axis=0, keepdims=True)
            dxh = dy * g
            return r * (dxh - xh * jnp.mean(dxh * xh, axis=-1, keepdims=True))

        d_cq = rms_bwd(d_cqn, xq, rq, gq, dgq_ref)
        d_ck = rms_bwd(d_ckn, xk, rk, gk, dgk_ref)
        dh_ref[...] = jnp.concatenate([d_cq, d_ck, d_kr], axis=1).astype(dh_ref.dtype)

    full = lambda *shape: pl.BlockSpec(shape, lambda i: (0,) * len(shape))
    slab = pl.BlockSpec((HEADS, tm, LANES), lambda i: (0, i, 0))
    return pl.pallas_call(
        body, name="mla_prep_bwd", grid=(seq // tm,),
        in_specs=[pl.BlockSpec((tm, DH_PART), lambda i: (i, 0)), full(1, Q_RANK), full(1, KV_RANK),
                  full(HEADS, Q_RANK, LANES), full(HEADS, KV_RANK, LANES), full(HEADS, KV_RANK, LANES),
                  pl.BlockSpec((tm, LANES), lambda i: (i, 0)), pl.BlockSpec((tm, LANES), lambda i: (i, 0)),
                  slab, slab, slab] + [_ANY_SPEC] * n_after,
        out_specs=[pl.BlockSpec((tm, DH_PART), lambda i: (i, 0)), full(HEADS, Q_RANK, LANES), full(HEADS, KV_RANK, LANES),
                   full(HEADS, KV_RANK, LANES), full(1, Q_RANK), full(1, KV_RANK)],
        out_shape=[jax.ShapeDtypeStruct((seq, DH_PART), MXU_DTYPE), jax.ShapeDtypeStruct((HEADS, Q_RANK, LANES), F32),
                   jax.ShapeDtypeStruct((HEADS, KV_RANK, LANES), F32), jax.ShapeDtypeStruct((HEADS, KV_RANK, LANES), F32),
                   jax.ShapeDtypeStruct((1, Q_RANK), F32), jax.ShapeDtypeStruct((1, KV_RANK), F32)],
        compiler_params=_params("arbitrary"),
    )(h, g_cq, g_ckv, wq, wk, wv, ctab, stab, dq, dk, dv, *after)


def _mla_attn_fwd(q, k, v, t=1024):
    _, seq, _ = q.shape
    t = min(t, seq)

    def body(q_ref, k_ref, v_ref, o_ref, ob_ref, lse_ref, m_ref, acc_ref, s_ref):
        i = pl.program_id(1)
        qb = q_ref[...]
        m_ref[...] = jnp.full_like(m_ref, NEG_BIG)
        acc_ref[...] = jnp.zeros_like(acc_ref)

        def scores(j):
            return _dot(qb, k_ref[pl.ds(pl.multiple_of(j * t, t), t), :], 1, 1) * MLA_SCALE_LOG2

        def softmax_pv(j, s, rows=slice(None), mask=None):
            vb = v_ref[pl.ds(pl.multiple_of(j * t, t), s.shape[1]), :]
            if mask is not None:
                s = jnp.where(mask, s, NEG_BIG)
            m_old = m_ref[rows, :]
            m_new = jnp.maximum(m_old, jnp.max(s, axis=1, keepdims=True))
            p = jnp.exp2(s - m_new)
            a = jnp.exp2(m_old - m_new)
            acc_ref[rows, :] = a * acc_ref[rows, :] + _dot(p.astype(MXU_DTYPE), vb, 1, 0)
            m_ref[rows, :] = m_new

        def softmax_pv_diagonal(j):
            th = t // MLA_FWD_SPLITS_DIAGONAL
            for hf in range(MLA_FWD_SPLITS_DIAGONAL):
                nk = (hf + 1) * th
                row = lax.broadcasted_iota(jnp.int32, (th, nk), 0) + hf * th
                rows = slice(hf * th, (hf + 1) * th)
                softmax_pv(j, s_ref[rows, 0:nk], rows, row >= lax.broadcasted_iota(jnp.int32, (th, nk), 1))

        s_ref[...] = scores(0)

        def loop_body(j, c):
            s_next = scores(j + 1)
            softmax_pv(j, s_ref[...])
            s_ref[...] = s_next
            return c

        lax.fori_loop(0, i, loop_body, 0)
        softmax_pv_diagonal(i)
        acc = acc_ref[...]
        l = acc[:, ONES_LANE:ONES_LANE + 1]
        o = jnp.where(lax.broadcasted_iota(jnp.int32, acc.shape, 1) < HEAD_DIM, acc * (1.0 / l), 0.0)
        o_ref[...] = o
        ob_ref[...] = o.astype(ob_ref.dtype)
        lse_ref[...] = jnp.broadcast_to(m_ref[...] + jnp.log2(l), lse_ref.shape)

    blk = pl.BlockSpec((None, t, LANES), lambda h, i: (h, i, 0))
    whole = pl.BlockSpec((None, seq, LANES), lambda h, i: (h, 0, 0))
    shp = jax.ShapeDtypeStruct((HEADS, seq, LANES), F32)
    return pl.pallas_call(
        body, name="mla_attn_fwd", grid=(HEADS, seq // t),
        in_specs=[blk, whole, whole], out_specs=[blk, blk, blk],
        out_shape=[shp, jax.ShapeDtypeStruct((HEADS, seq, LANES), MXU_DTYPE), shp],
        scratch_shapes=[pltpu.VMEM((t, 1), F32), pltpu.VMEM((t, LANES), F32), pltpu.VMEM((t, t), F32)],
        compiler_params=_params("parallel", "arbitrary"),
    )(q, k, v)


def _mla_attn_bwd(q, k, v, o, lse, do, t=1024):
    _, seq, _ = q.shape
    t = min(t, seq)
    nb = seq // t

    def body(q_ref, k_ref, v_ref, o_ref, lse_ref, do_ref, dq_ref, dk_ref, dv_ref, dl_ref, dka_ref, dva_ref):
        dq_ref[...] = jnp.zeros_like(dq_ref)

        def delta_body(i, c):
            rows = pl.ds(pl.multiple_of(i * t, t), t)
            dl_ref[rows, :] = jnp.sum(do_ref[rows, :] * o_ref[rows, :], axis=1, keepdims=True)
            return c

        lax.fori_loop(0, nb, delta_body, 0)

        def kblock(j, c):
            krows = pl.ds(pl.multiple_of(j * t, t), t)
            kb = k_ref[krows, :]
            vb = v_ref[krows, :]
            dka_ref[...] = jnp.zeros_like(dka_ref)
            dva_ref[...] = jnp.zeros_like(dva_ref)

            def qstep(i, masked):
                ns = MLA_BWD_SPLITS_DIAGONAL if masked else MLA_BWD_SPLITS
                th = t // ns
                rows = [pl.ds(pl.multiple_of(i * t + hf * th, th), th) for hf in range(ns)]
                qs = [q_ref[r, :] for r in rows]
                dos = [do_ref[r, :].astype(MXU_DTYPE) for r in rows]
                nkeys = [(hf + 1) * th if masked else t for hf in range(ns)]
                ss = [_dot(qs[hf], kb[:nkeys[hf]], 1, 1) * MLA_SCALE_LOG2 for hf in range(ns)]
                dps = [_dot(dos[hf], vb[:nkeys[hf]], 1, 1) for hf in range(ns)]
                for hf in range(ns):
                    s, nk = ss[hf], nkeys[hf]
                    if masked:
                        row = lax.broadcasted_iota(jnp.int32, (th, nk), 0) + hf * th
                        s = jnp.where(row >= lax.broadcasted_iota(jnp.int32, (th, nk), 1), s, NEG_BIG)
                    p = jnp.exp2(s - lse_ref[rows[hf], 0:1])
                    dva_ref[0:nk, :] += _dot(p.astype(MXU_DTYPE), dos[hf], 0, 0)
                    ds = (p * (dps[hf] - dl_ref[rows[hf], :]) * MLA_SCALE).astype(MXU_DTYPE)
                    dka_ref[0:nk, :] += _dot(ds, qs[hf], 0, 0)
                    dq_ref[rows[hf], :] += _dot(ds, kb[:nk], 1, 0)

            qstep(j, True)

            def qloop(i, c2):
                qstep(i, False)
                return c2

            lax.fori_loop(j + 1, nb, qloop, 0)
            dk_ref[krows, :] = dka_ref[...]
            dv_ref[krows, :] = dva_ref[...]
            return c

        lax.fori_loop(0, nb, kblock, 0)

    whole = pl.BlockSpec((None, seq, LANES), lambda h: (h, 0, 0))
    shp = jax.ShapeDtypeStruct((HEADS, seq, LANES), F32)
    return pl.pallas_call(
        body, name="mla_attn_bwd", grid=(HEADS,),
        in_specs=[whole] * 6, out_specs=[whole] * 3, out_shape=[shp] * 3,
        scratch_shapes=[pltpu.VMEM((seq, 1), F32), pltpu.VMEM((t, LANES), F32), pltpu.VMEM((t, LANES), F32)],
        compiler_params=_params("parallel"),
    )(q, k, v, o, lse, do)


DIL_CHUNK = DIL_BLOCK * max(d for _, d in DIL_PAIRS)
DIL_PAIR_LANES = 2 * HEAD_DIM
assert DIL_PAIR_LANES == LANES
DIL_UNROLL_FWD = 16
DIL_UNROLL_BWD = 16


def _dil_bias_tables(hp, dil):
    b = DIL_BLOCK
    iq = lax.broadcasted_iota(jnp.int32, (b, 2 * b), 0)
    ik = lax.broadcasted_iota(jnp.int32, (b, 2 * b), 1)
    off = iq + b - ik
    band = (off >= 0) & (off <= b)
    dist = (off * dil).astype(F32)
    every, first = [], []
    for hh in range(2):
        slope = jnp.where(hp == 0, ALIBI_SLOPES[hh], jnp.where(hp == 1, ALIBI_SLOPES[2 + hh],
                          jnp.where(hp == 2, ALIBI_SLOPES[4 + hh], ALIBI_SLOPES[6 + hh]))).astype(F32)
        bias = -slope * dist
        every.append(jnp.where(band, bias, NEG_BIG))
        first.append(jnp.where(band & (ik >= b), bias, NEG_BIG))
    return jnp.concatenate(every, axis=0), jnp.concatenate(first, axis=0)


def _dil_rows(start, dil):
    return pl.ds(start, DIL_BLOCK) if dil == 1 else pl.ds(start, DIL_BLOCK, stride=dil)


def _dil_block_pos(blk, c, dil):
    sc, r = blk // dil, blk % dil
    q0 = sc * (DIL_BLOCK * dil) + r
    kcur0 = c * DIL_CHUNK + q0
    first = kcur0 < DIL_BLOCK * dil
    kprev0 = jnp.where(first, kcur0, kcur0 - DIL_BLOCK * dil)
    return q0, kcur0, kprev0, first


def _pair_cols(hh):
    return slice(HEAD_DIM * hh, HEAD_DIM * (hh + 1))


def _first_head_lanes(shape):
    return lax.broadcasted_iota(jnp.int32, shape, 1) < HEAD_DIM


def _stack_pair(t):
    first = _first_head_lanes(t.shape)
    return jnp.concatenate([jnp.where(first, t, 0.0), jnp.where(first, 0.0, t)], axis=0).astype(MXU_DTYPE)


def _unstack_pair(t):
    rows = t.shape[0] // 2
    return jnp.where(_first_head_lanes((rows, t.shape[1])), t[:rows], t[rows:])


def _pair_column(t):
    return jnp.concatenate([t[:, 0:1], t[:, HEAD_DIM:HEAD_DIM + 1]], axis=0)


def _dil_fwd(h):
    seq = h.shape[0]
    assert seq % DIL_CHUNK == 0
    nblk = DIL_CHUNK // DIL_BLOCK
    rc = 256

    def body(q_ref, k_ref, v_ref, o_ref, ob_ref, lse_ref, *scr):
        o_scr, l_scr = scr[:3], scr[3:]
        hp, c = pl.program_id(0), pl.program_id(1)
        for bi, (_, dil) in enumerate(DIL_PAIRS):
            tables = _dil_bias_tables(hp, dil)

            def block(blk, carry, bi=bi, dil=dil, tables=tables):
                q0, kcur0, kprev0, first = _dil_block_pos(blk, c, dil)
                q2 = _stack_pair(q_ref[_dil_rows(q0, dil), :] * DIL_SCALE)
                kcat = jnp.concatenate([k_ref[_dil_rows(kprev0, dil), :], k_ref[_dil_rows(kcur0, dil), :]], axis=0).astype(MXU_DTYPE)
                vcat = jnp.concatenate([v_ref[_dil_rows(kprev0, dil), :], v_ref[_dil_rows(kcur0, dil), :]], axis=0).astype(MXU_DTYPE)
                s = _dot(q2, kcat, 1, 1) + jnp.where(first, tables[1], tables[0])
                mx = jnp.max(s, axis=1, keepdims=True)
                p = jnp.exp(s - mx)
                l = jnp.sum(p, axis=1, keepdims=True)
                o_scr[bi][_dil_rows(q0, dil), :] = _unstack_pair(_dot(p.astype(MXU_DTYPE), vcat, 1, 0) * (1.0 / l))
                l_scr[bi][_dil_rows(q0, dil), :] = _unstack_pair(jnp.broadcast_to(mx + jnp.log(l), (2 * DIL_BLOCK, LANES)))
                return carry

            lax.fori_loop(0, nblk, block, 0, unroll=DIL_UNROLL_FWD)

        def combine(i, carry):
            rows = pl.ds(pl.multiple_of(i * rc, rc), rc)
            ls = [l_scr[bi][rows, :] for bi in range(3)]
            mx = jnp.maximum(jnp.maximum(ls[0], ls[1]), ls[2])
            es = [jnp.exp(l - mx) for l in ls]
            den = es[0] + es[1] + es[2]
            o = (es[0] * o_scr[0][rows, :] + es[1] * o_scr[1][rows, :] + es[2] * o_scr[2][rows, :]) / den
            o_ref[rows, :] = o
            ob_ref[rows, :] = o.astype(ob_ref.dtype)
            lse_ref[rows, :] = mx + jnp.log(den)
            return carry

        lax.fori_loop(0, DIL_CHUNK // rc, combine, 0)

    nq = DIL_WIDTH // LANES
    chunk = lambda off: pl.BlockSpec((DIL_CHUNK, LANES), lambda hp, c: (c, off + hp))
    whole = lambda off: pl.BlockSpec((seq, LANES), lambda hp, c: (0, off + hp))
    shp = jax.ShapeDtypeStruct((seq, DIL_WIDTH), F32)
    return pl.pallas_call(
        body, name="dil_fwd", grid=(nq, seq // DIL_CHUNK),
        in_specs=[chunk(nq), whole(2 * nq), whole(3 * nq)], out_specs=[chunk(0), chunk(0), chunk(0)],
        out_shape=[shp, jax.ShapeDtypeStruct((seq, DIL_WIDTH), MXU_DTYPE), shp],
        scratch_shapes=[pltpu.VMEM((DIL_CHUNK, LANES), F32)] * 6,
        compiler_params=_params("parallel", "arbitrary"),
    )(h, h, h)


def _dil_bwd(h, o, lse, do, after=()):
    seq = h.shape[0]
    nblk = DIL_CHUNK // DIL_BLOCK
    nchunk = seq // DIL_CHUNK
    rc = 256

    n_after = len(after)

    def body(q_ref, k_ref, v_ref, o_ref, lse_ref, do_ref, *rest):
        dq_out, dk_out, dv_out, dl_scr, dq_ref, dk_ref, dv_ref = rest[n_after:]
        hp, c = pl.program_id(0), pl.program_id(1)

        @pl.when(c == 0)
        def _():
            dk_ref[...] = jnp.zeros_like(dk_ref)
            dv_ref[...] = jnp.zeros_like(dv_ref)

        def delta(i, carry):
            rows = pl.ds(pl.multiple_of(i * rc, rc), rc)
            prod = do_ref[rows, :] * o_ref[rows, :]
            dl_scr[rows, :] = jnp.concatenate(
                [jnp.broadcast_to(jnp.sum(prod[:, _pair_cols(hh)], axis=1, keepdims=True), (rc, HEAD_DIM)) for hh in range(2)], axis=1)
            return carry

        lax.fori_loop(0, DIL_CHUNK // rc, delta, 0)

        for bi, (_, dil) in enumerate(DIL_PAIRS):
            tables = _dil_bias_tables(hp, dil)

            def block(blk, carry, bi=bi, dil=dil, tables=tables):
                q0, kcur0, kprev0, first = _dil_block_pos(blk, c, dil)
                qrows = _dil_rows(q0, dil)
                q2 = _stack_pair(q_ref[qrows, :] * DIL_SCALE)
                kcat = jnp.concatenate([k_ref[_dil_rows(kprev0, dil), :], k_ref[_dil_rows(kcur0, dil), :]], axis=0).astype(MXU_DTYPE)
                vcat = jnp.concatenate([v_ref[_dil_rows(kprev0, dil), :], v_ref[_dil_rows(kcur0, dil), :]], axis=0).astype(MXU_DTYPE)
                do2 = _stack_pair(do_ref[qrows, :])
                s = _dot(q2, kcat, 1, 1) + jnp.where(first, tables[1], tables[0])
                p = jnp.exp(s - _pair_column(lse_ref[qrows, :]))
                dp = _dot(do2, vcat, 1, 1)
                ds = (p * (dp - _pair_column(dl_scr[qrows, :]))).astype(MXU_DTYPE)
                dq_b = _unstack_pair(_dot(ds, kcat, 1, 0)) * DIL_SCALE
                dk_b = _dot(ds, q2, 0, 0)
                dv_b = _dot(p.astype(MXU_DTYPE), do2, 0, 0)
                if bi == 0:
                    dq_ref[qrows, :] = dq_b
                else:
                    dq_ref[qrows, :] += dq_b
                dk_ref[_dil_rows(kprev0, dil), :] += dk_b[:DIL_BLOCK]
                dv_ref[_dil_rows(kprev0, dil), :] += dv_b[:DIL_BLOCK]
                dk_ref[_dil_rows(kcur0, dil), :] += dk_b[DIL_BLOCK:]
                dv_ref[_dil_rows(kcur0, dil), :] += dv_b[DIL_BLOCK:]
                return carry

            lax.fori_loop(0, nblk, block, 0, unroll=DIL_UNROLL_BWD)

        dq_out[...] = dq_ref[...].astype(dq_out.dtype)

        @pl.when(c == nchunk - 1)
        def _():
            dk_out[...] = dk_ref[...].astype(dk_out.dtype)
            dv_out[...] = dv_ref[...].astype(dv_out.dtype)

    nq = DIL_WIDTH // LANES
    chunk = lambda off: pl.BlockSpec((DIL_CHUNK, LANES), lambda hp, c: (c, off + hp))
    whole = lambda off: pl.BlockSpec((seq, LANES), lambda hp, c: (0, off + hp))
    shp = jax.ShapeDtypeStruct((seq, DIL_WIDTH), MXU_DTYPE)
    return pl.pallas_call(
        body, name="dil_bwd", grid=(nq, nchunk),
        in_specs=[chunk(nq), whole(2 * nq), whole(3 * nq), chunk(0), chunk(0), chunk(0)] + [_ANY_SPEC] * n_after,
        out_specs=[chunk(0), whole(0), whole(0)], out_shape=[shp, shp, shp],
        scratch_shapes=[pltpu.VMEM((DIL_CHUNK, LANES), F32), pltpu.VMEM((DIL_CHUNK, LANES), F32),
                        pltpu.VMEM((seq, LANES), F32), pltpu.VMEM((seq, LANES), F32)],
        compiler_params=_params("parallel", "arbitrary"),
    )(h, h, h, o, lse, do, *after)


def _mm_dx0(parts, w_in_t, res, tm=1024, after=()):
    seq, d = res.shape
    tm = min(tm, seq)
    n_after = len(after)

    def body(a0, a1, a2, a3, b_ref, r_ref, *rest):
        o_ref = rest[n_after]
        acc = _dot(a0[...], b_ref[0:DH_PART, :], 1, 0)
        for c, a in enumerate((a1, a2, a3), start=1):
            acc += _dot(a[...], b_ref[DH_PART * c:DH_PART * (c + 1), :], 1, 0)
        o_ref[...] = acc + DN_ALPHA * r_ref[...]

    blk = pl.BlockSpec((tm, DH_PART), lambda i: (i, 0))
    row = pl.BlockSpec((tm, d), lambda i: (i, 0))
    return pl.pallas_call(
        body, name="mm_dx0", grid=(seq // tm,),
        in_specs=[blk] * 4 + [pl.BlockSpec((IN_PAD, d), lambda i: (0, 0), pipeline_mode=pl.Buffered(1)), row] + [_ANY_SPEC] * n_after,
        out_specs=row, out_shape=jax.ShapeDtypeStruct((seq, d), F32), compiler_params=_params("parallel"),
    )(*parts, w_in_t, res, *after)


def _mm_dw_in(parts, x0, tk=1024):
    seq, d = x0.shape
    tk = min(tk, seq)
    nk = seq // tk

    def body(a0, a1, a2, a3, b_ref, o_ref, acc_ref):
        kk = pl.program_id(0)

        @pl.when(kk == 0)
        def _():
            acc_ref[...] = jnp.zeros_like(acc_ref)

        b = b_ref[...].astype(MXU_DTYPE)
        for c, a in enumerate((a0, a1, a2, a3)):
            acc_ref[DH_PART * c:DH_PART * (c + 1), :] += _dot(a[...], b, 0, 0)

        @pl.when(kk == nk - 1)
        def _():
            o_ref[...] = acc_ref[...].astype(o_ref.dtype)

    blk = pl.BlockSpec((tk, DH_PART), lambda kk: (kk, 0))
    return pl.pallas_call(
        body, name="mm_dw_in", grid=(nk,), in_specs=[blk] * 4 + [pl.BlockSpec((tk, d), lambda kk: (kk, 0))],
        out_specs=pl.BlockSpec((IN_PAD, d), lambda kk: (0, 0)), out_shape=jax.ShapeDtypeStruct((IN_PAD, d), MXU_DTYPE),
        scratch_shapes=[pltpu.VMEM((IN_PAD, d), F32)], compiler_params=_params("arbitrary"),
    )(*parts, x0)


def _ln_stats(z):
    mu = jnp.mean(z, axis=-1, keepdims=True)
    zc = z - mu
    r = lax.rsqrt(jnp.mean(zc * zc, axis=-1, keepdims=True) + LN_EPS)
    return zc * r, r


def _ln_bwd_math(dy, xh, r, g):
    dxh = dy * g
    return r * (dxh - jnp.mean(dxh, axis=-1, keepdims=True) - xh * jnp.mean(dxh * xh, axis=-1, keepdims=True))


def _mix_ln1(o_mla, o_dil, w_o_mla, w_o_dil, x0, g, b, tm=512):
    seq, d = x0.shape
    tm = min(tm, seq)

    def body(om_ref, od_ref, wm_ref, wd_ref, x_ref, g_ref, b_ref, z_ref, y_ref, yb_ref):
        mix = _dot(od_ref[...], wd_ref[...], 1, 0)
        for hd in range(HEADS):
            mix += _dot(om_ref[hd], wm_ref[LANES * hd:LANES * (hd + 1), :], 1, 0)
        z = DN_ALPHA * x_ref[...] + mix
        xh, _ = _ln_stats(z)
        y = xh * g_ref[...] + b_ref[...]
        z_ref[...] = z
        y_ref[...] = y
        yb_ref[...] = y.astype(yb_ref.dtype)

    blk = pl.BlockSpec((tm, d), lambda i: (i, 0))
    vec = pl.BlockSpec((1, d), lambda i: (0, 0))
    shp = jax.ShapeDtypeStruct((seq, d), F32)
    return pl.pallas_call(
        body, name="mix_ln1", grid=(seq // tm,),
        in_specs=[pl.BlockSpec((HEADS, tm, LANES), lambda i: (0, i, 0)), pl.BlockSpec((tm, DIL_WIDTH), lambda i: (i, 0)),
                  pl.BlockSpec((HEADS * LANES, d), lambda i: (0, 0)), pl.BlockSpec((DIL_WIDTH, d), lambda i: (0, 0)), blk, vec, vec],
        out_specs=[blk, blk, blk], out_shape=[shp, shp, jax.ShapeDtypeStruct((seq, d), MXU_DTYPE)],
        compiler_params=_params("parallel"))(o_mla, o_dil, w_o_mla, w_o_dil, x0, g, b)


def _dx1_ln1_bwd(du, w_up_t, dz2, z, g, tm=256, after=()):
    seq, d = z.shape
    kdim = du.shape[1]
    tm = min(tm, seq)
    n_after = len(after)

    def body(du_ref, w_ref, r_ref, z_ref, g_ref, *rest):
        dz_ref, dzb_ref, dg_ref, db_ref = rest[n_after:]

        @pl.when(pl.program_id(0) == 0)
        def _():
            dg_ref[...] = jnp.zeros_like(dg_ref)
            db_ref[...] = jnp.zeros_like(db_ref)

        dyb = _dot(du_ref[...], w_ref[...], 1, 0) + DN_ALPHA * r_ref[...]
        xh, r = _ln_stats(z_ref[...])
        dg_ref[...] += jnp.sum(dyb * xh, axis=0, keepdims=True)
        db_ref[...] += jnp.sum(dyb, axis=0, keepdims=True)
        dz = _ln_bwd_math(dyb, xh, r, g_ref[...])
        dz_ref[...] = dz
        dzb_ref[...] = dz.astype(dzb_ref.dtype)

    blk = pl.BlockSpec((tm, d), lambda i: (i, 0))
    vec = pl.BlockSpec((1, d), lambda i: (0, 0))
    return pl.pallas_call(
        body, name="dx1_ln1_bwd", grid=(seq // tm,),
        in_specs=[pl.BlockSpec((tm, kdim), lambda i: (i, 0)),
                  pl.BlockSpec((kdim, d), lambda i: (0, 0), pipeline_mode=pl.Buffered(1)), blk, blk, vec] + [_ANY_SPEC] * n_after,
        out_specs=[blk, blk, vec, vec],
        out_shape=[jax.ShapeDtypeStruct((seq, d), F32), jax.ShapeDtypeStruct((seq, d), MXU_DTYPE),
                   jax.ShapeDtypeStruct((1, d), F32), jax.ShapeDtypeStruct((1, d), F32)],
        compiler_params=_params("arbitrary"))(du, w_up_t, dz2, z, g, *after)


def _down_ln2_loss_bwd(act, w_down, x1, target, g, b, tm=512):
    seq, d = x1.shape
    kdim = act.shape[1]
    tm = min(tm, seq)

    def body(a_ref, w_ref, x_ref, t_ref, g_ref, b_ref, dz_ref, dzb_ref, loss_ref, dg_ref, db_ref):
        @pl.when(pl.program_id(0) == 0)
        def _():
            loss_ref[...] = jnp.zeros_like(loss_ref)
            dg_ref[...] = jnp.zeros_like(dg_ref)
            db_ref[...] = jnp.zeros_like(db_ref)

        gv = g_ref[...]
        z = DN_ALPHA * x_ref[...] + _dot(a_ref[...], w_ref[...], 1, 0)
        xh, r = _ln_stats(z)
        err = (xh * gv + b_ref[...]) - t_ref[...]
        loss_ref[...] += 0.5 * jnp.sum(jnp.mean(err * err, axis=-1, keepdims=True), axis=0, keepdims=True)
        dy = err * (1.0 / d)
        dg_ref[...] += jnp.sum(dy * xh, axis=0, keepdims=True)
        db_ref[...] += jnp.sum(dy, axis=0, keepdims=True)
        dz = _ln_bwd_math(dy, xh, r, gv)
        dz_ref[...] = dz
        dzb_ref[...] = dz.astype(dzb_ref.dtype)

    blk = pl.BlockSpec((tm, d), lambda i: (i, 0))
    vec = pl.BlockSpec((1, d), lambda i: (0, 0))
    return pl.pallas_call(
        body, name="down_ln2_loss_bwd", grid=(seq // tm,),
        in_specs=[pl.BlockSpec((tm, kdim), lambda i: (i, 0)),
                  pl.BlockSpec((kdim, d), lambda i: (0, 0), pipeline_mode=pl.Buffered(1)), blk, blk, vec, vec],
        out_specs=[blk, blk, pl.BlockSpec((1, LANES), lambda i: (0, 0)), vec, vec],
        out_shape=[jax.ShapeDtypeStruct((seq, d), F32), jax.ShapeDtypeStruct((seq, d), MXU_DTYPE),
                   jax.ShapeDtypeStruct((1, LANES), F32),
                   jax.ShapeDtypeStruct((1, d), F32), jax.ShapeDtypeStruct((1, d), F32)],
        compiler_params=_params("arbitrary"))(act, w_down, x1, target, g, b)


HALO = 16


def _conv_rows(e, w_ref, b_ref):
    y = b_ref[...] + w_ref[0:1, :] * pltpu.roll(e, 2, 0)
    y = y + w_ref[1:2, :] * pltpu.roll(e, 1, 0)
    return y + w_ref[2:3, :] * e


_GELU_C = math.sqrt(2.0 / math.pi)
_GELU_A = 0.044715


def _gelu(x):
    return x * (0.5 + 0.5 * jnp.tanh(x * (_GELU_C + (_GELU_C * _GELU_A) * (x * x))))


CONV_TN = 256


def _ffn_interleave(a, axis):
    shp = a.shape
    a = a.reshape(shp[:axis] + (2, D_FF // CONV_TN, CONV_TN) + shp[axis + 1:])
    return jnp.swapaxes(a, axis, axis + 1).reshape(shp)


def _ffn_deinterleave(a, axis):
    shp = a.shape
    a = a.reshape(shp[:axis] + (D_FF // CONV_TN, 2, CONV_TN) + shp[axis + 1:])
    return jnp.swapaxes(a, axis, axis + 1).reshape(shp)


def _up_conv_gate_fwd(x1, w_up_t, conv_w, conv_b, tm=1024):
    seq, d = x1.shape
    tm = min(tm, seq)
    tn = CONV_TN

    def body(x_ref, xp_ref, wu_ref, w_ref, b_ref, u_ref, y_ref, o_ref):
        first = pl.program_id(0) == 0
        halo = jnp.where(first, jnp.zeros_like(xp_ref), xp_ref[...])
        e = _dot(jnp.concatenate([halo, x_ref[...]], axis=0), wu_ref[...], 1, 1)
        u_ref[...] = e[HALO:]
        y = _conv_rows(e, w_ref, b_ref)[HALO:]
        y_ref[...] = y
        o_ref[...] = (_gelu(y[:, tn:]) * y[:, :tn]).astype(o_ref.dtype)

    hb = tm // HALO
    return pl.pallas_call(
        body, name="up_conv_gate_fwd", grid=(seq // tm, D_FF // tn),
        in_specs=[pl.BlockSpec((tm, d), lambda i, j: (i, 0)),
                  pl.BlockSpec((HALO, d), lambda i, j: (jnp.maximum(i * hb - 1, 0), 0)),
                  pl.BlockSpec((2 * tn, d), lambda i, j: (j, 0)),
                  pl.BlockSpec((3, 2 * tn), lambda i, j: (0, j)), pl.BlockSpec((1, 2 * tn), lambda i, j: (0, j))],
        out_specs=[pl.BlockSpec((tm, 2 * tn), lambda i, j: (i, j)), pl.BlockSpec((tm, 2 * tn), lambda i, j: (i, j)),
                   pl.BlockSpec((tm, tn), lambda i, j: (i, j))],
        out_shape=[jax.ShapeDtypeStruct((seq, 2 * D_FF), F32), jax.ShapeDtypeStruct((seq, 2 * D_FF), F32),
                   jax.ShapeDtypeStruct((seq, D_FF), MXU_DTYPE)],
        compiler_params=_params("parallel", "arbitrary"),
    )(x1, x1, w_up_t, conv_w, conv_b)


def _conv_gate_bwd(u, y, d_act, conv_w, tm=1024):
    seq = u.shape[0]
    tm = min(tm, seq)
    tn = CONV_TN
    ni = seq // tm
    rows = tm + HALO

    def body(u_ref, y_ref, yn_ref, da_ref, dan_ref, w_ref, du_ref, dw_ref, db_ref):
        i = pl.program_id(1)
        last = i == ni - 1

        @pl.when(i == 0)
        def _():
            dw_ref[...] = jnp.zeros_like(dw_ref)
            db_ref[...] = jnp.zeros_like(db_ref)

        yy = jnp.concatenate([y_ref[...], yn_ref[...]], axis=0)
        ya, yg = yy[:, :tn], yy[:, tn:]
        dact = jnp.concatenate([da_ref[...].astype(F32), jnp.where(last, 0.0, dan_ref[...].astype(F32))], axis=0)
        yg2 = yg * yg
        th = jnp.tanh(yg * (_GELU_C + (_GELU_C * _GELU_A) * yg2))
        half = 0.5 + 0.5 * th
        gelu = yg * half
        gelu_grad = half + gelu * (1.0 - half) * (2.0 * _GELU_C + (6.0 * _GELU_C * _GELU_A) * yg2)
        dy = jnp.concatenate([dact * gelu, dact * ya * gelu_grad], axis=1)
        dy0, dy1, dy2 = dy[:tm], pltpu.roll(dy, rows - 1, 0)[:tm], pltpu.roll(dy, rows - 2, 0)[:tm]
        du_ref[...] = (w_ref[2:3, :] * dy0 + w_ref[1:2, :] * dy1 + w_ref[0:1, :] * dy2).astype(du_ref.dtype)
        ut = u_ref[...]
        dw_ref[0:1, :] += jnp.sum(dy2 * ut, axis=0, keepdims=True)
        dw_ref[1:2, :] += jnp.sum(dy1 * ut, axis=0, keepdims=True)
        dw_ref[2:3, :] += jnp.sum(dy0 * ut, axis=0, keepdims=True)
        db_ref[...] += jnp.sum(dy0, axis=0, keepdims=True)

    hb = tm // HALO
    nh = seq // HALO
    nxt = lambda j, i: (jnp.minimum((i + 1) * hb, nh - 1), j)
    tile = pl.BlockSpec((tm, 2 * tn), lambda j, i: (i, j))
    return pl.pallas_call(
        body, name="conv_gate_bwd", grid=(D_FF // tn, ni),
        in_specs=[tile, tile, pl.BlockSpec((HALO, 2 * tn), nxt),
                  pl.BlockSpec((tm, tn), lambda j, i: (i, j)), pl.BlockSpec((HALO, tn), nxt),
                  pl.BlockSpec((3, 2 * tn), lambda j, i: (0, j))],
        out_specs=[tile, pl.BlockSpec((3, 2 * tn), lambda j, i: (0, j)), pl.BlockSpec((1, 2 * tn), lambda j, i: (0, j))],
        out_shape=[jax.ShapeDtypeStruct((seq, 2 * D_FF), MXU_DTYPE), jax.ShapeDtypeStruct((3, 2 * D_FF), F32),
                   jax.ShapeDtypeStruct((1, 2 * D_FF), F32)],
        compiler_params=_params("parallel", "arbitrary"),
    )(u, y, y, d_act, d_act, conv_w)


def _pad_heads(w, width):
    w = jnp.transpose(w, (1, 0, 2))
    return jnp.pad(w, ((0, 0), (0, 0), (0, LANES - width))).astype(MXU_DTYPE)


def _heads_major(a):
    return jnp.transpose(a, (1, 0, 2)).reshape(-1, a.shape[2])


def _heads_minor(a, heads):
    return jnp.transpose(a.reshape(heads, -1, a.shape[1]), (1, 0, 2))


_LATENT = Q_RANK + KV_RANK
_ROPE_AT = _LATENT + NOPE_DIM
_ROPE_END = _ROPE_AT + ROPE_DIM


def _split_pad_rows(w_t):
    z = lambda n: jnp.zeros((n, w_t.shape[1]), w_t.dtype)
    return jnp.concatenate([w_t[:_LATENT], z(_ROPE_AT - _LATENT), w_t[_LATENT:_LATENT + ROPE_DIM], z(DH_PART - _ROPE_END),
                            w_t[_LATENT + ROPE_DIM:]], axis=0)


def _split_unpad_rows(w_p):
    return jnp.concatenate([w_p[:_LATENT], w_p[_ROPE_AT:_ROPE_END], w_p[DH_PART:]], axis=0)


def _pad_w_o(w_o):
    mla = jnp.pad(w_o[:MLA_WIDTH].reshape(HEADS, HEAD_DIM, D_MODEL), ((0, 0), (0, LANES - HEAD_DIM), (0, 0)))
    return mla.reshape(HEADS * LANES, D_MODEL).astype(MXU_DTYPE), w_o[MLA_WIDTH:].astype(MXU_DTYPE)


def _unpad_w_o(d_mla, d_dil):
    return jnp.concatenate([d_mla.reshape(HEADS, LANES, D_MODEL)[:, :HEAD_DIM].reshape(MLA_WIDTH, D_MODEL), d_dil], axis=0)


def _row(v):
    return v.reshape(1, -1).astype(F32)


def _layer_grads(x0, target, cw, first_after=(), late_weights=None, on_grads=None):
    seq = x0.shape[0]
    ctab, stab = _rope_tables(seq)
    gq, gk = cw["g_cq"], cw["g_ckv"]
    wq, wk, wv = cw["wq"], cw["wk"], cw["wv"]
    notify = (lambda stage, grads: ()) if on_grads is None else on_grads

    h = _mm(x0, cw["w_in_t"], name="mm_h", tb=True, tm=1024, tn=IN_PAD, tk=1024, after=first_after)
    qf, kf, vp = _mla_prep(h, gq, gk, wq, wk, wv, ctab, stab)
    o_mla, o_mla_b, lse_mla = _mla_attn_fwd(qf, kf, vp)
    o_dil, o_dil_b, lse_dil = _dil_fwd(h)
    fetch = (lambda stage, after: {}) if late_weights is None else late_weights
    cw = {**cw, **fetch("w_o", o_mla_b), **fetch("w_up", o_mla_b)}
    cb = cw["conv_b"]
    z1, x1, x1b = _mix_ln1(o_mla_b, o_dil_b, cw["w_o_mla"], cw["w_o_dil"], x0, cw["ln1_g"], cw["ln1_b"])
    u, y, act = _up_conv_gate_fwd(x1b, cw["w_up_t"], cw["conv_w"], cb)
    cw = {**cw, **fetch("w_down", act)}
    dz2, dz2b, loss, d_ln2_g, d_ln2_b = _down_ln2_loss_bwd(act, cw["w_down"], x1, target, cw["ln2_g"], cw["ln2_b"])

    d_act = _mm(dz2b, cw["w_down"], name="mm_d_act", tb=True, out_dtype=MXU_DTYPE, tm=1024, tn=D_FF, tk=1024)
    d_w_down = _mm(act, dz2b, name="mm_dw_down", ta=True, out_dtype=MXU_DTYPE, tm=1408, tn=1024, tk=1024)
    du, d_conv_w, d_conv_b = _conv_gate_bwd(u, y, d_act, cw["conv_w"])
    d_w_up_t = _mm(du, x1b, name="mm_dw_up", ta=True, out_dtype=MXU_DTYPE, tm=1408, tn=1024, tk=2048)
    grads = dict(w_up_t=d_w_up_t, w_down=d_w_down, conv_w=d_conv_w, conv_b=d_conv_b, ln2_g=d_ln2_g, ln2_b=d_ln2_b)
    dz1, dz1b, d_ln1_g, d_ln1_b = _dx1_ln1_bwd(du, cw["w_up_t"], dz2, z1, cw["ln1_g"], after=notify("ffn", grads))
    do_mla, do_dil, d_w_o_mla, d_w_o_dil = _w_o_bwd(dz1b, o_mla_b, o_dil_b, cw["w_o_mla"], cw["w_o_dil"])
    grads.update(w_o_mla=d_w_o_mla, w_o_dil=d_w_o_dil, ln1_g=d_ln1_g, ln1_b=d_ln1_b)
    dqf, dkf, dvf = _mla_attn_bwd(qf, kf, vp, o_mla, lse_mla, do_mla)
    dh_mla, d_wq, d_wk, d_wv, d_gq, d_gk = _mla_prep_bwd(h, gq, gk, wq, wk, wv, ctab, stab, dqf, dkf, dvf,
                                                          after=notify("w_o", grads))
    grads.update(wq=d_wq, wk=d_wk, wv=d_wv, g_cq=d_gq, g_ckv=d_gk, loss=loss)
    dq_dil, dk_dil, dv_dil = _dil_bwd(h, o_dil, lse_dil, do_dil, after=notify("mla", grads))
    dh = (dh_mla, dq_dil, dk_dil, dv_dil)
    grads.update(w_in_t=_mm_dw_in(dh, x0))
    grad_x = _mm_dx0(dh, cw["w_in_t"], dz1, after=notify("w_in", grads))
    return loss, grad_x, grads


def _all_gather(blocks, name):
    na = len(blocks)

    def body(*refs):
        ins, outs = refs[:na], refs[na:2 * na]
        send_sems, recv_sems, local_sems = refs[2 * na:]
        x, y, c = lax.axis_index("x"), lax.axis_index("y"), lax.axis_index("c")
        me, sibling = (x, y, c), (x, y, 1 - c)
        chips = [(1 - x, y), (x, 1 - y), (1 - x, 1 - y)]

        def slot(out, pos):
            return out.at[4 * pos[0] + 2 * pos[1] + pos[2]]

        def copy(a, k, block, to, src=None):
            return pltpu.make_async_remote_copy(
                src_ref=slot(outs[a], block) if src is None else src, dst_ref=slot(outs[a], block),
                send_sem=send_sems.at[7 * a + k], recv_sem=recv_sems.at[7 * a + k],
                device_id=to, device_id_type=pl.DeviceIdType.MESH)

        mine = [pltpu.make_async_copy(ins[a], slot(outs[a], me), local_sems.at[a]) for a in range(na)]
        for cp in mine:
            cp.start()
        first = []
        for a in range(na):
            first.append(copy(a, 0, me, sibling, src=ins[a]))
            first += [copy(a, 1 + j, me, (*chip, c), src=ins[a]) for j, chip in enumerate(chips)]
        for cp in first:
            cp.start()
        passed = []
        for j, chip in enumerate(chips):
            for a in range(na):
                copy(a, 1 + j, (*chip, c), me).wait_recv()
                cp = copy(a, 4 + j, (*chip, c), sibling)
                cp.start()
                passed.append(cp)
        for a in range(na):
            copy(a, 0, sibling, me).wait_recv()
            for j, chip in enumerate(chips):
                copy(a, 4 + j, (*chip, 1 - c), me).wait_recv()
        for cp in first + passed:
            cp.wait_send()
        for cp in mine:
            cp.wait()

    any_spec = pl.BlockSpec(memory_space=pl.ANY)
    return pl.pallas_call(
        body, name=name, in_specs=[any_spec] * na, out_specs=[any_spec] * na,
        out_shape=[jax.ShapeDtypeStruct((N_DEV,) + b.shape, b.dtype) for b in blocks],
        scratch_shapes=[pltpu.SemaphoreType.DMA((7 * na,)), pltpu.SemaphoreType.DMA((7 * na,)), pltpu.SemaphoreType.DMA((na,))],
    )(*blocks)


_HBM_SPEC = pl.BlockSpec(memory_space=pltpu.HBM)
_SEM_SPEC = pl.BlockSpec(memory_space=pltpu.SEMAPHORE)
_DATAFLOW = pltpu.CompilerParams(has_side_effects=pltpu.SideEffectType.DATAFLOW_SIDE_EFFECTING)


def _split_copies(which, ins, lands, send_sems, recv_sems, gather):
    x, y, c = lax.axis_index("x"), lax.axis_index("y"), lax.axis_index("c")
    me = 4 * x + 2 * y + c
    copies = []
    for a, src, land in zip(which, ins, lands):
        for d in range(1, N_DEV):
            px, py, pc = x ^ (d >> 2), y ^ ((d >> 1) & 1), c ^ (d & 1)
            copies.append(pltpu.make_async_remote_copy(
                src_ref=src if gather[a] else src.at[4 * px + 2 * py + pc], dst_ref=land.at[me],
                send_sem=send_sems.at[7 * a + d - 1], recv_sem=recv_sems.at[7 * a + d - 1],
                device_id=(px, py, pc), device_id_type=pl.DeviceIdType.MESH))
    return copies


def _own_copies(which, ins, lands, own_sems, gather):
    me = 4 * lax.axis_index("x") + 2 * lax.axis_index("y") + lax.axis_index("c")
    return [pltpu.make_async_copy(src if gather[a] else src.at[me], land.at[me], own_sems.at[a])
            for a, src, land in zip(which, ins, lands)]


def _send_start(srcs, gather, name):
    na = len(srcs)
    assert len(gather) == na
    land_types = [pltpu.HBM(((N_DEV,) + s.shape) if g else s.shape, s.dtype) for s, g in zip(srcs, gather)]

    def body(*refs):
        ins, lands = refs[:na], refs[na:2 * na]
        send_sems, recv_sems, own_sems, token = refs[2 * na], refs[2 * na + 1], refs[2 * na + 2], refs[-1]
        for cp in _split_copies(range(na), ins, lands, send_sems, recv_sems, gather):
            cp.start()
        for cp in _own_copies(range(na), ins, lands, own_sems, gather):
            cp.start()
        token[...] = jnp.zeros_like(token)

    hbm = lambda a: pltpu.with_memory_space_constraint(a, pltpu.HBM)
    outs = pl.pallas_call(
        body, name=name,
        out_shape=(pltpu.SemaphoreType.DMA((7 * na,)), pltpu.SemaphoreType.DMA((7 * na,)), pltpu.SemaphoreType.DMA((na,)),
                   *[pltpu.HBM(s.shape, s.dtype) for s in srcs], *land_types, jax.ShapeDtypeStruct((8, LANES), F32)),
        in_specs=[_HBM_SPEC] * (2 * na),
        out_specs=(_SEM_SPEC, _SEM_SPEC, _SEM_SPEC, *[_HBM_SPEC] * (2 * na), pl.BlockSpec(memory_space=pltpu.VMEM)),
        input_output_aliases={i: 3 + i for i in range(2 * na)}, compiler_params=_DATAFLOW,
    )(*[hbm(s) for s in srcs], *[hbm(lax.empty(t.shape, t.dtype)) for t in land_types])
    return dict(send=outs[0], recv=outs[1], own=outs[2], srcs=list(outs[3:3 + na]), lands=list(outs[3 + na:3 + 2 * na]),
                token=outs[-1], gather=gather)


def _send_wait(handle, after, name, only=None):
    which = list(range(len(handle["srcs"]))) if only is None else list(only)
    na = len(which)
    gather = handle["gather"]
    after = list(after)

    def body(*refs):
        ins, lands = refs[:na], refs[na:2 * na]
        send_sems, recv_sems, own_sems = refs[2 * na], refs[2 * na + 1], refs[2 * na + 2]
        for cp in _split_copies(which, ins, lands, send_sems, recv_sems, gather):
            cp.wait_send()
            cp.wait_recv()
        for cp in _own_copies(which, ins, lands, own_sems, gather):
            cp.wait()

    both = [handle["srcs"][a] for a in which] + [handle["lands"][a] for a in which]
    outs = pl.pallas_call(
        body, name=name, out_shape=[pltpu.HBM(a.shape, a.dtype) for a in both],
        in_specs=[_HBM_SPEC] * (2 * na) + [_SEM_SPEC] * 3 + [_ANY_SPEC] * len(after),
        out_specs=[_HBM_SPEC] * (2 * na), input_output_aliases={i: i for i in range(2 * na)}, compiler_params=_DATAFLOW,
    )(*both, handle["send"], handle["recv"], handle["own"], *after)
    return list(outs[na:])


def _in_hbm(a):
    return pltpu.with_memory_space_constraint(a, pltpu.HBM)


def _sum_slots(p_ref):
    g = p_ref[0].astype(F32)
    for s in range(1, p_ref.shape[0]):
        g = g + p_ref[s].astype(F32)
    return g


def _adamw_refs(g, w_ref, m_ref, v_ref, g_out, d_out, m_out, v_out):
    c1 = 1.0 - ADAM_B1 ** ADAM_STEP
    c2 = 1.0 - ADAM_B2 ** ADAM_STEP
    m_new = ADAM_B1 * m_ref[...] + (1.0 - ADAM_B1) * g
    v_new = ADAM_B2 * v_ref[...] + (1.0 - ADAM_B2) * (g * g)
    g_out[...] = g
    m_out[...] = m_new
    v_out[...] = v_new
    d_out[...] = -ADAM_LR * ((m_new / c1) / (jnp.sqrt(v_new / c2) + ADAM_EPS) + ADAM_WD * w_ref[...])


def _adamw(parts, w, m, v, name):
    npart, r, n = parts.shape
    tr = r if r <= 256 else max(t for t in range(16, 257, 16) if r % t == 0)

    def body(p_ref, w_ref, m_ref, v_ref, g_out, d_out, m_out, v_out):
        _adamw_refs(_sum_slots(p_ref), w_ref, m_ref, v_ref, g_out, d_out, m_out, v_out)

    blk = pl.BlockSpec((tr, n), lambda i: (i, 0))
    return pl.pallas_call(
        body, name=name, grid=(r // tr,), in_specs=[pl.BlockSpec((npart, tr, n), lambda i: (0, i, 0)), blk, blk, blk],
        out_specs=[blk] * 4, out_shape=[pltpu.HBM((r, n), F32)] * 4, compiler_params=_params("parallel"),
    )(*map(_in_hbm, (parts, w, m, v)))


def _adamw_small(parts, ws, ms, vs, loss_parts, name):
    n = len(parts)

    def body(*refs):
        ins, outs = refs[:4 * n + 1], refs[4 * n + 1:]
        for i in range(n):
            _adamw_refs(_sum_slots(ins[i]), ins[n + i], ins[2 * n + i], ins[3 * n + i], *outs[4 * i:4 * i + 4])
        outs[4 * n][...] = _sum_slots(ins[4 * n])

    out_shape = [pltpu.HBM(w.shape, F32) for w in ws for _ in range(4)]
    res = pl.pallas_call(body, name=name, out_shape=out_shape + [pltpu.HBM((1, LANES), F32)],
                         compiler_params=_params())(*map(_in_hbm, (*parts, *ws, *ms, *vs, loss_parts)))
    return [res[4 * i:4 * i + 4] for i in range(n)], res[4 * n]


REPLICATED = ("g_cq", "g_ckv", "w_uk", "w_uv", "ln1_g", "ln1_b", "conv_b", "ln2_g", "ln2_b")
ALL_WEIGHTS = ("w_in", "g_cq", "g_ckv", "w_uq", "w_uk", "w_uv", "w_o", "ln1_g", "ln1_b", "w_up", "conv_w", "conv_b",
               "w_down", "ln2_g", "ln2_b")


def kernel(x, w_in, g_cq, g_ckv, w_uq, w_uk, w_uv, w_o, ln1_g, ln1_b, w_up, conv_w, conv_b, w_down, ln2_g, ln2_b, loss_target, m_w_in, m_g_cq, m_g_ckv, m_w_uq, m_w_uk, m_w_uv, m_w_o, m_ln1_g, m_ln1_b, m_w_up, m_conv_w, m_conv_b, m_w_down, m_ln2_g, m_ln2_b, v_w_in, v_g_cq, v_g_ckv, v_w_uq, v_w_uk, v_w_uv, v_w_o, v_ln1_g, v_ln1_b, v_w_up, v_conv_w, v_conv_b, v_w_down, v_ln2_g, v_ln2_b):
    w = dict(w_in=w_in, g_cq=g_cq, g_ckv=g_ckv, w_uq=w_uq, w_uk=w_uk, w_uv=w_uv, w_o=w_o, ln1_g=ln1_g, ln1_b=ln1_b,
             w_up=w_up, conv_w=conv_w, conv_b=conv_b, w_down=w_down, ln2_g=ln2_g, ln2_b=ln2_b)
    m = dict(w_in=m_w_in, g_cq=m_g_cq, g_ckv=m_g_ckv, w_uq=m_w_uq, w_uk=m_w_uk, w_uv=m_w_uv, w_o=m_w_o, ln1_g=m_ln1_g,
             ln1_b=m_ln1_b, w_up=m_w_up, conv_w=m_conv_w, conv_b=m_conv_b, w_down=m_w_down, ln2_g=m_ln2_g, ln2_b=m_ln2_b)
    v = dict(w_in=v_w_in, g_cq=v_g_cq, g_ckv=v_g_ckv, w_uq=v_w_uq, w_uk=v_w_uk, w_uv=v_w_uv, w_o=v_w_o, ln1_g=v_ln1_g,
             ln1_b=v_ln1_b, w_up=v_w_up, conv_w=v_conv_w, conv_b=v_conv_b, w_down=v_w_down, ln2_g=v_ln2_g, ln2_b=v_ln2_b)
    me = 4 * lax.axis_index("x") + 2 * lax.axis_index("y") + lax.axis_index("c")
    wire = lambda a: a.astype(WIRE_DTYPE)
    pad_taps = lambda a: jnp.pad(a, ((0, 8 - a.shape[0]), (0, 0)))

    blocks = lambda a: wire(a).reshape((N_DEV, a.shape[0] // N_DEV) + a.shape[1:])

    g_in, g_uq, g_conv = _all_gather(
        [wire(w_in).T, _heads_major(wire(w_uq)), pad_taps(conv_w)],
        "gather_weights")
    late = _send_start([wire(w_o), wire(w_up).T, wire(w_down)], [True] * 3, "gather_late_start")
    r_uq_dev, e_uq = w_uq.shape[0], w_uq.shape[2]
    wq = jnp.transpose(g_uq.reshape(N_DEV, HEADS, r_uq_dev, e_uq), (1, 0, 2, 3)).reshape(HEADS, Q_RANK, e_uq)
    cw = dict(
        w_in_t=_split_pad_rows(g_in.reshape(-1, D_MODEL)).astype(MXU_DTYPE),
        wq=jnp.pad(wq, ((0, 0), (0, 0), (0, LANES - e_uq))).astype(MXU_DTYPE),
        wk=_pad_heads(w_uk, NOPE_DIM), wv=_pad_heads(w_uv, HEAD_DIM),
        conv_w=_ffn_interleave(jnp.transpose(g_conv[:, :conv_w.shape[0]], (1, 0, 2)).reshape(conv_w.shape[0], -1), 1),
        g_cq=_row(g_cq), g_ckv=_row(g_ckv), ln1_g=_row(ln1_g), ln1_b=_row(ln1_b), conv_b=_ffn_interleave(_row(conv_b), 1),
        ln2_g=_row(ln2_g), ln2_b=_row(ln2_b))

    def late_weights(stage, after):
        (got,) = _send_wait(late, [after], f"gather_{stage}_wait", only=[("w_o", "w_up", "w_down").index(stage)])
        full = got.reshape(-1, D_MODEL)
        if stage == "w_o":
            w_o_mla, w_o_dil = _pad_w_o(full)
            return dict(w_o_mla=w_o_mla, w_o_dil=w_o_dil)
        if stage == "w_up":
            return dict(w_up_t=_ffn_interleave(full, 0).astype(MXU_DTYPE))
        return dict(w_down=full.astype(MXU_DTYPE))

    sent = {}

    def on_grads(stage, g):
        if stage == "ffn":
            sent[stage] = _send_start([blocks(_ffn_deinterleave(g["w_up_t"], 0)), blocks(g["w_down"])], [False] * 2,
                                      "exchange_ffn_start")
        elif stage == "w_o":
            return []
        elif stage == "mla":
            d_uq = wire(jnp.transpose(g["wq"][:, :, :e_uq].reshape(HEADS, N_DEV, r_uq_dev, e_uq), (1, 0, 2, 3))
                        ).reshape(N_DEV, HEADS * r_uq_dev, e_uq)
            dense = lambda a, width: wire(jnp.transpose(a[:, :, :width], (0, 2, 1))).reshape(-1, a.shape[1])
            small = dict(g_cq=g["g_cq"], g_ckv=g["g_ckv"], w_uk=dense(g["wk"], NOPE_DIM), w_uv=dense(g["wv"], HEAD_DIM),
                         ln1_g=g["ln1_g"], ln1_b=g["ln1_b"], conv_b=_ffn_deinterleave(g["conv_b"], 1), ln2_g=g["ln2_g"],
                         ln2_b=g["ln2_b"])
            everyone = [small[n] for n in REPLICATED] + [_ffn_deinterleave(g["conv_w"], 1), g["loss"]]
            sent[stage] = _send_start([blocks(_unpad_w_o(g["w_o_mla"], g["w_o_dil"])), d_uq] + everyone,
                                      [False] * 2 + [True] * len(everyone), "exchange_mla_start")
        else:
            sent[stage] = _send_start([blocks(_split_unpad_rows(g["w_in_t"]))], [False], "exchange_w_in_start")
        return [sent[stage]["token"]]

    _, grad_x, _ = _layer_grads(x[0], loss_target[0], cw, [late["token"]], late_weights, on_grads)


    out = {}

    def update(name, parts, view=None):
        to2d = {None: lambda a: a, "t": lambda a: a.T, "heads": _heads_major}[view]
        back = {None: lambda a: a, "t": lambda a: a.T, "heads": lambda a: _heads_minor(a, HEADS)}[view]
        res = _adamw(parts, to2d(w[name]), to2d(m[name]), to2d(v[name]), "adamw_" + name)
        for kind, a in zip(("grad", "delta", "new_m", "new_v"), res):
            out[kind, name] = back(a)
        return res[0]

    r_up, r_down = _send_wait(sent["ffn"], [grad_x], "exchange_ffn_wait")
    r_o, r_uq, *rep_all, cw_all, loss_all = _send_wait(sent["mla"], [grad_x], "exchange_mla_wait")
    done = [update("w_up", r_up, "t"), update("w_down", r_down), update("w_o", r_o), update("w_uq", r_uq, "heads")]
    rank_minor = ("w_uk", "w_uv")
    two_d = lambda n, a: jnp.transpose(a, (1, 2, 0)).reshape(-1, a.shape[0]) if n in rank_minor else a.reshape(1, -1)
    res, loss_sum = _adamw_small(rep_all, *[[two_d(n, d[n]) for n in REPLICATED] for d in (w, m, v)], loss_all, "adamw_replicated")
    for n, quad in zip(REPLICATED, res):
        for kind, a in zip(("grad", "delta", "new_m", "new_v"), quad):
            out[kind, n] = jnp.transpose(a.reshape(HEADS, -1, a.shape[1]), (2, 0, 1)) if n in rank_minor else a.reshape(w[n].shape)
    loss = loss_sum[0, 0]
    ncw = conv_w.shape[1]
    done += [loss_sum, update("conv_w", lax.dynamic_slice_in_dim(cw_all[:, :conv_w.shape[0]], me * ncw, ncw, axis=2))]
    (r_in,) = _send_wait(sent["w_in"], done, "exchange_w_in_wait")
    update("w_in", r_in, "t")

    return (loss, grad_x[None], *[out[kind, n] for kind in ("grad", "delta", "new_m", "new_v") for n in ALL_WEIGHTS])
```
